```python
import math
import jax, jax.numpy as jnp
from jax import lax
import numpy as np

D_MODEL = 1024
BATCH = 16
SEQ = 2048
DEPTH = 1

CHUNK = 64
N_MEM = 256
D_CONV = D_MODEL
CONV_WIDTH = 3
FOX_HEAD_DIM = 128
FOX_HEADS = D_MODEL // FOX_HEAD_DIM
FOX_WIDTH = FOX_HEADS * FOX_HEAD_DIM
XA_HEADS = 4
XA_HEAD_DIM = D_MODEL // XA_HEADS
XA_WIDTH = XA_HEADS * XA_HEAD_DIM
N_BRANCH = 3
D_FF = -(-8 * D_MODEL // (3 * 256)) * 256
Q_BLOCK = 128
EPS = 1e-6
IN_SPLITS = (D_CONV, D_CONV, D_CONV, FOX_WIDTH, FOX_WIDTH, FOX_WIDTH, XA_WIDTH,
             D_MODEL, D_MODEL, D_MODEL, FOX_HEADS)
IN_COLS = sum(IN_SPLITS)

kernel_name = "hybrid_conv_fox_memory_block"


def rms_norm(x, g):
    xf = x.astype(jnp.float32)
    y = xf * lax.rsqrt(jnp.mean(xf * xf, axis=-1, keepdims=True) + EPS)
    return (y * g.astype(jnp.float32)).astype(x.dtype)


def split_cols(z):
    outs, off = [], 0
    for w in IN_SPLITS:
        outs.append(z[..., off:off + w])
        off += w
    return outs


def short_gated_conv(b_gate, c_gate, v, conv_w, conv_b):
    u = c_gate * v
    rhs = conv_w[:, None, :]
    y = lax.conv_general_dilated(u, rhs, window_strides=(1,),
                                 padding=[(CONV_WIDTH - 1, 0)],
                                 dimension_numbers=('NWC', 'WIO', 'NWC'),
                                 feature_group_count=D_CONV)
    return b_gate * (y + conv_b)


def forgetting_attention(q, k, v, log_f):
    b, s, h, hd = q.shape
    nb = s // Q_BLOCK
    c = jnp.cumsum(log_f, axis=1).transpose(0, 2, 1)
    kh = k.transpose(0, 2, 1, 3)
    vh = v.transpose(0, 2, 1, 3)
    q_blocks = q.transpose(0, 2, 1, 3).reshape(b, h, nb, Q_BLOCK, hd).transpose(2, 0, 1, 3, 4)
    c_blocks = c.reshape(b, h, nb, Q_BLOCK).transpose(2, 0, 1, 3)
    k_pos = jnp.arange(s)
    scale = 1.0 / math.sqrt(hd)

    def block(args):
        qb, cb, i = args
        logits = jnp.einsum('bhqd,bhkd->bhqk', qb, kh,
                            preferred_element_type=jnp.float32) * scale
        logits = logits + cb[..., None] - c[:, :, None, :]
        q_pos = i * Q_BLOCK + jnp.arange(Q_BLOCK)
        mask = q_pos[:, None] >= k_pos[None, :]
        logits = jnp.where(mask, logits, -jnp.inf)
        p = jax.nn.softmax(logits, axis=-1).astype(vh.dtype)
        return jnp.einsum('bhqk,bhkd->bhqd', p, vh)

    out = lax.map(block, (q_blocks, c_blocks, jnp.arange(nb)))
    return out.transpose(1, 0, 3, 2, 4).reshape(b, s, h * hd)


def memory_cross_attention(q, mem_n, w_mem_kv, q_g, k_g):
    b, s = q.shape[0], q.shape[1]
    kv = mem_n @ w_mem_kv
    k = kv[..., :XA_WIDTH].reshape(b, -1, XA_HEADS, XA_HEAD_DIM)
    v = kv[..., XA_WIDTH:].reshape(b, -1, XA_HEADS, XA_HEAD_DIM)
    q = rms_norm(q, q_g)
    k = rms_norm(k, k_g)
    logits = jnp.einsum('bshd,bmhd->bhsm', q, k,
                        preferred_element_type=jnp.float32) / math.sqrt(XA_HEAD_DIM)
    p = jax.nn.softmax(logits, axis=-1).astype(v.dtype)
    return jnp.einsum('bhsm,bmhd->bshd', p, v).reshape(b, s, XA_WIDTH)


def _fwd_setup_inputs(seed: int = 0) -> dict:
    key = jax.random.key(seed)
    ks = jax.random.split(key, 24)
    f32 = jnp.float32
    nrm = lambda k, shape, fan: jax.random.normal(k, shape, f32) * fan ** -0.5
    gain = lambda k, shape: 1.0 + 0.02 * jax.random.normal(k, shape, f32)
    L = DEPTH
    return {
        "x": jax.random.normal(ks[0], (BATCH, SEQ, D_MODEL), f32),
        "mem": jax.random.normal(ks[1], (BATCH, N_MEM, D_MODEL), f32),
        "norm1_g": gain(ks[2], (L, D_MODEL)),
        "w_in": nrm(ks[3], (L, D_MODEL, IN_COLS), D_MODEL),
        "conv_w": nrm(ks[4], (L, CONV_WIDTH, D_CONV), CONV_WIDTH),
        "conv_b": 0.02 * jax.random.normal(ks[5], (L, D_CONV), f32),
        "fox_f_bias": 2.0 + 2.0 * jax.random.uniform(ks[6], (L, FOX_HEADS), f32),
        "fox_q_g": gain(ks[7], (L, FOX_HEAD_DIM)),
        "fox_k_g": gain(ks[8], (L, FOX_HEAD_DIM)),
        "mem_norm_g": gain(ks[9], (L, D_MODEL)),
        "w_mem_kv": nrm(ks[10], (L, D_MODEL, 2 * XA_WIDTH), D_MODEL),
        "xa_q_g": gain(ks[11], (L, XA_HEAD_DIM)),
        "xa_k_g": gain(ks[12], (L, XA_HEAD_DIM)),
        "w_br_conv": nrm(ks[13], (L, D_CONV, D_MODEL), D_CONV),
        "w_br_fox": nrm(ks[14], (L, FOX_WIDTH, D_MODEL), FOX_WIDTH),
        "w_br_xa": nrm(ks[15], (L, XA_WIDTH, D_MODEL), XA_WIDTH),
        "w_o": nrm(ks[16], (L, D_MODEL, D_MODEL), D_MODEL),
        "norm2_g": gain(ks[17], (L, D_MODEL)),
        "w_ffn_in": nrm(ks[18], (L, D_MODEL, 2 * D_FF), D_MODEL),
        "w_ffn_out": nrm(ks[19], (L, D_FF, D_MODEL), D_FF),
    }


def _fwd_reference(x, mem, norm1_g, w_in, conv_w, conv_b, fox_f_bias, fox_q_g, fox_k_g,
              mem_norm_g, w_mem_kv, xa_q_g, xa_k_g, w_br_conv, w_br_fox, w_br_xa,
              w_o, norm2_g, w_ffn_in, w_ffn_out):
    b, s, _ = x.shape
    for l in range(DEPTH):
        h = rms_norm(x, norm1_g[l])
        z = h @ w_in[l]
        (cb, cc, cv, fq, fk, fv, xq, ga, gb, gc, ff) = split_cols(z)

        y_conv = short_gated_conv(cb, cc, cv, conv_w[l], conv_b[l])

        fq = rms_norm(fq.reshape(b, s, FOX_HEADS, FOX_HEAD_DIM), fox_q_g[l])
        fk = rms_norm(fk.reshape(b, s, FOX_HEADS, FOX_HEAD_DIM), fox_k_g[l])
        fv = fv.reshape(b, s, FOX_HEADS, FOX_HEAD_DIM)
        log_f = jax.nn.log_sigmoid(ff.astype(jnp.float32) + fox_f_bias[l].astype(jnp.float32))
        y_fox = forgetting_attention(fq, fk, fv, log_f)

        mem_n = rms_norm(mem, mem_norm_g[l])
        y_xa = memory_cross_attention(xq.reshape(b, s, XA_HEADS, XA_HEAD_DIM), mem_n,
                                      w_mem_kv[l], xa_q_g[l], xa_k_g[l])

        merged = (jax.nn.sigmoid(ga) * (y_conv @ w_br_conv[l])
                  + jax.nn.sigmoid(gb) * (y_fox @ w_br_fox[l])
                  + jax.nn.sigmoid(gc) * (y_xa @ w_br_xa[l]))
        x = x + merged @ w_o[l]

        h2 = rms_norm(x, norm2_g[l])
        gu = h2 @ w_ffn_in[l]
        x = x + (jax.nn.silu(gu[..., :D_FF]) * gu[..., D_FF:]) @ w_ffn_out[l]
    return x


import jax as _jax
import jax.numpy as _jnp

TWIN_FORMAT = 'train_step'
FWD_PARAMS = ['x', 'mem', 'norm1_g', 'w_in', 'conv_w', 'conv_b', 'fox_f_bias', 'fox_q_g', 'fox_k_g', 'mem_norm_g', 'w_mem_kv', 'xa_q_g', 'xa_k_g', 'w_br_conv', 'w_br_fox', 'w_br_xa', 'w_o', 'norm2_g', 'w_ffn_in', 'w_ffn_out']
TWIN_WEIGHTS = ['norm1_g', 'w_in', 'conv_w', 'conv_b', 'fox_f_bias', 'fox_q_g', 'fox_k_g', 'mem_norm_g', 'w_mem_kv', 'xa_q_g', 'xa_k_g', 'w_br_conv', 'w_br_fox', 'w_br_xa', 'w_o', 'norm2_g', 'w_ffn_in', 'w_ffn_out']
TWIN_DIFF_INPUT = 'x'
TWIN_INPUTS = ['x', 'mem', 'norm1_g', 'w_in', 'conv_w', 'conv_b', 'fox_f_bias', 'fox_q_g', 'fox_k_g', 'mem_norm_g', 'w_mem_kv', 'xa_q_g', 'xa_k_g', 'w_br_conv', 'w_br_fox', 'w_br_xa', 'w_o', 'norm2_g', 'w_ffn_in', 'w_ffn_out', 'loss_target', 'm_norm1_g', 'm_w_in', 'm_conv_w', 'm_conv_b', 'm_fox_f_bias', 'm_fox_q_g', 'm_fox_k_g', 'm_mem_norm_g', 'm_w_mem_kv', 'm_xa_q_g', 'm_xa_k_g', 'm_w_br_conv', 'm_w_br_fox', 'm_w_br_xa', 'm_w_o', 'm_norm2_g', 'm_w_ffn_in', 'm_w_ffn_out', 'v_norm1_g', 'v_w_in', 'v_conv_w', 'v_conv_b', 'v_fox_f_bias', 'v_fox_q_g', 'v_fox_k_g', 'v_mem_norm_g', 'v_w_mem_kv', 'v_xa_q_g', 'v_xa_k_g', 'v_w_br_conv', 'v_w_br_fox', 'v_w_br_xa', 'v_w_o', 'v_norm2_g', 'v_w_ffn_in', 'v_w_ffn_out']
TWIN_OUTPUTS = ['loss', 'grad_x', 'grad_norm1_g', 'grad_w_in', 'grad_conv_w', 'grad_conv_b', 'grad_fox_f_bias', 'grad_fox_q_g', 'grad_fox_k_g', 'grad_mem_norm_g', 'grad_w_mem_kv', 'grad_xa_q_g', 'grad_xa_k_g', 'grad_w_br_conv', 'grad_w_br_fox', 'grad_w_br_xa', 'grad_w_o', 'grad_norm2_g', 'grad_w_ffn_in', 'grad_w_ffn_out', 'delta_norm1_g', 'delta_w_in', 'delta_conv_w', 'delta_conv_b', 'delta_fox_f_bias', 'delta_fox_q_g', 'delta_fox_k_g', 'delta_mem_norm_g', 'delta_w_mem_kv', 'delta_xa_q_g', 'delta_xa_k_g', 'delta_w_br_conv', 'delta_w_br_fox', 'delta_w_br_xa', 'delta_w_o', 'delta_norm2_g', 'delta_w_ffn_in', 'delta_w_ffn_out', 'new_m_norm1_g', 'new_m_w_in', 'new_m_conv_w', 'new_m_conv_b', 'new_m_fox_f_bias', 'new_m_fox_q_g', 'new_m_fox_k_g', 'new_m_mem_norm_g', 'new_m_w_mem_kv', 'new_m_xa_q_g', 'new_m_xa_k_g', 'new_m_w_br_conv', 'new_m_w_br_fox', 'new_m_w_br_xa', 'new_m_w_o', 'new_m_norm2_g', 'new_m_w_ffn_in', 'new_m_w_ffn_out', 'new_v_norm1_g', 'new_v_w_in', 'new_v_conv_w', 'new_v_conv_b', 'new_v_fox_f_bias', 'new_v_fox_q_g', 'new_v_fox_k_g', 'new_v_mem_norm_g', 'new_v_w_mem_kv', 'new_v_xa_q_g', 'new_v_xa_k_g', 'new_v_w_br_conv', 'new_v_w_br_fox', 'new_v_w_br_xa', 'new_v_w_o', 'new_v_norm2_g', 'new_v_w_ffn_in', 'new_v_w_ffn_out']
TWIN_LEAF_KINDS = {'loss': 'loss', 'grad_x': 'grad_x', 'grad_norm1_g': 'grad_w', 'grad_w_in': 'grad_w', 'grad_conv_w': 'grad_w', 'grad_conv_b': 'grad_w', 'grad_fox_f_bias': 'grad_w', 'grad_fox_q_g': 'grad_w', 'grad_fox_k_g': 'grad_w', 'grad_mem_norm_g': 'grad_w', 'grad_w_mem_kv': 'grad_w', 'grad_xa_q_g': 'grad_w', 'grad_xa_k_g': 'grad_w', 'grad_w_br_conv': 'grad_w', 'grad_w_br_fox': 'grad_w', 'grad_w_br_xa': 'grad_w', 'grad_w_o': 'grad_w', 'grad_norm2_g': 'grad_w', 'grad_w_ffn_in': 'grad_w', 'grad_w_ffn_out': 'grad_w', 'delta_norm1_g': 'delta_w', 'delta_w_in': 'delta_w', 'delta_conv_w': 'delta_w', 'delta_conv_b': 'delta_w', 'delta_fox_f_bias': 'delta_w', 'delta_fox_q_g': 'delta_w', 'delta_fox_k_g': 'delta_w', 'delta_mem_norm_g': 'delta_w', 'delta_w_mem_kv': 'delta_w', 'delta_xa_q_g': 'delta_w', 'delta_xa_k_g': 'delta_w', 'delta_w_br_conv': 'delta_w', 'delta_w_br_fox': 'delta_w', 'delta_w_br_xa': 'delta_w', 'delta_w_o': 'delta_w', 'delta_norm2_g': 'delta_w', 'delta_w_ffn_in': 'delta_w', 'delta_w_ffn_out': 'delta_w', 'new_m_norm1_g': 'new_m', 'new_m_w_in': 'new_m', 'new_m_conv_w': 'new_m', 'new_m_conv_b': 'new_m', 'new_m_fox_f_bias': 'new_m', 'new_m_fox_q_g': 'new_m', 'new_m_fox_k_g': 'new_m', 'new_m_mem_norm_g': 'new_m', 'new_m_w_mem_kv': 'new_m', 'new_m_xa_q_g': 'new_m', 'new_m_xa_k_g': 'new_m', 'new_m_w_br_conv': 'new_m', 'new_m_w_br_fox': 'new_m', 'new_m_w_br_xa': 'new_m', 'new_m_w_o': 'new_m', 'new_m_norm2_g': 'new_m', 'new_m_w_ffn_in': 'new_m', 'new_m_w_ffn_out': 'new_m', 'new_v_norm1_g': 'new_v', 'new_v_w_in': 'new_v', 'new_v_conv_w': 'new_v', 'new_v_conv_b': 'new_v', 'new_v_fox_f_bias': 'new_v', 'new_v_fox_q_g': 'new_v', 'new_v_fox_k_g': 'new_v', 'new_v_mem_norm_g': 'new_v', 'new_v_w_mem_kv': 'new_v', 'new_v_xa_q_g': 'new_v', 'new_v_xa_k_g': 'new_v', 'new_v_w_br_conv': 'new_v', 'new_v_w_br_fox': 'new_v', 'new_v_w_br_xa': 'new_v', 'new_v_w_o': 'new_v', 'new_v_norm2_g': 'new_v', 'new_v_w_ffn_in': 'new_v', 'new_v_w_ffn_out': 'new_v'}


def _forward(args):
    return _fwd_reference(*[args[k] for k in FWD_PARAMS])


def _output_shape():
    out = _jax.eval_shape(lambda: _forward(_fwd_setup_inputs(0)))
    return out.shape, out.dtype

N_MICROBATCH = 1
ADAM_LR = 0.001
ADAM_B1 = 0.9
ADAM_B2 = 0.999
ADAM_EPS = 1e-08
ADAM_WD = 0.01
ADAM_STEP = 10
PER_EXAMPLE_BATCH_AXIS = {'x': 0, 'mem': 0, 'loss_target': 0}
SHARED_INPUTS = []
_WEIGHT_DTYPES = {'norm1_g': _jnp.float32, 'w_in': _jnp.float32, 'conv_w': _jnp.float32, 'conv_b': _jnp.float32, 'fox_f_bias': _jnp.float32, 'fox_q_g': _jnp.float32, 'fox_k_g': _jnp.float32, 'mem_norm_g': _jnp.float32, 'w_mem_kv': _jnp.float32, 'xa_q_g': _jnp.float32, 'xa_k_g': _jnp.float32, 'w_br_conv': _jnp.float32, 'w_br_fox': _jnp.float32, 'w_br_xa': _jnp.float32, 'w_o': _jnp.float32, 'norm2_g': _jnp.float32, 'w_ffn_in': _jnp.float32, 'w_ffn_out': _jnp.float32}
MOMENT_SCALE = {'norm1_g': 3.041753e+01, 'w_in': 2.000745e-01, 'conv_w': 5.307555e+00, 'conv_b': 4.691103e-01, 'fox_f_bias': 3.259576e+01, 'fox_q_g': 3.194720e+00, 'fox_k_g': 3.187659e+00, 'mem_norm_g': 9.875246e-02, 'w_mem_kv': 3.945383e-02, 'xa_q_g': 3.567157e-01, 'xa_k_g': 3.530828e-01, 'w_br_conv': 3.758900e-01, 'w_br_fox': 7.101934e-02, 'w_br_xa': 4.942576e-02, 'w_o': 3.602794e-01, 'norm2_g': 2.470272e+01, 'w_ffn_in': 1.617160e-01, 'w_ffn_out': 2.572477e-01}


def _to_microbatches(a, axis):
    t = _jnp.moveaxis(a, axis, 0)
    t = t.reshape((N_MICROBATCH, t.shape[0] // N_MICROBATCH) + t.shape[1:])
    return _jnp.moveaxis(t, 1, axis + 1)


def setup_inputs(seed: int = 0) -> dict:
    inp = _fwd_setup_inputs(seed)
    key = _jax.random.fold_in(_jax.random.key(seed), 7919)
    shape, _ = _output_shape()
    out = dict(inp)
    out["loss_target"] = _jax.random.normal(_jax.random.fold_in(key, 0), shape, _jnp.float32)
    for i, name in enumerate(TWIN_WEIGHTS):
        w = inp[name].astype(_jnp.float32)
        if MOMENT_SCALE is None:
            s = _jnp.sqrt(_jnp.mean(_jnp.square(w)) + 1e-30)
        else:
            s = MOMENT_SCALE[name]
        km, kv = _jax.random.split(_jax.random.fold_in(key, i + 1))
        out[name] = w
        out["m_" + name] = s * _jax.random.normal(km, w.shape, _jnp.float32)
        out["v_" + name] = (s * s) * _jax.random.uniform(kv, w.shape, _jnp.float32, 0.5, 1.5)
    if N_MICROBATCH > 1:
        for name, axis in PER_EXAMPLE_BATCH_AXIS.items():
            out[name] = _to_microbatches(out[name], axis)
    return {'x': out['x'], 'mem': out['mem'], 'norm1_g': out['norm1_g'], 'w_in': out['w_in'], 'conv_w': out['conv_w'], 'conv_b': out['conv_b'], 'fox_f_bias': out['fox_f_bias'], 'fox_q_g': out['fox_q_g'], 'fox_k_g': out['fox_k_g'], 'mem_norm_g': out['mem_norm_g'], 'w_mem_kv': out['w_mem_kv'], 'xa_q_g': out['xa_q_g'], 'xa_k_g': out['xa_k_g'], 'w_br_conv': out['w_br_conv'], 'w_br_fox': out['w_br_fox'], 'w_br_xa': out['w_br_xa'], 'w_o': out['w_o'], 'norm2_g': out['norm2_g'], 'w_ffn_in': out['w_ffn_in'], 'w_ffn_out': out['w_ffn_out'], 'loss_target': out['loss_target'], 'm_norm1_g': out['m_norm1_g'], 'm_w_in': out['m_w_in'], 'm_conv_w': out['m_conv_w'], 'm_conv_b': out['m_conv_b'], 'm_fox_f_bias': out['m_fox_f_bias'], 'm_fox_q_g': out['m_fox_q_g'], 'm_fox_k_g': out['m_fox_k_g'], 'm_mem_norm_g': out['m_mem_norm_g'], 'm_w_mem_kv': out['m_w_mem_kv'], 'm_xa_q_g': out['m_xa_q_g'], 'm_xa_k_g': out['m_xa_k_g'], 'm_w_br_conv': out['m_w_br_conv'], 'm_w_br_fox': out['m_w_br_fox'], 'm_w_br_xa': out['m_w_br_xa'], 'm_w_o': out['m_w_o'], 'm_norm2_g': out['m_norm2_g'], 'm_w_ffn_in': out['m_w_ffn_in'], 'm_w_ffn_out': out['m_w_ffn_out'], 'v_norm1_g': out['v_norm1_g'], 'v_w_in': out['v_w_in'], 'v_conv_w': out['v_conv_w'], 'v_conv_b': out['v_conv_b'], 'v_fox_f_bias': out['v_fox_f_bias'], 'v_fox_q_g': out['v_fox_q_g'], 'v_fox_k_g': out['v_fox_k_g'], 'v_mem_norm_g': out['v_mem_norm_g'], 'v_w_mem_kv': out['v_w_mem_kv'], 'v_xa_q_g': out['v_xa_q_g'], 'v_xa_k_g': out['v_xa_k_g'], 'v_w_br_conv': out['v_w_br_conv'], 'v_w_br_fox': out['v_w_br_fox'], 'v_w_br_xa': out['v_w_br_xa'], 'v_w_o': out['v_w_o'], 'v_norm2_g': out['v_norm2_g'], 'v_w_ffn_in': out['v_w_ffn_in'], 'v_w_ffn_out': out['v_w_ffn_out']}


def _loss(weights, diff, rest, loss_target):
    with _jax.named_scope("forward"):
        args = {**rest, TWIN_DIFF_INPUT: diff, **{k: w.astype(_WEIGHT_DTYPES[k]) for k, w in weights.items()}}
        y = _forward(args)
    with _jax.named_scope("loss_head"):
        err = _jnp.square(y.astype(_jnp.float32) - loss_target)
        return 0.5 * _jnp.sum(_jnp.mean(err, axis=-1)) if err.ndim else 0.5 * err


def _adamw(w, g, m, v):
    m = ADAM_B1 * m + (1.0 - ADAM_B1) * g
    v = ADAM_B2 * v + (1.0 - ADAM_B2) * _jnp.square(g)
    m_hat = m / (1.0 - ADAM_B1 ** ADAM_STEP)
    v_hat = v / (1.0 - ADAM_B2 ** ADAM_STEP)
    delta = -ADAM_LR * (m_hat / (_jnp.sqrt(v_hat) + ADAM_EPS) + ADAM_WD * w)
    return delta, m, v


def reference(x, mem, norm1_g, w_in, conv_w, conv_b, fox_f_bias, fox_q_g, fox_k_g, mem_norm_g, w_mem_kv, xa_q_g, xa_k_g, w_br_conv, w_br_fox, w_br_xa, w_o, norm2_g, w_ffn_in, w_ffn_out, loss_target, m_norm1_g, m_w_in, m_conv_w, m_conv_b, m_fox_f_bias, m_fox_q_g, m_fox_k_g, m_mem_norm_g, m_w_mem_kv, m_xa_q_g, m_xa_k_g, m_w_br_conv, m_w_br_fox, m_w_br_xa, m_w_o, m_norm2_g, m_w_ffn_in, m_w_ffn_out, v_norm1_g, v_w_in, v_conv_w, v_conv_b, v_fox_f_bias, v_fox_q_g, v_fox_k_g, v_mem_norm_g, v_w_mem_kv, v_xa_q_g, v_xa_k_g, v_w_br_conv, v_w_br_fox, v_w_br_xa, v_w_o, v_norm2_g, v_w_ffn_in, v_w_ffn_out):
    given = dict(x=x, mem=mem, norm1_g=norm1_g, w_in=w_in, conv_w=conv_w, conv_b=conv_b, fox_f_bias=fox_f_bias, fox_q_g=fox_q_g, fox_k_g=fox_k_g, mem_norm_g=mem_norm_g, w_mem_kv=w_mem_kv, xa_q_g=xa_q_g, xa_k_g=xa_k_g, w_br_conv=w_br_conv, w_br_fox=w_br_fox, w_br_xa=w_br_xa, w_o=w_o, norm2_g=norm2_g, w_ffn_in=w_ffn_in, w_ffn_out=w_ffn_out, loss_target=loss_target, m_norm1_g=m_norm1_g, m_w_in=m_w_in, m_conv_w=m_conv_w, m_conv_b=m_conv_b, m_fox_f_bias=m_fox_f_bias, m_fox_q_g=m_fox_q_g, m_fox_k_g=m_fox_k_g, m_mem_norm_g=m_mem_norm_g, m_w_mem_kv=m_w_mem_kv, m_xa_q_g=m_xa_q_g, m_xa_k_g=m_xa_k_g, m_w_br_conv=m_w_br_conv, m_w_br_fox=m_w_br_fox, m_w_br_xa=m_w_br_xa, m_w_o=m_w_o, m_norm2_g=m_norm2_g, m_w_ffn_in=m_w_ffn_in, m_w_ffn_out=m_w_ffn_out, v_norm1_g=v_norm1_g, v_w_in=v_w_in, v_conv_w=v_conv_w, v_conv_b=v_conv_b, v_fox_f_bias=v_fox_f_bias, v_fox_q_g=v_fox_q_g, v_fox_k_g=v_fox_k_g, v_mem_norm_g=v_mem_norm_g, v_w_mem_kv=v_w_mem_kv, v_xa_q_g=v_xa_q_g, v_xa_k_g=v_xa_k_g, v_w_br_conv=v_w_br_conv, v_w_br_fox=v_w_br_fox, v_w_br_xa=v_w_br_xa, v_w_o=v_w_o, v_norm2_g=v_norm2_g, v_w_ffn_in=v_w_ffn_in, v_w_ffn_out=v_w_ffn_out)
    weights = {n: given[n] for n in TWIN_WEIGHTS}
    shared = {n: given[n] for n in SHARED_INPUTS}
    per_example = {n: given[n] for n in ['x', 'mem']}
    grad_fn = _jax.value_and_grad(_loss, argnums=(0, 1))

    def one_microbatch(ex, loss_target):
        ex = dict(ex)
        diff = ex.pop(TWIN_DIFF_INPUT)
        return grad_fn(weights, diff, {**shared, **ex}, loss_target)

    if N_MICROBATCH == 1:
        loss, (grad_w, grad_x) = one_microbatch(per_example, given["loss_target"])
    else:
        def body(carry, xs):
            loss_sum, grad_sum = carry
            l_k, (gw_k, gx_k) = one_microbatch(xs[0], xs[1])
            with _jax.named_scope("update"):
                return (loss_sum + l_k, _jax.tree.map(_jnp.add, grad_sum, gw_k)), gx_k

        init = (_jnp.zeros((), _jnp.float32), _jax.tree.map(_jnp.zeros_like, weights))
        (loss, grad_w), grad_x = _jax.lax.scan(body, init, (per_example, given["loss_target"]))
    with _jax.named_scope("update"):
        delta_w, new_m, new_v = {}, {}, {}
        for n in TWIN_WEIGHTS:
            delta_w[n], new_m[n], new_v[n] = _adamw(weights[n], grad_w[n], given["m_" + n], given["v_" + n])
    return (loss, grad_x, *[grad_w[n] for n in TWIN_WEIGHTS], *[delta_w[n] for n in TWIN_WEIGHTS],
            *[new_m[n] for n in TWIN_WEIGHTS], *[new_v[n] for n in TWIN_WEIGHTS])
```

```python
import functools
import math

import jax
import jax.numpy as jnp
from jax import lax
from jax.experimental import pallas as pl
from jax.experimental.pallas import tpu as pltpu

F32, BF16 = jnp.float32, jnp.bfloat16
EPS = 1e-6
HD = 128
XH = 4
N_DEV = 8
MESH = pl.DeviceIdType.MESH
VMEM_LIMIT = 52 * 1024 * 1024
NEG = -1e30

ADAM_LR, ADAM_B1, ADAM_B2, ADAM_EPS, ADAM_WD, ADAM_STEP = 0.001, 0.9, 0.999, 1e-08, 0.01, 10


def _cp(**kw):
    return pltpu.CompilerParams(vmem_limit_bytes=VMEM_LIMIT, **kw)


def _pcall(body, **kw):
    return pl.pallas_call(body, **kw)


def _sds(shape, dtype):
    return jax.ShapeDtypeStruct(shape, dtype)


def _mm(a, b, *, m, n, k, name, ta=False, tb=False, b_off=(0, 0), res=None, out_dtype=F32, tm=512, tn=1024, tk=1024):
    tm, tn, tk = min(tm, m), min(tn, n), min(tk, k)
    assert m % tm == 0 and n % tn == 0 and k % tk == 0, (name, m, n, k, tm, tn, tk)
    nk = k // tk
    bo0, bo1 = b_off
    if ta:
        a_spec = pl.BlockSpec((tk, tm), lambda i, j, kk: (kk, i))
    else:
        a_spec = pl.BlockSpec((tm, tk), lambda i, j, kk: (i, kk))
    if tb:
        b_spec = pl.BlockSpec((tn, tk), lambda i, j, kk: (j + bo0, kk + bo1))
    else:
        b_spec = pl.BlockSpec((tk, tn), lambda i, j, kk: (kk + bo0, j + bo1))
    o_spec = pl.BlockSpec((tm, tn), lambda i, j, kk: (i, j))
    dn = (((0 if ta else 1,), (1 if tb else 0,)), ((), ()))
    has_res = res is not None

    def body(*refs):
        a_ref, b_ref = refs[0], refs[1]
        r_ref = refs[2] if has_res else None
        o_ref = refs[3] if has_res else refs[2]
        p = lax.dot_general(a_ref[...].astype(BF16), b_ref[...].astype(BF16), dn, preferred_element_type=F32)

        def finish(acc):
            if has_res:
                acc = acc + r_ref[...]
            o_ref[...] = acc.astype(out_dtype)

        if nk == 1:
            finish(p)
        else:
            acc_ref = refs[-1]
            kk = pl.program_id(2)

            @pl.when(kk == 0)
            def _():
                acc_ref[...] = p

            @pl.when(kk > 0)
            def _():
                acc_ref[...] += p

            @pl.when(kk == nk - 1)
            def _():
                finish(acc_ref[...])

    ins = [a, b] + ([res] if has_res else [])
    in_specs = [a_spec, b_spec] + ([o_spec] if has_res else [])
    return _pcall(
        body, name=name, out_shape=_sds((m, n), out_dtype), grid=(m // tm, n // tn, nk),
        in_specs=in_specs, out_specs=o_spec,
        scratch_shapes=[pltpu.VMEM((tm, tn), F32)] if nk > 1 else [],
        compiler_params=_cp(dimension_semantics=("parallel", "parallel", "arbitrary")),
    )(*ins)


def _rms_fwd(x, g, *, name, tm=512):
    t, d = x.shape
    tm = min(tm, t)

    def body(x_ref, g_ref, h_ref):
        xv = x_ref[...]
        r = lax.rsqrt(jnp.mean(xv * xv, axis=-1, keepdims=True) + EPS)
        h_ref[...] = (xv * r * g_ref[...]).astype(BF16)

    return _pcall(
        body, name=name, out_shape=_sds((t, d), BF16), grid=(t // tm,),
        in_specs=[pl.BlockSpec((tm, d), lambda i: (i, 0)), pl.BlockSpec((1, d), lambda i: (0, 0))],
        out_specs=pl.BlockSpec((tm, d), lambda i: (i, 0)), compiler_params=_cp(),
    )(x, g)


def _rms_bwd(dh, x, g, res, *, name, tm=512):
    t, d = x.shape
    tm = min(tm, t)
    has_res = res is not None

    def body(*refs):
        dh_ref, x_ref, g_ref = refs[:3]
        r_ref = refs[3] if has_res else None
        dx_ref, dg_ref = refs[-2], refs[-1]
        xv = x_ref[...]
        r = lax.rsqrt(jnp.mean(xv * xv, axis=-1, keepdims=True) + EPS)
        xh = xv * r
        dhv = dh_ref[...]
        gd = dhv * g_ref[...]
        dx = r * (gd - xh * jnp.mean(gd * xh, axis=-1, keepdims=True))
        if has_res:
            dx = dx + r_ref[...]
        dx_ref[...] = dx

        @pl.when(pl.program_id(0) == 0)
        def _():
            dg_ref[...] = jnp.zeros_like(dg_ref)

        dg_ref[...] += jnp.sum(dhv * xh, axis=0, keepdims=True)

    row = pl.BlockSpec((tm, d), lambda i: (i, 0))
    vec = pl.BlockSpec((1, d), lambda i: (0, 0))
    return _pcall(
        body, name=name, out_shape=(_sds((t, d), F32), _sds((1, d), F32)), grid=(t // tm,),
        in_specs=[row, row, vec] + ([row] if has_res else []), out_specs=(row, vec),
        compiler_params=_cp(dimension_semantics=("arbitrary",)),
    )(*([dh, x, g] + ([res] if has_res else [])))


def _conv_fwd(z, conv_w, conv_b, *, nb, s, d, tc=256):
    nc = d // tc

    def body(cb_ref, cc_ref, cv_ref, w_ref, b_ref, y_ref):
        u = cc_ref[...] * cv_ref[...]
        row = lax.broadcasted_iota(jnp.int32, u.shape, 0)
        u1 = jnp.where(row >= 1, pltpu.roll(u, 1, axis=0), 0.0)
        u2 = jnp.where(row >= 2, pltpu.roll(u, 2, axis=0), 0.0)
        w = w_ref[...]
        y = w[2:3, :] * u + w[1:2, :] * u1 + w[0:1, :] * u2 + b_ref[...]
        y_ref[...] = (cb_ref[...] * y).astype(BF16)

    def part(p):
        return pl.BlockSpec((s, tc), lambda b, j: (b, p * nc + j))

    return _pcall(
        body, name="conv_fwd", out_shape=_sds((nb * s, d), BF16), grid=(nb, nc),
        in_specs=[part(0), part(1), part(2), pl.BlockSpec((3, tc), lambda b, j: (0, j)), pl.BlockSpec((1, tc), lambda b, j: (0, j))],
        out_specs=pl.BlockSpec((s, tc), lambda b, j: (b, j)), compiler_params=_cp(),
    )(z, z, z, conv_w, conv_b)


def _conv_bwd(z, dy, conv_w, conv_b, dz, *, nb, s, d, tc=256):
    nc = d // tc

    def body(cb_ref, cc_ref, cv_ref, dy_ref, w_ref, b_ref, dz_in, dz_ref, dw_ref, db_ref, stash):
        b_i, p = pl.program_id(1), pl.program_id(2)

        @pl.when(p == 0)
        def _():
            cc, cv = cc_ref[...], cv_ref[...]
            u = cc * cv
            n = u.shape[0]
            row = lax.broadcasted_iota(jnp.int32, u.shape, 0)
            u1 = jnp.where(row >= 1, pltpu.roll(u, 1, axis=0), 0.0)
            u2 = jnp.where(row >= 2, pltpu.roll(u, 2, axis=0), 0.0)
            w = w_ref[...]
            y = w[2:3, :] * u + w[1:2, :] * u1 + w[0:1, :] * u2 + b_ref[...]
            dyv = dy_ref[...]
            dconv = dyv * cb_ref[...]
            g1 = jnp.where(row < n - 1, pltpu.roll(dconv, n - 1, axis=0), 0.0)
            g2 = jnp.where(row < n - 2, pltpu.roll(dconv, n - 2, axis=0), 0.0)
            du = w[2:3, :] * dconv + w[1:2, :] * g1 + w[0:1, :] * g2
            stash[0] = (dyv * y).astype(BF16)
            stash[1] = (du * cv).astype(BF16)
            stash[2] = (du * cc).astype(BF16)
            dws = [jnp.sum(dconv * u2, axis=0, keepdims=True), jnp.sum(dconv * u1, axis=0, keepdims=True),
                   jnp.sum(dconv * u, axis=0, keepdims=True)]
            db = jnp.sum(dconv, axis=0, keepdims=True)

            @pl.when(b_i == 0)
            def _():
                for r in range(3):
                    dw_ref[r:r + 1, :] = dws[r]
                db_ref[...] = db

            @pl.when(b_i > 0)
            def _():
                for r in range(3):
                    dw_ref[r:r + 1, :] += dws[r]
                db_ref[...] += db

        dz_ref[...] = stash[p]

    return _pcall(
        body, name="conv_bwd",
        out_shape=(_sds(dz.shape, dz.dtype), _sds((3, d), F32), _sds((1, d), F32)), grid=(nc, nb, 3),
        in_specs=[
            pl.BlockSpec((s, tc), lambda j, b, p: (b, j)), pl.BlockSpec((s, tc), lambda j, b, p: (b, nc + j)),
            pl.BlockSpec((s, tc), lambda j, b, p: (b, 2 * nc + j)), pl.BlockSpec((s, tc), lambda j, b, p: (b, j)),
            pl.BlockSpec((3, tc), lambda j, b, p: (0, j)), pl.BlockSpec((1, tc), lambda j, b, p: (0, j)),
            pl.BlockSpec(memory_space=pl.ANY),
        ],
        out_specs=(pl.BlockSpec((s, tc), lambda j, b, p: (b, p * nc + j)),
                   pl.BlockSpec((3, tc), lambda j, b, p: (0, j)), pl.BlockSpec((1, tc), lambda j, b, p: (0, j))),
        scratch_shapes=[pltpu.VMEM((3, s, tc), BF16)],
        input_output_aliases={6: 0},
        compiler_params=_cp(dimension_semantics=("arbitrary", "arbitrary", "arbitrary")),
    )(z, z, z, dy, conv_w, conv_b, dz)


def _head_rms(xv, g):
    r = lax.rsqrt(jnp.mean(xv * xv, axis=-1, keepdims=True) + EPS)
    return xv * r * g


def _head_rms_bwd(xv, g, dy):
    r = lax.rsqrt(jnp.mean(xv * xv, axis=-1, keepdims=True) + EPS)
    xh = xv * r
    gd = dy * g
    dx = r * (gd - xh * jnp.mean(gd * xh, axis=-1, keepdims=True))
    return dx, jnp.sum(dy * xh, axis=0, keepdims=True)


def _fox_prep(z, q_g, k_g, *, t, d, tm=256):
    nh = d // HD

    def body(z_ref, qg_ref, kg_ref, q_ref, k_ref, v_ref):
        qg, kg = qg_ref[...], kg_ref[...]
        for h in range(nh):
            c0 = h * HD
            q_ref[:, c0:c0 + HD] = _head_rms(z_ref[:, c0:c0 + HD], qg).astype(BF16)
            k_ref[:, c0:c0 + HD] = _head_rms(z_ref[:, d + c0:d + c0 + HD], kg).astype(BF16)
        v_ref[...] = z_ref[:, 2 * d:3 * d].astype(BF16)

    o = pl.BlockSpec((tm, d), lambda i: (i, 0))
    g = pl.BlockSpec((1, HD), lambda i: (0, 0))
    return _pcall(
        body, name="fox_prep", out_shape=(_sds((t, d), BF16),) * 3, grid=(t // tm,),
        in_specs=[pl.BlockSpec((tm, 3 * d), lambda i: (i, 1)), g, g], out_specs=(o, o, o), compiler_params=_cp(),
    )(z, q_g, k_g)


def _lane_cumsum(v, reverse=False):
    n = v.shape[1]
    lane = lax.broadcasted_iota(jnp.int32, v.shape, 1)
    sh = 1
    while sh < n:
        if reverse:
            v = v + jnp.where(lane < n - sh, pltpu.roll(v, n - sh, axis=1), 0.0)
        else:
            v = v + jnp.where(lane >= sh, pltpu.roll(v, sh, axis=1), 0.0)
        sh *= 2
    return v


def _fox_gates(z, f_bias_col, *, nb, s, nh, ff_block):
    def body(z_ref, b_ref, c_ref):
        ff = z_ref[...].T[0:nh, :] + b_ref[...]
        lf = jnp.minimum(ff, 0.0) - jnp.log(1.0 + jnp.exp(-jnp.abs(ff)))
        c_ref[0] = _lane_cumsum(lf)

    return _pcall(
        body, name="fox_gates", out_shape=_sds((nb, nh, s), F32), grid=(nb,),
        in_specs=[pl.BlockSpec((s, HD), lambda b: (b, ff_block)), pl.BlockSpec((nh, 1), lambda b: (0, 0))],
        out_specs=pl.BlockSpec((1, nh, s), lambda b: (b, 0, 0)), compiler_params=_cp(),
    )(z, f_bias_col)


def _fox_gates_bwd(z, f_bias_col, dck, dz, *, nb, s, nh, ff_block):
    def body(z_ref, b_ref, dc_ref, dz_in, dz_ref, db_ref):
        ff = z_ref[...].T[0:nh, :] + b_ref[...]
        dlf = _lane_cumsum(dc_ref[0], reverse=True)
        dff = dlf * (1.0 / (1.0 + jnp.exp(ff)))
        pad = jnp.concatenate([dff, jnp.zeros((HD - nh, s), F32)], axis=0)
        dz_ref[...] = pad.T.astype(BF16)
        dbv = jnp.sum(dff, axis=1, keepdims=True)

        @pl.when(pl.program_id(0) == 0)
        def _():
            db_ref[...] = dbv

        @pl.when(pl.program_id(0) > 0)
        def _():
            db_ref[...] += dbv

    return _pcall(
        body, name="fox_gates_bwd", out_shape=(_sds(dz.shape, dz.dtype), _sds((nh, 1), F32)), grid=(nb,),
        in_specs=[pl.BlockSpec((s, HD), lambda b: (b, ff_block)), pl.BlockSpec((nh, 1), lambda b: (0, 0)),
                  pl.BlockSpec((1, nh, s), lambda b: (b, 0, 0)), pl.BlockSpec(memory_space=pl.ANY)],
        out_specs=(pl.BlockSpec((s, HD), lambda b: (b, ff_block)), pl.BlockSpec((nh, 1), lambda b: (0, 0))),
        input_output_aliases={3: 0}, compiler_params=_cp(dimension_semantics=("arbitrary",)),
    )(z, f_bias_col, dck, dz)


def _fox_logits(q_ref, k_ref, ck_ref, i, j, tq, tk):
    sc = lax.dot_general(q_ref[...], k_ref[...], (((1,), (1,)), ((), ())), preferred_element_type=F32)
    sc = sc * (1.0 / math.sqrt(HD)) - ck_ref[0]
    rows = i * tq + lax.broadcasted_iota(jnp.int32, (tq, tk), 0)
    cols = j * tk + lax.broadcasted_iota(jnp.int32, (tq, tk), 1)
    return jnp.where(rows >= cols, sc, NEG)


def _fox_fwd(q, k, v, ck, *, nb, s, nh, tq=512):
    tk = tq
    nq = s // tq
    t = nb * s

    def body(q_ref, k_ref, v_ref, ck_ref, o_ref, o32_ref, lse_ref, m_sc, l_sc, acc_sc):
        i, j = pl.program_id(2), pl.program_id(3)

        @pl.when(j == 0)
        def _():
            m_sc[...] = jnp.full_like(m_sc, NEG)
            l_sc[...] = jnp.zeros_like(l_sc)
            acc_sc[...] = jnp.zeros_like(acc_sc)

        @pl.when(j <= i)
        def _():
            sc = _fox_logits(q_ref, k_ref, ck_ref, i, j, tq, tk)
            m_old = m_sc[...]
            m_new = jnp.maximum(m_old, jnp.max(sc, axis=-1, keepdims=True))
            alpha = jnp.exp(m_old - m_new)
            p = jnp.exp(sc - m_new)
            l_sc[...] = alpha * l_sc[...] + jnp.sum(p, axis=-1, keepdims=True)
            p_hi = p.astype(BF16)
            p_lo = (p - p_hi.astype(F32)).astype(BF16)
            vv = v_ref[...]
            pv = jnp.dot(p_hi, vv, preferred_element_type=F32) + jnp.dot(p_lo, vv, preferred_element_type=F32)
            acc_sc[...] = alpha * acc_sc[...] + pv
            m_sc[...] = m_new

        @pl.when(j == nq - 1)
        def _():
            l = l_sc[...]
            o = acc_sc[...] / l
            o_ref[...] = o.astype(BF16)
            o32_ref[...] = o
            lse_ref[...] = jnp.broadcast_to(m_sc[...] + jnp.log(l), (tq, HD))

    qs = pl.BlockSpec((tq, HD), lambda b, h, i, j: (b * nq + i, h))
    ks = pl.BlockSpec((tk, HD), lambda b, h, i, j: (b * nq + jnp.minimum(i, j), h))
    cs = pl.BlockSpec((1, 1, tk), lambda b, h, i, j: (b * nh + h, 0, jnp.minimum(i, j)))
    return _pcall(
        body, name="fox_fwd", out_shape=(_sds((t, nh * HD), BF16), _sds((t, nh * HD), F32), _sds((t, nh * HD), F32)),
        grid=(nb, nh, nq, nq), in_specs=[qs, ks, ks, cs], out_specs=(qs, qs, qs),
        scratch_shapes=[pltpu.VMEM((tq, 1), F32), pltpu.VMEM((tq, 1), F32), pltpu.VMEM((tq, HD), F32)],
        compiler_params=_cp(dimension_semantics=("parallel", "parallel", "parallel", "arbitrary")),
    )(q, k, v, ck)


def _fox_bwd_dq(q, k, v, ck, do, o, lse, *, nb, s, nh, tq=512):
    tk = tq
    nq = s // tq
    t = nb * s

    def body(q_ref, k_ref, v_ref, ck_ref, do_ref, o_ref, lse_ref, dq_ref, acc_sc, dl_sc):
        i, j = pl.program_id(2), pl.program_id(3)

        @pl.when(j == 0)
        def _():
            acc_sc[...] = jnp.zeros_like(acc_sc)
            dl_sc[...] = jnp.sum(do_ref[...].astype(BF16).astype(F32) * o_ref[...], axis=-1, keepdims=True)

        @pl.when(j <= i)
        def _():
            sc = _fox_logits(q_ref, k_ref, ck_ref, i, j, tq, tk)
            p = jnp.exp(sc - lse_ref[:, 0:1])
            dp = lax.dot_general(do_ref[...].astype(BF16), v_ref[...], (((1,), (1,)), ((), ())), preferred_element_type=F32)
            ds = p * (dp - dl_sc[...])
            acc_sc[...] += jnp.dot(ds.astype(BF16), k_ref[...], preferred_element_type=F32)

        @pl.when(j == nq - 1)
        def _():
            dq_ref[...] = acc_sc[...] * (1.0 / math.sqrt(HD))

    qs = pl.BlockSpec((tq, HD), lambda b, h, i, j: (b * nq + i, h))
    ks = pl.BlockSpec((tk, HD), lambda b, h, i, j: (b * nq + jnp.minimum(i, j), h))
    cs = pl.BlockSpec((1, 1, tk), lambda b, h, i, j: (b * nh + h, 0, jnp.minimum(i, j)))
    return _pcall(
        body, name="fox_bwd_dq", out_shape=_sds((t, nh * HD), F32), grid=(nb, nh, nq, nq),
        in_specs=[qs, ks, ks, cs, qs, qs, qs], out_specs=qs,
        scratch_shapes=[pltpu.VMEM((tq, HD), F32), pltpu.VMEM((tq, 1), F32)],
        compiler_params=_cp(dimension_semantics=("parallel", "parallel", "parallel", "arbitrary")),
    )(q, k, v, ck, do, o, lse)


def _fox_bwd_dkv(q, k, v, ck, do, o, lse, *, nb, s, nh, tq=512):
    tk = tq
    nq = s // tq
    t = nb * s

    def body(q_ref, k_ref, v_ref, ck_ref, do_ref, o_ref, lse_ref, dk_ref, dv_ref, dc_ref, dk_sc, dv_sc, dc_sc):
        j, i = pl.program_id(2), pl.program_id(3)

        @pl.when(i == 0)
        def _():
            dk_sc[...] = jnp.zeros_like(dk_sc)
            dv_sc[...] = jnp.zeros_like(dv_sc)
            dc_sc[...] = jnp.zeros_like(dc_sc)

        @pl.when(i >= j)
        def _():
            sc = _fox_logits(q_ref, k_ref, ck_ref, i, j, tq, tk)
            p = jnp.exp(sc - lse_ref[:, 0:1])
            dov = do_ref[...]
            dob = dov.astype(BF16)
            dv_sc[...] += lax.dot_general(p.astype(BF16), dob, (((0,), (0,)), ((), ())), preferred_element_type=F32)
            dp = lax.dot_general(dob, v_ref[...], (((1,), (1,)), ((), ())), preferred_element_type=F32)
            delta = jnp.sum(dob.astype(F32) * o_ref[...], axis=-1, keepdims=True)
            ds = p * (dp - delta)
            dk_sc[...] += lax.dot_general(ds.astype(BF16), q_ref[...], (((0,), (0,)), ((), ())), preferred_element_type=F32)
            dc_sc[...] += jnp.sum(ds, axis=0, keepdims=True)

        @pl.when(i == nq - 1)
        def _():
            dk_ref[...] = dk_sc[...] * (1.0 / math.sqrt(HD))
            dv_ref[...] = dv_sc[...].astype(BF16)
            dc_ref[0] = -dc_sc[...]

    qs = pl.BlockSpec((tq, HD), lambda b, h, j, i: (b * nq + jnp.maximum(i, j), h))
    ks = pl.BlockSpec((tk, HD), lambda b, h, j, i: (b * nq + j, h))
    cs = pl.BlockSpec((1, 1, tk), lambda b, h, j, i: (b * nh + h, 0, j))
    return _pcall(
        body, name="fox_bwd_dkv",
        out_shape=(_sds((t, nh * HD), F32), _sds((t, nh * HD), BF16), _sds((nb * nh, 1, s), F32)), grid=(nb, nh, nq, nq),
        in_specs=[qs, ks, ks, cs, qs, qs, qs], out_specs=(ks, ks, cs),
        scratch_shapes=[pltpu.VMEM((tk, HD), F32), pltpu.VMEM((tk, HD), F32), pltpu.VMEM((1, tk), F32)],
        compiler_params=_cp(dimension_semantics=("parallel", "parallel", "parallel", "arbitrary")),
    )(q, k, v, ck, do, o, lse)


def _fox_post(z, dqn, dkn, dv, q_g, k_g, dz, *, t, d, tm=256):
    nh = d // HD

    def body(z_ref, dq_ref, dk_ref, dv_ref, qg_ref, kg_ref, dz_in, dz_ref, dqg_ref, dkg_ref):
        qg, kg = qg_ref[...], kg_ref[...]
        aq = jnp.zeros((1, HD), F32)
        ak = jnp.zeros((1, HD), F32)
        for h in range(nh):
            c0 = h * HD
            dx, sg = _head_rms_bwd(z_ref[:, c0:c0 + HD], qg, dq_ref[:, c0:c0 + HD])
            dz_ref[:, c0:c0 + HD] = dx.astype(BF16)
            aq = aq + sg
            dx, sg = _head_rms_bwd(z_ref[:, d + c0:d + c0 + HD], kg, dk_ref[:, c0:c0 + HD])
            dz_ref[:, d + c0:d + c0 + HD] = dx.astype(BF16)
            ak = ak + sg
        dz_ref[:, 2 * d:3 * d] = dv_ref[...]

        @pl.when(pl.program_id(0) == 0)
        def _():
            dqg_ref[...] = aq
            dkg_ref[...] = ak

        @pl.when(pl.program_id(0) > 0)
        def _():
            dqg_ref[...] += aq
            dkg_ref[...] += ak

    o = pl.BlockSpec((tm, d), lambda i: (i, 0))
    g = pl.BlockSpec((1, HD), lambda i: (0, 0))
    z3 = pl.BlockSpec((tm, 3 * d), lambda i: (i, 1))
    return _pcall(
        body, name="fox_post", out_shape=(_sds(dz.shape, dz.dtype), _sds((1, HD), F32), _sds((1, HD), F32)), grid=(t // tm,),
        in_specs=[z3, o, o, o, g, g, pl.BlockSpec(memory_space=pl.ANY)], out_specs=(z3, g, g),
        input_output_aliases={6: 0}, compiler_params=_cp(dimension_semantics=("arbitrary",)),
    )(z, dqn, dkn, dv, q_g, k_g, dz)


def _xa_prep_kv(kv, k_g, *, rows, d):
    xd = d // XH

    def body(kv_ref, g_ref, k_ref, v_ref):
        g = g_ref[...]
        for h in range(XH):
            c0 = h * xd
            k_ref[:, c0:c0 + xd] = _head_rms(kv_ref[:, c0:c0 + xd], g).astype(BF16)
        v_ref[...] = kv_ref[:, d:2 * d].astype(BF16)

    return _pcall(body, name="xa_prep_kv", out_shape=(_sds((rows, d), BF16),) * 2, compiler_params=_cp())(kv, k_g)


def _xa_fwd(z, kn, vb, q_g, *, nb, s, d, nm, q_block, tq=512):
    xd = d // XH
    nq = s // tq

    def body(z_ref, k_ref, v_ref, g_ref, y_ref):
        g = g_ref[...]
        for h in range(XH):
            c0 = h * xd
            qn = _head_rms(z_ref[:, c0:c0 + xd], g).astype(BF16)
            sc = lax.dot_general(qn, k_ref[:, c0:c0 + xd], (((1,), (1,)), ((), ())), preferred_element_type=F32)
            sc = sc / math.sqrt(xd)
            p = jnp.exp(sc - jnp.max(sc, axis=-1, keepdims=True))
            p = p / jnp.sum(p, axis=-1, keepdims=True)
            y_ref[:, c0:c0 + xd] = jnp.dot(p.astype(BF16), v_ref[:, c0:c0 + xd], preferred_element_type=F32).astype(BF16)

    kvs = pl.BlockSpec((nm, d), lambda b, i: (b, 0))
    return _pcall(
        body, name="xa_fwd", out_shape=_sds((nb * s, d), BF16), grid=(nb, nq),
        in_specs=[pl.BlockSpec((tq, d), lambda b, i: (b * nq + i, q_block)), kvs, kvs, pl.BlockSpec((1, xd), lambda b, i: (0, 0))],
        out_specs=pl.BlockSpec((tq, d), lambda b, i: (b * nq + i, 0)), compiler_params=_cp(),
    )(z, kn, vb, q_g)


def _xa_bwd(z, kn, vb, q_g, dy, dz, *, nb, s, d, nm, q_block, tq=512):
    xd = d // XH
    nq = s // tq

    def body(z_ref, k_ref, v_ref, g_ref, dy_ref, dz_in, dz_ref, dk_ref, dv_ref, dg_ref):
        b_i, i = pl.program_id(0), pl.program_id(1)
        g = g_ref[...]

        @pl.when(i == 0)
        def _():
            dk_ref[...] = jnp.zeros_like(dk_ref)
            dv_ref[...] = jnp.zeros_like(dv_ref)

        @pl.when((i == 0) & (b_i == 0))
        def _():
            dg_ref[...] = jnp.zeros_like(dg_ref)

        for h in range(XH):
            c0 = h * xd
            xq = z_ref[:, c0:c0 + xd]
            qn = _head_rms(xq, g).astype(BF16)
            kh, vh = k_ref[:, c0:c0 + xd], v_ref[:, c0:c0 + xd]
            sc = lax.dot_general(qn, kh, (((1,), (1,)), ((), ())), preferred_element_type=F32) / math.sqrt(xd)
            p = jnp.exp(sc - jnp.max(sc, axis=-1, keepdims=True))
            p = p / jnp.sum(p, axis=-1, keepdims=True)
            dob = dy_ref[:, c0:c0 + xd].astype(BF16)
            dv_ref[:, c0:c0 + xd] += lax.dot_general(p.astype(BF16), dob, (((0,), (0,)), ((), ())), preferred_element_type=F32)
            dp = lax.dot_general(dob, vh, (((1,), (1,)), ((), ())), preferred_element_type=F32)
            ds = p * (dp - jnp.sum(p * dp, axis=-1, keepdims=True)) / math.sqrt(xd)
            dsb = ds.astype(BF16)
            dqn = jnp.dot(dsb, kh, preferred_element_type=F32)
            dk_ref[:, c0:c0 + xd] += lax.dot_general(dsb, qn, (((0,), (0,)), ((), ())), preferred_element_type=F32)
            dx, sg = _head_rms_bwd(xq, g, dqn)
            dz_ref[:, c0:c0 + xd] = dx.astype(BF16)
            dg_ref[...] += sg

    kvs = pl.BlockSpec((nm, d), lambda b, i: (b, 0))
    qs = pl.BlockSpec((tq, d), lambda b, i: (b * nq + i, q_block))
    gs = pl.BlockSpec((1, xd), lambda b, i: (0, 0))
    return _pcall(
        body, name="xa_bwd",
        out_shape=(_sds(dz.shape, dz.dtype), _sds((nb * nm, d), F32), _sds((nb * nm, d), F32), _sds((1, xd), F32)), grid=(nb, nq),
        in_specs=[qs, kvs, kvs, gs, pl.BlockSpec((tq, d), lambda b, i: (b * nq + i, 0)), pl.BlockSpec(memory_space=pl.ANY)],
        out_specs=(qs, kvs, kvs, gs), input_output_aliases={5: 0},
        compiler_params=_cp(dimension_semantics=("arbitrary", "arbitrary")),
    )(z, kn, vb, q_g, dy, dz)


def _xa_post_kv(kv, dkn, dv, k_g, *, rows, d):
    xd = d // XH

    def body(kv_ref, dk_ref, dv_ref, g_ref, dkv_ref, dg_ref):
        g = g_ref[...]
        acc = jnp.zeros((1, xd), F32)
        for h in range(XH):
            c0 = h * xd
            dx, sg = _head_rms_bwd(kv_ref[:, c0:c0 + xd], g, dk_ref[:, c0:c0 + xd])
            dkv_ref[:, c0:c0 + xd] = dx.astype(BF16)
            acc = acc + sg
        dkv_ref[:, d:2 * d] = dv_ref[...].astype(BF16)
        dg_ref[...] = acc

    return _pcall(body, name="xa_post_kv", out_shape=(_sds((rows, 2 * d), BF16), _sds((1, xd), F32)), compiler_params=_cp())(
        kv, dkn, dv, k_g)


def _sigmoid(v):
    return 1.0 / (1.0 + jnp.exp(-v))


def _merge_fwd(z, pa, pb, pc, *, t, d, gate_block, tm=512):
    def body(ga_ref, gb_ref, gc_ref, pa_ref, pb_ref, pc_ref, o_ref):
        o_ref[...] = (_sigmoid(ga_ref[...]) * pa_ref[...] + _sigmoid(gb_ref[...]) * pb_ref[...]
                      + _sigmoid(gc_ref[...]) * pc_ref[...]).astype(BF16)

    def gate(q):
        return pl.BlockSpec((tm, d), lambda i: (i, gate_block + q))

    o = pl.BlockSpec((tm, d), lambda i: (i, 0))
    return _pcall(
        body, name="merge_fwd", out_shape=_sds((t, d), BF16), grid=(t // tm,),
        in_specs=[gate(0), gate(1), gate(2), o, o, o], out_specs=o, compiler_params=_cp(),
    )(z, z, z, pa, pb, pc)


def _gate_bwd(z, dm, pr, dz, *, t, d, col_block, name, tm=512):
    def body(g_ref, dm_ref, p_ref, dz_in, dz_ref, dp_ref):
        sg = _sigmoid(g_ref[...])
        dmv = dm_ref[...]
        dp_ref[...] = (dmv * sg).astype(BF16)
        dz_ref[...] = (dmv * p_ref[...] * sg * (1.0 - sg)).astype(BF16)

    gs = pl.BlockSpec((tm, d), lambda i: (i, col_block))
    o = pl.BlockSpec((tm, d), lambda i: (i, 0))
    return _pcall(
        body, name=name, out_shape=(_sds(dz.shape, dz.dtype), _sds((t, d), BF16)), grid=(t // tm,),
        in_specs=[gs, o, o, pl.BlockSpec(memory_space=pl.ANY)], out_specs=(gs, o), input_output_aliases={3: 0},
        compiler_params=_cp(),
    )(z, dm, pr, dz)


def _swiglu_fwd(gu, *, t, dff, tm=256):
    def body(g_ref, u_ref, a_ref):
        gv = g_ref[...]
        a_ref[...] = (gv * _sigmoid(gv) * u_ref[...]).astype(BF16)

    return _pcall(
        body, name="swiglu_fwd", out_shape=_sds((t, dff), BF16), grid=(t // tm,),
        in_specs=[pl.BlockSpec((tm, dff), lambda i: (i, 0)), pl.BlockSpec((tm, dff), lambda i: (i, 1))],
        out_specs=pl.BlockSpec((tm, dff), lambda i: (i, 0)), compiler_params=_cp(),
    )(gu, gu)


def _swiglu_bwd(gu, da, *, t, dff, tm=256):
    def body(g_ref, u_ref, da_ref, dg_ref, du_ref):
        gv, dav = g_ref[...], da_ref[...]
        sg = _sigmoid(gv)
        dg_ref[...] = (dav * u_ref[...] * sg * (1.0 + gv * (1.0 - sg))).astype(BF16)
        du_ref[...] = (dav * gv * sg).astype(BF16)

    h0 = pl.BlockSpec((tm, dff), lambda i: (i, 0))
    h1 = pl.BlockSpec((tm, dff), lambda i: (i, 1))
    return _pcall(
        body, name="swiglu_bwd", out_shape=(_sds((t, dff), BF16), _sds((t, dff), BF16)), grid=(t // tm,),
        in_specs=[h0, h1, h0], out_specs=(h0, h0), compiler_params=_cp(),
    )(gu, gu, da)


def _loss_head(y, target, *, t, d, tm=512):
    def body(y_ref, t_ref, dy_ref, l_ref):
        e = y_ref[...] - t_ref[...]
        dy_ref[...] = e * (1.0 / d)

        @pl.when(pl.program_id(0) == 0)
        def _():
            l_ref[...] = jnp.zeros_like(l_ref)

        l_ref[...] += jnp.sum(jnp.sum(e * e, axis=-1, keepdims=True), axis=0, keepdims=True) * (0.5 / d)

    o = pl.BlockSpec((tm, d), lambda i: (i, 0))
    dy, l = _pcall(
        body, name="loss_head", out_shape=(_sds((t, d), F32), _sds((1, HD), F32)), grid=(t // tm,),
        in_specs=[o, o], out_specs=(o, pl.BlockSpec((1, HD), lambda i: (0, 0))),
        compiler_params=_cp(dimension_semantics=("arbitrary",)),
    )(y, target)
    return dy, l[0, 0]


def _mesh_pos():
    return lax.axis_index("x"), lax.axis_index("y"), lax.axis_index("c")


def _all_gather(blocks):
    n = len(blocks)

    def body(*refs):
        ins, outs = refs[:n], refs[n:2 * n]
        send_sems, recv_sems, local_sems = refs[2 * n:]
        x, y, c = _mesh_pos()
        me, sibling = (x, y, c), (x, y, 1 - c)
        chips = [(1 - x, y), (x, 1 - y), (1 - x, 1 - y)]

        def slab(t, dev):
            return outs[t].at[4 * dev[0] + 2 * dev[1] + dev[2]]

        def copy(t, k, block, to, src=None):
            dst = slab(t, block)
            return pltpu.make_async_remote_copy(
                src_ref=dst if src is None else src, dst_ref=dst, send_sem=send_sems.at[t, k], recv_sem=recv_sems.at[t, k],
                device_id=to, device_id_type=MESH)

        mine = [pltpu.make_async_copy(ins[t], slab(t, me), local_sems.at[t]) for t in range(n)]
        for cp in mine:
            cp.start()
        first = []
        for t in range(n):
            first.append(copy(t, 0, me, sibling, src=ins[t]))
            first += [copy(t, 1 + j, me, (*chip, c), src=ins[t]) for j, chip in enumerate(chips)]
        for cp in first:
            cp.start()
        passed = []
        for j, chip in enumerate(chips):
            for t in range(n):
                copy(t, 1 + j, (*chip, c), me).wait_recv()
                fwd = copy(t, 4 + j, (*chip, c), sibling)
                fwd.start()
                passed.append(fwd)
        for t in range(n):
            copy(t, 0, sibling, me).wait_recv()
            for j, chip in enumerate(chips):
                copy(t, 4 + j, (*chip, 1 - c), me).wait_recv()
        for cp in first + passed:
            cp.wait_send()
        for cp in mine:
            cp.wait()

    hbm = pl.BlockSpec(memory_space=pl.ANY)
    return _pcall(
        body, name="all_gather", out_shape=tuple(_sds((N_DEV,) + b.shape, b.dtype) for b in blocks),
        in_specs=[hbm] * n, out_specs=tuple([hbm] * n),
        scratch_shapes=[pltpu.SemaphoreType.DMA((n, 7)), pltpu.SemaphoreType.DMA((n, 7)), pltpu.SemaphoreType.DMA((n,))],
        compiler_params=_cp(),
    )(*blocks)


def _sibling_exchange(send):
    def body(s_ref, r_ref, send_sem, recv_sem):
        x, y, c = _mesh_pos()
        cp = pltpu.make_async_remote_copy(src_ref=s_ref, dst_ref=r_ref, send_sem=send_sem, recv_sem=recv_sem,
                                          device_id=(x, y, 1 - c), device_id_type=MESH)
        cp.start()
        cp.wait()

    hbm = pl.BlockSpec(memory_space=pl.ANY)
    return _pcall(
        body, name="rs_sibling", out_shape=_sds(send.shape, send.dtype), in_specs=[hbm], out_specs=hbm,
        scratch_shapes=[pltpu.SemaphoreType.DMA, pltpu.SemaphoreType.DMA],
        compiler_params=_cp(),
    )(send)


def _chip_exchange(part):
    def body(p_ref, r_ref, send_sems, recv_sems):
        x, y, c = _mesh_pos()
        chips = [(1 - x, y), (x, 1 - y), (1 - x, 1 - y)]
        cps = [
            pltpu.make_async_remote_copy(src_ref=p_ref.at[2 * cx + cy], dst_ref=r_ref.at[j], send_sem=send_sems.at[j],
                                         recv_sem=recv_sems.at[j], device_id=(cx, cy, c), device_id_type=MESH)
            for j, (cx, cy) in enumerate(chips)
        ]
        for cp in cps:
            cp.start()
        for cp in cps:
            cp.wait()

    hbm = pl.BlockSpec(memory_space=pl.ANY)
    return _pcall(
        body, name="rs_chips", out_shape=_sds((3,) + part.shape[1:], part.dtype), in_specs=[hbm], out_specs=hbm,
        scratch_shapes=[pltpu.SemaphoreType.DMA((3,)), pltpu.SemaphoreType.DMA((3,))],
        compiler_params=_cp(),
    )(part)


def _row_tile(rows, cap=1040):
    return max(tt for tt in range(16, cap + 1, 16) if rows % tt == 0)


def _add2(a, b):
    nq, rows, cols = a.shape
    tm = _row_tile(rows)

    def body(a_ref, b_ref, o_ref):
        o_ref[...] = (a_ref[...].astype(F32) + b_ref[...].astype(F32)).astype(BF16)

    spec = pl.BlockSpec((1, tm, cols), lambda q, i: (q, i, 0))
    return _pcall(body, name="rs_add_sibling", out_shape=_sds(a.shape, BF16), grid=(nq, rows // tm),
                  in_specs=[spec, spec], out_specs=spec, compiler_params=_cp())(a, b)


def _add4(own, recv):
    rows, cols = own.shape
    tm = _row_tile(rows)

    def body(o_ref, r_ref, g_ref):
        g_ref[...] = ((o_ref[...].astype(F32) + r_ref[0].astype(F32)) + r_ref[1].astype(F32)) + r_ref[2].astype(F32)

    return _pcall(
        body, name="rs_add_chips", out_shape=_sds((rows, cols), F32), grid=(rows // tm,),
        in_specs=[pl.BlockSpec((tm, cols), lambda i: (i, 0)), pl.BlockSpec((3, tm, cols), lambda i: (0, i, 0))],
        out_specs=pl.BlockSpec((tm, cols), lambda i: (i, 0)), compiler_params=_cp(),
    )(own, recv)


def _all_reduce_small(v):
    rows, cols = v.shape

    def body(v_ref, o_ref, slots, send_sems, recv_sems):
        x, y, c = _mesh_pos()
        me = 4 * x + 2 * y + c
        slots[me] = v_ref[...]
        cps = []
        for k in range(1, N_DEV):
            px, py, pc = x ^ (k >> 2), y ^ ((k >> 1) & 1), c ^ (k & 1)
            cps.append(pltpu.make_async_remote_copy(
                src_ref=v_ref, dst_ref=slots.at[me], send_sem=send_sems.at[k - 1], recv_sem=recv_sems.at[k - 1],
                device_id=(px, py, pc), device_id_type=MESH))
        for cp in cps:
            cp.start()
        for cp in cps:
            cp.wait()
        acc = slots[0]
        for k in range(1, N_DEV):
            acc = acc + slots[k]
        o_ref[...] = acc

    return _pcall(
        body, name="all_reduce_small", out_shape=_sds((rows, cols), F32),
        in_specs=[pl.BlockSpec(memory_space=pltpu.VMEM)], out_specs=pl.BlockSpec(memory_space=pltpu.VMEM),
        scratch_shapes=[pltpu.VMEM((N_DEV, rows, cols), F32), pltpu.SemaphoreType.DMA((7,)), pltpu.SemaphoreType.DMA((7,))],
        compiler_params=_cp(),
    )(v)


def _adamw(w, g, m, v, *, name):
    rows, cols = w.shape
    tm = rows
    for cand in (256, 128, 64, 32, 16, 8):
        if rows > cand and rows % cand == 0:
            tm = cand
            break
    c1 = 1.0 - ADAM_B1 ** ADAM_STEP
    c2 = 1.0 - ADAM_B2 ** ADAM_STEP

    def body(w_ref, g_ref, m_ref, v_ref, d_ref, nm_ref, nv_ref):
        gv = g_ref[...]
        mn = ADAM_B1 * m_ref[...] + (1.0 - ADAM_B1) * gv
        vn = ADAM_B2 * v_ref[...] + (1.0 - ADAM_B2) * (gv * gv)
        d_ref[...] = -ADAM_LR * ((mn / c1) / (jnp.sqrt(vn / c2) + ADAM_EPS) + ADAM_WD * w_ref[...])
        nm_ref[...] = mn
        nv_ref[...] = vn

    spec = pl.BlockSpec((tm, cols), lambda i: (i, 0))
    return _pcall(
        body, name=name, out_shape=(_sds((rows, cols), F32),) * 3, grid=(rows // tm,),
        in_specs=[spec] * 4, out_specs=(spec,) * 3, compiler_params=_cp(),
    )(w, g, m, v)


def _pad_rows(a, rows):
    return a if a.shape[0] == rows else jnp.pad(a, ((0, rows - a.shape[0]),) + ((0, 0),) * (a.ndim - 1))


def kernel(x, mem, norm1_g, w_in, conv_w, conv_b, fox_f_bias, fox_q_g, fox_k_g, mem_norm_g, w_mem_kv, xa_q_g, xa_k_g, w_br_conv, w_br_fox, w_br_xa, w_o, norm2_g, w_ffn_in, w_ffn_out, loss_target, m_norm1_g, m_w_in, m_conv_w, m_conv_b, m_fox_f_bias, m_fox_q_g, m_fox_k_g, m_mem_norm_g, m_w_mem_kv, m_xa_q_g, m_xa_k_g, m_w_br_conv, m_w_br_fox, m_w_br_xa, m_w_o, m_norm2_g, m_w_ffn_in, m_w_ffn_out, v_norm1_g, v_w_in, v_conv_w, v_conv_b, v_fox_f_bias, v_fox_q_g, v_fox_k_g, v_mem_norm_g, v_w_mem_kv, v_xa_q_g, v_xa_k_g, v_w_br_conv, v_w_br_fox, v_w_br_xa, v_w_o, v_norm2_g, v_w_ffn_in, v_w_ffn_out):
    nb, s, d = x.shape
    t = nb * s
    nm = mem.shape[1]
    nh = d // HD
    dff = w_ffn_out.shape[1] * N_DEV
    n_in_sh = w_in.shape[2]
    n_in = n_in_sh * N_DEV
    n_in_pad = -(-n_in // 1152) * 1152
    ff_block = (10 * d) // HD
    my_c = lax.axis_index("c")
    my_q = 2 * lax.axis_index("x") + lax.axis_index("y")

    row_sizes = [n_in_sh, w_mem_kv.shape[2], w_br_conv.shape[1], w_br_fox.shape[1], w_br_xa.shape[1], w_o.shape[1],
                 w_ffn_in.shape[2], w_ffn_out.shape[1]]
    padded = [-(-r // 16) * 16 for r in row_sizes]
    offs = [sum(padded[:i]) for i in range(len(padded))]
    pack_rows = sum(padded)
    shard_blocks = [w_in[0].T, w_mem_kv[0].T, w_br_conv[0], w_br_fox[0], w_br_xa[0], w_o[0], w_ffn_in[0].T, w_ffn_out[0]]
    pack = jnp.concatenate([_pad_rows(b.astype(BF16), p) for b, p in zip(shard_blocks, padded)], axis=0)
    cw_block = _pad_rows(conv_w[0], 8)
    gathered, cw_g = _all_gather([pack, cw_block])

    def full(i):
        return gathered[:, offs[i]:offs[i] + row_sizes[i], :].reshape(N_DEV * row_sizes[i], d)

    win_t = _pad_rows(full(0), n_in_pad)
    wkv_t, wbc, wbf, wbx, wo, wfi_t, wfo = (full(i) for i in range(1, 8))
    conv_w_full = jnp.transpose(cw_g[:, 0:3, :], (1, 0, 2)).reshape(3, d)

    x2 = x.reshape(t, d)
    mem2 = mem.reshape(nb * nm, d)
    tgt2 = loss_target.reshape(t, d)
    h1 = _rms_fwd(x2, norm1_g, name="rms1_fwd")
    z = _mm(h1, win_t, m=t, n=n_in_pad, k=d, tb=True, tm=1024, tn=1152, name="mm_z")
    y_conv = _conv_fwd(z, conv_w_full, conv_b, nb=nb, s=s, d=d)
    qn, kn, vb = _fox_prep(z, fox_q_g, fox_k_g, t=t, d=d)
    f_bias_col = fox_f_bias.reshape(nh, 1)
    ck = _fox_gates(z, f_bias_col, nb=nb, s=s, nh=nh, ff_block=ff_block)
    ck_rows = ck.reshape(nb * nh, 1, s)
    y_fox, o_fox, lse = _fox_fwd(qn, kn, vb, ck_rows, nb=nb, s=s, nh=nh)
    mem_n = _rms_fwd(mem2, mem_norm_g, name="rms_mem_fwd")
    kv = _mm(mem_n, wkv_t, m=nb * nm, n=2 * d, k=d, tb=True, name="mm_kv")
    xkn, xvb = _xa_prep_kv(kv, xa_k_g, rows=nb * nm, d=d)
    y_xa = _xa_fwd(z, xkn, xvb, xa_q_g, nb=nb, s=s, d=d, nm=nm, q_block=6)
    pa = _mm(y_conv, wbc, m=t, n=d, k=d, name="mm_pa")
    pb = _mm(y_fox, wbf, m=t, n=d, k=d, name="mm_pb")
    pc = _mm(y_xa, wbx, m=t, n=d, k=d, name="mm_pc")
    merged = _merge_fwd(z, pa, pb, pc, t=t, d=d, gate_block=7)
    x1 = _mm(merged, wo, m=t, n=d, k=d, res=x2, name="mm_x1")
    h2 = _rms_fwd(x1, norm2_g, name="rms2_fwd")
    gu = _mm(h2, wfi_t, m=t, n=2 * dff, k=d, tb=True, tm=1024, tn=1408, name="mm_gu")
    act = _swiglu_fwd(gu, t=t, dff=dff)
    y = _mm(act, wfo, m=t, n=d, k=dff, res=x1, tk=1408, name="mm_y")
    dy, loss_local = _loss_head(y, tgt2, t=t, d=d)

    da = _mm(dy, wfo, m=t, n=dff, k=d, tb=True, tn=1408, name="mm_da")
    g_wfo = _mm(act, dy, m=dff, n=d, k=t, ta=True, tm=1408, tk=512, name="mm_g_wfo")
    dgate, dup = _swiglu_bwd(gu, da, t=t, dff=dff)
    dh2 = _mm(dgate, wfi_t, m=t, n=d, k=dff, tk=1408, name="mm_dh2_g")
    dh2 = _mm(dup, wfi_t, m=t, n=d, k=dff, tk=1408, b_off=(2, 0), res=dh2, name="mm_dh2_u")
    g_wfi_g = _mm(dgate, h2, m=dff, n=d, k=t, ta=True, tm=1408, tk=512, name="mm_g_wfi_g")
    g_wfi_u = _mm(dup, h2, m=dff, n=d, k=t, ta=True, tm=1408, tk=512, name="mm_g_wfi_u")
    dx1, g_norm2 = _rms_bwd(dh2, x1, norm2_g, dy, name="rms2_bwd")
    dmerged = _mm(dx1, wo, m=t, n=d, k=d, tb=True, name="mm_dmerged")
    g_wo = _mm(merged, dx1, m=d, n=d, k=t, ta=True, tk=512, name="mm_g_wo")

    dz = lax.empty((t, n_in_pad), BF16)
    dz, dpa = _gate_bwd(z, dmerged, pa, dz, t=t, d=d, col_block=7, name="gate_bwd_a")
    dz, dpb = _gate_bwd(z, dmerged, pb, dz, t=t, d=d, col_block=8, name="gate_bwd_b")
    dz, dpc = _gate_bwd(z, dmerged, pc, dz, t=t, d=d, col_block=9, name="gate_bwd_c")
    dy_conv = _mm(dpa, wbc, m=t, n=d, k=d, tb=True, name="mm_dy_conv")
    g_wbc = _mm(y_conv, dpa, m=d, n=d, k=t, ta=True, tk=512, name="mm_g_wbc")
    dy_fox = _mm(dpb, wbf, m=t, n=d, k=d, tb=True, name="mm_dy_fox")
    g_wbf = _mm(y_fox, dpb, m=d, n=d, k=t, ta=True, tk=512, name="mm_g_wbf")
    dy_xa = _mm(dpc, wbx, m=t, n=d, k=d, tb=True, name="mm_dy_xa")
    g_wbx = _mm(y_xa, dpc, m=d, n=d, k=t, ta=True, tk=512, name="mm_g_wbx")

    dz, g_conv_w, g_conv_b = _conv_bwd(z, dy_conv, conv_w_full, conv_b, dz, nb=nb, s=s, d=d)

    dqn = _fox_bwd_dq(qn, kn, vb, ck_rows, dy_fox, o_fox, lse, nb=nb, s=s, nh=nh)
    dkn, dvb, dck = _fox_bwd_dkv(qn, kn, vb, ck_rows, dy_fox, o_fox, lse, nb=nb, s=s, nh=nh)
    dz, g_fox_q, g_fox_k = _fox_post(z, dqn, dkn, dvb, fox_q_g, fox_k_g, dz, t=t, d=d)
    dz, g_f_bias = _fox_gates_bwd(z, f_bias_col, dck.reshape(nb, nh, s), dz, nb=nb, s=s, nh=nh, ff_block=ff_block)

    dz, dxkn, dxv, g_xa_q = _xa_bwd(z, xkn, xvb, xa_q_g, dy_xa, dz, nb=nb, s=s, d=d, nm=nm, q_block=6)
    dkv, g_xa_k = _xa_post_kv(kv, dxkn, dxv, xa_k_g, rows=nb * nm, d=d)
    g_wkv_t = _mm(dkv, mem_n, m=2 * d, n=d, k=nb * nm, ta=True, name="mm_g_wkv")
    dmem_n = _mm(dkv, wkv_t, m=nb * nm, n=d, k=2 * d, name="mm_dmem")
    _, g_mem_norm = _rms_bwd(dmem_n, mem2, mem_norm_g, None, name="rms_mem_bwd")

    dh1 = _mm(dz, win_t, m=t, n=d, k=n_in_pad, tk=1152, name="mm_dh1")
    g_win_t = _mm(dz, h1, m=n_in_pad, n=d, k=t, ta=True, tm=1152, tk=512, name="mm_g_win")
    dx, g_norm1 = _rms_bwd(dh1, x2, norm1_g, dx1, name="rms1_bwd")

    tail = jnp.concatenate(
        [_pad_rows(g_f_bias.reshape(nh), HD).reshape(1, HD), g_fox_q, g_fox_k, g_xa_q, g_xa_k,
         jnp.zeros((1, d - 3 * HD - 2 * (d // XH)), F32)], axis=1)
    small = jnp.concatenate([g_norm1, g_norm2, g_conv_w, g_conv_b, g_mem_norm, tail], axis=0)
    small = _all_reduce_small(small)
    xd = d // XH
    gs_norm1, gs_norm2, gs_conv_w_full, gs_conv_b, gs_mem_norm = small[0:1], small[1:2], small[2:5], small[5:6], small[6:7]
    gs_f_bias = small[7:8, 0:nh]
    gs_fox_q, gs_fox_k = small[7:8, HD:2 * HD], small[7:8, 2 * HD:3 * HD]
    gs_xa_q, gs_xa_k = small[7:8, 3 * HD:3 * HD + xd], small[7:8, 3 * HD + xd:3 * HD + 2 * xd]
    gs_conv_w = lax.dynamic_slice_in_dim(gs_conv_w_full, (4 * lax.axis_index("x") + 2 * lax.axis_index("y") + my_c) * (d // N_DEV),
                                         d // N_DEV, axis=1)

    full_grads = [g_win_t[:n_in], g_wkv_t, g_wbc, g_wbf, g_wbx, g_wo, jnp.concatenate([g_wfi_g, g_wfi_u], axis=0), g_wfo]

    def pack_for(core):
        parts = []
        for gfull, r, p in zip(full_grads, row_sizes, padded):
            blk = gfull.reshape(N_DEV // 2, 2, r, d)
            blk = lax.dynamic_index_in_dim(blk, core, axis=1, keepdims=False).astype(BF16)
            if p != r:
                blk = jnp.pad(blk, ((0, 0), (0, p - r), (0, 0)))
            parts.append(blk)
        return jnp.concatenate(parts, axis=1)

    keep, send = pack_for(my_c), pack_for(1 - my_c)
    from_sibling = _sibling_exchange(send)
    part = _add2(keep, from_sibling)
    from_chips = _chip_exchange(part)
    own = lax.dynamic_index_in_dim(part, my_q, axis=0, keepdims=False)
    g_pack = _add4(own, from_chips)

    def shard_grad(i, transposed):
        blk = g_pack[offs[i]:offs[i] + row_sizes[i], :]
        return blk.T if transposed else blk

    g_shards = [shard_grad(0, True), shard_grad(1, True), shard_grad(2, False), shard_grad(3, False), shard_grad(4, False),
                shard_grad(5, False), shard_grad(6, True), shard_grad(7, False)]

    grads = {
        "norm1_g": gs_norm1, "w_in": g_shards[0], "conv_w": gs_conv_w, "conv_b": gs_conv_b, "fox_f_bias": gs_f_bias,
        "fox_q_g": gs_fox_q, "fox_k_g": gs_fox_k, "mem_norm_g": gs_mem_norm, "w_mem_kv": g_shards[1], "xa_q_g": gs_xa_q,
        "xa_k_g": gs_xa_k, "w_br_conv": g_shards[2], "w_br_fox": g_shards[3], "w_br_xa": g_shards[4], "w_o": g_shards[5],
        "norm2_g": gs_norm2, "w_ffn_in": g_shards[6], "w_ffn_out": g_shards[7],
    }
    weights = {
        "norm1_g": (norm1_g, m_norm1_g, v_norm1_g), "w_in": (w_in, m_w_in, v_w_in), "conv_w": (conv_w, m_conv_w, v_conv_w),
        "conv_b": (conv_b, m_conv_b, v_conv_b), "fox_f_bias": (fox_f_bias, m_fox_f_bias, v_fox_f_bias),
        "fox_q_g": (fox_q_g, m_fox_q_g, v_fox_q_g), "fox_k_g": (fox_k_g, m_fox_k_g, v_fox_k_g),
        "mem_norm_g": (mem_norm_g, m_mem_norm_g, v_mem_norm_g), "w_mem_kv": (w_mem_kv, m_w_mem_kv, v_w_mem_kv),
        "xa_q_g": (xa_q_g, m_xa_q_g, v_xa_q_g), "xa_k_g": (xa_k_g, m_xa_k_g, v_xa_k_g),
        "w_br_conv": (w_br_conv, m_w_br_conv, v_w_br_conv), "w_br_fox": (w_br_fox, m_w_br_fox, v_w_br_fox),
        "w_br_xa": (w_br_xa, m_w_br_xa, v_w_br_xa), "w_o": (w_o, m_w_o, v_w_o), "norm2_g": (norm2_g, m_norm2_g, v_norm2_g),
        "w_ffn_in": (w_ffn_in, m_w_ffn_in, v_w_ffn_in), "w_ffn_out": (w_ffn_out, m_w_ffn_out, v_w_ffn_out),
    }
    out_g, out_d, out_m, out_v = [], [], [], []
    for name, (w, m, v) in weights.items():
        shape = w.shape
        w2 = w.reshape(shape[-2], shape[-1])
        g2 = grads[name].reshape(w2.shape)
        dl, mn, vn = _adamw(w2, g2, m.reshape(w2.shape), v.reshape(w2.shape), name="adamw_" + name)
        out_g.append(g2.reshape(shape))
        out_d.append(dl.reshape(shape))
        out_m.append(mn.reshape(shape))
        out_v.append(vn.reshape(shape))

    loss = lax.psum(loss_local, ("x", "y", "c"))
    return (loss, dx.reshape(nb, s, d), *out_g, *out_d, *out_m, *out_v)
```

```python
import functools
import math

import jax
import jax.numpy as jnp
from jax import lax
from jax.experimental import pallas as pl
from jax.experimental.pallas import tpu as pltpu

F32, BF16 = jnp.float32, jnp.bfloat16
EPS = 1e-6
HD = 128
XH = 4
N_DEV = 8
MESH = pl.DeviceIdType.MESH
VMEM_LIMIT = 52 * 1024 * 1024
NEG = -1e30

ADAM_LR, ADAM_B1, ADAM_B2, ADAM_EPS, ADAM_WD, ADAM_STEP = 0.001, 0.9, 0.999, 1e-08, 0.01, 10


def _cp(**kw):
    return pltpu.CompilerParams(vmem_limit_bytes=VMEM_LIMIT, **kw)


def _pcall(body, **kw):
    return pl.pallas_call(body, **kw)


def _sds(shape, dtype):
    return jax.ShapeDtypeStruct(shape, dtype)


def _mm(a, b, *, m, n, k, name, ta=False, tb=False, b_off=(0, 0), res=None, out_dtype=F32, tm=512, tn=1024, tk=1024):
    tm, tn, tk = min(tm, m), min(tn, n), min(tk, k)
    assert m % tm == 0 and n % tn == 0 and k % tk == 0, (name, m, n, k, tm, tn, tk)
    nk = k // tk
    bo0, bo1 = b_off
    if ta:
        a_spec = pl.BlockSpec((tk, tm), lambda i, j, kk: (kk, i))
    else:
        a_spec = pl.BlockSpec((tm, tk), lambda i, j, kk: (i, kk))
    if tb:
        b_spec = pl.BlockSpec((tn, tk), lambda i, j, kk: (j + bo0, kk + bo1))
    else:
        b_spec = pl.BlockSpec((tk, tn), lambda i, j, kk: (kk + bo0, j + bo1))
    o_spec = pl.BlockSpec((tm, tn), lambda i, j, kk: (i, j))
    dn = (((0 if ta else 1,), (1 if tb else 0,)), ((), ()))
    has_res = res is not None

    def body(*refs):
        a_ref, b_ref = refs[0], refs[1]
        r_ref = refs[2] if has_res else None
        o_ref = refs[3] if has_res else refs[2]
        p = lax.dot_general(a_ref[...].astype(BF16), b_ref[...].astype(BF16), dn, preferred_element_type=F32)

        def finish(acc):
            if has_res:
                acc = acc + r_ref[...]
            o_ref[...] = acc.astype(out_dtype)

        if nk == 1:
            finish(p)
        else:
            acc_ref = refs[-1]
            kk = pl.program_id(2)

            @pl.when(kk == 0)
            def _():
                acc_ref[...] = p

            @pl.when(kk > 0)
            def _():
                acc_ref[...] += p

            @pl.when(kk == nk - 1)
            def _():
                finish(acc_ref[...])

    ins = [a, b] + ([res] if has_res else [])
    in_specs = [a_spec, b_spec] + ([o_spec] if has_res else [])
    return _pcall(
        body, name=name, out_shape=_sds((m, n), out_dtype), grid=(m // tm, n // tn, nk),
        in_specs=in_specs, out_specs=o_spec,
        scratch_shapes=[pltpu.VMEM((tm, tn), F32)] if nk > 1 else [],
        compiler_params=_cp(dimension_semantics=("parallel", "parallel", "arbitrary")),
    )(*ins)


def _rms_fwd(x, g, *, name, tm=512):
    t, d = x.shape
    tm = min(tm, t)

    def body(x_ref, g_ref, h_ref):
        xv = x_ref[...]
        r = lax.rsqrt(jnp.mean(xv * xv, axis=-1, keepdims=True) + EPS)
        h_ref[...] = (xv * r * g_ref[...]).astype(BF16)

    return _pcall(
        body, name=name, out_shape=_sds((t, d), BF16), grid=(t // tm,),
        in_specs=[pl.BlockSpec((tm, d), lambda i: (i, 0)), pl.BlockSpec((1, d), lambda i: (0, 0))],
        out_specs=pl.BlockSpec((tm, d), lambda i: (i, 0)), compiler_params=_cp(),
    )(x, g)


def _rms_bwd(dh, x, g, res, *, name, tm=512):
    t, d = x.shape
    tm = min(tm, t)
    has_res = res is not None

    def body(*refs):
        dh_ref, x_ref, g_ref = refs[:3]
        r_ref = refs[3] if has_res else None
        dx_ref, dg_ref = refs[-2], refs[-1]
        xv = x_ref[...]
        r = lax.rsqrt(jnp.mean(xv * xv, axis=-1, keepdims=True) + EPS)
        xh = xv * r
        dhv = dh_ref[...]
        gd = dhv * g_ref[...]
        dx = r * (gd - xh * jnp.mean(gd * xh, axis=-1, keepdims=True))
        if has_res:
            dx = dx + r_ref[...]
        dx_ref[...] = dx

        @pl.when(pl.program_id(0) == 0)
        def _():
            dg_ref[...] = jnp.zeros_like(dg_ref)

        dg_ref[...] += jnp.sum(dhv * xh, axis=0, keepdims=True)

    row = pl.BlockSpec((tm, d), lambda i: (i, 0))
    vec = pl.BlockSpec((1, d), lambda i: (0, 0))
    return _pcall(
        body, name=name, out_shape=(_sds((t, d), F32), _sds((1, d), F32)), grid=(t // tm,),
        in_specs=[row, row, vec] + ([row] if has_res else []), out_specs=(row, vec),
        compiler_params=_cp(dimension_semantics=("arbitrary",)),
    )(*([dh, x, g] + ([res] if has_res else [])))


def _conv_fwd(z, conv_w, conv_b, *, nb, s, d, tc=256):
    nc = d // tc

    def body(cb_ref, cc_ref, cv_ref, w_ref, b_ref, y_ref):
        u = cc_ref[...] * cv_ref[...]
        row = lax.broadcasted_iota(jnp.int32, u.shape, 0)
        u1 = jnp.where(row >= 1, pltpu.roll(u, 1, axis=0), 0.0)
        u2 = jnp.where(row >= 2, pltpu.roll(u, 2, axis=0), 0.0)
        w = w_ref[...]
        y = w[2:3, :] * u + w[1:2, :] * u1 + w[0:1, :] * u2 + b_ref[...]
        y_ref[...] = (cb_ref[...] * y).astype(BF16)

    def part(p):
        return pl.BlockSpec((s, tc), lambda b, j: (b, p * nc + j))

    return _pcall(
        body, name="conv_fwd", out_shape=_sds((nb * s, d), BF16), grid=(nb, nc),
        in_specs=[part(0), part(1), part(2), pl.BlockSpec((3, tc), lambda b, j: (0, j)), pl.BlockSpec((1, tc), lambda b, j: (0, j))],
        out_specs=pl.BlockSpec((s, tc), lambda b, j: (b, j)), compiler_params=_cp(),
    )(z, z, z, conv_w, conv_b)


def _conv_bwd(z, dy, conv_w, conv_b, dz, *, nb, s, d, tc=256):
    nc = d // tc

    def body(cb_ref, cc_ref, cv_ref, dy_ref, w_ref, b_ref, dz_in, dz_ref, dw_ref, db_ref, stash):
        b_i, p = pl.program_id(1), pl.program_id(2)

        @pl.when(p == 0)
        def _():
            cc, cv = cc_ref[...], cv_ref[...]
            u = cc * cv
            n = u.shape[0]
            row = lax.broadcasted_iota(jnp.int32, u.shape, 0)
            u1 = jnp.where(row >= 1, pltpu.roll(u, 1, axis=0), 0.0)
            u2 = jnp.where(row >= 2, pltpu.roll(u, 2, axis=0), 0.0)
            w = w_ref[...]
            y = w[2:3, :] * u + w[1:2, :] * u1 + w[0:1, :] * u2 + b_ref[...]
            dyv = dy_ref[...]
            dconv = dyv * cb_ref[...]
            g1 = jnp.where(row < n - 1, pltpu.roll(dconv, n - 1, axis=0), 0.0)
            g2 = jnp.where(row < n - 2, pltpu.roll(dconv, n - 2, axis=0), 0.0)
            du = w[2:3, :] * dconv + w[1:2, :] * g1 + w[0:1, :] * g2
            stash[0] = (dyv * y).astype(BF16)
            stash[1] = (du * cv).astype(BF16)
            stash[2] = (du * cc).astype(BF16)
            dws = [jnp.sum(dconv * u2, axis=0, keepdims=True), jnp.sum(dconv * u1, axis=0, keepdims=True),
                   jnp.sum(dconv * u, axis=0, keepdims=True)]
            db = jnp.sum(dconv, axis=0, keepdims=True)

            @pl.when(b_i == 0)
            def _():
                for r in range(3):
                    dw_ref[r:r + 1, :] = dws[r]
                db_ref[...] = db

            @pl.when(b_i > 0)
            def _():
                for r in range(3):
                    dw_ref[r:r + 1, :] += dws[r]
                db_ref[...] += db

        dz_ref[...] = stash[p]

    return _pcall(
        body, name="conv_bwd",
        out_shape=(_sds(dz.shape, dz.dtype), _sds((3, d), F32), _sds((1, d), F32)), grid=(nc, nb, 3),
        in_specs=[
            pl.BlockSpec((s, tc), lambda j, b, p: (b, j)), pl.BlockSpec((s, tc), lambda j, b, p: (b, nc + j)),
            pl.BlockSpec((s, tc), lambda j, b, p: (b, 2 * nc + j)), pl.BlockSpec((s, tc), lambda j, b, p: (b, j)),
            pl.BlockSpec((3, tc), lambda j, b, p: (0, j)), pl.BlockSpec((1, tc), lambda j, b, p: (0, j)),
            pl.BlockSpec(memory_space=pl.ANY),
        ],
        out_specs=(pl.BlockSpec((s, tc), lambda j, b, p: (b, p * nc + j)),
                   pl.BlockSpec((3, tc), lambda j, b, p: (0, j)), pl.BlockSpec((1, tc), lambda j, b, p: (0, j))),
        scratch_shapes=[pltpu.VMEM((3, s, tc), BF16)],
        input_output_aliases={6: 0},
        compiler_params=_cp(dimension_semantics=("arbitrary", "arbitrary", "arbitrary")),
    )(z, z, z, dy, conv_w, conv_b, dz)


def _head_rms(xv, g):
    r = lax.rsqrt(jnp.mean(xv * xv, axis=-1, keepdims=True) + EPS)
    return xv * r * g


def _head_rms_bwd(xv, g, dy):
    r = lax.rsqrt(jnp.mean(xv * xv, axis=-1, keepdims=True) + EPS)
    xh = xv * r
    gd = dy * g
    dx = r * (gd - xh * jnp.mean(gd * xh, axis=-1, keepdims=True))
    return dx, jnp.sum(dy * xh, axis=0, keepdims=True)


def _fox_prep(z, q_g, k_g, *, t, d, tm=256):
    nh = d // HD

    def body(z_ref, qg_ref, kg_ref, q_ref, k_ref, v_ref):
        qg, kg = qg_ref[...], kg_ref[...]
        for h in range(nh):
            c0 = h * HD
            q_ref[:, c0:c0 + HD] = _head_rms(z_ref[:, c0:c0 + HD], qg).astype(BF16)
            k_ref[:, c0:c0 + HD] = _head_rms(z_ref[:, d + c0:d + c0 + HD], kg).astype(BF16)
        v_ref[...] = z_ref[:, 2 * d:3 * d].astype(BF16)

    o = pl.BlockSpec((tm, d), lambda i: (i, 0))
    g = pl.BlockSpec((1, HD), lambda i: (0, 0))
    return _pcall(
        body, name="fox_prep", out_shape=(_sds((t, d), BF16),) * 3, grid=(t // tm,),
        in_specs=[pl.BlockSpec((tm, 3 * d), lambda i: (i, 1)), g, g], out_specs=(o, o, o), compiler_params=_cp(),
    )(z, q_g, k_g)


def _lane_cumsum(v, reverse=False):
    n = v.shape[1]
    lane = lax.broadcasted_iota(jnp.int32, v.shape, 1)
    sh = 1
    while sh < n:
        if reverse:
            v = v + jnp.where(lane < n - sh, pltpu.roll(v, n - sh, axis=1), 0.0)
        else:
            v = v + jnp.where(lane >= sh, pltpu.roll(v, sh, axis=1), 0.0)
        sh *= 2
    return v


def _fox_gates(z, f_bias_col, *, nb, s, nh, ff_block):
    def body(z_ref, b_ref, c_ref):
        ff = z_ref[...].T[0:nh, :] + b_ref[...]
        lf = jnp.minimum(ff, 0.0) - jnp.log(1.0 + jnp.exp(-jnp.abs(ff)))
        c_ref[0] = _lane_cumsum(lf)

    return _pcall(
        body, name="fox_gates", out_shape=_sds((nb, nh, s), F32), grid=(nb,),
        in_specs=[pl.BlockSpec((s, HD), lambda b: (b, ff_block)), pl.BlockSpec((nh, 1), lambda b: (0, 0))],
        out_specs=pl.BlockSpec((1, nh, s), lambda b: (b, 0, 0)), compiler_params=_cp(),
    )(z, f_bias_col)


def _fox_gates_bwd(z, f_bias_col, dck, dz, *, nb, s, nh, ff_block):
    def body(z_ref, b_ref, dc_ref, dz_in, dz_ref, db_ref):
        ff = z_ref[...].T[0:nh, :] + b_ref[...]
        dlf = _lane_cumsum(dc_ref[0], reverse=True)
        dff = dlf * (1.0 / (1.0 + jnp.exp(ff)))
        pad = jnp.concatenate([dff, jnp.zeros((HD - nh, s), F32)], axis=0)
        dz_ref[...] = pad.T.astype(BF16)
        dbv = jnp.sum(dff, axis=1, keepdims=True)

        @pl.when(pl.program_id(0) == 0)
        def _():
            db_ref[...] = dbv

        @pl.when(pl.program_id(0) > 0)
        def _():
            db_ref[...] += dbv

    return _pcall(
        body, name="fox_gates_bwd", out_shape=(_sds(dz.shape, dz.dtype), _sds((nh, 1), F32)), grid=(nb,),
        in_specs=[pl.BlockSpec((s, HD), lambda b: (b, ff_block)), pl.BlockSpec((nh, 1), lambda b: (0, 0)),
                  pl.BlockSpec((1, nh, s), lambda b: (b, 0, 0)), pl.BlockSpec(memory_space=pl.ANY)],
        out_specs=(pl.BlockSpec((s, HD), lambda b: (b, ff_block)), pl.BlockSpec((nh, 1), lambda b: (0, 0))),
        input_output_aliases={3: 0}, compiler_params=_cp(dimension_semantics=("arbitrary",)),
    )(z, f_bias_col, dck, dz)


def _fox_logits(q_ref, k_ref, ck_ref, i, j, tq, tk):
    sc = lax.dot_general(q_ref[...], k_ref[...], (((1,), (1,)), ((), ())), preferred_element_type=F32)
    sc = sc * (1.0 / math.sqrt(HD)) - ck_ref[0]
    rows = i * tq + lax.broadcasted_iota(jnp.int32, (tq, tk), 0)
    cols = j * tk + lax.broadcasted_iota(jnp.int32, (tq, tk), 1)
    return jnp.where(rows >= cols, sc, NEG)


def _fox_fwd(q, k, v, ck, *, nb, s, nh, tq=512):
    tk = tq
    nq = s // tq
    t = nb * s

    def body(q_ref, k_ref, v_ref, ck_ref, o_ref, o32_ref, lse_ref, m_sc, l_sc, acc_sc):
        i, j = pl.program_id(2), pl.program_id(3)

        @pl.when(j == 0)
        def _():
            m_sc[...] = jnp.full_like(m_sc, NEG)
            l_sc[...] = jnp.zeros_like(l_sc)
            acc_sc[...] = jnp.zeros_like(acc_sc)

        @pl.when(j <= i)
        def _():
            sc = _fox_logits(q_ref, k_ref, ck_ref, i, j, tq, tk)
            m_old = m_sc[...]
            m_new = jnp.maximum(m_old, jnp.max(sc, axis=-1, keepdims=True))
            alpha = jnp.exp(m_old - m_new)
            p = jnp.exp(sc - m_new)
            l_sc[...] = alpha * l_sc[...] + jnp.sum(p, axis=-1, keepdims=True)
            p_hi = p.astype(BF16)
            p_lo = (p - p_hi.astype(F32)).astype(BF16)
            vv = v_ref[...]
            pv = jnp.dot(p_hi, vv, preferred_element_type=F32) + jnp.dot(p_lo, vv, preferred_element_type=F32)
            acc_sc[...] = alpha * acc_sc[...] + pv
            m_sc[...] = m_new

        @pl.when(j == nq - 1)
        def _():
            l = l_sc[...]
            o = acc_sc[...] / l
            o_ref[...] = o.astype(BF16)
            o32_ref[...] = o
            lse_ref[...] = jnp.broadcast_to(m_sc[...] + jnp.log(l), (tq, HD))

    qs = pl.BlockSpec((tq, HD), lambda b, h, i, j: (b * nq + i, h))
    ks = pl.BlockSpec((tk, HD), lambda b, h, i, j: (b * nq + jnp.minimum(i, j), h))
    cs = pl.BlockSpec((1, 1, tk), lambda b, h, i, j: (b * nh + h, 0, jnp.minimum(i, j)))
    return _pcall(
        body, name="fox_fwd", out_shape=(_sds((t, nh * HD), BF16), _sds((t, nh * HD), F32), _sds((t, nh * HD), F32)),
        grid=(nb, nh, nq, nq), in_specs=[qs, ks, ks, cs], out_specs=(qs, qs, qs),
        scratch_shapes=[pltpu.VMEM((tq, 1), F32), pltpu.VMEM((tq, 1), F32), pltpu.VMEM((tq, HD), F32)],
        compiler_params=_cp(dimension_semantics=("parallel", "parallel", "parallel", "arbitrary")),
    )(q, k, v, ck)


def _fox_bwd_dq(q, k, v, ck, do, o, lse, *, nb, s, nh, tq=512):
    tk = tq
    nq = s // tq
    t = nb * s

    def body(q_ref, k_ref, v_ref, ck_ref, do_ref, o_ref, lse_ref, dq_ref, acc_sc, dl_sc):
        i, j = pl.program_id(2), pl.program_id(3)

        @pl.when(j == 0)
        def _():
            acc_sc[...] = jnp.zeros_like(acc_sc)
            dl_sc[...] = jnp.sum(do_ref[...].astype(BF16).astype(F32) * o_ref[...], axis=-1, keepdims=True)

        @pl.when(j <= i)
        def _():
            sc = _fox_logits(q_ref, k_ref, ck_ref, i, j, tq, tk)
            p = jnp.exp(sc - lse_ref[:, 0:1])
            dp = lax.dot_general(do_ref[...].astype(BF16), v_ref[...], (((1,), (1,)), ((), ())), preferred_element_type=F32)
            ds = p * (dp - dl_sc[...])
            acc_sc[...] += jnp.dot(ds.astype(BF16), k_ref[...], preferred_element_type=F32)

        @pl.when(j == nq - 1)
        def _():
            dq_ref[...] = acc_sc[...] * (1.0 / math.sqrt(HD))

    qs = pl.BlockSpec((tq, HD), lambda b, h, i, j: (b * nq + i, h))
    ks = pl.BlockSpec((tk, HD), lambda b, h, i, j: (b * nq + jnp.minimum(i, j), h))
    cs = pl.BlockSpec((1, 1, tk), lambda b, h, i, j: (b * nh + h, 0, jnp.minimum(i, j)))
    return _pcall(
        body, name="fox_bwd_dq", out_shape=_sds((t, nh * HD), F32), grid=(nb, nh, nq, nq),
        in_specs=[qs, ks, ks, cs, qs, qs, qs], out_specs=qs,
        scratch_shapes=[pltpu.VMEM((tq, HD), F32), pltpu.VMEM((tq, 1), F32)],
        compiler_params=_cp(dimension_semantics=("parallel", "parallel", "parallel", "arbitrary")),
    )(q, k, v, ck, do, o, lse)


def _fox_bwd_dkv(q, k, v, ck, do, o, lse, *, nb, s, nh, tq=512):
    tk = tq
    nq = s // tq
    t = nb * s

    def body(q_ref, k_ref, v_ref, ck_ref, do_ref, o_ref, lse_ref, dk_ref, dv_ref, dc_ref, dk_sc, dv_sc, dc_sc):
        j, i = pl.program_id(2), pl.program_id(3)

        @pl.when(i == 0)
        def _():
            dk_sc[...] = jnp.zeros_like(dk_sc)
            dv_sc[...] = jnp.zeros_like(dv_sc)
            dc_sc[...] = jnp.zeros_like(dc_sc)

        @pl.when(i >= j)
        def _():
            sc = _fox_logits(q_ref, k_ref, ck_ref, i, j, tq, tk)
            p = jnp.exp(sc - lse_ref[:, 0:1])
            dov = do_ref[...]
            dob = dov.astype(BF16)
            dv_sc[...] += lax.dot_general(p.astype(BF16), dob, (((0,), (0,)), ((), ())), preferred_element_type=F32)
            dp = lax.dot_general(dob, v_ref[...], (((1,), (1,)), ((), ())), preferred_element_type=F32)
            delta = jnp.sum(dob.astype(F32) * o_ref[...], axis=-1, keepdims=True)
            ds = p * (dp - delta)
            dk_sc[...] += lax.dot_general(ds.astype(BF16), q_ref[...], (((0,), (0,)), ((), ())), preferred_element_type=F32)
            dc_sc[...] += jnp.sum(ds, axis=0, keepdims=True)

        @pl.when(i == nq - 1)
        def _():
            dk_ref[...] = dk_sc[...] * (1.0 / math.sqrt(HD))
            dv_ref[...] = dv_sc[...].astype(BF16)
            dc_ref[0] = -dc_sc[...]

    qs = pl.BlockSpec((tq, HD), lambda b, h, j, i: (b * nq + jnp.maximum(i, j), h))
    ks = pl.BlockSpec((tk, HD), lambda b, h, j, i: (b * nq + j, h))
    cs = pl.BlockSpec((1, 1, tk), lambda b, h, j, i: (b * nh + h, 0, j))
    return _pcall(
        body, name="fox_bwd_dkv",
        out_shape=(_sds((t, nh * HD), F32), _sds((t, nh * HD), BF16), _sds((nb * nh, 1, s), F32)), grid=(nb, nh, nq, nq),
        in_specs=[qs, ks, ks, cs, qs, qs, qs], out_specs=(ks, ks, cs),
        scratch_shapes=[pltpu.VMEM((tk, HD), F32), pltpu.VMEM((tk, HD), F32), pltpu.VMEM((1, tk), F32)],
        compiler_params=_cp(dimension_semantics=("parallel", "parallel", "parallel", "arbitrary")),
    )(q, k, v, ck, do, o, lse)


SQRT_HD = math.sqrt(HD)
EXP2_C = math.log2(math.e) / SQRT_HD
LOG2E = math.log2(math.e)


def _fox_pairs(nq, key_major):
    if key_major:
        pairs = [(i, j) for j in range(nq) for i in range(j, nq)]
    else:
        pairs = [(i, j) for i in range(nq) for j in range(i + 1)]
    return jnp.array([p[0] for p in pairs], jnp.int32), jnp.array([p[1] for p in pairs], jnp.int32)


def _with_ones(x):
    return jnp.concatenate([x, jnp.ones_like(x)], axis=1)


FOX_HP = 4


def _fox_specs(nb, nh, nq, tq, tk, hp):
    qs = pl.BlockSpec((tq, hp * HD), lambda b, h, p, it, jt: (b * nq + it[p], h))
    ks = pl.BlockSpec((tk, hp * HD), lambda b, h, p, it, jt: (b * nq + jt[p], h))
    cs = pl.BlockSpec((hp, 1, tk), lambda b, h, p, it, jt: (b * (nh // hp) + h, 0, jt[p]))
    return qs, ks, cs


def _fox_raw(qv, kv, ckv, diag, tq, tk):
    sc = lax.dot_general(qv, kv, (((1,), (1,)), ((), ())), preferred_element_type=F32) - ckv
    if diag:
        rows = lax.broadcasted_iota(jnp.int32, (tq, tk), 0)
        cols = lax.broadcasted_iota(jnp.int32, (tq, tk), 1)
        sc = jnp.where(rows >= cols, sc, NEG)
    return sc


def _fox_fwd2(q, k, v, cks, *, nb, s, nh, tq=512):
    tk = tq
    nq = s // tq
    t = nb * s
    it, jt = _fox_pairs(nq, key_major=False)

    hp = FOX_HP

    def body(it_ref, jt_ref, q_ref, k_ref, v_ref, c_ref, o_ref, lse_ref, m_sc, acc_sc):
        p_id = pl.program_id(2)
        i, j = it_ref[p_id], jt_ref[p_id]

        @pl.when(j == 0)
        def _():
            m_sc[...] = jnp.full_like(m_sc, NEG)
            acc_sc[...] = jnp.zeros_like(acc_sc)

        def step(diag):
            for hh in range(hp):
                cols = slice(hh * HD, (hh + 1) * HD)
                sc = _fox_raw(q_ref[:, cols], k_ref[:, cols], c_ref[hh], diag, tq, tk)
                m_old = m_sc[hh]
                m_new = jnp.maximum(m_old, jnp.max(sc, axis=-1, keepdims=True))
                alpha = jnp.exp2((m_old - m_new) * EXP2_C)
                p = jnp.exp2((sc - m_new) * EXP2_C).astype(BF16)
                acc = alpha * acc_sc[hh] + jnp.dot(p, _with_ones(v_ref[:, cols]), preferred_element_type=F32)
                if diag:
                    l = acc[:, HD:2 * HD]
                    o_ref[:, cols] = (acc[:, 0:HD] / l).astype(BF16)
                    lse_ref[:, cols] = m_new * (1.0 / SQRT_HD) + jnp.log(l)
                else:
                    acc_sc[hh] = acc
                    m_sc[hh] = m_new

        @pl.when(j < i)
        def _():
            step(False)

        @pl.when(j == i)
        def _():
            step(True)

    qs, ks, cs = _fox_specs(nb, nh, nq, tq, tk, hp)
    return _pcall(
        body, name="fox_fwd", out_shape=(_sds((t, nh * HD), BF16), _sds((t, nh * HD), F32)),
        grid_spec=pltpu.PrefetchScalarGridSpec(
            num_scalar_prefetch=2, grid=(nb, nh // hp, int(it.shape[0])), in_specs=[qs, ks, ks, cs], out_specs=(qs, qs),
            scratch_shapes=[pltpu.VMEM((hp, tq, 1), F32), pltpu.VMEM((hp, tq, 2 * HD), F32)]),
        compiler_params=_cp(dimension_semantics=("parallel", "parallel", "arbitrary")),
    )(it, jt, q, k, v, cks)


def _fox_probs(qv, kv, ckv, lse_col, diag, tq, tk):
    sc = _fox_raw(qv, kv, ckv, diag, tq, tk)
    return jnp.exp2(sc * EXP2_C - lse_col * LOG2E)


def _fox_bwd_dq2(q, k, v, cks, do, o, lse, *, nb, s, nh, tq=512):
    tk = tq
    nq = s // tq
    t = nb * s
    it, jt = _fox_pairs(nq, key_major=False)

    hp = FOX_HP

    def body(it_ref, jt_ref, q_ref, k_ref, v_ref, c_ref, do_ref, o_ref, lse_ref, dq_ref, rs_ref, acc_sc, dl_sc):
        p_id = pl.program_id(2)
        i, j = it_ref[p_id], jt_ref[p_id]

        @pl.when(j == 0)
        def _():
            acc_sc[...] = jnp.zeros_like(acc_sc)
            for hh in range(hp):
                cols = slice(hh * HD, (hh + 1) * HD)
                dl_sc[hh] = jnp.sum(do_ref[:, cols] * o_ref[:, cols].astype(F32), axis=-1, keepdims=True)

        def step(diag):
            for hh in range(hp):
                cols = slice(hh * HD, (hh + 1) * HD)
                kv = k_ref[:, cols]
                p = _fox_probs(q_ref[:, cols], kv, c_ref[hh], lse_ref[:, hh * HD:hh * HD + 1], diag, tq, tk)
                dp = lax.dot_general(do_ref[:, cols].astype(BF16), v_ref[:, cols], (((1,), (1,)), ((), ())),
                                     preferred_element_type=F32)
                ds = (p * (dp - dl_sc[hh])).astype(BF16)
                acc = acc_sc[hh] + jnp.dot(ds, _with_ones(kv), preferred_element_type=F32)
                if diag:
                    dq_ref[:, cols] = acc[:, 0:HD] * (1.0 / SQRT_HD)
                    rs_ref[:, cols] = acc[:, HD:2 * HD]
                else:
                    acc_sc[hh] = acc

        @pl.when(j < i)
        def _():
            step(False)

        @pl.when(j == i)
        def _():
            step(True)

    qs, ks, cs = _fox_specs(nb, nh, nq, tq, tk, hp)
    return _pcall(
        body, name="fox_bwd_dq", out_shape=(_sds((t, nh * HD), F32), _sds((t, nh * HD), F32)),
        grid_spec=pltpu.PrefetchScalarGridSpec(
            num_scalar_prefetch=2, grid=(nb, nh // hp, int(it.shape[0])), in_specs=[qs, ks, ks, cs, qs, qs, qs],
            out_specs=(qs, qs),
            scratch_shapes=[pltpu.VMEM((hp, tq, 2 * HD), F32), pltpu.VMEM((hp, tq, 1), F32)]),
        compiler_params=_cp(dimension_semantics=("parallel", "parallel", "arbitrary")),
    )(it, jt, q, k, v, cks, do, o, lse)


def _fox_bwd_dkv2(q, k, v, cks, do, o, lse, *, nb, s, nh, tq=512):
    tk = tq
    nq = s // tq
    t = nb * s
    it, jt = _fox_pairs(nq, key_major=True)

    hp = FOX_HP

    def body(it_ref, jt_ref, q_ref, k_ref, v_ref, c_ref, do_ref, o_ref, lse_ref, dk_ref, dv_ref, cs_ref, dk_sc, dv_sc):
        p_id = pl.program_id(2)
        i, j = it_ref[p_id], jt_ref[p_id]

        def step(diag):
            for hh in range(hp):
                cols = slice(hh * HD, (hh + 1) * HD)
                qv = q_ref[:, cols]
                p = _fox_probs(qv, k_ref[:, cols], c_ref[hh], lse_ref[:, hh * HD:hh * HD + 1], diag, tq, tk)
                dov = do_ref[:, cols]
                dob = dov.astype(BF16)
                dvp = lax.dot_general(p.astype(BF16), dob, (((0,), (0,)), ((), ())), preferred_element_type=F32)
                dp = lax.dot_general(dob, v_ref[:, cols], (((1,), (1,)), ((), ())), preferred_element_type=F32)
                delta = jnp.sum(dov * o_ref[:, cols].astype(F32), axis=-1, keepdims=True)
                ds = (p * (dp - delta)).astype(BF16)
                dkp = lax.dot_general(ds, _with_ones(qv), (((0,), (0,)), ((), ())), preferred_element_type=F32)
                if diag:
                    dk_sc[hh] = dkp
                    dv_sc[hh] = dvp
                else:
                    dk_sc[hh] += dkp
                    dv_sc[hh] += dvp

        @pl.when(i == j)
        def _():
            step(True)

        @pl.when(i > j)
        def _():
            step(False)

        @pl.when(i == nq - 1)
        def _():
            for hh in range(hp):
                cols = slice(hh * HD, (hh + 1) * HD)
                dk_ref[:, cols] = dk_sc[hh, :, 0:HD] * (1.0 / SQRT_HD)
                cs_ref[:, cols] = dk_sc[hh, :, HD:2 * HD]
                dv_ref[:, cols] = dv_sc[hh].astype(BF16)

    qs, ks, cs = _fox_specs(nb, nh, nq, tq, tk, hp)
    return _pcall(
        body, name="fox_bwd_dkv",
        out_shape=(_sds((t, nh * HD), F32), _sds((t, nh * HD), BF16), _sds((t, nh * HD), F32)),
        grid_spec=pltpu.PrefetchScalarGridSpec(
            num_scalar_prefetch=2, grid=(nb, nh // hp, int(it.shape[0])), in_specs=[qs, ks, ks, cs, qs, qs, qs],
            out_specs=(ks, ks, ks),
            scratch_shapes=[pltpu.VMEM((hp, tk, 2 * HD), F32), pltpu.VMEM((hp, tk, HD), F32)]),
        compiler_params=_cp(dimension_semantics=("parallel", "parallel", "arbitrary")),
    )(it, jt, q, k, v, cks, do, o, lse)


def _fox_gates2(z, f_bias_col, *, nb, s, nh, ff_block):
    def body(z_ref, b_ref, c_ref):
        ff = z_ref[...].T[0:nh, :] + b_ref[...]
        lf = jnp.minimum(ff, 0.0) - jnp.log(1.0 + jnp.exp(-jnp.abs(ff)))
        c_ref[0] = _lane_cumsum(lf) * SQRT_HD

    return _pcall(
        body, name="fox_gates", out_shape=_sds((nb, nh, s), F32), grid=(nb,),
        in_specs=[pl.BlockSpec((s, HD), lambda b: (b, ff_block)), pl.BlockSpec((nh, 1), lambda b: (0, 0))],
        out_specs=pl.BlockSpec((1, nh, s), lambda b: (b, 0, 0)), compiler_params=_cp(),
    )(z, f_bias_col)


def _fox_gates_bwd2(z, f_bias_col, row_sums, col_sums, dz, *, nb, s, nh, ff_block):
    def body(z_ref, b_ref, rs_ref, cs_ref, dz_in, dz_ref, db_ref, dc_sc):
        b_i, h = pl.program_id(0), pl.program_id(1)
        dc_sc[pl.ds(h, 1), :] = (rs_ref[...] - cs_ref[...]).T[0:1, :]

        @pl.when(h == nh - 1)
        def _():
            ff = z_ref[...].T[0:nh, :] + b_ref[...]
            dlf = _lane_cumsum(dc_sc[...], reverse=True)
            dff = dlf * (1.0 / (1.0 + jnp.exp(ff)))
            pad = jnp.concatenate([dff, jnp.zeros((HD - nh, s), F32)], axis=0)
            dz_ref[...] = pad.T.astype(BF16)
            dbv = jnp.sum(dff, axis=1, keepdims=True)

            @pl.when(b_i == 0)
            def _():
                db_ref[...] = dbv

            @pl.when(b_i > 0)
            def _():
                db_ref[...] += dbv

    hs = pl.BlockSpec((s, HD), lambda b, h: (b, h))
    zs = pl.BlockSpec((s, HD), lambda b, h: (b, ff_block))
    return _pcall(
        body, name="fox_gates_bwd", out_shape=(_sds(dz.shape, dz.dtype), _sds((nh, 1), F32)), grid=(nb, nh),
        in_specs=[zs, pl.BlockSpec((nh, 1), lambda b, h: (0, 0)), hs, hs, pl.BlockSpec(memory_space=pl.ANY)],
        out_specs=(zs, pl.BlockSpec((nh, 1), lambda b, h: (0, 0))),
        scratch_shapes=[pltpu.VMEM((nh, s), F32)],
        input_output_aliases={4: 0}, compiler_params=_cp(dimension_semantics=("arbitrary", "arbitrary")),
    )(z, f_bias_col, row_sums, col_sums, dz)


def _fox_post(z, dqn, dkn, dv, q_g, k_g, dz, *, t, d, tm=256):
    nh = d // HD

    def body(z_ref, dq_ref, dk_ref, dv_ref, qg_ref, kg_ref, dz_in, dz_ref, dqg_ref, dkg_ref):
        qg, kg = qg_ref[...], kg_ref[...]
        aq = jnp.zeros((1, HD), F32)
        ak = jnp.zeros((1, HD), F32)
        for h in range(nh):
            c0 = h * HD
            dx, sg = _head_rms_bwd(z_ref[:, c0:c0 + HD], qg, dq_ref[:, c0:c0 + HD])
            dz_ref[:, c0:c0 + HD] = dx.astype(BF16)
            aq = aq + sg
            dx, sg = _head_rms_bwd(z_ref[:, d + c0:d + c0 + HD], kg, dk_ref[:, c0:c0 + HD])
            dz_ref[:, d + c0:d + c0 + HD] = dx.astype(BF16)
            ak = ak + sg
        dz_ref[:, 2 * d:3 * d] = dv_ref[...]

        @pl.when(pl.program_id(0) == 0)
        def _():
            dqg_ref[...] = aq
            dkg_ref[...] = ak

        @pl.when(pl.program_id(0) > 0)
        def _():
            dqg_ref[...] += aq
            dkg_ref[...] += ak

    o = pl.BlockSpec((tm, d), lambda i: (i, 0))
    g = pl.BlockSpec((1, HD), lambda i: (0, 0))
    z3 = pl.BlockSpec((tm, 3 * d), lambda i: (i, 1))
    return _pcall(
        body, name="fox_post", out_shape=(_sds(dz.shape, dz.dtype), _sds((1, HD), F32), _sds((1, HD), F32)), grid=(t // tm,),
        in_specs=[z3, o, o, o, g, g, pl.BlockSpec(memory_space=pl.ANY)], out_specs=(z3, g, g),
        input_output_aliases={6: 0}, compiler_params=_cp(dimension_semantics=("arbitrary",)),
    )(z, dqn, dkn, dv, q_g, k_g, dz)


def _xa_prep_kv(kv, k_g, *, rows, d):
    xd = d // XH

    def body(kv_ref, g_ref, k_ref, v_ref):
        g = g_ref[...]
        for h in range(XH):
            c0 = h * xd
            k_ref[:, c0:c0 + xd] = _head_rms(kv_ref[:, c0:c0 + xd], g).astype(BF16)
        v_ref[...] = kv_ref[:, d:2 * d].astype(BF16)

    return _pcall(body, name="xa_prep_kv", out_shape=(_sds((rows, d), BF16),) * 2, compiler_params=_cp())(kv, k_g)


def _xa_fwd(z, kn, vb, q_g, *, nb, s, d, nm, q_block, tq=512):
    xd = d // XH
    nq = s // tq

    def body(z_ref, k_ref, v_ref, g_ref, y_ref):
        g = g_ref[...]
        for h in range(XH):
            c0 = h * xd
            qn = _head_rms(z_ref[:, c0:c0 + xd], g).astype(BF16)
            sc = lax.dot_general(qn, k_ref[:, c0:c0 + xd], (((1,), (1,)), ((), ())), preferred_element_type=F32)
            sc = sc / math.sqrt(xd)
            p = jnp.exp(sc - jnp.max(sc, axis=-1, keepdims=True))
            p = p / jnp.sum(p, axis=-1, keepdims=True)
            y_ref[:, c0:c0 + xd] = jnp.dot(p.astype(BF16), v_ref[:, c0:c0 + xd], preferred_element_type=F32).astype(BF16)

    kvs = pl.BlockSpec((nm, d), lambda b, i: (b, 0))
    return _pcall(
        body, name="xa_fwd", out_shape=_sds((nb * s, d), BF16), grid=(nb, nq),
        in_specs=[pl.BlockSpec((tq, d), lambda b, i: (b * nq + i, q_block)), kvs, kvs, pl.BlockSpec((1, xd), lambda b, i: (0, 0))],
        out_specs=pl.BlockSpec((tq, d), lambda b, i: (b * nq + i, 0)), compiler_params=_cp(),
    )(z, kn, vb, q_g)


def _xa_bwd(z, kn, vb, q_g, dy, dz, *, nb, s, d, nm, q_block, tq=512):
    xd = d // XH
    nq = s // tq

    def body(z_ref, k_ref, v_ref, g_ref, dy_ref, dz_in, dz_ref, dk_ref, dv_ref, dg_ref):
        b_i, i = pl.program_id(0), pl.program_id(1)
        g = g_ref[...]

        @pl.when(i == 0)
        def _():
            dk_ref[...] = jnp.zeros_like(dk_ref)
            dv_ref[...] = jnp.zeros_like(dv_ref)

        @pl.when((i == 0) & (b_i == 0))
        def _():
            dg_ref[...] = jnp.zeros_like(dg_ref)

        for h in range(XH):
            c0 = h * xd
            xq = z_ref[:, c0:c0 + xd]
            qn = _head_rms(xq, g).astype(BF16)
            kh, vh = k_ref[:, c0:c0 + xd], v_ref[:, c0:c0 + xd]
            sc = lax.dot_general(qn, kh, (((1,), (1,)), ((), ())), preferred_element_type=F32) / math.sqrt(xd)
            p = jnp.exp(sc - jnp.max(sc, axis=-1, keepdims=True))
            p = p / jnp.sum(p, axis=-1, keepdims=True)
            dob = dy_ref[:, c0:c0 + xd].astype(BF16)
            dv_ref[:, c0:c0 + xd] += lax.dot_general(p.astype(BF16), dob, (((0,), (0,)), ((), ())), preferred_element_type=F32)
            dp = lax.dot_general(dob, vh, (((1,), (1,)), ((), ())), preferred_element_type=F32)
            ds = p * (dp - jnp.sum(p * dp, axis=-1, keepdims=True)) / math.sqrt(xd)
            dsb = ds.astype(BF16)
            dqn = jnp.dot(dsb, kh, preferred_element_type=F32)
            dk_ref[:, c0:c0 + xd] += lax.dot_general(dsb, qn, (((0,), (0,)), ((), ())), preferred_element_type=F32)
            dx, sg = _head_rms_bwd(xq, g, dqn)
            dz_ref[:, c0:c0 + xd] = dx.astype(BF16)
            dg_ref[...] += sg

    kvs = pl.BlockSpec((nm, d), lambda b, i: (b, 0))
    qs = pl.BlockSpec((tq, d), lambda b, i: (b * nq + i, q_block))
    gs = pl.BlockSpec((1, xd), lambda b, i: (0, 0))
    return _pcall(
        body, name="xa_bwd",
        out_shape=(_sds(dz.shape, dz.dtype), _sds((nb * nm, d), F32), _sds((nb * nm, d), F32), _sds((1, xd), F32)), grid=(nb, nq),
        in_specs=[qs, kvs, kvs, gs, pl.BlockSpec((tq, d), lambda b, i: (b * nq + i, 0)), pl.BlockSpec(memory_space=pl.ANY)],
        out_specs=(qs, kvs, kvs, gs), input_output_aliases={5: 0},
        compiler_params=_cp(dimension_semantics=("arbitrary", "arbitrary")),
    )(z, kn, vb, q_g, dy, dz)


def _xa_post_kv(kv, dkn, dv, k_g, *, rows, d):
    xd = d // XH

    def body(kv_ref, dk_ref, dv_ref, g_ref, dkv_ref, dg_ref):
        g = g_ref[...]
        acc = jnp.zeros((1, xd), F32)
        for h in range(XH):
            c0 = h * xd
            dx, sg = _head_rms_bwd(kv_ref[:, c0:c0 + xd], g, dk_ref[:, c0:c0 + xd])
            dkv_ref[:, c0:c0 + xd] = dx.astype(BF16)
            acc = acc + sg
        dkv_ref[:, d:2 * d] = dv_ref[...].astype(BF16)
        dg_ref[...] = acc

    return _pcall(body, name="xa_post_kv", out_shape=(_sds((rows, 2 * d), BF16), _sds((1, xd), F32)), compiler_params=_cp())(
        kv, dkn, dv, k_g)


def _sigmoid(v):
    return 1.0 / (1.0 + jnp.exp(-v))


def _merge_fwd(z, pa, pb, pc, *, t, d, gate_block, tm=512):
    def body(ga_ref, gb_ref, gc_ref, pa_ref, pb_ref, pc_ref, o_ref):
        o_ref[...] = (_sigmoid(ga_ref[...]) * pa_ref[...] + _sigmoid(gb_ref[...]) * pb_ref[...]
                      + _sigmoid(gc_ref[...]) * pc_ref[...]).astype(BF16)

    def gate(q):
        return pl.BlockSpec((tm, d), lambda i: (i, gate_block + q))

    o = pl.BlockSpec((tm, d), lambda i: (i, 0))
    return _pcall(
        body, name="merge_fwd", out_shape=_sds((t, d), BF16), grid=(t // tm,),
        in_specs=[gate(0), gate(1), gate(2), o, o, o], out_specs=o, compiler_params=_cp(),
    )(z, z, z, pa, pb, pc)


def _gate_bwd(z, dm, pr, dz, *, t, d, col_block, name, tm=512):
    def body(g_ref, dm_ref, p_ref, dz_in, dz_ref, dp_ref):
        sg = _sigmoid(g_ref[...])
        dmv = dm_ref[...]
        dp_ref[...] = (dmv * sg).astype(BF16)
        dz_ref[...] = (dmv * p_ref[...] * sg * (1.0 - sg)).astype(BF16)

    gs = pl.BlockSpec((tm, d), lambda i: (i, col_block))
    o = pl.BlockSpec((tm, d), lambda i: (i, 0))
    return _pcall(
        body, name=name, out_shape=(_sds(dz.shape, dz.dtype), _sds((t, d), BF16)), grid=(t // tm,),
        in_specs=[gs, o, o, pl.BlockSpec(memory_space=pl.ANY)], out_specs=(gs, o), input_output_aliases={3: 0},
        compiler_params=_cp(),
    )(z, dm, pr, dz)


def _swiglu_fwd(gu, *, t, dff, tm=256):
    def body(g_ref, u_ref, a_ref):
        gv = g_ref[...]
        a_ref[...] = (gv * _sigmoid(gv) * u_ref[...]).astype(BF16)

    return _pcall(
        body, name="swiglu_fwd", out_shape=_sds((t, dff), BF16), grid=(t // tm,),
        in_specs=[pl.BlockSpec((tm, dff), lambda i: (i, 0)), pl.BlockSpec((tm, dff), lambda i: (i, 1))],
        out_specs=pl.BlockSpec((tm, dff), lambda i: (i, 0)), compiler_params=_cp(),
    )(gu, gu)


def _swiglu_bwd(gu, da, *, t, dff, tm=256):
    def body(g_ref, u_ref, da_ref, dg_ref, du_ref):
        gv, dav = g_ref[...], da_ref[...]
        sg = _sigmoid(gv)
        dg_ref[...] = (dav * u_ref[...] * sg * (1.0 + gv * (1.0 - sg))).astype(BF16)
        du_ref[...] = (dav * gv * sg).astype(BF16)

    h0 = pl.BlockSpec((tm, dff), lambda i: (i, 0))
    h1 = pl.BlockSpec((tm, dff), lambda i: (i, 1))
    return _pcall(
        body, name="swiglu_bwd", out_shape=(_sds((t, dff), BF16), _sds((t, dff), BF16)), grid=(t // tm,),
        in_specs=[h0, h1, h0], out_specs=(h0, h0), compiler_params=_cp(),
    )(gu, gu, da)


def _loss_head(y, target, *, t, d, tm=512):
    def body(y_ref, t_ref, dy_ref, l_ref):
        e = y_ref[...] - t_ref[...]
        dy_ref[...] = e * (1.0 / d)

        @pl.when(pl.program_id(0) == 0)
        def _():
            l_ref[...] = jnp.zeros_like(l_ref)

        l_ref[...] += jnp.sum(jnp.sum(e * e, axis=-1, keepdims=True), axis=0, keepdims=True) * (0.5 / d)

    o = pl.BlockSpec((tm, d), lambda i: (i, 0))
    dy, l = _pcall(
        body, name="loss_head", out_shape=(_sds((t, d), F32), _sds((1, HD), F32)), grid=(t // tm,),
        in_specs=[o, o], out_specs=(o, pl.BlockSpec((1, HD), lambda i: (0, 0))),
        compiler_params=_cp(dimension_semantics=("arbitrary",)),
    )(y, target)
    return dy, l[0, 0]


def _mesh_pos():
    return lax.axis_index("x"), lax.axis_index("y"), lax.axis_index("c")


def _all_gather(blocks):
    n = len(blocks)

    def body(*refs):
        ins, outs = refs[:n], refs[n:2 * n]
        send_sems, recv_sems, local_sems = refs[2 * n:]
        x, y, c = _mesh_pos()
        me, sibling = (x, y, c), (x, y, 1 - c)
        chips = [(1 - x, y), (x, 1 - y), (1 - x, 1 - y)]

        def slab(t, dev):
            return outs[t].at[4 * dev[0] + 2 * dev[1] + dev[2]]

        def copy(t, k, block, to, src=None):
            dst = slab(t, block)
            return pltpu.make_async_remote_copy(
                src_ref=dst if src is None else src, dst_ref=dst, send_sem=send_sems.at[t, k], recv_sem=recv_sems.at[t, k],
                device_id=to, device_id_type=MESH)

        mine = [pltpu.make_async_copy(ins[t], slab(t, me), local_sems.at[t]) for t in range(n)]
        for cp in mine:
            cp.start()
        first = []
        for t in range(n):
            first.append(copy(t, 0, me, sibling, src=ins[t]))
            first += [copy(t, 1 + j, me, (*chip, c), src=ins[t]) for j, chip in enumerate(chips)]
        for cp in first:
            cp.start()
        passed = []
        for j, chip in enumerate(chips):
            for t in range(n):
                copy(t, 1 + j, (*chip, c), me).wait_recv()
                fwd = copy(t, 4 + j, (*chip, c), sibling)
                fwd.start()
                passed.append(fwd)
        for t in range(n):
            copy(t, 0, sibling, me).wait_recv()
            for j, chip in enumerate(chips):
                copy(t, 4 + j, (*chip, 1 - c), me).wait_recv()
        for cp in first + passed:
            cp.wait_send()
        for cp in mine:
            cp.wait()

    hbm = pl.BlockSpec(memory_space=pl.ANY)
    return _pcall(
        body, name="all_gather", out_shape=tuple(_sds((N_DEV,) + b.shape, b.dtype) for b in blocks),
        in_specs=[hbm] * n, out_specs=tuple([hbm] * n),
        scratch_shapes=[pltpu.SemaphoreType.DMA((n, 7)), pltpu.SemaphoreType.DMA((n, 7)), pltpu.SemaphoreType.DMA((n,))],
        compiler_params=_cp(),
    )(*blocks)


def _sibling_exchange(send):
    def body(s_ref, r_ref, send_sem, recv_sem):
        x, y, c = _mesh_pos()
        cp = pltpu.make_async_remote_copy(src_ref=s_ref, dst_ref=r_ref, send_sem=send_sem, recv_sem=recv_sem,
                                          device_id=(x, y, 1 - c), device_id_type=MESH)
        cp.start()
        cp.wait()

    hbm = pl.BlockSpec(memory_space=pl.ANY)
    return _pcall(
        body, name="rs_sibling", out_shape=_sds(send.shape, send.dtype), in_specs=[hbm], out_specs=hbm,
        scratch_shapes=[pltpu.SemaphoreType.DMA, pltpu.SemaphoreType.DMA],
        compiler_params=_cp(),
    )(send)


def _chip_exchange(part):
    def body(p_ref, r_ref, send_sems, recv_sems):
        x, y, c = _mesh_pos()
        chips = [(1 - x, y), (x, 1 - y), (1 - x, 1 - y)]
        cps = [
            pltpu.make_async_remote_copy(src_ref=p_ref.at[2 * cx + cy], dst_ref=r_ref.at[j], send_sem=send_sems.at[j],
                                         recv_sem=recv_sems.at[j], device_id=(cx, cy, c), device_id_type=MESH)
            for j, (cx, cy) in enumerate(chips)
        ]
        for cp in cps:
            cp.start()
        for cp in cps:
            cp.wait()

    hbm = pl.BlockSpec(memory_space=pl.ANY)
    return _pcall(
        body, name="rs_chips", out_shape=_sds((3,) + part.shape[1:], part.dtype), in_specs=[hbm], out_specs=hbm,
        scratch_shapes=[pltpu.SemaphoreType.DMA((3,)), pltpu.SemaphoreType.DMA((3,))],
        compiler_params=_cp(),
    )(part)


def _row_tile(rows, cap=1040):
    return max(tt for tt in range(16, cap + 1, 16) if rows % tt == 0)


def _add2(a, b):
    nq, rows, cols = a.shape
    tm = _row_tile(rows)

    def body(a_ref, b_ref, o_ref):
        o_ref[...] = (a_ref[...].astype(F32) + b_ref[...].astype(F32)).astype(BF16)

    spec = pl.BlockSpec((1, tm, cols), lambda q, i: (q, i, 0))
    return _pcall(body, name="rs_add_sibling", out_shape=_sds(a.shape, BF16), grid=(nq, rows // tm),
                  in_specs=[spec, spec], out_specs=spec, compiler_params=_cp())(a, b)


def _add4(own, recv):
    rows, cols = own.shape
    tm = _row_tile(rows)

    def body(o_ref, r_ref, g_ref):
        g_ref[...] = ((o_ref[...].astype(F32) + r_ref[0].astype(F32)) + r_ref[1].astype(F32)) + r_ref[2].astype(F32)

    return _pcall(
        body, name="rs_add_chips", out_shape=_sds((rows, cols), F32), grid=(rows // tm,),
        in_specs=[pl.BlockSpec((tm, cols), lambda i: (i, 0)), pl.BlockSpec((3, tm, cols), lambda i: (0, i, 0))],
        out_specs=pl.BlockSpec((tm, cols), lambda i: (i, 0)), compiler_params=_cp(),
    )(own, recv)


def _all_reduce_small(v):
    rows, cols = v.shape

    def body(v_ref, o_ref, slots, send_sems, recv_sems):
        x, y, c = _mesh_pos()
        me = 4 * x + 2 * y + c
        slots[me] = v_ref[...]
        cps = []
        for k in range(1, N_DEV):
            px, py, pc = x ^ (k >> 2), y ^ ((k >> 1) & 1), c ^ (k & 1)
            cps.append(pltpu.make_async_remote_copy(
                src_ref=v_ref, dst_ref=slots.at[me], send_sem=send_sems.at[k - 1], recv_sem=recv_sems.at[k - 1],
                device_id=(px, py, pc), device_id_type=MESH))
        for cp in cps:
            cp.start()
        for cp in cps:
            cp.wait()
        acc = slots[0]
        for k in range(1, N_DEV):
            acc = acc + slots[k]
        o_ref[...] = acc

    return _pcall(
        body, name="all_reduce_small", out_shape=_sds((rows, cols), F32),
        in_specs=[pl.BlockSpec(memory_space=pltpu.VMEM)], out_specs=pl.BlockSpec(memory_space=pltpu.VMEM),
        scratch_shapes=[pltpu.VMEM((N_DEV, rows, cols), F32), pltpu.SemaphoreType.DMA((7,)), pltpu.SemaphoreType.DMA((7,))],
        compiler_params=_cp(),
    )(v)


def _adamw(w, g, m, v, *, name):
    rows, cols = w.shape
    tm, tn = rows, cols
    for cand in (256, 128, 64, 32, 16, 8):
        if rows > cand and rows % cand == 0:
            tm = cand
            break
    if tm == rows and rows > 256 and cols % 128 == 0:
        tn = 128
    c1 = 1.0 - ADAM_B1 ** ADAM_STEP
    c2 = 1.0 - ADAM_B2 ** ADAM_STEP

    def body(w_ref, g_ref, m_ref, v_ref, d_ref, nm_ref, nv_ref):
        gv = g_ref[...]
        mn = ADAM_B1 * m_ref[...] + (1.0 - ADAM_B1) * gv
        vn = ADAM_B2 * v_ref[...] + (1.0 - ADAM_B2) * (gv * gv)
        d_ref[...] = -ADAM_LR * ((mn / c1) / (jnp.sqrt(vn / c2) + ADAM_EPS) + ADAM_WD * w_ref[...])
        nm_ref[...] = mn
        nv_ref[...] = vn

    spec = pl.BlockSpec((tm, tn), lambda i, j: (i, j))
    return _pcall(
        body, name=name, out_shape=(_sds((rows, cols), F32),) * 3, grid=(rows // tm, cols // tn),
        in_specs=[spec] * 4, out_specs=(spec,) * 3, compiler_params=_cp(),
    )(w, g, m, v)


def _pad_rows(a, rows):
    return a if a.shape[0] == rows else jnp.pad(a, ((0, rows - a.shape[0]),) + ((0, 0),) * (a.ndim - 1))


def kernel(x, mem, norm1_g, w_in, conv_w, conv_b, fox_f_bias, fox_q_g, fox_k_g, mem_norm_g, w_mem_kv, xa_q_g, xa_k_g, w_br_conv, w_br_fox, w_br_xa, w_o, norm2_g, w_ffn_in, w_ffn_out, loss_target, m_norm1_g, m_w_in, m_conv_w, m_conv_b, m_fox_f_bias, m_fox_q_g, m_fox_k_g, m_mem_norm_g, m_w_mem_kv, m_xa_q_g, m_xa_k_g, m_w_br_conv, m_w_br_fox, m_w_br_xa, m_w_o, m_norm2_g, m_w_ffn_in, m_w_ffn_out, v_norm1_g, v_w_in, v_conv_w, v_conv_b, v_fox_f_bias, v_fox_q_g, v_fox_k_g, v_mem_norm_g, v_w_mem_kv, v_xa_q_g, v_xa_k_g, v_w_br_conv, v_w_br_fox, v_w_br_xa, v_w_o, v_norm2_g, v_w_ffn_in, v_w_ffn_out):
    nb, s, d = x.shape
    t = nb * s
    nm = mem.shape[1]
    nh = d // HD
    dff = w_ffn_out.shape[1] * N_DEV
    n_in_sh = w_in.shape[2]
    n_in = n_in_sh * N_DEV
    n_in_pad = -(-n_in // 1152) * 1152
    ff_block = (10 * d) // HD
    my_c = lax.axis_index("c")
    my_q = 2 * lax.axis_index("x") + lax.axis_index("y")

    row_sizes = [n_in_sh, w_mem_kv.shape[2], w_br_conv.shape[1], w_br_fox.shape[1], w_br_xa.shape[1], w_o.shape[1],
                 w_ffn_in.shape[2], w_ffn_out.shape[1]]
    padded = [-(-r // 16) * 16 for r in row_sizes]
    offs = [sum(padded[:i]) for i in range(len(padded))]
    pack_rows = sum(padded)
    shard_blocks = [w_in[0].T, w_mem_kv[0].T, w_br_conv[0], w_br_fox[0], w_br_xa[0], w_o[0], w_ffn_in[0].T, w_ffn_out[0]]
    pack = jnp.concatenate([_pad_rows(b.astype(BF16), p) for b, p in zip(shard_blocks, padded)], axis=0)
    cw_block = _pad_rows(conv_w[0], 8)
    gathered, cw_g = _all_gather([pack, cw_block])

    def full(i):
        return gathered[:, offs[i]:offs[i] + row_sizes[i], :].reshape(N_DEV * row_sizes[i], d)

    win_t = _pad_rows(full(0), n_in_pad)
    wkv_t, wbc, wbf, wbx, wo, wfi_t, wfo = (full(i) for i in range(1, 8))
    conv_w_full = jnp.transpose(cw_g[:, 0:3, :], (1, 0, 2)).reshape(3, d)

    x2 = x.reshape(t, d)
    mem2 = mem.reshape(nb * nm, d)
    tgt2 = loss_target.reshape(t, d)
    h1 = _rms_fwd(x2, norm1_g, name="rms1_fwd")
    z = _mm(h1, win_t, m=t, n=n_in_pad, k=d, tb=True, tm=1024, tn=1152, name="mm_z")
    y_conv = _conv_fwd(z, conv_w_full, conv_b, nb=nb, s=s, d=d)
    qn, kn, vb = _fox_prep(z, fox_q_g, fox_k_g, t=t, d=d)
    f_bias_col = fox_f_bias.reshape(nh, 1)
    ck = _fox_gates2(z, f_bias_col, nb=nb, s=s, nh=nh, ff_block=ff_block)
    ck_rows = ck.reshape(nb * nh, 1, s)
    y_fox, lse = _fox_fwd2(qn, kn, vb, ck_rows, nb=nb, s=s, nh=nh)
    mem_n = _rms_fwd(mem2, mem_norm_g, name="rms_mem_fwd")
    kv = _mm(mem_n, wkv_t, m=nb * nm, n=2 * d, k=d, tb=True, name="mm_kv")
    xkn, xvb = _xa_prep_kv(kv, xa_k_g, rows=nb * nm, d=d)
    y_xa = _xa_fwd(z, xkn, xvb, xa_q_g, nb=nb, s=s, d=d, nm=nm, q_block=6)
    pa = _mm(y_conv, wbc, m=t, n=d, k=d, name="mm_pa")
    pb = _mm(y_fox, wbf, m=t, n=d, k=d, name="mm_pb")
    pc = _mm(y_xa, wbx, m=t, n=d, k=d, name="mm_pc")
    merged = _merge_fwd(z, pa, pb, pc, t=t, d=d, gate_block=7)
    x1 = _mm(merged, wo, m=t, n=d, k=d, res=x2, name="mm_x1")
    h2 = _rms_fwd(x1, norm2_g, name="rms2_fwd")
    gu = _mm(h2, wfi_t, m=t, n=2 * dff, k=d, tb=True, tm=1024, tn=1408, name="mm_gu")
    act = _swiglu_fwd(gu, t=t, dff=dff)
    y = _mm(act, wfo, m=t, n=d, k=dff, res=x1, tk=1408, name="mm_y")
    dy, loss_local = _loss_head(y, tgt2, t=t, d=d)

    da = _mm(dy, wfo, m=t, n=dff, k=d, tb=True, tn=1408, name="mm_da")
    g_wfo = _mm(act, dy, m=dff, n=d, k=t, ta=True, tm=1408, tk=512, name="mm_g_wfo")
    dgate, dup = _swiglu_bwd(gu, da, t=t, dff=dff)
    dh2 = _mm(dgate, wfi_t, m=t, n=d, k=dff, tk=1408, name="mm_dh2_g")
    dh2 = _mm(dup, wfi_t, m=t, n=d, k=dff, tk=1408, b_off=(2, 0), res=dh2, name="mm_dh2_u")
    g_wfi_g = _mm(dgate, h2, m=dff, n=d, k=t, ta=True, tm=1408, tk=512, name="mm_g_wfi_g")
    g_wfi_u = _mm(dup, h2, m=dff, n=d, k=t, ta=True, tm=1408, tk=512, name="mm_g_wfi_u")
    dx1, g_norm2 = _rms_bwd(dh2, x1, norm2_g, dy, name="rms2_bwd")
    dmerged = _mm(dx1, wo, m=t, n=d, k=d, tb=True, name="mm_dmerged")
    g_wo = _mm(merged, dx1, m=d, n=d, k=t, ta=True, tk=512, name="mm_g_wo")

    dz = lax.empty((t, n_in_pad), BF16)
    dz, dpa = _gate_bwd(z, dmerged, pa, dz, t=t, d=d, col_block=7, name="gate_bwd_a")
    dz, dpb = _gate_bwd(z, dmerged, pb, dz, t=t, d=d, col_block=8, name="gate_bwd_b")
    dz, dpc = _gate_bwd(z, dmerged, pc, dz, t=t, d=d, col_block=9, name="gate_bwd_c")
    dy_conv = _mm(dpa, wbc, m=t, n=d, k=d, tb=True, name="mm_dy_conv")
    g_wbc = _mm(y_conv, dpa, m=d, n=d, k=t, ta=True, tk=512, name="mm_g_wbc")
    dy_fox = _mm(dpb, wbf, m=t, n=d, k=d, tb=True, name="mm_dy_fox")
    g_wbf = _mm(y_fox, dpb, m=d, n=d, k=t, ta=True, tk=512, name="mm_g_wbf")
    dy_xa = _mm(dpc, wbx, m=t, n=d, k=d, tb=True, name="mm_dy_xa")
    g_wbx = _mm(y_xa, dpc, m=d, n=d, k=t, ta=True, tk=512, name="mm_g_wbx")

    dz, g_conv_w, g_conv_b = _conv_bwd(z, dy_conv, conv_w_full, conv_b, dz, nb=nb, s=s, d=d)

    dqn, ds_rows = _fox_bwd_dq2(qn, kn, vb, ck_rows, dy_fox, y_fox, lse, nb=nb, s=s, nh=nh)
    dkn, dvb, ds_cols = _fox_bwd_dkv2(qn, kn, vb, ck_rows, dy_fox, y_fox, lse, nb=nb, s=s, nh=nh)
    dz, g_fox_q, g_fox_k = _fox_post(z, dqn, dkn, dvb, fox_q_g, fox_k_g, dz, t=t, d=d)
    dz, g_f_bias = _fox_gates_bwd2(z, f_bias_col, ds_rows, ds_cols, dz, nb=nb, s=s, nh=nh, ff_block=ff_block)

    dz, dxkn, dxv, g_xa_q = _xa_bwd(z, xkn, xvb, xa_q_g, dy_xa, dz, nb=nb, s=s, d=d, nm=nm, q_block=6)
    dkv, g_xa_k = _xa_post_kv(kv, dxkn, dxv, xa_k_g, rows=nb * nm, d=d)
    g_wkv_t = _mm(dkv, mem_n, m=2 * d, n=d, k=nb * nm, ta=True, name="mm_g_wkv")
    dmem_n = _mm(dkv, wkv_t, m=nb * nm, n=d, k=2 * d, name="mm_dmem")
    _, g_mem_norm = _rms_bwd(dmem_n, mem2, mem_norm_g, None, name="rms_mem_bwd")

    dh1 = _mm(dz, win_t, m=t, n=d, k=n_in_pad, tk=1152, name="mm_dh1")
    g_win_t = _mm(dz, h1, m=n_in_pad, n=d, k=t, ta=True, tm=1152, tk=512, name="mm_g_win")
    dx, g_norm1 = _rms_bwd(dh1, x2, norm1_g, dx1, name="rms1_bwd")

    tail = jnp.concatenate(
        [_pad_rows(g_f_bias.reshape(nh), HD).reshape(1, HD), g_fox_q, g_fox_k, g_xa_q, g_xa_k,
         jnp.zeros((1, d - 3 * HD - 2 * (d // XH)), F32)], axis=1)
    small = jnp.concatenate([g_norm1, g_norm2, g_conv_w, g_conv_b, g_mem_norm, tail], axis=0)
    small = _all_reduce_small(small)
    xd = d // XH
    gs_norm1, gs_norm2, gs_conv_w_full, gs_conv_b, gs_mem_norm = small[0:1], small[1:2], small[2:5], small[5:6], small[6:7]
    gs_f_bias = small[7:8, 0:nh]
    gs_fox_q, gs_fox_k = small[7:8, HD:2 * HD], small[7:8, 2 * HD:3 * HD]
    gs_xa_q, gs_xa_k = small[7:8, 3 * HD:3 * HD + xd], small[7:8, 3 * HD + xd:3 * HD + 2 * xd]
    gs_conv_w = lax.dynamic_slice_in_dim(gs_conv_w_full, (4 * lax.axis_index("x") + 2 * lax.axis_index("y") + my_c) * (d // N_DEV),
                                         d // N_DEV, axis=1)

    full_grads = [g_win_t[:n_in], g_wkv_t, g_wbc, g_wbf, g_wbx, g_wo, jnp.concatenate([g_wfi_g, g_wfi_u], axis=0), g_wfo]

    def pack_for(core):
        parts = []
        for gfull, r, p in zip(full_grads, row_sizes, padded):
            blk = gfull.reshape(N_DEV // 2, 2, r, d)
            blk = lax.dynamic_index_in_dim(blk, core, axis=1, keepdims=False).astype(BF16)
            if p != r:
                blk = jnp.pad(blk, ((0, 0), (0, p - r), (0, 0)))
            parts.append(blk)
        return jnp.concatenate(parts, axis=1)

    keep, send = pack_for(my_c), pack_for(1 - my_c)
    from_sibling = _sibling_exchange(send)
    part = _add2(keep, from_sibling)
    from_chips = _chip_exchange(part)
    own = lax.dynamic_index_in_dim(part, my_q, axis=0, keepdims=False)
    g_pack = _add4(own, from_chips)

    g_shards = [g_pack[offs[i]:offs[i] + row_sizes[i], :] for i in range(8)]
    transposed = {"w_in", "w_mem_kv", "w_ffn_in"}

    grads = {
        "norm1_g": gs_norm1, "w_in": g_shards[0], "conv_w": gs_conv_w, "conv_b": gs_conv_b, "fox_f_bias": gs_f_bias,
        "fox_q_g": gs_fox_q, "fox_k_g": gs_fox_k, "mem_norm_g": gs_mem_norm, "w_mem_kv": g_shards[1], "xa_q_g": gs_xa_q,
        "xa_k_g": gs_xa_k, "w_br_conv": g_shards[2], "w_br_fox": g_shards[3], "w_br_xa": g_shards[4], "w_o": g_shards[5],
        "norm2_g": gs_norm2, "w_ffn_in": g_shards[6], "w_ffn_out": g_shards[7],
    }
    weights = {
        "norm1_g": (norm1_g, m_norm1_g, v_norm1_g), "w_in": (w_in, m_w_in, v_w_in), "conv_w": (conv_w, m_conv_w, v_conv_w),
        "conv_b": (conv_b, m_conv_b, v_conv_b), "fox_f_bias": (fox_f_bias, m_fox_f_bias, v_fox_f_bias),
        "fox_q_g": (fox_q_g, m_fox_q_g, v_fox_q_g), "fox_k_g": (fox_k_g, m_fox_k_g, v_fox_k_g),
        "mem_norm_g": (mem_norm_g, m_mem_norm_g, v_mem_norm_g), "w_mem_kv": (w_mem_kv, m_w_mem_kv, v_w_mem_kv),
        "xa_q_g": (xa_q_g, m_xa_q_g, v_xa_q_g), "xa_k_g": (xa_k_g, m_xa_k_g, v_xa_k_g),
        "w_br_conv": (w_br_conv, m_w_br_conv, v_w_br_conv), "w_br_fox": (w_br_fox, m_w_br_fox, v_w_br_fox),
        "w_br_xa": (w_br_xa, m_w_br_xa, v_w_br_xa), "w_o": (w_o, m_w_o, v_w_o), "norm2_g": (norm2_g, m_norm2_g, v_norm2_g),
        "w_ffn_in": (w_ffn_in, m_w_ffn_in, v_w_ffn_in), "w_ffn_out": (w_ffn_out, m_w_ffn_out, v_w_ffn_out),
    }
    out_g, out_d, out_m, out_v = [], [], [], []
    for name, (w, m, v) in weights.items():
        shape = w.shape
        tr = name in transposed

        def as2d(a):
            a = a.reshape(shape[-2], shape[-1])
            return a.T if tr else a

        def back(a):
            return (a.T if tr else a).reshape(shape)

        w2 = as2d(w)
        g2 = grads[name].reshape(w2.shape)
        dl, mn, vn = _adamw(w2, g2, as2d(m), as2d(v), name="adamw_" + name)
        out_g.append(back(g2))
        out_d.append(back(dl))
        out_m.append(back(mn))
        out_v.append(back(vn))

    loss = lax.psum(loss_local, ("x", "y", "c"))
    return (loss, dx.reshape(nb, s, d), *out_g, *out_d, *out_m, *out_v)
```

```python
import math

import jax
import jax.numpy as jnp
from jax import lax
from jax.experimental import pallas as pl
from jax.experimental.pallas import tpu as pltpu

F32, BF16 = jnp.float32, jnp.bfloat16
EPS = 1e-6
HD = 128
XH = 4
N_DEV = 8
MESH = pl.DeviceIdType.MESH
VMEM_LIMIT = 52 * 1024 * 1024
NEG = -1e30
SQRT_HD = math.sqrt(HD)
EXP2_C = math.log2(math.e) / SQRT_HD
LOG2E = math.log2(math.e)
FOX_HP = 4

ADAM_LR, ADAM_B1, ADAM_B2, ADAM_EPS, ADAM_WD, ADAM_STEP = 0.001, 0.9, 0.999, 1e-08, 0.01, 10


def _cp(**kw):
    return pltpu.CompilerParams(vmem_limit_bytes=VMEM_LIMIT, **kw)


def _pcall(body, **kw):
    return pl.pallas_call(body, **kw)


def _sds(shape, dtype):
    return jax.ShapeDtypeStruct(shape, dtype)


HBM_SPEC = pl.BlockSpec(memory_space=pl.ANY)


def _mesh_pos():
    return lax.axis_index("x"), lax.axis_index("y"), lax.axis_index("c")


class _AllGather:
    def __init__(self, blocks):
        self.blocks = list(blocks)
        n = len(self.blocks)
        self.out_shape = [_sds((N_DEV,) + b.shape, b.dtype) for b in self.blocks]
        self.scratch = [pltpu.SemaphoreType.DMA((n, 7)), pltpu.SemaphoreType.DMA((n, 7)), pltpu.SemaphoreType.DMA((n,))]

    def bind(self, ins, outs, sems):
        n = len(ins)
        send_sems, recv_sems, local_sems = sems
        x, y, c = _mesh_pos()
        me, sibling = (x, y, c), (x, y, 1 - c)
        chips = [(1 - x, y), (x, 1 - y), (1 - x, 1 - y)]

        def slab(t, dev):
            return outs[t].at[4 * dev[0] + 2 * dev[1] + dev[2]]

        def copy(t, k, block, to, src=None):
            dst = slab(t, block)
            return pltpu.make_async_remote_copy(
                src_ref=dst if src is None else src, dst_ref=dst, send_sem=send_sems.at[t, k], recv_sem=recv_sems.at[t, k],
                device_id=to, device_id_type=MESH)

        mine = [pltpu.make_async_copy(ins[t], slab(t, me), local_sems.at[t]) for t in range(n)]
        first = []
        for t in range(n):
            first.append(copy(t, 0, me, sibling, src=ins[t]))
            first += [copy(t, 1 + j, me, (*chip, c), src=ins[t]) for j, chip in enumerate(chips)]
        passed = [copy(t, 4 + j, (*chip, c), sibling) for j, chip in enumerate(chips) for t in range(n)]

        def start():
            for cp in mine + first:
                cp.start()

        def mid():
            for j, chip in enumerate(chips):
                for t in range(n):
                    copy(t, 1 + j, (*chip, c), me).wait_recv()
                    passed[j * n + t].start()

        def finish():
            for t in range(n):
                copy(t, 0, sibling, me).wait_recv()
                for j, chip in enumerate(chips):
                    copy(t, 4 + j, (*chip, 1 - c), me).wait_recv()
            for cp in first + passed:
                cp.wait_send()
            for cp in mine:
                cp.wait()

        return start, mid, finish


class _SiblingSwap:
    def __init__(self, send):
        self.blocks = [send]
        self.out_shape = [_sds(send.shape, send.dtype)]
        self.scratch = [pltpu.SemaphoreType.DMA, pltpu.SemaphoreType.DMA]

    def bind(self, ins, outs, sems):
        x, y, c = _mesh_pos()
        cp = pltpu.make_async_remote_copy(src_ref=ins[0], dst_ref=outs[0], send_sem=sems[0], recv_sem=sems[1],
                                          device_id=(x, y, 1 - c), device_id_type=MESH)
        return cp.start, (lambda: None), cp.wait


class _ChipExchange:
    def __init__(self, part):
        self.blocks = [part]
        self.out_shape = [_sds((3,) + part.shape[1:], part.dtype)]
        self.scratch = [pltpu.SemaphoreType.DMA((3,)), pltpu.SemaphoreType.DMA((3,))]

    def bind(self, ins, outs, sems):
        x, y, c = _mesh_pos()
        chips = [(1 - x, y), (x, 1 - y), (1 - x, 1 - y)]
        cps = [
            pltpu.make_async_remote_copy(src_ref=ins[0].at[2 * cx + cy], dst_ref=outs[0].at[j], send_sem=sems[0].at[j],
                                         recv_sem=sems[1].at[j], device_id=(cx, cy, c), device_id_type=MESH)
            for j, (cx, cy) in enumerate(chips)
        ]

        def start():
            for cp in cps:
                cp.start()

        def finish():
            for cp in cps:
                cp.wait()

        return start, (lambda: None), finish


def _comm_lists(comm):
    if comm is None:
        return [], [], [], []
    n = len(comm.blocks)
    return list(comm.blocks), [HBM_SPEC] * n, list(comm.out_shape), list(comm.scratch)


def _split_refs(refs, n_pre, n_in, n_cin, n_out, n_cout, n_scr):
    cuts = [n_pre, n_in, n_cin, n_out, n_cout, n_scr]
    out, at = [], 0
    for c in cuts:
        out.append(refs[at:at + c])
        at += c
    out.append(refs[at:])
    return out


def _run_comm(comm, cin, cout, csem, step, n_steps, when="start"):
    if comm is None:
        return
    start, mid, finish = comm.bind(cin, cout, csem)
    if when == "start":
        pl.when(step == 0)(start)
    else:
        pl.when(step == max(n_steps - 2, 0))(mid)
        pl.when(step == n_steps - 1)(finish)


def _comm_only(comm, name):
    ins, in_specs, out_shape, scratch = _comm_lists(comm)
    n = len(ins)

    def body(*refs):
        start, mid, finish = comm.bind(refs[:n], refs[n:2 * n], refs[2 * n:])
        start()
        mid()
        finish()

    outs = _pcall(body, name=name, out_shape=tuple(out_shape), in_specs=in_specs, out_specs=tuple([HBM_SPEC] * n),
                  scratch_shapes=scratch, compiler_params=_cp())(*ins)
    return list(outs)


def _all_reduce_small(v):
    rows, cols = v.shape

    def body(v_ref, o_ref, slots, send_sems, recv_sems):
        x, y, c = _mesh_pos()
        me = 4 * x + 2 * y + c
        slots[me] = v_ref[...]
        cps = []
        for k in range(1, N_DEV):
            px, py, pc = x ^ (k >> 2), y ^ ((k >> 1) & 1), c ^ (k & 1)
            cps.append(pltpu.make_async_remote_copy(
                src_ref=v_ref, dst_ref=slots.at[me], send_sem=send_sems.at[k - 1], recv_sem=recv_sems.at[k - 1],
                device_id=(px, py, pc), device_id_type=MESH))
        for cp in cps:
            cp.start()
        for cp in cps:
            cp.wait()
        acc = slots[0]
        for k in range(1, N_DEV):
            acc = acc + slots[k]
        o_ref[...] = acc

    return _pcall(
        body, name="all_reduce_small", out_shape=_sds((rows, cols), F32),
        in_specs=[pl.BlockSpec(memory_space=pltpu.VMEM)], out_specs=pl.BlockSpec(memory_space=pltpu.VMEM),
        scratch_shapes=[pltpu.VMEM((N_DEV, rows, cols), F32), pltpu.SemaphoreType.DMA((7,)), pltpu.SemaphoreType.DMA((7,))],
        compiler_params=_cp(),
    )(v)


def _row_tile(rows, cap=1040):
    return max(tt for tt in range(16, cap + 1, 16) if rows % tt == 0)


def _add2(a, b, *, name):
    nq, rows, cols = a.shape
    tm = _row_tile(rows)

    def body(a_ref, b_ref, o_ref):
        o_ref[...] = (a_ref[...].astype(F32) + b_ref[...].astype(F32)).astype(BF16)

    spec = pl.BlockSpec((1, tm, cols), lambda q, i: (q, i, 0))
    return _pcall(body, name=name, out_shape=_sds(a.shape, BF16), grid=(nq, rows // tm),
                  in_specs=[spec, spec], out_specs=spec, compiler_params=_cp())(a, b)


def _add4(own, recv, *, name):
    rows, cols = own.shape
    tm = _row_tile(rows)

    def body(o_ref, r_ref, g_ref):
        g_ref[...] = ((o_ref[...].astype(F32) + r_ref[0].astype(F32)) + r_ref[1].astype(F32)) + r_ref[2].astype(F32)

    return _pcall(
        body, name=name, out_shape=_sds((rows, cols), F32), grid=(rows // tm,),
        in_specs=[pl.BlockSpec((tm, cols), lambda i: (i, 0)), pl.BlockSpec((3, tm, cols), lambda i: (0, i, 0))],
        out_specs=pl.BlockSpec((tm, cols), lambda i: (i, 0)), compiler_params=_cp(),
    )(own, recv)


def _mm(a, b, *, m, n, k, name, ta=False, tb=False, b_off=(0, 0), res=None, out_dtype=F32, tm=512, tn=1024, tk=1024,
        comm=None):
    tm, tn, tk = min(tm, m), min(tn, n), min(tk, k)
    assert m % tm == 0 and n % tn == 0 and k % tk == 0, (name, m, n, k, tm, tn, tk)
    nk = k // tk
    grid = (m // tm, n // tn, nk)
    bo0, bo1 = b_off
    if ta:
        a_spec = pl.BlockSpec((tk, tm), lambda i, j, kk: (kk, i))
    else:
        a_spec = pl.BlockSpec((tm, tk), lambda i, j, kk: (i, kk))
    if tb:
        b_spec = pl.BlockSpec((tn, tk), lambda i, j, kk: (j + bo0, kk + bo1))
    else:
        b_spec = pl.BlockSpec((tk, tn), lambda i, j, kk: (kk + bo0, j + bo1))
    o_spec = pl.BlockSpec((tm, tn), lambda i, j, kk: (i, j))
    dn = (((0 if ta else 1,), (1 if tb else 0,)), ((), ()))
    has_res = res is not None
    c_ins, c_in_specs, c_out_shape, c_scratch = _comm_lists(comm)
    n_in = 3 if has_res else 2
    n_scr = 1 if nk > 1 else 0

    def body(*refs):
        _, ins, cin, outs, cout, scr, csem = _split_refs(refs, 0, n_in, len(c_ins), 1, len(c_ins), n_scr)
        a_ref, b_ref = ins[0], ins[1]
        r_ref = ins[2] if has_res else None
        o_ref = outs[0]
        step = (pl.program_id(0) * grid[1] + pl.program_id(1)) * nk + pl.program_id(2)
        _run_comm(comm, cin, cout, csem, step, grid[0] * grid[1] * nk, "start")
        p = lax.dot_general(a_ref[...].astype(BF16), b_ref[...].astype(BF16), dn, preferred_element_type=F32)

        def finish(acc):
            if has_res:
                acc = acc + r_ref[...]
            o_ref[...] = acc.astype(out_dtype)

        if nk == 1:
            finish(p)
        else:
            acc_ref = scr[0]
            kk = pl.program_id(2)

            @pl.when(kk == 0)
            def _():
                acc_ref[...] = p

            @pl.when(kk > 0)
            def _():
                acc_ref[...] += p

            @pl.when(kk == nk - 1)
            def _():
                finish(acc_ref[...])

        _run_comm(comm, cin, cout, csem, step, grid[0] * grid[1] * nk, "end")

    ins = [a, b] + ([res] if has_res else [])
    in_specs = [a_spec, b_spec] + ([o_spec] if has_res else [])
    sem = ("arbitrary",) * 3 if comm is not None else ("parallel", "parallel", "arbitrary")
    outs = _pcall(
        body, name=name, out_shape=tuple([_sds((m, n), out_dtype)] + c_out_shape), grid=grid,
        in_specs=in_specs + c_in_specs, out_specs=tuple([o_spec] + [HBM_SPEC] * len(c_ins)),
        scratch_shapes=([pltpu.VMEM((tm, tn), F32)] if nk > 1 else []) + c_scratch,
        compiler_params=_cp(dimension_semantics=sem),
    )(*(ins + c_ins))
    return outs[0] if comm is None else (outs[0], list(outs[1:]))


def _rms_fwd(x, g, *, name, tm=512):
    t, d = x.shape
    tm = min(tm, t)

    def body(x_ref, g_ref, h_ref):
        xv = x_ref[...]
        r = lax.rsqrt(jnp.mean(xv * xv, axis=-1, keepdims=True) + EPS)
        h_ref[...] = (xv * r * g_ref[...]).astype(BF16)

    return _pcall(
        body, name=name, out_shape=_sds((t, d), BF16), grid=(t // tm,),
        in_specs=[pl.BlockSpec((tm, d), lambda i: (i, 0)), pl.BlockSpec((1, d), lambda i: (0, 0))],
        out_specs=pl.BlockSpec((tm, d), lambda i: (i, 0)), compiler_params=_cp(),
    )(x, g)


def _rms_bwd(dh, x, g, res, *, name, tm=512):
    t, d = x.shape
    tm = min(tm, t)
    has_res = res is not None

    def body(*refs):
        dh_ref, x_ref, g_ref = refs[:3]
        r_ref = refs[3] if has_res else None
        dx_ref, dg_ref = refs[-2], refs[-1]
        xv = x_ref[...]
        r = lax.rsqrt(jnp.mean(xv * xv, axis=-1, keepdims=True) + EPS)
        xh = xv * r
        dhv = dh_ref[...]
        gd = dhv * g_ref[...]
        dx = r * (gd - xh * jnp.mean(gd * xh, axis=-1, keepdims=True))
        if has_res:
            dx = dx + r_ref[...]
        dx_ref[...] = dx

        @pl.when(pl.program_id(0) == 0)
        def _():
            dg_ref[...] = jnp.zeros_like(dg_ref)

        dg_ref[...] += jnp.sum(dhv * xh, axis=0, keepdims=True)

    row = pl.BlockSpec((tm, d), lambda i: (i, 0))
    vec = pl.BlockSpec((1, d), lambda i: (0, 0))
    return _pcall(
        body, name=name, out_shape=(_sds((t, d), F32), _sds((1, d), F32)), grid=(t // tm,),
        in_specs=[row, row, vec] + ([row] if has_res else []), out_specs=(row, vec),
        compiler_params=_cp(dimension_semantics=("arbitrary",)),
    )(*([dh, x, g] + ([res] if has_res else [])))


def _conv_fwd(z, conv_w, conv_b, *, nb, s, d, tc=256):
    nc = d // tc

    def body(cb_ref, cc_ref, cv_ref, w_ref, b_ref, y_ref):
        u = cc_ref[...] * cv_ref[...]
        row = lax.broadcasted_iota(jnp.int32, u.shape, 0)
        u1 = jnp.where(row >= 1, pltpu.roll(u, 1, axis=0), 0.0)
        u2 = jnp.where(row >= 2, pltpu.roll(u, 2, axis=0), 0.0)
        w = w_ref[...]
        y = w[2:3, :] * u + w[1:2, :] * u1 + w[0:1, :] * u2 + b_ref[...]
        y_ref[...] = (cb_ref[...] * y).astype(BF16)

    def part(p):
        return pl.BlockSpec((s, tc), lambda b, j: (b, p * nc + j))

    return _pcall(
        body, name="conv_fwd", out_shape=_sds((nb * s, d), BF16), grid=(nb, nc),
        in_specs=[part(0), part(1), part(2), pl.BlockSpec((3, tc), lambda b, j: (0, j)), pl.BlockSpec((1, tc), lambda b, j: (0, j))],
        out_specs=pl.BlockSpec((s, tc), lambda b, j: (b, j)), compiler_params=_cp(),
    )(z, z, z, conv_w, conv_b)


def _conv_bwd(z, dy, conv_w, conv_b, dz, *, nb, s, d, tc=256):
    nc = d // tc

    def body(cb_ref, cc_ref, cv_ref, dy_ref, w_ref, b_ref, dz_in, dz_ref, dw_ref, db_ref, stash):
        b_i, p = pl.program_id(1), pl.program_id(2)

        @pl.when(p == 0)
        def _():
            cc, cv = cc_ref[...], cv_ref[...]
            u = cc * cv
            n = u.shape[0]
            row = lax.broadcasted_iota(jnp.int32, u.shape, 0)
            u1 = jnp.where(row >= 1, pltpu.roll(u, 1, axis=0), 0.0)
            u2 = jnp.where(row >= 2, pltpu.roll(u, 2, axis=0), 0.0)
            w = w_ref[...]
            y = w[2:3, :] * u + w[1:2, :] * u1 + w[0:1, :] * u2 + b_ref[...]
            dyv = dy_ref[...]
            dconv = dyv * cb_ref[...]
            g1 = jnp.where(row < n - 1, pltpu.roll(dconv, n - 1, axis=0), 0.0)
            g2 = jnp.where(row < n - 2, pltpu.roll(dconv, n - 2, axis=0), 0.0)
            du = w[2:3, :] * dconv + w[1:2, :] * g1 + w[0:1, :] * g2
            stash[0] = (dyv * y).astype(BF16)
            stash[1] = (du * cv).astype(BF16)
            stash[2] = (du * cc).astype(BF16)
            dws = [jnp.sum(dconv * u2, axis=0, keepdims=True), jnp.sum(dconv * u1, axis=0, keepdims=True),
                   jnp.sum(dconv * u, axis=0, keepdims=True)]
            db = jnp.sum(dconv, axis=0, keepdims=True)

            @pl.when(b_i == 0)
            def _():
                for r in range(3):
                    dw_ref[r:r + 1, :] = dws[r]
                db_ref[...] = db

            @pl.when(b_i > 0)
            def _():
                for r in range(3):
                    dw_ref[r:r + 1, :] += dws[r]
                db_ref[...] += db

        dz_ref[...] = stash[p]

    return _pcall(
        body, name="conv_bwd",
        out_shape=(_sds(dz.shape, dz.dtype), _sds((3, d), F32), _sds((1, d), F32)), grid=(nc, nb, 3),
        in_specs=[
            pl.BlockSpec((s, tc), lambda j, b, p: (b, j)), pl.BlockSpec((s, tc), lambda j, b, p: (b, nc + j)),
            pl.BlockSpec((s, tc), lambda j, b, p: (b, 2 * nc + j)), pl.BlockSpec((s, tc), lambda j, b, p: (b, j)),
            pl.BlockSpec((3, tc), lambda j, b, p: (0, j)), pl.BlockSpec((1, tc), lambda j, b, p: (0, j)),
            HBM_SPEC,
        ],
        out_specs=(pl.BlockSpec((s, tc), lambda j, b, p: (b, p * nc + j)),
                   pl.BlockSpec((3, tc), lambda j, b, p: (0, j)), pl.BlockSpec((1, tc), lambda j, b, p: (0, j))),
        scratch_shapes=[pltpu.VMEM((3, s, tc), BF16)],
        input_output_aliases={6: 0},
        compiler_params=_cp(dimension_semantics=("arbitrary", "arbitrary", "arbitrary")),
    )(z, z, z, dy, conv_w, conv_b, dz)


def _head_rms(xv, g):
    r = lax.rsqrt(jnp.mean(xv * xv, axis=-1, keepdims=True) + EPS)
    return xv * r * g


def _head_rms_bwd(xv, g, dy):
    r = lax.rsqrt(jnp.mean(xv * xv, axis=-1, keepdims=True) + EPS)
    xh = xv * r
    gd = dy * g
    dx = r * (gd - xh * jnp.mean(gd * xh, axis=-1, keepdims=True))
    return dx, jnp.sum(dy * xh, axis=0, keepdims=True)


def _fox_prep(z, q_g, k_g, *, t, d, tm=256):
    nh = d // HD

    def body(z_ref, qg_ref, kg_ref, q_ref, k_ref, v_ref):
        qg, kg = qg_ref[...], kg_ref[...]
        for h in range(nh):
            c0 = h * HD
            q_ref[:, c0:c0 + HD] = _head_rms(z_ref[:, c0:c0 + HD], qg).astype(BF16)
            k_ref[:, c0:c0 + HD] = _head_rms(z_ref[:, d + c0:d + c0 + HD], kg).astype(BF16)
        v_ref[...] = z_ref[:, 2 * d:3 * d].astype(BF16)

    o = pl.BlockSpec((tm, d), lambda i: (i, 0))
    g = pl.BlockSpec((1, HD), lambda i: (0, 0))
    return _pcall(
        body, name="fox_prep", out_shape=(_sds((t, d), BF16),) * 3, grid=(t // tm,),
        in_specs=[pl.BlockSpec((tm, 3 * d), lambda i: (i, 1)), g, g], out_specs=(o, o, o), compiler_params=_cp(),
    )(z, q_g, k_g)


def _lane_cumsum(v, reverse=False):
    n = v.shape[1]
    lane = lax.broadcasted_iota(jnp.int32, v.shape, 1)
    sh = 1
    while sh < n:
        if reverse:
            v = v + jnp.where(lane < n - sh, pltpu.roll(v, n - sh, axis=1), 0.0)
        else:
            v = v + jnp.where(lane >= sh, pltpu.roll(v, sh, axis=1), 0.0)
        sh *= 2
    return v


def _fox_gates(z, f_bias_col, *, nb, s, nh, ff_block):
    def body(z_ref, b_ref, c_ref):
        ff = z_ref[...].T[0:nh, :] + b_ref[...]
        lf = jnp.minimum(ff, 0.0) - jnp.log(1.0 + jnp.exp(-jnp.abs(ff)))
        c_ref[0] = _lane_cumsum(lf) * SQRT_HD

    return _pcall(
        body, name="fox_gates", out_shape=_sds((nb, nh, s), F32), grid=(nb,),
        in_specs=[pl.BlockSpec((s, HD), lambda b: (b, ff_block)), pl.BlockSpec((nh, 1), lambda b: (0, 0))],
        out_specs=pl.BlockSpec((1, nh, s), lambda b: (b, 0, 0)), compiler_params=_cp(),
    )(z, f_bias_col)


def _fox_gates_bwd(z, f_bias_col, row_sums, col_sums, dz, *, nb, s, nh, ff_block):
    def body(z_ref, b_ref, rs_ref, cs_ref, dz_in, dz_ref, db_ref, dc_sc):
        b_i, h = pl.program_id(0), pl.program_id(1)
        dc_sc[pl.ds(h, 1), :] = (rs_ref[...] - cs_ref[...]).T[0:1, :]

        @pl.when(h == nh - 1)
        def _():
            ff = z_ref[...].T[0:nh, :] + b_ref[...]
            dlf = _lane_cumsum(dc_sc[...], reverse=True)
            dff = dlf * (1.0 / (1.0 + jnp.exp(ff)))
            pad = jnp.concatenate([dff, jnp.zeros((HD - nh, s), F32)], axis=0)
            dz_ref[...] = pad.T.astype(BF16)
            dbv = jnp.sum(dff, axis=1, keepdims=True)

            @pl.when(b_i == 0)
            def _():
                db_ref[...] = dbv

            @pl.when(b_i > 0)
            def _():
                db_ref[...] += dbv

    hs = pl.BlockSpec((s, HD), lambda b, h: (b, h))
    zs = pl.BlockSpec((s, HD), lambda b, h: (b, ff_block))
    return _pcall(
        body, name="fox_gates_bwd", out_shape=(_sds(dz.shape, dz.dtype), _sds((nh, 1), F32)), grid=(nb, nh),
        in_specs=[zs, pl.BlockSpec((nh, 1), lambda b, h: (0, 0)), hs, hs, HBM_SPEC],
        out_specs=(zs, pl.BlockSpec((nh, 1), lambda b, h: (0, 0))),
        scratch_shapes=[pltpu.VMEM((nh, s), F32)],
        input_output_aliases={4: 0}, compiler_params=_cp(dimension_semantics=("arbitrary", "arbitrary")),
    )(z, f_bias_col, row_sums, col_sums, dz)


def _fox_pairs(nq, key_major):
    if key_major:
        pairs = [(i, j) for j in range(nq) for i in range(j, nq)]
    else:
        pairs = [(i, j) for i in range(nq) for j in range(i + 1)]
    return jnp.array([p[0] for p in pairs], jnp.int32), jnp.array([p[1] for p in pairs], jnp.int32)


def _with_ones(x):
    return jnp.concatenate([x, jnp.ones_like(x)], axis=1)


def _fox_specs(nb, nh, nq, tq, tk, hp):
    qs = pl.BlockSpec((tq, hp * HD), lambda b, h, p, it, jt: (b * nq + it[p], h))
    ks = pl.BlockSpec((tk, hp * HD), lambda b, h, p, it, jt: (b * nq + jt[p], h))
    cs = pl.BlockSpec((hp, 1, tk), lambda b, h, p, it, jt: (b * (nh // hp) + h, 0, jt[p]))
    return qs, ks, cs


def _fox_raw(qv, kv, ckv, diag, tq, tk):
    sc = lax.dot_general(qv, kv, (((1,), (1,)), ((), ())), preferred_element_type=F32) - ckv
    if diag:
        rows = lax.broadcasted_iota(jnp.int32, (tq, tk), 0)
        cols = lax.broadcasted_iota(jnp.int32, (tq, tk), 1)
        sc = jnp.where(rows >= cols, sc, NEG)
    return sc


def _fox_probs(qv, kv, ckv, lse_col, diag, tq, tk):
    sc = _fox_raw(qv, kv, ckv, diag, tq, tk)
    return jnp.exp2(sc * EXP2_C - lse_col * LOG2E)


def _fox_call(body, *, name, ins, in_specs, out_shape, out_specs, scratch, grid, tables, comm):
    c_ins, c_in_specs, c_out_shape, c_scratch = _comm_lists(comm)
    n_in, n_out, n_scr, nc = len(ins), len(out_shape), len(scratch), len(c_ins)
    n_steps = grid[0] * grid[1] * grid[2]

    def wrapped(*refs):
        pre, rin, cin, rout, cout, scr, csem = _split_refs(refs, 2, n_in, nc, n_out, nc, n_scr)
        step = (pl.program_id(0) * grid[1] + pl.program_id(1)) * grid[2] + pl.program_id(2)
        _run_comm(comm, cin, cout, csem, step, n_steps, "start")
        body(*pre, *rin, *rout, *scr)
        _run_comm(comm, cin, cout, csem, step, n_steps, "end")

    sem = ("arbitrary",) * 3 if comm is not None else ("parallel", "parallel", "arbitrary")
    outs = _pcall(
        wrapped, name=name, out_shape=tuple(list(out_shape) + c_out_shape),
        grid_spec=pltpu.PrefetchScalarGridSpec(
            num_scalar_prefetch=2, grid=grid, in_specs=list(in_specs) + c_in_specs,
            out_specs=tuple(list(out_specs) + [HBM_SPEC] * nc), scratch_shapes=list(scratch) + c_scratch),
        compiler_params=_cp(dimension_semantics=sem),
    )(*tables, *ins, *c_ins)
    return list(outs[:n_out]), list(outs[n_out:])


def _fox_fwd(q, k, v, cks, *, nb, s, nh, tq=512, comm=None):
    tk = tq
    nq = s // tq
    t = nb * s
    hp = FOX_HP
    it, jt = _fox_pairs(nq, key_major=False)

    def body(it_ref, jt_ref, q_ref, k_ref, v_ref, c_ref, o_ref, lse_ref, m_sc, acc_sc):
        p_id = pl.program_id(2)
        i, j = it_ref[p_id], jt_ref[p_id]

        @pl.when(j == 0)
        def _():
            m_sc[...] = jnp.full_like(m_sc, NEG)
            acc_sc[...] = jnp.zeros_like(acc_sc)

        def step(diag):
            for hh in range(hp):
                cols = slice(hh * HD, (hh + 1) * HD)
                sc = _fox_raw(q_ref[:, cols], k_ref[:, cols], c_ref[hh], diag, tq, tk)
                m_old = m_sc[hh]
                m_new = jnp.maximum(m_old, jnp.max(sc, axis=-1, keepdims=True))
                alpha = jnp.exp2((m_old - m_new) * EXP2_C)
                p = jnp.exp2((sc - m_new) * EXP2_C).astype(BF16)
                acc = alpha * acc_sc[hh] + jnp.dot(p, _with_ones(v_ref[:, cols]), preferred_element_type=F32)
                if diag:
                    l = acc[:, HD:2 * HD]
                    o_ref[:, cols] = (acc[:, 0:HD] / l).astype(BF16)
                    lse_ref[:, cols] = m_new * (1.0 / SQRT_HD) + jnp.log(l)
                else:
                    acc_sc[hh] = acc
                    m_sc[hh] = m_new

        @pl.when(j < i)
        def _():
            step(False)

        @pl.when(j == i)
        def _():
            step(True)

    qs, ks, cs = _fox_specs(nb, nh, nq, tq, tk, hp)
    (o, lse), c_out = _fox_call(
        body, name="fox_fwd", ins=[q, k, v, cks], in_specs=[qs, ks, ks, cs],
        out_shape=[_sds((t, nh * HD), BF16), _sds((t, nh * HD), F32)], out_specs=[qs, qs],
        scratch=[pltpu.VMEM((hp, tq, 1), F32), pltpu.VMEM((hp, tq, 2 * HD), F32)],
        grid=(nb, nh // hp, int(it.shape[0])), tables=(it, jt), comm=comm)
    return o, lse, c_out


def _fox_bwd_dq(q, k, v, cks, do, o, lse, *, nb, s, nh, tq=512, comm=None):
    tk = tq
    nq = s // tq
    t = nb * s
    hp = FOX_HP
    it, jt = _fox_pairs(nq, key_major=False)

    def body(it_ref, jt_ref, q_ref, k_ref, v_ref, c_ref, do_ref, o_ref, lse_ref, dq_ref, rs_ref, acc_sc, dl_sc):
        p_id = pl.program_id(2)
        i, j = it_ref[p_id], jt_ref[p_id]

        @pl.when(j == 0)
        def _():
            acc_sc[...] = jnp.zeros_like(acc_sc)
            for hh in range(hp):
                cols = slice(hh * HD, (hh + 1) * HD)
                dl_sc[hh] = jnp.sum(do_ref[:, cols] * o_ref[:, cols].astype(F32), axis=-1, keepdims=True)

        def step(diag):
            for hh in range(hp):
                cols = slice(hh * HD, (hh + 1) * HD)
                kv = k_ref[:, cols]
                p = _fox_probs(q_ref[:, cols], kv, c_ref[hh], lse_ref[:, hh * HD:hh * HD + 1], diag, tq, tk)
                dp = lax.dot_general(do_ref[:, cols].astype(BF16), v_ref[:, cols], (((1,), (1,)), ((), ())),
                                     preferred_element_type=F32)
                ds = (p * (dp - dl_sc[hh])).astype(BF16)
                acc = acc_sc[hh] + jnp.dot(ds, _with_ones(kv), preferred_element_type=F32)
                if diag:
                    dq_ref[:, cols] = acc[:, 0:HD] * (1.0 / SQRT_HD)
                    rs_ref[:, cols] = acc[:, HD:2 * HD]
                else:
                    acc_sc[hh] = acc

        @pl.when(j < i)
        def _():
            step(False)

        @pl.when(j == i)
        def _():
            step(True)

    qs, ks, cs = _fox_specs(nb, nh, nq, tq, tk, hp)
    (dq, rs), c_out = _fox_call(
        body, name="fox_bwd_dq", ins=[q, k, v, cks, do, o, lse], in_specs=[qs, ks, ks, cs, qs, qs, qs],
        out_shape=[_sds((t, nh * HD), F32), _sds((t, nh * HD), F32)], out_specs=[qs, qs],
        scratch=[pltpu.VMEM((hp, tq, 2 * HD), F32), pltpu.VMEM((hp, tq, 1), F32)],
        grid=(nb, nh // hp, int(it.shape[0])), tables=(it, jt), comm=comm)
    return dq, rs, c_out


def _fox_bwd_dkv(q, k, v, cks, do, o, lse, *, nb, s, nh, tq=512, comm=None):
    tk = tq
    nq = s // tq
    t = nb * s
    hp = FOX_HP
    it, jt = _fox_pairs(nq, key_major=True)

    def body(it_ref, jt_ref, q_ref, k_ref, v_ref, c_ref, do_ref, o_ref, lse_ref, dk_ref, dv_ref, cs_ref, dk_sc, dv_sc):
        p_id = pl.program_id(2)
        i, j = it_ref[p_id], jt_ref[p_id]

        def step(diag):
            for hh in range(hp):
                cols = slice(hh * HD, (hh + 1) * HD)
                qv = q_ref[:, cols]
                p = _fox_probs(qv, k_ref[:, cols], c_ref[hh], lse_ref[:, hh * HD:hh * HD + 1], diag, tq, tk)
                dov = do_ref[:, cols]
                dob = dov.astype(BF16)
                dvp = lax.dot_general(p.astype(BF16), dob, (((0,), (0,)), ((), ())), preferred_element_type=F32)
                dp = lax.dot_general(dob, v_ref[:, cols], (((1,), (1,)), ((), ())), preferred_element_type=F32)
                delta = jnp.sum(dov * o_ref[:, cols].astype(F32), axis=-1, keepdims=True)
                ds = (p * (dp - delta)).astype(BF16)
                dkp = lax.dot_general(ds, _with_ones(qv), (((0,), (0,)), ((), ())), preferred_element_type=F32)
                if diag:
                    dk_sc[hh] = dkp
                    dv_sc[hh] = dvp
                else:
                    dk_sc[hh] += dkp
                    dv_sc[hh] += dvp

        @pl.when(i == j)
        def _():
            step(True)

        @pl.when(i > j)
        def _():
            step(False)

        @pl.when(i == nq - 1)
        def _():
            for hh in range(hp):
                cols = slice(hh * HD, (hh + 1) * HD)
                dk_ref[:, cols] = dk_sc[hh, :, 0:HD] * (1.0 / SQRT_HD)
                cs_ref[:, cols] = dk_sc[hh, :, HD:2 * HD]
                dv_ref[:, cols] = dv_sc[hh].astype(BF16)

    qs, ks, cs = _fox_specs(nb, nh, nq, tq, tk, hp)
    (dk, dv, csum), c_out = _fox_call(
        body, name="fox_bwd_dkv", ins=[q, k, v, cks, do, o, lse], in_specs=[qs, ks, ks, cs, qs, qs, qs],
        out_shape=[_sds((t, nh * HD), F32), _sds((t, nh * HD), BF16), _sds((t, nh * HD), F32)], out_specs=[ks, ks, ks],
        scratch=[pltpu.VMEM((hp, tk, 2 * HD), F32), pltpu.VMEM((hp, tk, HD), F32)],
        grid=(nb, nh // hp, int(it.shape[0])), tables=(it, jt), comm=comm)
    return dk, dv, csum, c_out


def _fox_post(z, dqn, dkn, dv, q_g, k_g, dz, *, t, d, tm=256):
    nh = d // HD

    def body(z_ref, dq_ref, dk_ref, dv_ref, qg_ref, kg_ref, dz_in, dz_ref, dqg_ref, dkg_ref):
        qg, kg = qg_ref[...], kg_ref[...]
        aq = jnp.zeros((1, HD), F32)
        ak = jnp.zeros((1, HD), F32)
        for h in range(nh):
            c0 = h * HD
            dx, sg = _head_rms_bwd(z_ref[:, c0:c0 + HD], qg, dq_ref[:, c0:c0 + HD])
            dz_ref[:, c0:c0 + HD] = dx.astype(BF16)
            aq = aq + sg
            dx, sg = _head_rms_bwd(z_ref[:, d + c0:d + c0 + HD], kg, dk_ref[:, c0:c0 + HD])
            dz_ref[:, d + c0:d + c0 + HD] = dx.astype(BF16)
            ak = ak + sg
        dz_ref[:, 2 * d:3 * d] = dv_ref[...]

        @pl.when(pl.program_id(0) == 0)
        def _():
            dqg_ref[...] = aq
            dkg_ref[...] = ak

        @pl.when(pl.program_id(0) > 0)
        def _():
            dqg_ref[...] += aq
            dkg_ref[...] += ak

    o = pl.BlockSpec((tm, d), lambda i: (i, 0))
    g = pl.BlockSpec((1, HD), lambda i: (0, 0))
    z3 = pl.BlockSpec((tm, 3 * d), lambda i: (i, 1))
    return _pcall(
        body, name="fox_post", out_shape=(_sds(dz.shape, dz.dtype), _sds((1, HD), F32), _sds((1, HD), F32)), grid=(t // tm,),
        in_specs=[z3, o, o, o, g, g, HBM_SPEC], out_specs=(z3, g, g),
        input_output_aliases={6: 0}, compiler_params=_cp(dimension_semantics=("arbitrary",)),
    )(z, dqn, dkn, dv, q_g, k_g, dz)


def _xa_prep_kv(kv, k_g, *, rows, d):
    xd = d // XH

    def body(kv_ref, g_ref, k_ref, v_ref):
        g = g_ref[...]
        for h in range(XH):
            c0 = h * xd
            k_ref[:, c0:c0 + xd] = _head_rms(kv_ref[:, c0:c0 + xd], g).astype(BF16)
        v_ref[...] = kv_ref[:, d:2 * d].astype(BF16)

    return _pcall(body, name="xa_prep_kv", out_shape=(_sds((rows, d), BF16),) * 2, compiler_params=_cp())(kv, k_g)


def _xa_fwd(z, kn, vb, q_g, *, nb, s, d, nm, q_block, tq=512):
    xd = d // XH
    nq = s // tq

    def body(z_ref, k_ref, v_ref, g_ref, y_ref):
        g = g_ref[...]
        for h in range(XH):
            c0 = h * xd
            qn = _head_rms(z_ref[:, c0:c0 + xd], g).astype(BF16)
            sc = lax.dot_general(qn, k_ref[:, c0:c0 + xd], (((1,), (1,)), ((), ())), preferred_element_type=F32)
            sc = sc / math.sqrt(xd)
            p = jnp.exp(sc - jnp.max(sc, axis=-1, keepdims=True))
            p = p / jnp.sum(p, axis=-1, keepdims=True)
            y_ref[:, c0:c0 + xd] = jnp.dot(p.astype(BF16), v_ref[:, c0:c0 + xd], preferred_element_type=F32).astype(BF16)

    kvs = pl.BlockSpec((nm, d), lambda b, i: (b, 0))
    return _pcall(
        body, name="xa_fwd", out_shape=_sds((nb * s, d), BF16), grid=(nb, nq),
        in_specs=[pl.BlockSpec((tq, d), lambda b, i: (b * nq + i, q_block)), kvs, kvs, pl.BlockSpec((1, xd), lambda b, i: (0, 0))],
        out_specs=pl.BlockSpec((tq, d), lambda b, i: (b * nq + i, 0)), compiler_params=_cp(),
    )(z, kn, vb, q_g)


def _xa_bwd(z, kn, vb, q_g, dy, dz, *, nb, s, d, nm, q_block, tq=512):
    xd = d // XH
    nq = s // tq

    def body(z_ref, k_ref, v_ref, g_ref, dy_ref, dz_in, dz_ref, dk_ref, dv_ref, dg_ref):
        b_i, i = pl.program_id(0), pl.program_id(1)
        g = g_ref[...]

        @pl.when(i == 0)
        def _():
            dk_ref[...] = jnp.zeros_like(dk_ref)
            dv_ref[...] = jnp.zeros_like(dv_ref)

        @pl.when((i == 0) & (b_i == 0))
        def _():
            dg_ref[...] = jnp.zeros_like(dg_ref)

        for h in range(XH):
            c0 = h * xd
            xq = z_ref[:, c0:c0 + xd]
            qn = _head_rms(xq, g).astype(BF16)
            kh, vh = k_ref[:, c0:c0 + xd], v_ref[:, c0:c0 + xd]
            sc = lax.dot_general(qn, kh, (((1,), (1,)), ((), ())), preferred_element_type=F32) / math.sqrt(xd)
            p = jnp.exp(sc - jnp.max(sc, axis=-1, keepdims=True))
            p = p / jnp.sum(p, axis=-1, keepdims=True)
            dob = dy_ref[:, c0:c0 + xd].astype(BF16)
            dv_ref[:, c0:c0 + xd] += lax.dot_general(p.astype(BF16), dob, (((0,), (0,)), ((), ())), preferred_element_type=F32)
            dp = lax.dot_general(dob, vh, (((1,), (1,)), ((), ())), preferred_element_type=F32)
            ds = p * (dp - jnp.sum(p * dp, axis=-1, keepdims=True)) / math.sqrt(xd)
            dsb = ds.astype(BF16)
            dqn = jnp.dot(dsb, kh, preferred_element_type=F32)
            dk_ref[:, c0:c0 + xd] += lax.dot_general(dsb, qn, (((0,), (0,)), ((), ())), preferred_element_type=F32)
            dx, sg = _head_rms_bwd(xq, g, dqn)
            dz_ref[:, c0:c0 + xd] = dx.astype(BF16)
            dg_ref[...] += sg

    kvs = pl.BlockSpec((nm, d), lambda b, i: (b, 0))
    qs = pl.BlockSpec((tq, d), lambda b, i: (b * nq + i, q_block))
    gs = pl.BlockSpec((1, xd), lambda b, i: (0, 0))
    return _pcall(
        body, name="xa_bwd",
        out_shape=(_sds(dz.shape, dz.dtype), _sds((nb * nm, d), F32), _sds((nb * nm, d), F32), _sds((1, xd), F32)), grid=(nb, nq),
        in_specs=[qs, kvs, kvs, gs, pl.BlockSpec((tq, d), lambda b, i: (b * nq + i, 0)), HBM_SPEC],
        out_specs=(qs, kvs, kvs, gs), input_output_aliases={5: 0},
        compiler_params=_cp(dimension_semantics=("arbitrary", "arbitrary")),
    )(z, kn, vb, q_g, dy, dz)


def _xa_post_kv(kv, dkn, dv, k_g, *, rows, d):
    xd = d // XH

    def body(kv_ref, dk_ref, dv_ref, g_ref, dkv_ref, dg_ref):
        g = g_ref[...]
        acc = jnp.zeros((1, xd), F32)
        for h in range(XH):
            c0 = h * xd
            dx, sg = _head_rms_bwd(kv_ref[:, c0:c0 + xd], g, dk_ref[:, c0:c0 + xd])
            dkv_ref[:, c0:c0 + xd] = dx.astype(BF16)
            acc = acc + sg
        dkv_ref[:, d:2 * d] = dv_ref[...].astype(BF16)
        dg_ref[...] = acc

    return _pcall(body, name="xa_post_kv", out_shape=(_sds((rows, 2 * d), BF16), _sds((1, xd), F32)), compiler_params=_cp())(
        kv, dkn, dv, k_g)


def _sigmoid(v):
    return 1.0 / (1.0 + jnp.exp(-v))


def _merge_fwd(z, pa, pb, pc, *, t, d, gate_block, tm=512):
    def body(ga_ref, gb_ref, gc_ref, pa_ref, pb_ref, pc_ref, o_ref):
        o_ref[...] = (_sigmoid(ga_ref[...]) * pa_ref[...] + _sigmoid(gb_ref[...]) * pb_ref[...]
                      + _sigmoid(gc_ref[...]) * pc_ref[...]).astype(BF16)

    def gate(q):
        return pl.BlockSpec((tm, d), lambda i: (i, gate_block + q))

    o = pl.BlockSpec((tm, d), lambda i: (i, 0))
    return _pcall(
        body, name="merge_fwd", out_shape=_sds((t, d), BF16), grid=(t // tm,),
        in_specs=[gate(0), gate(1), gate(2), o, o, o], out_specs=o, compiler_params=_cp(),
    )(z, z, z, pa, pb, pc)


def _gate_bwd(z, dm, pr, dz, *, t, d, col_block, name, tm=512):
    def body(g_ref, dm_ref, p_ref, dz_in, dz_ref, dp_ref):
        sg = _sigmoid(g_ref[...])
        dmv = dm_ref[...]
        dp_ref[...] = (dmv * sg).astype(BF16)
        dz_ref[...] = (dmv * p_ref[...] * sg * (1.0 - sg)).astype(BF16)

    gs = pl.BlockSpec((tm, d), lambda i: (i, col_block))
    o = pl.BlockSpec((tm, d), lambda i: (i, 0))
    return _pcall(
        body, name=name, out_shape=(_sds(dz.shape, dz.dtype), _sds((t, d), BF16)), grid=(t // tm,),
        in_specs=[gs, o, o, HBM_SPEC], out_specs=(gs, o), input_output_aliases={3: 0},
        compiler_params=_cp(),
    )(z, dm, pr, dz)


def _swiglu_fwd(gu, *, t, dff, tm=256):
    def body(g_ref, u_ref, a_ref):
        gv = g_ref[...]
        a_ref[...] = (gv * _sigmoid(gv) * u_ref[...]).astype(BF16)

    return _pcall(
        body, name="swiglu_fwd", out_shape=_sds((t, dff), BF16), grid=(t // tm,),
        in_specs=[pl.BlockSpec((tm, dff), lambda i: (i, 0)), pl.BlockSpec((tm, dff), lambda i: (i, 1))],
        out_specs=pl.BlockSpec((tm, dff), lambda i: (i, 0)), compiler_params=_cp(),
    )(gu, gu)


def _swiglu_bwd(gu, da, *, t, dff, tm=256):
    def body(g_ref, u_ref, da_ref, dg_ref, du_ref):
        gv, dav = g_ref[...], da_ref[...]
        sg = _sigmoid(gv)
        dg_ref[...] = (dav * u_ref[...] * sg * (1.0 + gv * (1.0 - sg))).astype(BF16)
        du_ref[...] = (dav * gv * sg).astype(BF16)

    h0 = pl.BlockSpec((tm, dff), lambda i: (i, 0))
    h1 = pl.BlockSpec((tm, dff), lambda i: (i, 1))
    return _pcall(
        body, name="swiglu_bwd", out_shape=(_sds((t, dff), BF16), _sds((t, dff), BF16)), grid=(t // tm,),
        in_specs=[h0, h1, h0], out_specs=(h0, h0), compiler_params=_cp(),
    )(gu, gu, da)


def _loss_head(y, target, *, t, d, tm=512):
    def body(y_ref, t_ref, dy_ref, l_ref):
        e = y_ref[...] - t_ref[...]
        dy_ref[...] = e * (1.0 / d)

        @pl.when(pl.program_id(0) == 0)
        def _():
            l_ref[...] = jnp.zeros_like(l_ref)

        l_ref[...] += jnp.sum(jnp.sum(e * e, axis=-1, keepdims=True), axis=0, keepdims=True) * (0.5 / d)

    o = pl.BlockSpec((tm, d), lambda i: (i, 0))
    dy, l = _pcall(
        body, name="loss_head", out_shape=(_sds((t, d), F32), _sds((1, HD), F32)), grid=(t // tm,),
        in_specs=[o, o], out_specs=(o, pl.BlockSpec((1, HD), lambda i: (0, 0))),
        compiler_params=_cp(dimension_semantics=("arbitrary",)),
    )(y, target)
    return dy, l[0, 0]


def _adamw(w, g, m, v, *, name):
    rows, cols = w.shape
    tm, tn = rows, cols
    for cand in (256, 128, 64, 32, 16, 8):
        if rows > cand and rows % cand == 0:
            tm = cand
            break
    if tm == rows and rows > 256 and cols % 128 == 0:
        tn = 128
    c1 = 1.0 - ADAM_B1 ** ADAM_STEP
    c2 = 1.0 - ADAM_B2 ** ADAM_STEP

    def body(w_ref, g_ref, m_ref, v_ref, d_ref, nm_ref, nv_ref):
        gv = g_ref[...]
        mn = ADAM_B1 * m_ref[...] + (1.0 - ADAM_B1) * gv
        vn = ADAM_B2 * v_ref[...] + (1.0 - ADAM_B2) * (gv * gv)
        d_ref[...] = -ADAM_LR * ((mn / c1) / (jnp.sqrt(vn / c2) + ADAM_EPS) + ADAM_WD * w_ref[...])
        nm_ref[...] = mn
        nv_ref[...] = vn

    spec = pl.BlockSpec((tm, tn), lambda i, j: (i, j))
    return _pcall(
        body, name=name, out_shape=(_sds((rows, cols), F32),) * 3, grid=(rows // tm, cols // tn),
        in_specs=[spec] * 4, out_specs=(spec,) * 3, compiler_params=_cp(),
    )(w, g, m, v)


def _pad_rows(a, rows):
    return a if a.shape[0] == rows else jnp.pad(a, ((0, rows - a.shape[0]),) + ((0, 0),) * (a.ndim - 1))


class _Pack:
    def __init__(self, row_sizes):
        self.rows = list(row_sizes)
        self.padded = [-(-r // 16) * 16 for r in self.rows]
        self.offs = [sum(self.padded[:i]) for i in range(len(self.rows))]
        self.total = sum(self.padded)

    def stack(self, blocks):
        return jnp.concatenate([_pad_rows(b.astype(BF16), p) for b, p in zip(blocks, self.padded)], axis=0)

    def full(self, gathered, i):
        r = self.rows[i]
        return gathered[:, self.offs[i]:self.offs[i] + r, :].reshape(N_DEV * r, gathered.shape[2])

    def for_core(self, full_grads, core):
        parts = []
        for gfull, r, p in zip(full_grads, self.rows, self.padded):
            blk = gfull.reshape(N_DEV // 2, 2, r, gfull.shape[1])
            blk = lax.dynamic_index_in_dim(blk, core, axis=1, keepdims=False).astype(BF16)
            if p != r:
                blk = jnp.pad(blk, ((0, 0), (0, p - r), (0, 0)))
            parts.append(blk)
        return jnp.concatenate(parts, axis=1)

    def shard(self, g_pack, i):
        return g_pack[self.offs[i]:self.offs[i] + self.rows[i], :]


def kernel(x, mem, norm1_g, w_in, conv_w, conv_b, fox_f_bias, fox_q_g, fox_k_g, mem_norm_g, w_mem_kv, xa_q_g, xa_k_g, w_br_conv, w_br_fox, w_br_xa, w_o, norm2_g, w_ffn_in, w_ffn_out, loss_target, m_norm1_g, m_w_in, m_conv_w, m_conv_b, m_fox_f_bias, m_fox_q_g, m_fox_k_g, m_mem_norm_g, m_w_mem_kv, m_xa_q_g, m_xa_k_g, m_w_br_conv, m_w_br_fox, m_w_br_xa, m_w_o, m_norm2_g, m_w_ffn_in, m_w_ffn_out, v_norm1_g, v_w_in, v_conv_w, v_conv_b, v_fox_f_bias, v_fox_q_g, v_fox_k_g, v_mem_norm_g, v_w_mem_kv, v_xa_q_g, v_xa_k_g, v_w_br_conv, v_w_br_fox, v_w_br_xa, v_w_o, v_norm2_g, v_w_ffn_in, v_w_ffn_out):
    nb, s, d = x.shape
    t = nb * s
    nm = mem.shape[1]
    nh = d // HD
    dff = w_ffn_out.shape[1] * N_DEV
    n_in = w_in.shape[2] * N_DEV
    n_in_pad = -(-n_in // 1152) * 1152
    ff_block = (10 * d) // HD
    my_c = lax.axis_index("c")
    my_q = 2 * lax.axis_index("x") + lax.axis_index("y")

    pack_a = _Pack([w_in.shape[2]])
    pack_b = _Pack([w_mem_kv.shape[2], w_br_conv.shape[1], w_br_fox.shape[1], w_br_xa.shape[1], w_o.shape[1],
                    w_ffn_in.shape[2], w_ffn_out.shape[1]])
    stack_a = pack_a.stack([w_in[0].T])
    stack_b = pack_b.stack([w_mem_kv[0].T, w_br_conv[0], w_br_fox[0], w_br_xa[0], w_o[0], w_ffn_in[0].T, w_ffn_out[0]])
    cw_block = _pad_rows(conv_w[0], 8)
    (gathered_a,) = _comm_only(_AllGather([stack_a]), "all_gather_w_in")
    win_t = _pad_rows(pack_a.full(gathered_a, 0), n_in_pad)

    x2 = x.reshape(t, d)
    mem2 = mem.reshape(nb * nm, d)
    tgt2 = loss_target.reshape(t, d)
    h1 = _rms_fwd(x2, norm1_g, name="rms1_fwd")
    z = _mm(h1, win_t, m=t, n=n_in_pad, k=d, tb=True, tm=1024, tn=1152, name="mm_z")
    qn, kn, vb = _fox_prep(z, fox_q_g, fox_k_g, t=t, d=d)
    f_bias_col = fox_f_bias.reshape(nh, 1)
    ck_rows = _fox_gates(z, f_bias_col, nb=nb, s=s, nh=nh, ff_block=ff_block).reshape(nb * nh, 1, s)
    y_fox, lse, (gathered_b, cw_g) = _fox_fwd(qn, kn, vb, ck_rows, nb=nb, s=s, nh=nh, comm=_AllGather([stack_b, cw_block]))
    wkv_t, wbc, wbf, wbx, wo, wfi_t, wfo = (pack_b.full(gathered_b, i) for i in range(7))
    conv_w_full = jnp.transpose(cw_g[:, 0:3, :], (1, 0, 2)).reshape(3, d)

    y_conv = _conv_fwd(z, conv_w_full, conv_b, nb=nb, s=s, d=d)
    mem_n = _rms_fwd(mem2, mem_norm_g, name="rms_mem_fwd")
    kv = _mm(mem_n, wkv_t, m=nb * nm, n=2 * d, k=d, tb=True, name="mm_kv")
    xkn, xvb = _xa_prep_kv(kv, xa_k_g, rows=nb * nm, d=d)
    y_xa = _xa_fwd(z, xkn, xvb, xa_q_g, nb=nb, s=s, d=d, nm=nm, q_block=6)
    pa = _mm(y_conv, wbc, m=t, n=d, k=d, name="mm_pa")
    pb = _mm(y_fox, wbf, m=t, n=d, k=d, name="mm_pb")
    pc = _mm(y_xa, wbx, m=t, n=d, k=d, name="mm_pc")
    merged = _merge_fwd(z, pa, pb, pc, t=t, d=d, gate_block=7)
    x1 = _mm(merged, wo, m=t, n=d, k=d, res=x2, name="mm_x1")
    h2 = _rms_fwd(x1, norm2_g, name="rms2_fwd")
    gu = _mm(h2, wfi_t, m=t, n=2 * dff, k=d, tb=True, tm=1024, tn=1408, name="mm_gu")
    act = _swiglu_fwd(gu, t=t, dff=dff)
    y = _mm(act, wfo, m=t, n=d, k=dff, res=x1, tk=1408, name="mm_y")
    dy, loss_local = _loss_head(y, tgt2, t=t, d=d)

    da = _mm(dy, wfo, m=t, n=dff, k=d, tb=True, tn=1408, name="mm_da")
    g_wfo = _mm(act, dy, m=dff, n=d, k=t, ta=True, tm=1408, tk=512, out_dtype=BF16, name="mm_g_wfo")
    dgate, dup = _swiglu_bwd(gu, da, t=t, dff=dff)
    dh2 = _mm(dgate, wfi_t, m=t, n=d, k=dff, tk=1408, name="mm_dh2_g")
    dh2 = _mm(dup, wfi_t, m=t, n=d, k=dff, tk=1408, b_off=(2, 0), res=dh2, name="mm_dh2_u")
    g_wfi_g = _mm(dgate, h2, m=dff, n=d, k=t, ta=True, tm=1408, tk=512, out_dtype=BF16, name="mm_g_wfi_g")
    g_wfi_u = _mm(dup, h2, m=dff, n=d, k=t, ta=True, tm=1408, tk=512, out_dtype=BF16, name="mm_g_wfi_u")
    dx1, g_norm2 = _rms_bwd(dh2, x1, norm2_g, dy, name="rms2_bwd")
    dmerged = _mm(dx1, wo, m=t, n=d, k=d, tb=True, name="mm_dmerged")
    g_wo = _mm(merged, dx1, m=d, n=d, k=t, ta=True, tk=512, out_dtype=BF16, name="mm_g_wo")

    dz = lax.empty((t, n_in_pad), BF16)
    dz, dpa = _gate_bwd(z, dmerged, pa, dz, t=t, d=d, col_block=7, name="gate_bwd_a")
    dz, dpb = _gate_bwd(z, dmerged, pb, dz, t=t, d=d, col_block=8, name="gate_bwd_b")
    dz, dpc = _gate_bwd(z, dmerged, pc, dz, t=t, d=d, col_block=9, name="gate_bwd_c")
    dy_conv = _mm(dpa, wbc, m=t, n=d, k=d, tb=True, name="mm_dy_conv")
    g_wbc = _mm(y_conv, dpa, m=d, n=d, k=t, ta=True, tk=512, out_dtype=BF16, name="mm_g_wbc")
    dy_fox = _mm(dpb, wbf, m=t, n=d, k=d, tb=True, name="mm_dy_fox")
    g_wbf = _mm(y_fox, dpb, m=d, n=d, k=t, ta=True, tk=512, out_dtype=BF16, name="mm_g_wbf")
    dy_xa = _mm(dpc, wbx, m=t, n=d, k=d, tb=True, name="mm_dy_xa")
    g_wbx = _mm(y_xa, dpc, m=d, n=d, k=t, ta=True, tk=512, out_dtype=BF16, name="mm_g_wbx")

    dz, g_conv_w, g_conv_b = _conv_bwd(z, dy_conv, conv_w_full, conv_b, dz, nb=nb, s=s, d=d)
    dz, dxkn, dxv, g_xa_q = _xa_bwd(z, xkn, xvb, xa_q_g, dy_xa, dz, nb=nb, s=s, d=d, nm=nm, q_block=6)
    dkv, g_xa_k = _xa_post_kv(kv, dxkn, dxv, xa_k_g, rows=nb * nm, d=d)
    g_wkv_t = _mm(dkv, mem_n, m=2 * d, n=d, k=nb * nm, ta=True, out_dtype=BF16, name="mm_g_wkv")
    dmem_n = _mm(dkv, wkv_t, m=nb * nm, n=d, k=2 * d, name="mm_dmem")
    _, g_mem_norm = _rms_bwd(dmem_n, mem2, mem_norm_g, None, name="rms_mem_bwd")

    grads_b = [g_wkv_t, g_wbc, g_wbf, g_wbx, g_wo, jnp.concatenate([g_wfi_g, g_wfi_u], axis=0), g_wfo]
    keep_b, send_b = pack_b.for_core(grads_b, my_c), pack_b.for_core(grads_b, 1 - my_c)
    dqn, ds_rows, (from_sibling_b,) = _fox_bwd_dq(qn, kn, vb, ck_rows, dy_fox, y_fox, lse, nb=nb, s=s, nh=nh,
                                                  comm=_SiblingSwap(send_b))
    part_b = _add2(keep_b, from_sibling_b, name="rs_add_sibling_b")
    dkn, dvb, ds_cols, (from_chips_b,) = _fox_bwd_dkv(qn, kn, vb, ck_rows, dy_fox, y_fox, lse, nb=nb, s=s, nh=nh,
                                                     comm=_ChipExchange(part_b))
    g_pack_b = _add4(lax.dynamic_index_in_dim(part_b, my_q, axis=0, keepdims=False), from_chips_b, name="rs_add_chips_b")
    dz, g_fox_q, g_fox_k = _fox_post(z, dqn, dkn, dvb, fox_q_g, fox_k_g, dz, t=t, d=d)
    dz, g_f_bias = _fox_gates_bwd(z, f_bias_col, ds_rows, ds_cols, dz, nb=nb, s=s, nh=nh, ff_block=ff_block)

    g_win_t = _mm(dz, h1, m=n_in_pad, n=d, k=t, ta=True, tm=1152, tk=512, out_dtype=BF16, name="mm_g_win")
    grads_a = [g_win_t[:n_in]]
    keep_a, send_a = pack_a.for_core(grads_a, my_c), pack_a.for_core(grads_a, 1 - my_c)
    dh1, (from_sibling_a,) = _mm(dz, win_t, m=t, n=d, k=n_in_pad, tk=1152, name="mm_dh1", comm=_SiblingSwap(send_a))
    part_a = _add2(keep_a, from_sibling_a, name="rs_add_sibling_a")
    (from_chips_a,) = _comm_only(_ChipExchange(part_a), "rs_chips_a")
    g_pack_a = _add4(lax.dynamic_index_in_dim(part_a, my_q, axis=0, keepdims=False), from_chips_a, name="rs_add_chips_a")
    dx, g_norm1 = _rms_bwd(dh1, x2, norm1_g, dx1, name="rms1_bwd")

    xd = d // XH
    tail = jnp.concatenate(
        [_pad_rows(g_f_bias.reshape(nh), HD).reshape(1, HD), g_fox_q, g_fox_k, g_xa_q, g_xa_k,
         jnp.zeros((1, d - 3 * HD - 2 * xd), F32)], axis=1)
    small = jnp.concatenate([g_norm1, g_norm2, g_conv_w, g_conv_b, g_mem_norm, tail], axis=0)
    small = _all_reduce_small(small)
    gs_conv_w = lax.dynamic_slice_in_dim(small[2:5], (2 * my_q + my_c) * (d // N_DEV), d // N_DEV, axis=1)

    grads = {
        "norm1_g": small[0:1], "w_in": pack_a.shard(g_pack_a, 0), "conv_w": gs_conv_w, "conv_b": small[5:6],
        "fox_f_bias": small[7:8, 0:nh], "fox_q_g": small[7:8, HD:2 * HD], "fox_k_g": small[7:8, 2 * HD:3 * HD],
        "mem_norm_g": small[6:7], "w_mem_kv": pack_b.shard(g_pack_b, 0), "xa_q_g": small[7:8, 3 * HD:3 * HD + xd],
        "xa_k_g": small[7:8, 3 * HD + xd:3 * HD + 2 * xd], "w_br_conv": pack_b.shard(g_pack_b, 1),
        "w_br_fox": pack_b.shard(g_pack_b, 2), "w_br_xa": pack_b.shard(g_pack_b, 3), "w_o": pack_b.shard(g_pack_b, 4),
        "norm2_g": small[1:2], "w_ffn_in": pack_b.shard(g_pack_b, 5), "w_ffn_out": pack_b.shard(g_pack_b, 6),
    }
    transposed = {"w_in", "w_mem_kv", "w_ffn_in"}
    weights = {
        "norm1_g": (norm1_g, m_norm1_g, v_norm1_g), "w_in": (w_in, m_w_in, v_w_in), "conv_w": (conv_w, m_conv_w, v_conv_w),
        "conv_b": (conv_b, m_conv_b, v_conv_b), "fox_f_bias": (fox_f_bias, m_fox_f_bias, v_fox_f_bias),
        "fox_q_g": (fox_q_g, m_fox_q_g, v_fox_q_g), "fox_k_g": (fox_k_g, m_fox_k_g, v_fox_k_g),
        "mem_norm_g": (mem_norm_g, m_mem_norm_g, v_mem_norm_g), "w_mem_kv": (w_mem_kv, m_w_mem_kv, v_w_mem_kv),
        "xa_q_g": (xa_q_g, m_xa_q_g, v_xa_q_g), "xa_k_g": (xa_k_g, m_xa_k_g, v_xa_k_g),
        "w_br_conv": (w_br_conv, m_w_br_conv, v_w_br_conv), "w_br_fox": (w_br_fox, m_w_br_fox, v_w_br_fox),
        "w_br_xa": (w_br_xa, m_w_br_xa, v_w_br_xa), "w_o": (w_o, m_w_o, v_w_o), "norm2_g": (norm2_g, m_norm2_g, v_norm2_g),
        "w_ffn_in": (w_ffn_in, m_w_ffn_in, v_w_ffn_in), "w_ffn_out": (w_ffn_out, m_w_ffn_out, v_w_ffn_out),
    }
    out_g, out_d, out_m, out_v = [], [], [], []
    for name, (w, m, v) in weights.items():
        shape = w.shape
        tr = name in transposed

        def as2d(a):
            a = a.reshape(shape[-2], shape[-1])
            return a.T if tr else a

        def back(a):
            return (a.T if tr else a).reshape(shape)

        w2 = as2d(w)
        g2 = grads[name].reshape(w2.shape)
        dl, mn, vn = _adamw(w2, g2, as2d(m), as2d(v), name="adamw_" + name)
        out_g.append(back(g2))
        out_d.append(back(dl))
        out_m.append(back(mn))
        out_v.append(back(vn))

    loss = lax.psum(loss_local, ("x", "y", "c"))
    return (loss, dx.reshape(nb, s, d), *out_g, *out_d, *out_m, *out_v)
```

```python
import math

import jax
import jax.numpy as jnp
from jax import lax
from jax.experimental import pallas as pl
from jax.experimental.pallas import tpu as pltpu

F32, BF16 = jnp.float32, jnp.bfloat16
EPS = 1e-6
HD = 128
XH = 4
N_DEV = 8
MESH = pl.DeviceIdType.MESH
VMEM_LIMIT = 52 * 1024 * 1024
NEG = -1e30
SQRT_HD = math.sqrt(HD)
EXP2_C = math.log2(math.e) / SQRT_HD
LOG2E = math.log2(math.e)
FOX_HP = 4

ADAM_LR, ADAM_B1, ADAM_B2, ADAM_EPS, ADAM_WD, ADAM_STEP = 0.001, 0.9, 0.999, 1e-08, 0.01, 10


def _cp(**kw):
    return pltpu.CompilerParams(vmem_limit_bytes=VMEM_LIMIT, **kw)


def _pcall(body, **kw):
    return pl.pallas_call(body, **kw)


def _sds(shape, dtype):
    return jax.ShapeDtypeStruct(shape, dtype)


HBM_SPEC = pl.BlockSpec(memory_space=pl.ANY)


def _mesh_pos():
    return lax.axis_index("x"), lax.axis_index("y"), lax.axis_index("c")


class _AllGather:
    def __init__(self, blocks):
        self.blocks = list(blocks)
        n = len(self.blocks)
        self.out_shape = [_sds((N_DEV,) + b.shape, b.dtype) for b in self.blocks]
        self.scratch = [pltpu.SemaphoreType.DMA((n, 7)), pltpu.SemaphoreType.DMA((n, 7)), pltpu.SemaphoreType.DMA((n,))]

    def bind(self, ins, outs, sems):
        n = len(ins)
        send_sems, recv_sems, local_sems = sems
        x, y, c = _mesh_pos()
        me, sibling = (x, y, c), (x, y, 1 - c)
        chips = [(1 - x, y), (x, 1 - y), (1 - x, 1 - y)]

        def slab(t, dev):
            return outs[t].at[4 * dev[0] + 2 * dev[1] + dev[2]]

        def copy(t, k, block, to, src=None):
            dst = slab(t, block)
            return pltpu.make_async_remote_copy(
                src_ref=dst if src is None else src, dst_ref=dst, send_sem=send_sems.at[t, k], recv_sem=recv_sems.at[t, k],
                device_id=to, device_id_type=MESH)

        mine = [pltpu.make_async_copy(ins[t], slab(t, me), local_sems.at[t]) for t in range(n)]
        first = []
        for t in range(n):
            first.append(copy(t, 0, me, sibling, src=ins[t]))
            first += [copy(t, 1 + j, me, (*chip, c), src=ins[t]) for j, chip in enumerate(chips)]
        passed = [copy(t, 4 + j, (*chip, c), sibling) for j, chip in enumerate(chips) for t in range(n)]

        def start():
            for cp in mine + first:
                cp.start()

        def mid():
            for j, chip in enumerate(chips):
                for t in range(n):
                    copy(t, 1 + j, (*chip, c), me).wait_recv()
                    passed[j * n + t].start()

        def finish():
            for t in range(n):
                copy(t, 0, sibling, me).wait_recv()
                for j, chip in enumerate(chips):
                    copy(t, 4 + j, (*chip, 1 - c), me).wait_recv()
            for cp in first + passed:
                cp.wait_send()
            for cp in mine:
                cp.wait()

        return start, mid, finish


class _SiblingSwap:
    def __init__(self, send):
        self.blocks = [send]
        self.out_shape = [_sds(send.shape, send.dtype)]
        self.scratch = [pltpu.SemaphoreType.DMA, pltpu.SemaphoreType.DMA]

    def bind(self, ins, outs, sems):
        x, y, c = _mesh_pos()
        cp = pltpu.make_async_remote_copy(src_ref=ins[0], dst_ref=outs[0], send_sem=sems[0], recv_sem=sems[1],
                                          device_id=(x, y, 1 - c), device_id_type=MESH)
        return cp.start, (lambda: None), cp.wait


class _ChipExchange:
    def __init__(self, part):
        self.blocks = [part]
        self.out_shape = [_sds((3,) + part.shape[1:], part.dtype)]
        self.scratch = [pltpu.SemaphoreType.DMA((3,)), pltpu.SemaphoreType.DMA((3,))]

    def bind(self, ins, outs, sems):
        x, y, c = _mesh_pos()
        chips = [(1 - x, y), (x, 1 - y), (1 - x, 1 - y)]
        cps = [
            pltpu.make_async_remote_copy(src_ref=ins[0].at[2 * cx + cy], dst_ref=outs[0].at[j], send_sem=sems[0].at[j],
                                         recv_sem=sems[1].at[j], device_id=(cx, cy, c), device_id_type=MESH)
            for j, (cx, cy) in enumerate(chips)
        ]

        def start():
            for cp in cps:
                cp.start()

        def finish():
            for cp in cps:
                cp.wait()

        return start, (lambda: None), finish


def _comm_lists(comm):
    if comm is None:
        return [], [], [], []
    n = len(comm.blocks)
    return list(comm.blocks), [HBM_SPEC] * n, list(comm.out_shape), list(comm.scratch)


def _split_refs(refs, n_pre, n_in, n_cin, n_out, n_cout, n_scr):
    cuts = [n_pre, n_in, n_cin, n_out, n_cout, n_scr]
    out, at = [], 0
    for c in cuts:
        out.append(refs[at:at + c])
        at += c
    out.append(refs[at:])
    return out


def _run_comm(comm, cin, cout, csem, step, n_steps, when="start"):
    if comm is None:
        return
    start, mid, finish = comm.bind(cin, cout, csem)
    if when == "start":
        pl.when(step == 0)(start)
    else:
        pl.when(step == max(n_steps - 2, 0))(mid)
        pl.when(step == n_steps - 1)(finish)


def _comm_only(comm, name):
    ins, in_specs, out_shape, scratch = _comm_lists(comm)
    n = len(ins)

    def body(*refs):
        start, mid, finish = comm.bind(refs[:n], refs[n:2 * n], refs[2 * n:])
        start()
        mid()
        finish()

    outs = _pcall(body, name=name, out_shape=tuple(out_shape), in_specs=in_specs, out_specs=tuple([HBM_SPEC] * n),
                  scratch_shapes=scratch, compiler_params=_cp())(*ins)
    return list(outs)


def _all_reduce_small(v):
    rows, cols = v.shape

    def body(v_ref, o_ref, slots, send_sems, recv_sems):
        x, y, c = _mesh_pos()
        me = 4 * x + 2 * y + c
        slots[me] = v_ref[...]
        cps = []
        for k in range(1, N_DEV):
            px, py, pc = x ^ (k >> 2), y ^ ((k >> 1) & 1), c ^ (k & 1)
            cps.append(pltpu.make_async_remote_copy(
                src_ref=v_ref, dst_ref=slots.at[me], send_sem=send_sems.at[k - 1], recv_sem=recv_sems.at[k - 1],
                device_id=(px, py, pc), device_id_type=MESH))
        for cp in cps:
            cp.start()
        for cp in cps:
            cp.wait()
        acc = slots[0]
        for k in range(1, N_DEV):
            acc = acc + slots[k]
        o_ref[...] = acc

    return _pcall(
        body, name="all_reduce_small", out_shape=_sds((rows, cols), F32),
        in_specs=[pl.BlockSpec(memory_space=pltpu.VMEM)], out_specs=pl.BlockSpec(memory_space=pltpu.VMEM),
        scratch_shapes=[pltpu.VMEM((N_DEV, rows, cols), F32), pltpu.SemaphoreType.DMA((7,)), pltpu.SemaphoreType.DMA((7,))],
        compiler_params=_cp(),
    )(v)


def _row_tile(rows, cap=1040):
    return max(tt for tt in range(16, cap + 1, 16) if rows % tt == 0)


def _add2(a, b, *, name):
    nq, rows, cols = a.shape
    tm = _row_tile(rows)

    def body(a_ref, b_ref, o_ref):
        o_ref[...] = (a_ref[...].astype(F32) + b_ref[...].astype(F32)).astype(BF16)

    spec = pl.BlockSpec((1, tm, cols), lambda q, i: (q, i, 0))
    return _pcall(body, name=name, out_shape=_sds(a.shape, BF16), grid=(nq, rows // tm),
                  in_specs=[spec, spec], out_specs=spec, compiler_params=_cp())(a, b)


def _add4(own, recv, *, name, rows_apart=False):
    rows, cols = own.shape
    tm = _row_tile(rows, 256 if rows_apart else 1040)

    def body(o_ref, r_ref, g_ref):
        tot = ((o_ref[...].astype(F32) + r_ref[0].astype(F32)) + r_ref[1].astype(F32)) + r_ref[2].astype(F32)
        if rows_apart:
            g_ref[:, 0, :] = tot
        else:
            g_ref[...] = tot

    if rows_apart:
        out_shape, out_spec = _sds((rows, 1, cols), F32), pl.BlockSpec((tm, 1, cols), lambda i: (i, 0, 0))
    else:
        out_shape, out_spec = _sds((rows, cols), F32), pl.BlockSpec((tm, cols), lambda i: (i, 0))
    return _pcall(
        body, name=name, out_shape=out_shape, grid=(rows // tm,),
        in_specs=[pl.BlockSpec((tm, cols), lambda i: (i, 0)), pl.BlockSpec((3, tm, cols), lambda i: (0, i, 0))],
        out_specs=out_spec, compiler_params=_cp(),
    )(own, recv)


def _mm(a, b, *, m, n, k, name, ta=False, tb=False, b_off=(0, 0), res=None, out_dtype=F32, tm=512, tn=1024, tk=1024,
        comm=None):
    tm, tn, tk = min(tm, m), min(tn, n), min(tk, k)
    assert m % tm == 0 and n % tn == 0 and k % tk == 0, (name, m, n, k, tm, tn, tk)
    nk = k // tk
    grid = (m // tm, n // tn, nk)
    bo0, bo1 = b_off
    if ta:
        a_spec = pl.BlockSpec((tk, tm), lambda i, j, kk: (kk, i))
    else:
        a_spec = pl.BlockSpec((tm, tk), lambda i, j, kk: (i, kk))
    if tb:
        b_spec = pl.BlockSpec((tn, tk), lambda i, j, kk: (j + bo0, kk + bo1))
    else:
        b_spec = pl.BlockSpec((tk, tn), lambda i, j, kk: (kk + bo0, j + bo1))
    o_spec = pl.BlockSpec((tm, tn), lambda i, j, kk: (i, j))
    dn = (((0 if ta else 1,), (1 if tb else 0,)), ((), ()))
    has_res = res is not None
    c_ins, c_in_specs, c_out_shape, c_scratch = _comm_lists(comm)
    n_in = 3 if has_res else 2
    n_scr = 1 if nk > 1 else 0

    def body(*refs):
        _, ins, cin, outs, cout, scr, csem = _split_refs(refs, 0, n_in, len(c_ins), 1, len(c_ins), n_scr)
        a_ref, b_ref = ins[0], ins[1]
        r_ref = ins[2] if has_res else None
        o_ref = outs[0]
        step = (pl.program_id(0) * grid[1] + pl.program_id(1)) * nk + pl.program_id(2)
        _run_comm(comm, cin, cout, csem, step, grid[0] * grid[1] * nk, "start")
        p = lax.dot_general(a_ref[...].astype(BF16), b_ref[...].astype(BF16), dn, preferred_element_type=F32)

        def finish(acc):
            if has_res:
                acc = acc + r_ref[...]
            o_ref[...] = acc.astype(out_dtype)

        if nk == 1:
            finish(p)
        else:
            acc_ref = scr[0]
            kk = pl.program_id(2)

            @pl.when(kk == 0)
            def _():
                acc_ref[...] = p

            @pl.when(kk > 0)
            def _():
                acc_ref[...] += p

            @pl.when(kk == nk - 1)
            def _():
                finish(acc_ref[...])

        _run_comm(comm, cin, cout, csem, step, grid[0] * grid[1] * nk, "end")

    ins = [a, b] + ([res] if has_res else [])
    in_specs = [a_spec, b_spec] + ([o_spec] if has_res else [])
    sem = ("arbitrary",) * 3 if comm is not None else ("parallel", "parallel", "arbitrary")
    outs = _pcall(
        body, name=name, out_shape=tuple([_sds((m, n), out_dtype)] + c_out_shape), grid=grid,
        in_specs=in_specs + c_in_specs, out_specs=tuple([o_spec] + [HBM_SPEC] * len(c_ins)),
        scratch_shapes=([pltpu.VMEM((tm, tn), F32)] if nk > 1 else []) + c_scratch,
        compiler_params=_cp(dimension_semantics=sem),
    )(*(ins + c_ins))
    return outs[0] if comm is None else (outs[0], list(outs[1:]))


def _rms_fwd(x, g, *, name, tm=512):
    t, d = x.shape
    tm = min(tm, t)

    def body(x_ref, g_ref, h_ref):
        xv = x_ref[...]
        r = lax.rsqrt(jnp.mean(xv * xv, axis=-1, keepdims=True) + EPS)
        h_ref[...] = (xv * r * g_ref[...]).astype(BF16)

    return _pcall(
        body, name=name, out_shape=_sds((t, d), BF16), grid=(t // tm,),
        in_specs=[pl.BlockSpec((tm, d), lambda i: (i, 0)), pl.BlockSpec((1, d), lambda i: (0, 0))],
        out_specs=pl.BlockSpec((tm, d), lambda i: (i, 0)), compiler_params=_cp(),
    )(x, g)


def _rms_bwd(dh, x, g, res, *, name, tm=512):
    t, d = x.shape
    tm = min(tm, t)
    has_res = res is not None

    def body(*refs):
        dh_ref, x_ref, g_ref = refs[:3]
        r_ref = refs[3] if has_res else None
        dx_ref, dg_ref = refs[-2], refs[-1]
        xv = x_ref[...]
        r = lax.rsqrt(jnp.mean(xv * xv, axis=-1, keepdims=True) + EPS)
        xh = xv * r
        dhv = dh_ref[...]
        gd = dhv * g_ref[...]
        dx = r * (gd - xh * jnp.mean(gd * xh, axis=-1, keepdims=True))
        if has_res:
            dx = dx + r_ref[...]
        dx_ref[...] = dx

        @pl.when(pl.program_id(0) == 0)
        def _():
            dg_ref[...] = jnp.zeros_like(dg_ref)

        dg_ref[...] += jnp.sum(dhv * xh, axis=0, keepdims=True)

    row = pl.BlockSpec((tm, d), lambda i: (i, 0))
    vec = pl.BlockSpec((1, d), lambda i: (0, 0))
    return _pcall(
        body, name=name, out_shape=(_sds((t, d), F32), _sds((1, d), F32)), grid=(t // tm,),
        in_specs=[row, row, vec] + ([row] if has_res else []), out_specs=(row, vec),
        compiler_params=_cp(dimension_semantics=("arbitrary",)),
    )(*([dh, x, g] + ([res] if has_res else [])))


def _conv_fwd(z, conv_w, conv_b, *, nb, s, d, tc=256):
    nc = d // tc

    def body(cb_ref, cc_ref, cv_ref, w_ref, b_ref, y_ref):
        u = cc_ref[...] * cv_ref[...]
        row = lax.broadcasted_iota(jnp.int32, u.shape, 0)
        u1 = jnp.where(row >= 1, pltpu.roll(u, 1, axis=0), 0.0)
        u2 = jnp.where(row >= 2, pltpu.roll(u, 2, axis=0), 0.0)
        w = w_ref[...]
        y = w[2:3, :] * u + w[1:2, :] * u1 + w[0:1, :] * u2 + b_ref[...]
        y_ref[...] = (cb_ref[...] * y).astype(BF16)

    def part(p):
        return pl.BlockSpec((s, tc), lambda b, j: (b, p * nc + j))

    return _pcall(
        body, name="conv_fwd", out_shape=_sds((nb * s, d), BF16), grid=(nb, nc),
        in_specs=[part(0), part(1), part(2), pl.BlockSpec((3, tc), lambda b, j: (0, j)), pl.BlockSpec((1, tc), lambda b, j: (0, j))],
        out_specs=pl.BlockSpec((s, tc), lambda b, j: (b, j)), compiler_params=_cp(),
    )(z, z, z, conv_w, conv_b)


def _conv_bwd(z, dy, conv_w, conv_b, dz, *, nb, s, d, tc=256):
    nc = d // tc

    def body(cb_ref, cc_ref, cv_ref, dy_ref, w_ref, b_ref, dz_in, dz_ref, dw_ref, db_ref, stash):
        b_i, p = pl.program_id(1), pl.program_id(2)

        @pl.when(p == 0)
        def _():
            cc, cv = cc_ref[...], cv_ref[...]
            u = cc * cv
            n = u.shape[0]
            row = lax.broadcasted_iota(jnp.int32, u.shape, 0)
            u1 = jnp.where(row >= 1, pltpu.roll(u, 1, axis=0), 0.0)
            u2 = jnp.where(row >= 2, pltpu.roll(u, 2, axis=0), 0.0)
            w = w_ref[...]
            y = w[2:3, :] * u + w[1:2, :] * u1 + w[0:1, :] * u2 + b_ref[...]
            dyv = dy_ref[...]
            dconv = dyv * cb_ref[...]
            g1 = jnp.where(row < n - 1, pltpu.roll(dconv, n - 1, axis=0), 0.0)
            g2 = jnp.where(row < n - 2, pltpu.roll(dconv, n - 2, axis=0), 0.0)
            du = w[2:3, :] * dconv + w[1:2, :] * g1 + w[0:1, :] * g2
            stash[0] = (dyv * y).astype(BF16)
            stash[1] = (du * cv).astype(BF16)
            stash[2] = (du * cc).astype(BF16)
            dws = [jnp.sum(dconv * u2, axis=0, keepdims=True), jnp.sum(dconv * u1, axis=0, keepdims=True),
                   jnp.sum(dconv * u, axis=0, keepdims=True)]
            db = jnp.sum(dconv, axis=0, keepdims=True)

            @pl.when(b_i == 0)
            def _():
                for r in range(3):
                    dw_ref[r:r + 1, :] = dws[r]
                db_ref[...] = db

            @pl.when(b_i > 0)
            def _():
                for r in range(3):
                    dw_ref[r:r + 1, :] += dws[r]
                db_ref[...] += db

        dz_ref[...] = stash[p]

    return _pcall(
        body, name="conv_bwd",
        out_shape=(_sds(dz.shape, dz.dtype), _sds((3, d), F32), _sds((1, d), F32)), grid=(nc, nb, 3),
        in_specs=[
            pl.BlockSpec((s, tc), lambda j, b, p: (b, j)), pl.BlockSpec((s, tc), lambda j, b, p: (b, nc + j)),
            pl.BlockSpec((s, tc), lambda j, b, p: (b, 2 * nc + j)), pl.BlockSpec((s, tc), lambda j, b, p: (b, j)),
            pl.BlockSpec((3, tc), lambda j, b, p: (0, j)), pl.BlockSpec((1, tc), lambda j, b, p: (0, j)),
            HBM_SPEC,
        ],
        out_specs=(pl.BlockSpec((s, tc), lambda j, b, p: (b, p * nc + j)),
                   pl.BlockSpec((3, tc), lambda j, b, p: (0, j)), pl.BlockSpec((1, tc), lambda j, b, p: (0, j))),
        scratch_shapes=[pltpu.VMEM((3, s, tc), BF16)],
        input_output_aliases={6: 0},
        compiler_params=_cp(dimension_semantics=("arbitrary", "arbitrary", "arbitrary")),
    )(z, z, z, dy, conv_w, conv_b, dz)


def _head_rms(xv, g):
    r = lax.rsqrt(jnp.mean(xv * xv, axis=-1, keepdims=True) + EPS)
    return xv * r * g


def _head_rms_bwd(xv, g, dy):
    r = lax.rsqrt(jnp.mean(xv * xv, axis=-1, keepdims=True) + EPS)
    xh = xv * r
    gd = dy * g
    dx = r * (gd - xh * jnp.mean(gd * xh, axis=-1, keepdims=True))
    return dx, jnp.sum(dy * xh, axis=0, keepdims=True)


def _fox_prep(z, q_g, k_g, *, t, d, tm=256):
    nh = d // HD

    def body(z_ref, qg_ref, kg_ref, q_ref, k_ref, v_ref):
        qg, kg = qg_ref[...], kg_ref[...]
        for h in range(nh):
            c0 = h * HD
            q_ref[:, c0:c0 + HD] = _head_rms(z_ref[:, c0:c0 + HD], qg).astype(BF16)
            k_ref[:, c0:c0 + HD] = _head_rms(z_ref[:, d + c0:d + c0 + HD], kg).astype(BF16)
        v_ref[...] = z_ref[:, 2 * d:3 * d].astype(BF16)

    o = pl.BlockSpec((tm, d), lambda i: (i, 0))
    g = pl.BlockSpec((1, HD), lambda i: (0, 0))
    return _pcall(
        body, name="fox_prep", out_shape=(_sds((t, d), BF16),) * 3, grid=(t // tm,),
        in_specs=[pl.BlockSpec((tm, 3 * d), lambda i: (i, 1)), g, g], out_specs=(o, o, o), compiler_params=_cp(),
    )(z, q_g, k_g)


def _lane_cumsum(v, reverse=False):
    n = v.shape[1]
    lane = lax.broadcasted_iota(jnp.int32, v.shape, 1)
    sh = 1
    while sh < n:
        if reverse:
            v = v + jnp.where(lane < n - sh, pltpu.roll(v, n - sh, axis=1), 0.0)
        else:
            v = v + jnp.where(lane >= sh, pltpu.roll(v, sh, axis=1), 0.0)
        sh *= 2
    return v


def _fox_gates(z, f_bias_col, *, nb, s, nh, ff_block):
    def body(z_ref, b_ref, c_ref):
        ff = z_ref[...].T[0:nh, :] + b_ref[...]
        lf = jnp.minimum(ff, 0.0) - jnp.log(1.0 + jnp.exp(-jnp.abs(ff)))
        c_ref[0] = _lane_cumsum(lf) * SQRT_HD

    return _pcall(
        body, name="fox_gates", out_shape=_sds((nb, nh, s), F32), grid=(nb,),
        in_specs=[pl.BlockSpec((s, HD), lambda b: (b, ff_block)), pl.BlockSpec((nh, 1), lambda b: (0, 0))],
        out_specs=pl.BlockSpec((1, nh, s), lambda b: (b, 0, 0)), compiler_params=_cp(),
    )(z, f_bias_col)


def _fox_gates_bwd(z, f_bias_col, row_sums, col_sums, dz, *, nb, s, nh, ff_block):
    def body(z_ref, b_ref, rs_ref, cs_ref, dz_in, dz_ref, db_ref, dc_sc):
        b_i, h = pl.program_id(0), pl.program_id(1)
        dc_sc[pl.ds(h, 1), :] = (rs_ref[...] - cs_ref[...]).T[0:1, :]

        @pl.when(h == nh - 1)
        def _():
            ff = z_ref[...].T[0:nh, :] + b_ref[...]
            dlf = _lane_cumsum(dc_sc[...], reverse=True)
            dff = dlf * (1.0 / (1.0 + jnp.exp(ff)))
            pad = jnp.concatenate([dff, jnp.zeros((HD - nh, s), F32)], axis=0)
            dz_ref[...] = pad.T.astype(BF16)
            dbv = jnp.sum(dff, axis=1, keepdims=True)

            @pl.when(b_i == 0)
            def _():
                db_ref[...] = dbv

            @pl.when(b_i > 0)
            def _():
                db_ref[...] += dbv

    hs = pl.BlockSpec((s, HD), lambda b, h: (b, h))
    zs = pl.BlockSpec((s, HD), lambda b, h: (b, ff_block))
    return _pcall(
        body, name="fox_gates_bwd", out_shape=(_sds(dz.shape, dz.dtype), _sds((nh, 1), F32)), grid=(nb, nh),
        in_specs=[zs, pl.BlockSpec((nh, 1), lambda b, h: (0, 0)), hs, hs, HBM_SPEC],
        out_specs=(zs, pl.BlockSpec((nh, 1), lambda b, h: (0, 0))),
        scratch_shapes=[pltpu.VMEM((nh, s), F32)],
        input_output_aliases={4: 0}, compiler_params=_cp(dimension_semantics=("arbitrary", "arbitrary")),
    )(z, f_bias_col, row_sums, col_sums, dz)


def _fox_pairs(nq, key_major):
    if key_major:
        pairs = [(i, j) for j in range(nq) for i in range(j, nq)]
    else:
        pairs = [(i, j) for i in range(nq) for j in range(i + 1)]
    return jnp.array([p[0] for p in pairs], jnp.int32), jnp.array([p[1] for p in pairs], jnp.int32)


def _with_ones(x):
    return jnp.concatenate([x, jnp.ones_like(x)], axis=1)


def _fox_specs(nb, nh, nq, tq, tk, hp):
    qs = pl.BlockSpec((tq, hp * HD), lambda b, h, p, it, jt: (b * nq + it[p], h))
    ks = pl.BlockSpec((tk, hp * HD), lambda b, h, p, it, jt: (b * nq + jt[p], h))
    cs = pl.BlockSpec((hp, 1, tk), lambda b, h, p, it, jt: (b * (nh // hp) + h, 0, jt[p]))
    return qs, ks, cs


def _fox_raw(qv, kv, ckv, diag, tq, tk):
    sc = lax.dot_general(qv, kv, (((1,), (1,)), ((), ())), preferred_element_type=F32) - ckv
    if diag:
        rows = lax.broadcasted_iota(jnp.int32, (tq, tk), 0)
        cols = lax.broadcasted_iota(jnp.int32, (tq, tk), 1)
        sc = jnp.where(rows >= cols, sc, NEG)
    return sc


def _fox_probs(qv, kv, ckv, lse_col, diag, tq, tk):
    sc = _fox_raw(qv, kv, ckv, diag, tq, tk)
    return jnp.exp2(sc * EXP2_C - lse_col * LOG2E)


def _fox_call(body, *, name, ins, in_specs, out_shape, out_specs, scratch, grid, tables, comm):
    c_ins, c_in_specs, c_out_shape, c_scratch = _comm_lists(comm)
    n_in, n_out, n_scr, nc = len(ins), len(out_shape), len(scratch), len(c_ins)
    n_steps = grid[0] * grid[1] * grid[2]

    def wrapped(*refs):
        pre, rin, cin, rout, cout, scr, csem = _split_refs(refs, 2, n_in, nc, n_out, nc, n_scr)
        step = (pl.program_id(0) * grid[1] + pl.program_id(1)) * grid[2] + pl.program_id(2)
        _run_comm(comm, cin, cout, csem, step, n_steps, "start")
        body(*pre, *rin, *rout, *scr)
        _run_comm(comm, cin, cout, csem, step, n_steps, "end")

    sem = ("arbitrary",) * 3 if comm is not None else ("parallel", "parallel", "arbitrary")
    outs = _pcall(
        wrapped, name=name, out_shape=tuple(list(out_shape) + c_out_shape),
        grid_spec=pltpu.PrefetchScalarGridSpec(
            num_scalar_prefetch=2, grid=grid, in_specs=list(in_specs) + c_in_specs,
            out_specs=tuple(list(out_specs) + [HBM_SPEC] * nc), scratch_shapes=list(scratch) + c_scratch),
        compiler_params=_cp(dimension_semantics=sem),
    )(*tables, *ins, *c_ins)
    return list(outs[:n_out]), list(outs[n_out:])


def _fox_fwd(q, k, v, cks, *, nb, s, nh, tq=512, comm=None):
    tk = tq
    nq = s // tq
    t = nb * s
    hp = FOX_HP
    it, jt = _fox_pairs(nq, key_major=False)

    def body(it_ref, jt_ref, q_ref, k_ref, v_ref, c_ref, o_ref, lse_ref, m_sc, acc_sc):
        p_id = pl.program_id(2)
        i, j = it_ref[p_id], jt_ref[p_id]

        @pl.when(j == 0)
        def _():
            m_sc[...] = jnp.full_like(m_sc, NEG)
            acc_sc[...] = jnp.zeros_like(acc_sc)

        def step(diag):
            for hh in range(hp):
                cols = slice(hh * HD, (hh + 1) * HD)
                sc = _fox_raw(q_ref[:, cols], k_ref[:, cols], c_ref[hh], diag, tq, tk)
                m_old = m_sc[hh]
                m_new = jnp.maximum(m_old, jnp.max(sc, axis=-1, keepdims=True))
                alpha = jnp.exp2((m_old - m_new) * EXP2_C)
                p = jnp.exp2((sc - m_new) * EXP2_C).astype(BF16)
                acc = alpha * acc_sc[hh] + jnp.dot(p, _with_ones(v_ref[:, cols]), preferred_element_type=F32)
                if diag:
                    l = acc[:, HD:2 * HD]
                    o_ref[:, cols] = (acc[:, 0:HD] / l).astype(BF16)
                    lse_ref[:, cols] = m_new * (1.0 / SQRT_HD) + jnp.log(l)
                else:
                    acc_sc[hh] = acc
                    m_sc[hh] = m_new

        @pl.when(j < i)
        def _():
            step(False)

        @pl.when(j == i)
        def _():
            step(True)

    qs, ks, cs = _fox_specs(nb, nh, nq, tq, tk, hp)
    (o, lse), c_out = _fox_call(
        body, name="fox_fwd", ins=[q, k, v, cks], in_specs=[qs, ks, ks, cs],
        out_shape=[_sds((t, nh * HD), BF16), _sds((t, nh * HD), F32)], out_specs=[qs, qs],
        scratch=[pltpu.VMEM((hp, tq, 1), F32), pltpu.VMEM((hp, tq, 2 * HD), F32)],
        grid=(nb, nh // hp, int(it.shape[0])), tables=(it, jt), comm=comm)
    return o, lse, c_out


def _fox_bwd_dq(q, k, v, cks, do, o, lse, *, nb, s, nh, tq=512, comm=None):
    tk = tq
    nq = s // tq
    t = nb * s
    hp = FOX_HP
    it, jt = _fox_pairs(nq, key_major=False)

    def body(it_ref, jt_ref, q_ref, k_ref, v_ref, c_ref, do_ref, o_ref, lse_ref, dq_ref, rs_ref, acc_sc, dl_sc):
        p_id = pl.program_id(2)
        i, j = it_ref[p_id], jt_ref[p_id]

        @pl.when(j == 0)
        def _():
            acc_sc[...] = jnp.zeros_like(acc_sc)
            for hh in range(hp):
                cols = slice(hh * HD, (hh + 1) * HD)
                dl_sc[hh] = jnp.sum(do_ref[:, cols] * o_ref[:, cols].astype(F32), axis=-1, keepdims=True)

        def step(diag):
            for hh in range(hp):
                cols = slice(hh * HD, (hh + 1) * HD)
                kv = k_ref[:, cols]
                p = _fox_probs(q_ref[:, cols], kv, c_ref[hh], lse_ref[:, hh * HD:hh * HD + 1], diag, tq, tk)
                dp = lax.dot_general(do_ref[:, cols].astype(BF16), v_ref[:, cols], (((1,), (1,)), ((), ())),
                                     preferred_element_type=F32)
                ds = (p * (dp - dl_sc[hh])).astype(BF16)
                acc = acc_sc[hh] + jnp.dot(ds, _with_ones(kv), preferred_element_type=F32)
                if diag:
                    dq_ref[:, cols] = acc[:, 0:HD] * (1.0 / SQRT_HD)
                    rs_ref[:, cols] = acc[:, HD:2 * HD]
                else:
                    acc_sc[hh] = acc

        @pl.when(j < i)
        def _():
            step(False)

        @pl.when(j == i)
        def _():
            step(True)

    qs, ks, cs = _fox_specs(nb, nh, nq, tq, tk, hp)
    (dq, rs), c_out = _fox_call(
        body, name="fox_bwd_dq", ins=[q, k, v, cks, do, o, lse], in_specs=[qs, ks, ks, cs, qs, qs, qs],
        out_shape=[_sds((t, nh * HD), F32), _sds((t, nh * HD), F32)], out_specs=[qs, qs],
        scratch=[pltpu.VMEM((hp, tq, 2 * HD), F32), pltpu.VMEM((hp, tq, 1), F32)],
        grid=(nb, nh // hp, int(it.shape[0])), tables=(it, jt), comm=comm)
    return dq, rs, c_out


def _fox_bwd_dkv(q, k, v, cks, do, o, lse, *, nb, s, nh, tq=512, comm=None):
    tk = tq
    nq = s // tq
    t = nb * s
    hp = FOX_HP
    it, jt = _fox_pairs(nq, key_major=True)

    def body(it_ref, jt_ref, q_ref, k_ref, v_ref, c_ref, do_ref, o_ref, lse_ref, dk_ref, dv_ref, cs_ref, dk_sc, dv_sc):
        p_id = pl.program_id(2)
        i, j = it_ref[p_id], jt_ref[p_id]

        def step(diag):
            for hh in range(hp):
                cols = slice(hh * HD, (hh + 1) * HD)
                qv = q_ref[:, cols]
                p = _fox_probs(qv, k_ref[:, cols], c_ref[hh], lse_ref[:, hh * HD:hh * HD + 1], diag, tq, tk)
                dov = do_ref[:, cols]
                dob = dov.astype(BF16)
                dvp = lax.dot_general(p.astype(BF16), dob, (((0,), (0,)), ((), ())), preferred_element_type=F32)
                dp = lax.dot_general(dob, v_ref[:, cols], (((1,), (1,)), ((), ())), preferred_element_type=F32)
                delta = jnp.sum(dov * o_ref[:, cols].astype(F32), axis=-1, keepdims=True)
                ds = (p * (dp - delta)).astype(BF16)
                dkp = lax.dot_general(ds, _with_ones(qv), (((0,), (0,)), ((), ())), preferred_element_type=F32)
                if diag:
                    dk_sc[hh] = dkp
                    dv_sc[hh] = dvp
                else:
                    dk_sc[hh] += dkp
                    dv_sc[hh] += dvp

        @pl.when(i == j)
        def _():
            step(True)

        @pl.when(i > j)
        def _():
            step(False)

        @pl.when(i == nq - 1)
        def _():
            for hh in range(hp):
                cols = slice(hh * HD, (hh + 1) * HD)
                dk_ref[:, cols] = dk_sc[hh, :, 0:HD] * (1.0 / SQRT_HD)
                cs_ref[:, cols] = dk_sc[hh, :, HD:2 * HD]
                dv_ref[:, cols] = dv_sc[hh].astype(BF16)

    qs, ks, cs = _fox_specs(nb, nh, nq, tq, tk, hp)
    (dk, dv, csum), c_out = _fox_call(
        body, name="fox_bwd_dkv", ins=[q, k, v, cks, do, o, lse], in_specs=[qs, ks, ks, cs, qs, qs, qs],
        out_shape=[_sds((t, nh * HD), F32), _sds((t, nh * HD), BF16), _sds((t, nh * HD), F32)], out_specs=[ks, ks, ks],
        scratch=[pltpu.VMEM((hp, tk, 2 * HD), F32), pltpu.VMEM((hp, tk, HD), F32)],
        grid=(nb, nh // hp, int(it.shape[0])), tables=(it, jt), comm=comm)
    return dk, dv, csum, c_out


def _fox_post(z, dqn, dkn, dv, q_g, k_g, dz, *, t, d, tm=256):
    nh = d // HD

    def body(z_ref, dq_ref, dk_ref, dv_ref, qg_ref, kg_ref, dz_in, dz_ref, dqg_ref, dkg_ref):
        qg, kg = qg_ref[...], kg_ref[...]
        aq = jnp.zeros((1, HD), F32)
        ak = jnp.zeros((1, HD), F32)
        for h in range(nh):
            c0 = h * HD
            dx, sg = _head_rms_bwd(z_ref[:, c0:c0 + HD], qg, dq_ref[:, c0:c0 + HD])
            dz_ref[:, c0:c0 + HD] = dx.astype(BF16)
            aq = aq + sg
            dx, sg = _head_rms_bwd(z_ref[:, d + c0:d + c0 + HD], kg, dk_ref[:, c0:c0 + HD])
            dz_ref[:, d + c0:d + c0 + HD] = dx.astype(BF16)
            ak = ak + sg
        dz_ref[:, 2 * d:3 * d] = dv_ref[...]

        @pl.when(pl.program_id(0) == 0)
        def _():
            dqg_ref[...] = aq
            dkg_ref[...] = ak

        @pl.when(pl.program_id(0) > 0)
        def _():
            dqg_ref[...] += aq
            dkg_ref[...] += ak

    o = pl.BlockSpec((tm, d), lambda i: (i, 0))
    g = pl.BlockSpec((1, HD), lambda i: (0, 0))
    z3 = pl.BlockSpec((tm, 3 * d), lambda i: (i, 1))
    return _pcall(
        body, name="fox_post", out_shape=(_sds(dz.shape, dz.dtype), _sds((1, HD), F32), _sds((1, HD), F32)), grid=(t // tm,),
        in_specs=[z3, o, o, o, g, g, HBM_SPEC], out_specs=(z3, g, g),
        input_output_aliases={6: 0}, compiler_params=_cp(dimension_semantics=("arbitrary",)),
    )(z, dqn, dkn, dv, q_g, k_g, dz)


def _xa_prep_kv(kv, k_g, *, rows, d):
    xd = d // XH

    def body(kv_ref, g_ref, k_ref, v_ref):
        g = g_ref[...]
        for h in range(XH):
            c0 = h * xd
            k_ref[:, c0:c0 + xd] = _head_rms(kv_ref[:, c0:c0 + xd], g).astype(BF16)
        v_ref[...] = kv_ref[:, d:2 * d].astype(BF16)

    return _pcall(body, name="xa_prep_kv", out_shape=(_sds((rows, d), BF16),) * 2, compiler_params=_cp())(kv, k_g)


def _xa_fwd(z, kn, vb, q_g, *, nb, s, d, nm, q_block, tq=512):
    xd = d // XH
    nq = s // tq

    def body(z_ref, k_ref, v_ref, g_ref, y_ref):
        g = g_ref[...]
        for h in range(XH):
            c0 = h * xd
            qn = _head_rms(z_ref[:, c0:c0 + xd], g).astype(BF16)
            sc = lax.dot_general(qn, k_ref[:, c0:c0 + xd], (((1,), (1,)), ((), ())), preferred_element_type=F32)
            sc = sc / math.sqrt(xd)
            p = jnp.exp(sc - jnp.max(sc, axis=-1, keepdims=True))
            p = p / jnp.sum(p, axis=-1, keepdims=True)
            y_ref[:, c0:c0 + xd] = jnp.dot(p.astype(BF16), v_ref[:, c0:c0 + xd], preferred_element_type=F32).astype(BF16)

    kvs = pl.BlockSpec((nm, d), lambda b, i: (b, 0))
    return _pcall(
        body, name="xa_fwd", out_shape=_sds((nb * s, d), BF16), grid=(nb, nq),
        in_specs=[pl.BlockSpec((tq, d), lambda b, i: (b * nq + i, q_block)), kvs, kvs, pl.BlockSpec((1, xd), lambda b, i: (0, 0))],
        out_specs=pl.BlockSpec((tq, d), lambda b, i: (b * nq + i, 0)), compiler_params=_cp(),
    )(z, kn, vb, q_g)


def _xa_bwd(z, kn, vb, q_g, dy, dz, *, nb, s, d, nm, q_block, tq=512):
    xd = d // XH
    nq = s // tq

    def body(z_ref, k_ref, v_ref, g_ref, dy_ref, dz_in, dz_ref, dk_ref, dv_ref, dg_ref):
        b_i, i = pl.program_id(0), pl.program_id(1)
        g = g_ref[...]

        @pl.when(i == 0)
        def _():
            dk_ref[...] = jnp.zeros_like(dk_ref)
            dv_ref[...] = jnp.zeros_like(dv_ref)

        @pl.when((i == 0) & (b_i == 0))
        def _():
            dg_ref[...] = jnp.zeros_like(dg_ref)

        for h in range(XH):
            c0 = h * xd
            xq = z_ref[:, c0:c0 + xd]
            qn = _head_rms(xq, g).astype(BF16)
            kh, vh = k_ref[:, c0:c0 + xd], v_ref[:, c0:c0 + xd]
            sc = lax.dot_general(qn, kh, (((1,), (1,)), ((), ())), preferred_element_type=F32) / math.sqrt(xd)
            p = jnp.exp(sc - jnp.max(sc, axis=-1, keepdims=True))
            p = p / jnp.sum(p, axis=-1, keepdims=True)
            dob = dy_ref[:, c0:c0 + xd].astype(BF16)
            dv_ref[:, c0:c0 + xd] += lax.dot_general(p.astype(BF16), dob, (((0,), (0,)), ((), ())), preferred_element_type=F32)
            dp = lax.dot_general(dob, vh, (((1,), (1,)), ((), ())), preferred_element_type=F32)
            ds = p * (dp - jnp.sum(p * dp, axis=-1, keepdims=True)) / math.sqrt(xd)
            dsb = ds.astype(BF16)
            dqn = jnp.dot(dsb, kh, preferred_element_type=F32)
            dk_ref[:, c0:c0 + xd] += lax.dot_general(dsb, qn, (((0,), (0,)), ((), ())), preferred_element_type=F32)
            dx, sg = _head_rms_bwd(xq, g, dqn)
            dz_ref[:, c0:c0 + xd] = dx.astype(BF16)
            dg_ref[...] += sg

    kvs = pl.BlockSpec((nm, d), lambda b, i: (b, 0))
    qs = pl.BlockSpec((tq, d), lambda b, i: (b * nq + i, q_block))
    gs = pl.BlockSpec((1, xd), lambda b, i: (0, 0))
    return _pcall(
        body, name="xa_bwd",
        out_shape=(_sds(dz.shape, dz.dtype), _sds((nb * nm, d), F32), _sds((nb * nm, d), F32), _sds((1, xd), F32)), grid=(nb, nq),
        in_specs=[qs, kvs, kvs, gs, pl.BlockSpec((tq, d), lambda b, i: (b * nq + i, 0)), HBM_SPEC],
        out_specs=(qs, kvs, kvs, gs), input_output_aliases={5: 0},
        compiler_params=_cp(dimension_semantics=("arbitrary", "arbitrary")),
    )(z, kn, vb, q_g, dy, dz)


def _xa_post_kv(kv, dkn, dv, k_g, *, rows, d):
    xd = d // XH

    def body(kv_ref, dk_ref, dv_ref, g_ref, dkv_ref, dg_ref):
        g = g_ref[...]
        acc = jnp.zeros((1, xd), F32)
        for h in range(XH):
            c0 = h * xd
            dx, sg = _head_rms_bwd(kv_ref[:, c0:c0 + xd], g, dk_ref[:, c0:c0 + xd])
            dkv_ref[:, c0:c0 + xd] = dx.astype(BF16)
            acc = acc + sg
        dkv_ref[:, d:2 * d] = dv_ref[...].astype(BF16)
        dg_ref[...] = acc

    return _pcall(body, name="xa_post_kv", out_shape=(_sds((rows, 2 * d), BF16), _sds((1, xd), F32)), compiler_params=_cp())(
        kv, dkn, dv, k_g)


def _sigmoid(v):
    return 1.0 / (1.0 + jnp.exp(-v))


def _merge_fwd(z, pa, pb, pc, *, t, d, gate_block, tm=512):
    def body(ga_ref, gb_ref, gc_ref, pa_ref, pb_ref, pc_ref, o_ref):
        o_ref[...] = (_sigmoid(ga_ref[...]) * pa_ref[...] + _sigmoid(gb_ref[...]) * pb_ref[...]
                      + _sigmoid(gc_ref[...]) * pc_ref[...]).astype(BF16)

    def gate(q):
        return pl.BlockSpec((tm, d), lambda i: (i, gate_block + q))

    o = pl.BlockSpec((tm, d), lambda i: (i, 0))
    return _pcall(
        body, name="merge_fwd", out_shape=_sds((t, d), BF16), grid=(t // tm,),
        in_specs=[gate(0), gate(1), gate(2), o, o, o], out_specs=o, compiler_params=_cp(),
    )(z, z, z, pa, pb, pc)


def _gate_bwd(z, dm, pr, dz, *, t, d, col_block, name, tm=512):
    def body(g_ref, dm_ref, p_ref, dz_in, dz_ref, dp_ref):
        sg = _sigmoid(g_ref[...])
        dmv = dm_ref[...]
        dp_ref[...] = (dmv * sg).astype(BF16)
        dz_ref[...] = (dmv * p_ref[...] * sg * (1.0 - sg)).astype(BF16)

    gs = pl.BlockSpec((tm, d), lambda i: (i, col_block))
    o = pl.BlockSpec((tm, d), lambda i: (i, 0))
    return _pcall(
        body, name=name, out_shape=(_sds(dz.shape, dz.dtype), _sds((t, d), BF16)), grid=(t // tm,),
        in_specs=[gs, o, o, HBM_SPEC], out_specs=(gs, o), input_output_aliases={3: 0},
        compiler_params=_cp(),
    )(z, dm, pr, dz)


def _swiglu_fwd(gu, *, t, dff, tm=256):
    def body(g_ref, u_ref, a_ref):
        gv = g_ref[...]
        a_ref[...] = (gv * _sigmoid(gv) * u_ref[...]).astype(BF16)

    return _pcall(
        body, name="swiglu_fwd", out_shape=_sds((t, dff), BF16), grid=(t // tm,),
        in_specs=[pl.BlockSpec((tm, dff), lambda i: (i, 0)), pl.BlockSpec((tm, dff), lambda i: (i, 1))],
        out_specs=pl.BlockSpec((tm, dff), lambda i: (i, 0)), compiler_params=_cp(),
    )(gu, gu)


def _swiglu_bwd(gu, da, *, t, dff, tm=256):
    def body(g_ref, u_ref, da_ref, dg_ref, du_ref):
        gv, dav = g_ref[...], da_ref[...]
        sg = _sigmoid(gv)
        dg_ref[...] = (dav * u_ref[...] * sg * (1.0 + gv * (1.0 - sg))).astype(BF16)
        du_ref[...] = (dav * gv * sg).astype(BF16)

    h0 = pl.BlockSpec((tm, dff), lambda i: (i, 0))
    h1 = pl.BlockSpec((tm, dff), lambda i: (i, 1))
    return _pcall(
        body, name="swiglu_bwd", out_shape=(_sds((t, dff), BF16), _sds((t, dff), BF16)), grid=(t // tm,),
        in_specs=[h0, h1, h0], out_specs=(h0, h0), compiler_params=_cp(),
    )(gu, gu, da)


def _loss_head(y, target, *, t, d, tm=512):
    def body(y_ref, t_ref, dy_ref, l_ref):
        e = y_ref[...] - t_ref[...]
        dy_ref[...] = e * (1.0 / d)

        @pl.when(pl.program_id(0) == 0)
        def _():
            l_ref[...] = jnp.zeros_like(l_ref)

        l_ref[...] += jnp.sum(jnp.sum(e * e, axis=-1, keepdims=True), axis=0, keepdims=True) * (0.5 / d)

    o = pl.BlockSpec((tm, d), lambda i: (i, 0))
    dy, l = _pcall(
        body, name="loss_head", out_shape=(_sds((t, d), F32), _sds((1, HD), F32)), grid=(t // tm,),
        in_specs=[o, o], out_specs=(o, pl.BlockSpec((1, HD), lambda i: (0, 0))),
        compiler_params=_cp(dimension_semantics=("arbitrary",)),
    )(y, target)
    return dy, l[0, 0]


def _adamw(w, g, m, v, *, name):
    if w.ndim == 3:
        rows, _, cols = w.shape
        tm = max(tt for tt in range(1, 65) if rows % tt == 0)
        block, imap, grid = (tm, 1, cols), (lambda i, j: (i, 0, 0)), (rows // tm, 1)
    else:
        rows, cols = w.shape
        tm, tn = rows, cols
        for cand in (256, 128, 64, 32, 16, 8):
            if rows > cand and rows % cand == 0:
                tm = cand
                break
        if tm == rows and rows > 256 and cols % 128 == 0:
            tn = 128
        block, imap, grid = (tm, tn), (lambda i, j: (i, j)), (rows // tm, cols // tn)
    c1 = 1.0 - ADAM_B1 ** ADAM_STEP
    c2 = 1.0 - ADAM_B2 ** ADAM_STEP

    def body(w_ref, g_ref, m_ref, v_ref, d_ref, nm_ref, nv_ref):
        gv = g_ref[...]
        mn = ADAM_B1 * m_ref[...] + (1.0 - ADAM_B1) * gv
        vn = ADAM_B2 * v_ref[...] + (1.0 - ADAM_B2) * (gv * gv)
        d_ref[...] = -ADAM_LR * ((mn / c1) / (jnp.sqrt(vn / c2) + ADAM_EPS) + ADAM_WD * w_ref[...])
        nm_ref[...] = mn
        nv_ref[...] = vn

    spec = pl.BlockSpec(block, imap)
    return _pcall(
        body, name=name, out_shape=(_sds(w.shape, F32),) * 3, grid=grid,
        in_specs=[spec] * 4, out_specs=(spec,) * 3, compiler_params=_cp(),
    )(w, g, m, v)


def _pad_rows(a, rows):
    return a if a.shape[0] == rows else jnp.pad(a, ((0, rows - a.shape[0]),) + ((0, 0),) * (a.ndim - 1))


class _Pack:
    def __init__(self, row_sizes):
        self.rows = list(row_sizes)
        self.padded = [-(-r // 16) * 16 for r in self.rows]
        self.offs = [sum(self.padded[:i]) for i in range(len(self.rows))]
        self.total = sum(self.padded)

    def stack(self, blocks):
        return jnp.concatenate([_pad_rows(b.astype(BF16), p) for b, p in zip(blocks, self.padded)], axis=0)

    def full(self, gathered, i):
        r = self.rows[i]
        return gathered[:, self.offs[i]:self.offs[i] + r, :].reshape(N_DEV * r, gathered.shape[2])

    def for_core(self, full_grads, core):
        parts = []
        for gfull, r, p in zip(full_grads, self.rows, self.padded):
            blk = gfull.reshape(N_DEV // 2, 2, r, gfull.shape[1])
            blk = lax.dynamic_index_in_dim(blk, core, axis=1, keepdims=False).astype(BF16)
            if p != r:
                blk = jnp.pad(blk, ((0, 0), (0, p - r), (0, 0)))
            parts.append(blk)
        return jnp.concatenate(parts, axis=1)

    def shard(self, g_pack, i):
        return g_pack[self.offs[i]:self.offs[i] + self.rows[i], :]


def kernel(x, mem, norm1_g, w_in, conv_w, conv_b, fox_f_bias, fox_q_g, fox_k_g, mem_norm_g, w_mem_kv, xa_q_g, xa_k_g, w_br_conv, w_br_fox, w_br_xa, w_o, norm2_g, w_ffn_in, w_ffn_out, loss_target, m_norm1_g, m_w_in, m_conv_w, m_conv_b, m_fox_f_bias, m_fox_q_g, m_fox_k_g, m_mem_norm_g, m_w_mem_kv, m_xa_q_g, m_xa_k_g, m_w_br_conv, m_w_br_fox, m_w_br_xa, m_w_o, m_norm2_g, m_w_ffn_in, m_w_ffn_out, v_norm1_g, v_w_in, v_conv_w, v_conv_b, v_fox_f_bias, v_fox_q_g, v_fox_k_g, v_mem_norm_g, v_w_mem_kv, v_xa_q_g, v_xa_k_g, v_w_br_conv, v_w_br_fox, v_w_br_xa, v_w_o, v_norm2_g, v_w_ffn_in, v_w_ffn_out):
    nb, s, d = x.shape
    t = nb * s
    nm = mem.shape[1]
    nh = d // HD
    dff = w_ffn_out.shape[1] * N_DEV
    n_in = w_in.shape[2] * N_DEV
    n_in_pad = -(-n_in // 1152) * 1152
    ff_block = (10 * d) // HD
    my_c = lax.axis_index("c")
    my_q = 2 * lax.axis_index("x") + lax.axis_index("y")

    pack_a = _Pack([w_in.shape[2]])
    pack_b = _Pack([w_mem_kv.shape[2], w_br_conv.shape[1], w_br_fox.shape[1], w_br_xa.shape[1], w_o.shape[1],
                    w_ffn_in.shape[2], w_ffn_out.shape[1]])
    gather_1 = _Pack([w_ffn_in.shape[2], w_ffn_out.shape[1]])
    gather_2 = _Pack([w_mem_kv.shape[2], w_br_conv.shape[1], w_br_fox.shape[1], w_br_xa.shape[1], w_o.shape[1]])
    stack_a = pack_a.stack([w_in[0].T])
    stack_1 = gather_1.stack([w_ffn_in[0].T, w_ffn_out[0]])
    stack_2 = gather_2.stack([w_mem_kv[0].T, w_br_conv[0], w_br_fox[0], w_br_xa[0], w_o[0]])
    cw_block = _pad_rows(conv_w[0], 8)
    (gathered_a,) = _comm_only(_AllGather([stack_a]), "all_gather_w_in")
    win_t = _pad_rows(pack_a.full(gathered_a, 0), n_in_pad)

    x2 = x.reshape(t, d)
    mem2 = mem.reshape(nb * nm, d)
    tgt2 = loss_target.reshape(t, d)
    h1 = _rms_fwd(x2, norm1_g, name="rms1_fwd")
    z, (gathered_1,) = _mm(h1, win_t, m=t, n=n_in_pad, k=d, tb=True, tm=1024, tn=1152, name="mm_z", comm=_AllGather([stack_1]))
    qn, kn, vb = _fox_prep(z, fox_q_g, fox_k_g, t=t, d=d)
    f_bias_col = fox_f_bias.reshape(nh, 1)
    ck_rows = _fox_gates(z, f_bias_col, nb=nb, s=s, nh=nh, ff_block=ff_block).reshape(nb * nh, 1, s)
    y_fox, lse, (gathered_2, cw_g) = _fox_fwd(qn, kn, vb, ck_rows, nb=nb, s=s, nh=nh, comm=_AllGather([stack_2, cw_block]))
    wfi_t, wfo = (gather_1.full(gathered_1, i) for i in range(2))
    wkv_t, wbc, wbf, wbx, wo = (gather_2.full(gathered_2, i) for i in range(5))
    conv_w_full = jnp.transpose(cw_g[:, 0:3, :], (1, 0, 2)).reshape(3, d)

    y_conv = _conv_fwd(z, conv_w_full, conv_b, nb=nb, s=s, d=d)
    mem_n = _rms_fwd(mem2, mem_norm_g, name="rms_mem_fwd")
    kv = _mm(mem_n, wkv_t, m=nb * nm, n=2 * d, k=d, tb=True, name="mm_kv")
    xkn, xvb = _xa_prep_kv(kv, xa_k_g, rows=nb * nm, d=d)
    y_xa = _xa_fwd(z, xkn, xvb, xa_q_g, nb=nb, s=s, d=d, nm=nm, q_block=6)
    pa = _mm(y_conv, wbc, m=t, n=d, k=d, name="mm_pa")
    pb = _mm(y_fox, wbf, m=t, n=d, k=d, name="mm_pb")
    pc = _mm(y_xa, wbx, m=t, n=d, k=d, name="mm_pc")
    merged = _merge_fwd(z, pa, pb, pc, t=t, d=d, gate_block=7)
    x1 = _mm(merged, wo, m=t, n=d, k=d, res=x2, name="mm_x1")
    h2 = _rms_fwd(x1, norm2_g, name="rms2_fwd")
    gu = _mm(h2, wfi_t, m=t, n=2 * dff, k=d, tb=True, tm=1024, tn=1408, name="mm_gu")
    act = _swiglu_fwd(gu, t=t, dff=dff)
    y = _mm(act, wfo, m=t, n=d, k=dff, res=x1, tk=1408, name="mm_y")
    dy, loss_local = _loss_head(y, tgt2, t=t, d=d)

    da = _mm(dy, wfo, m=t, n=dff, k=d, tb=True, tn=1408, name="mm_da")
    g_wfo = _mm(act, dy, m=dff, n=d, k=t, ta=True, tm=1408, tk=512, out_dtype=BF16, name="mm_g_wfo")
    dgate, dup = _swiglu_bwd(gu, da, t=t, dff=dff)
    dh2 = _mm(dgate, wfi_t, m=t, n=d, k=dff, tk=1408, name="mm_dh2_g")
    dh2 = _mm(dup, wfi_t, m=t, n=d, k=dff, tk=1408, b_off=(2, 0), res=dh2, name="mm_dh2_u")
    g_wfi_g = _mm(dgate, h2, m=dff, n=d, k=t, ta=True, tm=1408, tk=512, out_dtype=BF16, name="mm_g_wfi_g")
    g_wfi_u = _mm(dup, h2, m=dff, n=d, k=t, ta=True, tm=1408, tk=512, out_dtype=BF16, name="mm_g_wfi_u")
    dx1, g_norm2 = _rms_bwd(dh2, x1, norm2_g, dy, name="rms2_bwd")
    dmerged = _mm(dx1, wo, m=t, n=d, k=d, tb=True, name="mm_dmerged")
    g_wo = _mm(merged, dx1, m=d, n=d, k=t, ta=True, tk=512, out_dtype=BF16, name="mm_g_wo")

    dz = lax.empty((t, n_in_pad), BF16)
    dz, dpa = _gate_bwd(z, dmerged, pa, dz, t=t, d=d, col_block=7, name="gate_bwd_a")
    dz, dpb = _gate_bwd(z, dmerged, pb, dz, t=t, d=d, col_block=8, name="gate_bwd_b")
    dz, dpc = _gate_bwd(z, dmerged, pc, dz, t=t, d=d, col_block=9, name="gate_bwd_c")
    dy_conv = _mm(dpa, wbc, m=t, n=d, k=d, tb=True, name="mm_dy_conv")
    g_wbc = _mm(y_conv, dpa, m=d, n=d, k=t, ta=True, tk=512, out_dtype=BF16, name="mm_g_wbc")
    dy_fox = _mm(dpb, wbf, m=t, n=d, k=d, tb=True, name="mm_dy_fox")
    g_wbf = _mm(y_fox, dpb, m=d, n=d, k=t, ta=True, tk=512, out_dtype=BF16, name="mm_g_wbf")
    dy_xa = _mm(dpc, wbx, m=t, n=d, k=d, tb=True, name="mm_dy_xa")
    g_wbx = _mm(y_xa, dpc, m=d, n=d, k=t, ta=True, tk=512, out_dtype=BF16, name="mm_g_wbx")

    dz, g_conv_w, g_conv_b = _conv_bwd(z, dy_conv, conv_w_full, conv_b, dz, nb=nb, s=s, d=d)
    dz, dxkn, dxv, g_xa_q = _xa_bwd(z, xkn, xvb, xa_q_g, dy_xa, dz, nb=nb, s=s, d=d, nm=nm, q_block=6)
    dkv, g_xa_k = _xa_post_kv(kv, dxkn, dxv, xa_k_g, rows=nb * nm, d=d)
    g_wkv_t = _mm(dkv, mem_n, m=2 * d, n=d, k=nb * nm, ta=True, out_dtype=BF16, name="mm_g_wkv")
    dmem_n = _mm(dkv, wkv_t, m=nb * nm, n=d, k=2 * d, name="mm_dmem")
    _, g_mem_norm = _rms_bwd(dmem_n, mem2, mem_norm_g, None, name="rms_mem_bwd")

    grads_b = [g_wkv_t, g_wbc, g_wbf, g_wbx, g_wo, jnp.concatenate([g_wfi_g, g_wfi_u], axis=0), g_wfo]
    keep_b, send_b = pack_b.for_core(grads_b, my_c), pack_b.for_core(grads_b, 1 - my_c)
    dqn, ds_rows, (from_sibling_b,) = _fox_bwd_dq(qn, kn, vb, ck_rows, dy_fox, y_fox, lse, nb=nb, s=s, nh=nh,
                                                  comm=_SiblingSwap(send_b))
    part_b = _add2(keep_b, from_sibling_b, name="rs_add_sibling_b")
    dkn, dvb, ds_cols, (from_chips_b,) = _fox_bwd_dkv(qn, kn, vb, ck_rows, dy_fox, y_fox, lse, nb=nb, s=s, nh=nh,
                                                     comm=_ChipExchange(part_b))
    g_pack_b = _add4(lax.dynamic_index_in_dim(part_b, my_q, axis=0, keepdims=False), from_chips_b, name="rs_add_chips_b")
    dz, g_fox_q, g_fox_k = _fox_post(z, dqn, dkn, dvb, fox_q_g, fox_k_g, dz, t=t, d=d)
    dz, g_f_bias = _fox_gates_bwd(z, f_bias_col, ds_rows, ds_cols, dz, nb=nb, s=s, nh=nh, ff_block=ff_block)

    g_win_t = _mm(dz, h1, m=n_in_pad, n=d, k=t, ta=True, tm=1152, tk=512, out_dtype=BF16, name="mm_g_win")
    grads_a = [g_win_t[:n_in]]
    keep_a, send_a = pack_a.for_core(grads_a, my_c), pack_a.for_core(grads_a, 1 - my_c)
    dh1, (from_sibling_a,) = _mm(dz, win_t, m=t, n=d, k=n_in_pad, tk=1152, name="mm_dh1", comm=_SiblingSwap(send_a))
    part_a = _add2(keep_a, from_sibling_a, name="rs_add_sibling_a")
    (from_chips_a,) = _comm_only(_ChipExchange(part_a), "rs_chips_a")
    g_pack_a = _add4(lax.dynamic_index_in_dim(part_a, my_q, axis=0, keepdims=False), from_chips_a, name="rs_add_chips_a",
                     rows_apart=True)
    dx, g_norm1 = _rms_bwd(dh1, x2, norm1_g, dx1, name="rms1_bwd")

    xd = d // XH
    tail = jnp.concatenate(
        [_pad_rows(g_f_bias.reshape(nh), HD).reshape(1, HD), g_fox_q, g_fox_k, g_xa_q, g_xa_k,
         jnp.zeros((1, d - 3 * HD - 2 * xd), F32)], axis=1)
    small = jnp.concatenate([g_norm1, g_norm2, g_conv_w, g_conv_b, g_mem_norm, tail], axis=0)
    small = _all_reduce_small(small)
    gs_conv_w = lax.dynamic_slice_in_dim(small[2:5], (2 * my_q + my_c) * (d // N_DEV), d // N_DEV, axis=1)

    grads = {
        "norm1_g": small[0:1], "w_in": g_pack_a, "conv_w": gs_conv_w, "conv_b": small[5:6],
        "fox_f_bias": small[7:8, 0:nh], "fox_q_g": small[7:8, HD:2 * HD], "fox_k_g": small[7:8, 2 * HD:3 * HD],
        "mem_norm_g": small[6:7], "w_mem_kv": pack_b.shard(g_pack_b, 0), "xa_q_g": small[7:8, 3 * HD:3 * HD + xd],
        "xa_k_g": small[7:8, 3 * HD + xd:3 * HD + 2 * xd], "w_br_conv": pack_b.shard(g_pack_b, 1),
        "w_br_fox": pack_b.shard(g_pack_b, 2), "w_br_xa": pack_b.shard(g_pack_b, 3), "w_o": pack_b.shard(g_pack_b, 4),
        "norm2_g": small[1:2], "w_ffn_in": pack_b.shard(g_pack_b, 5), "w_ffn_out": pack_b.shard(g_pack_b, 6),
    }
    transposed = {"w_in", "w_mem_kv", "w_ffn_in"}
    weights = {
        "norm1_g": (norm1_g, m_norm1_g, v_norm1_g), "w_in": (w_in, m_w_in, v_w_in), "conv_w": (conv_w, m_conv_w, v_conv_w),
        "conv_b": (conv_b, m_conv_b, v_conv_b), "fox_f_bias": (fox_f_bias, m_fox_f_bias, v_fox_f_bias),
        "fox_q_g": (fox_q_g, m_fox_q_g, v_fox_q_g), "fox_k_g": (fox_k_g, m_fox_k_g, v_fox_k_g),
        "mem_norm_g": (mem_norm_g, m_mem_norm_g, v_mem_norm_g), "w_mem_kv": (w_mem_kv, m_w_mem_kv, v_w_mem_kv),
        "xa_q_g": (xa_q_g, m_xa_q_g, v_xa_q_g), "xa_k_g": (xa_k_g, m_xa_k_g, v_xa_k_g),
        "w_br_conv": (w_br_conv, m_w_br_conv, v_w_br_conv), "w_br_fox": (w_br_fox, m_w_br_fox, v_w_br_fox),
        "w_br_xa": (w_br_xa, m_w_br_xa, v_w_br_xa), "w_o": (w_o, m_w_o, v_w_o), "norm2_g": (norm2_g, m_norm2_g, v_norm2_g),
        "w_ffn_in": (w_ffn_in, m_w_ffn_in, v_w_ffn_in), "w_ffn_out": (w_ffn_out, m_w_ffn_out, v_w_ffn_out),
    }
    out_g, out_d, out_m, out_v = [], [], [], []
    for name, (w, m, v) in weights.items():
        shape = w.shape
        if name == "w_in":
            w3, m3, v3 = (a.transpose(2, 0, 1) for a in (w, m, v))
            dl, mn, vn = _adamw(w3, g_pack_a, m3, v3, name="adamw_" + name)
            out_g.append(g_pack_a[:shape[2]].transpose(1, 2, 0))
            out_d.append(dl.transpose(1, 2, 0))
            out_m.append(mn.transpose(1, 2, 0))
            out_v.append(vn.transpose(1, 2, 0))
            continue
        tr = name in transposed

        def as2d(a):
            a = a.reshape(shape[-2], shape[-1])
            return a.T if tr else a

        def back(a):
            return (a.T if tr else a).reshape(shape)

        w2 = as2d(w)
        g2 = grads[name].reshape(w2.shape)
        dl, mn, vn = _adamw(w2, g2, as2d(m), as2d(v), name="adamw_" + name)
        out_g.append(back(g2))
        out_d.append(back(dl))
        out_m.append(back(mn))
        out_v.append(back(vn))

    loss = lax.psum(loss_local, ("x", "y", "c"))
    return (loss, dx.reshape(nb, s, d), *out_g, *out_d, *out_m, *out_v)
```

```python
import math

import jax
import jax.numpy as jnp
from jax import lax
from jax.experimental import pallas as pl
from jax.experimental.pallas import tpu as pltpu

F32, BF16 = jnp.float32, jnp.bfloat16
EPS = 1e-6
HD = 128
XH = 4
N_DEV = 8
MESH = pl.DeviceIdType.MESH
VMEM_LIMIT = 52 * 1024 * 1024
NEG = -1e30
SQRT_HD = math.sqrt(HD)
EXP2_C = math.log2(math.e) / SQRT_HD
LOG2E = math.log2(math.e)
FOX_HP = 4

ADAM_LR, ADAM_B1, ADAM_B2, ADAM_EPS, ADAM_WD, ADAM_STEP = 0.001, 0.9, 0.999, 1e-08, 0.01, 10


def _cp(**kw):
    return pltpu.CompilerParams(vmem_limit_bytes=VMEM_LIMIT, **kw)


def _pcall(body, **kw):
    return pl.pallas_call(body, **kw)


def _sds(shape, dtype):
    return jax.ShapeDtypeStruct(shape, dtype)


HBM_SPEC = pl.BlockSpec(memory_space=pl.ANY)


def _mesh_pos():
    return lax.axis_index("x"), lax.axis_index("y"), lax.axis_index("c")


class _AllGather:
    def __init__(self, blocks):
        self.blocks = list(blocks)
        n = len(self.blocks)
        self.out_shape = [_sds((N_DEV,) + b.shape, b.dtype) for b in self.blocks]
        self.scratch = [pltpu.SemaphoreType.DMA((n, 7)), pltpu.SemaphoreType.DMA((n, 7)), pltpu.SemaphoreType.DMA((n,))]

    def bind(self, ins, outs, sems):
        n = len(ins)
        send_sems, recv_sems, local_sems = sems
        x, y, c = _mesh_pos()
        me, sibling = (x, y, c), (x, y, 1 - c)
        chips = [(1 - x, y), (x, 1 - y), (1 - x, 1 - y)]

        def slab(t, dev):
            return outs[t].at[4 * dev[0] + 2 * dev[1] + dev[2]]

        def copy(t, k, block, to, src=None):
            dst = slab(t, block)
            return pltpu.make_async_remote_copy(
                src_ref=dst if src is None else src, dst_ref=dst, send_sem=send_sems.at[t, k], recv_sem=recv_sems.at[t, k],
                device_id=to, device_id_type=MESH)

        mine = [pltpu.make_async_copy(ins[t], slab(t, me), local_sems.at[t]) for t in range(n)]
        first = []
        for t in range(n):
            first.append(copy(t, 0, me, sibling, src=ins[t]))
            first += [copy(t, 1 + j, me, (*chip, c), src=ins[t]) for j, chip in enumerate(chips)]
        passed = [copy(t, 4 + j, (*chip, c), sibling) for j, chip in enumerate(chips) for t in range(n)]

        def start():
            for cp in mine + first:
                cp.start()

        def mid():
            for j, chip in enumerate(chips):
                for t in range(n):
                    copy(t, 1 + j, (*chip, c), me).wait_recv()
                    passed[j * n + t].start()

        def finish():
            for t in range(n):
                copy(t, 0, sibling, me).wait_recv()
                for j, chip in enumerate(chips):
                    copy(t, 4 + j, (*chip, 1 - c), me).wait_recv()
            for cp in first + passed:
                cp.wait_send()
            for cp in mine:
                cp.wait()

        return start, mid, finish


class _SiblingSwap:
    def __init__(self, send):
        self.blocks = [send]
        self.out_shape = [_sds(send.shape, send.dtype)]
        self.scratch = [pltpu.SemaphoreType.DMA, pltpu.SemaphoreType.DMA]

    def bind(self, ins, outs, sems):
        x, y, c = _mesh_pos()
        cp = pltpu.make_async_remote_copy(src_ref=ins[0], dst_ref=outs[0], send_sem=sems[0], recv_sem=sems[1],
                                          device_id=(x, y, 1 - c), device_id_type=MESH)
        return cp.start, (lambda: None), cp.wait


class _ChipExchange:
    def __init__(self, part):
        self.blocks = [part]
        self.out_shape = [_sds((3,) + part.shape[1:], part.dtype)]
        self.scratch = [pltpu.SemaphoreType.DMA((3,)), pltpu.SemaphoreType.DMA((3,))]

    def bind(self, ins, outs, sems):
        x, y, c = _mesh_pos()
        chips = [(1 - x, y), (x, 1 - y), (1 - x, 1 - y)]
        cps = [
            pltpu.make_async_remote_copy(src_ref=ins[0].at[2 * cx + cy], dst_ref=outs[0].at[j], send_sem=sems[0].at[j],
                                         recv_sem=sems[1].at[j], device_id=(cx, cy, c), device_id_type=MESH)
            for j, (cx, cy) in enumerate(chips)
        ]

        def start():
            for cp in cps:
                cp.start()

        def finish():
            for cp in cps:
                cp.wait()

        return start, (lambda: None), finish


def _comm_lists(comm):
    if comm is None:
        return [], [], [], []
    n = len(comm.blocks)
    return list(comm.blocks), [HBM_SPEC] * n, list(comm.out_shape), list(comm.scratch)


def _split_refs(refs, n_pre, n_in, n_cin, n_out, n_cout, n_scr):
    cuts = [n_pre, n_in, n_cin, n_out, n_cout, n_scr]
    out, at = [], 0
    for c in cuts:
        out.append(refs[at:at + c])
        at += c
    out.append(refs[at:])
    return out


def _run_comm(comm, cin, cout, csem, step, n_steps, when="start"):
    if comm is None:
        return
    start, mid, finish = comm.bind(cin, cout, csem)
    if when == "start":
        pl.when(step == 0)(start)
    else:
        pl.when(step == max(n_steps - 2, 0))(mid)
        pl.when(step == n_steps - 1)(finish)


def _comm_only(comm, name):
    ins, in_specs, out_shape, scratch = _comm_lists(comm)
    n = len(ins)

    def body(*refs):
        start, mid, finish = comm.bind(refs[:n], refs[n:2 * n], refs[2 * n:])
        start()
        mid()
        finish()

    outs = _pcall(body, name=name, out_shape=tuple(out_shape), in_specs=in_specs, out_specs=tuple([HBM_SPEC] * n),
                  scratch_shapes=scratch, compiler_params=_cp())(*ins)
    return list(outs)


def _all_reduce_small(v):
    rows, cols = v.shape

    def body(v_ref, o_ref, slots, send_sems, recv_sems):
        x, y, c = _mesh_pos()
        me = 4 * x + 2 * y + c
        slots[me] = v_ref[...]
        cps = []
        for k in range(1, N_DEV):
            px, py, pc = x ^ (k >> 2), y ^ ((k >> 1) & 1), c ^ (k & 1)
            cps.append(pltpu.make_async_remote_copy(
                src_ref=v_ref, dst_ref=slots.at[me], send_sem=send_sems.at[k - 1], recv_sem=recv_sems.at[k - 1],
                device_id=(px, py, pc), device_id_type=MESH))
        for cp in cps:
            cp.start()
        for cp in cps:
            cp.wait()
        acc = slots[0]
        for k in range(1, N_DEV):
            acc = acc + slots[k]
        o_ref[...] = acc

    return _pcall(
        body, name="all_reduce_small", out_shape=_sds((rows, cols), F32),
        in_specs=[pl.BlockSpec(memory_space=pltpu.VMEM)], out_specs=pl.BlockSpec(memory_space=pltpu.VMEM),
        scratch_shapes=[pltpu.VMEM((N_DEV, rows, cols), F32), pltpu.SemaphoreType.DMA((7,)), pltpu.SemaphoreType.DMA((7,))],
        compiler_params=_cp(),
    )(v)


def _row_tile(rows, cap=1040):
    return max(tt for tt in range(16, cap + 1, 16) if rows % tt == 0)


def _add2(a, b, *, name):
    nq, rows, cols = a.shape
    tm = _row_tile(rows)

    def body(a_ref, b_ref, o_ref):
        o_ref[...] = (a_ref[...].astype(F32) + b_ref[...].astype(F32)).astype(BF16)

    spec = pl.BlockSpec((1, tm, cols), lambda q, i: (q, i, 0))
    return _pcall(body, name=name, out_shape=_sds(a.shape, BF16), grid=(nq, rows // tm),
                  in_specs=[spec, spec], out_specs=spec, compiler_params=_cp())(a, b)


def _add4(own, recv, *, name, rows_apart=False):
    rows, cols = own.shape
    tm = _row_tile(rows, 256 if rows_apart else 1040)

    def body(o_ref, r_ref, g_ref):
        tot = ((o_ref[...].astype(F32) + r_ref[0].astype(F32)) + r_ref[1].astype(F32)) + r_ref[2].astype(F32)
        if rows_apart:
            g_ref[:, 0, :] = tot
        else:
            g_ref[...] = tot

    if rows_apart:
        out_shape, out_spec = _sds((rows, 1, cols), F32), pl.BlockSpec((tm, 1, cols), lambda i: (i, 0, 0))
    else:
        out_shape, out_spec = _sds((rows, cols), F32), pl.BlockSpec((tm, cols), lambda i: (i, 0))
    return _pcall(
        body, name=name, out_shape=out_shape, grid=(rows // tm,),
        in_specs=[pl.BlockSpec((tm, cols), lambda i: (i, 0)), pl.BlockSpec((3, tm, cols), lambda i: (0, i, 0))],
        out_specs=out_spec, compiler_params=_cp(),
    )(own, recv)


def _mm(a, b, *, m, n, k, name, ta=False, tb=False, b_off=(0, 0), res=None, out_dtype=F32, tm=512, tn=1024, tk=1024,
        comm=None):
    tm, tn, tk = min(tm, m), min(tn, n), min(tk, k)
    assert m % tm == 0 and n % tn == 0 and k % tk == 0, (name, m, n, k, tm, tn, tk)
    nk = k // tk
    grid = (m // tm, n // tn, nk)
    bo0, bo1 = b_off
    if ta:
        a_spec = pl.BlockSpec((tk, tm), lambda i, j, kk: (kk, i))
    else:
        a_spec = pl.BlockSpec((tm, tk), lambda i, j, kk: (i, kk))
    if tb:
        b_spec = pl.BlockSpec((tn, tk), lambda i, j, kk: (j + bo0, kk + bo1))
    else:
        b_spec = pl.BlockSpec((tk, tn), lambda i, j, kk: (kk + bo0, j + bo1))
    o_spec = pl.BlockSpec((tm, tn), lambda i, j, kk: (i, j))
    dn = (((0 if ta else 1,), (1 if tb else 0,)), ((), ()))
    has_res = res is not None
    c_ins, c_in_specs, c_out_shape, c_scratch = _comm_lists(comm)
    n_in = 3 if has_res else 2
    n_scr = 1 if nk > 1 else 0

    def body(*refs):
        _, ins, cin, outs, cout, scr, csem = _split_refs(refs, 0, n_in, len(c_ins), 1, len(c_ins), n_scr)
        a_ref, b_ref = ins[0], ins[1]
        r_ref = ins[2] if has_res else None
        o_ref = outs[0]
        step = (pl.program_id(0) * grid[1] + pl.program_id(1)) * nk + pl.program_id(2)
        _run_comm(comm, cin, cout, csem, step, grid[0] * grid[1] * nk, "start")
        p = lax.dot_general(a_ref[...].astype(BF16), b_ref[...].astype(BF16), dn, preferred_element_type=F32)

        def finish(acc):
            if has_res:
                acc = acc + r_ref[...]
            o_ref[...] = acc.astype(out_dtype)

        if nk == 1:
            finish(p)
        else:
            acc_ref = scr[0]
            kk = pl.program_id(2)

            @pl.when(kk == 0)
            def _():
                acc_ref[...] = p

            @pl.when(kk > 0)
            def _():
                acc_ref[...] += p

            @pl.when(kk == nk - 1)
            def _():
                finish(acc_ref[...])

        _run_comm(comm, cin, cout, csem, step, grid[0] * grid[1] * nk, "end")

    ins = [a, b] + ([res] if has_res else [])
    in_specs = [a_spec, b_spec] + ([o_spec] if has_res else [])
    sem = ("arbitrary",) * 3 if comm is not None else ("parallel", "parallel", "arbitrary")
    outs = _pcall(
        body, name=name, out_shape=tuple([_sds((m, n), out_dtype)] + c_out_shape), grid=grid,
        in_specs=in_specs + c_in_specs, out_specs=tuple([o_spec] + [HBM_SPEC] * len(c_ins)),
        scratch_shapes=([pltpu.VMEM((tm, tn), F32)] if nk > 1 else []) + c_scratch,
        compiler_params=_cp(dimension_semantics=sem),
    )(*(ins + c_ins))
    return outs[0] if comm is None else (outs[0], list(outs[1:]))


def _rms_fwd(x, g, *, name, tm=512):
    t, d = x.shape
    tm = min(tm, t)

    def body(x_ref, g_ref, h_ref):
        xv = x_ref[...]
        r = lax.rsqrt(jnp.mean(xv * xv, axis=-1, keepdims=True) + EPS)
        h_ref[...] = (xv * r * g_ref[...]).astype(BF16)

    return _pcall(
        body, name=name, out_shape=_sds((t, d), BF16), grid=(t // tm,),
        in_specs=[pl.BlockSpec((tm, d), lambda i: (i, 0)), pl.BlockSpec((1, d), lambda i: (0, 0))],
        out_specs=pl.BlockSpec((tm, d), lambda i: (i, 0)), compiler_params=_cp(),
    )(x, g)


def _rms_bwd(dh, x, g, res, *, name, tm=512):
    t, d = x.shape
    tm = min(tm, t)
    has_res = res is not None

    def body(*refs):
        dh_ref, x_ref, g_ref = refs[:3]
        r_ref = refs[3] if has_res else None
        dx_ref, dg_ref = refs[-2], refs[-1]
        xv = x_ref[...]
        r = lax.rsqrt(jnp.mean(xv * xv, axis=-1, keepdims=True) + EPS)
        xh = xv * r
        dhv = dh_ref[...]
        gd = dhv * g_ref[...]
        dx = r * (gd - xh * jnp.mean(gd * xh, axis=-1, keepdims=True))
        if has_res:
            dx = dx + r_ref[...]
        dx_ref[...] = dx

        @pl.when(pl.program_id(0) == 0)
        def _():
            dg_ref[...] = jnp.zeros_like(dg_ref)

        dg_ref[...] += jnp.sum(dhv * xh, axis=0, keepdims=True)

    row = pl.BlockSpec((tm, d), lambda i: (i, 0))
    vec = pl.BlockSpec((1, d), lambda i: (0, 0))
    return _pcall(
        body, name=name, out_shape=(_sds((t, d), F32), _sds((1, d), F32)), grid=(t // tm,),
        in_specs=[row, row, vec] + ([row] if has_res else []), out_specs=(row, vec),
        compiler_params=_cp(dimension_semantics=("arbitrary",)),
    )(*([dh, x, g] + ([res] if has_res else [])))


def _conv_fwd(z, conv_w, conv_b, *, nb, s, d, tc=256):
    nc = d // tc

    def body(cb_ref, cc_ref, cv_ref, w_ref, b_ref, y_ref):
        u = cc_ref[...] * cv_ref[...]
        row = lax.broadcasted_iota(jnp.int32, u.shape, 0)
        u1 = jnp.where(row >= 1, pltpu.roll(u, 1, axis=0), 0.0)
        u2 = jnp.where(row >= 2, pltpu.roll(u, 2, axis=0), 0.0)
        w = w_ref[...]
        y = w[2:3, :] * u + w[1:2, :] * u1 + w[0:1, :] * u2 + b_ref[...]
        y_ref[...] = (cb_ref[...] * y).astype(BF16)

    def part(p):
        return pl.BlockSpec((s, tc), lambda b, j: (b, p * nc + j))

    return _pcall(
        body, name="conv_fwd", out_shape=_sds((nb * s, d), BF16), grid=(nb, nc),
        in_specs=[part(0), part(1), part(2), pl.BlockSpec((3, tc), lambda b, j: (0, j)), pl.BlockSpec((1, tc), lambda b, j: (0, j))],
        out_specs=pl.BlockSpec((s, tc), lambda b, j: (b, j)), compiler_params=_cp(),
    )(z, z, z, conv_w, conv_b)


def _conv_bwd(z, dy, conv_w, conv_b, dz, *, nb, s, d, tc=256):
    nc = d // tc

    def body(cb_ref, cc_ref, cv_ref, dy_ref, w_ref, b_ref, dz_in, dz_ref, dw_ref, db_ref, stash):
        b_i, p = pl.program_id(1), pl.program_id(2)

        @pl.when(p == 0)
        def _():
            cc, cv = cc_ref[...], cv_ref[...]
            u = cc * cv
            n = u.shape[0]
            row = lax.broadcasted_iota(jnp.int32, u.shape, 0)
            u1 = jnp.where(row >= 1, pltpu.roll(u, 1, axis=0), 0.0)
            u2 = jnp.where(row >= 2, pltpu.roll(u, 2, axis=0), 0.0)
            w = w_ref[...]
            y = w[2:3, :] * u + w[1:2, :] * u1 + w[0:1, :] * u2 + b_ref[...]
            dyv = dy_ref[...]
            dconv = dyv * cb_ref[...]
            g1 = jnp.where(row < n - 1, pltpu.roll(dconv, n - 1, axis=0), 0.0)
            g2 = jnp.where(row < n - 2, pltpu.roll(dconv, n - 2, axis=0), 0.0)
            du = w[2:3, :] * dconv + w[1:2, :] * g1 + w[0:1, :] * g2
            stash[0] = (dyv * y).astype(BF16)
            stash[1] = (du * cv).astype(BF16)
            stash[2] = (du * cc).astype(BF16)
            dws = [jnp.sum(dconv * u2, axis=0, keepdims=True), jnp.sum(dconv * u1, axis=0, keepdims=True),
                   jnp.sum(dconv * u, axis=0, keepdims=True)]
            db = jnp.sum(dconv, axis=0, keepdims=True)

            @pl.when(b_i == 0)
            def _():
                for r in range(3):
                    dw_ref[r:r + 1, :] = dws[r]
                db_ref[...] = db

            @pl.when(b_i > 0)
            def _():
                for r in range(3):
                    dw_ref[r:r + 1, :] += dws[r]
                db_ref[...] += db

        dz_ref[...] = stash[p]

    return _pcall(
        body, name="conv_bwd",
        out_shape=(_sds(dz.shape, dz.dtype), _sds((3, d), F32), _sds((1, d), F32)), grid=(nc, nb, 3),
        in_specs=[
            pl.BlockSpec((s, tc), lambda j, b, p: (b, j)), pl.BlockSpec((s, tc), lambda j, b, p: (b, nc + j)),
            pl.BlockSpec((s, tc), lambda j, b, p: (b, 2 * nc + j)), pl.BlockSpec((s, tc), lambda j, b, p: (b, j)),
            pl.BlockSpec((3, tc), lambda j, b, p: (0, j)), pl.BlockSpec((1, tc), lambda j, b, p: (0, j)),
            HBM_SPEC,
        ],
        out_specs=(pl.BlockSpec((s, tc), lambda j, b, p: (b, p * nc + j)),
                   pl.BlockSpec((3, tc), lambda j, b, p: (0, j)), pl.BlockSpec((1, tc), lambda j, b, p: (0, j))),
        scratch_shapes=[pltpu.VMEM((3, s, tc), BF16)],
        input_output_aliases={6: 0},
        compiler_params=_cp(dimension_semantics=("arbitrary", "arbitrary", "arbitrary")),
    )(z, z, z, dy, conv_w, conv_b, dz)


def _head_rms(xv, g):
    r = lax.rsqrt(jnp.mean(xv * xv, axis=-1, keepdims=True) + EPS)
    return xv * r * g


def _head_rms_bwd(xv, g, dy):
    r = lax.rsqrt(jnp.mean(xv * xv, axis=-1, keepdims=True) + EPS)
    xh = xv * r
    gd = dy * g
    dx = r * (gd - xh * jnp.mean(gd * xh, axis=-1, keepdims=True))
    return dx, jnp.sum(dy * xh, axis=0, keepdims=True)


def _fox_prep(z, q_g, k_g, *, t, d, tm=256):
    nh = d // HD

    def body(z_ref, qg_ref, kg_ref, q_ref, k_ref, v_ref):
        qg, kg = qg_ref[...], kg_ref[...]
        for h in range(nh):
            c0 = h * HD
            q_ref[:, c0:c0 + HD] = _head_rms(z_ref[:, c0:c0 + HD], qg).astype(BF16)
            k_ref[:, c0:c0 + HD] = _head_rms(z_ref[:, d + c0:d + c0 + HD], kg).astype(BF16)
        v_ref[...] = z_ref[:, 2 * d:3 * d].astype(BF16)

    o = pl.BlockSpec((tm, d), lambda i: (i, 0))
    g = pl.BlockSpec((1, HD), lambda i: (0, 0))
    return _pcall(
        body, name="fox_prep", out_shape=(_sds((t, d), BF16),) * 3, grid=(t // tm,),
        in_specs=[pl.BlockSpec((tm, 3 * d), lambda i: (i, 1)), g, g], out_specs=(o, o, o), compiler_params=_cp(),
    )(z, q_g, k_g)


def _lane_cumsum(v, reverse=False):
    n = v.shape[1]
    lane = lax.broadcasted_iota(jnp.int32, v.shape, 1)
    sh = 1
    while sh < n:
        if reverse:
            v = v + jnp.where(lane < n - sh, pltpu.roll(v, n - sh, axis=1), 0.0)
        else:
            v = v + jnp.where(lane >= sh, pltpu.roll(v, sh, axis=1), 0.0)
        sh *= 2
    return v


def _fox_gates(z, f_bias_col, *, nb, s, nh, ff_block):
    def body(z_ref, b_ref, c_ref):
        ff = z_ref[...].T[0:nh, :] + b_ref[...]
        lf = jnp.minimum(ff, 0.0) - jnp.log(1.0 + jnp.exp(-jnp.abs(ff)))
        c_ref[0] = _lane_cumsum(lf) * SQRT_HD

    return _pcall(
        body, name="fox_gates", out_shape=_sds((nb, nh, s), F32), grid=(nb,),
        in_specs=[pl.BlockSpec((s, HD), lambda b: (b, ff_block)), pl.BlockSpec((nh, 1), lambda b: (0, 0))],
        out_specs=pl.BlockSpec((1, nh, s), lambda b: (b, 0, 0)), compiler_params=_cp(),
    )(z, f_bias_col)


def _fox_gates_bwd(z, f_bias_col, row_sums, col_sums, dz, *, nb, s, nh, ff_block):
    def body(z_ref, b_ref, rs_ref, cs_ref, dz_in, dz_ref, db_ref, dc_sc):
        b_i, h = pl.program_id(0), pl.program_id(1)
        dc_sc[pl.ds(h, 1), :] = (rs_ref[...] - cs_ref[...]).T[0:1, :]

        @pl.when(h == nh - 1)
        def _():
            ff = z_ref[...].T[0:nh, :] + b_ref[...]
            dlf = _lane_cumsum(dc_sc[...], reverse=True)
            dff = dlf * (1.0 / (1.0 + jnp.exp(ff)))
            pad = jnp.concatenate([dff, jnp.zeros((HD - nh, s), F32)], axis=0)
            dz_ref[...] = pad.T.astype(BF16)
            dbv = jnp.sum(dff, axis=1, keepdims=True)

            @pl.when(b_i == 0)
            def _():
                db_ref[...] = dbv

            @pl.when(b_i > 0)
            def _():
                db_ref[...] += dbv

    hs = pl.BlockSpec((s, HD), lambda b, h: (b, h))
    zs = pl.BlockSpec((s, HD), lambda b, h: (b, ff_block))
    return _pcall(
        body, name="fox_gates_bwd", out_shape=(_sds(dz.shape, dz.dtype), _sds((nh, 1), F32)), grid=(nb, nh),
        in_specs=[zs, pl.BlockSpec((nh, 1), lambda b, h: (0, 0)), hs, hs, HBM_SPEC],
        out_specs=(zs, pl.BlockSpec((nh, 1), lambda b, h: (0, 0))),
        scratch_shapes=[pltpu.VMEM((nh, s), F32)],
        input_output_aliases={4: 0}, compiler_params=_cp(dimension_semantics=("arbitrary", "arbitrary")),
    )(z, f_bias_col, row_sums, col_sums, dz)


def _fox_pairs(nq, key_major):
    if key_major:
        pairs = [(i, j) for j in range(nq) for i in range(j, nq)]
    else:
        pairs = [(i, j) for i in range(nq) for j in range(i + 1)]
    return jnp.array([p[0] for p in pairs], jnp.int32), jnp.array([p[1] for p in pairs], jnp.int32)


def _with_ones(x):
    return jnp.concatenate([x, jnp.ones_like(x)], axis=1)


def _fox_specs(nb, nh, nq, tq, tk, hp):
    qs = pl.BlockSpec((tq, hp * HD), lambda b, h, p, it, jt: (b * nq + it[p], h))
    ks = pl.BlockSpec((tk, hp * HD), lambda b, h, p, it, jt: (b * nq + jt[p], h))
    cs = pl.BlockSpec((hp, 1, tk), lambda b, h, p, it, jt: (b * (nh // hp) + h, 0, jt[p]))
    return qs, ks, cs


def _fox_raw(qv, kv, ckv, diag, tq, tk):
    sc = lax.dot_general(qv, kv, (((1,), (1,)), ((), ())), preferred_element_type=F32) - ckv
    if diag:
        rows = lax.broadcasted_iota(jnp.int32, (tq, tk), 0)
        cols = lax.broadcasted_iota(jnp.int32, (tq, tk), 1)
        sc = jnp.where(rows >= cols, sc, NEG)
    return sc


def _fox_probs(qv, kv, ckv, lse_col, diag, tq, tk):
    sc = _fox_raw(qv, kv, ckv, diag, tq, tk)
    return jnp.exp2(sc * EXP2_C - lse_col * LOG2E)


def _fox_call(body, *, name, ins, in_specs, out_shape, out_specs, scratch, grid, tables, comm):
    c_ins, c_in_specs, c_out_shape, c_scratch = _comm_lists(comm)
    n_in, n_out, n_scr, nc = len(ins), len(out_shape), len(scratch), len(c_ins)
    n_steps = grid[0] * grid[1] * grid[2]

    def wrapped(*refs):
        pre, rin, cin, rout, cout, scr, csem = _split_refs(refs, 2, n_in, nc, n_out, nc, n_scr)
        step = (pl.program_id(0) * grid[1] + pl.program_id(1)) * grid[2] + pl.program_id(2)
        _run_comm(comm, cin, cout, csem, step, n_steps, "start")
        body(*pre, *rin, *rout, *scr)
        _run_comm(comm, cin, cout, csem, step, n_steps, "end")

    sem = ("arbitrary",) * 3 if comm is not None else ("parallel", "parallel", "arbitrary")
    outs = _pcall(
        wrapped, name=name, out_shape=tuple(list(out_shape) + c_out_shape),
        grid_spec=pltpu.PrefetchScalarGridSpec(
            num_scalar_prefetch=2, grid=grid, in_specs=list(in_specs) + c_in_specs,
            out_specs=tuple(list(out_specs) + [HBM_SPEC] * nc), scratch_shapes=list(scratch) + c_scratch),
        compiler_params=_cp(dimension_semantics=sem),
    )(*tables, *ins, *c_ins)
    return list(outs[:n_out]), list(outs[n_out:])


def _fox_fwd(q, k, v, cks, *, nb, s, nh, tq=512, comm=None):
    tk = tq
    nq = s // tq
    t = nb * s
    hp = FOX_HP
    it, jt = _fox_pairs(nq, key_major=False)

    def body(it_ref, jt_ref, q_ref, k_ref, v_ref, c_ref, o_ref, lse_ref, m_sc, acc_sc):
        p_id = pl.program_id(2)
        i, j = it_ref[p_id], jt_ref[p_id]

        @pl.when(j == 0)
        def _():
            m_sc[...] = jnp.full_like(m_sc, NEG)
            acc_sc[...] = jnp.zeros_like(acc_sc)

        def step(diag):
            for hh in range(hp):
                cols = slice(hh * HD, (hh + 1) * HD)
                sc = _fox_raw(q_ref[:, cols], k_ref[:, cols], c_ref[hh], diag, tq, tk)
                m_old = m_sc[hh]
                m_new = jnp.maximum(m_old, jnp.max(sc, axis=-1, keepdims=True))
                alpha = jnp.exp2((m_old - m_new) * EXP2_C)
                p = jnp.exp2((sc - m_new) * EXP2_C).astype(BF16)
                acc = alpha * acc_sc[hh] + jnp.dot(p, _with_ones(v_ref[:, cols]), preferred_element_type=F32)
                if diag:
                    l = acc[:, HD:2 * HD]
                    o_ref[:, cols] = (acc[:, 0:HD] / l).astype(BF16)
                    lse_ref[:, cols] = m_new * (1.0 / SQRT_HD) + jnp.log(l)
                else:
                    acc_sc[hh] = acc
                    m_sc[hh] = m_new

        @pl.when(j < i)
        def _():
            step(False)

        @pl.when(j == i)
        def _():
            step(True)

    qs, ks, cs = _fox_specs(nb, nh, nq, tq, tk, hp)
    (o, lse), c_out = _fox_call(
        body, name="fox_fwd", ins=[q, k, v, cks], in_specs=[qs, ks, ks, cs],
        out_shape=[_sds((t, nh * HD), BF16), _sds((t, nh * HD), F32)], out_specs=[qs, qs],
        scratch=[pltpu.VMEM((hp, tq, 1), F32), pltpu.VMEM((hp, tq, 2 * HD), F32)],
        grid=(nb, nh // hp, int(it.shape[0])), tables=(it, jt), comm=comm)
    return o, lse, c_out


def _fox_bwd_dq(q, k, v, cks, do, o, lse, *, nb, s, nh, tq=512, comm=None):
    tk = tq
    nq = s // tq
    t = nb * s
    hp = FOX_HP
    it, jt = _fox_pairs(nq, key_major=False)

    def body(it_ref, jt_ref, q_ref, k_ref, v_ref, c_ref, do_ref, o_ref, lse_ref, dq_ref, rs_ref, acc_sc, dl_sc):
        p_id = pl.program_id(2)
        i, j = it_ref[p_id], jt_ref[p_id]

        @pl.when(j == 0)
        def _():
            acc_sc[...] = jnp.zeros_like(acc_sc)
            for hh in range(hp):
                cols = slice(hh * HD, (hh + 1) * HD)
                dl_sc[hh] = jnp.sum(do_ref[:, cols] * o_ref[:, cols].astype(F32), axis=-1, keepdims=True)

        def step(diag):
            for hh in range(hp):
                cols = slice(hh * HD, (hh + 1) * HD)
                kv = k_ref[:, cols]
                p = _fox_probs(q_ref[:, cols], kv, c_ref[hh], lse_ref[:, hh * HD:hh * HD + 1], diag, tq, tk)
                dp = lax.dot_general(do_ref[:, cols].astype(BF16), v_ref[:, cols], (((1,), (1,)), ((), ())),
                                     preferred_element_type=F32)
                ds = (p * (dp - dl_sc[hh])).astype(BF16)
                acc = acc_sc[hh] + jnp.dot(ds, _with_ones(kv), preferred_element_type=F32)
                if diag:
                    dq_ref[:, cols] = acc[:, 0:HD] * (1.0 / SQRT_HD)
                    rs_ref[:, cols] = acc[:, HD:2 * HD]
                else:
                    acc_sc[hh] = acc

        @pl.when(j < i)
        def _():
            step(False)

        @pl.when(j == i)
        def _():
            step(True)

    qs, ks, cs = _fox_specs(nb, nh, nq, tq, tk, hp)
    (dq, rs), c_out = _fox_call(
        body, name="fox_bwd_dq", ins=[q, k, v, cks, do, o, lse], in_specs=[qs, ks, ks, cs, qs, qs, qs],
        out_shape=[_sds((t, nh * HD), F32), _sds((t, nh * HD), F32)], out_specs=[qs, qs],
        scratch=[pltpu.VMEM((hp, tq, 2 * HD), F32), pltpu.VMEM((hp, tq, 1), F32)],
        grid=(nb, nh // hp, int(it.shape[0])), tables=(it, jt), comm=comm)
    return dq, rs, c_out


def _fox_bwd_dkv(q, k, v, cks, do, o, lse, *, nb, s, nh, tq=512, comm=None):
    tk = tq
    nq = s // tq
    t = nb * s
    hp = FOX_HP
    it, jt = _fox_pairs(nq, key_major=True)

    def body(it_ref, jt_ref, q_ref, k_ref, v_ref, c_ref, do_ref, o_ref, lse_ref, dk_ref, dv_ref, cs_ref, dk_sc, dv_sc):
        p_id = pl.program_id(2)
        i, j = it_ref[p_id], jt_ref[p_id]

        def step(diag):
            for hh in range(hp):
                cols = slice(hh * HD, (hh + 1) * HD)
                qv = q_ref[:, cols]
                p = _fox_probs(qv, k_ref[:, cols], c_ref[hh], lse_ref[:, hh * HD:hh * HD + 1], diag, tq, tk)
                dov = do_ref[:, cols]
                dob = dov.astype(BF16)
                dvp = lax.dot_general(p.astype(BF16), dob, (((0,), (0,)), ((), ())), preferred_element_type=F32)
                dp = lax.dot_general(dob, v_ref[:, cols], (((1,), (1,)), ((), ())), preferred_element_type=F32)
                delta = jnp.sum(dov * o_ref[:, cols].astype(F32), axis=-1, keepdims=True)
                ds = (p * (dp - delta)).astype(BF16)
                dkp = lax.dot_general(ds, _with_ones(qv), (((0,), (0,)), ((), ())), preferred_element_type=F32)
                if diag:
                    dk_sc[hh] = dkp
                    dv_sc[hh] = dvp
                else:
                    dk_sc[hh] += dkp
                    dv_sc[hh] += dvp

        @pl.when(i == j)
        def _():
            step(True)

        @pl.when(i > j)
        def _():
            step(False)

        @pl.when(i == nq - 1)
        def _():
            for hh in range(hp):
                cols = slice(hh * HD, (hh + 1) * HD)
                dk_ref[:, cols] = dk_sc[hh, :, 0:HD] * (1.0 / SQRT_HD)
                cs_ref[:, cols] = dk_sc[hh, :, HD:2 * HD]
                dv_ref[:, cols] = dv_sc[hh].astype(BF16)

    qs, ks, cs = _fox_specs(nb, nh, nq, tq, tk, hp)
    (dk, dv, csum), c_out = _fox_call(
        body, name="fox_bwd_dkv", ins=[q, k, v, cks, do, o, lse], in_specs=[qs, ks, ks, cs, qs, qs, qs],
        out_shape=[_sds((t, nh * HD), F32), _sds((t, nh * HD), BF16), _sds((t, nh * HD), F32)], out_specs=[ks, ks, ks],
        scratch=[pltpu.VMEM((hp, tk, 2 * HD), F32), pltpu.VMEM((hp, tk, HD), F32)],
        grid=(nb, nh // hp, int(it.shape[0])), tables=(it, jt), comm=comm)
    return dk, dv, csum, c_out


def _fox_post(z, dqn, dkn, dv, q_g, k_g, dz, *, t, d, tm=256):
    nh = d // HD

    def body(z_ref, dq_ref, dk_ref, dv_ref, qg_ref, kg_ref, dz_in, dz_ref, dqg_ref, dkg_ref):
        qg, kg = qg_ref[...], kg_ref[...]
        aq = jnp.zeros((1, HD), F32)
        ak = jnp.zeros((1, HD), F32)
        for h in range(nh):
            c0 = h * HD
            dx, sg = _head_rms_bwd(z_ref[:, c0:c0 + HD], qg, dq_ref[:, c0:c0 + HD])
            dz_ref[:, c0:c0 + HD] = dx.astype(BF16)
            aq = aq + sg
            dx, sg = _head_rms_bwd(z_ref[:, d + c0:d + c0 + HD], kg, dk_ref[:, c0:c0 + HD])
            dz_ref[:, d + c0:d + c0 + HD] = dx.astype(BF16)
            ak = ak + sg
        dz_ref[:, 2 * d:3 * d] = dv_ref[...]

        @pl.when(pl.program_id(0) == 0)
        def _():
            dqg_ref[...] = aq
            dkg_ref[...] = ak

        @pl.when(pl.program_id(0) > 0)
        def _():
            dqg_ref[...] += aq
            dkg_ref[...] += ak

    o = pl.BlockSpec((tm, d), lambda i: (i, 0))
    g = pl.BlockSpec((1, HD), lambda i: (0, 0))
    z3 = pl.BlockSpec((tm, 3 * d), lambda i: (i, 1))
    return _pcall(
        body, name="fox_post", out_shape=(_sds(dz.shape, dz.dtype), _sds((1, HD), F32), _sds((1, HD), F32)), grid=(t // tm,),
        in_specs=[z3, o, o, o, g, g, HBM_SPEC], out_specs=(z3, g, g),
        input_output_aliases={6: 0}, compiler_params=_cp(dimension_semantics=("arbitrary",)),
    )(z, dqn, dkn, dv, q_g, k_g, dz)


def _xa_prep_kv(kv, k_g, *, rows, d):
    xd = d // XH

    def body(kv_ref, g_ref, k_ref, v_ref):
        g = g_ref[...]
        for h in range(XH):
            c0 = h * xd
            k_ref[:, c0:c0 + xd] = _head_rms(kv_ref[:, c0:c0 + xd], g).astype(BF16)
        v_ref[...] = kv_ref[:, d:2 * d].astype(BF16)

    return _pcall(body, name="xa_prep_kv", out_shape=(_sds((rows, d), BF16),) * 2, compiler_params=_cp())(kv, k_g)


def _xa_fwd(z, kn, vb, q_g, *, nb, s, d, nm, q_block, tq=512):
    xd = d // XH
    nq = s // tq

    def body(z_ref, k_ref, v_ref, g_ref, y_ref):
        g = g_ref[...]
        for h in range(XH):
            c0 = h * xd
            qn = _head_rms(z_ref[:, c0:c0 + xd], g).astype(BF16)
            sc = lax.dot_general(qn, k_ref[:, c0:c0 + xd], (((1,), (1,)), ((), ())), preferred_element_type=F32)
            sc = sc / math.sqrt(xd)
            p = jnp.exp(sc - jnp.max(sc, axis=-1, keepdims=True))
            p = p / jnp.sum(p, axis=-1, keepdims=True)
            y_ref[:, c0:c0 + xd] = jnp.dot(p.astype(BF16), v_ref[:, c0:c0 + xd], preferred_element_type=F32).astype(BF16)

    kvs = pl.BlockSpec((nm, d), lambda b, i: (b, 0))
    return _pcall(
        body, name="xa_fwd", out_shape=_sds((nb * s, d), BF16), grid=(nb, nq),
        in_specs=[pl.BlockSpec((tq, d), lambda b, i: (b * nq + i, q_block)), kvs, kvs, pl.BlockSpec((1, xd), lambda b, i: (0, 0))],
        out_specs=pl.BlockSpec((tq, d), lambda b, i: (b * nq + i, 0)), compiler_params=_cp(),
    )(z, kn, vb, q_g)


def _xa_bwd(z, kn, vb, q_g, dy, dz, *, nb, s, d, nm, q_block, tq=512):
    xd = d // XH
    nq = s // tq

    def body(z_ref, k_ref, v_ref, g_ref, dy_ref, dz_in, dz_ref, dk_ref, dv_ref, dg_ref):
        b_i, i = pl.program_id(0), pl.program_id(1)
        g = g_ref[...]

        @pl.when(i == 0)
        def _():
            dk_ref[...] = jnp.zeros_like(dk_ref)
            dv_ref[...] = jnp.zeros_like(dv_ref)

        @pl.when((i == 0) & (b_i == 0))
        def _():
            dg_ref[...] = jnp.zeros_like(dg_ref)

        for h in range(XH):
            c0 = h * xd
            xq = z_ref[:, c0:c0 + xd]
            qn = _head_rms(xq, g).astype(BF16)
            kh, vh = k_ref[:, c0:c0 + xd], v_ref[:, c0:c0 + xd]
            sc = lax.dot_general(qn, kh, (((1,), (1,)), ((), ())), preferred_element_type=F32) / math.sqrt(xd)
            p = jnp.exp(sc - jnp.max(sc, axis=-1, keepdims=True))
            p = p / jnp.sum(p, axis=-1, keepdims=True)
            dob = dy_ref[:, c0:c0 + xd].astype(BF16)
            dv_ref[:, c0:c0 + xd] += lax.dot_general(p.astype(BF16), dob, (((0,), (0,)), ((), ())), preferred_element_type=F32)
            dp = lax.dot_general(dob, vh, (((1,), (1,)), ((), ())), preferred_element_type=F32)
            ds = p * (dp - jnp.sum(p * dp, axis=-1, keepdims=True)) / math.sqrt(xd)
            dsb = ds.astype(BF16)
            dqn = jnp.dot(dsb, kh, preferred_element_type=F32)
            dk_ref[:, c0:c0 + xd] += lax.dot_general(dsb, qn, (((0,), (0,)), ((), ())), preferred_element_type=F32)
            dx, sg = _head_rms_bwd(xq, g, dqn)
            dz_ref[:, c0:c0 + xd] = dx.astype(BF16)
            dg_ref[...] += sg

    kvs = pl.BlockSpec((nm, d), lambda b, i: (b, 0))
    qs = pl.BlockSpec((tq, d), lambda b, i: (b * nq + i, q_block))
    gs = pl.BlockSpec((1, xd), lambda b, i: (0, 0))
    return _pcall(
        body, name="xa_bwd",
        out_shape=(_sds(dz.shape, dz.dtype), _sds((nb * nm, d), F32), _sds((nb * nm, d), F32), _sds((1, xd), F32)), grid=(nb, nq),
        in_specs=[qs, kvs, kvs, gs, pl.BlockSpec((tq, d), lambda b, i: (b * nq + i, 0)), HBM_SPEC],
        out_specs=(qs, kvs, kvs, gs), input_output_aliases={5: 0},
        compiler_params=_cp(dimension_semantics=("arbitrary", "arbitrary")),
    )(z, kn, vb, q_g, dy, dz)


def _xa_post_kv(kv, dkn, dv, k_g, *, rows, d):
    xd = d // XH

    def body(kv_ref, dk_ref, dv_ref, g_ref, dkv_ref, dg_ref):
        g = g_ref[...]
        acc = jnp.zeros((1, xd), F32)
        for h in range(XH):
            c0 = h * xd
            dx, sg = _head_rms_bwd(kv_ref[:, c0:c0 + xd], g, dk_ref[:, c0:c0 + xd])
            dkv_ref[:, c0:c0 + xd] = dx.astype(BF16)
            acc = acc + sg
        dkv_ref[:, d:2 * d] = dv_ref[...].astype(BF16)
        dg_ref[...] = acc

    return _pcall(body, name="xa_post_kv", out_shape=(_sds((rows, 2 * d), BF16), _sds((1, xd), F32)), compiler_params=_cp())(
        kv, dkn, dv, k_g)


def _sigmoid(v):
    return 1.0 / (1.0 + jnp.exp(-v))


def _merge_fwd(z, pa, pb, pc, *, t, d, gate_block, tm=512):
    def body(ga_ref, gb_ref, gc_ref, pa_ref, pb_ref, pc_ref, o_ref):
        o_ref[...] = (_sigmoid(ga_ref[...]) * pa_ref[...] + _sigmoid(gb_ref[...]) * pb_ref[...]
                      + _sigmoid(gc_ref[...]) * pc_ref[...]).astype(BF16)

    def gate(q):
        return pl.BlockSpec((tm, d), lambda i: (i, gate_block + q))

    o = pl.BlockSpec((tm, d), lambda i: (i, 0))
    return _pcall(
        body, name="merge_fwd", out_shape=_sds((t, d), BF16), grid=(t // tm,),
        in_specs=[gate(0), gate(1), gate(2), o, o, o], out_specs=o, compiler_params=_cp(),
    )(z, z, z, pa, pb, pc)


def _gate_bwd(z, dm, pr, dz, *, t, d, col_block, name, tm=512):
    def body(g_ref, dm_ref, p_ref, dz_in, dz_ref, dp_ref):
        sg = _sigmoid(g_ref[...])
        dmv = dm_ref[...]
        dp_ref[...] = (dmv * sg).astype(BF16)
        dz_ref[...] = (dmv * p_ref[...] * sg * (1.0 - sg)).astype(BF16)

    gs = pl.BlockSpec((tm, d), lambda i: (i, col_block))
    o = pl.BlockSpec((tm, d), lambda i: (i, 0))
    return _pcall(
        body, name=name, out_shape=(_sds(dz.shape, dz.dtype), _sds((t, d), BF16)), grid=(t // tm,),
        in_specs=[gs, o, o, HBM_SPEC], out_specs=(gs, o), input_output_aliases={3: 0},
        compiler_params=_cp(),
    )(z, dm, pr, dz)


def _swiglu_fwd(gu, *, t, dff, tm=256):
    def body(g_ref, u_ref, a_ref):
        gv = g_ref[...]
        a_ref[...] = (gv * _sigmoid(gv) * u_ref[...]).astype(BF16)

    return _pcall(
        body, name="swiglu_fwd", out_shape=_sds((t, dff), BF16), grid=(t // tm,),
        in_specs=[pl.BlockSpec((tm, dff), lambda i: (i, 0)), pl.BlockSpec((tm, dff), lambda i: (i, 1))],
        out_specs=pl.BlockSpec((tm, dff), lambda i: (i, 0)), compiler_params=_cp(),
    )(gu, gu)


def _swiglu_bwd(gu, da, *, t, dff, tm=256):
    def body(g_ref, u_ref, da_ref, dg_ref, du_ref):
        gv, dav = g_ref[...], da_ref[...]
        sg = _sigmoid(gv)
        dg_ref[...] = (dav * u_ref[...] * sg * (1.0 + gv * (1.0 - sg))).astype(BF16)
        du_ref[...] = (dav * gv * sg).astype(BF16)

    h0 = pl.BlockSpec((tm, dff), lambda i: (i, 0))
    h1 = pl.BlockSpec((tm, dff), lambda i: (i, 1))
    return _pcall(
        body, name="swiglu_bwd", out_shape=(_sds((t, dff), BF16), _sds((t, dff), BF16)), grid=(t // tm,),
        in_specs=[h0, h1, h0], out_specs=(h0, h0), compiler_params=_cp(),
    )(gu, gu, da)


def _loss_head(y, target, *, t, d, tm=512):
    def body(y_ref, t_ref, dy_ref, l_ref):
        e = y_ref[...] - t_ref[...]
        dy_ref[...] = e * (1.0 / d)

        @pl.when(pl.program_id(0) == 0)
        def _():
            l_ref[...] = jnp.zeros_like(l_ref)

        l_ref[...] += jnp.sum(jnp.sum(e * e, axis=-1, keepdims=True), axis=0, keepdims=True) * (0.5 / d)

    o = pl.BlockSpec((tm, d), lambda i: (i, 0))
    dy, l = _pcall(
        body, name="loss_head", out_shape=(_sds((t, d), F32), _sds((1, HD), F32)), grid=(t // tm,),
        in_specs=[o, o], out_specs=(o, pl.BlockSpec((1, HD), lambda i: (0, 0))),
        compiler_params=_cp(dimension_semantics=("arbitrary",)),
    )(y, target)
    return dy, l[0, 0]


def _adamw(w, g, m, v, *, name):
    if w.ndim == 3:
        rows, _, cols = w.shape
        tm = max(tt for tt in range(1, 65) if rows % tt == 0)
        block, imap, grid = (tm, 1, cols), (lambda i, j: (i, 0, 0)), (rows // tm, 1)
    else:
        rows, cols = w.shape
        tm, tn = rows, cols
        for cand in (256, 128, 64, 32, 16, 8):
            if rows > cand and rows % cand == 0:
                tm = cand
                break
        if tm == rows and rows > 256 and cols % 128 == 0:
            tn = 128
        block, imap, grid = (tm, tn), (lambda i, j: (i, j)), (rows // tm, cols // tn)
    c1 = 1.0 - ADAM_B1 ** ADAM_STEP
    c2 = 1.0 - ADAM_B2 ** ADAM_STEP

    def body(w_ref, g_ref, m_ref, v_ref, d_ref, nm_ref, nv_ref):
        gv = g_ref[...]
        mn = ADAM_B1 * m_ref[...] + (1.0 - ADAM_B1) * gv
        vn = ADAM_B2 * v_ref[...] + (1.0 - ADAM_B2) * (gv * gv)
        d_ref[...] = -ADAM_LR * ((mn / c1) / (jnp.sqrt(vn / c2) + ADAM_EPS) + ADAM_WD * w_ref[...])
        nm_ref[...] = mn
        nv_ref[...] = vn

    spec = pl.BlockSpec(block, imap)
    return _pcall(
        body, name=name, out_shape=(_sds(w.shape, F32),) * 3, grid=grid,
        in_specs=[spec] * 4, out_specs=(spec,) * 3, compiler_params=_cp(),
    )(w, g, m, v)


def _pad_rows(a, rows):
    return a if a.shape[0] == rows else jnp.pad(a, ((0, rows - a.shape[0]),) + ((0, 0),) * (a.ndim - 1))


class _Pack:
    def __init__(self, row_sizes):
        self.rows = list(row_sizes)
        self.padded = [-(-r // 16) * 16 for r in self.rows]
        self.offs = [sum(self.padded[:i]) for i in range(len(self.rows))]
        self.total = sum(self.padded)

    def stack(self, blocks):
        return jnp.concatenate([_pad_rows(b.astype(BF16), p) for b, p in zip(blocks, self.padded)], axis=0)

    def full(self, gathered, i):
        r = self.rows[i]
        return gathered[:, self.offs[i]:self.offs[i] + r, :].reshape(N_DEV * r, gathered.shape[2])

    def for_core(self, full_grads, core):
        parts = []
        for gfull, r, p in zip(full_grads, self.rows, self.padded):
            blk = gfull.reshape(N_DEV // 2, 2, r, gfull.shape[1])
            blk = lax.dynamic_index_in_dim(blk, core, axis=1, keepdims=False).astype(BF16)
            if p != r:
                blk = jnp.pad(blk, ((0, 0), (0, p - r), (0, 0)))
            parts.append(blk)
        return jnp.concatenate(parts, axis=1)

    def shard(self, g_pack, i):
        return g_pack[self.offs[i]:self.offs[i] + self.rows[i], :]


def kernel(x, mem, norm1_g, w_in, conv_w, conv_b, fox_f_bias, fox_q_g, fox_k_g, mem_norm_g, w_mem_kv, xa_q_g, xa_k_g, w_br_conv, w_br_fox, w_br_xa, w_o, norm2_g, w_ffn_in, w_ffn_out, loss_target, m_norm1_g, m_w_in, m_conv_w, m_conv_b, m_fox_f_bias, m_fox_q_g, m_fox_k_g, m_mem_norm_g, m_w_mem_kv, m_xa_q_g, m_xa_k_g, m_w_br_conv, m_w_br_fox, m_w_br_xa, m_w_o, m_norm2_g, m_w_ffn_in, m_w_ffn_out, v_norm1_g, v_w_in, v_conv_w, v_conv_b, v_fox_f_bias, v_fox_q_g, v_fox_k_g, v_mem_norm_g, v_w_mem_kv, v_xa_q_g, v_xa_k_g, v_w_br_conv, v_w_br_fox, v_w_br_xa, v_w_o, v_norm2_g, v_w_ffn_in, v_w_ffn_out):
    nb, s, d = x.shape
    t = nb * s
    nm = mem.shape[1]
    nh = d // HD
    dff = w_ffn_out.shape[1] * N_DEV
    n_in = w_in.shape[2] * N_DEV
    n_in_pad = -(-n_in // 1152) * 1152
    ff_block = (10 * d) // HD
    my_c = lax.axis_index("c")
    my_q = 2 * lax.axis_index("x") + lax.axis_index("y")

    pack_a = _Pack([w_in.shape[2]])
    pack_b = _Pack([w_mem_kv.shape[2], w_br_conv.shape[1], w_br_fox.shape[1], w_br_xa.shape[1], w_o.shape[1],
                    w_ffn_in.shape[2], w_ffn_out.shape[1]])
    gather_1 = _Pack([w_ffn_in.shape[2], w_ffn_out.shape[1]])
    gather_2 = _Pack([w_mem_kv.shape[2], w_br_conv.shape[1], w_br_fox.shape[1], w_br_xa.shape[1], w_o.shape[1]])
    stack_a = pack_a.stack([w_in[0].T])
    stack_1 = gather_1.stack([w_ffn_in[0].T, w_ffn_out[0]])
    stack_2 = gather_2.stack([w_mem_kv[0].T, w_br_conv[0], w_br_fox[0], w_br_xa[0], w_o[0]])
    cw_block = _pad_rows(conv_w[0], 8)
    (gathered_a,) = _comm_only(_AllGather([stack_a]), "all_gather_w_in")
    win_t = _pad_rows(pack_a.full(gathered_a, 0), n_in_pad)

    x2 = x.reshape(t, d)
    mem2 = mem.reshape(nb * nm, d)
    tgt2 = loss_target.reshape(t, d)
    h1 = _rms_fwd(x2, norm1_g, name="rms1_fwd")
    z, (gathered_1,) = _mm(h1, win_t, m=t, n=n_in_pad, k=d, tb=True, tm=1024, tn=1152, name="mm_z", comm=_AllGather([stack_1]))
    qn, kn, vb = _fox_prep(z, fox_q_g, fox_k_g, t=t, d=d)
    f_bias_col = fox_f_bias.reshape(nh, 1)
    ck_rows = _fox_gates(z, f_bias_col, nb=nb, s=s, nh=nh, ff_block=ff_block).reshape(nb * nh, 1, s)
    y_fox, lse, (gathered_2, cw_g) = _fox_fwd(qn, kn, vb, ck_rows, nb=nb, s=s, nh=nh, comm=_AllGather([stack_2, cw_block]))
    wfi_t, wfo = (gather_1.full(gathered_1, i) for i in range(2))
    wkv_t, wbc, wbf, wbx, wo = (gather_2.full(gathered_2, i) for i in range(5))
    conv_w_full = jnp.transpose(cw_g[:, 0:3, :], (1, 0, 2)).reshape(3, d)

    y_conv = _conv_fwd(z, conv_w_full, conv_b, nb=nb, s=s, d=d)
    mem_n = _rms_fwd(mem2, mem_norm_g, name="rms_mem_fwd")
    kv = _mm(mem_n, wkv_t, m=nb * nm, n=2 * d, k=d, tb=True, name="mm_kv")
    xkn, xvb = _xa_prep_kv(kv, xa_k_g, rows=nb * nm, d=d)
    y_xa = _xa_fwd(z, xkn, xvb, xa_q_g, nb=nb, s=s, d=d, nm=nm, q_block=6)
    pa = _mm(y_conv, wbc, m=t, n=d, k=d, name="mm_pa")
    pb = _mm(y_fox, wbf, m=t, n=d, k=d, name="mm_pb")
    pc = _mm(y_xa, wbx, m=t, n=d, k=d, name="mm_pc")
    merged = _merge_fwd(z, pa, pb, pc, t=t, d=d, gate_block=7)
    x1 = _mm(merged, wo, m=t, n=d, k=d, res=x2, name="mm_x1")
    h2 = _rms_fwd(x1, norm2_g, name="rms2_fwd")
    gu = _mm(h2, wfi_t, m=t, n=2 * dff, k=d, tb=True, tm=1024, tn=1408, name="mm_gu")
    act = _swiglu_fwd(gu, t=t, dff=dff)
    y = _mm(act, wfo, m=t, n=d, k=dff, res=x1, tk=dff, name="mm_y")
    dy, loss_local = _loss_head(y, tgt2, t=t, d=d)

    da = _mm(dy, wfo, m=t, n=dff, k=d, tb=True, tn=1408, name="mm_da")
    g_wfo = _mm(act, dy, m=dff, n=d, k=t, ta=True, tm=1408, tk=2048, out_dtype=BF16, name="mm_g_wfo")
    dgate, dup = _swiglu_bwd(gu, da, t=t, dff=dff)
    dh2 = _mm(dgate, wfi_t, m=t, n=d, k=dff, tk=dff, name="mm_dh2_g")
    dh2 = _mm(dup, wfi_t, m=t, n=d, k=dff, tk=dff, b_off=(1, 0), res=dh2, name="mm_dh2_u")
    g_wfi_g = _mm(dgate, h2, m=dff, n=d, k=t, ta=True, tm=1408, tk=2048, out_dtype=BF16, name="mm_g_wfi_g")
    g_wfi_u = _mm(dup, h2, m=dff, n=d, k=t, ta=True, tm=1408, tk=2048, out_dtype=BF16, name="mm_g_wfi_u")
    dx1, g_norm2 = _rms_bwd(dh2, x1, norm2_g, dy, name="rms2_bwd")
    dmerged = _mm(dx1, wo, m=t, n=d, k=d, tb=True, name="mm_dmerged")
    g_wo = _mm(merged, dx1, m=d, n=d, k=t, ta=True, tk=2048, out_dtype=BF16, name="mm_g_wo")

    dz = lax.empty((t, n_in_pad), BF16)
    dz, dpa = _gate_bwd(z, dmerged, pa, dz, t=t, d=d, col_block=7, name="gate_bwd_a")
    dz, dpb = _gate_bwd(z, dmerged, pb, dz, t=t, d=d, col_block=8, name="gate_bwd_b")
    dz, dpc = _gate_bwd(z, dmerged, pc, dz, t=t, d=d, col_block=9, name="gate_bwd_c")
    dy_conv = _mm(dpa, wbc, m=t, n=d, k=d, tb=True, name="mm_dy_conv")
    g_wbc = _mm(y_conv, dpa, m=d, n=d, k=t, ta=True, tk=4096, out_dtype=BF16, name="mm_g_wbc")
    dy_fox = _mm(dpb, wbf, m=t, n=d, k=d, tb=True, name="mm_dy_fox")
    g_wbf = _mm(y_fox, dpb, m=d, n=d, k=t, ta=True, tk=4096, out_dtype=BF16, name="mm_g_wbf")
    dy_xa = _mm(dpc, wbx, m=t, n=d, k=d, tb=True, name="mm_dy_xa")
    g_wbx = _mm(y_xa, dpc, m=d, n=d, k=t, ta=True, tk=4096, out_dtype=BF16, name="mm_g_wbx")

    dz, g_conv_w, g_conv_b = _conv_bwd(z, dy_conv, conv_w_full, conv_b, dz, nb=nb, s=s, d=d)
    dz, dxkn, dxv, g_xa_q = _xa_bwd(z, xkn, xvb, xa_q_g, dy_xa, dz, nb=nb, s=s, d=d, nm=nm, q_block=6)
    dkv, g_xa_k = _xa_post_kv(kv, dxkn, dxv, xa_k_g, rows=nb * nm, d=d)
    g_wkv_t = _mm(dkv, mem_n, m=2 * d, n=d, k=nb * nm, ta=True, out_dtype=BF16, name="mm_g_wkv")
    dmem_n = _mm(dkv, wkv_t, m=nb * nm, n=d, k=2 * d, tk=2 * d, name="mm_dmem")
    _, g_mem_norm = _rms_bwd(dmem_n, mem2, mem_norm_g, None, name="rms_mem_bwd")

    grads_b = [g_wkv_t, g_wbc, g_wbf, g_wbx, g_wo, jnp.concatenate([g_wfi_g, g_wfi_u], axis=0), g_wfo]
    keep_b, send_b = pack_b.for_core(grads_b, my_c), pack_b.for_core(grads_b, 1 - my_c)
    dqn, ds_rows, (from_sibling_b,) = _fox_bwd_dq(qn, kn, vb, ck_rows, dy_fox, y_fox, lse, nb=nb, s=s, nh=nh,
                                                  comm=_SiblingSwap(send_b))
    part_b = _add2(keep_b, from_sibling_b, name="rs_add_sibling_b")
    dkn, dvb, ds_cols, (from_chips_b,) = _fox_bwd_dkv(qn, kn, vb, ck_rows, dy_fox, y_fox, lse, nb=nb, s=s, nh=nh,
                                                     comm=_ChipExchange(part_b))
    g_pack_b = _add4(lax.dynamic_index_in_dim(part_b, my_q, axis=0, keepdims=False), from_chips_b, name="rs_add_chips_b")
    dz, g_fox_q, g_fox_k = _fox_post(z, dqn, dkn, dvb, fox_q_g, fox_k_g, dz, t=t, d=d)
    dz, g_f_bias = _fox_gates_bwd(z, f_bias_col, ds_rows, ds_cols, dz, nb=nb, s=s, nh=nh, ff_block=ff_block)

    g_win_t = _mm(dz, h1, m=n_in_pad, n=d, k=t, ta=True, tm=1152, tk=2048, out_dtype=BF16, name="mm_g_win")
    grads_a = [g_win_t[:n_in]]
    keep_a, send_a = pack_a.for_core(grads_a, my_c), pack_a.for_core(grads_a, 1 - my_c)
    dh1, (from_sibling_a,) = _mm(dz, win_t, m=t, n=d, k=n_in_pad, tk=3456, name="mm_dh1", comm=_SiblingSwap(send_a))
    part_a = _add2(keep_a, from_sibling_a, name="rs_add_sibling_a")
    (from_chips_a,) = _comm_only(_ChipExchange(part_a), "rs_chips_a")
    g_pack_a = _add4(lax.dynamic_index_in_dim(part_a, my_q, axis=0, keepdims=False), from_chips_a, name="rs_add_chips_a",
                     rows_apart=True)
    dx, g_norm1 = _rms_bwd(dh1, x2, norm1_g, dx1, name="rms1_bwd")

    xd = d // XH
    tail = jnp.concatenate(
        [_pad_rows(g_f_bias.reshape(nh), HD).reshape(1, HD), g_fox_q, g_fox_k, g_xa_q, g_xa_k,
         jnp.zeros((1, d - 3 * HD - 2 * xd), F32)], axis=1)
    small = jnp.concatenate([g_norm1, g_norm2, g_conv_w, g_conv_b, g_mem_norm, tail], axis=0)
    small = _all_reduce_small(small)
    gs_conv_w = lax.dynamic_slice_in_dim(small[2:5], (2 * my_q + my_c) * (d // N_DEV), d // N_DEV, axis=1)

    grads = {
        "norm1_g": small[0:1], "w_in": g_pack_a, "conv_w": gs_conv_w, "conv_b": small[5:6],
        "fox_f_bias": small[7:8, 0:nh], "fox_q_g": small[7:8, HD:2 * HD], "fox_k_g": small[7:8, 2 * HD:3 * HD],
        "mem_norm_g": small[6:7], "w_mem_kv": pack_b.shard(g_pack_b, 0), "xa_q_g": small[7:8, 3 * HD:3 * HD + xd],
        "xa_k_g": small[7:8, 3 * HD + xd:3 * HD + 2 * xd], "w_br_conv": pack_b.shard(g_pack_b, 1),
        "w_br_fox": pack_b.shard(g_pack_b, 2), "w_br_xa": pack_b.shard(g_pack_b, 3), "w_o": pack_b.shard(g_pack_b, 4),
        "norm2_g": small[1:2], "w_ffn_in": pack_b.shard(g_pack_b, 5), "w_ffn_out": pack_b.shard(g_pack_b, 6),
    }
    transposed = {"w_in", "w_mem_kv", "w_ffn_in"}
    weights = {
        "norm1_g": (norm1_g, m_norm1_g, v_norm1_g), "w_in": (w_in, m_w_in, v_w_in), "conv_w": (conv_w, m_conv_w, v_conv_w),
        "conv_b": (conv_b, m_conv_b, v_conv_b), "fox_f_bias": (fox_f_bias, m_fox_f_bias, v_fox_f_bias),
        "fox_q_g": (fox_q_g, m_fox_q_g, v_fox_q_g), "fox_k_g": (fox_k_g, m_fox_k_g, v_fox_k_g),
        "mem_norm_g": (mem_norm_g, m_mem_norm_g, v_mem_norm_g), "w_mem_kv": (w_mem_kv, m_w_mem_kv, v_w_mem_kv),
        "xa_q_g": (xa_q_g, m_xa_q_g, v_xa_q_g), "xa_k_g": (xa_k_g, m_xa_k_g, v_xa_k_g),
        "w_br_conv": (w_br_conv, m_w_br_conv, v_w_br_conv), "w_br_fox": (w_br_fox, m_w_br_fox, v_w_br_fox),
        "w_br_xa": (w_br_xa, m_w_br_xa, v_w_br_xa), "w_o": (w_o, m_w_o, v_w_o), "norm2_g": (norm2_g, m_norm2_g, v_norm2_g),
        "w_ffn_in": (w_ffn_in, m_w_ffn_in, v_w_ffn_in), "w_ffn_out": (w_ffn_out, m_w_ffn_out, v_w_ffn_out),
    }
    out_g, out_d, out_m, out_v = [], [], [], []
    for name, (w, m, v) in weights.items():
        shape = w.shape
        if name == "w_in":
            w3, m3, v3 = (a.transpose(2, 0, 1) for a in (w, m, v))
            dl, mn, vn = _adamw(w3, g_pack_a, m3, v3, name="adamw_" + name)
            out_g.append(g_pack_a[:shape[2]].transpose(1, 2, 0))
            out_d.append(dl.transpose(1, 2, 0))
            out_m.append(mn.transpose(1, 2, 0))
            out_v.append(vn.transpose(1, 2, 0))
            continue
        tr = name in transposed

        def as2d(a):
            a = a.reshape(shape[-2], shape[-1])
            return a.T if tr else a

        def back(a):
            return (a.T if tr else a).reshape(shape)

        w2 = as2d(w)
        g2 = grads[name].reshape(w2.shape)
        dl, mn, vn = _adamw(w2, g2, as2d(m), as2d(v), name="adamw_" + name)
        out_g.append(back(g2))
        out_d.append(back(dl))
        out_m.append(back(mn))
        out_v.append(back(vn))

    loss = lax.psum(loss_local, ("x", "y", "c"))
    return (loss, dx.reshape(nb, s, d), *out_g, *out_d, *out_m, *out_v)
```

```python
import math

import jax
import jax.numpy as jnp
from jax import lax
from jax.experimental import pallas as pl
from jax.experimental.pallas import tpu as pltpu

F32, BF16 = jnp.float32, jnp.bfloat16
EPS = 1e-6
HD = 128
XH = 4
N_DEV = 8
MESH = pl.DeviceIdType.MESH
VMEM_LIMIT = 52 * 1024 * 1024
NEG = -1e30
SQRT_HD = math.sqrt(HD)
EXP2_C = math.log2(math.e) / SQRT_HD
LOG2E = math.log2(math.e)
FOX_HP = 4

ADAM_LR, ADAM_B1, ADAM_B2, ADAM_EPS, ADAM_WD, ADAM_STEP = 0.001, 0.9, 0.999, 1e-08, 0.01, 10


def _cp(**kw):
    return pltpu.CompilerParams(vmem_limit_bytes=VMEM_LIMIT, **kw)


def _pcall(body, **kw):
    return pl.pallas_call(body, **kw)


def _sds(shape, dtype):
    return jax.ShapeDtypeStruct(shape, dtype)


HBM_SPEC = pl.BlockSpec(memory_space=pl.ANY)


def _mesh_pos():
    return lax.axis_index("x"), lax.axis_index("y"), lax.axis_index("c")


class _AllGather:
    def __init__(self, blocks):
        self.blocks = list(blocks)
        n = len(self.blocks)
        self.out_shape = [_sds((N_DEV,) + b.shape, b.dtype) for b in self.blocks]
        self.scratch = [pltpu.SemaphoreType.DMA((n, 7)), pltpu.SemaphoreType.DMA((n, 7)), pltpu.SemaphoreType.DMA((n,))]

    def bind(self, ins, outs, sems):
        n = len(ins)
        send_sems, recv_sems, local_sems = sems
        x, y, c = _mesh_pos()
        me, sibling = (x, y, c), (x, y, 1 - c)
        chips = [(1 - x, y), (x, 1 - y), (1 - x, 1 - y)]

        def slab(t, dev):
            return outs[t].at[4 * dev[0] + 2 * dev[1] + dev[2]]

        def copy(t, k, block, to, src=None):
            dst = slab(t, block)
            return pltpu.make_async_remote_copy(
                src_ref=dst if src is None else src, dst_ref=dst, send_sem=send_sems.at[t, k], recv_sem=recv_sems.at[t, k],
                device_id=to, device_id_type=MESH)

        mine = [pltpu.make_async_copy(ins[t], slab(t, me), local_sems.at[t]) for t in range(n)]
        first = []
        for t in range(n):
            first.append(copy(t, 0, me, sibling, src=ins[t]))
            first += [copy(t, 1 + j, me, (*chip, c), src=ins[t]) for j, chip in enumerate(chips)]
        passed = [copy(t, 4 + j, (*chip, c), sibling) for j, chip in enumerate(chips) for t in range(n)]

        def start():
            for cp in mine + first:
                cp.start()

        def mid():
            for j, chip in enumerate(chips):
                for t in range(n):
                    copy(t, 1 + j, (*chip, c), me).wait_recv()
                    passed[j * n + t].start()

        def finish():
            for t in range(n):
                copy(t, 0, sibling, me).wait_recv()
                for j, chip in enumerate(chips):
                    copy(t, 4 + j, (*chip, 1 - c), me).wait_recv()
            for cp in first + passed:
                cp.wait_send()
            for cp in mine:
                cp.wait()

        return start, mid, finish


class _SiblingSwap:
    def __init__(self, send):
        self.blocks = [send]
        self.out_shape = [_sds(send.shape, send.dtype)]
        self.scratch = [pltpu.SemaphoreType.DMA, pltpu.SemaphoreType.DMA]

    def bind(self, ins, outs, sems):
        x, y, c = _mesh_pos()
        cp = pltpu.make_async_remote_copy(src_ref=ins[0], dst_ref=outs[0], send_sem=sems[0], recv_sem=sems[1],
                                          device_id=(x, y, 1 - c), device_id_type=MESH)
        return cp.start, (lambda: None), cp.wait


class _ChipExchange:
    def __init__(self, part):
        self.blocks = [part]
        self.out_shape = [_sds((3,) + part.shape[1:], part.dtype)]
        self.scratch = [pltpu.SemaphoreType.DMA((3,)), pltpu.SemaphoreType.DMA((3,))]

    def bind(self, ins, outs, sems):
        x, y, c = _mesh_pos()
        chips = [(1 - x, y), (x, 1 - y), (1 - x, 1 - y)]
        cps = [
            pltpu.make_async_remote_copy(src_ref=ins[0].at[2 * cx + cy], dst_ref=outs[0].at[j], send_sem=sems[0].at[j],
                                         recv_sem=sems[1].at[j], device_id=(cx, cy, c), device_id_type=MESH)
            for j, (cx, cy) in enumerate(chips)
        ]

        def start():
            for cp in cps:
                cp.start()

        def finish():
            for cp in cps:
                cp.wait()

        return start, (lambda: None), finish


def _comm_lists(comm):
    if comm is None:
        return [], [], [], []
    n = len(comm.blocks)
    return list(comm.blocks), [HBM_SPEC] * n, list(comm.out_shape), list(comm.scratch)


def _split_refs(refs, n_pre, n_in, n_cin, n_out, n_cout, n_scr):
    cuts = [n_pre, n_in, n_cin, n_out, n_cout, n_scr]
    out, at = [], 0
    for c in cuts:
        out.append(refs[at:at + c])
        at += c
    out.append(refs[at:])
    return out


def _run_comm(comm, cin, cout, csem, step, n_steps, when="start"):
    if comm is None:
        return
    start, mid, finish = comm.bind(cin, cout, csem)
    if when == "start":
        pl.when(step == 0)(start)
    else:
        pl.when(step == max(n_steps - 2, 0))(mid)
        pl.when(step == n_steps - 1)(finish)


def _comm_only(comm, name):
    ins, in_specs, out_shape, scratch = _comm_lists(comm)
    n = len(ins)

    def body(*refs):
        start, mid, finish = comm.bind(refs[:n], refs[n:2 * n], refs[2 * n:])
        start()
        mid()
        finish()

    outs = _pcall(body, name=name, out_shape=tuple(out_shape), in_specs=in_specs, out_specs=tuple([HBM_SPEC] * n),
                  scratch_shapes=scratch, compiler_params=_cp())(*ins)
    return list(outs)


def _all_reduce_small(v):
    rows, cols = v.shape

    def body(v_ref, o_ref, slots, send_sems, recv_sems):
        x, y, c = _mesh_pos()
        me = 4 * x + 2 * y + c
        slots[me] = v_ref[...]
        cps = []
        for k in range(1, N_DEV):
            px, py, pc = x ^ (k >> 2), y ^ ((k >> 1) & 1), c ^ (k & 1)
            cps.append(pltpu.make_async_remote_copy(
                src_ref=v_ref, dst_ref=slots.at[me], send_sem=send_sems.at[k - 1], recv_sem=recv_sems.at[k - 1],
                device_id=(px, py, pc), device_id_type=MESH))
        for cp in cps:
            cp.start()
        for cp in cps:
            cp.wait()
        acc = slots[0]
        for k in range(1, N_DEV):
            acc = acc + slots[k]
        o_ref[...] = acc

    return _pcall(
        body, name="all_reduce_small", out_shape=_sds((rows, cols), F32),
        in_specs=[pl.BlockSpec(memory_space=pltpu.VMEM)], out_specs=pl.BlockSpec(memory_space=pltpu.VMEM),
        scratch_shapes=[pltpu.VMEM((N_DEV, rows, cols), F32), pltpu.SemaphoreType.DMA((7,)), pltpu.SemaphoreType.DMA((7,))],
        compiler_params=_cp(),
    )(v)


def _row_tile(rows, cap=1040):
    return max(tt for tt in range(16, cap + 1, 16) if rows % tt == 0)


def _add2(a, b, *, name):
    nq, rows, cols = a.shape
    tm = _row_tile(rows)

    def body(a_ref, b_ref, o_ref):
        o_ref[...] = (a_ref[...].astype(F32) + b_ref[...].astype(F32)).astype(BF16)

    spec = pl.BlockSpec((1, tm, cols), lambda q, i: (q, i, 0))
    return _pcall(body, name=name, out_shape=_sds(a.shape, BF16), grid=(nq, rows // tm),
                  in_specs=[spec, spec], out_specs=spec, compiler_params=_cp())(a, b)


def _add4(own, recv, *, name, rows_apart=False):
    rows, cols = own.shape
    tm = _row_tile(rows, 256 if rows_apart else 1040)

    def body(o_ref, r_ref, g_ref):
        tot = ((o_ref[...].astype(F32) + r_ref[0].astype(F32)) + r_ref[1].astype(F32)) + r_ref[2].astype(F32)
        if rows_apart:
            g_ref[:, 0, :] = tot
        else:
            g_ref[...] = tot

    if rows_apart:
        out_shape, out_spec = _sds((rows, 1, cols), F32), pl.BlockSpec((tm, 1, cols), lambda i: (i, 0, 0))
    else:
        out_shape, out_spec = _sds((rows, cols), F32), pl.BlockSpec((tm, cols), lambda i: (i, 0))
    return _pcall(
        body, name=name, out_shape=out_shape, grid=(rows // tm,),
        in_specs=[pl.BlockSpec((tm, cols), lambda i: (i, 0)), pl.BlockSpec((3, tm, cols), lambda i: (0, i, 0))],
        out_specs=out_spec, compiler_params=_cp(),
    )(own, recv)


def _mm(a, b, *, m, n, k, name, ta=False, tb=False, b_off=(0, 0), res=None, out_dtype=F32, tm=512, tn=1024, tk=1024,
        comm=None):
    tm, tn, tk = min(tm, m), min(tn, n), min(tk, k)
    assert m % tm == 0 and n % tn == 0 and k % tk == 0, (name, m, n, k, tm, tn, tk)
    nk = k // tk
    grid = (m // tm, n // tn, nk)
    bo0, bo1 = b_off
    if ta:
        a_spec = pl.BlockSpec((tk, tm), lambda i, j, kk: (kk, i))
    else:
        a_spec = pl.BlockSpec((tm, tk), lambda i, j, kk: (i, kk))
    if tb:
        b_spec = pl.BlockSpec((tn, tk), lambda i, j, kk: (j + bo0, kk + bo1))
    else:
        b_spec = pl.BlockSpec((tk, tn), lambda i, j, kk: (kk + bo0, j + bo1))
    o_spec = pl.BlockSpec((tm, tn), lambda i, j, kk: (i, j))
    dn = (((0 if ta else 1,), (1 if tb else 0,)), ((), ()))
    has_res = res is not None
    c_ins, c_in_specs, c_out_shape, c_scratch = _comm_lists(comm)
    n_in = 3 if has_res else 2
    n_scr = 1 if nk > 1 else 0

    def body(*refs):
        _, ins, cin, outs, cout, scr, csem = _split_refs(refs, 0, n_in, len(c_ins), 1, len(c_ins), n_scr)
        a_ref, b_ref = ins[0], ins[1]
        r_ref = ins[2] if has_res else None
        o_ref = outs[0]
        step = (pl.program_id(0) * grid[1] + pl.program_id(1)) * nk + pl.program_id(2)
        _run_comm(comm, cin, cout, csem, step, grid[0] * grid[1] * nk, "start")
        p = lax.dot_general(a_ref[...].astype(BF16), b_ref[...].astype(BF16), dn, preferred_element_type=F32)

        def finish(acc):
            if has_res:
                acc = acc + r_ref[...]
            o_ref[...] = acc.astype(out_dtype)

        if nk == 1:
            finish(p)
        else:
            acc_ref = scr[0]
            kk = pl.program_id(2)

            @pl.when(kk == 0)
            def _():
                acc_ref[...] = p

            @pl.when(kk > 0)
            def _():
                acc_ref[...] += p

            @pl.when(kk == nk - 1)
            def _():
                finish(acc_ref[...])

        _run_comm(comm, cin, cout, csem, step, grid[0] * grid[1] * nk, "end")

    ins = [a, b] + ([res] if has_res else [])
    in_specs = [a_spec, b_spec] + ([o_spec] if has_res else [])
    sem = ("arbitrary",) * 3 if comm is not None else ("parallel", "parallel", "arbitrary")
    outs = _pcall(
        body, name=name, out_shape=tuple([_sds((m, n), out_dtype)] + c_out_shape), grid=grid,
        in_specs=in_specs + c_in_specs, out_specs=tuple([o_spec] + [HBM_SPEC] * len(c_ins)),
        scratch_shapes=([pltpu.VMEM((tm, tn), F32)] if nk > 1 else []) + c_scratch,
        compiler_params=_cp(dimension_semantics=sem),
    )(*(ins + c_ins))
    return outs[0] if comm is None else (outs[0], list(outs[1:]))


def _rms_fwd(x, g, *, name, tm=512):
    t, d = x.shape
    tm = min(tm, t)

    def body(x_ref, g_ref, h_ref):
        xv = x_ref[...]
        r = lax.rsqrt(jnp.mean(xv * xv, axis=-1, keepdims=True) + EPS)
        h_ref[...] = (xv * r * g_ref[...]).astype(BF16)

    return _pcall(
        body, name=name, out_shape=_sds((t, d), BF16), grid=(t // tm,),
        in_specs=[pl.BlockSpec((tm, d), lambda i: (i, 0)), pl.BlockSpec((1, d), lambda i: (0, 0))],
        out_specs=pl.BlockSpec((tm, d), lambda i: (i, 0)), compiler_params=_cp(),
    )(x, g)


def _rms_bwd(dh, x, g, res, *, name, tm=512):
    t, d = x.shape
    tm = min(tm, t)
    has_res = res is not None

    def body(*refs):
        dh_ref, x_ref, g_ref = refs[:3]
        r_ref = refs[3] if has_res else None
        dx_ref, dg_ref = refs[-2], refs[-1]
        xv = x_ref[...]
        r = lax.rsqrt(jnp.mean(xv * xv, axis=-1, keepdims=True) + EPS)
        xh = xv * r
        dhv = dh_ref[...]
        gd = dhv * g_ref[...]
        dx = r * (gd - xh * jnp.mean(gd * xh, axis=-1, keepdims=True))
        if has_res:
            dx = dx + r_ref[...]
        dx_ref[...] = dx

        @pl.when(pl.program_id(0) == 0)
        def _():
            dg_ref[...] = jnp.zeros_like(dg_ref)

        dg_ref[...] += jnp.sum(dhv * xh, axis=0, keepdims=True)

    row = pl.BlockSpec((tm, d), lambda i: (i, 0))
    vec = pl.BlockSpec((1, d), lambda i: (0, 0))
    return _pcall(
        body, name=name, out_shape=(_sds((t, d), F32), _sds((1, d), F32)), grid=(t // tm,),
        in_specs=[row, row, vec] + ([row] if has_res else []), out_specs=(row, vec),
        compiler_params=_cp(dimension_semantics=("arbitrary",)),
    )(*([dh, x, g] + ([res] if has_res else [])))


def _conv_fwd(z, conv_w, conv_b, *, nb, s, d, tc=256):
    nc = d // tc

    def body(cb_ref, cc_ref, cv_ref, w_ref, b_ref, y_ref):
        u = cc_ref[...] * cv_ref[...]
        row = lax.broadcasted_iota(jnp.int32, u.shape, 0)
        u1 = jnp.where(row >= 1, pltpu.roll(u, 1, axis=0), 0.0)
        u2 = jnp.where(row >= 2, pltpu.roll(u, 2, axis=0), 0.0)
        w = w_ref[...]
        y = w[2:3, :] * u + w[1:2, :] * u1 + w[0:1, :] * u2 + b_ref[...]
        y_ref[...] = (cb_ref[...] * y).astype(BF16)

    def part(p):
        return pl.BlockSpec((s, tc), lambda b, j: (b, p * nc + j))

    return _pcall(
        body, name="conv_fwd", out_shape=_sds((nb * s, d), BF16), grid=(nb, nc),
        in_specs=[part(0), part(1), part(2), pl.BlockSpec((3, tc), lambda b, j: (0, j)), pl.BlockSpec((1, tc), lambda b, j: (0, j))],
        out_specs=pl.BlockSpec((s, tc), lambda b, j: (b, j)), compiler_params=_cp(),
    )(z, z, z, conv_w, conv_b)


def _conv_bwd(z, dy, conv_w, conv_b, dz, *, nb, s, d, tc=256):
    nc = d // tc

    def body(cb_ref, cc_ref, cv_ref, dy_ref, w_ref, b_ref, dz_in, dz_ref, dw_ref, db_ref, stash):
        b_i, p = pl.program_id(1), pl.program_id(2)

        @pl.when(p == 0)
        def _():
            cc, cv = cc_ref[...], cv_ref[...]
            u = cc * cv
            n = u.shape[0]
            row = lax.broadcasted_iota(jnp.int32, u.shape, 0)
            u1 = jnp.where(row >= 1, pltpu.roll(u, 1, axis=0), 0.0)
            u2 = jnp.where(row >= 2, pltpu.roll(u, 2, axis=0), 0.0)
            w = w_ref[...]
            y = w[2:3, :] * u + w[1:2, :] * u1 + w[0:1, :] * u2 + b_ref[...]
            dyv = dy_ref[...]
            dconv = dyv * cb_ref[...]
            g1 = jnp.where(row < n - 1, pltpu.roll(dconv, n - 1, axis=0), 0.0)
            g2 = jnp.where(row < n - 2, pltpu.roll(dconv, n - 2, axis=0), 0.0)
            du = w[2:3, :] * dconv + w[1:2, :] * g1 + w[0:1, :] * g2
            stash[0] = (dyv * y).astype(BF16)
            stash[1] = (du * cv).astype(BF16)
            stash[2] = (du * cc).astype(BF16)
            dws = [jnp.sum(dconv * u2, axis=0, keepdims=True), jnp.sum(dconv * u1, axis=0, keepdims=True),
                   jnp.sum(dconv * u, axis=0, keepdims=True)]
            db = jnp.sum(dconv, axis=0, keepdims=True)

            @pl.when(b_i == 0)
            def _():
                for r in range(3):
                    dw_ref[r:r + 1, :] = dws[r]
                db_ref[...] = db

            @pl.when(b_i > 0)
            def _():
                for r in range(3):
                    dw_ref[r:r + 1, :] += dws[r]
                db_ref[...] += db

        dz_ref[...] = stash[p]

    return _pcall(
        body, name="conv_bwd",
        out_shape=(_sds(dz.shape, dz.dtype), _sds((3, d), F32), _sds((1, d), F32)), grid=(nc, nb, 3),
        in_specs=[
            pl.BlockSpec((s, tc), lambda j, b, p: (b, j)), pl.BlockSpec((s, tc), lambda j, b, p: (b, nc + j)),
            pl.BlockSpec((s, tc), lambda j, b, p: (b, 2 * nc + j)), pl.BlockSpec((s, tc), lambda j, b, p: (b, j)),
            pl.BlockSpec((3, tc), lambda j, b, p: (0, j)), pl.BlockSpec((1, tc), lambda j, b, p: (0, j)),
            HBM_SPEC,
        ],
        out_specs=(pl.BlockSpec((s, tc), lambda j, b, p: (b, p * nc + j)),
                   pl.BlockSpec((3, tc), lambda j, b, p: (0, j)), pl.BlockSpec((1, tc), lambda j, b, p: (0, j))),
        scratch_shapes=[pltpu.VMEM((3, s, tc), BF16)],
        input_output_aliases={6: 0},
        compiler_params=_cp(dimension_semantics=("arbitrary", "arbitrary", "arbitrary")),
    )(z, z, z, dy, conv_w, conv_b, dz)


def _head_rms(xv, g):
    r = lax.rsqrt(jnp.mean(xv * xv, axis=-1, keepdims=True) + EPS)
    return xv * r * g


def _head_rms_bwd(xv, g, dy):
    r = lax.rsqrt(jnp.mean(xv * xv, axis=-1, keepdims=True) + EPS)
    xh = xv * r
    gd = dy * g
    dx = r * (gd - xh * jnp.mean(gd * xh, axis=-1, keepdims=True))
    return dx, jnp.sum(dy * xh, axis=0, keepdims=True)


def _fox_prep(z, q_g, k_g, *, t, d, tm=256):
    nh = d // HD

    def body(z_ref, qg_ref, kg_ref, q_ref, k_ref, v_ref):
        qg, kg = qg_ref[...], kg_ref[...]
        for h in range(nh):
            c0 = h * HD
            q_ref[:, c0:c0 + HD] = _head_rms(z_ref[:, c0:c0 + HD], qg).astype(BF16)
            k_ref[:, c0:c0 + HD] = _head_rms(z_ref[:, d + c0:d + c0 + HD], kg).astype(BF16)
        v_ref[...] = z_ref[:, 2 * d:3 * d].astype(BF16)

    o = pl.BlockSpec((tm, d), lambda i: (i, 0))
    g = pl.BlockSpec((1, HD), lambda i: (0, 0))
    return _pcall(
        body, name="fox_prep", out_shape=(_sds((t, d), BF16),) * 3, grid=(t // tm,),
        in_specs=[pl.BlockSpec((tm, 3 * d), lambda i: (i, 1)), g, g], out_specs=(o, o, o), compiler_params=_cp(),
    )(z, q_g, k_g)


def _lane_cumsum(v, reverse=False):
    n = v.shape[1]
    lane = lax.broadcasted_iota(jnp.int32, v.shape, 1)
    sh = 1
    while sh < n:
        if reverse:
            v = v + jnp.where(lane < n - sh, pltpu.roll(v, n - sh, axis=1), 0.0)
        else:
            v = v + jnp.where(lane >= sh, pltpu.roll(v, sh, axis=1), 0.0)
        sh *= 2
    return v


def _fox_gates(z, f_bias_col, *, nb, s, nh, ff_block):
    def body(z_ref, b_ref, c_ref):
        ff = z_ref[...].T[0:nh, :] + b_ref[...]
        lf = jnp.minimum(ff, 0.0) - jnp.log(1.0 + jnp.exp(-jnp.abs(ff)))
        c_ref[0] = _lane_cumsum(lf) * SQRT_HD

    return _pcall(
        body, name="fox_gates", out_shape=_sds((nb, nh, s), F32), grid=(nb,),
        in_specs=[pl.BlockSpec((s, HD), lambda b: (b, ff_block)), pl.BlockSpec((nh, 1), lambda b: (0, 0))],
        out_specs=pl.BlockSpec((1, nh, s), lambda b: (b, 0, 0)), compiler_params=_cp(),
    )(z, f_bias_col)


def _fox_gates_bwd(z, f_bias_col, row_sums, col_sums, dz, *, nb, s, nh, ff_block):
    def body(z_ref, b_ref, rs_ref, cs_ref, dz_in, dz_ref, db_ref, dc_sc):
        b_i, h = pl.program_id(0), pl.program_id(1)
        dc_sc[pl.ds(h, 1), :] = (rs_ref[...] - cs_ref[...]).T[0:1, :]

        @pl.when(h == nh - 1)
        def _():
            ff = z_ref[...].T[0:nh, :] + b_ref[...]
            dlf = _lane_cumsum(dc_sc[...], reverse=True)
            dff = dlf * (1.0 / (1.0 + jnp.exp(ff)))
            pad = jnp.concatenate([dff, jnp.zeros((HD - nh, s), F32)], axis=0)
            dz_ref[...] = pad.T.astype(BF16)
            dbv = jnp.sum(dff, axis=1, keepdims=True)

            @pl.when(b_i == 0)
            def _():
                db_ref[...] = dbv

            @pl.when(b_i > 0)
            def _():
                db_ref[...] += dbv

    hs = pl.BlockSpec((s, HD), lambda b, h: (b, h))
    zs = pl.BlockSpec((s, HD), lambda b, h: (b, ff_block))
    return _pcall(
        body, name="fox_gates_bwd", out_shape=(_sds(dz.shape, dz.dtype), _sds((nh, 1), F32)), grid=(nb, nh),
        in_specs=[zs, pl.BlockSpec((nh, 1), lambda b, h: (0, 0)), hs, hs, HBM_SPEC],
        out_specs=(zs, pl.BlockSpec((nh, 1), lambda b, h: (0, 0))),
        scratch_shapes=[pltpu.VMEM((nh, s), F32)],
        input_output_aliases={4: 0}, compiler_params=_cp(dimension_semantics=("arbitrary", "arbitrary")),
    )(z, f_bias_col, row_sums, col_sums, dz)


def _fox_pairs(nq, key_major):
    if key_major:
        pairs = [(i, j) for j in range(nq) for i in range(j, nq)]
    else:
        pairs = [(i, j) for i in range(nq) for j in range(i + 1)]
    return jnp.array([p[0] for p in pairs], jnp.int32), jnp.array([p[1] for p in pairs], jnp.int32)


def _with_ones(x):
    return jnp.concatenate([x, jnp.ones_like(x)], axis=1)


def _fox_specs(nb, nh, nq, tq, tk, hp):
    qs = pl.BlockSpec((tq, hp * HD), lambda b, h, p, it, jt: (b * nq + it[p], h))
    ks = pl.BlockSpec((tk, hp * HD), lambda b, h, p, it, jt: (b * nq + jt[p], h))
    cs = pl.BlockSpec((hp, 1, tk), lambda b, h, p, it, jt: (b * (nh // hp) + h, 0, jt[p]))
    return qs, ks, cs


def _fox_raw(qv, kv, ckv, diag, tq, tk):
    sc = lax.dot_general(qv, kv, (((1,), (1,)), ((), ())), preferred_element_type=F32) - ckv
    if diag:
        rows = lax.broadcasted_iota(jnp.int32, (tq, tk), 0)
        cols = lax.broadcasted_iota(jnp.int32, (tq, tk), 1)
        sc = jnp.where(rows >= cols, sc, NEG)
    return sc


def _fox_probs(qv, kv, ckv, lse_col, diag, tq, tk):
    sc = _fox_raw(qv, kv, ckv, diag, tq, tk)
    return jnp.exp2(sc * EXP2_C - lse_col * LOG2E)


def _fox_call(body, *, name, ins, in_specs, out_shape, out_specs, scratch, grid, tables, comm):
    c_ins, c_in_specs, c_out_shape, c_scratch = _comm_lists(comm)
    n_in, n_out, n_scr, nc = len(ins), len(out_shape), len(scratch), len(c_ins)
    n_steps = grid[0] * grid[1] * grid[2]

    def wrapped(*refs):
        pre, rin, cin, rout, cout, scr, csem = _split_refs(refs, 2, n_in, nc, n_out, nc, n_scr)
        step = (pl.program_id(0) * grid[1] + pl.program_id(1)) * grid[2] + pl.program_id(2)
        _run_comm(comm, cin, cout, csem, step, n_steps, "start")
        body(*pre, *rin, *rout, *scr)
        _run_comm(comm, cin, cout, csem, step, n_steps, "end")

    sem = ("arbitrary",) * 3 if comm is not None else ("parallel", "parallel", "arbitrary")
    outs = _pcall(
        wrapped, name=name, out_shape=tuple(list(out_shape) + c_out_shape),
        grid_spec=pltpu.PrefetchScalarGridSpec(
            num_scalar_prefetch=2, grid=grid, in_specs=list(in_specs) + c_in_specs,
            out_specs=tuple(list(out_specs) + [HBM_SPEC] * nc), scratch_shapes=list(scratch) + c_scratch),
        compiler_params=_cp(dimension_semantics=sem),
    )(*tables, *ins, *c_ins)
    return list(outs[:n_out]), list(outs[n_out:])


def _fox_fwd(q, k, v, cks, *, nb, s, nh, tq=512, comm=None):
    tk = tq
    nq = s // tq
    t = nb * s
    hp = FOX_HP
    it, jt = _fox_pairs(nq, key_major=False)

    def body(it_ref, jt_ref, q_ref, k_ref, v_ref, c_ref, o_ref, lse_ref, m_sc, acc_sc):
        p_id = pl.program_id(2)
        i, j = it_ref[p_id], jt_ref[p_id]

        @pl.when(j == 0)
        def _():
            m_sc[...] = jnp.full_like(m_sc, NEG)
            acc_sc[...] = jnp.zeros_like(acc_sc)

        def step(diag):
            for hh in range(hp):
                cols = slice(hh * HD, (hh + 1) * HD)
                sc = _fox_raw(q_ref[:, cols], k_ref[:, cols], c_ref[hh], diag, tq, tk)
                m_old = m_sc[hh]
                m_new = jnp.maximum(m_old, jnp.max(sc, axis=-1, keepdims=True))
                alpha = jnp.exp2((m_old - m_new) * EXP2_C)
                p = jnp.exp2((sc - m_new) * EXP2_C).astype(BF16)
                acc = alpha * acc_sc[hh] + jnp.dot(p, _with_ones(v_ref[:, cols]), preferred_element_type=F32)
                if diag:
                    l = acc[:, HD:2 * HD]
                    o_ref[:, cols] = (acc[:, 0:HD] / l).astype(BF16)
                    lse_ref[:, cols] = m_new * (1.0 / SQRT_HD) + jnp.log(l)
                else:
                    acc_sc[hh] = acc
                    m_sc[hh] = m_new

        @pl.when(j < i)
        def _():
            step(False)

        @pl.when(j == i)
        def _():
            step(True)

    qs, ks, cs = _fox_specs(nb, nh, nq, tq, tk, hp)
    (o, lse), c_out = _fox_call(
        body, name="fox_fwd", ins=[q, k, v, cks], in_specs=[qs, ks, ks, cs],
        out_shape=[_sds((t, nh * HD), BF16), _sds((t, nh * HD), F32)], out_specs=[qs, qs],
        scratch=[pltpu.VMEM((hp, tq, 1), F32), pltpu.VMEM((hp, tq, 2 * HD), F32)],
        grid=(nb, nh // hp, int(it.shape[0])), tables=(it, jt), comm=comm)
    return o, lse, c_out


def _fox_bwd_dq(q, k, v, cks, do, o, lse, *, nb, s, nh, tq=512, comm=None):
    tk = tq
    nq = s // tq
    t = nb * s
    hp = FOX_HP
    it, jt = _fox_pairs(nq, key_major=False)

    def body(it_ref, jt_ref, q_ref, k_ref, v_ref, c_ref, do_ref, o_ref, lse_ref, dq_ref, rs_ref, acc_sc, dl_sc):
        p_id = pl.program_id(2)
        i, j = it_ref[p_id], jt_ref[p_id]

        @pl.when(j == 0)
        def _():
            acc_sc[...] = jnp.zeros_like(acc_sc)
            for hh in range(hp):
                cols = slice(hh * HD, (hh + 1) * HD)
                dl_sc[hh] = jnp.sum(do_ref[:, cols] * o_ref[:, cols].astype(F32), axis=-1, keepdims=True)

        def step(diag):
            for hh in range(hp):
                cols = slice(hh * HD, (hh + 1) * HD)
                kv = k_ref[:, cols]
                p = _fox_probs(q_ref[:, cols], kv, c_ref[hh], lse_ref[:, hh * HD:hh * HD + 1], diag, tq, tk)
                dp = lax.dot_general(do_ref[:, cols].astype(BF16), v_ref[:, cols], (((1,), (1,)), ((), ())),
                                     preferred_element_type=F32)
                ds = (p * (dp - dl_sc[hh])).astype(BF16)
                acc = acc_sc[hh] + jnp.dot(ds, _with_ones(kv), preferred_element_type=F32)
                if diag:
                    dq_ref[:, cols] = acc[:, 0:HD] * (1.0 / SQRT_HD)
                    rs_ref[:, cols] = acc[:, HD:2 * HD]
                else:
                    acc_sc[hh] = acc

        @pl.when(j < i)
        def _():
            step(False)

        @pl.when(j == i)
        def _():
            step(True)

    qs, ks, cs = _fox_specs(nb, nh, nq, tq, tk, hp)
    (dq, rs), c_out = _fox_call(
        body, name="fox_bwd_dq", ins=[q, k, v, cks, do, o, lse], in_specs=[qs, ks, ks, cs, qs, qs, qs],
        out_shape=[_sds((t, nh * HD), F32), _sds((t, nh * HD), F32)], out_specs=[qs, qs],
        scratch=[pltpu.VMEM((hp, tq, 2 * HD), F32), pltpu.VMEM((hp, tq, 1), F32)],
        grid=(nb, nh // hp, int(it.shape[0])), tables=(it, jt), comm=comm)
    return dq, rs, c_out


def _fox_bwd_dkv(q, k, v, cks, do, o, lse, *, nb, s, nh, tq=512, comm=None):
    tk = tq
    nq = s // tq
    t = nb * s
    hp = FOX_HP
    it, jt = _fox_pairs(nq, key_major=True)

    def body(it_ref, jt_ref, q_ref, k_ref, v_ref, c_ref, do_ref, o_ref, lse_ref, dk_ref, dv_ref, cs_ref, dk_sc, dv_sc):
        p_id = pl.program_id(2)
        i, j = it_ref[p_id], jt_ref[p_id]

        def step(diag):
            for hh in range(hp):
                cols = slice(hh * HD, (hh + 1) * HD)
                qv = q_ref[:, cols]
                p = _fox_probs(qv, k_ref[:, cols], c_ref[hh], lse_ref[:, hh * HD:hh * HD + 1], diag, tq, tk)
                dov = do_ref[:, cols]
                dob = dov.astype(BF16)
                dvp = lax.dot_general(p.astype(BF16), dob, (((0,), (0,)), ((), ())), preferred_element_type=F32)
                dp = lax.dot_general(dob, v_ref[:, cols], (((1,), (1,)), ((), ())), preferred_element_type=F32)
                delta = jnp.sum(dov * o_ref[:, cols].astype(F32), axis=-1, keepdims=True)
                ds = (p * (dp - delta)).astype(BF16)
                dkp = lax.dot_general(ds, _with_ones(qv), (((0,), (0,)), ((), ())), preferred_element_type=F32)
                if diag:
                    dk_sc[hh] = dkp
                    dv_sc[hh] = dvp
                else:
                    dk_sc[hh] += dkp
                    dv_sc[hh] += dvp

        @pl.when(i == j)
        def _():
            step(True)

        @pl.when(i > j)
        def _():
            step(False)

        @pl.when(i == nq - 1)
        def _():
            for hh in range(hp):
                cols = slice(hh * HD, (hh + 1) * HD)
                dk_ref[:, cols] = dk_sc[hh, :, 0:HD] * (1.0 / SQRT_HD)
                cs_ref[:, cols] = dk_sc[hh, :, HD:2 * HD]
                dv_ref[:, cols] = dv_sc[hh].astype(BF16)

    qs, ks, cs = _fox_specs(nb, nh, nq, tq, tk, hp)
    (dk, dv, csum), c_out = _fox_call(
        body, name="fox_bwd_dkv", ins=[q, k, v, cks, do, o, lse], in_specs=[qs, ks, ks, cs, qs, qs, qs],
        out_shape=[_sds((t, nh * HD), F32), _sds((t, nh * HD), BF16), _sds((t, nh * HD), F32)], out_specs=[ks, ks, ks],
        scratch=[pltpu.VMEM((hp, tk, 2 * HD), F32), pltpu.VMEM((hp, tk, HD), F32)],
        grid=(nb, nh // hp, int(it.shape[0])), tables=(it, jt), comm=comm)
    return dk, dv, csum, c_out


def _fox_post(z, dqn, dkn, dv, q_g, k_g, dz, *, t, d, tm=256):
    nh = d // HD

    def body(z_ref, dq_ref, dk_ref, dv_ref, qg_ref, kg_ref, dz_in, dz_ref, dqg_ref, dkg_ref):
        qg, kg = qg_ref[...], kg_ref[...]
        aq = jnp.zeros((1, HD), F32)
        ak = jnp.zeros((1, HD), F32)
        for h in range(nh):
            c0 = h * HD
            dx, sg = _head_rms_bwd(z_ref[:, c0:c0 + HD], qg, dq_ref[:, c0:c0 + HD])
            dz_ref[:, c0:c0 + HD] = dx.astype(BF16)
            aq = aq + sg
            dx, sg = _head_rms_bwd(z_ref[:, d + c0:d + c0 + HD], kg, dk_ref[:, c0:c0 + HD])
            dz_ref[:, d + c0:d + c0 + HD] = dx.astype(BF16)
            ak = ak + sg
        dz_ref[:, 2 * d:3 * d] = dv_ref[...]

        @pl.when(pl.program_id(0) == 0)
        def _():
            dqg_ref[...] = aq
            dkg_ref[...] = ak

        @pl.when(pl.program_id(0) > 0)
        def _():
            dqg_ref[...] += aq
            dkg_ref[...] += ak

    o = pl.BlockSpec((tm, d), lambda i: (i, 0))
    g = pl.BlockSpec((1, HD), lambda i: (0, 0))
    z3 = pl.BlockSpec((tm, 3 * d), lambda i: (i, 1))
    return _pcall(
        body, name="fox_post", out_shape=(_sds(dz.shape, dz.dtype), _sds((1, HD), F32), _sds((1, HD), F32)), grid=(t // tm,),
        in_specs=[z3, o, o, o, g, g, HBM_SPEC], out_specs=(z3, g, g),
        input_output_aliases={6: 0}, compiler_params=_cp(dimension_semantics=("arbitrary",)),
    )(z, dqn, dkn, dv, q_g, k_g, dz)


def _xa_prep_kv(kv, k_g, *, rows, d):
    xd = d // XH

    def body(kv_ref, g_ref, k_ref, v_ref):
        g = g_ref[...]
        for h in range(XH):
            c0 = h * xd
            k_ref[:, c0:c0 + xd] = _head_rms(kv_ref[:, c0:c0 + xd], g).astype(BF16)
        v_ref[...] = kv_ref[:, d:2 * d].astype(BF16)

    return _pcall(body, name="xa_prep_kv", out_shape=(_sds((rows, d), BF16),) * 2, compiler_params=_cp())(kv, k_g)


def _xa_fwd(z, kn, vb, q_g, *, nb, s, d, nm, q_block, tq=512):
    xd = d // XH
    nq = s // tq

    def body(z_ref, k_ref, v_ref, g_ref, y_ref):
        g = g_ref[...]
        for h in range(XH):
            c0 = h * xd
            qn = _head_rms(z_ref[:, c0:c0 + xd], g).astype(BF16)
            sc = lax.dot_general(qn, k_ref[:, c0:c0 + xd], (((1,), (1,)), ((), ())), preferred_element_type=F32)
            sc = sc / math.sqrt(xd)
            p = jnp.exp(sc - jnp.max(sc, axis=-1, keepdims=True))
            p = p / jnp.sum(p, axis=-1, keepdims=True)
            y_ref[:, c0:c0 + xd] = jnp.dot(p.astype(BF16), v_ref[:, c0:c0 + xd], preferred_element_type=F32).astype(BF16)

    kvs = pl.BlockSpec((nm, d), lambda b, i: (b, 0))
    return _pcall(
        body, name="xa_fwd", out_shape=_sds((nb * s, d), BF16), grid=(nb, nq),
        in_specs=[pl.BlockSpec((tq, d), lambda b, i: (b * nq + i, q_block)), kvs, kvs, pl.BlockSpec((1, xd), lambda b, i: (0, 0))],
        out_specs=pl.BlockSpec((tq, d), lambda b, i: (b * nq + i, 0)), compiler_params=_cp(),
    )(z, kn, vb, q_g)


def _xa_bwd(z, kn, vb, q_g, dy, dz, *, nb, s, d, nm, q_block, tq=512):
    xd = d // XH
    nq = s // tq

    def body(z_ref, k_ref, v_ref, g_ref, dy_ref, dz_in, dz_ref, dk_ref, dv_ref, dg_ref):
        b_i, i = pl.program_id(0), pl.program_id(1)
        g = g_ref[...]

        @pl.when(i == 0)
        def _():
            dk_ref[...] = jnp.zeros_like(dk_ref)
            dv_ref[...] = jnp.zeros_like(dv_ref)

        @pl.when((i == 0) & (b_i == 0))
        def _():
            dg_ref[...] = jnp.zeros_like(dg_ref)

        for h in range(XH):
            c0 = h * xd
            xq = z_ref[:, c0:c0 + xd]
            qn = _head_rms(xq, g).astype(BF16)
            kh, vh = k_ref[:, c0:c0 + xd], v_ref[:, c0:c0 + xd]
            sc = lax.dot_general(qn, kh, (((1,), (1,)), ((), ())), preferred_element_type=F32) / math.sqrt(xd)
            p = jnp.exp(sc - jnp.max(sc, axis=-1, keepdims=True))
            p = p / jnp.sum(p, axis=-1, keepdims=True)
            dob = dy_ref[:, c0:c0 + xd].astype(BF16)
            dv_ref[:, c0:c0 + xd] += lax.dot_general(p.astype(BF16), dob, (((0,), (0,)), ((), ())), preferred_element_type=F32)
            dp = lax.dot_general(dob, vh, (((1,), (1,)), ((), ())), preferred_element_type=F32)
            ds = p * (dp - jnp.sum(p * dp, axis=-1, keepdims=True)) / math.sqrt(xd)
            dsb = ds.astype(BF16)
            dqn = jnp.dot(dsb, kh, preferred_element_type=F32)
            dk_ref[:, c0:c0 + xd] += lax.dot_general(dsb, qn, (((0,), (0,)), ((), ())), preferred_element_type=F32)
            dx, sg = _head_rms_bwd(xq, g, dqn)
            dz_ref[:, c0:c0 + xd] = dx.astype(BF16)
            dg_ref[...] += sg

    kvs = pl.BlockSpec((nm, d), lambda b, i: (b, 0))
    qs = pl.BlockSpec((tq, d), lambda b, i: (b * nq + i, q_block))
    gs = pl.BlockSpec((1, xd), lambda b, i: (0, 0))
    return _pcall(
        body, name="xa_bwd",
        out_shape=(_sds(dz.shape, dz.dtype), _sds((nb * nm, d), F32), _sds((nb * nm, d), F32), _sds((1, xd), F32)), grid=(nb, nq),
        in_specs=[qs, kvs, kvs, gs, pl.BlockSpec((tq, d), lambda b, i: (b * nq + i, 0)), HBM_SPEC],
        out_specs=(qs, kvs, kvs, gs), input_output_aliases={5: 0},
        compiler_params=_cp(dimension_semantics=("arbitrary", "arbitrary")),
    )(z, kn, vb, q_g, dy, dz)


def _xa_post_kv(kv, dkn, dv, k_g, *, rows, d):
    xd = d // XH

    def body(kv_ref, dk_ref, dv_ref, g_ref, dkv_ref, dg_ref):
        g = g_ref[...]
        acc = jnp.zeros((1, xd), F32)
        for h in range(XH):
            c0 = h * xd
            dx, sg = _head_rms_bwd(kv_ref[:, c0:c0 + xd], g, dk_ref[:, c0:c0 + xd])
            dkv_ref[:, c0:c0 + xd] = dx.astype(BF16)
            acc = acc + sg
        dkv_ref[:, d:2 * d] = dv_ref[...].astype(BF16)
        dg_ref[...] = acc

    return _pcall(body, name="xa_post_kv", out_shape=(_sds((rows, 2 * d), BF16), _sds((1, xd), F32)), compiler_params=_cp())(
        kv, dkn, dv, k_g)


def _sigmoid(v):
    return 1.0 / (1.0 + jnp.exp(-v))


def _branches_fwd(z, ys, ws, *, t, d, gate_block, tm=512, tn=512):
    nj = d // tn

    def body(ya_ref, yb_ref, yc_ref, wa_ref, wb_ref, wc_ref, ga_ref, gb_ref, gc_ref, m_ref, p_ref):
        acc = None
        for q, (y_ref, w_ref, g_ref) in enumerate(((ya_ref, wa_ref, ga_ref), (yb_ref, wb_ref, gb_ref), (yc_ref, wc_ref, gc_ref))):
            pr = jnp.dot(y_ref[...], w_ref[...], preferred_element_type=F32)
            p_ref[q] = pr.astype(BF16)
            term = _sigmoid(g_ref[...]) * pr
            acc = term if acc is None else acc + term
        m_ref[...] = acc.astype(BF16)

    y_spec = pl.BlockSpec((tm, d), lambda i, j: (i, 0))
    w_spec = pl.BlockSpec((d, tn), lambda i, j: (0, j))

    def gate(q):
        return pl.BlockSpec((tm, tn), lambda i, j: (i, (gate_block + q) * nj + j))

    return _pcall(
        body, name="branches_fwd", out_shape=(_sds((t, d), BF16), _sds((3, t, d), BF16)), grid=(t // tm, nj),
        in_specs=[y_spec] * 3 + [w_spec] * 3 + [gate(0), gate(1), gate(2)],
        out_specs=(pl.BlockSpec((tm, tn), lambda i, j: (i, j)), pl.BlockSpec((3, tm, tn), lambda i, j: (0, i, j))),
        compiler_params=_cp(),
    )(*ys, *ws, z, z, z)


def _gates_bwd(z, dm, proj, dz, *, t, d, gate_block, tm=512):
    def body(g_ref, dm_ref, p_ref, dz_in, dz_ref, da_ref, db_ref, dc_ref):
        q = pl.program_id(1)
        sg = _sigmoid(g_ref[...])
        dmv = dm_ref[...]
        dz_ref[...] = (dmv * p_ref[0].astype(F32) * sg * (1.0 - sg)).astype(BF16)
        dp = (dmv * sg).astype(BF16)
        @pl.when(q == 0)
        def _():
            da_ref[...] = dp

        @pl.when(q == 1)
        def _():
            db_ref[...] = dp

        @pl.when(q == 2)
        def _():
            dc_ref[...] = dp

    gs = pl.BlockSpec((tm, d), lambda i, q: (i, gate_block + q))
    o = pl.BlockSpec((tm, d), lambda i, q: (i, 0))
    return _pcall(
        body, name="gates_bwd", out_shape=(_sds(dz.shape, dz.dtype),) + (_sds((t, d), BF16),) * 3, grid=(t // tm, 3),
        in_specs=[gs, o, pl.BlockSpec((1, tm, d), lambda i, q: (q, i, 0)), HBM_SPEC], out_specs=(gs, o, o, o),
        input_output_aliases={3: 0}, compiler_params=_cp(dimension_semantics=("arbitrary", "arbitrary")),
    )(z, dm, proj, dz)


def _ffn_in_fwd(h2, wfi_t, *, t, d, dff, tm=512, tn=1408):
    nj = dff // tn

    def body(h_ref, wg_ref, wu_ref, g_ref, u_ref, a_ref):
        hv = h_ref[...]
        dn = (((1,), (1,)), ((), ()))
        gv = lax.dot_general(hv, wg_ref[...], dn, preferred_element_type=F32)
        uv = lax.dot_general(hv, wu_ref[...], dn, preferred_element_type=F32)
        g_ref[...] = gv.astype(BF16)
        u_ref[...] = uv.astype(BF16)
        a_ref[...] = (gv * _sigmoid(gv) * uv).astype(BF16)

    o = pl.BlockSpec((tm, tn), lambda i, j: (i, j))
    return _pcall(
        body, name="ffn_in_fwd", out_shape=(_sds((t, dff), BF16),) * 3, grid=(t // tm, nj),
        in_specs=[pl.BlockSpec((tm, d), lambda i, j: (i, 0)), pl.BlockSpec((tn, d), lambda i, j: (j, 0)),
                  pl.BlockSpec((tn, d), lambda i, j: (nj + j, 0))],
        out_specs=(o, o, o), compiler_params=_cp(),
    )(h2, wfi_t, wfi_t)


def _ffn_out_bwd(dy, wfo, gate, up, *, t, d, dff, tm=512, tn=1408):
    def body(dy_ref, w_ref, g_ref, u_ref, dg_ref, du_ref):
        dav = lax.dot_general(dy_ref[...].astype(BF16), w_ref[...], (((1,), (1,)), ((), ())), preferred_element_type=F32)
        gv = g_ref[...].astype(F32)
        sg = _sigmoid(gv)
        dg_ref[...] = (dav * u_ref[...].astype(F32) * sg * (1.0 + gv * (1.0 - sg))).astype(BF16)
        du_ref[...] = (dav * gv * sg).astype(BF16)

    o = pl.BlockSpec((tm, tn), lambda i, j: (i, j))
    return _pcall(
        body, name="ffn_out_bwd", out_shape=(_sds((t, dff), BF16),) * 2, grid=(t // tm, dff // tn),
        in_specs=[pl.BlockSpec((tm, d), lambda i, j: (i, 0)), pl.BlockSpec((tn, d), lambda i, j: (j, 0)), o, o],
        out_specs=(o, o), compiler_params=_cp(),
    )(dy, wfo, gate, up)


def _loss_head(y, target, *, t, d, tm=512):
    def body(y_ref, t_ref, dy_ref, l_ref):
        e = y_ref[...] - t_ref[...]
        dy_ref[...] = e * (1.0 / d)

        @pl.when(pl.program_id(0) == 0)
        def _():
            l_ref[...] = jnp.zeros_like(l_ref)

        l_ref[...] += jnp.sum(jnp.sum(e * e, axis=-1, keepdims=True), axis=0, keepdims=True) * (0.5 / d)

    o = pl.BlockSpec((tm, d), lambda i: (i, 0))
    dy, l = _pcall(
        body, name="loss_head", out_shape=(_sds((t, d), F32), _sds((1, HD), F32)), grid=(t // tm,),
        in_specs=[o, o], out_specs=(o, pl.BlockSpec((1, HD), lambda i: (0, 0))),
        compiler_params=_cp(dimension_semantics=("arbitrary",)),
    )(y, target)
    return dy, l[0, 0]


def _adamw(w, g, m, v, *, name):
    if w.ndim == 3:
        rows, _, cols = w.shape
        tm = max(tt for tt in range(1, 65) if rows % tt == 0)
        block, imap, grid = (tm, 1, cols), (lambda i, j: (i, 0, 0)), (rows // tm, 1)
    else:
        rows, cols = w.shape
        tm, tn = rows, cols
        for cand in (256, 128, 64, 32, 16, 8):
            if rows > cand and rows % cand == 0:
                tm = cand
                break
        if tm == rows and rows > 256 and cols % 128 == 0:
            tn = 128
        block, imap, grid = (tm, tn), (lambda i, j: (i, j)), (rows // tm, cols // tn)
    c1 = 1.0 - ADAM_B1 ** ADAM_STEP
    c2 = 1.0 - ADAM_B2 ** ADAM_STEP

    def body(w_ref, g_ref, m_ref, v_ref, d_ref, nm_ref, nv_ref):
        gv = g_ref[...]
        mn = ADAM_B1 * m_ref[...] + (1.0 - ADAM_B1) * gv
        vn = ADAM_B2 * v_ref[...] + (1.0 - ADAM_B2) * (gv * gv)
        d_ref[...] = -ADAM_LR * ((mn / c1) / (jnp.sqrt(vn / c2) + ADAM_EPS) + ADAM_WD * w_ref[...])
        nm_ref[...] = mn
        nv_ref[...] = vn

    spec = pl.BlockSpec(block, imap)
    return _pcall(
        body, name=name, out_shape=(_sds(w.shape, F32),) * 3, grid=grid,
        in_specs=[spec] * 4, out_specs=(spec,) * 3, compiler_params=_cp(),
    )(w, g, m, v)


def _pad_rows(a, rows):
    return a if a.shape[0] == rows else jnp.pad(a, ((0, rows - a.shape[0]),) + ((0, 0),) * (a.ndim - 1))


class _Pack:
    def __init__(self, row_sizes):
        self.rows = list(row_sizes)
        self.padded = [-(-r // 16) * 16 for r in self.rows]
        self.offs = [sum(self.padded[:i]) for i in range(len(self.rows))]
        self.total = sum(self.padded)

    def stack(self, blocks):
        return jnp.concatenate([_pad_rows(b.astype(BF16), p) for b, p in zip(blocks, self.padded)], axis=0)

    def full(self, gathered, i):
        r = self.rows[i]
        return gathered[:, self.offs[i]:self.offs[i] + r, :].reshape(N_DEV * r, gathered.shape[2])

    def for_core(self, full_grads, core):
        parts = []
        for gfull, r, p in zip(full_grads, self.rows, self.padded):
            blk = gfull.reshape(N_DEV // 2, 2, r, gfull.shape[1])
            blk = lax.dynamic_index_in_dim(blk, core, axis=1, keepdims=False).astype(BF16)
            if p != r:
                blk = jnp.pad(blk, ((0, 0), (0, p - r), (0, 0)))
            parts.append(blk)
        return jnp.concatenate(parts, axis=1)

    def shard(self, g_pack, i):
        return g_pack[self.offs[i]:self.offs[i] + self.rows[i], :]


def kernel(x, mem, norm1_g, w_in, conv_w, conv_b, fox_f_bias, fox_q_g, fox_k_g, mem_norm_g, w_mem_kv, xa_q_g, xa_k_g, w_br_conv, w_br_fox, w_br_xa, w_o, norm2_g, w_ffn_in, w_ffn_out, loss_target, m_norm1_g, m_w_in, m_conv_w, m_conv_b, m_fox_f_bias, m_fox_q_g, m_fox_k_g, m_mem_norm_g, m_w_mem_kv, m_xa_q_g, m_xa_k_g, m_w_br_conv, m_w_br_fox, m_w_br_xa, m_w_o, m_norm2_g, m_w_ffn_in, m_w_ffn_out, v_norm1_g, v_w_in, v_conv_w, v_conv_b, v_fox_f_bias, v_fox_q_g, v_fox_k_g, v_mem_norm_g, v_w_mem_kv, v_xa_q_g, v_xa_k_g, v_w_br_conv, v_w_br_fox, v_w_br_xa, v_w_o, v_norm2_g, v_w_ffn_in, v_w_ffn_out):
    nb, s, d = x.shape
    t = nb * s
    nm = mem.shape[1]
    nh = d // HD
    dff = w_ffn_out.shape[1] * N_DEV
    n_in = w_in.shape[2] * N_DEV
    n_in_pad = -(-n_in // 1152) * 1152
    ff_block = (10 * d) // HD
    my_c = lax.axis_index("c")
    my_q = 2 * lax.axis_index("x") + lax.axis_index("y")

    pack_a = _Pack([w_in.shape[2]])
    pack_b = _Pack([w_mem_kv.shape[2], w_br_conv.shape[1], w_br_fox.shape[1], w_br_xa.shape[1], w_o.shape[1],
                    w_ffn_in.shape[2], w_ffn_out.shape[1]])
    gather_1 = _Pack([w_ffn_in.shape[2], w_ffn_out.shape[1]])
    gather_2 = _Pack([w_mem_kv.shape[2], w_br_conv.shape[1], w_br_fox.shape[1], w_br_xa.shape[1], w_o.shape[1]])
    stack_a = pack_a.stack([w_in[0].T])
    stack_1 = gather_1.stack([w_ffn_in[0].T, w_ffn_out[0]])
    stack_2 = gather_2.stack([w_mem_kv[0].T, w_br_conv[0], w_br_fox[0], w_br_xa[0], w_o[0]])
    cw_block = _pad_rows(conv_w[0], 8)
    (gathered_a,) = _comm_only(_AllGather([stack_a]), "all_gather_w_in")
    win_t = _pad_rows(pack_a.full(gathered_a, 0), n_in_pad)

    x2 = x.reshape(t, d)
    mem2 = mem.reshape(nb * nm, d)
    tgt2 = loss_target.reshape(t, d)
    h1 = _rms_fwd(x2, norm1_g, name="rms1_fwd")
    z, (gathered_1,) = _mm(h1, win_t, m=t, n=n_in_pad, k=d, tb=True, tm=1024, tn=1152, name="mm_z", comm=_AllGather([stack_1]))
    qn, kn, vb = _fox_prep(z, fox_q_g, fox_k_g, t=t, d=d)
    f_bias_col = fox_f_bias.reshape(nh, 1)
    ck_rows = _fox_gates(z, f_bias_col, nb=nb, s=s, nh=nh, ff_block=ff_block).reshape(nb * nh, 1, s)
    y_fox, lse, (gathered_2, cw_g) = _fox_fwd(qn, kn, vb, ck_rows, nb=nb, s=s, nh=nh, comm=_AllGather([stack_2, cw_block]))
    wfi_t, wfo = (gather_1.full(gathered_1, i) for i in range(2))
    wkv_t, wbc, wbf, wbx, wo = (gather_2.full(gathered_2, i) for i in range(5))
    conv_w_full = jnp.transpose(cw_g[:, 0:3, :], (1, 0, 2)).reshape(3, d)

    y_conv = _conv_fwd(z, conv_w_full, conv_b, nb=nb, s=s, d=d)
    mem_n = _rms_fwd(mem2, mem_norm_g, name="rms_mem_fwd")
    kv = _mm(mem_n, wkv_t, m=nb * nm, n=2 * d, k=d, tb=True, name="mm_kv")
    xkn, xvb = _xa_prep_kv(kv, xa_k_g, rows=nb * nm, d=d)
    y_xa = _xa_fwd(z, xkn, xvb, xa_q_g, nb=nb, s=s, d=d, nm=nm, q_block=6)
    merged, proj = _branches_fwd(z, (y_conv, y_fox, y_xa), (wbc, wbf, wbx), t=t, d=d, gate_block=7)
    x1 = _mm(merged, wo, m=t, n=d, k=d, res=x2, name="mm_x1")
    h2 = _rms_fwd(x1, norm2_g, name="rms2_fwd")
    ff_gate, ff_up, act = _ffn_in_fwd(h2, wfi_t, t=t, d=d, dff=dff)
    y = _mm(act, wfo, m=t, n=d, k=dff, res=x1, tk=dff, name="mm_y")
    dy, loss_local = _loss_head(y, tgt2, t=t, d=d)

    g_wfo = _mm(act, dy, m=dff, n=d, k=t, ta=True, tm=1408, tk=2048, out_dtype=BF16, name="mm_g_wfo")
    dgate, dup = _ffn_out_bwd(dy, wfo, ff_gate, ff_up, t=t, d=d, dff=dff)
    dh2 = _mm(dgate, wfi_t, m=t, n=d, k=dff, tk=dff, name="mm_dh2_g")
    dh2 = _mm(dup, wfi_t, m=t, n=d, k=dff, tk=dff, b_off=(1, 0), res=dh2, name="mm_dh2_u")
    g_wfi_g = _mm(dgate, h2, m=dff, n=d, k=t, ta=True, tm=1408, tk=2048, out_dtype=BF16, name="mm_g_wfi_g")
    g_wfi_u = _mm(dup, h2, m=dff, n=d, k=t, ta=True, tm=1408, tk=2048, out_dtype=BF16, name="mm_g_wfi_u")
    dx1, g_norm2 = _rms_bwd(dh2, x1, norm2_g, dy, name="rms2_bwd")
    dmerged = _mm(dx1, wo, m=t, n=d, k=d, tb=True, name="mm_dmerged")
    g_wo = _mm(merged, dx1, m=d, n=d, k=t, ta=True, tk=2048, out_dtype=BF16, name="mm_g_wo")

    dz = lax.empty((t, n_in_pad), BF16)
    dz, dpa, dpb, dpc = _gates_bwd(z, dmerged, proj, dz, t=t, d=d, gate_block=7)
    dy_conv = _mm(dpa, wbc, m=t, n=d, k=d, tb=True, name="mm_dy_conv")
    g_wbc = _mm(y_conv, dpa, m=d, n=d, k=t, ta=True, tk=4096, out_dtype=BF16, name="mm_g_wbc")
    dy_fox = _mm(dpb, wbf, m=t, n=d, k=d, tb=True, name="mm_dy_fox")
    g_wbf = _mm(y_fox, dpb, m=d, n=d, k=t, ta=True, tk=4096, out_dtype=BF16, name="mm_g_wbf")
    dy_xa = _mm(dpc, wbx, m=t, n=d, k=d, tb=True, name="mm_dy_xa")
    g_wbx = _mm(y_xa, dpc, m=d, n=d, k=t, ta=True, tk=4096, out_dtype=BF16, name="mm_g_wbx")

    dz, g_conv_w, g_conv_b = _conv_bwd(z, dy_conv, conv_w_full, conv_b, dz, nb=nb, s=s, d=d)
    dz, dxkn, dxv, g_xa_q = _xa_bwd(z, xkn, xvb, xa_q_g, dy_xa, dz, nb=nb, s=s, d=d, nm=nm, q_block=6)
    dkv, g_xa_k = _xa_post_kv(kv, dxkn, dxv, xa_k_g, rows=nb * nm, d=d)
    g_wkv_t = _mm(dkv, mem_n, m=2 * d, n=d, k=nb * nm, ta=True, out_dtype=BF16, name="mm_g_wkv")
    dmem_n = _mm(dkv, wkv_t, m=nb * nm, n=d, k=2 * d, tk=2 * d, name="mm_dmem")
    _, g_mem_norm = _rms_bwd(dmem_n, mem2, mem_norm_g, None, name="rms_mem_bwd")

    grads_b = [g_wkv_t, g_wbc, g_wbf, g_wbx, g_wo, jnp.concatenate([g_wfi_g, g_wfi_u], axis=0), g_wfo]
    keep_b, send_b = pack_b.for_core(grads_b, my_c), pack_b.for_core(grads_b, 1 - my_c)
    dqn, ds_rows, (from_sibling_b,) = _fox_bwd_dq(qn, kn, vb, ck_rows, dy_fox, y_fox, lse, nb=nb, s=s, nh=nh,
                                                  comm=_SiblingSwap(send_b))
    part_b = _add2(keep_b, from_sibling_b, name="rs_add_sibling_b")
    dkn, dvb, ds_cols, (from_chips_b,) = _fox_bwd_dkv(qn, kn, vb, ck_rows, dy_fox, y_fox, lse, nb=nb, s=s, nh=nh,
                                                     comm=_ChipExchange(part_b))
    g_pack_b = _add4(lax.dynamic_index_in_dim(part_b, my_q, axis=0, keepdims=False), from_chips_b, name="rs_add_chips_b")
    dz, g_fox_q, g_fox_k = _fox_post(z, dqn, dkn, dvb, fox_q_g, fox_k_g, dz, t=t, d=d)
    dz, g_f_bias = _fox_gates_bwd(z, f_bias_col, ds_rows, ds_cols, dz, nb=nb, s=s, nh=nh, ff_block=ff_block)

    g_win_t = _mm(dz, h1, m=n_in_pad, n=d, k=t, ta=True, tm=1152, tk=2048, out_dtype=BF16, name="mm_g_win")
    grads_a = [g_win_t[:n_in]]
    keep_a, send_a = pack_a.for_core(grads_a, my_c), pack_a.for_core(grads_a, 1 - my_c)
    dh1, (from_sibling_a,) = _mm(dz, win_t, m=t, n=d, k=n_in_pad, tk=3456, name="mm_dh1", comm=_SiblingSwap(send_a))
    part_a = _add2(keep_a, from_sibling_a, name="rs_add_sibling_a")
    (from_chips_a,) = _comm_only(_ChipExchange(part_a), "rs_chips_a")
    g_pack_a = _add4(lax.dynamic_index_in_dim(part_a, my_q, axis=0, keepdims=False), from_chips_a, name="rs_add_chips_a",
                     rows_apart=True)
    dx, g_norm1 = _rms_bwd(dh1, x2, norm1_g, dx1, name="rms1_bwd")

    xd = d // XH
    tail = jnp.concatenate(
        [_pad_rows(g_f_bias.reshape(nh), HD).reshape(1, HD), g_fox_q, g_fox_k, g_xa_q, g_xa_k,
         jnp.zeros((1, d - 3 * HD - 2 * xd), F32)], axis=1)
    small = jnp.concatenate([g_norm1, g_norm2, g_conv_w, g_conv_b, g_mem_norm, tail], axis=0)
    small = _all_reduce_small(small)
    gs_conv_w = lax.dynamic_slice_in_dim(small[2:5], (2 * my_q + my_c) * (d // N_DEV), d // N_DEV, axis=1)

    grads = {
        "norm1_g": small[0:1], "w_in": g_pack_a, "conv_w": gs_conv_w, "conv_b": small[5:6],
        "fox_f_bias": small[7:8, 0:nh], "fox_q_g": small[7:8, HD:2 * HD], "fox_k_g": small[7:8, 2 * HD:3 * HD],
        "mem_norm_g": small[6:7], "w_mem_kv": pack_b.shard(g_pack_b, 0), "xa_q_g": small[7:8, 3 * HD:3 * HD + xd],
        "xa_k_g": small[7:8, 3 * HD + xd:3 * HD + 2 * xd], "w_br_conv": pack_b.shard(g_pack_b, 1),
        "w_br_fox": pack_b.shard(g_pack_b, 2), "w_br_xa": pack_b.shard(g_pack_b, 3), "w_o": pack_b.shard(g_pack_b, 4),
        "norm2_g": small[1:2], "w_ffn_in": pack_b.shard(g_pack_b, 5), "w_ffn_out": pack_b.shard(g_pack_b, 6),
    }
    transposed = {"w_in", "w_mem_kv", "w_ffn_in"}
    weights = {
        "norm1_g": (norm1_g, m_norm1_g, v_norm1_g), "w_in": (w_in, m_w_in, v_w_in), "conv_w": (conv_w, m_conv_w, v_conv_w),
        "conv_b": (conv_b, m_conv_b, v_conv_b), "fox_f_bias": (fox_f_bias, m_fox_f_bias, v_fox_f_bias),
        "fox_q_g": (fox_q_g, m_fox_q_g, v_fox_q_g), "fox_k_g": (fox_k_g, m_fox_k_g, v_fox_k_g),
        "mem_norm_g": (mem_norm_g, m_mem_norm_g, v_mem_norm_g), "w_mem_kv": (w_mem_kv, m_w_mem_kv, v_w_mem_kv),
        "xa_q_g": (xa_q_g, m_xa_q_g, v_xa_q_g), "xa_k_g": (xa_k_g, m_xa_k_g, v_xa_k_g),
        "w_br_conv": (w_br_conv, m_w_br_conv, v_w_br_conv), "w_br_fox": (w_br_fox, m_w_br_fox, v_w_br_fox),
        "w_br_xa": (w_br_xa, m_w_br_xa, v_w_br_xa), "w_o": (w_o, m_w_o, v_w_o), "norm2_g": (norm2_g, m_norm2_g, v_norm2_g),
        "w_ffn_in": (w_ffn_in, m_w_ffn_in, v_w_ffn_in), "w_ffn_out": (w_ffn_out, m_w_ffn_out, v_w_ffn_out),
    }
    out_g, out_d, out_m, out_v = [], [], [], []
    for name, (w, m, v) in weights.items():
        shape = w.shape
        if name == "w_in":
            w3, m3, v3 = (a.transpose(2, 0, 1) for a in (w, m, v))
            dl, mn, vn = _adamw(w3, g_pack_a, m3, v3, name="adamw_" + name)
            out_g.append(g_pack_a[:shape[2]].transpose(1, 2, 0))
            out_d.append(dl.transpose(1, 2, 0))
            out_m.append(mn.transpose(1, 2, 0))
            out_v.append(vn.transpose(1, 2, 0))
            continue
        tr = name in transposed

        def as2d(a):
            a = a.reshape(shape[-2], shape[-1])
            return a.T if tr else a

        def back(a):
            return (a.T if tr else a).reshape(shape)

        w2 = as2d(w)
        g2 = grads[name].reshape(w2.shape)
        dl, mn, vn = _adamw(w2, g2, as2d(m), as2d(v), name="adamw_" + name)
        out_g.append(back(g2))
        out_d.append(back(dl))
        out_m.append(back(mn))
        out_v.append(back(vn))

    loss = lax.psum(loss_local, ("x", "y", "c"))
    return (loss, dx.reshape(nb, s, d), *out_g, *out_d, *out_m, *out_v)
```

```python
import math

import jax
import jax.numpy as jnp
from jax import lax
from jax.experimental import pallas as pl
from jax.experimental.pallas import tpu as pltpu

F32, BF16 = jnp.float32, jnp.bfloat16
EPS = 1e-6
HD = 128
XH = 4
N_DEV = 8
MESH = pl.DeviceIdType.MESH
VMEM_LIMIT = 52 * 1024 * 1024
NEG = -1e30
SQRT_HD = math.sqrt(HD)
EXP2_C = math.log2(math.e) / SQRT_HD
LOG2E = math.log2(math.e)
FOX_HP = 4

ADAM_LR, ADAM_B1, ADAM_B2, ADAM_EPS, ADAM_WD, ADAM_STEP = 0.001, 0.9, 0.999, 1e-08, 0.01, 10


def _cp(**kw):
    return pltpu.CompilerParams(vmem_limit_bytes=VMEM_LIMIT, **kw)


def _pcall(body, **kw):
    return pl.pallas_call(body, **kw)


def _sds(shape, dtype):
    return jax.ShapeDtypeStruct(shape, dtype)


HBM_SPEC = pl.BlockSpec(memory_space=pl.ANY)


def _mesh_pos():
    return lax.axis_index("x"), lax.axis_index("y"), lax.axis_index("c")


class _AllGather:
    def __init__(self, blocks):
        self.blocks = list(blocks)
        n = len(self.blocks)
        self.out_shape = [_sds((N_DEV,) + b.shape, b.dtype) for b in self.blocks]
        self.scratch = [pltpu.SemaphoreType.DMA((n, 7)), pltpu.SemaphoreType.DMA((n, 7)), pltpu.SemaphoreType.DMA((n,))]

    def bind(self, ins, outs, sems):
        n = len(ins)
        send_sems, recv_sems, local_sems = sems
        x, y, c = _mesh_pos()
        me, sibling = (x, y, c), (x, y, 1 - c)
        chips = [(1 - x, y), (x, 1 - y), (1 - x, 1 - y)]

        def slab(t, dev):
            return outs[t].at[4 * dev[0] + 2 * dev[1] + dev[2]]

        def copy(t, k, block, to, src=None):
            dst = slab(t, block)
            return pltpu.make_async_remote_copy(
                src_ref=dst if src is None else src, dst_ref=dst, send_sem=send_sems.at[t, k], recv_sem=recv_sems.at[t, k],
                device_id=to, device_id_type=MESH)

        mine = [pltpu.make_async_copy(ins[t], slab(t, me), local_sems.at[t]) for t in range(n)]
        first = []
        for t in range(n):
            first.append(copy(t, 0, me, sibling, src=ins[t]))
            first += [copy(t, 1 + j, me, (*chip, c), src=ins[t]) for j, chip in enumerate(chips)]
        passed = [copy(t, 4 + j, (*chip, c), sibling) for j, chip in enumerate(chips) for t in range(n)]

        def start():
            for cp in mine + first:
                cp.start()

        def mid():
            for j, chip in enumerate(chips):
                for t in range(n):
                    copy(t, 1 + j, (*chip, c), me).wait_recv()
                    passed[j * n + t].start()

        def finish():
            for t in range(n):
                copy(t, 0, sibling, me).wait_recv()
                for j, chip in enumerate(chips):
                    copy(t, 4 + j, (*chip, 1 - c), me).wait_recv()
            for cp in first + passed:
                cp.wait_send()
            for cp in mine:
                cp.wait()

        return start, mid, finish


class _SiblingSwap:
    def __init__(self, send):
        self.blocks = [send]
        self.out_shape = [_sds(send.shape, send.dtype)]
        self.scratch = [pltpu.SemaphoreType.DMA, pltpu.SemaphoreType.DMA]

    def bind(self, ins, outs, sems):
        x, y, c = _mesh_pos()
        cp = pltpu.make_async_remote_copy(src_ref=ins[0], dst_ref=outs[0], send_sem=sems[0], recv_sem=sems[1],
                                          device_id=(x, y, 1 - c), device_id_type=MESH)
        return cp.start, (lambda: None), cp.wait


class _ChipExchange:
    def __init__(self, part):
        self.blocks = [part]
        self.out_shape = [_sds((3,) + part.shape[1:], part.dtype)]
        self.scratch = [pltpu.SemaphoreType.DMA((3,)), pltpu.SemaphoreType.DMA((3,))]

    def bind(self, ins, outs, sems):
        x, y, c = _mesh_pos()
        chips = [(1 - x, y), (x, 1 - y), (1 - x, 1 - y)]
        cps = [
            pltpu.make_async_remote_copy(src_ref=ins[0].at[2 * cx + cy], dst_ref=outs[0].at[j], send_sem=sems[0].at[j],
                                         recv_sem=sems[1].at[j], device_id=(cx, cy, c), device_id_type=MESH)
            for j, (cx, cy) in enumerate(chips)
        ]

        def start():
            for cp in cps:
                cp.start()

        def finish():
            for cp in cps:
                cp.wait()

        return start, (lambda: None), finish


def _comm_lists(comm):
    if comm is None:
        return [], [], [], []
    n = len(comm.blocks)
    return list(comm.blocks), [HBM_SPEC] * n, list(comm.out_shape), list(comm.scratch)


def _split_refs(refs, n_pre, n_in, n_cin, n_out, n_cout, n_scr):
    cuts = [n_pre, n_in, n_cin, n_out, n_cout, n_scr]
    out, at = [], 0
    for c in cuts:
        out.append(refs[at:at + c])
        at += c
    out.append(refs[at:])
    return out


def _run_comm(comm, cin, cout, csem, step, n_steps, when="start"):
    if comm is None:
        return
    start, mid, finish = comm.bind(cin, cout, csem)
    if when == "start":
        pl.when(step == 0)(start)
    else:
        pl.when(step == max(n_steps - 2, 0))(mid)
        pl.when(step == n_steps - 1)(finish)


def _comm_only(comm, name):
    ins, in_specs, out_shape, scratch = _comm_lists(comm)
    n = len(ins)

    def body(*refs):
        start, mid, finish = comm.bind(refs[:n], refs[n:2 * n], refs[2 * n:])
        start()
        mid()
        finish()

    outs = _pcall(body, name=name, out_shape=tuple(out_shape), in_specs=in_specs, out_specs=tuple([HBM_SPEC] * n),
                  scratch_shapes=scratch, compiler_params=_cp())(*ins)
    return list(outs)


def _all_reduce_small(v):
    rows, cols = v.shape

    def body(v_ref, o_ref, slots, send_sems, recv_sems):
        x, y, c = _mesh_pos()
        me = 4 * x + 2 * y + c
        slots[me] = v_ref[...]
        cps = []
        for k in range(1, N_DEV):
            px, py, pc = x ^ (k >> 2), y ^ ((k >> 1) & 1), c ^ (k & 1)
            cps.append(pltpu.make_async_remote_copy(
                src_ref=v_ref, dst_ref=slots.at[me], send_sem=send_sems.at[k - 1], recv_sem=recv_sems.at[k - 1],
                device_id=(px, py, pc), device_id_type=MESH))
        for cp in cps:
            cp.start()
        for cp in cps:
            cp.wait()
        acc = slots[0]
        for k in range(1, N_DEV):
            acc = acc + slots[k]
        o_ref[...] = acc

    return _pcall(
        body, name="all_reduce_small", out_shape=_sds((rows, cols), F32),
        in_specs=[pl.BlockSpec(memory_space=pltpu.VMEM)], out_specs=pl.BlockSpec(memory_space=pltpu.VMEM),
        scratch_shapes=[pltpu.VMEM((N_DEV, rows, cols), F32), pltpu.SemaphoreType.DMA((7,)), pltpu.SemaphoreType.DMA((7,))],
        compiler_params=_cp(),
    )(v)


def _row_tile(rows, cap=1040):
    return max(tt for tt in range(16, cap + 1, 16) if rows % tt == 0)


def _add2(a, b, *, name):
    nq, rows, cols = a.shape
    tm = _row_tile(rows)

    def body(a_ref, b_ref, o_ref):
        o_ref[...] = (a_ref[...].astype(F32) + b_ref[...].astype(F32)).astype(BF16)

    spec = pl.BlockSpec((1, tm, cols), lambda q, i: (q, i, 0))
    return _pcall(body, name=name, out_shape=_sds(a.shape, BF16), grid=(nq, rows // tm),
                  in_specs=[spec, spec], out_specs=spec, compiler_params=_cp())(a, b)


def _add4(own, recv, *, name, rows_apart=False):
    rows, cols = own.shape
    tm = _row_tile(rows, 256 if rows_apart else 1040)

    def body(o_ref, r_ref, g_ref):
        tot = ((o_ref[...].astype(F32) + r_ref[0].astype(F32)) + r_ref[1].astype(F32)) + r_ref[2].astype(F32)
        if rows_apart:
            g_ref[:, 0, :] = tot
        else:
            g_ref[...] = tot

    if rows_apart:
        out_shape, out_spec = _sds((rows, 1, cols), F32), pl.BlockSpec((tm, 1, cols), lambda i: (i, 0, 0))
    else:
        out_shape, out_spec = _sds((rows, cols), F32), pl.BlockSpec((tm, cols), lambda i: (i, 0))
    return _pcall(
        body, name=name, out_shape=out_shape, grid=(rows // tm,),
        in_specs=[pl.BlockSpec((tm, cols), lambda i: (i, 0)), pl.BlockSpec((3, tm, cols), lambda i: (0, i, 0))],
        out_specs=out_spec, compiler_params=_cp(),
    )(own, recv)


def _mm(a, b, *, m, n, k, name, ta=False, tb=False, b_off=(0, 0), res=None, out_dtype=F32, tm=512, tn=1024, tk=1024,
        comm=None):
    tm, tn, tk = min(tm, m), min(tn, n), min(tk, k)
    assert m % tm == 0 and n % tn == 0 and k % tk == 0, (name, m, n, k, tm, tn, tk)
    nk = k // tk
    grid = (m // tm, n // tn, nk)
    bo0, bo1 = b_off
    if ta:
        a_spec = pl.BlockSpec((tk, tm), lambda i, j, kk: (kk, i))
    else:
        a_spec = pl.BlockSpec((tm, tk), lambda i, j, kk: (i, kk))
    if tb:
        b_spec = pl.BlockSpec((tn, tk), lambda i, j, kk: (j + bo0, kk + bo1))
    else:
        b_spec = pl.BlockSpec((tk, tn), lambda i, j, kk: (kk + bo0, j + bo1))
    o_spec = pl.BlockSpec((tm, tn), lambda i, j, kk: (i, j))
    dn = (((0 if ta else 1,), (1 if tb else 0,)), ((), ()))
    has_res = res is not None
    c_ins, c_in_specs, c_out_shape, c_scratch = _comm_lists(comm)
    n_in = 3 if has_res else 2
    n_scr = 1 if nk > 1 else 0

    def body(*refs):
        _, ins, cin, outs, cout, scr, csem = _split_refs(refs, 0, n_in, len(c_ins), 1, len(c_ins), n_scr)
        a_ref, b_ref = ins[0], ins[1]
        r_ref = ins[2] if has_res else None
        o_ref = outs[0]
        step = (pl.program_id(0) * grid[1] + pl.program_id(1)) * nk + pl.program_id(2)
        _run_comm(comm, cin, cout, csem, step, grid[0] * grid[1] * nk, "start")
        p = lax.dot_general(a_ref[...].astype(BF16), b_ref[...].astype(BF16), dn, preferred_element_type=F32)

        def finish(acc):
            if has_res:
                acc = acc + r_ref[...]
            o_ref[...] = acc.astype(out_dtype)

        if nk == 1:
            finish(p)
        else:
            acc_ref = scr[0]
            kk = pl.program_id(2)

            @pl.when(kk == 0)
            def _():
                acc_ref[...] = p

            @pl.when(kk > 0)
            def _():
                acc_ref[...] += p

            @pl.when(kk == nk - 1)
            def _():
                finish(acc_ref[...])

        _run_comm(comm, cin, cout, csem, step, grid[0] * grid[1] * nk, "end")

    ins = [a, b] + ([res] if has_res else [])
    in_specs = [a_spec, b_spec] + ([o_spec] if has_res else [])
    sem = ("arbitrary",) * 3 if comm is not None else ("parallel", "parallel", "arbitrary")
    outs = _pcall(
        body, name=name, out_shape=tuple([_sds((m, n), out_dtype)] + c_out_shape), grid=grid,
        in_specs=in_specs + c_in_specs, out_specs=tuple([o_spec] + [HBM_SPEC] * len(c_ins)),
        scratch_shapes=([pltpu.VMEM((tm, tn), F32)] if nk > 1 else []) + c_scratch,
        compiler_params=_cp(dimension_semantics=sem),
    )(*(ins + c_ins))
    return outs[0] if comm is None else (outs[0], list(outs[1:]))


SLAB_WIN = 1280
SLAB_TAIL = 48


def _z_from_slabs(h1, slabs, *, t, d, n_pad, rows, comm=None, tm=1024):
    n_slab, rp, _ = slabs.shape
    tm = min(tm, t)
    sh = rows - SLAB_WIN
    tail_lo = rp - SLAB_TAIL
    assert sh >= 0 and rows - (n_slab - 1) * sh >= tail_lo and rp % SLAB_TAIL == 0 and (n_slab - 1) * sh <= 128
    c_ins, c_in_specs, c_out_shape, c_scratch = _comm_lists(comm)
    grid = (n_slab, t // tm)

    def body(*refs):
        _, ins, cin, outs, cout, _, csem = _split_refs(refs, 0, 3, len(c_ins), 1, len(c_ins), 0)
        h_ref, s_ref, t_ref = ins
        o_ref = outs[0]
        k = pl.program_id(0)
        step = k * grid[1] + pl.program_id(1)
        _run_comm(comm, cin, cout, csem, step, grid[0] * grid[1], "start")
        hv = h_ref[...]
        dn = (((1,), (1,)), ((), ()))
        y = lax.dot_general(hv, s_ref[0], dn, preferred_element_type=F32)
        yr = pltpu.roll(y[:, 0:SLAB_WIN], k * sh, axis=1)
        p = lax.dot_general(hv, t_ref[0], dn, preferred_element_type=F32)
        pp = jnp.concatenate([p, jnp.zeros((tm, 128 - SLAB_TAIL), F32)], axis=1)
        pr = pltpu.roll(pp, 128 - (rows - tail_lo) + k * sh, axis=1)
        lane = lax.broadcasted_iota(jnp.int32, (tm, 128), 1)
        o_ref[:, 0:128] = jnp.where(lane < k * sh, pr, yr[:, 0:128])
        o_ref[:, 128:SLAB_WIN] = yr[:, 128:SLAB_WIN]
        _run_comm(comm, cin, cout, csem, step, grid[0] * grid[1], "end")

    outs = _pcall(
        body, name="mm_z", out_shape=tuple([_sds((t, n_pad), F32)] + c_out_shape), grid=grid,
        in_specs=[pl.BlockSpec((tm, d), lambda k, i: (i, 0)), pl.BlockSpec((1, rp, d), lambda k, i: (k, 0, 0)),
                  pl.BlockSpec((1, SLAB_TAIL, d), lambda k, i: (jnp.maximum(k - 1, 0), tail_lo // SLAB_TAIL, 0))] + c_in_specs,
        out_specs=tuple([pl.BlockSpec((tm, SLAB_WIN), lambda k, i: (i, k))] + [HBM_SPEC] * len(c_ins)),
        scratch_shapes=c_scratch, compiler_params=_cp(dimension_semantics=("arbitrary", "arbitrary")),
    )(h1, slabs, slabs, *c_ins)
    return outs[0], list(outs[1:])


def _z_tail(h1, w_rows, z, *, t, d, col_block, tm=1024):
    def body(h_ref, w_ref, z_in, o_ref):
        o_ref[...] = lax.dot_general(h_ref[...], w_ref[...], (((1,), (1,)), ((), ())), preferred_element_type=F32)

    return _pcall(
        body, name="mm_z_tail", out_shape=_sds(z.shape, z.dtype), grid=(t // tm,),
        in_specs=[pl.BlockSpec((tm, d), lambda i: (i, 0)), pl.BlockSpec((HD, d), lambda i: (0, 0)), HBM_SPEC],
        out_specs=pl.BlockSpec((tm, HD), lambda i: (i, col_block)), input_output_aliases={2: 0}, compiler_params=_cp(),
    )(h1, w_rows, z)


def _slab_window(a1_ref, a2_ref, k, sh):
    w = jnp.concatenate([a1_ref[...], a2_ref[...]], axis=1)
    width = SLAB_WIN + 128
    return pltpu.roll(w, width - k * sh, axis=1)


def _g_win_slabs(dz, h1, *, t, d, n_slab, rows, rp, tk=2048):
    sh = rows - SLAB_WIN
    tk = min(tk, t)
    nk = t // tk
    width = SLAB_WIN + 128

    def body(a1_ref, a2_ref, h_ref, o_ref, acc):
        k, kk = pl.program_id(0), pl.program_id(1)
        ws = _slab_window(a1_ref, a2_ref, k, sh)
        p = lax.dot_general(ws, h_ref[...], (((0,), (0,)), ((), ())), preferred_element_type=F32)

        @pl.when(kk == 0)
        def _():
            acc[...] = p

        @pl.when(kk > 0)
        def _():
            acc[...] += p

        @pl.when(kk == nk - 1)
        def _():
            o_ref[0] = acc[0:rp, :].astype(BF16)

    return _pcall(
        body, name="mm_g_win", out_shape=_sds((n_slab, rp, d), BF16), grid=(n_slab, nk),
        in_specs=[pl.BlockSpec((tk, SLAB_WIN), lambda k, kk: (kk, k)),
                  pl.BlockSpec((tk, 128), lambda k, kk: (kk, (SLAB_WIN // 128) * (k + 1))),
                  pl.BlockSpec((tk, d), lambda k, kk: (kk, 0))],
        out_specs=pl.BlockSpec((1, rp, d), lambda k, kk: (k, 0, 0)),
        scratch_shapes=[pltpu.VMEM((width, d), F32)], compiler_params=_cp(dimension_semantics=("parallel", "arbitrary")),
    )(dz, dz, h1)


def _dh1_from_slabs(dz, slabs, *, t, d, rows, comm=None, tm=1024):
    n_slab, rp, _ = slabs.shape
    sh = rows - SLAB_WIN
    width = SLAB_WIN + 128
    c_ins, c_in_specs, c_out_shape, c_scratch = _comm_lists(comm)
    grid = (t // tm, n_slab)

    def body(*refs):
        _, ins, cin, outs, cout, scr, csem = _split_refs(refs, 0, 3, len(c_ins), 1, len(c_ins), 1)
        a1_ref, a2_ref, s_ref = ins
        o_ref, acc = outs[0], scr[0]
        k = pl.program_id(1)
        step = pl.program_id(0) * n_slab + k
        _run_comm(comm, cin, cout, csem, step, grid[0] * n_slab, "start")
        ws = _slab_window(a1_ref, a2_ref, k, sh)
        s_end = jnp.concatenate([s_ref[0, SLAB_WIN:rp, :], jnp.zeros((width - rp, d), BF16)], axis=0)
        p = (jnp.dot(ws[:, 0:SLAB_WIN], s_ref[0, 0:SLAB_WIN, :], preferred_element_type=F32)
             + jnp.dot(ws[:, SLAB_WIN:width], s_end, preferred_element_type=F32))

        @pl.when(k == 0)
        def _():
            acc[...] = p

        @pl.when(k > 0)
        def _():
            acc[...] += p

        @pl.when(k == n_slab - 1)
        def _():
            o_ref[...] = acc[...]

        _run_comm(comm, cin, cout, csem, step, grid[0] * n_slab, "end")

    outs = _pcall(
        body, name="mm_dh1", out_shape=tuple([_sds((t, d), F32)] + c_out_shape), grid=grid,
        in_specs=[pl.BlockSpec((tm, SLAB_WIN), lambda i, k: (i, k)),
                  pl.BlockSpec((tm, 128), lambda i, k: (i, (SLAB_WIN // 128) * (k + 1))),
                  pl.BlockSpec((1, rp, d), lambda i, k: (k, 0, 0))] + c_in_specs,
        out_specs=tuple([pl.BlockSpec((tm, d), lambda i, k: (i, 0))] + [HBM_SPEC] * len(c_ins)),
        scratch_shapes=[pltpu.VMEM((tm, d), F32)] + c_scratch,
        compiler_params=_cp(dimension_semantics=("arbitrary", "arbitrary")),
    )(dz, dz, slabs, *c_ins)
    return outs[0], list(outs[1:])


def _rms_fwd(x, g, *, name, tm=512):
    t, d = x.shape
    tm = min(tm, t)

    def body(x_ref, g_ref, h_ref):
        xv = x_ref[...]
        r = lax.rsqrt(jnp.mean(xv * xv, axis=-1, keepdims=True) + EPS)
        h_ref[...] = (xv * r * g_ref[...]).astype(BF16)

    return _pcall(
        body, name=name, out_shape=_sds((t, d), BF16), grid=(t // tm,),
        in_specs=[pl.BlockSpec((tm, d), lambda i: (i, 0)), pl.BlockSpec((1, d), lambda i: (0, 0))],
        out_specs=pl.BlockSpec((tm, d), lambda i: (i, 0)), compiler_params=_cp(),
    )(x, g)


def _rms_bwd(dh, x, g, res, *, name, tm=512):
    t, d = x.shape
    tm = min(tm, t)
    has_res = res is not None

    def body(*refs):
        dh_ref, x_ref, g_ref = refs[:3]
        r_ref = refs[3] if has_res else None
        dx_ref, dg_ref = refs[-2], refs[-1]
        xv = x_ref[...]
        r = lax.rsqrt(jnp.mean(xv * xv, axis=-1, keepdims=True) + EPS)
        xh = xv * r
        dhv = dh_ref[...]
        gd = dhv * g_ref[...]
        dx = r * (gd - xh * jnp.mean(gd * xh, axis=-1, keepdims=True))
        if has_res:
            dx = dx + r_ref[...]
        dx_ref[...] = dx

        @pl.when(pl.program_id(0) == 0)
        def _():
            dg_ref[...] = jnp.zeros_like(dg_ref)

        dg_ref[...] += jnp.sum(dhv * xh, axis=0, keepdims=True)

    row = pl.BlockSpec((tm, d), lambda i: (i, 0))
    vec = pl.BlockSpec((1, d), lambda i: (0, 0))
    return _pcall(
        body, name=name, out_shape=(_sds((t, d), F32), _sds((1, d), F32)), grid=(t // tm,),
        in_specs=[row, row, vec] + ([row] if has_res else []), out_specs=(row, vec),
        compiler_params=_cp(dimension_semantics=("arbitrary",)),
    )(*([dh, x, g] + ([res] if has_res else [])))


def _conv_fwd(z, conv_w, conv_b, *, nb, s, d, tc=256):
    nc = d // tc

    def body(cb_ref, cc_ref, cv_ref, w_ref, b_ref, y_ref):
        u = cc_ref[...] * cv_ref[...]
        row = lax.broadcasted_iota(jnp.int32, u.shape, 0)
        u1 = jnp.where(row >= 1, pltpu.roll(u, 1, axis=0), 0.0)
        u2 = jnp.where(row >= 2, pltpu.roll(u, 2, axis=0), 0.0)
        w = w_ref[...]
        y = w[2:3, :] * u + w[1:2, :] * u1 + w[0:1, :] * u2 + b_ref[...]
        y_ref[...] = (cb_ref[...] * y).astype(BF16)

    def part(p):
        return pl.BlockSpec((s, tc), lambda b, j: (b, p * nc + j))

    return _pcall(
        body, name="conv_fwd", out_shape=_sds((nb * s, d), BF16), grid=(nb, nc),
        in_specs=[part(0), part(1), part(2), pl.BlockSpec((3, tc), lambda b, j: (0, j)), pl.BlockSpec((1, tc), lambda b, j: (0, j))],
        out_specs=pl.BlockSpec((s, tc), lambda b, j: (b, j)), compiler_params=_cp(),
    )(z, z, z, conv_w, conv_b)


def _conv_bwd(z, dy, conv_w, conv_b, dz, *, nb, s, d, tc=256):
    nc = d // tc

    def body(cb_ref, cc_ref, cv_ref, dy_ref, w_ref, b_ref, dz_in, dz_ref, dw_ref, db_ref, stash):
        b_i, p = pl.program_id(1), pl.program_id(2)

        @pl.when(p == 0)
        def _():
            cc, cv = cc_ref[...], cv_ref[...]
            u = cc * cv
            n = u.shape[0]
            row = lax.broadcasted_iota(jnp.int32, u.shape, 0)
            u1 = jnp.where(row >= 1, pltpu.roll(u, 1, axis=0), 0.0)
            u2 = jnp.where(row >= 2, pltpu.roll(u, 2, axis=0), 0.0)
            w = w_ref[...]
            y = w[2:3, :] * u + w[1:2, :] * u1 + w[0:1, :] * u2 + b_ref[...]
            dyv = dy_ref[...]
            dconv = dyv * cb_ref[...]
            g1 = jnp.where(row < n - 1, pltpu.roll(dconv, n - 1, axis=0), 0.0)
            g2 = jnp.where(row < n - 2, pltpu.roll(dconv, n - 2, axis=0), 0.0)
            du = w[2:3, :] * dconv + w[1:2, :] * g1 + w[0:1, :] * g2
            stash[0] = (dyv * y).astype(BF16)
            stash[1] = (du * cv).astype(BF16)
            stash[2] = (du * cc).astype(BF16)
            dws = [jnp.sum(dconv * u2, axis=0, keepdims=True), jnp.sum(dconv * u1, axis=0, keepdims=True),
                   jnp.sum(dconv * u, axis=0, keepdims=True)]
            db = jnp.sum(dconv, axis=0, keepdims=True)

            @pl.when(b_i == 0)
            def _():
                for r in range(3):
                    dw_ref[r:r + 1, :] = dws[r]
                db_ref[...] = db

            @pl.when(b_i > 0)
            def _():
                for r in range(3):
                    dw_ref[r:r + 1, :] += dws[r]
                db_ref[...] += db

        dz_ref[...] = stash[p]

    return _pcall(
        body, name="conv_bwd",
        out_shape=(_sds(dz.shape, dz.dtype), _sds((3, d), F32), _sds((1, d), F32)), grid=(nc, nb, 3),
        in_specs=[
            pl.BlockSpec((s, tc), lambda j, b, p: (b, j)), pl.BlockSpec((s, tc), lambda j, b, p: (b, nc + j)),
            pl.BlockSpec((s, tc), lambda j, b, p: (b, 2 * nc + j)), pl.BlockSpec((s, tc), lambda j, b, p: (b, j)),
            pl.BlockSpec((3, tc), lambda j, b, p: (0, j)), pl.BlockSpec((1, tc), lambda j, b, p: (0, j)),
            HBM_SPEC,
        ],
        out_specs=(pl.BlockSpec((s, tc), lambda j, b, p: (b, p * nc + j)),
                   pl.BlockSpec((3, tc), lambda j, b, p: (0, j)), pl.BlockSpec((1, tc), lambda j, b, p: (0, j))),
        scratch_shapes=[pltpu.VMEM((3, s, tc), BF16)],
        input_output_aliases={6: 0},
        compiler_params=_cp(dimension_semantics=("arbitrary", "arbitrary", "arbitrary")),
    )(z, z, z, dy, conv_w, conv_b, dz)


def _head_rms(xv, g):
    r = lax.rsqrt(jnp.mean(xv * xv, axis=-1, keepdims=True) + EPS)
    return xv * r * g


def _head_rms_bwd(xv, g, dy):
    r = lax.rsqrt(jnp.mean(xv * xv, axis=-1, keepdims=True) + EPS)
    xh = xv * r
    gd = dy * g
    dx = r * (gd - xh * jnp.mean(gd * xh, axis=-1, keepdims=True))
    return dx, jnp.sum(dy * xh, axis=0, keepdims=True)


def _fox_prep(z, q_g, k_g, *, t, d, tm=256):
    nh = d // HD

    def body(z_ref, qg_ref, kg_ref, q_ref, k_ref, v_ref):
        qg, kg = qg_ref[...], kg_ref[...]
        for h in range(nh):
            c0 = h * HD
            q_ref[:, c0:c0 + HD] = _head_rms(z_ref[:, c0:c0 + HD], qg).astype(BF16)
            k_ref[:, c0:c0 + HD] = _head_rms(z_ref[:, d + c0:d + c0 + HD], kg).astype(BF16)
        v_ref[...] = z_ref[:, 2 * d:3 * d].astype(BF16)

    o = pl.BlockSpec((tm, d), lambda i: (i, 0))
    g = pl.BlockSpec((1, HD), lambda i: (0, 0))
    return _pcall(
        body, name="fox_prep", out_shape=(_sds((t, d), BF16),) * 3, grid=(t // tm,),
        in_specs=[pl.BlockSpec((tm, 3 * d), lambda i: (i, 1)), g, g], out_specs=(o, o, o), compiler_params=_cp(),
    )(z, q_g, k_g)


def _lane_cumsum(v, reverse=False):
    n = v.shape[1]
    lane = lax.broadcasted_iota(jnp.int32, v.shape, 1)
    sh = 1
    while sh < n:
        if reverse:
            v = v + jnp.where(lane < n - sh, pltpu.roll(v, n - sh, axis=1), 0.0)
        else:
            v = v + jnp.where(lane >= sh, pltpu.roll(v, sh, axis=1), 0.0)
        sh *= 2
    return v


def _fox_gates(z, f_bias_col, *, nb, s, nh, ff_block):
    def body(z_ref, b_ref, c_ref):
        ff = z_ref[...].T[0:nh, :] + b_ref[...]
        lf = jnp.minimum(ff, 0.0) - jnp.log(1.0 + jnp.exp(-jnp.abs(ff)))
        c_ref[0] = _lane_cumsum(lf) * SQRT_HD

    return _pcall(
        body, name="fox_gates", out_shape=_sds((nb, nh, s), F32), grid=(nb,),
        in_specs=[pl.BlockSpec((s, HD), lambda b: (b, ff_block)), pl.BlockSpec((nh, 1), lambda b: (0, 0))],
        out_specs=pl.BlockSpec((1, nh, s), lambda b: (b, 0, 0)), compiler_params=_cp(),
    )(z, f_bias_col)


def _fox_gates_bwd(z, f_bias_col, row_sums, col_sums, dz, *, nb, s, nh, ff_block):
    def body(z_ref, b_ref, rs_ref, cs_ref, dz_in, dz_ref, db_ref, dc_sc):
        b_i, h = pl.program_id(0), pl.program_id(1)
        dc_sc[pl.ds(h, 1), :] = (rs_ref[...] - cs_ref[...]).T[0:1, :]

        @pl.when(h == nh - 1)
        def _():
            ff = z_ref[...].T[0:nh, :] + b_ref[...]
            dlf = _lane_cumsum(dc_sc[...], reverse=True)
            dff = dlf * (1.0 / (1.0 + jnp.exp(ff)))
            pad = jnp.concatenate([dff, jnp.zeros((HD - nh, s), F32)], axis=0)
            dz_ref[...] = pad.T.astype(BF16)
            dbv = jnp.sum(dff, axis=1, keepdims=True)

            @pl.when(b_i == 0)
            def _():
                db_ref[...] = dbv

            @pl.when(b_i > 0)
            def _():
                db_ref[...] += dbv

    hs = pl.BlockSpec((s, HD), lambda b, h: (b, h))
    zs = pl.BlockSpec((s, HD), lambda b, h: (b, ff_block))
    return _pcall(
        body, name="fox_gates_bwd", out_shape=(_sds(dz.shape, dz.dtype), _sds((nh, 1), F32)), grid=(nb, nh),
        in_specs=[zs, pl.BlockSpec((nh, 1), lambda b, h: (0, 0)), hs, hs, HBM_SPEC],
        out_specs=(zs, pl.BlockSpec((nh, 1), lambda b, h: (0, 0))),
        scratch_shapes=[pltpu.VMEM((nh, s), F32)],
        input_output_aliases={4: 0}, compiler_params=_cp(dimension_semantics=("arbitrary", "arbitrary")),
    )(z, f_bias_col, row_sums, col_sums, dz)


def _fox_pairs(nq, key_major):
    if key_major:
        pairs = [(i, j) for j in range(nq) for i in range(j, nq)]
    else:
        pairs = [(i, j) for i in range(nq) for j in range(i + 1)]
    return jnp.array([p[0] for p in pairs], jnp.int32), jnp.array([p[1] for p in pairs], jnp.int32)


def _with_ones(x):
    return jnp.concatenate([x, jnp.ones_like(x)], axis=1)


def _fox_specs(nb, nh, nq, tq, tk, hp):
    qs = pl.BlockSpec((tq, hp * HD), lambda b, h, p, it, jt: (b * nq + it[p], h))
    ks = pl.BlockSpec((tk, hp * HD), lambda b, h, p, it, jt: (b * nq + jt[p], h))
    cs = pl.BlockSpec((hp, 1, tk), lambda b, h, p, it, jt: (b * (nh // hp) + h, 0, jt[p]))
    return qs, ks, cs


def _fox_raw(qv, kv, ckv, diag, tq, tk):
    sc = lax.dot_general(qv, kv, (((1,), (1,)), ((), ())), preferred_element_type=F32) - ckv
    if diag:
        rows = lax.broadcasted_iota(jnp.int32, (tq, tk), 0)
        cols = lax.broadcasted_iota(jnp.int32, (tq, tk), 1)
        sc = jnp.where(rows >= cols, sc, NEG)
    return sc


def _fox_probs(qv, kv, ckv, lse_col, diag, tq, tk):
    sc = _fox_raw(qv, kv, ckv, diag, tq, tk)
    return jnp.exp2(sc * EXP2_C - lse_col * LOG2E)


def _fox_call(body, *, name, ins, in_specs, out_shape, out_specs, scratch, grid, tables, comm):
    c_ins, c_in_specs, c_out_shape, c_scratch = _comm_lists(comm)
    n_in, n_out, n_scr, nc = len(ins), len(out_shape), len(scratch), len(c_ins)
    n_steps = grid[0] * grid[1] * grid[2]

    def wrapped(*refs):
        pre, rin, cin, rout, cout, scr, csem = _split_refs(refs, 2, n_in, nc, n_out, nc, n_scr)
        step = (pl.program_id(0) * grid[1] + pl.program_id(1)) * grid[2] + pl.program_id(2)
        _run_comm(comm, cin, cout, csem, step, n_steps, "start")
        body(*pre, *rin, *rout, *scr)
        _run_comm(comm, cin, cout, csem, step, n_steps, "end")

    sem = ("arbitrary",) * 3 if comm is not None else ("parallel", "parallel", "arbitrary")
    outs = _pcall(
        wrapped, name=name, out_shape=tuple(list(out_shape) + c_out_shape),
        grid_spec=pltpu.PrefetchScalarGridSpec(
            num_scalar_prefetch=2, grid=grid, in_specs=list(in_specs) + c_in_specs,
            out_specs=tuple(list(out_specs) + [HBM_SPEC] * nc), scratch_shapes=list(scratch) + c_scratch),
        compiler_params=_cp(dimension_semantics=sem),
    )(*tables, *ins, *c_ins)
    return list(outs[:n_out]), list(outs[n_out:])


def _fox_fwd(q, k, v, cks, *, nb, s, nh, tq=512, comm=None):
    tk = tq
    nq = s // tq
    t = nb * s
    hp = FOX_HP
    it, jt = _fox_pairs(nq, key_major=False)

    def body(it_ref, jt_ref, q_ref, k_ref, v_ref, c_ref, o_ref, lse_ref, m_sc, acc_sc):
        p_id = pl.program_id(2)
        i, j = it_ref[p_id], jt_ref[p_id]

        @pl.when(j == 0)
        def _():
            m_sc[...] = jnp.full_like(m_sc, NEG)
            acc_sc[...] = jnp.zeros_like(acc_sc)

        def step(diag):
            for hh in range(hp):
                cols = slice(hh * HD, (hh + 1) * HD)
                sc = _fox_raw(q_ref[:, cols], k_ref[:, cols], c_ref[hh], diag, tq, tk)
                m_old = m_sc[hh]
                m_new = jnp.maximum(m_old, jnp.max(sc, axis=-1, keepdims=True))
                alpha = jnp.exp2((m_old - m_new) * EXP2_C)
                p = jnp.exp2((sc - m_new) * EXP2_C).astype(BF16)
                acc = alpha * acc_sc[hh] + jnp.dot(p, _with_ones(v_ref[:, cols]), preferred_element_type=F32)
                if diag:
                    l = acc[:, HD:2 * HD]
                    o_ref[:, cols] = (acc[:, 0:HD] / l).astype(BF16)
                    lse_ref[:, cols] = m_new * (1.0 / SQRT_HD) + jnp.log(l)
                else:
                    acc_sc[hh] = acc
                    m_sc[hh] = m_new

        @pl.when(j < i)
        def _():
            step(False)

        @pl.when(j == i)
        def _():
            step(True)

    qs, ks, cs = _fox_specs(nb, nh, nq, tq, tk, hp)
    (o, lse), c_out = _fox_call(
        body, name="fox_fwd", ins=[q, k, v, cks], in_specs=[qs, ks, ks, cs],
        out_shape=[_sds((t, nh * HD), BF16), _sds((t, nh * HD), F32)], out_specs=[qs, qs],
        scratch=[pltpu.VMEM((hp, tq, 1), F32), pltpu.VMEM((hp, tq, 2 * HD), F32)],
        grid=(nb, nh // hp, int(it.shape[0])), tables=(it, jt), comm=comm)
    return o, lse, c_out


def _fox_bwd_dq(q, k, v, cks, do, o, lse, *, nb, s, nh, tq=512, comm=None):
    tk = tq
    nq = s // tq
    t = nb * s
    hp = FOX_HP
    it, jt = _fox_pairs(nq, key_major=False)

    def body(it_ref, jt_ref, q_ref, k_ref, v_ref, c_ref, do_ref, o_ref, lse_ref, dq_ref, rs_ref, acc_sc, dl_sc):
        p_id = pl.program_id(2)
        i, j = it_ref[p_id], jt_ref[p_id]

        @pl.when(j == 0)
        def _():
            acc_sc[...] = jnp.zeros_like(acc_sc)
            for hh in range(hp):
                cols = slice(hh * HD, (hh + 1) * HD)
                dl_sc[hh] = jnp.sum(do_ref[:, cols] * o_ref[:, cols].astype(F32), axis=-1, keepdims=True)

        def step(diag):
            for hh in range(hp):
                cols = slice(hh * HD, (hh + 1) * HD)
                kv = k_ref[:, cols]
                p = _fox_probs(q_ref[:, cols], kv, c_ref[hh], lse_ref[:, hh * HD:hh * HD + 1], diag, tq, tk)
                dp = lax.dot_general(do_ref[:, cols].astype(BF16), v_ref[:, cols], (((1,), (1,)), ((), ())),
                                     preferred_element_type=F32)
                ds = (p * (dp - dl_sc[hh])).astype(BF16)
                acc = acc_sc[hh] + jnp.dot(ds, _with_ones(kv), preferred_element_type=F32)
                if diag:
                    dq_ref[:, cols] = acc[:, 0:HD] * (1.0 / SQRT_HD)
                    rs_ref[:, cols] = acc[:, HD:2 * HD]
                else:
                    acc_sc[hh] = acc

        @pl.when(j < i)
        def _():
            step(False)

        @pl.when(j == i)
        def _():
            step(True)

    qs, ks, cs = _fox_specs(nb, nh, nq, tq, tk, hp)
    (dq, rs), c_out = _fox_call(
        body, name="fox_bwd_dq", ins=[q, k, v, cks, do, o, lse], in_specs=[qs, ks, ks, cs, qs, qs, qs],
        out_shape=[_sds((t, nh * HD), F32), _sds((t, nh * HD), F32)], out_specs=[qs, qs],
        scratch=[pltpu.VMEM((hp, tq, 2 * HD), F32), pltpu.VMEM((hp, tq, 1), F32)],
        grid=(nb, nh // hp, int(it.shape[0])), tables=(it, jt), comm=comm)
    return dq, rs, c_out


def _fox_bwd_dkv(q, k, v, cks, do, o, lse, *, nb, s, nh, tq=512, comm=None):
    tk = tq
    nq = s // tq
    t = nb * s
    hp = FOX_HP
    it, jt = _fox_pairs(nq, key_major=True)

    def body(it_ref, jt_ref, q_ref, k_ref, v_ref, c_ref, do_ref, o_ref, lse_ref, dk_ref, dv_ref, cs_ref, dk_sc, dv_sc):
        p_id = pl.program_id(2)
        i, j = it_ref[p_id], jt_ref[p_id]

        def step(diag):
            for hh in range(hp):
                cols = slice(hh * HD, (hh + 1) * HD)
                qv = q_ref[:, cols]
                p = _fox_probs(qv, k_ref[:, cols], c_ref[hh], lse_ref[:, hh * HD:hh * HD + 1], diag, tq, tk)
                dov = do_ref[:, cols]
                dob = dov.astype(BF16)
                dvp = lax.dot_general(p.astype(BF16), dob, (((0,), (0,)), ((), ())), preferred_element_type=F32)
                dp = lax.dot_general(dob, v_ref[:, cols], (((1,), (1,)), ((), ())), preferred_element_type=F32)
                delta = jnp.sum(dov * o_ref[:, cols].astype(F32), axis=-1, keepdims=True)
                ds = (p * (dp - delta)).astype(BF16)
                dkp = lax.dot_general(ds, _with_ones(qv), (((0,), (0,)), ((), ())), preferred_element_type=F32)
                if diag:
                    dk_sc[hh] = dkp
                    dv_sc[hh] = dvp
                else:
                    dk_sc[hh] += dkp
                    dv_sc[hh] += dvp

        @pl.when(i == j)
        def _():
            step(True)

        @pl.when(i > j)
        def _():
            step(False)

        @pl.when(i == nq - 1)
        def _():
            for hh in range(hp):
                cols = slice(hh * HD, (hh + 1) * HD)
                dk_ref[:, cols] = dk_sc[hh, :, 0:HD] * (1.0 / SQRT_HD)
                cs_ref[:, cols] = dk_sc[hh, :, HD:2 * HD]
                dv_ref[:, cols] = dv_sc[hh].astype(BF16)

    qs, ks, cs = _fox_specs(nb, nh, nq, tq, tk, hp)
    (dk, dv, csum), c_out = _fox_call(
        body, name="fox_bwd_dkv", ins=[q, k, v, cks, do, o, lse], in_specs=[qs, ks, ks, cs, qs, qs, qs],
        out_shape=[_sds((t, nh * HD), F32), _sds((t, nh * HD), BF16), _sds((t, nh * HD), F32)], out_specs=[ks, ks, ks],
        scratch=[pltpu.VMEM((hp, tk, 2 * HD), F32), pltpu.VMEM((hp, tk, HD), F32)],
        grid=(nb, nh // hp, int(it.shape[0])), tables=(it, jt), comm=comm)
    return dk, dv, csum, c_out


def _fox_post(z, dqn, dkn, dv, q_g, k_g, dz, *, t, d, tm=256):
    nh = d // HD

    def body(z_ref, dq_ref, dk_ref, dv_ref, qg_ref, kg_ref, dz_in, dz_ref, dqg_ref, dkg_ref):
        qg, kg = qg_ref[...], kg_ref[...]
        aq = jnp.zeros((1, HD), F32)
        ak = jnp.zeros((1, HD), F32)
        for h in range(nh):
            c0 = h * HD
            dx, sg = _head_rms_bwd(z_ref[:, c0:c0 + HD], qg, dq_ref[:, c0:c0 + HD])
            dz_ref[:, c0:c0 + HD] = dx.astype(BF16)
            aq = aq + sg
            dx, sg = _head_rms_bwd(z_ref[:, d + c0:d + c0 + HD], kg, dk_ref[:, c0:c0 + HD])
            dz_ref[:, d + c0:d + c0 + HD] = dx.astype(BF16)
            ak = ak + sg
        dz_ref[:, 2 * d:3 * d] = dv_ref[...]

        @pl.when(pl.program_id(0) == 0)
        def _():
            dqg_ref[...] = aq
            dkg_ref[...] = ak

        @pl.when(pl.program_id(0) > 0)
        def _():
            dqg_ref[...] += aq
            dkg_ref[...] += ak

    o = pl.BlockSpec((tm, d), lambda i: (i, 0))
    g = pl.BlockSpec((1, HD), lambda i: (0, 0))
    z3 = pl.BlockSpec((tm, 3 * d), lambda i: (i, 1))
    return _pcall(
        body, name="fox_post", out_shape=(_sds(dz.shape, dz.dtype), _sds((1, HD), F32), _sds((1, HD), F32)), grid=(t // tm,),
        in_specs=[z3, o, o, o, g, g, HBM_SPEC], out_specs=(z3, g, g),
        input_output_aliases={6: 0}, compiler_params=_cp(dimension_semantics=("arbitrary",)),
    )(z, dqn, dkn, dv, q_g, k_g, dz)


def _xa_prep_kv(kv, k_g, *, rows, d):
    xd = d // XH

    def body(kv_ref, g_ref, k_ref, v_ref):
        g = g_ref[...]
        for h in range(XH):
            c0 = h * xd
            k_ref[:, c0:c0 + xd] = _head_rms(kv_ref[:, c0:c0 + xd], g).astype(BF16)
        v_ref[...] = kv_ref[:, d:2 * d].astype(BF16)

    return _pcall(body, name="xa_prep_kv", out_shape=(_sds((rows, d), BF16),) * 2, compiler_params=_cp())(kv, k_g)


def _xa_fwd(z, kn, vb, q_g, *, nb, s, d, nm, q_block, tq=512):
    xd = d // XH
    nq = s // tq

    def body(z_ref, k_ref, v_ref, g_ref, y_ref):
        g = g_ref[...]
        for h in range(XH):
            c0 = h * xd
            qn = _head_rms(z_ref[:, c0:c0 + xd], g).astype(BF16)
            sc = lax.dot_general(qn, k_ref[:, c0:c0 + xd], (((1,), (1,)), ((), ())), preferred_element_type=F32)
            sc = sc / math.sqrt(xd)
            p = jnp.exp(sc - jnp.max(sc, axis=-1, keepdims=True))
            p = p / jnp.sum(p, axis=-1, keepdims=True)
            y_ref[:, c0:c0 + xd] = jnp.dot(p.astype(BF16), v_ref[:, c0:c0 + xd], preferred_element_type=F32).astype(BF16)

    kvs = pl.BlockSpec((nm, d), lambda b, i: (b, 0))
    return _pcall(
        body, name="xa_fwd", out_shape=_sds((nb * s, d), BF16), grid=(nb, nq),
        in_specs=[pl.BlockSpec((tq, d), lambda b, i: (b * nq + i, q_block)), kvs, kvs, pl.BlockSpec((1, xd), lambda b, i: (0, 0))],
        out_specs=pl.BlockSpec((tq, d), lambda b, i: (b * nq + i, 0)), compiler_params=_cp(),
    )(z, kn, vb, q_g)


def _xa_bwd(z, kn, vb, q_g, dy, dz, *, nb, s, d, nm, q_block, tq=512):
    xd = d // XH
    nq = s // tq

    def body(z_ref, k_ref, v_ref, g_ref, dy_ref, dz_in, dz_ref, dk_ref, dv_ref, dg_ref):
        b_i, i = pl.program_id(0), pl.program_id(1)
        g = g_ref[...]

        @pl.when(i == 0)
        def _():
            dk_ref[...] = jnp.zeros_like(dk_ref)
            dv_ref[...] = jnp.zeros_like(dv_ref)

        @pl.when((i == 0) & (b_i == 0))
        def _():
            dg_ref[...] = jnp.zeros_like(dg_ref)

        for h in range(XH):
            c0 = h * xd
            xq = z_ref[:, c0:c0 + xd]
            qn = _head_rms(xq, g).astype(BF16)
            kh, vh = k_ref[:, c0:c0 + xd], v_ref[:, c0:c0 + xd]
            sc = lax.dot_general(qn, kh, (((1,), (1,)), ((), ())), preferred_element_type=F32) / math.sqrt(xd)
            p = jnp.exp(sc - jnp.max(sc, axis=-1, keepdims=True))
            p = p / jnp.sum(p, axis=-1, keepdims=True)
            dob = dy_ref[:, c0:c0 + xd].astype(BF16)
            dv_ref[:, c0:c0 + xd] += lax.dot_general(p.astype(BF16), dob, (((0,), (0,)), ((), ())), preferred_element_type=F32)
            dp = lax.dot_general(dob, vh, (((1,), (1,)), ((), ())), preferred_element_type=F32)
            ds = p * (dp - jnp.sum(p * dp, axis=-1, keepdims=True)) / math.sqrt(xd)
            dsb = ds.astype(BF16)
            dqn = jnp.dot(dsb, kh, preferred_element_type=F32)
            dk_ref[:, c0:c0 + xd] += lax.dot_general(dsb, qn, (((0,), (0,)), ((), ())), preferred_element_type=F32)
            dx, sg = _head_rms_bwd(xq, g, dqn)
            dz_ref[:, c0:c0 + xd] = dx.astype(BF16)
            dg_ref[...] += sg

    kvs = pl.BlockSpec((nm, d), lambda b, i: (b, 0))
    qs = pl.BlockSpec((tq, d), lambda b, i: (b * nq + i, q_block))
    gs = pl.BlockSpec((1, xd), lambda b, i: (0, 0))
    return _pcall(
        body, name="xa_bwd",
        out_shape=(_sds(dz.shape, dz.dtype), _sds((nb * nm, d), F32), _sds((nb * nm, d), F32), _sds((1, xd), F32)), grid=(nb, nq),
        in_specs=[qs, kvs, kvs, gs, pl.BlockSpec((tq, d), lambda b, i: (b * nq + i, 0)), HBM_SPEC],
        out_specs=(qs, kvs, kvs, gs), input_output_aliases={5: 0},
        compiler_params=_cp(dimension_semantics=("arbitrary", "arbitrary")),
    )(z, kn, vb, q_g, dy, dz)


def _xa_post_kv(kv, dkn, dv, k_g, *, rows, d):
    xd = d // XH

    def body(kv_ref, dk_ref, dv_ref, g_ref, dkv_ref, dg_ref):
        g = g_ref[...]
        acc = jnp.zeros((1, xd), F32)
        for h in range(XH):
            c0 = h * xd
            dx, sg = _head_rms_bwd(kv_ref[:, c0:c0 + xd], g, dk_ref[:, c0:c0 + xd])
            dkv_ref[:, c0:c0 + xd] = dx.astype(BF16)
            acc = acc + sg
        dkv_ref[:, d:2 * d] = dv_ref[...].astype(BF16)
        dg_ref[...] = acc

    return _pcall(body, name="xa_post_kv", out_shape=(_sds((rows, 2 * d), BF16), _sds((1, xd), F32)), compiler_params=_cp())(
        kv, dkn, dv, k_g)


def _sigmoid(v):
    return 1.0 / (1.0 + jnp.exp(-v))


def _branches_fwd(z, ys, ws, *, t, d, gate_block, tm=512, tn=512):
    nj = d // tn

    def body(ya_ref, yb_ref, yc_ref, wa_ref, wb_ref, wc_ref, ga_ref, gb_ref, gc_ref, m_ref, p_ref):
        acc = None
        for q, (y_ref, w_ref, g_ref) in enumerate(((ya_ref, wa_ref, ga_ref), (yb_ref, wb_ref, gb_ref), (yc_ref, wc_ref, gc_ref))):
            pr = jnp.dot(y_ref[...], w_ref[...], preferred_element_type=F32)
            p_ref[q] = pr.astype(BF16)
            term = _sigmoid(g_ref[...]) * pr
            acc = term if acc is None else acc + term
        m_ref[...] = acc.astype(BF16)

    y_spec = pl.BlockSpec((tm, d), lambda i, j: (i, 0))
    w_spec = pl.BlockSpec((d, tn), lambda i, j: (0, j))

    def gate(q):
        return pl.BlockSpec((tm, tn), lambda i, j: (i, (gate_block + q) * nj + j))

    return _pcall(
        body, name="branches_fwd", out_shape=(_sds((t, d), BF16), _sds((3, t, d), BF16)), grid=(t // tm, nj),
        in_specs=[y_spec] * 3 + [w_spec] * 3 + [gate(0), gate(1), gate(2)],
        out_specs=(pl.BlockSpec((tm, tn), lambda i, j: (i, j)), pl.BlockSpec((3, tm, tn), lambda i, j: (0, i, j))),
        compiler_params=_cp(),
    )(*ys, *ws, z, z, z)


def _gates_bwd(z, dm, proj, dz, *, t, d, gate_block, tm=512):
    def body(g_ref, dm_ref, p_ref, dz_in, dz_ref, da_ref, db_ref, dc_ref):
        q = pl.program_id(1)
        sg = _sigmoid(g_ref[...])
        dmv = dm_ref[...]
        dz_ref[...] = (dmv * p_ref[0].astype(F32) * sg * (1.0 - sg)).astype(BF16)
        dp = (dmv * sg).astype(BF16)
        @pl.when(q == 0)
        def _():
            da_ref[...] = dp

        @pl.when(q == 1)
        def _():
            db_ref[...] = dp

        @pl.when(q == 2)
        def _():
            dc_ref[...] = dp

    gs = pl.BlockSpec((tm, d), lambda i, q: (i, gate_block + q))
    o = pl.BlockSpec((tm, d), lambda i, q: (i, 0))
    return _pcall(
        body, name="gates_bwd", out_shape=(_sds(dz.shape, dz.dtype),) + (_sds((t, d), BF16),) * 3, grid=(t // tm, 3),
        in_specs=[gs, o, pl.BlockSpec((1, tm, d), lambda i, q: (q, i, 0)), HBM_SPEC], out_specs=(gs, o, o, o),
        input_output_aliases={3: 0}, compiler_params=_cp(dimension_semantics=("arbitrary", "arbitrary")),
    )(z, dm, proj, dz)


def _ffn_in_fwd(h2, wfi_t, *, t, d, dff, tm=512, tn=1408):
    nj = dff // tn

    def body(h_ref, wg_ref, wu_ref, g_ref, u_ref, a_ref):
        hv = h_ref[...]
        dn = (((1,), (1,)), ((), ()))
        gv = lax.dot_general(hv, wg_ref[...], dn, preferred_element_type=F32)
        uv = lax.dot_general(hv, wu_ref[...], dn, preferred_element_type=F32)
        g_ref[...] = gv.astype(BF16)
        u_ref[...] = uv.astype(BF16)
        a_ref[...] = (gv * _sigmoid(gv) * uv).astype(BF16)

    o = pl.BlockSpec((tm, tn), lambda i, j: (i, j))
    return _pcall(
        body, name="ffn_in_fwd", out_shape=(_sds((t, dff), BF16),) * 3, grid=(t // tm, nj),
        in_specs=[pl.BlockSpec((tm, d), lambda i, j: (i, 0)), pl.BlockSpec((tn, d), lambda i, j: (j, 0)),
                  pl.BlockSpec((tn, d), lambda i, j: (nj + j, 0))],
        out_specs=(o, o, o), compiler_params=_cp(),
    )(h2, wfi_t, wfi_t)


def _ffn_out_bwd(dy, wfo, gate, up, *, t, d, dff, tm=512, tn=1408):
    def body(dy_ref, w_ref, g_ref, u_ref, dg_ref, du_ref):
        dav = lax.dot_general(dy_ref[...].astype(BF16), w_ref[...], (((1,), (1,)), ((), ())), preferred_element_type=F32)
        gv = g_ref[...].astype(F32)
        sg = _sigmoid(gv)
        dg_ref[...] = (dav * u_ref[...].astype(F32) * sg * (1.0 + gv * (1.0 - sg))).astype(BF16)
        du_ref[...] = (dav * gv * sg).astype(BF16)

    o = pl.BlockSpec((tm, tn), lambda i, j: (i, j))
    return _pcall(
        body, name="ffn_out_bwd", out_shape=(_sds((t, dff), BF16),) * 2, grid=(t // tm, dff // tn),
        in_specs=[pl.BlockSpec((tm, d), lambda i, j: (i, 0)), pl.BlockSpec((tn, d), lambda i, j: (j, 0)), o, o],
        out_specs=(o, o), compiler_params=_cp(),
    )(dy, wfo, gate, up)


def _loss_head(y, target, *, t, d, tm=512):
    def body(y_ref, t_ref, dy_ref, l_ref):
        e = y_ref[...] - t_ref[...]
        dy_ref[...] = e * (1.0 / d)

        @pl.when(pl.program_id(0) == 0)
        def _():
            l_ref[...] = jnp.zeros_like(l_ref)

        l_ref[...] += jnp.sum(jnp.sum(e * e, axis=-1, keepdims=True), axis=0, keepdims=True) * (0.5 / d)

    o = pl.BlockSpec((tm, d), lambda i: (i, 0))
    dy, l = _pcall(
        body, name="loss_head", out_shape=(_sds((t, d), F32), _sds((1, HD), F32)), grid=(t // tm,),
        in_specs=[o, o], out_specs=(o, pl.BlockSpec((1, HD), lambda i: (0, 0))),
        compiler_params=_cp(dimension_semantics=("arbitrary",)),
    )(y, target)
    return dy, l[0, 0]


def _adamw(w, g, m, v, *, name):
    if w.ndim == 3:
        rows, _, cols = w.shape
        tm = max(tt for tt in range(1, 65) if rows % tt == 0)
        block, imap, grid = (tm, 1, cols), (lambda i, j: (i, 0, 0)), (rows // tm, 1)
    else:
        rows, cols = w.shape
        tm, tn = rows, cols
        for cand in (256, 128, 64, 32, 16, 8):
            if rows > cand and rows % cand == 0:
                tm = cand
                break
        if tm == rows and rows > 256 and cols % 128 == 0:
            tn = 128
        block, imap, grid = (tm, tn), (lambda i, j: (i, j)), (rows // tm, cols // tn)
    c1 = 1.0 - ADAM_B1 ** ADAM_STEP
    c2 = 1.0 - ADAM_B2 ** ADAM_STEP

    def body(w_ref, g_ref, m_ref, v_ref, d_ref, nm_ref, nv_ref):
        gv = g_ref[...]
        mn = ADAM_B1 * m_ref[...] + (1.0 - ADAM_B1) * gv
        vn = ADAM_B2 * v_ref[...] + (1.0 - ADAM_B2) * (gv * gv)
        d_ref[...] = -ADAM_LR * ((mn / c1) / (jnp.sqrt(vn / c2) + ADAM_EPS) + ADAM_WD * w_ref[...])
        nm_ref[...] = mn
        nv_ref[...] = vn

    spec = pl.BlockSpec(block, imap)
    return _pcall(
        body, name=name, out_shape=(_sds(w.shape, F32),) * 3, grid=grid,
        in_specs=[spec] * 4, out_specs=(spec,) * 3, compiler_params=_cp(),
    )(w, g, m, v)


def _pad_rows(a, rows):
    return a if a.shape[0] == rows else jnp.pad(a, ((0, rows - a.shape[0]),) + ((0, 0),) * (a.ndim - 1))


class _Pack:
    def __init__(self, row_sizes):
        self.rows = list(row_sizes)
        self.padded = [-(-r // 16) * 16 for r in self.rows]
        self.offs = [sum(self.padded[:i]) for i in range(len(self.rows))]
        self.total = sum(self.padded)

    def stack(self, blocks):
        return jnp.concatenate([_pad_rows(b.astype(BF16), p) for b, p in zip(blocks, self.padded)], axis=0)

    def full(self, gathered, i):
        r = self.rows[i]
        return gathered[:, self.offs[i]:self.offs[i] + r, :].reshape(N_DEV * r, gathered.shape[2])

    def for_core(self, full_grads, core):
        parts = []
        for gfull, r, p in zip(full_grads, self.rows, self.padded):
            blk = gfull.reshape(N_DEV // 2, 2, r, gfull.shape[1])
            blk = lax.dynamic_index_in_dim(blk, core, axis=1, keepdims=False).astype(BF16)
            if p != r:
                blk = jnp.pad(blk, ((0, 0), (0, p - r), (0, 0)))
            parts.append(blk)
        return jnp.concatenate(parts, axis=1)

    def shard(self, g_pack, i):
        return g_pack[self.offs[i]:self.offs[i] + self.rows[i], :]


def kernel(x, mem, norm1_g, w_in, conv_w, conv_b, fox_f_bias, fox_q_g, fox_k_g, mem_norm_g, w_mem_kv, xa_q_g, xa_k_g, w_br_conv, w_br_fox, w_br_xa, w_o, norm2_g, w_ffn_in, w_ffn_out, loss_target, m_norm1_g, m_w_in, m_conv_w, m_conv_b, m_fox_f_bias, m_fox_q_g, m_fox_k_g, m_mem_norm_g, m_w_mem_kv, m_xa_q_g, m_xa_k_g, m_w_br_conv, m_w_br_fox, m_w_br_xa, m_w_o, m_norm2_g, m_w_ffn_in, m_w_ffn_out, v_norm1_g, v_w_in, v_conv_w, v_conv_b, v_fox_f_bias, v_fox_q_g, v_fox_k_g, v_mem_norm_g, v_w_mem_kv, v_xa_q_g, v_xa_k_g, v_w_br_conv, v_w_br_fox, v_w_br_xa, v_w_o, v_norm2_g, v_w_ffn_in, v_w_ffn_out):
    nb, s, d = x.shape
    t = nb * s
    nm = mem.shape[1]
    nh = d // HD
    dff = w_ffn_out.shape[1] * N_DEV
    n_in = w_in.shape[2] * N_DEV
    n_in_pad = -(-n_in // 1152) * 1152
    ff_block = (10 * d) // HD
    my_c = lax.axis_index("c")
    my_q = 2 * lax.axis_index("x") + lax.axis_index("y")

    pack_a = _Pack([w_in.shape[2]])
    pack_b = _Pack([w_mem_kv.shape[2], w_br_conv.shape[1], w_br_fox.shape[1], w_br_xa.shape[1], w_o.shape[1],
                    w_ffn_in.shape[2], w_ffn_out.shape[1]])
    gather_1 = _Pack([w_ffn_in.shape[2], w_ffn_out.shape[1]])
    gather_2 = _Pack([w_mem_kv.shape[2], w_br_conv.shape[1], w_br_fox.shape[1], w_br_xa.shape[1], w_o.shape[1]])
    stack_a = pack_a.stack([w_in[0].T])
    stack_1 = gather_1.stack([w_ffn_in[0].T, w_ffn_out[0]])
    stack_2 = gather_2.stack([w_mem_kv[0].T, w_br_conv[0], w_br_fox[0], w_br_xa[0], w_o[0]])
    cw_block = _pad_rows(conv_w[0], 8)
    (slabs,) = _comm_only(_AllGather([stack_a]), "all_gather_w_in")
    rows_a = w_in.shape[2]
    ff_rows = _pad_rows(slabs[N_DEV - 1, rows_a - nh:rows_a, :], HD)

    x2 = x.reshape(t, d)
    mem2 = mem.reshape(nb * nm, d)
    tgt2 = loss_target.reshape(t, d)
    h1 = _rms_fwd(x2, norm1_g, name="rms1_fwd")
    z, (gathered_1,) = _z_from_slabs(h1, slabs, t=t, d=d, n_pad=n_in_pad, rows=rows_a, comm=_AllGather([stack_1]))
    z = _z_tail(h1, ff_rows, z, t=t, d=d, col_block=ff_block)
    qn, kn, vb = _fox_prep(z, fox_q_g, fox_k_g, t=t, d=d)
    f_bias_col = fox_f_bias.reshape(nh, 1)
    ck_rows = _fox_gates(z, f_bias_col, nb=nb, s=s, nh=nh, ff_block=ff_block).reshape(nb * nh, 1, s)
    y_fox, lse, (gathered_2, cw_g) = _fox_fwd(qn, kn, vb, ck_rows, nb=nb, s=s, nh=nh, comm=_AllGather([stack_2, cw_block]))
    wfi_t, wfo = (gather_1.full(gathered_1, i) for i in range(2))
    wkv_t, wbc, wbf, wbx, wo = (gather_2.full(gathered_2, i) for i in range(5))
    conv_w_full = jnp.transpose(cw_g[:, 0:3, :], (1, 0, 2)).reshape(3, d)

    y_conv = _conv_fwd(z, conv_w_full, conv_b, nb=nb, s=s, d=d)
    mem_n = _rms_fwd(mem2, mem_norm_g, name="rms_mem_fwd")
    kv = _mm(mem_n, wkv_t, m=nb * nm, n=2 * d, k=d, tb=True, name="mm_kv")
    xkn, xvb = _xa_prep_kv(kv, xa_k_g, rows=nb * nm, d=d)
    y_xa = _xa_fwd(z, xkn, xvb, xa_q_g, nb=nb, s=s, d=d, nm=nm, q_block=6)
    merged, proj = _branches_fwd(z, (y_conv, y_fox, y_xa), (wbc, wbf, wbx), t=t, d=d, gate_block=7)
    x1 = _mm(merged, wo, m=t, n=d, k=d, res=x2, name="mm_x1")
    h2 = _rms_fwd(x1, norm2_g, name="rms2_fwd")
    ff_gate, ff_up, act = _ffn_in_fwd(h2, wfi_t, t=t, d=d, dff=dff)
    y = _mm(act, wfo, m=t, n=d, k=dff, res=x1, tk=dff, name="mm_y")
    dy, loss_local = _loss_head(y, tgt2, t=t, d=d)

    g_wfo = _mm(act, dy, m=dff, n=d, k=t, ta=True, tm=1408, tk=2048, out_dtype=BF16, name="mm_g_wfo")
    dgate, dup = _ffn_out_bwd(dy, wfo, ff_gate, ff_up, t=t, d=d, dff=dff)
    dh2 = _mm(dgate, wfi_t, m=t, n=d, k=dff, tk=dff, name="mm_dh2_g")
    dh2 = _mm(dup, wfi_t, m=t, n=d, k=dff, tk=dff, b_off=(1, 0), res=dh2, name="mm_dh2_u")
    g_wfi_g = _mm(dgate, h2, m=dff, n=d, k=t, ta=True, tm=1408, tk=2048, out_dtype=BF16, name="mm_g_wfi_g")
    g_wfi_u = _mm(dup, h2, m=dff, n=d, k=t, ta=True, tm=1408, tk=2048, out_dtype=BF16, name="mm_g_wfi_u")
    dx1, g_norm2 = _rms_bwd(dh2, x1, norm2_g, dy, name="rms2_bwd")
    dmerged = _mm(dx1, wo, m=t, n=d, k=d, tb=True, name="mm_dmerged")
    g_wo = _mm(merged, dx1, m=d, n=d, k=t, ta=True, tk=2048, out_dtype=BF16, name="mm_g_wo")

    dz = lax.empty((t, n_in_pad), BF16)
    dz, dpa, dpb, dpc = _gates_bwd(z, dmerged, proj, dz, t=t, d=d, gate_block=7)
    dy_conv = _mm(dpa, wbc, m=t, n=d, k=d, tb=True, name="mm_dy_conv")
    g_wbc = _mm(y_conv, dpa, m=d, n=d, k=t, ta=True, tk=4096, out_dtype=BF16, name="mm_g_wbc")
    dy_fox = _mm(dpb, wbf, m=t, n=d, k=d, tb=True, name="mm_dy_fox")
    g_wbf = _mm(y_fox, dpb, m=d, n=d, k=t, ta=True, tk=4096, out_dtype=BF16, name="mm_g_wbf")
    dy_xa = _mm(dpc, wbx, m=t, n=d, k=d, tb=True, name="mm_dy_xa")
    g_wbx = _mm(y_xa, dpc, m=d, n=d, k=t, ta=True, tk=4096, out_dtype=BF16, name="mm_g_wbx")

    dz, g_conv_w, g_conv_b = _conv_bwd(z, dy_conv, conv_w_full, conv_b, dz, nb=nb, s=s, d=d)
    dz, dxkn, dxv, g_xa_q = _xa_bwd(z, xkn, xvb, xa_q_g, dy_xa, dz, nb=nb, s=s, d=d, nm=nm, q_block=6)
    dkv, g_xa_k = _xa_post_kv(kv, dxkn, dxv, xa_k_g, rows=nb * nm, d=d)
    g_wkv_t = _mm(dkv, mem_n, m=2 * d, n=d, k=nb * nm, ta=True, out_dtype=BF16, name="mm_g_wkv")
    dmem_n = _mm(dkv, wkv_t, m=nb * nm, n=d, k=2 * d, tk=2 * d, name="mm_dmem")
    _, g_mem_norm = _rms_bwd(dmem_n, mem2, mem_norm_g, None, name="rms_mem_bwd")

    grads_b = [g_wkv_t, g_wbc, g_wbf, g_wbx, g_wo, jnp.concatenate([g_wfi_g, g_wfi_u], axis=0), g_wfo]
    keep_b, send_b = pack_b.for_core(grads_b, my_c), pack_b.for_core(grads_b, 1 - my_c)
    dqn, ds_rows, (from_sibling_b,) = _fox_bwd_dq(qn, kn, vb, ck_rows, dy_fox, y_fox, lse, nb=nb, s=s, nh=nh,
                                                  comm=_SiblingSwap(send_b))
    part_b = _add2(keep_b, from_sibling_b, name="rs_add_sibling_b")
    dkn, dvb, ds_cols, (from_chips_b,) = _fox_bwd_dkv(qn, kn, vb, ck_rows, dy_fox, y_fox, lse, nb=nb, s=s, nh=nh,
                                                     comm=_ChipExchange(part_b))
    g_pack_b = _add4(lax.dynamic_index_in_dim(part_b, my_q, axis=0, keepdims=False), from_chips_b, name="rs_add_chips_b")
    dz, g_fox_q, g_fox_k = _fox_post(z, dqn, dkn, dvb, fox_q_g, fox_k_g, dz, t=t, d=d)
    dz, g_f_bias = _fox_gates_bwd(z, f_bias_col, ds_rows, ds_cols, dz, nb=nb, s=s, nh=nh, ff_block=ff_block)

    g_slabs = _g_win_slabs(dz, h1, t=t, d=d, n_slab=N_DEV, rows=rows_a, rp=slabs.shape[1])
    g_slabs = g_slabs.reshape(N_DEV // 2, 2, slabs.shape[1], d)
    keep_a = lax.dynamic_index_in_dim(g_slabs, my_c, axis=1, keepdims=False)
    send_a = lax.dynamic_index_in_dim(g_slabs, 1 - my_c, axis=1, keepdims=False)
    dh1, (from_sibling_a,) = _dh1_from_slabs(dz, slabs, t=t, d=d, rows=rows_a, comm=_SiblingSwap(send_a))
    part_a = _add2(keep_a, from_sibling_a, name="rs_add_sibling_a")
    (from_chips_a,) = _comm_only(_ChipExchange(part_a), "rs_chips_a")
    g_pack_a = _add4(lax.dynamic_index_in_dim(part_a, my_q, axis=0, keepdims=False), from_chips_a, name="rs_add_chips_a",
                     rows_apart=True)
    dx, g_norm1 = _rms_bwd(dh1, x2, norm1_g, dx1, name="rms1_bwd")

    xd = d // XH
    tail = jnp.concatenate(
        [_pad_rows(g_f_bias.reshape(nh), HD).reshape(1, HD), g_fox_q, g_fox_k, g_xa_q, g_xa_k,
         jnp.zeros((1, d - 3 * HD - 2 * xd), F32)], axis=1)
    small = jnp.concatenate([g_norm1, g_norm2, g_conv_w, g_conv_b, g_mem_norm, tail], axis=0)
    small = _all_reduce_small(small)
    gs_conv_w = lax.dynamic_slice_in_dim(small[2:5], (2 * my_q + my_c) * (d // N_DEV), d // N_DEV, axis=1)

    grads = {
        "norm1_g": small[0:1], "w_in": g_pack_a, "conv_w": gs_conv_w, "conv_b": small[5:6],
        "fox_f_bias": small[7:8, 0:nh], "fox_q_g": small[7:8, HD:2 * HD], "fox_k_g": small[7:8, 2 * HD:3 * HD],
        "mem_norm_g": small[6:7], "w_mem_kv": pack_b.shard(g_pack_b, 0), "xa_q_g": small[7:8, 3 * HD:3 * HD + xd],
        "xa_k_g": small[7:8, 3 * HD + xd:3 * HD + 2 * xd], "w_br_conv": pack_b.shard(g_pack_b, 1),
        "w_br_fox": pack_b.shard(g_pack_b, 2), "w_br_xa": pack_b.shard(g_pack_b, 3), "w_o": pack_b.shard(g_pack_b, 4),
        "norm2_g": small[1:2], "w_ffn_in": pack_b.shard(g_pack_b, 5), "w_ffn_out": pack_b.shard(g_pack_b, 6),
    }
    transposed = {"w_in", "w_mem_kv", "w_ffn_in"}
    weights = {
        "norm1_g": (norm1_g, m_norm1_g, v_norm1_g), "w_in": (w_in, m_w_in, v_w_in), "conv_w": (conv_w, m_conv_w, v_conv_w),
        "conv_b": (conv_b, m_conv_b, v_conv_b), "fox_f_bias": (fox_f_bias, m_fox_f_bias, v_fox_f_bias),
        "fox_q_g": (fox_q_g, m_fox_q_g, v_fox_q_g), "fox_k_g": (fox_k_g, m_fox_k_g, v_fox_k_g),
        "mem_norm_g": (mem_norm_g, m_mem_norm_g, v_mem_norm_g), "w_mem_kv": (w_mem_kv, m_w_mem_kv, v_w_mem_kv),
        "xa_q_g": (xa_q_g, m_xa_q_g, v_xa_q_g), "xa_k_g": (xa_k_g, m_xa_k_g, v_xa_k_g),
        "w_br_conv": (w_br_conv, m_w_br_conv, v_w_br_conv), "w_br_fox": (w_br_fox, m_w_br_fox, v_w_br_fox),
        "w_br_xa": (w_br_xa, m_w_br_xa, v_w_br_xa), "w_o": (w_o, m_w_o, v_w_o), "norm2_g": (norm2_g, m_norm2_g, v_norm2_g),
        "w_ffn_in": (w_ffn_in, m_w_ffn_in, v_w_ffn_in), "w_ffn_out": (w_ffn_out, m_w_ffn_out, v_w_ffn_out),
    }
    out_g, out_d, out_m, out_v = [], [], [], []
    for name, (w, m, v) in weights.items():
        shape = w.shape
        if name == "w_in":
            w3, m3, v3 = (a.transpose(2, 0, 1) for a in (w, m, v))
            dl, mn, vn = _adamw(w3, g_pack_a, m3, v3, name="adamw_" + name)
            out_g.append(g_pack_a[:shape[2]].transpose(1, 2, 0))
            out_d.append(dl.transpose(1, 2, 0))
            out_m.append(mn.transpose(1, 2, 0))
            out_v.append(vn.transpose(1, 2, 0))
            continue
        tr = name in transposed

        def as2d(a):
            a = a.reshape(shape[-2], shape[-1])
            return a.T if tr else a

        def back(a):
            return (a.T if tr else a).reshape(shape)

        w2 = as2d(w)
        g2 = grads[name].reshape(w2.shape)
        dl, mn, vn = _adamw(w2, g2, as2d(m), as2d(v), name="adamw_" + name)
        out_g.append(back(g2))
        out_d.append(back(dl))
        out_m.append(back(mn))
        out_v.append(back(vn))

    loss = lax.psum(loss_local, ("x", "y", "c"))
    return (loss, dx.reshape(nb, s, d), *out_g, *out_d, *out_m, *out_v)
```

```python
import math

import jax
import jax.numpy as jnp
from jax import lax
from jax.experimental import pallas as pl
from jax.experimental.pallas import tpu as pltpu

F32, BF16 = jnp.float32, jnp.bfloat16
EPS = 1e-6
HD = 128
XH = 4
N_DEV = 8
MESH = pl.DeviceIdType.MESH
VMEM_LIMIT = 52 * 1024 * 1024
NEG = -1e30
SQRT_HD = math.sqrt(HD)
EXP2_C = math.log2(math.e) / SQRT_HD
LOG2E = math.log2(math.e)
FOX_HP = 4

ADAM_LR, ADAM_B1, ADAM_B2, ADAM_EPS, ADAM_WD, ADAM_STEP = 0.001, 0.9, 0.999, 1e-08, 0.01, 10


def _cp(**kw):
    return pltpu.CompilerParams(vmem_limit_bytes=VMEM_LIMIT, **kw)


def _pcall(body, **kw):
    return pl.pallas_call(body, **kw)


def _sds(shape, dtype):
    return jax.ShapeDtypeStruct(shape, dtype)


HBM_SPEC = pl.BlockSpec(memory_space=pl.ANY)


def _mesh_pos():
    return lax.axis_index("x"), lax.axis_index("y"), lax.axis_index("c")


class _AllGather:
    def __init__(self, blocks):
        self.blocks = list(blocks)
        n = len(self.blocks)
        self.out_shape = [_sds((N_DEV,) + b.shape, b.dtype) for b in self.blocks]
        self.scratch = [pltpu.SemaphoreType.DMA((n, 7)), pltpu.SemaphoreType.DMA((n, 7)), pltpu.SemaphoreType.DMA((n,))]

    def bind(self, ins, outs, sems):
        n = len(ins)
        send_sems, recv_sems, local_sems = sems
        x, y, c = _mesh_pos()
        me, sibling = (x, y, c), (x, y, 1 - c)
        chips = [(1 - x, y), (x, 1 - y), (1 - x, 1 - y)]

        def slab(t, dev):
            return outs[t].at[4 * dev[0] + 2 * dev[1] + dev[2]]

        def copy(t, k, block, to, src=None):
            dst = slab(t, block)
            return pltpu.make_async_remote_copy(
                src_ref=dst if src is None else src, dst_ref=dst, send_sem=send_sems.at[t, k], recv_sem=recv_sems.at[t, k],
                device_id=to, device_id_type=MESH)

        mine = [pltpu.make_async_copy(ins[t], slab(t, me), local_sems.at[t]) for t in range(n)]
        first = []
        for t in range(n):
            first.append(copy(t, 0, me, sibling, src=ins[t]))
            first += [copy(t, 1 + j, me, (*chip, c), src=ins[t]) for j, chip in enumerate(chips)]
        passed = [copy(t, 4 + j, (*chip, c), sibling) for j, chip in enumerate(chips) for t in range(n)]

        def start():
            for cp in mine + first:
                cp.start()

        def mid():
            for j, chip in enumerate(chips):
                for t in range(n):
                    copy(t, 1 + j, (*chip, c), me).wait_recv()
                    passed[j * n + t].start()

        def finish():
            for t in range(n):
                copy(t, 0, sibling, me).wait_recv()
                for j, chip in enumerate(chips):
                    copy(t, 4 + j, (*chip, 1 - c), me).wait_recv()
            for cp in first + passed:
                cp.wait_send()
            for cp in mine:
                cp.wait()

        return start, mid, finish


class _SiblingSwap:
    def __init__(self, send):
        self.blocks = [send]
        self.out_shape = [_sds(send.shape, send.dtype)]
        self.scratch = [pltpu.SemaphoreType.DMA, pltpu.SemaphoreType.DMA]

    def bind(self, ins, outs, sems):
        x, y, c = _mesh_pos()
        cp = pltpu.make_async_remote_copy(src_ref=ins[0], dst_ref=outs[0], send_sem=sems[0], recv_sem=sems[1],
                                          device_id=(x, y, 1 - c), device_id_type=MESH)
        return cp.start, (lambda: None), cp.wait


class _ChipExchange:
    def __init__(self, part):
        self.blocks = [part]
        self.out_shape = [_sds((3,) + part.shape[1:], part.dtype)]
        self.scratch = [pltpu.SemaphoreType.DMA((3,)), pltpu.SemaphoreType.DMA((3,))]

    def bind(self, ins, outs, sems):
        x, y, c = _mesh_pos()
        chips = [(1 - x, y), (x, 1 - y), (1 - x, 1 - y)]
        cps = [
            pltpu.make_async_remote_copy(src_ref=ins[0].at[2 * cx + cy], dst_ref=outs[0].at[j], send_sem=sems[0].at[j],
                                         recv_sem=sems[1].at[j], device_id=(cx, cy, c), device_id_type=MESH)
            for j, (cx, cy) in enumerate(chips)
        ]

        def start():
            for cp in cps:
                cp.start()

        def finish():
            for cp in cps:
                cp.wait()

        return start, (lambda: None), finish


def _comm_lists(comm):
    if comm is None:
        return [], [], [], []
    n = len(comm.blocks)
    return list(comm.blocks), [HBM_SPEC] * n, list(comm.out_shape), list(comm.scratch)


def _split_refs(refs, n_pre, n_in, n_cin, n_out, n_cout, n_scr):
    cuts = [n_pre, n_in, n_cin, n_out, n_cout, n_scr]
    out, at = [], 0
    for c in cuts:
        out.append(refs[at:at + c])
        at += c
    out.append(refs[at:])
    return out


def _run_comm(comm, cin, cout, csem, step, n_steps, when="start"):
    if comm is None:
        return
    start, mid, finish = comm.bind(cin, cout, csem)
    if when == "start":
        pl.when(step == 0)(start)
    else:
        pl.when(step == max(n_steps - 2, 0))(mid)
        pl.when(step == n_steps - 1)(finish)


def _comm_only(comm, name):
    ins, in_specs, out_shape, scratch = _comm_lists(comm)
    n = len(ins)

    def body(*refs):
        start, mid, finish = comm.bind(refs[:n], refs[n:2 * n], refs[2 * n:])
        start()
        mid()
        finish()

    outs = _pcall(body, name=name, out_shape=tuple(out_shape), in_specs=in_specs, out_specs=tuple([HBM_SPEC] * n),
                  scratch_shapes=scratch, compiler_params=_cp())(*ins)
    return list(outs)


def _all_reduce_small(v):
    rows, cols = v.shape

    def body(v_ref, o_ref, slots, send_sems, recv_sems):
        x, y, c = _mesh_pos()
        me = 4 * x + 2 * y + c
        slots[me] = v_ref[...]
        cps = []
        for k in range(1, N_DEV):
            px, py, pc = x ^ (k >> 2), y ^ ((k >> 1) & 1), c ^ (k & 1)
            cps.append(pltpu.make_async_remote_copy(
                src_ref=v_ref, dst_ref=slots.at[me], send_sem=send_sems.at[k - 1], recv_sem=recv_sems.at[k - 1],
                device_id=(px, py, pc), device_id_type=MESH))
        for cp in cps:
            cp.start()
        for cp in cps:
            cp.wait()
        acc = slots[0]
        for k in range(1, N_DEV):
            acc = acc + slots[k]
        o_ref[...] = acc

    return _pcall(
        body, name="all_reduce_small", out_shape=_sds((rows, cols), F32),
        in_specs=[pl.BlockSpec(memory_space=pltpu.VMEM)], out_specs=pl.BlockSpec(memory_space=pltpu.VMEM),
        scratch_shapes=[pltpu.VMEM((N_DEV, rows, cols), F32), pltpu.SemaphoreType.DMA((7,)), pltpu.SemaphoreType.DMA((7,))],
        compiler_params=_cp(),
    )(v)


def _row_tile(rows, cap=1040):
    return max(tt for tt in range(16, cap + 1, 16) if rows % tt == 0)


def _add2(a, b, *, name):
    nq, rows, cols = a.shape
    tm = _row_tile(rows)

    def body(a_ref, b_ref, o_ref):
        o_ref[...] = (a_ref[...].astype(F32) + b_ref[...].astype(F32)).astype(BF16)

    spec = pl.BlockSpec((1, tm, cols), lambda q, i: (q, i, 0))
    return _pcall(body, name=name, out_shape=_sds(a.shape, BF16), grid=(nq, rows // tm),
                  in_specs=[spec, spec], out_specs=spec, compiler_params=_cp())(a, b)


def _add4(own, recv, *, name, rows_apart=False):
    rows, cols = own.shape
    tm = _row_tile(rows, 256 if rows_apart else 1040)

    def body(o_ref, r_ref, g_ref):
        tot = ((o_ref[...].astype(F32) + r_ref[0].astype(F32)) + r_ref[1].astype(F32)) + r_ref[2].astype(F32)
        if rows_apart:
            g_ref[:, 0, :] = tot
        else:
            g_ref[...] = tot

    if rows_apart:
        out_shape, out_spec = _sds((rows, 1, cols), F32), pl.BlockSpec((tm, 1, cols), lambda i: (i, 0, 0))
    else:
        out_shape, out_spec = _sds((rows, cols), F32), pl.BlockSpec((tm, cols), lambda i: (i, 0))
    return _pcall(
        body, name=name, out_shape=out_shape, grid=(rows // tm,),
        in_specs=[pl.BlockSpec((tm, cols), lambda i: (i, 0)), pl.BlockSpec((3, tm, cols), lambda i: (0, i, 0))],
        out_specs=out_spec, compiler_params=_cp(),
    )(own, recv)


def _mm(a, b, *, m, n, k, name, ta=False, tb=False, b_off=(0, 0), res=None, out_dtype=F32, tm=512, tn=1024, tk=1024,
        comm=None):
    tm, tn, tk = min(tm, m), min(tn, n), min(tk, k)
    assert m % tm == 0 and n % tn == 0 and k % tk == 0, (name, m, n, k, tm, tn, tk)
    nk = k // tk
    grid = (m // tm, n // tn, nk)
    bo0, bo1 = b_off
    if ta:
        a_spec = pl.BlockSpec((tk, tm), lambda i, j, kk: (kk, i))
    else:
        a_spec = pl.BlockSpec((tm, tk), lambda i, j, kk: (i, kk))
    if tb:
        b_spec = pl.BlockSpec((tn, tk), lambda i, j, kk: (j + bo0, kk + bo1))
    else:
        b_spec = pl.BlockSpec((tk, tn), lambda i, j, kk: (kk + bo0, j + bo1))
    o_spec = pl.BlockSpec((tm, tn), lambda i, j, kk: (i, j))
    dn = (((0 if ta else 1,), (1 if tb else 0,)), ((), ()))
    has_res = res is not None
    c_ins, c_in_specs, c_out_shape, c_scratch = _comm_lists(comm)
    n_in = 3 if has_res else 2
    n_scr = 1 if nk > 1 else 0

    def body(*refs):
        _, ins, cin, outs, cout, scr, csem = _split_refs(refs, 0, n_in, len(c_ins), 1, len(c_ins), n_scr)
        a_ref, b_ref = ins[0], ins[1]
        r_ref = ins[2] if has_res else None
        o_ref = outs[0]
        step = (pl.program_id(0) * grid[1] + pl.program_id(1)) * nk + pl.program_id(2)
        _run_comm(comm, cin, cout, csem, step, grid[0] * grid[1] * nk, "start")
        p = lax.dot_general(a_ref[...].astype(BF16), b_ref[...].astype(BF16), dn, preferred_element_type=F32)

        def finish(acc):
            if has_res:
                acc = acc + r_ref[...]
            o_ref[...] = acc.astype(out_dtype)

        if nk == 1:
            finish(p)
        else:
            acc_ref = scr[0]
            kk = pl.program_id(2)

            @pl.when(kk == 0)
            def _():
                acc_ref[...] = p

            @pl.when(kk > 0)
            def _():
                acc_ref[...] += p

            @pl.when(kk == nk - 1)
            def _():
                finish(acc_ref[...])

        _run_comm(comm, cin, cout, csem, step, grid[0] * grid[1] * nk, "end")

    ins = [a, b] + ([res] if has_res else [])
    in_specs = [a_spec, b_spec] + ([o_spec] if has_res else [])
    sem = ("arbitrary",) * 3 if comm is not None else ("parallel", "parallel", "arbitrary")
    outs = _pcall(
        body, name=name, out_shape=tuple([_sds((m, n), out_dtype)] + c_out_shape), grid=grid,
        in_specs=in_specs + c_in_specs, out_specs=tuple([o_spec] + [HBM_SPEC] * len(c_ins)),
        scratch_shapes=([pltpu.VMEM((tm, tn), F32)] if nk > 1 else []) + c_scratch,
        compiler_params=_cp(dimension_semantics=sem),
    )(*(ins + c_ins))
    return outs[0] if comm is None else (outs[0], list(outs[1:]))


SLAB_WIN = 1280
SLAB_TAIL = 48


def _z_from_slabs(h1, slabs, *, t, d, n_pad, rows, comm=None, tm=1024):
    n_slab, rp, _ = slabs.shape
    tm = min(tm, t)
    sh = rows - SLAB_WIN
    tail_lo = rp - SLAB_TAIL
    assert sh >= 0 and rows - (n_slab - 1) * sh >= tail_lo and rp % SLAB_TAIL == 0 and (n_slab - 1) * sh <= 128
    c_ins, c_in_specs, c_out_shape, c_scratch = _comm_lists(comm)
    grid = (n_slab, t // tm)

    def body(*refs):
        _, ins, cin, outs, cout, _, csem = _split_refs(refs, 0, 3, len(c_ins), 1, len(c_ins), 0)
        h_ref, s_ref, t_ref = ins
        o_ref = outs[0]
        k = pl.program_id(0)
        step = k * grid[1] + pl.program_id(1)
        _run_comm(comm, cin, cout, csem, step, grid[0] * grid[1], "start")
        hv = h_ref[...]
        dn = (((1,), (1,)), ((), ()))
        y = lax.dot_general(hv, s_ref[0], dn, preferred_element_type=F32)
        yr = pltpu.roll(y[:, 0:SLAB_WIN], k * sh, axis=1)
        p = lax.dot_general(hv, t_ref[0], dn, preferred_element_type=F32)
        pp = jnp.concatenate([p, jnp.zeros((tm, 128 - SLAB_TAIL), F32)], axis=1)
        pr = pltpu.roll(pp, 128 - (rows - tail_lo) + k * sh, axis=1)
        lane = lax.broadcasted_iota(jnp.int32, (tm, 128), 1)
        o_ref[:, 0:128] = jnp.where(lane < k * sh, pr, yr[:, 0:128])
        o_ref[:, 128:SLAB_WIN] = yr[:, 128:SLAB_WIN]
        _run_comm(comm, cin, cout, csem, step, grid[0] * grid[1], "end")

    outs = _pcall(
        body, name="mm_z", out_shape=tuple([_sds((t, n_pad), F32)] + c_out_shape), grid=grid,
        in_specs=[pl.BlockSpec((tm, d), lambda k, i: (i, 0)), pl.BlockSpec((1, rp, d), lambda k, i: (k, 0, 0)),
                  pl.BlockSpec((1, SLAB_TAIL, d), lambda k, i: (jnp.maximum(k - 1, 0), tail_lo // SLAB_TAIL, 0))] + c_in_specs,
        out_specs=tuple([pl.BlockSpec((tm, SLAB_WIN), lambda k, i: (i, k))] + [HBM_SPEC] * len(c_ins)),
        scratch_shapes=c_scratch, compiler_params=_cp(dimension_semantics=("arbitrary", "arbitrary")),
    )(h1, slabs, slabs, *c_ins)
    return outs[0], list(outs[1:])


def _z_tail(h1, w_rows, z, *, t, d, col_block, tm=1024):
    def body(h_ref, w_ref, z_in, o_ref):
        o_ref[...] = lax.dot_general(h_ref[...], w_ref[...], (((1,), (1,)), ((), ())), preferred_element_type=F32)

    return _pcall(
        body, name="mm_z_tail", out_shape=_sds(z.shape, z.dtype), grid=(t // tm,),
        in_specs=[pl.BlockSpec((tm, d), lambda i: (i, 0)), pl.BlockSpec((HD, d), lambda i: (0, 0)), HBM_SPEC],
        out_specs=pl.BlockSpec((tm, HD), lambda i: (i, col_block)), input_output_aliases={2: 0}, compiler_params=_cp(),
    )(h1, w_rows, z)


def _slab_window(a1_ref, a2_ref, k, sh):
    w = jnp.concatenate([a1_ref[...], a2_ref[...]], axis=1)
    width = SLAB_WIN + 128
    return pltpu.roll(w, width - k * sh, axis=1)


def _g_win_slabs(dz, h1, my_c, *, t, d, n_slab, rows, rp, tk=2048):
    sh = rows - SLAB_WIN
    tk = min(tk, t)
    nk = t // tk
    width = SLAB_WIN + 128
    half = n_slab // 2

    def slab_of(p, c):
        return 2 * (p % half) + jnp.where(p < half, 1 - c, c)

    def body(c_ref, a1_ref, a2_ref, h_ref, keep_ref, land_ref, acc, stage, send_sems, recv_sems):
        p, kk = pl.program_id(0), pl.program_id(1)
        x, y, c = _mesh_pos()
        ws = _slab_window(a1_ref, a2_ref, slab_of(p, c), sh)
        prod = lax.dot_general(ws, h_ref[...], (((0,), (0,)), ((), ())), preferred_element_type=F32)

        def push(q):
            return pltpu.make_async_remote_copy(
                src_ref=stage.at[q % 2], dst_ref=land_ref.at[q], send_sem=send_sems.at[q], recv_sem=recv_sems.at[q],
                device_id=(x, y, 1 - c), device_id_type=MESH)

        @pl.when(kk == 0)
        def _():
            acc[...] = prod

        @pl.when(kk > 0)
        def _():
            acc[...] += prod

        @pl.when(kk == nk - 1)
        def _():
            res = acc[0:rp, :].astype(BF16)
            for q in range(half):
                @pl.when(p == q)
                def _():
                    if q >= 2:
                        push(q - 2).wait_send()
                    stage[q % 2] = res
                    push(q).start()

            @pl.when(p >= half)
            def _():
                keep_ref[0] = res

        @pl.when((p == n_slab - 1) & (kk == nk - 1))
        def _():
            for q in range(max(half - 2, 0), half):
                push(q).wait_send()
            for q in range(half):
                push(q).wait_recv()

    outs = _pcall(
        body, name="mm_g_win", out_shape=(_sds((half, rp, d), BF16), _sds((half, rp, d), BF16)),
        grid_spec=pltpu.PrefetchScalarGridSpec(
            num_scalar_prefetch=1, grid=(n_slab, nk),
            in_specs=[pl.BlockSpec((tk, SLAB_WIN), lambda p, kk, c: (kk, slab_of(p, c[0]))),
                      pl.BlockSpec((tk, 128), lambda p, kk, c: (kk, (SLAB_WIN // 128) * (slab_of(p, c[0]) + 1))),
                      pl.BlockSpec((tk, d), lambda p, kk, c: (kk, 0))],
            out_specs=(pl.BlockSpec((1, rp, d), lambda p, kk, c: (jnp.maximum(p - half, 0), 0, 0)), HBM_SPEC),
            scratch_shapes=[pltpu.VMEM((width, d), F32), pltpu.VMEM((2, rp, d), BF16),
                            pltpu.SemaphoreType.DMA((half,)), pltpu.SemaphoreType.DMA((half,))]),
        compiler_params=_cp(dimension_semantics=("arbitrary", "arbitrary")),
    )(my_c.reshape(1).astype(jnp.int32), dz, dz, h1)
    return outs[0], outs[1]


def _dh1_from_slabs(dz, slabs, *, t, d, rows, comm=None, tm=1024):
    n_slab, rp, _ = slabs.shape
    sh = rows - SLAB_WIN
    width = SLAB_WIN + 128
    c_ins, c_in_specs, c_out_shape, c_scratch = _comm_lists(comm)
    grid = (t // tm, n_slab)

    def body(*refs):
        _, ins, cin, outs, cout, scr, csem = _split_refs(refs, 0, 3, len(c_ins), 1, len(c_ins), 1)
        a1_ref, a2_ref, s_ref = ins
        o_ref, acc = outs[0], scr[0]
        k = pl.program_id(1)
        step = pl.program_id(0) * n_slab + k
        _run_comm(comm, cin, cout, csem, step, grid[0] * n_slab, "start")
        ws = _slab_window(a1_ref, a2_ref, k, sh)
        s_end = jnp.concatenate([s_ref[0, SLAB_WIN:rp, :], jnp.zeros((width - rp, d), BF16)], axis=0)
        p = (jnp.dot(ws[:, 0:SLAB_WIN], s_ref[0, 0:SLAB_WIN, :], preferred_element_type=F32)
             + jnp.dot(ws[:, SLAB_WIN:width], s_end, preferred_element_type=F32))

        @pl.when(k == 0)
        def _():
            acc[...] = p

        @pl.when(k > 0)
        def _():
            acc[...] += p

        @pl.when(k == n_slab - 1)
        def _():
            o_ref[...] = acc[...]

        _run_comm(comm, cin, cout, csem, step, grid[0] * n_slab, "end")

    outs = _pcall(
        body, name="mm_dh1", out_shape=tuple([_sds((t, d), F32)] + c_out_shape), grid=grid,
        in_specs=[pl.BlockSpec((tm, SLAB_WIN), lambda i, k: (i, k)),
                  pl.BlockSpec((tm, 128), lambda i, k: (i, (SLAB_WIN // 128) * (k + 1))),
                  pl.BlockSpec((1, rp, d), lambda i, k: (k, 0, 0))] + c_in_specs,
        out_specs=tuple([pl.BlockSpec((tm, d), lambda i, k: (i, 0))] + [HBM_SPEC] * len(c_ins)),
        scratch_shapes=[pltpu.VMEM((tm, d), F32)] + c_scratch,
        compiler_params=_cp(dimension_semantics=("arbitrary", "arbitrary")),
    )(dz, dz, slabs, *c_ins)
    return outs[0], list(outs[1:])


def _rms_fwd(x, g, *, name, tm=512):
    t, d = x.shape
    tm = min(tm, t)

    def body(x_ref, g_ref, h_ref):
        xv = x_ref[...]
        r = lax.rsqrt(jnp.mean(xv * xv, axis=-1, keepdims=True) + EPS)
        h_ref[...] = (xv * r * g_ref[...]).astype(BF16)

    return _pcall(
        body, name=name, out_shape=_sds((t, d), BF16), grid=(t // tm,),
        in_specs=[pl.BlockSpec((tm, d), lambda i: (i, 0)), pl.BlockSpec((1, d), lambda i: (0, 0))],
        out_specs=pl.BlockSpec((tm, d), lambda i: (i, 0)), compiler_params=_cp(),
    )(x, g)


def _rms_bwd(dh, x, g, res, *, name, tm=512):
    t, d = x.shape
    tm = min(tm, t)
    has_res = res is not None

    def body(*refs):
        dh_ref, x_ref, g_ref = refs[:3]
        r_ref = refs[3] if has_res else None
        dx_ref, dg_ref = refs[-2], refs[-1]
        xv = x_ref[...]
        r = lax.rsqrt(jnp.mean(xv * xv, axis=-1, keepdims=True) + EPS)
        xh = xv * r
        dhv = dh_ref[...]
        gd = dhv * g_ref[...]
        dx = r * (gd - xh * jnp.mean(gd * xh, axis=-1, keepdims=True))
        if has_res:
            dx = dx + r_ref[...]
        dx_ref[...] = dx

        @pl.when(pl.program_id(0) == 0)
        def _():
            dg_ref[...] = jnp.zeros_like(dg_ref)

        dg_ref[...] += jnp.sum(dhv * xh, axis=0, keepdims=True)

    row = pl.BlockSpec((tm, d), lambda i: (i, 0))
    vec = pl.BlockSpec((1, d), lambda i: (0, 0))
    return _pcall(
        body, name=name, out_shape=(_sds((t, d), F32), _sds((1, d), F32)), grid=(t // tm,),
        in_specs=[row, row, vec] + ([row] if has_res else []), out_specs=(row, vec),
        compiler_params=_cp(dimension_semantics=("arbitrary",)),
    )(*([dh, x, g] + ([res] if has_res else [])))


def _conv_fwd(z, conv_w, conv_b, *, nb, s, d, tc=256):
    nc = d // tc

    def body(cb_ref, cc_ref, cv_ref, w_ref, b_ref, y_ref):
        u = cc_ref[...] * cv_ref[...]
        row = lax.broadcasted_iota(jnp.int32, u.shape, 0)
        u1 = jnp.where(row >= 1, pltpu.roll(u, 1, axis=0), 0.0)
        u2 = jnp.where(row >= 2, pltpu.roll(u, 2, axis=0), 0.0)
        w = w_ref[...]
        y = w[2:3, :] * u + w[1:2, :] * u1 + w[0:1, :] * u2 + b_ref[...]
        y_ref[...] = (cb_ref[...] * y).astype(BF16)

    def part(p):
        return pl.BlockSpec((s, tc), lambda b, j: (b, p * nc + j))

    return _pcall(
        body, name="conv_fwd", out_shape=_sds((nb * s, d), BF16), grid=(nb, nc),
        in_specs=[part(0), part(1), part(2), pl.BlockSpec((3, tc), lambda b, j: (0, j)), pl.BlockSpec((1, tc), lambda b, j: (0, j))],
        out_specs=pl.BlockSpec((s, tc), lambda b, j: (b, j)), compiler_params=_cp(),
    )(z, z, z, conv_w, conv_b)


def _conv_bwd(z, dy, conv_w, conv_b, dz, *, nb, s, d, tc=256):
    nc = d // tc

    def body(cb_ref, cc_ref, cv_ref, dy_ref, w_ref, b_ref, dz_in, dz_ref, dw_ref, db_ref, stash):
        b_i, p = pl.program_id(1), pl.program_id(2)

        @pl.when(p == 0)
        def _():
            cc, cv = cc_ref[...], cv_ref[...]
            u = cc * cv
            n = u.shape[0]
            row = lax.broadcasted_iota(jnp.int32, u.shape, 0)
            u1 = jnp.where(row >= 1, pltpu.roll(u, 1, axis=0), 0.0)
            u2 = jnp.where(row >= 2, pltpu.roll(u, 2, axis=0), 0.0)
            w = w_ref[...]
            y = w[2:3, :] * u + w[1:2, :] * u1 + w[0:1, :] * u2 + b_ref[...]
            dyv = dy_ref[...]
            dconv = dyv * cb_ref[...]
            g1 = jnp.where(row < n - 1, pltpu.roll(dconv, n - 1, axis=0), 0.0)
            g2 = jnp.where(row < n - 2, pltpu.roll(dconv, n - 2, axis=0), 0.0)
            du = w[2:3, :] * dconv + w[1:2, :] * g1 + w[0:1, :] * g2
            stash[0] = (dyv * y).astype(BF16)
            stash[1] = (du * cv).astype(BF16)
            stash[2] = (du * cc).astype(BF16)
            dws = [jnp.sum(dconv * u2, axis=0, keepdims=True), jnp.sum(dconv * u1, axis=0, keepdims=True),
                   jnp.sum(dconv * u, axis=0, keepdims=True)]
            db = jnp.sum(dconv, axis=0, keepdims=True)

            @pl.when(b_i == 0)
            def _():
                for r in range(3):
                    dw_ref[r:r + 1, :] = dws[r]
                db_ref[...] = db

            @pl.when(b_i > 0)
            def _():
                for r in range(3):
                    dw_ref[r:r + 1, :] += dws[r]
                db_ref[...] += db

        dz_ref[...] = stash[p]

    return _pcall(
        body, name="conv_bwd",
        out_shape=(_sds(dz.shape, dz.dtype), _sds((3, d), F32), _sds((1, d), F32)), grid=(nc, nb, 3),
        in_specs=[
            pl.BlockSpec((s, tc), lambda j, b, p: (b, j)), pl.BlockSpec((s, tc), lambda j, b, p: (b, nc + j)),
            pl.BlockSpec((s, tc), lambda j, b, p: (b, 2 * nc + j)), pl.BlockSpec((s, tc), lambda j, b, p: (b, j)),
            pl.BlockSpec((3, tc), lambda j, b, p: (0, j)), pl.BlockSpec((1, tc), lambda j, b, p: (0, j)),
            HBM_SPEC,
        ],
        out_specs=(pl.BlockSpec((s, tc), lambda j, b, p: (b, p * nc + j)),
                   pl.BlockSpec((3, tc), lambda j, b, p: (0, j)), pl.BlockSpec((1, tc), lambda j, b, p: (0, j))),
        scratch_shapes=[pltpu.VMEM((3, s, tc), BF16)],
        input_output_aliases={6: 0},
        compiler_params=_cp(dimension_semantics=("arbitrary", "arbitrary", "arbitrary")),
    )(z, z, z, dy, conv_w, conv_b, dz)


def _head_rms(xv, g):
    r = lax.rsqrt(jnp.mean(xv * xv, axis=-1, keepdims=True) + EPS)
    return xv * r * g


def _head_rms_bwd(xv, g, dy):
    r = lax.rsqrt(jnp.mean(xv * xv, axis=-1, keepdims=True) + EPS)
    xh = xv * r
    gd = dy * g
    dx = r * (gd - xh * jnp.mean(gd * xh, axis=-1, keepdims=True))
    return dx, jnp.sum(dy * xh, axis=0, keepdims=True)


def _fox_prep(z, q_g, k_g, *, t, d, tm=256):
    nh = d // HD

    def body(z_ref, qg_ref, kg_ref, q_ref, k_ref, v_ref):
        qg, kg = qg_ref[...], kg_ref[...]
        for h in range(nh):
            c0 = h * HD
            q_ref[:, c0:c0 + HD] = _head_rms(z_ref[:, c0:c0 + HD], qg).astype(BF16)
            k_ref[:, c0:c0 + HD] = _head_rms(z_ref[:, d + c0:d + c0 + HD], kg).astype(BF16)
        v_ref[...] = z_ref[:, 2 * d:3 * d].astype(BF16)

    o = pl.BlockSpec((tm, d), lambda i: (i, 0))
    g = pl.BlockSpec((1, HD), lambda i: (0, 0))
    return _pcall(
        body, name="fox_prep", out_shape=(_sds((t, d), BF16),) * 3, grid=(t // tm,),
        in_specs=[pl.BlockSpec((tm, 3 * d), lambda i: (i, 1)), g, g], out_specs=(o, o, o), compiler_params=_cp(),
    )(z, q_g, k_g)


def _lane_cumsum(v, reverse=False):
    n = v.shape[1]
    lane = lax.broadcasted_iota(jnp.int32, v.shape, 1)
    sh = 1
    while sh < n:
        if reverse:
            v = v + jnp.where(lane < n - sh, pltpu.roll(v, n - sh, axis=1), 0.0)
        else:
            v = v + jnp.where(lane >= sh, pltpu.roll(v, sh, axis=1), 0.0)
        sh *= 2
    return v


def _fox_gates(z, f_bias_col, *, nb, s, nh, ff_block):
    def body(z_ref, b_ref, c_ref):
        ff = z_ref[...].T[0:nh, :] + b_ref[...]
        lf = jnp.minimum(ff, 0.0) - jnp.log(1.0 + jnp.exp(-jnp.abs(ff)))
        c_ref[0] = _lane_cumsum(lf) * SQRT_HD

    return _pcall(
        body, name="fox_gates", out_shape=_sds((nb, nh, s), F32), grid=(nb,),
        in_specs=[pl.BlockSpec((s, HD), lambda b: (b, ff_block)), pl.BlockSpec((nh, 1), lambda b: (0, 0))],
        out_specs=pl.BlockSpec((1, nh, s), lambda b: (b, 0, 0)), compiler_params=_cp(),
    )(z, f_bias_col)


def _fox_gates_bwd(z, f_bias_col, row_sums, col_sums, dz, *, nb, s, nh, ff_block):
    def body(z_ref, b_ref, rs_ref, cs_ref, dz_in, dz_ref, db_ref, dc_sc):
        b_i, h = pl.program_id(0), pl.program_id(1)
        dc_sc[pl.ds(h, 1), :] = (rs_ref[...] - cs_ref[...]).T[0:1, :]

        @pl.when(h == nh - 1)
        def _():
            ff = z_ref[...].T[0:nh, :] + b_ref[...]
            dlf = _lane_cumsum(dc_sc[...], reverse=True)
            dff = dlf * (1.0 / (1.0 + jnp.exp(ff)))
            pad = jnp.concatenate([dff, jnp.zeros((HD - nh, s), F32)], axis=0)
            dz_ref[...] = pad.T.astype(BF16)
            dbv = jnp.sum(dff, axis=1, keepdims=True)

            @pl.when(b_i == 0)
            def _():
                db_ref[...] = dbv

            @pl.when(b_i > 0)
            def _():
                db_ref[...] += dbv

    hs = pl.BlockSpec((s, HD), lambda b, h: (b, h))
    zs = pl.BlockSpec((s, HD), lambda b, h: (b, ff_block))
    return _pcall(
        body, name="fox_gates_bwd", out_shape=(_sds(dz.shape, dz.dtype), _sds((nh, 1), F32)), grid=(nb, nh),
        in_specs=[zs, pl.BlockSpec((nh, 1), lambda b, h: (0, 0)), hs, hs, HBM_SPEC],
        out_specs=(zs, pl.BlockSpec((nh, 1), lambda b, h: (0, 0))),
        scratch_shapes=[pltpu.VMEM((nh, s), F32)],
        input_output_aliases={4: 0}, compiler_params=_cp(dimension_semantics=("arbitrary", "arbitrary")),
    )(z, f_bias_col, row_sums, col_sums, dz)


def _fox_pairs(nq, key_major):
    if key_major:
        pairs = [(i, j) for j in range(nq) for i in range(j, nq)]
    else:
        pairs = [(i, j) for i in range(nq) for j in range(i + 1)]
    return jnp.array([p[0] for p in pairs], jnp.int32), jnp.array([p[1] for p in pairs], jnp.int32)


def _with_ones(x):
    return jnp.concatenate([x, jnp.ones_like(x)], axis=1)


def _fox_specs(nb, nh, nq, tq, tk, hp):
    qs = pl.BlockSpec((tq, hp * HD), lambda b, h, p, it, jt: (b * nq + it[p], h))
    ks = pl.BlockSpec((tk, hp * HD), lambda b, h, p, it, jt: (b * nq + jt[p], h))
    cs = pl.BlockSpec((hp, 1, tk), lambda b, h, p, it, jt: (b * (nh // hp) + h, 0, jt[p]))
    return qs, ks, cs


def _fox_raw(qv, kv, ckv, diag, tq, tk):
    sc = lax.dot_general(qv, kv, (((1,), (1,)), ((), ())), preferred_element_type=F32) - ckv
    if diag:
        rows = lax.broadcasted_iota(jnp.int32, (tq, tk), 0)
        cols = lax.broadcasted_iota(jnp.int32, (tq, tk), 1)
        sc = jnp.where(rows >= cols, sc, NEG)
    return sc


def _fox_probs(qv, kv, ckv, lse_col, diag, tq, tk):
    sc = _fox_raw(qv, kv, ckv, diag, tq, tk)
    return jnp.exp2(sc * EXP2_C - lse_col * LOG2E)


def _fox_call(body, *, name, ins, in_specs, out_shape, out_specs, scratch, grid, tables, comm):
    c_ins, c_in_specs, c_out_shape, c_scratch = _comm_lists(comm)
    n_in, n_out, n_scr, nc = len(ins), len(out_shape), len(scratch), len(c_ins)
    n_steps = grid[0] * grid[1] * grid[2]

    def wrapped(*refs):
        pre, rin, cin, rout, cout, scr, csem = _split_refs(refs, 2, n_in, nc, n_out, nc, n_scr)
        step = (pl.program_id(0) * grid[1] + pl.program_id(1)) * grid[2] + pl.program_id(2)
        _run_comm(comm, cin, cout, csem, step, n_steps, "start")
        body(*pre, *rin, *rout, *scr)
        _run_comm(comm, cin, cout, csem, step, n_steps, "end")

    sem = ("arbitrary",) * 3 if comm is not None else ("parallel", "parallel", "arbitrary")
    outs = _pcall(
        wrapped, name=name, out_shape=tuple(list(out_shape) + c_out_shape),
        grid_spec=pltpu.PrefetchScalarGridSpec(
            num_scalar_prefetch=2, grid=grid, in_specs=list(in_specs) + c_in_specs,
            out_specs=tuple(list(out_specs) + [HBM_SPEC] * nc), scratch_shapes=list(scratch) + c_scratch),
        compiler_params=_cp(dimension_semantics=sem),
    )(*tables, *ins, *c_ins)
    return list(outs[:n_out]), list(outs[n_out:])


def _fox_fwd(q, k, v, cks, *, nb, s, nh, tq=512, comm=None):
    tk = tq
    nq = s // tq
    t = nb * s
    hp = FOX_HP
    it, jt = _fox_pairs(nq, key_major=False)

    def body(it_ref, jt_ref, q_ref, k_ref, v_ref, c_ref, o_ref, lse_ref, m_sc, acc_sc):
        p_id = pl.program_id(2)
        i, j = it_ref[p_id], jt_ref[p_id]

        @pl.when(j == 0)
        def _():
            m_sc[...] = jnp.full_like(m_sc, NEG)
            acc_sc[...] = jnp.zeros_like(acc_sc)

        def step(diag):
            for hh in range(hp):
                cols = slice(hh * HD, (hh + 1) * HD)
                sc = _fox_raw(q_ref[:, cols], k_ref[:, cols], c_ref[hh], diag, tq, tk)
                m_old = m_sc[hh]
                m_new = jnp.maximum(m_old, jnp.max(sc, axis=-1, keepdims=True))
                alpha = jnp.exp2((m_old - m_new) * EXP2_C)
                p = jnp.exp2((sc - m_new) * EXP2_C).astype(BF16)
                acc = alpha * acc_sc[hh] + jnp.dot(p, _with_ones(v_ref[:, cols]), preferred_element_type=F32)
                if diag:
                    l = acc[:, HD:2 * HD]
                    o_ref[:, cols] = (acc[:, 0:HD] / l).astype(BF16)
                    lse_ref[:, cols] = m_new * (1.0 / SQRT_HD) + jnp.log(l)
                else:
                    acc_sc[hh] = acc
                    m_sc[hh] = m_new

        @pl.when(j < i)
        def _():
            step(False)

        @pl.when(j == i)
        def _():
            step(True)

    qs, ks, cs = _fox_specs(nb, nh, nq, tq, tk, hp)
    (o, lse), c_out = _fox_call(
        body, name="fox_fwd", ins=[q, k, v, cks], in_specs=[qs, ks, ks, cs],
        out_shape=[_sds((t, nh * HD), BF16), _sds((t, nh * HD), F32)], out_specs=[qs, qs],
        scratch=[pltpu.VMEM((hp, tq, 1), F32), pltpu.VMEM((hp, tq, 2 * HD), F32)],
        grid=(nb, nh // hp, int(it.shape[0])), tables=(it, jt), comm=comm)
    return o, lse, c_out


def _fox_bwd_dq(q, k, v, cks, do, o, lse, *, nb, s, nh, tq=512, comm=None):
    tk = tq
    nq = s // tq
    t = nb * s
    hp = FOX_HP
    it, jt = _fox_pairs(nq, key_major=False)

    def body(it_ref, jt_ref, q_ref, k_ref, v_ref, c_ref, do_ref, o_ref, lse_ref, dq_ref, rs_ref, acc_sc, dl_sc):
        p_id = pl.program_id(2)
        i, j = it_ref[p_id], jt_ref[p_id]

        @pl.when(j == 0)
        def _():
            acc_sc[...] = jnp.zeros_like(acc_sc)
            for hh in range(hp):
                cols = slice(hh * HD, (hh + 1) * HD)
                dl_sc[hh] = jnp.sum(do_ref[:, cols] * o_ref[:, cols].astype(F32), axis=-1, keepdims=True)

        def step(diag):
            for hh in range(hp):
                cols = slice(hh * HD, (hh + 1) * HD)
                kv = k_ref[:, cols]
                p = _fox_probs(q_ref[:, cols], kv, c_ref[hh], lse_ref[:, hh * HD:hh * HD + 1], diag, tq, tk)
                dp = lax.dot_general(do_ref[:, cols].astype(BF16), v_ref[:, cols], (((1,), (1,)), ((), ())),
                                     preferred_element_type=F32)
                ds = (p * (dp - dl_sc[hh])).astype(BF16)
                acc = acc_sc[hh] + jnp.dot(ds, _with_ones(kv), preferred_element_type=F32)
                if diag:
                    dq_ref[:, cols] = acc[:, 0:HD] * (1.0 / SQRT_HD)
                    rs_ref[:, cols] = acc[:, HD:2 * HD]
                else:
                    acc_sc[hh] = acc

        @pl.when(j < i)
        def _():
            step(False)

        @pl.when(j == i)
        def _():
            step(True)

    qs, ks, cs = _fox_specs(nb, nh, nq, tq, tk, hp)
    (dq, rs), c_out = _fox_call(
        body, name="fox_bwd_dq", ins=[q, k, v, cks, do, o, lse], in_specs=[qs, ks, ks, cs, qs, qs, qs],
        out_shape=[_sds((t, nh * HD), F32), _sds((t, nh * HD), F32)], out_specs=[qs, qs],
        scratch=[pltpu.VMEM((hp, tq, 2 * HD), F32), pltpu.VMEM((hp, tq, 1), F32)],
        grid=(nb, nh // hp, int(it.shape[0])), tables=(it, jt), comm=comm)
    return dq, rs, c_out


def _fox_bwd_dkv(q, k, v, cks, do, o, lse, *, nb, s, nh, tq=512, comm=None):
    tk = tq
    nq = s // tq
    t = nb * s
    hp = FOX_HP
    it, jt = _fox_pairs(nq, key_major=True)

    def body(it_ref, jt_ref, q_ref, k_ref, v_ref, c_ref, do_ref, o_ref, lse_ref, dk_ref, dv_ref, cs_ref, dk_sc, dv_sc):
        p_id = pl.program_id(2)
        i, j = it_ref[p_id], jt_ref[p_id]

        def step(diag):
            for hh in range(hp):
                cols = slice(hh * HD, (hh + 1) * HD)
                qv = q_ref[:, cols]
                p = _fox_probs(qv, k_ref[:, cols], c_ref[hh], lse_ref[:, hh * HD:hh * HD + 1], diag, tq, tk)
                dov = do_ref[:, cols]
                dob = dov.astype(BF16)
                dvp = lax.dot_general(p.astype(BF16), dob, (((0,), (0,)), ((), ())), preferred_element_type=F32)
                dp = lax.dot_general(dob, v_ref[:, cols], (((1,), (1,)), ((), ())), preferred_element_type=F32)
                delta = jnp.sum(dov * o_ref[:, cols].astype(F32), axis=-1, keepdims=True)
                ds = (p * (dp - delta)).astype(BF16)
                dkp = lax.dot_general(ds, _with_ones(qv), (((0,), (0,)), ((), ())), preferred_element_type=F32)
                if diag:
                    dk_sc[hh] = dkp
                    dv_sc[hh] = dvp
                else:
                    dk_sc[hh] += dkp
                    dv_sc[hh] += dvp

        @pl.when(i == j)
        def _():
            step(True)

        @pl.when(i > j)
        def _():
            step(False)

        @pl.when(i == nq - 1)
        def _():
            for hh in range(hp):
                cols = slice(hh * HD, (hh + 1) * HD)
                dk_ref[:, cols] = dk_sc[hh, :, 0:HD] * (1.0 / SQRT_HD)
                cs_ref[:, cols] = dk_sc[hh, :, HD:2 * HD]
                dv_ref[:, cols] = dv_sc[hh].astype(BF16)

    qs, ks, cs = _fox_specs(nb, nh, nq, tq, tk, hp)
    (dk, dv, csum), c_out = _fox_call(
        body, name="fox_bwd_dkv", ins=[q, k, v, cks, do, o, lse], in_specs=[qs, ks, ks, cs, qs, qs, qs],
        out_shape=[_sds((t, nh * HD), F32), _sds((t, nh * HD), BF16), _sds((t, nh * HD), F32)], out_specs=[ks, ks, ks],
        scratch=[pltpu.VMEM((hp, tk, 2 * HD), F32), pltpu.VMEM((hp, tk, HD), F32)],
        grid=(nb, nh // hp, int(it.shape[0])), tables=(it, jt), comm=comm)
    return dk, dv, csum, c_out


def _fox_post(z, dqn, dkn, dv, q_g, k_g, dz, *, t, d, tm=256):
    nh = d // HD

    def body(z_ref, dq_ref, dk_ref, dv_ref, qg_ref, kg_ref, dz_in, dz_ref, dqg_ref, dkg_ref):
        qg, kg = qg_ref[...], kg_ref[...]
        aq = jnp.zeros((1, HD), F32)
        ak = jnp.zeros((1, HD), F32)
        for h in range(nh):
            c0 = h * HD
            dx, sg = _head_rms_bwd(z_ref[:, c0:c0 + HD], qg, dq_ref[:, c0:c0 + HD])
            dz_ref[:, c0:c0 + HD] = dx.astype(BF16)
            aq = aq + sg
            dx, sg = _head_rms_bwd(z_ref[:, d + c0:d + c0 + HD], kg, dk_ref[:, c0:c0 + HD])
            dz_ref[:, d + c0:d + c0 + HD] = dx.astype(BF16)
            ak = ak + sg
        dz_ref[:, 2 * d:3 * d] = dv_ref[...]

        @pl.when(pl.program_id(0) == 0)
        def _():
            dqg_ref[...] = aq
            dkg_ref[...] = ak

        @pl.when(pl.program_id(0) > 0)
        def _():
            dqg_ref[...] += aq
            dkg_ref[...] += ak

    o = pl.BlockSpec((tm, d), lambda i: (i, 0))
    g = pl.BlockSpec((1, HD), lambda i: (0, 0))
    z3 = pl.BlockSpec((tm, 3 * d), lambda i: (i, 1))
    return _pcall(
        body, name="fox_post", out_shape=(_sds(dz.shape, dz.dtype), _sds((1, HD), F32), _sds((1, HD), F32)), grid=(t // tm,),
        in_specs=[z3, o, o, o, g, g, HBM_SPEC], out_specs=(z3, g, g),
        input_output_aliases={6: 0}, compiler_params=_cp(dimension_semantics=("arbitrary",)),
    )(z, dqn, dkn, dv, q_g, k_g, dz)


def _xa_prep_kv(kv, k_g, *, rows, d):
    xd = d // XH

    def body(kv_ref, g_ref, k_ref, v_ref):
        g = g_ref[...]
        for h in range(XH):
            c0 = h * xd
            k_ref[:, c0:c0 + xd] = _head_rms(kv_ref[:, c0:c0 + xd], g).astype(BF16)
        v_ref[...] = kv_ref[:, d:2 * d].astype(BF16)

    return _pcall(body, name="xa_prep_kv", out_shape=(_sds((rows, d), BF16),) * 2, compiler_params=_cp())(kv, k_g)


def _xa_fwd(z, kn, vb, q_g, *, nb, s, d, nm, q_block, tq=512):
    xd = d // XH
    nq = s // tq

    def body(z_ref, k_ref, v_ref, g_ref, y_ref):
        g = g_ref[...]
        for h in range(XH):
            c0 = h * xd
            qn = _head_rms(z_ref[:, c0:c0 + xd], g).astype(BF16)
            sc = lax.dot_general(qn, k_ref[:, c0:c0 + xd], (((1,), (1,)), ((), ())), preferred_element_type=F32)
            sc = sc / math.sqrt(xd)
            p = jnp.exp(sc - jnp.max(sc, axis=-1, keepdims=True))
            p = p / jnp.sum(p, axis=-1, keepdims=True)
            y_ref[:, c0:c0 + xd] = jnp.dot(p.astype(BF16), v_ref[:, c0:c0 + xd], preferred_element_type=F32).astype(BF16)

    kvs = pl.BlockSpec((nm, d), lambda b, i: (b, 0))
    return _pcall(
        body, name="xa_fwd", out_shape=_sds((nb * s, d), BF16), grid=(nb, nq),
        in_specs=[pl.BlockSpec((tq, d), lambda b, i: (b * nq + i, q_block)), kvs, kvs, pl.BlockSpec((1, xd), lambda b, i: (0, 0))],
        out_specs=pl.BlockSpec((tq, d), lambda b, i: (b * nq + i, 0)), compiler_params=_cp(),
    )(z, kn, vb, q_g)


def _xa_bwd(z, kn, vb, q_g, dy, dz, *, nb, s, d, nm, q_block, tq=512):
    xd = d // XH
    nq = s // tq

    def body(z_ref, k_ref, v_ref, g_ref, dy_ref, dz_in, dz_ref, dk_ref, dv_ref, dg_ref):
        b_i, i = pl.program_id(0), pl.program_id(1)
        g = g_ref[...]

        @pl.when(i == 0)
        def _():
            dk_ref[...] = jnp.zeros_like(dk_ref)
            dv_ref[...] = jnp.zeros_like(dv_ref)

        @pl.when((i == 0) & (b_i == 0))
        def _():
            dg_ref[...] = jnp.zeros_like(dg_ref)

        for h in range(XH):
            c0 = h * xd
            xq = z_ref[:, c0:c0 + xd]
            qn = _head_rms(xq, g).astype(BF16)
            kh, vh = k_ref[:, c0:c0 + xd], v_ref[:, c0:c0 + xd]
            sc = lax.dot_general(qn, kh, (((1,), (1,)), ((), ())), preferred_element_type=F32) / math.sqrt(xd)
            p = jnp.exp(sc - jnp.max(sc, axis=-1, keepdims=True))
            p = p / jnp.sum(p, axis=-1, keepdims=True)
            dob = dy_ref[:, c0:c0 + xd].astype(BF16)
            dv_ref[:, c0:c0 + xd] += lax.dot_general(p.astype(BF16), dob, (((0,), (0,)), ((), ())), preferred_element_type=F32)
            dp = lax.dot_general(dob, vh, (((1,), (1,)), ((), ())), preferred_element_type=F32)
            ds = p * (dp - jnp.sum(p * dp, axis=-1, keepdims=True)) / math.sqrt(xd)
            dsb = ds.astype(BF16)
            dqn = jnp.dot(dsb, kh, preferred_element_type=F32)
            dk_ref[:, c0:c0 + xd] += lax.dot_general(dsb, qn, (((0,), (0,)), ((), ())), preferred_element_type=F32)
            dx, sg = _head_rms_bwd(xq, g, dqn)
            dz_ref[:, c0:c0 + xd] = dx.astype(BF16)
            dg_ref[...] += sg

    kvs = pl.BlockSpec((nm, d), lambda b, i: (b, 0))
    qs = pl.BlockSpec((tq, d), lambda b, i: (b * nq + i, q_block))
    gs = pl.BlockSpec((1, xd), lambda b, i: (0, 0))
    return _pcall(
        body, name="xa_bwd",
        out_shape=(_sds(dz.shape, dz.dtype), _sds((nb * nm, d), F32), _sds((nb * nm, d), F32), _sds((1, xd), F32)), grid=(nb, nq),
        in_specs=[qs, kvs, kvs, gs, pl.BlockSpec((tq, d), lambda b, i: (b * nq + i, 0)), HBM_SPEC],
        out_specs=(qs, kvs, kvs, gs), input_output_aliases={5: 0},
        compiler_params=_cp(dimension_semantics=("arbitrary", "arbitrary")),
    )(z, kn, vb, q_g, dy, dz)


def _xa_post_kv(kv, dkn, dv, k_g, *, rows, d):
    xd = d // XH

    def body(kv_ref, dk_ref, dv_ref, g_ref, dkv_ref, dg_ref):
        g = g_ref[...]
        acc = jnp.zeros((1, xd), F32)
        for h in range(XH):
            c0 = h * xd
            dx, sg = _head_rms_bwd(kv_ref[:, c0:c0 + xd], g, dk_ref[:, c0:c0 + xd])
            dkv_ref[:, c0:c0 + xd] = dx.astype(BF16)
            acc = acc + sg
        dkv_ref[:, d:2 * d] = dv_ref[...].astype(BF16)
        dg_ref[...] = acc

    return _pcall(body, name="xa_post_kv", out_shape=(_sds((rows, 2 * d), BF16), _sds((1, xd), F32)), compiler_params=_cp())(
        kv, dkn, dv, k_g)


def _sigmoid(v):
    return 1.0 / (1.0 + jnp.exp(-v))


def _branches_fwd(z, ys, ws, *, t, d, gate_block, tm=512, tn=512):
    nj = d // tn

    def body(ya_ref, yb_ref, yc_ref, wa_ref, wb_ref, wc_ref, ga_ref, gb_ref, gc_ref, m_ref, p_ref):
        acc = None
        for q, (y_ref, w_ref, g_ref) in enumerate(((ya_ref, wa_ref, ga_ref), (yb_ref, wb_ref, gb_ref), (yc_ref, wc_ref, gc_ref))):
            pr = jnp.dot(y_ref[...], w_ref[...], preferred_element_type=F32)
            p_ref[q] = pr.astype(BF16)
            term = _sigmoid(g_ref[...]) * pr
            acc = term if acc is None else acc + term
        m_ref[...] = acc.astype(BF16)

    y_spec = pl.BlockSpec((tm, d), lambda i, j: (i, 0))
    w_spec = pl.BlockSpec((d, tn), lambda i, j: (0, j))

    def gate(q):
        return pl.BlockSpec((tm, tn), lambda i, j: (i, (gate_block + q) * nj + j))

    return _pcall(
        body, name="branches_fwd", out_shape=(_sds((t, d), BF16), _sds((3, t, d), BF16)), grid=(t // tm, nj),
        in_specs=[y_spec] * 3 + [w_spec] * 3 + [gate(0), gate(1), gate(2)],
        out_specs=(pl.BlockSpec((tm, tn), lambda i, j: (i, j)), pl.BlockSpec((3, tm, tn), lambda i, j: (0, i, j))),
        compiler_params=_cp(),
    )(*ys, *ws, z, z, z)


def _gates_bwd(z, dm, proj, dz, *, t, d, gate_block, tm=512):
    def body(g_ref, dm_ref, p_ref, dz_in, dz_ref, da_ref, db_ref, dc_ref):
        q = pl.program_id(1)
        sg = _sigmoid(g_ref[...])
        dmv = dm_ref[...]
        dz_ref[...] = (dmv * p_ref[0].astype(F32) * sg * (1.0 - sg)).astype(BF16)
        dp = (dmv * sg).astype(BF16)
        @pl.when(q == 0)
        def _():
            da_ref[...] = dp

        @pl.when(q == 1)
        def _():
            db_ref[...] = dp

        @pl.when(q == 2)
        def _():
            dc_ref[...] = dp

    gs = pl.BlockSpec((tm, d), lambda i, q: (i, gate_block + q))
    o = pl.BlockSpec((tm, d), lambda i, q: (i, 0))
    return _pcall(
        body, name="gates_bwd", out_shape=(_sds(dz.shape, dz.dtype),) + (_sds((t, d), BF16),) * 3, grid=(t // tm, 3),
        in_specs=[gs, o, pl.BlockSpec((1, tm, d), lambda i, q: (q, i, 0)), HBM_SPEC], out_specs=(gs, o, o, o),
        input_output_aliases={3: 0}, compiler_params=_cp(dimension_semantics=("arbitrary", "arbitrary")),
    )(z, dm, proj, dz)


def _ffn_in_fwd(h2, wfi_t, *, t, d, dff, tm=512, tn=1408):
    nj = dff // tn

    def body(h_ref, wg_ref, wu_ref, g_ref, u_ref, a_ref):
        hv = h_ref[...]
        dn = (((1,), (1,)), ((), ()))
        gv = lax.dot_general(hv, wg_ref[...], dn, preferred_element_type=F32)
        uv = lax.dot_general(hv, wu_ref[...], dn, preferred_element_type=F32)
        g_ref[...] = gv.astype(BF16)
        u_ref[...] = uv.astype(BF16)
        a_ref[...] = (gv * _sigmoid(gv) * uv).astype(BF16)

    o = pl.BlockSpec((tm, tn), lambda i, j: (i, j))
    return _pcall(
        body, name="ffn_in_fwd", out_shape=(_sds((t, dff), BF16),) * 3, grid=(t // tm, nj),
        in_specs=[pl.BlockSpec((tm, d), lambda i, j: (i, 0)), pl.BlockSpec((tn, d), lambda i, j: (j, 0)),
                  pl.BlockSpec((tn, d), lambda i, j: (nj + j, 0))],
        out_specs=(o, o, o), compiler_params=_cp(),
    )(h2, wfi_t, wfi_t)


def _ffn_out_bwd(dy, wfo, gate, up, *, t, d, dff, tm=512, tn=1408):
    def body(dy_ref, w_ref, g_ref, u_ref, dg_ref, du_ref):
        dav = lax.dot_general(dy_ref[...].astype(BF16), w_ref[...], (((1,), (1,)), ((), ())), preferred_element_type=F32)
        gv = g_ref[...].astype(F32)
        sg = _sigmoid(gv)
        dg_ref[...] = (dav * u_ref[...].astype(F32) * sg * (1.0 + gv * (1.0 - sg))).astype(BF16)
        du_ref[...] = (dav * gv * sg).astype(BF16)

    o = pl.BlockSpec((tm, tn), lambda i, j: (i, j))
    return _pcall(
        body, name="ffn_out_bwd", out_shape=(_sds((t, dff), BF16),) * 2, grid=(t // tm, dff // tn),
        in_specs=[pl.BlockSpec((tm, d), lambda i, j: (i, 0)), pl.BlockSpec((tn, d), lambda i, j: (j, 0)), o, o],
        out_specs=(o, o), compiler_params=_cp(),
    )(dy, wfo, gate, up)


def _loss_head(y, target, *, t, d, tm=512):
    def body(y_ref, t_ref, dy_ref, l_ref):
        e = y_ref[...] - t_ref[...]
        dy_ref[...] = e * (1.0 / d)

        @pl.when(pl.program_id(0) == 0)
        def _():
            l_ref[...] = jnp.zeros_like(l_ref)

        l_ref[...] += jnp.sum(jnp.sum(e * e, axis=-1, keepdims=True), axis=0, keepdims=True) * (0.5 / d)

    o = pl.BlockSpec((tm, d), lambda i: (i, 0))
    dy, l = _pcall(
        body, name="loss_head", out_shape=(_sds((t, d), F32), _sds((1, HD), F32)), grid=(t // tm,),
        in_specs=[o, o], out_specs=(o, pl.BlockSpec((1, HD), lambda i: (0, 0))),
        compiler_params=_cp(dimension_semantics=("arbitrary",)),
    )(y, target)
    return dy, l[0, 0]


def _adamw(w, g, m, v, *, name):
    if w.ndim == 3:
        rows, _, cols = w.shape
        tm = max(tt for tt in range(1, 65) if rows % tt == 0)
        block, imap, grid = (tm, 1, cols), (lambda i, j: (i, 0, 0)), (rows // tm, 1)
    else:
        rows, cols = w.shape
        tm, tn = rows, cols
        for cand in (256, 128, 64, 32, 16, 8):
            if rows > cand and rows % cand == 0:
                tm = cand
                break
        if tm == rows and rows > 256 and cols % 128 == 0:
            tn = 128
        block, imap, grid = (tm, tn), (lambda i, j: (i, j)), (rows // tm, cols // tn)
    c1 = 1.0 - ADAM_B1 ** ADAM_STEP
    c2 = 1.0 - ADAM_B2 ** ADAM_STEP

    def body(w_ref, g_ref, m_ref, v_ref, d_ref, nm_ref, nv_ref):
        gv = g_ref[...]
        mn = ADAM_B1 * m_ref[...] + (1.0 - ADAM_B1) * gv
        vn = ADAM_B2 * v_ref[...] + (1.0 - ADAM_B2) * (gv * gv)
        d_ref[...] = -ADAM_LR * ((mn / c1) / (jnp.sqrt(vn / c2) + ADAM_EPS) + ADAM_WD * w_ref[...])
        nm_ref[...] = mn
        nv_ref[...] = vn

    spec = pl.BlockSpec(block, imap)
    return _pcall(
        body, name=name, out_shape=(_sds(w.shape, F32),) * 3, grid=grid,
        in_specs=[spec] * 4, out_specs=(spec,) * 3, compiler_params=_cp(),
    )(w, g, m, v)


def _pad_rows(a, rows):
    return a if a.shape[0] == rows else jnp.pad(a, ((0, rows - a.shape[0]),) + ((0, 0),) * (a.ndim - 1))


class _Pack:
    def __init__(self, row_sizes):
        self.rows = list(row_sizes)
        self.padded = [-(-r // 16) * 16 for r in self.rows]
        self.offs = [sum(self.padded[:i]) for i in range(len(self.rows))]
        self.total = sum(self.padded)

    def stack(self, blocks):
        return jnp.concatenate([_pad_rows(b.astype(BF16), p) for b, p in zip(blocks, self.padded)], axis=0)

    def full(self, gathered, i):
        r = self.rows[i]
        return gathered[:, self.offs[i]:self.offs[i] + r, :].reshape(N_DEV * r, gathered.shape[2])

    def for_core(self, full_grads, core):
        parts = []
        for gfull, r, p in zip(full_grads, self.rows, self.padded):
            blk = gfull.reshape(N_DEV // 2, 2, r, gfull.shape[1])
            blk = lax.dynamic_index_in_dim(blk, core, axis=1, keepdims=False).astype(BF16)
            if p != r:
                blk = jnp.pad(blk, ((0, 0), (0, p - r), (0, 0)))
            parts.append(blk)
        return jnp.concatenate(parts, axis=1)

    def shard(self, g_pack, i):
        return g_pack[self.offs[i]:self.offs[i] + self.rows[i], :]


def kernel(x, mem, norm1_g, w_in, conv_w, conv_b, fox_f_bias, fox_q_g, fox_k_g, mem_norm_g, w_mem_kv, xa_q_g, xa_k_g, w_br_conv, w_br_fox, w_br_xa, w_o, norm2_g, w_ffn_in, w_ffn_out, loss_target, m_norm1_g, m_w_in, m_conv_w, m_conv_b, m_fox_f_bias, m_fox_q_g, m_fox_k_g, m_mem_norm_g, m_w_mem_kv, m_xa_q_g, m_xa_k_g, m_w_br_conv, m_w_br_fox, m_w_br_xa, m_w_o, m_norm2_g, m_w_ffn_in, m_w_ffn_out, v_norm1_g, v_w_in, v_conv_w, v_conv_b, v_fox_f_bias, v_fox_q_g, v_fox_k_g, v_mem_norm_g, v_w_mem_kv, v_xa_q_g, v_xa_k_g, v_w_br_conv, v_w_br_fox, v_w_br_xa, v_w_o, v_norm2_g, v_w_ffn_in, v_w_ffn_out):
    nb, s, d = x.shape
    t = nb * s
    nm = mem.shape[1]
    nh = d // HD
    dff = w_ffn_out.shape[1] * N_DEV
    n_in = w_in.shape[2] * N_DEV
    n_in_pad = -(-n_in // 1152) * 1152
    ff_block = (10 * d) // HD
    my_c = lax.axis_index("c")
    my_q = 2 * lax.axis_index("x") + lax.axis_index("y")

    pack_a = _Pack([w_in.shape[2]])
    pack_b = _Pack([w_mem_kv.shape[2], w_br_conv.shape[1], w_br_fox.shape[1], w_br_xa.shape[1], w_o.shape[1],
                    w_ffn_in.shape[2], w_ffn_out.shape[1]])
    gather_1 = _Pack([w_ffn_in.shape[2], w_ffn_out.shape[1]])
    gather_2 = _Pack([w_mem_kv.shape[2], w_br_conv.shape[1], w_br_fox.shape[1], w_br_xa.shape[1], w_o.shape[1]])
    stack_a = pack_a.stack([w_in[0].T])
    stack_1 = gather_1.stack([w_ffn_in[0].T, w_ffn_out[0]])
    stack_2 = gather_2.stack([w_mem_kv[0].T, w_br_conv[0], w_br_fox[0], w_br_xa[0], w_o[0]])
    cw_block = _pad_rows(conv_w[0], 8)
    (slabs,) = _comm_only(_AllGather([stack_a]), "all_gather_w_in")
    rows_a = w_in.shape[2]
    ff_rows = _pad_rows(slabs[N_DEV - 1, rows_a - nh:rows_a, :], HD)

    x2 = x.reshape(t, d)
    mem2 = mem.reshape(nb * nm, d)
    tgt2 = loss_target.reshape(t, d)
    h1 = _rms_fwd(x2, norm1_g, name="rms1_fwd")
    z, (gathered_1,) = _z_from_slabs(h1, slabs, t=t, d=d, n_pad=n_in_pad, rows=rows_a, comm=_AllGather([stack_1]))
    z = _z_tail(h1, ff_rows, z, t=t, d=d, col_block=ff_block)
    qn, kn, vb = _fox_prep(z, fox_q_g, fox_k_g, t=t, d=d)
    f_bias_col = fox_f_bias.reshape(nh, 1)
    ck_rows = _fox_gates(z, f_bias_col, nb=nb, s=s, nh=nh, ff_block=ff_block).reshape(nb * nh, 1, s)
    y_fox, lse, (gathered_2, cw_g) = _fox_fwd(qn, kn, vb, ck_rows, nb=nb, s=s, nh=nh, comm=_AllGather([stack_2, cw_block]))
    wfi_t, wfo = (gather_1.full(gathered_1, i) for i in range(2))
    wkv_t, wbc, wbf, wbx, wo = (gather_2.full(gathered_2, i) for i in range(5))
    conv_w_full = jnp.transpose(cw_g[:, 0:3, :], (1, 0, 2)).reshape(3, d)

    y_conv = _conv_fwd(z, conv_w_full, conv_b, nb=nb, s=s, d=d)
    mem_n = _rms_fwd(mem2, mem_norm_g, name="rms_mem_fwd")
    kv = _mm(mem_n, wkv_t, m=nb * nm, n=2 * d, k=d, tb=True, name="mm_kv")
    xkn, xvb = _xa_prep_kv(kv, xa_k_g, rows=nb * nm, d=d)
    y_xa = _xa_fwd(z, xkn, xvb, xa_q_g, nb=nb, s=s, d=d, nm=nm, q_block=6)
    merged, proj = _branches_fwd(z, (y_conv, y_fox, y_xa), (wbc, wbf, wbx), t=t, d=d, gate_block=7)
    x1 = _mm(merged, wo, m=t, n=d, k=d, res=x2, name="mm_x1")
    h2 = _rms_fwd(x1, norm2_g, name="rms2_fwd")
    ff_gate, ff_up, act = _ffn_in_fwd(h2, wfi_t, t=t, d=d, dff=dff)
    y = _mm(act, wfo, m=t, n=d, k=dff, res=x1, tk=dff, name="mm_y")
    dy, loss_local = _loss_head(y, tgt2, t=t, d=d)

    g_wfo = _mm(act, dy, m=dff, n=d, k=t, ta=True, tm=1408, tk=2048, out_dtype=BF16, name="mm_g_wfo")
    dgate, dup = _ffn_out_bwd(dy, wfo, ff_gate, ff_up, t=t, d=d, dff=dff)
    dh2 = _mm(dgate, wfi_t, m=t, n=d, k=dff, tk=dff, name="mm_dh2_g")
    dh2 = _mm(dup, wfi_t, m=t, n=d, k=dff, tk=dff, b_off=(1, 0), res=dh2, name="mm_dh2_u")
    g_wfi_g = _mm(dgate, h2, m=dff, n=d, k=t, ta=True, tm=1408, tk=2048, out_dtype=BF16, name="mm_g_wfi_g")
    g_wfi_u = _mm(dup, h2, m=dff, n=d, k=t, ta=True, tm=1408, tk=2048, out_dtype=BF16, name="mm_g_wfi_u")
    dx1, g_norm2 = _rms_bwd(dh2, x1, norm2_g, dy, name="rms2_bwd")
    dmerged = _mm(dx1, wo, m=t, n=d, k=d, tb=True, name="mm_dmerged")
    g_wo = _mm(merged, dx1, m=d, n=d, k=t, ta=True, tk=2048, out_dtype=BF16, name="mm_g_wo")

    dz = lax.empty((t, n_in_pad), BF16)
    dz, dpa, dpb, dpc = _gates_bwd(z, dmerged, proj, dz, t=t, d=d, gate_block=7)
    dy_conv = _mm(dpa, wbc, m=t, n=d, k=d, tb=True, name="mm_dy_conv")
    g_wbc = _mm(y_conv, dpa, m=d, n=d, k=t, ta=True, tk=4096, out_dtype=BF16, name="mm_g_wbc")
    dy_fox = _mm(dpb, wbf, m=t, n=d, k=d, tb=True, name="mm_dy_fox")
    g_wbf = _mm(y_fox, dpb, m=d, n=d, k=t, ta=True, tk=4096, out_dtype=BF16, name="mm_g_wbf")
    dy_xa = _mm(dpc, wbx, m=t, n=d, k=d, tb=True, name="mm_dy_xa")
    g_wbx = _mm(y_xa, dpc, m=d, n=d, k=t, ta=True, tk=4096, out_dtype=BF16, name="mm_g_wbx")

    dz, g_conv_w, g_conv_b = _conv_bwd(z, dy_conv, conv_w_full, conv_b, dz, nb=nb, s=s, d=d)
    dz, dxkn, dxv, g_xa_q = _xa_bwd(z, xkn, xvb, xa_q_g, dy_xa, dz, nb=nb, s=s, d=d, nm=nm, q_block=6)
    dkv, g_xa_k = _xa_post_kv(kv, dxkn, dxv, xa_k_g, rows=nb * nm, d=d)
    g_wkv_t = _mm(dkv, mem_n, m=2 * d, n=d, k=nb * nm, ta=True, out_dtype=BF16, name="mm_g_wkv")
    dmem_n = _mm(dkv, wkv_t, m=nb * nm, n=d, k=2 * d, tk=2 * d, name="mm_dmem")
    _, g_mem_norm = _rms_bwd(dmem_n, mem2, mem_norm_g, None, name="rms_mem_bwd")

    grads_b = [g_wkv_t, g_wbc, g_wbf, g_wbx, g_wo, jnp.concatenate([g_wfi_g, g_wfi_u], axis=0), g_wfo]
    keep_b, send_b = pack_b.for_core(grads_b, my_c), pack_b.for_core(grads_b, 1 - my_c)
    dqn, ds_rows, (from_sibling_b,) = _fox_bwd_dq(qn, kn, vb, ck_rows, dy_fox, y_fox, lse, nb=nb, s=s, nh=nh,
                                                  comm=_SiblingSwap(send_b))
    part_b = _add2(keep_b, from_sibling_b, name="rs_add_sibling_b")
    dkn, dvb, ds_cols, (from_chips_b,) = _fox_bwd_dkv(qn, kn, vb, ck_rows, dy_fox, y_fox, lse, nb=nb, s=s, nh=nh,
                                                     comm=_ChipExchange(part_b))
    g_pack_b = _add4(lax.dynamic_index_in_dim(part_b, my_q, axis=0, keepdims=False), from_chips_b, name="rs_add_chips_b")
    dz, g_fox_q, g_fox_k = _fox_post(z, dqn, dkn, dvb, fox_q_g, fox_k_g, dz, t=t, d=d)
    dz, g_f_bias = _fox_gates_bwd(z, f_bias_col, ds_rows, ds_cols, dz, nb=nb, s=s, nh=nh, ff_block=ff_block)

    keep_a, from_sibling_a = _g_win_slabs(dz, h1, my_c, t=t, d=d, n_slab=N_DEV, rows=rows_a, rp=slabs.shape[1])
    part_a = _add2(keep_a, from_sibling_a, name="rs_add_sibling_a")
    dh1, (from_chips_a,) = _dh1_from_slabs(dz, slabs, t=t, d=d, rows=rows_a, comm=_ChipExchange(part_a))
    g_pack_a = _add4(lax.dynamic_index_in_dim(part_a, my_q, axis=0, keepdims=False), from_chips_a, name="rs_add_chips_a",
                     rows_apart=True)
    dx, g_norm1 = _rms_bwd(dh1, x2, norm1_g, dx1, name="rms1_bwd")

    xd = d // XH
    tail = jnp.concatenate(
        [_pad_rows(g_f_bias.reshape(nh), HD).reshape(1, HD), g_fox_q, g_fox_k, g_xa_q, g_xa_k,
         jnp.zeros((1, d - 3 * HD - 2 * xd), F32)], axis=1)
    small = jnp.concatenate([g_norm1, g_norm2, g_conv_w, g_conv_b, g_mem_norm, tail], axis=0)
    small = _all_reduce_small(small)
    gs_conv_w = lax.dynamic_slice_in_dim(small[2:5], (2 * my_q + my_c) * (d // N_DEV), d // N_DEV, axis=1)

    grads = {
        "norm1_g": small[0:1], "w_in": g_pack_a, "conv_w": gs_conv_w, "conv_b": small[5:6],
        "fox_f_bias": small[7:8, 0:nh], "fox_q_g": small[7:8, HD:2 * HD], "fox_k_g": small[7:8, 2 * HD:3 * HD],
        "mem_norm_g": small[6:7], "w_mem_kv": pack_b.shard(g_pack_b, 0), "xa_q_g": small[7:8, 3 * HD:3 * HD + xd],
        "xa_k_g": small[7:8, 3 * HD + xd:3 * HD + 2 * xd], "w_br_conv": pack_b.shard(g_pack_b, 1),
        "w_br_fox": pack_b.shard(g_pack_b, 2), "w_br_xa": pack_b.shard(g_pack_b, 3), "w_o": pack_b.shard(g_pack_b, 4),
        "norm2_g": small[1:2], "w_ffn_in": pack_b.shard(g_pack_b, 5), "w_ffn_out": pack_b.shard(g_pack_b, 6),
    }
    transposed = {"w_in", "w_mem_kv", "w_ffn_in"}
    weights = {
        "norm1_g": (norm1_g, m_norm1_g, v_norm1_g), "w_in": (w_in, m_w_in, v_w_in), "conv_w": (conv_w, m_conv_w, v_conv_w),
        "conv_b": (conv_b, m_conv_b, v_conv_b), "fox_f_bias": (fox_f_bias, m_fox_f_bias, v_fox_f_bias),
        "fox_q_g": (fox_q_g, m_fox_q_g, v_fox_q_g), "fox_k_g": (fox_k_g, m_fox_k_g, v_fox_k_g),
        "mem_norm_g": (mem_norm_g, m_mem_norm_g, v_mem_norm_g), "w_mem_kv": (w_mem_kv, m_w_mem_kv, v_w_mem_kv),
        "xa_q_g": (xa_q_g, m_xa_q_g, v_xa_q_g), "xa_k_g": (xa_k_g, m_xa_k_g, v_xa_k_g),
        "w_br_conv": (w_br_conv, m_w_br_conv, v_w_br_conv), "w_br_fox": (w_br_fox, m_w_br_fox, v_w_br_fox),
        "w_br_xa": (w_br_xa, m_w_br_xa, v_w_br_xa), "w_o": (w_o, m_w_o, v_w_o), "norm2_g": (norm2_g, m_norm2_g, v_norm2_g),
        "w_ffn_in": (w_ffn_in, m_w_ffn_in, v_w_ffn_in), "w_ffn_out": (w_ffn_out, m_w_ffn_out, v_w_ffn_out),
    }
    out_g, out_d, out_m, out_v = [], [], [], []
    for name, (w, m, v) in weights.items():
        shape = w.shape
        if name == "w_in":
            w3, m3, v3 = (a.transpose(2, 0, 1) for a in (w, m, v))
            dl, mn, vn = _adamw(w3, g_pack_a, m3, v3, name="adamw_" + name)
            out_g.append(g_pack_a[:shape[2]].transpose(1, 2, 0))
            out_d.append(dl.transpose(1, 2, 0))
            out_m.append(mn.transpose(1, 2, 0))
            out_v.append(vn.transpose(1, 2, 0))
            continue
        tr = name in transposed

        def as2d(a):
            a = a.reshape(shape[-2], shape[-1])
            return a.T if tr else a

        def back(a):
            return (a.T if tr else a).reshape(shape)

        w2 = as2d(w)
        g2 = grads[name].reshape(w2.shape)
        dl, mn, vn = _adamw(w2, g2, as2d(m), as2d(v), name="adamw_" + name)
        out_g.append(back(g2))
        out_d.append(back(dl))
        out_m.append(back(mn))
        out_v.append(back(vn))

    loss = lax.psum(loss_local, ("x", "y", "c"))
    return (loss, dx.reshape(nb, s, d), *out_g, *out_d, *out_m, *out_v)
```

```python
import math

import jax
import jax.numpy as jnp
from jax import lax
from jax.experimental import pallas as pl
from jax.experimental.pallas import tpu as pltpu

F32, BF16 = jnp.float32, jnp.bfloat16
EPS = 1e-6
HD = 128
XH = 4
N_DEV = 8
MESH = pl.DeviceIdType.MESH
VMEM_LIMIT = 52 * 1024 * 1024
NEG = -1e30
SQRT_HD = math.sqrt(HD)
EXP2_C = math.log2(math.e) / SQRT_HD
LOG2E = math.log2(math.e)
FOX_HP = 4

ADAM_LR, ADAM_B1, ADAM_B2, ADAM_EPS, ADAM_WD, ADAM_STEP = 0.001, 0.9, 0.999, 1e-08, 0.01, 10


def _cp(**kw):
    return pltpu.CompilerParams(vmem_limit_bytes=VMEM_LIMIT, **kw)


def _pcall(body, **kw):
    return pl.pallas_call(body, **kw)


def _sds(shape, dtype):
    return jax.ShapeDtypeStruct(shape, dtype)


HBM_SPEC = pl.BlockSpec(memory_space=pl.ANY)


def _mesh_pos():
    return lax.axis_index("x"), lax.axis_index("y"), lax.axis_index("c")


class _AllGather:
    def __init__(self, blocks):
        self.blocks = list(blocks)
        n = len(self.blocks)
        self.out_shape = [_sds((N_DEV,) + b.shape, b.dtype) for b in self.blocks]
        self.scratch = [pltpu.SemaphoreType.DMA((n, 7)), pltpu.SemaphoreType.DMA((n, 7)), pltpu.SemaphoreType.DMA((n,))]

    def bind(self, ins, outs, sems):
        n = len(ins)
        send_sems, recv_sems, local_sems = sems
        x, y, c = _mesh_pos()
        me, sibling = (x, y, c), (x, y, 1 - c)
        chips = [(1 - x, y), (x, 1 - y), (1 - x, 1 - y)]

        def slab(t, dev):
            return outs[t].at[4 * dev[0] + 2 * dev[1] + dev[2]]

        def copy(t, k, block, to, src=None):
            dst = slab(t, block)
            return pltpu.make_async_remote_copy(
                src_ref=dst if src is None else src, dst_ref=dst, send_sem=send_sems.at[t, k], recv_sem=recv_sems.at[t, k],
                device_id=to, device_id_type=MESH)

        mine = [pltpu.make_async_copy(ins[t], slab(t, me), local_sems.at[t]) for t in range(n)]
        first = []
        for t in range(n):
            first.append(copy(t, 0, me, sibling, src=ins[t]))
            first += [copy(t, 1 + j, me, (*chip, c), src=ins[t]) for j, chip in enumerate(chips)]
        passed = [copy(t, 4 + j, (*chip, c), sibling) for j, chip in enumerate(chips) for t in range(n)]

        def start():
            for cp in mine + first:
                cp.start()

        def mid():
            for j, chip in enumerate(chips):
                for t in range(n):
                    copy(t, 1 + j, (*chip, c), me).wait_recv()
                    passed[j * n + t].start()

        def finish():
            for t in range(n):
                copy(t, 0, sibling, me).wait_recv()
                for j, chip in enumerate(chips):
                    copy(t, 4 + j, (*chip, 1 - c), me).wait_recv()
            for cp in first + passed:
                cp.wait_send()
            for cp in mine:
                cp.wait()

        return start, mid, finish


class _SiblingSwap:
    def __init__(self, send):
        self.blocks = [send]
        self.out_shape = [_sds(send.shape, send.dtype)]
        self.scratch = [pltpu.SemaphoreType.DMA, pltpu.SemaphoreType.DMA]

    def bind(self, ins, outs, sems):
        x, y, c = _mesh_pos()
        cp = pltpu.make_async_remote_copy(src_ref=ins[0], dst_ref=outs[0], send_sem=sems[0], recv_sem=sems[1],
                                          device_id=(x, y, 1 - c), device_id_type=MESH)
        return cp.start, (lambda: None), cp.wait


class _ChipExchange:
    def __init__(self, part):
        self.blocks = [part]
        self.out_shape = [_sds((3,) + part.shape[1:], part.dtype)]
        self.scratch = [pltpu.SemaphoreType.DMA((3,)), pltpu.SemaphoreType.DMA((3,))]

    def bind(self, ins, outs, sems):
        x, y, c = _mesh_pos()
        chips = [(1 - x, y), (x, 1 - y), (1 - x, 1 - y)]
        cps = [
            pltpu.make_async_remote_copy(src_ref=ins[0].at[2 * cx + cy], dst_ref=outs[0].at[j], send_sem=sems[0].at[j],
                                         recv_sem=sems[1].at[j], device_id=(cx, cy, c), device_id_type=MESH)
            for j, (cx, cy) in enumerate(chips)
        ]

        def start():
            for cp in cps:
                cp.start()

        def finish():
            for cp in cps:
                cp.wait()

        return start, (lambda: None), finish


def _comm_lists(comm):
    if comm is None:
        return [], [], [], []
    n = len(comm.blocks)
    return list(comm.blocks), [HBM_SPEC] * n, list(comm.out_shape), list(comm.scratch)


def _split_refs(refs, n_pre, n_in, n_cin, n_out, n_cout, n_scr):
    cuts = [n_pre, n_in, n_cin, n_out, n_cout, n_scr]
    out, at = [], 0
    for c in cuts:
        out.append(refs[at:at + c])
        at += c
    out.append(refs[at:])
    return out


def _run_comm(comm, cin, cout, csem, step, n_steps, when="start"):
    if comm is None:
        return
    start, mid, finish = comm.bind(cin, cout, csem)
    if when == "start":
        pl.when(step == 0)(start)
    else:
        pl.when(step == max(n_steps - 2, 0))(mid)
        pl.when(step == n_steps - 1)(finish)


def _comm_only(comm, name):
    ins, in_specs, out_shape, scratch = _comm_lists(comm)
    n = len(ins)

    def body(*refs):
        start, mid, finish = comm.bind(refs[:n], refs[n:2 * n], refs[2 * n:])
        start()
        mid()
        finish()

    outs = _pcall(body, name=name, out_shape=tuple(out_shape), in_specs=in_specs, out_specs=tuple([HBM_SPEC] * n),
                  scratch_shapes=scratch, compiler_params=_cp())(*ins)
    return list(outs)


def _all_reduce_small(v):
    rows, cols = v.shape

    def body(v_ref, o_ref, slots, send_sems, recv_sems):
        x, y, c = _mesh_pos()
        me = 4 * x + 2 * y + c
        slots[me] = v_ref[...]
        cps = []
        for k in range(1, N_DEV):
            px, py, pc = x ^ (k >> 2), y ^ ((k >> 1) & 1), c ^ (k & 1)
            cps.append(pltpu.make_async_remote_copy(
                src_ref=v_ref, dst_ref=slots.at[me], send_sem=send_sems.at[k - 1], recv_sem=recv_sems.at[k - 1],
                device_id=(px, py, pc), device_id_type=MESH))
        for cp in cps:
            cp.start()
        for cp in cps:
            cp.wait()
        acc = slots[0]
        for k in range(1, N_DEV):
            acc = acc + slots[k]
        o_ref[...] = acc

    return _pcall(
        body, name="all_reduce_small", out_shape=_sds((rows, cols), F32),
        in_specs=[pl.BlockSpec(memory_space=pltpu.VMEM)], out_specs=pl.BlockSpec(memory_space=pltpu.VMEM),
        scratch_shapes=[pltpu.VMEM((N_DEV, rows, cols), F32), pltpu.SemaphoreType.DMA((7,)), pltpu.SemaphoreType.DMA((7,))],
        compiler_params=_cp(),
    )(v)


def _row_tile(rows, cap=1040):
    return max(tt for tt in range(16, cap + 1, 16) if rows % tt == 0)


def _add2(a, b, *, name):
    nq, rows, cols = a.shape
    tm = _row_tile(rows)

    def body(a_ref, b_ref, o_ref):
        o_ref[...] = (a_ref[...].astype(F32) + b_ref[...].astype(F32)).astype(BF16)

    spec = pl.BlockSpec((1, tm, cols), lambda q, i: (q, i, 0))
    return _pcall(body, name=name, out_shape=_sds(a.shape, BF16), grid=(nq, rows // tm),
                  in_specs=[spec, spec], out_specs=spec, compiler_params=_cp())(a, b)


def _add4(own, recv, *, name, rows_apart=False):
    rows, cols = own.shape
    tm = _row_tile(rows, 256 if rows_apart else 1040)

    def body(o_ref, r_ref, g_ref):
        tot = ((o_ref[...].astype(F32) + r_ref[0].astype(F32)) + r_ref[1].astype(F32)) + r_ref[2].astype(F32)
        if rows_apart:
            g_ref[:, 0, :] = tot
        else:
            g_ref[...] = tot

    if rows_apart:
        out_shape, out_spec = _sds((rows, 1, cols), F32), pl.BlockSpec((tm, 1, cols), lambda i: (i, 0, 0))
    else:
        out_shape, out_spec = _sds((rows, cols), F32), pl.BlockSpec((tm, cols), lambda i: (i, 0))
    return _pcall(
        body, name=name, out_shape=out_shape, grid=(rows // tm,),
        in_specs=[pl.BlockSpec((tm, cols), lambda i: (i, 0)), pl.BlockSpec((3, tm, cols), lambda i: (0, i, 0))],
        out_specs=out_spec, compiler_params=_cp(),
    )(own, recv)


def _mm(a, b, *, m, n, k, name, ta=False, tb=False, b_off=(0, 0), res=None, out_dtype=F32, tm=512, tn=1024, tk=1024,
        comm=None):
    tm, tn, tk = min(tm, m), min(tn, n), min(tk, k)
    assert m % tm == 0 and n % tn == 0 and k % tk == 0, (name, m, n, k, tm, tn, tk)
    nk = k // tk
    grid = (m // tm, n // tn, nk)
    bo0, bo1 = b_off
    if ta:
        a_spec = pl.BlockSpec((tk, tm), lambda i, j, kk: (kk, i))
    else:
        a_spec = pl.BlockSpec((tm, tk), lambda i, j, kk: (i, kk))
    if tb:
        b_spec = pl.BlockSpec((tn, tk), lambda i, j, kk: (j + bo0, kk + bo1))
    else:
        b_spec = pl.BlockSpec((tk, tn), lambda i, j, kk: (kk + bo0, j + bo1))
    o_spec = pl.BlockSpec((tm, tn), lambda i, j, kk: (i, j))
    dn = (((0 if ta else 1,), (1 if tb else 0,)), ((), ()))
    has_res = res is not None
    c_ins, c_in_specs, c_out_shape, c_scratch = _comm_lists(comm)
    n_in = 3 if has_res else 2
    n_scr = 1 if nk > 1 else 0

    def body(*refs):
        _, ins, cin, outs, cout, scr, csem = _split_refs(refs, 0, n_in, len(c_ins), 1, len(c_ins), n_scr)
        a_ref, b_ref = ins[0], ins[1]
        r_ref = ins[2] if has_res else None
        o_ref = outs[0]
        step = (pl.program_id(0) * grid[1] + pl.program_id(1)) * nk + pl.program_id(2)
        _run_comm(comm, cin, cout, csem, step, grid[0] * grid[1] * nk, "start")
        p = lax.dot_general(a_ref[...].astype(BF16), b_ref[...].astype(BF16), dn, preferred_element_type=F32)

        def finish(acc):
            if has_res:
                acc = acc + r_ref[...]
            o_ref[...] = acc.astype(out_dtype)

        if nk == 1:
            finish(p)
        else:
            acc_ref = scr[0]
            kk = pl.program_id(2)

            @pl.when(kk == 0)
            def _():
                acc_ref[...] = p

            @pl.when(kk > 0)
            def _():
                acc_ref[...] += p

            @pl.when(kk == nk - 1)
            def _():
                finish(acc_ref[...])

        _run_comm(comm, cin, cout, csem, step, grid[0] * grid[1] * nk, "end")

    ins = [a, b] + ([res] if has_res else [])
    in_specs = [a_spec, b_spec] + ([o_spec] if has_res else [])
    sem = ("arbitrary",) * 3 if comm is not None else ("parallel", "parallel", "arbitrary")
    outs = _pcall(
        body, name=name, out_shape=tuple([_sds((m, n), out_dtype)] + c_out_shape), grid=grid,
        in_specs=in_specs + c_in_specs, out_specs=tuple([o_spec] + [HBM_SPEC] * len(c_ins)),
        scratch_shapes=([pltpu.VMEM((tm, tn), F32)] if nk > 1 else []) + c_scratch,
        compiler_params=_cp(dimension_semantics=sem),
    )(*(ins + c_ins))
    return outs[0] if comm is None else (outs[0], list(outs[1:]))


SLAB_WIN = 1280
SLAB_TAIL = 48


def _z_gathered(h1, own_slab, order, *, t, d, n_pad, rows, comm=None, tm=1024):
    rp = own_slab.shape[0]
    n_slab = N_DEV
    tm = min(tm, t)
    ni = t // tm
    sh = rows - SLAB_WIN
    assert sh >= 0 and (n_slab - 1) * sh <= 128
    c_ins, c_in_specs, c_out_shape, c_scratch = _comm_lists(comm)
    n_steps = n_slab * ni

    def body(*refs):
        pre, ins, cin, outs, cout, scr, csem = _split_refs(refs, 1, 2, len(c_ins), 2, len(c_ins), 5)
        order_ref = pre[0]
        h_ref, own_ref = ins
        o_ref, slabs_ref = outs
        slab_vm, send_sems, recv_sems, local_sem, load_sem = scr
        p, i = pl.program_id(0), pl.program_id(1)
        step = p * ni + i
        x, y, c = _mesh_pos()
        me, sibling = (x, y, c), (x, y, 1 - c)
        chips = [(1 - x, y), (x, 1 - y), (1 - x, 1 - y)]

        def slab(dev):
            return slabs_ref.at[4 * dev[0] + 2 * dev[1] + dev[2]]

        def copy(k, block, to, src=None):
            dst = slab(block)
            return pltpu.make_async_remote_copy(
                src_ref=dst if src is None else src, dst_ref=dst, send_sem=send_sems.at[k], recv_sem=recv_sems.at[k],
                device_id=to, device_id_type=MESH)

        mine = pltpu.make_async_copy(own_ref, slab(me), local_sem)
        first = [copy(0, me, sibling, src=own_ref)] + [copy(1 + j, me, (*chip, c), src=own_ref) for j, chip in enumerate(chips)]
        passed = [copy(4 + j, (*chip, c), sibling) for j, chip in enumerate(chips)]

        @pl.when(step == 0)
        def _():
            mine.start()
            for cp in first:
                cp.start()

        _run_comm(comm, cin, cout, csem, step, n_steps, "start")

        def load(src):
            cp = pltpu.make_async_copy(src, slab_vm, load_sem)
            cp.start()
            cp.wait()

        @pl.when(i == 0)
        def _():
            @pl.when(p == 0)
            def _():
                load(own_ref)

            @pl.when(p == 1)
            def _():
                copy(0, sibling, me).wait_recv()
                load(slab(sibling))

            for j, chip in enumerate(chips):
                @pl.when(p == 2 + 2 * j)
                def _():
                    copy(1 + j, (*chip, c), me).wait_recv()
                    passed[j].start()
                    load(slab((*chip, c)))

                @pl.when(p == 3 + 2 * j)
                def _():
                    copy(4 + j, (*chip, 1 - c), me).wait_recv()
                    load(slab((*chip, 1 - c)))

        k = order_ref[p]
        y_k = lax.dot_general(h_ref[...], slab_vm[...], (((1,), (1,)), ((), ())), preferred_element_type=F32)
        o_ref[...] = pltpu.roll(y_k[:, 0:SLAB_WIN], k * sh, axis=1)

        @pl.when(step == n_steps - 1)
        def _():
            for cp in first + passed:
                cp.wait_send()
            mine.wait()

        _run_comm(comm, cin, cout, csem, step, n_steps, "end")

    outs = _pcall(
        body, name="mm_z", out_shape=tuple([_sds((t, n_pad), F32), _sds((n_slab, rp, d), BF16)] + c_out_shape),
        grid_spec=pltpu.PrefetchScalarGridSpec(
            num_scalar_prefetch=1, grid=(n_slab, ni),
            in_specs=[pl.BlockSpec((tm, d), lambda p, i, order: (i, 0)), HBM_SPEC] + c_in_specs,
            out_specs=tuple([pl.BlockSpec((tm, SLAB_WIN), lambda p, i, order: (i, order[p])), HBM_SPEC] + [HBM_SPEC] * len(c_ins)),
            scratch_shapes=[pltpu.VMEM((rp, d), BF16), pltpu.SemaphoreType.DMA((7,)), pltpu.SemaphoreType.DMA((7,)),
                            pltpu.SemaphoreType.DMA, pltpu.SemaphoreType.DMA] + c_scratch),
        compiler_params=_cp(dimension_semantics=("arbitrary", "arbitrary")),
    )(order, h1, own_slab, *c_ins)
    return outs[0], outs[1], list(outs[2:])


def _z_fix(h1, slabs, z, *, t, d, rows, tm=1024):
    n_slab, rp, _ = slabs.shape
    tm = min(tm, t)
    sh = rows - SLAB_WIN
    tail_lo = rp - SLAB_TAIL
    assert rows - (n_slab - 1) * sh >= tail_lo and rp % SLAB_TAIL == 0

    def body(h_ref, t_ref, z_ref, o_ref):
        k = pl.program_id(0) + 1
        p = lax.dot_general(h_ref[...], t_ref[0], (((1,), (1,)), ((), ())), preferred_element_type=F32)
        pp = jnp.concatenate([p, jnp.zeros((tm, 128 - SLAB_TAIL), F32)], axis=1)
        pr = pltpu.roll(pp, 128 - (rows - tail_lo) + k * sh, axis=1)
        lane = lax.broadcasted_iota(jnp.int32, (tm, 128), 1)
        o_ref[...] = jnp.where(lane < k * sh, pr, z_ref[...])

    blk = pl.BlockSpec((tm, 128), lambda b, i: (i, (SLAB_WIN // 128) * (b + 1)))
    return _pcall(
        body, name="mm_z_fix", out_shape=_sds(z.shape, z.dtype), grid=(n_slab - 1, t // tm),
        in_specs=[pl.BlockSpec((tm, d), lambda b, i: (i, 0)),
                  pl.BlockSpec((1, SLAB_TAIL, d), lambda b, i: (b, tail_lo // SLAB_TAIL, 0)), blk],
        out_specs=blk, input_output_aliases={2: 0}, compiler_params=_cp(),
    )(h1, slabs, z)


def _z_tail(h1, w_rows, z, *, t, d, col_block, tm=1024):
    def body(h_ref, w_ref, z_in, o_ref):
        o_ref[...] = lax.dot_general(h_ref[...], w_ref[...], (((1,), (1,)), ((), ())), preferred_element_type=F32)

    return _pcall(
        body, name="mm_z_tail", out_shape=_sds(z.shape, z.dtype), grid=(t // tm,),
        in_specs=[pl.BlockSpec((tm, d), lambda i: (i, 0)), pl.BlockSpec((HD, d), lambda i: (0, 0)), HBM_SPEC],
        out_specs=pl.BlockSpec((tm, HD), lambda i: (i, col_block)), input_output_aliases={2: 0}, compiler_params=_cp(),
    )(h1, w_rows, z)


def _slab_window(a1_ref, a2_ref, k, sh):
    w = jnp.concatenate([a1_ref[...], a2_ref[...]], axis=1)
    width = SLAB_WIN + 128
    return pltpu.roll(w, width - k * sh, axis=1)


def _g_win_slabs(dz, h1, my_c, *, t, d, n_slab, rows, rp, tk=2048):
    sh = rows - SLAB_WIN
    tk = min(tk, t)
    nk = t // tk
    width = SLAB_WIN + 128
    half = n_slab // 2

    def slab_of(p, c):
        return 2 * (p % half) + jnp.where(p < half, 1 - c, c)

    def body(c_ref, a1_ref, a2_ref, h_ref, keep_ref, land_ref, acc, stage, send_sems, recv_sems):
        p, kk = pl.program_id(0), pl.program_id(1)
        x, y, c = _mesh_pos()
        ws = _slab_window(a1_ref, a2_ref, slab_of(p, c), sh)
        prod = lax.dot_general(ws, h_ref[...], (((0,), (0,)), ((), ())), preferred_element_type=F32)

        def push(q):
            return pltpu.make_async_remote_copy(
                src_ref=stage.at[q % 2], dst_ref=land_ref.at[q], send_sem=send_sems.at[q], recv_sem=recv_sems.at[q],
                device_id=(x, y, 1 - c), device_id_type=MESH)

        @pl.when(kk == 0)
        def _():
            acc[...] = prod

        @pl.when(kk > 0)
        def _():
            acc[...] += prod

        @pl.when(kk == nk - 1)
        def _():
            res = acc[0:rp, :].astype(BF16)
            for q in range(half):
                @pl.when(p == q)
                def _():
                    if q >= 2:
                        push(q - 2).wait_send()
                    stage[q % 2] = res
                    push(q).start()

            @pl.when(p >= half)
            def _():
                keep_ref[0] = res

        @pl.when((p == n_slab - 1) & (kk == nk - 1))
        def _():
            for q in range(max(half - 2, 0), half):
                push(q).wait_send()
            for q in range(half):
                push(q).wait_recv()

    outs = _pcall(
        body, name="mm_g_win", out_shape=(_sds((half, rp, d), BF16), _sds((half, rp, d), BF16)),
        grid_spec=pltpu.PrefetchScalarGridSpec(
            num_scalar_prefetch=1, grid=(n_slab, nk),
            in_specs=[pl.BlockSpec((tk, SLAB_WIN), lambda p, kk, c: (kk, slab_of(p, c[0]))),
                      pl.BlockSpec((tk, 128), lambda p, kk, c: (kk, (SLAB_WIN // 128) * (slab_of(p, c[0]) + 1))),
                      pl.BlockSpec((tk, d), lambda p, kk, c: (kk, 0))],
            out_specs=(pl.BlockSpec((1, rp, d), lambda p, kk, c: (jnp.maximum(p - half, 0), 0, 0)), HBM_SPEC),
            scratch_shapes=[pltpu.VMEM((width, d), F32), pltpu.VMEM((2, rp, d), BF16),
                            pltpu.SemaphoreType.DMA((half,)), pltpu.SemaphoreType.DMA((half,))]),
        compiler_params=_cp(dimension_semantics=("arbitrary", "arbitrary")),
    )(my_c.reshape(1).astype(jnp.int32), dz, dz, h1)
    return outs[0], outs[1]


def _dh1_from_slabs(dz, slabs, *, t, d, rows, comm=None, tm=1024):
    n_slab, rp, _ = slabs.shape
    sh = rows - SLAB_WIN
    width = SLAB_WIN + 128
    c_ins, c_in_specs, c_out_shape, c_scratch = _comm_lists(comm)
    grid = (t // tm, n_slab)

    def body(*refs):
        _, ins, cin, outs, cout, scr, csem = _split_refs(refs, 0, 3, len(c_ins), 1, len(c_ins), 1)
        a1_ref, a2_ref, s_ref = ins
        o_ref, acc = outs[0], scr[0]
        k = pl.program_id(1)
        step = pl.program_id(0) * n_slab + k
        _run_comm(comm, cin, cout, csem, step, grid[0] * n_slab, "start")
        ws = _slab_window(a1_ref, a2_ref, k, sh)
        s_end = jnp.concatenate([s_ref[0, SLAB_WIN:rp, :], jnp.zeros((width - rp, d), BF16)], axis=0)
        p = (jnp.dot(ws[:, 0:SLAB_WIN], s_ref[0, 0:SLAB_WIN, :], preferred_element_type=F32)
             + jnp.dot(ws[:, SLAB_WIN:width], s_end, preferred_element_type=F32))

        @pl.when(k == 0)
        def _():
            acc[...] = p

        @pl.when(k > 0)
        def _():
            acc[...] += p

        @pl.when(k == n_slab - 1)
        def _():
            o_ref[...] = acc[...]

        _run_comm(comm, cin, cout, csem, step, grid[0] * n_slab, "end")

    outs = _pcall(
        body, name="mm_dh1", out_shape=tuple([_sds((t, d), F32)] + c_out_shape), grid=grid,
        in_specs=[pl.BlockSpec((tm, SLAB_WIN), lambda i, k: (i, k)),
                  pl.BlockSpec((tm, 128), lambda i, k: (i, (SLAB_WIN // 128) * (k + 1))),
                  pl.BlockSpec((1, rp, d), lambda i, k: (k, 0, 0))] + c_in_specs,
        out_specs=tuple([pl.BlockSpec((tm, d), lambda i, k: (i, 0))] + [HBM_SPEC] * len(c_ins)),
        scratch_shapes=[pltpu.VMEM((tm, d), F32)] + c_scratch,
        compiler_params=_cp(dimension_semantics=("arbitrary", "arbitrary")),
    )(dz, dz, slabs, *c_ins)
    return outs[0], list(outs[1:])


def _rms_fwd(x, g, *, name, tm=512):
    t, d = x.shape
    tm = min(tm, t)

    def body(x_ref, g_ref, h_ref):
        xv = x_ref[...]
        r = lax.rsqrt(jnp.mean(xv * xv, axis=-1, keepdims=True) + EPS)
        h_ref[...] = (xv * r * g_ref[...]).astype(BF16)

    return _pcall(
        body, name=name, out_shape=_sds((t, d), BF16), grid=(t // tm,),
        in_specs=[pl.BlockSpec((tm, d), lambda i: (i, 0)), pl.BlockSpec((1, d), lambda i: (0, 0))],
        out_specs=pl.BlockSpec((tm, d), lambda i: (i, 0)), compiler_params=_cp(),
    )(x, g)


def _rms_bwd(dh, x, g, res, *, name, tm=512):
    t, d = x.shape
    tm = min(tm, t)
    has_res = res is not None

    def body(*refs):
        dh_ref, x_ref, g_ref = refs[:3]
        r_ref = refs[3] if has_res else None
        dx_ref, dg_ref = refs[-2], refs[-1]
        xv = x_ref[...]
        r = lax.rsqrt(jnp.mean(xv * xv, axis=-1, keepdims=True) + EPS)
        xh = xv * r
        dhv = dh_ref[...]
        gd = dhv * g_ref[...]
        dx = r * (gd - xh * jnp.mean(gd * xh, axis=-1, keepdims=True))
        if has_res:
            dx = dx + r_ref[...]
        dx_ref[...] = dx

        @pl.when(pl.program_id(0) == 0)
        def _():
            dg_ref[...] = jnp.zeros_like(dg_ref)

        dg_ref[...] += jnp.sum(dhv * xh, axis=0, keepdims=True)

    row = pl.BlockSpec((tm, d), lambda i: (i, 0))
    vec = pl.BlockSpec((1, d), lambda i: (0, 0))
    return _pcall(
        body, name=name, out_shape=(_sds((t, d), F32), _sds((1, d), F32)), grid=(t // tm,),
        in_specs=[row, row, vec] + ([row] if has_res else []), out_specs=(row, vec),
        compiler_params=_cp(dimension_semantics=("arbitrary",)),
    )(*([dh, x, g] + ([res] if has_res else [])))


def _conv_fwd(z, conv_w, conv_b, *, nb, s, d, tc=256):
    nc = d // tc

    def body(cb_ref, cc_ref, cv_ref, w_ref, b_ref, y_ref):
        u = cc_ref[...] * cv_ref[...]
        row = lax.broadcasted_iota(jnp.int32, u.shape, 0)
        u1 = jnp.where(row >= 1, pltpu.roll(u, 1, axis=0), 0.0)
        u2 = jnp.where(row >= 2, pltpu.roll(u, 2, axis=0), 0.0)
        w = w_ref[...]
        y = w[2:3, :] * u + w[1:2, :] * u1 + w[0:1, :] * u2 + b_ref[...]
        y_ref[...] = (cb_ref[...] * y).astype(BF16)

    def part(p):
        return pl.BlockSpec((s, tc), lambda b, j: (b, p * nc + j))

    return _pcall(
        body, name="conv_fwd", out_shape=_sds((nb * s, d), BF16), grid=(nb, nc),
        in_specs=[part(0), part(1), part(2), pl.BlockSpec((3, tc), lambda b, j: (0, j)), pl.BlockSpec((1, tc), lambda b, j: (0, j))],
        out_specs=pl.BlockSpec((s, tc), lambda b, j: (b, j)), compiler_params=_cp(),
    )(z, z, z, conv_w, conv_b)


def _conv_bwd(z, dy, conv_w, conv_b, dz, *, nb, s, d, tc=256):
    nc = d // tc

    def body(cb_ref, cc_ref, cv_ref, dy_ref, w_ref, b_ref, dz_in, dz_ref, dw_ref, db_ref, stash):
        b_i, p = pl.program_id(1), pl.program_id(2)

        @pl.when(p == 0)
        def _():
            cc, cv = cc_ref[...], cv_ref[...]
            u = cc * cv
            n = u.shape[0]
            row = lax.broadcasted_iota(jnp.int32, u.shape, 0)
            u1 = jnp.where(row >= 1, pltpu.roll(u, 1, axis=0), 0.0)
            u2 = jnp.where(row >= 2, pltpu.roll(u, 2, axis=0), 0.0)
            w = w_ref[...]
            y = w[2:3, :] * u + w[1:2, :] * u1 + w[0:1, :] * u2 + b_ref[...]
            dyv = dy_ref[...]
            dconv = dyv * cb_ref[...]
            g1 = jnp.where(row < n - 1, pltpu.roll(dconv, n - 1, axis=0), 0.0)
            g2 = jnp.where(row < n - 2, pltpu.roll(dconv, n - 2, axis=0), 0.0)
            du = w[2:3, :] * dconv + w[1:2, :] * g1 + w[0:1, :] * g2
            stash[0] = (dyv * y).astype(BF16)
            stash[1] = (du * cv).astype(BF16)
            stash[2] = (du * cc).astype(BF16)
            dws = [jnp.sum(dconv * u2, axis=0, keepdims=True), jnp.sum(dconv * u1, axis=0, keepdims=True),
                   jnp.sum(dconv * u, axis=0, keepdims=True)]
            db = jnp.sum(dconv, axis=0, keepdims=True)

            @pl.when(b_i == 0)
            def _():
                for r in range(3):
                    dw_ref[r:r + 1, :] = dws[r]
                db_ref[...] = db

            @pl.when(b_i > 0)
            def _():
                for r in range(3):
                    dw_ref[r:r + 1, :] += dws[r]
                db_ref[...] += db

        dz_ref[...] = stash[p]

    return _pcall(
        body, name="conv_bwd",
        out_shape=(_sds(dz.shape, dz.dtype), _sds((3, d), F32), _sds((1, d), F32)), grid=(nc, nb, 3),
        in_specs=[
            pl.BlockSpec((s, tc), lambda j, b, p: (b, j)), pl.BlockSpec((s, tc), lambda j, b, p: (b, nc + j)),
            pl.BlockSpec((s, tc), lambda j, b, p: (b, 2 * nc + j)), pl.BlockSpec((s, tc), lambda j, b, p: (b, j)),
            pl.BlockSpec((3, tc), lambda j, b, p: (0, j)), pl.BlockSpec((1, tc), lambda j, b, p: (0, j)),
            HBM_SPEC,
        ],
        out_specs=(pl.BlockSpec((s, tc), lambda j, b, p: (b, p * nc + j)),
                   pl.BlockSpec((3, tc), lambda j, b, p: (0, j)), pl.BlockSpec((1, tc), lambda j, b, p: (0, j))),
        scratch_shapes=[pltpu.VMEM((3, s, tc), BF16)],
        input_output_aliases={6: 0},
        compiler_params=_cp(dimension_semantics=("arbitrary", "arbitrary", "arbitrary")),
    )(z, z, z, dy, conv_w, conv_b, dz)


def _head_rms(xv, g):
    r = lax.rsqrt(jnp.mean(xv * xv, axis=-1, keepdims=True) + EPS)
    return xv * r * g


def _head_rms_bwd(xv, g, dy):
    r = lax.rsqrt(jnp.mean(xv * xv, axis=-1, keepdims=True) + EPS)
    xh = xv * r
    gd = dy * g
    dx = r * (gd - xh * jnp.mean(gd * xh, axis=-1, keepdims=True))
    return dx, jnp.sum(dy * xh, axis=0, keepdims=True)


def _fox_prep(z, q_g, k_g, *, t, d, tm=256):
    nh = d // HD

    def body(z_ref, qg_ref, kg_ref, q_ref, k_ref, v_ref):
        qg, kg = qg_ref[...], kg_ref[...]
        for h in range(nh):
            c0 = h * HD
            q_ref[:, c0:c0 + HD] = _head_rms(z_ref[:, c0:c0 + HD], qg).astype(BF16)
            k_ref[:, c0:c0 + HD] = _head_rms(z_ref[:, d + c0:d + c0 + HD], kg).astype(BF16)
        v_ref[...] = z_ref[:, 2 * d:3 * d].astype(BF16)

    o = pl.BlockSpec((tm, d), lambda i: (i, 0))
    g = pl.BlockSpec((1, HD), lambda i: (0, 0))
    return _pcall(
        body, name="fox_prep", out_shape=(_sds((t, d), BF16),) * 3, grid=(t // tm,),
        in_specs=[pl.BlockSpec((tm, 3 * d), lambda i: (i, 1)), g, g], out_specs=(o, o, o), compiler_params=_cp(),
    )(z, q_g, k_g)


def _lane_cumsum(v, reverse=False):
    n = v.shape[1]
    lane = lax.broadcasted_iota(jnp.int32, v.shape, 1)
    sh = 1
    while sh < n:
        if reverse:
            v = v + jnp.where(lane < n - sh, pltpu.roll(v, n - sh, axis=1), 0.0)
        else:
            v = v + jnp.where(lane >= sh, pltpu.roll(v, sh, axis=1), 0.0)
        sh *= 2
    return v


def _fox_gates(z, f_bias_col, *, nb, s, nh, ff_block):
    def body(z_ref, b_ref, c_ref):
        ff = z_ref[...].T[0:nh, :] + b_ref[...]
        lf = jnp.minimum(ff, 0.0) - jnp.log(1.0 + jnp.exp(-jnp.abs(ff)))
        c_ref[0] = _lane_cumsum(lf) * SQRT_HD

    return _pcall(
        body, name="fox_gates", out_shape=_sds((nb, nh, s), F32), grid=(nb,),
        in_specs=[pl.BlockSpec((s, HD), lambda b: (b, ff_block)), pl.BlockSpec((nh, 1), lambda b: (0, 0))],
        out_specs=pl.BlockSpec((1, nh, s), lambda b: (b, 0, 0)), compiler_params=_cp(),
    )(z, f_bias_col)


def _fox_gates_bwd(z, f_bias_col, row_sums, col_sums, dz, *, nb, s, nh, ff_block):
    def body(z_ref, b_ref, rs_ref, cs_ref, dz_in, dz_ref, db_ref, dc_sc):
        b_i, h = pl.program_id(0), pl.program_id(1)
        dc_sc[pl.ds(h, 1), :] = (rs_ref[...] - cs_ref[...]).T[0:1, :]

        @pl.when(h == nh - 1)
        def _():
            ff = z_ref[...].T[0:nh, :] + b_ref[...]
            dlf = _lane_cumsum(dc_sc[...], reverse=True)
            dff = dlf * (1.0 / (1.0 + jnp.exp(ff)))
            pad = jnp.concatenate([dff, jnp.zeros((HD - nh, s), F32)], axis=0)
            dz_ref[...] = pad.T.astype(BF16)
            dbv = jnp.sum(dff, axis=1, keepdims=True)

            @pl.when(b_i == 0)
            def _():
                db_ref[...] = dbv

            @pl.when(b_i > 0)
            def _():
                db_ref[...] += dbv

    hs = pl.BlockSpec((s, HD), lambda b, h: (b, h))
    zs = pl.BlockSpec((s, HD), lambda b, h: (b, ff_block))
    return _pcall(
        body, name="fox_gates_bwd", out_shape=(_sds(dz.shape, dz.dtype), _sds((nh, 1), F32)), grid=(nb, nh),
        in_specs=[zs, pl.BlockSpec((nh, 1), lambda b, h: (0, 0)), hs, hs, HBM_SPEC],
        out_specs=(zs, pl.BlockSpec((nh, 1), lambda b, h: (0, 0))),
        scratch_shapes=[pltpu.VMEM((nh, s), F32)],
        input_output_aliases={4: 0}, compiler_params=_cp(dimension_semantics=("arbitrary", "arbitrary")),
    )(z, f_bias_col, row_sums, col_sums, dz)


def _fox_pairs(nq, key_major):
    if key_major:
        pairs = [(i, j) for j in range(nq) for i in range(j, nq)]
    else:
        pairs = [(i, j) for i in range(nq) for j in range(i + 1)]
    return jnp.array([p[0] for p in pairs], jnp.int32), jnp.array([p[1] for p in pairs], jnp.int32)


def _with_ones(x):
    return jnp.concatenate([x, jnp.ones_like(x)], axis=1)


def _fox_specs(nb, nh, nq, tq, tk, hp):
    qs = pl.BlockSpec((tq, hp * HD), lambda b, h, p, it, jt: (b * nq + it[p], h))
    ks = pl.BlockSpec((tk, hp * HD), lambda b, h, p, it, jt: (b * nq + jt[p], h))
    cs = pl.BlockSpec((hp, 1, tk), lambda b, h, p, it, jt: (b * (nh // hp) + h, 0, jt[p]))
    return qs, ks, cs


def _fox_raw(qv, kv, ckv, diag, tq, tk):
    sc = lax.dot_general(qv, kv, (((1,), (1,)), ((), ())), preferred_element_type=F32) - ckv
    if diag:
        rows = lax.broadcasted_iota(jnp.int32, (tq, tk), 0)
        cols = lax.broadcasted_iota(jnp.int32, (tq, tk), 1)
        sc = jnp.where(rows >= cols, sc, NEG)
    return sc


def _fox_probs(qv, kv, ckv, lse_col, diag, tq, tk):
    sc = _fox_raw(qv, kv, ckv, diag, tq, tk)
    return jnp.exp2(sc * EXP2_C - lse_col * LOG2E)


def _fox_call(body, *, name, ins, in_specs, out_shape, out_specs, scratch, grid, tables, comm):
    c_ins, c_in_specs, c_out_shape, c_scratch = _comm_lists(comm)
    n_in, n_out, n_scr, nc = len(ins), len(out_shape), len(scratch), len(c_ins)
    n_steps = grid[0] * grid[1] * grid[2]

    def wrapped(*refs):
        pre, rin, cin, rout, cout, scr, csem = _split_refs(refs, 2, n_in, nc, n_out, nc, n_scr)
        step = (pl.program_id(0) * grid[1] + pl.program_id(1)) * grid[2] + pl.program_id(2)
        _run_comm(comm, cin, cout, csem, step, n_steps, "start")
        body(*pre, *rin, *rout, *scr)
        _run_comm(comm, cin, cout, csem, step, n_steps, "end")

    sem = ("arbitrary",) * 3 if comm is not None else ("parallel", "parallel", "arbitrary")
    outs = _pcall(
        wrapped, name=name, out_shape=tuple(list(out_shape) + c_out_shape),
        grid_spec=pltpu.PrefetchScalarGridSpec(
            num_scalar_prefetch=2, grid=grid, in_specs=list(in_specs) + c_in_specs,
            out_specs=tuple(list(out_specs) + [HBM_SPEC] * nc), scratch_shapes=list(scratch) + c_scratch),
        compiler_params=_cp(dimension_semantics=sem),
    )(*tables, *ins, *c_ins)
    return list(outs[:n_out]), list(outs[n_out:])


def _fox_fwd(q, k, v, cks, *, nb, s, nh, tq=512, comm=None):
    tk = tq
    nq = s // tq
    t = nb * s
    hp = FOX_HP
    it, jt = _fox_pairs(nq, key_major=False)

    def body(it_ref, jt_ref, q_ref, k_ref, v_ref, c_ref, o_ref, lse_ref, m_sc, acc_sc):
        p_id = pl.program_id(2)
        i, j = it_ref[p_id], jt_ref[p_id]

        @pl.when(j == 0)
        def _():
            m_sc[...] = jnp.full_like(m_sc, NEG)
            acc_sc[...] = jnp.zeros_like(acc_sc)

        def step(diag):
            for hh in range(hp):
                cols = slice(hh * HD, (hh + 1) * HD)
                sc = _fox_raw(q_ref[:, cols], k_ref[:, cols], c_ref[hh], diag, tq, tk)
                m_old = m_sc[hh]
                m_new = jnp.maximum(m_old, jnp.max(sc, axis=-1, keepdims=True))
                alpha = jnp.exp2((m_old - m_new) * EXP2_C)
                p = jnp.exp2((sc - m_new) * EXP2_C).astype(BF16)
                acc = alpha * acc_sc[hh] + jnp.dot(p, _with_ones(v_ref[:, cols]), preferred_element_type=F32)
                if diag:
                    l = acc[:, HD:2 * HD]
                    o_ref[:, cols] = (acc[:, 0:HD] / l).astype(BF16)
                    lse_ref[:, cols] = m_new * (1.0 / SQRT_HD) + jnp.log(l)
                else:
                    acc_sc[hh] = acc
                    m_sc[hh] = m_new

        @pl.when(j < i)
        def _():
            step(False)

        @pl.when(j == i)
        def _():
            step(True)

    qs, ks, cs = _fox_specs(nb, nh, nq, tq, tk, hp)
    (o, lse), c_out = _fox_call(
        body, name="fox_fwd", ins=[q, k, v, cks], in_specs=[qs, ks, ks, cs],
        out_shape=[_sds((t, nh * HD), BF16), _sds((t, nh * HD), F32)], out_specs=[qs, qs],
        scratch=[pltpu.VMEM((hp, tq, 1), F32), pltpu.VMEM((hp, tq, 2 * HD), F32)],
        grid=(nb, nh // hp, int(it.shape[0])), tables=(it, jt), comm=comm)
    return o, lse, c_out


def _fox_bwd_dq(q, k, v, cks, do, o, lse, *, nb, s, nh, tq=512, comm=None):
    tk = tq
    nq = s // tq
    t = nb * s
    hp = FOX_HP
    it, jt = _fox_pairs(nq, key_major=False)

    def body(it_ref, jt_ref, q_ref, k_ref, v_ref, c_ref, do_ref, o_ref, lse_ref, dq_ref, rs_ref, acc_sc, dl_sc):
        p_id = pl.program_id(2)
        i, j = it_ref[p_id], jt_ref[p_id]

        @pl.when(j == 0)
        def _():
            acc_sc[...] = jnp.zeros_like(acc_sc)
            for hh in range(hp):
                cols = slice(hh * HD, (hh + 1) * HD)
                dl_sc[hh] = jnp.sum(do_ref[:, cols] * o_ref[:, cols].astype(F32), axis=-1, keepdims=True)

        def step(diag):
            for hh in range(hp):
                cols = slice(hh * HD, (hh + 1) * HD)
                kv = k_ref[:, cols]
                p = _fox_probs(q_ref[:, cols], kv, c_ref[hh], lse_ref[:, hh * HD:hh * HD + 1], diag, tq, tk)
                dp = lax.dot_general(do_ref[:, cols].astype(BF16), v_ref[:, cols], (((1,), (1,)), ((), ())),
                                     preferred_element_type=F32)
                ds = (p * (dp - dl_sc[hh])).astype(BF16)
                acc = acc_sc[hh] + jnp.dot(ds, _with_ones(kv), preferred_element_type=F32)
                if diag:
                    dq_ref[:, cols] = acc[:, 0:HD] * (1.0 / SQRT_HD)
                    rs_ref[:, cols] = acc[:, HD:2 * HD]
                else:
                    acc_sc[hh] = acc

        @pl.when(j < i)
        def _():
            step(False)

        @pl.when(j == i)
        def _():
            step(True)

    qs, ks, cs = _fox_specs(nb, nh, nq, tq, tk, hp)
    (dq, rs), c_out = _fox_call(
        body, name="fox_bwd_dq", ins=[q, k, v, cks, do, o, lse], in_specs=[qs, ks, ks, cs, qs, qs, qs],
        out_shape=[_sds((t, nh * HD), F32), _sds((t, nh * HD), F32)], out_specs=[qs, qs],
        scratch=[pltpu.VMEM((hp, tq, 2 * HD), F32), pltpu.VMEM((hp, tq, 1), F32)],
        grid=(nb, nh // hp, int(it.shape[0])), tables=(it, jt), comm=comm)
    return dq, rs, c_out


def _fox_bwd_dkv(q, k, v, cks, do, o, lse, *, nb, s, nh, tq=512, comm=None):
    tk = tq
    nq = s // tq
    t = nb * s
    hp = FOX_HP
    it, jt = _fox_pairs(nq, key_major=True)

    def body(it_ref, jt_ref, q_ref, k_ref, v_ref, c_ref, do_ref, o_ref, lse_ref, dk_ref, dv_ref, cs_ref, dk_sc, dv_sc):
        p_id = pl.program_id(2)
        i, j = it_ref[p_id], jt_ref[p_id]

        def step(diag):
            for hh in range(hp):
                cols = slice(hh * HD, (hh + 1) * HD)
                qv = q_ref[:, cols]
                p = _fox_probs(qv, k_ref[:, cols], c_ref[hh], lse_ref[:, hh * HD:hh * HD + 1], diag, tq, tk)
                dov = do_ref[:, cols]
                dob = dov.astype(BF16)
                dvp = lax.dot_general(p.astype(BF16), dob, (((0,), (0,)), ((), ())), preferred_element_type=F32)
                dp = lax.dot_general(dob, v_ref[:, cols], (((1,), (1,)), ((), ())), preferred_element_type=F32)
                delta = jnp.sum(dov * o_ref[:, cols].astype(F32), axis=-1, keepdims=True)
                ds = (p * (dp - delta)).astype(BF16)
                dkp = lax.dot_general(ds, _with_ones(qv), (((0,), (0,)), ((), ())), preferred_element_type=F32)
                if diag:
                    dk_sc[hh] = dkp
                    dv_sc[hh] = dvp
                else:
                    dk_sc[hh] += dkp
                    dv_sc[hh] += dvp

        @pl.when(i == j)
        def _():
            step(True)

        @pl.when(i > j)
        def _():
            step(False)

        @pl.when(i == nq - 1)
        def _():
            for hh in range(hp):
                cols = slice(hh * HD, (hh + 1) * HD)
                dk_ref[:, cols] = dk_sc[hh, :, 0:HD] * (1.0 / SQRT_HD)
                cs_ref[:, cols] = dk_sc[hh, :, HD:2 * HD]
                dv_ref[:, cols] = dv_sc[hh].astype(BF16)

    qs, ks, cs = _fox_specs(nb, nh, nq, tq, tk, hp)
    (dk, dv, csum), c_out = _fox_call(
        body, name="fox_bwd_dkv", ins=[q, k, v, cks, do, o, lse], in_specs=[qs, ks, ks, cs, qs, qs, qs],
        out_shape=[_sds((t, nh * HD), F32), _sds((t, nh * HD), BF16), _sds((t, nh * HD), F32)], out_specs=[ks, ks, ks],
        scratch=[pltpu.VMEM((hp, tk, 2 * HD), F32), pltpu.VMEM((hp, tk, HD), F32)],
        grid=(nb, nh // hp, int(it.shape[0])), tables=(it, jt), comm=comm)
    return dk, dv, csum, c_out


def _fox_post(z, dqn, dkn, dv, q_g, k_g, dz, *, t, d, tm=256):
    nh = d // HD

    def body(z_ref, dq_ref, dk_ref, dv_ref, qg_ref, kg_ref, dz_in, dz_ref, dqg_ref, dkg_ref):
        qg, kg = qg_ref[...], kg_ref[...]
        aq = jnp.zeros((1, HD), F32)
        ak = jnp.zeros((1, HD), F32)
        for h in range(nh):
            c0 = h * HD
            dx, sg = _head_rms_bwd(z_ref[:, c0:c0 + HD], qg, dq_ref[:, c0:c0 + HD])
            dz_ref[:, c0:c0 + HD] = dx.astype(BF16)
            aq = aq + sg
            dx, sg = _head_rms_bwd(z_ref[:, d + c0:d + c0 + HD], kg, dk_ref[:, c0:c0 + HD])
            dz_ref[:, d + c0:d + c0 + HD] = dx.astype(BF16)
            ak = ak + sg
        dz_ref[:, 2 * d:3 * d] = dv_ref[...]

        @pl.when(pl.program_id(0) == 0)
        def _():
            dqg_ref[...] = aq
            dkg_ref[...] = ak

        @pl.when(pl.program_id(0) > 0)
        def _():
            dqg_ref[...] += aq
            dkg_ref[...] += ak

    o = pl.BlockSpec((tm, d), lambda i: (i, 0))
    g = pl.BlockSpec((1, HD), lambda i: (0, 0))
    z3 = pl.BlockSpec((tm, 3 * d), lambda i: (i, 1))
    return _pcall(
        body, name="fox_post", out_shape=(_sds(dz.shape, dz.dtype), _sds((1, HD), F32), _sds((1, HD), F32)), grid=(t // tm,),
        in_specs=[z3, o, o, o, g, g, HBM_SPEC], out_specs=(z3, g, g),
        input_output_aliases={6: 0}, compiler_params=_cp(dimension_semantics=("arbitrary",)),
    )(z, dqn, dkn, dv, q_g, k_g, dz)


def _xa_prep_kv(kv, k_g, *, rows, d):
    xd = d // XH

    def body(kv_ref, g_ref, k_ref, v_ref):
        g = g_ref[...]
        for h in range(XH):
            c0 = h * xd
            k_ref[:, c0:c0 + xd] = _head_rms(kv_ref[:, c0:c0 + xd], g).astype(BF16)
        v_ref[...] = kv_ref[:, d:2 * d].astype(BF16)

    return _pcall(body, name="xa_prep_kv", out_shape=(_sds((rows, d), BF16),) * 2, compiler_params=_cp())(kv, k_g)


def _xa_fwd(z, kn, vb, q_g, *, nb, s, d, nm, q_block, tq=512):
    xd = d // XH
    nq = s // tq

    def body(z_ref, k_ref, v_ref, g_ref, y_ref):
        g = g_ref[...]
        for h in range(XH):
            c0 = h * xd
            qn = _head_rms(z_ref[:, c0:c0 + xd], g).astype(BF16)
            sc = lax.dot_general(qn, k_ref[:, c0:c0 + xd], (((1,), (1,)), ((), ())), preferred_element_type=F32)
            sc = sc / math.sqrt(xd)
            p = jnp.exp(sc - jnp.max(sc, axis=-1, keepdims=True))
            p = p / jnp.sum(p, axis=-1, keepdims=True)
            y_ref[:, c0:c0 + xd] = jnp.dot(p.astype(BF16), v_ref[:, c0:c0 + xd], preferred_element_type=F32).astype(BF16)

    kvs = pl.BlockSpec((nm, d), lambda b, i: (b, 0))
    return _pcall(
        body, name="xa_fwd", out_shape=_sds((nb * s, d), BF16), grid=(nb, nq),
        in_specs=[pl.BlockSpec((tq, d), lambda b, i: (b * nq + i, q_block)), kvs, kvs, pl.BlockSpec((1, xd), lambda b, i: (0, 0))],
        out_specs=pl.BlockSpec((tq, d), lambda b, i: (b * nq + i, 0)), compiler_params=_cp(),
    )(z, kn, vb, q_g)


def _xa_bwd(z, kn, vb, q_g, dy, dz, *, nb, s, d, nm, q_block, tq=512):
    xd = d // XH
    nq = s // tq

    def body(z_ref, k_ref, v_ref, g_ref, dy_ref, dz_in, dz_ref, dk_ref, dv_ref, dg_ref):
        b_i, i = pl.program_id(0), pl.program_id(1)
        g = g_ref[...]

        @pl.when(i == 0)
        def _():
            dk_ref[...] = jnp.zeros_like(dk_ref)
            dv_ref[...] = jnp.zeros_like(dv_ref)

        @pl.when((i == 0) & (b_i == 0))
        def _():
            dg_ref[...] = jnp.zeros_like(dg_ref)

        for h in range(XH):
            c0 = h * xd
            xq = z_ref[:, c0:c0 + xd]
            qn = _head_rms(xq, g).astype(BF16)
            kh, vh = k_ref[:, c0:c0 + xd], v_ref[:, c0:c0 + xd]
            sc = lax.dot_general(qn, kh, (((1,), (1,)), ((), ())), preferred_element_type=F32) / math.sqrt(xd)
            p = jnp.exp(sc - jnp.max(sc, axis=-1, keepdims=True))
            p = p / jnp.sum(p, axis=-1, keepdims=True)
            dob = dy_ref[:, c0:c0 + xd].astype(BF16)
            dv_ref[:, c0:c0 + xd] += lax.dot_general(p.astype(BF16), dob, (((0,), (0,)), ((), ())), preferred_element_type=F32)
            dp = lax.dot_general(dob, vh, (((1,), (1,)), ((), ())), preferred_element_type=F32)
            ds = p * (dp - jnp.sum(p * dp, axis=-1, keepdims=True)) / math.sqrt(xd)
            dsb = ds.astype(BF16)
            dqn = jnp.dot(dsb, kh, preferred_element_type=F32)
            dk_ref[:, c0:c0 + xd] += lax.dot_general(dsb, qn, (((0,), (0,)), ((), ())), preferred_element_type=F32)
            dx, sg = _head_rms_bwd(xq, g, dqn)
            dz_ref[:, c0:c0 + xd] = dx.astype(BF16)
            dg_ref[...] += sg

    kvs = pl.BlockSpec((nm, d), lambda b, i: (b, 0))
    qs = pl.BlockSpec((tq, d), lambda b, i: (b * nq + i, q_block))
    gs = pl.BlockSpec((1, xd), lambda b, i: (0, 0))
    return _pcall(
        body, name="xa_bwd",
        out_shape=(_sds(dz.shape, dz.dtype), _sds((nb * nm, d), F32), _sds((nb * nm, d), F32), _sds((1, xd), F32)), grid=(nb, nq),
        in_specs=[qs, kvs, kvs, gs, pl.BlockSpec((tq, d), lambda b, i: (b * nq + i, 0)), HBM_SPEC],
        out_specs=(qs, kvs, kvs, gs), input_output_aliases={5: 0},
        compiler_params=_cp(dimension_semantics=("arbitrary", "arbitrary")),
    )(z, kn, vb, q_g, dy, dz)


def _xa_post_kv(kv, dkn, dv, k_g, *, rows, d):
    xd = d // XH

    def body(kv_ref, dk_ref, dv_ref, g_ref, dkv_ref, dg_ref):
        g = g_ref[...]
        acc = jnp.zeros((1, xd), F32)
        for h in range(XH):
            c0 = h * xd
            dx, sg = _head_rms_bwd(kv_ref[:, c0:c0 + xd], g, dk_ref[:, c0:c0 + xd])
            dkv_ref[:, c0:c0 + xd] = dx.astype(BF16)
            acc = acc + sg
        dkv_ref[:, d:2 * d] = dv_ref[...].astype(BF16)
        dg_ref[...] = acc

    return _pcall(body, name="xa_post_kv", out_shape=(_sds((rows, 2 * d), BF16), _sds((1, xd), F32)), compiler_params=_cp())(
        kv, dkn, dv, k_g)


def _sigmoid(v):
    return 1.0 / (1.0 + jnp.exp(-v))


def _branches_fwd(z, ys, ws, *, t, d, gate_block, tm=512, tn=512):
    nj = d // tn

    def body(ya_ref, yb_ref, yc_ref, wa_ref, wb_ref, wc_ref, ga_ref, gb_ref, gc_ref, m_ref, p_ref):
        acc = None
        for q, (y_ref, w_ref, g_ref) in enumerate(((ya_ref, wa_ref, ga_ref), (yb_ref, wb_ref, gb_ref), (yc_ref, wc_ref, gc_ref))):
            pr = jnp.dot(y_ref[...], w_ref[...], preferred_element_type=F32)
            p_ref[q] = pr.astype(BF16)
            term = _sigmoid(g_ref[...]) * pr
            acc = term if acc is None else acc + term
        m_ref[...] = acc.astype(BF16)

    y_spec = pl.BlockSpec((tm, d), lambda i, j: (i, 0))
    w_spec = pl.BlockSpec((d, tn), lambda i, j: (0, j))

    def gate(q):
        return pl.BlockSpec((tm, tn), lambda i, j: (i, (gate_block + q) * nj + j))

    return _pcall(
        body, name="branches_fwd", out_shape=(_sds((t, d), BF16), _sds((3, t, d), BF16)), grid=(t // tm, nj),
        in_specs=[y_spec] * 3 + [w_spec] * 3 + [gate(0), gate(1), gate(2)],
        out_specs=(pl.BlockSpec((tm, tn), lambda i, j: (i, j)), pl.BlockSpec((3, tm, tn), lambda i, j: (0, i, j))),
        compiler_params=_cp(),
    )(*ys, *ws, z, z, z)


def _gates_bwd(z, dm, proj, dz, *, t, d, gate_block, tm=512):
    def body(g_ref, dm_ref, p_ref, dz_in, dz_ref, da_ref, db_ref, dc_ref):
        q = pl.program_id(1)
        sg = _sigmoid(g_ref[...])
        dmv = dm_ref[...]
        dz_ref[...] = (dmv * p_ref[0].astype(F32) * sg * (1.0 - sg)).astype(BF16)
        dp = (dmv * sg).astype(BF16)
        @pl.when(q == 0)
        def _():
            da_ref[...] = dp

        @pl.when(q == 1)
        def _():
            db_ref[...] = dp

        @pl.when(q == 2)
        def _():
            dc_ref[...] = dp

    gs = pl.BlockSpec((tm, d), lambda i, q: (i, gate_block + q))
    o = pl.BlockSpec((tm, d), lambda i, q: (i, 0))
    return _pcall(
        body, name="gates_bwd", out_shape=(_sds(dz.shape, dz.dtype),) + (_sds((t, d), BF16),) * 3, grid=(t // tm, 3),
        in_specs=[gs, o, pl.BlockSpec((1, tm, d), lambda i, q: (q, i, 0)), HBM_SPEC], out_specs=(gs, o, o, o),
        input_output_aliases={3: 0}, compiler_params=_cp(dimension_semantics=("arbitrary", "arbitrary")),
    )(z, dm, proj, dz)


def _ffn_in_fwd(h2, wfi_t, *, t, d, dff, tm=512, tn=1408):
    nj = dff // tn

    def body(h_ref, wg_ref, wu_ref, g_ref, u_ref, a_ref):
        hv = h_ref[...]
        dn = (((1,), (1,)), ((), ()))
        gv = lax.dot_general(hv, wg_ref[...], dn, preferred_element_type=F32)
        uv = lax.dot_general(hv, wu_ref[...], dn, preferred_element_type=F32)
        g_ref[...] = gv.astype(BF16)
        u_ref[...] = uv.astype(BF16)
        a_ref[...] = (gv * _sigmoid(gv) * uv).astype(BF16)

    o = pl.BlockSpec((tm, tn), lambda i, j: (i, j))
    return _pcall(
        body, name="ffn_in_fwd", out_shape=(_sds((t, dff), BF16),) * 3, grid=(t // tm, nj),
        in_specs=[pl.BlockSpec((tm, d), lambda i, j: (i, 0)), pl.BlockSpec((tn, d), lambda i, j: (j, 0)),
                  pl.BlockSpec((tn, d), lambda i, j: (nj + j, 0))],
        out_specs=(o, o, o), compiler_params=_cp(),
    )(h2, wfi_t, wfi_t)


def _ffn_out_bwd(dy, wfo, gate, up, *, t, d, dff, tm=512, tn=1408):
    def body(dy_ref, w_ref, g_ref, u_ref, dg_ref, du_ref):
        dav = lax.dot_general(dy_ref[...].astype(BF16), w_ref[...], (((1,), (1,)), ((), ())), preferred_element_type=F32)
        gv = g_ref[...].astype(F32)
        sg = _sigmoid(gv)
        dg_ref[...] = (dav * u_ref[...].astype(F32) * sg * (1.0 + gv * (1.0 - sg))).astype(BF16)
        du_ref[...] = (dav * gv * sg).astype(BF16)

    o = pl.BlockSpec((tm, tn), lambda i, j: (i, j))
    return _pcall(
        body, name="ffn_out_bwd", out_shape=(_sds((t, dff), BF16),) * 2, grid=(t // tm, dff // tn),
        in_specs=[pl.BlockSpec((tm, d), lambda i, j: (i, 0)), pl.BlockSpec((tn, d), lambda i, j: (j, 0)), o, o],
        out_specs=(o, o), compiler_params=_cp(),
    )(dy, wfo, gate, up)


def _loss_head(y, target, *, t, d, tm=512):
    def body(y_ref, t_ref, dy_ref, l_ref):
        e = y_ref[...] - t_ref[...]
        dy_ref[...] = e * (1.0 / d)

        @pl.when(pl.program_id(0) == 0)
        def _():
            l_ref[...] = jnp.zeros_like(l_ref)

        l_ref[...] += jnp.sum(jnp.sum(e * e, axis=-1, keepdims=True), axis=0, keepdims=True) * (0.5 / d)

    o = pl.BlockSpec((tm, d), lambda i: (i, 0))
    dy, l = _pcall(
        body, name="loss_head", out_shape=(_sds((t, d), F32), _sds((1, HD), F32)), grid=(t // tm,),
        in_specs=[o, o], out_specs=(o, pl.BlockSpec((1, HD), lambda i: (0, 0))),
        compiler_params=_cp(dimension_semantics=("arbitrary",)),
    )(y, target)
    return dy, l[0, 0]


def _adamw(w, g, m, v, *, name):
    if w.ndim == 3:
        rows, _, cols = w.shape
        tm = max(tt for tt in range(1, 65) if rows % tt == 0)
        block, imap, grid = (tm, 1, cols), (lambda i, j: (i, 0, 0)), (rows // tm, 1)
    else:
        rows, cols = w.shape
        tm, tn = rows, cols
        for cand in (256, 128, 64, 32, 16, 8):
            if rows > cand and rows % cand == 0:
                tm = cand
                break
        if tm == rows and rows > 256 and cols % 128 == 0:
            tn = 128
        block, imap, grid = (tm, tn), (lambda i, j: (i, j)), (rows // tm, cols // tn)
    c1 = 1.0 - ADAM_B1 ** ADAM_STEP
    c2 = 1.0 - ADAM_B2 ** ADAM_STEP

    def body(w_ref, g_ref, m_ref, v_ref, d_ref, nm_ref, nv_ref):
        gv = g_ref[...]
        mn = ADAM_B1 * m_ref[...] + (1.0 - ADAM_B1) * gv
        vn = ADAM_B2 * v_ref[...] + (1.0 - ADAM_B2) * (gv * gv)
        d_ref[...] = -ADAM_LR * ((mn / c1) / (jnp.sqrt(vn / c2) + ADAM_EPS) + ADAM_WD * w_ref[...])
        nm_ref[...] = mn
        nv_ref[...] = vn

    spec = pl.BlockSpec(block, imap)
    return _pcall(
        body, name=name, out_shape=(_sds(w.shape, F32),) * 3, grid=grid,
        in_specs=[spec] * 4, out_specs=(spec,) * 3, compiler_params=_cp(),
    )(w, g, m, v)


def _pad_rows(a, rows):
    return a if a.shape[0] == rows else jnp.pad(a, ((0, rows - a.shape[0]),) + ((0, 0),) * (a.ndim - 1))


class _Pack:
    def __init__(self, row_sizes):
        self.rows = list(row_sizes)
        self.padded = [-(-r // 16) * 16 for r in self.rows]
        self.offs = [sum(self.padded[:i]) for i in range(len(self.rows))]
        self.total = sum(self.padded)

    def stack(self, blocks):
        return jnp.concatenate([_pad_rows(b.astype(BF16), p) for b, p in zip(blocks, self.padded)], axis=0)

    def full(self, gathered, i):
        r = self.rows[i]
        return gathered[:, self.offs[i]:self.offs[i] + r, :].reshape(N_DEV * r, gathered.shape[2])

    def for_core(self, full_grads, core):
        parts = []
        for gfull, r, p in zip(full_grads, self.rows, self.padded):
            blk = gfull.reshape(N_DEV // 2, 2, r, gfull.shape[1])
            blk = lax.dynamic_index_in_dim(blk, core, axis=1, keepdims=False).astype(BF16)
            if p != r:
                blk = jnp.pad(blk, ((0, 0), (0, p - r), (0, 0)))
            parts.append(blk)
        return jnp.concatenate(parts, axis=1)

    def shard(self, g_pack, i):
        return g_pack[self.offs[i]:self.offs[i] + self.rows[i], :]


def kernel(x, mem, norm1_g, w_in, conv_w, conv_b, fox_f_bias, fox_q_g, fox_k_g, mem_norm_g, w_mem_kv, xa_q_g, xa_k_g, w_br_conv, w_br_fox, w_br_xa, w_o, norm2_g, w_ffn_in, w_ffn_out, loss_target, m_norm1_g, m_w_in, m_conv_w, m_conv_b, m_fox_f_bias, m_fox_q_g, m_fox_k_g, m_mem_norm_g, m_w_mem_kv, m_xa_q_g, m_xa_k_g, m_w_br_conv, m_w_br_fox, m_w_br_xa, m_w_o, m_norm2_g, m_w_ffn_in, m_w_ffn_out, v_norm1_g, v_w_in, v_conv_w, v_conv_b, v_fox_f_bias, v_fox_q_g, v_fox_k_g, v_mem_norm_g, v_w_mem_kv, v_xa_q_g, v_xa_k_g, v_w_br_conv, v_w_br_fox, v_w_br_xa, v_w_o, v_norm2_g, v_w_ffn_in, v_w_ffn_out):
    nb, s, d = x.shape
    t = nb * s
    nm = mem.shape[1]
    nh = d // HD
    dff = w_ffn_out.shape[1] * N_DEV
    n_in = w_in.shape[2] * N_DEV
    n_in_pad = -(-n_in // 1152) * 1152
    ff_block = (10 * d) // HD
    my_c = lax.axis_index("c")
    my_q = 2 * lax.axis_index("x") + lax.axis_index("y")

    pack_a = _Pack([w_in.shape[2]])
    pack_b = _Pack([w_mem_kv.shape[2], w_br_conv.shape[1], w_br_fox.shape[1], w_br_xa.shape[1], w_o.shape[1],
                    w_ffn_in.shape[2], w_ffn_out.shape[1]])
    gather_1 = _Pack([w_ffn_in.shape[2], w_ffn_out.shape[1]])
    gather_2 = _Pack([w_mem_kv.shape[2], w_br_conv.shape[1], w_br_fox.shape[1], w_br_xa.shape[1], w_o.shape[1]])
    stack_a = pack_a.stack([w_in[0].T])
    stack_1 = gather_1.stack([w_ffn_in[0].T, w_ffn_out[0]])
    stack_2 = gather_2.stack([w_mem_kv[0].T, w_br_conv[0], w_br_fox[0], w_br_xa[0], w_o[0]])
    cw_block = _pad_rows(conv_w[0], 8)
    rows_a = w_in.shape[2]
    my_x, my_y = lax.axis_index("x"), lax.axis_index("y")
    arrival = [(my_x, my_y, my_c), (my_x, my_y, 1 - my_c)]
    for cx, cy in [(1 - my_x, my_y), (my_x, 1 - my_y), (1 - my_x, 1 - my_y)]:
        arrival += [(cx, cy, my_c), (cx, cy, 1 - my_c)]
    order = jnp.stack([4 * ax + 2 * ay + ac for ax, ay, ac in arrival]).astype(jnp.int32)

    x2 = x.reshape(t, d)
    mem2 = mem.reshape(nb * nm, d)
    tgt2 = loss_target.reshape(t, d)
    h1 = _rms_fwd(x2, norm1_g, name="rms1_fwd")
    z, slabs, (gathered_1,) = _z_gathered(h1, stack_a, order, t=t, d=d, n_pad=n_in_pad, rows=rows_a,
                                          comm=_AllGather([stack_1]))
    z = _z_fix(h1, slabs, z, t=t, d=d, rows=rows_a)
    ff_rows = _pad_rows(slabs[N_DEV - 1, rows_a - nh:rows_a, :], HD)
    z = _z_tail(h1, ff_rows, z, t=t, d=d, col_block=ff_block)
    qn, kn, vb = _fox_prep(z, fox_q_g, fox_k_g, t=t, d=d)
    f_bias_col = fox_f_bias.reshape(nh, 1)
    ck_rows = _fox_gates(z, f_bias_col, nb=nb, s=s, nh=nh, ff_block=ff_block).reshape(nb * nh, 1, s)
    y_fox, lse, (gathered_2, cw_g) = _fox_fwd(qn, kn, vb, ck_rows, nb=nb, s=s, nh=nh, comm=_AllGather([stack_2, cw_block]))
    wfi_t, wfo = (gather_1.full(gathered_1, i) for i in range(2))
    wkv_t, wbc, wbf, wbx, wo = (gather_2.full(gathered_2, i) for i in range(5))
    conv_w_full = jnp.transpose(cw_g[:, 0:3, :], (1, 0, 2)).reshape(3, d)

    y_conv = _conv_fwd(z, conv_w_full, conv_b, nb=nb, s=s, d=d)
    mem_n = _rms_fwd(mem2, mem_norm_g, name="rms_mem_fwd")
    kv = _mm(mem_n, wkv_t, m=nb * nm, n=2 * d, k=d, tb=True, name="mm_kv")
    xkn, xvb = _xa_prep_kv(kv, xa_k_g, rows=nb * nm, d=d)
    y_xa = _xa_fwd(z, xkn, xvb, xa_q_g, nb=nb, s=s, d=d, nm=nm, q_block=6)
    merged, proj = _branches_fwd(z, (y_conv, y_fox, y_xa), (wbc, wbf, wbx), t=t, d=d, gate_block=7)
    x1 = _mm(merged, wo, m=t, n=d, k=d, res=x2, name="mm_x1")
    h2 = _rms_fwd(x1, norm2_g, name="rms2_fwd")
    ff_gate, ff_up, act = _ffn_in_fwd(h2, wfi_t, t=t, d=d, dff=dff)
    y = _mm(act, wfo, m=t, n=d, k=dff, res=x1, tk=dff, name="mm_y")
    dy, loss_local = _loss_head(y, tgt2, t=t, d=d)

    g_wfo = _mm(act, dy, m=dff, n=d, k=t, ta=True, tm=1408, tk=2048, out_dtype=BF16, name="mm_g_wfo")
    dgate, dup = _ffn_out_bwd(dy, wfo, ff_gate, ff_up, t=t, d=d, dff=dff)
    dh2 = _mm(dgate, wfi_t, m=t, n=d, k=dff, tk=dff, name="mm_dh2_g")
    dh2 = _mm(dup, wfi_t, m=t, n=d, k=dff, tk=dff, b_off=(1, 0), res=dh2, name="mm_dh2_u")
    g_wfi_g = _mm(dgate, h2, m=dff, n=d, k=t, ta=True, tm=1408, tk=2048, out_dtype=BF16, name="mm_g_wfi_g")
    g_wfi_u = _mm(dup, h2, m=dff, n=d, k=t, ta=True, tm=1408, tk=2048, out_dtype=BF16, name="mm_g_wfi_u")
    dx1, g_norm2 = _rms_bwd(dh2, x1, norm2_g, dy, name="rms2_bwd")
    dmerged = _mm(dx1, wo, m=t, n=d, k=d, tb=True, name="mm_dmerged")
    g_wo = _mm(merged, dx1, m=d, n=d, k=t, ta=True, tk=2048, out_dtype=BF16, name="mm_g_wo")

    dz = lax.empty((t, n_in_pad), BF16)
    dz, dpa, dpb, dpc = _gates_bwd(z, dmerged, proj, dz, t=t, d=d, gate_block=7)
    dy_conv = _mm(dpa, wbc, m=t, n=d, k=d, tb=True, name="mm_dy_conv")
    g_wbc = _mm(y_conv, dpa, m=d, n=d, k=t, ta=True, tk=4096, out_dtype=BF16, name="mm_g_wbc")
    dy_fox = _mm(dpb, wbf, m=t, n=d, k=d, tb=True, name="mm_dy_fox")
    g_wbf = _mm(y_fox, dpb, m=d, n=d, k=t, ta=True, tk=4096, out_dtype=BF16, name="mm_g_wbf")
    dy_xa = _mm(dpc, wbx, m=t, n=d, k=d, tb=True, name="mm_dy_xa")
    g_wbx = _mm(y_xa, dpc, m=d, n=d, k=t, ta=True, tk=4096, out_dtype=BF16, name="mm_g_wbx")

    dz, g_conv_w, g_conv_b = _conv_bwd(z, dy_conv, conv_w_full, conv_b, dz, nb=nb, s=s, d=d)
    dz, dxkn, dxv, g_xa_q = _xa_bwd(z, xkn, xvb, xa_q_g, dy_xa, dz, nb=nb, s=s, d=d, nm=nm, q_block=6)
    dkv, g_xa_k = _xa_post_kv(kv, dxkn, dxv, xa_k_g, rows=nb * nm, d=d)
    g_wkv_t = _mm(dkv, mem_n, m=2 * d, n=d, k=nb * nm, ta=True, out_dtype=BF16, name="mm_g_wkv")
    dmem_n = _mm(dkv, wkv_t, m=nb * nm, n=d, k=2 * d, tk=2 * d, name="mm_dmem")
    _, g_mem_norm = _rms_bwd(dmem_n, mem2, mem_norm_g, None, name="rms_mem_bwd")

    grads_b = [g_wkv_t, g_wbc, g_wbf, g_wbx, g_wo, jnp.concatenate([g_wfi_g, g_wfi_u], axis=0), g_wfo]
    keep_b, send_b = pack_b.for_core(grads_b, my_c), pack_b.for_core(grads_b, 1 - my_c)
    dqn, ds_rows, (from_sibling_b,) = _fox_bwd_dq(qn, kn, vb, ck_rows, dy_fox, y_fox, lse, nb=nb, s=s, nh=nh,
                                                  comm=_SiblingSwap(send_b))
    part_b = _add2(keep_b, from_sibling_b, name="rs_add_sibling_b")
    dkn, dvb, ds_cols, (from_chips_b,) = _fox_bwd_dkv(qn, kn, vb, ck_rows, dy_fox, y_fox, lse, nb=nb, s=s, nh=nh,
                                                     comm=_ChipExchange(part_b))
    g_pack_b = _add4(lax.dynamic_index_in_dim(part_b, my_q, axis=0, keepdims=False), from_chips_b, name="rs_add_chips_b")
    dz, g_fox_q, g_fox_k = _fox_post(z, dqn, dkn, dvb, fox_q_g, fox_k_g, dz, t=t, d=d)
    dz, g_f_bias = _fox_gates_bwd(z, f_bias_col, ds_rows, ds_cols, dz, nb=nb, s=s, nh=nh, ff_block=ff_block)

    keep_a, from_sibling_a = _g_win_slabs(dz, h1, my_c, t=t, d=d, n_slab=N_DEV, rows=rows_a, rp=slabs.shape[1])
    part_a = _add2(keep_a, from_sibling_a, name="rs_add_sibling_a")
    dh1, (from_chips_a,) = _dh1_from_slabs(dz, slabs, t=t, d=d, rows=rows_a, comm=_ChipExchange(part_a))
    g_pack_a = _add4(lax.dynamic_index_in_dim(part_a, my_q, axis=0, keepdims=False), from_chips_a, name="rs_add_chips_a",
                     rows_apart=True)
    dx, g_norm1 = _rms_bwd(dh1, x2, norm1_g, dx1, name="rms1_bwd")

    xd = d // XH
    tail = jnp.concatenate(
        [_pad_rows(g_f_bias.reshape(nh), HD).reshape(1, HD), g_fox_q, g_fox_k, g_xa_q, g_xa_k,
         jnp.zeros((1, d - 3 * HD - 2 * xd), F32)], axis=1)
    small = jnp.concatenate([g_norm1, g_norm2, g_conv_w, g_conv_b, g_mem_norm, tail], axis=0)
    small = _all_reduce_small(small)
    gs_conv_w = lax.dynamic_slice_in_dim(small[2:5], (2 * my_q + my_c) * (d // N_DEV), d // N_DEV, axis=1)

    grads = {
        "norm1_g": small[0:1], "w_in": g_pack_a, "conv_w": gs_conv_w, "conv_b": small[5:6],
        "fox_f_bias": small[7:8, 0:nh], "fox_q_g": small[7:8, HD:2 * HD], "fox_k_g": small[7:8, 2 * HD:3 * HD],
        "mem_norm_g": small[6:7], "w_mem_kv": pack_b.shard(g_pack_b, 0), "xa_q_g": small[7:8, 3 * HD:3 * HD + xd],
        "xa_k_g": small[7:8, 3 * HD + xd:3 * HD + 2 * xd], "w_br_conv": pack_b.shard(g_pack_b, 1),
        "w_br_fox": pack_b.shard(g_pack_b, 2), "w_br_xa": pack_b.shard(g_pack_b, 3), "w_o": pack_b.shard(g_pack_b, 4),
        "norm2_g": small[1:2], "w_ffn_in": pack_b.shard(g_pack_b, 5), "w_ffn_out": pack_b.shard(g_pack_b, 6),
    }
    transposed = {"w_in", "w_mem_kv", "w_ffn_in"}
    weights = {
        "norm1_g": (norm1_g, m_norm1_g, v_norm1_g), "w_in": (w_in, m_w_in, v_w_in), "conv_w": (conv_w, m_conv_w, v_conv_w),
        "conv_b": (conv_b, m_conv_b, v_conv_b), "fox_f_bias": (fox_f_bias, m_fox_f_bias, v_fox_f_bias),
        "fox_q_g": (fox_q_g, m_fox_q_g, v_fox_q_g), "fox_k_g": (fox_k_g, m_fox_k_g, v_fox_k_g),
        "mem_norm_g": (mem_norm_g, m_mem_norm_g, v_mem_norm_g), "w_mem_kv": (w_mem_kv, m_w_mem_kv, v_w_mem_kv),
        "xa_q_g": (xa_q_g, m_xa_q_g, v_xa_q_g), "xa_k_g": (xa_k_g, m_xa_k_g, v_xa_k_g),
        "w_br_conv": (w_br_conv, m_w_br_conv, v_w_br_conv), "w_br_fox": (w_br_fox, m_w_br_fox, v_w_br_fox),
        "w_br_xa": (w_br_xa, m_w_br_xa, v_w_br_xa), "w_o": (w_o, m_w_o, v_w_o), "norm2_g": (norm2_g, m_norm2_g, v_norm2_g),
        "w_ffn_in": (w_ffn_in, m_w_ffn_in, v_w_ffn_in), "w_ffn_out": (w_ffn_out, m_w_ffn_out, v_w_ffn_out),
    }
    out_g, out_d, out_m, out_v = [], [], [], []
    for name, (w, m, v) in weights.items():
        shape = w.shape
        if name == "w_in":
            w3, m3, v3 = (a.transpose(2, 0, 1) for a in (w, m, v))
            dl, mn, vn = _adamw(w3, g_pack_a, m3, v3, name="adamw_" + name)
            out_g.append(g_pack_a[:shape[2]].transpose(1, 2, 0))
            out_d.append(dl.transpose(1, 2, 0))
            out_m.append(mn.transpose(1, 2, 0))
            out_v.append(vn.transpose(1, 2, 0))
            continue
        tr = name in transposed

        def as2d(a):
            a = a.reshape(shape[-2], shape[-1])
            return a.T if tr else a

        def back(a):
            return (a.T if tr else a).reshape(shape)

        w2 = as2d(w)
        g2 = grads[name].reshape(w2.shape)
        dl, mn, vn = _adamw(w2, g2, as2d(m), as2d(v), name="adamw_" + name)
        out_g.append(back(g2))
        out_d.append(back(dl))
        out_m.append(back(mn))
        out_v.append(back(vn))

    loss = lax.psum(loss_local, ("x", "y", "c"))
    return (loss, dx.reshape(nb, s, d), *out_g, *out_d, *out_m, *out_v)
```

```python
import math

import jax
import jax.numpy as jnp
from jax import lax
from jax.experimental import pallas as pl
from jax.experimental.pallas import tpu as pltpu

F32, BF16 = jnp.float32, jnp.bfloat16
EPS = 1e-6
HD = 128
XH = 4
N_DEV = 8
MESH = pl.DeviceIdType.MESH
VMEM_LIMIT = 52 * 1024 * 1024
NEG = -1e30
SQRT_HD = math.sqrt(HD)
EXP2_C = math.log2(math.e) / SQRT_HD
LOG2E = math.log2(math.e)
FOX_HP = 4

ADAM_LR, ADAM_B1, ADAM_B2, ADAM_EPS, ADAM_WD, ADAM_STEP = 0.001, 0.9, 0.999, 1e-08, 0.01, 10


def _cp(**kw):
    return pltpu.CompilerParams(vmem_limit_bytes=VMEM_LIMIT, **kw)


def _pcall(body, **kw):
    return pl.pallas_call(body, **kw)


def _sds(shape, dtype):
    return jax.ShapeDtypeStruct(shape, dtype)


HBM_SPEC = pl.BlockSpec(memory_space=pl.ANY)


def _mesh_pos():
    return lax.axis_index("x"), lax.axis_index("y"), lax.axis_index("c")


class _AllGather:
    def __init__(self, blocks):
        self.blocks = list(blocks)
        n = len(self.blocks)
        self.out_shape = [_sds((N_DEV,) + b.shape, b.dtype) for b in self.blocks]
        self.scratch = [pltpu.SemaphoreType.DMA((n, 7)), pltpu.SemaphoreType.DMA((n, 7)), pltpu.SemaphoreType.DMA((n,))]

    def bind(self, ins, outs, sems):
        n = len(ins)
        send_sems, recv_sems, local_sems = sems
        x, y, c = _mesh_pos()
        me, sibling = (x, y, c), (x, y, 1 - c)
        chips = [(1 - x, y), (x, 1 - y), (1 - x, 1 - y)]

        def slab(t, dev):
            return outs[t].at[4 * dev[0] + 2 * dev[1] + dev[2]]

        def copy(t, k, block, to, src=None):
            dst = slab(t, block)
            return pltpu.make_async_remote_copy(
                src_ref=dst if src is None else src, dst_ref=dst, send_sem=send_sems.at[t, k], recv_sem=recv_sems.at[t, k],
                device_id=to, device_id_type=MESH)

        mine = [pltpu.make_async_copy(ins[t], slab(t, me), local_sems.at[t]) for t in range(n)]
        first = []
        for t in range(n):
            first.append(copy(t, 0, me, sibling, src=ins[t]))
            first += [copy(t, 1 + j, me, (*chip, c), src=ins[t]) for j, chip in enumerate(chips)]
        passed = [copy(t, 4 + j, (*chip, c), sibling) for j, chip in enumerate(chips) for t in range(n)]

        def start():
            for cp in mine + first:
                cp.start()

        def mid():
            for j, chip in enumerate(chips):
                for t in range(n):
                    copy(t, 1 + j, (*chip, c), me).wait_recv()
                    passed[j * n + t].start()

        def finish():
            for t in range(n):
                copy(t, 0, sibling, me).wait_recv()
                for j, chip in enumerate(chips):
                    copy(t, 4 + j, (*chip, 1 - c), me).wait_recv()
            for cp in first + passed:
                cp.wait_send()
            for cp in mine:
                cp.wait()

        return start, mid, finish


class _SiblingSwap:
    def __init__(self, send):
        self.blocks = [send]
        self.out_shape = [_sds(send.shape, send.dtype)]
        self.scratch = [pltpu.SemaphoreType.DMA, pltpu.SemaphoreType.DMA]

    def bind(self, ins, outs, sems):
        x, y, c = _mesh_pos()
        cp = pltpu.make_async_remote_copy(src_ref=ins[0], dst_ref=outs[0], send_sem=sems[0], recv_sem=sems[1],
                                          device_id=(x, y, 1 - c), device_id_type=MESH)
        return cp.start, (lambda: None), cp.wait


class _ChipExchange:
    def __init__(self, part):
        self.blocks = [part]
        self.out_shape = [_sds((3,) + part.shape[1:], part.dtype)]
        self.scratch = [pltpu.SemaphoreType.DMA((3,)), pltpu.SemaphoreType.DMA((3,))]

    def bind(self, ins, outs, sems):
        x, y, c = _mesh_pos()
        chips = [(1 - x, y), (x, 1 - y), (1 - x, 1 - y)]
        cps = [
            pltpu.make_async_remote_copy(src_ref=ins[0].at[2 * cx + cy], dst_ref=outs[0].at[j], send_sem=sems[0].at[j],
                                         recv_sem=sems[1].at[j], device_id=(cx, cy, c), device_id_type=MESH)
            for j, (cx, cy) in enumerate(chips)
        ]

        def start():
            for cp in cps:
                cp.start()

        def finish():
            for cp in cps:
                cp.wait()

        return start, (lambda: None), finish


def _comm_lists(comm):
    if comm is None:
        return [], [], [], []
    n = len(comm.blocks)
    return list(comm.blocks), [HBM_SPEC] * n, list(comm.out_shape), list(comm.scratch)


def _split_refs(refs, n_pre, n_in, n_cin, n_out, n_cout, n_scr):
    cuts = [n_pre, n_in, n_cin, n_out, n_cout, n_scr]
    out, at = [], 0
    for c in cuts:
        out.append(refs[at:at + c])
        at += c
    out.append(refs[at:])
    return out


def _run_comm(comm, cin, cout, csem, step, n_steps, when="start"):
    if comm is None:
        return
    start, mid, finish = comm.bind(cin, cout, csem)
    if when == "start":
        pl.when(step == 0)(start)
    else:
        pl.when(step == max(n_steps - 2, 0))(mid)
        pl.when(step == n_steps - 1)(finish)


def _comm_only(comm, name):
    ins, in_specs, out_shape, scratch = _comm_lists(comm)
    n = len(ins)

    def body(*refs):
        start, mid, finish = comm.bind(refs[:n], refs[n:2 * n], refs[2 * n:])
        start()
        mid()
        finish()

    outs = _pcall(body, name=name, out_shape=tuple(out_shape), in_specs=in_specs, out_specs=tuple([HBM_SPEC] * n),
                  scratch_shapes=scratch, compiler_params=_cp())(*ins)
    return list(outs)


def _all_reduce_small(v):
    rows, cols = v.shape

    def body(v_ref, o_ref, slots, send_sems, recv_sems):
        x, y, c = _mesh_pos()
        me = 4 * x + 2 * y + c
        slots[me] = v_ref[...]
        cps = []
        for k in range(1, N_DEV):
            px, py, pc = x ^ (k >> 2), y ^ ((k >> 1) & 1), c ^ (k & 1)
            cps.append(pltpu.make_async_remote_copy(
                src_ref=v_ref, dst_ref=slots.at[me], send_sem=send_sems.at[k - 1], recv_sem=recv_sems.at[k - 1],
                device_id=(px, py, pc), device_id_type=MESH))
        for cp in cps:
            cp.start()
        for cp in cps:
            cp.wait()
        acc = slots[0]
        for k in range(1, N_DEV):
            acc = acc + slots[k]
        o_ref[...] = acc

    return _pcall(
        body, name="all_reduce_small", out_shape=_sds((rows, cols), F32),
        in_specs=[pl.BlockSpec(memory_space=pltpu.VMEM)], out_specs=pl.BlockSpec(memory_space=pltpu.VMEM),
        scratch_shapes=[pltpu.VMEM((N_DEV, rows, cols), F32), pltpu.SemaphoreType.DMA((7,)), pltpu.SemaphoreType.DMA((7,))],
        compiler_params=_cp(),
    )(v)


def _row_tile(rows, cap=1040):
    return max(tt for tt in range(16, cap + 1, 16) if rows % tt == 0)


def _add2(a, b, *, name):
    nq, rows, cols = a.shape
    tm = _row_tile(rows)

    def body(a_ref, b_ref, o_ref):
        o_ref[...] = (a_ref[...].astype(F32) + b_ref[...].astype(F32)).astype(BF16)

    spec = pl.BlockSpec((1, tm, cols), lambda q, i: (q, i, 0))
    return _pcall(body, name=name, out_shape=_sds(a.shape, BF16), grid=(nq, rows // tm),
                  in_specs=[spec, spec], out_specs=spec, compiler_params=_cp())(a, b)


def _add4(own, recv, *, name, rows_apart=False):
    rows, cols = own.shape
    tm = _row_tile(rows, 256 if rows_apart else 1040)

    def body(o_ref, r_ref, g_ref):
        tot = ((o_ref[...].astype(F32) + r_ref[0].astype(F32)) + r_ref[1].astype(F32)) + r_ref[2].astype(F32)
        if rows_apart:
            g_ref[:, 0, :] = tot
        else:
            g_ref[...] = tot

    if rows_apart:
        out_shape, out_spec = _sds((rows, 1, cols), F32), pl.BlockSpec((tm, 1, cols), lambda i: (i, 0, 0))
    else:
        out_shape, out_spec = _sds((rows, cols), F32), pl.BlockSpec((tm, cols), lambda i: (i, 0))
    return _pcall(
        body, name=name, out_shape=out_shape, grid=(rows // tm,),
        in_specs=[pl.BlockSpec((tm, cols), lambda i: (i, 0)), pl.BlockSpec((3, tm, cols), lambda i: (0, i, 0))],
        out_specs=out_spec, compiler_params=_cp(),
    )(own, recv)


def _mm(a, b, *, m, n, k, name, ta=False, tb=False, b_off=(0, 0), res=None, out_dtype=F32, tm=512, tn=1024, tk=1024,
        comm=None):
    tm, tn, tk = min(tm, m), min(tn, n), min(tk, k)
    assert m % tm == 0 and n % tn == 0 and k % tk == 0, (name, m, n, k, tm, tn, tk)
    nk = k // tk
    grid = (m // tm, n // tn, nk)
    bo0, bo1 = b_off
    if ta:
        a_spec = pl.BlockSpec((tk, tm), lambda i, j, kk: (kk, i))
    else:
        a_spec = pl.BlockSpec((tm, tk), lambda i, j, kk: (i, kk))
    if tb:
        b_spec = pl.BlockSpec((tn, tk), lambda i, j, kk: (j + bo0, kk + bo1))
    else:
        b_spec = pl.BlockSpec((tk, tn), lambda i, j, kk: (kk + bo0, j + bo1))
    o_spec = pl.BlockSpec((tm, tn), lambda i, j, kk: (i, j))
    dn = (((0 if ta else 1,), (1 if tb else 0,)), ((), ()))
    has_res = res is not None
    c_ins, c_in_specs, c_out_shape, c_scratch = _comm_lists(comm)
    n_in = 3 if has_res else 2
    n_scr = 1 if nk > 1 else 0

    def body(*refs):
        _, ins, cin, outs, cout, scr, csem = _split_refs(refs, 0, n_in, len(c_ins), 1, len(c_ins), n_scr)
        a_ref, b_ref = ins[0], ins[1]
        r_ref = ins[2] if has_res else None
        o_ref = outs[0]
        step = (pl.program_id(0) * grid[1] + pl.program_id(1)) * nk + pl.program_id(2)
        _run_comm(comm, cin, cout, csem, step, grid[0] * grid[1] * nk, "start")
        p = lax.dot_general(a_ref[...].astype(BF16), b_ref[...].astype(BF16), dn, preferred_element_type=F32)

        def finish(acc):
            if has_res:
                acc = acc + r_ref[...]
            o_ref[...] = acc.astype(out_dtype)

        if nk == 1:
            finish(p)
        else:
            acc_ref = scr[0]
            kk = pl.program_id(2)

            @pl.when(kk == 0)
            def _():
                acc_ref[...] = p

            @pl.when(kk > 0)
            def _():
                acc_ref[...] += p

            @pl.when(kk == nk - 1)
            def _():
                finish(acc_ref[...])

        _run_comm(comm, cin, cout, csem, step, grid[0] * grid[1] * nk, "end")

    ins = [a, b] + ([res] if has_res else [])
    in_specs = [a_spec, b_spec] + ([o_spec] if has_res else [])
    sem = ("arbitrary",) * 3 if comm is not None else ("parallel", "parallel", "arbitrary")
    outs = _pcall(
        body, name=name, out_shape=tuple([_sds((m, n), out_dtype)] + c_out_shape), grid=grid,
        in_specs=in_specs + c_in_specs, out_specs=tuple([o_spec] + [HBM_SPEC] * len(c_ins)),
        scratch_shapes=([pltpu.VMEM((tm, tn), F32)] if nk > 1 else []) + c_scratch,
        compiler_params=_cp(dimension_semantics=sem),
    )(*(ins + c_ins))
    return outs[0] if comm is None else (outs[0], list(outs[1:]))


SLAB_WIN = 1280
SLAB_TAIL = 48


def _z_gathered(h1, own_slab, order, *, t, d, n_pad, rows, comm=None, tm=1024):
    rp = own_slab.shape[0]
    n_slab = N_DEV
    tm = min(tm, t)
    ni = t // tm
    sh = rows - SLAB_WIN
    assert sh >= 0 and (n_slab - 1) * sh <= 128
    c_ins, c_in_specs, c_out_shape, c_scratch = _comm_lists(comm)
    n_steps = n_slab * ni

    def body(*refs):
        pre, ins, cin, outs, cout, scr, csem = _split_refs(refs, 1, 2, len(c_ins), 2, len(c_ins), 5)
        order_ref = pre[0]
        h_ref, own_ref = ins
        o_ref, slabs_ref = outs
        slab_vm, send_sems, recv_sems, local_sem, load_sem = scr
        p, i = pl.program_id(0), pl.program_id(1)
        step = p * ni + i
        x, y, c = _mesh_pos()
        me, sibling = (x, y, c), (x, y, 1 - c)
        chips = [(1 - x, y), (x, 1 - y), (1 - x, 1 - y)]

        def slab(dev):
            return slabs_ref.at[4 * dev[0] + 2 * dev[1] + dev[2]]

        def copy(k, block, to, src=None):
            dst = slab(block)
            return pltpu.make_async_remote_copy(
                src_ref=dst if src is None else src, dst_ref=dst, send_sem=send_sems.at[k], recv_sem=recv_sems.at[k],
                device_id=to, device_id_type=MESH)

        mine = pltpu.make_async_copy(own_ref, slab(me), local_sem)
        first = [copy(0, me, sibling, src=own_ref)] + [copy(1 + j, me, (*chip, c), src=own_ref) for j, chip in enumerate(chips)]
        passed = [copy(4 + j, (*chip, c), sibling) for j, chip in enumerate(chips)]

        @pl.when(step == 0)
        def _():
            mine.start()
            for cp in first:
                cp.start()

        _run_comm(comm, cin, cout, csem, step, n_steps, "start")

        def load(src):
            cp = pltpu.make_async_copy(src, slab_vm, load_sem)
            cp.start()
            cp.wait()

        @pl.when(i == 0)
        def _():
            @pl.when(p == 0)
            def _():
                load(own_ref)

            @pl.when(p == 1)
            def _():
                copy(0, sibling, me).wait_recv()
                load(slab(sibling))

            for j, chip in enumerate(chips):
                @pl.when(p == 2 + 2 * j)
                def _():
                    copy(1 + j, (*chip, c), me).wait_recv()
                    passed[j].start()
                    load(slab((*chip, c)))

                @pl.when(p == 3 + 2 * j)
                def _():
                    copy(4 + j, (*chip, 1 - c), me).wait_recv()
                    load(slab((*chip, 1 - c)))

        k = order_ref[p]
        y_k = lax.dot_general(h_ref[...], slab_vm[...], (((1,), (1,)), ((), ())), preferred_element_type=F32)
        o_ref[...] = pltpu.roll(y_k[:, 0:SLAB_WIN], k * sh, axis=1)

        @pl.when(step == n_steps - 1)
        def _():
            for cp in first + passed:
                cp.wait_send()
            mine.wait()

        _run_comm(comm, cin, cout, csem, step, n_steps, "end")

    outs = _pcall(
        body, name="mm_z", out_shape=tuple([_sds((t, n_pad), F32), _sds((n_slab, rp, d), BF16)] + c_out_shape),
        grid_spec=pltpu.PrefetchScalarGridSpec(
            num_scalar_prefetch=1, grid=(n_slab, ni),
            in_specs=[pl.BlockSpec((tm, d), lambda p, i, order: (i, 0)), HBM_SPEC] + c_in_specs,
            out_specs=tuple([pl.BlockSpec((tm, SLAB_WIN), lambda p, i, order: (i, order[p])), HBM_SPEC] + [HBM_SPEC] * len(c_ins)),
            scratch_shapes=[pltpu.VMEM((rp, d), BF16), pltpu.SemaphoreType.DMA((7,)), pltpu.SemaphoreType.DMA((7,)),
                            pltpu.SemaphoreType.DMA, pltpu.SemaphoreType.DMA] + c_scratch),
        compiler_params=_cp(dimension_semantics=("arbitrary", "arbitrary")),
    )(order, h1, own_slab, *c_ins)
    return outs[0], outs[1], list(outs[2:])


def _z_fix(h1, slabs, z, *, t, d, rows, tm=4096):
    n_slab, rp, _ = slabs.shape
    tm = min(tm, t)
    sh = rows - SLAB_WIN
    tail_lo = rp - SLAB_TAIL
    assert rows - (n_slab - 1) * sh >= tail_lo and rp % SLAB_TAIL == 0

    def body(h_ref, t_ref, z_ref, o_ref):
        k = pl.program_id(0) + 1
        p = lax.dot_general(h_ref[...], t_ref[0], (((1,), (1,)), ((), ())), preferred_element_type=F32)
        pp = jnp.concatenate([p, jnp.zeros((tm, 128 - SLAB_TAIL), F32)], axis=1)
        pr = pltpu.roll(pp, 128 - (rows - tail_lo) + k * sh, axis=1)
        lane = lax.broadcasted_iota(jnp.int32, (tm, 128), 1)
        o_ref[...] = jnp.where(lane < k * sh, pr, z_ref[...])

    blk = pl.BlockSpec((tm, 128), lambda b, i: (i, (SLAB_WIN // 128) * (b + 1)))
    return _pcall(
        body, name="mm_z_fix", out_shape=_sds(z.shape, z.dtype), grid=(n_slab - 1, t // tm),
        in_specs=[pl.BlockSpec((tm, d), lambda b, i: (i, 0)),
                  pl.BlockSpec((1, SLAB_TAIL, d), lambda b, i: (b, tail_lo // SLAB_TAIL, 0)), blk],
        out_specs=blk, input_output_aliases={2: 0}, compiler_params=_cp(),
    )(h1, slabs, z)


def _z_tail(h1, w_rows, z, *, t, d, col_block, tm=1024):
    def body(h_ref, w_ref, z_in, o_ref):
        o_ref[...] = lax.dot_general(h_ref[...], w_ref[...], (((1,), (1,)), ((), ())), preferred_element_type=F32)

    return _pcall(
        body, name="mm_z_tail", out_shape=_sds(z.shape, z.dtype), grid=(t // tm,),
        in_specs=[pl.BlockSpec((tm, d), lambda i: (i, 0)), pl.BlockSpec((HD, d), lambda i: (0, 0)), HBM_SPEC],
        out_specs=pl.BlockSpec((tm, HD), lambda i: (i, col_block)), input_output_aliases={2: 0}, compiler_params=_cp(),
    )(h1, w_rows, z)


def _slab_window(a1_ref, a2_ref, k, sh):
    w = jnp.concatenate([a1_ref[...], a2_ref[...]], axis=1)
    width = SLAB_WIN + 128
    return pltpu.roll(w, width - k * sh, axis=1)


def _g_win_slabs(dz, h1, my_c, *, t, d, n_slab, rows, rp, tk=2048):
    sh = rows - SLAB_WIN
    tk = min(tk, t)
    nk = t // tk
    width = SLAB_WIN + 128
    half = n_slab // 2

    def slab_of(p, c):
        return 2 * (p % half) + jnp.where(p < half, 1 - c, c)

    def body(c_ref, a1_ref, a2_ref, h_ref, keep_ref, land_ref, acc, stage, send_sems, recv_sems):
        p, kk = pl.program_id(0), pl.program_id(1)
        x, y, c = _mesh_pos()
        ws = _slab_window(a1_ref, a2_ref, slab_of(p, c), sh)
        prod = lax.dot_general(ws, h_ref[...], (((0,), (0,)), ((), ())), preferred_element_type=F32)

        def push(q):
            return pltpu.make_async_remote_copy(
                src_ref=stage.at[q % 2], dst_ref=land_ref.at[q], send_sem=send_sems.at[q], recv_sem=recv_sems.at[q],
                device_id=(x, y, 1 - c), device_id_type=MESH)

        @pl.when(kk == 0)
        def _():
            acc[...] = prod

        @pl.when(kk > 0)
        def _():
            acc[...] += prod

        @pl.when(kk == nk - 1)
        def _():
            res = acc[0:rp, :].astype(BF16)
            for q in range(half):
                @pl.when(p == q)
                def _():
                    if q >= 2:
                        push(q - 2).wait_send()
                    stage[q % 2] = res
                    push(q).start()

            @pl.when(p >= half)
            def _():
                keep_ref[0] = res

        @pl.when((p == n_slab - 1) & (kk == nk - 1))
        def _():
            for q in range(max(half - 2, 0), half):
                push(q).wait_send()
            for q in range(half):
                push(q).wait_recv()

    outs = _pcall(
        body, name="mm_g_win", out_shape=(_sds((half, rp, d), BF16), _sds((half, rp, d), BF16)),
        grid_spec=pltpu.PrefetchScalarGridSpec(
            num_scalar_prefetch=1, grid=(n_slab, nk),
            in_specs=[pl.BlockSpec((tk, SLAB_WIN), lambda p, kk, c: (kk, slab_of(p, c[0]))),
                      pl.BlockSpec((tk, 128), lambda p, kk, c: (kk, (SLAB_WIN // 128) * (slab_of(p, c[0]) + 1))),
                      pl.BlockSpec((tk, d), lambda p, kk, c: (kk, 0))],
            out_specs=(pl.BlockSpec((1, rp, d), lambda p, kk, c: (jnp.maximum(p - half, 0), 0, 0)), HBM_SPEC),
            scratch_shapes=[pltpu.VMEM((width, d), F32), pltpu.VMEM((2, rp, d), BF16),
                            pltpu.SemaphoreType.DMA((half,)), pltpu.SemaphoreType.DMA((half,))]),
        compiler_params=_cp(dimension_semantics=("arbitrary", "arbitrary")),
    )(my_c.reshape(1).astype(jnp.int32), dz, dz, h1)
    return outs[0], outs[1]


def _dh1_from_slabs(dz, slabs, *, t, d, rows, comm=None, tm=1024):
    n_slab, rp, _ = slabs.shape
    sh = rows - SLAB_WIN
    width = SLAB_WIN + 128
    c_ins, c_in_specs, c_out_shape, c_scratch = _comm_lists(comm)
    grid = (t // tm, n_slab)

    def body(*refs):
        _, ins, cin, outs, cout, scr, csem = _split_refs(refs, 0, 3, len(c_ins), 1, len(c_ins), 1)
        a1_ref, a2_ref, s_ref = ins
        o_ref, acc = outs[0], scr[0]
        k = pl.program_id(1)
        step = pl.program_id(0) * n_slab + k
        _run_comm(comm, cin, cout, csem, step, grid[0] * n_slab, "start")
        ws = _slab_window(a1_ref, a2_ref, k, sh)
        s_end = jnp.concatenate([s_ref[0, SLAB_WIN:rp, :], jnp.zeros((width - rp, d), BF16)], axis=0)
        p = (jnp.dot(ws[:, 0:SLAB_WIN], s_ref[0, 0:SLAB_WIN, :], preferred_element_type=F32)
             + jnp.dot(ws[:, SLAB_WIN:width], s_end, preferred_element_type=F32))

        @pl.when(k == 0)
        def _():
            acc[...] = p

        @pl.when(k > 0)
        def _():
            acc[...] += p

        @pl.when(k == n_slab - 1)
        def _():
            o_ref[...] = acc[...]

        _run_comm(comm, cin, cout, csem, step, grid[0] * n_slab, "end")

    outs = _pcall(
        body, name="mm_dh1", out_shape=tuple([_sds((t, d), F32)] + c_out_shape), grid=grid,
        in_specs=[pl.BlockSpec((tm, SLAB_WIN), lambda i, k: (i, k)),
                  pl.BlockSpec((tm, 128), lambda i, k: (i, (SLAB_WIN // 128) * (k + 1))),
                  pl.BlockSpec((1, rp, d), lambda i, k: (k, 0, 0))] + c_in_specs,
        out_specs=tuple([pl.BlockSpec((tm, d), lambda i, k: (i, 0))] + [HBM_SPEC] * len(c_ins)),
        scratch_shapes=[pltpu.VMEM((tm, d), F32)] + c_scratch,
        compiler_params=_cp(dimension_semantics=("arbitrary", "arbitrary")),
    )(dz, dz, slabs, *c_ins)
    return outs[0], list(outs[1:])


def _rms_fwd(x, g, *, name, tm=512):
    t, d = x.shape
    tm = min(tm, t)

    def body(x_ref, g_ref, h_ref):
        xv = x_ref[...]
        r = lax.rsqrt(jnp.mean(xv * xv, axis=-1, keepdims=True) + EPS)
        h_ref[...] = (xv * r * g_ref[...]).astype(BF16)

    return _pcall(
        body, name=name, out_shape=_sds((t, d), BF16), grid=(t // tm,),
        in_specs=[pl.BlockSpec((tm, d), lambda i: (i, 0)), pl.BlockSpec((1, d), lambda i: (0, 0))],
        out_specs=pl.BlockSpec((tm, d), lambda i: (i, 0)), compiler_params=_cp(),
    )(x, g)


def _rms_bwd(dh, x, g, res, *, name, tm=512):
    t, d = x.shape
    tm = min(tm, t)
    has_res = res is not None

    def body(*refs):
        dh_ref, x_ref, g_ref = refs[:3]
        r_ref = refs[3] if has_res else None
        dx_ref, dg_ref = refs[-2], refs[-1]
        xv = x_ref[...]
        r = lax.rsqrt(jnp.mean(xv * xv, axis=-1, keepdims=True) + EPS)
        xh = xv * r
        dhv = dh_ref[...]
        gd = dhv * g_ref[...]
        dx = r * (gd - xh * jnp.mean(gd * xh, axis=-1, keepdims=True))
        if has_res:
            dx = dx + r_ref[...]
        dx_ref[...] = dx

        @pl.when(pl.program_id(0) == 0)
        def _():
            dg_ref[...] = jnp.zeros_like(dg_ref)

        dg_ref[...] += jnp.sum(dhv * xh, axis=0, keepdims=True)

    row = pl.BlockSpec((tm, d), lambda i: (i, 0))
    vec = pl.BlockSpec((1, d), lambda i: (0, 0))
    return _pcall(
        body, name=name, out_shape=(_sds((t, d), F32), _sds((1, d), F32)), grid=(t // tm,),
        in_specs=[row, row, vec] + ([row] if has_res else []), out_specs=(row, vec),
        compiler_params=_cp(dimension_semantics=("arbitrary",)),
    )(*([dh, x, g] + ([res] if has_res else [])))


def _conv_fwd(z, conv_w, conv_b, *, nb, s, d, tc=256):
    nc = d // tc

    def body(cb_ref, cc_ref, cv_ref, w_ref, b_ref, y_ref):
        u = cc_ref[...] * cv_ref[...]
        row = lax.broadcasted_iota(jnp.int32, u.shape, 0)
        u1 = jnp.where(row >= 1, pltpu.roll(u, 1, axis=0), 0.0)
        u2 = jnp.where(row >= 2, pltpu.roll(u, 2, axis=0), 0.0)
        w = w_ref[...]
        y = w[2:3, :] * u + w[1:2, :] * u1 + w[0:1, :] * u2 + b_ref[...]
        y_ref[...] = (cb_ref[...] * y).astype(BF16)

    def part(p):
        return pl.BlockSpec((s, tc), lambda b, j: (b, p * nc + j))

    return _pcall(
        body, name="conv_fwd", out_shape=_sds((nb * s, d), BF16), grid=(nb, nc),
        in_specs=[part(0), part(1), part(2), pl.BlockSpec((3, tc), lambda b, j: (0, j)), pl.BlockSpec((1, tc), lambda b, j: (0, j))],
        out_specs=pl.BlockSpec((s, tc), lambda b, j: (b, j)), compiler_params=_cp(),
    )(z, z, z, conv_w, conv_b)


def _conv_bwd(z, dy, conv_w, conv_b, dz, *, nb, s, d, tc=256):
    nc = d // tc

    def body(cb_ref, cc_ref, cv_ref, dy_ref, w_ref, b_ref, dz_in, dz_ref, dw_ref, db_ref, stash):
        b_i, p = pl.program_id(1), pl.program_id(2)

        @pl.when(p == 0)
        def _():
            cc, cv = cc_ref[...], cv_ref[...]
            u = cc * cv
            n = u.shape[0]
            row = lax.broadcasted_iota(jnp.int32, u.shape, 0)
            u1 = jnp.where(row >= 1, pltpu.roll(u, 1, axis=0), 0.0)
            u2 = jnp.where(row >= 2, pltpu.roll(u, 2, axis=0), 0.0)
            w = w_ref[...]
            y = w[2:3, :] * u + w[1:2, :] * u1 + w[0:1, :] * u2 + b_ref[...]
            dyv = dy_ref[...]
            dconv = dyv * cb_ref[...]
            g1 = jnp.where(row < n - 1, pltpu.roll(dconv, n - 1, axis=0), 0.0)
            g2 = jnp.where(row < n - 2, pltpu.roll(dconv, n - 2, axis=0), 0.0)
            du = w[2:3, :] * dconv + w[1:2, :] * g1 + w[0:1, :] * g2
            stash[0] = (dyv * y).astype(BF16)
            stash[1] = (du * cv).astype(BF16)
            stash[2] = (du * cc).astype(BF16)
            dws = [jnp.sum(dconv * u2, axis=0, keepdims=True), jnp.sum(dconv * u1, axis=0, keepdims=True),
                   jnp.sum(dconv * u, axis=0, keepdims=True)]
            db = jnp.sum(dconv, axis=0, keepdims=True)

            @pl.when(b_i == 0)
            def _():
                for r in range(3):
                    dw_ref[r:r + 1, :] = dws[r]
                db_ref[...] = db

            @pl.when(b_i > 0)
            def _():
                for r in range(3):
                    dw_ref[r:r + 1, :] += dws[r]
                db_ref[...] += db

        dz_ref[...] = stash[p]

    return _pcall(
        body, name="conv_bwd",
        out_shape=(_sds(dz.shape, dz.dtype), _sds((3, d), F32), _sds((1, d), F32)), grid=(nc, nb, 3),
        in_specs=[
            pl.BlockSpec((s, tc), lambda j, b, p: (b, j)), pl.BlockSpec((s, tc), lambda j, b, p: (b, nc + j)),
            pl.BlockSpec((s, tc), lambda j, b, p: (b, 2 * nc + j)), pl.BlockSpec((s, tc), lambda j, b, p: (b, j)),
            pl.BlockSpec((3, tc), lambda j, b, p: (0, j)), pl.BlockSpec((1, tc), lambda j, b, p: (0, j)),
            HBM_SPEC,
        ],
        out_specs=(pl.BlockSpec((s, tc), lambda j, b, p: (b, p * nc + j)),
                   pl.BlockSpec((3, tc), lambda j, b, p: (0, j)), pl.BlockSpec((1, tc), lambda j, b, p: (0, j))),
        scratch_shapes=[pltpu.VMEM((3, s, tc), BF16)],
        input_output_aliases={6: 0},
        compiler_params=_cp(dimension_semantics=("arbitrary", "arbitrary", "arbitrary")),
    )(z, z, z, dy, conv_w, conv_b, dz)


def _head_rms(xv, g):
    r = lax.rsqrt(jnp.mean(xv * xv, axis=-1, keepdims=True) + EPS)
    return xv * r * g


def _head_rms_bwd(xv, g, dy):
    r = lax.rsqrt(jnp.mean(xv * xv, axis=-1, keepdims=True) + EPS)
    xh = xv * r
    gd = dy * g
    dx = r * (gd - xh * jnp.mean(gd * xh, axis=-1, keepdims=True))
    return dx, jnp.sum(dy * xh, axis=0, keepdims=True)


def _fox_prep(z, q_g, k_g, *, t, d, tm=256):
    nh = d // HD

    def body(z_ref, qg_ref, kg_ref, q_ref, k_ref, v_ref):
        qg, kg = qg_ref[...], kg_ref[...]
        for h in range(nh):
            c0 = h * HD
            q_ref[:, c0:c0 + HD] = _head_rms(z_ref[:, c0:c0 + HD], qg).astype(BF16)
            k_ref[:, c0:c0 + HD] = _head_rms(z_ref[:, d + c0:d + c0 + HD], kg).astype(BF16)
        v_ref[...] = z_ref[:, 2 * d:3 * d].astype(BF16)

    o = pl.BlockSpec((tm, d), lambda i: (i, 0))
    g = pl.BlockSpec((1, HD), lambda i: (0, 0))
    return _pcall(
        body, name="fox_prep", out_shape=(_sds((t, d), BF16),) * 3, grid=(t // tm,),
        in_specs=[pl.BlockSpec((tm, 3 * d), lambda i: (i, 1)), g, g], out_specs=(o, o, o), compiler_params=_cp(),
    )(z, q_g, k_g)


def _lane_cumsum(v, reverse=False):
    n = v.shape[1]
    lane = lax.broadcasted_iota(jnp.int32, v.shape, 1)
    sh = 1
    while sh < n:
        if reverse:
            v = v + jnp.where(lane < n - sh, pltpu.roll(v, n - sh, axis=1), 0.0)
        else:
            v = v + jnp.where(lane >= sh, pltpu.roll(v, sh, axis=1), 0.0)
        sh *= 2
    return v


def _fox_gates(z, f_bias_col, *, nb, s, nh, ff_block):
    def body(z_ref, b_ref, c_ref):
        ff = z_ref[...].T[0:nh, :] + b_ref[...]
        lf = jnp.minimum(ff, 0.0) - jnp.log(1.0 + jnp.exp(-jnp.abs(ff)))
        c_ref[0] = _lane_cumsum(lf) * SQRT_HD

    return _pcall(
        body, name="fox_gates", out_shape=_sds((nb, nh, s), F32), grid=(nb,),
        in_specs=[pl.BlockSpec((s, HD), lambda b: (b, ff_block)), pl.BlockSpec((nh, 1), lambda b: (0, 0))],
        out_specs=pl.BlockSpec((1, nh, s), lambda b: (b, 0, 0)), compiler_params=_cp(),
    )(z, f_bias_col)


def _fox_gates_bwd(z, f_bias_col, row_sums, col_sums, dz, *, nb, s, nh, ff_block):
    def body(z_ref, b_ref, rs_ref, cs_ref, dz_in, dz_ref, db_ref, dc_sc):
        b_i, h = pl.program_id(0), pl.program_id(1)
        dc_sc[pl.ds(h, 1), :] = (rs_ref[...] - cs_ref[...]).T[0:1, :]

        @pl.when(h == nh - 1)
        def _():
            ff = z_ref[...].T[0:nh, :] + b_ref[...]
            dlf = _lane_cumsum(dc_sc[...], reverse=True)
            dff = dlf * (1.0 / (1.0 + jnp.exp(ff)))
            pad = jnp.concatenate([dff, jnp.zeros((HD - nh, s), F32)], axis=0)
            dz_ref[...] = pad.T.astype(BF16)
            dbv = jnp.sum(dff, axis=1, keepdims=True)

            @pl.when(b_i == 0)
            def _():
                db_ref[...] = dbv

            @pl.when(b_i > 0)
            def _():
                db_ref[...] += dbv

    hs = pl.BlockSpec((s, HD), lambda b, h: (b, h))
    zs = pl.BlockSpec((s, HD), lambda b, h: (b, ff_block))
    return _pcall(
        body, name="fox_gates_bwd", out_shape=(_sds(dz.shape, dz.dtype), _sds((nh, 1), F32)), grid=(nb, nh),
        in_specs=[zs, pl.BlockSpec((nh, 1), lambda b, h: (0, 0)), hs, hs, HBM_SPEC],
        out_specs=(zs, pl.BlockSpec((nh, 1), lambda b, h: (0, 0))),
        scratch_shapes=[pltpu.VMEM((nh, s), F32)],
        input_output_aliases={4: 0}, compiler_params=_cp(dimension_semantics=("arbitrary", "arbitrary")),
    )(z, f_bias_col, row_sums, col_sums, dz)


def _fox_pairs(nq, key_major):
    if key_major:
        pairs = [(i, j) for j in range(nq) for i in range(j, nq)]
    else:
        pairs = [(i, j) for i in range(nq) for j in range(i + 1)]
    return jnp.array([p[0] for p in pairs], jnp.int32), jnp.array([p[1] for p in pairs], jnp.int32)


def _with_ones(x):
    return jnp.concatenate([x, jnp.ones_like(x)], axis=1)


def _fox_specs(nb, nh, nq, tq, tk, hp):
    qs = pl.BlockSpec((tq, hp * HD), lambda b, h, p, it, jt: (b * nq + it[p], h))
    ks = pl.BlockSpec((tk, hp * HD), lambda b, h, p, it, jt: (b * nq + jt[p], h))
    cs = pl.BlockSpec((hp, 1, tk), lambda b, h, p, it, jt: (b * (nh // hp) + h, 0, jt[p]))
    return qs, ks, cs


def _fox_raw(qv, kv, ckv, diag, tq, tk):
    sc = lax.dot_general(qv, kv, (((1,), (1,)), ((), ())), preferred_element_type=F32) - ckv
    if diag:
        rows = lax.broadcasted_iota(jnp.int32, (tq, tk), 0)
        cols = lax.broadcasted_iota(jnp.int32, (tq, tk), 1)
        sc = jnp.where(rows >= cols, sc, NEG)
    return sc


def _fox_probs(qv, kv, ckv, lse_col, diag, tq, tk):
    sc = _fox_raw(qv, kv, ckv, diag, tq, tk)
    return jnp.exp2(sc * EXP2_C - lse_col * LOG2E)


def _fox_call(body, *, name, ins, in_specs, out_shape, out_specs, scratch, grid, tables, comm):
    c_ins, c_in_specs, c_out_shape, c_scratch = _comm_lists(comm)
    n_in, n_out, n_scr, nc = len(ins), len(out_shape), len(scratch), len(c_ins)
    n_steps = grid[0] * grid[1] * grid[2]

    def wrapped(*refs):
        pre, rin, cin, rout, cout, scr, csem = _split_refs(refs, 2, n_in, nc, n_out, nc, n_scr)
        step = (pl.program_id(0) * grid[1] + pl.program_id(1)) * grid[2] + pl.program_id(2)
        _run_comm(comm, cin, cout, csem, step, n_steps, "start")
        body(*pre, *rin, *rout, *scr)
        _run_comm(comm, cin, cout, csem, step, n_steps, "end")

    sem = ("arbitrary",) * 3 if comm is not None else ("parallel", "parallel", "arbitrary")
    outs = _pcall(
        wrapped, name=name, out_shape=tuple(list(out_shape) + c_out_shape),
        grid_spec=pltpu.PrefetchScalarGridSpec(
            num_scalar_prefetch=2, grid=grid, in_specs=list(in_specs) + c_in_specs,
            out_specs=tuple(list(out_specs) + [HBM_SPEC] * nc), scratch_shapes=list(scratch) + c_scratch),
        compiler_params=_cp(dimension_semantics=sem),
    )(*tables, *ins, *c_ins)
    return list(outs[:n_out]), list(outs[n_out:])


def _fox_fwd(q, k, v, cks, *, nb, s, nh, tq=512, comm=None):
    tk = tq
    nq = s // tq
    t = nb * s
    hp = FOX_HP
    it, jt = _fox_pairs(nq, key_major=False)

    def body(it_ref, jt_ref, q_ref, k_ref, v_ref, c_ref, o_ref, lse_ref, m_sc, acc_sc):
        p_id = pl.program_id(2)
        i, j = it_ref[p_id], jt_ref[p_id]

        @pl.when(j == 0)
        def _():
            m_sc[...] = jnp.full_like(m_sc, NEG)
            acc_sc[...] = jnp.zeros_like(acc_sc)

        def step(diag):
            for hh in range(hp):
                cols = slice(hh * HD, (hh + 1) * HD)
                sc = _fox_raw(q_ref[:, cols], k_ref[:, cols], c_ref[hh], diag, tq, tk)
                m_old = m_sc[hh]
                m_new = jnp.maximum(m_old, jnp.max(sc, axis=-1, keepdims=True))
                alpha = jnp.exp2((m_old - m_new) * EXP2_C)
                p = jnp.exp2((sc - m_new) * EXP2_C).astype(BF16)
                acc = alpha * acc_sc[hh] + jnp.dot(p, _with_ones(v_ref[:, cols]), preferred_element_type=F32)
                if diag:
                    l = acc[:, HD:2 * HD]
                    o_ref[:, cols] = (acc[:, 0:HD] / l).astype(BF16)
                    lse_ref[:, cols] = m_new * (1.0 / SQRT_HD) + jnp.log(l)
                else:
                    acc_sc[hh] = acc
                    m_sc[hh] = m_new

        @pl.when(j < i)
        def _():
            step(False)

        @pl.when(j == i)
        def _():
            step(True)

    qs, ks, cs = _fox_specs(nb, nh, nq, tq, tk, hp)
    (o, lse), c_out = _fox_call(
        body, name="fox_fwd", ins=[q, k, v, cks], in_specs=[qs, ks, ks, cs],
        out_shape=[_sds((t, nh * HD), BF16), _sds((t, nh * HD), F32)], out_specs=[qs, qs],
        scratch=[pltpu.VMEM((hp, tq, 1), F32), pltpu.VMEM((hp, tq, 2 * HD), F32)],
        grid=(nb, nh // hp, int(it.shape[0])), tables=(it, jt), comm=comm)
    return o, lse, c_out


def _fox_bwd_dq(q, k, v, cks, do, o, lse, *, nb, s, nh, tq=512, comm=None):
    tk = tq
    nq = s // tq
    t = nb * s
    hp = FOX_HP
    it, jt = _fox_pairs(nq, key_major=False)

    def body(it_ref, jt_ref, q_ref, k_ref, v_ref, c_ref, do_ref, o_ref, lse_ref, dq_ref, rs_ref, acc_sc, dl_sc):
        p_id = pl.program_id(2)
        i, j = it_ref[p_id], jt_ref[p_id]

        @pl.when(j == 0)
        def _():
            acc_sc[...] = jnp.zeros_like(acc_sc)
            for hh in range(hp):
                cols = slice(hh * HD, (hh + 1) * HD)
                dl_sc[hh] = jnp.sum(do_ref[:, cols] * o_ref[:, cols].astype(F32), axis=-1, keepdims=True)

        def step(diag):
            for hh in range(hp):
                cols = slice(hh * HD, (hh + 1) * HD)
                kv = k_ref[:, cols]
                p = _fox_probs(q_ref[:, cols], kv, c_ref[hh], lse_ref[:, hh * HD:hh * HD + 1], diag, tq, tk)
                dp = lax.dot_general(do_ref[:, cols].astype(BF16), v_ref[:, cols], (((1,), (1,)), ((), ())),
                                     preferred_element_type=F32)
                ds = (p * (dp - dl_sc[hh])).astype(BF16)
                acc = acc_sc[hh] + jnp.dot(ds, _with_ones(kv), preferred_element_type=F32)
                if diag:
                    dq_ref[:, cols] = acc[:, 0:HD] * (1.0 / SQRT_HD)
                    rs_ref[:, cols] = acc[:, HD:2 * HD]
                else:
                    acc_sc[hh] = acc

        @pl.when(j < i)
        def _():
            step(False)

        @pl.when(j == i)
        def _():
            step(True)

    qs, ks, cs = _fox_specs(nb, nh, nq, tq, tk, hp)
    (dq, rs), c_out = _fox_call(
        body, name="fox_bwd_dq", ins=[q, k, v, cks, do, o, lse], in_specs=[qs, ks, ks, cs, qs, qs, qs],
        out_shape=[_sds((t, nh * HD), F32), _sds((t, nh * HD), F32)], out_specs=[qs, qs],
        scratch=[pltpu.VMEM((hp, tq, 2 * HD), F32), pltpu.VMEM((hp, tq, 1), F32)],
        grid=(nb, nh // hp, int(it.shape[0])), tables=(it, jt), comm=comm)
    return dq, rs, c_out


def _fox_bwd_dkv(q, k, v, cks, do, o, lse, *, nb, s, nh, tq=512, comm=None):
    tk = tq
    nq = s // tq
    t = nb * s
    hp = FOX_HP
    it, jt = _fox_pairs(nq, key_major=True)

    def body(it_ref, jt_ref, q_ref, k_ref, v_ref, c_ref, do_ref, o_ref, lse_ref, dk_ref, dv_ref, cs_ref, dk_sc, dv_sc):
        p_id = pl.program_id(2)
        i, j = it_ref[p_id], jt_ref[p_id]

        def step(diag):
            for hh in range(hp):
                cols = slice(hh * HD, (hh + 1) * HD)
                qv = q_ref[:, cols]
                p = _fox_probs(qv, k_ref[:, cols], c_ref[hh], lse_ref[:, hh * HD:hh * HD + 1], diag, tq, tk)
                dov = do_ref[:, cols]
                dob = dov.astype(BF16)
                dvp = lax.dot_general(p.astype(BF16), dob, (((0,), (0,)), ((), ())), preferred_element_type=F32)
                dp = lax.dot_general(dob, v_ref[:, cols], (((1,), (1,)), ((), ())), preferred_element_type=F32)
                delta = jnp.sum(dov * o_ref[:, cols].astype(F32), axis=-1, keepdims=True)
                ds = (p * (dp - delta)).astype(BF16)
                dkp = lax.dot_general(ds, _with_ones(qv), (((0,), (0,)), ((), ())), preferred_element_type=F32)
                if diag:
                    dk_sc[hh] = dkp
                    dv_sc[hh] = dvp
                else:
                    dk_sc[hh] += dkp
                    dv_sc[hh] += dvp

        @pl.when(i == j)
        def _():
            step(True)

        @pl.when(i > j)
        def _():
            step(False)

        @pl.when(i == nq - 1)
        def _():
            for hh in range(hp):
                cols = slice(hh * HD, (hh + 1) * HD)
                dk_ref[:, cols] = dk_sc[hh, :, 0:HD] * (1.0 / SQRT_HD)
                cs_ref[:, cols] = dk_sc[hh, :, HD:2 * HD]
                dv_ref[:, cols] = dv_sc[hh].astype(BF16)

    qs, ks, cs = _fox_specs(nb, nh, nq, tq, tk, hp)
    (dk, dv, csum), c_out = _fox_call(
        body, name="fox_bwd_dkv", ins=[q, k, v, cks, do, o, lse], in_specs=[qs, ks, ks, cs, qs, qs, qs],
        out_shape=[_sds((t, nh * HD), F32), _sds((t, nh * HD), BF16), _sds((t, nh * HD), F32)], out_specs=[ks, ks, ks],
        scratch=[pltpu.VMEM((hp, tk, 2 * HD), F32), pltpu.VMEM((hp, tk, HD), F32)],
        grid=(nb, nh // hp, int(it.shape[0])), tables=(it, jt), comm=comm)
    return dk, dv, csum, c_out


def _fox_post(z, dqn, dkn, dv, q_g, k_g, dz, *, t, d, tm=256):
    nh = d // HD

    def body(z_ref, dq_ref, dk_ref, dv_ref, qg_ref, kg_ref, dz_in, dz_ref, dqg_ref, dkg_ref):
        qg, kg = qg_ref[...], kg_ref[...]
        aq = jnp.zeros((1, HD), F32)
        ak = jnp.zeros((1, HD), F32)
        for h in range(nh):
            c0 = h * HD
            dx, sg = _head_rms_bwd(z_ref[:, c0:c0 + HD], qg, dq_ref[:, c0:c0 + HD])
            dz_ref[:, c0:c0 + HD] = dx.astype(BF16)
            aq = aq + sg
            dx, sg = _head_rms_bwd(z_ref[:, d + c0:d + c0 + HD], kg, dk_ref[:, c0:c0 + HD])
            dz_ref[:, d + c0:d + c0 + HD] = dx.astype(BF16)
            ak = ak + sg
        dz_ref[:, 2 * d:3 * d] = dv_ref[...]

        @pl.when(pl.program_id(0) == 0)
        def _():
            dqg_ref[...] = aq
            dkg_ref[...] = ak

        @pl.when(pl.program_id(0) > 0)
        def _():
            dqg_ref[...] += aq
            dkg_ref[...] += ak

    o = pl.BlockSpec((tm, d), lambda i: (i, 0))
    g = pl.BlockSpec((1, HD), lambda i: (0, 0))
    z3 = pl.BlockSpec((tm, 3 * d), lambda i: (i, 1))
    return _pcall(
        body, name="fox_post", out_shape=(_sds(dz.shape, dz.dtype), _sds((1, HD), F32), _sds((1, HD), F32)), grid=(t // tm,),
        in_specs=[z3, o, o, o, g, g, HBM_SPEC], out_specs=(z3, g, g),
        input_output_aliases={6: 0}, compiler_params=_cp(dimension_semantics=("arbitrary",)),
    )(z, dqn, dkn, dv, q_g, k_g, dz)


def _xa_prep_kv(kv, k_g, *, rows, d):
    xd = d // XH

    def body(kv_ref, g_ref, k_ref, v_ref):
        g = g_ref[...]
        for h in range(XH):
            c0 = h * xd
            k_ref[:, c0:c0 + xd] = _head_rms(kv_ref[:, c0:c0 + xd], g).astype(BF16)
        v_ref[...] = kv_ref[:, d:2 * d].astype(BF16)

    return _pcall(body, name="xa_prep_kv", out_shape=(_sds((rows, d), BF16),) * 2, compiler_params=_cp())(kv, k_g)


def _xa_fwd(z, kn, vb, q_g, *, nb, s, d, nm, q_block, tq=512):
    xd = d // XH
    nq = s // tq

    def body(z_ref, k_ref, v_ref, g_ref, y_ref):
        g = g_ref[...]
        for h in range(XH):
            c0 = h * xd
            qn = _head_rms(z_ref[:, c0:c0 + xd], g).astype(BF16)
            sc = lax.dot_general(qn, k_ref[:, c0:c0 + xd], (((1,), (1,)), ((), ())), preferred_element_type=F32)
            sc = sc / math.sqrt(xd)
            p = jnp.exp(sc - jnp.max(sc, axis=-1, keepdims=True))
            p = p / jnp.sum(p, axis=-1, keepdims=True)
            y_ref[:, c0:c0 + xd] = jnp.dot(p.astype(BF16), v_ref[:, c0:c0 + xd], preferred_element_type=F32).astype(BF16)

    kvs = pl.BlockSpec((nm, d), lambda b, i: (b, 0))
    return _pcall(
        body, name="xa_fwd", out_shape=_sds((nb * s, d), BF16), grid=(nb, nq),
        in_specs=[pl.BlockSpec((tq, d), lambda b, i: (b * nq + i, q_block)), kvs, kvs, pl.BlockSpec((1, xd), lambda b, i: (0, 0))],
        out_specs=pl.BlockSpec((tq, d), lambda b, i: (b * nq + i, 0)), compiler_params=_cp(),
    )(z, kn, vb, q_g)


def _xa_bwd(z, kn, vb, q_g, dy, dz, *, nb, s, d, nm, q_block, tq=512):
    xd = d // XH
    nq = s // tq

    def body(z_ref, k_ref, v_ref, g_ref, dy_ref, dz_in, dz_ref, dk_ref, dv_ref, dg_ref):
        b_i, i = pl.program_id(0), pl.program_id(1)
        g = g_ref[...]

        @pl.when(i == 0)
        def _():
            dk_ref[...] = jnp.zeros_like(dk_ref)
            dv_ref[...] = jnp.zeros_like(dv_ref)

        @pl.when((i == 0) & (b_i == 0))
        def _():
            dg_ref[...] = jnp.zeros_like(dg_ref)

        for h in range(XH):
            c0 = h * xd
            xq = z_ref[:, c0:c0 + xd]
            qn = _head_rms(xq, g).astype(BF16)
            kh, vh = k_ref[:, c0:c0 + xd], v_ref[:, c0:c0 + xd]
            sc = lax.dot_general(qn, kh, (((1,), (1,)), ((), ())), preferred_element_type=F32) / math.sqrt(xd)
            p = jnp.exp(sc - jnp.max(sc, axis=-1, keepdims=True))
            p = p / jnp.sum(p, axis=-1, keepdims=True)
            dob = dy_ref[:, c0:c0 + xd].astype(BF16)
            dv_ref[:, c0:c0 + xd] += lax.dot_general(p.astype(BF16), dob, (((0,), (0,)), ((), ())), preferred_element_type=F32)
            dp = lax.dot_general(dob, vh, (((1,), (1,)), ((), ())), preferred_element_type=F32)
            ds = p * (dp - jnp.sum(p * dp, axis=-1, keepdims=True)) / math.sqrt(xd)
            dsb = ds.astype(BF16)
            dqn = jnp.dot(dsb, kh, preferred_element_type=F32)
            dk_ref[:, c0:c0 + xd] += lax.dot_general(dsb, qn, (((0,), (0,)), ((), ())), preferred_element_type=F32)
            dx, sg = _head_rms_bwd(xq, g, dqn)
            dz_ref[:, c0:c0 + xd] = dx.astype(BF16)
            dg_ref[...] += sg

    kvs = pl.BlockSpec((nm, d), lambda b, i: (b, 0))
    qs = pl.BlockSpec((tq, d), lambda b, i: (b * nq + i, q_block))
    gs = pl.BlockSpec((1, xd), lambda b, i: (0, 0))
    return _pcall(
        body, name="xa_bwd",
        out_shape=(_sds(dz.shape, dz.dtype), _sds((nb * nm, d), F32), _sds((nb * nm, d), F32), _sds((1, xd), F32)), grid=(nb, nq),
        in_specs=[qs, kvs, kvs, gs, pl.BlockSpec((tq, d), lambda b, i: (b * nq + i, 0)), HBM_SPEC],
        out_specs=(qs, kvs, kvs, gs), input_output_aliases={5: 0},
        compiler_params=_cp(dimension_semantics=("arbitrary", "arbitrary")),
    )(z, kn, vb, q_g, dy, dz)


def _xa_post_kv(kv, dkn, dv, k_g, *, rows, d):
    xd = d // XH

    def body(kv_ref, dk_ref, dv_ref, g_ref, dkv_ref, dg_ref):
        g = g_ref[...]
        acc = jnp.zeros((1, xd), F32)
        for h in range(XH):
            c0 = h * xd
            dx, sg = _head_rms_bwd(kv_ref[:, c0:c0 + xd], g, dk_ref[:, c0:c0 + xd])
            dkv_ref[:, c0:c0 + xd] = dx.astype(BF16)
            acc = acc + sg
        dkv_ref[:, d:2 * d] = dv_ref[...].astype(BF16)
        dg_ref[...] = acc

    return _pcall(body, name="xa_post_kv", out_shape=(_sds((rows, 2 * d), BF16), _sds((1, xd), F32)), compiler_params=_cp())(
        kv, dkn, dv, k_g)


def _sigmoid(v):
    return 1.0 / (1.0 + jnp.exp(-v))


def _branches_fwd(z, ys, ws, *, t, d, gate_block, tm=512, tn=512, comm=None):
    nj = d // tn
    c_ins, c_in_specs, c_out_shape, c_scratch = _comm_lists(comm)
    n_steps = (t // tm) * nj

    def body(*refs):
        _, ins, cin, outs, cout, _, csem = _split_refs(refs, 0, 9, len(c_ins), 2, len(c_ins), 0)
        ya_ref, yb_ref, yc_ref, wa_ref, wb_ref, wc_ref, ga_ref, gb_ref, gc_ref = ins
        m_ref, p_ref = outs
        step = pl.program_id(0) * nj + pl.program_id(1)
        _run_comm(comm, cin, cout, csem, step, n_steps, "start")
        acc = None
        for q, (y_ref, w_ref, g_ref) in enumerate(((ya_ref, wa_ref, ga_ref), (yb_ref, wb_ref, gb_ref), (yc_ref, wc_ref, gc_ref))):
            pr = jnp.dot(y_ref[...], w_ref[...], preferred_element_type=F32)
            p_ref[q] = pr.astype(BF16)
            term = _sigmoid(g_ref[...]) * pr
            acc = term if acc is None else acc + term
        m_ref[...] = acc.astype(BF16)
        _run_comm(comm, cin, cout, csem, step, n_steps, "end")

    y_spec = pl.BlockSpec((tm, d), lambda i, j: (i, 0))
    w_spec = pl.BlockSpec((d, tn), lambda i, j: (0, j))

    def gate(q):
        return pl.BlockSpec((tm, tn), lambda i, j: (i, (gate_block + q) * nj + j))

    outs = _pcall(
        body, name="branches_fwd", out_shape=tuple([_sds((t, d), BF16), _sds((3, t, d), BF16)] + c_out_shape), grid=(t // tm, nj),
        in_specs=[y_spec] * 3 + [w_spec] * 3 + [gate(0), gate(1), gate(2)] + c_in_specs,
        out_specs=tuple([pl.BlockSpec((tm, tn), lambda i, j: (i, j)), pl.BlockSpec((3, tm, tn), lambda i, j: (0, i, j))]
                        + [HBM_SPEC] * len(c_ins)),
        scratch_shapes=c_scratch, compiler_params=_cp(dimension_semantics=("arbitrary", "arbitrary")),
    )(*ys, *ws, z, z, z, *c_ins)
    return outs[0], outs[1], list(outs[2:])


def _gates_bwd(z, dm, proj, dz, *, t, d, gate_block, tm=512):
    def body(g_ref, dm_ref, p_ref, dz_in, dz_ref, da_ref, db_ref, dc_ref):
        q = pl.program_id(1)
        sg = _sigmoid(g_ref[...])
        dmv = dm_ref[...]
        dz_ref[...] = (dmv * p_ref[0].astype(F32) * sg * (1.0 - sg)).astype(BF16)
        dp = (dmv * sg).astype(BF16)
        @pl.when(q == 0)
        def _():
            da_ref[...] = dp

        @pl.when(q == 1)
        def _():
            db_ref[...] = dp

        @pl.when(q == 2)
        def _():
            dc_ref[...] = dp

    gs = pl.BlockSpec((tm, d), lambda i, q: (i, gate_block + q))
    o = pl.BlockSpec((tm, d), lambda i, q: (i, 0))
    return _pcall(
        body, name="gates_bwd", out_shape=(_sds(dz.shape, dz.dtype),) + (_sds((t, d), BF16),) * 3, grid=(t // tm, 3),
        in_specs=[gs, o, pl.BlockSpec((1, tm, d), lambda i, q: (q, i, 0)), HBM_SPEC], out_specs=(gs, o, o, o),
        input_output_aliases={3: 0}, compiler_params=_cp(dimension_semantics=("arbitrary", "arbitrary")),
    )(z, dm, proj, dz)


def _ffn_in_fwd(h2, wfi_t, *, t, d, dff, tm=512, tn=1408):
    nj = dff // tn

    def body(h_ref, wg_ref, wu_ref, g_ref, u_ref, a_ref):
        hv = h_ref[...]
        dn = (((1,), (1,)), ((), ()))
        gv = lax.dot_general(hv, wg_ref[...], dn, preferred_element_type=F32)
        uv = lax.dot_general(hv, wu_ref[...], dn, preferred_element_type=F32)
        g_ref[...] = gv.astype(BF16)
        u_ref[...] = uv.astype(BF16)
        a_ref[...] = (gv * _sigmoid(gv) * uv).astype(BF16)

    o = pl.BlockSpec((tm, tn), lambda i, j: (i, j))
    return _pcall(
        body, name="ffn_in_fwd", out_shape=(_sds((t, dff), BF16),) * 3, grid=(t // tm, nj),
        in_specs=[pl.BlockSpec((tm, d), lambda i, j: (i, 0)), pl.BlockSpec((tn, d), lambda i, j: (j, 0)),
                  pl.BlockSpec((tn, d), lambda i, j: (nj + j, 0))],
        out_specs=(o, o, o), compiler_params=_cp(),
    )(h2, wfi_t, wfi_t)


def _ffn_out_bwd(dy, wfo, gate, up, *, t, d, dff, tm=512, tn=1408):
    def body(dy_ref, w_ref, g_ref, u_ref, dg_ref, du_ref):
        dav = lax.dot_general(dy_ref[...].astype(BF16), w_ref[...], (((1,), (1,)), ((), ())), preferred_element_type=F32)
        gv = g_ref[...].astype(F32)
        sg = _sigmoid(gv)
        dg_ref[...] = (dav * u_ref[...].astype(F32) * sg * (1.0 + gv * (1.0 - sg))).astype(BF16)
        du_ref[...] = (dav * gv * sg).astype(BF16)

    o = pl.BlockSpec((tm, tn), lambda i, j: (i, j))
    return _pcall(
        body, name="ffn_out_bwd", out_shape=(_sds((t, dff), BF16),) * 2, grid=(t // tm, dff // tn),
        in_specs=[pl.BlockSpec((tm, d), lambda i, j: (i, 0)), pl.BlockSpec((tn, d), lambda i, j: (j, 0)), o, o],
        out_specs=(o, o), compiler_params=_cp(),
    )(dy, wfo, gate, up)


def _loss_head(y, target, *, t, d, tm=512):
    def body(y_ref, t_ref, dy_ref, l_ref):
        e = y_ref[...] - t_ref[...]
        dy_ref[...] = e * (1.0 / d)

        @pl.when(pl.program_id(0) == 0)
        def _():
            l_ref[...] = jnp.zeros_like(l_ref)

        l_ref[...] += jnp.sum(jnp.sum(e * e, axis=-1, keepdims=True), axis=0, keepdims=True) * (0.5 / d)

    o = pl.BlockSpec((tm, d), lambda i: (i, 0))
    dy, l = _pcall(
        body, name="loss_head", out_shape=(_sds((t, d), F32), _sds((1, HD), F32)), grid=(t // tm,),
        in_specs=[o, o], out_specs=(o, pl.BlockSpec((1, HD), lambda i: (0, 0))),
        compiler_params=_cp(dimension_semantics=("arbitrary",)),
    )(y, target)
    return dy, l[0, 0]


def _adamw(w, g, m, v, *, name):
    if w.ndim == 3:
        rows, _, cols = w.shape
        tm = max(tt for tt in range(1, 65) if rows % tt == 0)
        block, imap, grid = (tm, 1, cols), (lambda i, j: (i, 0, 0)), (rows // tm, 1)
    else:
        rows, cols = w.shape
        tm, tn = rows, cols
        for cand in (256, 128, 64, 32, 16, 8):
            if rows > cand and rows % cand == 0:
                tm = cand
                break
        if tm == rows and rows > 256 and cols % 128 == 0:
            tn = 128
        block, imap, grid = (tm, tn), (lambda i, j: (i, j)), (rows // tm, cols // tn)
    c1 = 1.0 - ADAM_B1 ** ADAM_STEP
    c2 = 1.0 - ADAM_B2 ** ADAM_STEP

    def body(w_ref, g_ref, m_ref, v_ref, d_ref, nm_ref, nv_ref):
        gv = g_ref[...]
        mn = ADAM_B1 * m_ref[...] + (1.0 - ADAM_B1) * gv
        vn = ADAM_B2 * v_ref[...] + (1.0 - ADAM_B2) * (gv * gv)
        d_ref[...] = -ADAM_LR * ((mn / c1) / (jnp.sqrt(vn / c2) + ADAM_EPS) + ADAM_WD * w_ref[...])
        nm_ref[...] = mn
        nv_ref[...] = vn

    spec = pl.BlockSpec(block, imap)
    return _pcall(
        body, name=name, out_shape=(_sds(w.shape, F32),) * 3, grid=grid,
        in_specs=[spec] * 4, out_specs=(spec,) * 3, compiler_params=_cp(),
    )(w, g, m, v)


def _pad_rows(a, rows):
    return a if a.shape[0] == rows else jnp.pad(a, ((0, rows - a.shape[0]),) + ((0, 0),) * (a.ndim - 1))


class _Pack:
    def __init__(self, row_sizes):
        self.rows = list(row_sizes)
        self.padded = [-(-r // 16) * 16 for r in self.rows]
        self.offs = [sum(self.padded[:i]) for i in range(len(self.rows))]
        self.total = sum(self.padded)

    def stack(self, blocks):
        return jnp.concatenate([_pad_rows(b.astype(BF16), p) for b, p in zip(blocks, self.padded)], axis=0)

    def full(self, gathered, i):
        r = self.rows[i]
        return gathered[:, self.offs[i]:self.offs[i] + r, :].reshape(N_DEV * r, gathered.shape[2])

    def for_core(self, full_grads, core):
        parts = []
        for gfull, r, p in zip(full_grads, self.rows, self.padded):
            blk = gfull.reshape(N_DEV // 2, 2, r, gfull.shape[1])
            blk = lax.dynamic_index_in_dim(blk, core, axis=1, keepdims=False).astype(BF16)
            if p != r:
                blk = jnp.pad(blk, ((0, 0), (0, p - r), (0, 0)))
            parts.append(blk)
        return jnp.concatenate(parts, axis=1)

    def shard(self, g_pack, i):
        return g_pack[self.offs[i]:self.offs[i] + self.rows[i], :]


def kernel(x, mem, norm1_g, w_in, conv_w, conv_b, fox_f_bias, fox_q_g, fox_k_g, mem_norm_g, w_mem_kv, xa_q_g, xa_k_g, w_br_conv, w_br_fox, w_br_xa, w_o, norm2_g, w_ffn_in, w_ffn_out, loss_target, m_norm1_g, m_w_in, m_conv_w, m_conv_b, m_fox_f_bias, m_fox_q_g, m_fox_k_g, m_mem_norm_g, m_w_mem_kv, m_xa_q_g, m_xa_k_g, m_w_br_conv, m_w_br_fox, m_w_br_xa, m_w_o, m_norm2_g, m_w_ffn_in, m_w_ffn_out, v_norm1_g, v_w_in, v_conv_w, v_conv_b, v_fox_f_bias, v_fox_q_g, v_fox_k_g, v_mem_norm_g, v_w_mem_kv, v_xa_q_g, v_xa_k_g, v_w_br_conv, v_w_br_fox, v_w_br_xa, v_w_o, v_norm2_g, v_w_ffn_in, v_w_ffn_out):
    nb, s, d = x.shape
    t = nb * s
    nm = mem.shape[1]
    nh = d // HD
    dff = w_ffn_out.shape[1] * N_DEV
    n_in = w_in.shape[2] * N_DEV
    n_in_pad = -(-n_in // 1152) * 1152
    ff_block = (10 * d) // HD
    my_c = lax.axis_index("c")
    my_q = 2 * lax.axis_index("x") + lax.axis_index("y")

    pack_a = _Pack([w_in.shape[2]])
    pack_b = _Pack([w_mem_kv.shape[2], w_br_conv.shape[1], w_br_fox.shape[1], w_br_xa.shape[1], w_o.shape[1],
                    w_ffn_in.shape[2], w_ffn_out.shape[1]])
    gather_2 = _Pack([w_mem_kv.shape[2], w_br_conv.shape[1], w_br_fox.shape[1], w_br_xa.shape[1], w_o.shape[1]])
    stack_a = pack_a.stack([w_in[0].T])
    stack_2 = gather_2.stack([w_mem_kv[0].T, w_br_conv[0], w_br_fox[0], w_br_xa[0], w_o[0]])
    stack_fi = w_ffn_in[0].T.astype(BF16)
    stack_fo = w_ffn_out[0].astype(BF16)
    cw_block = _pad_rows(conv_w[0], 8)
    rows_a = w_in.shape[2]
    my_x, my_y = lax.axis_index("x"), lax.axis_index("y")
    arrival = [(my_x, my_y, my_c), (my_x, my_y, 1 - my_c)]
    for cx, cy in [(1 - my_x, my_y), (my_x, 1 - my_y), (1 - my_x, 1 - my_y)]:
        arrival += [(cx, cy, my_c), (cx, cy, 1 - my_c)]
    order = jnp.stack([4 * ax + 2 * ay + ac for ax, ay, ac in arrival]).astype(jnp.int32)

    x2 = x.reshape(t, d)
    mem2 = mem.reshape(nb * nm, d)
    tgt2 = loss_target.reshape(t, d)
    h1 = _rms_fwd(x2, norm1_g, name="rms1_fwd")
    z, slabs, _ = _z_gathered(h1, stack_a, order, t=t, d=d, n_pad=n_in_pad, rows=rows_a)
    z = _z_fix(h1, slabs, z, t=t, d=d, rows=rows_a)
    ff_rows = _pad_rows(slabs[N_DEV - 1, rows_a - nh:rows_a, :], HD)
    z = _z_tail(h1, ff_rows, z, t=t, d=d, col_block=ff_block)
    qn, kn, vb = _fox_prep(z, fox_q_g, fox_k_g, t=t, d=d)
    f_bias_col = fox_f_bias.reshape(nh, 1)
    ck_rows = _fox_gates(z, f_bias_col, nb=nb, s=s, nh=nh, ff_block=ff_block).reshape(nb * nh, 1, s)
    y_fox, lse, (gathered_2, cw_g, gathered_fi) = _fox_fwd(qn, kn, vb, ck_rows, nb=nb, s=s, nh=nh,
                                                           comm=_AllGather([stack_2, cw_block, stack_fi]))
    wkv_t, wbc, wbf, wbx, wo = (gather_2.full(gathered_2, i) for i in range(5))
    wfi_t = gathered_fi.reshape(2 * dff, d)
    conv_w_full = jnp.transpose(cw_g[:, 0:3, :], (1, 0, 2)).reshape(3, d)

    y_conv = _conv_fwd(z, conv_w_full, conv_b, nb=nb, s=s, d=d)
    mem_n = _rms_fwd(mem2, mem_norm_g, name="rms_mem_fwd")
    kv = _mm(mem_n, wkv_t, m=nb * nm, n=2 * d, k=d, tb=True, name="mm_kv")
    xkn, xvb = _xa_prep_kv(kv, xa_k_g, rows=nb * nm, d=d)
    y_xa = _xa_fwd(z, xkn, xvb, xa_q_g, nb=nb, s=s, d=d, nm=nm, q_block=6)
    merged, proj, (gathered_fo,) = _branches_fwd(z, (y_conv, y_fox, y_xa), (wbc, wbf, wbx), t=t, d=d, gate_block=7,
                                                 comm=_AllGather([stack_fo]))
    wfo = gathered_fo.reshape(dff, d)
    x1 = _mm(merged, wo, m=t, n=d, k=d, res=x2, name="mm_x1")
    h2 = _rms_fwd(x1, norm2_g, name="rms2_fwd")
    ff_gate, ff_up, act = _ffn_in_fwd(h2, wfi_t, t=t, d=d, dff=dff)
    y = _mm(act, wfo, m=t, n=d, k=dff, res=x1, tk=dff, name="mm_y")
    dy, loss_local = _loss_head(y, tgt2, t=t, d=d)

    g_wfo = _mm(act, dy, m=dff, n=d, k=t, ta=True, tm=1408, tk=2048, out_dtype=BF16, name="mm_g_wfo")
    dgate, dup = _ffn_out_bwd(dy, wfo, ff_gate, ff_up, t=t, d=d, dff=dff)
    dh2 = _mm(dgate, wfi_t, m=t, n=d, k=dff, tk=dff, name="mm_dh2_g")
    dh2 = _mm(dup, wfi_t, m=t, n=d, k=dff, tk=dff, b_off=(1, 0), res=dh2, name="mm_dh2_u")
    g_wfi_g = _mm(dgate, h2, m=dff, n=d, k=t, ta=True, tm=1408, tk=2048, out_dtype=BF16, name="mm_g_wfi_g")
    g_wfi_u = _mm(dup, h2, m=dff, n=d, k=t, ta=True, tm=1408, tk=2048, out_dtype=BF16, name="mm_g_wfi_u")
    dx1, g_norm2 = _rms_bwd(dh2, x1, norm2_g, dy, name="rms2_bwd")
    dmerged = _mm(dx1, wo, m=t, n=d, k=d, tb=True, name="mm_dmerged")
    g_wo = _mm(merged, dx1, m=d, n=d, k=t, ta=True, tk=2048, out_dtype=BF16, name="mm_g_wo")

    dz = lax.empty((t, n_in_pad), BF16)
    dz, dpa, dpb, dpc = _gates_bwd(z, dmerged, proj, dz, t=t, d=d, gate_block=7)
    dy_conv = _mm(dpa, wbc, m=t, n=d, k=d, tb=True, name="mm_dy_conv")
    g_wbc = _mm(y_conv, dpa, m=d, n=d, k=t, ta=True, tk=4096, out_dtype=BF16, name="mm_g_wbc")
    dy_fox = _mm(dpb, wbf, m=t, n=d, k=d, tb=True, name="mm_dy_fox")
    g_wbf = _mm(y_fox, dpb, m=d, n=d, k=t, ta=True, tk=4096, out_dtype=BF16, name="mm_g_wbf")
    dy_xa = _mm(dpc, wbx, m=t, n=d, k=d, tb=True, name="mm_dy_xa")
    g_wbx = _mm(y_xa, dpc, m=d, n=d, k=t, ta=True, tk=4096, out_dtype=BF16, name="mm_g_wbx")

    dz, g_conv_w, g_conv_b = _conv_bwd(z, dy_conv, conv_w_full, conv_b, dz, nb=nb, s=s, d=d)
    dz, dxkn, dxv, g_xa_q = _xa_bwd(z, xkn, xvb, xa_q_g, dy_xa, dz, nb=nb, s=s, d=d, nm=nm, q_block=6)
    dkv, g_xa_k = _xa_post_kv(kv, dxkn, dxv, xa_k_g, rows=nb * nm, d=d)
    g_wkv_t = _mm(dkv, mem_n, m=2 * d, n=d, k=nb * nm, ta=True, out_dtype=BF16, name="mm_g_wkv")
    dmem_n = _mm(dkv, wkv_t, m=nb * nm, n=d, k=2 * d, tk=2 * d, name="mm_dmem")
    _, g_mem_norm = _rms_bwd(dmem_n, mem2, mem_norm_g, None, name="rms_mem_bwd")

    grads_b = [g_wkv_t, g_wbc, g_wbf, g_wbx, g_wo, jnp.concatenate([g_wfi_g, g_wfi_u], axis=0), g_wfo]
    keep_b, send_b = pack_b.for_core(grads_b, my_c), pack_b.for_core(grads_b, 1 - my_c)
    dqn, ds_rows, (from_sibling_b,) = _fox_bwd_dq(qn, kn, vb, ck_rows, dy_fox, y_fox, lse, nb=nb, s=s, nh=nh,
                                                  comm=_SiblingSwap(send_b))
    part_b = _add2(keep_b, from_sibling_b, name="rs_add_sibling_b")
    dkn, dvb, ds_cols, (from_chips_b,) = _fox_bwd_dkv(qn, kn, vb, ck_rows, dy_fox, y_fox, lse, nb=nb, s=s, nh=nh,
                                                     comm=_ChipExchange(part_b))
    g_pack_b = _add4(lax.dynamic_index_in_dim(part_b, my_q, axis=0, keepdims=False), from_chips_b, name="rs_add_chips_b")
    dz, g_fox_q, g_fox_k = _fox_post(z, dqn, dkn, dvb, fox_q_g, fox_k_g, dz, t=t, d=d)
    dz, g_f_bias = _fox_gates_bwd(z, f_bias_col, ds_rows, ds_cols, dz, nb=nb, s=s, nh=nh, ff_block=ff_block)

    keep_a, from_sibling_a = _g_win_slabs(dz, h1, my_c, t=t, d=d, n_slab=N_DEV, rows=rows_a, rp=slabs.shape[1])
    part_a = _add2(keep_a, from_sibling_a, name="rs_add_sibling_a")
    dh1, (from_chips_a,) = _dh1_from_slabs(dz, slabs, t=t, d=d, rows=rows_a, comm=_ChipExchange(part_a))
    g_pack_a = _add4(lax.dynamic_index_in_dim(part_a, my_q, axis=0, keepdims=False), from_chips_a, name="rs_add_chips_a",
                     rows_apart=True)
    dx, g_norm1 = _rms_bwd(dh1, x2, norm1_g, dx1, name="rms1_bwd")

    xd = d // XH
    tail = jnp.concatenate(
        [_pad_rows(g_f_bias.reshape(nh), HD).reshape(1, HD), g_fox_q, g_fox_k, g_xa_q, g_xa_k,
         jnp.zeros((1, d - 3 * HD - 2 * xd), F32)], axis=1)
    small = jnp.concatenate([g_norm1, g_norm2, g_conv_w, g_conv_b, g_mem_norm, tail], axis=0)
    small = _all_reduce_small(small)
    gs_conv_w = lax.dynamic_slice_in_dim(small[2:5], (2 * my_q + my_c) * (d // N_DEV), d // N_DEV, axis=1)

    grads = {
        "norm1_g": small[0:1], "w_in": g_pack_a, "conv_w": gs_conv_w, "conv_b": small[5:6],
        "fox_f_bias": small[7:8, 0:nh], "fox_q_g": small[7:8, HD:2 * HD], "fox_k_g": small[7:8, 2 * HD:3 * HD],
        "mem_norm_g": small[6:7], "w_mem_kv": pack_b.shard(g_pack_b, 0), "xa_q_g": small[7:8, 3 * HD:3 * HD + xd],
        "xa_k_g": small[7:8, 3 * HD + xd:3 * HD + 2 * xd], "w_br_conv": pack_b.shard(g_pack_b, 1),
        "w_br_fox": pack_b.shard(g_pack_b, 2), "w_br_xa": pack_b.shard(g_pack_b, 3), "w_o": pack_b.shard(g_pack_b, 4),
        "norm2_g": small[1:2], "w_ffn_in": pack_b.shard(g_pack_b, 5), "w_ffn_out": pack_b.shard(g_pack_b, 6),
    }
    transposed = {"w_in", "w_mem_kv", "w_ffn_in"}
    weights = {
        "norm1_g": (norm1_g, m_norm1_g, v_norm1_g), "w_in": (w_in, m_w_in, v_w_in), "conv_w": (conv_w, m_conv_w, v_conv_w),
        "conv_b": (conv_b, m_conv_b, v_conv_b), "fox_f_bias": (fox_f_bias, m_fox_f_bias, v_fox_f_bias),
        "fox_q_g": (fox_q_g, m_fox_q_g, v_fox_q_g), "fox_k_g": (fox_k_g, m_fox_k_g, v_fox_k_g),
        "mem_norm_g": (mem_norm_g, m_mem_norm_g, v_mem_norm_g), "w_mem_kv": (w_mem_kv, m_w_mem_kv, v_w_mem_kv),
        "xa_q_g": (xa_q_g, m_xa_q_g, v_xa_q_g), "xa_k_g": (xa_k_g, m_xa_k_g, v_xa_k_g),
        "w_br_conv": (w_br_conv, m_w_br_conv, v_w_br_conv), "w_br_fox": (w_br_fox, m_w_br_fox, v_w_br_fox),
        "w_br_xa": (w_br_xa, m_w_br_xa, v_w_br_xa), "w_o": (w_o, m_w_o, v_w_o), "norm2_g": (norm2_g, m_norm2_g, v_norm2_g),
        "w_ffn_in": (w_ffn_in, m_w_ffn_in, v_w_ffn_in), "w_ffn_out": (w_ffn_out, m_w_ffn_out, v_w_ffn_out),
    }
    out_g, out_d, out_m, out_v = [], [], [], []
    for name, (w, m, v) in weights.items():
        shape = w.shape
        if name == "w_in":
            w3, m3, v3 = (a.transpose(2, 0, 1) for a in (w, m, v))
            dl, mn, vn = _adamw(w3, g_pack_a, m3, v3, name="adamw_" + name)
            out_g.append(g_pack_a[:shape[2]].transpose(1, 2, 0))
            out_d.append(dl.transpose(1, 2, 0))
            out_m.append(mn.transpose(1, 2, 0))
            out_v.append(vn.transpose(1, 2, 0))
            continue
        tr = name in transposed

        def as2d(a):
            a = a.reshape(shape[-2], shape[-1])
            return a.T if tr else a

        def back(a):
            return (a.T if tr else a).reshape(shape)

        w2 = as2d(w)
        g2 = grads[name].reshape(w2.shape)
        dl, mn, vn = _adamw(w2, g2, as2d(m), as2d(v), name="adamw_" + name)
        out_g.append(back(g2))
        out_d.append(back(dl))
        out_m.append(back(mn))
        out_v.append(back(vn))

    loss = lax.psum(loss_local, ("x", "y", "c"))
    return (loss, dx.reshape(nb, s, d), *out_g, *out_d, *out_m, *out_v)
```

```python
import math

import jax
import jax.numpy as jnp
from jax import lax
from jax.experimental import pallas as pl
from jax.experimental.pallas import tpu as pltpu

F32, BF16 = jnp.float32, jnp.bfloat16
EPS = 1e-6
HD = 128
XH = 4
N_DEV = 8
MESH = pl.DeviceIdType.MESH
VMEM_LIMIT = 52 * 1024 * 1024
NEG = -1e30
SQRT_HD = math.sqrt(HD)
EXP2_C = math.log2(math.e) / SQRT_HD
LOG2E = math.log2(math.e)
FOX_HP = 4

ADAM_LR, ADAM_B1, ADAM_B2, ADAM_EPS, ADAM_WD, ADAM_STEP = 0.001, 0.9, 0.999, 1e-08, 0.01, 10


def _cp(**kw):
    return pltpu.CompilerParams(vmem_limit_bytes=VMEM_LIMIT, **kw)


def _pcall(body, **kw):
    return pl.pallas_call(body, **kw)


def _sds(shape, dtype):
    return jax.ShapeDtypeStruct(shape, dtype)


HBM_SPEC = pl.BlockSpec(memory_space=pl.ANY)


def _mesh_pos():
    return lax.axis_index("x"), lax.axis_index("y"), lax.axis_index("c")


class _AllGather:
    def __init__(self, blocks):
        self.blocks = list(blocks)
        n = len(self.blocks)
        self.out_shape = [_sds((N_DEV,) + b.shape, b.dtype) for b in self.blocks]
        self.scratch = [pltpu.SemaphoreType.DMA((n, 7)), pltpu.SemaphoreType.DMA((n, 7)), pltpu.SemaphoreType.DMA((n,))]

    def bind(self, ins, outs, sems):
        n = len(ins)
        send_sems, recv_sems, local_sems = sems
        x, y, c = _mesh_pos()
        me, sibling = (x, y, c), (x, y, 1 - c)
        chips = [(1 - x, y), (x, 1 - y), (1 - x, 1 - y)]

        def slab(t, dev):
            return outs[t].at[4 * dev[0] + 2 * dev[1] + dev[2]]

        def copy(t, k, block, to, src=None):
            dst = slab(t, block)
            return pltpu.make_async_remote_copy(
                src_ref=dst if src is None else src, dst_ref=dst, send_sem=send_sems.at[t, k], recv_sem=recv_sems.at[t, k],
                device_id=to, device_id_type=MESH)

        mine = [pltpu.make_async_copy(ins[t], slab(t, me), local_sems.at[t]) for t in range(n)]
        first = []
        for t in range(n):
            first.append(copy(t, 0, me, sibling, src=ins[t]))
            first += [copy(t, 1 + j, me, (*chip, c), src=ins[t]) for j, chip in enumerate(chips)]
        passed = [copy(t, 4 + j, (*chip, c), sibling) for j, chip in enumerate(chips) for t in range(n)]

        def start():
            for cp in mine + first:
                cp.start()

        def mid():
            for j, chip in enumerate(chips):
                for t in range(n):
                    copy(t, 1 + j, (*chip, c), me).wait_recv()
                    passed[j * n + t].start()

        def finish():
            for t in range(n):
                copy(t, 0, sibling, me).wait_recv()
                for j, chip in enumerate(chips):
                    copy(t, 4 + j, (*chip, 1 - c), me).wait_recv()
            for cp in first + passed:
                cp.wait_send()
            for cp in mine:
                cp.wait()

        return start, mid, finish


class _SiblingSwap:
    def __init__(self, send):
        self.blocks = [send]
        self.out_shape = [_sds(send.shape, send.dtype)]
        self.scratch = [pltpu.SemaphoreType.DMA, pltpu.SemaphoreType.DMA]

    def bind(self, ins, outs, sems):
        x, y, c = _mesh_pos()
        cp = pltpu.make_async_remote_copy(src_ref=ins[0], dst_ref=outs[0], send_sem=sems[0], recv_sem=sems[1],
                                          device_id=(x, y, 1 - c), device_id_type=MESH)
        return cp.start, (lambda: None), cp.wait


class _ChipExchange:
    def __init__(self, part):
        self.blocks = [part]
        self.out_shape = [_sds((3,) + part.shape[1:], part.dtype)]
        self.scratch = [pltpu.SemaphoreType.DMA((3,)), pltpu.SemaphoreType.DMA((3,))]

    def bind(self, ins, outs, sems):
        x, y, c = _mesh_pos()
        chips = [(1 - x, y), (x, 1 - y), (1 - x, 1 - y)]
        cps = [
            pltpu.make_async_remote_copy(src_ref=ins[0].at[2 * cx + cy], dst_ref=outs[0].at[j], send_sem=sems[0].at[j],
                                         recv_sem=sems[1].at[j], device_id=(cx, cy, c), device_id_type=MESH)
            for j, (cx, cy) in enumerate(chips)
        ]

        def start():
            for cp in cps:
                cp.start()

        def finish():
            for cp in cps:
                cp.wait()

        return start, (lambda: None), finish


def _comm_lists(comm):
    if comm is None:
        return [], [], [], []
    n = len(comm.blocks)
    return list(comm.blocks), [HBM_SPEC] * n, list(comm.out_shape), list(comm.scratch)


def _split_refs(refs, n_pre, n_in, n_cin, n_out, n_cout, n_scr):
    cuts = [n_pre, n_in, n_cin, n_out, n_cout, n_scr]
    out, at = [], 0
    for c in cuts:
        out.append(refs[at:at + c])
        at += c
    out.append(refs[at:])
    return out


def _run_comm(comm, cin, cout, csem, step, n_steps, when="start"):
    if comm is None:
        return
    start, mid, finish = comm.bind(cin, cout, csem)
    if when == "start":
        pl.when(step == 0)(start)
    else:
        pl.when(step == max(n_steps - 2, 0))(mid)
        pl.when(step == n_steps - 1)(finish)


def _comm_only(comm, name):
    ins, in_specs, out_shape, scratch = _comm_lists(comm)
    n = len(ins)

    def body(*refs):
        start, mid, finish = comm.bind(refs[:n], refs[n:2 * n], refs[2 * n:])
        start()
        mid()
        finish()

    outs = _pcall(body, name=name, out_shape=tuple(out_shape), in_specs=in_specs, out_specs=tuple([HBM_SPEC] * n),
                  scratch_shapes=scratch, compiler_params=_cp())(*ins)
    return list(outs)


def _all_reduce_small(v):
    rows, cols = v.shape

    def body(v_ref, o_ref, slots, send_sems, recv_sems):
        x, y, c = _mesh_pos()
        me = 4 * x + 2 * y + c
        slots[me] = v_ref[...]
        cps = []
        for k in range(1, N_DEV):
            px, py, pc = x ^ (k >> 2), y ^ ((k >> 1) & 1), c ^ (k & 1)
            cps.append(pltpu.make_async_remote_copy(
                src_ref=v_ref, dst_ref=slots.at[me], send_sem=send_sems.at[k - 1], recv_sem=recv_sems.at[k - 1],
                device_id=(px, py, pc), device_id_type=MESH))
        for cp in cps:
            cp.start()
        for cp in cps:
            cp.wait()
        acc = slots[0]
        for k in range(1, N_DEV):
            acc = acc + slots[k]
        o_ref[...] = acc

    return _pcall(
        body, name="all_reduce_small", out_shape=_sds((rows, cols), F32),
        in_specs=[pl.BlockSpec(memory_space=pltpu.VMEM)], out_specs=pl.BlockSpec(memory_space=pltpu.VMEM),
        scratch_shapes=[pltpu.VMEM((N_DEV, rows, cols), F32), pltpu.SemaphoreType.DMA((7,)), pltpu.SemaphoreType.DMA((7,))],
        compiler_params=_cp(),
    )(v)


def _row_tile(rows, cap=1040):
    return max(tt for tt in range(16, cap + 1, 16) if rows % tt == 0)


def _add2(a, b, *, name):
    nq, rows, cols = a.shape
    tm = _row_tile(rows)

    def body(a_ref, b_ref, o_ref):
        o_ref[...] = (a_ref[...].astype(F32) + b_ref[...].astype(F32)).astype(BF16)

    spec = pl.BlockSpec((1, tm, cols), lambda q, i: (q, i, 0))
    return _pcall(body, name=name, out_shape=_sds(a.shape, BF16), grid=(nq, rows // tm),
                  in_specs=[spec, spec], out_specs=spec, compiler_params=_cp())(a, b)


def _add4(own, recv, *, name, rows_apart=False):
    rows, cols = own.shape
    tm = _row_tile(rows, 256 if rows_apart else 1040)

    def body(o_ref, r_ref, g_ref):
        tot = ((o_ref[...].astype(F32) + r_ref[0].astype(F32)) + r_ref[1].astype(F32)) + r_ref[2].astype(F32)
        if rows_apart:
            g_ref[:, 0, :] = tot
        else:
            g_ref[...] = tot

    if rows_apart:
        out_shape, out_spec = _sds((rows, 1, cols), F32), pl.BlockSpec((tm, 1, cols), lambda i: (i, 0, 0))
    else:
        out_shape, out_spec = _sds((rows, cols), F32), pl.BlockSpec((tm, cols), lambda i: (i, 0))
    return _pcall(
        body, name=name, out_shape=out_shape, grid=(rows // tm,),
        in_specs=[pl.BlockSpec((tm, cols), lambda i: (i, 0)), pl.BlockSpec((3, tm, cols), lambda i: (0, i, 0))],
        out_specs=out_spec, compiler_params=_cp(),
    )(own, recv)


def _mm(a, b, *, m, n, k, name, ta=False, tb=False, b_off=(0, 0), res=None, out_dtype=F32, tm=512, tn=1024, tk=1024,
        comm=None):
    tm, tn, tk = min(tm, m), min(tn, n), min(tk, k)
    assert m % tm == 0 and n % tn == 0 and k % tk == 0, (name, m, n, k, tm, tn, tk)
    nk = k // tk
    grid = (m // tm, n // tn, nk)
    bo0, bo1 = b_off
    if ta:
        a_spec = pl.BlockSpec((tk, tm), lambda i, j, kk: (kk, i))
    else:
        a_spec = pl.BlockSpec((tm, tk), lambda i, j, kk: (i, kk))
    if tb:
        b_spec = pl.BlockSpec((tn, tk), lambda i, j, kk: (j + bo0, kk + bo1))
    else:
        b_spec = pl.BlockSpec((tk, tn), lambda i, j, kk: (kk + bo0, j + bo1))
    o_spec = pl.BlockSpec((tm, tn), lambda i, j, kk: (i, j))
    dn = (((0 if ta else 1,), (1 if tb else 0,)), ((), ()))
    has_res = res is not None
    c_ins, c_in_specs, c_out_shape, c_scratch = _comm_lists(comm)
    n_in = 3 if has_res else 2
    n_scr = 1 if nk > 1 else 0

    def body(*refs):
        _, ins, cin, outs, cout, scr, csem = _split_refs(refs, 0, n_in, len(c_ins), 1, len(c_ins), n_scr)
        a_ref, b_ref = ins[0], ins[1]
        r_ref = ins[2] if has_res else None
        o_ref = outs[0]
        step = (pl.program_id(0) * grid[1] + pl.program_id(1)) * nk + pl.program_id(2)
        _run_comm(comm, cin, cout, csem, step, grid[0] * grid[1] * nk, "start")
        p = lax.dot_general(a_ref[...].astype(BF16), b_ref[...].astype(BF16), dn, preferred_element_type=F32)

        def finish(acc):
            if has_res:
                acc = acc + r_ref[...]
            o_ref[...] = acc.astype(out_dtype)

        if nk == 1:
            finish(p)
        else:
            acc_ref = scr[0]
            kk = pl.program_id(2)

            @pl.when(kk == 0)
            def _():
                acc_ref[...] = p

            @pl.when(kk > 0)
            def _():
                acc_ref[...] += p

            @pl.when(kk == nk - 1)
            def _():
                finish(acc_ref[...])

        _run_comm(comm, cin, cout, csem, step, grid[0] * grid[1] * nk, "end")

    ins = [a, b] + ([res] if has_res else [])
    in_specs = [a_spec, b_spec] + ([o_spec] if has_res else [])
    sem = ("arbitrary",) * 3 if comm is not None else ("parallel", "parallel", "arbitrary")
    outs = _pcall(
        body, name=name, out_shape=tuple([_sds((m, n), out_dtype)] + c_out_shape), grid=grid,
        in_specs=in_specs + c_in_specs, out_specs=tuple([o_spec] + [HBM_SPEC] * len(c_ins)),
        scratch_shapes=([pltpu.VMEM((tm, tn), F32)] if nk > 1 else []) + c_scratch,
        compiler_params=_cp(dimension_semantics=sem),
    )(*(ins + c_ins))
    return outs[0] if comm is None else (outs[0], list(outs[1:]))


SLAB_WIN = 1280
SLAB_TAIL = 48


def _z_gathered(h1, own_slab, order, *, t, d, n_pad, rows, comm=None, tm=1024):
    rp = own_slab.shape[0]
    n_slab = N_DEV
    tm = min(tm, t)
    ni = t // tm
    sh = rows - SLAB_WIN
    assert sh >= 0 and (n_slab - 1) * sh <= 128
    c_ins, c_in_specs, c_out_shape, c_scratch = _comm_lists(comm)
    n_steps = n_slab * ni

    def body(*refs):
        pre, ins, cin, outs, cout, scr, csem = _split_refs(refs, 1, 2, len(c_ins), 2, len(c_ins), 5)
        order_ref = pre[0]
        h_ref, own_ref = ins
        o_ref, slabs_ref = outs
        slab_vm, send_sems, recv_sems, local_sem, load_sem = scr
        p, i = pl.program_id(0), pl.program_id(1)
        step = p * ni + i
        x, y, c = _mesh_pos()
        me, sibling = (x, y, c), (x, y, 1 - c)
        chips = [(1 - x, y), (x, 1 - y), (1 - x, 1 - y)]

        def slab(dev):
            return slabs_ref.at[4 * dev[0] + 2 * dev[1] + dev[2]]

        def copy(k, block, to, src=None):
            dst = slab(block)
            return pltpu.make_async_remote_copy(
                src_ref=dst if src is None else src, dst_ref=dst, send_sem=send_sems.at[k], recv_sem=recv_sems.at[k],
                device_id=to, device_id_type=MESH)

        mine = pltpu.make_async_copy(own_ref, slab(me), local_sem)
        first = [copy(0, me, sibling, src=own_ref)] + [copy(1 + j, me, (*chip, c), src=own_ref) for j, chip in enumerate(chips)]
        passed = [copy(4 + j, (*chip, c), sibling) for j, chip in enumerate(chips)]

        @pl.when(step == 0)
        def _():
            mine.start()
            for cp in first:
                cp.start()

        _run_comm(comm, cin, cout, csem, step, n_steps, "start")

        def load(src):
            cp = pltpu.make_async_copy(src, slab_vm, load_sem)
            cp.start()
            cp.wait()

        @pl.when(i == 0)
        def _():
            @pl.when(p == 0)
            def _():
                load(own_ref)

            @pl.when(p == 1)
            def _():
                copy(0, sibling, me).wait_recv()
                load(slab(sibling))

            for j, chip in enumerate(chips):
                @pl.when(p == 2 + 2 * j)
                def _():
                    copy(1 + j, (*chip, c), me).wait_recv()
                    passed[j].start()
                    load(slab((*chip, c)))

                @pl.when(p == 3 + 2 * j)
                def _():
                    copy(4 + j, (*chip, 1 - c), me).wait_recv()
                    load(slab((*chip, 1 - c)))

        k = order_ref[p]
        y_k = lax.dot_general(h_ref[...], slab_vm[...], (((1,), (1,)), ((), ())), preferred_element_type=F32)
        o_ref[...] = pltpu.roll(y_k[:, 0:SLAB_WIN], k * sh, axis=1).astype(BF16)

        @pl.when(step == n_steps - 1)
        def _():
            for cp in first + passed:
                cp.wait_send()
            mine.wait()

        _run_comm(comm, cin, cout, csem, step, n_steps, "end")

    outs = _pcall(
        body, name="mm_z", out_shape=tuple([_sds((t, n_pad), BF16), _sds((n_slab, rp, d), BF16)] + c_out_shape),
        grid_spec=pltpu.PrefetchScalarGridSpec(
            num_scalar_prefetch=1, grid=(n_slab, ni),
            in_specs=[pl.BlockSpec((tm, d), lambda p, i, order: (i, 0)), HBM_SPEC] + c_in_specs,
            out_specs=tuple([pl.BlockSpec((tm, SLAB_WIN), lambda p, i, order: (i, order[p])), HBM_SPEC] + [HBM_SPEC] * len(c_ins)),
            scratch_shapes=[pltpu.VMEM((rp, d), BF16), pltpu.SemaphoreType.DMA((7,)), pltpu.SemaphoreType.DMA((7,)),
                            pltpu.SemaphoreType.DMA, pltpu.SemaphoreType.DMA] + c_scratch),
        compiler_params=_cp(dimension_semantics=("arbitrary", "arbitrary")),
    )(order, h1, own_slab, *c_ins)
    return outs[0], outs[1], list(outs[2:])


def _z_fix(h1, slabs, z, *, t, d, rows, tm=4096):
    n_slab, rp, _ = slabs.shape
    tm = min(tm, t)
    sh = rows - SLAB_WIN
    tail_lo = rp - SLAB_TAIL
    assert rows - (n_slab - 1) * sh >= tail_lo and rp % SLAB_TAIL == 0

    def body(h_ref, t_ref, z_ref, o_ref):
        k = pl.program_id(0) + 1
        p = lax.dot_general(h_ref[...], t_ref[0], (((1,), (1,)), ((), ())), preferred_element_type=F32)
        pp = jnp.concatenate([p, jnp.zeros((tm, 128 - SLAB_TAIL), F32)], axis=1)
        pr = pltpu.roll(pp, 128 - (rows - tail_lo) + k * sh, axis=1)
        lane = lax.broadcasted_iota(jnp.int32, (tm, 128), 1)
        o_ref[...] = jnp.where(lane < k * sh, pr.astype(BF16), z_ref[...])

    blk = pl.BlockSpec((tm, 128), lambda b, i: (i, (SLAB_WIN // 128) * (b + 1)))
    return _pcall(
        body, name="mm_z_fix", out_shape=_sds(z.shape, z.dtype), grid=(n_slab - 1, t // tm),
        in_specs=[pl.BlockSpec((tm, d), lambda b, i: (i, 0)),
                  pl.BlockSpec((1, SLAB_TAIL, d), lambda b, i: (b, tail_lo // SLAB_TAIL, 0)), blk],
        out_specs=blk, input_output_aliases={2: 0}, compiler_params=_cp(),
    )(h1, slabs, z)


def _slab_window(a1_ref, a2_ref, k, sh):
    w = jnp.concatenate([a1_ref[...], a2_ref[...]], axis=1)
    width = SLAB_WIN + 128
    return pltpu.roll(w, width - k * sh, axis=1)


def _g_win_slabs(dz, h1, my_c, *, t, d, n_slab, rows, rp, tk=2048):
    sh = rows - SLAB_WIN
    tk = min(tk, t)
    nk = t // tk
    width = SLAB_WIN + 128
    half = n_slab // 2

    def slab_of(p, c):
        return 2 * (p % half) + jnp.where(p < half, 1 - c, c)

    def body(c_ref, a1_ref, a2_ref, h_ref, keep_ref, land_ref, acc, stage, send_sems, recv_sems):
        p, kk = pl.program_id(0), pl.program_id(1)
        x, y, c = _mesh_pos()
        ws = _slab_window(a1_ref, a2_ref, slab_of(p, c), sh)
        prod = lax.dot_general(ws, h_ref[...], (((0,), (0,)), ((), ())), preferred_element_type=F32)

        def push(q):
            return pltpu.make_async_remote_copy(
                src_ref=stage.at[q % 2], dst_ref=land_ref.at[q], send_sem=send_sems.at[q], recv_sem=recv_sems.at[q],
                device_id=(x, y, 1 - c), device_id_type=MESH)

        @pl.when(kk == 0)
        def _():
            acc[...] = prod

        @pl.when(kk > 0)
        def _():
            acc[...] += prod

        @pl.when(kk == nk - 1)
        def _():
            res = acc[0:rp, :].astype(BF16)
            for q in range(half):
                @pl.when(p == q)
                def _():
                    if q >= 2:
                        push(q - 2).wait_send()
                    stage[q % 2] = res
                    push(q).start()

            @pl.when(p >= half)
            def _():
                keep_ref[0] = res

        @pl.when((p == n_slab - 1) & (kk == nk - 1))
        def _():
            for q in range(max(half - 2, 0), half):
                push(q).wait_send()
            for q in range(half):
                push(q).wait_recv()

    outs = _pcall(
        body, name="mm_g_win", out_shape=(_sds((half, rp, d), BF16), _sds((half, rp, d), BF16)),
        grid_spec=pltpu.PrefetchScalarGridSpec(
            num_scalar_prefetch=1, grid=(n_slab, nk),
            in_specs=[pl.BlockSpec((tk, SLAB_WIN), lambda p, kk, c: (kk, slab_of(p, c[0]))),
                      pl.BlockSpec((tk, 128), lambda p, kk, c: (kk, (SLAB_WIN // 128) * (slab_of(p, c[0]) + 1))),
                      pl.BlockSpec((tk, d), lambda p, kk, c: (kk, 0))],
            out_specs=(pl.BlockSpec((1, rp, d), lambda p, kk, c: (jnp.maximum(p - half, 0), 0, 0)), HBM_SPEC),
            scratch_shapes=[pltpu.VMEM((width, d), F32), pltpu.VMEM((2, rp, d), BF16),
                            pltpu.SemaphoreType.DMA((half,)), pltpu.SemaphoreType.DMA((half,))]),
        compiler_params=_cp(dimension_semantics=("arbitrary", "arbitrary")),
    )(my_c.reshape(1).astype(jnp.int32), dz, dz, h1)
    return outs[0], outs[1]


def _dh1_from_slabs(dz, slabs, *, t, d, rows, comm=None, tm=1024):
    n_slab, rp, _ = slabs.shape
    sh = rows - SLAB_WIN
    width = SLAB_WIN + 128
    c_ins, c_in_specs, c_out_shape, c_scratch = _comm_lists(comm)
    grid = (t // tm, n_slab)

    def body(*refs):
        _, ins, cin, outs, cout, scr, csem = _split_refs(refs, 0, 3, len(c_ins), 1, len(c_ins), 1)
        a1_ref, a2_ref, s_ref = ins
        o_ref, acc = outs[0], scr[0]
        k = pl.program_id(1)
        step = pl.program_id(0) * n_slab + k
        _run_comm(comm, cin, cout, csem, step, grid[0] * n_slab, "start")
        ws = _slab_window(a1_ref, a2_ref, k, sh)
        s_end = jnp.concatenate([s_ref[0, SLAB_WIN:rp, :], jnp.zeros((width - rp, d), BF16)], axis=0)
        p = (jnp.dot(ws[:, 0:SLAB_WIN], s_ref[0, 0:SLAB_WIN, :], preferred_element_type=F32)
             + jnp.dot(ws[:, SLAB_WIN:width], s_end, preferred_element_type=F32))

        @pl.when(k == 0)
        def _():
            acc[...] = p

        @pl.when(k > 0)
        def _():
            acc[...] += p

        @pl.when(k == n_slab - 1)
        def _():
            o_ref[...] = acc[...]

        _run_comm(comm, cin, cout, csem, step, grid[0] * n_slab, "end")

    outs = _pcall(
        body, name="mm_dh1", out_shape=tuple([_sds((t, d), F32)] + c_out_shape), grid=grid,
        in_specs=[pl.BlockSpec((tm, SLAB_WIN), lambda i, k: (i, k)),
                  pl.BlockSpec((tm, 128), lambda i, k: (i, (SLAB_WIN // 128) * (k + 1))),
                  pl.BlockSpec((1, rp, d), lambda i, k: (k, 0, 0))] + c_in_specs,
        out_specs=tuple([pl.BlockSpec((tm, d), lambda i, k: (i, 0))] + [HBM_SPEC] * len(c_ins)),
        scratch_shapes=[pltpu.VMEM((tm, d), F32)] + c_scratch,
        compiler_params=_cp(dimension_semantics=("arbitrary", "arbitrary")),
    )(dz, dz, slabs, *c_ins)
    return outs[0], list(outs[1:])


def _rms_fwd(x, g, *, name, tm=512):
    t, d = x.shape
    tm = min(tm, t)

    def body(x_ref, g_ref, h_ref):
        xv = x_ref[...]
        r = lax.rsqrt(jnp.mean(xv * xv, axis=-1, keepdims=True) + EPS)
        h_ref[...] = (xv * r * g_ref[...]).astype(BF16)

    return _pcall(
        body, name=name, out_shape=_sds((t, d), BF16), grid=(t // tm,),
        in_specs=[pl.BlockSpec((tm, d), lambda i: (i, 0)), pl.BlockSpec((1, d), lambda i: (0, 0))],
        out_specs=pl.BlockSpec((tm, d), lambda i: (i, 0)), compiler_params=_cp(),
    )(x, g)


def _rms_bwd(dh, x, g, res, *, name, tm=512):
    t, d = x.shape
    tm = min(tm, t)
    has_res = res is not None

    def body(*refs):
        dh_ref, x_ref, g_ref = refs[:3]
        r_ref = refs[3] if has_res else None
        dx_ref, dg_ref = refs[-2], refs[-1]
        xv = x_ref[...]
        r = lax.rsqrt(jnp.mean(xv * xv, axis=-1, keepdims=True) + EPS)
        xh = xv * r
        dhv = dh_ref[...]
        gd = dhv * g_ref[...]
        dx = r * (gd - xh * jnp.mean(gd * xh, axis=-1, keepdims=True))
        if has_res:
            dx = dx + r_ref[...]
        dx_ref[...] = dx

        @pl.when(pl.program_id(0) == 0)
        def _():
            dg_ref[...] = jnp.zeros_like(dg_ref)

        dg_ref[...] += jnp.sum(dhv * xh, axis=0, keepdims=True)

    row = pl.BlockSpec((tm, d), lambda i: (i, 0))
    vec = pl.BlockSpec((1, d), lambda i: (0, 0))
    return _pcall(
        body, name=name, out_shape=(_sds((t, d), F32), _sds((1, d), F32)), grid=(t // tm,),
        in_specs=[row, row, vec] + ([row] if has_res else []), out_specs=(row, vec),
        compiler_params=_cp(dimension_semantics=("arbitrary",)),
    )(*([dh, x, g] + ([res] if has_res else [])))


def _conv_fwd(z, conv_w, conv_b, *, nb, s, d, tc=256):
    nc = d // tc

    def body(cb_ref, cc_ref, cv_ref, w_ref, b_ref, y_ref):
        u = cc_ref[...].astype(F32) * cv_ref[...].astype(F32)
        row = lax.broadcasted_iota(jnp.int32, u.shape, 0)
        u1 = jnp.where(row >= 1, pltpu.roll(u, 1, axis=0), 0.0)
        u2 = jnp.where(row >= 2, pltpu.roll(u, 2, axis=0), 0.0)
        w = w_ref[...]
        y = w[2:3, :] * u + w[1:2, :] * u1 + w[0:1, :] * u2 + b_ref[...]
        y_ref[...] = (cb_ref[...].astype(F32) * y).astype(BF16)

    def part(p):
        return pl.BlockSpec((s, tc), lambda b, j: (b, p * nc + j))

    return _pcall(
        body, name="conv_fwd", out_shape=_sds((nb * s, d), BF16), grid=(nb, nc),
        in_specs=[part(0), part(1), part(2), pl.BlockSpec((3, tc), lambda b, j: (0, j)), pl.BlockSpec((1, tc), lambda b, j: (0, j))],
        out_specs=pl.BlockSpec((s, tc), lambda b, j: (b, j)), compiler_params=_cp(),
    )(z, z, z, conv_w, conv_b)


def _conv_bwd(z, dy, conv_w, conv_b, dz, *, nb, s, d, tc=256):
    nc = d // tc

    def body(cb_ref, cc_ref, cv_ref, dy_ref, w_ref, b_ref, dz_in, dz_ref, dw_ref, db_ref, stash):
        b_i, p = pl.program_id(1), pl.program_id(2)

        @pl.when(p == 0)
        def _():
            cc, cv = cc_ref[...].astype(F32), cv_ref[...].astype(F32)
            u = cc * cv
            n = u.shape[0]
            row = lax.broadcasted_iota(jnp.int32, u.shape, 0)
            u1 = jnp.where(row >= 1, pltpu.roll(u, 1, axis=0), 0.0)
            u2 = jnp.where(row >= 2, pltpu.roll(u, 2, axis=0), 0.0)
            w = w_ref[...]
            y = w[2:3, :] * u + w[1:2, :] * u1 + w[0:1, :] * u2 + b_ref[...]
            dyv = dy_ref[...]
            dconv = dyv * cb_ref[...].astype(F32)
            g1 = jnp.where(row < n - 1, pltpu.roll(dconv, n - 1, axis=0), 0.0)
            g2 = jnp.where(row < n - 2, pltpu.roll(dconv, n - 2, axis=0), 0.0)
            du = w[2:3, :] * dconv + w[1:2, :] * g1 + w[0:1, :] * g2
            stash[0] = (dyv * y).astype(BF16)
            stash[1] = (du * cv).astype(BF16)
            stash[2] = (du * cc).astype(BF16)
            dws = [jnp.sum(dconv * u2, axis=0, keepdims=True), jnp.sum(dconv * u1, axis=0, keepdims=True),
                   jnp.sum(dconv * u, axis=0, keepdims=True)]
            db = jnp.sum(dconv, axis=0, keepdims=True)

            @pl.when(b_i == 0)
            def _():
                for r in range(3):
                    dw_ref[r:r + 1, :] = dws[r]
                db_ref[...] = db

            @pl.when(b_i > 0)
            def _():
                for r in range(3):
                    dw_ref[r:r + 1, :] += dws[r]
                db_ref[...] += db

        dz_ref[...] = stash[p]

    return _pcall(
        body, name="conv_bwd",
        out_shape=(_sds(dz.shape, dz.dtype), _sds((3, d), F32), _sds((1, d), F32)), grid=(nc, nb, 3),
        in_specs=[
            pl.BlockSpec((s, tc), lambda j, b, p: (b, j)), pl.BlockSpec((s, tc), lambda j, b, p: (b, nc + j)),
            pl.BlockSpec((s, tc), lambda j, b, p: (b, 2 * nc + j)), pl.BlockSpec((s, tc), lambda j, b, p: (b, j)),
            pl.BlockSpec((3, tc), lambda j, b, p: (0, j)), pl.BlockSpec((1, tc), lambda j, b, p: (0, j)),
            HBM_SPEC,
        ],
        out_specs=(pl.BlockSpec((s, tc), lambda j, b, p: (b, p * nc + j)),
                   pl.BlockSpec((3, tc), lambda j, b, p: (0, j)), pl.BlockSpec((1, tc), lambda j, b, p: (0, j))),
        scratch_shapes=[pltpu.VMEM((3, s, tc), BF16)],
        input_output_aliases={6: 0},
        compiler_params=_cp(dimension_semantics=("arbitrary", "arbitrary", "arbitrary")),
    )(z, z, z, dy, conv_w, conv_b, dz)


def _head_rms(xv, g):
    r = lax.rsqrt(jnp.mean(xv * xv, axis=-1, keepdims=True) + EPS)
    return xv * r * g


def _head_rms_bwd(xv, g, dy):
    r = lax.rsqrt(jnp.mean(xv * xv, axis=-1, keepdims=True) + EPS)
    xh = xv * r
    gd = dy * g
    dx = r * (gd - xh * jnp.mean(gd * xh, axis=-1, keepdims=True))
    return dx, jnp.sum(dy * xh, axis=0, keepdims=True)


def _fox_prep(z, q_g, k_g, *, t, d, tm=256):
    nh = d // HD

    def body(z_ref, qg_ref, kg_ref, q_ref, k_ref, v_ref):
        qg, kg = qg_ref[...], kg_ref[...]
        for h in range(nh):
            c0 = h * HD
            q_ref[:, c0:c0 + HD] = _head_rms(z_ref[:, c0:c0 + HD].astype(F32), qg).astype(BF16)
            k_ref[:, c0:c0 + HD] = _head_rms(z_ref[:, d + c0:d + c0 + HD].astype(F32), kg).astype(BF16)
        v_ref[...] = z_ref[:, 2 * d:3 * d]

    o = pl.BlockSpec((tm, d), lambda i: (i, 0))
    g = pl.BlockSpec((1, HD), lambda i: (0, 0))
    return _pcall(
        body, name="fox_prep", out_shape=(_sds((t, d), BF16),) * 3, grid=(t // tm,),
        in_specs=[pl.BlockSpec((tm, 3 * d), lambda i: (i, 1)), g, g], out_specs=(o, o, o), compiler_params=_cp(),
    )(z, q_g, k_g)


def _lane_cumsum(v, reverse=False):
    n = v.shape[1]
    lane = lax.broadcasted_iota(jnp.int32, v.shape, 1)
    sh = 1
    while sh < n:
        if reverse:
            v = v + jnp.where(lane < n - sh, pltpu.roll(v, n - sh, axis=1), 0.0)
        else:
            v = v + jnp.where(lane >= sh, pltpu.roll(v, sh, axis=1), 0.0)
        sh *= 2
    return v


def _fox_gates(z, f_bias_col, *, nb, s, nh):
    def body(z_ref, b_ref, c_ref):
        ff = z_ref[...].T[0:nh, :] + b_ref[...]
        lf = jnp.minimum(ff, 0.0) - jnp.log(1.0 + jnp.exp(-jnp.abs(ff)))
        c_ref[0] = _lane_cumsum(lf) * SQRT_HD

    return _pcall(
        body, name="fox_gates", out_shape=_sds((nb, nh, s), F32), grid=(nb,),
        in_specs=[pl.BlockSpec((s, HD), lambda b: (b, 0)), pl.BlockSpec((nh, 1), lambda b: (0, 0))],
        out_specs=pl.BlockSpec((1, nh, s), lambda b: (b, 0, 0)), compiler_params=_cp(),
    )(z, f_bias_col)


def _fox_gates_bwd(z, f_bias_col, row_sums, col_sums, dz, *, nb, s, nh, ff_block):
    def body(z_ref, b_ref, rs_ref, cs_ref, dz_in, dz_ref, db_ref, dc_sc):
        b_i, h = pl.program_id(0), pl.program_id(1)
        dc_sc[pl.ds(h, 1), :] = (rs_ref[...] - cs_ref[...]).T[0:1, :]

        @pl.when(h == nh - 1)
        def _():
            ff = z_ref[...].T[0:nh, :] + b_ref[...]
            dlf = _lane_cumsum(dc_sc[...], reverse=True)
            dff = dlf * (1.0 / (1.0 + jnp.exp(ff)))
            pad = jnp.concatenate([dff, jnp.zeros((HD - nh, s), F32)], axis=0)
            dz_ref[...] = pad.T.astype(BF16)
            dbv = jnp.sum(dff, axis=1, keepdims=True)

            @pl.when(b_i == 0)
            def _():
                db_ref[...] = dbv

            @pl.when(b_i > 0)
            def _():
                db_ref[...] += dbv

    hs = pl.BlockSpec((s, HD), lambda b, h: (b, h))
    return _pcall(
        body, name="fox_gates_bwd", out_shape=(_sds(dz.shape, dz.dtype), _sds((nh, 1), F32)), grid=(nb, nh),
        in_specs=[pl.BlockSpec((s, HD), lambda b, h: (b, 0)), pl.BlockSpec((nh, 1), lambda b, h: (0, 0)), hs, hs, HBM_SPEC],
        out_specs=(pl.BlockSpec((s, HD), lambda b, h: (b, ff_block)), pl.BlockSpec((nh, 1), lambda b, h: (0, 0))),
        scratch_shapes=[pltpu.VMEM((nh, s), F32)],
        input_output_aliases={4: 0}, compiler_params=_cp(dimension_semantics=("arbitrary", "arbitrary")),
    )(z, f_bias_col, row_sums, col_sums, dz)


def _fox_pairs(nq, key_major):
    if key_major:
        pairs = [(i, j) for j in range(nq) for i in range(j, nq)]
    else:
        pairs = [(i, j) for i in range(nq) for j in range(i + 1)]
    return jnp.array([p[0] for p in pairs], jnp.int32), jnp.array([p[1] for p in pairs], jnp.int32)


def _with_ones(x):
    return jnp.concatenate([x, jnp.ones_like(x)], axis=1)


def _fox_specs(nb, nh, nq, tq, tk, hp):
    qs = pl.BlockSpec((tq, hp * HD), lambda b, h, p, it, jt: (b * nq + it[p], h))
    ks = pl.BlockSpec((tk, hp * HD), lambda b, h, p, it, jt: (b * nq + jt[p], h))
    cs = pl.BlockSpec((hp, 1, tk), lambda b, h, p, it, jt: (b * (nh // hp) + h, 0, jt[p]))
    return qs, ks, cs


def _fox_raw(qv, kv, ckv, diag, tq, tk):
    sc = lax.dot_general(qv, kv, (((1,), (1,)), ((), ())), preferred_element_type=F32) - ckv
    if diag:
        rows = lax.broadcasted_iota(jnp.int32, (tq, tk), 0)
        cols = lax.broadcasted_iota(jnp.int32, (tq, tk), 1)
        sc = jnp.where(rows >= cols, sc, NEG)
    return sc


def _fox_probs(qv, kv, ckv, lse_col, diag, tq, tk):
    sc = _fox_raw(qv, kv, ckv, diag, tq, tk)
    return jnp.exp2(sc * EXP2_C - lse_col * LOG2E)


def _fox_call(body, *, name, ins, in_specs, out_shape, out_specs, scratch, grid, tables, comm):
    c_ins, c_in_specs, c_out_shape, c_scratch = _comm_lists(comm)
    n_in, n_out, n_scr, nc = len(ins), len(out_shape), len(scratch), len(c_ins)
    n_steps = grid[0] * grid[1] * grid[2]

    def wrapped(*refs):
        pre, rin, cin, rout, cout, scr, csem = _split_refs(refs, 2, n_in, nc, n_out, nc, n_scr)
        step = (pl.program_id(0) * grid[1] + pl.program_id(1)) * grid[2] + pl.program_id(2)
        _run_comm(comm, cin, cout, csem, step, n_steps, "start")
        body(*pre, *rin, *rout, *scr)
        _run_comm(comm, cin, cout, csem, step, n_steps, "end")

    sem = ("arbitrary",) * 3 if comm is not None else ("parallel", "parallel", "arbitrary")
    outs = _pcall(
        wrapped, name=name, out_shape=tuple(list(out_shape) + c_out_shape),
        grid_spec=pltpu.PrefetchScalarGridSpec(
            num_scalar_prefetch=2, grid=grid, in_specs=list(in_specs) + c_in_specs,
            out_specs=tuple(list(out_specs) + [HBM_SPEC] * nc), scratch_shapes=list(scratch) + c_scratch),
        compiler_params=_cp(dimension_semantics=sem),
    )(*tables, *ins, *c_ins)
    return list(outs[:n_out]), list(outs[n_out:])


def _fox_fwd(q, k, v, cks, *, nb, s, nh, tq=512, comm=None):
    tk = tq
    nq = s // tq
    t = nb * s
    hp = FOX_HP
    it, jt = _fox_pairs(nq, key_major=False)

    def body(it_ref, jt_ref, q_ref, k_ref, v_ref, c_ref, o_ref, lse_ref, m_sc, acc_sc):
        p_id = pl.program_id(2)
        i, j = it_ref[p_id], jt_ref[p_id]

        @pl.when(j == 0)
        def _():
            m_sc[...] = jnp.full_like(m_sc, NEG)
            acc_sc[...] = jnp.zeros_like(acc_sc)

        def step(diag):
            for hh in range(hp):
                cols = slice(hh * HD, (hh + 1) * HD)
                sc = _fox_raw(q_ref[:, cols], k_ref[:, cols], c_ref[hh], diag, tq, tk)
                m_old = m_sc[hh]
                m_new = jnp.maximum(m_old, jnp.max(sc, axis=-1, keepdims=True))
                alpha = jnp.exp2((m_old - m_new) * EXP2_C)
                p = jnp.exp2((sc - m_new) * EXP2_C).astype(BF16)
                acc = alpha * acc_sc[hh] + jnp.dot(p, _with_ones(v_ref[:, cols]), preferred_element_type=F32)
                if diag:
                    l = acc[:, HD:2 * HD]
                    o_ref[:, cols] = (acc[:, 0:HD] / l).astype(BF16)
                    lse_ref[:, cols] = m_new * (1.0 / SQRT_HD) + jnp.log(l)
                else:
                    acc_sc[hh] = acc
                    m_sc[hh] = m_new

        @pl.when(j < i)
        def _():
            step(False)

        @pl.when(j == i)
        def _():
            step(True)

    qs, ks, cs = _fox_specs(nb, nh, nq, tq, tk, hp)
    (o, lse), c_out = _fox_call(
        body, name="fox_fwd", ins=[q, k, v, cks], in_specs=[qs, ks, ks, cs],
        out_shape=[_sds((t, nh * HD), BF16), _sds((t, nh * HD), F32)], out_specs=[qs, qs],
        scratch=[pltpu.VMEM((hp, tq, 1), F32), pltpu.VMEM((hp, tq, 2 * HD), F32)],
        grid=(nb, nh // hp, int(it.shape[0])), tables=(it, jt), comm=comm)
    return o, lse, c_out


def _fox_bwd_dq(q, k, v, cks, do, o, lse, *, nb, s, nh, tq=512, comm=None):
    tk = tq
    nq = s // tq
    t = nb * s
    hp = FOX_HP
    it, jt = _fox_pairs(nq, key_major=False)

    def body(it_ref, jt_ref, q_ref, k_ref, v_ref, c_ref, do_ref, o_ref, lse_ref, dq_ref, rs_ref, acc_sc, dl_sc):
        p_id = pl.program_id(2)
        i, j = it_ref[p_id], jt_ref[p_id]

        @pl.when(j == 0)
        def _():
            acc_sc[...] = jnp.zeros_like(acc_sc)
            for hh in range(hp):
                cols = slice(hh * HD, (hh + 1) * HD)
                dl_sc[hh] = jnp.sum(do_ref[:, cols] * o_ref[:, cols].astype(F32), axis=-1, keepdims=True)

        def step(diag):
            for hh in range(hp):
                cols = slice(hh * HD, (hh + 1) * HD)
                kv = k_ref[:, cols]
                p = _fox_probs(q_ref[:, cols], kv, c_ref[hh], lse_ref[:, hh * HD:hh * HD + 1], diag, tq, tk)
                dp = lax.dot_general(do_ref[:, cols].astype(BF16), v_ref[:, cols], (((1,), (1,)), ((), ())),
                                     preferred_element_type=F32)
                ds = (p * (dp - dl_sc[hh])).astype(BF16)
                acc = acc_sc[hh] + jnp.dot(ds, _with_ones(kv), preferred_element_type=F32)
                if diag:
                    dq_ref[:, cols] = acc[:, 0:HD] * (1.0 / SQRT_HD)
                    rs_ref[:, cols] = acc[:, HD:2 * HD]
                else:
                    acc_sc[hh] = acc

        @pl.when(j < i)
        def _():
            step(False)

        @pl.when(j == i)
        def _():
            step(True)

    qs, ks, cs = _fox_specs(nb, nh, nq, tq, tk, hp)
    (dq, rs), c_out = _fox_call(
        body, name="fox_bwd_dq", ins=[q, k, v, cks, do, o, lse], in_specs=[qs, ks, ks, cs, qs, qs, qs],
        out_shape=[_sds((t, nh * HD), F32), _sds((t, nh * HD), F32)], out_specs=[qs, qs],
        scratch=[pltpu.VMEM((hp, tq, 2 * HD), F32), pltpu.VMEM((hp, tq, 1), F32)],
        grid=(nb, nh // hp, int(it.shape[0])), tables=(it, jt), comm=comm)
    return dq, rs, c_out


def _fox_bwd_dkv(q, k, v, cks, do, o, lse, *, nb, s, nh, tq=512, comm=None):
    tk = tq
    nq = s // tq
    t = nb * s
    hp = FOX_HP
    it, jt = _fox_pairs(nq, key_major=True)

    def body(it_ref, jt_ref, q_ref, k_ref, v_ref, c_ref, do_ref, o_ref, lse_ref, dk_ref, dv_ref, cs_ref, dk_sc, dv_sc):
        p_id = pl.program_id(2)
        i, j = it_ref[p_id], jt_ref[p_id]

        def step(diag):
            for hh in range(hp):
                cols = slice(hh * HD, (hh + 1) * HD)
                qv = q_ref[:, cols]
                p = _fox_probs(qv, k_ref[:, cols], c_ref[hh], lse_ref[:, hh * HD:hh * HD + 1], diag, tq, tk)
                dov = do_ref[:, cols]
                dob = dov.astype(BF16)
                dvp = lax.dot_general(p.astype(BF16), dob, (((0,), (0,)), ((), ())), preferred_element_type=F32)
                dp = lax.dot_general(dob, v_ref[:, cols], (((1,), (1,)), ((), ())), preferred_element_type=F32)
                delta = jnp.sum(dov * o_ref[:, cols].astype(F32), axis=-1, keepdims=True)
                ds = (p * (dp - delta)).astype(BF16)
                dkp = lax.dot_general(ds, _with_ones(qv), (((0,), (0,)), ((), ())), preferred_element_type=F32)
                if diag:
                    dk_sc[hh] = dkp
                    dv_sc[hh] = dvp
                else:
                    dk_sc[hh] += dkp
                    dv_sc[hh] += dvp

        @pl.when(i == j)
        def _():
            step(True)

        @pl.when(i > j)
        def _():
            step(False)

        @pl.when(i == nq - 1)
        def _():
            for hh in range(hp):
                cols = slice(hh * HD, (hh + 1) * HD)
                dk_ref[:, cols] = dk_sc[hh, :, 0:HD] * (1.0 / SQRT_HD)
                cs_ref[:, cols] = dk_sc[hh, :, HD:2 * HD]
                dv_ref[:, cols] = dv_sc[hh].astype(BF16)

    qs, ks, cs = _fox_specs(nb, nh, nq, tq, tk, hp)
    (dk, dv, csum), c_out = _fox_call(
        body, name="fox_bwd_dkv", ins=[q, k, v, cks, do, o, lse], in_specs=[qs, ks, ks, cs, qs, qs, qs],
        out_shape=[_sds((t, nh * HD), F32), _sds((t, nh * HD), BF16), _sds((t, nh * HD), F32)], out_specs=[ks, ks, ks],
        scratch=[pltpu.VMEM((hp, tk, 2 * HD), F32), pltpu.VMEM((hp, tk, HD), F32)],
        grid=(nb, nh // hp, int(it.shape[0])), tables=(it, jt), comm=comm)
    return dk, dv, csum, c_out


def _fox_post(z, dqn, dkn, dv, q_g, k_g, dz, *, t, d, tm=256):
    nh = d // HD

    def body(z_ref, dq_ref, dk_ref, dv_ref, qg_ref, kg_ref, dz_in, dz_ref, dqg_ref, dkg_ref):
        qg, kg = qg_ref[...], kg_ref[...]
        aq = jnp.zeros((1, HD), F32)
        ak = jnp.zeros((1, HD), F32)
        for h in range(nh):
            c0 = h * HD
            dx, sg = _head_rms_bwd(z_ref[:, c0:c0 + HD].astype(F32), qg, dq_ref[:, c0:c0 + HD])
            dz_ref[:, c0:c0 + HD] = dx.astype(BF16)
            aq = aq + sg
            dx, sg = _head_rms_bwd(z_ref[:, d + c0:d + c0 + HD].astype(F32), kg, dk_ref[:, c0:c0 + HD])
            dz_ref[:, d + c0:d + c0 + HD] = dx.astype(BF16)
            ak = ak + sg
        dz_ref[:, 2 * d:3 * d] = dv_ref[...]

        @pl.when(pl.program_id(0) == 0)
        def _():
            dqg_ref[...] = aq
            dkg_ref[...] = ak

        @pl.when(pl.program_id(0) > 0)
        def _():
            dqg_ref[...] += aq
            dkg_ref[...] += ak

    o = pl.BlockSpec((tm, d), lambda i: (i, 0))
    g = pl.BlockSpec((1, HD), lambda i: (0, 0))
    z3 = pl.BlockSpec((tm, 3 * d), lambda i: (i, 1))
    return _pcall(
        body, name="fox_post", out_shape=(_sds(dz.shape, dz.dtype), _sds((1, HD), F32), _sds((1, HD), F32)), grid=(t // tm,),
        in_specs=[z3, o, o, o, g, g, HBM_SPEC], out_specs=(z3, g, g),
        input_output_aliases={6: 0}, compiler_params=_cp(dimension_semantics=("arbitrary",)),
    )(z, dqn, dkn, dv, q_g, k_g, dz)


def _xa_prep_kv(kv, k_g, *, rows, d):
    xd = d // XH

    def body(kv_ref, g_ref, k_ref, v_ref):
        g = g_ref[...]
        for h in range(XH):
            c0 = h * xd
            k_ref[:, c0:c0 + xd] = _head_rms(kv_ref[:, c0:c0 + xd], g).astype(BF16)
        v_ref[...] = kv_ref[:, d:2 * d].astype(BF16)

    return _pcall(body, name="xa_prep_kv", out_shape=(_sds((rows, d), BF16),) * 2, compiler_params=_cp())(kv, k_g)


def _xa_fwd(z, kn, vb, q_g, *, nb, s, d, nm, q_block, tq=512):
    xd = d // XH
    nq = s // tq

    def body(z_ref, k_ref, v_ref, g_ref, y_ref):
        g = g_ref[...]
        for h in range(XH):
            c0 = h * xd
            qn = _head_rms(z_ref[:, c0:c0 + xd].astype(F32), g).astype(BF16)
            sc = lax.dot_general(qn, k_ref[:, c0:c0 + xd], (((1,), (1,)), ((), ())), preferred_element_type=F32)
            sc = sc / math.sqrt(xd)
            p = jnp.exp(sc - jnp.max(sc, axis=-1, keepdims=True))
            p = p / jnp.sum(p, axis=-1, keepdims=True)
            y_ref[:, c0:c0 + xd] = jnp.dot(p.astype(BF16), v_ref[:, c0:c0 + xd], preferred_element_type=F32).astype(BF16)

    kvs = pl.BlockSpec((nm, d), lambda b, i: (b, 0))
    return _pcall(
        body, name="xa_fwd", out_shape=_sds((nb * s, d), BF16), grid=(nb, nq),
        in_specs=[pl.BlockSpec((tq, d), lambda b, i: (b * nq + i, q_block)), kvs, kvs, pl.BlockSpec((1, xd), lambda b, i: (0, 0))],
        out_specs=pl.BlockSpec((tq, d), lambda b, i: (b * nq + i, 0)), compiler_params=_cp(),
    )(z, kn, vb, q_g)


def _xa_bwd(z, kn, vb, q_g, dy, dz, *, nb, s, d, nm, q_block, tq=512):
    xd = d // XH
    nq = s // tq

    def body(z_ref, k_ref, v_ref, g_ref, dy_ref, dz_in, dz_ref, dk_ref, dv_ref, dg_ref):
        b_i, i = pl.program_id(0), pl.program_id(1)
        g = g_ref[...]

        @pl.when(i == 0)
        def _():
            dk_ref[...] = jnp.zeros_like(dk_ref)
            dv_ref[...] = jnp.zeros_like(dv_ref)

        @pl.when((i == 0) & (b_i == 0))
        def _():
            dg_ref[...] = jnp.zeros_like(dg_ref)

        for h in range(XH):
            c0 = h * xd
            xq = z_ref[:, c0:c0 + xd].astype(F32)
            qn = _head_rms(xq, g).astype(BF16)
            kh, vh = k_ref[:, c0:c0 + xd], v_ref[:, c0:c0 + xd]
            sc = lax.dot_general(qn, kh, (((1,), (1,)), ((), ())), preferred_element_type=F32) / math.sqrt(xd)
            p = jnp.exp(sc - jnp.max(sc, axis=-1, keepdims=True))
            p = p / jnp.sum(p, axis=-1, keepdims=True)
            dob = dy_ref[:, c0:c0 + xd].astype(BF16)
            dv_ref[:, c0:c0 + xd] += lax.dot_general(p.astype(BF16), dob, (((0,), (0,)), ((), ())), preferred_element_type=F32)
            dp = lax.dot_general(dob, vh, (((1,), (1,)), ((), ())), preferred_element_type=F32)
            ds = p * (dp - jnp.sum(p * dp, axis=-1, keepdims=True)) / math.sqrt(xd)
            dsb = ds.astype(BF16)
            dqn = jnp.dot(dsb, kh, preferred_element_type=F32)
            dk_ref[:, c0:c0 + xd] += lax.dot_general(dsb, qn, (((0,), (0,)), ((), ())), preferred_element_type=F32)
            dx, sg = _head_rms_bwd(xq, g, dqn)
            dz_ref[:, c0:c0 + xd] = dx.astype(BF16)
            dg_ref[...] += sg

    kvs = pl.BlockSpec((nm, d), lambda b, i: (b, 0))
    qs = pl.BlockSpec((tq, d), lambda b, i: (b * nq + i, q_block))
    gs = pl.BlockSpec((1, xd), lambda b, i: (0, 0))
    return _pcall(
        body, name="xa_bwd",
        out_shape=(_sds(dz.shape, dz.dtype), _sds((nb * nm, d), F32), _sds((nb * nm, d), F32), _sds((1, xd), F32)), grid=(nb, nq),
        in_specs=[qs, kvs, kvs, gs, pl.BlockSpec((tq, d), lambda b, i: (b * nq + i, 0)), HBM_SPEC],
        out_specs=(qs, kvs, kvs, gs), input_output_aliases={5: 0},
        compiler_params=_cp(dimension_semantics=("arbitrary", "arbitrary")),
    )(z, kn, vb, q_g, dy, dz)


def _xa_post_kv(kv, dkn, dv, k_g, *, rows, d):
    xd = d // XH

    def body(kv_ref, dk_ref, dv_ref, g_ref, dkv_ref, dg_ref):
        g = g_ref[...]
        acc = jnp.zeros((1, xd), F32)
        for h in range(XH):
            c0 = h * xd
            dx, sg = _head_rms_bwd(kv_ref[:, c0:c0 + xd], g, dk_ref[:, c0:c0 + xd])
            dkv_ref[:, c0:c0 + xd] = dx.astype(BF16)
            acc = acc + sg
        dkv_ref[:, d:2 * d] = dv_ref[...].astype(BF16)
        dg_ref[...] = acc

    return _pcall(body, name="xa_post_kv", out_shape=(_sds((rows, 2 * d), BF16), _sds((1, xd), F32)), compiler_params=_cp())(
        kv, dkn, dv, k_g)


def _sigmoid(v):
    return 1.0 / (1.0 + jnp.exp(-v))


def _branches_fwd(z, ys, ws, *, t, d, gate_block, tm=512, tn=512, comm=None):
    nj = d // tn
    c_ins, c_in_specs, c_out_shape, c_scratch = _comm_lists(comm)
    n_steps = (t // tm) * nj

    def body(*refs):
        _, ins, cin, outs, cout, _, csem = _split_refs(refs, 0, 9, len(c_ins), 2, len(c_ins), 0)
        ya_ref, yb_ref, yc_ref, wa_ref, wb_ref, wc_ref, ga_ref, gb_ref, gc_ref = ins
        m_ref, p_ref = outs
        step = pl.program_id(0) * nj + pl.program_id(1)
        _run_comm(comm, cin, cout, csem, step, n_steps, "start")
        acc = None
        for q, (y_ref, w_ref, g_ref) in enumerate(((ya_ref, wa_ref, ga_ref), (yb_ref, wb_ref, gb_ref), (yc_ref, wc_ref, gc_ref))):
            pr = jnp.dot(y_ref[...], w_ref[...], preferred_element_type=F32)
            p_ref[q] = pr.astype(BF16)
            term = _sigmoid(g_ref[...].astype(F32)) * pr
            acc = term if acc is None else acc + term
        m_ref[...] = acc.astype(BF16)
        _run_comm(comm, cin, cout, csem, step, n_steps, "end")

    y_spec = pl.BlockSpec((tm, d), lambda i, j: (i, 0))
    w_spec = pl.BlockSpec((d, tn), lambda i, j: (0, j))

    def gate(q):
        return pl.BlockSpec((tm, tn), lambda i, j: (i, (gate_block + q) * nj + j))

    outs = _pcall(
        body, name="branches_fwd", out_shape=tuple([_sds((t, d), BF16), _sds((3, t, d), BF16)] + c_out_shape), grid=(t // tm, nj),
        in_specs=[y_spec] * 3 + [w_spec] * 3 + [gate(0), gate(1), gate(2)] + c_in_specs,
        out_specs=tuple([pl.BlockSpec((tm, tn), lambda i, j: (i, j)), pl.BlockSpec((3, tm, tn), lambda i, j: (0, i, j))]
                        + [HBM_SPEC] * len(c_ins)),
        scratch_shapes=c_scratch, compiler_params=_cp(dimension_semantics=("arbitrary", "arbitrary")),
    )(*ys, *ws, z, z, z, *c_ins)
    return outs[0], outs[1], list(outs[2:])


def _gates_bwd(z, dm, proj, dz, *, t, d, gate_block, tm=512):
    def body(g_ref, dm_ref, p_ref, dz_in, dz_ref, da_ref, db_ref, dc_ref):
        q = pl.program_id(1)
        sg = _sigmoid(g_ref[...].astype(F32))
        dmv = dm_ref[...]
        dz_ref[...] = (dmv * p_ref[0].astype(F32) * sg * (1.0 - sg)).astype(BF16)
        dp = (dmv * sg).astype(BF16)
        @pl.when(q == 0)
        def _():
            da_ref[...] = dp

        @pl.when(q == 1)
        def _():
            db_ref[...] = dp

        @pl.when(q == 2)
        def _():
            dc_ref[...] = dp

    gs = pl.BlockSpec((tm, d), lambda i, q: (i, gate_block + q))
    o = pl.BlockSpec((tm, d), lambda i, q: (i, 0))
    return _pcall(
        body, name="gates_bwd", out_shape=(_sds(dz.shape, dz.dtype),) + (_sds((t, d), BF16),) * 3, grid=(t // tm, 3),
        in_specs=[gs, o, pl.BlockSpec((1, tm, d), lambda i, q: (q, i, 0)), HBM_SPEC], out_specs=(gs, o, o, o),
        input_output_aliases={3: 0}, compiler_params=_cp(dimension_semantics=("arbitrary", "arbitrary")),
    )(z, dm, proj, dz)


def _ffn_in_fwd(h2, wfi_t, *, t, d, dff, tm=512, tn=1408):
    nj = dff // tn

    def body(h_ref, wg_ref, wu_ref, g_ref, u_ref, a_ref):
        hv = h_ref[...]
        dn = (((1,), (1,)), ((), ()))
        gv = lax.dot_general(hv, wg_ref[...], dn, preferred_element_type=F32)
        uv = lax.dot_general(hv, wu_ref[...], dn, preferred_element_type=F32)
        g_ref[...] = gv.astype(BF16)
        u_ref[...] = uv.astype(BF16)
        a_ref[...] = (gv * _sigmoid(gv) * uv).astype(BF16)

    o = pl.BlockSpec((tm, tn), lambda i, j: (i, j))
    return _pcall(
        body, name="ffn_in_fwd", out_shape=(_sds((t, dff), BF16),) * 3, grid=(t // tm, nj),
        in_specs=[pl.BlockSpec((tm, d), lambda i, j: (i, 0)), pl.BlockSpec((tn, d), lambda i, j: (j, 0)),
                  pl.BlockSpec((tn, d), lambda i, j: (nj + j, 0))],
        out_specs=(o, o, o), compiler_params=_cp(),
    )(h2, wfi_t, wfi_t)


def _ffn_out_bwd(dy, wfo, gate, up, *, t, d, dff, tm=512, tn=1408):
    def body(dy_ref, w_ref, g_ref, u_ref, dg_ref, du_ref):
        dav = lax.dot_general(dy_ref[...].astype(BF16), w_ref[...], (((1,), (1,)), ((), ())), preferred_element_type=F32)
        gv = g_ref[...].astype(F32)
        sg = _sigmoid(gv)
        dg_ref[...] = (dav * u_ref[...].astype(F32) * sg * (1.0 + gv * (1.0 - sg))).astype(BF16)
        du_ref[...] = (dav * gv * sg).astype(BF16)

    o = pl.BlockSpec((tm, tn), lambda i, j: (i, j))
    return _pcall(
        body, name="ffn_out_bwd", out_shape=(_sds((t, dff), BF16),) * 2, grid=(t // tm, dff // tn),
        in_specs=[pl.BlockSpec((tm, d), lambda i, j: (i, 0)), pl.BlockSpec((tn, d), lambda i, j: (j, 0)), o, o],
        out_specs=(o, o), compiler_params=_cp(),
    )(dy, wfo, gate, up)


def _loss_head(y, target, *, t, d, tm=512):
    def body(y_ref, t_ref, dy_ref, l_ref):
        e = y_ref[...] - t_ref[...]
        dy_ref[...] = e * (1.0 / d)

        @pl.when(pl.program_id(0) == 0)
        def _():
            l_ref[...] = jnp.zeros_like(l_ref)

        l_ref[...] += jnp.sum(jnp.sum(e * e, axis=-1, keepdims=True), axis=0, keepdims=True) * (0.5 / d)

    o = pl.BlockSpec((tm, d), lambda i: (i, 0))
    dy, l = _pcall(
        body, name="loss_head", out_shape=(_sds((t, d), F32), _sds((1, HD), F32)), grid=(t // tm,),
        in_specs=[o, o], out_specs=(o, pl.BlockSpec((1, HD), lambda i: (0, 0))),
        compiler_params=_cp(dimension_semantics=("arbitrary",)),
    )(y, target)
    return dy, l[0, 0]


def _adamw(w, g, m, v, *, name):
    if w.ndim == 3:
        rows, _, cols = w.shape
        tm = max(tt for tt in range(1, 65) if rows % tt == 0)
        block, imap, grid = (tm, 1, cols), (lambda i, j: (i, 0, 0)), (rows // tm, 1)
    else:
        rows, cols = w.shape
        tm, tn = rows, cols
        for cand in (256, 128, 64, 32, 16, 8):
            if rows > cand and rows % cand == 0:
                tm = cand
                break
        if tm == rows and rows > 256 and cols % 128 == 0:
            tn = 128
        block, imap, grid = (tm, tn), (lambda i, j: (i, j)), (rows // tm, cols // tn)
    c1 = 1.0 - ADAM_B1 ** ADAM_STEP
    c2 = 1.0 - ADAM_B2 ** ADAM_STEP

    def body(w_ref, g_ref, m_ref, v_ref, d_ref, nm_ref, nv_ref):
        gv = g_ref[...]
        mn = ADAM_B1 * m_ref[...] + (1.0 - ADAM_B1) * gv
        vn = ADAM_B2 * v_ref[...] + (1.0 - ADAM_B2) * (gv * gv)
        d_ref[...] = -ADAM_LR * ((mn / c1) / (jnp.sqrt(vn / c2) + ADAM_EPS) + ADAM_WD * w_ref[...])
        nm_ref[...] = mn
        nv_ref[...] = vn

    spec = pl.BlockSpec(block, imap)
    return _pcall(
        body, name=name, out_shape=(_sds(w.shape, F32),) * 3, grid=grid,
        in_specs=[spec] * 4, out_specs=(spec,) * 3, compiler_params=_cp(),
    )(w, g, m, v)


def _pad_rows(a, rows):
    return a if a.shape[0] == rows else jnp.pad(a, ((0, rows - a.shape[0]),) + ((0, 0),) * (a.ndim - 1))


class _Pack:
    def __init__(self, row_sizes):
        self.rows = list(row_sizes)
        self.padded = [-(-r // 16) * 16 for r in self.rows]
        self.offs = [sum(self.padded[:i]) for i in range(len(self.rows))]
        self.total = sum(self.padded)

    def stack(self, blocks):
        return jnp.concatenate([_pad_rows(b.astype(BF16), p) for b, p in zip(blocks, self.padded)], axis=0)

    def full(self, gathered, i):
        r = self.rows[i]
        return gathered[:, self.offs[i]:self.offs[i] + r, :].reshape(N_DEV * r, gathered.shape[2])

    def for_core(self, full_grads, core):
        parts = []
        for gfull, r, p in zip(full_grads, self.rows, self.padded):
            pieces = gfull if isinstance(gfull, (tuple, list)) else (gfull,)
            blk = jnp.concatenate(
                [lax.dynamic_index_in_dim(g.reshape(-1, 2, r, g.shape[1]), core, axis=1, keepdims=False) for g in pieces],
                axis=0).astype(BF16)
            if p != r:
                blk = jnp.pad(blk, ((0, 0), (0, p - r), (0, 0)))
            parts.append(blk)
        return jnp.concatenate(parts, axis=1)

    def shard(self, g_pack, i):
        return g_pack[self.offs[i]:self.offs[i] + self.rows[i], :]


def kernel(x, mem, norm1_g, w_in, conv_w, conv_b, fox_f_bias, fox_q_g, fox_k_g, mem_norm_g, w_mem_kv, xa_q_g, xa_k_g, w_br_conv, w_br_fox, w_br_xa, w_o, norm2_g, w_ffn_in, w_ffn_out, loss_target, m_norm1_g, m_w_in, m_conv_w, m_conv_b, m_fox_f_bias, m_fox_q_g, m_fox_k_g, m_mem_norm_g, m_w_mem_kv, m_xa_q_g, m_xa_k_g, m_w_br_conv, m_w_br_fox, m_w_br_xa, m_w_o, m_norm2_g, m_w_ffn_in, m_w_ffn_out, v_norm1_g, v_w_in, v_conv_w, v_conv_b, v_fox_f_bias, v_fox_q_g, v_fox_k_g, v_mem_norm_g, v_w_mem_kv, v_xa_q_g, v_xa_k_g, v_w_br_conv, v_w_br_fox, v_w_br_xa, v_w_o, v_norm2_g, v_w_ffn_in, v_w_ffn_out):
    nb, s, d = x.shape
    t = nb * s
    nm = mem.shape[1]
    nh = d // HD
    dff = w_ffn_out.shape[1] * N_DEV
    n_in = w_in.shape[2] * N_DEV
    n_in_pad = -(-n_in // 1152) * 1152
    ff_block = (10 * d) // HD
    my_c = lax.axis_index("c")
    my_q = 2 * lax.axis_index("x") + lax.axis_index("y")

    pack_a = _Pack([w_in.shape[2]])
    pack_b = _Pack([w_mem_kv.shape[2], w_br_conv.shape[1], w_br_fox.shape[1], w_br_xa.shape[1], w_o.shape[1],
                    w_ffn_in.shape[2], w_ffn_out.shape[1]])
    gather_2 = _Pack([w_mem_kv.shape[2], w_br_conv.shape[1], w_br_fox.shape[1], w_br_xa.shape[1], w_o.shape[1]])
    stack_a = pack_a.stack([w_in[0].T])
    stack_2 = gather_2.stack([w_mem_kv[0].T, w_br_conv[0], w_br_fox[0], w_br_xa[0], w_o[0]])
    stack_fi = w_ffn_in[0].T.astype(BF16)
    stack_fo = w_ffn_out[0].astype(BF16)
    cw_block = _pad_rows(conv_w[0], 8)
    rows_a = w_in.shape[2]
    my_x, my_y = lax.axis_index("x"), lax.axis_index("y")
    arrival = [(my_x, my_y, my_c), (my_x, my_y, 1 - my_c)]
    for cx, cy in [(1 - my_x, my_y), (my_x, 1 - my_y), (1 - my_x, 1 - my_y)]:
        arrival += [(cx, cy, my_c), (cx, cy, 1 - my_c)]
    order = jnp.stack([4 * ax + 2 * ay + ac for ax, ay, ac in arrival]).astype(jnp.int32)

    x2 = x.reshape(t, d)
    mem2 = mem.reshape(nb * nm, d)
    tgt2 = loss_target.reshape(t, d)
    h1 = _rms_fwd(x2, norm1_g, name="rms1_fwd")
    z, slabs, _ = _z_gathered(h1, stack_a, order, t=t, d=d, n_pad=N_DEV * SLAB_WIN, rows=rows_a)
    z = _z_fix(h1, slabs, z, t=t, d=d, rows=rows_a)
    ff_rows = _pad_rows(slabs[N_DEV - 1, rows_a - nh:rows_a, :], HD)
    zff = _mm(h1, ff_rows, m=t, n=HD, k=d, tb=True, tm=1024, name="mm_z_ff")
    qn, kn, vb = _fox_prep(z, fox_q_g, fox_k_g, t=t, d=d)
    f_bias_col = fox_f_bias.reshape(nh, 1)
    ck_rows = _fox_gates(zff, f_bias_col, nb=nb, s=s, nh=nh).reshape(nb * nh, 1, s)
    y_fox, lse, (gathered_2, cw_g, gathered_fo) = _fox_fwd(qn, kn, vb, ck_rows, nb=nb, s=s, nh=nh,
                                                           comm=_AllGather([stack_2, cw_block, stack_fo]))
    wkv_t, wbc, wbf, wbx, wo = (gather_2.full(gathered_2, i) for i in range(5))
    wfo = gathered_fo.reshape(dff, d)
    conv_w_full = jnp.transpose(cw_g[:, 0:3, :], (1, 0, 2)).reshape(3, d)

    y_conv = _conv_fwd(z, conv_w_full, conv_b, nb=nb, s=s, d=d)
    mem_n = _rms_fwd(mem2, mem_norm_g, name="rms_mem_fwd")
    kv = _mm(mem_n, wkv_t, m=nb * nm, n=2 * d, k=d, tb=True, name="mm_kv")
    xkn, xvb = _xa_prep_kv(kv, xa_k_g, rows=nb * nm, d=d)
    y_xa = _xa_fwd(z, xkn, xvb, xa_q_g, nb=nb, s=s, d=d, nm=nm, q_block=6)
    merged, proj, (gathered_fi,) = _branches_fwd(z, (y_conv, y_fox, y_xa), (wbc, wbf, wbx), t=t, d=d, gate_block=7,
                                                 comm=_AllGather([stack_fi]))
    wfi_t = gathered_fi.reshape(2 * dff, d)
    x1 = _mm(merged, wo, m=t, n=d, k=d, res=x2, name="mm_x1")
    h2 = _rms_fwd(x1, norm2_g, name="rms2_fwd")
    ff_gate, ff_up, act = _ffn_in_fwd(h2, wfi_t, t=t, d=d, dff=dff)
    y = _mm(act, wfo, m=t, n=d, k=dff, res=x1, tk=dff, name="mm_y")
    dy, loss_local = _loss_head(y, tgt2, t=t, d=d)

    g_wfo = _mm(act, dy, m=dff, n=d, k=t, ta=True, tm=1408, tk=2048, out_dtype=BF16, name="mm_g_wfo")
    dgate, dup = _ffn_out_bwd(dy, wfo, ff_gate, ff_up, t=t, d=d, dff=dff)
    dh2 = _mm(dgate, wfi_t, m=t, n=d, k=dff, tk=dff, name="mm_dh2_g")
    dh2 = _mm(dup, wfi_t, m=t, n=d, k=dff, tk=dff, b_off=(1, 0), res=dh2, name="mm_dh2_u")
    g_wfi_g = _mm(dgate, h2, m=dff, n=d, k=t, ta=True, tm=1408, tk=2048, out_dtype=BF16, name="mm_g_wfi_g")
    g_wfi_u = _mm(dup, h2, m=dff, n=d, k=t, ta=True, tm=1408, tk=2048, out_dtype=BF16, name="mm_g_wfi_u")
    dx1, g_norm2 = _rms_bwd(dh2, x1, norm2_g, dy, name="rms2_bwd")
    dmerged = _mm(dx1, wo, m=t, n=d, k=d, tb=True, name="mm_dmerged")
    g_wo = _mm(merged, dx1, m=d, n=d, k=t, ta=True, tk=2048, out_dtype=BF16, name="mm_g_wo")

    dz = lax.empty((t, n_in_pad), BF16)
    dz, dpa, dpb, dpc = _gates_bwd(z, dmerged, proj, dz, t=t, d=d, gate_block=7)
    dy_conv = _mm(dpa, wbc, m=t, n=d, k=d, tb=True, name="mm_dy_conv")
    g_wbc = _mm(y_conv, dpa, m=d, n=d, k=t, ta=True, tk=4096, out_dtype=BF16, name="mm_g_wbc")
    dy_fox = _mm(dpb, wbf, m=t, n=d, k=d, tb=True, name="mm_dy_fox")
    g_wbf = _mm(y_fox, dpb, m=d, n=d, k=t, ta=True, tk=4096, out_dtype=BF16, name="mm_g_wbf")
    dy_xa = _mm(dpc, wbx, m=t, n=d, k=d, tb=True, name="mm_dy_xa")
    g_wbx = _mm(y_xa, dpc, m=d, n=d, k=t, ta=True, tk=4096, out_dtype=BF16, name="mm_g_wbx")

    dz, g_conv_w, g_conv_b = _conv_bwd(z, dy_conv, conv_w_full, conv_b, dz, nb=nb, s=s, d=d)
    dz, dxkn, dxv, g_xa_q = _xa_bwd(z, xkn, xvb, xa_q_g, dy_xa, dz, nb=nb, s=s, d=d, nm=nm, q_block=6)
    dkv, g_xa_k = _xa_post_kv(kv, dxkn, dxv, xa_k_g, rows=nb * nm, d=d)
    g_wkv_t = _mm(dkv, mem_n, m=2 * d, n=d, k=nb * nm, ta=True, out_dtype=BF16, name="mm_g_wkv")
    dmem_n = _mm(dkv, wkv_t, m=nb * nm, n=d, k=2 * d, tk=2 * d, name="mm_dmem")
    _, g_mem_norm = _rms_bwd(dmem_n, mem2, mem_norm_g, None, name="rms_mem_bwd")

    grads_b = [g_wkv_t, g_wbc, g_wbf, g_wbx, g_wo, (g_wfi_g, g_wfi_u), g_wfo]
    keep_b, send_b = pack_b.for_core(grads_b, my_c), pack_b.for_core(grads_b, 1 - my_c)
    dqn, ds_rows, (from_sibling_b,) = _fox_bwd_dq(qn, kn, vb, ck_rows, dy_fox, y_fox, lse, nb=nb, s=s, nh=nh,
                                                  comm=_SiblingSwap(send_b))
    part_b = _add2(keep_b, from_sibling_b, name="rs_add_sibling_b")
    dkn, dvb, ds_cols, (from_chips_b,) = _fox_bwd_dkv(qn, kn, vb, ck_rows, dy_fox, y_fox, lse, nb=nb, s=s, nh=nh,
                                                     comm=_ChipExchange(part_b))
    g_pack_b = _add4(lax.dynamic_index_in_dim(part_b, my_q, axis=0, keepdims=False), from_chips_b, name="rs_add_chips_b")
    dz, g_fox_q, g_fox_k = _fox_post(z, dqn, dkn, dvb, fox_q_g, fox_k_g, dz, t=t, d=d)
    dz, g_f_bias = _fox_gates_bwd(zff, f_bias_col, ds_rows, ds_cols, dz, nb=nb, s=s, nh=nh, ff_block=ff_block)

    keep_a, from_sibling_a = _g_win_slabs(dz, h1, my_c, t=t, d=d, n_slab=N_DEV, rows=rows_a, rp=slabs.shape[1])
    part_a = _add2(keep_a, from_sibling_a, name="rs_add_sibling_a")
    dh1, (from_chips_a,) = _dh1_from_slabs(dz, slabs, t=t, d=d, rows=rows_a, comm=_ChipExchange(part_a))
    g_pack_a = _add4(lax.dynamic_index_in_dim(part_a, my_q, axis=0, keepdims=False), from_chips_a, name="rs_add_chips_a",
                     rows_apart=True)
    dx, g_norm1 = _rms_bwd(dh1, x2, norm1_g, dx1, name="rms1_bwd")

    xd = d // XH
    tail = jnp.concatenate(
        [_pad_rows(g_f_bias.reshape(nh), HD).reshape(1, HD), g_fox_q, g_fox_k, g_xa_q, g_xa_k,
         jnp.zeros((1, d - 3 * HD - 2 * xd), F32)], axis=1)
    small = jnp.concatenate([g_norm1, g_norm2, g_conv_w, g_conv_b, g_mem_norm, tail], axis=0)
    small = _all_reduce_small(small)
    gs_conv_w = lax.dynamic_slice_in_dim(small[2:5], (2 * my_q + my_c) * (d // N_DEV), d // N_DEV, axis=1)

    grads = {
        "norm1_g": small[0:1], "w_in": g_pack_a, "conv_w": gs_conv_w, "conv_b": small[5:6],
        "fox_f_bias": small[7:8, 0:nh], "fox_q_g": small[7:8, HD:2 * HD], "fox_k_g": small[7:8, 2 * HD:3 * HD],
        "mem_norm_g": small[6:7], "w_mem_kv": pack_b.shard(g_pack_b, 0), "xa_q_g": small[7:8, 3 * HD:3 * HD + xd],
        "xa_k_g": small[7:8, 3 * HD + xd:3 * HD + 2 * xd], "w_br_conv": pack_b.shard(g_pack_b, 1),
        "w_br_fox": pack_b.shard(g_pack_b, 2), "w_br_xa": pack_b.shard(g_pack_b, 3), "w_o": pack_b.shard(g_pack_b, 4),
        "norm2_g": small[1:2], "w_ffn_in": pack_b.shard(g_pack_b, 5), "w_ffn_out": pack_b.shard(g_pack_b, 6),
    }
    transposed = {"w_in", "w_mem_kv", "w_ffn_in"}
    weights = {
        "norm1_g": (norm1_g, m_norm1_g, v_norm1_g), "w_in": (w_in, m_w_in, v_w_in), "conv_w": (conv_w, m_conv_w, v_conv_w),
        "conv_b": (conv_b, m_conv_b, v_conv_b), "fox_f_bias": (fox_f_bias, m_fox_f_bias, v_fox_f_bias),
        "fox_q_g": (fox_q_g, m_fox_q_g, v_fox_q_g), "fox_k_g": (fox_k_g, m_fox_k_g, v_fox_k_g),
        "mem_norm_g": (mem_norm_g, m_mem_norm_g, v_mem_norm_g), "w_mem_kv": (w_mem_kv, m_w_mem_kv, v_w_mem_kv),
        "xa_q_g": (xa_q_g, m_xa_q_g, v_xa_q_g), "xa_k_g": (xa_k_g, m_xa_k_g, v_xa_k_g),
        "w_br_conv": (w_br_conv, m_w_br_conv, v_w_br_conv), "w_br_fox": (w_br_fox, m_w_br_fox, v_w_br_fox),
        "w_br_xa": (w_br_xa, m_w_br_xa, v_w_br_xa), "w_o": (w_o, m_w_o, v_w_o), "norm2_g": (norm2_g, m_norm2_g, v_norm2_g),
        "w_ffn_in": (w_ffn_in, m_w_ffn_in, v_w_ffn_in), "w_ffn_out": (w_ffn_out, m_w_ffn_out, v_w_ffn_out),
    }
    out_g, out_d, out_m, out_v = [], [], [], []
    for name, (w, m, v) in weights.items():
        shape = w.shape
        if name == "w_in":
            w3, m3, v3 = (a.transpose(2, 0, 1) for a in (w, m, v))
            dl, mn, vn = _adamw(w3, g_pack_a, m3, v3, name="adamw_" + name)
            out_g.append(g_pack_a[:shape[2]].transpose(1, 2, 0))
            out_d.append(dl.transpose(1, 2, 0))
            out_m.append(mn.transpose(1, 2, 0))
            out_v.append(vn.transpose(1, 2, 0))
            continue
        tr = name in transposed

        def as2d(a):
            a = a.reshape(shape[-2], shape[-1])
            return a.T if tr else a

        def back(a):
            return (a.T if tr else a).reshape(shape)

        w2 = as2d(w)
        g2 = grads[name].reshape(w2.shape)
        dl, mn, vn = _adamw(w2, g2, as2d(m), as2d(v), name="adamw_" + name)
        out_g.append(back(g2))
        out_d.append(back(dl))
        out_m.append(back(mn))
        out_v.append(back(vn))

    loss = lax.psum(loss_local, ("x", "y", "c"))
    return (loss, dx.reshape(nb, s, d), *out_g, *out_d, *out_m, *out_v)
```

```python
import math

import jax
import jax.numpy as jnp
from jax import lax
from jax.experimental import pallas as pl
from jax.experimental.pallas import tpu as pltpu

F32, BF16 = jnp.float32, jnp.bfloat16
EPS = 1e-6
HD = 128
XH = 4
N_DEV = 8
MESH = pl.DeviceIdType.MESH
VMEM_LIMIT = 52 * 1024 * 1024
NEG = -1e30
SQRT_HD = math.sqrt(HD)
EXP2_C = math.log2(math.e) / SQRT_HD
LOG2E = math.log2(math.e)
FOX_HP = 4

ADAM_LR, ADAM_B1, ADAM_B2, ADAM_EPS, ADAM_WD, ADAM_STEP = 0.001, 0.9, 0.999, 1e-08, 0.01, 10


def _cp(**kw):
    return pltpu.CompilerParams(vmem_limit_bytes=VMEM_LIMIT, **kw)


def _pcall(body, **kw):
    return pl.pallas_call(body, **kw)


def _sds(shape, dtype):
    return jax.ShapeDtypeStruct(shape, dtype)


HBM_SPEC = pl.BlockSpec(memory_space=pl.ANY)


def _mesh_pos():
    return lax.axis_index("x"), lax.axis_index("y"), lax.axis_index("c")


class _AllGather:
    def __init__(self, blocks):
        self.blocks = list(blocks)
        n = len(self.blocks)
        self.out_shape = [_sds((N_DEV,) + b.shape, b.dtype) for b in self.blocks]
        self.scratch = [pltpu.SemaphoreType.DMA((n, 7)), pltpu.SemaphoreType.DMA((n, 7)), pltpu.SemaphoreType.DMA((n,))]

    def bind(self, ins, outs, sems):
        n = len(ins)
        send_sems, recv_sems, local_sems = sems
        x, y, c = _mesh_pos()
        me, sibling = (x, y, c), (x, y, 1 - c)
        chips = [(1 - x, y), (x, 1 - y), (1 - x, 1 - y)]

        def slab(t, dev):
            return outs[t].at[4 * dev[0] + 2 * dev[1] + dev[2]]

        def copy(t, k, block, to, src=None):
            dst = slab(t, block)
            return pltpu.make_async_remote_copy(
                src_ref=dst if src is None else src, dst_ref=dst, send_sem=send_sems.at[t, k], recv_sem=recv_sems.at[t, k],
                device_id=to, device_id_type=MESH)

        mine = [pltpu.make_async_copy(ins[t], slab(t, me), local_sems.at[t]) for t in range(n)]
        first = []
        for t in range(n):
            first.append(copy(t, 0, me, sibling, src=ins[t]))
            first += [copy(t, 1 + j, me, (*chip, c), src=ins[t]) for j, chip in enumerate(chips)]
        passed = [copy(t, 4 + j, (*chip, c), sibling) for j, chip in enumerate(chips) for t in range(n)]

        def start():
            for cp in mine + first:
                cp.start()

        def mid():
            for j, chip in enumerate(chips):
                for t in range(n):
                    copy(t, 1 + j, (*chip, c), me).wait_recv()
                    passed[j * n + t].start()

        def finish():
            for t in range(n):
                copy(t, 0, sibling, me).wait_recv()
                for j, chip in enumerate(chips):
                    copy(t, 4 + j, (*chip, 1 - c), me).wait_recv()
            for cp in first + passed:
                cp.wait_send()
            for cp in mine:
                cp.wait()

        return start, mid, finish


class _SiblingSwap:
    def __init__(self, send):
        self.blocks = [send]
        self.out_shape = [_sds(send.shape, send.dtype)]
        self.scratch = [pltpu.SemaphoreType.DMA, pltpu.SemaphoreType.DMA]

    def bind(self, ins, outs, sems):
        x, y, c = _mesh_pos()
        cp = pltpu.make_async_remote_copy(src_ref=ins[0], dst_ref=outs[0], send_sem=sems[0], recv_sem=sems[1],
                                          device_id=(x, y, 1 - c), device_id_type=MESH)
        return cp.start, (lambda: None), cp.wait


class _ChipExchange:
    def __init__(self, part):
        self.blocks = [part]
        self.out_shape = [_sds((3,) + part.shape[1:], part.dtype)]
        self.scratch = [pltpu.SemaphoreType.DMA((3,)), pltpu.SemaphoreType.DMA((3,))]

    def bind(self, ins, outs, sems):
        x, y, c = _mesh_pos()
        chips = [(1 - x, y), (x, 1 - y), (1 - x, 1 - y)]
        cps = [
            pltpu.make_async_remote_copy(src_ref=ins[0].at[2 * cx + cy], dst_ref=outs[0].at[j], send_sem=sems[0].at[j],
                                         recv_sem=sems[1].at[j], device_id=(cx, cy, c), device_id_type=MESH)
            for j, (cx, cy) in enumerate(chips)
        ]

        def start():
            for cp in cps:
                cp.start()

        def finish():
            for cp in cps:
                cp.wait()

        return start, (lambda: None), finish


def _comm_lists(comm):
    if comm is None:
        return [], [], [], []
    n = len(comm.blocks)
    return list(comm.blocks), [HBM_SPEC] * n, list(comm.out_shape), list(comm.scratch)


def _split_refs(refs, n_pre, n_in, n_cin, n_out, n_cout, n_scr):
    cuts = [n_pre, n_in, n_cin, n_out, n_cout, n_scr]
    out, at = [], 0
    for c in cuts:
        out.append(refs[at:at + c])
        at += c
    out.append(refs[at:])
    return out


def _run_comm(comm, cin, cout, csem, step, n_steps, when="start"):
    if comm is None:
        return
    start, mid, finish = comm.bind(cin, cout, csem)
    if when == "start":
        pl.when(step == 0)(start)
    else:
        pl.when(step == max(n_steps - 2, 0))(mid)
        pl.when(step == n_steps - 1)(finish)


def _comm_only(comm, name):
    ins, in_specs, out_shape, scratch = _comm_lists(comm)
    n = len(ins)

    def body(*refs):
        start, mid, finish = comm.bind(refs[:n], refs[n:2 * n], refs[2 * n:])
        start()
        mid()
        finish()

    outs = _pcall(body, name=name, out_shape=tuple(out_shape), in_specs=in_specs, out_specs=tuple([HBM_SPEC] * n),
                  scratch_shapes=scratch, compiler_params=_cp())(*ins)
    return list(outs)


def _all_reduce_small(v):
    rows, cols = v.shape

    def body(v_ref, o_ref, slots, send_sems, recv_sems):
        x, y, c = _mesh_pos()
        me = 4 * x + 2 * y + c
        slots[me] = v_ref[...]
        cps = []
        for k in range(1, N_DEV):
            px, py, pc = x ^ (k >> 2), y ^ ((k >> 1) & 1), c ^ (k & 1)
            cps.append(pltpu.make_async_remote_copy(
                src_ref=v_ref, dst_ref=slots.at[me], send_sem=send_sems.at[k - 1], recv_sem=recv_sems.at[k - 1],
                device_id=(px, py, pc), device_id_type=MESH))
        for cp in cps:
            cp.start()
        for cp in cps:
            cp.wait()
        acc = slots[0]
        for k in range(1, N_DEV):
            acc = acc + slots[k]
        o_ref[...] = acc

    return _pcall(
        body, name="all_reduce_small", out_shape=_sds((rows, cols), F32),
        in_specs=[pl.BlockSpec(memory_space=pltpu.VMEM)], out_specs=pl.BlockSpec(memory_space=pltpu.VMEM),
        scratch_shapes=[pltpu.VMEM((N_DEV, rows, cols), F32), pltpu.SemaphoreType.DMA((7,)), pltpu.SemaphoreType.DMA((7,))],
        compiler_params=_cp(),
    )(v)


def _row_tile(rows, cap=1040):
    return max(tt for tt in range(16, cap + 1, 16) if rows % tt == 0)


def _add2(a, b, *, name):
    nq, rows, cols = a.shape
    tm = _row_tile(rows)

    def body(a_ref, b_ref, o_ref):
        o_ref[...] = (a_ref[...].astype(F32) + b_ref[...].astype(F32)).astype(BF16)

    spec = pl.BlockSpec((1, tm, cols), lambda q, i: (q, i, 0))
    return _pcall(body, name=name, out_shape=_sds(a.shape, BF16), grid=(nq, rows // tm),
                  in_specs=[spec, spec], out_specs=spec, compiler_params=_cp())(a, b)


def _add4(own, recv, *, name, rows_apart=False):
    rows, cols = own.shape
    tm = _row_tile(rows, 256 if rows_apart else 1040)

    def body(o_ref, r_ref, g_ref):
        tot = ((o_ref[...].astype(F32) + r_ref[0].astype(F32)) + r_ref[1].astype(F32)) + r_ref[2].astype(F32)
        if rows_apart:
            g_ref[:, 0, :] = tot
        else:
            g_ref[...] = tot

    if rows_apart:
        out_shape, out_spec = _sds((rows, 1, cols), F32), pl.BlockSpec((tm, 1, cols), lambda i: (i, 0, 0))
    else:
        out_shape, out_spec = _sds((rows, cols), F32), pl.BlockSpec((tm, cols), lambda i: (i, 0))
    return _pcall(
        body, name=name, out_shape=out_shape, grid=(rows // tm,),
        in_specs=[pl.BlockSpec((tm, cols), lambda i: (i, 0)), pl.BlockSpec((3, tm, cols), lambda i: (0, i, 0))],
        out_specs=out_spec, compiler_params=_cp(),
    )(own, recv)


def _mm(a, b, *, m, n, k, name, ta=False, tb=False, b_off=(0, 0), res=None, out_dtype=F32, tm=512, tn=1024, tk=1024,
        comm=None):
    tm, tn, tk = min(tm, m), min(tn, n), min(tk, k)
    assert m % tm == 0 and n % tn == 0 and k % tk == 0, (name, m, n, k, tm, tn, tk)
    nk = k // tk
    grid = (m // tm, n // tn, nk)
    bo0, bo1 = b_off
    if ta:
        a_spec = pl.BlockSpec((tk, tm), lambda i, j, kk: (kk, i))
    else:
        a_spec = pl.BlockSpec((tm, tk), lambda i, j, kk: (i, kk))
    if tb:
        b_spec = pl.BlockSpec((tn, tk), lambda i, j, kk: (j + bo0, kk + bo1))
    else:
        b_spec = pl.BlockSpec((tk, tn), lambda i, j, kk: (kk + bo0, j + bo1))
    o_spec = pl.BlockSpec((tm, tn), lambda i, j, kk: (i, j))
    dn = (((0 if ta else 1,), (1 if tb else 0,)), ((), ()))
    has_res = res is not None
    c_ins, c_in_specs, c_out_shape, c_scratch = _comm_lists(comm)
    n_in = 3 if has_res else 2
    n_scr = 1 if nk > 1 else 0

    def body(*refs):
        _, ins, cin, outs, cout, scr, csem = _split_refs(refs, 0, n_in, len(c_ins), 1, len(c_ins), n_scr)
        a_ref, b_ref = ins[0], ins[1]
        r_ref = ins[2] if has_res else None
        o_ref = outs[0]
        step = (pl.program_id(0) * grid[1] + pl.program_id(1)) * nk + pl.program_id(2)
        _run_comm(comm, cin, cout, csem, step, grid[0] * grid[1] * nk, "start")
        p = lax.dot_general(a_ref[...].astype(BF16), b_ref[...].astype(BF16), dn, preferred_element_type=F32)

        def finish(acc):
            if has_res:
                acc = acc + r_ref[...]
            o_ref[...] = acc.astype(out_dtype)

        if nk == 1:
            finish(p)
        else:
            acc_ref = scr[0]
            kk = pl.program_id(2)

            @pl.when(kk == 0)
            def _():
                acc_ref[...] = p

            @pl.when(kk > 0)
            def _():
                acc_ref[...] += p

            @pl.when(kk == nk - 1)
            def _():
                finish(acc_ref[...])

        _run_comm(comm, cin, cout, csem, step, grid[0] * grid[1] * nk, "end")

    ins = [a, b] + ([res] if has_res else [])
    in_specs = [a_spec, b_spec] + ([o_spec] if has_res else [])
    sem = ("arbitrary",) * 3 if comm is not None else ("parallel", "parallel", "arbitrary")
    outs = _pcall(
        body, name=name, out_shape=tuple([_sds((m, n), out_dtype)] + c_out_shape), grid=grid,
        in_specs=in_specs + c_in_specs, out_specs=tuple([o_spec] + [HBM_SPEC] * len(c_ins)),
        scratch_shapes=([pltpu.VMEM((tm, tn), F32)] if nk > 1 else []) + c_scratch,
        compiler_params=_cp(dimension_semantics=sem),
    )(*(ins + c_ins))
    return outs[0] if comm is None else (outs[0], list(outs[1:]))


SLAB_WIN = 1280
SLAB_TAIL = 48


def _z_gathered(h1, own_slab, order, *, t, d, n_pad, rows, comm=None, tm=1024):
    rp = own_slab.shape[0]
    n_slab = N_DEV
    tm = min(tm, t)
    ni = t // tm
    sh = rows - SLAB_WIN
    assert sh >= 0 and (n_slab - 1) * sh <= 128
    c_ins, c_in_specs, c_out_shape, c_scratch = _comm_lists(comm)
    n_steps = n_slab * ni

    def body(*refs):
        pre, ins, cin, outs, cout, scr, csem = _split_refs(refs, 1, 2, len(c_ins), 2, len(c_ins), 5)
        order_ref = pre[0]
        h_ref, own_ref = ins
        o_ref, slabs_ref = outs
        slab_vm, send_sems, recv_sems, local_sem, load_sem = scr
        p, i = pl.program_id(0), pl.program_id(1)
        step = p * ni + i
        x, y, c = _mesh_pos()
        me, sibling = (x, y, c), (x, y, 1 - c)
        chips = [(1 - x, y), (x, 1 - y), (1 - x, 1 - y)]

        def slab(dev):
            return slabs_ref.at[4 * dev[0] + 2 * dev[1] + dev[2]]

        def copy(k, block, to, src=None):
            dst = slab(block)
            return pltpu.make_async_remote_copy(
                src_ref=dst if src is None else src, dst_ref=dst, send_sem=send_sems.at[k], recv_sem=recv_sems.at[k],
                device_id=to, device_id_type=MESH)

        mine = pltpu.make_async_copy(own_ref, slab(me), local_sem)
        first = [copy(0, me, sibling, src=own_ref)] + [copy(1 + j, me, (*chip, c), src=own_ref) for j, chip in enumerate(chips)]
        passed = [copy(4 + j, (*chip, c), sibling) for j, chip in enumerate(chips)]

        @pl.when(step == 0)
        def _():
            mine.start()
            for cp in first:
                cp.start()

        _run_comm(comm, cin, cout, csem, step, n_steps, "start")

        def load(src):
            cp = pltpu.make_async_copy(src, slab_vm, load_sem)
            cp.start()
            cp.wait()

        @pl.when(i == 0)
        def _():
            @pl.when(p == 0)
            def _():
                load(own_ref)

            @pl.when(p == 1)
            def _():
                copy(0, sibling, me).wait_recv()
                load(slab(sibling))

            for j, chip in enumerate(chips):
                @pl.when(p == 2 + 2 * j)
                def _():
                    copy(1 + j, (*chip, c), me).wait_recv()
                    passed[j].start()
                    load(slab((*chip, c)))

                @pl.when(p == 3 + 2 * j)
                def _():
                    copy(4 + j, (*chip, 1 - c), me).wait_recv()
                    load(slab((*chip, 1 - c)))

        k = order_ref[p]
        y_k = lax.dot_general(h_ref[...], slab_vm[...], (((1,), (1,)), ((), ())), preferred_element_type=F32)
        o_ref[...] = pltpu.roll(y_k[:, 0:SLAB_WIN], k * sh, axis=1).astype(BF16)

        @pl.when(step == n_steps - 1)
        def _():
            for cp in first + passed:
                cp.wait_send()
            mine.wait()

        _run_comm(comm, cin, cout, csem, step, n_steps, "end")

    outs = _pcall(
        body, name="mm_z", out_shape=tuple([_sds((t, n_pad), BF16), _sds((n_slab, rp, d), BF16)] + c_out_shape),
        grid_spec=pltpu.PrefetchScalarGridSpec(
            num_scalar_prefetch=1, grid=(n_slab, ni),
            in_specs=[pl.BlockSpec((tm, d), lambda p, i, order: (i, 0)), HBM_SPEC] + c_in_specs,
            out_specs=tuple([pl.BlockSpec((tm, SLAB_WIN), lambda p, i, order: (i, order[p])), HBM_SPEC] + [HBM_SPEC] * len(c_ins)),
            scratch_shapes=[pltpu.VMEM((rp, d), BF16), pltpu.SemaphoreType.DMA((7,)), pltpu.SemaphoreType.DMA((7,)),
                            pltpu.SemaphoreType.DMA, pltpu.SemaphoreType.DMA] + c_scratch),
        compiler_params=_cp(dimension_semantics=("arbitrary", "arbitrary")),
    )(order, h1, own_slab, *c_ins)
    return outs[0], outs[1], list(outs[2:])


def _z_fix(h1, slabs, z, *, t, d, rows, tm=4096):
    n_slab, rp, _ = slabs.shape
    tm = min(tm, t)
    sh = rows - SLAB_WIN
    tail_lo = rp - SLAB_TAIL
    assert rows - (n_slab - 1) * sh >= tail_lo and rp % SLAB_TAIL == 0

    def body(h_ref, t_ref, z_ref, o_ref):
        k = pl.program_id(0) + 1
        p = lax.dot_general(h_ref[...], t_ref[0], (((1,), (1,)), ((), ())), preferred_element_type=F32)
        pp = jnp.concatenate([p, jnp.zeros((tm, 128 - SLAB_TAIL), F32)], axis=1)
        pr = pltpu.roll(pp, 128 - (rows - tail_lo) + k * sh, axis=1)
        lane = lax.broadcasted_iota(jnp.int32, (tm, 128), 1)
        o_ref[...] = jnp.where(lane < k * sh, pr.astype(BF16), z_ref[...])

    blk = pl.BlockSpec((tm, 128), lambda b, i: (i, (SLAB_WIN // 128) * (b + 1)))
    return _pcall(
        body, name="mm_z_fix", out_shape=_sds(z.shape, z.dtype), grid=(n_slab - 1, t // tm),
        in_specs=[pl.BlockSpec((tm, d), lambda b, i: (i, 0)),
                  pl.BlockSpec((1, SLAB_TAIL, d), lambda b, i: (b, tail_lo // SLAB_TAIL, 0)), blk],
        out_specs=blk, input_output_aliases={2: 0}, compiler_params=_cp(),
    )(h1, slabs, z)


def _slab_window(a1_ref, a2_ref, k, sh):
    w = jnp.concatenate([a1_ref[...], a2_ref[...]], axis=1)
    width = SLAB_WIN + 128
    return pltpu.roll(w, width - k * sh, axis=1)


def _g_win_slabs(dz, h1, my_c, *, t, d, n_slab, rows, rp, tk=2048):
    sh = rows - SLAB_WIN
    tk = min(tk, t)
    nk = t // tk
    width = SLAB_WIN + 128
    half = n_slab // 2

    def slab_of(p, c):
        return 2 * (p % half) + jnp.where(p < half, 1 - c, c)

    def body(c_ref, a1_ref, a2_ref, h_ref, keep_ref, land_ref, acc, stage, send_sems, recv_sems):
        p, kk = pl.program_id(0), pl.program_id(1)
        x, y, c = _mesh_pos()
        ws = _slab_window(a1_ref, a2_ref, slab_of(p, c), sh)
        prod = lax.dot_general(ws, h_ref[...], (((0,), (0,)), ((), ())), preferred_element_type=F32)

        def push(q):
            return pltpu.make_async_remote_copy(
                src_ref=stage.at[q % 2], dst_ref=land_ref.at[q], send_sem=send_sems.at[q], recv_sem=recv_sems.at[q],
                device_id=(x, y, 1 - c), device_id_type=MESH)

        @pl.when(kk == 0)
        def _():
            acc[...] = prod

        @pl.when(kk > 0)
        def _():
            acc[...] += prod

        @pl.when(kk == nk - 1)
        def _():
            res = acc[0:rp, :].astype(BF16)
            for q in range(half):
                @pl.when(p == q)
                def _():
                    if q >= 2:
                        push(q - 2).wait_send()
                    stage[q % 2] = res
                    push(q).start()

            @pl.when(p >= half)
            def _():
                keep_ref[0] = res

        @pl.when((p == n_slab - 1) & (kk == nk - 1))
        def _():
            for q in range(max(half - 2, 0), half):
                push(q).wait_send()
            for q in range(half):
                push(q).wait_recv()

    outs = _pcall(
        body, name="mm_g_win", out_shape=(_sds((half, rp, d), BF16), _sds((half, rp, d), BF16)),
        grid_spec=pltpu.PrefetchScalarGridSpec(
            num_scalar_prefetch=1, grid=(n_slab, nk),
            in_specs=[pl.BlockSpec((tk, SLAB_WIN), lambda p, kk, c: (kk, slab_of(p, c[0]))),
                      pl.BlockSpec((tk, 128), lambda p, kk, c: (kk, (SLAB_WIN // 128) * (slab_of(p, c[0]) + 1))),
                      pl.BlockSpec((tk, d), lambda p, kk, c: (kk, 0))],
            out_specs=(pl.BlockSpec((1, rp, d), lambda p, kk, c: (jnp.maximum(p - half, 0), 0, 0)), HBM_SPEC),
            scratch_shapes=[pltpu.VMEM((width, d), F32), pltpu.VMEM((2, rp, d), BF16),
                            pltpu.SemaphoreType.DMA((half,)), pltpu.SemaphoreType.DMA((half,))]),
        compiler_params=_cp(dimension_semantics=("arbitrary", "arbitrary")),
    )(my_c.reshape(1).astype(jnp.int32), dz, dz, h1)
    return outs[0], outs[1]


def _dh1_from_slabs(dz, slabs, *, t, d, rows, comm=None, tm=1024):
    n_slab, rp, _ = slabs.shape
    sh = rows - SLAB_WIN
    width = SLAB_WIN + 128
    c_ins, c_in_specs, c_out_shape, c_scratch = _comm_lists(comm)
    grid = (t // tm, n_slab)

    def body(*refs):
        _, ins, cin, outs, cout, scr, csem = _split_refs(refs, 0, 3, len(c_ins), 1, len(c_ins), 1)
        a1_ref, a2_ref, s_ref = ins
        o_ref, acc = outs[0], scr[0]
        k = pl.program_id(1)
        step = pl.program_id(0) * n_slab + k
        _run_comm(comm, cin, cout, csem, step, grid[0] * n_slab, "start")
        ws = _slab_window(a1_ref, a2_ref, k, sh)
        s_end = jnp.concatenate([s_ref[0, SLAB_WIN:rp, :], jnp.zeros((width - rp, d), BF16)], axis=0)
        p = (jnp.dot(ws[:, 0:SLAB_WIN], s_ref[0, 0:SLAB_WIN, :], preferred_element_type=F32)
             + jnp.dot(ws[:, SLAB_WIN:width], s_end, preferred_element_type=F32))

        @pl.when(k == 0)
        def _():
            acc[...] = p

        @pl.when(k > 0)
        def _():
            acc[...] += p

        @pl.when(k == n_slab - 1)
        def _():
            o_ref[...] = acc[...]

        _run_comm(comm, cin, cout, csem, step, grid[0] * n_slab, "end")

    outs = _pcall(
        body, name="mm_dh1", out_shape=tuple([_sds((t, d), F32)] + c_out_shape), grid=grid,
        in_specs=[pl.BlockSpec((tm, SLAB_WIN), lambda i, k: (i, k)),
                  pl.BlockSpec((tm, 128), lambda i, k: (i, (SLAB_WIN // 128) * (k + 1))),
                  pl.BlockSpec((1, rp, d), lambda i, k: (k, 0, 0))] + c_in_specs,
        out_specs=tuple([pl.BlockSpec((tm, d), lambda i, k: (i, 0))] + [HBM_SPEC] * len(c_ins)),
        scratch_shapes=[pltpu.VMEM((tm, d), F32)] + c_scratch,
        compiler_params=_cp(dimension_semantics=("arbitrary", "arbitrary")),
    )(dz, dz, slabs, *c_ins)
    return outs[0], list(outs[1:])


def _rms_fwd(x, g, *, name, tm=512):
    t, d = x.shape
    tm = min(tm, t)

    def body(x_ref, g_ref, h_ref):
        xv = x_ref[...]
        r = lax.rsqrt(jnp.mean(xv * xv, axis=-1, keepdims=True) + EPS)
        h_ref[...] = (xv * r * g_ref[...]).astype(BF16)

    return _pcall(
        body, name=name, out_shape=_sds((t, d), BF16), grid=(t // tm,),
        in_specs=[pl.BlockSpec((tm, d), lambda i: (i, 0)), pl.BlockSpec((1, d), lambda i: (0, 0))],
        out_specs=pl.BlockSpec((tm, d), lambda i: (i, 0)), compiler_params=_cp(),
    )(x, g)


def _rms_bwd(dh, x, g, res, *, name, tm=512):
    t, d = x.shape
    tm = min(tm, t)
    has_res = res is not None

    def body(*refs):
        dh_ref, x_ref, g_ref = refs[:3]
        r_ref = refs[3] if has_res else None
        dx_ref, dg_ref = refs[-2], refs[-1]
        xv = x_ref[...]
        r = lax.rsqrt(jnp.mean(xv * xv, axis=-1, keepdims=True) + EPS)
        xh = xv * r
        dhv = dh_ref[...]
        gd = dhv * g_ref[...]
        dx = r * (gd - xh * jnp.mean(gd * xh, axis=-1, keepdims=True))
        if has_res:
            dx = dx + r_ref[...]
        dx_ref[...] = dx

        @pl.when(pl.program_id(0) == 0)
        def _():
            dg_ref[...] = jnp.zeros_like(dg_ref)

        dg_ref[...] += jnp.sum(dhv * xh, axis=0, keepdims=True)

    row = pl.BlockSpec((tm, d), lambda i: (i, 0))
    vec = pl.BlockSpec((1, d), lambda i: (0, 0))
    return _pcall(
        body, name=name, out_shape=(_sds((t, d), F32), _sds((1, d), F32)), grid=(t // tm,),
        in_specs=[row, row, vec] + ([row] if has_res else []), out_specs=(row, vec),
        compiler_params=_cp(dimension_semantics=("arbitrary",)),
    )(*([dh, x, g] + ([res] if has_res else [])))


def _conv_fwd(z, conv_w, conv_b, *, nb, s, d, tc=256):
    nc = d // tc

    def body(cb_ref, cc_ref, cv_ref, w_ref, b_ref, y_ref):
        u = cc_ref[...].astype(F32) * cv_ref[...].astype(F32)
        row = lax.broadcasted_iota(jnp.int32, u.shape, 0)
        u1 = jnp.where(row >= 1, pltpu.roll(u, 1, axis=0), 0.0)
        u2 = jnp.where(row >= 2, pltpu.roll(u, 2, axis=0), 0.0)
        w = w_ref[...]
        y = w[2:3, :] * u + w[1:2, :] * u1 + w[0:1, :] * u2 + b_ref[...]
        y_ref[...] = (cb_ref[...].astype(F32) * y).astype(BF16)

    def part(p):
        return pl.BlockSpec((s, tc), lambda b, j: (b, p * nc + j))

    return _pcall(
        body, name="conv_fwd", out_shape=_sds((nb * s, d), BF16), grid=(nb, nc),
        in_specs=[part(0), part(1), part(2), pl.BlockSpec((3, tc), lambda b, j: (0, j)), pl.BlockSpec((1, tc), lambda b, j: (0, j))],
        out_specs=pl.BlockSpec((s, tc), lambda b, j: (b, j)), compiler_params=_cp(),
    )(z, z, z, conv_w, conv_b)


def _conv_bwd(z, dy, conv_w, conv_b, dz, *, nb, s, d, tc=256, comm=None):
    nc = d // tc
    c_ins, c_in_specs, c_out_shape, c_scratch = _comm_lists(comm)
    n_steps = nc * nb * 3

    def body(*refs):
        _, ins, cin, outs, cout, scr, csem = _split_refs(refs, 0, 7, len(c_ins), 3, len(c_ins), 1)
        cb_ref, cc_ref, cv_ref, dy_ref, w_ref, b_ref, _ = ins
        dz_ref, dw_ref, db_ref = outs
        stash = scr[0]
        b_i, p = pl.program_id(1), pl.program_id(2)
        step = (pl.program_id(0) * nb + b_i) * 3 + p
        _run_comm(comm, cin, cout, csem, step, n_steps, "start")

        @pl.when(p == 0)
        def _():
            cc, cv = cc_ref[...].astype(F32), cv_ref[...].astype(F32)
            u = cc * cv
            n = u.shape[0]
            row = lax.broadcasted_iota(jnp.int32, u.shape, 0)
            u1 = jnp.where(row >= 1, pltpu.roll(u, 1, axis=0), 0.0)
            u2 = jnp.where(row >= 2, pltpu.roll(u, 2, axis=0), 0.0)
            w = w_ref[...]
            y = w[2:3, :] * u + w[1:2, :] * u1 + w[0:1, :] * u2 + b_ref[...]
            dyv = dy_ref[...]
            dconv = dyv * cb_ref[...].astype(F32)
            g1 = jnp.where(row < n - 1, pltpu.roll(dconv, n - 1, axis=0), 0.0)
            g2 = jnp.where(row < n - 2, pltpu.roll(dconv, n - 2, axis=0), 0.0)
            du = w[2:3, :] * dconv + w[1:2, :] * g1 + w[0:1, :] * g2
            stash[0] = (dyv * y).astype(BF16)
            stash[1] = (du * cv).astype(BF16)
            stash[2] = (du * cc).astype(BF16)
            dws = [jnp.sum(dconv * u2, axis=0, keepdims=True), jnp.sum(dconv * u1, axis=0, keepdims=True),
                   jnp.sum(dconv * u, axis=0, keepdims=True)]
            db = jnp.sum(dconv, axis=0, keepdims=True)

            @pl.when(b_i == 0)
            def _():
                for r in range(3):
                    dw_ref[r:r + 1, :] = dws[r]
                db_ref[...] = db

            @pl.when(b_i > 0)
            def _():
                for r in range(3):
                    dw_ref[r:r + 1, :] += dws[r]
                db_ref[...] += db

        dz_ref[...] = stash[p]
        _run_comm(comm, cin, cout, csem, step, n_steps, "end")

    outs = _pcall(
        body, name="conv_bwd",
        out_shape=tuple([_sds(dz.shape, dz.dtype), _sds((3, d), F32), _sds((1, d), F32)] + c_out_shape), grid=(nc, nb, 3),
        in_specs=[
            pl.BlockSpec((s, tc), lambda j, b, p: (b, j)), pl.BlockSpec((s, tc), lambda j, b, p: (b, nc + j)),
            pl.BlockSpec((s, tc), lambda j, b, p: (b, 2 * nc + j)), pl.BlockSpec((s, tc), lambda j, b, p: (b, j)),
            pl.BlockSpec((3, tc), lambda j, b, p: (0, j)), pl.BlockSpec((1, tc), lambda j, b, p: (0, j)),
            HBM_SPEC,
        ] + c_in_specs,
        out_specs=tuple([pl.BlockSpec((s, tc), lambda j, b, p: (b, p * nc + j)),
                         pl.BlockSpec((3, tc), lambda j, b, p: (0, j)), pl.BlockSpec((1, tc), lambda j, b, p: (0, j))]
                        + [HBM_SPEC] * len(c_ins)),
        scratch_shapes=[pltpu.VMEM((3, s, tc), BF16)] + c_scratch,
        input_output_aliases={6: 0},
        compiler_params=_cp(dimension_semantics=("arbitrary", "arbitrary", "arbitrary")),
    )(z, z, z, dy, conv_w, conv_b, dz, *c_ins)
    return outs[0], outs[1], outs[2], list(outs[3:])


def _head_rms(xv, g):
    r = lax.rsqrt(jnp.mean(xv * xv, axis=-1, keepdims=True) + EPS)
    return xv * r * g


def _head_rms_bwd(xv, g, dy):
    r = lax.rsqrt(jnp.mean(xv * xv, axis=-1, keepdims=True) + EPS)
    xh = xv * r
    gd = dy * g
    dx = r * (gd - xh * jnp.mean(gd * xh, axis=-1, keepdims=True))
    return dx, jnp.sum(dy * xh, axis=0, keepdims=True)


def _fox_prep(z, q_g, k_g, *, t, d, tm=256):
    nh = d // HD

    def body(z_ref, qg_ref, kg_ref, q_ref, k_ref, v_ref):
        qg, kg = qg_ref[...], kg_ref[...]
        for h in range(nh):
            c0 = h * HD
            q_ref[:, c0:c0 + HD] = _head_rms(z_ref[:, c0:c0 + HD].astype(F32), qg).astype(BF16)
            k_ref[:, c0:c0 + HD] = _head_rms(z_ref[:, d + c0:d + c0 + HD].astype(F32), kg).astype(BF16)
        v_ref[...] = z_ref[:, 2 * d:3 * d]

    o = pl.BlockSpec((tm, d), lambda i: (i, 0))
    g = pl.BlockSpec((1, HD), lambda i: (0, 0))
    return _pcall(
        body, name="fox_prep", out_shape=(_sds((t, d), BF16),) * 3, grid=(t // tm,),
        in_specs=[pl.BlockSpec((tm, 3 * d), lambda i: (i, 1)), g, g], out_specs=(o, o, o), compiler_params=_cp(),
    )(z, q_g, k_g)


def _lane_cumsum(v, reverse=False):
    n = v.shape[1]
    lane = lax.broadcasted_iota(jnp.int32, v.shape, 1)
    sh = 1
    while sh < n:
        if reverse:
            v = v + jnp.where(lane < n - sh, pltpu.roll(v, n - sh, axis=1), 0.0)
        else:
            v = v + jnp.where(lane >= sh, pltpu.roll(v, sh, axis=1), 0.0)
        sh *= 2
    return v


def _fox_gates(z, f_bias_col, *, nb, s, nh):
    def body(z_ref, b_ref, c_ref):
        ff = z_ref[...].T[0:nh, :] + b_ref[...]
        lf = jnp.minimum(ff, 0.0) - jnp.log(1.0 + jnp.exp(-jnp.abs(ff)))
        c_ref[0] = _lane_cumsum(lf) * SQRT_HD

    return _pcall(
        body, name="fox_gates", out_shape=_sds((nb, nh, s), F32), grid=(nb,),
        in_specs=[pl.BlockSpec((s, HD), lambda b: (b, 0)), pl.BlockSpec((nh, 1), lambda b: (0, 0))],
        out_specs=pl.BlockSpec((1, nh, s), lambda b: (b, 0, 0)), compiler_params=_cp(),
    )(z, f_bias_col)


def _fox_gates_bwd(z, f_bias_col, row_sums, col_sums, dz, *, nb, s, nh, ff_block):
    def body(z_ref, b_ref, rs_ref, cs_ref, dz_in, dz_ref, db_ref, dc_sc):
        b_i, h = pl.program_id(0), pl.program_id(1)
        dc_sc[pl.ds(h, 1), :] = (rs_ref[...] - cs_ref[...]).T[0:1, :]

        @pl.when(h == nh - 1)
        def _():
            ff = z_ref[...].T[0:nh, :] + b_ref[...]
            dlf = _lane_cumsum(dc_sc[...], reverse=True)
            dff = dlf * (1.0 / (1.0 + jnp.exp(ff)))
            pad = jnp.concatenate([dff, jnp.zeros((HD - nh, s), F32)], axis=0)
            dz_ref[...] = pad.T.astype(BF16)
            dbv = jnp.sum(dff, axis=1, keepdims=True)

            @pl.when(b_i == 0)
            def _():
                db_ref[...] = dbv

            @pl.when(b_i > 0)
            def _():
                db_ref[...] += dbv

    hs = pl.BlockSpec((s, HD), lambda b, h: (b, h))
    return _pcall(
        body, name="fox_gates_bwd", out_shape=(_sds(dz.shape, dz.dtype), _sds((nh, 1), F32)), grid=(nb, nh),
        in_specs=[pl.BlockSpec((s, HD), lambda b, h: (b, 0)), pl.BlockSpec((nh, 1), lambda b, h: (0, 0)), hs, hs, HBM_SPEC],
        out_specs=(pl.BlockSpec((s, HD), lambda b, h: (b, ff_block)), pl.BlockSpec((nh, 1), lambda b, h: (0, 0))),
        scratch_shapes=[pltpu.VMEM((nh, s), F32)],
        input_output_aliases={4: 0}, compiler_params=_cp(dimension_semantics=("arbitrary", "arbitrary")),
    )(z, f_bias_col, row_sums, col_sums, dz)


def _fox_pairs(nq, key_major):
    if key_major:
        pairs = [(i, j) for j in range(nq) for i in range(j, nq)]
    else:
        pairs = [(i, j) for i in range(nq) for j in range(i + 1)]
    return jnp.array([p[0] for p in pairs], jnp.int32), jnp.array([p[1] for p in pairs], jnp.int32)


def _with_ones(x):
    return jnp.concatenate([x, jnp.ones_like(x)], axis=1)


def _fox_specs(nb, nh, nq, tq, tk, hp):
    qs = pl.BlockSpec((tq, hp * HD), lambda b, h, p, *tb: (b * nq + tb[0][p], h))
    ks = pl.BlockSpec((tk, hp * HD), lambda b, h, p, *tb: (b * nq + tb[1][p], h))
    cs = pl.BlockSpec((hp, 1, tk), lambda b, h, p, *tb: (b * (nh // hp) + h, 0, tb[1][p]))
    return qs, ks, cs


def _fox_raw(qv, kv, ckv, diag, tq, tk):
    sc = lax.dot_general(qv, kv, (((1,), (1,)), ((), ())), preferred_element_type=F32) - ckv
    if diag:
        rows = lax.broadcasted_iota(jnp.int32, (tq, tk), 0)
        cols = lax.broadcasted_iota(jnp.int32, (tq, tk), 1)
        sc = jnp.where(rows >= cols, sc, NEG)
    return sc


def _fox_probs(qv, kv, ckv, lse_col, diag, tq, tk):
    sc = _fox_raw(qv, kv, ckv, diag, tq, tk)
    return jnp.exp2(sc * EXP2_C - lse_col * LOG2E)


def _fox_call(body, *, name, ins, in_specs, out_shape, out_specs, scratch, grid, tables, comm):
    c_ins, c_in_specs, c_out_shape, c_scratch = _comm_lists(comm)
    n_in, n_out, n_scr, nc = len(ins), len(out_shape), len(scratch), len(c_ins)
    n_steps = grid[0] * grid[1] * grid[2]

    def wrapped(*refs):
        pre, rin, cin, rout, cout, scr, csem = _split_refs(refs, len(tables), n_in, nc, n_out, nc, n_scr)
        step = (pl.program_id(0) * grid[1] + pl.program_id(1)) * grid[2] + pl.program_id(2)
        _run_comm(comm, cin, cout, csem, step, n_steps, "start")
        body(*pre, *rin, *rout, *scr)
        _run_comm(comm, cin, cout, csem, step, n_steps, "end")

    sem = ("arbitrary",) * 3 if comm is not None else ("parallel", "parallel", "arbitrary")
    outs = _pcall(
        wrapped, name=name, out_shape=tuple(list(out_shape) + c_out_shape),
        grid_spec=pltpu.PrefetchScalarGridSpec(
            num_scalar_prefetch=len(tables), grid=grid, in_specs=list(in_specs) + c_in_specs,
            out_specs=tuple(list(out_specs) + [HBM_SPEC] * nc), scratch_shapes=list(scratch) + c_scratch),
        compiler_params=_cp(dimension_semantics=sem),
    )(*tables, *ins, *c_ins)
    return list(outs[:n_out]), list(outs[n_out:])


def _fox_fwd(q, k, v, cks, *, nb, s, nh, tq=512, comm=None):
    tk = tq
    nq = s // tq
    t = nb * s
    hp = FOX_HP
    it, jt = _fox_pairs(nq, key_major=False)

    def body(it_ref, jt_ref, q_ref, k_ref, v_ref, c_ref, o_ref, lse_ref, m_sc, acc_sc):
        p_id = pl.program_id(2)
        i, j = it_ref[p_id], jt_ref[p_id]

        @pl.when(j == 0)
        def _():
            m_sc[...] = jnp.full_like(m_sc, NEG)
            acc_sc[...] = jnp.zeros_like(acc_sc)

        def step(diag):
            for hh in range(hp):
                cols = slice(hh * HD, (hh + 1) * HD)
                sc = _fox_raw(q_ref[:, cols], k_ref[:, cols], c_ref[hh], diag, tq, tk)
                m_old = m_sc[hh]
                m_new = jnp.maximum(m_old, jnp.max(sc, axis=-1, keepdims=True))
                alpha = jnp.exp2((m_old - m_new) * EXP2_C)
                p = jnp.exp2((sc - m_new) * EXP2_C).astype(BF16)
                acc = alpha * acc_sc[hh] + jnp.dot(p, _with_ones(v_ref[:, cols]), preferred_element_type=F32)
                if diag:
                    l = acc[:, HD:2 * HD]
                    o_ref[:, cols] = (acc[:, 0:HD] / l).astype(BF16)
                    lse_ref[:, cols] = m_new * (1.0 / SQRT_HD) + jnp.log(l)
                else:
                    acc_sc[hh] = acc
                    m_sc[hh] = m_new

        @pl.when(j < i)
        def _():
            step(False)

        @pl.when(j == i)
        def _():
            step(True)

    qs, ks, cs = _fox_specs(nb, nh, nq, tq, tk, hp)
    (o, lse), c_out = _fox_call(
        body, name="fox_fwd", ins=[q, k, v, cks], in_specs=[qs, ks, ks, cs],
        out_shape=[_sds((t, nh * HD), BF16), _sds((t, nh * HD), F32)], out_specs=[qs, qs],
        scratch=[pltpu.VMEM((hp, tq, 1), F32), pltpu.VMEM((hp, tq, 2 * HD), F32)],
        grid=(nb, nh // hp, int(it.shape[0])), tables=(it, jt), comm=comm)
    return o, lse, c_out


def _fox_bwd_dq(q, k, v, cks, do, o, lse, *, nb, s, nh, tq=512, comm=None):
    tk = tq
    nq = s // tq
    t = nb * s
    hp = FOX_HP
    it, jt = _fox_pairs(nq, key_major=False)

    def body(it_ref, jt_ref, q_ref, k_ref, v_ref, c_ref, do_ref, o_ref, lse_ref, dq_ref, rs_ref, acc_sc, dl_sc):
        p_id = pl.program_id(2)
        i, j = it_ref[p_id], jt_ref[p_id]

        @pl.when(j == 0)
        def _():
            acc_sc[...] = jnp.zeros_like(acc_sc)
            for hh in range(hp):
                cols = slice(hh * HD, (hh + 1) * HD)
                dl_sc[hh] = jnp.sum(do_ref[:, cols] * o_ref[:, cols].astype(F32), axis=-1, keepdims=True)

        def step(diag):
            for hh in range(hp):
                cols = slice(hh * HD, (hh + 1) * HD)
                kv = k_ref[:, cols]
                p = _fox_probs(q_ref[:, cols], kv, c_ref[hh], lse_ref[:, hh * HD:hh * HD + 1], diag, tq, tk)
                dp = lax.dot_general(do_ref[:, cols].astype(BF16), v_ref[:, cols], (((1,), (1,)), ((), ())),
                                     preferred_element_type=F32)
                ds = (p * (dp - dl_sc[hh])).astype(BF16)
                acc = acc_sc[hh] + jnp.dot(ds, _with_ones(kv), preferred_element_type=F32)
                if diag:
                    dq_ref[:, cols] = acc[:, 0:HD] * (1.0 / SQRT_HD)
                    rs_ref[:, cols] = acc[:, HD:2 * HD]
                else:
                    acc_sc[hh] = acc

        @pl.when(j < i)
        def _():
            step(False)

        @pl.when(j == i)
        def _():
            step(True)

    qs, ks, cs = _fox_specs(nb, nh, nq, tq, tk, hp)
    (dq, rs), c_out = _fox_call(
        body, name="fox_bwd_dq", ins=[q, k, v, cks, do, o, lse], in_specs=[qs, ks, ks, cs, qs, qs, qs],
        out_shape=[_sds((t, nh * HD), F32), _sds((t, nh * HD), F32)], out_specs=[qs, qs],
        scratch=[pltpu.VMEM((hp, tq, 2 * HD), F32), pltpu.VMEM((hp, tq, 1), F32)],
        grid=(nb, nh // hp, int(it.shape[0])), tables=(it, jt), comm=comm)
    return dq, rs, c_out


def _fox_bwd_dkv(q, k, v, cks, do, o, lse, *, nb, s, nh, tq=512, comm=None):
    tk = tq
    nq = s // tq
    t = nb * s
    hp = FOX_HP
    it, jt = _fox_pairs(nq, key_major=True)

    def body(it_ref, jt_ref, q_ref, k_ref, v_ref, c_ref, do_ref, o_ref, lse_ref, dk_ref, dv_ref, cs_ref, dk_sc, dv_sc):
        p_id = pl.program_id(2)
        i, j = it_ref[p_id], jt_ref[p_id]

        def step(diag):
            for hh in range(hp):
                cols = slice(hh * HD, (hh + 1) * HD)
                qv = q_ref[:, cols]
                p = _fox_probs(qv, k_ref[:, cols], c_ref[hh], lse_ref[:, hh * HD:hh * HD + 1], diag, tq, tk)
                dov = do_ref[:, cols]
                dob = dov.astype(BF16)
                dvp = lax.dot_general(p.astype(BF16), dob, (((0,), (0,)), ((), ())), preferred_element_type=F32)
                dp = lax.dot_general(dob, v_ref[:, cols], (((1,), (1,)), ((), ())), preferred_element_type=F32)
                delta = jnp.sum(dov * o_ref[:, cols].astype(F32), axis=-1, keepdims=True)
                ds = (p * (dp - delta)).astype(BF16)
                dkp = lax.dot_general(ds, _with_ones(qv), (((0,), (0,)), ((), ())), preferred_element_type=F32)
                if diag:
                    dk_sc[hh] = dkp
                    dv_sc[hh] = dvp
                else:
                    dk_sc[hh] += dkp
                    dv_sc[hh] += dvp

        @pl.when(i == j)
        def _():
            step(True)

        @pl.when(i > j)
        def _():
            step(False)

        @pl.when(i == nq - 1)
        def _():
            for hh in range(hp):
                cols = slice(hh * HD, (hh + 1) * HD)
                dk_ref[:, cols] = dk_sc[hh, :, 0:HD] * (1.0 / SQRT_HD)
                cs_ref[:, cols] = dk_sc[hh, :, HD:2 * HD]
                dv_ref[:, cols] = dv_sc[hh].astype(BF16)

    qs, ks, cs = _fox_specs(nb, nh, nq, tq, tk, hp)
    (dk, dv, csum), c_out = _fox_call(
        body, name="fox_bwd_dkv", ins=[q, k, v, cks, do, o, lse], in_specs=[qs, ks, ks, cs, qs, qs, qs],
        out_shape=[_sds((t, nh * HD), F32), _sds((t, nh * HD), BF16), _sds((t, nh * HD), F32)], out_specs=[ks, ks, ks],
        scratch=[pltpu.VMEM((hp, tk, 2 * HD), F32), pltpu.VMEM((hp, tk, HD), F32)],
        grid=(nb, nh // hp, int(it.shape[0])), tables=(it, jt), comm=comm)
    return dk, dv, csum, c_out


def _fox_bwd(q, k, v, cks, do, o, lse, *, nb, s, nh, tq=512, comm=None):
    tk = tq
    nq = s // tq
    t = nb * s
    hp = FOX_HP
    it, jt = _fox_pairs(nq, key_major=True)
    done = jnp.array([j for j in range(nq) for _ in range(j, nq)], jnp.int32)

    def body(it_ref, jt_ref, done_ref, q_ref, k_ref, v_ref, c_ref, do_ref, o_ref, lse_ref,
             dq_ref, rs_ref, dk_ref, dv_ref, cs_ref, dq_sc, dk_sc, dv_sc):
        p_id = pl.program_id(2)
        i, j = it_ref[p_id], jt_ref[p_id]
        rows_i = pl.ds(pl.multiple_of(i * tq, tq), tq)

        def step(diag):
            for hh in range(hp):
                cols = slice(hh * HD, (hh + 1) * HD)
                qv, kv = q_ref[:, cols], k_ref[:, cols]
                p = _fox_probs(qv, kv, c_ref[hh], lse_ref[:, hh * HD:hh * HD + 1], diag, tq, tk)
                dov = do_ref[:, cols]
                dob = dov.astype(BF16)
                dvp = lax.dot_general(p.astype(BF16), dob, (((0,), (0,)), ((), ())), preferred_element_type=F32)
                dp = lax.dot_general(dob, v_ref[:, cols], (((1,), (1,)), ((), ())), preferred_element_type=F32)
                delta = jnp.sum(dov * o_ref[:, cols].astype(F32), axis=-1, keepdims=True)
                ds = (p * (dp - delta)).astype(BF16)
                dkp = lax.dot_general(ds, _with_ones(qv), (((0,), (0,)), ((), ())), preferred_element_type=F32)
                dqp = jnp.dot(ds, _with_ones(kv), preferred_element_type=F32)
                if diag:
                    dk_sc[hh] = dkp
                    dv_sc[hh] = dvp
                else:
                    dk_sc[hh] += dkp
                    dv_sc[hh] += dvp

                @pl.when(j == 0)
                def _():
                    dq_sc[hh, rows_i, :] = dqp

                @pl.when(j > 0)
                def _():
                    dq_sc[hh, rows_i, :] += dqp

                if diag:
                    acc = dq_sc[hh, rows_i, :]
                    dq_ref[:, cols] = acc[:, 0:HD] * (1.0 / SQRT_HD)
                    rs_ref[:, cols] = acc[:, HD:2 * HD]

        @pl.when(i == j)
        def _():
            step(True)

        @pl.when(i > j)
        def _():
            step(False)

        @pl.when(i == nq - 1)
        def _():
            for hh in range(hp):
                cols = slice(hh * HD, (hh + 1) * HD)
                dk_ref[:, cols] = dk_sc[hh, :, 0:HD] * (1.0 / SQRT_HD)
                cs_ref[:, cols] = dk_sc[hh, :, HD:2 * HD]
                dv_ref[:, cols] = dv_sc[hh].astype(BF16)

    qs, ks, cs = _fox_specs(nb, nh, nq, tq, tk, hp)
    ds_spec = pl.BlockSpec((tq, hp * HD), lambda b, h, p, *tb: (b * nq + tb[2][p], h))
    (dq, rs, dk, dv, csum), c_out = _fox_call(
        body, name="fox_bwd", ins=[q, k, v, cks, do, o, lse], in_specs=[qs, ks, ks, cs, qs, qs, qs],
        out_shape=[_sds((t, nh * HD), F32), _sds((t, nh * HD), F32), _sds((t, nh * HD), F32), _sds((t, nh * HD), BF16),
                   _sds((t, nh * HD), F32)],
        out_specs=[ds_spec, ds_spec, ks, ks, ks],
        scratch=[pltpu.VMEM((hp, s, 2 * HD), F32), pltpu.VMEM((hp, tk, 2 * HD), F32), pltpu.VMEM((hp, tk, HD), F32)],
        grid=(nb, nh // hp, int(it.shape[0])), tables=(it, jt, done), comm=comm)
    return dq, rs, dk, dv, csum, c_out


def _fox_post(z, dqn, dkn, dv, q_g, k_g, dz, *, t, d, tm=256):
    nh = d // HD

    def body(z_ref, dq_ref, dk_ref, dv_ref, qg_ref, kg_ref, dz_in, dz_ref, dqg_ref, dkg_ref):
        qg, kg = qg_ref[...], kg_ref[...]
        aq = jnp.zeros((1, HD), F32)
        ak = jnp.zeros((1, HD), F32)
        for h in range(nh):
            c0 = h * HD
            dx, sg = _head_rms_bwd(z_ref[:, c0:c0 + HD].astype(F32), qg, dq_ref[:, c0:c0 + HD])
            dz_ref[:, c0:c0 + HD] = dx.astype(BF16)
            aq = aq + sg
            dx, sg = _head_rms_bwd(z_ref[:, d + c0:d + c0 + HD].astype(F32), kg, dk_ref[:, c0:c0 + HD])
            dz_ref[:, d + c0:d + c0 + HD] = dx.astype(BF16)
            ak = ak + sg
        dz_ref[:, 2 * d:3 * d] = dv_ref[...]

        @pl.when(pl.program_id(0) == 0)
        def _():
            dqg_ref[...] = aq
            dkg_ref[...] = ak

        @pl.when(pl.program_id(0) > 0)
        def _():
            dqg_ref[...] += aq
            dkg_ref[...] += ak

    o = pl.BlockSpec((tm, d), lambda i: (i, 0))
    g = pl.BlockSpec((1, HD), lambda i: (0, 0))
    z3 = pl.BlockSpec((tm, 3 * d), lambda i: (i, 1))
    return _pcall(
        body, name="fox_post", out_shape=(_sds(dz.shape, dz.dtype), _sds((1, HD), F32), _sds((1, HD), F32)), grid=(t // tm,),
        in_specs=[z3, o, o, o, g, g, HBM_SPEC], out_specs=(z3, g, g),
        input_output_aliases={6: 0}, compiler_params=_cp(dimension_semantics=("arbitrary",)),
    )(z, dqn, dkn, dv, q_g, k_g, dz)


def _xa_prep_kv(kv, k_g, *, rows, d):
    xd = d // XH

    def body(kv_ref, g_ref, k_ref, v_ref):
        g = g_ref[...]
        for h in range(XH):
            c0 = h * xd
            k_ref[:, c0:c0 + xd] = _head_rms(kv_ref[:, c0:c0 + xd], g).astype(BF16)
        v_ref[...] = kv_ref[:, d:2 * d].astype(BF16)

    return _pcall(body, name="xa_prep_kv", out_shape=(_sds((rows, d), BF16),) * 2, compiler_params=_cp())(kv, k_g)


def _xa_fwd(z, kn, vb, q_g, *, nb, s, d, nm, q_block, tq=512):
    xd = d // XH
    nq = s // tq

    def body(z_ref, k_ref, v_ref, g_ref, y_ref):
        g = g_ref[...]
        for h in range(XH):
            c0 = h * xd
            qn = _head_rms(z_ref[:, c0:c0 + xd].astype(F32), g).astype(BF16)
            sc = lax.dot_general(qn, k_ref[:, c0:c0 + xd], (((1,), (1,)), ((), ())), preferred_element_type=F32)
            sc = sc / math.sqrt(xd)
            p = jnp.exp(sc - jnp.max(sc, axis=-1, keepdims=True))
            p = p / jnp.sum(p, axis=-1, keepdims=True)
            y_ref[:, c0:c0 + xd] = jnp.dot(p.astype(BF16), v_ref[:, c0:c0 + xd], preferred_element_type=F32).astype(BF16)

    kvs = pl.BlockSpec((nm, d), lambda b, i: (b, 0))
    return _pcall(
        body, name="xa_fwd", out_shape=_sds((nb * s, d), BF16), grid=(nb, nq),
        in_specs=[pl.BlockSpec((tq, d), lambda b, i: (b * nq + i, q_block)), kvs, kvs, pl.BlockSpec((1, xd), lambda b, i: (0, 0))],
        out_specs=pl.BlockSpec((tq, d), lambda b, i: (b * nq + i, 0)), compiler_params=_cp(),
    )(z, kn, vb, q_g)


def _xa_bwd(z, kn, vb, q_g, dy, dz, *, nb, s, d, nm, q_block, tq=512):
    xd = d // XH
    nq = s // tq

    def body(z_ref, k_ref, v_ref, g_ref, dy_ref, dz_in, dz_ref, dk_ref, dv_ref, dg_ref):
        b_i, i = pl.program_id(0), pl.program_id(1)
        g = g_ref[...]

        @pl.when(i == 0)
        def _():
            dk_ref[...] = jnp.zeros_like(dk_ref)
            dv_ref[...] = jnp.zeros_like(dv_ref)

        @pl.when((i == 0) & (b_i == 0))
        def _():
            dg_ref[...] = jnp.zeros_like(dg_ref)

        for h in range(XH):
            c0 = h * xd
            xq = z_ref[:, c0:c0 + xd].astype(F32)
            qn = _head_rms(xq, g).astype(BF16)
            kh, vh = k_ref[:, c0:c0 + xd], v_ref[:, c0:c0 + xd]
            sc = lax.dot_general(qn, kh, (((1,), (1,)), ((), ())), preferred_element_type=F32) / math.sqrt(xd)
            p = jnp.exp(sc - jnp.max(sc, axis=-1, keepdims=True))
            p = p / jnp.sum(p, axis=-1, keepdims=True)
            dob = dy_ref[:, c0:c0 + xd].astype(BF16)
            dv_ref[:, c0:c0 + xd] += lax.dot_general(p.astype(BF16), dob, (((0,), (0,)), ((), ())), preferred_element_type=F32)
            dp = lax.dot_general(dob, vh, (((1,), (1,)), ((), ())), preferred_element_type=F32)
            ds = p * (dp - jnp.sum(p * dp, axis=-1, keepdims=True)) / math.sqrt(xd)
            dsb = ds.astype(BF16)
            dqn = jnp.dot(dsb, kh, preferred_element_type=F32)
            dk_ref[:, c0:c0 + xd] += lax.dot_general(dsb, qn, (((0,), (0,)), ((), ())), preferred_element_type=F32)
            dx, sg = _head_rms_bwd(xq, g, dqn)
            dz_ref[:, c0:c0 + xd] = dx.astype(BF16)
            dg_ref[...] += sg

    kvs = pl.BlockSpec((nm, d), lambda b, i: (b, 0))
    qs = pl.BlockSpec((tq, d), lambda b, i: (b * nq + i, q_block))
    gs = pl.BlockSpec((1, xd), lambda b, i: (0, 0))
    return _pcall(
        body, name="xa_bwd",
        out_shape=(_sds(dz.shape, dz.dtype), _sds((nb * nm, d), F32), _sds((nb * nm, d), F32), _sds((1, xd), F32)), grid=(nb, nq),
        in_specs=[qs, kvs, kvs, gs, pl.BlockSpec((tq, d), lambda b, i: (b * nq + i, 0)), HBM_SPEC],
        out_specs=(qs, kvs, kvs, gs), input_output_aliases={5: 0},
        compiler_params=_cp(dimension_semantics=("arbitrary", "arbitrary")),
    )(z, kn, vb, q_g, dy, dz)


def _xa_post_kv(kv, dkn, dv, k_g, *, rows, d):
    xd = d // XH

    def body(kv_ref, dk_ref, dv_ref, g_ref, dkv_ref, dg_ref):
        g = g_ref[...]
        acc = jnp.zeros((1, xd), F32)
        for h in range(XH):
            c0 = h * xd
            dx, sg = _head_rms_bwd(kv_ref[:, c0:c0 + xd], g, dk_ref[:, c0:c0 + xd])
            dkv_ref[:, c0:c0 + xd] = dx.astype(BF16)
            acc = acc + sg
        dkv_ref[:, d:2 * d] = dv_ref[...].astype(BF16)
        dg_ref[...] = acc

    return _pcall(body, name="xa_post_kv", out_shape=(_sds((rows, 2 * d), BF16), _sds((1, xd), F32)), compiler_params=_cp())(
        kv, dkn, dv, k_g)


def _sigmoid(v):
    return 1.0 / (1.0 + jnp.exp(-v))


def _branches_fwd(z, ys, ws, *, t, d, gate_block, tm=512, tn=512, comm=None):
    nj = d // tn
    c_ins, c_in_specs, c_out_shape, c_scratch = _comm_lists(comm)
    n_steps = (t // tm) * nj

    def body(*refs):
        _, ins, cin, outs, cout, _, csem = _split_refs(refs, 0, 9, len(c_ins), 2, len(c_ins), 0)
        ya_ref, yb_ref, yc_ref, wa_ref, wb_ref, wc_ref, ga_ref, gb_ref, gc_ref = ins
        m_ref, p_ref = outs
        step = pl.program_id(0) * nj + pl.program_id(1)
        _run_comm(comm, cin, cout, csem, step, n_steps, "start")
        acc = None
        for q, (y_ref, w_ref, g_ref) in enumerate(((ya_ref, wa_ref, ga_ref), (yb_ref, wb_ref, gb_ref), (yc_ref, wc_ref, gc_ref))):
            pr = jnp.dot(y_ref[...], w_ref[...], preferred_element_type=F32)
            p_ref[q] = pr.astype(BF16)
            term = _sigmoid(g_ref[...].astype(F32)) * pr
            acc = term if acc is None else acc + term
        m_ref[...] = acc.astype(BF16)
        _run_comm(comm, cin, cout, csem, step, n_steps, "end")

    y_spec = pl.BlockSpec((tm, d), lambda i, j: (i, 0))
    w_spec = pl.BlockSpec((d, tn), lambda i, j: (0, j))

    def gate(q):
        return pl.BlockSpec((tm, tn), lambda i, j: (i, (gate_block + q) * nj + j))

    outs = _pcall(
        body, name="branches_fwd", out_shape=tuple([_sds((t, d), BF16), _sds((3, t, d), BF16)] + c_out_shape), grid=(t // tm, nj),
        in_specs=[y_spec] * 3 + [w_spec] * 3 + [gate(0), gate(1), gate(2)] + c_in_specs,
        out_specs=tuple([pl.BlockSpec((tm, tn), lambda i, j: (i, j)), pl.BlockSpec((3, tm, tn), lambda i, j: (0, i, j))]
                        + [HBM_SPEC] * len(c_ins)),
        scratch_shapes=c_scratch, compiler_params=_cp(dimension_semantics=("arbitrary", "arbitrary")),
    )(*ys, *ws, z, z, z, *c_ins)
    return outs[0], outs[1], list(outs[2:])


def _gates_bwd(z, dm, proj, dz, *, t, d, gate_block, tm=512):
    def body(g_ref, dm_ref, p_ref, dz_in, dz_ref, da_ref, db_ref, dc_ref):
        q = pl.program_id(1)
        sg = _sigmoid(g_ref[...].astype(F32))
        dmv = dm_ref[...]
        dz_ref[...] = (dmv * p_ref[0].astype(F32) * sg * (1.0 - sg)).astype(BF16)
        dp = (dmv * sg).astype(BF16)
        @pl.when(q == 0)
        def _():
            da_ref[...] = dp

        @pl.when(q == 1)
        def _():
            db_ref[...] = dp

        @pl.when(q == 2)
        def _():
            dc_ref[...] = dp

    gs = pl.BlockSpec((tm, d), lambda i, q: (i, gate_block + q))
    o = pl.BlockSpec((tm, d), lambda i, q: (i, 0))
    return _pcall(
        body, name="gates_bwd", out_shape=(_sds(dz.shape, dz.dtype),) + (_sds((t, d), BF16),) * 3, grid=(t // tm, 3),
        in_specs=[gs, o, pl.BlockSpec((1, tm, d), lambda i, q: (q, i, 0)), HBM_SPEC], out_specs=(gs, o, o, o),
        input_output_aliases={3: 0}, compiler_params=_cp(dimension_semantics=("arbitrary", "arbitrary")),
    )(z, dm, proj, dz)


def _ffn_in_fwd(h2, wfi_t, *, t, d, dff, tm=512, tn=1408):
    nj = dff // tn

    def body(h_ref, wg_ref, wu_ref, g_ref, u_ref, a_ref):
        hv = h_ref[...]
        dn = (((1,), (1,)), ((), ()))
        gv = lax.dot_general(hv, wg_ref[...], dn, preferred_element_type=F32)
        uv = lax.dot_general(hv, wu_ref[...], dn, preferred_element_type=F32)
        g_ref[...] = gv.astype(BF16)
        u_ref[...] = uv.astype(BF16)
        a_ref[...] = (gv * _sigmoid(gv) * uv).astype(BF16)

    o = pl.BlockSpec((tm, tn), lambda i, j: (i, j))
    return _pcall(
        body, name="ffn_in_fwd", out_shape=(_sds((t, dff), BF16),) * 3, grid=(t // tm, nj),
        in_specs=[pl.BlockSpec((tm, d), lambda i, j: (i, 0)), pl.BlockSpec((tn, d), lambda i, j: (j, 0)),
                  pl.BlockSpec((tn, d), lambda i, j: (nj + j, 0))],
        out_specs=(o, o, o), compiler_params=_cp(),
    )(h2, wfi_t, wfi_t)


def _ffn_out_bwd(dy, wfo, gate, up, *, t, d, dff, tm=512, tn=1408):
    def body(dy_ref, w_ref, g_ref, u_ref, dg_ref, du_ref):
        dav = lax.dot_general(dy_ref[...].astype(BF16), w_ref[...], (((1,), (1,)), ((), ())), preferred_element_type=F32)
        gv = g_ref[...].astype(F32)
        sg = _sigmoid(gv)
        dg_ref[...] = (dav * u_ref[...].astype(F32) * sg * (1.0 + gv * (1.0 - sg))).astype(BF16)
        du_ref[...] = (dav * gv * sg).astype(BF16)

    o = pl.BlockSpec((tm, tn), lambda i, j: (i, j))
    return _pcall(
        body, name="ffn_out_bwd", out_shape=(_sds((t, dff), BF16),) * 2, grid=(t // tm, dff // tn),
        in_specs=[pl.BlockSpec((tm, d), lambda i, j: (i, 0)), pl.BlockSpec((tn, d), lambda i, j: (j, 0)), o, o],
        out_specs=(o, o), compiler_params=_cp(),
    )(dy, wfo, gate, up)


def _loss_head(y, target, *, t, d, tm=512):
    def body(y_ref, t_ref, dy_ref, l_ref):
        e = y_ref[...] - t_ref[...]
        dy_ref[...] = e * (1.0 / d)

        @pl.when(pl.program_id(0) == 0)
        def _():
            l_ref[...] = jnp.zeros_like(l_ref)

        l_ref[...] += jnp.sum(jnp.sum(e * e, axis=-1, keepdims=True), axis=0, keepdims=True) * (0.5 / d)

    o = pl.BlockSpec((tm, d), lambda i: (i, 0))
    dy, l = _pcall(
        body, name="loss_head", out_shape=(_sds((t, d), F32), _sds((1, HD), F32)), grid=(t // tm,),
        in_specs=[o, o], out_specs=(o, pl.BlockSpec((1, HD), lambda i: (0, 0))),
        compiler_params=_cp(dimension_semantics=("arbitrary",)),
    )(y, target)
    return dy, l[0, 0]


def _adamw(w, g, m, v, *, name):
    if w.ndim == 3:
        rows, _, cols = w.shape
        tm = max(tt for tt in range(1, 65) if rows % tt == 0)
        block, imap, grid = (tm, 1, cols), (lambda i, j: (i, 0, 0)), (rows // tm, 1)
    else:
        rows, cols = w.shape
        tm, tn = rows, cols
        for cand in (256, 128, 64, 32, 16, 8):
            if rows > cand and rows % cand == 0:
                tm = cand
                break
        if tm == rows and rows > 256 and cols % 128 == 0:
            tn = 128
        block, imap, grid = (tm, tn), (lambda i, j: (i, j)), (rows // tm, cols // tn)
    c1 = 1.0 - ADAM_B1 ** ADAM_STEP
    c2 = 1.0 - ADAM_B2 ** ADAM_STEP

    def body(w_ref, g_ref, m_ref, v_ref, d_ref, nm_ref, nv_ref):
        gv = g_ref[...]
        mn = ADAM_B1 * m_ref[...] + (1.0 - ADAM_B1) * gv
        vn = ADAM_B2 * v_ref[...] + (1.0 - ADAM_B2) * (gv * gv)
        d_ref[...] = -ADAM_LR * ((mn / c1) / (jnp.sqrt(vn / c2) + ADAM_EPS) + ADAM_WD * w_ref[...])
        nm_ref[...] = mn
        nv_ref[...] = vn

    spec = pl.BlockSpec(block, imap)
    return _pcall(
        body, name=name, out_shape=(_sds(w.shape, F32),) * 3, grid=grid,
        in_specs=[spec] * 4, out_specs=(spec,) * 3, compiler_params=_cp(),
    )(w, g, m, v)


def _pad_rows(a, rows):
    return a if a.shape[0] == rows else jnp.pad(a, ((0, rows - a.shape[0]),) + ((0, 0),) * (a.ndim - 1))


class _Pack:
    def __init__(self, row_sizes):
        self.rows = list(row_sizes)
        self.padded = [-(-r // 16) * 16 for r in self.rows]
        self.offs = [sum(self.padded[:i]) for i in range(len(self.rows))]
        self.total = sum(self.padded)

    def stack(self, blocks):
        return jnp.concatenate([_pad_rows(b.astype(BF16), p) for b, p in zip(blocks, self.padded)], axis=0)

    def full(self, gathered, i):
        r = self.rows[i]
        return gathered[:, self.offs[i]:self.offs[i] + r, :].reshape(N_DEV * r, gathered.shape[2])

    def for_core(self, full_grads, core):
        parts = []
        for gfull, r, p in zip(full_grads, self.rows, self.padded):
            pieces = gfull if isinstance(gfull, (tuple, list)) else (gfull,)
            blk = jnp.concatenate(
                [lax.dynamic_index_in_dim(g.reshape(-1, 2, r, g.shape[1]), core, axis=1, keepdims=False) for g in pieces],
                axis=0).astype(BF16)
            if p != r:
                blk = jnp.pad(blk, ((0, 0), (0, p - r), (0, 0)))
            parts.append(blk)
        return jnp.concatenate(parts, axis=1)

    def shard(self, g_pack, i):
        return g_pack[self.offs[i]:self.offs[i] + self.rows[i], :]


def kernel(x, mem, norm1_g, w_in, conv_w, conv_b, fox_f_bias, fox_q_g, fox_k_g, mem_norm_g, w_mem_kv, xa_q_g, xa_k_g, w_br_conv, w_br_fox, w_br_xa, w_o, norm2_g, w_ffn_in, w_ffn_out, loss_target, m_norm1_g, m_w_in, m_conv_w, m_conv_b, m_fox_f_bias, m_fox_q_g, m_fox_k_g, m_mem_norm_g, m_w_mem_kv, m_xa_q_g, m_xa_k_g, m_w_br_conv, m_w_br_fox, m_w_br_xa, m_w_o, m_norm2_g, m_w_ffn_in, m_w_ffn_out, v_norm1_g, v_w_in, v_conv_w, v_conv_b, v_fox_f_bias, v_fox_q_g, v_fox_k_g, v_mem_norm_g, v_w_mem_kv, v_xa_q_g, v_xa_k_g, v_w_br_conv, v_w_br_fox, v_w_br_xa, v_w_o, v_norm2_g, v_w_ffn_in, v_w_ffn_out):
    nb, s, d = x.shape
    t = nb * s
    nm = mem.shape[1]
    nh = d // HD
    dff = w_ffn_out.shape[1] * N_DEV
    n_in = w_in.shape[2] * N_DEV
    n_in_pad = -(-n_in // 1152) * 1152
    ff_block = (10 * d) // HD
    my_c = lax.axis_index("c")
    my_q = 2 * lax.axis_index("x") + lax.axis_index("y")

    pack_a = _Pack([w_in.shape[2]])
    pack_b = _Pack([w_mem_kv.shape[2], w_br_conv.shape[1], w_br_fox.shape[1], w_br_xa.shape[1], w_o.shape[1],
                    w_ffn_in.shape[2], w_ffn_out.shape[1]])
    gather_2 = _Pack([w_mem_kv.shape[2], w_br_conv.shape[1], w_br_fox.shape[1], w_br_xa.shape[1], w_o.shape[1]])
    stack_a = pack_a.stack([w_in[0].T])
    stack_2 = gather_2.stack([w_mem_kv[0].T, w_br_conv[0], w_br_fox[0], w_br_xa[0], w_o[0]])
    stack_fi = w_ffn_in[0].T.astype(BF16)
    stack_fo = w_ffn_out[0].astype(BF16)
    cw_block = _pad_rows(conv_w[0], 8)
    rows_a = w_in.shape[2]
    my_x, my_y = lax.axis_index("x"), lax.axis_index("y")
    arrival = [(my_x, my_y, my_c), (my_x, my_y, 1 - my_c)]
    for cx, cy in [(1 - my_x, my_y), (my_x, 1 - my_y), (1 - my_x, 1 - my_y)]:
        arrival += [(cx, cy, my_c), (cx, cy, 1 - my_c)]
    order = jnp.stack([4 * ax + 2 * ay + ac for ax, ay, ac in arrival]).astype(jnp.int32)

    x2 = x.reshape(t, d)
    mem2 = mem.reshape(nb * nm, d)
    tgt2 = loss_target.reshape(t, d)
    h1 = _rms_fwd(x2, norm1_g, name="rms1_fwd")
    z, slabs, _ = _z_gathered(h1, stack_a, order, t=t, d=d, n_pad=N_DEV * SLAB_WIN, rows=rows_a)
    z = _z_fix(h1, slabs, z, t=t, d=d, rows=rows_a)
    ff_rows = _pad_rows(slabs[N_DEV - 1, rows_a - nh:rows_a, :], HD)
    zff = _mm(h1, ff_rows, m=t, n=HD, k=d, tb=True, tm=1024, name="mm_z_ff")
    qn, kn, vb = _fox_prep(z, fox_q_g, fox_k_g, t=t, d=d)
    f_bias_col = fox_f_bias.reshape(nh, 1)
    ck_rows = _fox_gates(zff, f_bias_col, nb=nb, s=s, nh=nh).reshape(nb * nh, 1, s)
    y_fox, lse, (gathered_2, cw_g, gathered_fo) = _fox_fwd(qn, kn, vb, ck_rows, nb=nb, s=s, nh=nh,
                                                           comm=_AllGather([stack_2, cw_block, stack_fo]))
    wkv_t, wbc, wbf, wbx, wo = (gather_2.full(gathered_2, i) for i in range(5))
    wfo = gathered_fo.reshape(dff, d)
    conv_w_full = jnp.transpose(cw_g[:, 0:3, :], (1, 0, 2)).reshape(3, d)

    y_conv = _conv_fwd(z, conv_w_full, conv_b, nb=nb, s=s, d=d)
    mem_n = _rms_fwd(mem2, mem_norm_g, name="rms_mem_fwd")
    kv = _mm(mem_n, wkv_t, m=nb * nm, n=2 * d, k=d, tb=True, name="mm_kv")
    xkn, xvb = _xa_prep_kv(kv, xa_k_g, rows=nb * nm, d=d)
    y_xa = _xa_fwd(z, xkn, xvb, xa_q_g, nb=nb, s=s, d=d, nm=nm, q_block=6)
    merged, proj, (gathered_fi,) = _branches_fwd(z, (y_conv, y_fox, y_xa), (wbc, wbf, wbx), t=t, d=d, gate_block=7,
                                                 comm=_AllGather([stack_fi]))
    wfi_t = gathered_fi.reshape(2 * dff, d)
    x1 = _mm(merged, wo, m=t, n=d, k=d, res=x2, name="mm_x1")
    h2 = _rms_fwd(x1, norm2_g, name="rms2_fwd")
    ff_gate, ff_up, act = _ffn_in_fwd(h2, wfi_t, t=t, d=d, dff=dff)
    y = _mm(act, wfo, m=t, n=d, k=dff, res=x1, tk=dff, name="mm_y")
    dy, loss_local = _loss_head(y, tgt2, t=t, d=d)

    g_wfo = _mm(act, dy, m=dff, n=d, k=t, ta=True, tm=1408, tk=2048, out_dtype=BF16, name="mm_g_wfo")
    dgate, dup = _ffn_out_bwd(dy, wfo, ff_gate, ff_up, t=t, d=d, dff=dff)
    dh2 = _mm(dgate, wfi_t, m=t, n=d, k=dff, tk=dff, name="mm_dh2_g")
    dh2 = _mm(dup, wfi_t, m=t, n=d, k=dff, tk=dff, b_off=(1, 0), res=dh2, name="mm_dh2_u")
    g_wfi_g = _mm(dgate, h2, m=dff, n=d, k=t, ta=True, tm=1408, tk=2048, out_dtype=BF16, name="mm_g_wfi_g")
    g_wfi_u = _mm(dup, h2, m=dff, n=d, k=t, ta=True, tm=1408, tk=2048, out_dtype=BF16, name="mm_g_wfi_u")
    dx1, g_norm2 = _rms_bwd(dh2, x1, norm2_g, dy, name="rms2_bwd")
    dmerged = _mm(dx1, wo, m=t, n=d, k=d, tb=True, name="mm_dmerged")
    g_wo = _mm(merged, dx1, m=d, n=d, k=t, ta=True, tk=2048, out_dtype=BF16, name="mm_g_wo")

    dz = lax.empty((t, n_in_pad), BF16)
    dz, dpa, dpb, dpc = _gates_bwd(z, dmerged, proj, dz, t=t, d=d, gate_block=7)
    dy_conv = _mm(dpa, wbc, m=t, n=d, k=d, tb=True, name="mm_dy_conv")
    g_wbc = _mm(y_conv, dpa, m=d, n=d, k=t, ta=True, tk=4096, out_dtype=BF16, name="mm_g_wbc")
    dy_fox = _mm(dpb, wbf, m=t, n=d, k=d, tb=True, name="mm_dy_fox")
    g_wbf = _mm(y_fox, dpb, m=d, n=d, k=t, ta=True, tk=4096, out_dtype=BF16, name="mm_g_wbf")
    dy_xa = _mm(dpc, wbx, m=t, n=d, k=d, tb=True, name="mm_dy_xa")
    g_wbx = _mm(y_xa, dpc, m=d, n=d, k=t, ta=True, tk=4096, out_dtype=BF16, name="mm_g_wbx")

    dz, dxkn, dxv, g_xa_q = _xa_bwd(z, xkn, xvb, xa_q_g, dy_xa, dz, nb=nb, s=s, d=d, nm=nm, q_block=6)
    dkv, g_xa_k = _xa_post_kv(kv, dxkn, dxv, xa_k_g, rows=nb * nm, d=d)
    g_wkv_t = _mm(dkv, mem_n, m=2 * d, n=d, k=nb * nm, ta=True, out_dtype=BF16, name="mm_g_wkv")
    dmem_n = _mm(dkv, wkv_t, m=nb * nm, n=d, k=2 * d, tk=2 * d, name="mm_dmem")
    _, g_mem_norm = _rms_bwd(dmem_n, mem2, mem_norm_g, None, name="rms_mem_bwd")

    grads_b = [g_wkv_t, g_wbc, g_wbf, g_wbx, g_wo, (g_wfi_g, g_wfi_u), g_wfo]
    keep_b, send_b = pack_b.for_core(grads_b, my_c), pack_b.for_core(grads_b, 1 - my_c)
    dz, g_conv_w, g_conv_b, (from_sibling_b,) = _conv_bwd(z, dy_conv, conv_w_full, conv_b, dz, nb=nb, s=s, d=d,
                                                          comm=_SiblingSwap(send_b))
    part_b = _add2(keep_b, from_sibling_b, name="rs_add_sibling_b")
    dqn, ds_rows, dkn, dvb, ds_cols, (from_chips_b,) = _fox_bwd(qn, kn, vb, ck_rows, dy_fox, y_fox, lse, nb=nb, s=s, nh=nh,
                                                                comm=_ChipExchange(part_b))
    g_pack_b = _add4(lax.dynamic_index_in_dim(part_b, my_q, axis=0, keepdims=False), from_chips_b, name="rs_add_chips_b")
    dz, g_fox_q, g_fox_k = _fox_post(z, dqn, dkn, dvb, fox_q_g, fox_k_g, dz, t=t, d=d)
    dz, g_f_bias = _fox_gates_bwd(zff, f_bias_col, ds_rows, ds_cols, dz, nb=nb, s=s, nh=nh, ff_block=ff_block)

    keep_a, from_sibling_a = _g_win_slabs(dz, h1, my_c, t=t, d=d, n_slab=N_DEV, rows=rows_a, rp=slabs.shape[1])
    part_a = _add2(keep_a, from_sibling_a, name="rs_add_sibling_a")
    dh1, (from_chips_a,) = _dh1_from_slabs(dz, slabs, t=t, d=d, rows=rows_a, comm=_ChipExchange(part_a))
    g_pack_a = _add4(lax.dynamic_index_in_dim(part_a, my_q, axis=0, keepdims=False), from_chips_a, name="rs_add_chips_a",
                     rows_apart=True)
    dx, g_norm1 = _rms_bwd(dh1, x2, norm1_g, dx1, name="rms1_bwd")

    xd = d // XH
    tail = jnp.concatenate(
        [_pad_rows(g_f_bias.reshape(nh), HD).reshape(1, HD), g_fox_q, g_fox_k, g_xa_q, g_xa_k,
         jnp.zeros((1, d - 3 * HD - 2 * xd), F32)], axis=1)
    small = jnp.concatenate([g_norm1, g_norm2, g_conv_w, g_conv_b, g_mem_norm, tail], axis=0)
    small = _all_reduce_small(small)
    gs_conv_w = lax.dynamic_slice_in_dim(small[2:5], (2 * my_q + my_c) * (d // N_DEV), d // N_DEV, axis=1)

    grads = {
        "norm1_g": small[0:1], "w_in": g_pack_a, "conv_w": gs_conv_w, "conv_b": small[5:6],
        "fox_f_bias": small[7:8, 0:nh], "fox_q_g": small[7:8, HD:2 * HD], "fox_k_g": small[7:8, 2 * HD:3 * HD],
        "mem_norm_g": small[6:7], "w_mem_kv": pack_b.shard(g_pack_b, 0), "xa_q_g": small[7:8, 3 * HD:3 * HD + xd],
        "xa_k_g": small[7:8, 3 * HD + xd:3 * HD + 2 * xd], "w_br_conv": pack_b.shard(g_pack_b, 1),
        "w_br_fox": pack_b.shard(g_pack_b, 2), "w_br_xa": pack_b.shard(g_pack_b, 3), "w_o": pack_b.shard(g_pack_b, 4),
        "norm2_g": small[1:2], "w_ffn_in": pack_b.shard(g_pack_b, 5), "w_ffn_out": pack_b.shard(g_pack_b, 6),
    }
    transposed = {"w_in", "w_mem_kv", "w_ffn_in"}
    weights = {
        "norm1_g": (norm1_g, m_norm1_g, v_norm1_g), "w_in": (w_in, m_w_in, v_w_in), "conv_w": (conv_w, m_conv_w, v_conv_w),
        "conv_b": (conv_b, m_conv_b, v_conv_b), "fox_f_bias": (fox_f_bias, m_fox_f_bias, v_fox_f_bias),
        "fox_q_g": (fox_q_g, m_fox_q_g, v_fox_q_g), "fox_k_g": (fox_k_g, m_fox_k_g, v_fox_k_g),
        "mem_norm_g": (mem_norm_g, m_mem_norm_g, v_mem_norm_g), "w_mem_kv": (w_mem_kv, m_w_mem_kv, v_w_mem_kv),
        "xa_q_g": (xa_q_g, m_xa_q_g, v_xa_q_g), "xa_k_g": (xa_k_g, m_xa_k_g, v_xa_k_g),
        "w_br_conv": (w_br_conv, m_w_br_conv, v_w_br_conv), "w_br_fox": (w_br_fox, m_w_br_fox, v_w_br_fox),
        "w_br_xa": (w_br_xa, m_w_br_xa, v_w_br_xa), "w_o": (w_o, m_w_o, v_w_o), "norm2_g": (norm2_g, m_norm2_g, v_norm2_g),
        "w_ffn_in": (w_ffn_in, m_w_ffn_in, v_w_ffn_in), "w_ffn_out": (w_ffn_out, m_w_ffn_out, v_w_ffn_out),
    }
    out_g, out_d, out_m, out_v = [], [], [], []
    for name, (w, m, v) in weights.items():
        shape = w.shape
        if name == "w_in":
            w3, m3, v3 = (a.transpose(2, 0, 1) for a in (w, m, v))
            dl, mn, vn = _adamw(w3, g_pack_a, m3, v3, name="adamw_" + name)
            out_g.append(g_pack_a[:shape[2]].transpose(1, 2, 0))
            out_d.append(dl.transpose(1, 2, 0))
            out_m.append(mn.transpose(1, 2, 0))
            out_v.append(vn.transpose(1, 2, 0))
            continue
        tr = name in transposed

        def as2d(a):
            a = a.reshape(shape[-2], shape[-1])
            return a.T if tr else a

        def back(a):
            return (a.T if tr else a).reshape(shape)

        w2 = as2d(w)
        g2 = grads[name].reshape(w2.shape)
        dl, mn, vn = _adamw(w2, g2, as2d(m), as2d(v), name="adamw_" + name)
        out_g.append(back(g2))
        out_d.append(back(dl))
        out_m.append(back(mn))
        out_v.append(back(vn))

    loss = lax.psum(loss_local, ("x", "y", "c"))
    return (loss, dx.reshape(nb, s, d), *out_g, *out_d, *out_m, *out_v)
```

```python
import math

import jax
import jax.numpy as jnp
from jax import lax
from jax.experimental import pallas as pl
from jax.experimental.pallas import tpu as pltpu

F32, BF16 = jnp.float32, jnp.bfloat16
EPS = 1e-6
HD = 128
XH = 4
N_DEV = 8
MESH = pl.DeviceIdType.MESH
VMEM_LIMIT = 52 * 1024 * 1024
NEG = -1e30
SQRT_HD = math.sqrt(HD)
EXP2_C = math.log2(math.e) / SQRT_HD
LOG2E = math.log2(math.e)
FOX_HP = 4

ADAM_LR, ADAM_B1, ADAM_B2, ADAM_EPS, ADAM_WD, ADAM_STEP = 0.001, 0.9, 0.999, 1e-08, 0.01, 10


def _cp(**kw):
    return pltpu.CompilerParams(vmem_limit_bytes=VMEM_LIMIT, **kw)


def _pcall(body, **kw):
    return pl.pallas_call(body, **kw)


def _sds(shape, dtype):
    return jax.ShapeDtypeStruct(shape, dtype)


HBM_SPEC = pl.BlockSpec(memory_space=pl.ANY)


def _mesh_pos():
    return lax.axis_index("x"), lax.axis_index("y"), lax.axis_index("c")


class _AllGather:
    def __init__(self, blocks, forward_at=None):
        self.blocks = list(blocks)
        self.forward_at = forward_at
        n = len(self.blocks)
        self.out_shape = [_sds((N_DEV,) + b.shape, b.dtype) for b in self.blocks]
        self.scratch = [pltpu.SemaphoreType.DMA((n, 7)), pltpu.SemaphoreType.DMA((n, 7)), pltpu.SemaphoreType.DMA((n,))]

    def bind(self, ins, outs, sems):
        n = len(ins)
        send_sems, recv_sems, local_sems = sems
        x, y, c = _mesh_pos()
        me, sibling = (x, y, c), (x, y, 1 - c)
        chips = [(1 - x, y), (x, 1 - y), (1 - x, 1 - y)]

        def slab(t, dev):
            return outs[t].at[4 * dev[0] + 2 * dev[1] + dev[2]]

        def copy(t, k, block, to, src=None):
            dst = slab(t, block)
            return pltpu.make_async_remote_copy(
                src_ref=dst if src is None else src, dst_ref=dst, send_sem=send_sems.at[t, k], recv_sem=recv_sems.at[t, k],
                device_id=to, device_id_type=MESH)

        mine = [pltpu.make_async_copy(ins[t], slab(t, me), local_sems.at[t]) for t in range(n)]
        first = []
        for t in range(n):
            first.append(copy(t, 0, me, sibling, src=ins[t]))
            first += [copy(t, 1 + j, me, (*chip, c), src=ins[t]) for j, chip in enumerate(chips)]
        passed = [copy(t, 4 + j, (*chip, c), sibling) for j, chip in enumerate(chips) for t in range(n)]

        def start():
            for cp in mine + first:
                cp.start()

        def mid():
            for j, chip in enumerate(chips):
                for t in range(n):
                    copy(t, 1 + j, (*chip, c), me).wait_recv()
                    passed[j * n + t].start()

        def finish():
            for t in range(n):
                copy(t, 0, sibling, me).wait_recv()
                for j, chip in enumerate(chips):
                    copy(t, 4 + j, (*chip, 1 - c), me).wait_recv()
            for cp in first + passed:
                cp.wait_send()
            for cp in mine:
                cp.wait()

        return start, mid, finish


class _SiblingSwap:
    def __init__(self, send):
        self.blocks = [send]
        self.out_shape = [_sds(send.shape, send.dtype)]
        self.scratch = [pltpu.SemaphoreType.DMA, pltpu.SemaphoreType.DMA]

    def bind(self, ins, outs, sems):
        x, y, c = _mesh_pos()
        cp = pltpu.make_async_remote_copy(src_ref=ins[0], dst_ref=outs[0], send_sem=sems[0], recv_sem=sems[1],
                                          device_id=(x, y, 1 - c), device_id_type=MESH)
        return cp.start, (lambda: None), cp.wait


class _ChipExchange:
    def __init__(self, part):
        self.blocks = [part]
        self.out_shape = [_sds((3,) + part.shape[1:], part.dtype)]
        self.scratch = [pltpu.SemaphoreType.DMA((3,)), pltpu.SemaphoreType.DMA((3,))]

    def bind(self, ins, outs, sems):
        x, y, c = _mesh_pos()
        chips = [(1 - x, y), (x, 1 - y), (1 - x, 1 - y)]
        cps = [
            pltpu.make_async_remote_copy(src_ref=ins[0].at[2 * cx + cy], dst_ref=outs[0].at[j], send_sem=sems[0].at[j],
                                         recv_sem=sems[1].at[j], device_id=(cx, cy, c), device_id_type=MESH)
            for j, (cx, cy) in enumerate(chips)
        ]

        def start():
            for cp in cps:
                cp.start()

        def finish():
            for cp in cps:
                cp.wait()

        return start, (lambda: None), finish


def _comm_lists(comm):
    if comm is None:
        return [], [], [], []
    n = len(comm.blocks)
    return list(comm.blocks), [HBM_SPEC] * n, list(comm.out_shape), list(comm.scratch)


def _split_refs(refs, n_pre, n_in, n_cin, n_out, n_cout, n_scr):
    cuts = [n_pre, n_in, n_cin, n_out, n_cout, n_scr]
    out, at = [], 0
    for c in cuts:
        out.append(refs[at:at + c])
        at += c
    out.append(refs[at:])
    return out


def _run_comm(comm, cin, cout, csem, step, n_steps, when="start"):
    if comm is None:
        return
    start, mid, finish = comm.bind(cin, cout, csem)
    if when == "start":
        pl.when(step == 0)(start)
    else:
        frac = getattr(comm, "forward_at", None)
        mid_step = max(n_steps - 2, 0) if frac is None else min(int(frac * n_steps), max(n_steps - 2, 0))
        pl.when(step == mid_step)(mid)
        pl.when(step == n_steps - 1)(finish)


def _comm_only(comm, name):
    ins, in_specs, out_shape, scratch = _comm_lists(comm)
    n = len(ins)

    def body(*refs):
        start, mid, finish = comm.bind(refs[:n], refs[n:2 * n], refs[2 * n:])
        start()
        mid()
        finish()

    outs = _pcall(body, name=name, out_shape=tuple(out_shape), in_specs=in_specs, out_specs=tuple([HBM_SPEC] * n),
                  scratch_shapes=scratch, compiler_params=_cp())(*ins)
    return list(outs)


def _all_reduce_small(v):
    rows, cols = v.shape

    def body(v_ref, o_ref, slots, send_sems, recv_sems):
        x, y, c = _mesh_pos()
        me = 4 * x + 2 * y + c
        slots[me] = v_ref[...]
        cps = []
        for k in range(1, N_DEV):
            px, py, pc = x ^ (k >> 2), y ^ ((k >> 1) & 1), c ^ (k & 1)
            cps.append(pltpu.make_async_remote_copy(
                src_ref=v_ref, dst_ref=slots.at[me], send_sem=send_sems.at[k - 1], recv_sem=recv_sems.at[k - 1],
                device_id=(px, py, pc), device_id_type=MESH))
        for cp in cps:
            cp.start()
        for cp in cps:
            cp.wait()
        acc = slots[0]
        for k in range(1, N_DEV):
            acc = acc + slots[k]
        o_ref[...] = acc

    return _pcall(
        body, name="all_reduce_small", out_shape=_sds((rows, cols), F32),
        in_specs=[pl.BlockSpec(memory_space=pltpu.VMEM)], out_specs=pl.BlockSpec(memory_space=pltpu.VMEM),
        scratch_shapes=[pltpu.VMEM((N_DEV, rows, cols), F32), pltpu.SemaphoreType.DMA((7,)), pltpu.SemaphoreType.DMA((7,))],
        compiler_params=_cp(),
    )(v)


def _row_tile(rows, cap=1040):
    return max(tt for tt in range(16, cap + 1, 16) if rows % tt == 0)


def _add2(a, b, *, name):
    nq, rows, cols = a.shape
    tm = _row_tile(rows)

    def body(a_ref, b_ref, o_ref):
        o_ref[...] = (a_ref[...].astype(F32) + b_ref[...].astype(F32)).astype(BF16)

    spec = pl.BlockSpec((1, tm, cols), lambda q, i: (q, i, 0))
    return _pcall(body, name=name, out_shape=_sds(a.shape, BF16), grid=(nq, rows // tm),
                  in_specs=[spec, spec], out_specs=spec, compiler_params=_cp())(a, b)


def _add4(own, recv, *, name, rows_apart=False):
    rows, cols = own.shape
    tm = _row_tile(rows, 256 if rows_apart else 1040)

    def body(o_ref, r_ref, g_ref):
        tot = ((o_ref[...].astype(F32) + r_ref[0].astype(F32)) + r_ref[1].astype(F32)) + r_ref[2].astype(F32)
        if rows_apart:
            g_ref[:, 0, :] = tot
        else:
            g_ref[...] = tot

    if rows_apart:
        out_shape, out_spec = _sds((rows, 1, cols), F32), pl.BlockSpec((tm, 1, cols), lambda i: (i, 0, 0))
    else:
        out_shape, out_spec = _sds((rows, cols), F32), pl.BlockSpec((tm, cols), lambda i: (i, 0))
    return _pcall(
        body, name=name, out_shape=out_shape, grid=(rows // tm,),
        in_specs=[pl.BlockSpec((tm, cols), lambda i: (i, 0)), pl.BlockSpec((3, tm, cols), lambda i: (0, i, 0))],
        out_specs=out_spec, compiler_params=_cp(),
    )(own, recv)


def _mm(a, b, *, m, n, k, name, ta=False, tb=False, b_off=(0, 0), res=None, out_dtype=F32, tm=512, tn=1024, tk=1024,
        comm=None):
    tm, tn, tk = min(tm, m), min(tn, n), min(tk, k)
    assert m % tm == 0 and n % tn == 0 and k % tk == 0, (name, m, n, k, tm, tn, tk)
    nk = k // tk
    grid = (m // tm, n // tn, nk)
    bo0, bo1 = b_off
    if ta:
        a_spec = pl.BlockSpec((tk, tm), lambda i, j, kk: (kk, i))
    else:
        a_spec = pl.BlockSpec((tm, tk), lambda i, j, kk: (i, kk))
    if tb:
        b_spec = pl.BlockSpec((tn, tk), lambda i, j, kk: (j + bo0, kk + bo1))
    else:
        b_spec = pl.BlockSpec((tk, tn), lambda i, j, kk: (kk + bo0, j + bo1))
    o_spec = pl.BlockSpec((tm, tn), lambda i, j, kk: (i, j))
    dn = (((0 if ta else 1,), (1 if tb else 0,)), ((), ()))
    has_res = res is not None
    c_ins, c_in_specs, c_out_shape, c_scratch = _comm_lists(comm)
    n_in = 3 if has_res else 2
    n_scr = 1 if nk > 1 else 0

    def body(*refs):
        _, ins, cin, outs, cout, scr, csem = _split_refs(refs, 0, n_in, len(c_ins), 1, len(c_ins), n_scr)
        a_ref, b_ref = ins[0], ins[1]
        r_ref = ins[2] if has_res else None
        o_ref = outs[0]
        step = (pl.program_id(0) * grid[1] + pl.program_id(1)) * nk + pl.program_id(2)
        _run_comm(comm, cin, cout, csem, step, grid[0] * grid[1] * nk, "start")
        p = lax.dot_general(a_ref[...].astype(BF16), b_ref[...].astype(BF16), dn, preferred_element_type=F32)

        def finish(acc):
            if has_res:
                acc = acc + r_ref[...]
            o_ref[...] = acc.astype(out_dtype)

        if nk == 1:
            finish(p)
        else:
            acc_ref = scr[0]
            kk = pl.program_id(2)

            @pl.when(kk == 0)
            def _():
                acc_ref[...] = p

            @pl.when(kk > 0)
            def _():
                acc_ref[...] += p

            @pl.when(kk == nk - 1)
            def _():
                finish(acc_ref[...])

        _run_comm(comm, cin, cout, csem, step, grid[0] * grid[1] * nk, "end")

    ins = [a, b] + ([res] if has_res else [])
    in_specs = [a_spec, b_spec] + ([o_spec] if has_res else [])
    sem = ("arbitrary",) * 3 if comm is not None else ("parallel", "parallel", "arbitrary")
    outs = _pcall(
        body, name=name, out_shape=tuple([_sds((m, n), out_dtype)] + c_out_shape), grid=grid,
        in_specs=in_specs + c_in_specs, out_specs=tuple([o_spec] + [HBM_SPEC] * len(c_ins)),
        scratch_shapes=([pltpu.VMEM((tm, tn), F32)] if nk > 1 else []) + c_scratch,
        compiler_params=_cp(dimension_semantics=sem),
    )(*(ins + c_ins))
    return outs[0] if comm is None else (outs[0], list(outs[1:]))


SLAB_WIN = 1280
SLAB_TAIL = 48


def _z_gathered(h1, own_slab, order, *, t, d, n_pad, rows, comm=None, tm=1024):
    rp = own_slab.shape[0]
    n_slab = N_DEV
    tm = min(tm, t)
    ni = t // tm
    sh = rows - SLAB_WIN
    assert sh >= 0 and (n_slab - 1) * sh <= 128
    c_ins, c_in_specs, c_out_shape, c_scratch = _comm_lists(comm)
    n_steps = n_slab * ni

    def body(*refs):
        pre, ins, cin, outs, cout, scr, csem = _split_refs(refs, 1, 2, len(c_ins), 2, len(c_ins), 5)
        order_ref = pre[0]
        h_ref, own_ref = ins
        o_ref, slabs_ref = outs
        slab_vm, send_sems, recv_sems, local_sem, load_sem = scr
        p, i = pl.program_id(0), pl.program_id(1)
        step = p * ni + i
        x, y, c = _mesh_pos()
        me, sibling = (x, y, c), (x, y, 1 - c)
        chips = [(1 - x, y), (x, 1 - y), (1 - x, 1 - y)]

        def slab(dev):
            return slabs_ref.at[4 * dev[0] + 2 * dev[1] + dev[2]]

        def copy(k, block, to, src=None):
            dst = slab(block)
            return pltpu.make_async_remote_copy(
                src_ref=dst if src is None else src, dst_ref=dst, send_sem=send_sems.at[k], recv_sem=recv_sems.at[k],
                device_id=to, device_id_type=MESH)

        mine = pltpu.make_async_copy(own_ref, slab(me), local_sem)
        first = [copy(0, me, sibling, src=own_ref)] + [copy(1 + j, me, (*chip, c), src=own_ref) for j, chip in enumerate(chips)]
        passed = [copy(4 + j, (*chip, c), sibling) for j, chip in enumerate(chips)]

        @pl.when(step == 0)
        def _():
            mine.start()
            for cp in first:
                cp.start()

        _run_comm(comm, cin, cout, csem, step, n_steps, "start")

        def load(src):
            cp = pltpu.make_async_copy(src, slab_vm, load_sem)
            cp.start()
            cp.wait()

        @pl.when(i == 0)
        def _():
            @pl.when(p == 0)
            def _():
                load(own_ref)

            @pl.when(p == 1)
            def _():
                copy(0, sibling, me).wait_recv()
                load(slab(sibling))

            for j, chip in enumerate(chips):
                @pl.when(p == 2 + 2 * j)
                def _():
                    copy(1 + j, (*chip, c), me).wait_recv()
                    passed[j].start()
                    load(slab((*chip, c)))

                @pl.when(p == 3 + 2 * j)
                def _():
                    copy(4 + j, (*chip, 1 - c), me).wait_recv()
                    load(slab((*chip, 1 - c)))

        k = order_ref[p]
        y_k = lax.dot_general(h_ref[...], slab_vm[...], (((1,), (1,)), ((), ())), preferred_element_type=F32)
        o_ref[...] = pltpu.roll(y_k[:, 0:SLAB_WIN], k * sh, axis=1).astype(BF16)

        @pl.when(step == n_steps - 1)
        def _():
            for cp in first + passed:
                cp.wait_send()
            mine.wait()

        _run_comm(comm, cin, cout, csem, step, n_steps, "end")

    outs = _pcall(
        body, name="mm_z", out_shape=tuple([_sds((t, n_pad), BF16), _sds((n_slab, rp, d), BF16)] + c_out_shape),
        grid_spec=pltpu.PrefetchScalarGridSpec(
            num_scalar_prefetch=1, grid=(n_slab, ni),
            in_specs=[pl.BlockSpec((tm, d), lambda p, i, order: (i, 0)), HBM_SPEC] + c_in_specs,
            out_specs=tuple([pl.BlockSpec((tm, SLAB_WIN), lambda p, i, order: (i, order[p])), HBM_SPEC] + [HBM_SPEC] * len(c_ins)),
            scratch_shapes=[pltpu.VMEM((rp, d), BF16), pltpu.SemaphoreType.DMA((7,)), pltpu.SemaphoreType.DMA((7,)),
                            pltpu.SemaphoreType.DMA, pltpu.SemaphoreType.DMA] + c_scratch),
        compiler_params=_cp(dimension_semantics=("arbitrary", "arbitrary")),
    )(order, h1, own_slab, *c_ins)
    return outs[0], outs[1], list(outs[2:])


def _z_fix(h1, slabs, z, *, t, d, rows, tm=4096):
    n_slab, rp, _ = slabs.shape
    tm = min(tm, t)
    sh = rows - SLAB_WIN
    tail_lo = rp - SLAB_TAIL
    assert rows - (n_slab - 1) * sh >= tail_lo and rp % SLAB_TAIL == 0

    def body(h_ref, t_ref, z_ref, o_ref):
        k = pl.program_id(0) + 1
        p = lax.dot_general(h_ref[...], t_ref[0], (((1,), (1,)), ((), ())), preferred_element_type=F32)
        pp = jnp.concatenate([p, jnp.zeros((tm, 128 - SLAB_TAIL), F32)], axis=1)
        pr = pltpu.roll(pp, 128 - (rows - tail_lo) + k * sh, axis=1)
        lane = lax.broadcasted_iota(jnp.int32, (tm, 128), 1)
        o_ref[...] = jnp.where(lane < k * sh, pr.astype(BF16), z_ref[...])

    blk = pl.BlockSpec((tm, 128), lambda b, i: (i, (SLAB_WIN // 128) * (b + 1)))
    return _pcall(
        body, name="mm_z_fix", out_shape=_sds(z.shape, z.dtype), grid=(n_slab - 1, t // tm),
        in_specs=[pl.BlockSpec((tm, d), lambda b, i: (i, 0)),
                  pl.BlockSpec((1, SLAB_TAIL, d), lambda b, i: (b, tail_lo // SLAB_TAIL, 0)), blk],
        out_specs=blk, input_output_aliases={2: 0}, compiler_params=_cp(),
    )(h1, slabs, z)


def _slab_window(a1_ref, a2_ref, k, sh):
    w = jnp.concatenate([a1_ref[...], a2_ref[...]], axis=1)
    width = SLAB_WIN + 128
    return pltpu.roll(w, width - k * sh, axis=1)


def _g_win_slabs(dz, h1, my_c, *, t, d, n_slab, rows, rp, tk=2048):
    sh = rows - SLAB_WIN
    tk = min(tk, t)
    nk = t // tk
    width = SLAB_WIN + 128
    half = n_slab // 2

    def slab_of(p, c):
        return 2 * (p % half) + jnp.where(p < half, 1 - c, c)

    def body(c_ref, a1_ref, a2_ref, h_ref, keep_ref, land_ref, acc, stage, send_sems, recv_sems):
        p, kk = pl.program_id(0), pl.program_id(1)
        x, y, c = _mesh_pos()
        ws = _slab_window(a1_ref, a2_ref, slab_of(p, c), sh)
        prod = lax.dot_general(ws, h_ref[...], (((0,), (0,)), ((), ())), preferred_element_type=F32)

        def push(q):
            return pltpu.make_async_remote_copy(
                src_ref=stage.at[q % 2], dst_ref=land_ref.at[q], send_sem=send_sems.at[q], recv_sem=recv_sems.at[q],
                device_id=(x, y, 1 - c), device_id_type=MESH)

        @pl.when(kk == 0)
        def _():
            acc[...] = prod

        @pl.when(kk > 0)
        def _():
            acc[...] += prod

        @pl.when(kk == nk - 1)
        def _():
            res = acc[0:rp, :].astype(BF16)
            for q in range(half):
                @pl.when(p == q)
                def _():
                    if q >= 2:
                        push(q - 2).wait_send()
                    stage[q % 2] = res
                    push(q).start()

            @pl.when(p >= half)
            def _():
                keep_ref[0] = res

        @pl.when((p == n_slab - 1) & (kk == nk - 1))
        def _():
            for q in range(max(half - 2, 0), half):
                push(q).wait_send()
            for q in range(half):
                push(q).wait_recv()

    outs = _pcall(
        body, name="mm_g_win", out_shape=(_sds((half, rp, d), BF16), _sds((half, rp, d), BF16)),
        grid_spec=pltpu.PrefetchScalarGridSpec(
            num_scalar_prefetch=1, grid=(n_slab, nk),
            in_specs=[pl.BlockSpec((tk, SLAB_WIN), lambda p, kk, c: (kk, slab_of(p, c[0]))),
                      pl.BlockSpec((tk, 128), lambda p, kk, c: (kk, (SLAB_WIN // 128) * (slab_of(p, c[0]) + 1))),
                      pl.BlockSpec((tk, d), lambda p, kk, c: (kk, 0))],
            out_specs=(pl.BlockSpec((1, rp, d), lambda p, kk, c: (jnp.maximum(p - half, 0), 0, 0)), HBM_SPEC),
            scratch_shapes=[pltpu.VMEM((width, d), F32), pltpu.VMEM((2, rp, d), BF16),
                            pltpu.SemaphoreType.DMA((half,)), pltpu.SemaphoreType.DMA((half,))]),
        compiler_params=_cp(dimension_semantics=("arbitrary", "arbitrary")),
    )(my_c.reshape(1).astype(jnp.int32), dz, dz, h1)
    return outs[0], outs[1]


def _dh1_from_slabs(dz, slabs, *, t, d, rows, comm=None, tm=1024):
    n_slab, rp, _ = slabs.shape
    sh = rows - SLAB_WIN
    width = SLAB_WIN + 128
    c_ins, c_in_specs, c_out_shape, c_scratch = _comm_lists(comm)
    grid = (t // tm, n_slab)

    def body(*refs):
        _, ins, cin, outs, cout, scr, csem = _split_refs(refs, 0, 3, len(c_ins), 1, len(c_ins), 1)
        a1_ref, a2_ref, s_ref = ins
        o_ref, acc = outs[0], scr[0]
        k = pl.program_id(1)
        step = pl.program_id(0) * n_slab + k
        _run_comm(comm, cin, cout, csem, step, grid[0] * n_slab, "start")
        ws = _slab_window(a1_ref, a2_ref, k, sh)
        s_end = jnp.concatenate([s_ref[0, SLAB_WIN:rp, :], jnp.zeros((width - rp, d), BF16)], axis=0)
        p = (jnp.dot(ws[:, 0:SLAB_WIN], s_ref[0, 0:SLAB_WIN, :], preferred_element_type=F32)
             + jnp.dot(ws[:, SLAB_WIN:width], s_end, preferred_element_type=F32))

        @pl.when(k == 0)
        def _():
            acc[...] = p

        @pl.when(k > 0)
        def _():
            acc[...] += p

        @pl.when(k == n_slab - 1)
        def _():
            o_ref[...] = acc[...]

        _run_comm(comm, cin, cout, csem, step, grid[0] * n_slab, "end")

    outs = _pcall(
        body, name="mm_dh1", out_shape=tuple([_sds((t, d), F32)] + c_out_shape), grid=grid,
        in_specs=[pl.BlockSpec((tm, SLAB_WIN), lambda i, k: (i, k)),
                  pl.BlockSpec((tm, 128), lambda i, k: (i, (SLAB_WIN // 128) * (k + 1))),
                  pl.BlockSpec((1, rp, d), lambda i, k: (k, 0, 0))] + c_in_specs,
        out_specs=tuple([pl.BlockSpec((tm, d), lambda i, k: (i, 0))] + [HBM_SPEC] * len(c_ins)),
        scratch_shapes=[pltpu.VMEM((tm, d), F32)] + c_scratch,
        compiler_params=_cp(dimension_semantics=("arbitrary", "arbitrary")),
    )(dz, dz, slabs, *c_ins)
    return outs[0], list(outs[1:])


def _rms_fwd(x, g, *, name, tm=512):
    t, d = x.shape
    tm = min(tm, t)

    def body(x_ref, g_ref, h_ref):
        xv = x_ref[...]
        r = lax.rsqrt(jnp.mean(xv * xv, axis=-1, keepdims=True) + EPS)
        h_ref[...] = (xv * r * g_ref[...]).astype(BF16)

    return _pcall(
        body, name=name, out_shape=_sds((t, d), BF16), grid=(t // tm,),
        in_specs=[pl.BlockSpec((tm, d), lambda i: (i, 0)), pl.BlockSpec((1, d), lambda i: (0, 0))],
        out_specs=pl.BlockSpec((tm, d), lambda i: (i, 0)), compiler_params=_cp(),
    )(x, g)


def _rms_bwd(dh, x, g, res, *, name, tm=512):
    t, d = x.shape
    tm = min(tm, t)
    has_res = res is not None

    def body(*refs):
        dh_ref, x_ref, g_ref = refs[:3]
        r_ref = refs[3] if has_res else None
        dx_ref, dg_ref = refs[-2], refs[-1]
        xv = x_ref[...]
        r = lax.rsqrt(jnp.mean(xv * xv, axis=-1, keepdims=True) + EPS)
        xh = xv * r
        dhv = dh_ref[...]
        gd = dhv * g_ref[...]
        dx = r * (gd - xh * jnp.mean(gd * xh, axis=-1, keepdims=True))
        if has_res:
            dx = dx + r_ref[...]
        dx_ref[...] = dx

        @pl.when(pl.program_id(0) == 0)
        def _():
            dg_ref[...] = jnp.zeros_like(dg_ref)

        dg_ref[...] += jnp.sum(dhv * xh, axis=0, keepdims=True)

    row = pl.BlockSpec((tm, d), lambda i: (i, 0))
    vec = pl.BlockSpec((1, d), lambda i: (0, 0))
    return _pcall(
        body, name=name, out_shape=(_sds((t, d), F32), _sds((1, d), F32)), grid=(t // tm,),
        in_specs=[row, row, vec] + ([row] if has_res else []), out_specs=(row, vec),
        compiler_params=_cp(dimension_semantics=("arbitrary",)),
    )(*([dh, x, g] + ([res] if has_res else [])))


def _conv_fwd(z, conv_w, conv_b, *, nb, s, d, tc=256):
    nc = d // tc

    def body(cb_ref, cc_ref, cv_ref, w_ref, b_ref, y_ref):
        u = cc_ref[...].astype(F32) * cv_ref[...].astype(F32)
        row = lax.broadcasted_iota(jnp.int32, u.shape, 0)
        u1 = jnp.where(row >= 1, pltpu.roll(u, 1, axis=0), 0.0)
        u2 = jnp.where(row >= 2, pltpu.roll(u, 2, axis=0), 0.0)
        w = w_ref[...]
        y = w[2:3, :] * u + w[1:2, :] * u1 + w[0:1, :] * u2 + b_ref[...]
        y_ref[...] = (cb_ref[...].astype(F32) * y).astype(BF16)

    def part(p):
        return pl.BlockSpec((s, tc), lambda b, j: (b, p * nc + j))

    return _pcall(
        body, name="conv_fwd", out_shape=_sds((nb * s, d), BF16), grid=(nb, nc),
        in_specs=[part(0), part(1), part(2), pl.BlockSpec((3, tc), lambda b, j: (0, j)), pl.BlockSpec((1, tc), lambda b, j: (0, j))],
        out_specs=pl.BlockSpec((s, tc), lambda b, j: (b, j)), compiler_params=_cp(),
    )(z, z, z, conv_w, conv_b)


def _conv_bwd(z, dy, conv_w, conv_b, dz, *, nb, s, d, tc=256, comm=None):
    nc = d // tc
    c_ins, c_in_specs, c_out_shape, c_scratch = _comm_lists(comm)
    n_steps = nc * nb * 3

    def body(*refs):
        _, ins, cin, outs, cout, scr, csem = _split_refs(refs, 0, 7, len(c_ins), 3, len(c_ins), 1)
        cb_ref, cc_ref, cv_ref, dy_ref, w_ref, b_ref, _ = ins
        dz_ref, dw_ref, db_ref = outs
        stash = scr[0]
        b_i, p = pl.program_id(1), pl.program_id(2)
        step = (pl.program_id(0) * nb + b_i) * 3 + p
        _run_comm(comm, cin, cout, csem, step, n_steps, "start")

        @pl.when(p == 0)
        def _():
            cc, cv = cc_ref[...].astype(F32), cv_ref[...].astype(F32)
            u = cc * cv
            n = u.shape[0]
            row = lax.broadcasted_iota(jnp.int32, u.shape, 0)
            u1 = jnp.where(row >= 1, pltpu.roll(u, 1, axis=0), 0.0)
            u2 = jnp.where(row >= 2, pltpu.roll(u, 2, axis=0), 0.0)
            w = w_ref[...]
            y = w[2:3, :] * u + w[1:2, :] * u1 + w[0:1, :] * u2 + b_ref[...]
            dyv = dy_ref[...]
            dconv = dyv * cb_ref[...].astype(F32)
            g1 = jnp.where(row < n - 1, pltpu.roll(dconv, n - 1, axis=0), 0.0)
            g2 = jnp.where(row < n - 2, pltpu.roll(dconv, n - 2, axis=0), 0.0)
            du = w[2:3, :] * dconv + w[1:2, :] * g1 + w[0:1, :] * g2
            stash[0] = (dyv * y).astype(BF16)
            stash[1] = (du * cv).astype(BF16)
            stash[2] = (du * cc).astype(BF16)
            dws = [jnp.sum(dconv * u2, axis=0, keepdims=True), jnp.sum(dconv * u1, axis=0, keepdims=True),
                   jnp.sum(dconv * u, axis=0, keepdims=True)]
            db = jnp.sum(dconv, axis=0, keepdims=True)

            @pl.when(b_i == 0)
            def _():
                for r in range(3):
                    dw_ref[r:r + 1, :] = dws[r]
                db_ref[...] = db

            @pl.when(b_i > 0)
            def _():
                for r in range(3):
                    dw_ref[r:r + 1, :] += dws[r]
                db_ref[...] += db

        dz_ref[...] = stash[p]
        _run_comm(comm, cin, cout, csem, step, n_steps, "end")

    outs = _pcall(
        body, name="conv_bwd",
        out_shape=tuple([_sds(dz.shape, dz.dtype), _sds((3, d), F32), _sds((1, d), F32)] + c_out_shape), grid=(nc, nb, 3),
        in_specs=[
            pl.BlockSpec((s, tc), lambda j, b, p: (b, j)), pl.BlockSpec((s, tc), lambda j, b, p: (b, nc + j)),
            pl.BlockSpec((s, tc), lambda j, b, p: (b, 2 * nc + j)), pl.BlockSpec((s, tc), lambda j, b, p: (b, j)),
            pl.BlockSpec((3, tc), lambda j, b, p: (0, j)), pl.BlockSpec((1, tc), lambda j, b, p: (0, j)),
            HBM_SPEC,
        ] + c_in_specs,
        out_specs=tuple([pl.BlockSpec((s, tc), lambda j, b, p: (b, p * nc + j)),
                         pl.BlockSpec((3, tc), lambda j, b, p: (0, j)), pl.BlockSpec((1, tc), lambda j, b, p: (0, j))]
                        + [HBM_SPEC] * len(c_ins)),
        scratch_shapes=[pltpu.VMEM((3, s, tc), BF16)] + c_scratch,
        input_output_aliases={6: 0},
        compiler_params=_cp(dimension_semantics=("arbitrary", "arbitrary", "arbitrary")),
    )(z, z, z, dy, conv_w, conv_b, dz, *c_ins)
    return outs[0], outs[1], outs[2], list(outs[3:])


def _head_rms(xv, g):
    r = lax.rsqrt(jnp.mean(xv * xv, axis=-1, keepdims=True) + EPS)
    return xv * r * g


def _head_rms_bwd(xv, g, dy):
    r = lax.rsqrt(jnp.mean(xv * xv, axis=-1, keepdims=True) + EPS)
    xh = xv * r
    gd = dy * g
    dx = r * (gd - xh * jnp.mean(gd * xh, axis=-1, keepdims=True))
    return dx, jnp.sum(dy * xh, axis=0, keepdims=True)


def _fox_prep(z, q_g, k_g, *, t, d, tm=256):
    nh = d // HD

    def body(z_ref, qg_ref, kg_ref, q_ref, k_ref, v_ref):
        qg, kg = qg_ref[...], kg_ref[...]
        for h in range(nh):
            c0 = h * HD
            q_ref[:, c0:c0 + HD] = _head_rms(z_ref[:, c0:c0 + HD].astype(F32), qg).astype(BF16)
            k_ref[:, c0:c0 + HD] = _head_rms(z_ref[:, d + c0:d + c0 + HD].astype(F32), kg).astype(BF16)
        v_ref[...] = z_ref[:, 2 * d:3 * d]

    o = pl.BlockSpec((tm, d), lambda i: (i, 0))
    g = pl.BlockSpec((1, HD), lambda i: (0, 0))
    return _pcall(
        body, name="fox_prep", out_shape=(_sds((t, d), BF16),) * 3, grid=(t // tm,),
        in_specs=[pl.BlockSpec((tm, 3 * d), lambda i: (i, 1)), g, g], out_specs=(o, o, o), compiler_params=_cp(),
    )(z, q_g, k_g)


def _lane_cumsum(v, reverse=False):
    n = v.shape[1]
    lane = lax.broadcasted_iota(jnp.int32, v.shape, 1)
    sh = 1
    while sh < n:
        if reverse:
            v = v + jnp.where(lane < n - sh, pltpu.roll(v, n - sh, axis=1), 0.0)
        else:
            v = v + jnp.where(lane >= sh, pltpu.roll(v, sh, axis=1), 0.0)
        sh *= 2
    return v


def _fox_gates(z, f_bias_col, *, nb, s, nh):
    def body(z_ref, b_ref, c_ref):
        ff = z_ref[...].T[0:nh, :] + b_ref[...]
        lf = jnp.minimum(ff, 0.0) - jnp.log(1.0 + jnp.exp(-jnp.abs(ff)))
        c_ref[0] = _lane_cumsum(lf) * SQRT_HD

    return _pcall(
        body, name="fox_gates", out_shape=_sds((nb, nh, s), F32), grid=(nb,),
        in_specs=[pl.BlockSpec((s, HD), lambda b: (b, 0)), pl.BlockSpec((nh, 1), lambda b: (0, 0))],
        out_specs=pl.BlockSpec((1, nh, s), lambda b: (b, 0, 0)), compiler_params=_cp(),
    )(z, f_bias_col)


def _fox_gates_bwd(z, f_bias_col, row_sums, col_sums, dz, *, nb, s, nh, ff_block):
    def body(z_ref, b_ref, rs_ref, cs_ref, dz_in, dz_ref, db_ref, dc_sc):
        b_i, h = pl.program_id(0), pl.program_id(1)
        dc_sc[pl.ds(h, 1), :] = (rs_ref[...] - cs_ref[...]).T[0:1, :]

        @pl.when(h == nh - 1)
        def _():
            ff = z_ref[...].T[0:nh, :] + b_ref[...]
            dlf = _lane_cumsum(dc_sc[...], reverse=True)
            dff = dlf * (1.0 / (1.0 + jnp.exp(ff)))
            pad = jnp.concatenate([dff, jnp.zeros((HD - nh, s), F32)], axis=0)
            dz_ref[...] = pad.T.astype(BF16)
            dbv = jnp.sum(dff, axis=1, keepdims=True)

            @pl.when(b_i == 0)
            def _():
                db_ref[...] = dbv

            @pl.when(b_i > 0)
            def _():
                db_ref[...] += dbv

    hs = pl.BlockSpec((s, HD), lambda b, h: (b, h))
    return _pcall(
        body, name="fox_gates_bwd", out_shape=(_sds(dz.shape, dz.dtype), _sds((nh, 1), F32)), grid=(nb, nh),
        in_specs=[pl.BlockSpec((s, HD), lambda b, h: (b, 0)), pl.BlockSpec((nh, 1), lambda b, h: (0, 0)), hs, hs, HBM_SPEC],
        out_specs=(pl.BlockSpec((s, HD), lambda b, h: (b, ff_block)), pl.BlockSpec((nh, 1), lambda b, h: (0, 0))),
        scratch_shapes=[pltpu.VMEM((nh, s), F32)],
        input_output_aliases={4: 0}, compiler_params=_cp(dimension_semantics=("arbitrary", "arbitrary")),
    )(z, f_bias_col, row_sums, col_sums, dz)


def _fox_pairs(nq, key_major):
    if key_major:
        pairs = [(i, j) for j in range(nq) for i in range(j, nq)]
    else:
        pairs = [(i, j) for i in range(nq) for j in range(i + 1)]
    return jnp.array([p[0] for p in pairs], jnp.int32), jnp.array([p[1] for p in pairs], jnp.int32)


def _with_ones(x):
    return jnp.concatenate([x, jnp.ones_like(x)], axis=1)


def _fox_specs(nb, nh, nq, tq, tk, hp):
    qs = pl.BlockSpec((tq, hp * HD), lambda b, h, p, *tb: (b * nq + tb[0][p], h))
    ks = pl.BlockSpec((tk, hp * HD), lambda b, h, p, *tb: (b * nq + tb[1][p], h))
    cs = pl.BlockSpec((hp, 1, tk), lambda b, h, p, *tb: (b * (nh // hp) + h, 0, tb[1][p]))
    return qs, ks, cs


def _fox_raw(qv, kv, ckv, diag, tq, tk):
    sc = lax.dot_general(qv, kv, (((1,), (1,)), ((), ())), preferred_element_type=F32) - ckv
    if diag:
        rows = lax.broadcasted_iota(jnp.int32, (tq, tk), 0)
        cols = lax.broadcasted_iota(jnp.int32, (tq, tk), 1)
        sc = jnp.where(rows >= cols, sc, NEG)
    return sc


def _fox_probs(qv, kv, ckv, lse_col, diag, tq, tk):
    sc = _fox_raw(qv, kv, ckv, diag, tq, tk)
    return jnp.exp2(sc * EXP2_C - lse_col * LOG2E)


def _fox_call(body, *, name, ins, in_specs, out_shape, out_specs, scratch, grid, tables, comm):
    c_ins, c_in_specs, c_out_shape, c_scratch = _comm_lists(comm)
    n_in, n_out, n_scr, nc = len(ins), len(out_shape), len(scratch), len(c_ins)
    n_steps = grid[0] * grid[1] * grid[2]

    def wrapped(*refs):
        pre, rin, cin, rout, cout, scr, csem = _split_refs(refs, len(tables), n_in, nc, n_out, nc, n_scr)
        step = (pl.program_id(0) * grid[1] + pl.program_id(1)) * grid[2] + pl.program_id(2)
        _run_comm(comm, cin, cout, csem, step, n_steps, "start")
        body(*pre, *rin, *rout, *scr)
        _run_comm(comm, cin, cout, csem, step, n_steps, "end")

    sem = ("arbitrary",) * 3 if comm is not None else ("parallel", "parallel", "arbitrary")
    outs = _pcall(
        wrapped, name=name, out_shape=tuple(list(out_shape) + c_out_shape),
        grid_spec=pltpu.PrefetchScalarGridSpec(
            num_scalar_prefetch=len(tables), grid=grid, in_specs=list(in_specs) + c_in_specs,
            out_specs=tuple(list(out_specs) + [HBM_SPEC] * nc), scratch_shapes=list(scratch) + c_scratch),
        compiler_params=_cp(dimension_semantics=sem),
    )(*tables, *ins, *c_ins)
    return list(outs[:n_out]), list(outs[n_out:])


def _fox_fwd(q, k, v, cks, *, nb, s, nh, tq=512, comm=None):
    tk = tq
    nq = s // tq
    t = nb * s
    hp = FOX_HP
    it, jt = _fox_pairs(nq, key_major=False)

    def body(it_ref, jt_ref, q_ref, k_ref, v_ref, c_ref, o_ref, lse_ref, m_sc, acc_sc):
        p_id = pl.program_id(2)
        i, j = it_ref[p_id], jt_ref[p_id]

        @pl.when(j == 0)
        def _():
            m_sc[...] = jnp.full_like(m_sc, NEG)
            acc_sc[...] = jnp.zeros_like(acc_sc)

        def step(diag):
            for hh in range(hp):
                cols = slice(hh * HD, (hh + 1) * HD)
                sc = _fox_raw(q_ref[:, cols], k_ref[:, cols], c_ref[hh], diag, tq, tk)
                m_old = m_sc[hh]
                m_new = jnp.maximum(m_old, jnp.max(sc, axis=-1, keepdims=True))
                alpha = jnp.exp2((m_old - m_new) * EXP2_C)
                p = jnp.exp2((sc - m_new) * EXP2_C).astype(BF16)
                acc = alpha * acc_sc[hh] + jnp.dot(p, _with_ones(v_ref[:, cols]), preferred_element_type=F32)
                if diag:
                    l = acc[:, HD:2 * HD]
                    o_ref[:, cols] = (acc[:, 0:HD] / l).astype(BF16)
                    lse_ref[:, cols] = m_new * (1.0 / SQRT_HD) + jnp.log(l)
                else:
                    acc_sc[hh] = acc
                    m_sc[hh] = m_new

        @pl.when(j < i)
        def _():
            step(False)

        @pl.when(j == i)
        def _():
            step(True)

    qs, ks, cs = _fox_specs(nb, nh, nq, tq, tk, hp)
    (o, lse), c_out = _fox_call(
        body, name="fox_fwd", ins=[q, k, v, cks], in_specs=[qs, ks, ks, cs],
        out_shape=[_sds((t, nh * HD), BF16), _sds((t, nh * HD), F32)], out_specs=[qs, qs],
        scratch=[pltpu.VMEM((hp, tq, 1), F32), pltpu.VMEM((hp, tq, 2 * HD), F32)],
        grid=(nb, nh // hp, int(it.shape[0])), tables=(it, jt), comm=comm)
    return o, lse, c_out


def _fox_bwd(q, k, v, cks, do, o, lse, *, nb, s, nh, tq=512, comm=None):
    tk = tq
    nq = s // tq
    t = nb * s
    hp = FOX_HP
    it, jt = _fox_pairs(nq, key_major=True)
    done = jnp.array([j for j in range(nq) for _ in range(j, nq)], jnp.int32)

    def body(it_ref, jt_ref, done_ref, q_ref, k_ref, v_ref, c_ref, do_ref, o_ref, lse_ref,
             dq_ref, rs_ref, dk_ref, dv_ref, cs_ref, dq_sc, dk_sc, dv_sc):
        p_id = pl.program_id(2)
        i, j = it_ref[p_id], jt_ref[p_id]
        rows_i = pl.ds(pl.multiple_of(i * tq, tq), tq)

        def step(diag):
            for hh in range(hp):
                cols = slice(hh * HD, (hh + 1) * HD)
                qv, kv = q_ref[:, cols], k_ref[:, cols]
                p = _fox_probs(qv, kv, c_ref[hh], lse_ref[:, hh * HD:hh * HD + 1], diag, tq, tk)
                dov = do_ref[:, cols]
                dob = dov.astype(BF16)
                dvp = lax.dot_general(p.astype(BF16), dob, (((0,), (0,)), ((), ())), preferred_element_type=F32)
                dp = lax.dot_general(dob, v_ref[:, cols], (((1,), (1,)), ((), ())), preferred_element_type=F32)
                delta = jnp.sum(dov * o_ref[:, cols].astype(F32), axis=-1, keepdims=True)
                ds = (p * (dp - delta)).astype(BF16)
                dkp = lax.dot_general(ds, _with_ones(qv), (((0,), (0,)), ((), ())), preferred_element_type=F32)
                dqp = jnp.dot(ds, _with_ones(kv), preferred_element_type=F32)
                if diag:
                    dk_sc[hh] = dkp
                    dv_sc[hh] = dvp
                else:
                    dk_sc[hh] += dkp
                    dv_sc[hh] += dvp

                @pl.when(j == 0)
                def _():
                    dq_sc[hh, rows_i, :] = dqp

                @pl.when(j > 0)
                def _():
                    dq_sc[hh, rows_i, :] += dqp

                if diag:
                    acc = dq_sc[hh, rows_i, :]
                    dq_ref[:, cols] = acc[:, 0:HD] * (1.0 / SQRT_HD)
                    rs_ref[:, cols] = acc[:, HD:2 * HD]

        @pl.when(i == j)
        def _():
            step(True)

        @pl.when(i > j)
        def _():
            step(False)

        @pl.when(i == nq - 1)
        def _():
            for hh in range(hp):
                cols = slice(hh * HD, (hh + 1) * HD)
                dk_ref[:, cols] = dk_sc[hh, :, 0:HD] * (1.0 / SQRT_HD)
                cs_ref[:, cols] = dk_sc[hh, :, HD:2 * HD]
                dv_ref[:, cols] = dv_sc[hh].astype(BF16)

    qs, ks, cs = _fox_specs(nb, nh, nq, tq, tk, hp)
    ds_spec = pl.BlockSpec((tq, hp * HD), lambda b, h, p, *tb: (b * nq + tb[2][p], h))
    (dq, rs, dk, dv, csum), c_out = _fox_call(
        body, name="fox_bwd", ins=[q, k, v, cks, do, o, lse], in_specs=[qs, ks, ks, cs, qs, qs, qs],
        out_shape=[_sds((t, nh * HD), F32), _sds((t, nh * HD), F32), _sds((t, nh * HD), F32), _sds((t, nh * HD), BF16),
                   _sds((t, nh * HD), F32)],
        out_specs=[ds_spec, ds_spec, ks, ks, ks],
        scratch=[pltpu.VMEM((hp, s, 2 * HD), F32), pltpu.VMEM((hp, tk, 2 * HD), F32), pltpu.VMEM((hp, tk, HD), F32)],
        grid=(nb, nh // hp, int(it.shape[0])), tables=(it, jt, done), comm=comm)
    return dq, rs, dk, dv, csum, c_out


def _fox_post(z, dqn, dkn, dv, q_g, k_g, dz, *, t, d, tm=256):
    nh = d // HD

    def body(z_ref, dq_ref, dk_ref, dv_ref, qg_ref, kg_ref, dz_in, dz_ref, dqg_ref, dkg_ref):
        qg, kg = qg_ref[...], kg_ref[...]
        aq = jnp.zeros((1, HD), F32)
        ak = jnp.zeros((1, HD), F32)
        for h in range(nh):
            c0 = h * HD
            dx, sg = _head_rms_bwd(z_ref[:, c0:c0 + HD].astype(F32), qg, dq_ref[:, c0:c0 + HD])
            dz_ref[:, c0:c0 + HD] = dx.astype(BF16)
            aq = aq + sg
            dx, sg = _head_rms_bwd(z_ref[:, d + c0:d + c0 + HD].astype(F32), kg, dk_ref[:, c0:c0 + HD])
            dz_ref[:, d + c0:d + c0 + HD] = dx.astype(BF16)
            ak = ak + sg
        dz_ref[:, 2 * d:3 * d] = dv_ref[...]

        @pl.when(pl.program_id(0) == 0)
        def _():
            dqg_ref[...] = aq
            dkg_ref[...] = ak

        @pl.when(pl.program_id(0) > 0)
        def _():
            dqg_ref[...] += aq
            dkg_ref[...] += ak

    o = pl.BlockSpec((tm, d), lambda i: (i, 0))
    g = pl.BlockSpec((1, HD), lambda i: (0, 0))
    z3 = pl.BlockSpec((tm, 3 * d), lambda i: (i, 1))
    return _pcall(
        body, name="fox_post", out_shape=(_sds(dz.shape, dz.dtype), _sds((1, HD), F32), _sds((1, HD), F32)), grid=(t // tm,),
        in_specs=[z3, o, o, o, g, g, HBM_SPEC], out_specs=(z3, g, g),
        input_output_aliases={6: 0}, compiler_params=_cp(dimension_semantics=("arbitrary",)),
    )(z, dqn, dkn, dv, q_g, k_g, dz)


def _xa_prep_kv(kv, k_g, *, rows, d):
    xd = d // XH

    def body(kv_ref, g_ref, k_ref, v_ref):
        g = g_ref[...]
        for h in range(XH):
            c0 = h * xd
            k_ref[:, c0:c0 + xd] = _head_rms(kv_ref[:, c0:c0 + xd], g).astype(BF16)
        v_ref[...] = kv_ref[:, d:2 * d].astype(BF16)

    return _pcall(body, name="xa_prep_kv", out_shape=(_sds((rows, d), BF16),) * 2, compiler_params=_cp())(kv, k_g)


def _xa_fwd(z, kn, vb, q_g, *, nb, s, d, nm, q_block, tq=512):
    xd = d // XH
    nq = s // tq

    def body(z_ref, k_ref, v_ref, g_ref, y_ref):
        g = g_ref[...]
        for h in range(XH):
            c0 = h * xd
            qn = _head_rms(z_ref[:, c0:c0 + xd].astype(F32), g).astype(BF16)
            sc = lax.dot_general(qn, k_ref[:, c0:c0 + xd], (((1,), (1,)), ((), ())), preferred_element_type=F32)
            sc = sc / math.sqrt(xd)
            p = jnp.exp(sc - jnp.max(sc, axis=-1, keepdims=True))
            p = p / jnp.sum(p, axis=-1, keepdims=True)
            y_ref[:, c0:c0 + xd] = jnp.dot(p.astype(BF16), v_ref[:, c0:c0 + xd], preferred_element_type=F32).astype(BF16)

    kvs = pl.BlockSpec((nm, d), lambda b, i: (b, 0))
    return _pcall(
        body, name="xa_fwd", out_shape=_sds((nb * s, d), BF16), grid=(nb, nq),
        in_specs=[pl.BlockSpec((tq, d), lambda b, i: (b * nq + i, q_block)), kvs, kvs, pl.BlockSpec((1, xd), lambda b, i: (0, 0))],
        out_specs=pl.BlockSpec((tq, d), lambda b, i: (b * nq + i, 0)), compiler_params=_cp(),
    )(z, kn, vb, q_g)


def _xa_bwd(z, kn, vb, q_g, dy, dz, *, nb, s, d, nm, q_block, tq=512):
    xd = d // XH
    nq = s // tq

    def body(z_ref, k_ref, v_ref, g_ref, dy_ref, dz_in, dz_ref, dk_ref, dv_ref, dg_ref):
        b_i, i = pl.program_id(0), pl.program_id(1)
        g = g_ref[...]

        @pl.when(i == 0)
        def _():
            dk_ref[...] = jnp.zeros_like(dk_ref)
            dv_ref[...] = jnp.zeros_like(dv_ref)

        @pl.when((i == 0) & (b_i == 0))
        def _():
            dg_ref[...] = jnp.zeros_like(dg_ref)

        for h in range(XH):
            c0 = h * xd
            xq = z_ref[:, c0:c0 + xd].astype(F32)
            qn = _head_rms(xq, g).astype(BF16)
            kh, vh = k_ref[:, c0:c0 + xd], v_ref[:, c0:c0 + xd]
            sc = lax.dot_general(qn, kh, (((1,), (1,)), ((), ())), preferred_element_type=F32) / math.sqrt(xd)
            p = jnp.exp(sc - jnp.max(sc, axis=-1, keepdims=True))
            p = p / jnp.sum(p, axis=-1, keepdims=True)
            dob = dy_ref[:, c0:c0 + xd].astype(BF16)
            dv_ref[:, c0:c0 + xd] += lax.dot_general(p.astype(BF16), dob, (((0,), (0,)), ((), ())), preferred_element_type=F32)
            dp = lax.dot_general(dob, vh, (((1,), (1,)), ((), ())), preferred_element_type=F32)
            ds = p * (dp - jnp.sum(p * dp, axis=-1, keepdims=True)) / math.sqrt(xd)
            dsb = ds.astype(BF16)
            dqn = jnp.dot(dsb, kh, preferred_element_type=F32)
            dk_ref[:, c0:c0 + xd] += lax.dot_general(dsb, qn, (((0,), (0,)), ((), ())), preferred_element_type=F32)
            dx, sg = _head_rms_bwd(xq, g, dqn)
            dz_ref[:, c0:c0 + xd] = dx.astype(BF16)
            dg_ref[...] += sg

    kvs = pl.BlockSpec((nm, d), lambda b, i: (b, 0))
    qs = pl.BlockSpec((tq, d), lambda b, i: (b * nq + i, q_block))
    gs = pl.BlockSpec((1, xd), lambda b, i: (0, 0))
    return _pcall(
        body, name="xa_bwd",
        out_shape=(_sds(dz.shape, dz.dtype), _sds((nb * nm, d), F32), _sds((nb * nm, d), F32), _sds((1, xd), F32)), grid=(nb, nq),
        in_specs=[qs, kvs, kvs, gs, pl.BlockSpec((tq, d), lambda b, i: (b * nq + i, 0)), HBM_SPEC],
        out_specs=(qs, kvs, kvs, gs), input_output_aliases={5: 0},
        compiler_params=_cp(dimension_semantics=("arbitrary", "arbitrary")),
    )(z, kn, vb, q_g, dy, dz)


def _xa_post_kv(kv, dkn, dv, k_g, *, rows, d):
    xd = d // XH

    def body(kv_ref, dk_ref, dv_ref, g_ref, dkv_ref, dg_ref):
        g = g_ref[...]
        acc = jnp.zeros((1, xd), F32)
        for h in range(XH):
            c0 = h * xd
            dx, sg = _head_rms_bwd(kv_ref[:, c0:c0 + xd], g, dk_ref[:, c0:c0 + xd])
            dkv_ref[:, c0:c0 + xd] = dx.astype(BF16)
            acc = acc + sg
        dkv_ref[:, d:2 * d] = dv_ref[...].astype(BF16)
        dg_ref[...] = acc

    return _pcall(body, name="xa_post_kv", out_shape=(_sds((rows, 2 * d), BF16), _sds((1, xd), F32)), compiler_params=_cp())(
        kv, dkn, dv, k_g)


def _sigmoid(v):
    return 1.0 / (1.0 + jnp.exp(-v))


def _branches_fwd(z, ys, ws, *, t, d, gate_block, tm=512, tn=512, comm=None):
    nj = d // tn
    c_ins, c_in_specs, c_out_shape, c_scratch = _comm_lists(comm)
    n_steps = (t // tm) * nj

    def body(*refs):
        _, ins, cin, outs, cout, _, csem = _split_refs(refs, 0, 9, len(c_ins), 2, len(c_ins), 0)
        ya_ref, yb_ref, yc_ref, wa_ref, wb_ref, wc_ref, ga_ref, gb_ref, gc_ref = ins
        m_ref, p_ref = outs
        step = pl.program_id(0) * nj + pl.program_id(1)
        _run_comm(comm, cin, cout, csem, step, n_steps, "start")
        acc = None
        for q, (y_ref, w_ref, g_ref) in enumerate(((ya_ref, wa_ref, ga_ref), (yb_ref, wb_ref, gb_ref), (yc_ref, wc_ref, gc_ref))):
            pr = jnp.dot(y_ref[...], w_ref[...], preferred_element_type=F32)
            p_ref[q] = pr.astype(BF16)
            term = _sigmoid(g_ref[...].astype(F32)) * pr
            acc = term if acc is None else acc + term
        m_ref[...] = acc.astype(BF16)
        _run_comm(comm, cin, cout, csem, step, n_steps, "end")

    y_spec = pl.BlockSpec((tm, d), lambda i, j: (i, 0))
    w_spec = pl.BlockSpec((d, tn), lambda i, j: (0, j))

    def gate(q):
        return pl.BlockSpec((tm, tn), lambda i, j: (i, (gate_block + q) * nj + j))

    outs = _pcall(
        body, name="branches_fwd", out_shape=tuple([_sds((t, d), BF16), _sds((3, t, d), BF16)] + c_out_shape), grid=(t // tm, nj),
        in_specs=[y_spec] * 3 + [w_spec] * 3 + [gate(0), gate(1), gate(2)] + c_in_specs,
        out_specs=tuple([pl.BlockSpec((tm, tn), lambda i, j: (i, j)), pl.BlockSpec((3, tm, tn), lambda i, j: (0, i, j))]
                        + [HBM_SPEC] * len(c_ins)),
        scratch_shapes=c_scratch, compiler_params=_cp(dimension_semantics=("arbitrary", "arbitrary")),
    )(*ys, *ws, z, z, z, *c_ins)
    return outs[0], outs[1], list(outs[2:])


def _gates_bwd(z, dm, proj, dz, *, t, d, gate_block, tm=512):
    def body(g_ref, dm_ref, p_ref, dz_in, dz_ref, da_ref, db_ref, dc_ref):
        q = pl.program_id(1)
        sg = _sigmoid(g_ref[...].astype(F32))
        dmv = dm_ref[...]
        dz_ref[...] = (dmv * p_ref[0].astype(F32) * sg * (1.0 - sg)).astype(BF16)
        dp = (dmv * sg).astype(BF16)
        @pl.when(q == 0)
        def _():
            da_ref[...] = dp

        @pl.when(q == 1)
        def _():
            db_ref[...] = dp

        @pl.when(q == 2)
        def _():
            dc_ref[...] = dp

    gs = pl.BlockSpec((tm, d), lambda i, q: (i, gate_block + q))
    o = pl.BlockSpec((tm, d), lambda i, q: (i, 0))
    return _pcall(
        body, name="gates_bwd", out_shape=(_sds(dz.shape, dz.dtype),) + (_sds((t, d), BF16),) * 3, grid=(t // tm, 3),
        in_specs=[gs, o, pl.BlockSpec((1, tm, d), lambda i, q: (q, i, 0)), HBM_SPEC], out_specs=(gs, o, o, o),
        input_output_aliases={3: 0}, compiler_params=_cp(dimension_semantics=("arbitrary", "arbitrary")),
    )(z, dm, proj, dz)


def _ffn_in_fwd(h2, wfi_t, *, t, d, dff, tm=512, tn=1408):
    nj = dff // tn

    def body(h_ref, wg_ref, wu_ref, g_ref, u_ref, a_ref):
        hv = h_ref[...]
        dn = (((1,), (1,)), ((), ()))
        gv = lax.dot_general(hv, wg_ref[...], dn, preferred_element_type=F32)
        uv = lax.dot_general(hv, wu_ref[...], dn, preferred_element_type=F32)
        g_ref[...] = gv.astype(BF16)
        u_ref[...] = uv.astype(BF16)
        a_ref[...] = (gv * _sigmoid(gv) * uv).astype(BF16)

    o = pl.BlockSpec((tm, tn), lambda i, j: (i, j))
    return _pcall(
        body, name="ffn_in_fwd", out_shape=(_sds((t, dff), BF16),) * 3, grid=(t // tm, nj),
        in_specs=[pl.BlockSpec((tm, d), lambda i, j: (i, 0)), pl.BlockSpec((tn, d), lambda i, j: (j, 0)),
                  pl.BlockSpec((tn, d), lambda i, j: (nj + j, 0))],
        out_specs=(o, o, o), compiler_params=_cp(),
    )(h2, wfi_t, wfi_t)


def _ffn_out_bwd(dy, wfo, gate, up, *, t, d, dff, tm=512, tn=1408):
    def body(dy_ref, w_ref, g_ref, u_ref, dg_ref, du_ref):
        dav = lax.dot_general(dy_ref[...].astype(BF16), w_ref[...], (((1,), (1,)), ((), ())), preferred_element_type=F32)
        gv = g_ref[...].astype(F32)
        sg = _sigmoid(gv)
        dg_ref[...] = (dav * u_ref[...].astype(F32) * sg * (1.0 + gv * (1.0 - sg))).astype(BF16)
        du_ref[...] = (dav * gv * sg).astype(BF16)

    o = pl.BlockSpec((tm, tn), lambda i, j: (i, j))
    return _pcall(
        body, name="ffn_out_bwd", out_shape=(_sds((t, dff), BF16),) * 2, grid=(t // tm, dff // tn),
        in_specs=[pl.BlockSpec((tm, d), lambda i, j: (i, 0)), pl.BlockSpec((tn, d), lambda i, j: (j, 0)), o, o],
        out_specs=(o, o), compiler_params=_cp(),
    )(dy, wfo, gate, up)


def _loss_head(y, target, *, t, d, tm=512):
    def body(y_ref, t_ref, dy_ref, l_ref):
        e = y_ref[...] - t_ref[...]
        dy_ref[...] = e * (1.0 / d)

        @pl.when(pl.program_id(0) == 0)
        def _():
            l_ref[...] = jnp.zeros_like(l_ref)

        l_ref[...] += jnp.sum(jnp.sum(e * e, axis=-1, keepdims=True), axis=0, keepdims=True) * (0.5 / d)

    o = pl.BlockSpec((tm, d), lambda i: (i, 0))
    dy, l = _pcall(
        body, name="loss_head", out_shape=(_sds((t, d), F32), _sds((1, HD), F32)), grid=(t // tm,),
        in_specs=[o, o], out_specs=(o, pl.BlockSpec((1, HD), lambda i: (0, 0))),
        compiler_params=_cp(dimension_semantics=("arbitrary",)),
    )(y, target)
    return dy, l[0, 0]


def _adamw(w, g, m, v, *, name):
    if w.ndim == 3:
        rows, _, cols = w.shape
        tm = max(tt for tt in range(1, 65) if rows % tt == 0)
        block, imap, grid = (tm, 1, cols), (lambda i, j: (i, 0, 0)), (rows // tm, 1)
    else:
        rows, cols = w.shape
        tm, tn = rows, cols
        for cand in (256, 128, 64, 32, 16, 8):
            if rows > cand and rows % cand == 0:
                tm = cand
                break
        if tm == rows and rows > 256 and cols % 128 == 0:
            tn = 128
        block, imap, grid = (tm, tn), (lambda i, j: (i, j)), (rows // tm, cols // tn)
    c1 = 1.0 - ADAM_B1 ** ADAM_STEP
    c2 = 1.0 - ADAM_B2 ** ADAM_STEP

    def body(w_ref, g_ref, m_ref, v_ref, d_ref, nm_ref, nv_ref):
        gv = g_ref[...]
        mn = ADAM_B1 * m_ref[...] + (1.0 - ADAM_B1) * gv
        vn = ADAM_B2 * v_ref[...] + (1.0 - ADAM_B2) * (gv * gv)
        d_ref[...] = -ADAM_LR * ((mn / c1) / (jnp.sqrt(vn / c2) + ADAM_EPS) + ADAM_WD * w_ref[...])
        nm_ref[...] = mn
        nv_ref[...] = vn

    spec = pl.BlockSpec(block, imap)
    return _pcall(
        body, name=name, out_shape=(_sds(w.shape, F32),) * 3, grid=grid,
        in_specs=[spec] * 4, out_specs=(spec,) * 3, compiler_params=_cp(),
    )(w, g, m, v)


def _pad_rows(a, rows):
    return a if a.shape[0] == rows else jnp.pad(a, ((0, rows - a.shape[0]),) + ((0, 0),) * (a.ndim - 1))


class _Pack:
    def __init__(self, row_sizes):
        self.rows = list(row_sizes)
        self.padded = [-(-r // 16) * 16 for r in self.rows]
        self.offs = [sum(self.padded[:i]) for i in range(len(self.rows))]
        self.total = sum(self.padded)

    def stack(self, blocks):
        return jnp.concatenate([_pad_rows(b.astype(BF16), p) for b, p in zip(blocks, self.padded)], axis=0)

    def full(self, gathered, i):
        r = self.rows[i]
        return gathered[:, self.offs[i]:self.offs[i] + r, :].reshape(N_DEV * r, gathered.shape[2])

    def for_core(self, full_grads, core):
        parts = []
        for gfull, r, p in zip(full_grads, self.rows, self.padded):
            pieces = gfull if isinstance(gfull, (tuple, list)) else (gfull,)
            blk = jnp.concatenate(
                [lax.dynamic_index_in_dim(g.reshape(-1, 2, r, g.shape[1]), core, axis=1, keepdims=False) for g in pieces],
                axis=0).astype(BF16)
            if p != r:
                blk = jnp.pad(blk, ((0, 0), (0, p - r), (0, 0)))
            parts.append(blk)
        return jnp.concatenate(parts, axis=1)

    def shard(self, g_pack, i):
        return g_pack[self.offs[i]:self.offs[i] + self.rows[i], :]


def kernel(x, mem, norm1_g, w_in, conv_w, conv_b, fox_f_bias, fox_q_g, fox_k_g, mem_norm_g, w_mem_kv, xa_q_g, xa_k_g, w_br_conv, w_br_fox, w_br_xa, w_o, norm2_g, w_ffn_in, w_ffn_out, loss_target, m_norm1_g, m_w_in, m_conv_w, m_conv_b, m_fox_f_bias, m_fox_q_g, m_fox_k_g, m_mem_norm_g, m_w_mem_kv, m_xa_q_g, m_xa_k_g, m_w_br_conv, m_w_br_fox, m_w_br_xa, m_w_o, m_norm2_g, m_w_ffn_in, m_w_ffn_out, v_norm1_g, v_w_in, v_conv_w, v_conv_b, v_fox_f_bias, v_fox_q_g, v_fox_k_g, v_mem_norm_g, v_w_mem_kv, v_xa_q_g, v_xa_k_g, v_w_br_conv, v_w_br_fox, v_w_br_xa, v_w_o, v_norm2_g, v_w_ffn_in, v_w_ffn_out):
    nb, s, d = x.shape
    t = nb * s
    nm = mem.shape[1]
    nh = d // HD
    dff = w_ffn_out.shape[1] * N_DEV
    n_in = w_in.shape[2] * N_DEV
    n_in_pad = -(-n_in // 1152) * 1152
    ff_block = (10 * d) // HD
    my_c = lax.axis_index("c")
    my_q = 2 * lax.axis_index("x") + lax.axis_index("y")

    pack_a = _Pack([w_in.shape[2]])
    pack_b = _Pack([w_mem_kv.shape[2], w_br_conv.shape[1], w_br_fox.shape[1], w_br_xa.shape[1], w_o.shape[1],
                    w_ffn_in.shape[2], w_ffn_out.shape[1]])
    gather_2 = _Pack([w_mem_kv.shape[2], w_br_conv.shape[1], w_br_fox.shape[1], w_br_xa.shape[1], w_o.shape[1]])
    stack_a = pack_a.stack([w_in[0].T])
    stack_2 = gather_2.stack([w_mem_kv[0].T, w_br_conv[0], w_br_fox[0], w_br_xa[0], w_o[0]])
    stack_fi = w_ffn_in[0].T.astype(BF16)
    stack_fo = w_ffn_out[0].astype(BF16)
    cw_block = _pad_rows(conv_w[0], 8)
    rows_a = w_in.shape[2]
    my_x, my_y = lax.axis_index("x"), lax.axis_index("y")
    arrival = [(my_x, my_y, my_c), (my_x, my_y, 1 - my_c)]
    for cx, cy in [(1 - my_x, my_y), (my_x, 1 - my_y), (1 - my_x, 1 - my_y)]:
        arrival += [(cx, cy, my_c), (cx, cy, 1 - my_c)]
    order = jnp.stack([4 * ax + 2 * ay + ac for ax, ay, ac in arrival]).astype(jnp.int32)

    x2 = x.reshape(t, d)
    mem2 = mem.reshape(nb * nm, d)
    tgt2 = loss_target.reshape(t, d)
    h1 = _rms_fwd(x2, norm1_g, name="rms1_fwd")
    z, slabs, _ = _z_gathered(h1, stack_a, order, t=t, d=d, n_pad=N_DEV * SLAB_WIN, rows=rows_a)
    z = _z_fix(h1, slabs, z, t=t, d=d, rows=rows_a)
    ff_rows = _pad_rows(slabs[N_DEV - 1, rows_a - nh:rows_a, :], HD)
    zff = _mm(h1, ff_rows, m=t, n=HD, k=d, tb=True, tm=1024, name="mm_z_ff")
    qn, kn, vb = _fox_prep(z, fox_q_g, fox_k_g, t=t, d=d)
    f_bias_col = fox_f_bias.reshape(nh, 1)
    ck_rows = _fox_gates(zff, f_bias_col, nb=nb, s=s, nh=nh).reshape(nb * nh, 1, s)
    y_fox, lse, (gathered_2, cw_g, gathered_fo) = _fox_fwd(qn, kn, vb, ck_rows, nb=nb, s=s, nh=nh,
                                                           comm=_AllGather([stack_2, cw_block, stack_fo], forward_at=0.8))
    wkv_t, wbc, wbf, wbx, wo = (gather_2.full(gathered_2, i) for i in range(5))
    wfo = gathered_fo.reshape(dff, d)
    conv_w_full = jnp.transpose(cw_g[:, 0:3, :], (1, 0, 2)).reshape(3, d)

    y_conv = _conv_fwd(z, conv_w_full, conv_b, nb=nb, s=s, d=d)
    mem_n = _rms_fwd(mem2, mem_norm_g, name="rms_mem_fwd")
    kv = _mm(mem_n, wkv_t, m=nb * nm, n=2 * d, k=d, tb=True, name="mm_kv")
    xkn, xvb = _xa_prep_kv(kv, xa_k_g, rows=nb * nm, d=d)
    y_xa = _xa_fwd(z, xkn, xvb, xa_q_g, nb=nb, s=s, d=d, nm=nm, q_block=6)
    merged, proj, (gathered_fi,) = _branches_fwd(z, (y_conv, y_fox, y_xa), (wbc, wbf, wbx), t=t, d=d, gate_block=7,
                                                 comm=_AllGather([stack_fi]))
    wfi_t = gathered_fi.reshape(2 * dff, d)
    x1 = _mm(merged, wo, m=t, n=d, k=d, res=x2, name="mm_x1")
    h2 = _rms_fwd(x1, norm2_g, name="rms2_fwd")
    ff_gate, ff_up, act = _ffn_in_fwd(h2, wfi_t, t=t, d=d, dff=dff)
    y = _mm(act, wfo, m=t, n=d, k=dff, res=x1, tk=dff, name="mm_y")
    dy, loss_local = _loss_head(y, tgt2, t=t, d=d)

    g_wfo = _mm(act, dy, m=dff, n=d, k=t, ta=True, tm=1408, tk=2048, out_dtype=BF16, name="mm_g_wfo")
    dgate, dup = _ffn_out_bwd(dy, wfo, ff_gate, ff_up, t=t, d=d, dff=dff)
    dh2 = _mm(dgate, wfi_t, m=t, n=d, k=dff, tk=dff, name="mm_dh2_g")
    dh2 = _mm(dup, wfi_t, m=t, n=d, k=dff, tk=dff, b_off=(1, 0), res=dh2, name="mm_dh2_u")
    g_wfi_g = _mm(dgate, h2, m=dff, n=d, k=t, ta=True, tm=1408, tk=2048, out_dtype=BF16, name="mm_g_wfi_g")
    g_wfi_u = _mm(dup, h2, m=dff, n=d, k=t, ta=True, tm=1408, tk=2048, out_dtype=BF16, name="mm_g_wfi_u")
    dx1, g_norm2 = _rms_bwd(dh2, x1, norm2_g, dy, name="rms2_bwd")
    dmerged = _mm(dx1, wo, m=t, n=d, k=d, tb=True, name="mm_dmerged")
    g_wo = _mm(merged, dx1, m=d, n=d, k=t, ta=True, tk=2048, out_dtype=BF16, name="mm_g_wo")

    dz = lax.empty((t, n_in_pad), BF16)
    dz, dpa, dpb, dpc = _gates_bwd(z, dmerged, proj, dz, t=t, d=d, gate_block=7)
    dy_conv = _mm(dpa, wbc, m=t, n=d, k=d, tb=True, name="mm_dy_conv")
    g_wbc = _mm(y_conv, dpa, m=d, n=d, k=t, ta=True, tk=4096, out_dtype=BF16, name="mm_g_wbc")
    dy_fox = _mm(dpb, wbf, m=t, n=d, k=d, tb=True, name="mm_dy_fox")
    g_wbf = _mm(y_fox, dpb, m=d, n=d, k=t, ta=True, tk=4096, out_dtype=BF16, name="mm_g_wbf")
    dy_xa = _mm(dpc, wbx, m=t, n=d, k=d, tb=True, name="mm_dy_xa")
    g_wbx = _mm(y_xa, dpc, m=d, n=d, k=t, ta=True, tk=4096, out_dtype=BF16, name="mm_g_wbx")

    dz, dxkn, dxv, g_xa_q = _xa_bwd(z, xkn, xvb, xa_q_g, dy_xa, dz, nb=nb, s=s, d=d, nm=nm, q_block=6)
    dkv, g_xa_k = _xa_post_kv(kv, dxkn, dxv, xa_k_g, rows=nb * nm, d=d)
    g_wkv_t = _mm(dkv, mem_n, m=2 * d, n=d, k=nb * nm, ta=True, out_dtype=BF16, name="mm_g_wkv")
    dmem_n = _mm(dkv, wkv_t, m=nb * nm, n=d, k=2 * d, tk=2 * d, name="mm_dmem")
    _, g_mem_norm = _rms_bwd(dmem_n, mem2, mem_norm_g, None, name="rms_mem_bwd")

    grads_b = [g_wkv_t, g_wbc, g_wbf, g_wbx, g_wo, (g_wfi_g, g_wfi_u), g_wfo]
    keep_b, send_b = pack_b.for_core(grads_b, my_c), pack_b.for_core(grads_b, 1 - my_c)
    dz, g_conv_w, g_conv_b, (from_sibling_b,) = _conv_bwd(z, dy_conv, conv_w_full, conv_b, dz, nb=nb, s=s, d=d,
                                                          comm=_SiblingSwap(send_b))
    part_b = _add2(keep_b, from_sibling_b, name="rs_add_sibling_b")
    dqn, ds_rows, dkn, dvb, ds_cols, (from_chips_b,) = _fox_bwd(qn, kn, vb, ck_rows, dy_fox, y_fox, lse, nb=nb, s=s, nh=nh,
                                                                comm=_ChipExchange(part_b))
    g_pack_b = _add4(lax.dynamic_index_in_dim(part_b, my_q, axis=0, keepdims=False), from_chips_b, name="rs_add_chips_b")
    dz, g_fox_q, g_fox_k = _fox_post(z, dqn, dkn, dvb, fox_q_g, fox_k_g, dz, t=t, d=d)
    dz, g_f_bias = _fox_gates_bwd(zff, f_bias_col, ds_rows, ds_cols, dz, nb=nb, s=s, nh=nh, ff_block=ff_block)

    keep_a, from_sibling_a = _g_win_slabs(dz, h1, my_c, t=t, d=d, n_slab=N_DEV, rows=rows_a, rp=slabs.shape[1])
    part_a = _add2(keep_a, from_sibling_a, name="rs_add_sibling_a")
    dh1, (from_chips_a,) = _dh1_from_slabs(dz, slabs, t=t, d=d, rows=rows_a, comm=_ChipExchange(part_a))
    g_pack_a = _add4(lax.dynamic_index_in_dim(part_a, my_q, axis=0, keepdims=False), from_chips_a, name="rs_add_chips_a",
                     rows_apart=True)
    dx, g_norm1 = _rms_bwd(dh1, x2, norm1_g, dx1, name="rms1_bwd")

    xd = d // XH
    tail = jnp.concatenate(
        [_pad_rows(g_f_bias.reshape(nh), HD).reshape(1, HD), g_fox_q, g_fox_k, g_xa_q, g_xa_k,
         jnp.zeros((1, d - 3 * HD - 2 * xd), F32)], axis=1)
    small = jnp.concatenate([g_norm1, g_norm2, g_conv_w, g_conv_b, g_mem_norm, tail], axis=0)
    small = _all_reduce_small(small)
    gs_conv_w = lax.dynamic_slice_in_dim(small[2:5], (2 * my_q + my_c) * (d // N_DEV), d // N_DEV, axis=1)

    grads = {
        "norm1_g": small[0:1], "w_in": g_pack_a, "conv_w": gs_conv_w, "conv_b": small[5:6],
        "fox_f_bias": small[7:8, 0:nh], "fox_q_g": small[7:8, HD:2 * HD], "fox_k_g": small[7:8, 2 * HD:3 * HD],
        "mem_norm_g": small[6:7], "w_mem_kv": pack_b.shard(g_pack_b, 0), "xa_q_g": small[7:8, 3 * HD:3 * HD + xd],
        "xa_k_g": small[7:8, 3 * HD + xd:3 * HD + 2 * xd], "w_br_conv": pack_b.shard(g_pack_b, 1),
        "w_br_fox": pack_b.shard(g_pack_b, 2), "w_br_xa": pack_b.shard(g_pack_b, 3), "w_o": pack_b.shard(g_pack_b, 4),
        "norm2_g": small[1:2], "w_ffn_in": pack_b.shard(g_pack_b, 5), "w_ffn_out": pack_b.shard(g_pack_b, 6),
    }
    transposed = {"w_in", "w_mem_kv", "w_ffn_in"}
    weights = {
        "norm1_g": (norm1_g, m_norm1_g, v_norm1_g), "w_in": (w_in, m_w_in, v_w_in), "conv_w": (conv_w, m_conv_w, v_conv_w),
        "conv_b": (conv_b, m_conv_b, v_conv_b), "fox_f_bias": (fox_f_bias, m_fox_f_bias, v_fox_f_bias),
        "fox_q_g": (fox_q_g, m_fox_q_g, v_fox_q_g), "fox_k_g": (fox_k_g, m_fox_k_g, v_fox_k_g),
        "mem_norm_g": (mem_norm_g, m_mem_norm_g, v_mem_norm_g), "w_mem_kv": (w_mem_kv, m_w_mem_kv, v_w_mem_kv),
        "xa_q_g": (xa_q_g, m_xa_q_g, v_xa_q_g), "xa_k_g": (xa_k_g, m_xa_k_g, v_xa_k_g),
        "w_br_conv": (w_br_conv, m_w_br_conv, v_w_br_conv), "w_br_fox": (w_br_fox, m_w_br_fox, v_w_br_fox),
        "w_br_xa": (w_br_xa, m_w_br_xa, v_w_br_xa), "w_o": (w_o, m_w_o, v_w_o), "norm2_g": (norm2_g, m_norm2_g, v_norm2_g),
        "w_ffn_in": (w_ffn_in, m_w_ffn_in, v_w_ffn_in), "w_ffn_out": (w_ffn_out, m_w_ffn_out, v_w_ffn_out),
    }
    vec_rows = {"norm1_g": 0, "norm2_g": 1, "conv_b": 5, "mem_norm_g": 6}
    vec_lanes = {"fox_f_bias": (0, nh), "fox_q_g": (HD, HD), "fox_k_g": (2 * HD, HD), "xa_q_g": (3 * HD, xd),
                 "xa_k_g": (3 * HD + xd, xd)}

    def pack_vectors(which):
        pick = {n: weights[n][which] for n in list(vec_rows) + list(vec_lanes)}
        last = jnp.concatenate(
            [_pad_rows(pick["fox_f_bias"].reshape(nh), HD).reshape(1, HD), pick["fox_q_g"], pick["fox_k_g"], pick["xa_q_g"],
             pick["xa_k_g"], jnp.zeros((1, d - 3 * HD - 2 * xd), F32)], axis=1)
        return jnp.concatenate([pick["norm1_g"], pick["norm2_g"], jnp.zeros((3, d), F32), pick["conv_b"], pick["mem_norm_g"],
                                last], axis=0)

    g_vec = jnp.concatenate([small[0:2], jnp.zeros((3, d), F32), small[5:8]], axis=0)
    vec_out = _adamw(pack_vectors(0), g_vec, pack_vectors(1), pack_vectors(2), name="adamw_vectors")

    def vec_leaf(a, name, shape):
        if name in vec_rows:
            return a[vec_rows[name]:vec_rows[name] + 1].reshape(shape)
        lo, n = vec_lanes[name]
        return a[7:8, lo:lo + n].reshape(shape)

    out_g, out_d, out_m, out_v = [], [], [], []
    for name, (w, m, v) in weights.items():
        shape = w.shape
        if name in vec_rows or name in vec_lanes:
            out_g.append(vec_leaf(small, name, shape))
            for lst, a in zip((out_d, out_m, out_v), vec_out):
                lst.append(vec_leaf(a, name, shape))
            continue
        if name == "w_in":
            w3, m3, v3 = (a.transpose(2, 0, 1) for a in (w, m, v))
            dl, mn, vn = _adamw(w3, g_pack_a, m3, v3, name="adamw_" + name)
            out_g.append(g_pack_a[:shape[2]].transpose(1, 2, 0))
            out_d.append(dl.transpose(1, 2, 0))
            out_m.append(mn.transpose(1, 2, 0))
            out_v.append(vn.transpose(1, 2, 0))
            continue
        tr = name in transposed

        def as2d(a):
            a = a.reshape(shape[-2], shape[-1])
            return a.T if tr else a

        def back(a):
            return (a.T if tr else a).reshape(shape)

        w2 = as2d(w)
        g2 = grads[name].reshape(w2.shape)
        dl, mn, vn = _adamw(w2, g2, as2d(m), as2d(v), name="adamw_" + name)
        out_g.append(back(g2))
        out_d.append(back(dl))
        out_m.append(back(mn))
        out_v.append(back(vn))

    loss = lax.psum(loss_local, ("x", "y", "c"))
    return (loss, dx.reshape(nb, s, d), *out_g, *out_d, *out_m, *out_v)
```

```python
import math

import jax
import jax.numpy as jnp
from jax import lax
from jax.experimental import pallas as pl
from jax.experimental.pallas import tpu as pltpu

F32, BF16 = jnp.float32, jnp.bfloat16
EPS = 1e-6
HD = 128
XH = 4
N_DEV = 8
MESH = pl.DeviceIdType.MESH
VMEM_LIMIT = 52 * 1024 * 1024
NEG = -1e30
SQRT_HD = math.sqrt(HD)
EXP2_C = math.log2(math.e) / SQRT_HD
LOG2E = math.log2(math.e)
FOX_HP = 4

ADAM_LR, ADAM_B1, ADAM_B2, ADAM_EPS, ADAM_WD, ADAM_STEP = 0.001, 0.9, 0.999, 1e-08, 0.01, 10


def _cp(**kw):
    return pltpu.CompilerParams(vmem_limit_bytes=VMEM_LIMIT, **kw)


def _pcall(body, **kw):
    return pl.pallas_call(body, **kw)


def _sds(shape, dtype):
    return jax.ShapeDtypeStruct(shape, dtype)


HBM_SPEC = pl.BlockSpec(memory_space=pl.ANY)


def _mesh_pos():
    return lax.axis_index("x"), lax.axis_index("y"), lax.axis_index("c")


class _AllGather:
    def __init__(self, blocks, forward_at=None):
        self.blocks = list(blocks)
        self.forward_at = forward_at
        n = len(self.blocks)
        self.out_shape = [_sds((N_DEV,) + b.shape, b.dtype) for b in self.blocks]
        self.scratch = [pltpu.SemaphoreType.DMA((n, 7)), pltpu.SemaphoreType.DMA((n, 7)), pltpu.SemaphoreType.DMA((n,))]

    def bind(self, ins, outs, sems):
        n = len(ins)
        send_sems, recv_sems, local_sems = sems
        x, y, c = _mesh_pos()
        me, sibling = (x, y, c), (x, y, 1 - c)
        chips = [(1 - x, y), (x, 1 - y), (1 - x, 1 - y)]

        def slab(t, dev):
            return outs[t].at[4 * dev[0] + 2 * dev[1] + dev[2]]

        def copy(t, k, block, to, src=None):
            dst = slab(t, block)
            return pltpu.make_async_remote_copy(
                src_ref=dst if src is None else src, dst_ref=dst, send_sem=send_sems.at[t, k], recv_sem=recv_sems.at[t, k],
                device_id=to, device_id_type=MESH)

        mine = [pltpu.make_async_copy(ins[t], slab(t, me), local_sems.at[t]) for t in range(n)]
        first = []
        for t in range(n):
            first.append(copy(t, 0, me, sibling, src=ins[t]))
            first += [copy(t, 1 + j, me, (*chip, c), src=ins[t]) for j, chip in enumerate(chips)]
        passed = [copy(t, 4 + j, (*chip, c), sibling) for j, chip in enumerate(chips) for t in range(n)]

        def start():
            for cp in mine + first:
                cp.start()

        def mid():
            for j, chip in enumerate(chips):
                for t in range(n):
                    copy(t, 1 + j, (*chip, c), me).wait_recv()
                    passed[j * n + t].start()

        def finish():
            for t in range(n):
                copy(t, 0, sibling, me).wait_recv()
                for j, chip in enumerate(chips):
                    copy(t, 4 + j, (*chip, 1 - c), me).wait_recv()
            for cp in first + passed:
                cp.wait_send()
            for cp in mine:
                cp.wait()

        return start, mid, finish


class _SiblingSwap:
    def __init__(self, send):
        self.blocks = [send]
        self.out_shape = [_sds(send.shape, send.dtype)]
        self.scratch = [pltpu.SemaphoreType.DMA, pltpu.SemaphoreType.DMA]

    def bind(self, ins, outs, sems):
        x, y, c = _mesh_pos()
        cp = pltpu.make_async_remote_copy(src_ref=ins[0], dst_ref=outs[0], send_sem=sems[0], recv_sem=sems[1],
                                          device_id=(x, y, 1 - c), device_id_type=MESH)
        return cp.start, (lambda: None), cp.wait


class _ChipExchange:
    def __init__(self, part):
        self.blocks = [part]
        self.out_shape = [_sds((3,) + part.shape[1:], part.dtype)]
        self.scratch = [pltpu.SemaphoreType.DMA((3,)), pltpu.SemaphoreType.DMA((3,))]

    def bind(self, ins, outs, sems):
        x, y, c = _mesh_pos()
        chips = [(1 - x, y), (x, 1 - y), (1 - x, 1 - y)]
        cps = [
            pltpu.make_async_remote_copy(src_ref=ins[0].at[2 * cx + cy], dst_ref=outs[0].at[j], send_sem=sems[0].at[j],
                                         recv_sem=sems[1].at[j], device_id=(cx, cy, c), device_id_type=MESH)
            for j, (cx, cy) in enumerate(chips)
        ]

        def start():
            for cp in cps:
                cp.start()

        def finish():
            for cp in cps:
                cp.wait()

        return start, (lambda: None), finish


def _comm_lists(comm):
    if comm is None:
        return [], [], [], []
    n = len(comm.blocks)
    return list(comm.blocks), [HBM_SPEC] * n, list(comm.out_shape), list(comm.scratch)


def _split_refs(refs, n_pre, n_in, n_cin, n_out, n_cout, n_scr):
    cuts = [n_pre, n_in, n_cin, n_out, n_cout, n_scr]
    out, at = [], 0
    for c in cuts:
        out.append(refs[at:at + c])
        at += c
    out.append(refs[at:])
    return out


def _run_comm(comm, cin, cout, csem, step, n_steps, when="start"):
    if comm is None:
        return
    start, mid, finish = comm.bind(cin, cout, csem)
    if when == "start":
        pl.when(step == 0)(start)
    else:
        frac = getattr(comm, "forward_at", None)
        mid_step = max(n_steps - 2, 0) if frac is None else min(int(frac * n_steps), max(n_steps - 2, 0))
        pl.when(step == mid_step)(mid)
        pl.when(step == n_steps - 1)(finish)


def _comm_only(comm, name):
    ins, in_specs, out_shape, scratch = _comm_lists(comm)
    n = len(ins)

    def body(*refs):
        start, mid, finish = comm.bind(refs[:n], refs[n:2 * n], refs[2 * n:])
        start()
        mid()
        finish()

    outs = _pcall(body, name=name, out_shape=tuple(out_shape), in_specs=in_specs, out_specs=tuple([HBM_SPEC] * n),
                  scratch_shapes=scratch, compiler_params=_cp())(*ins)
    return list(outs)


def _all_reduce_small(v):
    rows, cols = v.shape

    def body(v_ref, o_ref, slots, send_sems, recv_sems):
        x, y, c = _mesh_pos()
        me = 4 * x + 2 * y + c
        slots[me] = v_ref[...]
        cps = []
        for k in range(1, N_DEV):
            px, py, pc = x ^ (k >> 2), y ^ ((k >> 1) & 1), c ^ (k & 1)
            cps.append(pltpu.make_async_remote_copy(
                src_ref=v_ref, dst_ref=slots.at[me], send_sem=send_sems.at[k - 1], recv_sem=recv_sems.at[k - 1],
                device_id=(px, py, pc), device_id_type=MESH))
        for cp in cps:
            cp.start()
        for cp in cps:
            cp.wait()
        acc = slots[0]
        for k in range(1, N_DEV):
            acc = acc + slots[k]
        o_ref[...] = acc

    return _pcall(
        body, name="all_reduce_small", out_shape=_sds((rows, cols), F32),
        in_specs=[pl.BlockSpec(memory_space=pltpu.VMEM)], out_specs=pl.BlockSpec(memory_space=pltpu.VMEM),
        scratch_shapes=[pltpu.VMEM((N_DEV, rows, cols), F32), pltpu.SemaphoreType.DMA((7,)), pltpu.SemaphoreType.DMA((7,))],
        compiler_params=_cp(),
    )(v)


def _row_tile(rows, cap=1040):
    return max(tt for tt in range(16, cap + 1, 16) if rows % tt == 0)


def _add2(a, b, *, name):
    nq, rows, cols = a.shape
    tm = _row_tile(rows)

    def body(a_ref, b_ref, o_ref):
        o_ref[...] = (a_ref[...].astype(F32) + b_ref[...].astype(F32)).astype(BF16)

    spec = pl.BlockSpec((1, tm, cols), lambda q, i: (q, i, 0))
    return _pcall(body, name=name, out_shape=_sds(a.shape, BF16), grid=(nq, rows // tm),
                  in_specs=[spec, spec], out_specs=spec, compiler_params=_cp())(a, b)


def _add4(own, recv, *, name, rows_apart=False):
    rows, cols = own.shape
    tm = _row_tile(rows, 256 if rows_apart else 1040)

    def body(o_ref, r_ref, g_ref):
        tot = ((o_ref[...].astype(F32) + r_ref[0].astype(F32)) + r_ref[1].astype(F32)) + r_ref[2].astype(F32)
        if rows_apart:
            g_ref[:, 0, :] = tot
        else:
            g_ref[...] = tot

    if rows_apart:
        out_shape, out_spec = _sds((rows, 1, cols), F32), pl.BlockSpec((tm, 1, cols), lambda i: (i, 0, 0))
    else:
        out_shape, out_spec = _sds((rows, cols), F32), pl.BlockSpec((tm, cols), lambda i: (i, 0))
    return _pcall(
        body, name=name, out_shape=out_shape, grid=(rows // tm,),
        in_specs=[pl.BlockSpec((tm, cols), lambda i: (i, 0)), pl.BlockSpec((3, tm, cols), lambda i: (0, i, 0))],
        out_specs=out_spec, compiler_params=_cp(),
    )(own, recv)


def _mm(a, b, *, m, n, k, name, ta=False, tb=False, b_off=(0, 0), res=None, out_dtype=F32, tm=512, tn=1024, tk=1024,
        comm=None):
    tm, tn, tk = min(tm, m), min(tn, n), min(tk, k)
    assert m % tm == 0 and n % tn == 0 and k % tk == 0, (name, m, n, k, tm, tn, tk)
    nk = k // tk
    grid = (m // tm, n // tn, nk)
    bo0, bo1 = b_off
    if ta:
        a_spec = pl.BlockSpec((tk, tm), lambda i, j, kk: (kk, i))
    else:
        a_spec = pl.BlockSpec((tm, tk), lambda i, j, kk: (i, kk))
    if tb:
        b_spec = pl.BlockSpec((tn, tk), lambda i, j, kk: (j + bo0, kk + bo1))
    else:
        b_spec = pl.BlockSpec((tk, tn), lambda i, j, kk: (kk + bo0, j + bo1))
    o_spec = pl.BlockSpec((tm, tn), lambda i, j, kk: (i, j))
    dn = (((0 if ta else 1,), (1 if tb else 0,)), ((), ()))
    has_res = res is not None
    c_ins, c_in_specs, c_out_shape, c_scratch = _comm_lists(comm)
    n_in = 3 if has_res else 2
    n_scr = 1 if nk > 1 else 0

    def body(*refs):
        _, ins, cin, outs, cout, scr, csem = _split_refs(refs, 0, n_in, len(c_ins), 1, len(c_ins), n_scr)
        a_ref, b_ref = ins[0], ins[1]
        r_ref = ins[2] if has_res else None
        o_ref = outs[0]
        step = (pl.program_id(0) * grid[1] + pl.program_id(1)) * nk + pl.program_id(2)
        _run_comm(comm, cin, cout, csem, step, grid[0] * grid[1] * nk, "start")
        p = lax.dot_general(a_ref[...].astype(BF16), b_ref[...].astype(BF16), dn, preferred_element_type=F32)

        def finish(acc):
            if has_res:
                acc = acc + r_ref[...]
            o_ref[...] = acc.astype(out_dtype)

        if nk == 1:
            finish(p)
        else:
            acc_ref = scr[0]
            kk = pl.program_id(2)

            @pl.when(kk == 0)
            def _():
                acc_ref[...] = p

            @pl.when(kk > 0)
            def _():
                acc_ref[...] += p

            @pl.when(kk == nk - 1)
            def _():
                finish(acc_ref[...])

        _run_comm(comm, cin, cout, csem, step, grid[0] * grid[1] * nk, "end")

    ins = [a, b] + ([res] if has_res else [])
    in_specs = [a_spec, b_spec] + ([o_spec] if has_res else [])
    sem = ("arbitrary",) * 3 if comm is not None else ("parallel", "parallel", "arbitrary")
    outs = _pcall(
        body, name=name, out_shape=tuple([_sds((m, n), out_dtype)] + c_out_shape), grid=grid,
        in_specs=in_specs + c_in_specs, out_specs=tuple([o_spec] + [HBM_SPEC] * len(c_ins)),
        scratch_shapes=([pltpu.VMEM((tm, tn), F32)] if nk > 1 else []) + c_scratch,
        compiler_params=_cp(dimension_semantics=sem),
    )(*(ins + c_ins))
    return outs[0] if comm is None else (outs[0], list(outs[1:]))


SLAB_WIN = 1280
SLAB_TAIL = 48


def _z_gathered(h1, own_slab, order, *, t, d, n_pad, rows, comm=None, tm=1024):
    rp = own_slab.shape[0]
    n_slab = N_DEV
    tm = min(tm, t)
    ni = t // tm
    sh = rows - SLAB_WIN
    assert sh >= 0 and (n_slab - 1) * sh <= 128
    c_ins, c_in_specs, c_out_shape, c_scratch = _comm_lists(comm)
    n_steps = n_slab * ni

    def body(*refs):
        pre, ins, cin, outs, cout, scr, csem = _split_refs(refs, 1, 2, len(c_ins), 2, len(c_ins), 5)
        order_ref = pre[0]
        h_ref, own_ref = ins
        o_ref, slabs_ref = outs
        slab_vm, send_sems, recv_sems, local_sem, load_sem = scr
        p, i = pl.program_id(0), pl.program_id(1)
        step = p * ni + i
        x, y, c = _mesh_pos()
        me, sibling = (x, y, c), (x, y, 1 - c)
        chips = [(1 - x, y), (x, 1 - y), (1 - x, 1 - y)]

        def slab(dev):
            return slabs_ref.at[4 * dev[0] + 2 * dev[1] + dev[2]]

        def copy(k, block, to, src=None, part=None):
            dst = slab(block) if part is None else slab(block).at[part]
            return pltpu.make_async_remote_copy(
                src_ref=dst if src is None else src, dst_ref=dst, send_sem=send_sems.at[k], recv_sem=recv_sems.at[k],
                device_id=to, device_id_type=MESH)

        x_nb, y_nb, diag = ((*chip, c) for chip in chips)
        cut = (rp // 32) * 16
        half_a, half_b = pl.ds(0, cut), pl.ds(cut, rp - cut)
        mine = pltpu.make_async_copy(own_ref, slab(me), local_sem)
        first = [copy(0, me, sibling, src=own_ref), copy(1, me, x_nb, src=own_ref), copy(2, me, y_nb, src=own_ref)]
        relays = [copy(3, x_nb, y_nb, part=half_a), copy(4, y_nb, x_nb, part=half_b)]
        passed = [copy(5 + j, (*chip, c), sibling) for j, chip in enumerate(chips)]

        @pl.when(step == 0)
        def _():
            mine.start()
            for cp in first:
                cp.start()

        _run_comm(comm, cin, cout, csem, step, n_steps, "start")

        def load(src):
            cp = pltpu.make_async_copy(src, slab_vm, load_sem)
            cp.start()
            cp.wait()

        @pl.when(i == 0)
        def _():
            @pl.when(p == 0)
            def _():
                load(own_ref)

            @pl.when(p == 1)
            def _():
                copy(0, sibling, me).wait_recv()
                load(slab(sibling))

            for j, chip in enumerate(chips):
                @pl.when(p == 2 + 2 * j)
                def _():
                    if j < 2:
                        copy(1 + j, (*chip, c), me).wait_recv()
                        relays[j].start()
                    else:
                        copy(3, diag, me, part=half_a).wait_recv()
                        copy(4, diag, me, part=half_b).wait_recv()
                    passed[j].start()
                    load(slab((*chip, c)))

                @pl.when(p == 3 + 2 * j)
                def _():
                    copy(5 + j, (*chip, 1 - c), me).wait_recv()
                    load(slab((*chip, 1 - c)))

        k = order_ref[p]
        y_k = lax.dot_general(h_ref[...], slab_vm[...], (((1,), (1,)), ((), ())), preferred_element_type=F32)
        o_ref[...] = pltpu.roll(y_k[:, 0:SLAB_WIN], k * sh, axis=1).astype(BF16)

        @pl.when(step == n_steps - 1)
        def _():
            for cp in first + relays + passed:
                cp.wait_send()
            mine.wait()

        _run_comm(comm, cin, cout, csem, step, n_steps, "end")

    outs = _pcall(
        body, name="mm_z", out_shape=tuple([_sds((t, n_pad), BF16), _sds((n_slab, rp, d), BF16)] + c_out_shape),
        grid_spec=pltpu.PrefetchScalarGridSpec(
            num_scalar_prefetch=1, grid=(n_slab, ni),
            in_specs=[pl.BlockSpec((tm, d), lambda p, i, order: (i, 0)), HBM_SPEC] + c_in_specs,
            out_specs=tuple([pl.BlockSpec((tm, SLAB_WIN), lambda p, i, order: (i, order[p])), HBM_SPEC] + [HBM_SPEC] * len(c_ins)),
            scratch_shapes=[pltpu.VMEM((rp, d), BF16), pltpu.SemaphoreType.DMA((8,)), pltpu.SemaphoreType.DMA((8,)),
                            pltpu.SemaphoreType.DMA, pltpu.SemaphoreType.DMA] + c_scratch),
        compiler_params=_cp(dimension_semantics=("arbitrary", "arbitrary")),
    )(order, h1, own_slab, *c_ins)
    return outs[0], outs[1], list(outs[2:])


def _z_fix(h1, slabs, z, *, t, d, rows, tm=4096):
    n_slab, rp, _ = slabs.shape
    tm = min(tm, t)
    sh = rows - SLAB_WIN
    tail_lo = rp - SLAB_TAIL
    assert rows - (n_slab - 1) * sh >= tail_lo and rp % SLAB_TAIL == 0

    def body(h_ref, t_ref, z_ref, o_ref):
        k = pl.program_id(0) + 1
        p = lax.dot_general(h_ref[...], t_ref[0], (((1,), (1,)), ((), ())), preferred_element_type=F32)
        pp = jnp.concatenate([p, jnp.zeros((tm, 128 - SLAB_TAIL), F32)], axis=1)
        pr = pltpu.roll(pp, 128 - (rows - tail_lo) + k * sh, axis=1)
        lane = lax.broadcasted_iota(jnp.int32, (tm, 128), 1)
        o_ref[...] = jnp.where(lane < k * sh, pr.astype(BF16), z_ref[...])

    blk = pl.BlockSpec((tm, 128), lambda b, i: (i, (SLAB_WIN // 128) * (b + 1)))
    return _pcall(
        body, name="mm_z_fix", out_shape=_sds(z.shape, z.dtype), grid=(n_slab - 1, t // tm),
        in_specs=[pl.BlockSpec((tm, d), lambda b, i: (i, 0)),
                  pl.BlockSpec((1, SLAB_TAIL, d), lambda b, i: (b, tail_lo // SLAB_TAIL, 0)), blk],
        out_specs=blk, input_output_aliases={2: 0}, compiler_params=_cp(),
    )(h1, slabs, z)


def _slab_window(a1_ref, a2_ref, k, sh):
    w = jnp.concatenate([a1_ref[...], a2_ref[...]], axis=1)
    width = SLAB_WIN + 128
    return pltpu.roll(w, width - k * sh, axis=1)


def _g_win_slabs(dz, h1, my_c, *, t, d, n_slab, rows, rp, tk=2048):
    sh = rows - SLAB_WIN
    tk = min(tk, t)
    nk = t // tk
    width = SLAB_WIN + 128
    half = n_slab // 2

    def slab_of(p, c):
        return 2 * (p % half) + jnp.where(p < half, 1 - c, c)

    def body(c_ref, a1_ref, a2_ref, h_ref, keep_ref, land_ref, acc, stage, send_sems, recv_sems):
        p, kk = pl.program_id(0), pl.program_id(1)
        x, y, c = _mesh_pos()
        ws = _slab_window(a1_ref, a2_ref, slab_of(p, c), sh)
        prod = lax.dot_general(ws, h_ref[...], (((0,), (0,)), ((), ())), preferred_element_type=F32)

        def push(q):
            return pltpu.make_async_remote_copy(
                src_ref=stage.at[q % 2], dst_ref=land_ref.at[q], send_sem=send_sems.at[q], recv_sem=recv_sems.at[q],
                device_id=(x, y, 1 - c), device_id_type=MESH)

        @pl.when(kk == 0)
        def _():
            acc[...] = prod

        @pl.when(kk > 0)
        def _():
            acc[...] += prod

        @pl.when(kk == nk - 1)
        def _():
            res = acc[0:rp, :].astype(BF16)
            for q in range(half):
                @pl.when(p == q)
                def _():
                    if q >= 2:
                        push(q - 2).wait_send()
                    stage[q % 2] = res
                    push(q).start()

            @pl.when(p >= half)
            def _():
                keep_ref[0] = res

        @pl.when((p == n_slab - 1) & (kk == nk - 1))
        def _():
            for q in range(max(half - 2, 0), half):
                push(q).wait_send()
            for q in range(half):
                push(q).wait_recv()

    outs = _pcall(
        body, name="mm_g_win", out_shape=(_sds((half, rp, d), BF16), _sds((half, rp, d), BF16)),
        grid_spec=pltpu.PrefetchScalarGridSpec(
            num_scalar_prefetch=1, grid=(n_slab, nk),
            in_specs=[pl.BlockSpec((tk, SLAB_WIN), lambda p, kk, c: (kk, slab_of(p, c[0]))),
                      pl.BlockSpec((tk, 128), lambda p, kk, c: (kk, (SLAB_WIN // 128) * (slab_of(p, c[0]) + 1))),
                      pl.BlockSpec((tk, d), lambda p, kk, c: (kk, 0))],
            out_specs=(pl.BlockSpec((1, rp, d), lambda p, kk, c: (jnp.maximum(p - half, 0), 0, 0)), HBM_SPEC),
            scratch_shapes=[pltpu.VMEM((width, d), F32), pltpu.VMEM((2, rp, d), BF16),
                            pltpu.SemaphoreType.DMA((half,)), pltpu.SemaphoreType.DMA((half,))]),
        compiler_params=_cp(dimension_semantics=("arbitrary", "arbitrary")),
    )(my_c.reshape(1).astype(jnp.int32), dz, dz, h1)
    return outs[0], outs[1]


def _dh1_from_slabs(dz, slabs, *, t, d, rows, comm=None, tm=1024):
    n_slab, rp, _ = slabs.shape
    sh = rows - SLAB_WIN
    width = SLAB_WIN + 128
    c_ins, c_in_specs, c_out_shape, c_scratch = _comm_lists(comm)
    grid = (t // tm, n_slab)

    def body(*refs):
        _, ins, cin, outs, cout, scr, csem = _split_refs(refs, 0, 3, len(c_ins), 1, len(c_ins), 1)
        a1_ref, a2_ref, s_ref = ins
        o_ref, acc = outs[0], scr[0]
        k = pl.program_id(1)
        step = pl.program_id(0) * n_slab + k
        _run_comm(comm, cin, cout, csem, step, grid[0] * n_slab, "start")
        ws = _slab_window(a1_ref, a2_ref, k, sh)
        s_end = jnp.concatenate([s_ref[0, SLAB_WIN:rp, :], jnp.zeros((width - rp, d), BF16)], axis=0)
        p = (jnp.dot(ws[:, 0:SLAB_WIN], s_ref[0, 0:SLAB_WIN, :], preferred_element_type=F32)
             + jnp.dot(ws[:, SLAB_WIN:width], s_end, preferred_element_type=F32))

        @pl.when(k == 0)
        def _():
            acc[...] = p

        @pl.when(k > 0)
        def _():
            acc[...] += p

        @pl.when(k == n_slab - 1)
        def _():
            o_ref[...] = acc[...]

        _run_comm(comm, cin, cout, csem, step, grid[0] * n_slab, "end")

    outs = _pcall(
        body, name="mm_dh1", out_shape=tuple([_sds((t, d), F32)] + c_out_shape), grid=grid,
        in_specs=[pl.BlockSpec((tm, SLAB_WIN), lambda i, k: (i, k)),
                  pl.BlockSpec((tm, 128), lambda i, k: (i, (SLAB_WIN // 128) * (k + 1))),
                  pl.BlockSpec((1, rp, d), lambda i, k: (k, 0, 0))] + c_in_specs,
        out_specs=tuple([pl.BlockSpec((tm, d), lambda i, k: (i, 0))] + [HBM_SPEC] * len(c_ins)),
        scratch_shapes=[pltpu.VMEM((tm, d), F32)] + c_scratch,
        compiler_params=_cp(dimension_semantics=("arbitrary", "arbitrary")),
    )(dz, dz, slabs, *c_ins)
    return outs[0], list(outs[1:])


def _rms_fwd(x, g, *, name, tm=512):
    t, d = x.shape
    tm = min(tm, t)

    def body(x_ref, g_ref, h_ref):
        xv = x_ref[...]
        r = lax.rsqrt(jnp.mean(xv * xv, axis=-1, keepdims=True) + EPS)
        h_ref[...] = (xv * r * g_ref[...]).astype(BF16)

    return _pcall(
        body, name=name, out_shape=_sds((t, d), BF16), grid=(t // tm,),
        in_specs=[pl.BlockSpec((tm, d), lambda i: (i, 0)), pl.BlockSpec((1, d), lambda i: (0, 0))],
        out_specs=pl.BlockSpec((tm, d), lambda i: (i, 0)), compiler_params=_cp(),
    )(x, g)


def _rms_bwd(dh, x, g, res, *, name, tm=512):
    t, d = x.shape
    tm = min(tm, t)
    has_res = res is not None

    def body(*refs):
        dh_ref, x_ref, g_ref = refs[:3]
        r_ref = refs[3] if has_res else None
        dx_ref, dg_ref = refs[-2], refs[-1]
        xv = x_ref[...]
        r = lax.rsqrt(jnp.mean(xv * xv, axis=-1, keepdims=True) + EPS)
        xh = xv * r
        dhv = dh_ref[...]
        gd = dhv * g_ref[...]
        dx = r * (gd - xh * jnp.mean(gd * xh, axis=-1, keepdims=True))
        if has_res:
            dx = dx + r_ref[...]
        dx_ref[...] = dx

        @pl.when(pl.program_id(0) == 0)
        def _():
            dg_ref[...] = jnp.zeros_like(dg_ref)

        dg_ref[...] += jnp.sum(dhv * xh, axis=0, keepdims=True)

    row = pl.BlockSpec((tm, d), lambda i: (i, 0))
    vec = pl.BlockSpec((1, d), lambda i: (0, 0))
    return _pcall(
        body, name=name, out_shape=(_sds((t, d), F32), _sds((1, d), F32)), grid=(t // tm,),
        in_specs=[row, row, vec] + ([row] if has_res else []), out_specs=(row, vec),
        compiler_params=_cp(dimension_semantics=("arbitrary",)),
    )(*([dh, x, g] + ([res] if has_res else [])))


def _conv_fwd(z, conv_w, conv_b, *, nb, s, d, tc=256):
    nc = d // tc

    def body(cb_ref, cc_ref, cv_ref, w_ref, b_ref, y_ref):
        u = cc_ref[...].astype(F32) * cv_ref[...].astype(F32)
        row = lax.broadcasted_iota(jnp.int32, u.shape, 0)
        u1 = jnp.where(row >= 1, pltpu.roll(u, 1, axis=0), 0.0)
        u2 = jnp.where(row >= 2, pltpu.roll(u, 2, axis=0), 0.0)
        w = w_ref[...]
        y = w[2:3, :] * u + w[1:2, :] * u1 + w[0:1, :] * u2 + b_ref[...]
        y_ref[...] = (cb_ref[...].astype(F32) * y).astype(BF16)

    def part(p):
        return pl.BlockSpec((s, tc), lambda b, j: (b, p * nc + j))

    return _pcall(
        body, name="conv_fwd", out_shape=_sds((nb * s, d), BF16), grid=(nb, nc),
        in_specs=[part(0), part(1), part(2), pl.BlockSpec((3, tc), lambda b, j: (0, j)), pl.BlockSpec((1, tc), lambda b, j: (0, j))],
        out_specs=pl.BlockSpec((s, tc), lambda b, j: (b, j)), compiler_params=_cp(),
    )(z, z, z, conv_w, conv_b)


def _conv_bwd(z, dy, conv_w, conv_b, dz, *, nb, s, d, tc=256, comm=None):
    nc = d // tc
    c_ins, c_in_specs, c_out_shape, c_scratch = _comm_lists(comm)
    n_steps = nc * nb * 3

    def body(*refs):
        _, ins, cin, outs, cout, scr, csem = _split_refs(refs, 0, 7, len(c_ins), 3, len(c_ins), 1)
        cb_ref, cc_ref, cv_ref, dy_ref, w_ref, b_ref, _ = ins
        dz_ref, dw_ref, db_ref = outs
        stash = scr[0]
        b_i, p = pl.program_id(1), pl.program_id(2)
        step = (pl.program_id(0) * nb + b_i) * 3 + p
        _run_comm(comm, cin, cout, csem, step, n_steps, "start")

        @pl.when(p == 0)
        def _():
            cc, cv = cc_ref[...].astype(F32), cv_ref[...].astype(F32)
            u = cc * cv
            n = u.shape[0]
            row = lax.broadcasted_iota(jnp.int32, u.shape, 0)
            u1 = jnp.where(row >= 1, pltpu.roll(u, 1, axis=0), 0.0)
            u2 = jnp.where(row >= 2, pltpu.roll(u, 2, axis=0), 0.0)
            w = w_ref[...]
            y = w[2:3, :] * u + w[1:2, :] * u1 + w[0:1, :] * u2 + b_ref[...]
            dyv = dy_ref[...]
            dconv = dyv * cb_ref[...].astype(F32)
            g1 = jnp.where(row < n - 1, pltpu.roll(dconv, n - 1, axis=0), 0.0)
            g2 = jnp.where(row < n - 2, pltpu.roll(dconv, n - 2, axis=0), 0.0)
            du = w[2:3, :] * dconv + w[1:2, :] * g1 + w[0:1, :] * g2
            stash[0] = (dyv * y).astype(BF16)
            stash[1] = (du * cv).astype(BF16)
            stash[2] = (du * cc).astype(BF16)
            dws = [jnp.sum(dconv * u2, axis=0, keepdims=True), jnp.sum(dconv * u1, axis=0, keepdims=True),
                   jnp.sum(dconv * u, axis=0, keepdims=True)]
            db = jnp.sum(dconv, axis=0, keepdims=True)

            @pl.when(b_i == 0)
            def _():
                for r in range(3):
                    dw_ref[r:r + 1, :] = dws[r]
                db_ref[...] = db

            @pl.when(b_i > 0)
            def _():
                for r in range(3):
                    dw_ref[r:r + 1, :] += dws[r]
                db_ref[...] += db

        dz_ref[...] = stash[p]
        _run_comm(comm, cin, cout, csem, step, n_steps, "end")

    outs = _pcall(
        body, name="conv_bwd",
        out_shape=tuple([_sds(dz.shape, dz.dtype), _sds((3, d), F32), _sds((1, d), F32)] + c_out_shape), grid=(nc, nb, 3),
        in_specs=[
            pl.BlockSpec((s, tc), lambda j, b, p: (b, j)), pl.BlockSpec((s, tc), lambda j, b, p: (b, nc + j)),
            pl.BlockSpec((s, tc), lambda j, b, p: (b, 2 * nc + j)), pl.BlockSpec((s, tc), lambda j, b, p: (b, j)),
            pl.BlockSpec((3, tc), lambda j, b, p: (0, j)), pl.BlockSpec((1, tc), lambda j, b, p: (0, j)),
            HBM_SPEC,
        ] + c_in_specs,
        out_specs=tuple([pl.BlockSpec((s, tc), lambda j, b, p: (b, p * nc + j)),
                         pl.BlockSpec((3, tc), lambda j, b, p: (0, j)), pl.BlockSpec((1, tc), lambda j, b, p: (0, j))]
                        + [HBM_SPEC] * len(c_ins)),
        scratch_shapes=[pltpu.VMEM((3, s, tc), BF16)] + c_scratch,
        input_output_aliases={6: 0},
        compiler_params=_cp(dimension_semantics=("arbitrary", "arbitrary", "arbitrary")),
    )(z, z, z, dy, conv_w, conv_b, dz, *c_ins)
    return outs[0], outs[1], outs[2], list(outs[3:])


def _head_rms(xv, g):
    r = lax.rsqrt(jnp.mean(xv * xv, axis=-1, keepdims=True) + EPS)
    return xv * r * g


def _head_rms_bwd(xv, g, dy):
    r = lax.rsqrt(jnp.mean(xv * xv, axis=-1, keepdims=True) + EPS)
    xh = xv * r
    gd = dy * g
    dx = r * (gd - xh * jnp.mean(gd * xh, axis=-1, keepdims=True))
    return dx, jnp.sum(dy * xh, axis=0, keepdims=True)


def _fox_prep(z, q_g, k_g, *, t, d, tm=256):
    nh = d // HD

    def body(z_ref, qg_ref, kg_ref, q_ref, k_ref, v_ref):
        qg, kg = qg_ref[...], kg_ref[...]
        for h in range(nh):
            c0 = h * HD
            q_ref[:, c0:c0 + HD] = _head_rms(z_ref[:, c0:c0 + HD].astype(F32), qg).astype(BF16)
            k_ref[:, c0:c0 + HD] = _head_rms(z_ref[:, d + c0:d + c0 + HD].astype(F32), kg).astype(BF16)
        v_ref[...] = z_ref[:, 2 * d:3 * d]

    o = pl.BlockSpec((tm, d), lambda i: (i, 0))
    g = pl.BlockSpec((1, HD), lambda i: (0, 0))
    return _pcall(
        body, name="fox_prep", out_shape=(_sds((t, d), BF16),) * 3, grid=(t // tm,),
        in_specs=[pl.BlockSpec((tm, 3 * d), lambda i: (i, 1)), g, g], out_specs=(o, o, o), compiler_params=_cp(),
    )(z, q_g, k_g)


def _lane_cumsum(v, reverse=False):
    n = v.shape[1]
    lane = lax.broadcasted_iota(jnp.int32, v.shape, 1)
    sh = 1
    while sh < n:
        if reverse:
            v = v + jnp.where(lane < n - sh, pltpu.roll(v, n - sh, axis=1), 0.0)
        else:
            v = v + jnp.where(lane >= sh, pltpu.roll(v, sh, axis=1), 0.0)
        sh *= 2
    return v


def _fox_gates(z, f_bias_col, *, nb, s, nh):
    def body(z_ref, b_ref, c_ref):
        ff = z_ref[...].T[0:nh, :] + b_ref[...]
        lf = jnp.minimum(ff, 0.0) - jnp.log(1.0 + jnp.exp(-jnp.abs(ff)))
        c_ref[0] = _lane_cumsum(lf) * SQRT_HD

    return _pcall(
        body, name="fox_gates", out_shape=_sds((nb, nh, s), F32), grid=(nb,),
        in_specs=[pl.BlockSpec((s, HD), lambda b: (b, 0)), pl.BlockSpec((nh, 1), lambda b: (0, 0))],
        out_specs=pl.BlockSpec((1, nh, s), lambda b: (b, 0, 0)), compiler_params=_cp(),
    )(z, f_bias_col)


def _fox_gates_bwd(z, f_bias_col, row_sums, col_sums, dz, *, nb, s, nh, ff_block):
    def body(z_ref, b_ref, rs_ref, cs_ref, dz_in, dz_ref, db_ref, dc_sc):
        b_i, h = pl.program_id(0), pl.program_id(1)
        dc_sc[pl.ds(h, 1), :] = (rs_ref[...] - cs_ref[...]).T[0:1, :]

        @pl.when(h == nh - 1)
        def _():
            ff = z_ref[...].T[0:nh, :] + b_ref[...]
            dlf = _lane_cumsum(dc_sc[...], reverse=True)
            dff = dlf * (1.0 / (1.0 + jnp.exp(ff)))
            pad = jnp.concatenate([dff, jnp.zeros((HD - nh, s), F32)], axis=0)
            dz_ref[...] = pad.T.astype(BF16)
            dbv = jnp.sum(dff, axis=1, keepdims=True)

            @pl.when(b_i == 0)
            def _():
                db_ref[...] = dbv

            @pl.when(b_i > 0)
            def _():
                db_ref[...] += dbv

    hs = pl.BlockSpec((s, HD), lambda b, h: (b, h))
    return _pcall(
        body, name="fox_gates_bwd", out_shape=(_sds(dz.shape, dz.dtype), _sds((nh, 1), F32)), grid=(nb, nh),
        in_specs=[pl.BlockSpec((s, HD), lambda b, h: (b, 0)), pl.BlockSpec((nh, 1), lambda b, h: (0, 0)), hs, hs, HBM_SPEC],
        out_specs=(pl.BlockSpec((s, HD), lambda b, h: (b, ff_block)), pl.BlockSpec((nh, 1), lambda b, h: (0, 0))),
        scratch_shapes=[pltpu.VMEM((nh, s), F32)],
        input_output_aliases={4: 0}, compiler_params=_cp(dimension_semantics=("arbitrary", "arbitrary")),
    )(z, f_bias_col, row_sums, col_sums, dz)


def _fox_pairs(nq, key_major):
    if key_major:
        pairs = [(i, j) for j in range(nq) for i in range(j, nq)]
    else:
        pairs = [(i, j) for i in range(nq) for j in range(i + 1)]
    return jnp.array([p[0] for p in pairs], jnp.int32), jnp.array([p[1] for p in pairs], jnp.int32)


def _with_ones(x):
    return jnp.concatenate([x, jnp.ones_like(x)], axis=1)


def _fox_specs(nb, nh, nq, tq, tk, hp):
    qs = pl.BlockSpec((tq, hp * HD), lambda b, h, p, *tb: (b * nq + tb[0][p], h))
    ks = pl.BlockSpec((tk, hp * HD), lambda b, h, p, *tb: (b * nq + tb[1][p], h))
    cs = pl.BlockSpec((hp, 1, tk), lambda b, h, p, *tb: (b * (nh // hp) + h, 0, tb[1][p]))
    return qs, ks, cs


def _fox_raw(qv, kv, ckv, diag, tq, tk):
    sc = lax.dot_general(qv, kv, (((1,), (1,)), ((), ())), preferred_element_type=F32) - ckv
    if diag:
        rows = lax.broadcasted_iota(jnp.int32, (tq, tk), 0)
        cols = lax.broadcasted_iota(jnp.int32, (tq, tk), 1)
        sc = jnp.where(rows >= cols, sc, NEG)
    return sc


def _fox_probs(qv, kv, ckv, lse_col, diag, tq, tk):
    sc = _fox_raw(qv, kv, ckv, diag, tq, tk)
    return jnp.exp2(sc * EXP2_C - lse_col * LOG2E)


def _fox_call(body, *, name, ins, in_specs, out_shape, out_specs, scratch, grid, tables, comm):
    c_ins, c_in_specs, c_out_shape, c_scratch = _comm_lists(comm)
    n_in, n_out, n_scr, nc = len(ins), len(out_shape), len(scratch), len(c_ins)
    n_steps = grid[0] * grid[1] * grid[2]

    def wrapped(*refs):
        pre, rin, cin, rout, cout, scr, csem = _split_refs(refs, len(tables), n_in, nc, n_out, nc, n_scr)
        step = (pl.program_id(0) * grid[1] + pl.program_id(1)) * grid[2] + pl.program_id(2)
        _run_comm(comm, cin, cout, csem, step, n_steps, "start")
        body(*pre, *rin, *rout, *scr)
        _run_comm(comm, cin, cout, csem, step, n_steps, "end")

    sem = ("arbitrary",) * 3 if comm is not None else ("parallel", "parallel", "arbitrary")
    outs = _pcall(
        wrapped, name=name, out_shape=tuple(list(out_shape) + c_out_shape),
        grid_spec=pltpu.PrefetchScalarGridSpec(
            num_scalar_prefetch=len(tables), grid=grid, in_specs=list(in_specs) + c_in_specs,
            out_specs=tuple(list(out_specs) + [HBM_SPEC] * nc), scratch_shapes=list(scratch) + c_scratch),
        compiler_params=_cp(dimension_semantics=sem),
    )(*tables, *ins, *c_ins)
    return list(outs[:n_out]), list(outs[n_out:])


def _fox_fwd(q, k, v, cks, *, nb, s, nh, tq=512, comm=None):
    tk = tq
    nq = s // tq
    t = nb * s
    hp = FOX_HP
    it, jt = _fox_pairs(nq, key_major=False)

    def body(it_ref, jt_ref, q_ref, k_ref, v_ref, c_ref, o_ref, lse_ref, m_sc, acc_sc):
        p_id = pl.program_id(2)
        i, j = it_ref[p_id], jt_ref[p_id]

        @pl.when(j == 0)
        def _():
            m_sc[...] = jnp.full_like(m_sc, NEG)
            acc_sc[...] = jnp.zeros_like(acc_sc)

        def step(diag):
            for hh in range(hp):
                cols = slice(hh * HD, (hh + 1) * HD)
                sc = _fox_raw(q_ref[:, cols], k_ref[:, cols], c_ref[hh], diag, tq, tk)
                m_old = m_sc[hh]
                m_new = jnp.maximum(m_old, jnp.max(sc, axis=-1, keepdims=True))
                alpha = jnp.exp2((m_old - m_new) * EXP2_C)
                p = jnp.exp2((sc - m_new) * EXP2_C).astype(BF16)
                acc = alpha * acc_sc[hh] + jnp.dot(p, _with_ones(v_ref[:, cols]), preferred_element_type=F32)
                if diag:
                    l = acc[:, HD:2 * HD]
                    o_ref[:, cols] = (acc[:, 0:HD] / l).astype(BF16)
                    lse_ref[:, cols] = m_new * (1.0 / SQRT_HD) + jnp.log(l)
                else:
                    acc_sc[hh] = acc
                    m_sc[hh] = m_new

        @pl.when(j < i)
        def _():
            step(False)

        @pl.when(j == i)
        def _():
            step(True)

    qs, ks, cs = _fox_specs(nb, nh, nq, tq, tk, hp)
    (o, lse), c_out = _fox_call(
        body, name="fox_fwd", ins=[q, k, v, cks], in_specs=[qs, ks, ks, cs],
        out_shape=[_sds((t, nh * HD), BF16), _sds((t, nh * HD), F32)], out_specs=[qs, qs],
        scratch=[pltpu.VMEM((hp, tq, 1), F32), pltpu.VMEM((hp, tq, 2 * HD), F32)],
        grid=(nb, nh // hp, int(it.shape[0])), tables=(it, jt), comm=comm)
    return o, lse, c_out


def _fox_bwd(q, k, v, cks, do, o, lse, *, nb, s, nh, tq=512, comm=None):
    tk = tq
    nq = s // tq
    t = nb * s
    hp = FOX_HP
    it, jt = _fox_pairs(nq, key_major=True)
    done = jnp.array([j for j in range(nq) for _ in range(j, nq)], jnp.int32)

    def body(it_ref, jt_ref, done_ref, q_ref, k_ref, v_ref, c_ref, do_ref, o_ref, lse_ref,
             dq_ref, rs_ref, dk_ref, dv_ref, cs_ref, dq_sc, dk_sc, dv_sc):
        p_id = pl.program_id(2)
        i, j = it_ref[p_id], jt_ref[p_id]
        rows_i = pl.ds(pl.multiple_of(i * tq, tq), tq)

        def step(diag):
            for hh in range(hp):
                cols = slice(hh * HD, (hh + 1) * HD)
                qv, kv = q_ref[:, cols], k_ref[:, cols]
                p = _fox_probs(qv, kv, c_ref[hh], lse_ref[:, hh * HD:hh * HD + 1], diag, tq, tk)
                dov = do_ref[:, cols]
                dob = dov.astype(BF16)
                dvp = lax.dot_general(p.astype(BF16), dob, (((0,), (0,)), ((), ())), preferred_element_type=F32)
                dp = lax.dot_general(dob, v_ref[:, cols], (((1,), (1,)), ((), ())), preferred_element_type=F32)
                delta = jnp.sum(dov * o_ref[:, cols].astype(F32), axis=-1, keepdims=True)
                ds = (p * (dp - delta)).astype(BF16)
                dkp = lax.dot_general(ds, _with_ones(qv), (((0,), (0,)), ((), ())), preferred_element_type=F32)
                dqp = jnp.dot(ds, _with_ones(kv), preferred_element_type=F32)
                if diag:
                    dk_sc[hh] = dkp
                    dv_sc[hh] = dvp
                else:
                    dk_sc[hh] += dkp
                    dv_sc[hh] += dvp

                @pl.when(j == 0)
                def _():
                    dq_sc[hh, rows_i, :] = dqp

                @pl.when(j > 0)
                def _():
                    dq_sc[hh, rows_i, :] += dqp

                if diag:
                    acc = dq_sc[hh, rows_i, :]
                    dq_ref[:, cols] = acc[:, 0:HD] * (1.0 / SQRT_HD)
                    rs_ref[:, cols] = acc[:, HD:2 * HD]

        @pl.when(i == j)
        def _():
            step(True)

        @pl.when(i > j)
        def _():
            step(False)

        @pl.when(i == nq - 1)
        def _():
            for hh in range(hp):
                cols = slice(hh * HD, (hh + 1) * HD)
                dk_ref[:, cols] = dk_sc[hh, :, 0:HD] * (1.0 / SQRT_HD)
                cs_ref[:, cols] = dk_sc[hh, :, HD:2 * HD]
                dv_ref[:, cols] = dv_sc[hh].astype(BF16)

    qs, ks, cs = _fox_specs(nb, nh, nq, tq, tk, hp)
    ds_spec = pl.BlockSpec((tq, hp * HD), lambda b, h, p, *tb: (b * nq + tb[2][p], h))
    (dq, rs, dk, dv, csum), c_out = _fox_call(
        body, name="fox_bwd", ins=[q, k, v, cks, do, o, lse], in_specs=[qs, ks, ks, cs, qs, qs, qs],
        out_shape=[_sds((t, nh * HD), F32), _sds((t, nh * HD), F32), _sds((t, nh * HD), F32), _sds((t, nh * HD), BF16),
                   _sds((t, nh * HD), F32)],
        out_specs=[ds_spec, ds_spec, ks, ks, ks],
        scratch=[pltpu.VMEM((hp, s, 2 * HD), F32), pltpu.VMEM((hp, tk, 2 * HD), F32), pltpu.VMEM((hp, tk, HD), F32)],
        grid=(nb, nh // hp, int(it.shape[0])), tables=(it, jt, done), comm=comm)
    return dq, rs, dk, dv, csum, c_out


def _fox_post(z, dqn, dkn, dv, q_g, k_g, dz, *, t, d, tm=256):
    nh = d // HD

    def body(z_ref, dq_ref, dk_ref, dv_ref, qg_ref, kg_ref, dz_in, dz_ref, dqg_ref, dkg_ref):
        qg, kg = qg_ref[...], kg_ref[...]
        aq = jnp.zeros((1, HD), F32)
        ak = jnp.zeros((1, HD), F32)
        for h in range(nh):
            c0 = h * HD
            dx, sg = _head_rms_bwd(z_ref[:, c0:c0 + HD].astype(F32), qg, dq_ref[:, c0:c0 + HD])
            dz_ref[:, c0:c0 + HD] = dx.astype(BF16)
            aq = aq + sg
            dx, sg = _head_rms_bwd(z_ref[:, d + c0:d + c0 + HD].astype(F32), kg, dk_ref[:, c0:c0 + HD])
            dz_ref[:, d + c0:d + c0 + HD] = dx.astype(BF16)
            ak = ak + sg
        dz_ref[:, 2 * d:3 * d] = dv_ref[...]

        @pl.when(pl.program_id(0) == 0)
        def _():
            dqg_ref[...] = aq
            dkg_ref[...] = ak

        @pl.when(pl.program_id(0) > 0)
        def _():
            dqg_ref[...] += aq
            dkg_ref[...] += ak

    o = pl.BlockSpec((tm, d), lambda i: (i, 0))
    g = pl.BlockSpec((1, HD), lambda i: (0, 0))
    z3 = pl.BlockSpec((tm, 3 * d), lambda i: (i, 1))
    return _pcall(
        body, name="fox_post", out_shape=(_sds(dz.shape, dz.dtype), _sds((1, HD), F32), _sds((1, HD), F32)), grid=(t // tm,),
        in_specs=[z3, o, o, o, g, g, HBM_SPEC], out_specs=(z3, g, g),
        input_output_aliases={6: 0}, compiler_params=_cp(dimension_semantics=("arbitrary",)),
    )(z, dqn, dkn, dv, q_g, k_g, dz)


def _xa_prep_kv(kv, k_g, *, rows, d):
    xd = d // XH

    def body(kv_ref, g_ref, k_ref, v_ref):
        g = g_ref[...]
        for h in range(XH):
            c0 = h * xd
            k_ref[:, c0:c0 + xd] = _head_rms(kv_ref[:, c0:c0 + xd], g).astype(BF16)
        v_ref[...] = kv_ref[:, d:2 * d].astype(BF16)

    return _pcall(body, name="xa_prep_kv", out_shape=(_sds((rows, d), BF16),) * 2, compiler_params=_cp())(kv, k_g)


def _xa_fwd(z, kn, vb, q_g, *, nb, s, d, nm, q_block, tq=512):
    xd = d // XH
    nq = s // tq

    def body(z_ref, k_ref, v_ref, g_ref, y_ref):
        g = g_ref[...]
        for h in range(XH):
            c0 = h * xd
            qn = _head_rms(z_ref[:, c0:c0 + xd].astype(F32), g).astype(BF16)
            sc = lax.dot_general(qn, k_ref[:, c0:c0 + xd], (((1,), (1,)), ((), ())), preferred_element_type=F32)
            sc = sc / math.sqrt(xd)
            p = jnp.exp(sc - jnp.max(sc, axis=-1, keepdims=True))
            p = p / jnp.sum(p, axis=-1, keepdims=True)
            y_ref[:, c0:c0 + xd] = jnp.dot(p.astype(BF16), v_ref[:, c0:c0 + xd], preferred_element_type=F32).astype(BF16)

    kvs = pl.BlockSpec((nm, d), lambda b, i: (b, 0))
    return _pcall(
        body, name="xa_fwd", out_shape=_sds((nb * s, d), BF16), grid=(nb, nq),
        in_specs=[pl.BlockSpec((tq, d), lambda b, i: (b * nq + i, q_block)), kvs, kvs, pl.BlockSpec((1, xd), lambda b, i: (0, 0))],
        out_specs=pl.BlockSpec((tq, d), lambda b, i: (b * nq + i, 0)), compiler_params=_cp(),
    )(z, kn, vb, q_g)


def _xa_bwd(z, kn, vb, q_g, dy, dz, *, nb, s, d, nm, q_block, tq=512):
    xd = d // XH
    nq = s // tq

    def body(z_ref, k_ref, v_ref, g_ref, dy_ref, dz_in, dz_ref, dk_ref, dv_ref, dg_ref):
        b_i, i = pl.program_id(0), pl.program_id(1)
        g = g_ref[...]

        @pl.when(i == 0)
        def _():
            dk_ref[...] = jnp.zeros_like(dk_ref)
            dv_ref[...] = jnp.zeros_like(dv_ref)

        @pl.when((i == 0) & (b_i == 0))
        def _():
            dg_ref[...] = jnp.zeros_like(dg_ref)

        for h in range(XH):
            c0 = h * xd
            xq = z_ref[:, c0:c0 + xd].astype(F32)
            qn = _head_rms(xq, g).astype(BF16)
            kh, vh = k_ref[:, c0:c0 + xd], v_ref[:, c0:c0 + xd]
            sc = lax.dot_general(qn, kh, (((1,), (1,)), ((), ())), preferred_element_type=F32) / math.sqrt(xd)
            p = jnp.exp(sc - jnp.max(sc, axis=-1, keepdims=True))
            p = p / jnp.sum(p, axis=-1, keepdims=True)
            dob = dy_ref[:, c0:c0 + xd].astype(BF16)
            dv_ref[:, c0:c0 + xd] += lax.dot_general(p.astype(BF16), dob, (((0,), (0,)), ((), ())), preferred_element_type=F32)
            dp = lax.dot_general(dob, vh, (((1,), (1,)), ((), ())), preferred_element_type=F32)
            ds = p * (dp - jnp.sum(p * dp, axis=-1, keepdims=True)) / math.sqrt(xd)
            dsb = ds.astype(BF16)
            dqn = jnp.dot(dsb, kh, preferred_element_type=F32)
            dk_ref[:, c0:c0 + xd] += lax.dot_general(dsb, qn, (((0,), (0,)), ((), ())), preferred_element_type=F32)
            dx, sg = _head_rms_bwd(xq, g, dqn)
            dz_ref[:, c0:c0 + xd] = dx.astype(BF16)
            dg_ref[...] += sg

    kvs = pl.BlockSpec((nm, d), lambda b, i: (b, 0))
    qs = pl.BlockSpec((tq, d), lambda b, i: (b * nq + i, q_block))
    gs = pl.BlockSpec((1, xd), lambda b, i: (0, 0))
    return _pcall(
        body, name="xa_bwd",
        out_shape=(_sds(dz.shape, dz.dtype), _sds((nb * nm, d), F32), _sds((nb * nm, d), F32), _sds((1, xd), F32)), grid=(nb, nq),
        in_specs=[qs, kvs, kvs, gs, pl.BlockSpec((tq, d), lambda b, i: (b * nq + i, 0)), HBM_SPEC],
        out_specs=(qs, kvs, kvs, gs), input_output_aliases={5: 0},
        compiler_params=_cp(dimension_semantics=("arbitrary", "arbitrary")),
    )(z, kn, vb, q_g, dy, dz)


def _xa_post_kv(kv, dkn, dv, k_g, *, rows, d):
    xd = d // XH

    def body(kv_ref, dk_ref, dv_ref, g_ref, dkv_ref, dg_ref):
        g = g_ref[...]
        acc = jnp.zeros((1, xd), F32)
        for h in range(XH):
            c0 = h * xd
            dx, sg = _head_rms_bwd(kv_ref[:, c0:c0 + xd], g, dk_ref[:, c0:c0 + xd])
            dkv_ref[:, c0:c0 + xd] = dx.astype(BF16)
            acc = acc + sg
        dkv_ref[:, d:2 * d] = dv_ref[...].astype(BF16)
        dg_ref[...] = acc

    return _pcall(body, name="xa_post_kv", out_shape=(_sds((rows, 2 * d), BF16), _sds((1, xd), F32)), compiler_params=_cp())(
        kv, dkn, dv, k_g)


def _sigmoid(v):
    return 1.0 / (1.0 + jnp.exp(-v))


def _branches_fwd(z, ys, ws, *, t, d, gate_block, tm=512, tn=512, comm=None):
    nj = d // tn
    c_ins, c_in_specs, c_out_shape, c_scratch = _comm_lists(comm)
    n_steps = (t // tm) * nj

    def body(*refs):
        _, ins, cin, outs, cout, _, csem = _split_refs(refs, 0, 9, len(c_ins), 2, len(c_ins), 0)
        ya_ref, yb_ref, yc_ref, wa_ref, wb_ref, wc_ref, ga_ref, gb_ref, gc_ref = ins
        m_ref, p_ref = outs
        step = pl.program_id(0) * nj + pl.program_id(1)
        _run_comm(comm, cin, cout, csem, step, n_steps, "start")
        acc = None
        for q, (y_ref, w_ref, g_ref) in enumerate(((ya_ref, wa_ref, ga_ref), (yb_ref, wb_ref, gb_ref), (yc_ref, wc_ref, gc_ref))):
            pr = jnp.dot(y_ref[...], w_ref[...], preferred_element_type=F32)
            p_ref[q] = pr.astype(BF16)
            term = _sigmoid(g_ref[...].astype(F32)) * pr
            acc = term if acc is None else acc + term
        m_ref[...] = acc.astype(BF16)
        _run_comm(comm, cin, cout, csem, step, n_steps, "end")

    y_spec = pl.BlockSpec((tm, d), lambda i, j: (i, 0))
    w_spec = pl.BlockSpec((d, tn), lambda i, j: (0, j))

    def gate(q):
        return pl.BlockSpec((tm, tn), lambda i, j: (i, (gate_block + q) * nj + j))

    outs = _pcall(
        body, name="branches_fwd", out_shape=tuple([_sds((t, d), BF16), _sds((3, t, d), BF16)] + c_out_shape), grid=(t // tm, nj),
        in_specs=[y_spec] * 3 + [w_spec] * 3 + [gate(0), gate(1), gate(2)] + c_in_specs,
        out_specs=tuple([pl.BlockSpec((tm, tn), lambda i, j: (i, j)), pl.BlockSpec((3, tm, tn), lambda i, j: (0, i, j))]
                        + [HBM_SPEC] * len(c_ins)),
        scratch_shapes=c_scratch, compiler_params=_cp(dimension_semantics=("arbitrary", "arbitrary")),
    )(*ys, *ws, z, z, z, *c_ins)
    return outs[0], outs[1], list(outs[2:])


def _gates_bwd(z, dm, proj, dz, *, t, d, gate_block, tm=512):
    def body(g_ref, dm_ref, p_ref, dz_in, dz_ref, da_ref, db_ref, dc_ref):
        q = pl.program_id(1)
        sg = _sigmoid(g_ref[...].astype(F32))
        dmv = dm_ref[...]
        dz_ref[...] = (dmv * p_ref[0].astype(F32) * sg * (1.0 - sg)).astype(BF16)
        dp = (dmv * sg).astype(BF16)
        @pl.when(q == 0)
        def _():
            da_ref[...] = dp

        @pl.when(q == 1)
        def _():
            db_ref[...] = dp

        @pl.when(q == 2)
        def _():
            dc_ref[...] = dp

    gs = pl.BlockSpec((tm, d), lambda i, q: (i, gate_block + q))
    o = pl.BlockSpec((tm, d), lambda i, q: (i, 0))
    return _pcall(
        body, name="gates_bwd", out_shape=(_sds(dz.shape, dz.dtype),) + (_sds((t, d), BF16),) * 3, grid=(t // tm, 3),
        in_specs=[gs, o, pl.BlockSpec((1, tm, d), lambda i, q: (q, i, 0)), HBM_SPEC], out_specs=(gs, o, o, o),
        input_output_aliases={3: 0}, compiler_params=_cp(dimension_semantics=("arbitrary", "arbitrary")),
    )(z, dm, proj, dz)


def _ffn_in_fwd(h2, wfi_t, *, t, d, dff, tm=512, tn=1408):
    nj = dff // tn

    def body(h_ref, wg_ref, wu_ref, g_ref, u_ref, a_ref):
        hv = h_ref[...]
        dn = (((1,), (1,)), ((), ()))
        gv = lax.dot_general(hv, wg_ref[...], dn, preferred_element_type=F32)
        uv = lax.dot_general(hv, wu_ref[...], dn, preferred_element_type=F32)
        g_ref[...] = gv.astype(BF16)
        u_ref[...] = uv.astype(BF16)
        a_ref[...] = (gv * _sigmoid(gv) * uv).astype(BF16)

    o = pl.BlockSpec((tm, tn), lambda i, j: (i, j))
    return _pcall(
        body, name="ffn_in_fwd", out_shape=(_sds((t, dff), BF16),) * 3, grid=(t // tm, nj),
        in_specs=[pl.BlockSpec((tm, d), lambda i, j: (i, 0)), pl.BlockSpec((tn, d), lambda i, j: (j, 0)),
                  pl.BlockSpec((tn, d), lambda i, j: (nj + j, 0))],
        out_specs=(o, o, o), compiler_params=_cp(),
    )(h2, wfi_t, wfi_t)


def _ffn_out_bwd(dy, wfo, gate, up, *, t, d, dff, tm=512, tn=1408):
    def body(dy_ref, w_ref, g_ref, u_ref, dg_ref, du_ref):
        dav = lax.dot_general(dy_ref[...].astype(BF16), w_ref[...], (((1,), (1,)), ((), ())), preferred_element_type=F32)
        gv = g_ref[...].astype(F32)
        sg = _sigmoid(gv)
        dg_ref[...] = (dav * u_ref[...].astype(F32) * sg * (1.0 + gv * (1.0 - sg))).astype(BF16)
        du_ref[...] = (dav * gv * sg).astype(BF16)

    o = pl.BlockSpec((tm, tn), lambda i, j: (i, j))
    return _pcall(
        body, name="ffn_out_bwd", out_shape=(_sds((t, dff), BF16),) * 2, grid=(t // tm, dff // tn),
        in_specs=[pl.BlockSpec((tm, d), lambda i, j: (i, 0)), pl.BlockSpec((tn, d), lambda i, j: (j, 0)), o, o],
        out_specs=(o, o), compiler_params=_cp(),
    )(dy, wfo, gate, up)


def _loss_head(y, target, *, t, d, tm=512):
    def body(y_ref, t_ref, dy_ref, l_ref):
        e = y_ref[...] - t_ref[...]
        dy_ref[...] = e * (1.0 / d)

        @pl.when(pl.program_id(0) == 0)
        def _():
            l_ref[...] = jnp.zeros_like(l_ref)

        l_ref[...] += jnp.sum(jnp.sum(e * e, axis=-1, keepdims=True), axis=0, keepdims=True) * (0.5 / d)

    o = pl.BlockSpec((tm, d), lambda i: (i, 0))
    dy, l = _pcall(
        body, name="loss_head", out_shape=(_sds((t, d), F32), _sds((1, HD), F32)), grid=(t // tm,),
        in_specs=[o, o], out_specs=(o, pl.BlockSpec((1, HD), lambda i: (0, 0))),
        compiler_params=_cp(dimension_semantics=("arbitrary",)),
    )(y, target)
    return dy, l[0, 0]


def _adamw(w, g, m, v, *, name):
    if w.ndim == 3:
        rows, _, cols = w.shape
        tm = max(tt for tt in range(1, 65) if rows % tt == 0)
        block, imap, grid = (tm, 1, cols), (lambda i, j: (i, 0, 0)), (rows // tm, 1)
    else:
        rows, cols = w.shape
        tm, tn = rows, cols
        for cand in (256, 128, 64, 32, 16, 8):
            if rows > cand and rows % cand == 0:
                tm = cand
                break
        if tm == rows and rows > 256 and cols % 128 == 0:
            tn = 128
        block, imap, grid = (tm, tn), (lambda i, j: (i, j)), (rows // tm, cols // tn)
    c1 = 1.0 - ADAM_B1 ** ADAM_STEP
    c2 = 1.0 - ADAM_B2 ** ADAM_STEP

    def body(w_ref, g_ref, m_ref, v_ref, d_ref, nm_ref, nv_ref):
        gv = g_ref[...]
        mn = ADAM_B1 * m_ref[...] + (1.0 - ADAM_B1) * gv
        vn = ADAM_B2 * v_ref[...] + (1.0 - ADAM_B2) * (gv * gv)
        d_ref[...] = -ADAM_LR * ((mn / c1) / (jnp.sqrt(vn / c2) + ADAM_EPS) + ADAM_WD * w_ref[...])
        nm_ref[...] = mn
        nv_ref[...] = vn

    spec = pl.BlockSpec(block, imap)
    return _pcall(
        body, name=name, out_shape=(_sds(w.shape, F32),) * 3, grid=grid,
        in_specs=[spec] * 4, out_specs=(spec,) * 3, compiler_params=_cp(),
    )(w, g, m, v)


def _pad_rows(a, rows):
    return a if a.shape[0] == rows else jnp.pad(a, ((0, rows - a.shape[0]),) + ((0, 0),) * (a.ndim - 1))


class _Pack:
    def __init__(self, row_sizes):
        self.rows = list(row_sizes)
        self.padded = [-(-r // 16) * 16 for r in self.rows]
        self.offs = [sum(self.padded[:i]) for i in range(len(self.rows))]
        self.total = sum(self.padded)

    def stack(self, blocks):
        return jnp.concatenate([_pad_rows(b.astype(BF16), p) for b, p in zip(blocks, self.padded)], axis=0)

    def full(self, gathered, i):
        r = self.rows[i]
        return gathered[:, self.offs[i]:self.offs[i] + r, :].reshape(N_DEV * r, gathered.shape[2])

    def for_core(self, full_grads, core):
        parts = []
        for gfull, r, p in zip(full_grads, self.rows, self.padded):
            pieces = gfull if isinstance(gfull, (tuple, list)) else (gfull,)
            blk = jnp.concatenate(
                [lax.dynamic_index_in_dim(g.reshape(-1, 2, r, g.shape[1]), core, axis=1, keepdims=False) for g in pieces],
                axis=0).astype(BF16)
            if p != r:
                blk = jnp.pad(blk, ((0, 0), (0, p - r), (0, 0)))
            parts.append(blk)
        return jnp.concatenate(parts, axis=1)

    def shard(self, g_pack, i):
        return g_pack[self.offs[i]:self.offs[i] + self.rows[i], :]


def kernel(x, mem, norm1_g, w_in, conv_w, conv_b, fox_f_bias, fox_q_g, fox_k_g, mem_norm_g, w_mem_kv, xa_q_g, xa_k_g, w_br_conv, w_br_fox, w_br_xa, w_o, norm2_g, w_ffn_in, w_ffn_out, loss_target, m_norm1_g, m_w_in, m_conv_w, m_conv_b, m_fox_f_bias, m_fox_q_g, m_fox_k_g, m_mem_norm_g, m_w_mem_kv, m_xa_q_g, m_xa_k_g, m_w_br_conv, m_w_br_fox, m_w_br_xa, m_w_o, m_norm2_g, m_w_ffn_in, m_w_ffn_out, v_norm1_g, v_w_in, v_conv_w, v_conv_b, v_fox_f_bias, v_fox_q_g, v_fox_k_g, v_mem_norm_g, v_w_mem_kv, v_xa_q_g, v_xa_k_g, v_w_br_conv, v_w_br_fox, v_w_br_xa, v_w_o, v_norm2_g, v_w_ffn_in, v_w_ffn_out):
    nb, s, d = x.shape
    t = nb * s
    nm = mem.shape[1]
    nh = d // HD
    dff = w_ffn_out.shape[1] * N_DEV
    n_in = w_in.shape[2] * N_DEV
    n_in_pad = -(-n_in // 1152) * 1152
    ff_block = (10 * d) // HD
    my_c = lax.axis_index("c")
    my_q = 2 * lax.axis_index("x") + lax.axis_index("y")

    pack_a = _Pack([w_in.shape[2]])
    pack_b = _Pack([w_mem_kv.shape[2], w_br_conv.shape[1], w_br_fox.shape[1], w_br_xa.shape[1], w_o.shape[1],
                    w_ffn_in.shape[2], w_ffn_out.shape[1]])
    gather_2 = _Pack([w_mem_kv.shape[2], w_br_conv.shape[1], w_br_fox.shape[1], w_br_xa.shape[1], w_o.shape[1]])
    stack_a = pack_a.stack([w_in[0].T])
    stack_2 = gather_2.stack([w_mem_kv[0].T, w_br_conv[0], w_br_fox[0], w_br_xa[0], w_o[0]])
    stack_fi = w_ffn_in[0].T.astype(BF16)
    stack_fo = w_ffn_out[0].astype(BF16)
    cw_block = _pad_rows(conv_w[0], 8)
    rows_a = w_in.shape[2]
    my_x, my_y = lax.axis_index("x"), lax.axis_index("y")
    arrival = [(my_x, my_y, my_c), (my_x, my_y, 1 - my_c)]
    for cx, cy in [(1 - my_x, my_y), (my_x, 1 - my_y), (1 - my_x, 1 - my_y)]:
        arrival += [(cx, cy, my_c), (cx, cy, 1 - my_c)]
    order = jnp.stack([4 * ax + 2 * ay + ac for ax, ay, ac in arrival]).astype(jnp.int32)

    x2 = x.reshape(t, d)
    mem2 = mem.reshape(nb * nm, d)
    tgt2 = loss_target.reshape(t, d)
    h1 = _rms_fwd(x2, norm1_g, name="rms1_fwd")
    z, slabs, _ = _z_gathered(h1, stack_a, order, t=t, d=d, n_pad=N_DEV * SLAB_WIN, rows=rows_a)
    z = _z_fix(h1, slabs, z, t=t, d=d, rows=rows_a)
    ff_rows = _pad_rows(slabs[N_DEV - 1, rows_a - nh:rows_a, :], HD)
    zff = _mm(h1, ff_rows, m=t, n=HD, k=d, tb=True, tm=1024, name="mm_z_ff")
    qn, kn, vb = _fox_prep(z, fox_q_g, fox_k_g, t=t, d=d)
    f_bias_col = fox_f_bias.reshape(nh, 1)
    ck_rows = _fox_gates(zff, f_bias_col, nb=nb, s=s, nh=nh).reshape(nb * nh, 1, s)
    y_fox, lse, (gathered_2, cw_g, gathered_fo) = _fox_fwd(qn, kn, vb, ck_rows, nb=nb, s=s, nh=nh,
                                                           comm=_AllGather([stack_2, cw_block, stack_fo], forward_at=0.8))
    wkv_t, wbc, wbf, wbx, wo = (gather_2.full(gathered_2, i) for i in range(5))
    wfo = gathered_fo.reshape(dff, d)
    conv_w_full = jnp.transpose(cw_g[:, 0:3, :], (1, 0, 2)).reshape(3, d)

    y_conv = _conv_fwd(z, conv_w_full, conv_b, nb=nb, s=s, d=d)
    mem_n = _rms_fwd(mem2, mem_norm_g, name="rms_mem_fwd")
    kv = _mm(mem_n, wkv_t, m=nb * nm, n=2 * d, k=d, tb=True, name="mm_kv")
    xkn, xvb = _xa_prep_kv(kv, xa_k_g, rows=nb * nm, d=d)
    y_xa = _xa_fwd(z, xkn, xvb, xa_q_g, nb=nb, s=s, d=d, nm=nm, q_block=6)
    merged, proj, (gathered_fi,) = _branches_fwd(z, (y_conv, y_fox, y_xa), (wbc, wbf, wbx), t=t, d=d, gate_block=7,
                                                 comm=_AllGather([stack_fi]))
    wfi_t = gathered_fi.reshape(2 * dff, d)
    x1 = _mm(merged, wo, m=t, n=d, k=d, res=x2, name="mm_x1")
    h2 = _rms_fwd(x1, norm2_g, name="rms2_fwd")
    ff_gate, ff_up, act = _ffn_in_fwd(h2, wfi_t, t=t, d=d, dff=dff)
    y = _mm(act, wfo, m=t, n=d, k=dff, res=x1, tk=dff, name="mm_y")
    dy, loss_local = _loss_head(y, tgt2, t=t, d=d)

    g_wfo = _mm(act, dy, m=dff, n=d, k=t, ta=True, tm=1408, tk=2048, out_dtype=BF16, name="mm_g_wfo")
    dgate, dup = _ffn_out_bwd(dy, wfo, ff_gate, ff_up, t=t, d=d, dff=dff)
    dh2 = _mm(dgate, wfi_t, m=t, n=d, k=dff, tk=dff, name="mm_dh2_g")
    dh2 = _mm(dup, wfi_t, m=t, n=d, k=dff, tk=dff, b_off=(1, 0), res=dh2, name="mm_dh2_u")
    g_wfi_g = _mm(dgate, h2, m=dff, n=d, k=t, ta=True, tm=1408, tk=2048, out_dtype=BF16, name="mm_g_wfi_g")
    g_wfi_u = _mm(dup, h2, m=dff, n=d, k=t, ta=True, tm=1408, tk=2048, out_dtype=BF16, name="mm_g_wfi_u")
    dx1, g_norm2 = _rms_bwd(dh2, x1, norm2_g, dy, name="rms2_bwd")
    dmerged = _mm(dx1, wo, m=t, n=d, k=d, tb=True, name="mm_dmerged")
    g_wo = _mm(merged, dx1, m=d, n=d, k=t, ta=True, tk=2048, out_dtype=BF16, name="mm_g_wo")

    dz = lax.empty((t, n_in_pad), BF16)
    dz, dpa, dpb, dpc = _gates_bwd(z, dmerged, proj, dz, t=t, d=d, gate_block=7)
    dy_conv = _mm(dpa, wbc, m=t, n=d, k=d, tb=True, name="mm_dy_conv")
    g_wbc = _mm(y_conv, dpa, m=d, n=d, k=t, ta=True, tk=4096, out_dtype=BF16, name="mm_g_wbc")
    dy_fox = _mm(dpb, wbf, m=t, n=d, k=d, tb=True, name="mm_dy_fox")
    g_wbf = _mm(y_fox, dpb, m=d, n=d, k=t, ta=True, tk=4096, out_dtype=BF16, name="mm_g_wbf")
    dy_xa = _mm(dpc, wbx, m=t, n=d, k=d, tb=True, name="mm_dy_xa")
    g_wbx = _mm(y_xa, dpc, m=d, n=d, k=t, ta=True, tk=4096, out_dtype=BF16, name="mm_g_wbx")

    dz, dxkn, dxv, g_xa_q = _xa_bwd(z, xkn, xvb, xa_q_g, dy_xa, dz, nb=nb, s=s, d=d, nm=nm, q_block=6)
    dkv, g_xa_k = _xa_post_kv(kv, dxkn, dxv, xa_k_g, rows=nb * nm, d=d)
    g_wkv_t = _mm(dkv, mem_n, m=2 * d, n=d, k=nb * nm, ta=True, out_dtype=BF16, name="mm_g_wkv")
    dmem_n = _mm(dkv, wkv_t, m=nb * nm, n=d, k=2 * d, tk=2 * d, name="mm_dmem")
    _, g_mem_norm = _rms_bwd(dmem_n, mem2, mem_norm_g, None, name="rms_mem_bwd")

    grads_b = [g_wkv_t, g_wbc, g_wbf, g_wbx, g_wo, (g_wfi_g, g_wfi_u), g_wfo]
    keep_b, send_b = pack_b.for_core(grads_b, my_c), pack_b.for_core(grads_b, 1 - my_c)
    dz, g_conv_w, g_conv_b, (from_sibling_b,) = _conv_bwd(z, dy_conv, conv_w_full, conv_b, dz, nb=nb, s=s, d=d,
                                                          comm=_SiblingSwap(send_b))
    part_b = _add2(keep_b, from_sibling_b, name="rs_add_sibling_b")
    dqn, ds_rows, dkn, dvb, ds_cols, (from_chips_b,) = _fox_bwd(qn, kn, vb, ck_rows, dy_fox, y_fox, lse, nb=nb, s=s, nh=nh,
                                                                comm=_ChipExchange(part_b))
    g_pack_b = _add4(lax.dynamic_index_in_dim(part_b, my_q, axis=0, keepdims=False), from_chips_b, name="rs_add_chips_b")
    dz, g_fox_q, g_fox_k = _fox_post(z, dqn, dkn, dvb, fox_q_g, fox_k_g, dz, t=t, d=d)
    dz, g_f_bias = _fox_gates_bwd(zff, f_bias_col, ds_rows, ds_cols, dz, nb=nb, s=s, nh=nh, ff_block=ff_block)

    keep_a, from_sibling_a = _g_win_slabs(dz, h1, my_c, t=t, d=d, n_slab=N_DEV, rows=rows_a, rp=slabs.shape[1])
    part_a = _add2(keep_a, from_sibling_a, name="rs_add_sibling_a")
    dh1, (from_chips_a,) = _dh1_from_slabs(dz, slabs, t=t, d=d, rows=rows_a, comm=_ChipExchange(part_a))
    g_pack_a = _add4(lax.dynamic_index_in_dim(part_a, my_q, axis=0, keepdims=False), from_chips_a, name="rs_add_chips_a",
                     rows_apart=True)
    dx, g_norm1 = _rms_bwd(dh1, x2, norm1_g, dx1, name="rms1_bwd")

    xd = d // XH
    tail = jnp.concatenate(
        [_pad_rows(g_f_bias.reshape(nh), HD).reshape(1, HD), g_fox_q, g_fox_k, g_xa_q, g_xa_k,
         jnp.zeros((1, d - 3 * HD - 2 * xd), F32)], axis=1)
    small = jnp.concatenate([g_norm1, g_norm2, g_conv_w, g_conv_b, g_mem_norm, tail], axis=0)
    small = _all_reduce_small(small)
    gs_conv_w = lax.dynamic_slice_in_dim(small[2:5], (2 * my_q + my_c) * (d // N_DEV), d // N_DEV, axis=1)

    grads = {
        "norm1_g": small[0:1], "w_in": g_pack_a, "conv_w": gs_conv_w, "conv_b": small[5:6],
        "fox_f_bias": small[7:8, 0:nh], "fox_q_g": small[7:8, HD:2 * HD], "fox_k_g": small[7:8, 2 * HD:3 * HD],
        "mem_norm_g": small[6:7], "w_mem_kv": pack_b.shard(g_pack_b, 0), "xa_q_g": small[7:8, 3 * HD:3 * HD + xd],
        "xa_k_g": small[7:8, 3 * HD + xd:3 * HD + 2 * xd], "w_br_conv": pack_b.shard(g_pack_b, 1),
        "w_br_fox": pack_b.shard(g_pack_b, 2), "w_br_xa": pack_b.shard(g_pack_b, 3), "w_o": pack_b.shard(g_pack_b, 4),
        "norm2_g": small[1:2], "w_ffn_in": pack_b.shard(g_pack_b, 5), "w_ffn_out": pack_b.shard(g_pack_b, 6),
    }
    transposed = {"w_in", "w_mem_kv", "w_ffn_in"}
    weights = {
        "norm1_g": (norm1_g, m_norm1_g, v_norm1_g), "w_in": (w_in, m_w_in, v_w_in), "conv_w": (conv_w, m_conv_w, v_conv_w),
        "conv_b": (conv_b, m_conv_b, v_conv_b), "fox_f_bias": (fox_f_bias, m_fox_f_bias, v_fox_f_bias),
        "fox_q_g": (fox_q_g, m_fox_q_g, v_fox_q_g), "fox_k_g": (fox_k_g, m_fox_k_g, v_fox_k_g),
        "mem_norm_g": (mem_norm_g, m_mem_norm_g, v_mem_norm_g), "w_mem_kv": (w_mem_kv, m_w_mem_kv, v_w_mem_kv),
        "xa_q_g": (xa_q_g, m_xa_q_g, v_xa_q_g), "xa_k_g": (xa_k_g, m_xa_k_g, v_xa_k_g),
        "w_br_conv": (w_br_conv, m_w_br_conv, v_w_br_conv), "w_br_fox": (w_br_fox, m_w_br_fox, v_w_br_fox),
        "w_br_xa": (w_br_xa, m_w_br_xa, v_w_br_xa), "w_o": (w_o, m_w_o, v_w_o), "norm2_g": (norm2_g, m_norm2_g, v_norm2_g),
        "w_ffn_in": (w_ffn_in, m_w_ffn_in, v_w_ffn_in), "w_ffn_out": (w_ffn_out, m_w_ffn_out, v_w_ffn_out),
    }
    out_g, out_d, out_m, out_v = [], [], [], []
    for name, (w, m, v) in weights.items():
        shape = w.shape
        if name == "w_in":
            w3, m3, v3 = (a.transpose(2, 0, 1) for a in (w, m, v))
            dl, mn, vn = _adamw(w3, g_pack_a, m3, v3, name="adamw_" + name)
            out_g.append(g_pack_a[:shape[2]].transpose(1, 2, 0))
            out_d.append(dl.transpose(1, 2, 0))
            out_m.append(mn.transpose(1, 2, 0))
            out_v.append(vn.transpose(1, 2, 0))
            continue
        tr = name in transposed

        def as2d(a):
            a = a.reshape(shape[-2], shape[-1])
            return a.T if tr else a

        def back(a):
            return (a.T if tr else a).reshape(shape)

        w2 = as2d(w)
        g2 = grads[name].reshape(w2.shape)
        dl, mn, vn = _adamw(w2, g2, as2d(m), as2d(v), name="adamw_" + name)
        out_g.append(back(g2))
        out_d.append(back(dl))
        out_m.append(back(mn))
        out_v.append(back(vn))

    loss = lax.psum(loss_local, ("x", "y", "c"))
    return (loss, dx.reshape(nb, s, d), *out_g, *out_d, *out_m, *out_v)
```

```python
import math

import jax
import jax.numpy as jnp
from jax import lax
from jax.experimental import pallas as pl
from jax.experimental.pallas import tpu as pltpu

F32, BF16 = jnp.float32, jnp.bfloat16
EPS = 1e-6
HD = 128
XH = 4
N_DEV = 8
MESH = pl.DeviceIdType.MESH
VMEM_LIMIT = 52 * 1024 * 1024
NEG = -1e30
SQRT_HD = math.sqrt(HD)
EXP2_C = math.log2(math.e) / SQRT_HD
LOG2E = math.log2(math.e)
FOX_HP = 4

ADAM_LR, ADAM_B1, ADAM_B2, ADAM_EPS, ADAM_WD, ADAM_STEP = 0.001, 0.9, 0.999, 1e-08, 0.01, 10


def _cp(**kw):
    return pltpu.CompilerParams(vmem_limit_bytes=VMEM_LIMIT, **kw)


def _pcall(body, **kw):
    return pl.pallas_call(body, **kw)


def _sds(shape, dtype):
    return jax.ShapeDtypeStruct(shape, dtype)


HBM_SPEC = pl.BlockSpec(memory_space=pl.ANY)


def _mesh_pos():
    return lax.axis_index("x"), lax.axis_index("y"), lax.axis_index("c")


class _AllGather:
    def __init__(self, blocks, forward_at=None):
        self.blocks = list(blocks)
        self.forward_at = forward_at
        n = len(self.blocks)
        self.out_shape = [_sds((N_DEV,) + b.shape, b.dtype) for b in self.blocks]
        self.scratch = [pltpu.SemaphoreType.DMA((n, 7)), pltpu.SemaphoreType.DMA((n, 7)), pltpu.SemaphoreType.DMA((n,))]

    def bind(self, ins, outs, sems):
        n = len(ins)
        send_sems, recv_sems, local_sems = sems
        x, y, c = _mesh_pos()
        me, sibling = (x, y, c), (x, y, 1 - c)
        chips = [(1 - x, y), (x, 1 - y), (1 - x, 1 - y)]

        def slab(t, dev):
            return outs[t].at[4 * dev[0] + 2 * dev[1] + dev[2]]

        def copy(t, k, block, to, src=None):
            dst = slab(t, block)
            return pltpu.make_async_remote_copy(
                src_ref=dst if src is None else src, dst_ref=dst, send_sem=send_sems.at[t, k], recv_sem=recv_sems.at[t, k],
                device_id=to, device_id_type=MESH)

        mine = [pltpu.make_async_copy(ins[t], slab(t, me), local_sems.at[t]) for t in range(n)]
        first = []
        for t in range(n):
            first.append(copy(t, 0, me, sibling, src=ins[t]))
            first += [copy(t, 1 + j, me, (*chip, c), src=ins[t]) for j, chip in enumerate(chips)]
        passed = [copy(t, 4 + j, (*chip, c), sibling) for j, chip in enumerate(chips) for t in range(n)]

        def start():
            for cp in mine + first:
                cp.start()

        def mid():
            for j, chip in enumerate(chips):
                for t in range(n):
                    copy(t, 1 + j, (*chip, c), me).wait_recv()
                    passed[j * n + t].start()

        def finish():
            for t in range(n):
                copy(t, 0, sibling, me).wait_recv()
                for j, chip in enumerate(chips):
                    copy(t, 4 + j, (*chip, 1 - c), me).wait_recv()
            for cp in first + passed:
                cp.wait_send()
            for cp in mine:
                cp.wait()

        return start, mid, finish


class _SiblingSwap:
    def __init__(self, send):
        self.blocks = [send]
        self.out_shape = [_sds(send.shape, send.dtype)]
        self.scratch = [pltpu.SemaphoreType.DMA, pltpu.SemaphoreType.DMA]

    def bind(self, ins, outs, sems):
        x, y, c = _mesh_pos()
        cp = pltpu.make_async_remote_copy(src_ref=ins[0], dst_ref=outs[0], send_sem=sems[0], recv_sem=sems[1],
                                          device_id=(x, y, 1 - c), device_id_type=MESH)
        return cp.start, (lambda: None), cp.wait


class _ChipExchange:
    def __init__(self, part):
        self.blocks = [part]
        self.out_shape = [_sds((3,) + part.shape[1:], part.dtype)]
        self.scratch = [pltpu.SemaphoreType.DMA((3,)), pltpu.SemaphoreType.DMA((3,))]

    def bind(self, ins, outs, sems):
        x, y, c = _mesh_pos()
        chips = [(1 - x, y), (x, 1 - y), (1 - x, 1 - y)]
        cps = [
            pltpu.make_async_remote_copy(src_ref=ins[0].at[2 * cx + cy], dst_ref=outs[0].at[j], send_sem=sems[0].at[j],
                                         recv_sem=sems[1].at[j], device_id=(cx, cy, c), device_id_type=MESH)
            for j, (cx, cy) in enumerate(chips)
        ]

        def start():
            for cp in cps:
                cp.start()

        def finish():
            for cp in cps:
                cp.wait()

        return start, (lambda: None), finish


def _comm_lists(comm):
    if comm is None:
        return [], [], [], []
    n = len(comm.blocks)
    return list(comm.blocks), [HBM_SPEC] * n, list(comm.out_shape), list(comm.scratch)


def _split_refs(refs, n_pre, n_in, n_cin, n_out, n_cout, n_scr):
    cuts = [n_pre, n_in, n_cin, n_out, n_cout, n_scr]
    out, at = [], 0
    for c in cuts:
        out.append(refs[at:at + c])
        at += c
    out.append(refs[at:])
    return out


def _run_comm(comm, cin, cout, csem, step, n_steps, when="start"):
    if comm is None:
        return
    start, mid, finish = comm.bind(cin, cout, csem)
    if when == "start":
        pl.when(step == 0)(start)
    else:
        frac = getattr(comm, "forward_at", None)
        mid_step = max(n_steps - 2, 0) if frac is None else min(int(frac * n_steps), max(n_steps - 2, 0))
        pl.when(step == mid_step)(mid)
        pl.when(step == n_steps - 1)(finish)


def _comm_only(comm, name):
    ins, in_specs, out_shape, scratch = _comm_lists(comm)
    n = len(ins)

    def body(*refs):
        start, mid, finish = comm.bind(refs[:n], refs[n:2 * n], refs[2 * n:])
        start()
        mid()
        finish()

    outs = _pcall(body, name=name, out_shape=tuple(out_shape), in_specs=in_specs, out_specs=tuple([HBM_SPEC] * n),
                  scratch_shapes=scratch, compiler_params=_cp())(*ins)
    return list(outs)


def _all_reduce_small(v):
    rows, cols = v.shape

    def body(v_ref, o_ref, slots, send_sems, recv_sems):
        x, y, c = _mesh_pos()
        me = 4 * x + 2 * y + c
        slots[me] = v_ref[...]
        cps = []
        for k in range(1, N_DEV):
            px, py, pc = x ^ (k >> 2), y ^ ((k >> 1) & 1), c ^ (k & 1)
            cps.append(pltpu.make_async_remote_copy(
                src_ref=v_ref, dst_ref=slots.at[me], send_sem=send_sems.at[k - 1], recv_sem=recv_sems.at[k - 1],
                device_id=(px, py, pc), device_id_type=MESH))
        for cp in cps:
            cp.start()
        for cp in cps:
            cp.wait()
        acc = slots[0]
        for k in range(1, N_DEV):
            acc = acc + slots[k]
        o_ref[...] = acc

    return _pcall(
        body, name="all_reduce_small", out_shape=_sds((rows, cols), F32),
        in_specs=[pl.BlockSpec(memory_space=pltpu.VMEM)], out_specs=pl.BlockSpec(memory_space=pltpu.VMEM),
        scratch_shapes=[pltpu.VMEM((N_DEV, rows, cols), F32), pltpu.SemaphoreType.DMA((7,)), pltpu.SemaphoreType.DMA((7,))],
        compiler_params=_cp(),
    )(v)


def _row_tile(rows, cap=1040):
    return max(tt for tt in range(16, cap + 1, 16) if rows % tt == 0)


def _add2(a, b, *, name):
    nq, rows, cols = a.shape
    tm = _row_tile(rows)

    def body(a_ref, b_ref, o_ref):
        o_ref[...] = (a_ref[...].astype(F32) + b_ref[...].astype(F32)).astype(BF16)

    spec = pl.BlockSpec((1, tm, cols), lambda q, i: (q, i, 0))
    return _pcall(body, name=name, out_shape=_sds(a.shape, BF16), grid=(nq, rows // tm),
                  in_specs=[spec, spec], out_specs=spec, compiler_params=_cp())(a, b)


def _add4(own, recv, *, name, rows_apart=False):
    rows, cols = own.shape
    tm = _row_tile(rows, 256 if rows_apart else 1040)

    def body(o_ref, r_ref, g_ref):
        tot = ((o_ref[...].astype(F32) + r_ref[0].astype(F32)) + r_ref[1].astype(F32)) + r_ref[2].astype(F32)
        if rows_apart:
            g_ref[:, 0, :] = tot
        else:
            g_ref[...] = tot

    if rows_apart:
        out_shape, out_spec = _sds((rows, 1, cols), F32), pl.BlockSpec((tm, 1, cols), lambda i: (i, 0, 0))
    else:
        out_shape, out_spec = _sds((rows, cols), F32), pl.BlockSpec((tm, cols), lambda i: (i, 0))
    return _pcall(
        body, name=name, out_shape=out_shape, grid=(rows // tm,),
        in_specs=[pl.BlockSpec((tm, cols), lambda i: (i, 0)), pl.BlockSpec((3, tm, cols), lambda i: (0, i, 0))],
        out_specs=out_spec, compiler_params=_cp(),
    )(own, recv)


def _mm(a, b, *, m, n, k, name, ta=False, tb=False, b_off=(0, 0), res=None, out_dtype=F32, tm=512, tn=1024, tk=1024,
        comm=None):
    tm, tn, tk = min(tm, m), min(tn, n), min(tk, k)
    assert m % tm == 0 and n % tn == 0 and k % tk == 0, (name, m, n, k, tm, tn, tk)
    nk = k // tk
    grid = (m // tm, n // tn, nk)
    bo0, bo1 = b_off
    if ta:
        a_spec = pl.BlockSpec((tk, tm), lambda i, j, kk: (kk, i))
    else:
        a_spec = pl.BlockSpec((tm, tk), lambda i, j, kk: (i, kk))
    if tb:
        b_spec = pl.BlockSpec((tn, tk), lambda i, j, kk: (j + bo0, kk + bo1))
    else:
        b_spec = pl.BlockSpec((tk, tn), lambda i, j, kk: (kk + bo0, j + bo1))
    o_spec = pl.BlockSpec((tm, tn), lambda i, j, kk: (i, j))
    dn = (((0 if ta else 1,), (1 if tb else 0,)), ((), ()))
    has_res = res is not None
    c_ins, c_in_specs, c_out_shape, c_scratch = _comm_lists(comm)
    n_in = 3 if has_res else 2
    n_scr = 1 if nk > 1 else 0

    def body(*refs):
        _, ins, cin, outs, cout, scr, csem = _split_refs(refs, 0, n_in, len(c_ins), 1, len(c_ins), n_scr)
        a_ref, b_ref = ins[0], ins[1]
        r_ref = ins[2] if has_res else None
        o_ref = outs[0]
        step = (pl.program_id(0) * grid[1] + pl.program_id(1)) * nk + pl.program_id(2)
        _run_comm(comm, cin, cout, csem, step, grid[0] * grid[1] * nk, "start")
        p = lax.dot_general(a_ref[...].astype(BF16), b_ref[...].astype(BF16), dn, preferred_element_type=F32)

        def finish(acc):
            if has_res:
                acc = acc + r_ref[...]
            o_ref[...] = acc.astype(out_dtype)

        if nk == 1:
            finish(p)
        else:
            acc_ref = scr[0]
            kk = pl.program_id(2)

            @pl.when(kk == 0)
            def _():
                acc_ref[...] = p

            @pl.when(kk > 0)
            def _():
                acc_ref[...] += p

            @pl.when(kk == nk - 1)
            def _():
                finish(acc_ref[...])

        _run_comm(comm, cin, cout, csem, step, grid[0] * grid[1] * nk, "end")

    ins = [a, b] + ([res] if has_res else [])
    in_specs = [a_spec, b_spec] + ([o_spec] if has_res else [])
    sem = ("arbitrary",) * 3 if comm is not None else ("parallel", "parallel", "arbitrary")
    outs = _pcall(
        body, name=name, out_shape=tuple([_sds((m, n), out_dtype)] + c_out_shape), grid=grid,
        in_specs=in_specs + c_in_specs, out_specs=tuple([o_spec] + [HBM_SPEC] * len(c_ins)),
        scratch_shapes=([pltpu.VMEM((tm, tn), F32)] if nk > 1 else []) + c_scratch,
        compiler_params=_cp(dimension_semantics=sem),
    )(*(ins + c_ins))
    return outs[0] if comm is None else (outs[0], list(outs[1:]))


SLAB_WIN = 1280
SLAB_TAIL = 48


def _chip_order(x, y, c):
    north = c == 1
    first = (jnp.where(north, 1 - x, x), jnp.where(north, y, 1 - y))
    second = (jnp.where(north, x, 1 - x), jnp.where(north, 1 - y, y))
    return [first, second, (1 - x, 1 - y)]


def _z_gathered(h1, own_slab, order, *, t, d, n_pad, rows, comm=None, tm=1024):
    rp = own_slab.shape[0]
    n_slab = N_DEV
    tm = min(tm, t)
    ni = t // tm
    sh = rows - SLAB_WIN
    assert sh >= 0 and (n_slab - 1) * sh <= 128
    c_ins, c_in_specs, c_out_shape, c_scratch = _comm_lists(comm)
    n_steps = n_slab * ni

    def body(*refs):
        pre, ins, cin, outs, cout, scr, csem = _split_refs(refs, 1, 2, len(c_ins), 2, len(c_ins), 5)
        order_ref = pre[0]
        h_ref, own_ref = ins
        o_ref, slabs_ref = outs
        slab_vm, send_sems, recv_sems, local_sem, load_sem = scr
        p, i = pl.program_id(0), pl.program_id(1)
        step = p * ni + i
        x, y, c = _mesh_pos()
        me, sibling = (x, y, c), (x, y, 1 - c)
        chips = _chip_order(x, y, c)

        def slab(dev):
            return slabs_ref.at[4 * dev[0] + 2 * dev[1] + dev[2]]

        def copy(k, block, to, src=None, part=None):
            dst = slab(block) if part is None else slab(block).at[part]
            return pltpu.make_async_remote_copy(
                src_ref=dst if src is None else src, dst_ref=dst, send_sem=send_sems.at[k], recv_sem=recv_sems.at[k],
                device_id=to, device_id_type=MESH)

        nb_1, nb_2, diag = ((*chip, c) for chip in chips)
        cut = (rp // 32) * 16
        half_a, half_b = pl.ds(0, cut), pl.ds(cut, rp - cut)
        mine = pltpu.make_async_copy(own_ref, slab(me), local_sem)
        first = [copy(0, me, sibling, src=own_ref), copy(1, me, nb_1, src=own_ref), copy(2, me, nb_2, src=own_ref)]
        relays = [copy(3, nb_1, nb_2, part=half_a), copy(4, nb_2, nb_1, part=half_b)]
        axis_of = [jnp.where(c == 1, 0, 1), jnp.where(c == 1, 1, 0), 2]
        passed = [copy(5 + axis_of[j], (*chip, c), sibling) for j, chip in enumerate(chips)]
        later = min(n_steps // 5, n_steps - 1)

        @pl.when(step == 0)
        def _():
            mine.start()
            first[0].start()
            first[1].start()

        @pl.when(step == later)
        def _():
            first[2].start()

        _run_comm(comm, cin, cout, csem, step, n_steps, "start")

        def load(src):
            cp = pltpu.make_async_copy(src, slab_vm, load_sem)
            cp.start()
            cp.wait()

        @pl.when(i == 0)
        def _():
            @pl.when(p == 0)
            def _():
                load(own_ref)

            @pl.when(p == 1)
            def _():
                copy(0, sibling, me).wait_recv()
                load(slab(sibling))

            for j, chip in enumerate(chips):
                @pl.when(p == 2 + 2 * j)
                def _():
                    if j < 2:
                        copy(1 + j, (*chip, c), me).wait_recv()
                        relays[j].start()
                    else:
                        copy(3, diag, me, part=half_a).wait_recv()
                        copy(4, diag, me, part=half_b).wait_recv()
                    passed[j].start()
                    load(slab((*chip, c)))

                other = (1, 0, 2)[j]

                @pl.when(p == 3 + 2 * j)
                def _():
                    copy(5 + axis_of[other], (*chips[other], 1 - c), me).wait_recv()
                    load(slab((*chips[other], 1 - c)))

        k = order_ref[p]
        y_k = lax.dot_general(h_ref[...], slab_vm[...], (((1,), (1,)), ((), ())), preferred_element_type=F32)
        o_ref[...] = pltpu.roll(y_k[:, 0:SLAB_WIN], k * sh, axis=1).astype(BF16)

        @pl.when(step == n_steps - 1)
        def _():
            for cp in first + relays + passed:
                cp.wait_send()
            mine.wait()

        _run_comm(comm, cin, cout, csem, step, n_steps, "end")

    outs = _pcall(
        body, name="mm_z", out_shape=tuple([_sds((t, n_pad), BF16), _sds((n_slab, rp, d), BF16)] + c_out_shape),
        grid_spec=pltpu.PrefetchScalarGridSpec(
            num_scalar_prefetch=1, grid=(n_slab, ni),
            in_specs=[pl.BlockSpec((tm, d), lambda p, i, order: (i, 0)), HBM_SPEC] + c_in_specs,
            out_specs=tuple([pl.BlockSpec((tm, SLAB_WIN), lambda p, i, order: (i, order[p])), HBM_SPEC] + [HBM_SPEC] * len(c_ins)),
            scratch_shapes=[pltpu.VMEM((rp, d), BF16), pltpu.SemaphoreType.DMA((8,)), pltpu.SemaphoreType.DMA((8,)),
                            pltpu.SemaphoreType.DMA, pltpu.SemaphoreType.DMA] + c_scratch),
        compiler_params=_cp(dimension_semantics=("arbitrary", "arbitrary")),
    )(order, h1, own_slab, *c_ins)
    return outs[0], outs[1], list(outs[2:])


def _z_fix(h1, slabs, z, *, t, d, rows, tm=4096):
    n_slab, rp, _ = slabs.shape
    tm = min(tm, t)
    sh = rows - SLAB_WIN
    tail_lo = rp - SLAB_TAIL
    assert rows - (n_slab - 1) * sh >= tail_lo and rp % SLAB_TAIL == 0

    def body(h_ref, t_ref, z_ref, o_ref):
        k = pl.program_id(0) + 1
        p = lax.dot_general(h_ref[...], t_ref[0], (((1,), (1,)), ((), ())), preferred_element_type=F32)
        pp = jnp.concatenate([p, jnp.zeros((tm, 128 - SLAB_TAIL), F32)], axis=1)
        pr = pltpu.roll(pp, 128 - (rows - tail_lo) + k * sh, axis=1)
        lane = lax.broadcasted_iota(jnp.int32, (tm, 128), 1)
        o_ref[...] = jnp.where(lane < k * sh, pr.astype(BF16), z_ref[...])

    blk = pl.BlockSpec((tm, 128), lambda b, i: (i, (SLAB_WIN // 128) * (b + 1)))
    return _pcall(
        body, name="mm_z_fix", out_shape=_sds(z.shape, z.dtype), grid=(n_slab - 1, t // tm),
        in_specs=[pl.BlockSpec((tm, d), lambda b, i: (i, 0)),
                  pl.BlockSpec((1, SLAB_TAIL, d), lambda b, i: (b, tail_lo // SLAB_TAIL, 0)), blk],
        out_specs=blk, input_output_aliases={2: 0}, compiler_params=_cp(),
    )(h1, slabs, z)


def _slab_window(a1_ref, a2_ref, k, sh):
    w = jnp.concatenate([a1_ref[...], a2_ref[...]], axis=1)
    width = SLAB_WIN + 128
    return pltpu.roll(w, width - k * sh, axis=1)


def _g_win_slabs(dz, h1, my_c, *, t, d, n_slab, rows, rp, tk=2048):
    sh = rows - SLAB_WIN
    tk = min(tk, t)
    nk = t // tk
    width = SLAB_WIN + 128
    half = n_slab // 2

    def slab_of(p, c):
        return 2 * (p % half) + jnp.where(p < half, 1 - c, c)

    def body(c_ref, a1_ref, a2_ref, h_ref, keep_ref, land_ref, acc, stage, send_sems, recv_sems):
        p, kk = pl.program_id(0), pl.program_id(1)
        x, y, c = _mesh_pos()
        ws = _slab_window(a1_ref, a2_ref, slab_of(p, c), sh)
        prod = lax.dot_general(ws, h_ref[...], (((0,), (0,)), ((), ())), preferred_element_type=F32)

        def push(q):
            return pltpu.make_async_remote_copy(
                src_ref=stage.at[q % 2], dst_ref=land_ref.at[q], send_sem=send_sems.at[q], recv_sem=recv_sems.at[q],
                device_id=(x, y, 1 - c), device_id_type=MESH)

        @pl.when(kk == 0)
        def _():
            acc[...] = prod

        @pl.when(kk > 0)
        def _():
            acc[...] += prod

        @pl.when(kk == nk - 1)
        def _():
            res = acc[0:rp, :].astype(BF16)
            for q in range(half):
                @pl.when(p == q)
                def _():
                    if q >= 2:
                        push(q - 2).wait_send()
                    stage[q % 2] = res
                    push(q).start()

            @pl.when(p >= half)
            def _():
                keep_ref[0] = res

        @pl.when((p == n_slab - 1) & (kk == nk - 1))
        def _():
            for q in range(max(half - 2, 0), half):
                push(q).wait_send()
            for q in range(half):
                push(q).wait_recv()

    outs = _pcall(
        body, name="mm_g_win", out_shape=(_sds((half, rp, d), BF16), _sds((half, rp, d), BF16)),
        grid_spec=pltpu.PrefetchScalarGridSpec(
            num_scalar_prefetch=1, grid=(n_slab, nk),
            in_specs=[pl.BlockSpec((tk, SLAB_WIN), lambda p, kk, c: (kk, slab_of(p, c[0]))),
                      pl.BlockSpec((tk, 128), lambda p, kk, c: (kk, (SLAB_WIN // 128) * (slab_of(p, c[0]) + 1))),
                      pl.BlockSpec((tk, d), lambda p, kk, c: (kk, 0))],
            out_specs=(pl.BlockSpec((1, rp, d), lambda p, kk, c: (jnp.maximum(p - half, 0), 0, 0)), HBM_SPEC),
            scratch_shapes=[pltpu.VMEM((width, d), F32), pltpu.VMEM((2, rp, d), BF16),
                            pltpu.SemaphoreType.DMA((half,)), pltpu.SemaphoreType.DMA((half,))]),
        compiler_params=_cp(dimension_semantics=("arbitrary", "arbitrary")),
    )(my_c.reshape(1).astype(jnp.int32), dz, dz, h1)
    return outs[0], outs[1]


def _dh1_from_slabs(dz, slabs, *, t, d, rows, comm=None, tm=1024):
    n_slab, rp, _ = slabs.shape
    sh = rows - SLAB_WIN
    width = SLAB_WIN + 128
    c_ins, c_in_specs, c_out_shape, c_scratch = _comm_lists(comm)
    grid = (t // tm, n_slab)

    def body(*refs):
        _, ins, cin, outs, cout, scr, csem = _split_refs(refs, 0, 3, len(c_ins), 1, len(c_ins), 1)
        a1_ref, a2_ref, s_ref = ins
        o_ref, acc = outs[0], scr[0]
        k = pl.program_id(1)
        step = pl.program_id(0) * n_slab + k
        _run_comm(comm, cin, cout, csem, step, grid[0] * n_slab, "start")
        ws = _slab_window(a1_ref, a2_ref, k, sh)
        s_end = jnp.concatenate([s_ref[0, SLAB_WIN:rp, :], jnp.zeros((width - rp, d), BF16)], axis=0)
        p = (jnp.dot(ws[:, 0:SLAB_WIN], s_ref[0, 0:SLAB_WIN, :], preferred_element_type=F32)
             + jnp.dot(ws[:, SLAB_WIN:width], s_end, preferred_element_type=F32))

        @pl.when(k == 0)
        def _():
            acc[...] = p

        @pl.when(k > 0)
        def _():
            acc[...] += p

        @pl.when(k == n_slab - 1)
        def _():
            o_ref[...] = acc[...]

        _run_comm(comm, cin, cout, csem, step, grid[0] * n_slab, "end")

    outs = _pcall(
        body, name="mm_dh1", out_shape=tuple([_sds((t, d), F32)] + c_out_shape), grid=grid,
        in_specs=[pl.BlockSpec((tm, SLAB_WIN), lambda i, k: (i, k)),
                  pl.BlockSpec((tm, 128), lambda i, k: (i, (SLAB_WIN // 128) * (k + 1))),
                  pl.BlockSpec((1, rp, d), lambda i, k: (k, 0, 0))] + c_in_specs,
        out_specs=tuple([pl.BlockSpec((tm, d), lambda i, k: (i, 0))] + [HBM_SPEC] * len(c_ins)),
        scratch_shapes=[pltpu.VMEM((tm, d), F32)] + c_scratch,
        compiler_params=_cp(dimension_semantics=("arbitrary", "arbitrary")),
    )(dz, dz, slabs, *c_ins)
    return outs[0], list(outs[1:])


def _rms_fwd(x, g, *, name, tm=512):
    t, d = x.shape
    tm = min(tm, t)

    def body(x_ref, g_ref, h_ref):
        xv = x_ref[...]
        r = lax.rsqrt(jnp.mean(xv * xv, axis=-1, keepdims=True) + EPS)
        h_ref[...] = (xv * r * g_ref[...]).astype(BF16)

    return _pcall(
        body, name=name, out_shape=_sds((t, d), BF16), grid=(t // tm,),
        in_specs=[pl.BlockSpec((tm, d), lambda i: (i, 0)), pl.BlockSpec((1, d), lambda i: (0, 0))],
        out_specs=pl.BlockSpec((tm, d), lambda i: (i, 0)), compiler_params=_cp(),
    )(x, g)


def _rms_bwd(dh, x, g, res, *, name, tm=512):
    t, d = x.shape
    tm = min(tm, t)
    has_res = res is not None

    def body(*refs):
        dh_ref, x_ref, g_ref = refs[:3]
        r_ref = refs[3] if has_res else None
        dx_ref, dg_ref = refs[-2], refs[-1]
        xv = x_ref[...]
        r = lax.rsqrt(jnp.mean(xv * xv, axis=-1, keepdims=True) + EPS)
        xh = xv * r
        dhv = dh_ref[...]
        gd = dhv * g_ref[...]
        dx = r * (gd - xh * jnp.mean(gd * xh, axis=-1, keepdims=True))
        if has_res:
            dx = dx + r_ref[...]
        dx_ref[...] = dx

        @pl.when(pl.program_id(0) == 0)
        def _():
            dg_ref[...] = jnp.zeros_like(dg_ref)

        dg_ref[...] += jnp.sum(dhv * xh, axis=0, keepdims=True)

    row = pl.BlockSpec((tm, d), lambda i: (i, 0))
    vec = pl.BlockSpec((1, d), lambda i: (0, 0))
    return _pcall(
        body, name=name, out_shape=(_sds((t, d), F32), _sds((1, d), F32)), grid=(t // tm,),
        in_specs=[row, row, vec] + ([row] if has_res else []), out_specs=(row, vec),
        compiler_params=_cp(dimension_semantics=("arbitrary",)),
    )(*([dh, x, g] + ([res] if has_res else [])))


def _conv_fwd(z, conv_w, conv_b, *, nb, s, d, tc=256):
    nc = d // tc

    def body(cb_ref, cc_ref, cv_ref, w_ref, b_ref, y_ref):
        u = cc_ref[...].astype(F32) * cv_ref[...].astype(F32)
        row = lax.broadcasted_iota(jnp.int32, u.shape, 0)
        u1 = jnp.where(row >= 1, pltpu.roll(u, 1, axis=0), 0.0)
        u2 = jnp.where(row >= 2, pltpu.roll(u, 2, axis=0), 0.0)
        w = w_ref[...]
        y = w[2:3, :] * u + w[1:2, :] * u1 + w[0:1, :] * u2 + b_ref[...]
        y_ref[...] = (cb_ref[...].astype(F32) * y).astype(BF16)

    def part(p):
        return pl.BlockSpec((s, tc), lambda b, j: (b, p * nc + j))

    return _pcall(
        body, name="conv_fwd", out_shape=_sds((nb * s, d), BF16), grid=(nb, nc),
        in_specs=[part(0), part(1), part(2), pl.BlockSpec((3, tc), lambda b, j: (0, j)), pl.BlockSpec((1, tc), lambda b, j: (0, j))],
        out_specs=pl.BlockSpec((s, tc), lambda b, j: (b, j)), compiler_params=_cp(),
    )(z, z, z, conv_w, conv_b)


def _conv_bwd(z, dy, conv_w, conv_b, dz, *, nb, s, d, tc=256, comm=None):
    nc = d // tc
    c_ins, c_in_specs, c_out_shape, c_scratch = _comm_lists(comm)
    n_steps = nc * nb * 3

    def body(*refs):
        _, ins, cin, outs, cout, scr, csem = _split_refs(refs, 0, 7, len(c_ins), 3, len(c_ins), 1)
        cb_ref, cc_ref, cv_ref, dy_ref, w_ref, b_ref, _ = ins
        dz_ref, dw_ref, db_ref = outs
        stash = scr[0]
        b_i, p = pl.program_id(1), pl.program_id(2)
        step = (pl.program_id(0) * nb + b_i) * 3 + p
        _run_comm(comm, cin, cout, csem, step, n_steps, "start")

        @pl.when(p == 0)
        def _():
            cc, cv = cc_ref[...].astype(F32), cv_ref[...].astype(F32)
            u = cc * cv
            n = u.shape[0]
            row = lax.broadcasted_iota(jnp.int32, u.shape, 0)
            u1 = jnp.where(row >= 1, pltpu.roll(u, 1, axis=0), 0.0)
            u2 = jnp.where(row >= 2, pltpu.roll(u, 2, axis=0), 0.0)
            w = w_ref[...]
            y = w[2:3, :] * u + w[1:2, :] * u1 + w[0:1, :] * u2 + b_ref[...]
            dyv = dy_ref[...]
            dconv = dyv * cb_ref[...].astype(F32)
            g1 = jnp.where(row < n - 1, pltpu.roll(dconv, n - 1, axis=0), 0.0)
            g2 = jnp.where(row < n - 2, pltpu.roll(dconv, n - 2, axis=0), 0.0)
            du = w[2:3, :] * dconv + w[1:2, :] * g1 + w[0:1, :] * g2
            stash[0] = (dyv * y).astype(BF16)
            stash[1] = (du * cv).astype(BF16)
            stash[2] = (du * cc).astype(BF16)
            dws = [jnp.sum(dconv * u2, axis=0, keepdims=True), jnp.sum(dconv * u1, axis=0, keepdims=True),
                   jnp.sum(dconv * u, axis=0, keepdims=True)]
            db = jnp.sum(dconv, axis=0, keepdims=True)

            @pl.when(b_i == 0)
            def _():
                for r in range(3):
                    dw_ref[r:r + 1, :] = dws[r]
                db_ref[...] = db

            @pl.when(b_i > 0)
            def _():
                for r in range(3):
                    dw_ref[r:r + 1, :] += dws[r]
                db_ref[...] += db

        dz_ref[...] = stash[p]
        _run_comm(comm, cin, cout, csem, step, n_steps, "end")

    outs = _pcall(
        body, name="conv_bwd",
        out_shape=tuple([_sds(dz.shape, dz.dtype), _sds((3, d), F32), _sds((1, d), F32)] + c_out_shape), grid=(nc, nb, 3),
        in_specs=[
            pl.BlockSpec((s, tc), lambda j, b, p: (b, j)), pl.BlockSpec((s, tc), lambda j, b, p: (b, nc + j)),
            pl.BlockSpec((s, tc), lambda j, b, p: (b, 2 * nc + j)), pl.BlockSpec((s, tc), lambda j, b, p: (b, j)),
            pl.BlockSpec((3, tc), lambda j, b, p: (0, j)), pl.BlockSpec((1, tc), lambda j, b, p: (0, j)),
            HBM_SPEC,
        ] + c_in_specs,
        out_specs=tuple([pl.BlockSpec((s, tc), lambda j, b, p: (b, p * nc + j)),
                         pl.BlockSpec((3, tc), lambda j, b, p: (0, j)), pl.BlockSpec((1, tc), lambda j, b, p: (0, j))]
                        + [HBM_SPEC] * len(c_ins)),
        scratch_shapes=[pltpu.VMEM((3, s, tc), BF16)] + c_scratch,
        input_output_aliases={6: 0},
        compiler_params=_cp(dimension_semantics=("arbitrary", "arbitrary", "arbitrary")),
    )(z, z, z, dy, conv_w, conv_b, dz, *c_ins)
    return outs[0], outs[1], outs[2], list(outs[3:])


def _head_rms(xv, g):
    r = lax.rsqrt(jnp.mean(xv * xv, axis=-1, keepdims=True) + EPS)
    return xv * r * g


def _head_rms_bwd(xv, g, dy):
    r = lax.rsqrt(jnp.mean(xv * xv, axis=-1, keepdims=True) + EPS)
    xh = xv * r
    gd = dy * g
    dx = r * (gd - xh * jnp.mean(gd * xh, axis=-1, keepdims=True))
    return dx, jnp.sum(dy * xh, axis=0, keepdims=True)


def _fox_prep(z, q_g, k_g, *, t, d, tm=256):
    nh = d // HD

    def body(z_ref, qg_ref, kg_ref, q_ref, k_ref, v_ref):
        qg, kg = qg_ref[...], kg_ref[...]
        for h in range(nh):
            c0 = h * HD
            q_ref[:, c0:c0 + HD] = _head_rms(z_ref[:, c0:c0 + HD].astype(F32), qg).astype(BF16)
            k_ref[:, c0:c0 + HD] = _head_rms(z_ref[:, d + c0:d + c0 + HD].astype(F32), kg).astype(BF16)
        v_ref[...] = z_ref[:, 2 * d:3 * d]

    o = pl.BlockSpec((tm, d), lambda i: (i, 0))
    g = pl.BlockSpec((1, HD), lambda i: (0, 0))
    return _pcall(
        body, name="fox_prep", out_shape=(_sds((t, d), BF16),) * 3, grid=(t // tm,),
        in_specs=[pl.BlockSpec((tm, 3 * d), lambda i: (i, 1)), g, g], out_specs=(o, o, o), compiler_params=_cp(),
    )(z, q_g, k_g)


def _lane_cumsum(v, reverse=False):
    n = v.shape[1]
    lane = lax.broadcasted_iota(jnp.int32, v.shape, 1)
    sh = 1
    while sh < n:
        if reverse:
            v = v + jnp.where(lane < n - sh, pltpu.roll(v, n - sh, axis=1), 0.0)
        else:
            v = v + jnp.where(lane >= sh, pltpu.roll(v, sh, axis=1), 0.0)
        sh *= 2
    return v


def _fox_gates(z, f_bias_col, *, nb, s, nh):
    def body(z_ref, b_ref, c_ref):
        ff = z_ref[...].T[0:nh, :] + b_ref[...]
        lf = jnp.minimum(ff, 0.0) - jnp.log(1.0 + jnp.exp(-jnp.abs(ff)))
        c_ref[0] = _lane_cumsum(lf) * SQRT_HD

    return _pcall(
        body, name="fox_gates", out_shape=_sds((nb, nh, s), F32), grid=(nb,),
        in_specs=[pl.BlockSpec((s, HD), lambda b: (b, 0)), pl.BlockSpec((nh, 1), lambda b: (0, 0))],
        out_specs=pl.BlockSpec((1, nh, s), lambda b: (b, 0, 0)), compiler_params=_cp(),
    )(z, f_bias_col)


def _fox_gates_bwd(z, f_bias_col, row_sums, col_sums, dz, *, nb, s, nh, ff_block):
    def body(z_ref, b_ref, rs_ref, cs_ref, dz_in, dz_ref, db_ref, dc_sc):
        b_i, h = pl.program_id(0), pl.program_id(1)
        dc_sc[pl.ds(h, 1), :] = (rs_ref[...] - cs_ref[...]).T[0:1, :]

        @pl.when(h == nh - 1)
        def _():
            ff = z_ref[...].T[0:nh, :] + b_ref[...]
            dlf = _lane_cumsum(dc_sc[...], reverse=True)
            dff = dlf * (1.0 / (1.0 + jnp.exp(ff)))
            pad = jnp.concatenate([dff, jnp.zeros((HD - nh, s), F32)], axis=0)
            dz_ref[...] = pad.T.astype(BF16)
            dbv = jnp.sum(dff, axis=1, keepdims=True)

            @pl.when(b_i == 0)
            def _():
                db_ref[...] = dbv

            @pl.when(b_i > 0)
            def _():
                db_ref[...] += dbv

    hs = pl.BlockSpec((s, HD), lambda b, h: (b, h))
    return _pcall(
        body, name="fox_gates_bwd", out_shape=(_sds(dz.shape, dz.dtype), _sds((nh, 1), F32)), grid=(nb, nh),
        in_specs=[pl.BlockSpec((s, HD), lambda b, h: (b, 0)), pl.BlockSpec((nh, 1), lambda b, h: (0, 0)), hs, hs, HBM_SPEC],
        out_specs=(pl.BlockSpec((s, HD), lambda b, h: (b, ff_block)), pl.BlockSpec((nh, 1), lambda b, h: (0, 0))),
        scratch_shapes=[pltpu.VMEM((nh, s), F32)],
        input_output_aliases={4: 0}, compiler_params=_cp(dimension_semantics=("arbitrary", "arbitrary")),
    )(z, f_bias_col, row_sums, col_sums, dz)


def _fox_pairs(nq, key_major):
    if key_major:
        pairs = [(i, j) for j in range(nq) for i in range(j, nq)]
    else:
        pairs = [(i, j) for i in range(nq) for j in range(i + 1)]
    return jnp.array([p[0] for p in pairs], jnp.int32), jnp.array([p[1] for p in pairs], jnp.int32)


def _with_ones(x):
    return jnp.concatenate([x, jnp.ones_like(x)], axis=1)


def _fox_specs(nb, nh, nq, tq, tk, hp):
    qs = pl.BlockSpec((tq, hp * HD), lambda b, h, p, *tb: (b * nq + tb[0][p], h))
    ks = pl.BlockSpec((tk, hp * HD), lambda b, h, p, *tb: (b * nq + tb[1][p], h))
    cs = pl.BlockSpec((hp, 1, tk), lambda b, h, p, *tb: (b * (nh // hp) + h, 0, tb[1][p]))
    return qs, ks, cs


def _fox_raw(qv, kv, ckv, diag, tq, tk):
    sc = lax.dot_general(qv, kv, (((1,), (1,)), ((), ())), preferred_element_type=F32) - ckv
    if diag:
        rows = lax.broadcasted_iota(jnp.int32, (tq, tk), 0)
        cols = lax.broadcasted_iota(jnp.int32, (tq, tk), 1)
        sc = jnp.where(rows >= cols, sc, NEG)
    return sc


def _fox_probs(qv, kv, ckv, lse_col, diag, tq, tk):
    sc = _fox_raw(qv, kv, ckv, diag, tq, tk)
    return jnp.exp2(sc * EXP2_C - lse_col * LOG2E)


def _fox_call(body, *, name, ins, in_specs, out_shape, out_specs, scratch, grid, tables, comm):
    c_ins, c_in_specs, c_out_shape, c_scratch = _comm_lists(comm)
    n_in, n_out, n_scr, nc = len(ins), len(out_shape), len(scratch), len(c_ins)
    n_steps = grid[0] * grid[1] * grid[2]

    def wrapped(*refs):
        pre, rin, cin, rout, cout, scr, csem = _split_refs(refs, len(tables), n_in, nc, n_out, nc, n_scr)
        step = (pl.program_id(0) * grid[1] + pl.program_id(1)) * grid[2] + pl.program_id(2)
        _run_comm(comm, cin, cout, csem, step, n_steps, "start")
        body(*pre, *rin, *rout, *scr)
        _run_comm(comm, cin, cout, csem, step, n_steps, "end")

    sem = ("arbitrary",) * 3 if comm is not None else ("parallel", "parallel", "arbitrary")
    outs = _pcall(
        wrapped, name=name, out_shape=tuple(list(out_shape) + c_out_shape),
        grid_spec=pltpu.PrefetchScalarGridSpec(
            num_scalar_prefetch=len(tables), grid=grid, in_specs=list(in_specs) + c_in_specs,
            out_specs=tuple(list(out_specs) + [HBM_SPEC] * nc), scratch_shapes=list(scratch) + c_scratch),
        compiler_params=_cp(dimension_semantics=sem),
    )(*tables, *ins, *c_ins)
    return list(outs[:n_out]), list(outs[n_out:])


def _fox_fwd(q, k, v, cks, *, nb, s, nh, tq=512, comm=None):
    tk = tq
    nq = s // tq
    t = nb * s
    hp = FOX_HP
    it, jt = _fox_pairs(nq, key_major=False)

    def body(it_ref, jt_ref, q_ref, k_ref, v_ref, c_ref, o_ref, lse_ref, m_sc, acc_sc):
        p_id = pl.program_id(2)
        i, j = it_ref[p_id], jt_ref[p_id]

        @pl.when(j == 0)
        def _():
            m_sc[...] = jnp.full_like(m_sc, NEG)
            acc_sc[...] = jnp.zeros_like(acc_sc)

        def step(diag):
            for hh in range(hp):
                cols = slice(hh * HD, (hh + 1) * HD)
                sc = _fox_raw(q_ref[:, cols], k_ref[:, cols], c_ref[hh], diag, tq, tk)
                m_old = m_sc[hh]
                m_new = jnp.maximum(m_old, jnp.max(sc, axis=-1, keepdims=True))
                alpha = jnp.exp2((m_old - m_new) * EXP2_C)
                p = jnp.exp2((sc - m_new) * EXP2_C).astype(BF16)
                acc = alpha * acc_sc[hh] + jnp.dot(p, _with_ones(v_ref[:, cols]), preferred_element_type=F32)
                if diag:
                    l = acc[:, HD:2 * HD]
                    o_ref[:, cols] = (acc[:, 0:HD] / l).astype(BF16)
                    lse_ref[:, cols] = m_new * (1.0 / SQRT_HD) + jnp.log(l)
                else:
                    acc_sc[hh] = acc
                    m_sc[hh] = m_new

        @pl.when(j < i)
        def _():
            step(False)

        @pl.when(j == i)
        def _():
            step(True)

    qs, ks, cs = _fox_specs(nb, nh, nq, tq, tk, hp)
    (o, lse), c_out = _fox_call(
        body, name="fox_fwd", ins=[q, k, v, cks], in_specs=[qs, ks, ks, cs],
        out_shape=[_sds((t, nh * HD), BF16), _sds((t, nh * HD), F32)], out_specs=[qs, qs],
        scratch=[pltpu.VMEM((hp, tq, 1), F32), pltpu.VMEM((hp, tq, 2 * HD), F32)],
        grid=(nb, nh // hp, int(it.shape[0])), tables=(it, jt), comm=comm)
    return o, lse, c_out


def _fox_bwd(q, k, v, cks, do, o, lse, *, nb, s, nh, tq=512, comm=None):
    tk = tq
    nq = s // tq
    t = nb * s
    hp = FOX_HP
    it, jt = _fox_pairs(nq, key_major=True)
    done = jnp.array([j for j in range(nq) for _ in range(j, nq)], jnp.int32)

    def body(it_ref, jt_ref, done_ref, q_ref, k_ref, v_ref, c_ref, do_ref, o_ref, lse_ref,
             dq_ref, rs_ref, dk_ref, dv_ref, cs_ref, dq_sc, dk_sc, dv_sc):
        p_id = pl.program_id(2)
        i, j = it_ref[p_id], jt_ref[p_id]
        rows_i = pl.ds(pl.multiple_of(i * tq, tq), tq)

        def step(diag):
            for hh in range(hp):
                cols = slice(hh * HD, (hh + 1) * HD)
                qv, kv = q_ref[:, cols], k_ref[:, cols]
                p = _fox_probs(qv, kv, c_ref[hh], lse_ref[:, hh * HD:hh * HD + 1], diag, tq, tk)
                dov = do_ref[:, cols]
                dob = dov.astype(BF16)
                dvp = lax.dot_general(p.astype(BF16), dob, (((0,), (0,)), ((), ())), preferred_element_type=F32)
                dp = lax.dot_general(dob, v_ref[:, cols], (((1,), (1,)), ((), ())), preferred_element_type=F32)
                delta = jnp.sum(dov * o_ref[:, cols].astype(F32), axis=-1, keepdims=True)
                ds = (p * (dp - delta)).astype(BF16)
                dkp = lax.dot_general(ds, _with_ones(qv), (((0,), (0,)), ((), ())), preferred_element_type=F32)
                dqp = jnp.dot(ds, _with_ones(kv), preferred_element_type=F32)
                if diag:
                    dk_sc[hh] = dkp
                    dv_sc[hh] = dvp
                else:
                    dk_sc[hh] += dkp
                    dv_sc[hh] += dvp

                @pl.when(j == 0)
                def _():
                    dq_sc[hh, rows_i, :] = dqp

                @pl.when(j > 0)
                def _():
                    dq_sc[hh, rows_i, :] += dqp

                if diag:
                    acc = dq_sc[hh, rows_i, :]
                    dq_ref[:, cols] = acc[:, 0:HD] * (1.0 / SQRT_HD)
                    rs_ref[:, cols] = acc[:, HD:2 * HD]

        @pl.when(i == j)
        def _():
            step(True)

        @pl.when(i > j)
        def _():
            step(False)

        @pl.when(i == nq - 1)
        def _():
            for hh in range(hp):
                cols = slice(hh * HD, (hh + 1) * HD)
                dk_ref[:, cols] = dk_sc[hh, :, 0:HD] * (1.0 / SQRT_HD)
                cs_ref[:, cols] = dk_sc[hh, :, HD:2 * HD]
                dv_ref[:, cols] = dv_sc[hh].astype(BF16)

    qs, ks, cs = _fox_specs(nb, nh, nq, tq, tk, hp)
    ds_spec = pl.BlockSpec((tq, hp * HD), lambda b, h, p, *tb: (b * nq + tb[2][p], h))
    (dq, rs, dk, dv, csum), c_out = _fox_call(
        body, name="fox_bwd", ins=[q, k, v, cks, do, o, lse], in_specs=[qs, ks, ks, cs, qs, qs, qs],
        out_shape=[_sds((t, nh * HD), F32), _sds((t, nh * HD), F32), _sds((t, nh * HD), F32), _sds((t, nh * HD), BF16),
                   _sds((t, nh * HD), F32)],
        out_specs=[ds_spec, ds_spec, ks, ks, ks],
        scratch=[pltpu.VMEM((hp, s, 2 * HD), F32), pltpu.VMEM((hp, tk, 2 * HD), F32), pltpu.VMEM((hp, tk, HD), F32)],
        grid=(nb, nh // hp, int(it.shape[0])), tables=(it, jt, done), comm=comm)
    return dq, rs, dk, dv, csum, c_out


def _fox_post(z, dqn, dkn, dv, q_g, k_g, dz, *, t, d, tm=256):
    nh = d // HD

    def body(z_ref, dq_ref, dk_ref, dv_ref, qg_ref, kg_ref, dz_in, dz_ref, dqg_ref, dkg_ref):
        qg, kg = qg_ref[...], kg_ref[...]
        aq = jnp.zeros((1, HD), F32)
        ak = jnp.zeros((1, HD), F32)
        for h in range(nh):
            c0 = h * HD
            dx, sg = _head_rms_bwd(z_ref[:, c0:c0 + HD].astype(F32), qg, dq_ref[:, c0:c0 + HD])
            dz_ref[:, c0:c0 + HD] = dx.astype(BF16)
            aq = aq + sg
            dx, sg = _head_rms_bwd(z_ref[:, d + c0:d + c0 + HD].astype(F32), kg, dk_ref[:, c0:c0 + HD])
            dz_ref[:, d + c0:d + c0 + HD] = dx.astype(BF16)
            ak = ak + sg
        dz_ref[:, 2 * d:3 * d] = dv_ref[...]

        @pl.when(pl.program_id(0) == 0)
        def _():
            dqg_ref[...] = aq
            dkg_ref[...] = ak

        @pl.when(pl.program_id(0) > 0)
        def _():
            dqg_ref[...] += aq
            dkg_ref[...] += ak

    o = pl.BlockSpec((tm, d), lambda i: (i, 0))
    g = pl.BlockSpec((1, HD), lambda i: (0, 0))
    z3 = pl.BlockSpec((tm, 3 * d), lambda i: (i, 1))
    return _pcall(
        body, name="fox_post", out_shape=(_sds(dz.shape, dz.dtype), _sds((1, HD), F32), _sds((1, HD), F32)), grid=(t // tm,),
        in_specs=[z3, o, o, o, g, g, HBM_SPEC], out_specs=(z3, g, g),
        input_output_aliases={6: 0}, compiler_params=_cp(dimension_semantics=("arbitrary",)),
    )(z, dqn, dkn, dv, q_g, k_g, dz)


def _xa_prep_kv(kv, k_g, *, rows, d):
    xd = d // XH

    def body(kv_ref, g_ref, k_ref, v_ref):
        g = g_ref[...]
        for h in range(XH):
            c0 = h * xd
            k_ref[:, c0:c0 + xd] = _head_rms(kv_ref[:, c0:c0 + xd], g).astype(BF16)
        v_ref[...] = kv_ref[:, d:2 * d].astype(BF16)

    return _pcall(body, name="xa_prep_kv", out_shape=(_sds((rows, d), BF16),) * 2, compiler_params=_cp())(kv, k_g)


def _xa_fwd(z, kn, vb, q_g, *, nb, s, d, nm, q_block, tq=512):
    xd = d // XH
    nq = s // tq

    def body(z_ref, k_ref, v_ref, g_ref, y_ref):
        g = g_ref[...]
        for h in range(XH):
            c0 = h * xd
            qn = _head_rms(z_ref[:, c0:c0 + xd].astype(F32), g).astype(BF16)
            sc = lax.dot_general(qn, k_ref[:, c0:c0 + xd], (((1,), (1,)), ((), ())), preferred_element_type=F32)
            sc = sc / math.sqrt(xd)
            p = jnp.exp(sc - jnp.max(sc, axis=-1, keepdims=True))
            p = p / jnp.sum(p, axis=-1, keepdims=True)
            y_ref[:, c0:c0 + xd] = jnp.dot(p.astype(BF16), v_ref[:, c0:c0 + xd], preferred_element_type=F32).astype(BF16)

    kvs = pl.BlockSpec((nm, d), lambda b, i: (b, 0))
    return _pcall(
        body, name="xa_fwd", out_shape=_sds((nb * s, d), BF16), grid=(nb, nq),
        in_specs=[pl.BlockSpec((tq, d), lambda b, i: (b * nq + i, q_block)), kvs, kvs, pl.BlockSpec((1, xd), lambda b, i: (0, 0))],
        out_specs=pl.BlockSpec((tq, d), lambda b, i: (b * nq + i, 0)), compiler_params=_cp(),
    )(z, kn, vb, q_g)


def _xa_bwd(z, kn, vb, q_g, dy, dz, *, nb, s, d, nm, q_block, tq=512):
    xd = d // XH
    nq = s // tq

    def body(z_ref, k_ref, v_ref, g_ref, dy_ref, dz_in, dz_ref, dk_ref, dv_ref, dg_ref):
        b_i, i = pl.program_id(0), pl.program_id(1)
        g = g_ref[...]

        @pl.when(i == 0)
        def _():
            dk_ref[...] = jnp.zeros_like(dk_ref)
            dv_ref[...] = jnp.zeros_like(dv_ref)

        @pl.when((i == 0) & (b_i == 0))
        def _():
            dg_ref[...] = jnp.zeros_like(dg_ref)

        for h in range(XH):
            c0 = h * xd
            xq = z_ref[:, c0:c0 + xd].astype(F32)
            qn = _head_rms(xq, g).astype(BF16)
            kh, vh = k_ref[:, c0:c0 + xd], v_ref[:, c0:c0 + xd]
            sc = lax.dot_general(qn, kh, (((1,), (1,)), ((), ())), preferred_element_type=F32) / math.sqrt(xd)
            p = jnp.exp(sc - jnp.max(sc, axis=-1, keepdims=True))
            p = p / jnp.sum(p, axis=-1, keepdims=True)
            dob = dy_ref[:, c0:c0 + xd].astype(BF16)
            dv_ref[:, c0:c0 + xd] += lax.dot_general(p.astype(BF16), dob, (((0,), (0,)), ((), ())), preferred_element_type=F32)
            dp = lax.dot_general(dob, vh, (((1,), (1,)), ((), ())), preferred_element_type=F32)
            ds = p * (dp - jnp.sum(p * dp, axis=-1, keepdims=True)) / math.sqrt(xd)
            dsb = ds.astype(BF16)
            dqn = jnp.dot(dsb, kh, preferred_element_type=F32)
            dk_ref[:, c0:c0 + xd] += lax.dot_general(dsb, qn, (((0,), (0,)), ((), ())), preferred_element_type=F32)
            dx, sg = _head_rms_bwd(xq, g, dqn)
            dz_ref[:, c0:c0 + xd] = dx.astype(BF16)
            dg_ref[...] += sg

    kvs = pl.BlockSpec((nm, d), lambda b, i: (b, 0))
    qs = pl.BlockSpec((tq, d), lambda b, i: (b * nq + i, q_block))
    gs = pl.BlockSpec((1, xd), lambda b, i: (0, 0))
    return _pcall(
        body, name="xa_bwd",
        out_shape=(_sds(dz.shape, dz.dtype), _sds((nb * nm, d), F32), _sds((nb * nm, d), F32), _sds((1, xd), F32)), grid=(nb, nq),
        in_specs=[qs, kvs, kvs, gs, pl.BlockSpec((tq, d), lambda b, i: (b * nq + i, 0)), HBM_SPEC],
        out_specs=(qs, kvs, kvs, gs), input_output_aliases={5: 0},
        compiler_params=_cp(dimension_semantics=("arbitrary", "arbitrary")),
    )(z, kn, vb, q_g, dy, dz)


def _xa_post_kv(kv, dkn, dv, k_g, *, rows, d):
    xd = d // XH

    def body(kv_ref, dk_ref, dv_ref, g_ref, dkv_ref, dg_ref):
        g = g_ref[...]
        acc = jnp.zeros((1, xd), F32)
        for h in range(XH):
            c0 = h * xd
            dx, sg = _head_rms_bwd(kv_ref[:, c0:c0 + xd], g, dk_ref[:, c0:c0 + xd])
            dkv_ref[:, c0:c0 + xd] = dx.astype(BF16)
            acc = acc + sg
        dkv_ref[:, d:2 * d] = dv_ref[...].astype(BF16)
        dg_ref[...] = acc

    return _pcall(body, name="xa_post_kv", out_shape=(_sds((rows, 2 * d), BF16), _sds((1, xd), F32)), compiler_params=_cp())(
        kv, dkn, dv, k_g)


def _sigmoid(v):
    return 1.0 / (1.0 + jnp.exp(-v))


def _branches_fwd(z, ys, ws, *, t, d, gate_block, tm=512, tn=512, comm=None):
    nj = d // tn
    c_ins, c_in_specs, c_out_shape, c_scratch = _comm_lists(comm)
    n_steps = (t // tm) * nj

    def body(*refs):
        _, ins, cin, outs, cout, _, csem = _split_refs(refs, 0, 9, len(c_ins), 2, len(c_ins), 0)
        ya_ref, yb_ref, yc_ref, wa_ref, wb_ref, wc_ref, ga_ref, gb_ref, gc_ref = ins
        m_ref, p_ref = outs
        step = pl.program_id(0) * nj + pl.program_id(1)
        _run_comm(comm, cin, cout, csem, step, n_steps, "start")
        acc = None
        for q, (y_ref, w_ref, g_ref) in enumerate(((ya_ref, wa_ref, ga_ref), (yb_ref, wb_ref, gb_ref), (yc_ref, wc_ref, gc_ref))):
            pr = jnp.dot(y_ref[...], w_ref[...], preferred_element_type=F32)
            p_ref[q] = pr.astype(BF16)
            term = _sigmoid(g_ref[...].astype(F32)) * pr
            acc = term if acc is None else acc + term
        m_ref[...] = acc.astype(BF16)
        _run_comm(comm, cin, cout, csem, step, n_steps, "end")

    y_spec = pl.BlockSpec((tm, d), lambda i, j: (i, 0))
    w_spec = pl.BlockSpec((d, tn), lambda i, j: (0, j))

    def gate(q):
        return pl.BlockSpec((tm, tn), lambda i, j: (i, (gate_block + q) * nj + j))

    outs = _pcall(
        body, name="branches_fwd", out_shape=tuple([_sds((t, d), BF16), _sds((3, t, d), BF16)] + c_out_shape), grid=(t // tm, nj),
        in_specs=[y_spec] * 3 + [w_spec] * 3 + [gate(0), gate(1), gate(2)] + c_in_specs,
        out_specs=tuple([pl.BlockSpec((tm, tn), lambda i, j: (i, j)), pl.BlockSpec((3, tm, tn), lambda i, j: (0, i, j))]
                        + [HBM_SPEC] * len(c_ins)),
        scratch_shapes=c_scratch, compiler_params=_cp(dimension_semantics=("arbitrary", "arbitrary")),
    )(*ys, *ws, z, z, z, *c_ins)
    return outs[0], outs[1], list(outs[2:])


def _gates_bwd(z, dm, proj, dz, *, t, d, gate_block, tm=512):
    def body(g_ref, dm_ref, p_ref, dz_in, dz_ref, da_ref, db_ref, dc_ref):
        q = pl.program_id(1)
        sg = _sigmoid(g_ref[...].astype(F32))
        dmv = dm_ref[...]
        dz_ref[...] = (dmv * p_ref[0].astype(F32) * sg * (1.0 - sg)).astype(BF16)
        dp = (dmv * sg).astype(BF16)
        @pl.when(q == 0)
        def _():
            da_ref[...] = dp

        @pl.when(q == 1)
        def _():
            db_ref[...] = dp

        @pl.when(q == 2)
        def _():
            dc_ref[...] = dp

    gs = pl.BlockSpec((tm, d), lambda i, q: (i, gate_block + q))
    o = pl.BlockSpec((tm, d), lambda i, q: (i, 0))
    return _pcall(
        body, name="gates_bwd", out_shape=(_sds(dz.shape, dz.dtype),) + (_sds((t, d), BF16),) * 3, grid=(t // tm, 3),
        in_specs=[gs, o, pl.BlockSpec((1, tm, d), lambda i, q: (q, i, 0)), HBM_SPEC], out_specs=(gs, o, o, o),
        input_output_aliases={3: 0}, compiler_params=_cp(dimension_semantics=("arbitrary", "arbitrary")),
    )(z, dm, proj, dz)


def _ffn_in_fwd(h2, wfi_t, *, t, d, dff, tm=512, tn=1408):
    nj = dff // tn

    def body(h_ref, wg_ref, wu_ref, g_ref, u_ref, a_ref):
        hv = h_ref[...]
        dn = (((1,), (1,)), ((), ()))
        gv = lax.dot_general(hv, wg_ref[...], dn, preferred_element_type=F32)
        uv = lax.dot_general(hv, wu_ref[...], dn, preferred_element_type=F32)
        g_ref[...] = gv.astype(BF16)
        u_ref[...] = uv.astype(BF16)
        a_ref[...] = (gv * _sigmoid(gv) * uv).astype(BF16)

    o = pl.BlockSpec((tm, tn), lambda i, j: (i, j))
    return _pcall(
        body, name="ffn_in_fwd", out_shape=(_sds((t, dff), BF16),) * 3, grid=(t // tm, nj),
        in_specs=[pl.BlockSpec((tm, d), lambda i, j: (i, 0)), pl.BlockSpec((tn, d), lambda i, j: (j, 0)),
                  pl.BlockSpec((tn, d), lambda i, j: (nj + j, 0))],
        out_specs=(o, o, o), compiler_params=_cp(),
    )(h2, wfi_t, wfi_t)


def _ffn_out_bwd(dy, wfo, gate, up, *, t, d, dff, tm=512, tn=1408):
    def body(dy_ref, w_ref, g_ref, u_ref, dg_ref, du_ref):
        dav = lax.dot_general(dy_ref[...].astype(BF16), w_ref[...], (((1,), (1,)), ((), ())), preferred_element_type=F32)
        gv = g_ref[...].astype(F32)
        sg = _sigmoid(gv)
        dg_ref[...] = (dav * u_ref[...].astype(F32) * sg * (1.0 + gv * (1.0 - sg))).astype(BF16)
        du_ref[...] = (dav * gv * sg).astype(BF16)

    o = pl.BlockSpec((tm, tn), lambda i, j: (i, j))
    return _pcall(
        body, name="ffn_out_bwd", out_shape=(_sds((t, dff), BF16),) * 2, grid=(t // tm, dff // tn),
        in_specs=[pl.BlockSpec((tm, d), lambda i, j: (i, 0)), pl.BlockSpec((tn, d), lambda i, j: (j, 0)), o, o],
        out_specs=(o, o), compiler_params=_cp(),
    )(dy, wfo, gate, up)


def _loss_head(y, target, *, t, d, tm=512):
    def body(y_ref, t_ref, dy_ref, l_ref):
        e = y_ref[...] - t_ref[...]
        dy_ref[...] = e * (1.0 / d)

        @pl.when(pl.program_id(0) == 0)
        def _():
            l_ref[...] = jnp.zeros_like(l_ref)

        l_ref[...] += jnp.sum(jnp.sum(e * e, axis=-1, keepdims=True), axis=0, keepdims=True) * (0.5 / d)

    o = pl.BlockSpec((tm, d), lambda i: (i, 0))
    dy, l = _pcall(
        body, name="loss_head", out_shape=(_sds((t, d), F32), _sds((1, HD), F32)), grid=(t // tm,),
        in_specs=[o, o], out_specs=(o, pl.BlockSpec((1, HD), lambda i: (0, 0))),
        compiler_params=_cp(dimension_semantics=("arbitrary",)),
    )(y, target)
    return dy, l[0, 0]


def _adamw(w, g, m, v, *, name):
    if w.ndim == 3:
        rows, _, cols = w.shape
        tm = max(tt for tt in range(1, 65) if rows % tt == 0)
        block, imap, grid = (tm, 1, cols), (lambda i, j: (i, 0, 0)), (rows // tm, 1)
    else:
        rows, cols = w.shape
        tm, tn = rows, cols
        for cand in (256, 128, 64, 32, 16, 8):
            if rows > cand and rows % cand == 0:
                tm = cand
                break
        if tm == rows and rows > 256 and cols % 128 == 0:
            tn = 128
        block, imap, grid = (tm, tn), (lambda i, j: (i, j)), (rows // tm, cols // tn)
    c1 = 1.0 - ADAM_B1 ** ADAM_STEP
    c2 = 1.0 - ADAM_B2 ** ADAM_STEP

    def body(w_ref, g_ref, m_ref, v_ref, d_ref, nm_ref, nv_ref):
        gv = g_ref[...]
        mn = ADAM_B1 * m_ref[...] + (1.0 - ADAM_B1) * gv
        vn = ADAM_B2 * v_ref[...] + (1.0 - ADAM_B2) * (gv * gv)
        d_ref[...] = -ADAM_LR * ((mn / c1) / (jnp.sqrt(vn / c2) + ADAM_EPS) + ADAM_WD * w_ref[...])
        nm_ref[...] = mn
        nv_ref[...] = vn

    spec = pl.BlockSpec(block, imap)
    return _pcall(
        body, name=name, out_shape=(_sds(w.shape, F32),) * 3, grid=grid,
        in_specs=[spec] * 4, out_specs=(spec,) * 3, compiler_params=_cp(),
    )(w, g, m, v)


def _pad_rows(a, rows):
    return a if a.shape[0] == rows else jnp.pad(a, ((0, rows - a.shape[0]),) + ((0, 0),) * (a.ndim - 1))


class _Pack:
    def __init__(self, row_sizes):
        self.rows = list(row_sizes)
        self.padded = [-(-r // 16) * 16 for r in self.rows]
        self.offs = [sum(self.padded[:i]) for i in range(len(self.rows))]
        self.total = sum(self.padded)

    def stack(self, blocks):
        return jnp.concatenate([_pad_rows(b.astype(BF16), p) for b, p in zip(blocks, self.padded)], axis=0)

    def full(self, gathered, i):
        r = self.rows[i]
        return gathered[:, self.offs[i]:self.offs[i] + r, :].reshape(N_DEV * r, gathered.shape[2])

    def for_core(self, full_grads, core):
        parts = []
        for gfull, r, p in zip(full_grads, self.rows, self.padded):
            pieces = gfull if isinstance(gfull, (tuple, list)) else (gfull,)
            blk = jnp.concatenate(
                [lax.dynamic_index_in_dim(g.reshape(-1, 2, r, g.shape[1]), core, axis=1, keepdims=False) for g in pieces],
                axis=0).astype(BF16)
            if p != r:
                blk = jnp.pad(blk, ((0, 0), (0, p - r), (0, 0)))
            parts.append(blk)
        return jnp.concatenate(parts, axis=1)

    def shard(self, g_pack, i):
        return g_pack[self.offs[i]:self.offs[i] + self.rows[i], :]


def kernel(x, mem, norm1_g, w_in, conv_w, conv_b, fox_f_bias, fox_q_g, fox_k_g, mem_norm_g, w_mem_kv, xa_q_g, xa_k_g, w_br_conv, w_br_fox, w_br_xa, w_o, norm2_g, w_ffn_in, w_ffn_out, loss_target, m_norm1_g, m_w_in, m_conv_w, m_conv_b, m_fox_f_bias, m_fox_q_g, m_fox_k_g, m_mem_norm_g, m_w_mem_kv, m_xa_q_g, m_xa_k_g, m_w_br_conv, m_w_br_fox, m_w_br_xa, m_w_o, m_norm2_g, m_w_ffn_in, m_w_ffn_out, v_norm1_g, v_w_in, v_conv_w, v_conv_b, v_fox_f_bias, v_fox_q_g, v_fox_k_g, v_mem_norm_g, v_w_mem_kv, v_xa_q_g, v_xa_k_g, v_w_br_conv, v_w_br_fox, v_w_br_xa, v_w_o, v_norm2_g, v_w_ffn_in, v_w_ffn_out):
    nb, s, d = x.shape
    t = nb * s
    nm = mem.shape[1]
    nh = d // HD
    dff = w_ffn_out.shape[1] * N_DEV
    n_in = w_in.shape[2] * N_DEV
    n_in_pad = -(-n_in // 1152) * 1152
    ff_block = (10 * d) // HD
    my_c = lax.axis_index("c")
    my_q = 2 * lax.axis_index("x") + lax.axis_index("y")

    pack_a = _Pack([w_in.shape[2]])
    pack_b = _Pack([w_mem_kv.shape[2], w_br_conv.shape[1], w_br_fox.shape[1], w_br_xa.shape[1], w_o.shape[1],
                    w_ffn_in.shape[2], w_ffn_out.shape[1]])
    gather_2 = _Pack([w_mem_kv.shape[2], w_br_conv.shape[1], w_br_fox.shape[1], w_br_xa.shape[1], w_o.shape[1]])
    stack_a = pack_a.stack([w_in[0].T])
    stack_2 = gather_2.stack([w_mem_kv[0].T, w_br_conv[0], w_br_fox[0], w_br_xa[0], w_o[0]])
    stack_fi = w_ffn_in[0].T.astype(BF16)
    stack_fo = w_ffn_out[0].astype(BF16)
    cw_block = _pad_rows(conv_w[0], 8)
    rows_a = w_in.shape[2]
    my_x, my_y = lax.axis_index("x"), lax.axis_index("y")
    arrival = [(my_x, my_y, my_c), (my_x, my_y, 1 - my_c)]
    chip_1, chip_2, chip_d = _chip_order(my_x, my_y, my_c)
    arrival += [(*chip_1, my_c), (*chip_2, 1 - my_c), (*chip_2, my_c), (*chip_1, 1 - my_c), (*chip_d, my_c), (*chip_d, 1 - my_c)]
    order = jnp.stack([4 * ax + 2 * ay + ac for ax, ay, ac in arrival]).astype(jnp.int32)

    x2 = x.reshape(t, d)
    mem2 = mem.reshape(nb * nm, d)
    tgt2 = loss_target.reshape(t, d)
    h1 = _rms_fwd(x2, norm1_g, name="rms1_fwd")
    z, slabs, _ = _z_gathered(h1, stack_a, order, t=t, d=d, n_pad=N_DEV * SLAB_WIN, rows=rows_a)
    z = _z_fix(h1, slabs, z, t=t, d=d, rows=rows_a)
    ff_rows = _pad_rows(slabs[N_DEV - 1, rows_a - nh:rows_a, :], HD)
    zff = _mm(h1, ff_rows, m=t, n=HD, k=d, tb=True, tm=1024, name="mm_z_ff")
    qn, kn, vb = _fox_prep(z, fox_q_g, fox_k_g, t=t, d=d)
    f_bias_col = fox_f_bias.reshape(nh, 1)
    ck_rows = _fox_gates(zff, f_bias_col, nb=nb, s=s, nh=nh).reshape(nb * nh, 1, s)
    y_fox, lse, (gathered_2, cw_g, gathered_fo) = _fox_fwd(qn, kn, vb, ck_rows, nb=nb, s=s, nh=nh,
                                                           comm=_AllGather([stack_2, cw_block, stack_fo], forward_at=0.8))
    wkv_t, wbc, wbf, wbx, wo = (gather_2.full(gathered_2, i) for i in range(5))
    wfo = gathered_fo.reshape(dff, d)
    conv_w_full = jnp.transpose(cw_g[:, 0:3, :], (1, 0, 2)).reshape(3, d)

    y_conv = _conv_fwd(z, conv_w_full, conv_b, nb=nb, s=s, d=d)
    mem_n = _rms_fwd(mem2, mem_norm_g, name="rms_mem_fwd")
    kv = _mm(mem_n, wkv_t, m=nb * nm, n=2 * d, k=d, tb=True, name="mm_kv")
    xkn, xvb = _xa_prep_kv(kv, xa_k_g, rows=nb * nm, d=d)
    y_xa = _xa_fwd(z, xkn, xvb, xa_q_g, nb=nb, s=s, d=d, nm=nm, q_block=6)
    merged, proj, (gathered_fi,) = _branches_fwd(z, (y_conv, y_fox, y_xa), (wbc, wbf, wbx), t=t, d=d, gate_block=7,
                                                 comm=_AllGather([stack_fi]))
    wfi_t = gathered_fi.reshape(2 * dff, d)
    x1 = _mm(merged, wo, m=t, n=d, k=d, res=x2, name="mm_x1")
    h2 = _rms_fwd(x1, norm2_g, name="rms2_fwd")
    ff_gate, ff_up, act = _ffn_in_fwd(h2, wfi_t, t=t, d=d, dff=dff)
    y = _mm(act, wfo, m=t, n=d, k=dff, res=x1, tk=dff, name="mm_y")
    dy, loss_local = _loss_head(y, tgt2, t=t, d=d)

    g_wfo = _mm(act, dy, m=dff, n=d, k=t, ta=True, tm=1408, tk=2048, out_dtype=BF16, name="mm_g_wfo")
    dgate, dup = _ffn_out_bwd(dy, wfo, ff_gate, ff_up, t=t, d=d, dff=dff)
    dh2 = _mm(dgate, wfi_t, m=t, n=d, k=dff, tk=dff, name="mm_dh2_g")
    dh2 = _mm(dup, wfi_t, m=t, n=d, k=dff, tk=dff, b_off=(1, 0), res=dh2, name="mm_dh2_u")
    g_wfi_g = _mm(dgate, h2, m=dff, n=d, k=t, ta=True, tm=1408, tk=2048, out_dtype=BF16, name="mm_g_wfi_g")
    g_wfi_u = _mm(dup, h2, m=dff, n=d, k=t, ta=True, tm=1408, tk=2048, out_dtype=BF16, name="mm_g_wfi_u")
    dx1, g_norm2 = _rms_bwd(dh2, x1, norm2_g, dy, name="rms2_bwd")
    dmerged = _mm(dx1, wo, m=t, n=d, k=d, tb=True, name="mm_dmerged")
    g_wo = _mm(merged, dx1, m=d, n=d, k=t, ta=True, tk=2048, out_dtype=BF16, name="mm_g_wo")

    dz = lax.empty((t, n_in_pad), BF16)
    dz, dpa, dpb, dpc = _gates_bwd(z, dmerged, proj, dz, t=t, d=d, gate_block=7)
    dy_conv = _mm(dpa, wbc, m=t, n=d, k=d, tb=True, name="mm_dy_conv")
    g_wbc = _mm(y_conv, dpa, m=d, n=d, k=t, ta=True, tk=4096, out_dtype=BF16, name="mm_g_wbc")
    dy_fox = _mm(dpb, wbf, m=t, n=d, k=d, tb=True, name="mm_dy_fox")
    g_wbf = _mm(y_fox, dpb, m=d, n=d, k=t, ta=True, tk=4096, out_dtype=BF16, name="mm_g_wbf")
    dy_xa = _mm(dpc, wbx, m=t, n=d, k=d, tb=True, name="mm_dy_xa")
    g_wbx = _mm(y_xa, dpc, m=d, n=d, k=t, ta=True, tk=4096, out_dtype=BF16, name="mm_g_wbx")

    dz, dxkn, dxv, g_xa_q = _xa_bwd(z, xkn, xvb, xa_q_g, dy_xa, dz, nb=nb, s=s, d=d, nm=nm, q_block=6)
    dkv, g_xa_k = _xa_post_kv(kv, dxkn, dxv, xa_k_g, rows=nb * nm, d=d)
    g_wkv_t = _mm(dkv, mem_n, m=2 * d, n=d, k=nb * nm, ta=True, out_dtype=BF16, name="mm_g_wkv")
    dmem_n = _mm(dkv, wkv_t, m=nb * nm, n=d, k=2 * d, tk=2 * d, name="mm_dmem")
    _, g_mem_norm = _rms_bwd(dmem_n, mem2, mem_norm_g, None, name="rms_mem_bwd")

    grads_b = [g_wkv_t, g_wbc, g_wbf, g_wbx, g_wo, (g_wfi_g, g_wfi_u), g_wfo]
    keep_b, send_b = pack_b.for_core(grads_b, my_c), pack_b.for_core(grads_b, 1 - my_c)
    dz, g_conv_w, g_conv_b, (from_sibling_b,) = _conv_bwd(z, dy_conv, conv_w_full, conv_b, dz, nb=nb, s=s, d=d,
                                                          comm=_SiblingSwap(send_b))
    part_b = _add2(keep_b, from_sibling_b, name="rs_add_sibling_b")
    dqn, ds_rows, dkn, dvb, ds_cols, (from_chips_b,) = _fox_bwd(qn, kn, vb, ck_rows, dy_fox, y_fox, lse, nb=nb, s=s, nh=nh,
                                                                comm=_ChipExchange(part_b))
    g_pack_b = _add4(lax.dynamic_index_in_dim(part_b, my_q, axis=0, keepdims=False), from_chips_b, name="rs_add_chips_b")
    dz, g_fox_q, g_fox_k = _fox_post(z, dqn, dkn, dvb, fox_q_g, fox_k_g, dz, t=t, d=d)
    dz, g_f_bias = _fox_gates_bwd(zff, f_bias_col, ds_rows, ds_cols, dz, nb=nb, s=s, nh=nh, ff_block=ff_block)

    keep_a, from_sibling_a = _g_win_slabs(dz, h1, my_c, t=t, d=d, n_slab=N_DEV, rows=rows_a, rp=slabs.shape[1])
    part_a = _add2(keep_a, from_sibling_a, name="rs_add_sibling_a")
    dh1, (from_chips_a,) = _dh1_from_slabs(dz, slabs, t=t, d=d, rows=rows_a, comm=_ChipExchange(part_a))
    g_pack_a = _add4(lax.dynamic_index_in_dim(part_a, my_q, axis=0, keepdims=False), from_chips_a, name="rs_add_chips_a",
                     rows_apart=True)
    dx, g_norm1 = _rms_bwd(dh1, x2, norm1_g, dx1, name="rms1_bwd")

    xd = d // XH
    tail = jnp.concatenate(
        [_pad_rows(g_f_bias.reshape(nh), HD).reshape(1, HD), g_fox_q, g_fox_k, g_xa_q, g_xa_k,
         jnp.zeros((1, d - 3 * HD - 2 * xd), F32)], axis=1)
    small = jnp.concatenate([g_norm1, g_norm2, g_conv_w, g_conv_b, g_mem_norm, tail], axis=0)
    small = _all_reduce_small(small)
    gs_conv_w = lax.dynamic_slice_in_dim(small[2:5], (2 * my_q + my_c) * (d // N_DEV), d // N_DEV, axis=1)

    grads = {
        "norm1_g": small[0:1], "w_in": g_pack_a, "conv_w": gs_conv_w, "conv_b": small[5:6],
        "fox_f_bias": small[7:8, 0:nh], "fox_q_g": small[7:8, HD:2 * HD], "fox_k_g": small[7:8, 2 * HD:3 * HD],
        "mem_norm_g": small[6:7], "w_mem_kv": pack_b.shard(g_pack_b, 0), "xa_q_g": small[7:8, 3 * HD:3 * HD + xd],
        "xa_k_g": small[7:8, 3 * HD + xd:3 * HD + 2 * xd], "w_br_conv": pack_b.shard(g_pack_b, 1),
        "w_br_fox": pack_b.shard(g_pack_b, 2), "w_br_xa": pack_b.shard(g_pack_b, 3), "w_o": pack_b.shard(g_pack_b, 4),
        "norm2_g": small[1:2], "w_ffn_in": pack_b.shard(g_pack_b, 5), "w_ffn_out": pack_b.shard(g_pack_b, 6),
    }
    transposed = {"w_in", "w_mem_kv", "w_ffn_in"}
    weights = {
        "norm1_g": (norm1_g, m_norm1_g, v_norm1_g), "w_in": (w_in, m_w_in, v_w_in), "conv_w": (conv_w, m_conv_w, v_conv_w),
        "conv_b": (conv_b, m_conv_b, v_conv_b), "fox_f_bias": (fox_f_bias, m_fox_f_bias, v_fox_f_bias),
        "fox_q_g": (fox_q_g, m_fox_q_g, v_fox_q_g), "fox_k_g": (fox_k_g, m_fox_k_g, v_fox_k_g),
        "mem_norm_g": (mem_norm_g, m_mem_norm_g, v_mem_norm_g), "w_mem_kv": (w_mem_kv, m_w_mem_kv, v_w_mem_kv),
        "xa_q_g": (xa_q_g, m_xa_q_g, v_xa_q_g), "xa_k_g": (xa_k_g, m_xa_k_g, v_xa_k_g),
        "w_br_conv": (w_br_conv, m_w_br_conv, v_w_br_conv), "w_br_fox": (w_br_fox, m_w_br_fox, v_w_br_fox),
        "w_br_xa": (w_br_xa, m_w_br_xa, v_w_br_xa), "w_o": (w_o, m_w_o, v_w_o), "norm2_g": (norm2_g, m_norm2_g, v_norm2_g),
        "w_ffn_in": (w_ffn_in, m_w_ffn_in, v_w_ffn_in), "w_ffn_out": (w_ffn_out, m_w_ffn_out, v_w_ffn_out),
    }
    out_g, out_d, out_m, out_v = [], [], [], []
    for name, (w, m, v) in weights.items():
        shape = w.shape
        if name == "w_in":
            w3, m3, v3 = (a.transpose(2, 0, 1) for a in (w, m, v))
            dl, mn, vn = _adamw(w3, g_pack_a, m3, v3, name="adamw_" + name)
            out_g.append(g_pack_a[:shape[2]].transpose(1, 2, 0))
            out_d.append(dl.transpose(1, 2, 0))
            out_m.append(mn.transpose(1, 2, 0))
            out_v.append(vn.transpose(1, 2, 0))
            continue
        tr = name in transposed

        def as2d(a):
            a = a.reshape(shape[-2], shape[-1])
            return a.T if tr else a

        def back(a):
            return (a.T if tr else a).reshape(shape)

        w2 = as2d(w)
        g2 = grads[name].reshape(w2.shape)
        dl, mn, vn = _adamw(w2, g2, as2d(m), as2d(v), name="adamw_" + name)
        out_g.append(back(g2))
        out_d.append(back(dl))
        out_m.append(back(mn))
        out_v.append(back(vn))

    loss = lax.psum(loss_local, ("x", "y", "c"))
    return (loss, dx.reshape(nb, s, d), *out_g, *out_d, *out_m, *out_v)
```

```python
import math

import jax
import jax.numpy as jnp
from jax import lax
from jax.experimental import pallas as pl
from jax.experimental.pallas import tpu as pltpu

F32, BF16 = jnp.float32, jnp.bfloat16
EPS = 1e-6
HD = 128
XH = 4
N_DEV = 8
MESH = pl.DeviceIdType.MESH
VMEM_LIMIT = 52 * 1024 * 1024
NEG = -1e30
SQRT_HD = math.sqrt(HD)
EXP2_C = math.log2(math.e) / SQRT_HD
LOG2E = math.log2(math.e)
FOX_HP = 4

ADAM_LR, ADAM_B1, ADAM_B2, ADAM_EPS, ADAM_WD, ADAM_STEP = 0.001, 0.9, 0.999, 1e-08, 0.01, 10


def _cp(**kw):
    return pltpu.CompilerParams(vmem_limit_bytes=VMEM_LIMIT, **kw)


def _pcall(body, **kw):
    return pl.pallas_call(body, **kw)


def _sds(shape, dtype):
    return jax.ShapeDtypeStruct(shape, dtype)


HBM_SPEC = pl.BlockSpec(memory_space=pl.ANY)


def _mesh_pos():
    return lax.axis_index("x"), lax.axis_index("y"), lax.axis_index("c")


class _AllGather:
    def __init__(self, blocks, forward_at=None):
        self.blocks = list(blocks)
        self.forward_at = forward_at
        n = len(self.blocks)
        self.out_shape = [_sds((N_DEV,) + b.shape, b.dtype) for b in self.blocks]
        self.scratch = [pltpu.SemaphoreType.DMA((n, 7)), pltpu.SemaphoreType.DMA((n, 7)), pltpu.SemaphoreType.DMA((n,))]

    def bind(self, ins, outs, sems):
        n = len(ins)
        send_sems, recv_sems, local_sems = sems
        x, y, c = _mesh_pos()
        me, sibling = (x, y, c), (x, y, 1 - c)
        chips = [(1 - x, y), (x, 1 - y), (1 - x, 1 - y)]

        def slab(t, dev):
            return outs[t].at[4 * dev[0] + 2 * dev[1] + dev[2]]

        def copy(t, k, block, to, src=None):
            dst = slab(t, block)
            return pltpu.make_async_remote_copy(
                src_ref=dst if src is None else src, dst_ref=dst, send_sem=send_sems.at[t, k], recv_sem=recv_sems.at[t, k],
                device_id=to, device_id_type=MESH)

        mine = [pltpu.make_async_copy(ins[t], slab(t, me), local_sems.at[t]) for t in range(n)]
        first = []
        for t in range(n):
            first.append(copy(t, 0, me, sibling, src=ins[t]))
            first += [copy(t, 1 + j, me, (*chip, c), src=ins[t]) for j, chip in enumerate(chips)]
        passed = [copy(t, 4 + j, (*chip, c), sibling) for j, chip in enumerate(chips) for t in range(n)]

        def start():
            for cp in mine + first:
                cp.start()

        def mid():
            for j, chip in enumerate(chips):
                for t in range(n):
                    copy(t, 1 + j, (*chip, c), me).wait_recv()
                    passed[j * n + t].start()

        def finish():
            for t in range(n):
                copy(t, 0, sibling, me).wait_recv()
                for j, chip in enumerate(chips):
                    copy(t, 4 + j, (*chip, 1 - c), me).wait_recv()
            for cp in first + passed:
                cp.wait_send()
            for cp in mine:
                cp.wait()

        return start, mid, finish


class _SiblingSwap:
    def __init__(self, send):
        self.blocks = [send]
        self.out_shape = [_sds(send.shape, send.dtype)]
        self.scratch = [pltpu.SemaphoreType.DMA, pltpu.SemaphoreType.DMA]

    def bind(self, ins, outs, sems):
        x, y, c = _mesh_pos()
        cp = pltpu.make_async_remote_copy(src_ref=ins[0], dst_ref=outs[0], send_sem=sems[0], recv_sem=sems[1],
                                          device_id=(x, y, 1 - c), device_id_type=MESH)
        return cp.start, (lambda: None), cp.wait


class _ChipExchange:
    def __init__(self, part):
        self.blocks = [part]
        self.out_shape = [_sds((3,) + part.shape[1:], part.dtype)]
        self.scratch = [pltpu.SemaphoreType.DMA((3,)), pltpu.SemaphoreType.DMA((3,))]

    def bind(self, ins, outs, sems):
        x, y, c = _mesh_pos()
        chips = [(1 - x, y), (x, 1 - y), (1 - x, 1 - y)]
        cps = [
            pltpu.make_async_remote_copy(src_ref=ins[0].at[2 * cx + cy], dst_ref=outs[0].at[j], send_sem=sems[0].at[j],
                                         recv_sem=sems[1].at[j], device_id=(cx, cy, c), device_id_type=MESH)
            for j, (cx, cy) in enumerate(chips)
        ]

        def start():
            for cp in cps:
                cp.start()

        def finish():
            for cp in cps:
                cp.wait()

        return start, (lambda: None), finish


def _comm_lists(comm):
    if comm is None:
        return [], [], [], []
    n = len(comm.blocks)
    return list(comm.blocks), [HBM_SPEC] * n, list(comm.out_shape), list(comm.scratch)


def _split_refs(refs, n_pre, n_in, n_cin, n_out, n_cout, n_scr):
    cuts = [n_pre, n_in, n_cin, n_out, n_cout, n_scr]
    out, at = [], 0
    for c in cuts:
        out.append(refs[at:at + c])
        at += c
    out.append(refs[at:])
    return out


def _run_comm(comm, cin, cout, csem, step, n_steps, when="start"):
    if comm is None:
        return
    start, mid, finish = comm.bind(cin, cout, csem)
    if when == "start":
        pl.when(step == 0)(start)
    else:
        frac = getattr(comm, "forward_at", None)
        mid_step = max(n_steps - 2, 0) if frac is None else min(int(frac * n_steps), max(n_steps - 2, 0))
        pl.when(step == mid_step)(mid)
        pl.when(step == n_steps - 1)(finish)


def _comm_only(comm, name):
    ins, in_specs, out_shape, scratch = _comm_lists(comm)
    n = len(ins)

    def body(*refs):
        start, mid, finish = comm.bind(refs[:n], refs[n:2 * n], refs[2 * n:])
        start()
        mid()
        finish()

    outs = _pcall(body, name=name, out_shape=tuple(out_shape), in_specs=in_specs, out_specs=tuple([HBM_SPEC] * n),
                  scratch_shapes=scratch, compiler_params=_cp())(*ins)
    return list(outs)


def _all_reduce_small(v):
    rows, cols = v.shape

    def body(v_ref, o_ref, slots, send_sems, recv_sems):
        x, y, c = _mesh_pos()
        me = 4 * x + 2 * y + c
        slots[me] = v_ref[...]
        cps = []
        for k in range(1, N_DEV):
            px, py, pc = x ^ (k >> 2), y ^ ((k >> 1) & 1), c ^ (k & 1)
            cps.append(pltpu.make_async_remote_copy(
                src_ref=v_ref, dst_ref=slots.at[me], send_sem=send_sems.at[k - 1], recv_sem=recv_sems.at[k - 1],
                device_id=(px, py, pc), device_id_type=MESH))
        for cp in cps:
            cp.start()
        for cp in cps:
            cp.wait()
        acc = slots[0]
        for k in range(1, N_DEV):
            acc = acc + slots[k]
        o_ref[...] = acc

    return _pcall(
        body, name="all_reduce_small", out_shape=_sds((rows, cols), F32),
        in_specs=[pl.BlockSpec(memory_space=pltpu.VMEM)], out_specs=pl.BlockSpec(memory_space=pltpu.VMEM),
        scratch_shapes=[pltpu.VMEM((N_DEV, rows, cols), F32), pltpu.SemaphoreType.DMA((7,)), pltpu.SemaphoreType.DMA((7,))],
        compiler_params=_cp(),
    )(v)


def _row_tile(rows, cap=1040):
    return max(tt for tt in range(16, cap + 1, 16) if rows % tt == 0)


def _add2(a, b, *, name):
    nq, rows, cols = a.shape
    tm = _row_tile(rows)

    def body(a_ref, b_ref, o_ref):
        o_ref[...] = (a_ref[...].astype(F32) + b_ref[...].astype(F32)).astype(BF16)

    spec = pl.BlockSpec((1, tm, cols), lambda q, i: (q, i, 0))
    return _pcall(body, name=name, out_shape=_sds(a.shape, BF16), grid=(nq, rows // tm),
                  in_specs=[spec, spec], out_specs=spec, compiler_params=_cp())(a, b)


def _add4(own, recv, *, name, rows_apart=False):
    rows, cols = own.shape
    tm = _row_tile(rows, 256 if rows_apart else 1040)

    def body(o_ref, r_ref, g_ref):
        tot = ((o_ref[...].astype(F32) + r_ref[0].astype(F32)) + r_ref[1].astype(F32)) + r_ref[2].astype(F32)
        if rows_apart:
            g_ref[:, 0, :] = tot
        else:
            g_ref[...] = tot

    if rows_apart:
        out_shape, out_spec = _sds((rows, 1, cols), F32), pl.BlockSpec((tm, 1, cols), lambda i: (i, 0, 0))
    else:
        out_shape, out_spec = _sds((rows, cols), F32), pl.BlockSpec((tm, cols), lambda i: (i, 0))
    return _pcall(
        body, name=name, out_shape=out_shape, grid=(rows // tm,),
        in_specs=[pl.BlockSpec((tm, cols), lambda i: (i, 0)), pl.BlockSpec((3, tm, cols), lambda i: (0, i, 0))],
        out_specs=out_spec, compiler_params=_cp(),
    )(own, recv)


def _mm(a, b, *, m, n, k, name, ta=False, tb=False, b_off=(0, 0), res=None, out_dtype=F32, tm=512, tn=1024, tk=1024,
        comm=None):
    tm, tn, tk = min(tm, m), min(tn, n), min(tk, k)
    assert m % tm == 0 and n % tn == 0 and k % tk == 0, (name, m, n, k, tm, tn, tk)
    nk = k // tk
    grid = (m // tm, n // tn, nk)
    bo0, bo1 = b_off
    if ta:
        a_spec = pl.BlockSpec((tk, tm), lambda i, j, kk: (kk, i))
    else:
        a_spec = pl.BlockSpec((tm, tk), lambda i, j, kk: (i, kk))
    if tb:
        b_spec = pl.BlockSpec((tn, tk), lambda i, j, kk: (j + bo0, kk + bo1))
    else:
        b_spec = pl.BlockSpec((tk, tn), lambda i, j, kk: (kk + bo0, j + bo1))
    o_spec = pl.BlockSpec((tm, tn), lambda i, j, kk: (i, j))
    dn = (((0 if ta else 1,), (1 if tb else 0,)), ((), ()))
    has_res = res is not None
    c_ins, c_in_specs, c_out_shape, c_scratch = _comm_lists(comm)
    n_in = 3 if has_res else 2
    n_scr = 1 if nk > 1 else 0

    def body(*refs):
        _, ins, cin, outs, cout, scr, csem = _split_refs(refs, 0, n_in, len(c_ins), 1, len(c_ins), n_scr)
        a_ref, b_ref = ins[0], ins[1]
        r_ref = ins[2] if has_res else None
        o_ref = outs[0]
        step = (pl.program_id(0) * grid[1] + pl.program_id(1)) * nk + pl.program_id(2)
        _run_comm(comm, cin, cout, csem, step, grid[0] * grid[1] * nk, "start")
        p = lax.dot_general(a_ref[...].astype(BF16), b_ref[...].astype(BF16), dn, preferred_element_type=F32)

        def finish(acc):
            if has_res:
                acc = acc + r_ref[...]
            o_ref[...] = acc.astype(out_dtype)

        if nk == 1:
            finish(p)
        else:
            acc_ref = scr[0]
            kk = pl.program_id(2)

            @pl.when(kk == 0)
            def _():
                acc_ref[...] = p

            @pl.when(kk > 0)
            def _():
                acc_ref[...] += p

            @pl.when(kk == nk - 1)
            def _():
                finish(acc_ref[...])

        _run_comm(comm, cin, cout, csem, step, grid[0] * grid[1] * nk, "end")

    ins = [a, b] + ([res] if has_res else [])
    in_specs = [a_spec, b_spec] + ([o_spec] if has_res else [])
    sem = ("arbitrary",) * 3 if comm is not None else ("parallel", "parallel", "arbitrary")
    outs = _pcall(
        body, name=name, out_shape=tuple([_sds((m, n), out_dtype)] + c_out_shape), grid=grid,
        in_specs=in_specs + c_in_specs, out_specs=tuple([o_spec] + [HBM_SPEC] * len(c_ins)),
        scratch_shapes=([pltpu.VMEM((tm, tn), F32)] if nk > 1 else []) + c_scratch,
        compiler_params=_cp(dimension_semantics=sem),
    )(*(ins + c_ins))
    return outs[0] if comm is None else (outs[0], list(outs[1:]))


SLAB_WIN = 1280
SLAB_TAIL = 48


def _chip_order(x, y, c):
    north = c == 1
    first = (jnp.where(north, 1 - x, x), jnp.where(north, y, 1 - y))
    second = (jnp.where(north, x, 1 - x), jnp.where(north, 1 - y, y))
    return [first, second, (1 - x, 1 - y)]


def _z_gathered(h1, own_slab, order, *, t, d, n_pad, rows, comm=None, tm=1024):
    rp = own_slab.shape[0]
    n_slab = N_DEV
    tm = min(tm, t)
    ni = t // tm
    sh = rows - SLAB_WIN
    assert sh >= 0 and (n_slab - 1) * sh <= 128
    c_ins, c_in_specs, c_out_shape, c_scratch = _comm_lists(comm)
    n_steps = n_slab * ni

    def body(*refs):
        pre, ins, cin, outs, cout, scr, csem = _split_refs(refs, 1, 2, len(c_ins), 2, len(c_ins), 5)
        order_ref = pre[0]
        h_ref, own_ref = ins
        o_ref, slabs_ref = outs
        slab_vm, send_sems, recv_sems, local_sem, load_sem = scr
        p, i = pl.program_id(0), pl.program_id(1)
        step = p * ni + i
        x, y, c = _mesh_pos()
        me, sibling = (x, y, c), (x, y, 1 - c)
        chips = _chip_order(x, y, c)

        def slab(dev):
            return slabs_ref.at[4 * dev[0] + 2 * dev[1] + dev[2]]

        def copy(k, block, to, src=None, part=None):
            dst = slab(block) if part is None else slab(block).at[part]
            return pltpu.make_async_remote_copy(
                src_ref=dst if src is None else src, dst_ref=dst, send_sem=send_sems.at[k], recv_sem=recv_sems.at[k],
                device_id=to, device_id_type=MESH)

        nb_1, nb_2, diag = ((*chip, c) for chip in chips)
        cut = (rp // 32) * 16
        half_a, half_b = pl.ds(0, cut), pl.ds(cut, rp - cut)
        mine = pltpu.make_async_copy(own_ref, slab(me), local_sem)
        first = [copy(0, me, sibling, src=own_ref), copy(1, me, nb_1, src=own_ref), copy(2, me, nb_2, src=own_ref)]
        relays = [copy(3, nb_1, nb_2, part=half_a), copy(4, nb_2, nb_1, part=half_b)]
        axis_of = [jnp.where(c == 1, 0, 1), jnp.where(c == 1, 1, 0), 2]
        passed = [copy(5 + axis_of[j], (*chip, c), sibling) for j, chip in enumerate(chips)]
        later = min(n_steps // 5, n_steps - 1)

        @pl.when(step == 0)
        def _():
            mine.start()
            first[0].start()
            first[1].start()

        @pl.when(step == later)
        def _():
            first[2].start()

        _run_comm(comm, cin, cout, csem, step, n_steps, "start")

        def load(src):
            cp = pltpu.make_async_copy(src, slab_vm, load_sem)
            cp.start()
            cp.wait()

        @pl.when(i == 0)
        def _():
            @pl.when(p == 0)
            def _():
                load(own_ref)

            @pl.when(p == 1)
            def _():
                copy(0, sibling, me).wait_recv()
                load(slab(sibling))

            for j, chip in enumerate(chips):
                @pl.when(p == 2 + 2 * j)
                def _():
                    if j < 2:
                        copy(1 + j, (*chip, c), me).wait_recv()
                        relays[j].start()
                    else:
                        copy(3, diag, me, part=half_a).wait_recv()
                        copy(4, diag, me, part=half_b).wait_recv()
                    passed[j].start()
                    load(slab((*chip, c)))

                other = (1, 0, 2)[j]

                @pl.when(p == 3 + 2 * j)
                def _():
                    copy(5 + axis_of[other], (*chips[other], 1 - c), me).wait_recv()
                    load(slab((*chips[other], 1 - c)))

        k = order_ref[p]
        y_k = lax.dot_general(h_ref[...], slab_vm[...], (((1,), (1,)), ((), ())), preferred_element_type=F32)
        o_ref[...] = pltpu.roll(y_k[:, 0:SLAB_WIN], k * sh, axis=1).astype(BF16)

        @pl.when(step == n_steps - 1)
        def _():
            for cp in first + relays + passed:
                cp.wait_send()
            mine.wait()

        _run_comm(comm, cin, cout, csem, step, n_steps, "end")

    outs = _pcall(
        body, name="mm_z", out_shape=tuple([_sds((t, n_pad), BF16), _sds((n_slab, rp, d), BF16)] + c_out_shape),
        grid_spec=pltpu.PrefetchScalarGridSpec(
            num_scalar_prefetch=1, grid=(n_slab, ni),
            in_specs=[pl.BlockSpec((tm, d), lambda p, i, order: (i, 0)), HBM_SPEC] + c_in_specs,
            out_specs=tuple([pl.BlockSpec((tm, SLAB_WIN), lambda p, i, order: (i, order[p])), HBM_SPEC] + [HBM_SPEC] * len(c_ins)),
            scratch_shapes=[pltpu.VMEM((rp, d), BF16), pltpu.SemaphoreType.DMA((8,)), pltpu.SemaphoreType.DMA((8,)),
                            pltpu.SemaphoreType.DMA, pltpu.SemaphoreType.DMA] + c_scratch),
        compiler_params=_cp(dimension_semantics=("arbitrary", "arbitrary")),
    )(order, h1, own_slab, *c_ins)
    return outs[0], outs[1], list(outs[2:])


def _z_fix(h1, slabs, z, *, t, d, rows, tm=4096):
    n_slab, rp, _ = slabs.shape
    tm = min(tm, t)
    sh = rows - SLAB_WIN
    tail_lo = rp - SLAB_TAIL
    assert rows - (n_slab - 1) * sh >= tail_lo and rp % SLAB_TAIL == 0

    def body(h_ref, t_ref, z_ref, o_ref):
        k = pl.program_id(0) + 1
        p = lax.dot_general(h_ref[...], t_ref[0], (((1,), (1,)), ((), ())), preferred_element_type=F32)
        pp = jnp.concatenate([p, jnp.zeros((tm, 128 - SLAB_TAIL), F32)], axis=1)
        pr = pltpu.roll(pp, 128 - (rows - tail_lo) + k * sh, axis=1)
        lane = lax.broadcasted_iota(jnp.int32, (tm, 128), 1)
        o_ref[...] = jnp.where(lane < k * sh, pr.astype(BF16), z_ref[...])

    blk = pl.BlockSpec((tm, 128), lambda b, i: (i, (SLAB_WIN // 128) * (b + 1)))
    return _pcall(
        body, name="mm_z_fix", out_shape=_sds(z.shape, z.dtype), grid=(n_slab - 1, t // tm),
        in_specs=[pl.BlockSpec((tm, d), lambda b, i: (i, 0)),
                  pl.BlockSpec((1, SLAB_TAIL, d), lambda b, i: (b, tail_lo // SLAB_TAIL, 0)), blk],
        out_specs=blk, input_output_aliases={2: 0}, compiler_params=_cp(),
    )(h1, slabs, z)


def _slab_window(a1_ref, a2_ref, k, sh):
    w = jnp.concatenate([a1_ref[...], a2_ref[...]], axis=1)
    width = SLAB_WIN + 128
    return pltpu.roll(w, width - k * sh, axis=1)


def _g_win_slabs(dz, h1, my_c, *, t, d, n_slab, rows, rp, tk=2048):
    sh = rows - SLAB_WIN
    tk = min(tk, t)
    nk = t // tk
    width = SLAB_WIN + 128
    half = n_slab // 2

    def slab_of(p, c):
        return 2 * (p % half) + jnp.where(p < half, 1 - c, c)

    def body(c_ref, a1_ref, a2_ref, h_ref, keep_ref, land_ref, acc, stage, send_sems, recv_sems):
        p, kk = pl.program_id(0), pl.program_id(1)
        x, y, c = _mesh_pos()
        ws = _slab_window(a1_ref, a2_ref, slab_of(p, c), sh)
        prod = lax.dot_general(ws, h_ref[...], (((0,), (0,)), ((), ())), preferred_element_type=F32)

        def push(q):
            return pltpu.make_async_remote_copy(
                src_ref=stage.at[q % 2], dst_ref=land_ref.at[q], send_sem=send_sems.at[q], recv_sem=recv_sems.at[q],
                device_id=(x, y, 1 - c), device_id_type=MESH)

        @pl.when(kk == 0)
        def _():
            acc[...] = prod

        @pl.when(kk > 0)
        def _():
            acc[...] += prod

        @pl.when(kk == nk - 1)
        def _():
            res = acc[0:rp, :].astype(BF16)
            for q in range(half):
                @pl.when(p == q)
                def _():
                    if q >= 2:
                        push(q - 2).wait_send()
                    stage[q % 2] = res
                    push(q).start()

            @pl.when(p >= half)
            def _():
                keep_ref[0] = res

        @pl.when((p == n_slab - 1) & (kk == nk - 1))
        def _():
            for q in range(max(half - 2, 0), half):
                push(q).wait_send()
            for q in range(half):
                push(q).wait_recv()

    outs = _pcall(
        body, name="mm_g_win", out_shape=(_sds((half, rp, d), BF16), _sds((half, rp, d), BF16)),
        grid_spec=pltpu.PrefetchScalarGridSpec(
            num_scalar_prefetch=1, grid=(n_slab, nk),
            in_specs=[pl.BlockSpec((tk, SLAB_WIN), lambda p, kk, c: (kk, slab_of(p, c[0]))),
                      pl.BlockSpec((tk, 128), lambda p, kk, c: (kk, (SLAB_WIN // 128) * (slab_of(p, c[0]) + 1))),
                      pl.BlockSpec((tk, d), lambda p, kk, c: (kk, 0))],
            out_specs=(pl.BlockSpec((1, rp, d), lambda p, kk, c: (jnp.maximum(p - half, 0), 0, 0)), HBM_SPEC),
            scratch_shapes=[pltpu.VMEM((width, d), F32), pltpu.VMEM((2, rp, d), BF16),
                            pltpu.SemaphoreType.DMA((half,)), pltpu.SemaphoreType.DMA((half,))]),
        compiler_params=_cp(dimension_semantics=("arbitrary", "arbitrary")),
    )(my_c.reshape(1).astype(jnp.int32), dz, dz, h1)
    return outs[0], outs[1]


def _dh1_from_slabs(dz, slabs, *, t, d, rows, comm=None, tm=1024):
    n_slab, rp, _ = slabs.shape
    sh = rows - SLAB_WIN
    width = SLAB_WIN + 128
    c_ins, c_in_specs, c_out_shape, c_scratch = _comm_lists(comm)
    grid = (t // tm, n_slab)

    def body(*refs):
        _, ins, cin, outs, cout, scr, csem = _split_refs(refs, 0, 3, len(c_ins), 1, len(c_ins), 1)
        a1_ref, a2_ref, s_ref = ins
        o_ref, acc = outs[0], scr[0]
        k = pl.program_id(1)
        step = pl.program_id(0) * n_slab + k
        _run_comm(comm, cin, cout, csem, step, grid[0] * n_slab, "start")
        ws = _slab_window(a1_ref, a2_ref, k, sh)
        s_end = jnp.concatenate([s_ref[0, SLAB_WIN:rp, :], jnp.zeros((width - rp, d), BF16)], axis=0)
        p = (jnp.dot(ws[:, 0:SLAB_WIN], s_ref[0, 0:SLAB_WIN, :], preferred_element_type=F32)
             + jnp.dot(ws[:, SLAB_WIN:width], s_end, preferred_element_type=F32))

        @pl.when(k == 0)
        def _():
            acc[...] = p

        @pl.when(k > 0)
        def _():
            acc[...] += p

        @pl.when(k == n_slab - 1)
        def _():
            o_ref[...] = acc[...]

        _run_comm(comm, cin, cout, csem, step, grid[0] * n_slab, "end")

    outs = _pcall(
        body, name="mm_dh1", out_shape=tuple([_sds((t, d), F32)] + c_out_shape), grid=grid,
        in_specs=[pl.BlockSpec((tm, SLAB_WIN), lambda i, k: (i, k)),
                  pl.BlockSpec((tm, 128), lambda i, k: (i, (SLAB_WIN // 128) * (k + 1))),
                  pl.BlockSpec((1, rp, d), lambda i, k: (k, 0, 0))] + c_in_specs,
        out_specs=tuple([pl.BlockSpec((tm, d), lambda i, k: (i, 0))] + [HBM_SPEC] * len(c_ins)),
        scratch_shapes=[pltpu.VMEM((tm, d), F32)] + c_scratch,
        compiler_params=_cp(dimension_semantics=("arbitrary", "arbitrary")),
    )(dz, dz, slabs, *c_ins)
    return outs[0], list(outs[1:])


def _rms_fwd(x, g, *, name, tm=512):
    t, d = x.shape
    tm = min(tm, t)

    def body(x_ref, g_ref, h_ref):
        xv = x_ref[...]
        r = lax.rsqrt(jnp.mean(xv * xv, axis=-1, keepdims=True) + EPS)
        h_ref[...] = (xv * r * g_ref[...]).astype(BF16)

    return _pcall(
        body, name=name, out_shape=_sds((t, d), BF16), grid=(t // tm,),
        in_specs=[pl.BlockSpec((tm, d), lambda i: (i, 0)), pl.BlockSpec((1, d), lambda i: (0, 0))],
        out_specs=pl.BlockSpec((tm, d), lambda i: (i, 0)), compiler_params=_cp(),
    )(x, g)


def _rms_bwd(dh, x, g, res, *, name, tm=512):
    t, d = x.shape
    tm = min(tm, t)
    has_res = res is not None

    def body(*refs):
        dh_ref, x_ref, g_ref = refs[:3]
        r_ref = refs[3] if has_res else None
        dx_ref, dg_ref = refs[-2], refs[-1]
        xv = x_ref[...]
        r = lax.rsqrt(jnp.mean(xv * xv, axis=-1, keepdims=True) + EPS)
        xh = xv * r
        dhv = dh_ref[...]
        gd = dhv * g_ref[...]
        dx = r * (gd - xh * jnp.mean(gd * xh, axis=-1, keepdims=True))
        if has_res:
            dx = dx + r_ref[...]
        dx_ref[...] = dx

        @pl.when(pl.program_id(0) == 0)
        def _():
            dg_ref[...] = jnp.zeros_like(dg_ref)

        dg_ref[...] += jnp.sum(dhv * xh, axis=0, keepdims=True)

    row = pl.BlockSpec((tm, d), lambda i: (i, 0))
    vec = pl.BlockSpec((1, d), lambda i: (0, 0))
    return _pcall(
        body, name=name, out_shape=(_sds((t, d), F32), _sds((1, d), F32)), grid=(t // tm,),
        in_specs=[row, row, vec] + ([row] if has_res else []), out_specs=(row, vec),
        compiler_params=_cp(dimension_semantics=("arbitrary",)),
    )(*([dh, x, g] + ([res] if has_res else [])))


def _conv_fwd(z, conv_w, conv_b, *, nb, s, d, tc=256):
    nc = d // tc

    def body(cb_ref, cc_ref, cv_ref, w_ref, b_ref, y_ref):
        u = cc_ref[...].astype(F32) * cv_ref[...].astype(F32)
        row = lax.broadcasted_iota(jnp.int32, u.shape, 0)
        u1 = jnp.where(row >= 1, pltpu.roll(u, 1, axis=0), 0.0)
        u2 = jnp.where(row >= 2, pltpu.roll(u, 2, axis=0), 0.0)
        w = w_ref[...]
        y = w[2:3, :] * u + w[1:2, :] * u1 + w[0:1, :] * u2 + b_ref[...]
        y_ref[...] = (cb_ref[...].astype(F32) * y).astype(BF16)

    def part(p):
        return pl.BlockSpec((s, tc), lambda b, j: (b, p * nc + j))

    return _pcall(
        body, name="conv_fwd", out_shape=_sds((nb * s, d), BF16), grid=(nb, nc),
        in_specs=[part(0), part(1), part(2), pl.BlockSpec((3, tc), lambda b, j: (0, j)), pl.BlockSpec((1, tc), lambda b, j: (0, j))],
        out_specs=pl.BlockSpec((s, tc), lambda b, j: (b, j)), compiler_params=_cp(),
    )(z, z, z, conv_w, conv_b)


def _conv_bwd(z, dy, conv_w, conv_b, dz, *, nb, s, d, tc=256, comm=None):
    nc = d // tc
    c_ins, c_in_specs, c_out_shape, c_scratch = _comm_lists(comm)
    n_steps = nc * nb * 3

    def body(*refs):
        _, ins, cin, outs, cout, scr, csem = _split_refs(refs, 0, 7, len(c_ins), 3, len(c_ins), 1)
        cb_ref, cc_ref, cv_ref, dy_ref, w_ref, b_ref, _ = ins
        dz_ref, dw_ref, db_ref = outs
        stash = scr[0]
        b_i, p = pl.program_id(1), pl.program_id(2)
        step = (pl.program_id(0) * nb + b_i) * 3 + p
        _run_comm(comm, cin, cout, csem, step, n_steps, "start")

        @pl.when(p == 0)
        def _():
            cc, cv = cc_ref[...].astype(F32), cv_ref[...].astype(F32)
            u = cc * cv
            n = u.shape[0]
            row = lax.broadcasted_iota(jnp.int32, u.shape, 0)
            u1 = jnp.where(row >= 1, pltpu.roll(u, 1, axis=0), 0.0)
            u2 = jnp.where(row >= 2, pltpu.roll(u, 2, axis=0), 0.0)
            w = w_ref[...]
            y = w[2:3, :] * u + w[1:2, :] * u1 + w[0:1, :] * u2 + b_ref[...]
            dyv = dy_ref[...]
            dconv = dyv * cb_ref[...].astype(F32)
            g1 = jnp.where(row < n - 1, pltpu.roll(dconv, n - 1, axis=0), 0.0)
            g2 = jnp.where(row < n - 2, pltpu.roll(dconv, n - 2, axis=0), 0.0)
            du = w[2:3, :] * dconv + w[1:2, :] * g1 + w[0:1, :] * g2
            stash[0] = (dyv * y).astype(BF16)
            stash[1] = (du * cv).astype(BF16)
            stash[2] = (du * cc).astype(BF16)
            dws = [jnp.sum(dconv * u2, axis=0, keepdims=True), jnp.sum(dconv * u1, axis=0, keepdims=True),
                   jnp.sum(dconv * u, axis=0, keepdims=True)]
            db = jnp.sum(dconv, axis=0, keepdims=True)

            @pl.when(b_i == 0)
            def _():
                for r in range(3):
                    dw_ref[r:r + 1, :] = dws[r]
                db_ref[...] = db

            @pl.when(b_i > 0)
            def _():
                for r in range(3):
                    dw_ref[r:r + 1, :] += dws[r]
                db_ref[...] += db

        dz_ref[...] = stash[p]
        _run_comm(comm, cin, cout, csem, step, n_steps, "end")

    outs = _pcall(
        body, name="conv_bwd",
        out_shape=tuple([_sds(dz.shape, dz.dtype), _sds((3, d), F32), _sds((1, d), F32)] + c_out_shape), grid=(nc, nb, 3),
        in_specs=[
            pl.BlockSpec((s, tc), lambda j, b, p: (b, j)), pl.BlockSpec((s, tc), lambda j, b, p: (b, nc + j)),
            pl.BlockSpec((s, tc), lambda j, b, p: (b, 2 * nc + j)), pl.BlockSpec((s, tc), lambda j, b, p: (b, j)),
            pl.BlockSpec((3, tc), lambda j, b, p: (0, j)), pl.BlockSpec((1, tc), lambda j, b, p: (0, j)),
            HBM_SPEC,
        ] + c_in_specs,
        out_specs=tuple([pl.BlockSpec((s, tc), lambda j, b, p: (b, p * nc + j)),
                         pl.BlockSpec((3, tc), lambda j, b, p: (0, j)), pl.BlockSpec((1, tc), lambda j, b, p: (0, j))]
                        + [HBM_SPEC] * len(c_ins)),
        scratch_shapes=[pltpu.VMEM((3, s, tc), BF16)] + c_scratch,
        input_output_aliases={6: 0},
        compiler_params=_cp(dimension_semantics=("arbitrary", "arbitrary", "arbitrary")),
    )(z, z, z, dy, conv_w, conv_b, dz, *c_ins)
    return outs[0], outs[1], outs[2], list(outs[3:])


def _head_rms(xv, g):
    r = lax.rsqrt(jnp.mean(xv * xv, axis=-1, keepdims=True) + EPS)
    return xv * r * g


def _head_rms_bwd(xv, g, dy):
    r = lax.rsqrt(jnp.mean(xv * xv, axis=-1, keepdims=True) + EPS)
    xh = xv * r
    gd = dy * g
    dx = r * (gd - xh * jnp.mean(gd * xh, axis=-1, keepdims=True))
    return dx, jnp.sum(dy * xh, axis=0, keepdims=True)


def _fox_prep(z, q_g, k_g, *, t, d, tm=256):
    nh = d // HD

    def body(z_ref, qg_ref, kg_ref, q_ref, k_ref, v_ref):
        qg, kg = qg_ref[...], kg_ref[...]
        for h in range(nh):
            c0 = h * HD
            q_ref[:, c0:c0 + HD] = _head_rms(z_ref[:, c0:c0 + HD].astype(F32), qg).astype(BF16)
            k_ref[:, c0:c0 + HD] = _head_rms(z_ref[:, d + c0:d + c0 + HD].astype(F32), kg).astype(BF16)
        v_ref[...] = z_ref[:, 2 * d:3 * d]

    o = pl.BlockSpec((tm, d), lambda i: (i, 0))
    g = pl.BlockSpec((1, HD), lambda i: (0, 0))
    return _pcall(
        body, name="fox_prep", out_shape=(_sds((t, d), BF16),) * 3, grid=(t // tm,),
        in_specs=[pl.BlockSpec((tm, 3 * d), lambda i: (i, 1)), g, g], out_specs=(o, o, o), compiler_params=_cp(),
    )(z, q_g, k_g)


def _lane_cumsum(v, reverse=False):
    n = v.shape[1]
    lane = lax.broadcasted_iota(jnp.int32, v.shape, 1)
    sh = 1
    while sh < n:
        if reverse:
            v = v + jnp.where(lane < n - sh, pltpu.roll(v, n - sh, axis=1), 0.0)
        else:
            v = v + jnp.where(lane >= sh, pltpu.roll(v, sh, axis=1), 0.0)
        sh *= 2
    return v


def _fox_gates(z, f_bias_col, *, nb, s, nh):
    def body(z_ref, b_ref, c_ref):
        ff = z_ref[...].T[0:nh, :] + b_ref[...]
        lf = jnp.minimum(ff, 0.0) - jnp.log(1.0 + jnp.exp(-jnp.abs(ff)))
        c_ref[0] = _lane_cumsum(lf) * SQRT_HD

    return _pcall(
        body, name="fox_gates", out_shape=_sds((nb, nh, s), F32), grid=(nb,),
        in_specs=[pl.BlockSpec((s, HD), lambda b: (b, 0)), pl.BlockSpec((nh, 1), lambda b: (0, 0))],
        out_specs=pl.BlockSpec((1, nh, s), lambda b: (b, 0, 0)), compiler_params=_cp(),
    )(z, f_bias_col)


def _fox_gates_bwd(z, f_bias_col, row_sums, col_sums, dz, *, nb, s, nh, ff_block):
    def body(z_ref, b_ref, rs_ref, cs_ref, dz_in, dz_ref, db_ref, dc_sc):
        b_i, h = pl.program_id(0), pl.program_id(1)
        dc_sc[pl.ds(h, 1), :] = (rs_ref[...] - cs_ref[...]).T[0:1, :]

        @pl.when(h == nh - 1)
        def _():
            ff = z_ref[...].T[0:nh, :] + b_ref[...]
            dlf = _lane_cumsum(dc_sc[...], reverse=True)
            dff = dlf * (1.0 / (1.0 + jnp.exp(ff)))
            pad = jnp.concatenate([dff, jnp.zeros((HD - nh, s), F32)], axis=0)
            dz_ref[...] = pad.T.astype(BF16)
            dbv = jnp.sum(dff, axis=1, keepdims=True)

            @pl.when(b_i == 0)
            def _():
                db_ref[...] = dbv

            @pl.when(b_i > 0)
            def _():
                db_ref[...] += dbv

    hs = pl.BlockSpec((s, HD), lambda b, h: (b, h))
    return _pcall(
        body, name="fox_gates_bwd", out_shape=(_sds(dz.shape, dz.dtype), _sds((nh, 1), F32)), grid=(nb, nh),
        in_specs=[pl.BlockSpec((s, HD), lambda b, h: (b, 0)), pl.BlockSpec((nh, 1), lambda b, h: (0, 0)), hs, hs, HBM_SPEC],
        out_specs=(pl.BlockSpec((s, HD), lambda b, h: (b, ff_block)), pl.BlockSpec((nh, 1), lambda b, h: (0, 0))),
        scratch_shapes=[pltpu.VMEM((nh, s), F32)],
        input_output_aliases={4: 0}, compiler_params=_cp(dimension_semantics=("arbitrary", "arbitrary")),
    )(z, f_bias_col, row_sums, col_sums, dz)


def _fox_pairs(nq, key_major):
    if key_major:
        pairs = [(i, j) for j in range(nq) for i in range(j, nq)]
    else:
        pairs = [(i, j) for i in range(nq) for j in range(i + 1)]
    return jnp.array([p[0] for p in pairs], jnp.int32), jnp.array([p[1] for p in pairs], jnp.int32)


def _with_ones(x):
    return jnp.concatenate([x, jnp.ones_like(x)], axis=1)


def _fox_specs(nb, nh, nq, tq, tk, hp):
    qs = pl.BlockSpec((tq, hp * HD), lambda b, h, p, *tb: (b * nq + tb[0][p], h))
    ks = pl.BlockSpec((tk, hp * HD), lambda b, h, p, *tb: (b * nq + tb[1][p], h))
    cs = pl.BlockSpec((hp, 1, tk), lambda b, h, p, *tb: (b * (nh // hp) + h, 0, tb[1][p]))
    return qs, ks, cs


def _fox_raw(qv, kv, ckv, diag, tq, tk):
    sc = lax.dot_general(qv, kv, (((1,), (1,)), ((), ())), preferred_element_type=F32) - ckv
    if diag:
        rows = lax.broadcasted_iota(jnp.int32, (tq, tk), 0)
        cols = lax.broadcasted_iota(jnp.int32, (tq, tk), 1)
        sc = jnp.where(rows >= cols, sc, NEG)
    return sc


def _fox_probs(qv, kv, ckv, lse_col, diag, tq, tk):
    sc = _fox_raw(qv, kv, ckv, diag, tq, tk)
    return jnp.exp2(sc * EXP2_C - lse_col * LOG2E)


def _fox_call(body, *, name, ins, in_specs, out_shape, out_specs, scratch, grid, tables, comm):
    c_ins, c_in_specs, c_out_shape, c_scratch = _comm_lists(comm)
    n_in, n_out, n_scr, nc = len(ins), len(out_shape), len(scratch), len(c_ins)
    n_steps = grid[0] * grid[1] * grid[2]

    def wrapped(*refs):
        pre, rin, cin, rout, cout, scr, csem = _split_refs(refs, len(tables), n_in, nc, n_out, nc, n_scr)
        step = (pl.program_id(0) * grid[1] + pl.program_id(1)) * grid[2] + pl.program_id(2)
        _run_comm(comm, cin, cout, csem, step, n_steps, "start")
        body(*pre, *rin, *rout, *scr)
        _run_comm(comm, cin, cout, csem, step, n_steps, "end")

    sem = ("arbitrary",) * 3 if comm is not None else ("parallel", "parallel", "arbitrary")
    outs = _pcall(
        wrapped, name=name, out_shape=tuple(list(out_shape) + c_out_shape),
        grid_spec=pltpu.PrefetchScalarGridSpec(
            num_scalar_prefetch=len(tables), grid=grid, in_specs=list(in_specs) + c_in_specs,
            out_specs=tuple(list(out_specs) + [HBM_SPEC] * nc), scratch_shapes=list(scratch) + c_scratch),
        compiler_params=_cp(dimension_semantics=sem),
    )(*tables, *ins, *c_ins)
    return list(outs[:n_out]), list(outs[n_out:])


def _fox_fwd(q, k, v, cks, *, nb, s, nh, tq=512, comm=None):
    tk = tq
    nq = s // tq
    t = nb * s
    hp = FOX_HP
    it, jt = _fox_pairs(nq, key_major=False)

    def body(it_ref, jt_ref, q_ref, k_ref, v_ref, c_ref, o_ref, lse_ref, m_sc, acc_sc):
        p_id = pl.program_id(2)
        i, j = it_ref[p_id], jt_ref[p_id]

        @pl.when(j == 0)
        def _():
            m_sc[...] = jnp.full_like(m_sc, NEG)
            acc_sc[...] = jnp.zeros_like(acc_sc)

        def step(diag):
            for hh in range(hp):
                cols = slice(hh * HD, (hh + 1) * HD)
                sc = _fox_raw(q_ref[:, cols], k_ref[:, cols], c_ref[hh], diag, tq, tk)
                m_old = m_sc[hh]
                m_new = jnp.maximum(m_old, jnp.max(sc, axis=-1, keepdims=True))
                alpha = jnp.exp2((m_old - m_new) * EXP2_C)
                p = jnp.exp2((sc - m_new) * EXP2_C).astype(BF16)
                acc = alpha * acc_sc[hh] + jnp.dot(p, _with_ones(v_ref[:, cols]), preferred_element_type=F32)
                if diag:
                    l = acc[:, HD:2 * HD]
                    o_ref[:, cols] = (acc[:, 0:HD] / l).astype(BF16)
                    lse_ref[:, cols] = m_new * (1.0 / SQRT_HD) + jnp.log(l)
                else:
                    acc_sc[hh] = acc
                    m_sc[hh] = m_new

        @pl.when(j < i)
        def _():
            step(False)

        @pl.when(j == i)
        def _():
            step(True)

    qs, ks, cs = _fox_specs(nb, nh, nq, tq, tk, hp)
    (o, lse), c_out = _fox_call(
        body, name="fox_fwd", ins=[q, k, v, cks], in_specs=[qs, ks, ks, cs],
        out_shape=[_sds((t, nh * HD), BF16), _sds((t, nh * HD), F32)], out_specs=[qs, qs],
        scratch=[pltpu.VMEM((hp, tq, 1), F32), pltpu.VMEM((hp, tq, 2 * HD), F32)],
        grid=(nb, nh // hp, int(it.shape[0])), tables=(it, jt), comm=comm)
    return o, lse, c_out


def _fox_bwd(q, k, v, cks, do, o, lse, *, nb, s, nh, tq=512, comm=None):
    tk = tq
    nq = s // tq
    t = nb * s
    hp = FOX_HP
    it, jt = _fox_pairs(nq, key_major=True)
    done = jnp.array([j for j in range(nq) for _ in range(j, nq)], jnp.int32)

    def body(it_ref, jt_ref, done_ref, q_ref, k_ref, v_ref, c_ref, do_ref, o_ref, lse_ref,
             dq_ref, rs_ref, dk_ref, dv_ref, cs_ref, dq_sc, dk_sc, dv_sc):
        p_id = pl.program_id(2)
        i, j = it_ref[p_id], jt_ref[p_id]
        rows_i = pl.ds(pl.multiple_of(i * tq, tq), tq)

        def step(diag):
            for hh in range(hp):
                cols = slice(hh * HD, (hh + 1) * HD)
                qv, kv = q_ref[:, cols], k_ref[:, cols]
                p = _fox_probs(qv, kv, c_ref[hh], lse_ref[:, hh * HD:hh * HD + 1], diag, tq, tk)
                dov = do_ref[:, cols]
                dob = dov.astype(BF16)
                dvp = lax.dot_general(p.astype(BF16), dob, (((0,), (0,)), ((), ())), preferred_element_type=F32)
                dp = lax.dot_general(dob, v_ref[:, cols], (((1,), (1,)), ((), ())), preferred_element_type=F32)
                delta = jnp.sum(dov * o_ref[:, cols].astype(F32), axis=-1, keepdims=True)
                ds = (p * (dp - delta)).astype(BF16)
                dkp = lax.dot_general(ds, _with_ones(qv), (((0,), (0,)), ((), ())), preferred_element_type=F32)
                dqp = jnp.dot(ds, _with_ones(kv), preferred_element_type=F32)
                if diag:
                    dk_sc[hh] = dkp
                    dv_sc[hh] = dvp
                else:
                    dk_sc[hh] += dkp
                    dv_sc[hh] += dvp

                @pl.when(j == 0)
                def _():
                    dq_sc[hh, rows_i, :] = dqp

                @pl.when(j > 0)
                def _():
                    dq_sc[hh, rows_i, :] += dqp

                if diag:
                    acc = dq_sc[hh, rows_i, :]
                    dq_ref[:, cols] = acc[:, 0:HD] * (1.0 / SQRT_HD)
                    rs_ref[:, cols] = acc[:, HD:2 * HD]

        @pl.when(i == j)
        def _():
            step(True)

        @pl.when(i > j)
        def _():
            step(False)

        @pl.when(i == nq - 1)
        def _():
            for hh in range(hp):
                cols = slice(hh * HD, (hh + 1) * HD)
                dk_ref[:, cols] = dk_sc[hh, :, 0:HD] * (1.0 / SQRT_HD)
                cs_ref[:, cols] = dk_sc[hh, :, HD:2 * HD]
                dv_ref[:, cols] = dv_sc[hh].astype(BF16)

    qs, ks, cs = _fox_specs(nb, nh, nq, tq, tk, hp)
    ds_spec = pl.BlockSpec((tq, hp * HD), lambda b, h, p, *tb: (b * nq + tb[2][p], h))
    (dq, rs, dk, dv, csum), c_out = _fox_call(
        body, name="fox_bwd", ins=[q, k, v, cks, do, o, lse], in_specs=[qs, ks, ks, cs, qs, qs, qs],
        out_shape=[_sds((t, nh * HD), F32), _sds((t, nh * HD), F32), _sds((t, nh * HD), F32), _sds((t, nh * HD), BF16),
                   _sds((t, nh * HD), F32)],
        out_specs=[ds_spec, ds_spec, ks, ks, ks],
        scratch=[pltpu.VMEM((hp, s, 2 * HD), F32), pltpu.VMEM((hp, tk, 2 * HD), F32), pltpu.VMEM((hp, tk, HD), F32)],
        grid=(nb, nh // hp, int(it.shape[0])), tables=(it, jt, done), comm=comm)
    return dq, rs, dk, dv, csum, c_out


def _fox_post(z, dqn, dkn, dv, q_g, k_g, dz, *, t, d, tm=256):
    nh = d // HD

    def body(z_ref, dq_ref, dk_ref, dv_ref, qg_ref, kg_ref, dz_in, dz_ref, dqg_ref, dkg_ref):
        qg, kg = qg_ref[...], kg_ref[...]
        aq = jnp.zeros((1, HD), F32)
        ak = jnp.zeros((1, HD), F32)
        for h in range(nh):
            c0 = h * HD
            dx, sg = _head_rms_bwd(z_ref[:, c0:c0 + HD].astype(F32), qg, dq_ref[:, c0:c0 + HD])
            dz_ref[:, c0:c0 + HD] = dx.astype(BF16)
            aq = aq + sg
            dx, sg = _head_rms_bwd(z_ref[:, d + c0:d + c0 + HD].astype(F32), kg, dk_ref[:, c0:c0 + HD])
            dz_ref[:, d + c0:d + c0 + HD] = dx.astype(BF16)
            ak = ak + sg
        dz_ref[:, 2 * d:3 * d] = dv_ref[...]

        @pl.when(pl.program_id(0) == 0)
        def _():
            dqg_ref[...] = aq
            dkg_ref[...] = ak

        @pl.when(pl.program_id(0) > 0)
        def _():
            dqg_ref[...] += aq
            dkg_ref[...] += ak

    o = pl.BlockSpec((tm, d), lambda i: (i, 0))
    g = pl.BlockSpec((1, HD), lambda i: (0, 0))
    z3 = pl.BlockSpec((tm, 3 * d), lambda i: (i, 1))
    return _pcall(
        body, name="fox_post", out_shape=(_sds(dz.shape, dz.dtype), _sds((1, HD), F32), _sds((1, HD), F32)), grid=(t // tm,),
        in_specs=[z3, o, o, o, g, g, HBM_SPEC], out_specs=(z3, g, g),
        input_output_aliases={6: 0}, compiler_params=_cp(dimension_semantics=("arbitrary",)),
    )(z, dqn, dkn, dv, q_g, k_g, dz)


def _xa_prep_kv(kv, k_g, *, rows, d):
    xd = d // XH

    def body(kv_ref, g_ref, k_ref, v_ref):
        g = g_ref[...]
        for h in range(XH):
            c0 = h * xd
            k_ref[:, c0:c0 + xd] = _head_rms(kv_ref[:, c0:c0 + xd], g).astype(BF16)
        v_ref[...] = kv_ref[:, d:2 * d].astype(BF16)

    return _pcall(body, name="xa_prep_kv", out_shape=(_sds((rows, d), BF16),) * 2, compiler_params=_cp())(kv, k_g)


def _xa_fwd(z, kn, vb, q_g, *, nb, s, d, nm, q_block, tq=512):
    xd = d // XH
    nq = s // tq

    def body(z_ref, k_ref, v_ref, g_ref, y_ref):
        g = g_ref[...]
        for h in range(XH):
            c0 = h * xd
            qn = _head_rms(z_ref[:, c0:c0 + xd].astype(F32), g).astype(BF16)
            sc = lax.dot_general(qn, k_ref[:, c0:c0 + xd], (((1,), (1,)), ((), ())), preferred_element_type=F32)
            sc = sc / math.sqrt(xd)
            p = jnp.exp(sc - jnp.max(sc, axis=-1, keepdims=True))
            p = p / jnp.sum(p, axis=-1, keepdims=True)
            y_ref[:, c0:c0 + xd] = jnp.dot(p.astype(BF16), v_ref[:, c0:c0 + xd], preferred_element_type=F32).astype(BF16)

    kvs = pl.BlockSpec((nm, d), lambda b, i: (b, 0))
    return _pcall(
        body, name="xa_fwd", out_shape=_sds((nb * s, d), BF16), grid=(nb, nq),
        in_specs=[pl.BlockSpec((tq, d), lambda b, i: (b * nq + i, q_block)), kvs, kvs, pl.BlockSpec((1, xd), lambda b, i: (0, 0))],
        out_specs=pl.BlockSpec((tq, d), lambda b, i: (b * nq + i, 0)), compiler_params=_cp(),
    )(z, kn, vb, q_g)


def _xa_bwd(z, kn, vb, q_g, dy, dz, *, nb, s, d, nm, q_block, tq=512):
    xd = d // XH
    nq = s // tq

    def body(z_ref, k_ref, v_ref, g_ref, dy_ref, dz_in, dz_ref, dk_ref, dv_ref, dg_ref):
        b_i, i = pl.program_id(0), pl.program_id(1)
        g = g_ref[...]

        @pl.when(i == 0)
        def _():
            dk_ref[...] = jnp.zeros_like(dk_ref)
            dv_ref[...] = jnp.zeros_like(dv_ref)

        @pl.when((i == 0) & (b_i == 0))
        def _():
            dg_ref[...] = jnp.zeros_like(dg_ref)

        for h in range(XH):
            c0 = h * xd
            xq = z_ref[:, c0:c0 + xd].astype(F32)
            qn = _head_rms(xq, g).astype(BF16)
            kh, vh = k_ref[:, c0:c0 + xd], v_ref[:, c0:c0 + xd]
            sc = lax.dot_general(qn, kh, (((1,), (1,)), ((), ())), preferred_element_type=F32) / math.sqrt(xd)
            p = jnp.exp(sc - jnp.max(sc, axis=-1, keepdims=True))
            p = p / jnp.sum(p, axis=-1, keepdims=True)
            dob = dy_ref[:, c0:c0 + xd].astype(BF16)
            dv_ref[:, c0:c0 + xd] += lax.dot_general(p.astype(BF16), dob, (((0,), (0,)), ((), ())), preferred_element_type=F32)
            dp = lax.dot_general(dob, vh, (((1,), (1,)), ((), ())), preferred_element_type=F32)
            ds = p * (dp - jnp.sum(p * dp, axis=-1, keepdims=True)) / math.sqrt(xd)
            dsb = ds.astype(BF16)
            dqn = jnp.dot(dsb, kh, preferred_element_type=F32)
            dk_ref[:, c0:c0 + xd] += lax.dot_general(dsb, qn, (((0,), (0,)), ((), ())), preferred_element_type=F32)
            dx, sg = _head_rms_bwd(xq, g, dqn)
            dz_ref[:, c0:c0 + xd] = dx.astype(BF16)
            dg_ref[...] += sg

    kvs = pl.BlockSpec((nm, d), lambda b, i: (b, 0))
    qs = pl.BlockSpec((tq, d), lambda b, i: (b * nq + i, q_block))
    gs = pl.BlockSpec((1, xd), lambda b, i: (0, 0))
    return _pcall(
        body, name="xa_bwd",
        out_shape=(_sds(dz.shape, dz.dtype), _sds((nb * nm, d), F32), _sds((nb * nm, d), F32), _sds((1, xd), F32)), grid=(nb, nq),
        in_specs=[qs, kvs, kvs, gs, pl.BlockSpec((tq, d), lambda b, i: (b * nq + i, 0)), HBM_SPEC],
        out_specs=(qs, kvs, kvs, gs), input_output_aliases={5: 0},
        compiler_params=_cp(dimension_semantics=("arbitrary", "arbitrary")),
    )(z, kn, vb, q_g, dy, dz)


def _xa_post_kv(kv, dkn, dv, k_g, *, rows, d):
    xd = d // XH

    def body(kv_ref, dk_ref, dv_ref, g_ref, dkv_ref, dg_ref):
        g = g_ref[...]
        acc = jnp.zeros((1, xd), F32)
        for h in range(XH):
            c0 = h * xd
            dx, sg = _head_rms_bwd(kv_ref[:, c0:c0 + xd], g, dk_ref[:, c0:c0 + xd])
            dkv_ref[:, c0:c0 + xd] = dx.astype(BF16)
            acc = acc + sg
        dkv_ref[:, d:2 * d] = dv_ref[...].astype(BF16)
        dg_ref[...] = acc

    return _pcall(body, name="xa_post_kv", out_shape=(_sds((rows, 2 * d), BF16), _sds((1, xd), F32)), compiler_params=_cp())(
        kv, dkn, dv, k_g)


def _sigmoid(v):
    return 1.0 / (1.0 + jnp.exp(-v))


def _branches_fwd(z, ys, ws, *, t, d, gate_block, tm=512, tn=512, comm=None):
    nj = d // tn
    c_ins, c_in_specs, c_out_shape, c_scratch = _comm_lists(comm)
    n_steps = (t // tm) * nj

    def body(*refs):
        _, ins, cin, outs, cout, _, csem = _split_refs(refs, 0, 9, len(c_ins), 2, len(c_ins), 0)
        ya_ref, yb_ref, yc_ref, wa_ref, wb_ref, wc_ref, ga_ref, gb_ref, gc_ref = ins
        m_ref, p_ref = outs
        step = pl.program_id(0) * nj + pl.program_id(1)
        _run_comm(comm, cin, cout, csem, step, n_steps, "start")
        acc = None
        for q, (y_ref, w_ref, g_ref) in enumerate(((ya_ref, wa_ref, ga_ref), (yb_ref, wb_ref, gb_ref), (yc_ref, wc_ref, gc_ref))):
            pr = jnp.dot(y_ref[...], w_ref[...], preferred_element_type=F32)
            p_ref[q] = pr.astype(BF16)
            term = _sigmoid(g_ref[...].astype(F32)) * pr
            acc = term if acc is None else acc + term
        m_ref[...] = acc.astype(BF16)
        _run_comm(comm, cin, cout, csem, step, n_steps, "end")

    y_spec = pl.BlockSpec((tm, d), lambda i, j: (i, 0))
    w_spec = pl.BlockSpec((d, tn), lambda i, j: (0, j))

    def gate(q):
        return pl.BlockSpec((tm, tn), lambda i, j: (i, (gate_block + q) * nj + j))

    outs = _pcall(
        body, name="branches_fwd", out_shape=tuple([_sds((t, d), BF16), _sds((3, t, d), BF16)] + c_out_shape), grid=(t // tm, nj),
        in_specs=[y_spec] * 3 + [w_spec] * 3 + [gate(0), gate(1), gate(2)] + c_in_specs,
        out_specs=tuple([pl.BlockSpec((tm, tn), lambda i, j: (i, j)), pl.BlockSpec((3, tm, tn), lambda i, j: (0, i, j))]
                        + [HBM_SPEC] * len(c_ins)),
        scratch_shapes=c_scratch, compiler_params=_cp(dimension_semantics=("arbitrary", "arbitrary")),
    )(*ys, *ws, z, z, z, *c_ins)
    return outs[0], outs[1], list(outs[2:])


def _gates_bwd(z, dm, proj, dz, *, t, d, gate_block, tm=512):
    def body(g_ref, dm_ref, p_ref, dz_in, dz_ref, da_ref, db_ref, dc_ref):
        q = pl.program_id(1)
        sg = _sigmoid(g_ref[...].astype(F32))
        dmv = dm_ref[...]
        dz_ref[...] = (dmv * p_ref[0].astype(F32) * sg * (1.0 - sg)).astype(BF16)
        dp = (dmv * sg).astype(BF16)
        @pl.when(q == 0)
        def _():
            da_ref[...] = dp

        @pl.when(q == 1)
        def _():
            db_ref[...] = dp

        @pl.when(q == 2)
        def _():
            dc_ref[...] = dp

    gs = pl.BlockSpec((tm, d), lambda i, q: (i, gate_block + q))
    o = pl.BlockSpec((tm, d), lambda i, q: (i, 0))
    return _pcall(
        body, name="gates_bwd", out_shape=(_sds(dz.shape, dz.dtype),) + (_sds((t, d), BF16),) * 3, grid=(t // tm, 3),
        in_specs=[gs, o, pl.BlockSpec((1, tm, d), lambda i, q: (q, i, 0)), HBM_SPEC], out_specs=(gs, o, o, o),
        input_output_aliases={3: 0}, compiler_params=_cp(dimension_semantics=("arbitrary", "arbitrary")),
    )(z, dm, proj, dz)


def _ffn_in_fwd(h2, wfi_t, *, t, d, dff, tm=512, tn=1408):
    nj = dff // tn

    def body(h_ref, wg_ref, wu_ref, g_ref, u_ref, a_ref):
        hv = h_ref[...]
        dn = (((1,), (1,)), ((), ()))
        gv = lax.dot_general(hv, wg_ref[...], dn, preferred_element_type=F32)
        uv = lax.dot_general(hv, wu_ref[...], dn, preferred_element_type=F32)
        g_ref[...] = gv.astype(BF16)
        u_ref[...] = uv.astype(BF16)
        a_ref[...] = (gv * _sigmoid(gv) * uv).astype(BF16)

    o = pl.BlockSpec((tm, tn), lambda i, j: (i, j))
    return _pcall(
        body, name="ffn_in_fwd", out_shape=(_sds((t, dff), BF16),) * 3, grid=(t // tm, nj),
        in_specs=[pl.BlockSpec((tm, d), lambda i, j: (i, 0)), pl.BlockSpec((tn, d), lambda i, j: (j, 0)),
                  pl.BlockSpec((tn, d), lambda i, j: (nj + j, 0))],
        out_specs=(o, o, o), compiler_params=_cp(),
    )(h2, wfi_t, wfi_t)


def _ffn_out_bwd(dy, wfo, gate, up, *, t, d, dff, tm=512, tn=1408):
    def body(dy_ref, w_ref, g_ref, u_ref, dg_ref, du_ref):
        dav = lax.dot_general(dy_ref[...].astype(BF16), w_ref[...], (((1,), (1,)), ((), ())), preferred_element_type=F32)
        gv = g_ref[...].astype(F32)
        sg = _sigmoid(gv)
        dg_ref[...] = (dav * u_ref[...].astype(F32) * sg * (1.0 + gv * (1.0 - sg))).astype(BF16)
        du_ref[...] = (dav * gv * sg).astype(BF16)

    o = pl.BlockSpec((tm, tn), lambda i, j: (i, j))
    return _pcall(
        body, name="ffn_out_bwd", out_shape=(_sds((t, dff), BF16),) * 2, grid=(t // tm, dff // tn),
        in_specs=[pl.BlockSpec((tm, d), lambda i, j: (i, 0)), pl.BlockSpec((tn, d), lambda i, j: (j, 0)), o, o],
        out_specs=(o, o), compiler_params=_cp(),
    )(dy, wfo, gate, up)


def _ffn_out_loss(act, wfo, x1, target, *, t, d, dff, tm=512):
    def body(a_ref, w_ref, x_ref, t_ref, dy_ref, l_ref):
        yv = x_ref[...] + jnp.dot(a_ref[...], w_ref[...], preferred_element_type=F32)
        e = yv - t_ref[...]
        dy_ref[...] = e * (1.0 / d)

        @pl.when(pl.program_id(0) == 0)
        def _():
            l_ref[...] = jnp.zeros_like(l_ref)

        l_ref[...] += jnp.sum(jnp.sum(e * e, axis=-1, keepdims=True), axis=0, keepdims=True) * (0.5 / d)

    o = pl.BlockSpec((tm, d), lambda i: (i, 0))
    dy, l = _pcall(
        body, name="ffn_out_loss", out_shape=(_sds((t, d), F32), _sds((1, HD), F32)), grid=(t // tm,),
        in_specs=[pl.BlockSpec((tm, dff), lambda i: (i, 0)), pl.BlockSpec((dff, d), lambda i: (0, 0)), o, o],
        out_specs=(o, pl.BlockSpec((1, HD), lambda i: (0, 0))),
        compiler_params=_cp(dimension_semantics=("arbitrary",)),
    )(act, wfo, x1, target)
    return dy, l[0, 0]


def _adamw(w, g, m, v, *, name):
    if w.ndim == 3:
        rows, _, cols = w.shape
        tm = max(tt for tt in range(1, 65) if rows % tt == 0)
        block, imap, grid = (tm, 1, cols), (lambda i, j: (i, 0, 0)), (rows // tm, 1)
    else:
        rows, cols = w.shape
        tm, tn = rows, cols
        for cand in (256, 128, 64, 32, 16, 8):
            if rows > cand and rows % cand == 0:
                tm = cand
                break
        if tm == rows and rows > 256 and cols % 128 == 0:
            tn = 128
        block, imap, grid = (tm, tn), (lambda i, j: (i, j)), (rows // tm, cols // tn)
    c1 = 1.0 - ADAM_B1 ** ADAM_STEP
    c2 = 1.0 - ADAM_B2 ** ADAM_STEP

    at = (slice(None), 0, slice(None)) if w.ndim == 3 else Ellipsis

    def body(w_ref, g_ref, m_ref, v_ref, d_ref, nm_ref, nv_ref):
        gv = g_ref[at]
        mn = ADAM_B1 * m_ref[at] + (1.0 - ADAM_B1) * gv
        vn = ADAM_B2 * v_ref[at] + (1.0 - ADAM_B2) * (gv * gv)
        d_ref[at] = -ADAM_LR * ((mn / c1) / (jnp.sqrt(vn / c2) + ADAM_EPS) + ADAM_WD * w_ref[at])
        nm_ref[at] = mn
        nv_ref[at] = vn

    spec = pl.BlockSpec(block, imap)
    return _pcall(
        body, name=name, out_shape=(_sds(w.shape, F32),) * 3, grid=grid,
        in_specs=[spec] * 4, out_specs=(spec,) * 3, compiler_params=_cp(),
    )(w, g, m, v)


def _pad_rows(a, rows):
    return a if a.shape[0] == rows else jnp.pad(a, ((0, rows - a.shape[0]),) + ((0, 0),) * (a.ndim - 1))


class _Pack:
    def __init__(self, row_sizes):
        self.rows = list(row_sizes)
        self.padded = [-(-r // 16) * 16 for r in self.rows]
        self.offs = [sum(self.padded[:i]) for i in range(len(self.rows))]
        self.total = sum(self.padded)

    def stack(self, blocks):
        return jnp.concatenate([_pad_rows(b.astype(BF16), p) for b, p in zip(blocks, self.padded)], axis=0)

    def full(self, gathered, i):
        r = self.rows[i]
        return gathered[:, self.offs[i]:self.offs[i] + r, :].reshape(N_DEV * r, gathered.shape[2])

    def for_core(self, full_grads, core):
        parts = []
        for gfull, r, p in zip(full_grads, self.rows, self.padded):
            pieces = gfull if isinstance(gfull, (tuple, list)) else (gfull,)
            blk = jnp.concatenate(
                [lax.dynamic_index_in_dim(g.reshape(-1, 2, r, g.shape[1]), core, axis=1, keepdims=False) for g in pieces],
                axis=0).astype(BF16)
            if p != r:
                blk = jnp.pad(blk, ((0, 0), (0, p - r), (0, 0)))
            parts.append(blk)
        return jnp.concatenate(parts, axis=1)

    def shard(self, g_pack, i):
        return g_pack[self.offs[i]:self.offs[i] + self.rows[i], :]


def kernel(x, mem, norm1_g, w_in, conv_w, conv_b, fox_f_bias, fox_q_g, fox_k_g, mem_norm_g, w_mem_kv, xa_q_g, xa_k_g, w_br_conv, w_br_fox, w_br_xa, w_o, norm2_g, w_ffn_in, w_ffn_out, loss_target, m_norm1_g, m_w_in, m_conv_w, m_conv_b, m_fox_f_bias, m_fox_q_g, m_fox_k_g, m_mem_norm_g, m_w_mem_kv, m_xa_q_g, m_xa_k_g, m_w_br_conv, m_w_br_fox, m_w_br_xa, m_w_o, m_norm2_g, m_w_ffn_in, m_w_ffn_out, v_norm1_g, v_w_in, v_conv_w, v_conv_b, v_fox_f_bias, v_fox_q_g, v_fox_k_g, v_mem_norm_g, v_w_mem_kv, v_xa_q_g, v_xa_k_g, v_w_br_conv, v_w_br_fox, v_w_br_xa, v_w_o, v_norm2_g, v_w_ffn_in, v_w_ffn_out):
    nb, s, d = x.shape
    t = nb * s
    nm = mem.shape[1]
    nh = d // HD
    dff = w_ffn_out.shape[1] * N_DEV
    n_in = w_in.shape[2] * N_DEV
    n_in_pad = -(-n_in // 1152) * 1152
    ff_block = (10 * d) // HD
    my_c = lax.axis_index("c")
    my_q = 2 * lax.axis_index("x") + lax.axis_index("y")

    pack_a = _Pack([w_in.shape[2]])
    pack_b = _Pack([w_mem_kv.shape[2], w_br_conv.shape[1], w_br_fox.shape[1], w_br_xa.shape[1], w_o.shape[1],
                    w_ffn_in.shape[2], w_ffn_out.shape[1]])
    gather_2 = _Pack([w_mem_kv.shape[2], w_br_conv.shape[1], w_br_fox.shape[1], w_br_xa.shape[1], w_o.shape[1]])
    stack_a = pack_a.stack([w_in[0].T])
    stack_2 = gather_2.stack([w_mem_kv[0].T, w_br_conv[0], w_br_fox[0], w_br_xa[0], w_o[0]])
    stack_fi = w_ffn_in[0].T.astype(BF16)
    stack_fo = w_ffn_out[0].astype(BF16)
    cw_block = _pad_rows(conv_w[0], 8)
    rows_a = w_in.shape[2]
    my_x, my_y = lax.axis_index("x"), lax.axis_index("y")
    arrival = [(my_x, my_y, my_c), (my_x, my_y, 1 - my_c)]
    chip_1, chip_2, chip_d = _chip_order(my_x, my_y, my_c)
    arrival += [(*chip_1, my_c), (*chip_2, 1 - my_c), (*chip_2, my_c), (*chip_1, 1 - my_c), (*chip_d, my_c), (*chip_d, 1 - my_c)]
    order = jnp.stack([4 * ax + 2 * ay + ac for ax, ay, ac in arrival]).astype(jnp.int32)

    x2 = x.reshape(t, d)
    mem2 = mem.reshape(nb * nm, d)
    tgt2 = loss_target.reshape(t, d)
    h1 = _rms_fwd(x2, norm1_g, name="rms1_fwd")
    z, slabs, _ = _z_gathered(h1, stack_a, order, t=t, d=d, n_pad=N_DEV * SLAB_WIN, rows=rows_a)
    z = _z_fix(h1, slabs, z, t=t, d=d, rows=rows_a)
    ff_rows = _pad_rows(slabs[N_DEV - 1, rows_a - nh:rows_a, :], HD)
    zff = _mm(h1, ff_rows, m=t, n=HD, k=d, tb=True, tm=1024, name="mm_z_ff")
    qn, kn, vb = _fox_prep(z, fox_q_g, fox_k_g, t=t, d=d)
    f_bias_col = fox_f_bias.reshape(nh, 1)
    ck_rows = _fox_gates(zff, f_bias_col, nb=nb, s=s, nh=nh).reshape(nb * nh, 1, s)
    y_fox, lse, (gathered_2, cw_g, gathered_fo) = _fox_fwd(qn, kn, vb, ck_rows, nb=nb, s=s, nh=nh,
                                                           comm=_AllGather([stack_2, cw_block, stack_fo], forward_at=0.8))
    wkv_t, wbc, wbf, wbx, wo = (gather_2.full(gathered_2, i) for i in range(5))
    wfo = gathered_fo.reshape(dff, d)
    conv_w_full = jnp.transpose(cw_g[:, 0:3, :], (1, 0, 2)).reshape(3, d)

    y_conv = _conv_fwd(z, conv_w_full, conv_b, nb=nb, s=s, d=d)
    mem_n = _rms_fwd(mem2, mem_norm_g, name="rms_mem_fwd")
    kv = _mm(mem_n, wkv_t, m=nb * nm, n=2 * d, k=d, tb=True, name="mm_kv")
    xkn, xvb = _xa_prep_kv(kv, xa_k_g, rows=nb * nm, d=d)
    y_xa = _xa_fwd(z, xkn, xvb, xa_q_g, nb=nb, s=s, d=d, nm=nm, q_block=6)
    merged, proj, (gathered_fi,) = _branches_fwd(z, (y_conv, y_fox, y_xa), (wbc, wbf, wbx), t=t, d=d, gate_block=7,
                                                 comm=_AllGather([stack_fi]))
    wfi_t = gathered_fi.reshape(2 * dff, d)
    x1 = _mm(merged, wo, m=t, n=d, k=d, res=x2, name="mm_x1")
    h2 = _rms_fwd(x1, norm2_g, name="rms2_fwd")
    ff_gate, ff_up, act = _ffn_in_fwd(h2, wfi_t, t=t, d=d, dff=dff)
    dy, loss_local = _ffn_out_loss(act, wfo, x1, tgt2, t=t, d=d, dff=dff)

    g_wfo = _mm(act, dy, m=dff, n=d, k=t, ta=True, tm=1408, tk=2048, out_dtype=BF16, name="mm_g_wfo")
    dgate, dup = _ffn_out_bwd(dy, wfo, ff_gate, ff_up, t=t, d=d, dff=dff)
    dh2 = _mm(dgate, wfi_t, m=t, n=d, k=dff, tk=dff, name="mm_dh2_g")
    dh2 = _mm(dup, wfi_t, m=t, n=d, k=dff, tk=dff, b_off=(1, 0), res=dh2, name="mm_dh2_u")
    g_wfi_g = _mm(dgate, h2, m=dff, n=d, k=t, ta=True, tm=1408, tk=2048, out_dtype=BF16, name="mm_g_wfi_g")
    g_wfi_u = _mm(dup, h2, m=dff, n=d, k=t, ta=True, tm=1408, tk=2048, out_dtype=BF16, name="mm_g_wfi_u")
    dx1, g_norm2 = _rms_bwd(dh2, x1, norm2_g, dy, name="rms2_bwd")
    dmerged = _mm(dx1, wo, m=t, n=d, k=d, tb=True, name="mm_dmerged")
    g_wo = _mm(merged, dx1, m=d, n=d, k=t, ta=True, tk=2048, out_dtype=BF16, name="mm_g_wo")

    dz = lax.empty((t, n_in_pad), BF16)
    dz, dpa, dpb, dpc = _gates_bwd(z, dmerged, proj, dz, t=t, d=d, gate_block=7)
    dy_conv = _mm(dpa, wbc, m=t, n=d, k=d, tb=True, name="mm_dy_conv")
    g_wbc = _mm(y_conv, dpa, m=d, n=d, k=t, ta=True, tk=4096, out_dtype=BF16, name="mm_g_wbc")
    dy_fox = _mm(dpb, wbf, m=t, n=d, k=d, tb=True, name="mm_dy_fox")
    g_wbf = _mm(y_fox, dpb, m=d, n=d, k=t, ta=True, tk=4096, out_dtype=BF16, name="mm_g_wbf")
    dy_xa = _mm(dpc, wbx, m=t, n=d, k=d, tb=True, name="mm_dy_xa")
    g_wbx = _mm(y_xa, dpc, m=d, n=d, k=t, ta=True, tk=4096, out_dtype=BF16, name="mm_g_wbx")

    dz, dxkn, dxv, g_xa_q = _xa_bwd(z, xkn, xvb, xa_q_g, dy_xa, dz, nb=nb, s=s, d=d, nm=nm, q_block=6)
    dkv, g_xa_k = _xa_post_kv(kv, dxkn, dxv, xa_k_g, rows=nb * nm, d=d)
    g_wkv_t = _mm(dkv, mem_n, m=2 * d, n=d, k=nb * nm, ta=True, out_dtype=BF16, name="mm_g_wkv")
    dmem_n = _mm(dkv, wkv_t, m=nb * nm, n=d, k=2 * d, tk=2 * d, name="mm_dmem")
    _, g_mem_norm = _rms_bwd(dmem_n, mem2, mem_norm_g, None, name="rms_mem_bwd")

    grads_b = [g_wkv_t, g_wbc, g_wbf, g_wbx, g_wo, (g_wfi_g, g_wfi_u), g_wfo]
    keep_b, send_b = pack_b.for_core(grads_b, my_c), pack_b.for_core(grads_b, 1 - my_c)
    dz, g_conv_w, g_conv_b, (from_sibling_b,) = _conv_bwd(z, dy_conv, conv_w_full, conv_b, dz, nb=nb, s=s, d=d,
                                                          comm=_SiblingSwap(send_b))
    part_b = _add2(keep_b, from_sibling_b, name="rs_add_sibling_b")
    dqn, ds_rows, dkn, dvb, ds_cols, (from_chips_b,) = _fox_bwd(qn, kn, vb, ck_rows, dy_fox, y_fox, lse, nb=nb, s=s, nh=nh,
                                                                comm=_ChipExchange(part_b))
    g_pack_b = _add4(lax.dynamic_index_in_dim(part_b, my_q, axis=0, keepdims=False), from_chips_b, name="rs_add_chips_b")
    dz, g_fox_q, g_fox_k = _fox_post(z, dqn, dkn, dvb, fox_q_g, fox_k_g, dz, t=t, d=d)
    dz, g_f_bias = _fox_gates_bwd(zff, f_bias_col, ds_rows, ds_cols, dz, nb=nb, s=s, nh=nh, ff_block=ff_block)

    keep_a, from_sibling_a = _g_win_slabs(dz, h1, my_c, t=t, d=d, n_slab=N_DEV, rows=rows_a, rp=slabs.shape[1])
    part_a = _add2(keep_a, from_sibling_a, name="rs_add_sibling_a")
    dh1, (from_chips_a,) = _dh1_from_slabs(dz, slabs, t=t, d=d, rows=rows_a, comm=_ChipExchange(part_a))
    g_pack_a = _add4(lax.dynamic_index_in_dim(part_a, my_q, axis=0, keepdims=False), from_chips_a, name="rs_add_chips_a",
                     rows_apart=True)
    dx, g_norm1 = _rms_bwd(dh1, x2, norm1_g, dx1, name="rms1_bwd")

    xd = d // XH
    tail = jnp.concatenate(
        [_pad_rows(g_f_bias.reshape(nh), HD).reshape(1, HD), g_fox_q, g_fox_k, g_xa_q, g_xa_k,
         jnp.zeros((1, d - 3 * HD - 2 * xd), F32)], axis=1)
    small = jnp.concatenate([g_norm1, g_norm2, g_conv_w, g_conv_b, g_mem_norm, tail], axis=0)
    small = _all_reduce_small(small)
    gs_conv_w = lax.dynamic_slice_in_dim(small[2:5], (2 * my_q + my_c) * (d // N_DEV), d // N_DEV, axis=1)

    grads = {
        "norm1_g": small[0:1], "w_in": g_pack_a, "conv_w": gs_conv_w, "conv_b": small[5:6],
        "fox_f_bias": small[7:8, 0:nh], "fox_q_g": small[7:8, HD:2 * HD], "fox_k_g": small[7:8, 2 * HD:3 * HD],
        "mem_norm_g": small[6:7], "w_mem_kv": pack_b.shard(g_pack_b, 0), "xa_q_g": small[7:8, 3 * HD:3 * HD + xd],
        "xa_k_g": small[7:8, 3 * HD + xd:3 * HD + 2 * xd], "w_br_conv": pack_b.shard(g_pack_b, 1),
        "w_br_fox": pack_b.shard(g_pack_b, 2), "w_br_xa": pack_b.shard(g_pack_b, 3), "w_o": pack_b.shard(g_pack_b, 4),
        "norm2_g": small[1:2], "w_ffn_in": pack_b.shard(g_pack_b, 5), "w_ffn_out": pack_b.shard(g_pack_b, 6),
    }
    transposed = {"w_in", "w_mem_kv", "w_ffn_in"}
    weights = {
        "norm1_g": (norm1_g, m_norm1_g, v_norm1_g), "w_in": (w_in, m_w_in, v_w_in), "conv_w": (conv_w, m_conv_w, v_conv_w),
        "conv_b": (conv_b, m_conv_b, v_conv_b), "fox_f_bias": (fox_f_bias, m_fox_f_bias, v_fox_f_bias),
        "fox_q_g": (fox_q_g, m_fox_q_g, v_fox_q_g), "fox_k_g": (fox_k_g, m_fox_k_g, v_fox_k_g),
        "mem_norm_g": (mem_norm_g, m_mem_norm_g, v_mem_norm_g), "w_mem_kv": (w_mem_kv, m_w_mem_kv, v_w_mem_kv),
        "xa_q_g": (xa_q_g, m_xa_q_g, v_xa_q_g), "xa_k_g": (xa_k_g, m_xa_k_g, v_xa_k_g),
        "w_br_conv": (w_br_conv, m_w_br_conv, v_w_br_conv), "w_br_fox": (w_br_fox, m_w_br_fox, v_w_br_fox),
        "w_br_xa": (w_br_xa, m_w_br_xa, v_w_br_xa), "w_o": (w_o, m_w_o, v_w_o), "norm2_g": (norm2_g, m_norm2_g, v_norm2_g),
        "w_ffn_in": (w_ffn_in, m_w_ffn_in, v_w_ffn_in), "w_ffn_out": (w_ffn_out, m_w_ffn_out, v_w_ffn_out),
    }
    out_g, out_d, out_m, out_v = [], [], [], []
    for name, (w, m, v) in weights.items():
        shape = w.shape
        if name == "w_in":
            w3, m3, v3 = (a.transpose(2, 0, 1) for a in (w, m, v))
            dl, mn, vn = _adamw(w3, g_pack_a, m3, v3, name="adamw_" + name)
            out_g.append(g_pack_a[:shape[2]].transpose(1, 2, 0))
            out_d.append(dl.transpose(1, 2, 0))
            out_m.append(mn.transpose(1, 2, 0))
            out_v.append(vn.transpose(1, 2, 0))
            continue
        tr = name in transposed

        def as2d(a):
            a = a.reshape(shape[-2], shape[-1])
            return a.T if tr else a

        def back(a):
            return (a.T if tr else a).reshape(shape)

        w2 = as2d(w)
        g2 = grads[name].reshape(w2.shape)
        dl, mn, vn = _adamw(w2, g2, as2d(m), as2d(v), name="adamw_" + name)
        out_g.append(back(g2))
        out_d.append(back(dl))
        out_m.append(back(mn))
        out_v.append(back(vn))

    loss = lax.psum(loss_local, ("x", "y", "c"))
    return (loss, dx.reshape(nb, s, d), *out_g, *out_d, *out_m, *out_v)
```

```python
import math

import jax
import jax.numpy as jnp
from jax import lax
from jax.experimental import pallas as pl
from jax.experimental.pallas import tpu as pltpu

F32, BF16 = jnp.float32, jnp.bfloat16
EPS = 1e-6
HD = 128
XH = 4
N_DEV = 8
MESH = pl.DeviceIdType.MESH
VMEM_LIMIT = 52 * 1024 * 1024
NEG = -1e30
SQRT_HD = math.sqrt(HD)
EXP2_C = math.log2(math.e) / SQRT_HD
LOG2E = math.log2(math.e)
FOX_HP = 4

ADAM_LR, ADAM_B1, ADAM_B2, ADAM_EPS, ADAM_WD, ADAM_STEP = 0.001, 0.9, 0.999, 1e-08, 0.01, 10


def _cp(**kw):
    return pltpu.CompilerParams(vmem_limit_bytes=VMEM_LIMIT, **kw)


def _pcall(body, **kw):
    return pl.pallas_call(body, **kw)


def _sds(shape, dtype):
    return jax.ShapeDtypeStruct(shape, dtype)


HBM_SPEC = pl.BlockSpec(memory_space=pl.ANY)


def _mesh_pos():
    return lax.axis_index("x"), lax.axis_index("y"), lax.axis_index("c")


class _AllGather:
    def __init__(self, blocks, forward_at=None):
        self.blocks = list(blocks)
        self.forward_at = forward_at
        n = len(self.blocks)
        self.out_shape = [_sds((N_DEV,) + b.shape, b.dtype) for b in self.blocks]
        self.scratch = [pltpu.SemaphoreType.DMA((n, 7)), pltpu.SemaphoreType.DMA((n, 7)), pltpu.SemaphoreType.DMA((n,))]

    def bind(self, ins, outs, sems):
        n = len(ins)
        send_sems, recv_sems, local_sems = sems
        x, y, c = _mesh_pos()
        me, sibling = (x, y, c), (x, y, 1 - c)
        chips = [(1 - x, y), (x, 1 - y), (1 - x, 1 - y)]

        def slab(t, dev):
            return outs[t].at[4 * dev[0] + 2 * dev[1] + dev[2]]

        def copy(t, k, block, to, src=None):
            dst = slab(t, block)
            return pltpu.make_async_remote_copy(
                src_ref=dst if src is None else src, dst_ref=dst, send_sem=send_sems.at[t, k], recv_sem=recv_sems.at[t, k],
                device_id=to, device_id_type=MESH)

        mine = [pltpu.make_async_copy(ins[t], slab(t, me), local_sems.at[t]) for t in range(n)]
        first = []
        for t in range(n):
            first.append(copy(t, 0, me, sibling, src=ins[t]))
            first += [copy(t, 1 + j, me, (*chip, c), src=ins[t]) for j, chip in enumerate(chips)]
        passed = [copy(t, 4 + j, (*chip, c), sibling) for j, chip in enumerate(chips) for t in range(n)]

        def start():
            for cp in mine + first:
                cp.start()

        def mid():
            for j, chip in enumerate(chips):
                for t in range(n):
                    copy(t, 1 + j, (*chip, c), me).wait_recv()
                    passed[j * n + t].start()

        def finish():
            for t in range(n):
                copy(t, 0, sibling, me).wait_recv()
                for j, chip in enumerate(chips):
                    copy(t, 4 + j, (*chip, 1 - c), me).wait_recv()
            for cp in first + passed:
                cp.wait_send()
            for cp in mine:
                cp.wait()

        return start, mid, finish


class _SiblingSwap:
    def __init__(self, send):
        self.blocks = [send]
        self.out_shape = [_sds(send.shape, send.dtype)]
        self.scratch = [pltpu.SemaphoreType.DMA, pltpu.SemaphoreType.DMA]

    def bind(self, ins, outs, sems):
        x, y, c = _mesh_pos()
        cp = pltpu.make_async_remote_copy(src_ref=ins[0], dst_ref=outs[0], send_sem=sems[0], recv_sem=sems[1],
                                          device_id=(x, y, 1 - c), device_id_type=MESH)
        return cp.start, (lambda: None), cp.wait


class _ChipExchange:
    def __init__(self, part):
        self.blocks = [part]
        self.out_shape = [_sds((3,) + part.shape[1:], part.dtype)]
        self.scratch = [pltpu.SemaphoreType.DMA((3,)), pltpu.SemaphoreType.DMA((3,))]

    def bind(self, ins, outs, sems):
        x, y, c = _mesh_pos()
        chips = [(1 - x, y), (x, 1 - y), (1 - x, 1 - y)]
        cps = [
            pltpu.make_async_remote_copy(src_ref=ins[0].at[2 * cx + cy], dst_ref=outs[0].at[j], send_sem=sems[0].at[j],
                                         recv_sem=sems[1].at[j], device_id=(cx, cy, c), device_id_type=MESH)
            for j, (cx, cy) in enumerate(chips)
        ]

        def start():
            for cp in cps:
                cp.start()

        def finish():
            for cp in cps:
                cp.wait()

        return start, (lambda: None), finish


def _comm_lists(comm):
    if comm is None:
        return [], [], [], []
    n = len(comm.blocks)
    return list(comm.blocks), [HBM_SPEC] * n, list(comm.out_shape), list(comm.scratch)


def _split_refs(refs, n_pre, n_in, n_cin, n_out, n_cout, n_scr):
    cuts = [n_pre, n_in, n_cin, n_out, n_cout, n_scr]
    out, at = [], 0
    for c in cuts:
        out.append(refs[at:at + c])
        at += c
    out.append(refs[at:])
    return out


def _run_comm(comm, cin, cout, csem, step, n_steps, when="start"):
    if comm is None:
        return
    start, mid, finish = comm.bind(cin, cout, csem)
    if when == "start":
        pl.when(step == 0)(start)
    else:
        frac = getattr(comm, "forward_at", None)
        mid_step = max(n_steps - 2, 0) if frac is None else min(int(frac * n_steps), max(n_steps - 2, 0))
        pl.when(step == mid_step)(mid)
        pl.when(step == n_steps - 1)(finish)


def _all_reduce_small(v):
    rows, cols = v.shape

    def body(v_ref, o_ref, slots, send_sems, recv_sems):
        x, y, c = _mesh_pos()
        me = 4 * x + 2 * y + c
        slots[me] = v_ref[...]
        cps = []
        for k in range(1, N_DEV):
            px, py, pc = x ^ (k >> 2), y ^ ((k >> 1) & 1), c ^ (k & 1)
            cps.append(pltpu.make_async_remote_copy(
                src_ref=v_ref, dst_ref=slots.at[me], send_sem=send_sems.at[k - 1], recv_sem=recv_sems.at[k - 1],
                device_id=(px, py, pc), device_id_type=MESH))
        for cp in cps:
            cp.start()
        for cp in cps:
            cp.wait()
        acc = slots[0]
        for k in range(1, N_DEV):
            acc = acc + slots[k]
        o_ref[...] = acc

    return _pcall(
        body, name="all_reduce_small", out_shape=_sds((rows, cols), F32),
        in_specs=[pl.BlockSpec(memory_space=pltpu.VMEM)], out_specs=pl.BlockSpec(memory_space=pltpu.VMEM),
        scratch_shapes=[pltpu.VMEM((N_DEV, rows, cols), F32), pltpu.SemaphoreType.DMA((7,)), pltpu.SemaphoreType.DMA((7,))],
        compiler_params=_cp(),
    )(v)


def _row_tile(rows, cap=1040):
    return max(tt for tt in range(16, cap + 1, 16) if rows % tt == 0)


def _add2(a, b, *, name):
    nq, rows, cols = a.shape
    tm = _row_tile(rows)

    def body(a_ref, b_ref, o_ref):
        o_ref[...] = (a_ref[...].astype(F32) + b_ref[...].astype(F32)).astype(BF16)

    spec = pl.BlockSpec((1, tm, cols), lambda q, i: (q, i, 0))
    return _pcall(body, name=name, out_shape=_sds(a.shape, BF16), grid=(nq, rows // tm),
                  in_specs=[spec, spec], out_specs=spec, compiler_params=_cp())(a, b)


def _add4(own, recv, *, name, rows_apart=False):
    rows, cols = own.shape
    tm = _row_tile(rows, 256 if rows_apart else 1040)

    def body(o_ref, r_ref, g_ref):
        tot = ((o_ref[...].astype(F32) + r_ref[0].astype(F32)) + r_ref[1].astype(F32)) + r_ref[2].astype(F32)
        if rows_apart:
            g_ref[:, 0, :] = tot
        else:
            g_ref[...] = tot

    if rows_apart:
        out_shape, out_spec = _sds((rows, 1, cols), F32), pl.BlockSpec((tm, 1, cols), lambda i: (i, 0, 0))
    else:
        out_shape, out_spec = _sds((rows, cols), F32), pl.BlockSpec((tm, cols), lambda i: (i, 0))
    return _pcall(
        body, name=name, out_shape=out_shape, grid=(rows // tm,),
        in_specs=[pl.BlockSpec((tm, cols), lambda i: (i, 0)), pl.BlockSpec((3, tm, cols), lambda i: (0, i, 0))],
        out_specs=out_spec, compiler_params=_cp(),
    )(own, recv)


def _mm(a, b, *, m, n, k, name, ta=False, tb=False, b_off=(0, 0), res=None, out_dtype=F32, tm=512, tn=1024, tk=1024,
        comm=None):
    tm, tn, tk = min(tm, m), min(tn, n), min(tk, k)
    assert m % tm == 0 and n % tn == 0 and k % tk == 0, (name, m, n, k, tm, tn, tk)
    nk = k // tk
    grid = (m // tm, n // tn, nk)
    bo0, bo1 = b_off
    if ta:
        a_spec = pl.BlockSpec((tk, tm), lambda i, j, kk: (kk, i))
    else:
        a_spec = pl.BlockSpec((tm, tk), lambda i, j, kk: (i, kk))
    if tb:
        b_spec = pl.BlockSpec((tn, tk), lambda i, j, kk: (j + bo0, kk + bo1))
    else:
        b_spec = pl.BlockSpec((tk, tn), lambda i, j, kk: (kk + bo0, j + bo1))
    o_spec = pl.BlockSpec((tm, tn), lambda i, j, kk: (i, j))
    dn = (((0 if ta else 1,), (1 if tb else 0,)), ((), ()))
    has_res = res is not None
    c_ins, c_in_specs, c_out_shape, c_scratch = _comm_lists(comm)
    n_in = 3 if has_res else 2
    n_scr = 1 if nk > 1 else 0

    def body(*refs):
        _, ins, cin, outs, cout, scr, csem = _split_refs(refs, 0, n_in, len(c_ins), 1, len(c_ins), n_scr)
        a_ref, b_ref = ins[0], ins[1]
        r_ref = ins[2] if has_res else None
        o_ref = outs[0]
        step = (pl.program_id(0) * grid[1] + pl.program_id(1)) * nk + pl.program_id(2)
        _run_comm(comm, cin, cout, csem, step, grid[0] * grid[1] * nk, "start")
        p = lax.dot_general(a_ref[...].astype(BF16), b_ref[...].astype(BF16), dn, preferred_element_type=F32)

        def finish(acc):
            if has_res:
                acc = acc + r_ref[...]
            o_ref[...] = acc.astype(out_dtype)

        if nk == 1:
            finish(p)
        else:
            acc_ref = scr[0]
            kk = pl.program_id(2)

            @pl.when(kk == 0)
            def _():
                acc_ref[...] = p

            @pl.when(kk > 0)
            def _():
                acc_ref[...] += p

            @pl.when(kk == nk - 1)
            def _():
                finish(acc_ref[...])

        _run_comm(comm, cin, cout, csem, step, grid[0] * grid[1] * nk, "end")

    ins = [a, b] + ([res] if has_res else [])
    in_specs = [a_spec, b_spec] + ([o_spec] if has_res else [])
    sem = ("arbitrary",) * 3 if comm is not None else ("parallel", "parallel", "arbitrary")
    outs = _pcall(
        body, name=name, out_shape=tuple([_sds((m, n), out_dtype)] + c_out_shape), grid=grid,
        in_specs=in_specs + c_in_specs, out_specs=tuple([o_spec] + [HBM_SPEC] * len(c_ins)),
        scratch_shapes=([pltpu.VMEM((tm, tn), F32)] if nk > 1 else []) + c_scratch,
        compiler_params=_cp(dimension_semantics=sem),
    )(*(ins + c_ins))
    return outs[0] if comm is None else (outs[0], list(outs[1:]))


SLAB_WIN = 1280
SLAB_TAIL = 48


def _chip_order(x, y, c):
    north = c == 1
    first = (jnp.where(north, 1 - x, x), jnp.where(north, y, 1 - y))
    second = (jnp.where(north, x, 1 - x), jnp.where(north, 1 - y, y))
    return [first, second, (1 - x, 1 - y)]


def _z_gathered(h1, own_slab, order, *, t, d, n_pad, rows, comm=None, tm=1024):
    rp = own_slab.shape[0]
    n_slab = N_DEV
    tm = min(tm, t)
    ni = t // tm
    sh = rows - SLAB_WIN
    assert sh >= 0 and (n_slab - 1) * sh <= 128
    c_ins, c_in_specs, c_out_shape, c_scratch = _comm_lists(comm)
    n_steps = n_slab * ni

    def body(*refs):
        pre, ins, cin, outs, cout, scr, csem = _split_refs(refs, 1, 2, len(c_ins), 2, len(c_ins), 5)
        order_ref = pre[0]
        h_ref, own_ref = ins
        o_ref, slabs_ref = outs
        slab_vm, send_sems, recv_sems, local_sem, load_sem = scr
        p, i = pl.program_id(0), pl.program_id(1)
        step = p * ni + i
        x, y, c = _mesh_pos()
        me, sibling = (x, y, c), (x, y, 1 - c)
        chips = _chip_order(x, y, c)

        def slab(dev):
            return slabs_ref.at[4 * dev[0] + 2 * dev[1] + dev[2]]

        def copy(k, block, to, src=None, part=None):
            dst = slab(block) if part is None else slab(block).at[part]
            return pltpu.make_async_remote_copy(
                src_ref=dst if src is None else src, dst_ref=dst, send_sem=send_sems.at[k], recv_sem=recv_sems.at[k],
                device_id=to, device_id_type=MESH)

        nb_1, nb_2, diag = ((*chip, c) for chip in chips)
        cut = (rp // 32) * 16
        half_a, half_b = pl.ds(0, cut), pl.ds(cut, rp - cut)
        mine = pltpu.make_async_copy(own_ref, slab(me), local_sem)
        first = [copy(0, me, sibling, src=own_ref), copy(1, me, nb_1, src=own_ref), copy(2, me, nb_2, src=own_ref)]
        relays = [copy(3, nb_1, nb_2, part=half_a), copy(4, nb_2, nb_1, part=half_b)]
        axis_of = [jnp.where(c == 1, 0, 1), jnp.where(c == 1, 1, 0), 2]
        passed = [copy(5 + axis_of[j], (*chip, c), sibling) for j, chip in enumerate(chips)]
        later = min(n_steps // 5, n_steps - 1)

        @pl.when(step == 0)
        def _():
            mine.start()
            first[0].start()
            first[1].start()

        @pl.when(step == later)
        def _():
            first[2].start()

        _run_comm(comm, cin, cout, csem, step, n_steps, "start")

        def load(src):
            cp = pltpu.make_async_copy(src, slab_vm, load_sem)
            cp.start()
            cp.wait()

        @pl.when(i == 0)
        def _():
            @pl.when(p == 0)
            def _():
                load(own_ref)

            @pl.when(p == 1)
            def _():
                copy(0, sibling, me).wait_recv()
                load(slab(sibling))

            for j, chip in enumerate(chips):
                @pl.when(p == 2 + 2 * j)
                def _():
                    if j < 2:
                        copy(1 + j, (*chip, c), me).wait_recv()
                        relays[j].start()
                    else:
                        copy(3, diag, me, part=half_a).wait_recv()
                        copy(4, diag, me, part=half_b).wait_recv()
                    passed[j].start()
                    load(slab((*chip, c)))

                other = (1, 0, 2)[j]

                @pl.when(p == 3 + 2 * j)
                def _():
                    copy(5 + axis_of[other], (*chips[other], 1 - c), me).wait_recv()
                    load(slab((*chips[other], 1 - c)))

        k = order_ref[p]
        y_k = lax.dot_general(h_ref[...], slab_vm[...], (((1,), (1,)), ((), ())), preferred_element_type=F32)
        o_ref[...] = pltpu.roll(y_k[:, 0:SLAB_WIN], k * sh, axis=1).astype(BF16)

        @pl.when(step == n_steps - 1)
        def _():
            for cp in first + relays + passed:
                cp.wait_send()
            mine.wait()

        _run_comm(comm, cin, cout, csem, step, n_steps, "end")

    outs = _pcall(
        body, name="mm_z", out_shape=tuple([_sds((t, n_pad), BF16), _sds((n_slab, rp, d), BF16)] + c_out_shape),
        grid_spec=pltpu.PrefetchScalarGridSpec(
            num_scalar_prefetch=1, grid=(n_slab, ni),
            in_specs=[pl.BlockSpec((tm, d), lambda p, i, order: (i, 0)), HBM_SPEC] + c_in_specs,
            out_specs=tuple([pl.BlockSpec((tm, SLAB_WIN), lambda p, i, order: (i, order[p])), HBM_SPEC] + [HBM_SPEC] * len(c_ins)),
            scratch_shapes=[pltpu.VMEM((rp, d), BF16), pltpu.SemaphoreType.DMA((8,)), pltpu.SemaphoreType.DMA((8,)),
                            pltpu.SemaphoreType.DMA, pltpu.SemaphoreType.DMA] + c_scratch),
        compiler_params=_cp(dimension_semantics=("arbitrary", "arbitrary")),
    )(order, h1, own_slab, *c_ins)
    return outs[0], outs[1], list(outs[2:])


def _z_fix(h1, slabs, z, *, t, d, rows, tm=4096):
    n_slab, rp, _ = slabs.shape
    tm = min(tm, t)
    sh = rows - SLAB_WIN
    tail_lo = rp - SLAB_TAIL
    assert rows - (n_slab - 1) * sh >= tail_lo and rp % SLAB_TAIL == 0

    def body(h_ref, t_ref, z_ref, o_ref):
        k = pl.program_id(0) + 1
        p = lax.dot_general(h_ref[...], t_ref[0], (((1,), (1,)), ((), ())), preferred_element_type=F32)
        pp = jnp.concatenate([p, jnp.zeros((tm, 128 - SLAB_TAIL), F32)], axis=1)
        pr = pltpu.roll(pp, 128 - (rows - tail_lo) + k * sh, axis=1)
        lane = lax.broadcasted_iota(jnp.int32, (tm, 128), 1)
        o_ref[...] = jnp.where(lane < k * sh, pr.astype(BF16), z_ref[...])

    blk = pl.BlockSpec((tm, 128), lambda b, i: (i, (SLAB_WIN // 128) * (b + 1)))
    return _pcall(
        body, name="mm_z_fix", out_shape=_sds(z.shape, z.dtype), grid=(n_slab - 1, t // tm),
        in_specs=[pl.BlockSpec((tm, d), lambda b, i: (i, 0)),
                  pl.BlockSpec((1, SLAB_TAIL, d), lambda b, i: (b, tail_lo // SLAB_TAIL, 0)), blk],
        out_specs=blk, input_output_aliases={2: 0}, compiler_params=_cp(),
    )(h1, slabs, z)


def _slab_window(a1_ref, a2_ref, k, sh):
    w = jnp.concatenate([a1_ref[...], a2_ref[...]], axis=1)
    width = SLAB_WIN + 128
    return pltpu.roll(w, width - k * sh, axis=1)


def _g_win_slabs(dz, h1, my_c, *, t, d, n_slab, rows, rp, tk=2048):
    sh = rows - SLAB_WIN
    tk = min(tk, t)
    nk = t // tk
    width = SLAB_WIN + 128
    half = n_slab // 2

    def slab_of(p, c):
        return 2 * (p % half) + jnp.where(p < half, 1 - c, c)

    def body(c_ref, a1_ref, a2_ref, h_ref, keep_ref, land_ref, acc, stage, send_sems, recv_sems):
        p, kk = pl.program_id(0), pl.program_id(1)
        x, y, c = _mesh_pos()
        ws = _slab_window(a1_ref, a2_ref, slab_of(p, c), sh)
        prod = lax.dot_general(ws, h_ref[...], (((0,), (0,)), ((), ())), preferred_element_type=F32)

        def push(q):
            return pltpu.make_async_remote_copy(
                src_ref=stage.at[q % 2], dst_ref=land_ref.at[q], send_sem=send_sems.at[q], recv_sem=recv_sems.at[q],
                device_id=(x, y, 1 - c), device_id_type=MESH)

        @pl.when(kk == 0)
        def _():
            acc[...] = prod

        @pl.when(kk > 0)
        def _():
            acc[...] += prod

        @pl.when(kk == nk - 1)
        def _():
            res = acc[0:rp, :].astype(BF16)
            for q in range(half):
                @pl.when(p == q)
                def _():
                    if q >= 2:
                        push(q - 2).wait_send()
                    stage[q % 2] = res
                    push(q).start()

            @pl.when(p >= half)
            def _():
                keep_ref[0] = res

        @pl.when((p == n_slab - 1) & (kk == nk - 1))
        def _():
            for q in range(max(half - 2, 0), half):
                push(q).wait_send()
            for q in range(half):
                push(q).wait_recv()

    outs = _pcall(
        body, name="mm_g_win", out_shape=(_sds((half, rp, d), BF16), _sds((half, rp, d), BF16)),
        grid_spec=pltpu.PrefetchScalarGridSpec(
            num_scalar_prefetch=1, grid=(n_slab, nk),
            in_specs=[pl.BlockSpec((tk, SLAB_WIN), lambda p, kk, c: (kk, slab_of(p, c[0]))),
                      pl.BlockSpec((tk, 128), lambda p, kk, c: (kk, (SLAB_WIN // 128) * (slab_of(p, c[0]) + 1))),
                      pl.BlockSpec((tk, d), lambda p, kk, c: (kk, 0))],
            out_specs=(pl.BlockSpec((1, rp, d), lambda p, kk, c: (jnp.maximum(p - half, 0), 0, 0)), HBM_SPEC),
            scratch_shapes=[pltpu.VMEM((width, d), F32), pltpu.VMEM((2, rp, d), BF16),
                            pltpu.SemaphoreType.DMA((half,)), pltpu.SemaphoreType.DMA((half,))]),
        compiler_params=_cp(dimension_semantics=("arbitrary", "arbitrary")),
    )(my_c.reshape(1).astype(jnp.int32), dz, dz, h1)
    return outs[0], outs[1]


def _dh1_from_slabs(dz, slabs, *, t, d, rows, comm=None, tm=1024):
    n_slab, rp, _ = slabs.shape
    sh = rows - SLAB_WIN
    width = SLAB_WIN + 128
    c_ins, c_in_specs, c_out_shape, c_scratch = _comm_lists(comm)
    grid = (t // tm, n_slab)

    def body(*refs):
        _, ins, cin, outs, cout, scr, csem = _split_refs(refs, 0, 3, len(c_ins), 1, len(c_ins), 1)
        a1_ref, a2_ref, s_ref = ins
        o_ref, acc = outs[0], scr[0]
        k = pl.program_id(1)
        step = pl.program_id(0) * n_slab + k
        _run_comm(comm, cin, cout, csem, step, grid[0] * n_slab, "start")
        ws = _slab_window(a1_ref, a2_ref, k, sh)
        s_end = jnp.concatenate([s_ref[0, SLAB_WIN:rp, :], jnp.zeros((width - rp, d), BF16)], axis=0)
        p = (jnp.dot(ws[:, 0:SLAB_WIN], s_ref[0, 0:SLAB_WIN, :], preferred_element_type=F32)
             + jnp.dot(ws[:, SLAB_WIN:width], s_end, preferred_element_type=F32))

        @pl.when(k == 0)
        def _():
            acc[...] = p

        @pl.when(k > 0)
        def _():
            acc[...] += p

        @pl.when(k == n_slab - 1)
        def _():
            o_ref[...] = acc[...]

        _run_comm(comm, cin, cout, csem, step, grid[0] * n_slab, "end")

    outs = _pcall(
        body, name="mm_dh1", out_shape=tuple([_sds((t, d), F32)] + c_out_shape), grid=grid,
        in_specs=[pl.BlockSpec((tm, SLAB_WIN), lambda i, k: (i, k)),
                  pl.BlockSpec((tm, 128), lambda i, k: (i, (SLAB_WIN // 128) * (k + 1))),
                  pl.BlockSpec((1, rp, d), lambda i, k: (k, 0, 0))] + c_in_specs,
        out_specs=tuple([pl.BlockSpec((tm, d), lambda i, k: (i, 0))] + [HBM_SPEC] * len(c_ins)),
        scratch_shapes=[pltpu.VMEM((tm, d), F32)] + c_scratch,
        compiler_params=_cp(dimension_semantics=("arbitrary", "arbitrary")),
    )(dz, dz, slabs, *c_ins)
    return outs[0], list(outs[1:])


def _rms_fwd(x, g, *, name, tm=512):
    t, d = x.shape
    tm = min(tm, t)

    def body(x_ref, g_ref, h_ref):
        xv = x_ref[...]
        r = lax.rsqrt(jnp.mean(xv * xv, axis=-1, keepdims=True) + EPS)
        h_ref[...] = (xv * r * g_ref[...]).astype(BF16)

    return _pcall(
        body, name=name, out_shape=_sds((t, d), BF16), grid=(t // tm,),
        in_specs=[pl.BlockSpec((tm, d), lambda i: (i, 0)), pl.BlockSpec((1, d), lambda i: (0, 0))],
        out_specs=pl.BlockSpec((tm, d), lambda i: (i, 0)), compiler_params=_cp(),
    )(x, g)


def _rms_bwd(dh, x, g, res, *, name, tm=512):
    t, d = x.shape
    tm = min(tm, t)
    has_res = res is not None

    def body(*refs):
        dh_ref, x_ref, g_ref = refs[:3]
        r_ref = refs[3] if has_res else None
        dx_ref, dg_ref = refs[-2], refs[-1]
        xv = x_ref[...]
        r = lax.rsqrt(jnp.mean(xv * xv, axis=-1, keepdims=True) + EPS)
        xh = xv * r
        dhv = dh_ref[...]
        gd = dhv * g_ref[...]
        dx = r * (gd - xh * jnp.mean(gd * xh, axis=-1, keepdims=True))
        if has_res:
            dx = dx + r_ref[...]
        dx_ref[...] = dx

        @pl.when(pl.program_id(0) == 0)
        def _():
            dg_ref[...] = jnp.zeros_like(dg_ref)

        dg_ref[...] += jnp.sum(dhv * xh, axis=0, keepdims=True)

    row = pl.BlockSpec((tm, d), lambda i: (i, 0))
    vec = pl.BlockSpec((1, d), lambda i: (0, 0))
    return _pcall(
        body, name=name, out_shape=(_sds((t, d), F32), _sds((1, d), F32)), grid=(t // tm,),
        in_specs=[row, row, vec] + ([row] if has_res else []), out_specs=(row, vec),
        compiler_params=_cp(dimension_semantics=("arbitrary",)),
    )(*([dh, x, g] + ([res] if has_res else [])))


def _conv_fwd(z, conv_w, conv_b, *, nb, s, d, tc=256):
    nc = d // tc

    def body(cb_ref, cc_ref, cv_ref, w_ref, b_ref, y_ref):
        u = cc_ref[...].astype(F32) * cv_ref[...].astype(F32)
        row = lax.broadcasted_iota(jnp.int32, u.shape, 0)
        u1 = jnp.where(row >= 1, pltpu.roll(u, 1, axis=0), 0.0)
        u2 = jnp.where(row >= 2, pltpu.roll(u, 2, axis=0), 0.0)
        w = w_ref[...]
        y = w[2:3, :] * u + w[1:2, :] * u1 + w[0:1, :] * u2 + b_ref[...]
        y_ref[...] = (cb_ref[...].astype(F32) * y).astype(BF16)

    def part(p):
        return pl.BlockSpec((s, tc), lambda b, j: (b, p * nc + j))

    return _pcall(
        body, name="conv_fwd", out_shape=_sds((nb * s, d), BF16), grid=(nb, nc),
        in_specs=[part(0), part(1), part(2), pl.BlockSpec((3, tc), lambda b, j: (0, j)), pl.BlockSpec((1, tc), lambda b, j: (0, j))],
        out_specs=pl.BlockSpec((s, tc), lambda b, j: (b, j)), compiler_params=_cp(),
    )(z, z, z, conv_w, conv_b)


def _conv_bwd(z, dy, conv_w, conv_b, dz, *, nb, s, d, tc=256, comm=None):
    nc = d // tc
    c_ins, c_in_specs, c_out_shape, c_scratch = _comm_lists(comm)
    n_steps = nc * nb * 3

    def body(*refs):
        _, ins, cin, outs, cout, scr, csem = _split_refs(refs, 0, 7, len(c_ins), 3, len(c_ins), 1)
        cb_ref, cc_ref, cv_ref, dy_ref, w_ref, b_ref, _ = ins
        dz_ref, dw_ref, db_ref = outs
        stash = scr[0]
        b_i, p = pl.program_id(1), pl.program_id(2)
        step = (pl.program_id(0) * nb + b_i) * 3 + p
        _run_comm(comm, cin, cout, csem, step, n_steps, "start")

        @pl.when(p == 0)
        def _():
            cc, cv = cc_ref[...].astype(F32), cv_ref[...].astype(F32)
            u = cc * cv
            n = u.shape[0]
            row = lax.broadcasted_iota(jnp.int32, u.shape, 0)
            u1 = jnp.where(row >= 1, pltpu.roll(u, 1, axis=0), 0.0)
            u2 = jnp.where(row >= 2, pltpu.roll(u, 2, axis=0), 0.0)
            w = w_ref[...]
            y = w[2:3, :] * u + w[1:2, :] * u1 + w[0:1, :] * u2 + b_ref[...]
            dyv = dy_ref[...]
            dconv = dyv * cb_ref[...].astype(F32)
            g1 = jnp.where(row < n - 1, pltpu.roll(dconv, n - 1, axis=0), 0.0)
            g2 = jnp.where(row < n - 2, pltpu.roll(dconv, n - 2, axis=0), 0.0)
            du = w[2:3, :] * dconv + w[1:2, :] * g1 + w[0:1, :] * g2
            stash[0] = (dyv * y).astype(BF16)
            stash[1] = (du * cv).astype(BF16)
            stash[2] = (du * cc).astype(BF16)
            dws = [jnp.sum(dconv * u2, axis=0, keepdims=True), jnp.sum(dconv * u1, axis=0, keepdims=True),
                   jnp.sum(dconv * u, axis=0, keepdims=True)]
            db = jnp.sum(dconv, axis=0, keepdims=True)

            @pl.when(b_i == 0)
            def _():
                for r in range(3):
                    dw_ref[r:r + 1, :] = dws[r]
                db_ref[...] = db

            @pl.when(b_i > 0)
            def _():
                for r in range(3):
                    dw_ref[r:r + 1, :] += dws[r]
                db_ref[...] += db

        dz_ref[...] = stash[p]
        _run_comm(comm, cin, cout, csem, step, n_steps, "end")

    outs = _pcall(
        body, name="conv_bwd",
        out_shape=tuple([_sds(dz.shape, dz.dtype), _sds((3, d), F32), _sds((1, d), F32)] + c_out_shape), grid=(nc, nb, 3),
        in_specs=[
            pl.BlockSpec((s, tc), lambda j, b, p: (b, j)), pl.BlockSpec((s, tc), lambda j, b, p: (b, nc + j)),
            pl.BlockSpec((s, tc), lambda j, b, p: (b, 2 * nc + j)), pl.BlockSpec((s, tc), lambda j, b, p: (b, j)),
            pl.BlockSpec((3, tc), lambda j, b, p: (0, j)), pl.BlockSpec((1, tc), lambda j, b, p: (0, j)),
            HBM_SPEC,
        ] + c_in_specs,
        out_specs=tuple([pl.BlockSpec((s, tc), lambda j, b, p: (b, p * nc + j)),
                         pl.BlockSpec((3, tc), lambda j, b, p: (0, j)), pl.BlockSpec((1, tc), lambda j, b, p: (0, j))]
                        + [HBM_SPEC] * len(c_ins)),
        scratch_shapes=[pltpu.VMEM((3, s, tc), BF16)] + c_scratch,
        input_output_aliases={6: 0},
        compiler_params=_cp(dimension_semantics=("arbitrary", "arbitrary", "arbitrary")),
    )(z, z, z, dy, conv_w, conv_b, dz, *c_ins)
    return outs[0], outs[1], outs[2], list(outs[3:])


def _head_rms(xv, g):
    r = lax.rsqrt(jnp.mean(xv * xv, axis=-1, keepdims=True) + EPS)
    return xv * r * g


def _head_rms_bwd(xv, g, dy):
    r = lax.rsqrt(jnp.mean(xv * xv, axis=-1, keepdims=True) + EPS)
    xh = xv * r
    gd = dy * g
    dx = r * (gd - xh * jnp.mean(gd * xh, axis=-1, keepdims=True))
    return dx, jnp.sum(dy * xh, axis=0, keepdims=True)


def _fox_prep(z, q_g, k_g, *, t, d, tm=512):
    nh = d // HD

    def body(z_ref, qg_ref, kg_ref, q_ref, k_ref, v_ref):
        qg, kg = qg_ref[...], kg_ref[...]
        for h in range(nh):
            c0 = h * HD
            q_ref[:, c0:c0 + HD] = _head_rms(z_ref[:, c0:c0 + HD].astype(F32), qg).astype(BF16)
            k_ref[:, c0:c0 + HD] = _head_rms(z_ref[:, d + c0:d + c0 + HD].astype(F32), kg).astype(BF16)
        v_ref[...] = z_ref[:, 2 * d:3 * d]

    o = pl.BlockSpec((tm, d), lambda i: (i, 0))
    g = pl.BlockSpec((1, HD), lambda i: (0, 0))
    return _pcall(
        body, name="fox_prep", out_shape=(_sds((t, d), BF16),) * 3, grid=(t // tm,),
        in_specs=[pl.BlockSpec((tm, 3 * d), lambda i: (i, 1)), g, g], out_specs=(o, o, o), compiler_params=_cp(),
    )(z, q_g, k_g)


def _lane_cumsum(v, reverse=False):
    n = v.shape[1]
    lane = lax.broadcasted_iota(jnp.int32, v.shape, 1)
    sh = 1
    while sh < n:
        if reverse:
            v = v + jnp.where(lane < n - sh, pltpu.roll(v, n - sh, axis=1), 0.0)
        else:
            v = v + jnp.where(lane >= sh, pltpu.roll(v, sh, axis=1), 0.0)
        sh *= 2
    return v


def _fox_gates(z, f_bias_col, *, nb, s, nh):
    def body(z_ref, b_ref, c_ref):
        ff = z_ref[...].T[0:nh, :] + b_ref[...]
        lf = jnp.minimum(ff, 0.0) - jnp.log(1.0 + jnp.exp(-jnp.abs(ff)))
        c_ref[0] = _lane_cumsum(lf) * SQRT_HD

    return _pcall(
        body, name="fox_gates", out_shape=_sds((nb, nh, s), F32), grid=(nb,),
        in_specs=[pl.BlockSpec((s, HD), lambda b: (b, 0)), pl.BlockSpec((nh, 1), lambda b: (0, 0))],
        out_specs=pl.BlockSpec((1, nh, s), lambda b: (b, 0, 0)), compiler_params=_cp(),
    )(z, f_bias_col)


def _fox_gates_bwd(z, f_bias_col, row_sums, col_sums, dz, *, nb, s, nh, ff_block):
    def body(z_ref, b_ref, rs_ref, cs_ref, dz_in, dz_ref, db_ref, dc_sc):
        b_i, h = pl.program_id(0), pl.program_id(1)
        dc_sc[pl.ds(h, 1), :] = (rs_ref[...] - cs_ref[...]).T[0:1, :]

        @pl.when(h == nh - 1)
        def _():
            ff = z_ref[...].T[0:nh, :] + b_ref[...]
            dlf = _lane_cumsum(dc_sc[...], reverse=True)
            dff = dlf * (1.0 / (1.0 + jnp.exp(ff)))
            pad = jnp.concatenate([dff, jnp.zeros((HD - nh, s), F32)], axis=0)
            dz_ref[...] = pad.T.astype(BF16)
            dbv = jnp.sum(dff, axis=1, keepdims=True)

            @pl.when(b_i == 0)
            def _():
                db_ref[...] = dbv

            @pl.when(b_i > 0)
            def _():
                db_ref[...] += dbv

    hs = pl.BlockSpec((s, HD), lambda b, h: (b, h))
    return _pcall(
        body, name="fox_gates_bwd", out_shape=(_sds(dz.shape, dz.dtype), _sds((nh, 1), F32)), grid=(nb, nh),
        in_specs=[pl.BlockSpec((s, HD), lambda b, h: (b, 0)), pl.BlockSpec((nh, 1), lambda b, h: (0, 0)), hs, hs, HBM_SPEC],
        out_specs=(pl.BlockSpec((s, HD), lambda b, h: (b, ff_block)), pl.BlockSpec((nh, 1), lambda b, h: (0, 0))),
        scratch_shapes=[pltpu.VMEM((nh, s), F32)],
        input_output_aliases={4: 0}, compiler_params=_cp(dimension_semantics=("arbitrary", "arbitrary")),
    )(z, f_bias_col, row_sums, col_sums, dz)


def _fox_pairs(nq, key_major):
    if key_major:
        pairs = [(i, j) for j in range(nq) for i in range(j, nq)]
    else:
        pairs = [(i, j) for i in range(nq) for j in range(i + 1)]
    return jnp.array([p[0] for p in pairs], jnp.int32), jnp.array([p[1] for p in pairs], jnp.int32)


def _with_ones(x):
    return jnp.concatenate([x, jnp.ones_like(x)], axis=1)


def _fox_specs(nb, nh, nq, tq, tk, hp):
    qs = pl.BlockSpec((tq, hp * HD), lambda b, h, p, *tb: (b * nq + tb[0][p], h))
    ks = pl.BlockSpec((tk, hp * HD), lambda b, h, p, *tb: (b * nq + tb[1][p], h))
    cs = pl.BlockSpec((hp, 1, tk), lambda b, h, p, *tb: (b * (nh // hp) + h, 0, tb[1][p]))
    return qs, ks, cs


def _fox_raw(qv, kv, ckv, diag, tq, tk):
    sc = lax.dot_general(qv, kv, (((1,), (1,)), ((), ())), preferred_element_type=F32) - ckv
    if diag:
        rows = lax.broadcasted_iota(jnp.int32, (tq, tk), 0)
        cols = lax.broadcasted_iota(jnp.int32, (tq, tk), 1)
        sc = jnp.where(rows >= cols, sc, NEG)
    return sc


def _fox_probs(qv, kv, ckv, lse_col, diag, tq, tk):
    sc = _fox_raw(qv, kv, ckv, diag, tq, tk)
    return jnp.exp2(sc * EXP2_C - lse_col * LOG2E)


def _fox_call(body, *, name, ins, in_specs, out_shape, out_specs, scratch, grid, tables, comm):
    c_ins, c_in_specs, c_out_shape, c_scratch = _comm_lists(comm)
    n_in, n_out, n_scr, nc = len(ins), len(out_shape), len(scratch), len(c_ins)
    n_steps = grid[0] * grid[1] * grid[2]

    def wrapped(*refs):
        pre, rin, cin, rout, cout, scr, csem = _split_refs(refs, len(tables), n_in, nc, n_out, nc, n_scr)
        step = (pl.program_id(0) * grid[1] + pl.program_id(1)) * grid[2] + pl.program_id(2)
        _run_comm(comm, cin, cout, csem, step, n_steps, "start")
        body(*pre, *rin, *rout, *scr)
        _run_comm(comm, cin, cout, csem, step, n_steps, "end")

    sem = ("arbitrary",) * 3 if comm is not None else ("parallel", "parallel", "arbitrary")
    outs = _pcall(
        wrapped, name=name, out_shape=tuple(list(out_shape) + c_out_shape),
        grid_spec=pltpu.PrefetchScalarGridSpec(
            num_scalar_prefetch=len(tables), grid=grid, in_specs=list(in_specs) + c_in_specs,
            out_specs=tuple(list(out_specs) + [HBM_SPEC] * nc), scratch_shapes=list(scratch) + c_scratch),
        compiler_params=_cp(dimension_semantics=sem),
    )(*tables, *ins, *c_ins)
    return list(outs[:n_out]), list(outs[n_out:])


def _fox_fwd(q, k, v, cks, *, nb, s, nh, tq=512, comm=None):
    tk = tq
    nq = s // tq
    t = nb * s
    hp = FOX_HP
    it, jt = _fox_pairs(nq, key_major=False)

    def body(it_ref, jt_ref, q_ref, k_ref, v_ref, c_ref, o_ref, lse_ref, m_sc, acc_sc):
        p_id = pl.program_id(2)
        i, j = it_ref[p_id], jt_ref[p_id]

        @pl.when(j == 0)
        def _():
            m_sc[...] = jnp.full_like(m_sc, NEG)
            acc_sc[...] = jnp.zeros_like(acc_sc)

        def step(diag):
            for hh in range(hp):
                cols = slice(hh * HD, (hh + 1) * HD)
                sc = _fox_raw(q_ref[:, cols], k_ref[:, cols], c_ref[hh], diag, tq, tk)
                m_old = m_sc[hh]
                m_new = jnp.maximum(m_old, jnp.max(sc, axis=-1, keepdims=True))
                alpha = jnp.exp2((m_old - m_new) * EXP2_C)
                p = jnp.exp2((sc - m_new) * EXP2_C).astype(BF16)
                acc = alpha * acc_sc[hh] + jnp.dot(p, _with_ones(v_ref[:, cols]), preferred_element_type=F32)
                if diag:
                    l = acc[:, HD:2 * HD]
                    o_ref[:, cols] = (acc[:, 0:HD] / l).astype(BF16)
                    lse_ref[:, cols] = m_new * (1.0 / SQRT_HD) + jnp.log(l)
                else:
                    acc_sc[hh] = acc
                    m_sc[hh] = m_new

        @pl.when(j < i)
        def _():
            step(False)

        @pl.when(j == i)
        def _():
            step(True)

    qs, ks, cs = _fox_specs(nb, nh, nq, tq, tk, hp)
    (o, lse), c_out = _fox_call(
        body, name="fox_fwd", ins=[q, k, v, cks], in_specs=[qs, ks, ks, cs],
        out_shape=[_sds((t, nh * HD), BF16), _sds((t, nh * HD), F32)], out_specs=[qs, qs],
        scratch=[pltpu.VMEM((hp, tq, 1), F32), pltpu.VMEM((hp, tq, 2 * HD), F32)],
        grid=(nb, nh // hp, int(it.shape[0])), tables=(it, jt), comm=comm)
    return o, lse, c_out


def _fox_bwd(q, k, v, cks, do, o, lse, *, nb, s, nh, tq=512, comm=None):
    tk = tq
    nq = s // tq
    t = nb * s
    hp = FOX_HP
    it, jt = _fox_pairs(nq, key_major=True)
    done = jnp.array([j for j in range(nq) for _ in range(j, nq)], jnp.int32)

    def body(it_ref, jt_ref, done_ref, q_ref, k_ref, v_ref, c_ref, do_ref, o_ref, lse_ref,
             dq_ref, rs_ref, dk_ref, dv_ref, cs_ref, dq_sc, dk_sc, dv_sc):
        p_id = pl.program_id(2)
        i, j = it_ref[p_id], jt_ref[p_id]
        rows_i = pl.ds(pl.multiple_of(i * tq, tq), tq)

        def step(diag):
            for hh in range(hp):
                cols = slice(hh * HD, (hh + 1) * HD)
                qv, kv = q_ref[:, cols], k_ref[:, cols]
                p = _fox_probs(qv, kv, c_ref[hh], lse_ref[:, hh * HD:hh * HD + 1], diag, tq, tk)
                dov = do_ref[:, cols]
                dob = dov.astype(BF16)
                dvp = lax.dot_general(p.astype(BF16), dob, (((0,), (0,)), ((), ())), preferred_element_type=F32)
                dp = lax.dot_general(dob, v_ref[:, cols], (((1,), (1,)), ((), ())), preferred_element_type=F32)
                delta = jnp.sum(dov * o_ref[:, cols].astype(F32), axis=-1, keepdims=True)
                ds = (p * (dp - delta)).astype(BF16)
                dkp = lax.dot_general(ds, _with_ones(qv), (((0,), (0,)), ((), ())), preferred_element_type=F32)
                dqp = jnp.dot(ds, _with_ones(kv), preferred_element_type=F32)
                if diag:
                    dk_sc[hh] = dkp
                    dv_sc[hh] = dvp
                else:
                    dk_sc[hh] += dkp
                    dv_sc[hh] += dvp

                @pl.when(j == 0)
                def _():
                    dq_sc[hh, rows_i, :] = dqp

                @pl.when(j > 0)
                def _():
                    dq_sc[hh, rows_i, :] += dqp

                if diag:
                    acc = dq_sc[hh, rows_i, :]
                    dq_ref[:, cols] = acc[:, 0:HD] * (1.0 / SQRT_HD)
                    rs_ref[:, cols] = acc[:, HD:2 * HD]

        @pl.when(i == j)
        def _():
            step(True)

        @pl.when(i > j)
        def _():
            step(False)

        @pl.when(i == nq - 1)
        def _():
            for hh in range(hp):
                cols = slice(hh * HD, (hh + 1) * HD)
                dk_ref[:, cols] = dk_sc[hh, :, 0:HD] * (1.0 / SQRT_HD)
                cs_ref[:, cols] = dk_sc[hh, :, HD:2 * HD]
                dv_ref[:, cols] = dv_sc[hh].astype(BF16)

    qs, ks, cs = _fox_specs(nb, nh, nq, tq, tk, hp)
    ds_spec = pl.BlockSpec((tq, hp * HD), lambda b, h, p, *tb: (b * nq + tb[2][p], h))
    (dq, rs, dk, dv, csum), c_out = _fox_call(
        body, name="fox_bwd", ins=[q, k, v, cks, do, o, lse], in_specs=[qs, ks, ks, cs, qs, qs, qs],
        out_shape=[_sds((t, nh * HD), F32), _sds((t, nh * HD), F32), _sds((t, nh * HD), F32), _sds((t, nh * HD), BF16),
                   _sds((t, nh * HD), F32)],
        out_specs=[ds_spec, ds_spec, ks, ks, ks],
        scratch=[pltpu.VMEM((hp, s, 2 * HD), F32), pltpu.VMEM((hp, tk, 2 * HD), F32), pltpu.VMEM((hp, tk, HD), F32)],
        grid=(nb, nh // hp, int(it.shape[0])), tables=(it, jt, done), comm=comm)
    return dq, rs, dk, dv, csum, c_out


def _fox_post(z, dqn, dkn, dv, q_g, k_g, dz, *, t, d, tm=512):
    nh = d // HD

    def body(z_ref, dq_ref, dk_ref, dv_ref, qg_ref, kg_ref, dz_in, dz_ref, dqg_ref, dkg_ref):
        qg, kg = qg_ref[...], kg_ref[...]
        aq = jnp.zeros((1, HD), F32)
        ak = jnp.zeros((1, HD), F32)
        for h in range(nh):
            c0 = h * HD
            dx, sg = _head_rms_bwd(z_ref[:, c0:c0 + HD].astype(F32), qg, dq_ref[:, c0:c0 + HD])
            dz_ref[:, c0:c0 + HD] = dx.astype(BF16)
            aq = aq + sg
            dx, sg = _head_rms_bwd(z_ref[:, d + c0:d + c0 + HD].astype(F32), kg, dk_ref[:, c0:c0 + HD])
            dz_ref[:, d + c0:d + c0 + HD] = dx.astype(BF16)
            ak = ak + sg
        dz_ref[:, 2 * d:3 * d] = dv_ref[...]

        @pl.when(pl.program_id(0) == 0)
        def _():
            dqg_ref[...] = aq
            dkg_ref[...] = ak

        @pl.when(pl.program_id(0) > 0)
        def _():
            dqg_ref[...] += aq
            dkg_ref[...] += ak

    o = pl.BlockSpec((tm, d), lambda i: (i, 0))
    g = pl.BlockSpec((1, HD), lambda i: (0, 0))
    z3 = pl.BlockSpec((tm, 3 * d), lambda i: (i, 1))
    return _pcall(
        body, name="fox_post", out_shape=(_sds(dz.shape, dz.dtype), _sds((1, HD), F32), _sds((1, HD), F32)), grid=(t // tm,),
        in_specs=[z3, o, o, o, g, g, HBM_SPEC], out_specs=(z3, g, g),
        input_output_aliases={6: 0}, compiler_params=_cp(dimension_semantics=("arbitrary",)),
    )(z, dqn, dkn, dv, q_g, k_g, dz)


def _xa_prep_kv(kv, k_g, *, rows, d):
    xd = d // XH

    def body(kv_ref, g_ref, k_ref, v_ref):
        g = g_ref[...]
        for h in range(XH):
            c0 = h * xd
            k_ref[:, c0:c0 + xd] = _head_rms(kv_ref[:, c0:c0 + xd], g).astype(BF16)
        v_ref[...] = kv_ref[:, d:2 * d].astype(BF16)

    return _pcall(body, name="xa_prep_kv", out_shape=(_sds((rows, d), BF16),) * 2, compiler_params=_cp())(kv, k_g)


def _xa_fwd(z, kn, vb, q_g, *, nb, s, d, nm, q_block, tq=512):
    xd = d // XH
    nq = s // tq

    def body(z_ref, k_ref, v_ref, g_ref, y_ref):
        g = g_ref[...]
        for h in range(XH):
            c0 = h * xd
            qn = _head_rms(z_ref[:, c0:c0 + xd].astype(F32), g).astype(BF16)
            sc = lax.dot_general(qn, k_ref[:, c0:c0 + xd], (((1,), (1,)), ((), ())), preferred_element_type=F32)
            sc = sc / math.sqrt(xd)
            p = jnp.exp(sc - jnp.max(sc, axis=-1, keepdims=True))
            p = p / jnp.sum(p, axis=-1, keepdims=True)
            y_ref[:, c0:c0 + xd] = jnp.dot(p.astype(BF16), v_ref[:, c0:c0 + xd], preferred_element_type=F32).astype(BF16)

    kvs = pl.BlockSpec((nm, d), lambda b, i: (b, 0))
    return _pcall(
        body, name="xa_fwd", out_shape=_sds((nb * s, d), BF16), grid=(nb, nq),
        in_specs=[pl.BlockSpec((tq, d), lambda b, i: (b * nq + i, q_block)), kvs, kvs, pl.BlockSpec((1, xd), lambda b, i: (0, 0))],
        out_specs=pl.BlockSpec((tq, d), lambda b, i: (b * nq + i, 0)), compiler_params=_cp(),
    )(z, kn, vb, q_g)


def _xa_bwd(z, kn, vb, q_g, dy, dz, *, nb, s, d, nm, q_block, tq=512):
    xd = d // XH
    nq = s // tq

    def body(z_ref, k_ref, v_ref, g_ref, dy_ref, dz_in, dz_ref, dk_ref, dv_ref, dg_ref):
        b_i, i = pl.program_id(0), pl.program_id(1)
        g = g_ref[...]

        @pl.when(i == 0)
        def _():
            dk_ref[...] = jnp.zeros_like(dk_ref)
            dv_ref[...] = jnp.zeros_like(dv_ref)

        @pl.when((i == 0) & (b_i == 0))
        def _():
            dg_ref[...] = jnp.zeros_like(dg_ref)

        for h in range(XH):
            c0 = h * xd
            xq = z_ref[:, c0:c0 + xd].astype(F32)
            qn = _head_rms(xq, g).astype(BF16)
            kh, vh = k_ref[:, c0:c0 + xd], v_ref[:, c0:c0 + xd]
            sc = lax.dot_general(qn, kh, (((1,), (1,)), ((), ())), preferred_element_type=F32) / math.sqrt(xd)
            p = jnp.exp(sc - jnp.max(sc, axis=-1, keepdims=True))
            p = p / jnp.sum(p, axis=-1, keepdims=True)
            dob = dy_ref[:, c0:c0 + xd].astype(BF16)
            dv_ref[:, c0:c0 + xd] += lax.dot_general(p.astype(BF16), dob, (((0,), (0,)), ((), ())), preferred_element_type=F32)
            dp = lax.dot_general(dob, vh, (((1,), (1,)), ((), ())), preferred_element_type=F32)
            ds = p * (dp - jnp.sum(p * dp, axis=-1, keepdims=True)) / math.sqrt(xd)
            dsb = ds.astype(BF16)
            dqn = jnp.dot(dsb, kh, preferred_element_type=F32)
            dk_ref[:, c0:c0 + xd] += lax.dot_general(dsb, qn, (((0,), (0,)), ((), ())), preferred_element_type=F32)
            dx, sg = _head_rms_bwd(xq, g, dqn)
            dz_ref[:, c0:c0 + xd] = dx.astype(BF16)
            dg_ref[...] += sg

    kvs = pl.BlockSpec((nm, d), lambda b, i: (b, 0))
    qs = pl.BlockSpec((tq, d), lambda b, i: (b * nq + i, q_block))
    gs = pl.BlockSpec((1, xd), lambda b, i: (0, 0))
    return _pcall(
        body, name="xa_bwd",
        out_shape=(_sds(dz.shape, dz.dtype), _sds((nb * nm, d), F32), _sds((nb * nm, d), F32), _sds((1, xd), F32)), grid=(nb, nq),
        in_specs=[qs, kvs, kvs, gs, pl.BlockSpec((tq, d), lambda b, i: (b * nq + i, 0)), HBM_SPEC],
        out_specs=(qs, kvs, kvs, gs), input_output_aliases={5: 0},
        compiler_params=_cp(dimension_semantics=("arbitrary", "arbitrary")),
    )(z, kn, vb, q_g, dy, dz)


def _xa_post_kv(kv, dkn, dv, k_g, *, rows, d):
    xd = d // XH

    def body(kv_ref, dk_ref, dv_ref, g_ref, dkv_ref, dg_ref):
        g = g_ref[...]
        acc = jnp.zeros((1, xd), F32)
        for h in range(XH):
            c0 = h * xd
            dx, sg = _head_rms_bwd(kv_ref[:, c0:c0 + xd], g, dk_ref[:, c0:c0 + xd])
            dkv_ref[:, c0:c0 + xd] = dx.astype(BF16)
            acc = acc + sg
        dkv_ref[:, d:2 * d] = dv_ref[...].astype(BF16)
        dg_ref[...] = acc

    return _pcall(body, name="xa_post_kv", out_shape=(_sds((rows, 2 * d), BF16), _sds((1, xd), F32)), compiler_params=_cp())(
        kv, dkn, dv, k_g)


def _sigmoid(v):
    return 1.0 / (1.0 + jnp.exp(-v))


def _branches_fwd(z, ys, ws, *, t, d, gate_block, tm=512, tn=512, comm=None):
    nj = d // tn
    c_ins, c_in_specs, c_out_shape, c_scratch = _comm_lists(comm)
    n_steps = (t // tm) * nj

    def body(*refs):
        _, ins, cin, outs, cout, _, csem = _split_refs(refs, 0, 9, len(c_ins), 2, len(c_ins), 0)
        ya_ref, yb_ref, yc_ref, wa_ref, wb_ref, wc_ref, ga_ref, gb_ref, gc_ref = ins
        m_ref, p_ref = outs
        step = pl.program_id(0) * nj + pl.program_id(1)
        _run_comm(comm, cin, cout, csem, step, n_steps, "start")
        acc = None
        for q, (y_ref, w_ref, g_ref) in enumerate(((ya_ref, wa_ref, ga_ref), (yb_ref, wb_ref, gb_ref), (yc_ref, wc_ref, gc_ref))):
            pr = jnp.dot(y_ref[...], w_ref[...], preferred_element_type=F32)
            p_ref[q] = pr.astype(BF16)
            term = _sigmoid(g_ref[...].astype(F32)) * pr
            acc = term if acc is None else acc + term
        m_ref[...] = acc.astype(BF16)
        _run_comm(comm, cin, cout, csem, step, n_steps, "end")

    y_spec = pl.BlockSpec((tm, d), lambda i, j: (i, 0))
    w_spec = pl.BlockSpec((d, tn), lambda i, j: (0, j))

    def gate(q):
        return pl.BlockSpec((tm, tn), lambda i, j: (i, (gate_block + q) * nj + j))

    outs = _pcall(
        body, name="branches_fwd", out_shape=tuple([_sds((t, d), BF16), _sds((3, t, d), BF16)] + c_out_shape), grid=(t // tm, nj),
        in_specs=[y_spec] * 3 + [w_spec] * 3 + [gate(0), gate(1), gate(2)] + c_in_specs,
        out_specs=tuple([pl.BlockSpec((tm, tn), lambda i, j: (i, j)), pl.BlockSpec((3, tm, tn), lambda i, j: (0, i, j))]
                        + [HBM_SPEC] * len(c_ins)),
        scratch_shapes=c_scratch, compiler_params=_cp(dimension_semantics=("arbitrary", "arbitrary")),
    )(*ys, *ws, z, z, z, *c_ins)
    return outs[0], outs[1], list(outs[2:])


def _gates_bwd(z, dm, proj, dz, *, t, d, gate_block, tm=512):
    def body(g_ref, dm_ref, p_ref, dz_in, dz_ref, da_ref, db_ref, dc_ref):
        q = pl.program_id(1)
        sg = _sigmoid(g_ref[...].astype(F32))
        dmv = dm_ref[...]
        dz_ref[...] = (dmv * p_ref[0].astype(F32) * sg * (1.0 - sg)).astype(BF16)
        dp = (dmv * sg).astype(BF16)
        @pl.when(q == 0)
        def _():
            da_ref[...] = dp

        @pl.when(q == 1)
        def _():
            db_ref[...] = dp

        @pl.when(q == 2)
        def _():
            dc_ref[...] = dp

    gs = pl.BlockSpec((tm, d), lambda i, q: (i, gate_block + q))
    o = pl.BlockSpec((tm, d), lambda i, q: (i, 0))
    return _pcall(
        body, name="gates_bwd", out_shape=(_sds(dz.shape, dz.dtype),) + (_sds((t, d), BF16),) * 3, grid=(t // tm, 3),
        in_specs=[gs, o, pl.BlockSpec((1, tm, d), lambda i, q: (q, i, 0)), HBM_SPEC], out_specs=(gs, o, o, o),
        input_output_aliases={3: 0}, compiler_params=_cp(dimension_semantics=("arbitrary", "arbitrary")),
    )(z, dm, proj, dz)


def _ffn_in_fwd(h2, wfi_t, *, t, d, dff, tm=512, tn=1408):
    nj = dff // tn

    def body(h_ref, wg_ref, wu_ref, g_ref, u_ref, a_ref):
        hv = h_ref[...]
        dn = (((1,), (1,)), ((), ()))
        gv = lax.dot_general(hv, wg_ref[...], dn, preferred_element_type=F32)
        uv = lax.dot_general(hv, wu_ref[...], dn, preferred_element_type=F32)
        g_ref[...] = gv.astype(BF16)
        u_ref[...] = uv.astype(BF16)
        a_ref[...] = (gv * _sigmoid(gv) * uv).astype(BF16)

    o = pl.BlockSpec((tm, tn), lambda i, j: (i, j))
    return _pcall(
        body, name="ffn_in_fwd", out_shape=(_sds((t, dff), BF16),) * 3, grid=(t // tm, nj),
        in_specs=[pl.BlockSpec((tm, d), lambda i, j: (i, 0)), pl.BlockSpec((tn, d), lambda i, j: (j, 0)),
                  pl.BlockSpec((tn, d), lambda i, j: (nj + j, 0))],
        out_specs=(o, o, o), compiler_params=_cp(),
    )(h2, wfi_t, wfi_t)


def _ffn_out_bwd(dy, wfo, gate, up, *, t, d, dff, tm=512, tn=1408):
    def body(dy_ref, w_ref, g_ref, u_ref, dg_ref, du_ref):
        dav = lax.dot_general(dy_ref[...].astype(BF16), w_ref[...], (((1,), (1,)), ((), ())), preferred_element_type=F32)
        gv = g_ref[...].astype(F32)
        sg = _sigmoid(gv)
        dg_ref[...] = (dav * u_ref[...].astype(F32) * sg * (1.0 + gv * (1.0 - sg))).astype(BF16)
        du_ref[...] = (dav * gv * sg).astype(BF16)

    o = pl.BlockSpec((tm, tn), lambda i, j: (i, j))
    return _pcall(
        body, name="ffn_out_bwd", out_shape=(_sds((t, dff), BF16),) * 2, grid=(t // tm, dff // tn),
        in_specs=[pl.BlockSpec((tm, d), lambda i, j: (i, 0)), pl.BlockSpec((tn, d), lambda i, j: (j, 0)), o, o],
        out_specs=(o, o), compiler_params=_cp(),
    )(dy, wfo, gate, up)


def _ffn_out_loss(act, wfo, x1, target, *, t, d, dff, tm=512):
    def body(a_ref, w_ref, x_ref, t_ref, dy_ref, l_ref):
        yv = x_ref[...] + jnp.dot(a_ref[...], w_ref[...], preferred_element_type=F32)
        e = yv - t_ref[...]
        dy_ref[...] = e * (1.0 / d)

        @pl.when(pl.program_id(0) == 0)
        def _():
            l_ref[...] = jnp.zeros_like(l_ref)

        l_ref[...] += jnp.sum(jnp.sum(e * e, axis=-1, keepdims=True), axis=0, keepdims=True) * (0.5 / d)

    o = pl.BlockSpec((tm, d), lambda i: (i, 0))
    dy, l = _pcall(
        body, name="ffn_out_loss", out_shape=(_sds((t, d), F32), _sds((1, HD), F32)), grid=(t // tm,),
        in_specs=[pl.BlockSpec((tm, dff), lambda i: (i, 0)), pl.BlockSpec((dff, d), lambda i: (0, 0)), o, o],
        out_specs=(o, pl.BlockSpec((1, HD), lambda i: (0, 0))),
        compiler_params=_cp(dimension_semantics=("arbitrary",)),
    )(act, wfo, x1, target)
    return dy, l[0, 0]


def _adamw(w, g, m, v, *, name):
    if w.ndim == 3:
        rows, _, cols = w.shape
        tm = max(tt for tt in range(1, 65) if rows % tt == 0)
        block, imap, grid = (tm, 1, cols), (lambda i, j: (i, 0, 0)), (rows // tm, 1)
    else:
        rows, cols = w.shape
        tm, tn = rows, cols
        for cand in (256, 128, 64, 32, 16, 8):
            if rows > cand and rows % cand == 0:
                tm = cand
                break
        if tm == rows and rows > 256 and cols % 128 == 0:
            tn = 128
        block, imap, grid = (tm, tn), (lambda i, j: (i, j)), (rows // tm, cols // tn)
    c1 = 1.0 - ADAM_B1 ** ADAM_STEP
    c2 = 1.0 - ADAM_B2 ** ADAM_STEP

    at = (slice(None), 0, slice(None)) if w.ndim == 3 else Ellipsis

    def body(w_ref, g_ref, m_ref, v_ref, d_ref, nm_ref, nv_ref):
        gv = g_ref[at]
        mn = ADAM_B1 * m_ref[at] + (1.0 - ADAM_B1) * gv
        vn = ADAM_B2 * v_ref[at] + (1.0 - ADAM_B2) * (gv * gv)
        d_ref[at] = -ADAM_LR * ((mn / c1) / (jnp.sqrt(vn / c2) + ADAM_EPS) + ADAM_WD * w_ref[at])
        nm_ref[at] = mn
        nv_ref[at] = vn

    spec = pl.BlockSpec(block, imap)
    return _pcall(
        body, name=name, out_shape=(_sds(w.shape, F32),) * 3, grid=grid,
        in_specs=[spec] * 4, out_specs=(spec,) * 3, compiler_params=_cp(),
    )(w, g, m, v)


def _pad_rows(a, rows):
    return a if a.shape[0] == rows else jnp.pad(a, ((0, rows - a.shape[0]),) + ((0, 0),) * (a.ndim - 1))


class _Pack:
    def __init__(self, row_sizes):
        self.rows = list(row_sizes)
        self.padded = [-(-r // 16) * 16 for r in self.rows]
        self.offs = [sum(self.padded[:i]) for i in range(len(self.rows))]
        self.total = sum(self.padded)

    def stack(self, blocks):
        return jnp.concatenate([_pad_rows(b.astype(BF16), p) for b, p in zip(blocks, self.padded)], axis=0)

    def full(self, gathered, i):
        r = self.rows[i]
        return gathered[:, self.offs[i]:self.offs[i] + r, :].reshape(N_DEV * r, gathered.shape[2])

    def for_core(self, full_grads, core):
        parts = []
        for gfull, r, p in zip(full_grads, self.rows, self.padded):
            pieces = gfull if isinstance(gfull, (tuple, list)) else (gfull,)
            blk = jnp.concatenate(
                [lax.dynamic_index_in_dim(g.reshape(-1, 2, r, g.shape[1]), core, axis=1, keepdims=False) for g in pieces],
                axis=0).astype(BF16)
            if p != r:
                blk = jnp.pad(blk, ((0, 0), (0, p - r), (0, 0)))
            parts.append(blk)
        return jnp.concatenate(parts, axis=1)

    def shard(self, g_pack, i):
        return g_pack[self.offs[i]:self.offs[i] + self.rows[i], :]


def kernel(x, mem, norm1_g, w_in, conv_w, conv_b, fox_f_bias, fox_q_g, fox_k_g, mem_norm_g, w_mem_kv, xa_q_g, xa_k_g, w_br_conv, w_br_fox, w_br_xa, w_o, norm2_g, w_ffn_in, w_ffn_out, loss_target, m_norm1_g, m_w_in, m_conv_w, m_conv_b, m_fox_f_bias, m_fox_q_g, m_fox_k_g, m_mem_norm_g, m_w_mem_kv, m_xa_q_g, m_xa_k_g, m_w_br_conv, m_w_br_fox, m_w_br_xa, m_w_o, m_norm2_g, m_w_ffn_in, m_w_ffn_out, v_norm1_g, v_w_in, v_conv_w, v_conv_b, v_fox_f_bias, v_fox_q_g, v_fox_k_g, v_mem_norm_g, v_w_mem_kv, v_xa_q_g, v_xa_k_g, v_w_br_conv, v_w_br_fox, v_w_br_xa, v_w_o, v_norm2_g, v_w_ffn_in, v_w_ffn_out):
    nb, s, d = x.shape
    t = nb * s
    nm = mem.shape[1]
    nh = d // HD
    dff = w_ffn_out.shape[1] * N_DEV
    n_in = w_in.shape[2] * N_DEV
    n_in_pad = -(-n_in // 1152) * 1152
    ff_block = (10 * d) // HD
    my_c = lax.axis_index("c")
    my_q = 2 * lax.axis_index("x") + lax.axis_index("y")

    pack_a = _Pack([w_in.shape[2]])
    pack_b = _Pack([w_mem_kv.shape[2], w_br_conv.shape[1], w_br_fox.shape[1], w_br_xa.shape[1], w_o.shape[1],
                    w_ffn_in.shape[2], w_ffn_out.shape[1]])
    gather_2 = _Pack([w_mem_kv.shape[2], w_br_conv.shape[1], w_br_fox.shape[1], w_br_xa.shape[1], w_o.shape[1]])
    stack_a = pack_a.stack([w_in[0].T])
    stack_2 = gather_2.stack([w_mem_kv[0].T, w_br_conv[0], w_br_fox[0], w_br_xa[0], w_o[0]])
    stack_fi = w_ffn_in[0].T.astype(BF16)
    stack_fo = w_ffn_out[0].astype(BF16)
    cw_block = _pad_rows(conv_w[0], 8)
    rows_a = w_in.shape[2]
    my_x, my_y = lax.axis_index("x"), lax.axis_index("y")
    arrival = [(my_x, my_y, my_c), (my_x, my_y, 1 - my_c)]
    chip_1, chip_2, chip_d = _chip_order(my_x, my_y, my_c)
    arrival += [(*chip_1, my_c), (*chip_2, 1 - my_c), (*chip_2, my_c), (*chip_1, 1 - my_c), (*chip_d, my_c), (*chip_d, 1 - my_c)]
    order = jnp.stack([4 * ax + 2 * ay + ac for ax, ay, ac in arrival]).astype(jnp.int32)

    x2 = x.reshape(t, d)
    mem2 = mem.reshape(nb * nm, d)
    tgt2 = loss_target.reshape(t, d)
    h1 = _rms_fwd(x2, norm1_g, name="rms1_fwd")
    z, slabs, _ = _z_gathered(h1, stack_a, order, t=t, d=d, n_pad=N_DEV * SLAB_WIN, rows=rows_a)
    z = _z_fix(h1, slabs, z, t=t, d=d, rows=rows_a)
    ff_rows = _pad_rows(slabs[N_DEV - 1, rows_a - nh:rows_a, :], HD)
    zff = _mm(h1, ff_rows, m=t, n=HD, k=d, tb=True, tm=1024, name="mm_z_ff")
    qn, kn, vb = _fox_prep(z, fox_q_g, fox_k_g, t=t, d=d)
    f_bias_col = fox_f_bias.reshape(nh, 1)
    ck_rows = _fox_gates(zff, f_bias_col, nb=nb, s=s, nh=nh).reshape(nb * nh, 1, s)
    y_fox, lse, (gathered_2, cw_g, gathered_fo) = _fox_fwd(qn, kn, vb, ck_rows, nb=nb, s=s, nh=nh,
                                                           comm=_AllGather([stack_2, cw_block, stack_fo], forward_at=0.8))
    wkv_t, wbc, wbf, wbx, wo = (gather_2.full(gathered_2, i) for i in range(5))
    wfo = gathered_fo.reshape(dff, d)
    conv_w_full = jnp.transpose(cw_g[:, 0:3, :], (1, 0, 2)).reshape(3, d)

    y_conv = _conv_fwd(z, conv_w_full, conv_b, nb=nb, s=s, d=d)
    mem_n = _rms_fwd(mem2, mem_norm_g, name="rms_mem_fwd")
    kv = _mm(mem_n, wkv_t, m=nb * nm, n=2 * d, k=d, tb=True, name="mm_kv")
    xkn, xvb = _xa_prep_kv(kv, xa_k_g, rows=nb * nm, d=d)
    y_xa = _xa_fwd(z, xkn, xvb, xa_q_g, nb=nb, s=s, d=d, nm=nm, q_block=6)
    merged, proj, (gathered_fi,) = _branches_fwd(z, (y_conv, y_fox, y_xa), (wbc, wbf, wbx), t=t, d=d, gate_block=7,
                                                 comm=_AllGather([stack_fi]))
    wfi_t = gathered_fi.reshape(2 * dff, d)
    x1 = _mm(merged, wo, m=t, n=d, k=d, res=x2, name="mm_x1")
    h2 = _rms_fwd(x1, norm2_g, name="rms2_fwd")
    ff_gate, ff_up, act = _ffn_in_fwd(h2, wfi_t, t=t, d=d, dff=dff)
    dy, loss_local = _ffn_out_loss(act, wfo, x1, tgt2, t=t, d=d, dff=dff)

    g_wfo = _mm(act, dy, m=dff, n=d, k=t, ta=True, tm=1408, tk=2048, out_dtype=BF16, name="mm_g_wfo")
    dgate, dup = _ffn_out_bwd(dy, wfo, ff_gate, ff_up, t=t, d=d, dff=dff)
    dh2 = _mm(dgate, wfi_t, m=t, n=d, k=dff, tk=dff, name="mm_dh2_g")
    dh2 = _mm(dup, wfi_t, m=t, n=d, k=dff, tk=dff, b_off=(1, 0), res=dh2, name="mm_dh2_u")
    g_wfi_g = _mm(dgate, h2, m=dff, n=d, k=t, ta=True, tm=1408, tk=2048, out_dtype=BF16, name="mm_g_wfi_g")
    g_wfi_u = _mm(dup, h2, m=dff, n=d, k=t, ta=True, tm=1408, tk=2048, out_dtype=BF16, name="mm_g_wfi_u")
    dx1, g_norm2 = _rms_bwd(dh2, x1, norm2_g, dy, name="rms2_bwd")
    dmerged = _mm(dx1, wo, m=t, n=d, k=d, tb=True, name="mm_dmerged")
    g_wo = _mm(merged, dx1, m=d, n=d, k=t, ta=True, tk=2048, out_dtype=BF16, name="mm_g_wo")

    dz = lax.empty((t, n_in_pad), BF16)
    dz, dpa, dpb, dpc = _gates_bwd(z, dmerged, proj, dz, t=t, d=d, gate_block=7)
    dy_conv = _mm(dpa, wbc, m=t, n=d, k=d, tb=True, name="mm_dy_conv")
    g_wbc = _mm(y_conv, dpa, m=d, n=d, k=t, ta=True, tk=4096, out_dtype=BF16, name="mm_g_wbc")
    dy_fox = _mm(dpb, wbf, m=t, n=d, k=d, tb=True, name="mm_dy_fox")
    g_wbf = _mm(y_fox, dpb, m=d, n=d, k=t, ta=True, tk=4096, out_dtype=BF16, name="mm_g_wbf")
    dy_xa = _mm(dpc, wbx, m=t, n=d, k=d, tb=True, name="mm_dy_xa")
    g_wbx = _mm(y_xa, dpc, m=d, n=d, k=t, ta=True, tk=4096, out_dtype=BF16, name="mm_g_wbx")

    dz, dxkn, dxv, g_xa_q = _xa_bwd(z, xkn, xvb, xa_q_g, dy_xa, dz, nb=nb, s=s, d=d, nm=nm, q_block=6)
    dkv, g_xa_k = _xa_post_kv(kv, dxkn, dxv, xa_k_g, rows=nb * nm, d=d)
    g_wkv_t = _mm(dkv, mem_n, m=2 * d, n=d, k=nb * nm, ta=True, out_dtype=BF16, name="mm_g_wkv")
    dmem_n = _mm(dkv, wkv_t, m=nb * nm, n=d, k=2 * d, tk=2 * d, name="mm_dmem")
    _, g_mem_norm = _rms_bwd(dmem_n, mem2, mem_norm_g, None, name="rms_mem_bwd")

    grads_b = [g_wkv_t, g_wbc, g_wbf, g_wbx, g_wo, (g_wfi_g, g_wfi_u), g_wfo]
    keep_b, send_b = pack_b.for_core(grads_b, my_c), pack_b.for_core(grads_b, 1 - my_c)
    dz, g_conv_w, g_conv_b, (from_sibling_b,) = _conv_bwd(z, dy_conv, conv_w_full, conv_b, dz, nb=nb, s=s, d=d,
                                                          comm=_SiblingSwap(send_b))
    part_b = _add2(keep_b, from_sibling_b, name="rs_add_sibling_b")
    dqn, ds_rows, dkn, dvb, ds_cols, (from_chips_b,) = _fox_bwd(qn, kn, vb, ck_rows, dy_fox, y_fox, lse, nb=nb, s=s, nh=nh,
                                                                comm=_ChipExchange(part_b))
    g_pack_b = _add4(lax.dynamic_index_in_dim(part_b, my_q, axis=0, keepdims=False), from_chips_b, name="rs_add_chips_b")
    dz, g_fox_q, g_fox_k = _fox_post(z, dqn, dkn, dvb, fox_q_g, fox_k_g, dz, t=t, d=d)
    dz, g_f_bias = _fox_gates_bwd(zff, f_bias_col, ds_rows, ds_cols, dz, nb=nb, s=s, nh=nh, ff_block=ff_block)

    keep_a, from_sibling_a = _g_win_slabs(dz, h1, my_c, t=t, d=d, n_slab=N_DEV, rows=rows_a, rp=slabs.shape[1])
    part_a = _add2(keep_a, from_sibling_a, name="rs_add_sibling_a")
    dh1, (from_chips_a,) = _dh1_from_slabs(dz, slabs, t=t, d=d, rows=rows_a, comm=_ChipExchange(part_a))
    g_pack_a = _add4(lax.dynamic_index_in_dim(part_a, my_q, axis=0, keepdims=False), from_chips_a, name="rs_add_chips_a",
                     rows_apart=True)
    dx, g_norm1 = _rms_bwd(dh1, x2, norm1_g, dx1, name="rms1_bwd")

    xd = d // XH
    tail = jnp.concatenate(
        [_pad_rows(g_f_bias.reshape(nh), HD).reshape(1, HD), g_fox_q, g_fox_k, g_xa_q, g_xa_k,
         jnp.zeros((1, d - 3 * HD - 2 * xd), F32)], axis=1)
    small = jnp.concatenate([g_norm1, g_norm2, g_conv_w, g_conv_b, g_mem_norm, tail], axis=0)
    small = _all_reduce_small(small)
    gs_conv_w = lax.dynamic_slice_in_dim(small[2:5], (2 * my_q + my_c) * (d // N_DEV), d // N_DEV, axis=1)

    grads = {
        "norm1_g": small[0:1], "w_in": g_pack_a, "conv_w": gs_conv_w, "conv_b": small[5:6],
        "fox_f_bias": small[7:8, 0:nh], "fox_q_g": small[7:8, HD:2 * HD], "fox_k_g": small[7:8, 2 * HD:3 * HD],
        "mem_norm_g": small[6:7], "w_mem_kv": pack_b.shard(g_pack_b, 0), "xa_q_g": small[7:8, 3 * HD:3 * HD + xd],
        "xa_k_g": small[7:8, 3 * HD + xd:3 * HD + 2 * xd], "w_br_conv": pack_b.shard(g_pack_b, 1),
        "w_br_fox": pack_b.shard(g_pack_b, 2), "w_br_xa": pack_b.shard(g_pack_b, 3), "w_o": pack_b.shard(g_pack_b, 4),
        "norm2_g": small[1:2], "w_ffn_in": pack_b.shard(g_pack_b, 5), "w_ffn_out": pack_b.shard(g_pack_b, 6),
    }
    transposed = {"w_in", "w_mem_kv", "w_ffn_in"}
    weights = {
        "norm1_g": (norm1_g, m_norm1_g, v_norm1_g), "w_in": (w_in, m_w_in, v_w_in), "conv_w": (conv_w, m_conv_w, v_conv_w),
        "conv_b": (conv_b, m_conv_b, v_conv_b), "fox_f_bias": (fox_f_bias, m_fox_f_bias, v_fox_f_bias),
        "fox_q_g": (fox_q_g, m_fox_q_g, v_fox_q_g), "fox_k_g": (fox_k_g, m_fox_k_g, v_fox_k_g),
        "mem_norm_g": (mem_norm_g, m_mem_norm_g, v_mem_norm_g), "w_mem_kv": (w_mem_kv, m_w_mem_kv, v_w_mem_kv),
        "xa_q_g": (xa_q_g, m_xa_q_g, v_xa_q_g), "xa_k_g": (xa_k_g, m_xa_k_g, v_xa_k_g),
        "w_br_conv": (w_br_conv, m_w_br_conv, v_w_br_conv), "w_br_fox": (w_br_fox, m_w_br_fox, v_w_br_fox),
        "w_br_xa": (w_br_xa, m_w_br_xa, v_w_br_xa), "w_o": (w_o, m_w_o, v_w_o), "norm2_g": (norm2_g, m_norm2_g, v_norm2_g),
        "w_ffn_in": (w_ffn_in, m_w_ffn_in, v_w_ffn_in), "w_ffn_out": (w_ffn_out, m_w_ffn_out, v_w_ffn_out),
    }
    out_g, out_d, out_m, out_v = [], [], [], []
    for name, (w, m, v) in weights.items():
        shape = w.shape
        if name == "w_in":
            w3, m3, v3 = (a.transpose(2, 0, 1) for a in (w, m, v))
            dl, mn, vn = _adamw(w3, g_pack_a, m3, v3, name="adamw_" + name)
            out_g.append(g_pack_a[:shape[2]].transpose(1, 2, 0))
            out_d.append(dl.transpose(1, 2, 0))
            out_m.append(mn.transpose(1, 2, 0))
            out_v.append(vn.transpose(1, 2, 0))
            continue
        tr = name in transposed

        def as2d(a):
            a = a.reshape(shape[-2], shape[-1])
            return a.T if tr else a

        def back(a):
            return (a.T if tr else a).reshape(shape)

        w2 = as2d(w)
        g2 = grads[name].reshape(w2.shape)
        dl, mn, vn = _adamw(w2, g2, as2d(m), as2d(v), name="adamw_" + name)
        out_g.append(back(g2))
        out_d.append(back(dl))
        out_m.append(back(mn))
        out_v.append(back(vn))

    loss = lax.psum(loss_local, ("x", "y", "c"))
    return (loss, dx.reshape(nb, s, d), *out_g, *out_d, *out_m, *out_v)
```

```python
import math

import jax
import jax.numpy as jnp
from jax import lax
from jax.experimental import pallas as pl
from jax.experimental.pallas import tpu as pltpu

F32, BF16 = jnp.float32, jnp.bfloat16
EPS = 1e-6
HD = 128
XH = 4
N_DEV = 8
MESH = pl.DeviceIdType.MESH
VMEM_LIMIT = 52 * 1024 * 1024
NEG = -1e30
SQRT_HD = math.sqrt(HD)
EXP2_C = math.log2(math.e) / SQRT_HD
LOG2E = math.log2(math.e)
FOX_HP = 4

ADAM_LR, ADAM_B1, ADAM_B2, ADAM_EPS, ADAM_WD, ADAM_STEP = 0.001, 0.9, 0.999, 1e-08, 0.01, 10


def _cp(**kw):
    return pltpu.CompilerParams(vmem_limit_bytes=VMEM_LIMIT, **kw)


def _pcall(body, **kw):
    return pl.pallas_call(body, **kw)


def _sds(shape, dtype):
    return jax.ShapeDtypeStruct(shape, dtype)


HBM_SPEC = pl.BlockSpec(memory_space=pl.ANY)


def _mesh_pos():
    return lax.axis_index("x"), lax.axis_index("y"), lax.axis_index("c")


class _AllGather:
    def __init__(self, blocks, forward_at=None):
        self.blocks = list(blocks)
        self.forward_at = forward_at
        n = len(self.blocks)
        self.out_shape = [_sds((N_DEV,) + b.shape, b.dtype) for b in self.blocks]
        self.scratch = [pltpu.SemaphoreType.DMA((n, 7)), pltpu.SemaphoreType.DMA((n, 7)), pltpu.SemaphoreType.DMA((n,))]

    def bind(self, ins, outs, sems):
        n = len(ins)
        send_sems, recv_sems, local_sems = sems
        x, y, c = _mesh_pos()
        me, sibling = (x, y, c), (x, y, 1 - c)
        chips = [(1 - x, y), (x, 1 - y), (1 - x, 1 - y)]

        def slab(t, dev):
            return outs[t].at[4 * dev[0] + 2 * dev[1] + dev[2]]

        def copy(t, k, block, to, src=None):
            dst = slab(t, block)
            return pltpu.make_async_remote_copy(
                src_ref=dst if src is None else src, dst_ref=dst, send_sem=send_sems.at[t, k], recv_sem=recv_sems.at[t, k],
                device_id=to, device_id_type=MESH)

        mine = [pltpu.make_async_copy(ins[t], slab(t, me), local_sems.at[t]) for t in range(n)]
        first = []
        for t in range(n):
            first.append(copy(t, 0, me, sibling, src=ins[t]))
            first += [copy(t, 1 + j, me, (*chip, c), src=ins[t]) for j, chip in enumerate(chips)]
        passed = [copy(t, 4 + j, (*chip, c), sibling) for j, chip in enumerate(chips) for t in range(n)]

        def start():
            for cp in mine + first:
                cp.start()

        def mid():
            for j, chip in enumerate(chips):
                for t in range(n):
                    copy(t, 1 + j, (*chip, c), me).wait_recv()
                    passed[j * n + t].start()

        def finish():
            for t in range(n):
                copy(t, 0, sibling, me).wait_recv()
                for j, chip in enumerate(chips):
                    copy(t, 4 + j, (*chip, 1 - c), me).wait_recv()
            for cp in first + passed:
                cp.wait_send()
            for cp in mine:
                cp.wait()

        return start, mid, finish


class _SiblingSwap:
    def __init__(self, send):
        self.blocks = [send]
        self.out_shape = [_sds(send.shape, send.dtype)]
        self.scratch = [pltpu.SemaphoreType.DMA, pltpu.SemaphoreType.DMA]

    def bind(self, ins, outs, sems):
        x, y, c = _mesh_pos()
        cp = pltpu.make_async_remote_copy(src_ref=ins[0], dst_ref=outs[0], send_sem=sems[0], recv_sem=sems[1],
                                          device_id=(x, y, 1 - c), device_id_type=MESH)
        return cp.start, (lambda: None), cp.wait


class _ChipExchange:
    def __init__(self, part):
        self.blocks = [part]
        self.out_shape = [_sds((3,) + part.shape[1:], part.dtype)]
        self.scratch = [pltpu.SemaphoreType.DMA((3,)), pltpu.SemaphoreType.DMA((3,))]

    def bind(self, ins, outs, sems):
        x, y, c = _mesh_pos()
        chips = [(1 - x, y), (x, 1 - y), (1 - x, 1 - y)]
        cps = [
            pltpu.make_async_remote_copy(src_ref=ins[0].at[2 * cx + cy], dst_ref=outs[0].at[j], send_sem=sems[0].at[j],
                                         recv_sem=sems[1].at[j], device_id=(cx, cy, c), device_id_type=MESH)
            for j, (cx, cy) in enumerate(chips)
        ]

        def start():
            for cp in cps:
                cp.start()

        def finish():
            for cp in cps:
                cp.wait()

        return start, (lambda: None), finish


def _comm_lists(comm):
    if comm is None:
        return [], [], [], []
    n = len(comm.blocks)
    return list(comm.blocks), [HBM_SPEC] * n, list(comm.out_shape), list(comm.scratch)


def _split_refs(refs, n_pre, n_in, n_cin, n_out, n_cout, n_scr):
    cuts = [n_pre, n_in, n_cin, n_out, n_cout, n_scr]
    out, at = [], 0
    for c in cuts:
        out.append(refs[at:at + c])
        at += c
    out.append(refs[at:])
    return out


def _run_comm(comm, cin, cout, csem, step, n_steps, when="start"):
    if comm is None:
        return
    start, mid, finish = comm.bind(cin, cout, csem)
    if when == "start":
        pl.when(step == 0)(start)
    else:
        frac = getattr(comm, "forward_at", None)
        mid_step = max(n_steps - 2, 0) if frac is None else min(int(frac * n_steps), max(n_steps - 2, 0))
        pl.when(step == mid_step)(mid)
        pl.when(step == n_steps - 1)(finish)


def _all_reduce_small(v):
    rows, cols = v.shape

    def body(v_ref, o_ref, slots, send_sems, recv_sems):
        x, y, c = _mesh_pos()
        me = 4 * x + 2 * y + c
        slots[me] = v_ref[...]
        cps = []
        for k in range(1, N_DEV):
            px, py, pc = x ^ (k >> 2), y ^ ((k >> 1) & 1), c ^ (k & 1)
            cps.append(pltpu.make_async_remote_copy(
                src_ref=v_ref, dst_ref=slots.at[me], send_sem=send_sems.at[k - 1], recv_sem=recv_sems.at[k - 1],
                device_id=(px, py, pc), device_id_type=MESH))
        for cp in cps:
            cp.start()
        for cp in cps:
            cp.wait()
        acc = slots[0]
        for k in range(1, N_DEV):
            acc = acc + slots[k]
        o_ref[...] = acc

    return _pcall(
        body, name="all_reduce_small", out_shape=_sds((rows, cols), F32),
        in_specs=[pl.BlockSpec(memory_space=pltpu.VMEM)], out_specs=pl.BlockSpec(memory_space=pltpu.VMEM),
        scratch_shapes=[pltpu.VMEM((N_DEV, rows, cols), F32), pltpu.SemaphoreType.DMA((7,)), pltpu.SemaphoreType.DMA((7,))],
        compiler_params=_cp(),
    )(v)


def _row_tile(rows, cap=1040):
    return max(tt for tt in range(16, cap + 1, 16) if rows % tt == 0)


def _add2(a, b, *, name):
    nq, rows, cols = a.shape
    tm = _row_tile(rows)

    def body(a_ref, b_ref, o_ref):
        o_ref[...] = (a_ref[...].astype(F32) + b_ref[...].astype(F32)).astype(BF16)

    spec = pl.BlockSpec((1, tm, cols), lambda q, i: (q, i, 0))
    return _pcall(body, name=name, out_shape=_sds(a.shape, BF16), grid=(nq, rows // tm),
                  in_specs=[spec, spec], out_specs=spec, compiler_params=_cp())(a, b)


def _add4(own, recv, *, name, rows_apart=False):
    rows, cols = own.shape
    tm = _row_tile(rows, 256 if rows_apart else 1040)

    def body(o_ref, r_ref, g_ref):
        tot = ((o_ref[...].astype(F32) + r_ref[0].astype(F32)) + r_ref[1].astype(F32)) + r_ref[2].astype(F32)
        if rows_apart:
            g_ref[:, 0, :] = tot
        else:
            g_ref[...] = tot

    if rows_apart:
        out_shape, out_spec = _sds((rows, 1, cols), F32), pl.BlockSpec((tm, 1, cols), lambda i: (i, 0, 0))
    else:
        out_shape, out_spec = _sds((rows, cols), F32), pl.BlockSpec((tm, cols), lambda i: (i, 0))
    return _pcall(
        body, name=name, out_shape=out_shape, grid=(rows // tm,),
        in_specs=[pl.BlockSpec((tm, cols), lambda i: (i, 0)), pl.BlockSpec((3, tm, cols), lambda i: (0, i, 0))],
        out_specs=out_spec, compiler_params=_cp(),
    )(own, recv)


def _mm(a, b, *, m, n, k, name, ta=False, tb=False, b_off=(0, 0), res=None, out_dtype=F32, tm=512, tn=1024, tk=1024,
        comm=None):
    tm, tn, tk = min(tm, m), min(tn, n), min(tk, k)
    assert m % tm == 0 and n % tn == 0 and k % tk == 0, (name, m, n, k, tm, tn, tk)
    nk = k // tk
    grid = (m // tm, n // tn, nk)
    bo0, bo1 = b_off
    if ta:
        a_spec = pl.BlockSpec((tk, tm), lambda i, j, kk: (kk, i))
    else:
        a_spec = pl.BlockSpec((tm, tk), lambda i, j, kk: (i, kk))
    if tb:
        b_spec = pl.BlockSpec((tn, tk), lambda i, j, kk: (j + bo0, kk + bo1))
    else:
        b_spec = pl.BlockSpec((tk, tn), lambda i, j, kk: (kk + bo0, j + bo1))
    o_spec = pl.BlockSpec((tm, tn), lambda i, j, kk: (i, j))
    dn = (((0 if ta else 1,), (1 if tb else 0,)), ((), ()))
    has_res = res is not None
    c_ins, c_in_specs, c_out_shape, c_scratch = _comm_lists(comm)
    n_in = 3 if has_res else 2
    n_scr = 1 if nk > 1 else 0

    def body(*refs):
        _, ins, cin, outs, cout, scr, csem = _split_refs(refs, 0, n_in, len(c_ins), 1, len(c_ins), n_scr)
        a_ref, b_ref = ins[0], ins[1]
        r_ref = ins[2] if has_res else None
        o_ref = outs[0]
        step = (pl.program_id(0) * grid[1] + pl.program_id(1)) * nk + pl.program_id(2)
        _run_comm(comm, cin, cout, csem, step, grid[0] * grid[1] * nk, "start")
        p = lax.dot_general(a_ref[...].astype(BF16), b_ref[...].astype(BF16), dn, preferred_element_type=F32)

        def finish(acc):
            if has_res:
                acc = acc + r_ref[...]
            o_ref[...] = acc.astype(out_dtype)

        if nk == 1:
            finish(p)
        else:
            acc_ref = scr[0]
            kk = pl.program_id(2)

            @pl.when(kk == 0)
            def _():
                acc_ref[...] = p

            @pl.when(kk > 0)
            def _():
                acc_ref[...] += p

            @pl.when(kk == nk - 1)
            def _():
                finish(acc_ref[...])

        _run_comm(comm, cin, cout, csem, step, grid[0] * grid[1] * nk, "end")

    ins = [a, b] + ([res] if has_res else [])
    in_specs = [a_spec, b_spec] + ([o_spec] if has_res else [])
    sem = ("arbitrary",) * 3 if comm is not None else ("parallel", "parallel", "arbitrary")
    outs = _pcall(
        body, name=name, out_shape=tuple([_sds((m, n), out_dtype)] + c_out_shape), grid=grid,
        in_specs=in_specs + c_in_specs, out_specs=tuple([o_spec] + [HBM_SPEC] * len(c_ins)),
        scratch_shapes=([pltpu.VMEM((tm, tn), F32)] if nk > 1 else []) + c_scratch,
        compiler_params=_cp(dimension_semantics=sem),
    )(*(ins + c_ins))
    return outs[0] if comm is None else (outs[0], list(outs[1:]))


SLAB_WIN = 1280
SLAB_TAIL = 48


def _chip_order(x, y, c):
    north = c == 1
    first = (jnp.where(north, 1 - x, x), jnp.where(north, y, 1 - y))
    second = (jnp.where(north, x, 1 - x), jnp.where(north, 1 - y, y))
    return [first, second, (1 - x, 1 - y)]


def _z_gathered(h1, own_slab, order, *, t, d, n_pad, rows, comm=None, tm=1024):
    rp = own_slab.shape[0]
    n_slab = N_DEV
    tm = min(tm, t)
    ni = t // tm
    sh = rows - SLAB_WIN
    assert sh >= 0 and (n_slab - 1) * sh <= 128
    c_ins, c_in_specs, c_out_shape, c_scratch = _comm_lists(comm)
    n_steps = n_slab * ni

    def body(*refs):
        pre, ins, cin, outs, cout, scr, csem = _split_refs(refs, 1, 2, len(c_ins), 2, len(c_ins), 5)
        order_ref = pre[0]
        h_ref, own_ref = ins
        o_ref, slabs_ref = outs
        slab_vm, send_sems, recv_sems, local_sem, load_sem = scr
        p, i = pl.program_id(0), pl.program_id(1)
        step = p * ni + i
        x, y, c = _mesh_pos()
        me, sibling = (x, y, c), (x, y, 1 - c)
        chips = _chip_order(x, y, c)

        def slab(dev):
            return slabs_ref.at[4 * dev[0] + 2 * dev[1] + dev[2]]

        def copy(k, block, to, src=None, part=None):
            dst = slab(block) if part is None else slab(block).at[part]
            return pltpu.make_async_remote_copy(
                src_ref=dst if src is None else src, dst_ref=dst, send_sem=send_sems.at[k], recv_sem=recv_sems.at[k],
                device_id=to, device_id_type=MESH)

        nb_1, nb_2, diag = ((*chip, c) for chip in chips)
        cut = (rp // 32) * 16
        half_a, half_b = pl.ds(0, cut), pl.ds(cut, rp - cut)
        mine = pltpu.make_async_copy(own_ref, slab(me), local_sem)
        first = [copy(0, me, sibling, src=own_ref), copy(1, me, nb_1, src=own_ref), copy(2, me, nb_2, src=own_ref)]
        relays = [copy(3, nb_1, nb_2, part=half_a), copy(4, nb_2, nb_1, part=half_b)]
        axis_of = [jnp.where(c == 1, 0, 1), jnp.where(c == 1, 1, 0), 2]
        passed = [copy(5 + axis_of[j], (*chip, c), sibling) for j, chip in enumerate(chips)]
        later = min(n_steps // 5, n_steps - 1)

        @pl.when(step == 0)
        def _():
            mine.start()
            first[0].start()
            first[1].start()

        @pl.when(step == later)
        def _():
            first[2].start()

        _run_comm(comm, cin, cout, csem, step, n_steps, "start")

        def load(src):
            cp = pltpu.make_async_copy(src, slab_vm, load_sem)
            cp.start()
            cp.wait()

        @pl.when(i == 0)
        def _():
            @pl.when(p == 0)
            def _():
                load(own_ref)

            @pl.when(p == 1)
            def _():
                copy(0, sibling, me).wait_recv()
                load(slab(sibling))

            for j, chip in enumerate(chips):
                @pl.when(p == 2 + 2 * j)
                def _():
                    if j < 2:
                        copy(1 + j, (*chip, c), me).wait_recv()
                        relays[j].start()
                    else:
                        copy(3, diag, me, part=half_a).wait_recv()
                        copy(4, diag, me, part=half_b).wait_recv()
                    passed[j].start()
                    load(slab((*chip, c)))

                other = (1, 0, 2)[j]

                @pl.when(p == 3 + 2 * j)
                def _():
                    copy(5 + axis_of[other], (*chips[other], 1 - c), me).wait_recv()
                    load(slab((*chips[other], 1 - c)))

        k = order_ref[p]
        y_k = lax.dot_general(h_ref[...], slab_vm[...], (((1,), (1,)), ((), ())), preferred_element_type=F32)
        o_ref[...] = pltpu.roll(y_k[:, 0:SLAB_WIN], k * sh, axis=1).astype(BF16)

        @pl.when(step == n_steps - 1)
        def _():
            for cp in first + relays + passed:
                cp.wait_send()
            mine.wait()

        _run_comm(comm, cin, cout, csem, step, n_steps, "end")

    outs = _pcall(
        body, name="mm_z", out_shape=tuple([_sds((t, n_pad), BF16), _sds((n_slab, rp, d), BF16)] + c_out_shape),
        grid_spec=pltpu.PrefetchScalarGridSpec(
            num_scalar_prefetch=1, grid=(n_slab, ni),
            in_specs=[pl.BlockSpec((tm, d), lambda p, i, order: (i, 0)), HBM_SPEC] + c_in_specs,
            out_specs=tuple([pl.BlockSpec((tm, SLAB_WIN), lambda p, i, order: (i, order[p])), HBM_SPEC] + [HBM_SPEC] * len(c_ins)),
            scratch_shapes=[pltpu.VMEM((rp, d), BF16), pltpu.SemaphoreType.DMA((8,)), pltpu.SemaphoreType.DMA((8,)),
                            pltpu.SemaphoreType.DMA, pltpu.SemaphoreType.DMA] + c_scratch),
        compiler_params=_cp(dimension_semantics=("arbitrary", "arbitrary")),
    )(order, h1, own_slab, *c_ins)
    return outs[0], outs[1], list(outs[2:])


def _z_fix(h1, slabs, z, *, t, d, rows, tm=4096):
    n_slab, rp, _ = slabs.shape
    tm = min(tm, t)
    sh = rows - SLAB_WIN
    tail_lo = rp - SLAB_TAIL
    assert rows - (n_slab - 1) * sh >= tail_lo and rp % SLAB_TAIL == 0

    def body(h_ref, t_ref, z_ref, o_ref):
        k = pl.program_id(0) + 1
        p = lax.dot_general(h_ref[...], t_ref[0], (((1,), (1,)), ((), ())), preferred_element_type=F32)
        pp = jnp.concatenate([p, jnp.zeros((tm, 128 - SLAB_TAIL), F32)], axis=1)
        pr = pltpu.roll(pp, 128 - (rows - tail_lo) + k * sh, axis=1)
        lane = lax.broadcasted_iota(jnp.int32, (tm, 128), 1)
        o_ref[...] = jnp.where(lane < k * sh, pr.astype(BF16), z_ref[...])

    blk = pl.BlockSpec((tm, 128), lambda b, i: (i, (SLAB_WIN // 128) * (b + 1)))
    return _pcall(
        body, name="mm_z_fix", out_shape=_sds(z.shape, z.dtype), grid=(n_slab - 1, t // tm),
        in_specs=[pl.BlockSpec((tm, d), lambda b, i: (i, 0)),
                  pl.BlockSpec((1, SLAB_TAIL, d), lambda b, i: (b, tail_lo // SLAB_TAIL, 0)), blk],
        out_specs=blk, input_output_aliases={2: 0}, compiler_params=_cp(),
    )(h1, slabs, z)


def _slab_window(a1_ref, a2_ref, k, sh):
    w = jnp.concatenate([a1_ref[...], a2_ref[...]], axis=1)
    width = SLAB_WIN + 128
    return pltpu.roll(w, width - k * sh, axis=1)


def _g_win_slabs(dz, h1, my_c, *, t, d, n_slab, rows, rp, tk=2048):
    sh = rows - SLAB_WIN
    tk = min(tk, t)
    nk = t // tk
    width = SLAB_WIN + 128
    half = n_slab // 2

    def slab_of(p, c):
        return 2 * (p % half) + jnp.where(p < half, 1 - c, c)

    def body(c_ref, a1_ref, a2_ref, h_ref, keep_ref, land_ref, acc, stage, send_sems, recv_sems):
        p, kk = pl.program_id(0), pl.program_id(1)
        x, y, c = _mesh_pos()
        ws = _slab_window(a1_ref, a2_ref, slab_of(p, c), sh)
        prod = lax.dot_general(ws, h_ref[...], (((0,), (0,)), ((), ())), preferred_element_type=F32)

        def push(q):
            return pltpu.make_async_remote_copy(
                src_ref=stage.at[q % 2], dst_ref=land_ref.at[q], send_sem=send_sems.at[q], recv_sem=recv_sems.at[q],
                device_id=(x, y, 1 - c), device_id_type=MESH)

        @pl.when(kk == 0)
        def _():
            acc[...] = prod

        @pl.when(kk > 0)
        def _():
            acc[...] += prod

        @pl.when(kk == nk - 1)
        def _():
            res = acc[0:rp, :].astype(BF16)
            for q in range(half):
                @pl.when(p == q)
                def _():
                    if q >= 2:
                        push(q - 2).wait_send()
                    stage[q % 2] = res
                    push(q).start()

            @pl.when(p >= half)
            def _():
                keep_ref[0] = res

        @pl.when((p == n_slab - 1) & (kk == nk - 1))
        def _():
            for q in range(max(half - 2, 0), half):
                push(q).wait_send()
            for q in range(half):
                push(q).wait_recv()

    outs = _pcall(
        body, name="mm_g_win", out_shape=(_sds((half, rp, d), BF16), _sds((half, rp, d), BF16)),
        grid_spec=pltpu.PrefetchScalarGridSpec(
            num_scalar_prefetch=1, grid=(n_slab, nk),
            in_specs=[pl.BlockSpec((tk, SLAB_WIN), lambda p, kk, c: (kk, slab_of(p, c[0]))),
                      pl.BlockSpec((tk, 128), lambda p, kk, c: (kk, (SLAB_WIN // 128) * (slab_of(p, c[0]) + 1))),
                      pl.BlockSpec((tk, d), lambda p, kk, c: (kk, 0))],
            out_specs=(pl.BlockSpec((1, rp, d), lambda p, kk, c: (jnp.maximum(p - half, 0), 0, 0)), HBM_SPEC),
            scratch_shapes=[pltpu.VMEM((width, d), F32), pltpu.VMEM((2, rp, d), BF16),
                            pltpu.SemaphoreType.DMA((half,)), pltpu.SemaphoreType.DMA((half,))]),
        compiler_params=_cp(dimension_semantics=("arbitrary", "arbitrary")),
    )(my_c.reshape(1).astype(jnp.int32), dz, dz, h1)
    return outs[0], outs[1]


def _dh1_from_slabs(dz, slabs, *, t, d, rows, comm=None, tm=1024):
    n_slab, rp, _ = slabs.shape
    sh = rows - SLAB_WIN
    width = SLAB_WIN + 128
    c_ins, c_in_specs, c_out_shape, c_scratch = _comm_lists(comm)
    grid = (t // tm, n_slab)

    def body(*refs):
        _, ins, cin, outs, cout, scr, csem = _split_refs(refs, 0, 3, len(c_ins), 1, len(c_ins), 1)
        a1_ref, a2_ref, s_ref = ins
        o_ref, acc = outs[0], scr[0]
        k = pl.program_id(1)
        step = pl.program_id(0) * n_slab + k
        _run_comm(comm, cin, cout, csem, step, grid[0] * n_slab, "start")
        ws = _slab_window(a1_ref, a2_ref, k, sh)
        s_end = jnp.concatenate([s_ref[0, SLAB_WIN:rp, :], jnp.zeros((width - rp, d), BF16)], axis=0)
        p = (jnp.dot(ws[:, 0:SLAB_WIN], s_ref[0, 0:SLAB_WIN, :], preferred_element_type=F32)
             + jnp.dot(ws[:, SLAB_WIN:width], s_end, preferred_element_type=F32))

        @pl.when(k == 0)
        def _():
            acc[...] = p

        @pl.when(k > 0)
        def _():
            acc[...] += p

        @pl.when(k == n_slab - 1)
        def _():
            o_ref[...] = acc[...]

        _run_comm(comm, cin, cout, csem, step, grid[0] * n_slab, "end")

    outs = _pcall(
        body, name="mm_dh1", out_shape=tuple([_sds((t, d), F32)] + c_out_shape), grid=grid,
        in_specs=[pl.BlockSpec((tm, SLAB_WIN), lambda i, k: (i, k)),
                  pl.BlockSpec((tm, 128), lambda i, k: (i, (SLAB_WIN // 128) * (k + 1))),
                  pl.BlockSpec((1, rp, d), lambda i, k: (k, 0, 0))] + c_in_specs,
        out_specs=tuple([pl.BlockSpec((tm, d), lambda i, k: (i, 0))] + [HBM_SPEC] * len(c_ins)),
        scratch_shapes=[pltpu.VMEM((tm, d), F32)] + c_scratch,
        compiler_params=_cp(dimension_semantics=("arbitrary", "arbitrary")),
    )(dz, dz, slabs, *c_ins)
    return outs[0], list(outs[1:])


def _rms_fwd(x, g, *, name, tm=1024):
    t, d = x.shape
    tm = min(tm, t)

    def body(x_ref, g_ref, h_ref):
        xv = x_ref[...]
        r = lax.rsqrt(jnp.mean(xv * xv, axis=-1, keepdims=True) + EPS)
        h_ref[...] = (xv * r * g_ref[...]).astype(BF16)

    return _pcall(
        body, name=name, out_shape=_sds((t, d), BF16), grid=(t // tm,),
        in_specs=[pl.BlockSpec((tm, d), lambda i: (i, 0)), pl.BlockSpec((1, d), lambda i: (0, 0))],
        out_specs=pl.BlockSpec((tm, d), lambda i: (i, 0)), compiler_params=_cp(),
    )(x, g)


def _rms_bwd(dh, x, g, res, *, name, tm=1024):
    t, d = x.shape
    tm = min(tm, t)
    has_res = res is not None

    def body(*refs):
        dh_ref, x_ref, g_ref = refs[:3]
        r_ref = refs[3] if has_res else None
        dx_ref, dg_ref = refs[-2], refs[-1]
        xv = x_ref[...]
        r = lax.rsqrt(jnp.mean(xv * xv, axis=-1, keepdims=True) + EPS)
        xh = xv * r
        dhv = dh_ref[...]
        gd = dhv * g_ref[...]
        dx = r * (gd - xh * jnp.mean(gd * xh, axis=-1, keepdims=True))
        if has_res:
            dx = dx + r_ref[...]
        dx_ref[...] = dx

        @pl.when(pl.program_id(0) == 0)
        def _():
            dg_ref[...] = jnp.zeros_like(dg_ref)

        dg_ref[...] += jnp.sum(dhv * xh, axis=0, keepdims=True)

    row = pl.BlockSpec((tm, d), lambda i: (i, 0))
    vec = pl.BlockSpec((1, d), lambda i: (0, 0))
    return _pcall(
        body, name=name, out_shape=(_sds((t, d), F32), _sds((1, d), F32)), grid=(t // tm,),
        in_specs=[row, row, vec] + ([row] if has_res else []), out_specs=(row, vec),
        compiler_params=_cp(dimension_semantics=("arbitrary",)),
    )(*([dh, x, g] + ([res] if has_res else [])))


def _conv_fwd(z, conv_w, conv_b, *, nb, s, d, tc=256):
    nc = d // tc

    def body(cb_ref, cc_ref, cv_ref, w_ref, b_ref, y_ref):
        u = cc_ref[...].astype(F32) * cv_ref[...].astype(F32)
        row = lax.broadcasted_iota(jnp.int32, u.shape, 0)
        u1 = jnp.where(row >= 1, pltpu.roll(u, 1, axis=0), 0.0)
        u2 = jnp.where(row >= 2, pltpu.roll(u, 2, axis=0), 0.0)
        w = w_ref[...]
        y = w[2:3, :] * u + w[1:2, :] * u1 + w[0:1, :] * u2 + b_ref[...]
        y_ref[...] = (cb_ref[...].astype(F32) * y).astype(BF16)

    def part(p):
        return pl.BlockSpec((s, tc), lambda b, j: (b, p * nc + j))

    return _pcall(
        body, name="conv_fwd", out_shape=_sds((nb * s, d), BF16), grid=(nb, nc),
        in_specs=[part(0), part(1), part(2), pl.BlockSpec((3, tc), lambda b, j: (0, j)), pl.BlockSpec((1, tc), lambda b, j: (0, j))],
        out_specs=pl.BlockSpec((s, tc), lambda b, j: (b, j)), compiler_params=_cp(),
    )(z, z, z, conv_w, conv_b)


def _conv_bwd(z, dy, conv_w, conv_b, dz, *, nb, s, d, tc=256, comm=None):
    nc = d // tc
    c_ins, c_in_specs, c_out_shape, c_scratch = _comm_lists(comm)
    n_steps = nc * nb * 3

    def body(*refs):
        _, ins, cin, outs, cout, scr, csem = _split_refs(refs, 0, 7, len(c_ins), 3, len(c_ins), 1)
        cb_ref, cc_ref, cv_ref, dy_ref, w_ref, b_ref, _ = ins
        dz_ref, dw_ref, db_ref = outs
        stash = scr[0]
        b_i, p = pl.program_id(1), pl.program_id(2)
        step = (pl.program_id(0) * nb + b_i) * 3 + p
        _run_comm(comm, cin, cout, csem, step, n_steps, "start")

        @pl.when(p == 0)
        def _():
            cc, cv = cc_ref[...].astype(F32), cv_ref[...].astype(F32)
            u = cc * cv
            n = u.shape[0]
            row = lax.broadcasted_iota(jnp.int32, u.shape, 0)
            u1 = jnp.where(row >= 1, pltpu.roll(u, 1, axis=0), 0.0)
            u2 = jnp.where(row >= 2, pltpu.roll(u, 2, axis=0), 0.0)
            w = w_ref[...]
            y = w[2:3, :] * u + w[1:2, :] * u1 + w[0:1, :] * u2 + b_ref[...]
            dyv = dy_ref[...]
            dconv = dyv * cb_ref[...].astype(F32)
            g1 = jnp.where(row < n - 1, pltpu.roll(dconv, n - 1, axis=0), 0.0)
            g2 = jnp.where(row < n - 2, pltpu.roll(dconv, n - 2, axis=0), 0.0)
            du = w[2:3, :] * dconv + w[1:2, :] * g1 + w[0:1, :] * g2
            stash[0] = (dyv * y).astype(BF16)
            stash[1] = (du * cv).astype(BF16)
            stash[2] = (du * cc).astype(BF16)
            dws = [jnp.sum(dconv * u2, axis=0, keepdims=True), jnp.sum(dconv * u1, axis=0, keepdims=True),
                   jnp.sum(dconv * u, axis=0, keepdims=True)]
            db = jnp.sum(dconv, axis=0, keepdims=True)

            @pl.when(b_i == 0)
            def _():
                for r in range(3):
                    dw_ref[r:r + 1, :] = dws[r]
                db_ref[...] = db

            @pl.when(b_i > 0)
            def _():
                for r in range(3):
                    dw_ref[r:r + 1, :] += dws[r]
                db_ref[...] += db

        dz_ref[...] = stash[p]
        _run_comm(comm, cin, cout, csem, step, n_steps, "end")

    outs = _pcall(
        body, name="conv_bwd",
        out_shape=tuple([_sds(dz.shape, dz.dtype), _sds((3, d), F32), _sds((1, d), F32)] + c_out_shape), grid=(nc, nb, 3),
        in_specs=[
            pl.BlockSpec((s, tc), lambda j, b, p: (b, j)), pl.BlockSpec((s, tc), lambda j, b, p: (b, nc + j)),
            pl.BlockSpec((s, tc), lambda j, b, p: (b, 2 * nc + j)), pl.BlockSpec((s, tc), lambda j, b, p: (b, j)),
            pl.BlockSpec((3, tc), lambda j, b, p: (0, j)), pl.BlockSpec((1, tc), lambda j, b, p: (0, j)),
            HBM_SPEC,
        ] + c_in_specs,
        out_specs=tuple([pl.BlockSpec((s, tc), lambda j, b, p: (b, p * nc + j)),
                         pl.BlockSpec((3, tc), lambda j, b, p: (0, j)), pl.BlockSpec((1, tc), lambda j, b, p: (0, j))]
                        + [HBM_SPEC] * len(c_ins)),
        scratch_shapes=[pltpu.VMEM((3, s, tc), BF16)] + c_scratch,
        input_output_aliases={6: 0},
        compiler_params=_cp(dimension_semantics=("arbitrary", "arbitrary", "arbitrary")),
    )(z, z, z, dy, conv_w, conv_b, dz, *c_ins)
    return outs[0], outs[1], outs[2], list(outs[3:])


def _head_rms(xv, g):
    r = lax.rsqrt(jnp.mean(xv * xv, axis=-1, keepdims=True) + EPS)
    return xv * r * g


def _head_rms_bwd(xv, g, dy):
    r = lax.rsqrt(jnp.mean(xv * xv, axis=-1, keepdims=True) + EPS)
    xh = xv * r
    gd = dy * g
    dx = r * (gd - xh * jnp.mean(gd * xh, axis=-1, keepdims=True))
    return dx, jnp.sum(dy * xh, axis=0, keepdims=True)


def _fox_prep(z, q_g, k_g, *, t, d, tm=512):
    nh = d // HD

    def body(z_ref, qg_ref, kg_ref, q_ref, k_ref, v_ref):
        qg, kg = qg_ref[...], kg_ref[...]
        for h in range(nh):
            c0 = h * HD
            q_ref[:, c0:c0 + HD] = _head_rms(z_ref[:, c0:c0 + HD].astype(F32), qg).astype(BF16)
            k_ref[:, c0:c0 + HD] = _head_rms(z_ref[:, d + c0:d + c0 + HD].astype(F32), kg).astype(BF16)
        v_ref[...] = z_ref[:, 2 * d:3 * d]

    o = pl.BlockSpec((tm, d), lambda i: (i, 0))
    g = pl.BlockSpec((1, HD), lambda i: (0, 0))
    return _pcall(
        body, name="fox_prep", out_shape=(_sds((t, d), BF16),) * 3, grid=(t // tm,),
        in_specs=[pl.BlockSpec((tm, 3 * d), lambda i: (i, 1)), g, g], out_specs=(o, o, o), compiler_params=_cp(),
    )(z, q_g, k_g)


def _lane_cumsum(v, reverse=False):
    n = v.shape[1]
    lane = lax.broadcasted_iota(jnp.int32, v.shape, 1)
    sh = 1
    while sh < n:
        if reverse:
            v = v + jnp.where(lane < n - sh, pltpu.roll(v, n - sh, axis=1), 0.0)
        else:
            v = v + jnp.where(lane >= sh, pltpu.roll(v, sh, axis=1), 0.0)
        sh *= 2
    return v


def _fox_gates(z, f_bias_col, *, nb, s, nh):
    def body(z_ref, b_ref, c_ref):
        ff = z_ref[...].T[0:nh, :] + b_ref[...]
        lf = jnp.minimum(ff, 0.0) - jnp.log(1.0 + jnp.exp(-jnp.abs(ff)))
        c_ref[0] = _lane_cumsum(lf) * SQRT_HD

    return _pcall(
        body, name="fox_gates", out_shape=_sds((nb, nh, s), F32), grid=(nb,),
        in_specs=[pl.BlockSpec((s, HD), lambda b: (b, 0)), pl.BlockSpec((nh, 1), lambda b: (0, 0))],
        out_specs=pl.BlockSpec((1, nh, s), lambda b: (b, 0, 0)), compiler_params=_cp(),
    )(z, f_bias_col)


def _fox_gates_bwd(z, f_bias_col, row_sums, col_sums, dz, *, nb, s, nh, ff_block):
    def body(z_ref, b_ref, rs_ref, cs_ref, dz_in, dz_ref, db_ref, dc_sc):
        b_i, h = pl.program_id(0), pl.program_id(1)
        dc_sc[pl.ds(h, 1), :] = (rs_ref[...] - cs_ref[...]).T[0:1, :]

        @pl.when(h == nh - 1)
        def _():
            ff = z_ref[...].T[0:nh, :] + b_ref[...]
            dlf = _lane_cumsum(dc_sc[...], reverse=True)
            dff = dlf * (1.0 / (1.0 + jnp.exp(ff)))
            pad = jnp.concatenate([dff, jnp.zeros((HD - nh, s), F32)], axis=0)
            dz_ref[...] = pad.T.astype(BF16)
            dbv = jnp.sum(dff, axis=1, keepdims=True)

            @pl.when(b_i == 0)
            def _():
                db_ref[...] = dbv

            @pl.when(b_i > 0)
            def _():
                db_ref[...] += dbv

    hs = pl.BlockSpec((s, HD), lambda b, h: (b, h))
    return _pcall(
        body, name="fox_gates_bwd", out_shape=(_sds(dz.shape, dz.dtype), _sds((nh, 1), F32)), grid=(nb, nh),
        in_specs=[pl.BlockSpec((s, HD), lambda b, h: (b, 0)), pl.BlockSpec((nh, 1), lambda b, h: (0, 0)), hs, hs, HBM_SPEC],
        out_specs=(pl.BlockSpec((s, HD), lambda b, h: (b, ff_block)), pl.BlockSpec((nh, 1), lambda b, h: (0, 0))),
        scratch_shapes=[pltpu.VMEM((nh, s), F32)],
        input_output_aliases={4: 0}, compiler_params=_cp(dimension_semantics=("arbitrary", "arbitrary")),
    )(z, f_bias_col, row_sums, col_sums, dz)


def _fox_pairs(nq, key_major):
    if key_major:
        pairs = [(i, j) for j in range(nq) for i in range(j, nq)]
    else:
        pairs = [(i, j) for i in range(nq) for j in range(i + 1)]
    return jnp.array([p[0] for p in pairs], jnp.int32), jnp.array([p[1] for p in pairs], jnp.int32)


def _with_ones(x):
    return jnp.concatenate([x, jnp.ones_like(x)], axis=1)


def _fox_specs(nb, nh, nq, tq, tk, hp):
    qs = pl.BlockSpec((tq, hp * HD), lambda b, h, p, *tb: (b * nq + tb[0][p], h))
    ks = pl.BlockSpec((tk, hp * HD), lambda b, h, p, *tb: (b * nq + tb[1][p], h))
    cs = pl.BlockSpec((hp, 1, tk), lambda b, h, p, *tb: (b * (nh // hp) + h, 0, tb[1][p]))
    return qs, ks, cs


def _fox_raw(qv, kv, ckv, diag, tq, tk):
    sc = lax.dot_general(qv, kv, (((1,), (1,)), ((), ())), preferred_element_type=F32) - ckv
    if diag:
        rows = lax.broadcasted_iota(jnp.int32, (tq, tk), 0)
        cols = lax.broadcasted_iota(jnp.int32, (tq, tk), 1)
        sc = jnp.where(rows >= cols, sc, NEG)
    return sc


def _fox_probs(qv, kv, ckv, lse_col, diag, tq, tk):
    sc = _fox_raw(qv, kv, ckv, diag, tq, tk)
    return jnp.exp2(sc * EXP2_C - lse_col * LOG2E)


def _fox_call(body, *, name, ins, in_specs, out_shape, out_specs, scratch, grid, tables, comm):
    c_ins, c_in_specs, c_out_shape, c_scratch = _comm_lists(comm)
    n_in, n_out, n_scr, nc = len(ins), len(out_shape), len(scratch), len(c_ins)
    n_steps = grid[0] * grid[1] * grid[2]

    def wrapped(*refs):
        pre, rin, cin, rout, cout, scr, csem = _split_refs(refs, len(tables), n_in, nc, n_out, nc, n_scr)
        step = (pl.program_id(0) * grid[1] + pl.program_id(1)) * grid[2] + pl.program_id(2)
        _run_comm(comm, cin, cout, csem, step, n_steps, "start")
        body(*pre, *rin, *rout, *scr)
        _run_comm(comm, cin, cout, csem, step, n_steps, "end")

    sem = ("arbitrary",) * 3 if comm is not None else ("parallel", "parallel", "arbitrary")
    outs = _pcall(
        wrapped, name=name, out_shape=tuple(list(out_shape) + c_out_shape),
        grid_spec=pltpu.PrefetchScalarGridSpec(
            num_scalar_prefetch=len(tables), grid=grid, in_specs=list(in_specs) + c_in_specs,
            out_specs=tuple(list(out_specs) + [HBM_SPEC] * nc), scratch_shapes=list(scratch) + c_scratch),
        compiler_params=_cp(dimension_semantics=sem),
    )(*tables, *ins, *c_ins)
    return list(outs[:n_out]), list(outs[n_out:])


def _fox_fwd(q, k, v, cks, *, nb, s, nh, tq=512, comm=None):
    tk = tq
    nq = s // tq
    t = nb * s
    hp = FOX_HP
    it, jt = _fox_pairs(nq, key_major=False)

    def body(it_ref, jt_ref, q_ref, k_ref, v_ref, c_ref, o_ref, lse_ref, m_sc, acc_sc):
        p_id = pl.program_id(2)
        i, j = it_ref[p_id], jt_ref[p_id]

        @pl.when(j == 0)
        def _():
            m_sc[...] = jnp.full_like(m_sc, NEG)
            acc_sc[...] = jnp.zeros_like(acc_sc)

        def step(diag):
            for hh in range(hp):
                cols = slice(hh * HD, (hh + 1) * HD)
                sc = _fox_raw(q_ref[:, cols], k_ref[:, cols], c_ref[hh], diag, tq, tk)
                m_old = m_sc[hh]
                m_new = jnp.maximum(m_old, jnp.max(sc, axis=-1, keepdims=True))
                alpha = jnp.exp2((m_old - m_new) * EXP2_C)
                p = jnp.exp2((sc - m_new) * EXP2_C).astype(BF16)
                acc = alpha * acc_sc[hh] + jnp.dot(p, _with_ones(v_ref[:, cols]), preferred_element_type=F32)
                if diag:
                    l = acc[:, HD:2 * HD]
                    o_ref[:, cols] = (acc[:, 0:HD] / l).astype(BF16)
                    lse_ref[:, cols] = m_new * (1.0 / SQRT_HD) + jnp.log(l)
                else:
                    acc_sc[hh] = acc
                    m_sc[hh] = m_new

        @pl.when(j < i)
        def _():
            step(False)

        @pl.when(j == i)
        def _():
            step(True)

    qs, ks, cs = _fox_specs(nb, nh, nq, tq, tk, hp)
    (o, lse), c_out = _fox_call(
        body, name="fox_fwd", ins=[q, k, v, cks], in_specs=[qs, ks, ks, cs],
        out_shape=[_sds((t, nh * HD), BF16), _sds((t, nh * HD), F32)], out_specs=[qs, qs],
        scratch=[pltpu.VMEM((hp, tq, 1), F32), pltpu.VMEM((hp, tq, 2 * HD), F32)],
        grid=(nb, nh // hp, int(it.shape[0])), tables=(it, jt), comm=comm)
    return o, lse, c_out


def _fox_bwd(q, k, v, cks, do, o, lse, *, nb, s, nh, tq=512, comm=None):
    tk = tq
    nq = s // tq
    t = nb * s
    hp = FOX_HP
    it, jt = _fox_pairs(nq, key_major=True)
    done = jnp.array([j for j in range(nq) for _ in range(j, nq)], jnp.int32)

    def body(it_ref, jt_ref, done_ref, q_ref, k_ref, v_ref, c_ref, do_ref, o_ref, lse_ref,
             dq_ref, rs_ref, dk_ref, dv_ref, cs_ref, dq_sc, dk_sc, dv_sc):
        p_id = pl.program_id(2)
        i, j = it_ref[p_id], jt_ref[p_id]
        rows_i = pl.ds(pl.multiple_of(i * tq, tq), tq)

        def step(diag):
            for hh in range(hp):
                cols = slice(hh * HD, (hh + 1) * HD)
                qv, kv = q_ref[:, cols], k_ref[:, cols]
                p = _fox_probs(qv, kv, c_ref[hh], lse_ref[:, hh * HD:hh * HD + 1], diag, tq, tk)
                dov = do_ref[:, cols]
                dob = dov.astype(BF16)
                dvp = lax.dot_general(p.astype(BF16), dob, (((0,), (0,)), ((), ())), preferred_element_type=F32)
                dp = lax.dot_general(dob, v_ref[:, cols], (((1,), (1,)), ((), ())), preferred_element_type=F32)
                delta = jnp.sum(dov * o_ref[:, cols].astype(F32), axis=-1, keepdims=True)
                ds = (p * (dp - delta)).astype(BF16)
                dkp = lax.dot_general(ds, _with_ones(qv), (((0,), (0,)), ((), ())), preferred_element_type=F32)
                dqp = jnp.dot(ds, _with_ones(kv), preferred_element_type=F32)
                if diag:
                    dk_sc[hh] = dkp
                    dv_sc[hh] = dvp
                else:
                    dk_sc[hh] += dkp
                    dv_sc[hh] += dvp

                @pl.when(j == 0)
                def _():
                    dq_sc[hh, rows_i, :] = dqp

                @pl.when(j > 0)
                def _():
                    dq_sc[hh, rows_i, :] += dqp

                if diag:
                    acc = dq_sc[hh, rows_i, :]
                    dq_ref[:, cols] = acc[:, 0:HD] * (1.0 / SQRT_HD)
                    rs_ref[:, cols] = acc[:, HD:2 * HD]

        @pl.when(i == j)
        def _():
            step(True)

        @pl.when(i > j)
        def _():
            step(False)

        @pl.when(i == nq - 1)
        def _():
            for hh in range(hp):
                cols = slice(hh * HD, (hh + 1) * HD)
                dk_ref[:, cols] = dk_sc[hh, :, 0:HD] * (1.0 / SQRT_HD)
                cs_ref[:, cols] = dk_sc[hh, :, HD:2 * HD]
                dv_ref[:, cols] = dv_sc[hh].astype(BF16)

    qs, ks, cs = _fox_specs(nb, nh, nq, tq, tk, hp)
    ds_spec = pl.BlockSpec((tq, hp * HD), lambda b, h, p, *tb: (b * nq + tb[2][p], h))
    (dq, rs, dk, dv, csum), c_out = _fox_call(
        body, name="fox_bwd", ins=[q, k, v, cks, do, o, lse], in_specs=[qs, ks, ks, cs, qs, qs, qs],
        out_shape=[_sds((t, nh * HD), F32), _sds((t, nh * HD), F32), _sds((t, nh * HD), F32), _sds((t, nh * HD), BF16),
                   _sds((t, nh * HD), F32)],
        out_specs=[ds_spec, ds_spec, ks, ks, ks],
        scratch=[pltpu.VMEM((hp, s, 2 * HD), F32), pltpu.VMEM((hp, tk, 2 * HD), F32), pltpu.VMEM((hp, tk, HD), F32)],
        grid=(nb, nh // hp, int(it.shape[0])), tables=(it, jt, done), comm=comm)
    return dq, rs, dk, dv, csum, c_out


def _fox_post(z, dqn, dkn, dv, q_g, k_g, dz, *, t, d, tm=512):
    nh = d // HD

    def body(z_ref, dq_ref, dk_ref, dv_ref, qg_ref, kg_ref, dz_in, dz_ref, dqg_ref, dkg_ref):
        qg, kg = qg_ref[...], kg_ref[...]
        aq = jnp.zeros((1, HD), F32)
        ak = jnp.zeros((1, HD), F32)
        for h in range(nh):
            c0 = h * HD
            dx, sg = _head_rms_bwd(z_ref[:, c0:c0 + HD].astype(F32), qg, dq_ref[:, c0:c0 + HD])
            dz_ref[:, c0:c0 + HD] = dx.astype(BF16)
            aq = aq + sg
            dx, sg = _head_rms_bwd(z_ref[:, d + c0:d + c0 + HD].astype(F32), kg, dk_ref[:, c0:c0 + HD])
            dz_ref[:, d + c0:d + c0 + HD] = dx.astype(BF16)
            ak = ak + sg
        dz_ref[:, 2 * d:3 * d] = dv_ref[...]

        @pl.when(pl.program_id(0) == 0)
        def _():
            dqg_ref[...] = aq
            dkg_ref[...] = ak

        @pl.when(pl.program_id(0) > 0)
        def _():
            dqg_ref[...] += aq
            dkg_ref[...] += ak

    o = pl.BlockSpec((tm, d), lambda i: (i, 0))
    g = pl.BlockSpec((1, HD), lambda i: (0, 0))
    z3 = pl.BlockSpec((tm, 3 * d), lambda i: (i, 1))
    return _pcall(
        body, name="fox_post", out_shape=(_sds(dz.shape, dz.dtype), _sds((1, HD), F32), _sds((1, HD), F32)), grid=(t // tm,),
        in_specs=[z3, o, o, o, g, g, HBM_SPEC], out_specs=(z3, g, g),
        input_output_aliases={6: 0}, compiler_params=_cp(dimension_semantics=("arbitrary",)),
    )(z, dqn, dkn, dv, q_g, k_g, dz)


def _xa_prep_kv(kv, k_g, *, rows, d):
    xd = d // XH

    def body(kv_ref, g_ref, k_ref, v_ref):
        g = g_ref[...]
        for h in range(XH):
            c0 = h * xd
            k_ref[:, c0:c0 + xd] = _head_rms(kv_ref[:, c0:c0 + xd], g).astype(BF16)
        v_ref[...] = kv_ref[:, d:2 * d].astype(BF16)

    return _pcall(body, name="xa_prep_kv", out_shape=(_sds((rows, d), BF16),) * 2, compiler_params=_cp())(kv, k_g)


def _xa_fwd(z, kn, vb, q_g, *, nb, s, d, nm, q_block, tq=512):
    xd = d // XH
    nq = s // tq

    def body(z_ref, k_ref, v_ref, g_ref, y_ref):
        g = g_ref[...]
        for h in range(XH):
            c0 = h * xd
            qn = _head_rms(z_ref[:, c0:c0 + xd].astype(F32), g).astype(BF16)
            sc = lax.dot_general(qn, k_ref[:, c0:c0 + xd], (((1,), (1,)), ((), ())), preferred_element_type=F32)
            sc = sc / math.sqrt(xd)
            p = jnp.exp(sc - jnp.max(sc, axis=-1, keepdims=True))
            p = p / jnp.sum(p, axis=-1, keepdims=True)
            y_ref[:, c0:c0 + xd] = jnp.dot(p.astype(BF16), v_ref[:, c0:c0 + xd], preferred_element_type=F32).astype(BF16)

    kvs = pl.BlockSpec((nm, d), lambda b, i: (b, 0))
    return _pcall(
        body, name="xa_fwd", out_shape=_sds((nb * s, d), BF16), grid=(nb, nq),
        in_specs=[pl.BlockSpec((tq, d), lambda b, i: (b * nq + i, q_block)), kvs, kvs, pl.BlockSpec((1, xd), lambda b, i: (0, 0))],
        out_specs=pl.BlockSpec((tq, d), lambda b, i: (b * nq + i, 0)), compiler_params=_cp(),
    )(z, kn, vb, q_g)


def _xa_bwd(z, kn, vb, q_g, dy, dz, *, nb, s, d, nm, q_block, tq=512):
    xd = d // XH
    nq = s // tq

    def body(z_ref, k_ref, v_ref, g_ref, dy_ref, dz_in, dz_ref, dk_ref, dv_ref, dg_ref):
        b_i, i = pl.program_id(0), pl.program_id(1)
        g = g_ref[...]

        @pl.when(i == 0)
        def _():
            dk_ref[...] = jnp.zeros_like(dk_ref)
            dv_ref[...] = jnp.zeros_like(dv_ref)

        @pl.when((i == 0) & (b_i == 0))
        def _():
            dg_ref[...] = jnp.zeros_like(dg_ref)

        for h in range(XH):
            c0 = h * xd
            xq = z_ref[:, c0:c0 + xd].astype(F32)
            qn = _head_rms(xq, g).astype(BF16)
            kh, vh = k_ref[:, c0:c0 + xd], v_ref[:, c0:c0 + xd]
            sc = lax.dot_general(qn, kh, (((1,), (1,)), ((), ())), preferred_element_type=F32) / math.sqrt(xd)
            p = jnp.exp(sc - jnp.max(sc, axis=-1, keepdims=True))
            p = p / jnp.sum(p, axis=-1, keepdims=True)
            dob = dy_ref[:, c0:c0 + xd].astype(BF16)
            dv_ref[:, c0:c0 + xd] += lax.dot_general(p.astype(BF16), dob, (((0,), (0,)), ((), ())), preferred_element_type=F32)
            dp = lax.dot_general(dob, vh, (((1,), (1,)), ((), ())), preferred_element_type=F32)
            ds = p * (dp - jnp.sum(p * dp, axis=-1, keepdims=True)) / math.sqrt(xd)
            dsb = ds.astype(BF16)
            dqn = jnp.dot(dsb, kh, preferred_element_type=F32)
            dk_ref[:, c0:c0 + xd] += lax.dot_general(dsb, qn, (((0,), (0,)), ((), ())), preferred_element_type=F32)
            dx, sg = _head_rms_bwd(xq, g, dqn)
            dz_ref[:, c0:c0 + xd] = dx.astype(BF16)
            dg_ref[...] += sg

    kvs = pl.BlockSpec((nm, d), lambda b, i: (b, 0))
    qs = pl.BlockSpec((tq, d), lambda b, i: (b * nq + i, q_block))
    gs = pl.BlockSpec((1, xd), lambda b, i: (0, 0))
    return _pcall(
        body, name="xa_bwd",
        out_shape=(_sds(dz.shape, dz.dtype), _sds((nb * nm, d), F32), _sds((nb * nm, d), F32), _sds((1, xd), F32)), grid=(nb, nq),
        in_specs=[qs, kvs, kvs, gs, pl.BlockSpec((tq, d), lambda b, i: (b * nq + i, 0)), HBM_SPEC],
        out_specs=(qs, kvs, kvs, gs), input_output_aliases={5: 0},
        compiler_params=_cp(dimension_semantics=("arbitrary", "arbitrary")),
    )(z, kn, vb, q_g, dy, dz)


def _xa_post_kv(kv, dkn, dv, k_g, *, rows, d):
    xd = d // XH

    def body(kv_ref, dk_ref, dv_ref, g_ref, dkv_ref, dg_ref):
        g = g_ref[...]
        acc = jnp.zeros((1, xd), F32)
        for h in range(XH):
            c0 = h * xd
            dx, sg = _head_rms_bwd(kv_ref[:, c0:c0 + xd], g, dk_ref[:, c0:c0 + xd])
            dkv_ref[:, c0:c0 + xd] = dx.astype(BF16)
            acc = acc + sg
        dkv_ref[:, d:2 * d] = dv_ref[...].astype(BF16)
        dg_ref[...] = acc

    return _pcall(body, name="xa_post_kv", out_shape=(_sds((rows, 2 * d), BF16), _sds((1, xd), F32)), compiler_params=_cp())(
        kv, dkn, dv, k_g)


def _sigmoid(v):
    return 1.0 / (1.0 + jnp.exp(-v))


def _branches_fwd(z, ys, ws, *, t, d, gate_block, tm=1024, tn=512, comm=None):
    nj = d // tn
    c_ins, c_in_specs, c_out_shape, c_scratch = _comm_lists(comm)
    n_steps = (t // tm) * nj

    def body(*refs):
        _, ins, cin, outs, cout, _, csem = _split_refs(refs, 0, 9, len(c_ins), 2, len(c_ins), 0)
        ya_ref, yb_ref, yc_ref, wa_ref, wb_ref, wc_ref, ga_ref, gb_ref, gc_ref = ins
        m_ref, p_ref = outs
        step = pl.program_id(0) * nj + pl.program_id(1)
        _run_comm(comm, cin, cout, csem, step, n_steps, "start")
        acc = None
        for q, (y_ref, w_ref, g_ref) in enumerate(((ya_ref, wa_ref, ga_ref), (yb_ref, wb_ref, gb_ref), (yc_ref, wc_ref, gc_ref))):
            pr = jnp.dot(y_ref[...], w_ref[...], preferred_element_type=F32)
            p_ref[q] = pr.astype(BF16)
            term = _sigmoid(g_ref[...].astype(F32)) * pr
            acc = term if acc is None else acc + term
        m_ref[...] = acc.astype(BF16)
        _run_comm(comm, cin, cout, csem, step, n_steps, "end")

    y_spec = pl.BlockSpec((tm, d), lambda i, j: (i, 0))
    w_spec = pl.BlockSpec((d, tn), lambda i, j: (0, j))

    def gate(q):
        return pl.BlockSpec((tm, tn), lambda i, j: (i, (gate_block + q) * nj + j))

    outs = _pcall(
        body, name="branches_fwd", out_shape=tuple([_sds((t, d), BF16), _sds((3, t, d), BF16)] + c_out_shape), grid=(t // tm, nj),
        in_specs=[y_spec] * 3 + [w_spec] * 3 + [gate(0), gate(1), gate(2)] + c_in_specs,
        out_specs=tuple([pl.BlockSpec((tm, tn), lambda i, j: (i, j)), pl.BlockSpec((3, tm, tn), lambda i, j: (0, i, j))]
                        + [HBM_SPEC] * len(c_ins)),
        scratch_shapes=c_scratch, compiler_params=_cp(dimension_semantics=("arbitrary", "arbitrary")),
    )(*ys, *ws, z, z, z, *c_ins)
    return outs[0], outs[1], list(outs[2:])


def _gates_bwd(z, dm, proj, dz, *, t, d, gate_block, tm=1024):
    def body(g_ref, dm_ref, p_ref, dz_in, dz_ref, da_ref, db_ref, dc_ref):
        q = pl.program_id(1)
        sg = _sigmoid(g_ref[...].astype(F32))
        dmv = dm_ref[...]
        dz_ref[...] = (dmv * p_ref[0].astype(F32) * sg * (1.0 - sg)).astype(BF16)
        dp = (dmv * sg).astype(BF16)
        @pl.when(q == 0)
        def _():
            da_ref[...] = dp

        @pl.when(q == 1)
        def _():
            db_ref[...] = dp

        @pl.when(q == 2)
        def _():
            dc_ref[...] = dp

    gs = pl.BlockSpec((tm, d), lambda i, q: (i, gate_block + q))
    o = pl.BlockSpec((tm, d), lambda i, q: (i, 0))
    return _pcall(
        body, name="gates_bwd", out_shape=(_sds(dz.shape, dz.dtype),) + (_sds((t, d), BF16),) * 3, grid=(t // tm, 3),
        in_specs=[gs, o, pl.BlockSpec((1, tm, d), lambda i, q: (q, i, 0)), HBM_SPEC], out_specs=(gs, o, o, o),
        input_output_aliases={3: 0}, compiler_params=_cp(dimension_semantics=("arbitrary", "arbitrary")),
    )(z, dm, proj, dz)


def _ffn_in_fwd(h2, wfi_t, *, t, d, dff, tm=512, tn=1408):
    nj = dff // tn

    def body(h_ref, wg_ref, wu_ref, g_ref, u_ref, a_ref):
        hv = h_ref[...]
        dn = (((1,), (1,)), ((), ()))
        gv = lax.dot_general(hv, wg_ref[...], dn, preferred_element_type=F32)
        uv = lax.dot_general(hv, wu_ref[...], dn, preferred_element_type=F32)
        g_ref[...] = gv.astype(BF16)
        u_ref[...] = uv.astype(BF16)
        a_ref[...] = (gv * _sigmoid(gv) * uv).astype(BF16)

    o = pl.BlockSpec((tm, tn), lambda i, j: (i, j))
    return _pcall(
        body, name="ffn_in_fwd", out_shape=(_sds((t, dff), BF16),) * 3, grid=(t // tm, nj),
        in_specs=[pl.BlockSpec((tm, d), lambda i, j: (i, 0)), pl.BlockSpec((tn, d), lambda i, j: (j, 0)),
                  pl.BlockSpec((tn, d), lambda i, j: (nj + j, 0))],
        out_specs=(o, o, o), compiler_params=_cp(),
    )(h2, wfi_t, wfi_t)


def _ffn_out_bwd(dy, wfo, gate, up, *, t, d, dff, tm=512, tn=1408):
    def body(dy_ref, w_ref, g_ref, u_ref, dg_ref, du_ref):
        dav = lax.dot_general(dy_ref[...].astype(BF16), w_ref[...], (((1,), (1,)), ((), ())), preferred_element_type=F32)
        gv = g_ref[...].astype(F32)
        sg = _sigmoid(gv)
        dg_ref[...] = (dav * u_ref[...].astype(F32) * sg * (1.0 + gv * (1.0 - sg))).astype(BF16)
        du_ref[...] = (dav * gv * sg).astype(BF16)

    o = pl.BlockSpec((tm, tn), lambda i, j: (i, j))
    return _pcall(
        body, name="ffn_out_bwd", out_shape=(_sds((t, dff), BF16),) * 2, grid=(t // tm, dff // tn),
        in_specs=[pl.BlockSpec((tm, d), lambda i, j: (i, 0)), pl.BlockSpec((tn, d), lambda i, j: (j, 0)), o, o],
        out_specs=(o, o), compiler_params=_cp(),
    )(dy, wfo, gate, up)


def _ffn_out_loss(act, wfo, x1, target, *, t, d, dff, tm=512):
    def body(a_ref, w_ref, x_ref, t_ref, dy_ref, l_ref):
        yv = x_ref[...] + jnp.dot(a_ref[...], w_ref[...], preferred_element_type=F32)
        e = yv - t_ref[...]
        dy_ref[...] = e * (1.0 / d)

        @pl.when(pl.program_id(0) == 0)
        def _():
            l_ref[...] = jnp.zeros_like(l_ref)

        l_ref[...] += jnp.sum(jnp.sum(e * e, axis=-1, keepdims=True), axis=0, keepdims=True) * (0.5 / d)

    o = pl.BlockSpec((tm, d), lambda i: (i, 0))
    dy, l = _pcall(
        body, name="ffn_out_loss", out_shape=(_sds((t, d), F32), _sds((1, HD), F32)), grid=(t // tm,),
        in_specs=[pl.BlockSpec((tm, dff), lambda i: (i, 0)), pl.BlockSpec((dff, d), lambda i: (0, 0)), o, o],
        out_specs=(o, pl.BlockSpec((1, HD), lambda i: (0, 0))),
        compiler_params=_cp(dimension_semantics=("arbitrary",)),
    )(act, wfo, x1, target)
    return dy, l[0, 0]


def _adamw(w, g, m, v, *, name):
    if w.ndim == 3:
        rows, _, cols = w.shape
        tm = max(tt for tt in range(1, 65) if rows % tt == 0)
        block, imap, grid = (tm, 1, cols), (lambda i, j: (i, 0, 0)), (rows // tm, 1)
    else:
        rows, cols = w.shape
        tm, tn = rows, cols
        if rows > 512 and rows % 8 == 0:
            tm = max(tt for tt in range(8, 513, 8) if rows % tt == 0)
        elif rows > 512 and cols % 128 == 0:
            tn = 128
        block, imap, grid = (tm, tn), (lambda i, j: (i, j)), (rows // tm, cols // tn)
    c1 = 1.0 - ADAM_B1 ** ADAM_STEP
    c2 = 1.0 - ADAM_B2 ** ADAM_STEP

    at = (slice(None), 0, slice(None)) if w.ndim == 3 else Ellipsis

    def body(w_ref, g_ref, m_ref, v_ref, d_ref, nm_ref, nv_ref):
        gv = g_ref[at]
        mn = ADAM_B1 * m_ref[at] + (1.0 - ADAM_B1) * gv
        vn = ADAM_B2 * v_ref[at] + (1.0 - ADAM_B2) * (gv * gv)
        d_ref[at] = -ADAM_LR * ((mn / c1) / (jnp.sqrt(vn / c2) + ADAM_EPS) + ADAM_WD * w_ref[at])
        nm_ref[at] = mn
        nv_ref[at] = vn

    spec = pl.BlockSpec(block, imap)
    return _pcall(
        body, name=name, out_shape=(_sds(w.shape, F32),) * 3, grid=grid,
        in_specs=[spec] * 4, out_specs=(spec,) * 3, compiler_params=_cp(),
    )(w, g, m, v)


def _pad_rows(a, rows):
    return a if a.shape[0] == rows else jnp.pad(a, ((0, rows - a.shape[0]),) + ((0, 0),) * (a.ndim - 1))


class _Pack:
    def __init__(self, row_sizes):
        self.rows = list(row_sizes)
        self.padded = [-(-r // 16) * 16 for r in self.rows]
        self.offs = [sum(self.padded[:i]) for i in range(len(self.rows))]
        self.total = sum(self.padded)

    def stack(self, blocks):
        return jnp.concatenate([_pad_rows(b.astype(BF16), p) for b, p in zip(blocks, self.padded)], axis=0)

    def full(self, gathered, i):
        r = self.rows[i]
        return gathered[:, self.offs[i]:self.offs[i] + r, :].reshape(N_DEV * r, gathered.shape[2])

    def for_core(self, full_grads, core):
        parts = []
        for gfull, r, p in zip(full_grads, self.rows, self.padded):
            pieces = gfull if isinstance(gfull, (tuple, list)) else (gfull,)
            blk = jnp.concatenate(
                [lax.dynamic_index_in_dim(g.reshape(-1, 2, r, g.shape[1]), core, axis=1, keepdims=False) for g in pieces],
                axis=0).astype(BF16)
            if p != r:
                blk = jnp.pad(blk, ((0, 0), (0, p - r), (0, 0)))
            parts.append(blk)
        return jnp.concatenate(parts, axis=1)

    def shard(self, g_pack, i):
        return g_pack[self.offs[i]:self.offs[i] + self.rows[i], :]


def kernel(x, mem, norm1_g, w_in, conv_w, conv_b, fox_f_bias, fox_q_g, fox_k_g, mem_norm_g, w_mem_kv, xa_q_g, xa_k_g, w_br_conv, w_br_fox, w_br_xa, w_o, norm2_g, w_ffn_in, w_ffn_out, loss_target, m_norm1_g, m_w_in, m_conv_w, m_conv_b, m_fox_f_bias, m_fox_q_g, m_fox_k_g, m_mem_norm_g, m_w_mem_kv, m_xa_q_g, m_xa_k_g, m_w_br_conv, m_w_br_fox, m_w_br_xa, m_w_o, m_norm2_g, m_w_ffn_in, m_w_ffn_out, v_norm1_g, v_w_in, v_conv_w, v_conv_b, v_fox_f_bias, v_fox_q_g, v_fox_k_g, v_mem_norm_g, v_w_mem_kv, v_xa_q_g, v_xa_k_g, v_w_br_conv, v_w_br_fox, v_w_br_xa, v_w_o, v_norm2_g, v_w_ffn_in, v_w_ffn_out):
    nb, s, d = x.shape
    t = nb * s
    nm = mem.shape[1]
    nh = d // HD
    dff = w_ffn_out.shape[1] * N_DEV
    n_in = w_in.shape[2] * N_DEV
    n_in_pad = -(-n_in // 1152) * 1152
    ff_block = (10 * d) // HD
    my_c = lax.axis_index("c")
    my_q = 2 * lax.axis_index("x") + lax.axis_index("y")

    pack_a = _Pack([w_in.shape[2]])
    pack_b = _Pack([w_mem_kv.shape[2], w_br_conv.shape[1], w_br_fox.shape[1], w_br_xa.shape[1], w_o.shape[1],
                    w_ffn_in.shape[2], w_ffn_out.shape[1]])
    gather_2 = _Pack([w_mem_kv.shape[2], w_br_conv.shape[1], w_br_fox.shape[1], w_br_xa.shape[1], w_o.shape[1]])
    stack_a = pack_a.stack([w_in[0].T])
    stack_2 = gather_2.stack([w_mem_kv[0].T, w_br_conv[0], w_br_fox[0], w_br_xa[0], w_o[0]])
    stack_fi = w_ffn_in[0].T.astype(BF16)
    stack_fo = w_ffn_out[0].astype(BF16)
    cw_block = _pad_rows(conv_w[0], 8)
    rows_a = w_in.shape[2]
    my_x, my_y = lax.axis_index("x"), lax.axis_index("y")
    arrival = [(my_x, my_y, my_c), (my_x, my_y, 1 - my_c)]
    chip_1, chip_2, chip_d = _chip_order(my_x, my_y, my_c)
    arrival += [(*chip_1, my_c), (*chip_2, 1 - my_c), (*chip_2, my_c), (*chip_1, 1 - my_c), (*chip_d, my_c), (*chip_d, 1 - my_c)]
    order = jnp.stack([4 * ax + 2 * ay + ac for ax, ay, ac in arrival]).astype(jnp.int32)

    x2 = x.reshape(t, d)
    mem2 = mem.reshape(nb * nm, d)
    tgt2 = loss_target.reshape(t, d)
    h1 = _rms_fwd(x2, norm1_g, name="rms1_fwd")
    z, slabs, _ = _z_gathered(h1, stack_a, order, t=t, d=d, n_pad=N_DEV * SLAB_WIN, rows=rows_a)
    z = _z_fix(h1, slabs, z, t=t, d=d, rows=rows_a)
    ff_rows = _pad_rows(slabs[N_DEV - 1, rows_a - nh:rows_a, :], HD)
    zff = _mm(h1, ff_rows, m=t, n=HD, k=d, tb=True, tm=1024, name="mm_z_ff")
    qn, kn, vb = _fox_prep(z, fox_q_g, fox_k_g, t=t, d=d)
    f_bias_col = fox_f_bias.reshape(nh, 1)
    ck_rows = _fox_gates(zff, f_bias_col, nb=nb, s=s, nh=nh).reshape(nb * nh, 1, s)
    y_fox, lse, (gathered_2, cw_g, gathered_fo) = _fox_fwd(qn, kn, vb, ck_rows, nb=nb, s=s, nh=nh,
                                                           comm=_AllGather([stack_2, cw_block, stack_fo], forward_at=0.8))
    wkv_t, wbc, wbf, wbx, wo = (gather_2.full(gathered_2, i) for i in range(5))
    wfo = gathered_fo.reshape(dff, d)
    conv_w_full = jnp.transpose(cw_g[:, 0:3, :], (1, 0, 2)).reshape(3, d)

    y_conv = _conv_fwd(z, conv_w_full, conv_b, nb=nb, s=s, d=d)
    mem_n = _rms_fwd(mem2, mem_norm_g, name="rms_mem_fwd")
    kv = _mm(mem_n, wkv_t, m=nb * nm, n=2 * d, k=d, tb=True, name="mm_kv")
    xkn, xvb = _xa_prep_kv(kv, xa_k_g, rows=nb * nm, d=d)
    y_xa = _xa_fwd(z, xkn, xvb, xa_q_g, nb=nb, s=s, d=d, nm=nm, q_block=6)
    merged, proj, (gathered_fi,) = _branches_fwd(z, (y_conv, y_fox, y_xa), (wbc, wbf, wbx), t=t, d=d, gate_block=7,
                                                 comm=_AllGather([stack_fi]))
    wfi_t = gathered_fi.reshape(2 * dff, d)
    x1 = _mm(merged, wo, m=t, n=d, k=d, res=x2, name="mm_x1")
    h2 = _rms_fwd(x1, norm2_g, name="rms2_fwd")
    ff_gate, ff_up, act = _ffn_in_fwd(h2, wfi_t, t=t, d=d, dff=dff)
    dy, loss_local = _ffn_out_loss(act, wfo, x1, tgt2, t=t, d=d, dff=dff)

    g_wfo = _mm(act, dy, m=dff, n=d, k=t, ta=True, tm=1408, tk=2048, out_dtype=BF16, name="mm_g_wfo")
    dgate, dup = _ffn_out_bwd(dy, wfo, ff_gate, ff_up, t=t, d=d, dff=dff)
    dh2 = _mm(dgate, wfi_t, m=t, n=d, k=dff, tk=dff, name="mm_dh2_g")
    dh2 = _mm(dup, wfi_t, m=t, n=d, k=dff, tk=dff, b_off=(1, 0), res=dh2, name="mm_dh2_u")
    g_wfi_g = _mm(dgate, h2, m=dff, n=d, k=t, ta=True, tm=1408, tk=2048, out_dtype=BF16, name="mm_g_wfi_g")
    g_wfi_u = _mm(dup, h2, m=dff, n=d, k=t, ta=True, tm=1408, tk=2048, out_dtype=BF16, name="mm_g_wfi_u")
    dx1, g_norm2 = _rms_bwd(dh2, x1, norm2_g, dy, name="rms2_bwd")
    dmerged = _mm(dx1, wo, m=t, n=d, k=d, tb=True, name="mm_dmerged")
    g_wo = _mm(merged, dx1, m=d, n=d, k=t, ta=True, tk=2048, out_dtype=BF16, name="mm_g_wo")

    dz = lax.empty((t, n_in_pad), BF16)
    dz, dpa, dpb, dpc = _gates_bwd(z, dmerged, proj, dz, t=t, d=d, gate_block=7)
    dy_conv = _mm(dpa, wbc, m=t, n=d, k=d, tb=True, name="mm_dy_conv")
    g_wbc = _mm(y_conv, dpa, m=d, n=d, k=t, ta=True, tk=4096, out_dtype=BF16, name="mm_g_wbc")
    dy_fox = _mm(dpb, wbf, m=t, n=d, k=d, tb=True, name="mm_dy_fox")
    g_wbf = _mm(y_fox, dpb, m=d, n=d, k=t, ta=True, tk=4096, out_dtype=BF16, name="mm_g_wbf")
    dy_xa = _mm(dpc, wbx, m=t, n=d, k=d, tb=True, name="mm_dy_xa")
    g_wbx = _mm(y_xa, dpc, m=d, n=d, k=t, ta=True, tk=4096, out_dtype=BF16, name="mm_g_wbx")

    dz, dxkn, dxv, g_xa_q = _xa_bwd(z, xkn, xvb, xa_q_g, dy_xa, dz, nb=nb, s=s, d=d, nm=nm, q_block=6)
    dkv, g_xa_k = _xa_post_kv(kv, dxkn, dxv, xa_k_g, rows=nb * nm, d=d)
    g_wkv_t = _mm(dkv, mem_n, m=2 * d, n=d, k=nb * nm, ta=True, out_dtype=BF16, name="mm_g_wkv")
    dmem_n = _mm(dkv, wkv_t, m=nb * nm, n=d, k=2 * d, tk=2 * d, name="mm_dmem")
    _, g_mem_norm = _rms_bwd(dmem_n, mem2, mem_norm_g, None, name="rms_mem_bwd")

    grads_b = [g_wkv_t, g_wbc, g_wbf, g_wbx, g_wo, (g_wfi_g, g_wfi_u), g_wfo]
    keep_b, send_b = pack_b.for_core(grads_b, my_c), pack_b.for_core(grads_b, 1 - my_c)
    dz, g_conv_w, g_conv_b, (from_sibling_b,) = _conv_bwd(z, dy_conv, conv_w_full, conv_b, dz, nb=nb, s=s, d=d,
                                                          comm=_SiblingSwap(send_b))
    part_b = _add2(keep_b, from_sibling_b, name="rs_add_sibling_b")
    dqn, ds_rows, dkn, dvb, ds_cols, (from_chips_b,) = _fox_bwd(qn, kn, vb, ck_rows, dy_fox, y_fox, lse, nb=nb, s=s, nh=nh,
                                                                comm=_ChipExchange(part_b))
    g_pack_b = _add4(lax.dynamic_index_in_dim(part_b, my_q, axis=0, keepdims=False), from_chips_b, name="rs_add_chips_b")
    dz, g_fox_q, g_fox_k = _fox_post(z, dqn, dkn, dvb, fox_q_g, fox_k_g, dz, t=t, d=d)
    dz, g_f_bias = _fox_gates_bwd(zff, f_bias_col, ds_rows, ds_cols, dz, nb=nb, s=s, nh=nh, ff_block=ff_block)

    keep_a, from_sibling_a = _g_win_slabs(dz, h1, my_c, t=t, d=d, n_slab=N_DEV, rows=rows_a, rp=slabs.shape[1])
    part_a = _add2(keep_a, from_sibling_a, name="rs_add_sibling_a")
    dh1, (from_chips_a,) = _dh1_from_slabs(dz, slabs, t=t, d=d, rows=rows_a, comm=_ChipExchange(part_a))
    g_pack_a = _add4(lax.dynamic_index_in_dim(part_a, my_q, axis=0, keepdims=False), from_chips_a, name="rs_add_chips_a",
                     rows_apart=True)
    dx, g_norm1 = _rms_bwd(dh1, x2, norm1_g, dx1, name="rms1_bwd")

    xd = d // XH
    tail = jnp.concatenate(
        [_pad_rows(g_f_bias.reshape(nh), HD).reshape(1, HD), g_fox_q, g_fox_k, g_xa_q, g_xa_k,
         jnp.zeros((1, d - 3 * HD - 2 * xd), F32)], axis=1)
    small = jnp.concatenate([g_norm1, g_norm2, g_conv_w, g_conv_b, g_mem_norm, tail], axis=0)
    small = _all_reduce_small(small)
    gs_conv_w = lax.dynamic_slice_in_dim(small[2:5], (2 * my_q + my_c) * (d // N_DEV), d // N_DEV, axis=1)

    grads = {
        "norm1_g": small[0:1], "w_in": g_pack_a, "conv_w": gs_conv_w, "conv_b": small[5:6],
        "fox_f_bias": small[7:8, 0:nh], "fox_q_g": small[7:8, HD:2 * HD], "fox_k_g": small[7:8, 2 * HD:3 * HD],
        "mem_norm_g": small[6:7], "w_mem_kv": pack_b.shard(g_pack_b, 0), "xa_q_g": small[7:8, 3 * HD:3 * HD + xd],
        "xa_k_g": small[7:8, 3 * HD + xd:3 * HD + 2 * xd], "w_br_conv": pack_b.shard(g_pack_b, 1),
        "w_br_fox": pack_b.shard(g_pack_b, 2), "w_br_xa": pack_b.shard(g_pack_b, 3), "w_o": pack_b.shard(g_pack_b, 4),
        "norm2_g": small[1:2], "w_ffn_in": pack_b.shard(g_pack_b, 5), "w_ffn_out": pack_b.shard(g_pack_b, 6),
    }
    transposed = {"w_in", "w_mem_kv", "w_ffn_in"}
    weights = {
        "norm1_g": (norm1_g, m_norm1_g, v_norm1_g), "w_in": (w_in, m_w_in, v_w_in), "conv_w": (conv_w, m_conv_w, v_conv_w),
        "conv_b": (conv_b, m_conv_b, v_conv_b), "fox_f_bias": (fox_f_bias, m_fox_f_bias, v_fox_f_bias),
        "fox_q_g": (fox_q_g, m_fox_q_g, v_fox_q_g), "fox_k_g": (fox_k_g, m_fox_k_g, v_fox_k_g),
        "mem_norm_g": (mem_norm_g, m_mem_norm_g, v_mem_norm_g), "w_mem_kv": (w_mem_kv, m_w_mem_kv, v_w_mem_kv),
        "xa_q_g": (xa_q_g, m_xa_q_g, v_xa_q_g), "xa_k_g": (xa_k_g, m_xa_k_g, v_xa_k_g),
        "w_br_conv": (w_br_conv, m_w_br_conv, v_w_br_conv), "w_br_fox": (w_br_fox, m_w_br_fox, v_w_br_fox),
        "w_br_xa": (w_br_xa, m_w_br_xa, v_w_br_xa), "w_o": (w_o, m_w_o, v_w_o), "norm2_g": (norm2_g, m_norm2_g, v_norm2_g),
        "w_ffn_in": (w_ffn_in, m_w_ffn_in, v_w_ffn_in), "w_ffn_out": (w_ffn_out, m_w_ffn_out, v_w_ffn_out),
    }
    out_g, out_d, out_m, out_v = [], [], [], []
    for name, (w, m, v) in weights.items():
        shape = w.shape
        if name == "w_in":
            w3, m3, v3 = (a.transpose(2, 0, 1) for a in (w, m, v))
            dl, mn, vn = _adamw(w3, g_pack_a, m3, v3, name="adamw_" + name)
            out_g.append(g_pack_a[:shape[2]].transpose(1, 2, 0))
            out_d.append(dl.transpose(1, 2, 0))
            out_m.append(mn.transpose(1, 2, 0))
            out_v.append(vn.transpose(1, 2, 0))
            continue
        tr = name in transposed

        def as2d(a):
            a = a.reshape(shape[-2], shape[-1])
            return a.T if tr else a

        def back(a):
            return (a.T if tr else a).reshape(shape)

        w2 = as2d(w)
        g2 = grads[name].reshape(w2.shape)
        dl, mn, vn = _adamw(w2, g2, as2d(m), as2d(v), name="adamw_" + name)
        out_g.append(back(g2))
        out_d.append(back(dl))
        out_m.append(back(mn))
        out_v.append(back(vn))

    loss = lax.psum(loss_local, ("x", "y", "c"))
    return (loss, dx.reshape(nb, s, d), *out_g, *out_d, *out_m, *out_v)
```

```python
import math

import jax
import jax.numpy as jnp
from jax import lax
from jax.experimental import pallas as pl
from jax.experimental.pallas import tpu as pltpu

F32, BF16 = jnp.float32, jnp.bfloat16
EPS = 1e-6
HD = 128
XH = 4
N_DEV = 8
MESH = pl.DeviceIdType.MESH
VMEM_LIMIT = 52 * 1024 * 1024
NEG = -1e30
SQRT_HD = math.sqrt(HD)
EXP2_C = math.log2(math.e) / SQRT_HD
LOG2E = math.log2(math.e)
FOX_HP = 4

ADAM_LR, ADAM_B1, ADAM_B2, ADAM_EPS, ADAM_WD, ADAM_STEP = 0.001, 0.9, 0.999, 1e-08, 0.01, 10


def _cp(**kw):
    return pltpu.CompilerParams(vmem_limit_bytes=VMEM_LIMIT, **kw)


def _pcall(body, **kw):
    return pl.pallas_call(body, **kw)


def _sds(shape, dtype):
    return jax.ShapeDtypeStruct(shape, dtype)


HBM_SPEC = pl.BlockSpec(memory_space=pl.ANY)


def _mesh_pos():
    return lax.axis_index("x"), lax.axis_index("y"), lax.axis_index("c")


class _AllGather:
    def __init__(self, blocks, forward_at=None):
        self.blocks = list(blocks)
        self.forward_at = forward_at
        n = len(self.blocks)
        self.out_shape = [_sds((N_DEV,) + b.shape, b.dtype) for b in self.blocks]
        self.scratch = [pltpu.SemaphoreType.DMA((n, 7)), pltpu.SemaphoreType.DMA((n, 7)), pltpu.SemaphoreType.DMA((n,))]

    def bind(self, ins, outs, sems):
        n = len(ins)
        send_sems, recv_sems, local_sems = sems
        x, y, c = _mesh_pos()
        me, sibling = (x, y, c), (x, y, 1 - c)
        chips = [(1 - x, y), (x, 1 - y), (1 - x, 1 - y)]

        def slab(t, dev):
            return outs[t].at[4 * dev[0] + 2 * dev[1] + dev[2]]

        def copy(t, k, block, to, src=None):
            dst = slab(t, block)
            return pltpu.make_async_remote_copy(
                src_ref=dst if src is None else src, dst_ref=dst, send_sem=send_sems.at[t, k], recv_sem=recv_sems.at[t, k],
                device_id=to, device_id_type=MESH)

        mine = [pltpu.make_async_copy(ins[t], slab(t, me), local_sems.at[t]) for t in range(n)]
        first = []
        for t in range(n):
            first.append(copy(t, 0, me, sibling, src=ins[t]))
            first += [copy(t, 1 + j, me, (*chip, c), src=ins[t]) for j, chip in enumerate(chips)]
        passed = [copy(t, 4 + j, (*chip, c), sibling) for j, chip in enumerate(chips) for t in range(n)]

        def start():
            for cp in mine + first:
                cp.start()

        def mid():
            for j, chip in enumerate(chips):
                for t in range(n):
                    copy(t, 1 + j, (*chip, c), me).wait_recv()
                    passed[j * n + t].start()

        def finish():
            for t in range(n):
                copy(t, 0, sibling, me).wait_recv()
                for j, chip in enumerate(chips):
                    copy(t, 4 + j, (*chip, 1 - c), me).wait_recv()
            for cp in first + passed:
                cp.wait_send()
            for cp in mine:
                cp.wait()

        return start, mid, finish


class _SiblingSwap:
    def __init__(self, send):
        self.blocks = [send]
        self.out_shape = [_sds(send.shape, send.dtype)]
        self.scratch = [pltpu.SemaphoreType.DMA, pltpu.SemaphoreType.DMA]

    def bind(self, ins, outs, sems):
        x, y, c = _mesh_pos()
        cp = pltpu.make_async_remote_copy(src_ref=ins[0], dst_ref=outs[0], send_sem=sems[0], recv_sem=sems[1],
                                          device_id=(x, y, 1 - c), device_id_type=MESH)
        return cp.start, (lambda: None), cp.wait


class _ChipExchange:
    def __init__(self, part):
        self.blocks = [part]
        self.out_shape = [_sds((3,) + part.shape[1:], part.dtype)]
        self.scratch = [pltpu.SemaphoreType.DMA((3,)), pltpu.SemaphoreType.DMA((3,))]

    def bind(self, ins, outs, sems):
        x, y, c = _mesh_pos()
        chips = [(1 - x, y), (x, 1 - y), (1 - x, 1 - y)]
        cps = [
            pltpu.make_async_remote_copy(src_ref=ins[0].at[2 * cx + cy], dst_ref=outs[0].at[j], send_sem=sems[0].at[j],
                                         recv_sem=sems[1].at[j], device_id=(cx, cy, c), device_id_type=MESH)
            for j, (cx, cy) in enumerate(chips)
        ]

        def start():
            for cp in cps:
                cp.start()

        def finish():
            for cp in cps:
                cp.wait()

        return start, (lambda: None), finish


def _comm_lists(comm):
    if comm is None:
        return [], [], [], []
    n = len(comm.blocks)
    return list(comm.blocks), [HBM_SPEC] * n, list(comm.out_shape), list(comm.scratch)


def _split_refs(refs, n_pre, n_in, n_cin, n_out, n_cout, n_scr):
    cuts = [n_pre, n_in, n_cin, n_out, n_cout, n_scr]
    out, at = [], 0
    for c in cuts:
        out.append(refs[at:at + c])
        at += c
    out.append(refs[at:])
    return out


def _run_comm(comm, cin, cout, csem, step, n_steps, when="start"):
    if comm is None:
        return
    start, mid, finish = comm.bind(cin, cout, csem)
    if when == "start":
        pl.when(step == 0)(start)
    else:
        frac = getattr(comm, "forward_at", None)
        mid_step = max(n_steps - 2, 0) if frac is None else min(int(frac * n_steps), max(n_steps - 2, 0))
        pl.when(step == mid_step)(mid)
        pl.when(step == n_steps - 1)(finish)


def _all_reduce_small(v):
    rows, cols = v.shape

    def body(v_ref, o_ref, slots, send_sems, recv_sems):
        x, y, c = _mesh_pos()
        me = 4 * x + 2 * y + c
        slots[me] = v_ref[...]
        cps = []
        for k in range(1, N_DEV):
            px, py, pc = x ^ (k >> 2), y ^ ((k >> 1) & 1), c ^ (k & 1)
            cps.append(pltpu.make_async_remote_copy(
                src_ref=v_ref, dst_ref=slots.at[me], send_sem=send_sems.at[k - 1], recv_sem=recv_sems.at[k - 1],
                device_id=(px, py, pc), device_id_type=MESH))
        for cp in cps:
            cp.start()
        for cp in cps:
            cp.wait()
        acc = slots[0]
        for k in range(1, N_DEV):
            acc = acc + slots[k]
        o_ref[...] = acc

    return _pcall(
        body, name="all_reduce_small", out_shape=_sds((rows, cols), F32),
        in_specs=[pl.BlockSpec(memory_space=pltpu.VMEM)], out_specs=pl.BlockSpec(memory_space=pltpu.VMEM),
        scratch_shapes=[pltpu.VMEM((N_DEV, rows, cols), F32), pltpu.SemaphoreType.DMA((7,)), pltpu.SemaphoreType.DMA((7,))],
        compiler_params=_cp(),
    )(v)


def _row_tile(rows, cap=1040):
    return max(tt for tt in range(16, cap + 1, 16) if rows % tt == 0)


def _add2(a, b, *, name):
    nq, rows, cols = a.shape
    tm = _row_tile(rows)

    def body(a_ref, b_ref, o_ref):
        o_ref[...] = (a_ref[...].astype(F32) + b_ref[...].astype(F32)).astype(BF16)

    spec = pl.BlockSpec((1, tm, cols), lambda q, i: (q, i, 0))
    return _pcall(body, name=name, out_shape=_sds(a.shape, BF16), grid=(nq, rows // tm),
                  in_specs=[spec, spec], out_specs=spec, compiler_params=_cp())(a, b)


def _add4(own, recv, *, name, rows_apart=False):
    rows, cols = own.shape
    tm = _row_tile(rows, 256 if rows_apart else 1040)

    def body(o_ref, r_ref, g_ref):
        tot = ((o_ref[...].astype(F32) + r_ref[0].astype(F32)) + r_ref[1].astype(F32)) + r_ref[2].astype(F32)
        if rows_apart:
            g_ref[:, 0, :] = tot
        else:
            g_ref[...] = tot

    if rows_apart:
        out_shape, out_spec = _sds((rows, 1, cols), F32), pl.BlockSpec((tm, 1, cols), lambda i: (i, 0, 0))
    else:
        out_shape, out_spec = _sds((rows, cols), F32), pl.BlockSpec((tm, cols), lambda i: (i, 0))
    return _pcall(
        body, name=name, out_shape=out_shape, grid=(rows // tm,),
        in_specs=[pl.BlockSpec((tm, cols), lambda i: (i, 0)), pl.BlockSpec((3, tm, cols), lambda i: (0, i, 0))],
        out_specs=out_spec, compiler_params=_cp(),
    )(own, recv)


def _mm(a, b, *, m, n, k, name, ta=False, tb=False, b_off=(0, 0), res=None, out_dtype=F32, tm=512, tn=1024, tk=1024,
        comm=None):
    tm, tn, tk = min(tm, m), min(tn, n), min(tk, k)
    assert m % tm == 0 and n % tn == 0 and k % tk == 0, (name, m, n, k, tm, tn, tk)
    nk = k // tk
    grid = (m // tm, n // tn, nk)
    bo0, bo1 = b_off
    if ta:
        a_spec = pl.BlockSpec((tk, tm), lambda i, j, kk: (kk, i))
    else:
        a_spec = pl.BlockSpec((tm, tk), lambda i, j, kk: (i, kk))
    if tb:
        b_spec = pl.BlockSpec((tn, tk), lambda i, j, kk: (j + bo0, kk + bo1))
    else:
        b_spec = pl.BlockSpec((tk, tn), lambda i, j, kk: (kk + bo0, j + bo1))
    o_spec = pl.BlockSpec((tm, tn), lambda i, j, kk: (i, j))
    dn = (((0 if ta else 1,), (1 if tb else 0,)), ((), ()))
    has_res = res is not None
    c_ins, c_in_specs, c_out_shape, c_scratch = _comm_lists(comm)
    n_in = 3 if has_res else 2
    n_scr = 1 if nk > 1 else 0

    def body(*refs):
        _, ins, cin, outs, cout, scr, csem = _split_refs(refs, 0, n_in, len(c_ins), 1, len(c_ins), n_scr)
        a_ref, b_ref = ins[0], ins[1]
        r_ref = ins[2] if has_res else None
        o_ref = outs[0]
        step = (pl.program_id(0) * grid[1] + pl.program_id(1)) * nk + pl.program_id(2)
        _run_comm(comm, cin, cout, csem, step, grid[0] * grid[1] * nk, "start")
        p = lax.dot_general(a_ref[...].astype(BF16), b_ref[...].astype(BF16), dn, preferred_element_type=F32)

        def finish(acc):
            if has_res:
                acc = acc + r_ref[...]
            o_ref[...] = acc.astype(out_dtype)

        if nk == 1:
            finish(p)
        else:
            acc_ref = scr[0]
            kk = pl.program_id(2)

            @pl.when(kk == 0)
            def _():
                acc_ref[...] = p

            @pl.when(kk > 0)
            def _():
                acc_ref[...] += p

            @pl.when(kk == nk - 1)
            def _():
                finish(acc_ref[...])

        _run_comm(comm, cin, cout, csem, step, grid[0] * grid[1] * nk, "end")

    ins = [a, b] + ([res] if has_res else [])
    in_specs = [a_spec, b_spec] + ([o_spec] if has_res else [])
    sem = ("arbitrary",) * 3 if comm is not None else ("parallel", "parallel", "arbitrary")
    outs = _pcall(
        body, name=name, out_shape=tuple([_sds((m, n), out_dtype)] + c_out_shape), grid=grid,
        in_specs=in_specs + c_in_specs, out_specs=tuple([o_spec] + [HBM_SPEC] * len(c_ins)),
        scratch_shapes=([pltpu.VMEM((tm, tn), F32)] if nk > 1 else []) + c_scratch,
        compiler_params=_cp(dimension_semantics=sem),
    )(*(ins + c_ins))
    return outs[0] if comm is None else (outs[0], list(outs[1:]))


SLAB_WIN = 1280
SLAB_TAIL = 48


def _chip_order(x, y, c):
    north = c == 1
    first = (jnp.where(north, 1 - x, x), jnp.where(north, y, 1 - y))
    second = (jnp.where(north, x, 1 - x), jnp.where(north, 1 - y, y))
    return [first, second, (1 - x, 1 - y)]


def _z_gathered(h1, own_slab, order, *, t, d, n_pad, rows, comm=None, tm=1024):
    rp = own_slab.shape[0]
    n_slab = N_DEV
    tm = min(tm, t)
    ni = t // tm
    sh = rows - SLAB_WIN
    assert sh >= 0 and (n_slab - 1) * sh <= 128
    c_ins, c_in_specs, c_out_shape, c_scratch = _comm_lists(comm)
    n_steps = n_slab * ni

    def body(*refs):
        pre, ins, cin, outs, cout, scr, csem = _split_refs(refs, 1, 2, len(c_ins), 2, len(c_ins), 5)
        order_ref = pre[0]
        h_ref, own_ref = ins
        o_ref, slabs_ref = outs
        slab_vm, send_sems, recv_sems, local_sem, load_sem = scr
        p, i = pl.program_id(0), pl.program_id(1)
        step = p * ni + i
        x, y, c = _mesh_pos()
        me, sibling = (x, y, c), (x, y, 1 - c)
        chips = _chip_order(x, y, c)

        def slab(dev):
            return slabs_ref.at[4 * dev[0] + 2 * dev[1] + dev[2]]

        def copy(k, block, to, src=None, part=None):
            dst = slab(block) if part is None else slab(block).at[part]
            return pltpu.make_async_remote_copy(
                src_ref=dst if src is None else src, dst_ref=dst, send_sem=send_sems.at[k], recv_sem=recv_sems.at[k],
                device_id=to, device_id_type=MESH)

        nb_1, nb_2, diag = ((*chip, c) for chip in chips)
        cut = (rp // 32) * 16
        half_a, half_b = pl.ds(0, cut), pl.ds(cut, rp - cut)
        mine = pltpu.make_async_copy(own_ref, slab(me), local_sem)
        first = [copy(0, me, sibling, src=own_ref), copy(1, me, nb_1, src=own_ref), copy(2, me, nb_2, src=own_ref)]
        relays = [copy(3, nb_1, nb_2, part=half_a), copy(4, nb_2, nb_1, part=half_b)]
        axis_of = [jnp.where(c == 1, 0, 1), jnp.where(c == 1, 1, 0), 2]
        passed = [copy(5 + axis_of[j], (*chip, c), sibling) for j, chip in enumerate(chips)]
        later = min(n_steps // 5, n_steps - 1)

        @pl.when(step == 0)
        def _():
            mine.start()
            first[0].start()
            first[1].start()

        @pl.when(step == later)
        def _():
            first[2].start()

        _run_comm(comm, cin, cout, csem, step, n_steps, "start")

        def load(src):
            cp = pltpu.make_async_copy(src, slab_vm, load_sem)
            cp.start()
            cp.wait()

        @pl.when(i == 0)
        def _():
            @pl.when(p == 0)
            def _():
                load(own_ref)

            @pl.when(p == 1)
            def _():
                copy(0, sibling, me).wait_recv()
                load(slab(sibling))

            for j, chip in enumerate(chips):
                @pl.when(p == 2 + 2 * j)
                def _():
                    if j < 2:
                        copy(1 + j, (*chip, c), me).wait_recv()
                        relays[j].start()
                    else:
                        copy(3, diag, me, part=half_a).wait_recv()
                        copy(4, diag, me, part=half_b).wait_recv()
                    passed[j].start()
                    load(slab((*chip, c)))

                other = (1, 0, 2)[j]

                @pl.when(p == 3 + 2 * j)
                def _():
                    copy(5 + axis_of[other], (*chips[other], 1 - c), me).wait_recv()
                    load(slab((*chips[other], 1 - c)))

        k = order_ref[p]
        y_k = lax.dot_general(h_ref[...], slab_vm[...], (((1,), (1,)), ((), ())), preferred_element_type=F32)
        o_ref[...] = pltpu.roll(y_k[:, 0:SLAB_WIN], k * sh, axis=1).astype(BF16)

        @pl.when(step == n_steps - 1)
        def _():
            for cp in first + relays + passed:
                cp.wait_send()
            mine.wait()

        _run_comm(comm, cin, cout, csem, step, n_steps, "end")

    outs = _pcall(
        body, name="mm_z", out_shape=tuple([_sds((t, n_pad), BF16), _sds((n_slab, rp, d), BF16)] + c_out_shape),
        grid_spec=pltpu.PrefetchScalarGridSpec(
            num_scalar_prefetch=1, grid=(n_slab, ni),
            in_specs=[pl.BlockSpec((tm, d), lambda p, i, order: (i, 0)), HBM_SPEC] + c_in_specs,
            out_specs=tuple([pl.BlockSpec((tm, SLAB_WIN), lambda p, i, order: (i, order[p])), HBM_SPEC] + [HBM_SPEC] * len(c_ins)),
            scratch_shapes=[pltpu.VMEM((rp, d), BF16), pltpu.SemaphoreType.DMA((8,)), pltpu.SemaphoreType.DMA((8,)),
                            pltpu.SemaphoreType.DMA, pltpu.SemaphoreType.DMA] + c_scratch),
        compiler_params=_cp(dimension_semantics=("arbitrary", "arbitrary")),
    )(order, h1, own_slab, *c_ins)
    return outs[0], outs[1], list(outs[2:])


def _z_fix(h1, slabs, z, *, t, d, rows, tm=4096):
    n_slab, rp, _ = slabs.shape
    tm = min(tm, t)
    sh = rows - SLAB_WIN
    tail_lo = rp - SLAB_TAIL
    assert rows - (n_slab - 1) * sh >= tail_lo and rp % SLAB_TAIL == 0

    def body(h_ref, t_ref, z_ref, o_ref):
        k = pl.program_id(0) + 1
        p = lax.dot_general(h_ref[...], t_ref[0], (((1,), (1,)), ((), ())), preferred_element_type=F32)
        pp = jnp.concatenate([p, jnp.zeros((tm, 128 - SLAB_TAIL), F32)], axis=1)
        pr = pltpu.roll(pp, 128 - (rows - tail_lo) + k * sh, axis=1)
        lane = lax.broadcasted_iota(jnp.int32, (tm, 128), 1)
        o_ref[...] = jnp.where(lane < k * sh, pr.astype(BF16), z_ref[...])

    blk = pl.BlockSpec((tm, 128), lambda b, i: (i, (SLAB_WIN // 128) * (b + 1)))
    return _pcall(
        body, name="mm_z_fix", out_shape=_sds(z.shape, z.dtype), grid=(n_slab - 1, t // tm),
        in_specs=[pl.BlockSpec((tm, d), lambda b, i: (i, 0)),
                  pl.BlockSpec((1, SLAB_TAIL, d), lambda b, i: (b, tail_lo // SLAB_TAIL, 0)), blk],
        out_specs=blk, input_output_aliases={2: 0}, compiler_params=_cp(),
    )(h1, slabs, z)


def _slab_window(a1_ref, a2_ref, k, sh):
    w = jnp.concatenate([a1_ref[...], a2_ref[...]], axis=1)
    width = SLAB_WIN + 128
    return pltpu.roll(w, width - k * sh, axis=1)


def _g_win_slabs(dz, h1, my_c, *, t, d, n_slab, rows, rp, tk=2048):
    sh = rows - SLAB_WIN
    tk = min(tk, t)
    nk = t // tk
    width = SLAB_WIN + 128
    half = n_slab // 2

    def slab_of(p, c):
        return 2 * (p % half) + jnp.where(p < half, 1 - c, c)

    def body(c_ref, a1_ref, a2_ref, h_ref, keep_ref, land_ref, acc, stage, send_sems, recv_sems):
        p, kk = pl.program_id(0), pl.program_id(1)
        x, y, c = _mesh_pos()
        ws = _slab_window(a1_ref, a2_ref, slab_of(p, c), sh)
        prod = lax.dot_general(ws, h_ref[...], (((0,), (0,)), ((), ())), preferred_element_type=F32)

        def push(q):
            return pltpu.make_async_remote_copy(
                src_ref=stage.at[q % 2], dst_ref=land_ref.at[q], send_sem=send_sems.at[q], recv_sem=recv_sems.at[q],
                device_id=(x, y, 1 - c), device_id_type=MESH)

        @pl.when(kk == 0)
        def _():
            acc[...] = prod

        @pl.when(kk > 0)
        def _():
            acc[...] += prod

        @pl.when(kk == nk - 1)
        def _():
            res = acc[0:rp, :].astype(BF16)
            for q in range(half):
                @pl.when(p == q)
                def _():
                    if q >= 2:
                        push(q - 2).wait_send()
                    stage[q % 2] = res
                    push(q).start()

            @pl.when(p >= half)
            def _():
                keep_ref[0] = res

        @pl.when((p == n_slab - 1) & (kk == nk - 1))
        def _():
            for q in range(max(half - 2, 0), half):
                push(q).wait_send()
            for q in range(half):
                push(q).wait_recv()

    outs = _pcall(
        body, name="mm_g_win", out_shape=(_sds((half, rp, d), BF16), _sds((half, rp, d), BF16)),
        grid_spec=pltpu.PrefetchScalarGridSpec(
            num_scalar_prefetch=1, grid=(n_slab, nk),
            in_specs=[pl.BlockSpec((tk, SLAB_WIN), lambda p, kk, c: (kk, slab_of(p, c[0]))),
                      pl.BlockSpec((tk, 128), lambda p, kk, c: (kk, (SLAB_WIN // 128) * (slab_of(p, c[0]) + 1))),
                      pl.BlockSpec((tk, d), lambda p, kk, c: (kk, 0))],
            out_specs=(pl.BlockSpec((1, rp, d), lambda p, kk, c: (jnp.maximum(p - half, 0), 0, 0)), HBM_SPEC),
            scratch_shapes=[pltpu.VMEM((width, d), F32), pltpu.VMEM((2, rp, d), BF16),
                            pltpu.SemaphoreType.DMA((half,)), pltpu.SemaphoreType.DMA((half,))]),
        compiler_params=_cp(dimension_semantics=("arbitrary", "arbitrary")),
    )(my_c.reshape(1).astype(jnp.int32), dz, dz, h1)
    return outs[0], outs[1]


def _dh1_from_slabs(dz, slabs, *, t, d, rows, comm=None, tm=1024):
    n_slab, rp, _ = slabs.shape
    sh = rows - SLAB_WIN
    width = SLAB_WIN + 128
    c_ins, c_in_specs, c_out_shape, c_scratch = _comm_lists(comm)
    grid = (t // tm, n_slab)

    def body(*refs):
        _, ins, cin, outs, cout, scr, csem = _split_refs(refs, 0, 3, len(c_ins), 1, len(c_ins), 1)
        a1_ref, a2_ref, s_ref = ins
        o_ref, acc = outs[0], scr[0]
        k = pl.program_id(1)
        step = pl.program_id(0) * n_slab + k
        _run_comm(comm, cin, cout, csem, step, grid[0] * n_slab, "start")
        ws = _slab_window(a1_ref, a2_ref, k, sh)
        s_end = jnp.concatenate([s_ref[0, SLAB_WIN:rp, :], jnp.zeros((width - rp, d), BF16)], axis=0)
        p = (jnp.dot(ws[:, 0:SLAB_WIN], s_ref[0, 0:SLAB_WIN, :], preferred_element_type=F32)
             + jnp.dot(ws[:, SLAB_WIN:width], s_end, preferred_element_type=F32))

        @pl.when(k == 0)
        def _():
            acc[...] = p

        @pl.when(k > 0)
        def _():
            acc[...] += p

        @pl.when(k == n_slab - 1)
        def _():
            o_ref[...] = acc[...]

        _run_comm(comm, cin, cout, csem, step, grid[0] * n_slab, "end")

    outs = _pcall(
        body, name="mm_dh1", out_shape=tuple([_sds((t, d), F32)] + c_out_shape), grid=grid,
        in_specs=[pl.BlockSpec((tm, SLAB_WIN), lambda i, k: (i, k)),
                  pl.BlockSpec((tm, 128), lambda i, k: (i, (SLAB_WIN // 128) * (k + 1))),
                  pl.BlockSpec((1, rp, d), lambda i, k: (k, 0, 0))] + c_in_specs,
        out_specs=tuple([pl.BlockSpec((tm, d), lambda i, k: (i, 0))] + [HBM_SPEC] * len(c_ins)),
        scratch_shapes=[pltpu.VMEM((tm, d), F32)] + c_scratch,
        compiler_params=_cp(dimension_semantics=("arbitrary", "arbitrary")),
    )(dz, dz, slabs, *c_ins)
    return outs[0], list(outs[1:])


def _rms_fwd(x, g, *, name, tm=1024):
    t, d = x.shape
    tm = min(tm, t)

    def body(x_ref, g_ref, h_ref):
        xv = x_ref[...]
        r = lax.rsqrt(jnp.mean(xv * xv, axis=-1, keepdims=True) + EPS)
        h_ref[...] = (xv * r * g_ref[...]).astype(BF16)

    return _pcall(
        body, name=name, out_shape=_sds((t, d), BF16), grid=(t // tm,),
        in_specs=[pl.BlockSpec((tm, d), lambda i: (i, 0)), pl.BlockSpec((1, d), lambda i: (0, 0))],
        out_specs=pl.BlockSpec((tm, d), lambda i: (i, 0)), compiler_params=_cp(),
    )(x, g)


def _rms_bwd(dh, x, g, res, *, name, tm=1024):
    t, d = x.shape
    tm = min(tm, t)
    has_res = res is not None

    def body(*refs):
        dh_ref, x_ref, g_ref = refs[:3]
        r_ref = refs[3] if has_res else None
        dx_ref, dg_ref = refs[-2], refs[-1]
        xv = x_ref[...]
        r = lax.rsqrt(jnp.mean(xv * xv, axis=-1, keepdims=True) + EPS)
        xh = xv * r
        dhv = dh_ref[...]
        gd = dhv * g_ref[...]
        dx = r * (gd - xh * jnp.mean(gd * xh, axis=-1, keepdims=True))
        if has_res:
            dx = dx + r_ref[...]
        dx_ref[...] = dx

        @pl.when(pl.program_id(0) == 0)
        def _():
            dg_ref[...] = jnp.zeros_like(dg_ref)

        dg_ref[...] += jnp.sum(dhv * xh, axis=0, keepdims=True)

    row = pl.BlockSpec((tm, d), lambda i: (i, 0))
    vec = pl.BlockSpec((1, d), lambda i: (0, 0))
    return _pcall(
        body, name=name, out_shape=(_sds((t, d), F32), _sds((1, d), F32)), grid=(t // tm,),
        in_specs=[row, row, vec] + ([row] if has_res else []), out_specs=(row, vec),
        compiler_params=_cp(dimension_semantics=("arbitrary",)),
    )(*([dh, x, g] + ([res] if has_res else [])))


def _conv_fwd(z, conv_w, conv_b, *, nb, s, d, tc=256):
    nc = d // tc

    def body(cb_ref, cc_ref, cv_ref, w_ref, b_ref, y_ref):
        u = cc_ref[...].astype(F32) * cv_ref[...].astype(F32)
        row = lax.broadcasted_iota(jnp.int32, u.shape, 0)
        u1 = jnp.where(row >= 1, pltpu.roll(u, 1, axis=0), 0.0)
        u2 = jnp.where(row >= 2, pltpu.roll(u, 2, axis=0), 0.0)
        w = w_ref[...]
        y = w[2:3, :] * u + w[1:2, :] * u1 + w[0:1, :] * u2 + b_ref[...]
        y_ref[...] = (cb_ref[...].astype(F32) * y).astype(BF16)

    def part(p):
        return pl.BlockSpec((s, tc), lambda b, j: (b, p * nc + j))

    return _pcall(
        body, name="conv_fwd", out_shape=_sds((nb * s, d), BF16), grid=(nb, nc),
        in_specs=[part(0), part(1), part(2), pl.BlockSpec((3, tc), lambda b, j: (0, j)), pl.BlockSpec((1, tc), lambda b, j: (0, j))],
        out_specs=pl.BlockSpec((s, tc), lambda b, j: (b, j)), compiler_params=_cp(),
    )(z, z, z, conv_w, conv_b)


def _conv_bwd(z, dy, conv_w, conv_b, dz, *, nb, s, d, tc=256, comm=None):
    nc = d // tc
    c_ins, c_in_specs, c_out_shape, c_scratch = _comm_lists(comm)
    n_steps = nc * nb * 3

    def body(*refs):
        _, ins, cin, outs, cout, scr, csem = _split_refs(refs, 0, 7, len(c_ins), 3, len(c_ins), 1)
        cb_ref, cc_ref, cv_ref, dy_ref, w_ref, b_ref, _ = ins
        dz_ref, dw_ref, db_ref = outs
        stash = scr[0]
        b_i, p = pl.program_id(1), pl.program_id(2)
        step = (pl.program_id(0) * nb + b_i) * 3 + p
        _run_comm(comm, cin, cout, csem, step, n_steps, "start")

        @pl.when(p == 0)
        def _():
            cc, cv = cc_ref[...].astype(F32), cv_ref[...].astype(F32)
            u = cc * cv
            n = u.shape[0]
            row = lax.broadcasted_iota(jnp.int32, u.shape, 0)
            u1 = jnp.where(row >= 1, pltpu.roll(u, 1, axis=0), 0.0)
            u2 = jnp.where(row >= 2, pltpu.roll(u, 2, axis=0), 0.0)
            w = w_ref[...]
            y = w[2:3, :] * u + w[1:2, :] * u1 + w[0:1, :] * u2 + b_ref[...]
            dyv = dy_ref[...]
            dconv = dyv * cb_ref[...].astype(F32)
            g1 = jnp.where(row < n - 1, pltpu.roll(dconv, n - 1, axis=0), 0.0)
            g2 = jnp.where(row < n - 2, pltpu.roll(dconv, n - 2, axis=0), 0.0)
            du = w[2:3, :] * dconv + w[1:2, :] * g1 + w[0:1, :] * g2
            stash[0] = (dyv * y).astype(BF16)
            stash[1] = (du * cv).astype(BF16)
            stash[2] = (du * cc).astype(BF16)
            dws = [jnp.sum(dconv * u2, axis=0, keepdims=True), jnp.sum(dconv * u1, axis=0, keepdims=True),
                   jnp.sum(dconv * u, axis=0, keepdims=True)]
            db = jnp.sum(dconv, axis=0, keepdims=True)

            @pl.when(b_i == 0)
            def _():
                for r in range(3):
                    dw_ref[r:r + 1, :] = dws[r]
                db_ref[...] = db

            @pl.when(b_i > 0)
            def _():
                for r in range(3):
                    dw_ref[r:r + 1, :] += dws[r]
                db_ref[...] += db

        dz_ref[...] = stash[p]
        _run_comm(comm, cin, cout, csem, step, n_steps, "end")

    outs = _pcall(
        body, name="conv_bwd",
        out_shape=tuple([_sds(dz.shape, dz.dtype), _sds((3, d), F32), _sds((1, d), F32)] + c_out_shape), grid=(nc, nb, 3),
        in_specs=[
            pl.BlockSpec((s, tc), lambda j, b, p: (b, j)), pl.BlockSpec((s, tc), lambda j, b, p: (b, nc + j)),
            pl.BlockSpec((s, tc), lambda j, b, p: (b, 2 * nc + j)), pl.BlockSpec((s, tc), lambda j, b, p: (b, j)),
            pl.BlockSpec((3, tc), lambda j, b, p: (0, j)), pl.BlockSpec((1, tc), lambda j, b, p: (0, j)),
            HBM_SPEC,
        ] + c_in_specs,
        out_specs=tuple([pl.BlockSpec((s, tc), lambda j, b, p: (b, p * nc + j)),
                         pl.BlockSpec((3, tc), lambda j, b, p: (0, j)), pl.BlockSpec((1, tc), lambda j, b, p: (0, j))]
                        + [HBM_SPEC] * len(c_ins)),
        scratch_shapes=[pltpu.VMEM((3, s, tc), BF16)] + c_scratch,
        input_output_aliases={6: 0},
        compiler_params=_cp(dimension_semantics=("arbitrary", "arbitrary", "arbitrary")),
    )(z, z, z, dy, conv_w, conv_b, dz, *c_ins)
    return outs[0], outs[1], outs[2], list(outs[3:])


def _head_rms(xv, g):
    r = lax.rsqrt(jnp.mean(xv * xv, axis=-1, keepdims=True) + EPS)
    return xv * r * g


def _head_rms_bwd(xv, g, dy):
    r = lax.rsqrt(jnp.mean(xv * xv, axis=-1, keepdims=True) + EPS)
    xh = xv * r
    gd = dy * g
    dx = r * (gd - xh * jnp.mean(gd * xh, axis=-1, keepdims=True))
    return dx, jnp.sum(dy * xh, axis=0, keepdims=True)


def _fox_prep(z, q_g, k_g, *, t, d, tm=512):
    nh = d // HD

    def body(z_ref, qg_ref, kg_ref, q_ref, k_ref, v_ref):
        qg, kg = qg_ref[...], kg_ref[...]
        for h in range(nh):
            c0 = h * HD
            q_ref[:, c0:c0 + HD] = _head_rms(z_ref[:, c0:c0 + HD].astype(F32), qg).astype(BF16)
            k_ref[:, c0:c0 + HD] = _head_rms(z_ref[:, d + c0:d + c0 + HD].astype(F32), kg).astype(BF16)
        v_ref[...] = z_ref[:, 2 * d:3 * d]

    o = pl.BlockSpec((tm, d), lambda i: (i, 0))
    g = pl.BlockSpec((1, HD), lambda i: (0, 0))
    return _pcall(
        body, name="fox_prep", out_shape=(_sds((t, d), BF16),) * 3, grid=(t // tm,),
        in_specs=[pl.BlockSpec((tm, 3 * d), lambda i: (i, 1)), g, g], out_specs=(o, o, o), compiler_params=_cp(),
    )(z, q_g, k_g)


def _lane_cumsum(v, reverse=False):
    n = v.shape[1]
    lane = lax.broadcasted_iota(jnp.int32, v.shape, 1)
    sh = 1
    while sh < n:
        if reverse:
            v = v + jnp.where(lane < n - sh, pltpu.roll(v, n - sh, axis=1), 0.0)
        else:
            v = v + jnp.where(lane >= sh, pltpu.roll(v, sh, axis=1), 0.0)
        sh *= 2
    return v


def _fox_gates(z, f_bias_col, *, nb, s, nh):
    def body(z_ref, b_ref, c_ref):
        ff = z_ref[...].T[0:nh, :] + b_ref[...]
        lf = jnp.minimum(ff, 0.0) - jnp.log(1.0 + jnp.exp(-jnp.abs(ff)))
        c_ref[0] = _lane_cumsum(lf) * SQRT_HD

    return _pcall(
        body, name="fox_gates", out_shape=_sds((nb, nh, s), F32), grid=(nb,),
        in_specs=[pl.BlockSpec((s, HD), lambda b: (b, 0)), pl.BlockSpec((nh, 1), lambda b: (0, 0))],
        out_specs=pl.BlockSpec((1, nh, s), lambda b: (b, 0, 0)), compiler_params=_cp(),
    )(z, f_bias_col)


def _fox_gates_bwd(z, f_bias_col, row_sums, col_sums, dz, *, nb, s, nh, ff_block):
    def body(z_ref, b_ref, rs_ref, cs_ref, dz_in, dz_ref, db_ref, dc_sc):
        b_i, h = pl.program_id(0), pl.program_id(1)
        dc_sc[pl.ds(h, 1), :] = (rs_ref[...] - cs_ref[...]).T[0:1, :]

        @pl.when(h == nh - 1)
        def _():
            ff = z_ref[...].T[0:nh, :] + b_ref[...]
            dlf = _lane_cumsum(dc_sc[...], reverse=True)
            dff = dlf * (1.0 / (1.0 + jnp.exp(ff)))
            pad = jnp.concatenate([dff, jnp.zeros((HD - nh, s), F32)], axis=0)
            dz_ref[...] = pad.T.astype(BF16)
            dbv = jnp.sum(dff, axis=1, keepdims=True)

            @pl.when(b_i == 0)
            def _():
                db_ref[...] = dbv

            @pl.when(b_i > 0)
            def _():
                db_ref[...] += dbv

    hs = pl.BlockSpec((s, HD), lambda b, h: (b, h))
    return _pcall(
        body, name="fox_gates_bwd", out_shape=(_sds(dz.shape, dz.dtype), _sds((nh, 1), F32)), grid=(nb, nh),
        in_specs=[pl.BlockSpec((s, HD), lambda b, h: (b, 0)), pl.BlockSpec((nh, 1), lambda b, h: (0, 0)), hs, hs, HBM_SPEC],
        out_specs=(pl.BlockSpec((s, HD), lambda b, h: (b, ff_block)), pl.BlockSpec((nh, 1), lambda b, h: (0, 0))),
        scratch_shapes=[pltpu.VMEM((nh, s), F32)],
        input_output_aliases={4: 0}, compiler_params=_cp(dimension_semantics=("arbitrary", "arbitrary")),
    )(z, f_bias_col, row_sums, col_sums, dz)


def _fox_pairs(nq, key_major):
    if key_major:
        pairs = [(i, j) for j in range(nq) for i in range(j, nq)]
    else:
        pairs = [(i, j) for i in range(nq) for j in range(i + 1)]
    return jnp.array([p[0] for p in pairs], jnp.int32), jnp.array([p[1] for p in pairs], jnp.int32)


def _with_ones(x):
    return jnp.concatenate([x, jnp.ones_like(x)], axis=1)


def _fox_specs(nb, nh, nq, tq, tk, hp):
    qs = pl.BlockSpec((tq, hp * HD), lambda b, h, p, *tb: (b * nq + tb[0][p], h))
    ks = pl.BlockSpec((tk, hp * HD), lambda b, h, p, *tb: (b * nq + tb[1][p], h))
    cs = pl.BlockSpec((hp, 1, tk), lambda b, h, p, *tb: (b * (nh // hp) + h, 0, tb[1][p]))
    return qs, ks, cs


def _fox_raw(qv, kv, ckv, diag, tq, tk):
    sc = lax.dot_general(qv, kv, (((1,), (1,)), ((), ())), preferred_element_type=F32) - ckv
    if diag:
        rows = lax.broadcasted_iota(jnp.int32, (tq, tk), 0)
        cols = lax.broadcasted_iota(jnp.int32, (tq, tk), 1)
        sc = jnp.where(rows >= cols, sc, NEG)
    return sc


def _fox_probs(qv, kv, ckv, lse_col, diag, tq, tk):
    sc = _fox_raw(qv, kv, ckv, diag, tq, tk)
    return jnp.exp2(sc * EXP2_C - lse_col * LOG2E)


def _fox_call(body, *, name, ins, in_specs, out_shape, out_specs, scratch, grid, tables, comm):
    c_ins, c_in_specs, c_out_shape, c_scratch = _comm_lists(comm)
    n_in, n_out, n_scr, nc = len(ins), len(out_shape), len(scratch), len(c_ins)
    n_steps = grid[0] * grid[1] * grid[2]

    def wrapped(*refs):
        pre, rin, cin, rout, cout, scr, csem = _split_refs(refs, len(tables), n_in, nc, n_out, nc, n_scr)
        step = (pl.program_id(0) * grid[1] + pl.program_id(1)) * grid[2] + pl.program_id(2)
        _run_comm(comm, cin, cout, csem, step, n_steps, "start")
        body(*pre, *rin, *rout, *scr)
        _run_comm(comm, cin, cout, csem, step, n_steps, "end")

    sem = ("arbitrary",) * 3 if comm is not None else ("parallel", "parallel", "arbitrary")
    outs = _pcall(
        wrapped, name=name, out_shape=tuple(list(out_shape) + c_out_shape),
        grid_spec=pltpu.PrefetchScalarGridSpec(
            num_scalar_prefetch=len(tables), grid=grid, in_specs=list(in_specs) + c_in_specs,
            out_specs=tuple(list(out_specs) + [HBM_SPEC] * nc), scratch_shapes=list(scratch) + c_scratch),
        compiler_params=_cp(dimension_semantics=sem),
    )(*tables, *ins, *c_ins)
    return list(outs[:n_out]), list(outs[n_out:])


def _fox_fwd(q, k, v, cks, *, nb, s, nh, tq=512, comm=None):
    tk = tq
    nq = s // tq
    t = nb * s
    hp = FOX_HP
    it, jt = _fox_pairs(nq, key_major=False)

    def body(it_ref, jt_ref, q_ref, k_ref, v_ref, c_ref, o_ref, lse_ref, m_sc, acc_sc):
        p_id = pl.program_id(2)
        i, j = it_ref[p_id], jt_ref[p_id]

        @pl.when(j == 0)
        def _():
            m_sc[...] = jnp.full_like(m_sc, NEG)
            acc_sc[...] = jnp.zeros_like(acc_sc)

        def step(diag):
            for hh in range(hp):
                cols = slice(hh * HD, (hh + 1) * HD)
                sc = _fox_raw(q_ref[:, cols], k_ref[:, cols], c_ref[hh], diag, tq, tk)
                m_old = m_sc[hh]
                m_new = jnp.maximum(m_old, jnp.max(sc, axis=-1, keepdims=True))
                alpha = jnp.exp2((m_old - m_new) * EXP2_C)
                p = jnp.exp2((sc - m_new) * EXP2_C).astype(BF16)
                acc = alpha * acc_sc[hh] + jnp.dot(p, _with_ones(v_ref[:, cols]), preferred_element_type=F32)
                if diag:
                    l = acc[:, HD:2 * HD]
                    o_ref[:, cols] = (acc[:, 0:HD] / l).astype(BF16)
                    lse_ref[:, cols] = m_new * (1.0 / SQRT_HD) + jnp.log(l)
                else:
                    acc_sc[hh] = acc
                    m_sc[hh] = m_new

        @pl.when(j < i)
        def _():
            step(False)

        @pl.when(j == i)
        def _():
            step(True)

    qs, ks, cs = _fox_specs(nb, nh, nq, tq, tk, hp)
    (o, lse), c_out = _fox_call(
        body, name="fox_fwd", ins=[q, k, v, cks], in_specs=[qs, ks, ks, cs],
        out_shape=[_sds((t, nh * HD), BF16), _sds((t, nh * HD), F32)], out_specs=[qs, qs],
        scratch=[pltpu.VMEM((hp, tq, 1), F32), pltpu.VMEM((hp, tq, 2 * HD), F32)],
        grid=(nb, nh // hp, int(it.shape[0])), tables=(it, jt), comm=comm)
    return o, lse, c_out


def _fox_bwd(q, k, v, cks, do, o, lse, *, nb, s, nh, tq=512, comm=None):
    tk = tq
    nq = s // tq
    t = nb * s
    hp = FOX_HP
    it, jt = _fox_pairs(nq, key_major=True)
    done = jnp.array([j for j in range(nq) for _ in range(j, nq)], jnp.int32)

    def body(it_ref, jt_ref, done_ref, q_ref, k_ref, v_ref, c_ref, do_ref, o_ref, lse_ref,
             dq_ref, rs_ref, dk_ref, dv_ref, cs_ref, dq_sc, dk_sc, dv_sc):
        p_id = pl.program_id(2)
        i, j = it_ref[p_id], jt_ref[p_id]
        rows_i = pl.ds(pl.multiple_of(i * tq, tq), tq)

        def step(diag):
            for hh in range(hp):
                cols = slice(hh * HD, (hh + 1) * HD)
                qv, kv = q_ref[:, cols], k_ref[:, cols]
                p = _fox_probs(qv, kv, c_ref[hh], lse_ref[:, hh * HD:hh * HD + 1], diag, tq, tk)
                dov = do_ref[:, cols]
                dob = dov.astype(BF16)
                dvp = lax.dot_general(p.astype(BF16), dob, (((0,), (0,)), ((), ())), preferred_element_type=F32)
                dp = lax.dot_general(dob, v_ref[:, cols], (((1,), (1,)), ((), ())), preferred_element_type=F32)
                delta = jnp.sum(dov * o_ref[:, cols].astype(F32), axis=-1, keepdims=True)
                ds = (p * (dp - delta)).astype(BF16)
                dkp = lax.dot_general(ds, _with_ones(qv), (((0,), (0,)), ((), ())), preferred_element_type=F32)
                dqp = jnp.dot(ds, _with_ones(kv), preferred_element_type=F32)
                if diag:
                    dk_sc[hh] = dkp
                    dv_sc[hh] = dvp
                else:
                    dk_sc[hh] += dkp
                    dv_sc[hh] += dvp

                @pl.when(j == 0)
                def _():
                    dq_sc[hh, rows_i, :] = dqp

                @pl.when(j > 0)
                def _():
                    dq_sc[hh, rows_i, :] += dqp

                if diag:
                    acc = dq_sc[hh, rows_i, :]
                    dq_ref[:, cols] = acc[:, 0:HD] * (1.0 / SQRT_HD)
                    rs_ref[:, cols] = acc[:, HD:2 * HD]

        @pl.when(i == j)
        def _():
            step(True)

        @pl.when(i > j)
        def _():
            step(False)

        @pl.when(i == nq - 1)
        def _():
            for hh in range(hp):
                cols = slice(hh * HD, (hh + 1) * HD)
                dk_ref[:, cols] = dk_sc[hh, :, 0:HD] * (1.0 / SQRT_HD)
                cs_ref[:, cols] = dk_sc[hh, :, HD:2 * HD]
                dv_ref[:, cols] = dv_sc[hh].astype(BF16)

    qs, ks, cs = _fox_specs(nb, nh, nq, tq, tk, hp)
    ds_spec = pl.BlockSpec((tq, hp * HD), lambda b, h, p, *tb: (b * nq + tb[2][p], h))
    (dq, rs, dk, dv, csum), c_out = _fox_call(
        body, name="fox_bwd", ins=[q, k, v, cks, do, o, lse], in_specs=[qs, ks, ks, cs, qs, qs, qs],
        out_shape=[_sds((t, nh * HD), F32), _sds((t, nh * HD), F32), _sds((t, nh * HD), F32), _sds((t, nh * HD), BF16),
                   _sds((t, nh * HD), F32)],
        out_specs=[ds_spec, ds_spec, ks, ks, ks],
        scratch=[pltpu.VMEM((hp, s, 2 * HD), F32), pltpu.VMEM((hp, tk, 2 * HD), F32), pltpu.VMEM((hp, tk, HD), F32)],
        grid=(nb, nh // hp, int(it.shape[0])), tables=(it, jt, done), comm=comm)
    return dq, rs, dk, dv, csum, c_out


def _fox_post(z, dqn, dkn, dv, q_g, k_g, dz, *, t, d, tm=512):
    nh = d // HD

    def body(z_ref, dq_ref, dk_ref, dv_ref, qg_ref, kg_ref, dz_in, dz_ref, dqg_ref, dkg_ref):
        qg, kg = qg_ref[...], kg_ref[...]
        aq = jnp.zeros((1, HD), F32)
        ak = jnp.zeros((1, HD), F32)
        for h in range(nh):
            c0 = h * HD
            dx, sg = _head_rms_bwd(z_ref[:, c0:c0 + HD].astype(F32), qg, dq_ref[:, c0:c0 + HD])
            dz_ref[:, c0:c0 + HD] = dx.astype(BF16)
            aq = aq + sg
            dx, sg = _head_rms_bwd(z_ref[:, d + c0:d + c0 + HD].astype(F32), kg, dk_ref[:, c0:c0 + HD])
            dz_ref[:, d + c0:d + c0 + HD] = dx.astype(BF16)
            ak = ak + sg
        dz_ref[:, 2 * d:3 * d] = dv_ref[...]

        @pl.when(pl.program_id(0) == 0)
        def _():
            dqg_ref[...] = aq
            dkg_ref[...] = ak

        @pl.when(pl.program_id(0) > 0)
        def _():
            dqg_ref[...] += aq
            dkg_ref[...] += ak

    o = pl.BlockSpec((tm, d), lambda i: (i, 0))
    g = pl.BlockSpec((1, HD), lambda i: (0, 0))
    z3 = pl.BlockSpec((tm, 3 * d), lambda i: (i, 1))
    return _pcall(
        body, name="fox_post", out_shape=(_sds(dz.shape, dz.dtype), _sds((1, HD), F32), _sds((1, HD), F32)), grid=(t // tm,),
        in_specs=[z3, o, o, o, g, g, HBM_SPEC], out_specs=(z3, g, g),
        input_output_aliases={6: 0}, compiler_params=_cp(dimension_semantics=("arbitrary",)),
    )(z, dqn, dkn, dv, q_g, k_g, dz)


def _xa_prep_kv(kv, k_g, *, rows, d):
    xd = d // XH

    def body(kv_ref, g_ref, k_ref, v_ref):
        g = g_ref[...]
        for h in range(XH):
            c0 = h * xd
            k_ref[:, c0:c0 + xd] = _head_rms(kv_ref[:, c0:c0 + xd], g).astype(BF16)
        v_ref[...] = kv_ref[:, d:2 * d].astype(BF16)

    return _pcall(body, name="xa_prep_kv", out_shape=(_sds((rows, d), BF16),) * 2, compiler_params=_cp())(kv, k_g)


def _xa_fwd(z, kn, vb, q_g, *, nb, s, d, nm, q_block, tq=1024):
    xd = d // XH
    tq = min(tq, s)
    nq = s // tq

    def body(z_ref, k_ref, v_ref, g_ref, y_ref):
        g = g_ref[...]
        for h in range(XH):
            c0 = h * xd
            qn = _head_rms(z_ref[:, c0:c0 + xd].astype(F32), g).astype(BF16)
            sc = lax.dot_general(qn, k_ref[:, c0:c0 + xd], (((1,), (1,)), ((), ())), preferred_element_type=F32)
            sc = sc / math.sqrt(xd)
            p = jnp.exp(sc - jnp.max(sc, axis=-1, keepdims=True))
            p = p / jnp.sum(p, axis=-1, keepdims=True)
            y_ref[:, c0:c0 + xd] = jnp.dot(p.astype(BF16), v_ref[:, c0:c0 + xd], preferred_element_type=F32).astype(BF16)

    kvs = pl.BlockSpec((nm, d), lambda b, i: (b, 0))
    return _pcall(
        body, name="xa_fwd", out_shape=_sds((nb * s, d), BF16), grid=(nb, nq),
        in_specs=[pl.BlockSpec((tq, d), lambda b, i: (b * nq + i, q_block)), kvs, kvs, pl.BlockSpec((1, xd), lambda b, i: (0, 0))],
        out_specs=pl.BlockSpec((tq, d), lambda b, i: (b * nq + i, 0)), compiler_params=_cp(),
    )(z, kn, vb, q_g)


def _xa_bwd(z, kn, vb, q_g, dy, dz, *, nb, s, d, nm, q_block, tq=1024):
    xd = d // XH
    tq = min(tq, s)
    nq = s // tq

    def body(z_ref, k_ref, v_ref, g_ref, dy_ref, dz_in, dz_ref, dk_ref, dv_ref, dg_ref):
        b_i, i = pl.program_id(0), pl.program_id(1)
        g = g_ref[...]

        @pl.when(i == 0)
        def _():
            dk_ref[...] = jnp.zeros_like(dk_ref)
            dv_ref[...] = jnp.zeros_like(dv_ref)

        @pl.when((i == 0) & (b_i == 0))
        def _():
            dg_ref[...] = jnp.zeros_like(dg_ref)

        for h in range(XH):
            c0 = h * xd
            xq = z_ref[:, c0:c0 + xd].astype(F32)
            qn = _head_rms(xq, g).astype(BF16)
            kh, vh = k_ref[:, c0:c0 + xd], v_ref[:, c0:c0 + xd]
            sc = lax.dot_general(qn, kh, (((1,), (1,)), ((), ())), preferred_element_type=F32) / math.sqrt(xd)
            p = jnp.exp(sc - jnp.max(sc, axis=-1, keepdims=True))
            p = p / jnp.sum(p, axis=-1, keepdims=True)
            dob = dy_ref[:, c0:c0 + xd].astype(BF16)
            dv_ref[:, c0:c0 + xd] += lax.dot_general(p.astype(BF16), dob, (((0,), (0,)), ((), ())), preferred_element_type=F32)
            dp = lax.dot_general(dob, vh, (((1,), (1,)), ((), ())), preferred_element_type=F32)
            ds = p * (dp - jnp.sum(p * dp, axis=-1, keepdims=True)) / math.sqrt(xd)
            dsb = ds.astype(BF16)
            dqn = jnp.dot(dsb, kh, preferred_element_type=F32)
            dk_ref[:, c0:c0 + xd] += lax.dot_general(dsb, qn, (((0,), (0,)), ((), ())), preferred_element_type=F32)
            dx, sg = _head_rms_bwd(xq, g, dqn)
            dz_ref[:, c0:c0 + xd] = dx.astype(BF16)
            dg_ref[...] += sg

    kvs = pl.BlockSpec((nm, d), lambda b, i: (b, 0))
    qs = pl.BlockSpec((tq, d), lambda b, i: (b * nq + i, q_block))
    gs = pl.BlockSpec((1, xd), lambda b, i: (0, 0))
    return _pcall(
        body, name="xa_bwd",
        out_shape=(_sds(dz.shape, dz.dtype), _sds((nb * nm, d), F32), _sds((nb * nm, d), F32), _sds((1, xd), F32)), grid=(nb, nq),
        in_specs=[qs, kvs, kvs, gs, pl.BlockSpec((tq, d), lambda b, i: (b * nq + i, 0)), HBM_SPEC],
        out_specs=(qs, kvs, kvs, gs), input_output_aliases={5: 0},
        compiler_params=_cp(dimension_semantics=("arbitrary", "arbitrary")),
    )(z, kn, vb, q_g, dy, dz)


def _xa_post_kv(kv, dkn, dv, k_g, *, rows, d):
    xd = d // XH

    def body(kv_ref, dk_ref, dv_ref, g_ref, dkv_ref, dg_ref):
        g = g_ref[...]
        acc = jnp.zeros((1, xd), F32)
        for h in range(XH):
            c0 = h * xd
            dx, sg = _head_rms_bwd(kv_ref[:, c0:c0 + xd], g, dk_ref[:, c0:c0 + xd])
            dkv_ref[:, c0:c0 + xd] = dx.astype(BF16)
            acc = acc + sg
        dkv_ref[:, d:2 * d] = dv_ref[...].astype(BF16)
        dg_ref[...] = acc

    return _pcall(body, name="xa_post_kv", out_shape=(_sds((rows, 2 * d), BF16), _sds((1, xd), F32)), compiler_params=_cp())(
        kv, dkn, dv, k_g)


def _sigmoid(v):
    return 1.0 / (1.0 + jnp.exp(-v))


def _branches_fwd(z, ys, ws, *, t, d, gate_block, tm=1024, tn=512, comm=None):
    nj = d // tn
    c_ins, c_in_specs, c_out_shape, c_scratch = _comm_lists(comm)
    n_steps = (t // tm) * nj

    def body(*refs):
        _, ins, cin, outs, cout, _, csem = _split_refs(refs, 0, 9, len(c_ins), 2, len(c_ins), 0)
        ya_ref, yb_ref, yc_ref, wa_ref, wb_ref, wc_ref, ga_ref, gb_ref, gc_ref = ins
        m_ref, p_ref = outs
        step = pl.program_id(0) * nj + pl.program_id(1)
        _run_comm(comm, cin, cout, csem, step, n_steps, "start")
        acc = None
        for q, (y_ref, w_ref, g_ref) in enumerate(((ya_ref, wa_ref, ga_ref), (yb_ref, wb_ref, gb_ref), (yc_ref, wc_ref, gc_ref))):
            pr = jnp.dot(y_ref[...], w_ref[...], preferred_element_type=F32)
            p_ref[q] = pr.astype(BF16)
            term = _sigmoid(g_ref[...].astype(F32)) * pr
            acc = term if acc is None else acc + term
        m_ref[...] = acc.astype(BF16)
        _run_comm(comm, cin, cout, csem, step, n_steps, "end")

    y_spec = pl.BlockSpec((tm, d), lambda i, j: (i, 0))
    w_spec = pl.BlockSpec((d, tn), lambda i, j: (0, j))

    def gate(q):
        return pl.BlockSpec((tm, tn), lambda i, j: (i, (gate_block + q) * nj + j))

    outs = _pcall(
        body, name="branches_fwd", out_shape=tuple([_sds((t, d), BF16), _sds((3, t, d), BF16)] + c_out_shape), grid=(t // tm, nj),
        in_specs=[y_spec] * 3 + [w_spec] * 3 + [gate(0), gate(1), gate(2)] + c_in_specs,
        out_specs=tuple([pl.BlockSpec((tm, tn), lambda i, j: (i, j)), pl.BlockSpec((3, tm, tn), lambda i, j: (0, i, j))]
                        + [HBM_SPEC] * len(c_ins)),
        scratch_shapes=c_scratch, compiler_params=_cp(dimension_semantics=("arbitrary", "arbitrary")),
    )(*ys, *ws, z, z, z, *c_ins)
    return outs[0], outs[1], list(outs[2:])


def _gates_bwd(z, dm, proj, dz, *, t, d, gate_block, tm=1024):
    def body(g_ref, dm_ref, p_ref, dz_in, dz_ref, da_ref, db_ref, dc_ref):
        q = pl.program_id(1)
        sg = _sigmoid(g_ref[...].astype(F32))
        dmv = dm_ref[...]
        dz_ref[...] = (dmv * p_ref[0].astype(F32) * sg * (1.0 - sg)).astype(BF16)
        dp = (dmv * sg).astype(BF16)
        @pl.when(q == 0)
        def _():
            da_ref[...] = dp

        @pl.when(q == 1)
        def _():
            db_ref[...] = dp

        @pl.when(q == 2)
        def _():
            dc_ref[...] = dp

    gs = pl.BlockSpec((tm, d), lambda i, q: (i, gate_block + q))
    o = pl.BlockSpec((tm, d), lambda i, q: (i, 0))
    return _pcall(
        body, name="gates_bwd", out_shape=(_sds(dz.shape, dz.dtype),) + (_sds((t, d), BF16),) * 3, grid=(t // tm, 3),
        in_specs=[gs, o, pl.BlockSpec((1, tm, d), lambda i, q: (q, i, 0)), HBM_SPEC], out_specs=(gs, o, o, o),
        input_output_aliases={3: 0}, compiler_params=_cp(dimension_semantics=("arbitrary", "arbitrary")),
    )(z, dm, proj, dz)


def _ffn_in_fwd(h2, wfi_t, *, t, d, dff, tm=512, tn=1408):
    nj = dff // tn

    def body(h_ref, wg_ref, wu_ref, g_ref, u_ref, a_ref):
        hv = h_ref[...]
        dn = (((1,), (1,)), ((), ()))
        gv = lax.dot_general(hv, wg_ref[...], dn, preferred_element_type=F32)
        uv = lax.dot_general(hv, wu_ref[...], dn, preferred_element_type=F32)
        g_ref[...] = gv.astype(BF16)
        u_ref[...] = uv.astype(BF16)
        a_ref[...] = (gv * _sigmoid(gv) * uv).astype(BF16)

    o = pl.BlockSpec((tm, tn), lambda i, j: (i, j))
    return _pcall(
        body, name="ffn_in_fwd", out_shape=(_sds((t, dff), BF16),) * 3, grid=(t // tm, nj),
        in_specs=[pl.BlockSpec((tm, d), lambda i, j: (i, 0)), pl.BlockSpec((tn, d), lambda i, j: (j, 0)),
                  pl.BlockSpec((tn, d), lambda i, j: (nj + j, 0))],
        out_specs=(o, o, o), compiler_params=_cp(),
    )(h2, wfi_t, wfi_t)


def _ffn_out_bwd(dy, wfo, gate, up, *, t, d, dff, tm=512, tn=1408):
    def body(dy_ref, w_ref, g_ref, u_ref, dg_ref, du_ref):
        dav = lax.dot_general(dy_ref[...].astype(BF16), w_ref[...], (((1,), (1,)), ((), ())), preferred_element_type=F32)
        gv = g_ref[...].astype(F32)
        sg = _sigmoid(gv)
        dg_ref[...] = (dav * u_ref[...].astype(F32) * sg * (1.0 + gv * (1.0 - sg))).astype(BF16)
        du_ref[...] = (dav * gv * sg).astype(BF16)

    o = pl.BlockSpec((tm, tn), lambda i, j: (i, j))
    return _pcall(
        body, name="ffn_out_bwd", out_shape=(_sds((t, dff), BF16),) * 2, grid=(t // tm, dff // tn),
        in_specs=[pl.BlockSpec((tm, d), lambda i, j: (i, 0)), pl.BlockSpec((tn, d), lambda i, j: (j, 0)), o, o],
        out_specs=(o, o), compiler_params=_cp(),
    )(dy, wfo, gate, up)


def _ffn_out_loss(act, wfo, x1, target, *, t, d, dff, tm=512):
    def body(a_ref, w_ref, x_ref, t_ref, dy_ref, l_ref):
        yv = x_ref[...] + jnp.dot(a_ref[...], w_ref[...], preferred_element_type=F32)
        e = yv - t_ref[...]
        dy_ref[...] = e * (1.0 / d)

        @pl.when(pl.program_id(0) == 0)
        def _():
            l_ref[...] = jnp.zeros_like(l_ref)

        l_ref[...] += jnp.sum(jnp.sum(e * e, axis=-1, keepdims=True), axis=0, keepdims=True) * (0.5 / d)

    o = pl.BlockSpec((tm, d), lambda i: (i, 0))
    dy, l = _pcall(
        body, name="ffn_out_loss", out_shape=(_sds((t, d), F32), _sds((1, HD), F32)), grid=(t // tm,),
        in_specs=[pl.BlockSpec((tm, dff), lambda i: (i, 0)), pl.BlockSpec((dff, d), lambda i: (0, 0)), o, o],
        out_specs=(o, pl.BlockSpec((1, HD), lambda i: (0, 0))),
        compiler_params=_cp(dimension_semantics=("arbitrary",)),
    )(act, wfo, x1, target)
    return dy, l[0, 0]


def _adamw(w, g, m, v, *, name):
    if w.ndim == 3:
        rows, _, cols = w.shape
        tm = max(tt for tt in range(1, 65) if rows % tt == 0)
        block, imap, grid = (tm, 1, cols), (lambda i, j: (i, 0, 0)), (rows // tm, 1)
    else:
        rows, cols = w.shape
        tm, tn = rows, cols
        if rows > 512 and rows % 8 == 0:
            tm = max(tt for tt in range(8, 513, 8) if rows % tt == 0)
        elif rows > 512 and cols % 128 == 0:
            tn = 128
        block, imap, grid = (tm, tn), (lambda i, j: (i, j)), (rows // tm, cols // tn)
    c1 = 1.0 - ADAM_B1 ** ADAM_STEP
    c2 = 1.0 - ADAM_B2 ** ADAM_STEP

    at = (slice(None), 0, slice(None)) if w.ndim == 3 else Ellipsis

    def body(w_ref, g_ref, m_ref, v_ref, d_ref, nm_ref, nv_ref):
        gv = g_ref[at]
        mn = ADAM_B1 * m_ref[at] + (1.0 - ADAM_B1) * gv
        vn = ADAM_B2 * v_ref[at] + (1.0 - ADAM_B2) * (gv * gv)
        d_ref[at] = -ADAM_LR * ((mn / c1) / (jnp.sqrt(vn / c2) + ADAM_EPS) + ADAM_WD * w_ref[at])
        nm_ref[at] = mn
        nv_ref[at] = vn

    spec = pl.BlockSpec(block, imap)
    return _pcall(
        body, name=name, out_shape=(_sds(w.shape, F32),) * 3, grid=grid,
        in_specs=[spec] * 4, out_specs=(spec,) * 3, compiler_params=_cp(),
    )(w, g, m, v)


def _pad_rows(a, rows):
    return a if a.shape[0] == rows else jnp.pad(a, ((0, rows - a.shape[0]),) + ((0, 0),) * (a.ndim - 1))


class _Pack:
    def __init__(self, row_sizes):
        self.rows = list(row_sizes)
        self.padded = [-(-r // 16) * 16 for r in self.rows]
        self.offs = [sum(self.padded[:i]) for i in range(len(self.rows))]
        self.total = sum(self.padded)

    def stack(self, blocks):
        return jnp.concatenate([_pad_rows(b.astype(BF16), p) for b, p in zip(blocks, self.padded)], axis=0)

    def full(self, gathered, i):
        r = self.rows[i]
        return gathered[:, self.offs[i]:self.offs[i] + r, :].reshape(N_DEV * r, gathered.shape[2])

    def for_core(self, full_grads, core):
        parts = []
        for gfull, r, p in zip(full_grads, self.rows, self.padded):
            pieces = gfull if isinstance(gfull, (tuple, list)) else (gfull,)
            blk = jnp.concatenate(
                [lax.dynamic_index_in_dim(g.reshape(-1, 2, r, g.shape[1]), core, axis=1, keepdims=False) for g in pieces],
                axis=0).astype(BF16)
            if p != r:
                blk = jnp.pad(blk, ((0, 0), (0, p - r), (0, 0)))
            parts.append(blk)
        return jnp.concatenate(parts, axis=1)

    def shard(self, g_pack, i):
        return g_pack[self.offs[i]:self.offs[i] + self.rows[i], :]


def kernel(x, mem, norm1_g, w_in, conv_w, conv_b, fox_f_bias, fox_q_g, fox_k_g, mem_norm_g, w_mem_kv, xa_q_g, xa_k_g, w_br_conv, w_br_fox, w_br_xa, w_o, norm2_g, w_ffn_in, w_ffn_out, loss_target, m_norm1_g, m_w_in, m_conv_w, m_conv_b, m_fox_f_bias, m_fox_q_g, m_fox_k_g, m_mem_norm_g, m_w_mem_kv, m_xa_q_g, m_xa_k_g, m_w_br_conv, m_w_br_fox, m_w_br_xa, m_w_o, m_norm2_g, m_w_ffn_in, m_w_ffn_out, v_norm1_g, v_w_in, v_conv_w, v_conv_b, v_fox_f_bias, v_fox_q_g, v_fox_k_g, v_mem_norm_g, v_w_mem_kv, v_xa_q_g, v_xa_k_g, v_w_br_conv, v_w_br_fox, v_w_br_xa, v_w_o, v_norm2_g, v_w_ffn_in, v_w_ffn_out):
    nb, s, d = x.shape
    t = nb * s
    nm = mem.shape[1]
    nh = d // HD
    dff = w_ffn_out.shape[1] * N_DEV
    n_in = w_in.shape[2] * N_DEV
    n_in_pad = -(-n_in // 1152) * 1152
    ff_block = (10 * d) // HD
    my_c = lax.axis_index("c")
    my_q = 2 * lax.axis_index("x") + lax.axis_index("y")

    pack_a = _Pack([w_in.shape[2]])
    pack_b = _Pack([w_mem_kv.shape[2], w_br_conv.shape[1], w_br_fox.shape[1], w_br_xa.shape[1], w_o.shape[1],
                    w_ffn_in.shape[2], w_ffn_out.shape[1]])
    gather_2 = _Pack([w_mem_kv.shape[2], w_br_conv.shape[1], w_br_fox.shape[1], w_br_xa.shape[1], w_o.shape[1]])
    stack_a = pack_a.stack([w_in[0].T])
    stack_2 = gather_2.stack([w_mem_kv[0].T, w_br_conv[0], w_br_fox[0], w_br_xa[0], w_o[0]])
    stack_fi = w_ffn_in[0].T.astype(BF16)
    stack_fo = w_ffn_out[0].astype(BF16)
    cw_block = _pad_rows(conv_w[0], 8)
    rows_a = w_in.shape[2]
    my_x, my_y = lax.axis_index("x"), lax.axis_index("y")
    arrival = [(my_x, my_y, my_c), (my_x, my_y, 1 - my_c)]
    chip_1, chip_2, chip_d = _chip_order(my_x, my_y, my_c)
    arrival += [(*chip_1, my_c), (*chip_2, 1 - my_c), (*chip_2, my_c), (*chip_1, 1 - my_c), (*chip_d, my_c), (*chip_d, 1 - my_c)]
    order = jnp.stack([4 * ax + 2 * ay + ac for ax, ay, ac in arrival]).astype(jnp.int32)

    x2 = x.reshape(t, d)
    mem2 = mem.reshape(nb * nm, d)
    tgt2 = loss_target.reshape(t, d)
    h1 = _rms_fwd(x2, norm1_g, name="rms1_fwd")
    z, slabs, _ = _z_gathered(h1, stack_a, order, t=t, d=d, n_pad=N_DEV * SLAB_WIN, rows=rows_a)
    z = _z_fix(h1, slabs, z, t=t, d=d, rows=rows_a)
    ff_rows = _pad_rows(slabs[N_DEV - 1, rows_a - nh:rows_a, :], HD)
    zff = _mm(h1, ff_rows, m=t, n=HD, k=d, tb=True, tm=1024, name="mm_z_ff")
    qn, kn, vb = _fox_prep(z, fox_q_g, fox_k_g, t=t, d=d)
    f_bias_col = fox_f_bias.reshape(nh, 1)
    ck_rows = _fox_gates(zff, f_bias_col, nb=nb, s=s, nh=nh).reshape(nb * nh, 1, s)
    y_fox, lse, (gathered_2, cw_g, gathered_fo) = _fox_fwd(qn, kn, vb, ck_rows, nb=nb, s=s, nh=nh,
                                                           comm=_AllGather([stack_2, cw_block, stack_fo], forward_at=0.8))
    wkv_t, wbc, wbf, wbx, wo = (gather_2.full(gathered_2, i) for i in range(5))
    wfo = gathered_fo.reshape(dff, d)
    conv_w_full = jnp.transpose(cw_g[:, 0:3, :], (1, 0, 2)).reshape(3, d)

    y_conv = _conv_fwd(z, conv_w_full, conv_b, nb=nb, s=s, d=d)
    mem_n = _rms_fwd(mem2, mem_norm_g, name="rms_mem_fwd")
    kv = _mm(mem_n, wkv_t, m=nb * nm, n=2 * d, k=d, tb=True, name="mm_kv")
    xkn, xvb = _xa_prep_kv(kv, xa_k_g, rows=nb * nm, d=d)
    y_xa = _xa_fwd(z, xkn, xvb, xa_q_g, nb=nb, s=s, d=d, nm=nm, q_block=6)
    merged, proj, (gathered_fi,) = _branches_fwd(z, (y_conv, y_fox, y_xa), (wbc, wbf, wbx), t=t, d=d, gate_block=7,
                                                 comm=_AllGather([stack_fi]))
    wfi_t = gathered_fi.reshape(2 * dff, d)
    x1 = _mm(merged, wo, m=t, n=d, k=d, res=x2, tm=1024, name="mm_x1")
    h2 = _rms_fwd(x1, norm2_g, name="rms2_fwd")
    ff_gate, ff_up, act = _ffn_in_fwd(h2, wfi_t, t=t, d=d, dff=dff)
    dy, loss_local = _ffn_out_loss(act, wfo, x1, tgt2, t=t, d=d, dff=dff)

    g_wfo = _mm(act, dy, m=dff, n=d, k=t, ta=True, tm=1408, tk=2048, out_dtype=BF16, name="mm_g_wfo")
    dgate, dup = _ffn_out_bwd(dy, wfo, ff_gate, ff_up, t=t, d=d, dff=dff)
    dh2 = _mm(dgate, wfi_t, m=t, n=d, k=dff, tk=dff, name="mm_dh2_g")
    dh2 = _mm(dup, wfi_t, m=t, n=d, k=dff, tk=dff, b_off=(1, 0), res=dh2, name="mm_dh2_u")
    g_wfi_g = _mm(dgate, h2, m=dff, n=d, k=t, ta=True, tm=1408, tk=2048, out_dtype=BF16, name="mm_g_wfi_g")
    g_wfi_u = _mm(dup, h2, m=dff, n=d, k=t, ta=True, tm=1408, tk=2048, out_dtype=BF16, name="mm_g_wfi_u")
    dx1, g_norm2 = _rms_bwd(dh2, x1, norm2_g, dy, name="rms2_bwd")
    dmerged = _mm(dx1, wo, m=t, n=d, k=d, tb=True, tm=1024, name="mm_dmerged")
    g_wo = _mm(merged, dx1, m=d, n=d, k=t, ta=True, tk=2048, out_dtype=BF16, name="mm_g_wo")

    dz = lax.empty((t, n_in_pad), BF16)
    dz, dpa, dpb, dpc = _gates_bwd(z, dmerged, proj, dz, t=t, d=d, gate_block=7)
    dy_conv = _mm(dpa, wbc, m=t, n=d, k=d, tb=True, tm=1024, name="mm_dy_conv")
    g_wbc = _mm(y_conv, dpa, m=d, n=d, k=t, ta=True, tk=4096, out_dtype=BF16, name="mm_g_wbc")
    dy_fox = _mm(dpb, wbf, m=t, n=d, k=d, tb=True, tm=1024, name="mm_dy_fox")
    g_wbf = _mm(y_fox, dpb, m=d, n=d, k=t, ta=True, tk=4096, out_dtype=BF16, name="mm_g_wbf")
    dy_xa = _mm(dpc, wbx, m=t, n=d, k=d, tb=True, tm=1024, name="mm_dy_xa")
    g_wbx = _mm(y_xa, dpc, m=d, n=d, k=t, ta=True, tk=4096, out_dtype=BF16, name="mm_g_wbx")

    dz, dxkn, dxv, g_xa_q = _xa_bwd(z, xkn, xvb, xa_q_g, dy_xa, dz, nb=nb, s=s, d=d, nm=nm, q_block=6)
    dkv, g_xa_k = _xa_post_kv(kv, dxkn, dxv, xa_k_g, rows=nb * nm, d=d)
    g_wkv_t = _mm(dkv, mem_n, m=2 * d, n=d, k=nb * nm, ta=True, out_dtype=BF16, name="mm_g_wkv")
    dmem_n = _mm(dkv, wkv_t, m=nb * nm, n=d, k=2 * d, tk=2 * d, name="mm_dmem")
    _, g_mem_norm = _rms_bwd(dmem_n, mem2, mem_norm_g, None, name="rms_mem_bwd")

    grads_b = [g_wkv_t, g_wbc, g_wbf, g_wbx, g_wo, (g_wfi_g, g_wfi_u), g_wfo]
    keep_b, send_b = pack_b.for_core(grads_b, my_c), pack_b.for_core(grads_b, 1 - my_c)
    dz, g_conv_w, g_conv_b, (from_sibling_b,) = _conv_bwd(z, dy_conv, conv_w_full, conv_b, dz, nb=nb, s=s, d=d,
                                                          comm=_SiblingSwap(send_b))
    part_b = _add2(keep_b, from_sibling_b, name="rs_add_sibling_b")
    dqn, ds_rows, dkn, dvb, ds_cols, (from_chips_b,) = _fox_bwd(qn, kn, vb, ck_rows, dy_fox, y_fox, lse, nb=nb, s=s, nh=nh,
                                                                comm=_ChipExchange(part_b))
    g_pack_b = _add4(lax.dynamic_index_in_dim(part_b, my_q, axis=0, keepdims=False), from_chips_b, name="rs_add_chips_b")
    dz, g_fox_q, g_fox_k = _fox_post(z, dqn, dkn, dvb, fox_q_g, fox_k_g, dz, t=t, d=d)
    dz, g_f_bias = _fox_gates_bwd(zff, f_bias_col, ds_rows, ds_cols, dz, nb=nb, s=s, nh=nh, ff_block=ff_block)

    keep_a, from_sibling_a = _g_win_slabs(dz, h1, my_c, t=t, d=d, n_slab=N_DEV, rows=rows_a, rp=slabs.shape[1])
    part_a = _add2(keep_a, from_sibling_a, name="rs_add_sibling_a")
    dh1, (from_chips_a,) = _dh1_from_slabs(dz, slabs, t=t, d=d, rows=rows_a, comm=_ChipExchange(part_a))
    g_pack_a = _add4(lax.dynamic_index_in_dim(part_a, my_q, axis=0, keepdims=False), from_chips_a, name="rs_add_chips_a",
                     rows_apart=True)
    dx, g_norm1 = _rms_bwd(dh1, x2, norm1_g, dx1, name="rms1_bwd")

    xd = d // XH
    tail = jnp.concatenate(
        [_pad_rows(g_f_bias.reshape(nh), HD).reshape(1, HD), g_fox_q, g_fox_k, g_xa_q, g_xa_k,
         jnp.zeros((1, d - 3 * HD - 2 * xd), F32)], axis=1)
    small = jnp.concatenate([g_norm1, g_norm2, g_conv_w, g_conv_b, g_mem_norm, tail], axis=0)
    small = _all_reduce_small(small)
    gs_conv_w = lax.dynamic_slice_in_dim(small[2:5], (2 * my_q + my_c) * (d // N_DEV), d // N_DEV, axis=1)

    grads = {
        "norm1_g": small[0:1], "w_in": g_pack_a, "conv_w": gs_conv_w, "conv_b": small[5:6],
        "fox_f_bias": small[7:8, 0:nh], "fox_q_g": small[7:8, HD:2 * HD], "fox_k_g": small[7:8, 2 * HD:3 * HD],
        "mem_norm_g": small[6:7], "w_mem_kv": pack_b.shard(g_pack_b, 0), "xa_q_g": small[7:8, 3 * HD:3 * HD + xd],
        "xa_k_g": small[7:8, 3 * HD + xd:3 * HD + 2 * xd], "w_br_conv": pack_b.shard(g_pack_b, 1),
        "w_br_fox": pack_b.shard(g_pack_b, 2), "w_br_xa": pack_b.shard(g_pack_b, 3), "w_o": pack_b.shard(g_pack_b, 4),
        "norm2_g": small[1:2], "w_ffn_in": pack_b.shard(g_pack_b, 5), "w_ffn_out": pack_b.shard(g_pack_b, 6),
    }
    transposed = {"w_in", "w_mem_kv", "w_ffn_in"}
    weights = {
        "norm1_g": (norm1_g, m_norm1_g, v_norm1_g), "w_in": (w_in, m_w_in, v_w_in), "conv_w": (conv_w, m_conv_w, v_conv_w),
        "conv_b": (conv_b, m_conv_b, v_conv_b), "fox_f_bias": (fox_f_bias, m_fox_f_bias, v_fox_f_bias),
        "fox_q_g": (fox_q_g, m_fox_q_g, v_fox_q_g), "fox_k_g": (fox_k_g, m_fox_k_g, v_fox_k_g),
        "mem_norm_g": (mem_norm_g, m_mem_norm_g, v_mem_norm_g), "w_mem_kv": (w_mem_kv, m_w_mem_kv, v_w_mem_kv),
        "xa_q_g": (xa_q_g, m_xa_q_g, v_xa_q_g), "xa_k_g": (xa_k_g, m_xa_k_g, v_xa_k_g),
        "w_br_conv": (w_br_conv, m_w_br_conv, v_w_br_conv), "w_br_fox": (w_br_fox, m_w_br_fox, v_w_br_fox),
        "w_br_xa": (w_br_xa, m_w_br_xa, v_w_br_xa), "w_o": (w_o, m_w_o, v_w_o), "norm2_g": (norm2_g, m_norm2_g, v_norm2_g),
        "w_ffn_in": (w_ffn_in, m_w_ffn_in, v_w_ffn_in), "w_ffn_out": (w_ffn_out, m_w_ffn_out, v_w_ffn_out),
    }
    out_g, out_d, out_m, out_v = [], [], [], []
    for name, (w, m, v) in weights.items():
        shape = w.shape
        if name == "w_in":
            w3, m3, v3 = (a.transpose(2, 0, 1) for a in (w, m, v))
            dl, mn, vn = _adamw(w3, g_pack_a, m3, v3, name="adamw_" + name)
            out_g.append(g_pack_a[:shape[2]].transpose(1, 2, 0))
            out_d.append(dl.transpose(1, 2, 0))
            out_m.append(mn.transpose(1, 2, 0))
            out_v.append(vn.transpose(1, 2, 0))
            continue
        tr = name in transposed

        def as2d(a):
            a = a.reshape(shape[-2], shape[-1])
            return a.T if tr else a

        def back(a):
            return (a.T if tr else a).reshape(shape)

        w2 = as2d(w)
        g2 = grads[name].reshape(w2.shape)
        dl, mn, vn = _adamw(w2, g2, as2d(m), as2d(v), name="adamw_" + name)
        out_g.append(back(g2))
        out_d.append(back(dl))
        out_m.append(back(mn))
        out_v.append(back(vn))

    loss = lax.psum(loss_local, ("x", "y", "c"))
    return (loss, dx.reshape(nb, s, d), *out_g, *out_d, *out_m, *out_v)
```

```python
import math

import jax
import jax.numpy as jnp
from jax import lax
from jax.experimental import pallas as pl
from jax.experimental.pallas import tpu as pltpu

F32, BF16 = jnp.float32, jnp.bfloat16
EPS = 1e-6
HD = 128
XH = 4
N_DEV = 8
MESH = pl.DeviceIdType.MESH
VMEM_LIMIT = 52 * 1024 * 1024
NEG = -1e30
SQRT_HD = math.sqrt(HD)
EXP2_C = math.log2(math.e) / SQRT_HD
LOG2E = math.log2(math.e)
FOX_HP = 4

ADAM_LR, ADAM_B1, ADAM_B2, ADAM_EPS, ADAM_WD, ADAM_STEP = 0.001, 0.9, 0.999, 1e-08, 0.01, 10


def _cp(**kw):
    return pltpu.CompilerParams(vmem_limit_bytes=VMEM_LIMIT, **kw)


def _pcall(body, **kw):
    return pl.pallas_call(body, **kw)


def _sds(shape, dtype):
    return jax.ShapeDtypeStruct(shape, dtype)


HBM_SPEC = pl.BlockSpec(memory_space=pl.ANY)


def _mesh_pos():
    return lax.axis_index("x"), lax.axis_index("y"), lax.axis_index("c")


class _AllGather:
    def __init__(self, blocks, forward_at=None):
        self.blocks = list(blocks)
        self.forward_at = forward_at
        n = len(self.blocks)
        self.out_shape = [_sds((N_DEV,) + b.shape, b.dtype) for b in self.blocks]
        self.scratch = [pltpu.SemaphoreType.DMA((n, 7)), pltpu.SemaphoreType.DMA((n, 7)), pltpu.SemaphoreType.DMA((n,))]

    def bind(self, ins, outs, sems):
        n = len(ins)
        send_sems, recv_sems, local_sems = sems
        x, y, c = _mesh_pos()
        me, sibling = (x, y, c), (x, y, 1 - c)
        chips = [(1 - x, y), (x, 1 - y), (1 - x, 1 - y)]

        def slab(t, dev):
            return outs[t].at[4 * dev[0] + 2 * dev[1] + dev[2]]

        def copy(t, k, block, to, src=None):
            dst = slab(t, block)
            return pltpu.make_async_remote_copy(
                src_ref=dst if src is None else src, dst_ref=dst, send_sem=send_sems.at[t, k], recv_sem=recv_sems.at[t, k],
                device_id=to, device_id_type=MESH)

        mine = [pltpu.make_async_copy(ins[t], slab(t, me), local_sems.at[t]) for t in range(n)]
        first = []
        for t in range(n):
            first.append(copy(t, 0, me, sibling, src=ins[t]))
            first += [copy(t, 1 + j, me, (*chip, c), src=ins[t]) for j, chip in enumerate(chips)]
        passed = [copy(t, 4 + j, (*chip, c), sibling) for j, chip in enumerate(chips) for t in range(n)]

        def start():
            for cp in mine + first:
                cp.start()

        def mid():
            for j, chip in enumerate(chips):
                for t in range(n):
                    copy(t, 1 + j, (*chip, c), me).wait_recv()
                    passed[j * n + t].start()

        def finish():
            for t in range(n):
                copy(t, 0, sibling, me).wait_recv()
                for j, chip in enumerate(chips):
                    copy(t, 4 + j, (*chip, 1 - c), me).wait_recv()
            for cp in first + passed:
                cp.wait_send()
            for cp in mine:
                cp.wait()

        return start, mid, finish


class _SiblingSwap:
    def __init__(self, send):
        self.blocks = [send]
        self.out_shape = [_sds(send.shape, send.dtype)]
        self.scratch = [pltpu.SemaphoreType.DMA, pltpu.SemaphoreType.DMA]

    def bind(self, ins, outs, sems):
        x, y, c = _mesh_pos()
        cp = pltpu.make_async_remote_copy(src_ref=ins[0], dst_ref=outs[0], send_sem=sems[0], recv_sem=sems[1],
                                          device_id=(x, y, 1 - c), device_id_type=MESH)
        return cp.start, (lambda: None), cp.wait


class _ChipExchange:
    def __init__(self, part):
        self.blocks = [part]
        self.out_shape = [_sds((3,) + part.shape[1:], part.dtype)]
        self.scratch = [pltpu.SemaphoreType.DMA((3,)), pltpu.SemaphoreType.DMA((3,))]

    def bind(self, ins, outs, sems):
        x, y, c = _mesh_pos()
        chips = [(1 - x, y), (x, 1 - y), (1 - x, 1 - y)]
        cps = [
            pltpu.make_async_remote_copy(src_ref=ins[0].at[2 * cx + cy], dst_ref=outs[0].at[j], send_sem=sems[0].at[j],
                                         recv_sem=sems[1].at[j], device_id=(cx, cy, c), device_id_type=MESH)
            for j, (cx, cy) in enumerate(chips)
        ]

        def start():
            for cp in cps:
                cp.start()

        def finish():
            for cp in cps:
                cp.wait()

        return start, (lambda: None), finish


def _comm_lists(comm):
    if comm is None:
        return [], [], [], []
    n = len(comm.blocks)
    return list(comm.blocks), [HBM_SPEC] * n, list(comm.out_shape), list(comm.scratch)


def _split_refs(refs, n_pre, n_in, n_cin, n_out, n_cout, n_scr):
    cuts = [n_pre, n_in, n_cin, n_out, n_cout, n_scr]
    out, at = [], 0
    for c in cuts:
        out.append(refs[at:at + c])
        at += c
    out.append(refs[at:])
    return out


def _run_comm(comm, cin, cout, csem, step, n_steps, when="start"):
    if comm is None:
        return
    start, mid, finish = comm.bind(cin, cout, csem)
    if when == "start":
        pl.when(step == 0)(start)
    else:
        frac = getattr(comm, "forward_at", None)
        mid_step = max(n_steps - 2, 0) if frac is None else min(int(frac * n_steps), max(n_steps - 2, 0))
        pl.when(step == mid_step)(mid)
        pl.when(step == n_steps - 1)(finish)


def _all_reduce_small(v):
    rows, cols = v.shape

    def body(v_ref, o_ref, slots, send_sems, recv_sems):
        x, y, c = _mesh_pos()
        me = 4 * x + 2 * y + c
        slots[me] = v_ref[...]
        cps = []
        for k in range(1, N_DEV):
            px, py, pc = x ^ (k >> 2), y ^ ((k >> 1) & 1), c ^ (k & 1)
            cps.append(pltpu.make_async_remote_copy(
                src_ref=v_ref, dst_ref=slots.at[me], send_sem=send_sems.at[k - 1], recv_sem=recv_sems.at[k - 1],
                device_id=(px, py, pc), device_id_type=MESH))
        for cp in cps:
            cp.start()
        for cp in cps:
            cp.wait()
        acc = slots[0]
        for k in range(1, N_DEV):
            acc = acc + slots[k]
        o_ref[...] = acc

    return _pcall(
        body, name="all_reduce_small", out_shape=_sds((rows, cols), F32),
        in_specs=[pl.BlockSpec(memory_space=pltpu.VMEM)], out_specs=pl.BlockSpec(memory_space=pltpu.VMEM),
        scratch_shapes=[pltpu.VMEM((N_DEV, rows, cols), F32), pltpu.SemaphoreType.DMA((7,)), pltpu.SemaphoreType.DMA((7,))],
        compiler_params=_cp(),
    )(v)


def _row_tile(rows, cap=1040):
    return max(tt for tt in range(16, cap + 1, 16) if rows % tt == 0)


def _add2(a, b, *, name):
    nq, rows, cols = a.shape
    tm = _row_tile(rows)

    def body(a_ref, b_ref, o_ref):
        o_ref[...] = (a_ref[...].astype(F32) + b_ref[...].astype(F32)).astype(BF16)

    spec = pl.BlockSpec((1, tm, cols), lambda q, i: (q, i, 0))
    return _pcall(body, name=name, out_shape=_sds(a.shape, BF16), grid=(nq, rows // tm),
                  in_specs=[spec, spec], out_specs=spec, compiler_params=_cp())(a, b)


def _add4(own, recv, *, name, rows_apart=False):
    rows, cols = own.shape
    tm = _row_tile(rows, 256 if rows_apart else 1040)

    def body(o_ref, r_ref, g_ref):
        tot = ((o_ref[...].astype(F32) + r_ref[0].astype(F32)) + r_ref[1].astype(F32)) + r_ref[2].astype(F32)
        if rows_apart:
            g_ref[:, 0, :] = tot
        else:
            g_ref[...] = tot

    if rows_apart:
        out_shape, out_spec = _sds((rows, 1, cols), F32), pl.BlockSpec((tm, 1, cols), lambda i: (i, 0, 0))
    else:
        out_shape, out_spec = _sds((rows, cols), F32), pl.BlockSpec((tm, cols), lambda i: (i, 0))
    return _pcall(
        body, name=name, out_shape=out_shape, grid=(rows // tm,),
        in_specs=[pl.BlockSpec((tm, cols), lambda i: (i, 0)), pl.BlockSpec((3, tm, cols), lambda i: (0, i, 0))],
        out_specs=out_spec, compiler_params=_cp(),
    )(own, recv)


def _mm(a, b, *, m, n, k, name, ta=False, tb=False, b_off=(0, 0), res=None, out_dtype=F32, tm=512, tn=1024, tk=1024,
        comm=None):
    tm, tn, tk = min(tm, m), min(tn, n), min(tk, k)
    assert m % tm == 0 and n % tn == 0 and k % tk == 0, (name, m, n, k, tm, tn, tk)
    nk = k // tk
    grid = (m // tm, n // tn, nk)
    bo0, bo1 = b_off
    if ta:
        a_spec = pl.BlockSpec((tk, tm), lambda i, j, kk: (kk, i))
    else:
        a_spec = pl.BlockSpec((tm, tk), lambda i, j, kk: (i, kk))
    if tb:
        b_spec = pl.BlockSpec((tn, tk), lambda i, j, kk: (j + bo0, kk + bo1))
    else:
        b_spec = pl.BlockSpec((tk, tn), lambda i, j, kk: (kk + bo0, j + bo1))
    o_spec = pl.BlockSpec((tm, tn), lambda i, j, kk: (i, j))
    dn = (((0 if ta else 1,), (1 if tb else 0,)), ((), ()))
    has_res = res is not None
    c_ins, c_in_specs, c_out_shape, c_scratch = _comm_lists(comm)
    n_in = 3 if has_res else 2
    n_scr = 1 if nk > 1 else 0

    def body(*refs):
        _, ins, cin, outs, cout, scr, csem = _split_refs(refs, 0, n_in, len(c_ins), 1, len(c_ins), n_scr)
        a_ref, b_ref = ins[0], ins[1]
        r_ref = ins[2] if has_res else None
        o_ref = outs[0]
        step = (pl.program_id(0) * grid[1] + pl.program_id(1)) * nk + pl.program_id(2)
        _run_comm(comm, cin, cout, csem, step, grid[0] * grid[1] * nk, "start")
        p = lax.dot_general(a_ref[...].astype(BF16), b_ref[...].astype(BF16), dn, preferred_element_type=F32)

        def finish(acc):
            if has_res:
                acc = acc + r_ref[...]
            o_ref[...] = acc.astype(out_dtype)

        if nk == 1:
            finish(p)
        else:
            acc_ref = scr[0]
            kk = pl.program_id(2)

            @pl.when(kk == 0)
            def _():
                acc_ref[...] = p

            @pl.when(kk > 0)
            def _():
                acc_ref[...] += p

            @pl.when(kk == nk - 1)
            def _():
                finish(acc_ref[...])

        _run_comm(comm, cin, cout, csem, step, grid[0] * grid[1] * nk, "end")

    ins = [a, b] + ([res] if has_res else [])
    in_specs = [a_spec, b_spec] + ([o_spec] if has_res else [])
    sem = ("arbitrary",) * 3 if comm is not None else ("parallel", "parallel", "arbitrary")
    outs = _pcall(
        body, name=name, out_shape=tuple([_sds((m, n), out_dtype)] + c_out_shape), grid=grid,
        in_specs=in_specs + c_in_specs, out_specs=tuple([o_spec] + [HBM_SPEC] * len(c_ins)),
        scratch_shapes=([pltpu.VMEM((tm, tn), F32)] if nk > 1 else []) + c_scratch,
        compiler_params=_cp(dimension_semantics=sem),
    )(*(ins + c_ins))
    return outs[0] if comm is None else (outs[0], list(outs[1:]))


SLAB_WIN = 1280
SLAB_TAIL = 48


def _chip_order(x, y, c):
    north = c == 1
    first = (jnp.where(north, 1 - x, x), jnp.where(north, y, 1 - y))
    second = (jnp.where(north, x, 1 - x), jnp.where(north, 1 - y, y))
    return [first, second, (1 - x, 1 - y)]


def _z_gathered(h1, own_slab, order, *, t, d, n_pad, rows, comm=None, tm=1024):
    rp = own_slab.shape[0]
    n_slab = N_DEV
    tm = min(tm, t)
    ni = t // tm
    sh = rows - SLAB_WIN
    assert sh >= 0 and (n_slab - 1) * sh <= 128
    c_ins, c_in_specs, c_out_shape, c_scratch = _comm_lists(comm)
    n_steps = n_slab * ni

    def body(*refs):
        pre, ins, cin, outs, cout, scr, csem = _split_refs(refs, 1, 2, len(c_ins), 2, len(c_ins), 5)
        order_ref = pre[0]
        h_ref, own_ref = ins
        o_ref, slabs_ref = outs
        slab_vm, send_sems, recv_sems, local_sem, load_sem = scr
        p, i = pl.program_id(0), pl.program_id(1)
        step = p * ni + i
        x, y, c = _mesh_pos()
        me, sibling = (x, y, c), (x, y, 1 - c)
        chips = _chip_order(x, y, c)

        def slab(dev):
            return slabs_ref.at[4 * dev[0] + 2 * dev[1] + dev[2]]

        def copy(k, block, to, src=None, part=None):
            dst = slab(block) if part is None else slab(block).at[part]
            return pltpu.make_async_remote_copy(
                src_ref=dst if src is None else src, dst_ref=dst, send_sem=send_sems.at[k], recv_sem=recv_sems.at[k],
                device_id=to, device_id_type=MESH)

        nb_1, nb_2, diag = ((*chip, c) for chip in chips)
        cut = (rp // 32) * 16
        half_a, half_b = pl.ds(0, cut), pl.ds(cut, rp - cut)
        mine = pltpu.make_async_copy(own_ref, slab(me), local_sem)
        first = [copy(0, me, sibling, src=own_ref), copy(1, me, nb_1, src=own_ref), copy(2, me, nb_2, src=own_ref)]
        relays = [copy(3, nb_1, nb_2, part=half_a), copy(4, nb_2, nb_1, part=half_b)]
        axis_of = [jnp.where(c == 1, 0, 1), jnp.where(c == 1, 1, 0), 2]
        passed = [copy(5 + axis_of[j], (*chip, c), sibling) for j, chip in enumerate(chips)]
        later = min(n_steps // 5, n_steps - 1)

        @pl.when(step == 0)
        def _():
            mine.start()
            first[0].start()
            first[1].start()

        @pl.when(step == later)
        def _():
            first[2].start()

        _run_comm(comm, cin, cout, csem, step, n_steps, "start")

        def load(src):
            cp = pltpu.make_async_copy(src, slab_vm, load_sem)
            cp.start()
            cp.wait()

        @pl.when(i == 0)
        def _():
            @pl.when(p == 0)
            def _():
                load(own_ref)

            @pl.when(p == 1)
            def _():
                copy(0, sibling, me).wait_recv()
                load(slab(sibling))

            for j, chip in enumerate(chips):
                @pl.when(p == 2 + 2 * j)
                def _():
                    if j < 2:
                        copy(1 + j, (*chip, c), me).wait_recv()
                        relays[j].start()
                    else:
                        copy(3, diag, me, part=half_a).wait_recv()
                        copy(4, diag, me, part=half_b).wait_recv()
                    passed[j].start()
                    load(slab((*chip, c)))

                other = (1, 0, 2)[j]

                @pl.when(p == 3 + 2 * j)
                def _():
                    copy(5 + axis_of[other], (*chips[other], 1 - c), me).wait_recv()
                    load(slab((*chips[other], 1 - c)))

        k = order_ref[p]
        y_k = lax.dot_general(h_ref[...], slab_vm[...], (((1,), (1,)), ((), ())), preferred_element_type=F32)
        o_ref[...] = pltpu.roll(y_k[:, 0:SLAB_WIN], k * sh, axis=1).astype(BF16)

        @pl.when(step == n_steps - 1)
        def _():
            for cp in first + relays + passed:
                cp.wait_send()
            mine.wait()

        _run_comm(comm, cin, cout, csem, step, n_steps, "end")

    outs = _pcall(
        body, name="mm_z", out_shape=tuple([_sds((t, n_pad), BF16), _sds((n_slab, rp, d), BF16)] + c_out_shape),
        grid_spec=pltpu.PrefetchScalarGridSpec(
            num_scalar_prefetch=1, grid=(n_slab, ni),
            in_specs=[pl.BlockSpec((tm, d), lambda p, i, order: (i, 0)), HBM_SPEC] + c_in_specs,
            out_specs=tuple([pl.BlockSpec((tm, SLAB_WIN), lambda p, i, order: (i, order[p])), HBM_SPEC] + [HBM_SPEC] * len(c_ins)),
            scratch_shapes=[pltpu.VMEM((rp, d), BF16), pltpu.SemaphoreType.DMA((8,)), pltpu.SemaphoreType.DMA((8,)),
                            pltpu.SemaphoreType.DMA, pltpu.SemaphoreType.DMA] + c_scratch),
        compiler_params=_cp(dimension_semantics=("arbitrary", "arbitrary")),
    )(order, h1, own_slab, *c_ins)
    return outs[0], outs[1], list(outs[2:])


def _z_fix(h1, slabs, z, *, t, d, rows, tm=4096):
    n_slab, rp, _ = slabs.shape
    tm = min(tm, t)
    sh = rows - SLAB_WIN
    tail_lo = rp - SLAB_TAIL
    assert rows - (n_slab - 1) * sh >= tail_lo and rp % SLAB_TAIL == 0

    def body(h_ref, t_ref, z_ref, o_ref):
        k = pl.program_id(0) + 1
        p = lax.dot_general(h_ref[...], t_ref[0], (((1,), (1,)), ((), ())), preferred_element_type=F32)
        pp = jnp.concatenate([p, jnp.zeros((tm, 128 - SLAB_TAIL), F32)], axis=1)
        pr = pltpu.roll(pp, 128 - (rows - tail_lo) + k * sh, axis=1)
        lane = lax.broadcasted_iota(jnp.int32, (tm, 128), 1)
        o_ref[...] = jnp.where(lane < k * sh, pr.astype(BF16), z_ref[...])

    blk = pl.BlockSpec((tm, 128), lambda b, i: (i, (SLAB_WIN // 128) * (b + 1)))
    return _pcall(
        body, name="mm_z_fix", out_shape=_sds(z.shape, z.dtype), grid=(n_slab - 1, t // tm),
        in_specs=[pl.BlockSpec((tm, d), lambda b, i: (i, 0)),
                  pl.BlockSpec((1, SLAB_TAIL, d), lambda b, i: (b, tail_lo // SLAB_TAIL, 0)), blk],
        out_specs=blk, input_output_aliases={2: 0}, compiler_params=_cp(),
    )(h1, slabs, z)


def _slab_window(a1_ref, a2_ref, k, sh):
    w = jnp.concatenate([a1_ref[...], a2_ref[...]], axis=1)
    width = SLAB_WIN + 128
    return pltpu.roll(w, width - k * sh, axis=1)


def _g_win_slabs(dz, h1, my_c, *, t, d, n_slab, rows, rp, tk=2048):
    sh = rows - SLAB_WIN
    tk = min(tk, t)
    nk = t // tk
    width = SLAB_WIN + 128
    half = n_slab // 2

    def slab_of(p, c):
        return 2 * (p % half) + jnp.where(p < half, 1 - c, c)

    def body(c_ref, a1_ref, a2_ref, h_ref, keep_ref, land_ref, acc, stage, send_sems, recv_sems):
        p, kk = pl.program_id(0), pl.program_id(1)
        x, y, c = _mesh_pos()
        ws = _slab_window(a1_ref, a2_ref, slab_of(p, c), sh)
        prod = lax.dot_general(ws, h_ref[...], (((0,), (0,)), ((), ())), preferred_element_type=F32)

        def push(q):
            return pltpu.make_async_remote_copy(
                src_ref=stage.at[q % 2], dst_ref=land_ref.at[q], send_sem=send_sems.at[q], recv_sem=recv_sems.at[q],
                device_id=(x, y, 1 - c), device_id_type=MESH)

        @pl.when(kk == 0)
        def _():
            acc[...] = prod

        @pl.when(kk > 0)
        def _():
            acc[...] += prod

        @pl.when(kk == nk - 1)
        def _():
            res = acc[0:rp, :].astype(BF16)
            for q in range(half):
                @pl.when(p == q)
                def _():
                    if q >= 2:
                        push(q - 2).wait_send()
                    stage[q % 2] = res
                    push(q).start()

            @pl.when(p >= half)
            def _():
                keep_ref[0] = res

        @pl.when((p == n_slab - 1) & (kk == nk - 1))
        def _():
            for q in range(max(half - 2, 0), half):
                push(q).wait_send()
            for q in range(half):
                push(q).wait_recv()

    outs = _pcall(
        body, name="mm_g_win", out_shape=(_sds((half, rp, d), BF16), _sds((half, rp, d), BF16)),
        grid_spec=pltpu.PrefetchScalarGridSpec(
            num_scalar_prefetch=1, grid=(n_slab, nk),
            in_specs=[pl.BlockSpec((tk, SLAB_WIN), lambda p, kk, c: (kk, slab_of(p, c[0]))),
                      pl.BlockSpec((tk, 128), lambda p, kk, c: (kk, (SLAB_WIN // 128) * (slab_of(p, c[0]) + 1))),
                      pl.BlockSpec((tk, d), lambda p, kk, c: (kk, 0))],
            out_specs=(pl.BlockSpec((1, rp, d), lambda p, kk, c: (jnp.maximum(p - half, 0), 0, 0)), HBM_SPEC),
            scratch_shapes=[pltpu.VMEM((width, d), F32), pltpu.VMEM((2, rp, d), BF16),
                            pltpu.SemaphoreType.DMA((half,)), pltpu.SemaphoreType.DMA((half,))]),
        compiler_params=_cp(dimension_semantics=("arbitrary", "arbitrary")),
    )(my_c.reshape(1).astype(jnp.int32), dz, dz, h1)
    return outs[0], outs[1]


def _dh1_from_slabs(dz, slabs, *, t, d, rows, comm=None, tm=1024):
    n_slab, rp, _ = slabs.shape
    sh = rows - SLAB_WIN
    width = SLAB_WIN + 128
    c_ins, c_in_specs, c_out_shape, c_scratch = _comm_lists(comm)
    grid = (t // tm, n_slab)

    def body(*refs):
        _, ins, cin, outs, cout, scr, csem = _split_refs(refs, 0, 3, len(c_ins), 1, len(c_ins), 1)
        a1_ref, a2_ref, s_ref = ins
        o_ref, acc = outs[0], scr[0]
        k = pl.program_id(1)
        step = pl.program_id(0) * n_slab + k
        _run_comm(comm, cin, cout, csem, step, grid[0] * n_slab, "start")
        ws = _slab_window(a1_ref, a2_ref, k, sh)
        s_end = jnp.concatenate([s_ref[0, SLAB_WIN:rp, :], jnp.zeros((width - rp, d), BF16)], axis=0)
        p = (jnp.dot(ws[:, 0:SLAB_WIN], s_ref[0, 0:SLAB_WIN, :], preferred_element_type=F32)
             + jnp.dot(ws[:, SLAB_WIN:width], s_end, preferred_element_type=F32))

        @pl.when(k == 0)
        def _():
            acc[...] = p

        @pl.when(k > 0)
        def _():
            acc[...] += p

        @pl.when(k == n_slab - 1)
        def _():
            o_ref[...] = acc[...]

        _run_comm(comm, cin, cout, csem, step, grid[0] * n_slab, "end")

    outs = _pcall(
        body, name="mm_dh1", out_shape=tuple([_sds((t, d), F32)] + c_out_shape), grid=grid,
        in_specs=[pl.BlockSpec((tm, SLAB_WIN), lambda i, k: (i, k)),
                  pl.BlockSpec((tm, 128), lambda i, k: (i, (SLAB_WIN // 128) * (k + 1))),
                  pl.BlockSpec((1, rp, d), lambda i, k: (k, 0, 0))] + c_in_specs,
        out_specs=tuple([pl.BlockSpec((tm, d), lambda i, k: (i, 0))] + [HBM_SPEC] * len(c_ins)),
        scratch_shapes=[pltpu.VMEM((tm, d), F32)] + c_scratch,
        compiler_params=_cp(dimension_semantics=("arbitrary", "arbitrary")),
    )(dz, dz, slabs, *c_ins)
    return outs[0], list(outs[1:])


def _rms_fwd(x, g, *, name, tm=1024):
    t, d = x.shape
    tm = min(tm, t)

    def body(x_ref, g_ref, h_ref):
        xv = x_ref[...]
        r = lax.rsqrt(jnp.mean(xv * xv, axis=-1, keepdims=True) + EPS)
        h_ref[...] = (xv * r * g_ref[...]).astype(BF16)

    return _pcall(
        body, name=name, out_shape=_sds((t, d), BF16), grid=(t // tm,),
        in_specs=[pl.BlockSpec((tm, d), lambda i: (i, 0)), pl.BlockSpec((1, d), lambda i: (0, 0))],
        out_specs=pl.BlockSpec((tm, d), lambda i: (i, 0)), compiler_params=_cp(),
    )(x, g)


def _rms_bwd(dh, x, g, res, *, name, tm=1024):
    t, d = x.shape
    tm = min(tm, t)
    has_res = res is not None

    def body(*refs):
        dh_ref, x_ref, g_ref = refs[:3]
        r_ref = refs[3] if has_res else None
        dx_ref, dg_ref = refs[-2], refs[-1]
        xv = x_ref[...]
        r = lax.rsqrt(jnp.mean(xv * xv, axis=-1, keepdims=True) + EPS)
        xh = xv * r
        dhv = dh_ref[...]
        gd = dhv * g_ref[...]
        dx = r * (gd - xh * jnp.mean(gd * xh, axis=-1, keepdims=True))
        if has_res:
            dx = dx + r_ref[...]
        dx_ref[...] = dx

        @pl.when(pl.program_id(0) == 0)
        def _():
            dg_ref[...] = jnp.zeros_like(dg_ref)

        dg_ref[...] += jnp.sum(dhv * xh, axis=0, keepdims=True)

    row = pl.BlockSpec((tm, d), lambda i: (i, 0))
    vec = pl.BlockSpec((1, d), lambda i: (0, 0))
    return _pcall(
        body, name=name, out_shape=(_sds((t, d), F32), _sds((1, d), F32)), grid=(t // tm,),
        in_specs=[row, row, vec] + ([row] if has_res else []), out_specs=(row, vec),
        compiler_params=_cp(dimension_semantics=("arbitrary",)),
    )(*([dh, x, g] + ([res] if has_res else [])))


def _conv_fwd(z, conv_w, conv_b, *, nb, s, d, tc=512):
    nc = d // tc

    def body(cb_ref, cc_ref, cv_ref, w_ref, b_ref, y_ref):
        u = cc_ref[...].astype(F32) * cv_ref[...].astype(F32)
        row = lax.broadcasted_iota(jnp.int32, u.shape, 0)
        u1 = jnp.where(row >= 1, pltpu.roll(u, 1, axis=0), 0.0)
        u2 = jnp.where(row >= 2, pltpu.roll(u, 2, axis=0), 0.0)
        w = w_ref[...]
        y = w[2:3, :] * u + w[1:2, :] * u1 + w[0:1, :] * u2 + b_ref[...]
        y_ref[...] = (cb_ref[...].astype(F32) * y).astype(BF16)

    def part(p):
        return pl.BlockSpec((s, tc), lambda b, j: (b, p * nc + j))

    return _pcall(
        body, name="conv_fwd", out_shape=_sds((nb * s, d), BF16), grid=(nb, nc),
        in_specs=[part(0), part(1), part(2), pl.BlockSpec((3, tc), lambda b, j: (0, j)), pl.BlockSpec((1, tc), lambda b, j: (0, j))],
        out_specs=pl.BlockSpec((s, tc), lambda b, j: (b, j)), compiler_params=_cp(),
    )(z, z, z, conv_w, conv_b)


def _conv_bwd(z, dy, conv_w, conv_b, dz, *, nb, s, d, tc=256, comm=None):
    nc = d // tc
    c_ins, c_in_specs, c_out_shape, c_scratch = _comm_lists(comm)
    n_steps = nc * nb * 3

    def body(*refs):
        _, ins, cin, outs, cout, scr, csem = _split_refs(refs, 0, 7, len(c_ins), 3, len(c_ins), 1)
        cb_ref, cc_ref, cv_ref, dy_ref, w_ref, b_ref, _ = ins
        dz_ref, dw_ref, db_ref = outs
        stash = scr[0]
        b_i, p = pl.program_id(1), pl.program_id(2)
        step = (pl.program_id(0) * nb + b_i) * 3 + p
        _run_comm(comm, cin, cout, csem, step, n_steps, "start")

        @pl.when(p == 0)
        def _():
            cc, cv = cc_ref[...].astype(F32), cv_ref[...].astype(F32)
            u = cc * cv
            n = u.shape[0]
            row = lax.broadcasted_iota(jnp.int32, u.shape, 0)
            u1 = jnp.where(row >= 1, pltpu.roll(u, 1, axis=0), 0.0)
            u2 = jnp.where(row >= 2, pltpu.roll(u, 2, axis=0), 0.0)
            w = w_ref[...]
            y = w[2:3, :] * u + w[1:2, :] * u1 + w[0:1, :] * u2 + b_ref[...]
            dyv = dy_ref[...]
            dconv = dyv * cb_ref[...].astype(F32)
            g1 = jnp.where(row < n - 1, pltpu.roll(dconv, n - 1, axis=0), 0.0)
            g2 = jnp.where(row < n - 2, pltpu.roll(dconv, n - 2, axis=0), 0.0)
            du = w[2:3, :] * dconv + w[1:2, :] * g1 + w[0:1, :] * g2
            stash[0] = (dyv * y).astype(BF16)
            stash[1] = (du * cv).astype(BF16)
            stash[2] = (du * cc).astype(BF16)
            dws = [jnp.sum(dconv * u2, axis=0, keepdims=True), jnp.sum(dconv * u1, axis=0, keepdims=True),
                   jnp.sum(dconv * u, axis=0, keepdims=True)]
            db = jnp.sum(dconv, axis=0, keepdims=True)

            @pl.when(b_i == 0)
            def _():
                for r in range(3):
                    dw_ref[r:r + 1, :] = dws[r]
                db_ref[...] = db

            @pl.when(b_i > 0)
            def _():
                for r in range(3):
                    dw_ref[r:r + 1, :] += dws[r]
                db_ref[...] += db

        dz_ref[...] = stash[p]
        _run_comm(comm, cin, cout, csem, step, n_steps, "end")

    outs = _pcall(
        body, name="conv_bwd",
        out_shape=tuple([_sds(dz.shape, dz.dtype), _sds((3, d), F32), _sds((1, d), F32)] + c_out_shape), grid=(nc, nb, 3),
        in_specs=[
            pl.BlockSpec((s, tc), lambda j, b, p: (b, j)), pl.BlockSpec((s, tc), lambda j, b, p: (b, nc + j)),
            pl.BlockSpec((s, tc), lambda j, b, p: (b, 2 * nc + j)), pl.BlockSpec((s, tc), lambda j, b, p: (b, j)),
            pl.BlockSpec((3, tc), lambda j, b, p: (0, j)), pl.BlockSpec((1, tc), lambda j, b, p: (0, j)),
            HBM_SPEC,
        ] + c_in_specs,
        out_specs=tuple([pl.BlockSpec((s, tc), lambda j, b, p: (b, p * nc + j)),
                         pl.BlockSpec((3, tc), lambda j, b, p: (0, j)), pl.BlockSpec((1, tc), lambda j, b, p: (0, j))]
                        + [HBM_SPEC] * len(c_ins)),
        scratch_shapes=[pltpu.VMEM((3, s, tc), BF16)] + c_scratch,
        input_output_aliases={6: 0},
        compiler_params=_cp(dimension_semantics=("arbitrary", "arbitrary", "arbitrary")),
    )(z, z, z, dy, conv_w, conv_b, dz, *c_ins)
    return outs[0], outs[1], outs[2], list(outs[3:])


def _head_rms(xv, g):
    r = lax.rsqrt(jnp.mean(xv * xv, axis=-1, keepdims=True) + EPS)
    return xv * r * g


def _head_rms_bwd(xv, g, dy):
    r = lax.rsqrt(jnp.mean(xv * xv, axis=-1, keepdims=True) + EPS)
    xh = xv * r
    gd = dy * g
    dx = r * (gd - xh * jnp.mean(gd * xh, axis=-1, keepdims=True))
    return dx, jnp.sum(dy * xh, axis=0, keepdims=True)


def _fox_prep(z, q_g, k_g, *, t, d, tm=512):
    nh = d // HD

    def body(z_ref, qg_ref, kg_ref, q_ref, k_ref, v_ref):
        qg, kg = qg_ref[...], kg_ref[...]
        for h in range(nh):
            c0 = h * HD
            q_ref[:, c0:c0 + HD] = _head_rms(z_ref[:, c0:c0 + HD].astype(F32), qg).astype(BF16)
            k_ref[:, c0:c0 + HD] = _head_rms(z_ref[:, d + c0:d + c0 + HD].astype(F32), kg).astype(BF16)
        v_ref[...] = z_ref[:, 2 * d:3 * d]

    o = pl.BlockSpec((tm, d), lambda i: (i, 0))
    g = pl.BlockSpec((1, HD), lambda i: (0, 0))
    return _pcall(
        body, name="fox_prep", out_shape=(_sds((t, d), BF16),) * 3, grid=(t // tm,),
        in_specs=[pl.BlockSpec((tm, 3 * d), lambda i: (i, 1)), g, g], out_specs=(o, o, o), compiler_params=_cp(),
    )(z, q_g, k_g)


def _lane_cumsum(v, reverse=False):
    n = v.shape[1]
    lane = lax.broadcasted_iota(jnp.int32, v.shape, 1)
    sh = 1
    while sh < n:
        if reverse:
            v = v + jnp.where(lane < n - sh, pltpu.roll(v, n - sh, axis=1), 0.0)
        else:
            v = v + jnp.where(lane >= sh, pltpu.roll(v, sh, axis=1), 0.0)
        sh *= 2
    return v


def _fox_gates(z, f_bias_col, *, nb, s, nh):
    def body(z_ref, b_ref, c_ref):
        ff = z_ref[...].T[0:nh, :] + b_ref[...]
        lf = jnp.minimum(ff, 0.0) - jnp.log(1.0 + jnp.exp(-jnp.abs(ff)))
        c_ref[0] = _lane_cumsum(lf) * SQRT_HD

    return _pcall(
        body, name="fox_gates", out_shape=_sds((nb, nh, s), F32), grid=(nb,),
        in_specs=[pl.BlockSpec((s, HD), lambda b: (b, 0)), pl.BlockSpec((nh, 1), lambda b: (0, 0))],
        out_specs=pl.BlockSpec((1, nh, s), lambda b: (b, 0, 0)), compiler_params=_cp(),
    )(z, f_bias_col)


def _fox_gates_bwd(z, f_bias_col, row_sums, col_sums, dz, *, nb, s, nh, ff_block):
    def body(z_ref, b_ref, rs_ref, cs_ref, dz_in, dz_ref, db_ref, dc_sc):
        b_i, h = pl.program_id(0), pl.program_id(1)
        dc_sc[pl.ds(h, 1), :] = (rs_ref[...] - cs_ref[...]).T[0:1, :]

        @pl.when(h == nh - 1)
        def _():
            ff = z_ref[...].T[0:nh, :] + b_ref[...]
            dlf = _lane_cumsum(dc_sc[...], reverse=True)
            dff = dlf * (1.0 / (1.0 + jnp.exp(ff)))
            pad = jnp.concatenate([dff, jnp.zeros((HD - nh, s), F32)], axis=0)
            dz_ref[...] = pad.T.astype(BF16)
            dbv = jnp.sum(dff, axis=1, keepdims=True)

            @pl.when(b_i == 0)
            def _():
                db_ref[...] = dbv

            @pl.when(b_i > 0)
            def _():
                db_ref[...] += dbv

    hs = pl.BlockSpec((s, HD), lambda b, h: (b, h))
    return _pcall(
        body, name="fox_gates_bwd", out_shape=(_sds(dz.shape, dz.dtype), _sds((nh, 1), F32)), grid=(nb, nh),
        in_specs=[pl.BlockSpec((s, HD), lambda b, h: (b, 0)), pl.BlockSpec((nh, 1), lambda b, h: (0, 0)), hs, hs, HBM_SPEC],
        out_specs=(pl.BlockSpec((s, HD), lambda b, h: (b, ff_block)), pl.BlockSpec((nh, 1), lambda b, h: (0, 0))),
        scratch_shapes=[pltpu.VMEM((nh, s), F32)],
        input_output_aliases={4: 0}, compiler_params=_cp(dimension_semantics=("arbitrary", "arbitrary")),
    )(z, f_bias_col, row_sums, col_sums, dz)


def _fox_pairs(nq, key_major):
    if key_major:
        pairs = [(i, j) for j in range(nq) for i in range(j, nq)]
    else:
        pairs = [(i, j) for i in range(nq) for j in range(i + 1)]
    return jnp.array([p[0] for p in pairs], jnp.int32), jnp.array([p[1] for p in pairs], jnp.int32)


def _with_ones(x):
    return jnp.concatenate([x, jnp.ones_like(x)], axis=1)


def _fox_specs(nb, nh, nq, tq, tk, hp):
    qs = pl.BlockSpec((tq, hp * HD), lambda b, h, p, *tb: (b * nq + tb[0][p], h))
    ks = pl.BlockSpec((tk, hp * HD), lambda b, h, p, *tb: (b * nq + tb[1][p], h))
    cs = pl.BlockSpec((hp, 1, tk), lambda b, h, p, *tb: (b * (nh // hp) + h, 0, tb[1][p]))
    return qs, ks, cs


def _fox_raw(qv, kv, ckv, diag, tq, tk):
    sc = lax.dot_general(qv, kv, (((1,), (1,)), ((), ())), preferred_element_type=F32) - ckv
    if diag:
        rows = lax.broadcasted_iota(jnp.int32, (tq, tk), 0)
        cols = lax.broadcasted_iota(jnp.int32, (tq, tk), 1)
        sc = jnp.where(rows >= cols, sc, NEG)
    return sc


def _fox_probs(qv, kv, ckv, lse_col, diag, tq, tk):
    sc = _fox_raw(qv, kv, ckv, diag, tq, tk)
    return jnp.exp2(sc * EXP2_C - lse_col * LOG2E)


def _fox_call(body, *, name, ins, in_specs, out_shape, out_specs, scratch, grid, tables, comm):
    c_ins, c_in_specs, c_out_shape, c_scratch = _comm_lists(comm)
    n_in, n_out, n_scr, nc = len(ins), len(out_shape), len(scratch), len(c_ins)
    n_steps = grid[0] * grid[1] * grid[2]

    def wrapped(*refs):
        pre, rin, cin, rout, cout, scr, csem = _split_refs(refs, len(tables), n_in, nc, n_out, nc, n_scr)
        step = (pl.program_id(0) * grid[1] + pl.program_id(1)) * grid[2] + pl.program_id(2)
        _run_comm(comm, cin, cout, csem, step, n_steps, "start")
        body(*pre, *rin, *rout, *scr)
        _run_comm(comm, cin, cout, csem, step, n_steps, "end")

    sem = ("arbitrary",) * 3 if comm is not None else ("parallel", "parallel", "arbitrary")
    outs = _pcall(
        wrapped, name=name, out_shape=tuple(list(out_shape) + c_out_shape),
        grid_spec=pltpu.PrefetchScalarGridSpec(
            num_scalar_prefetch=len(tables), grid=grid, in_specs=list(in_specs) + c_in_specs,
            out_specs=tuple(list(out_specs) + [HBM_SPEC] * nc), scratch_shapes=list(scratch) + c_scratch),
        compiler_params=_cp(dimension_semantics=sem),
    )(*tables, *ins, *c_ins)
    return list(outs[:n_out]), list(outs[n_out:])


def _fox_fwd(q, k, v, cks, *, nb, s, nh, tq=512, comm=None):
    tk = tq
    nq = s // tq
    t = nb * s
    hp = FOX_HP
    it, jt = _fox_pairs(nq, key_major=False)

    def body(it_ref, jt_ref, q_ref, k_ref, v_ref, c_ref, o_ref, lse_ref, m_sc, acc_sc):
        p_id = pl.program_id(2)
        i, j = it_ref[p_id], jt_ref[p_id]

        @pl.when(j == 0)
        def _():
            m_sc[...] = jnp.full_like(m_sc, NEG)
            acc_sc[...] = jnp.zeros_like(acc_sc)

        def step(diag):
            for hh in range(hp):
                cols = slice(hh * HD, (hh + 1) * HD)
                sc = _fox_raw(q_ref[:, cols], k_ref[:, cols], c_ref[hh], diag, tq, tk)
                m_old = m_sc[hh]
                m_new = jnp.maximum(m_old, jnp.max(sc, axis=-1, keepdims=True))
                alpha = jnp.exp2((m_old - m_new) * EXP2_C)
                p = jnp.exp2((sc - m_new) * EXP2_C).astype(BF16)
                acc = alpha * acc_sc[hh] + jnp.dot(p, _with_ones(v_ref[:, cols]), preferred_element_type=F32)
                if diag:
                    l = acc[:, HD:2 * HD]
                    o_ref[:, cols] = (acc[:, 0:HD] / l).astype(BF16)
                    lse_ref[:, cols] = m_new * (1.0 / SQRT_HD) + jnp.log(l)
                else:
                    acc_sc[hh] = acc
                    m_sc[hh] = m_new

        @pl.when(j < i)
        def _():
            step(False)

        @pl.when(j == i)
        def _():
            step(True)

    qs, ks, cs = _fox_specs(nb, nh, nq, tq, tk, hp)
    (o, lse), c_out = _fox_call(
        body, name="fox_fwd", ins=[q, k, v, cks], in_specs=[qs, ks, ks, cs],
        out_shape=[_sds((t, nh * HD), BF16), _sds((t, nh * HD), F32)], out_specs=[qs, qs],
        scratch=[pltpu.VMEM((hp, tq, 1), F32), pltpu.VMEM((hp, tq, 2 * HD), F32)],
        grid=(nb, nh // hp, int(it.shape[0])), tables=(it, jt), comm=comm)
    return o, lse, c_out


def _fox_bwd(q, k, v, cks, do, o, lse, *, nb, s, nh, tq=512, comm=None):
    tk = tq
    nq = s // tq
    t = nb * s
    hp = FOX_HP
    it, jt = _fox_pairs(nq, key_major=True)
    done = jnp.array([j for j in range(nq) for _ in range(j, nq)], jnp.int32)

    def body(it_ref, jt_ref, done_ref, q_ref, k_ref, v_ref, c_ref, do_ref, o_ref, lse_ref,
             dq_ref, rs_ref, dk_ref, dv_ref, cs_ref, dq_sc, dk_sc, dv_sc):
        p_id = pl.program_id(2)
        i, j = it_ref[p_id], jt_ref[p_id]
        rows_i = pl.ds(pl.multiple_of(i * tq, tq), tq)

        def step(diag):
            for hh in range(hp):
                cols = slice(hh * HD, (hh + 1) * HD)
                qv, kv = q_ref[:, cols], k_ref[:, cols]
                p = _fox_probs(qv, kv, c_ref[hh], lse_ref[:, hh * HD:hh * HD + 1], diag, tq, tk)
                dov = do_ref[:, cols]
                dob = dov.astype(BF16)
                dvp = lax.dot_general(p.astype(BF16), dob, (((0,), (0,)), ((), ())), preferred_element_type=F32)
                dp = lax.dot_general(dob, v_ref[:, cols], (((1,), (1,)), ((), ())), preferred_element_type=F32)
                delta = jnp.sum(dov * o_ref[:, cols].astype(F32), axis=-1, keepdims=True)
                ds = (p * (dp - delta)).astype(BF16)
                dkp = lax.dot_general(ds, _with_ones(qv), (((0,), (0,)), ((), ())), preferred_element_type=F32)
                dqp = jnp.dot(ds, _with_ones(kv), preferred_element_type=F32)
                if diag:
                    dk_sc[hh] = dkp
                    dv_sc[hh] = dvp
                else:
                    dk_sc[hh] += dkp
                    dv_sc[hh] += dvp

                @pl.when(j == 0)
                def _():
                    dq_sc[hh, rows_i, :] = dqp

                @pl.when(j > 0)
                def _():
                    dq_sc[hh, rows_i, :] += dqp

                if diag:
                    acc = dq_sc[hh, rows_i, :]
                    dq_ref[:, cols] = acc[:, 0:HD] * (1.0 / SQRT_HD)
                    rs_ref[:, cols] = acc[:, HD:2 * HD]

        @pl.when(i == j)
        def _():
            step(True)

        @pl.when(i > j)
        def _():
            step(False)

        @pl.when(i == nq - 1)
        def _():
            for hh in range(hp):
                cols = slice(hh * HD, (hh + 1) * HD)
                dk_ref[:, cols] = dk_sc[hh, :, 0:HD] * (1.0 / SQRT_HD)
                cs_ref[:, cols] = dk_sc[hh, :, HD:2 * HD]
                dv_ref[:, cols] = dv_sc[hh].astype(BF16)

    qs, ks, cs = _fox_specs(nb, nh, nq, tq, tk, hp)
    ds_spec = pl.BlockSpec((tq, hp * HD), lambda b, h, p, *tb: (b * nq + tb[2][p], h))
    (dq, rs, dk, dv, csum), c_out = _fox_call(
        body, name="fox_bwd", ins=[q, k, v, cks, do, o, lse], in_specs=[qs, ks, ks, cs, qs, qs, qs],
        out_shape=[_sds((t, nh * HD), F32), _sds((t, nh * HD), F32), _sds((t, nh * HD), F32), _sds((t, nh * HD), BF16),
                   _sds((t, nh * HD), F32)],
        out_specs=[ds_spec, ds_spec, ks, ks, ks],
        scratch=[pltpu.VMEM((hp, s, 2 * HD), F32), pltpu.VMEM((hp, tk, 2 * HD), F32), pltpu.VMEM((hp, tk, HD), F32)],
        grid=(nb, nh // hp, int(it.shape[0])), tables=(it, jt, done), comm=comm)
    return dq, rs, dk, dv, csum, c_out


def _fox_post(z, dqn, dkn, dv, q_g, k_g, dz, *, t, d, tm=512):
    nh = d // HD

    def body(z_ref, dq_ref, dk_ref, dv_ref, qg_ref, kg_ref, dz_in, dz_ref, dqg_ref, dkg_ref):
        qg, kg = qg_ref[...], kg_ref[...]
        aq = jnp.zeros((1, HD), F32)
        ak = jnp.zeros((1, HD), F32)
        for h in range(nh):
            c0 = h * HD
            dx, sg = _head_rms_bwd(z_ref[:, c0:c0 + HD].astype(F32), qg, dq_ref[:, c0:c0 + HD])
            dz_ref[:, c0:c0 + HD] = dx.astype(BF16)
            aq = aq + sg
            dx, sg = _head_rms_bwd(z_ref[:, d + c0:d + c0 + HD].astype(F32), kg, dk_ref[:, c0:c0 + HD])
            dz_ref[:, d + c0:d + c0 + HD] = dx.astype(BF16)
            ak = ak + sg
        dz_ref[:, 2 * d:3 * d] = dv_ref[...]

        @pl.when(pl.program_id(0) == 0)
        def _():
            dqg_ref[...] = aq
            dkg_ref[...] = ak

        @pl.when(pl.program_id(0) > 0)
        def _():
            dqg_ref[...] += aq
            dkg_ref[...] += ak

    o = pl.BlockSpec((tm, d), lambda i: (i, 0))
    g = pl.BlockSpec((1, HD), lambda i: (0, 0))
    z3 = pl.BlockSpec((tm, 3 * d), lambda i: (i, 1))
    return _pcall(
        body, name="fox_post", out_shape=(_sds(dz.shape, dz.dtype), _sds((1, HD), F32), _sds((1, HD), F32)), grid=(t // tm,),
        in_specs=[z3, o, o, o, g, g, HBM_SPEC], out_specs=(z3, g, g),
        input_output_aliases={6: 0}, compiler_params=_cp(dimension_semantics=("arbitrary",)),
    )(z, dqn, dkn, dv, q_g, k_g, dz)


def _xa_prep_kv(kv, k_g, *, rows, d):
    xd = d // XH

    def body(kv_ref, g_ref, k_ref, v_ref):
        g = g_ref[...]
        for h in range(XH):
            c0 = h * xd
            k_ref[:, c0:c0 + xd] = _head_rms(kv_ref[:, c0:c0 + xd], g).astype(BF16)
        v_ref[...] = kv_ref[:, d:2 * d].astype(BF16)

    return _pcall(body, name="xa_prep_kv", out_shape=(_sds((rows, d), BF16),) * 2, compiler_params=_cp())(kv, k_g)


def _xa_fwd(z, kn, vb, q_g, *, nb, s, d, nm, q_block, tq=1024):
    xd = d // XH
    tq = min(tq, s)
    nq = s // tq

    def body(z_ref, k_ref, v_ref, g_ref, y_ref):
        g = g_ref[...]
        for h in range(XH):
            c0 = h * xd
            qn = _head_rms(z_ref[:, c0:c0 + xd].astype(F32), g).astype(BF16)
            sc = lax.dot_general(qn, k_ref[:, c0:c0 + xd], (((1,), (1,)), ((), ())), preferred_element_type=F32)
            sc = sc / math.sqrt(xd)
            p = jnp.exp(sc - jnp.max(sc, axis=-1, keepdims=True))
            p = p / jnp.sum(p, axis=-1, keepdims=True)
            y_ref[:, c0:c0 + xd] = jnp.dot(p.astype(BF16), v_ref[:, c0:c0 + xd], preferred_element_type=F32).astype(BF16)

    kvs = pl.BlockSpec((nm, d), lambda b, i: (b, 0))
    return _pcall(
        body, name="xa_fwd", out_shape=_sds((nb * s, d), BF16), grid=(nb, nq),
        in_specs=[pl.BlockSpec((tq, d), lambda b, i: (b * nq + i, q_block)), kvs, kvs, pl.BlockSpec((1, xd), lambda b, i: (0, 0))],
        out_specs=pl.BlockSpec((tq, d), lambda b, i: (b * nq + i, 0)), compiler_params=_cp(),
    )(z, kn, vb, q_g)


def _xa_bwd(z, kn, vb, q_g, dy, dz, *, nb, s, d, nm, q_block, tq=1024):
    xd = d // XH
    tq = min(tq, s)
    nq = s // tq

    def body(z_ref, k_ref, v_ref, g_ref, dy_ref, dz_in, dz_ref, dk_ref, dv_ref, dg_ref):
        b_i, i = pl.program_id(0), pl.program_id(1)
        g = g_ref[...]

        @pl.when(i == 0)
        def _():
            dk_ref[...] = jnp.zeros_like(dk_ref)
            dv_ref[...] = jnp.zeros_like(dv_ref)

        @pl.when((i == 0) & (b_i == 0))
        def _():
            dg_ref[...] = jnp.zeros_like(dg_ref)

        for h in range(XH):
            c0 = h * xd
            xq = z_ref[:, c0:c0 + xd].astype(F32)
            qn = _head_rms(xq, g).astype(BF16)
            kh, vh = k_ref[:, c0:c0 + xd], v_ref[:, c0:c0 + xd]
            sc = lax.dot_general(qn, kh, (((1,), (1,)), ((), ())), preferred_element_type=F32) / math.sqrt(xd)
            p = jnp.exp(sc - jnp.max(sc, axis=-1, keepdims=True))
            p = p / jnp.sum(p, axis=-1, keepdims=True)
            dob = dy_ref[:, c0:c0 + xd].astype(BF16)
            dv_ref[:, c0:c0 + xd] += lax.dot_general(p.astype(BF16), dob, (((0,), (0,)), ((), ())), preferred_element_type=F32)
            dp = lax.dot_general(dob, vh, (((1,), (1,)), ((), ())), preferred_element_type=F32)
            ds = p * (dp - jnp.sum(p * dp, axis=-1, keepdims=True)) / math.sqrt(xd)
            dsb = ds.astype(BF16)
            dqn = jnp.dot(dsb, kh, preferred_element_type=F32)
            dk_ref[:, c0:c0 + xd] += lax.dot_general(dsb, qn, (((0,), (0,)), ((), ())), preferred_element_type=F32)
            dx, sg = _head_rms_bwd(xq, g, dqn)
            dz_ref[:, c0:c0 + xd] = dx.astype(BF16)
            dg_ref[...] += sg

    kvs = pl.BlockSpec((nm, d), lambda b, i: (b, 0))
    qs = pl.BlockSpec((tq, d), lambda b, i: (b * nq + i, q_block))
    gs = pl.BlockSpec((1, xd), lambda b, i: (0, 0))
    return _pcall(
        body, name="xa_bwd",
        out_shape=(_sds(dz.shape, dz.dtype), _sds((nb * nm, d), F32), _sds((nb * nm, d), F32), _sds((1, xd), F32)), grid=(nb, nq),
        in_specs=[qs, kvs, kvs, gs, pl.BlockSpec((tq, d), lambda b, i: (b * nq + i, 0)), HBM_SPEC],
        out_specs=(qs, kvs, kvs, gs), input_output_aliases={5: 0},
        compiler_params=_cp(dimension_semantics=("arbitrary", "arbitrary")),
    )(z, kn, vb, q_g, dy, dz)


def _xa_post_kv(kv, dkn, dv, k_g, *, rows, d):
    xd = d // XH

    def body(kv_ref, dk_ref, dv_ref, g_ref, dkv_ref, dg_ref):
        g = g_ref[...]
        acc = jnp.zeros((1, xd), F32)
        for h in range(XH):
            c0 = h * xd
            dx, sg = _head_rms_bwd(kv_ref[:, c0:c0 + xd], g, dk_ref[:, c0:c0 + xd])
            dkv_ref[:, c0:c0 + xd] = dx.astype(BF16)
            acc = acc + sg
        dkv_ref[:, d:2 * d] = dv_ref[...].astype(BF16)
        dg_ref[...] = acc

    return _pcall(body, name="xa_post_kv", out_shape=(_sds((rows, 2 * d), BF16), _sds((1, xd), F32)), compiler_params=_cp())(
        kv, dkn, dv, k_g)


def _sigmoid(v):
    return 1.0 / (1.0 + jnp.exp(-v))


def _branches_fwd(z, ys, ws, *, t, d, gate_block, tm=1024, tn=512, comm=None):
    nj = d // tn
    c_ins, c_in_specs, c_out_shape, c_scratch = _comm_lists(comm)
    n_steps = (t // tm) * nj

    def body(*refs):
        _, ins, cin, outs, cout, _, csem = _split_refs(refs, 0, 9, len(c_ins), 2, len(c_ins), 0)
        ya_ref, yb_ref, yc_ref, wa_ref, wb_ref, wc_ref, ga_ref, gb_ref, gc_ref = ins
        m_ref, p_ref = outs
        step = pl.program_id(0) * nj + pl.program_id(1)
        _run_comm(comm, cin, cout, csem, step, n_steps, "start")
        acc = None
        for q, (y_ref, w_ref, g_ref) in enumerate(((ya_ref, wa_ref, ga_ref), (yb_ref, wb_ref, gb_ref), (yc_ref, wc_ref, gc_ref))):
            pr = jnp.dot(y_ref[...], w_ref[...], preferred_element_type=F32)
            p_ref[q] = pr.astype(BF16)
            term = _sigmoid(g_ref[...].astype(F32)) * pr
            acc = term if acc is None else acc + term
        m_ref[...] = acc.astype(BF16)
        _run_comm(comm, cin, cout, csem, step, n_steps, "end")

    y_spec = pl.BlockSpec((tm, d), lambda i, j: (i, 0))
    w_spec = pl.BlockSpec((d, tn), lambda i, j: (0, j))

    def gate(q):
        return pl.BlockSpec((tm, tn), lambda i, j: (i, (gate_block + q) * nj + j))

    outs = _pcall(
        body, name="branches_fwd", out_shape=tuple([_sds((t, d), BF16), _sds((3, t, d), BF16)] + c_out_shape), grid=(t // tm, nj),
        in_specs=[y_spec] * 3 + [w_spec] * 3 + [gate(0), gate(1), gate(2)] + c_in_specs,
        out_specs=tuple([pl.BlockSpec((tm, tn), lambda i, j: (i, j)), pl.BlockSpec((3, tm, tn), lambda i, j: (0, i, j))]
                        + [HBM_SPEC] * len(c_ins)),
        scratch_shapes=c_scratch, compiler_params=_cp(dimension_semantics=("arbitrary", "arbitrary")),
    )(*ys, *ws, z, z, z, *c_ins)
    return outs[0], outs[1], list(outs[2:])


def _gates_bwd(z, dm, proj, dz, *, t, d, gate_block, tm=1024):
    def body(g_ref, dm_ref, p_ref, dz_in, dz_ref, da_ref, db_ref, dc_ref):
        q = pl.program_id(1)
        sg = _sigmoid(g_ref[...].astype(F32))
        dmv = dm_ref[...]
        dz_ref[...] = (dmv * p_ref[0].astype(F32) * sg * (1.0 - sg)).astype(BF16)
        dp = (dmv * sg).astype(BF16)
        @pl.when(q == 0)
        def _():
            da_ref[...] = dp

        @pl.when(q == 1)
        def _():
            db_ref[...] = dp

        @pl.when(q == 2)
        def _():
            dc_ref[...] = dp

    gs = pl.BlockSpec((tm, d), lambda i, q: (i, gate_block + q))
    o = pl.BlockSpec((tm, d), lambda i, q: (i, 0))
    return _pcall(
        body, name="gates_bwd", out_shape=(_sds(dz.shape, dz.dtype),) + (_sds((t, d), BF16),) * 3, grid=(t // tm, 3),
        in_specs=[gs, o, pl.BlockSpec((1, tm, d), lambda i, q: (q, i, 0)), HBM_SPEC], out_specs=(gs, o, o, o),
        input_output_aliases={3: 0}, compiler_params=_cp(dimension_semantics=("arbitrary", "arbitrary")),
    )(z, dm, proj, dz)


def _ffn_in_fwd(h2, wfi_t, *, t, d, dff, tm=512, tn=1408):
    nj = dff // tn

    def body(h_ref, wg_ref, wu_ref, g_ref, u_ref, a_ref):
        hv = h_ref[...]
        dn = (((1,), (1,)), ((), ()))
        gv = lax.dot_general(hv, wg_ref[...], dn, preferred_element_type=F32)
        uv = lax.dot_general(hv, wu_ref[...], dn, preferred_element_type=F32)
        g_ref[...] = gv.astype(BF16)
        u_ref[...] = uv.astype(BF16)
        a_ref[...] = (gv * _sigmoid(gv) * uv).astype(BF16)

    o = pl.BlockSpec((tm, tn), lambda i, j: (i, j))
    return _pcall(
        body, name="ffn_in_fwd", out_shape=(_sds((t, dff), BF16),) * 3, grid=(t // tm, nj),
        in_specs=[pl.BlockSpec((tm, d), lambda i, j: (i, 0)), pl.BlockSpec((tn, d), lambda i, j: (j, 0)),
                  pl.BlockSpec((tn, d), lambda i, j: (nj + j, 0))],
        out_specs=(o, o, o), compiler_params=_cp(),
    )(h2, wfi_t, wfi_t)


def _ffn_out_bwd(dy, wfo, gate, up, *, t, d, dff, tm=512, tn=1408):
    def body(dy_ref, w_ref, g_ref, u_ref, dg_ref, du_ref):
        dav = lax.dot_general(dy_ref[...].astype(BF16), w_ref[...], (((1,), (1,)), ((), ())), preferred_element_type=F32)
        gv = g_ref[...].astype(F32)
        sg = _sigmoid(gv)
        dg_ref[...] = (dav * u_ref[...].astype(F32) * sg * (1.0 + gv * (1.0 - sg))).astype(BF16)
        du_ref[...] = (dav * gv * sg).astype(BF16)

    o = pl.BlockSpec((tm, tn), lambda i, j: (i, j))
    return _pcall(
        body, name="ffn_out_bwd", out_shape=(_sds((t, dff), BF16),) * 2, grid=(t // tm, dff // tn),
        in_specs=[pl.BlockSpec((tm, d), lambda i, j: (i, 0)), pl.BlockSpec((tn, d), lambda i, j: (j, 0)), o, o],
        out_specs=(o, o), compiler_params=_cp(),
    )(dy, wfo, gate, up)


def _ffn_out_loss(act, wfo, x1, target, *, t, d, dff, tm=512):
    def body(a_ref, w_ref, x_ref, t_ref, dy_ref, l_ref):
        yv = x_ref[...] + jnp.dot(a_ref[...], w_ref[...], preferred_element_type=F32)
        e = yv - t_ref[...]
        dy_ref[...] = e * (1.0 / d)

        @pl.when(pl.program_id(0) == 0)
        def _():
            l_ref[...] = jnp.zeros_like(l_ref)

        l_ref[...] += jnp.sum(jnp.sum(e * e, axis=-1, keepdims=True), axis=0, keepdims=True) * (0.5 / d)

    o = pl.BlockSpec((tm, d), lambda i: (i, 0))
    dy, l = _pcall(
        body, name="ffn_out_loss", out_shape=(_sds((t, d), F32), _sds((1, HD), F32)), grid=(t // tm,),
        in_specs=[pl.BlockSpec((tm, dff), lambda i: (i, 0)), pl.BlockSpec((dff, d), lambda i: (0, 0)), o, o],
        out_specs=(o, pl.BlockSpec((1, HD), lambda i: (0, 0))),
        compiler_params=_cp(dimension_semantics=("arbitrary",)),
    )(act, wfo, x1, target)
    return dy, l[0, 0]


def _adamw(w, g, m, v, *, name):
    if w.ndim == 3:
        rows, _, cols = w.shape
        tm = max(tt for tt in range(1, 65) if rows % tt == 0)
        block, imap, grid = (tm, 1, cols), (lambda i, j: (i, 0, 0)), (rows // tm, 1)
    else:
        rows, cols = w.shape
        tm, tn = rows, cols
        if rows > 512 and rows % 8 == 0:
            tm = max(tt for tt in range(8, 513, 8) if rows % tt == 0)
        elif rows > 512 and cols % 128 == 0:
            tn = 128
        block, imap, grid = (tm, tn), (lambda i, j: (i, j)), (rows // tm, cols // tn)
    c1 = 1.0 - ADAM_B1 ** ADAM_STEP
    c2 = 1.0 - ADAM_B2 ** ADAM_STEP

    at = (slice(None), 0, slice(None)) if w.ndim == 3 else Ellipsis

    def body(w_ref, g_ref, m_ref, v_ref, d_ref, nm_ref, nv_ref):
        gv = g_ref[at]
        mn = ADAM_B1 * m_ref[at] + (1.0 - ADAM_B1) * gv
        vn = ADAM_B2 * v_ref[at] + (1.0 - ADAM_B2) * (gv * gv)
        d_ref[at] = -ADAM_LR * ((mn / c1) / (jnp.sqrt(vn / c2) + ADAM_EPS) + ADAM_WD * w_ref[at])
        nm_ref[at] = mn
        nv_ref[at] = vn

    spec = pl.BlockSpec(block, imap)
    return _pcall(
        body, name=name, out_shape=(_sds(w.shape, F32),) * 3, grid=grid,
        in_specs=[spec] * 4, out_specs=(spec,) * 3, compiler_params=_cp(),
    )(w, g, m, v)


def _pad_rows(a, rows):
    return a if a.shape[0] == rows else jnp.pad(a, ((0, rows - a.shape[0]),) + ((0, 0),) * (a.ndim - 1))


class _Pack:
    def __init__(self, row_sizes):
        self.rows = list(row_sizes)
        self.padded = [-(-r // 16) * 16 for r in self.rows]
        self.offs = [sum(self.padded[:i]) for i in range(len(self.rows))]
        self.total = sum(self.padded)

    def stack(self, blocks):
        return jnp.concatenate([_pad_rows(b.astype(BF16), p) for b, p in zip(blocks, self.padded)], axis=0)

    def full(self, gathered, i):
        r = self.rows[i]
        return gathered[:, self.offs[i]:self.offs[i] + r, :].reshape(N_DEV * r, gathered.shape[2])

    def for_core(self, full_grads, core):
        parts = []
        for gfull, r, p in zip(full_grads, self.rows, self.padded):
            pieces = gfull if isinstance(gfull, (tuple, list)) else (gfull,)
            blk = jnp.concatenate(
                [lax.dynamic_index_in_dim(g.reshape(-1, 2, r, g.shape[1]), core, axis=1, keepdims=False) for g in pieces],
                axis=0).astype(BF16)
            if p != r:
                blk = jnp.pad(blk, ((0, 0), (0, p - r), (0, 0)))
            parts.append(blk)
        return jnp.concatenate(parts, axis=1)

    def shard(self, g_pack, i):
        return g_pack[self.offs[i]:self.offs[i] + self.rows[i], :]


def kernel(x, mem, norm1_g, w_in, conv_w, conv_b, fox_f_bias, fox_q_g, fox_k_g, mem_norm_g, w_mem_kv, xa_q_g, xa_k_g, w_br_conv, w_br_fox, w_br_xa, w_o, norm2_g, w_ffn_in, w_ffn_out, loss_target, m_norm1_g, m_w_in, m_conv_w, m_conv_b, m_fox_f_bias, m_fox_q_g, m_fox_k_g, m_mem_norm_g, m_w_mem_kv, m_xa_q_g, m_xa_k_g, m_w_br_conv, m_w_br_fox, m_w_br_xa, m_w_o, m_norm2_g, m_w_ffn_in, m_w_ffn_out, v_norm1_g, v_w_in, v_conv_w, v_conv_b, v_fox_f_bias, v_fox_q_g, v_fox_k_g, v_mem_norm_g, v_w_mem_kv, v_xa_q_g, v_xa_k_g, v_w_br_conv, v_w_br_fox, v_w_br_xa, v_w_o, v_norm2_g, v_w_ffn_in, v_w_ffn_out):
    nb, s, d = x.shape
    t = nb * s
    nm = mem.shape[1]
    nh = d // HD
    dff = w_ffn_out.shape[1] * N_DEV
    n_in = w_in.shape[2] * N_DEV
    n_in_pad = -(-n_in // 1152) * 1152
    ff_block = (10 * d) // HD
    my_c = lax.axis_index("c")
    my_q = 2 * lax.axis_index("x") + lax.axis_index("y")

    pack_a = _Pack([w_in.shape[2]])
    pack_b = _Pack([w_mem_kv.shape[2], w_br_conv.shape[1], w_br_fox.shape[1], w_br_xa.shape[1], w_o.shape[1],
                    w_ffn_in.shape[2], w_ffn_out.shape[1]])
    gather_2 = _Pack([w_mem_kv.shape[2], w_br_conv.shape[1], w_br_fox.shape[1], w_br_xa.shape[1], w_o.shape[1]])
    stack_a = pack_a.stack([w_in[0].T])
    stack_2 = gather_2.stack([w_mem_kv[0].T, w_br_conv[0], w_br_fox[0], w_br_xa[0], w_o[0]])
    stack_fi = w_ffn_in[0].T.astype(BF16)
    stack_fo = w_ffn_out[0].astype(BF16)
    cw_block = _pad_rows(conv_w[0], 8)
    rows_a = w_in.shape[2]
    my_x, my_y = lax.axis_index("x"), lax.axis_index("y")
    arrival = [(my_x, my_y, my_c), (my_x, my_y, 1 - my_c)]
    chip_1, chip_2, chip_d = _chip_order(my_x, my_y, my_c)
    arrival += [(*chip_1, my_c), (*chip_2, 1 - my_c), (*chip_2, my_c), (*chip_1, 1 - my_c), (*chip_d, my_c), (*chip_d, 1 - my_c)]
    order = jnp.stack([4 * ax + 2 * ay + ac for ax, ay, ac in arrival]).astype(jnp.int32)

    x2 = x.reshape(t, d)
    mem2 = mem.reshape(nb * nm, d)
    tgt2 = loss_target.reshape(t, d)
    h1 = _rms_fwd(x2, norm1_g, name="rms1_fwd")
    z, slabs, _ = _z_gathered(h1, stack_a, order, t=t, d=d, n_pad=N_DEV * SLAB_WIN, rows=rows_a)
    z = _z_fix(h1, slabs, z, t=t, d=d, rows=rows_a)
    ff_rows = _pad_rows(slabs[N_DEV - 1, rows_a - nh:rows_a, :], HD)
    zff = _mm(h1, ff_rows, m=t, n=HD, k=d, tb=True, tm=1024, name="mm_z_ff")
    qn, kn, vb = _fox_prep(z, fox_q_g, fox_k_g, t=t, d=d)
    f_bias_col = fox_f_bias.reshape(nh, 1)
    ck_rows = _fox_gates(zff, f_bias_col, nb=nb, s=s, nh=nh).reshape(nb * nh, 1, s)
    y_fox, lse, (gathered_2, cw_g, gathered_fo) = _fox_fwd(qn, kn, vb, ck_rows, nb=nb, s=s, nh=nh,
                                                           comm=_AllGather([stack_2, cw_block, stack_fo], forward_at=0.8))
    wkv_t, wbc, wbf, wbx, wo = (gather_2.full(gathered_2, i) for i in range(5))
    wfo = gathered_fo.reshape(dff, d)
    conv_w_full = jnp.transpose(cw_g[:, 0:3, :], (1, 0, 2)).reshape(3, d)

    y_conv = _conv_fwd(z, conv_w_full, conv_b, nb=nb, s=s, d=d)
    mem_n = _rms_fwd(mem2, mem_norm_g, name="rms_mem_fwd")
    kv = _mm(mem_n, wkv_t, m=nb * nm, n=2 * d, k=d, tb=True, name="mm_kv")
    xkn, xvb = _xa_prep_kv(kv, xa_k_g, rows=nb * nm, d=d)
    y_xa = _xa_fwd(z, xkn, xvb, xa_q_g, nb=nb, s=s, d=d, nm=nm, q_block=6)
    merged, proj, (gathered_fi,) = _branches_fwd(z, (y_conv, y_fox, y_xa), (wbc, wbf, wbx), t=t, d=d, gate_block=7,
                                                 comm=_AllGather([stack_fi]))
    wfi_t = gathered_fi.reshape(2 * dff, d)
    x1 = _mm(merged, wo, m=t, n=d, k=d, res=x2, tm=1024, name="mm_x1")
    h2 = _rms_fwd(x1, norm2_g, name="rms2_fwd")
    ff_gate, ff_up, act = _ffn_in_fwd(h2, wfi_t, t=t, d=d, dff=dff)
    dy, loss_local = _ffn_out_loss(act, wfo, x1, tgt2, t=t, d=d, dff=dff)

    g_wfo = _mm(act, dy, m=dff, n=d, k=t, ta=True, tm=1408, tk=2048, out_dtype=BF16, name="mm_g_wfo")
    dgate, dup = _ffn_out_bwd(dy, wfo, ff_gate, ff_up, t=t, d=d, dff=dff)
    dh2 = _mm(dgate, wfi_t, m=t, n=d, k=dff, tk=dff, name="mm_dh2_g")
    dh2 = _mm(dup, wfi_t, m=t, n=d, k=dff, tk=dff, b_off=(1, 0), res=dh2, name="mm_dh2_u")
    g_wfi_g = _mm(dgate, h2, m=dff, n=d, k=t, ta=True, tm=1408, tk=2048, out_dtype=BF16, name="mm_g_wfi_g")
    g_wfi_u = _mm(dup, h2, m=dff, n=d, k=t, ta=True, tm=1408, tk=2048, out_dtype=BF16, name="mm_g_wfi_u")
    dx1, g_norm2 = _rms_bwd(dh2, x1, norm2_g, dy, name="rms2_bwd")
    dmerged = _mm(dx1, wo, m=t, n=d, k=d, tb=True, tm=1024, name="mm_dmerged")
    g_wo = _mm(merged, dx1, m=d, n=d, k=t, ta=True, tk=2048, out_dtype=BF16, name="mm_g_wo")

    dz = lax.empty((t, n_in_pad), BF16)
    dz, dpa, dpb, dpc = _gates_bwd(z, dmerged, proj, dz, t=t, d=d, gate_block=7)
    dy_conv = _mm(dpa, wbc, m=t, n=d, k=d, tb=True, tm=1024, name="mm_dy_conv")
    g_wbc = _mm(y_conv, dpa, m=d, n=d, k=t, ta=True, tk=4096, out_dtype=BF16, name="mm_g_wbc")
    dy_fox = _mm(dpb, wbf, m=t, n=d, k=d, tb=True, tm=1024, name="mm_dy_fox")
    g_wbf = _mm(y_fox, dpb, m=d, n=d, k=t, ta=True, tk=4096, out_dtype=BF16, name="mm_g_wbf")
    dy_xa = _mm(dpc, wbx, m=t, n=d, k=d, tb=True, tm=1024, name="mm_dy_xa")
    g_wbx = _mm(y_xa, dpc, m=d, n=d, k=t, ta=True, tk=4096, out_dtype=BF16, name="mm_g_wbx")

    dz, dxkn, dxv, g_xa_q = _xa_bwd(z, xkn, xvb, xa_q_g, dy_xa, dz, nb=nb, s=s, d=d, nm=nm, q_block=6)
    dkv, g_xa_k = _xa_post_kv(kv, dxkn, dxv, xa_k_g, rows=nb * nm, d=d)
    g_wkv_t = _mm(dkv, mem_n, m=2 * d, n=d, k=nb * nm, ta=True, out_dtype=BF16, name="mm_g_wkv")
    dmem_n = _mm(dkv, wkv_t, m=nb * nm, n=d, k=2 * d, tk=2 * d, name="mm_dmem")
    _, g_mem_norm = _rms_bwd(dmem_n, mem2, mem_norm_g, None, name="rms_mem_bwd")

    grads_b = [g_wkv_t, g_wbc, g_wbf, g_wbx, g_wo, (g_wfi_g, g_wfi_u), g_wfo]
    keep_b, send_b = pack_b.for_core(grads_b, my_c), pack_b.for_core(grads_b, 1 - my_c)
    dz, g_conv_w, g_conv_b, (from_sibling_b,) = _conv_bwd(z, dy_conv, conv_w_full, conv_b, dz, nb=nb, s=s, d=d,
                                                          comm=_SiblingSwap(send_b))
    part_b = _add2(keep_b, from_sibling_b, name="rs_add_sibling_b")
    dqn, ds_rows, dkn, dvb, ds_cols, (from_chips_b,) = _fox_bwd(qn, kn, vb, ck_rows, dy_fox, y_fox, lse, nb=nb, s=s, nh=nh,
                                                                comm=_ChipExchange(part_b))
    g_pack_b = _add4(lax.dynamic_index_in_dim(part_b, my_q, axis=0, keepdims=False), from_chips_b, name="rs_add_chips_b")
    dz, g_fox_q, g_fox_k = _fox_post(z, dqn, dkn, dvb, fox_q_g, fox_k_g, dz, t=t, d=d)
    dz, g_f_bias = _fox_gates_bwd(zff, f_bias_col, ds_rows, ds_cols, dz, nb=nb, s=s, nh=nh, ff_block=ff_block)

    keep_a, from_sibling_a = _g_win_slabs(dz, h1, my_c, t=t, d=d, n_slab=N_DEV, rows=rows_a, rp=slabs.shape[1])
    part_a = _add2(keep_a, from_sibling_a, name="rs_add_sibling_a")
    dh1, (from_chips_a,) = _dh1_from_slabs(dz, slabs, t=t, d=d, rows=rows_a, comm=_ChipExchange(part_a))
    g_pack_a = _add4(lax.dynamic_index_in_dim(part_a, my_q, axis=0, keepdims=False), from_chips_a, name="rs_add_chips_a",
                     rows_apart=True)
    dx, g_norm1 = _rms_bwd(dh1, x2, norm1_g, dx1, name="rms1_bwd")

    xd = d // XH
    tail = jnp.concatenate(
        [_pad_rows(g_f_bias.reshape(nh), HD).reshape(1, HD), g_fox_q, g_fox_k, g_xa_q, g_xa_k,
         jnp.zeros((1, d - 3 * HD - 2 * xd), F32)], axis=1)
    small = jnp.concatenate([g_norm1, g_norm2, g_conv_w, g_conv_b, g_mem_norm, tail], axis=0)
    small = _all_reduce_small(small)
    gs_conv_w = lax.dynamic_slice_in_dim(small[2:5], (2 * my_q + my_c) * (d // N_DEV), d // N_DEV, axis=1)

    grads = {
        "norm1_g": small[0:1], "w_in": g_pack_a, "conv_w": gs_conv_w, "conv_b": small[5:6],
        "fox_f_bias": small[7:8, 0:nh], "fox_q_g": small[7:8, HD:2 * HD], "fox_k_g": small[7:8, 2 * HD:3 * HD],
        "mem_norm_g": small[6:7], "w_mem_kv": pack_b.shard(g_pack_b, 0), "xa_q_g": small[7:8, 3 * HD:3 * HD + xd],
        "xa_k_g": small[7:8, 3 * HD + xd:3 * HD + 2 * xd], "w_br_conv": pack_b.shard(g_pack_b, 1),
        "w_br_fox": pack_b.shard(g_pack_b, 2), "w_br_xa": pack_b.shard(g_pack_b, 3), "w_o": pack_b.shard(g_pack_b, 4),
        "norm2_g": small[1:2], "w_ffn_in": pack_b.shard(g_pack_b, 5), "w_ffn_out": pack_b.shard(g_pack_b, 6),
    }
    transposed = {"w_in", "w_mem_kv", "w_ffn_in"}
    weights = {
        "norm1_g": (norm1_g, m_norm1_g, v_norm1_g), "w_in": (w_in, m_w_in, v_w_in), "conv_w": (conv_w, m_conv_w, v_conv_w),
        "conv_b": (conv_b, m_conv_b, v_conv_b), "fox_f_bias": (fox_f_bias, m_fox_f_bias, v_fox_f_bias),
        "fox_q_g": (fox_q_g, m_fox_q_g, v_fox_q_g), "fox_k_g": (fox_k_g, m_fox_k_g, v_fox_k_g),
        "mem_norm_g": (mem_norm_g, m_mem_norm_g, v_mem_norm_g), "w_mem_kv": (w_mem_kv, m_w_mem_kv, v_w_mem_kv),
        "xa_q_g": (xa_q_g, m_xa_q_g, v_xa_q_g), "xa_k_g": (xa_k_g, m_xa_k_g, v_xa_k_g),
        "w_br_conv": (w_br_conv, m_w_br_conv, v_w_br_conv), "w_br_fox": (w_br_fox, m_w_br_fox, v_w_br_fox),
        "w_br_xa": (w_br_xa, m_w_br_xa, v_w_br_xa), "w_o": (w_o, m_w_o, v_w_o), "norm2_g": (norm2_g, m_norm2_g, v_norm2_g),
        "w_ffn_in": (w_ffn_in, m_w_ffn_in, v_w_ffn_in), "w_ffn_out": (w_ffn_out, m_w_ffn_out, v_w_ffn_out),
    }
    out_g, out_d, out_m, out_v = [], [], [], []
    for name, (w, m, v) in weights.items():
        shape = w.shape
        if name == "w_in":
            w3, m3, v3 = (a.transpose(2, 0, 1) for a in (w, m, v))
            dl, mn, vn = _adamw(w3, g_pack_a, m3, v3, name="adamw_" + name)
            out_g.append(g_pack_a[:shape[2]].transpose(1, 2, 0))
            out_d.append(dl.transpose(1, 2, 0))
            out_m.append(mn.transpose(1, 2, 0))
            out_v.append(vn.transpose(1, 2, 0))
            continue
        tr = name in transposed

        def as2d(a):
            a = a.reshape(shape[-2], shape[-1])
            return a.T if tr else a

        def back(a):
            return (a.T if tr else a).reshape(shape)

        w2 = as2d(w)
        g2 = grads[name].reshape(w2.shape)
        dl, mn, vn = _adamw(w2, g2, as2d(m), as2d(v), name="adamw_" + name)
        out_g.append(back(g2))
        out_d.append(back(dl))
        out_m.append(back(mn))
        out_v.append(back(vn))

    loss = lax.psum(loss_local, ("x", "y", "c"))
    return (loss, dx.reshape(nb, s, d), *out_g, *out_d, *out_m, *out_v)
```

```python
import math

import jax
import jax.numpy as jnp
from jax import lax
from jax.experimental import pallas as pl
from jax.experimental.pallas import tpu as pltpu

F32, BF16 = jnp.float32, jnp.bfloat16
EPS = 1e-6
HD = 128
XH = 4
N_DEV = 8
MESH = pl.DeviceIdType.MESH
VMEM_LIMIT = 52 * 1024 * 1024
NEG = -1e30
SQRT_HD = math.sqrt(HD)
EXP2_C = math.log2(math.e) / SQRT_HD
LOG2E = math.log2(math.e)
FOX_HP = 4

ADAM_LR, ADAM_B1, ADAM_B2, ADAM_EPS, ADAM_WD, ADAM_STEP = 0.001, 0.9, 0.999, 1e-08, 0.01, 10


def _cp(**kw):
    return pltpu.CompilerParams(vmem_limit_bytes=VMEM_LIMIT, **kw)


def _pcall(body, **kw):
    return pl.pallas_call(body, **kw)


def _sds(shape, dtype):
    return jax.ShapeDtypeStruct(shape, dtype)


HBM_SPEC = pl.BlockSpec(memory_space=pl.ANY)


def _mesh_pos():
    return lax.axis_index("x"), lax.axis_index("y"), lax.axis_index("c")


class _AllGather:
    def __init__(self, blocks, forward_at=None):
        self.blocks = list(blocks)
        self.forward_at = forward_at
        n = len(self.blocks)
        self.out_shape = [_sds((N_DEV,) + b.shape, b.dtype) for b in self.blocks]
        self.scratch = [pltpu.SemaphoreType.DMA((n, 7)), pltpu.SemaphoreType.DMA((n, 7)), pltpu.SemaphoreType.DMA((n,))]

    def bind(self, ins, outs, sems):
        n = len(ins)
        send_sems, recv_sems, local_sems = sems
        x, y, c = _mesh_pos()
        me, sibling = (x, y, c), (x, y, 1 - c)
        chips = [(1 - x, y), (x, 1 - y), (1 - x, 1 - y)]

        def slab(t, dev):
            return outs[t].at[4 * dev[0] + 2 * dev[1] + dev[2]]

        def copy(t, k, block, to, src=None):
            dst = slab(t, block)
            return pltpu.make_async_remote_copy(
                src_ref=dst if src is None else src, dst_ref=dst, send_sem=send_sems.at[t, k], recv_sem=recv_sems.at[t, k],
                device_id=to, device_id_type=MESH)

        mine = [pltpu.make_async_copy(ins[t], slab(t, me), local_sems.at[t]) for t in range(n)]
        first = []
        for t in range(n):
            first.append(copy(t, 0, me, sibling, src=ins[t]))
            first += [copy(t, 1 + j, me, (*chip, c), src=ins[t]) for j, chip in enumerate(chips)]
        passed = [copy(t, 4 + j, (*chip, c), sibling) for j, chip in enumerate(chips) for t in range(n)]

        def start():
            for cp in mine + first:
                cp.start()

        def mid():
            for j, chip in enumerate(chips):
                for t in range(n):
                    copy(t, 1 + j, (*chip, c), me).wait_recv()
                    passed[j * n + t].start()

        def finish():
            for t in range(n):
                copy(t, 0, sibling, me).wait_recv()
                for j, chip in enumerate(chips):
                    copy(t, 4 + j, (*chip, 1 - c), me).wait_recv()
            for cp in first + passed:
                cp.wait_send()
            for cp in mine:
                cp.wait()

        return start, mid, finish


class _SiblingSwap:
    def __init__(self, send):
        self.blocks = [send]
        self.out_shape = [_sds(send.shape, send.dtype)]
        self.scratch = [pltpu.SemaphoreType.DMA, pltpu.SemaphoreType.DMA]

    def bind(self, ins, outs, sems):
        x, y, c = _mesh_pos()
        cp = pltpu.make_async_remote_copy(src_ref=ins[0], dst_ref=outs[0], send_sem=sems[0], recv_sem=sems[1],
                                          device_id=(x, y, 1 - c), device_id_type=MESH)
        return cp.start, (lambda: None), cp.wait


class _ChipExchange:
    def __init__(self, part):
        self.blocks = [part]
        self.out_shape = [_sds((3,) + part.shape[1:], part.dtype)]
        self.scratch = [pltpu.SemaphoreType.DMA((3,)), pltpu.SemaphoreType.DMA((3,))]

    def bind(self, ins, outs, sems):
        x, y, c = _mesh_pos()
        chips = [(1 - x, y), (x, 1 - y), (1 - x, 1 - y)]
        cps = [
            pltpu.make_async_remote_copy(src_ref=ins[0].at[2 * cx + cy], dst_ref=outs[0].at[j], send_sem=sems[0].at[j],
                                         recv_sem=sems[1].at[j], device_id=(cx, cy, c), device_id_type=MESH)
            for j, (cx, cy) in enumerate(chips)
        ]

        def start():
            for cp in cps:
                cp.start()

        def finish():
            for cp in cps:
                cp.wait()

        return start, (lambda: None), finish


def _comm_lists(comm):
    if comm is None:
        return [], [], [], []
    n = len(comm.blocks)
    return list(comm.blocks), [HBM_SPEC] * n, list(comm.out_shape), list(comm.scratch)


def _split_refs(refs, n_pre, n_in, n_cin, n_out, n_cout, n_scr):
    cuts = [n_pre, n_in, n_cin, n_out, n_cout, n_scr]
    out, at = [], 0
    for c in cuts:
        out.append(refs[at:at + c])
        at += c
    out.append(refs[at:])
    return out


def _run_comm(comm, cin, cout, csem, step, n_steps, when="start"):
    if comm is None:
        return
    start, mid, finish = comm.bind(cin, cout, csem)
    if when == "start":
        pl.when(step == 0)(start)
    else:
        frac = getattr(comm, "forward_at", None)
        mid_step = max(n_steps - 2, 0) if frac is None else min(int(frac * n_steps), max(n_steps - 2, 0))
        pl.when(step == mid_step)(mid)
        pl.when(step == n_steps - 1)(finish)


def _all_reduce_small(v):
    rows, cols = v.shape

    def body(v_ref, o_ref, slots, send_sems, recv_sems):
        x, y, c = _mesh_pos()
        me = 4 * x + 2 * y + c
        slots[me] = v_ref[...]
        cps = []
        for k in range(1, N_DEV):
            px, py, pc = x ^ (k >> 2), y ^ ((k >> 1) & 1), c ^ (k & 1)
            cps.append(pltpu.make_async_remote_copy(
                src_ref=v_ref, dst_ref=slots.at[me], send_sem=send_sems.at[k - 1], recv_sem=recv_sems.at[k - 1],
                device_id=(px, py, pc), device_id_type=MESH))
        for cp in cps:
            cp.start()
        for cp in cps:
            cp.wait()
        acc = slots[0]
        for k in range(1, N_DEV):
            acc = acc + slots[k]
        o_ref[...] = acc

    return _pcall(
        body, name="all_reduce_small", out_shape=_sds((rows, cols), F32),
        in_specs=[pl.BlockSpec(memory_space=pltpu.VMEM)], out_specs=pl.BlockSpec(memory_space=pltpu.VMEM),
        scratch_shapes=[pltpu.VMEM((N_DEV, rows, cols), F32), pltpu.SemaphoreType.DMA((7,)), pltpu.SemaphoreType.DMA((7,))],
        compiler_params=_cp(),
    )(v)


def _row_tile(rows, cap=1040):
    return max(tt for tt in range(16, cap + 1, 16) if rows % tt == 0)


def _add2(a, b, *, name):
    nq, rows, cols = a.shape
    tm = _row_tile(rows)

    def body(a_ref, b_ref, o_ref):
        o_ref[...] = (a_ref[...].astype(F32) + b_ref[...].astype(F32)).astype(BF16)

    spec = pl.BlockSpec((1, tm, cols), lambda q, i: (q, i, 0))
    return _pcall(body, name=name, out_shape=_sds(a.shape, BF16), grid=(nq, rows // tm),
                  in_specs=[spec, spec], out_specs=spec, compiler_params=_cp())(a, b)


def _add4(own, recv, *, name, rows_apart=False):
    rows, cols = own.shape
    tm = _row_tile(rows, 256 if rows_apart else 1040)

    def body(o_ref, r_ref, g_ref):
        tot = ((o_ref[...].astype(F32) + r_ref[0].astype(F32)) + r_ref[1].astype(F32)) + r_ref[2].astype(F32)
        if rows_apart:
            g_ref[:, 0, :] = tot
        else:
            g_ref[...] = tot

    if rows_apart:
        out_shape, out_spec = _sds((rows, 1, cols), F32), pl.BlockSpec((tm, 1, cols), lambda i: (i, 0, 0))
    else:
        out_shape, out_spec = _sds((rows, cols), F32), pl.BlockSpec((tm, cols), lambda i: (i, 0))
    return _pcall(
        body, name=name, out_shape=out_shape, grid=(rows // tm,),
        in_specs=[pl.BlockSpec((tm, cols), lambda i: (i, 0)), pl.BlockSpec((3, tm, cols), lambda i: (0, i, 0))],
        out_specs=out_spec, compiler_params=_cp(),
    )(own, recv)


def _mm(a, b, *, m, n, k, name, ta=False, tb=False, b_off=(0, 0), res=None, out_dtype=F32, tm=512, tn=1024, tk=1024,
        comm=None):
    tm, tn, tk = min(tm, m), min(tn, n), min(tk, k)
    assert m % tm == 0 and n % tn == 0 and k % tk == 0, (name, m, n, k, tm, tn, tk)
    nk = k // tk
    grid = (m // tm, n // tn, nk)
    bo0, bo1 = b_off
    if ta:
        a_spec = pl.BlockSpec((tk, tm), lambda i, j, kk: (kk, i))
    else:
        a_spec = pl.BlockSpec((tm, tk), lambda i, j, kk: (i, kk))
    if tb:
        b_spec = pl.BlockSpec((tn, tk), lambda i, j, kk: (j + bo0, kk + bo1))
    else:
        b_spec = pl.BlockSpec((tk, tn), lambda i, j, kk: (kk + bo0, j + bo1))
    o_spec = pl.BlockSpec((tm, tn), lambda i, j, kk: (i, j))
    dn = (((0 if ta else 1,), (1 if tb else 0,)), ((), ()))
    has_res = res is not None
    c_ins, c_in_specs, c_out_shape, c_scratch = _comm_lists(comm)
    n_in = 3 if has_res else 2
    n_scr = 1 if nk > 1 else 0

    def body(*refs):
        _, ins, cin, outs, cout, scr, csem = _split_refs(refs, 0, n_in, len(c_ins), 1, len(c_ins), n_scr)
        a_ref, b_ref = ins[0], ins[1]
        r_ref = ins[2] if has_res else None
        o_ref = outs[0]
        step = (pl.program_id(0) * grid[1] + pl.program_id(1)) * nk + pl.program_id(2)
        _run_comm(comm, cin, cout, csem, step, grid[0] * grid[1] * nk, "start")
        p = lax.dot_general(a_ref[...].astype(BF16), b_ref[...].astype(BF16), dn, preferred_element_type=F32)

        def finish(acc):
            if has_res:
                acc = acc + r_ref[...]
            o_ref[...] = acc.astype(out_dtype)

        if nk == 1:
            finish(p)
        else:
            acc_ref = scr[0]
            kk = pl.program_id(2)

            @pl.when(kk == 0)
            def _():
                acc_ref[...] = p

            @pl.when(kk > 0)
            def _():
                acc_ref[...] += p

            @pl.when(kk == nk - 1)
            def _():
                finish(acc_ref[...])

        _run_comm(comm, cin, cout, csem, step, grid[0] * grid[1] * nk, "end")

    ins = [a, b] + ([res] if has_res else [])
    in_specs = [a_spec, b_spec] + ([o_spec] if has_res else [])
    sem = ("arbitrary",) * 3 if comm is not None else ("parallel", "parallel", "arbitrary")
    outs = _pcall(
        body, name=name, out_shape=tuple([_sds((m, n), out_dtype)] + c_out_shape), grid=grid,
        in_specs=in_specs + c_in_specs, out_specs=tuple([o_spec] + [HBM_SPEC] * len(c_ins)),
        scratch_shapes=([pltpu.VMEM((tm, tn), F32)] if nk > 1 else []) + c_scratch,
        compiler_params=_cp(dimension_semantics=sem),
    )(*(ins + c_ins))
    return outs[0] if comm is None else (outs[0], list(outs[1:]))


SLAB_WIN = 1280
SLAB_TAIL = 48


def _chip_order(x, y, c):
    north = c == 1
    first = (jnp.where(north, 1 - x, x), jnp.where(north, y, 1 - y))
    second = (jnp.where(north, x, 1 - x), jnp.where(north, 1 - y, y))
    return [first, second, (1 - x, 1 - y)]


def _z_gathered(h1, own_slab, order, *, t, d, n_pad, rows, comm=None, tm=1024):
    rp = own_slab.shape[0]
    n_slab = N_DEV
    tm = min(tm, t)
    ni = t // tm
    sh = rows - SLAB_WIN
    assert sh >= 0 and (n_slab - 1) * sh <= 128
    c_ins, c_in_specs, c_out_shape, c_scratch = _comm_lists(comm)
    n_steps = n_slab * ni

    def body(*refs):
        pre, ins, cin, outs, cout, scr, csem = _split_refs(refs, 1, 2, len(c_ins), 2, len(c_ins), 5)
        order_ref = pre[0]
        h_ref, own_ref = ins
        o_ref, slabs_ref = outs
        slab_vm, send_sems, recv_sems, local_sem, load_sem = scr
        p, i = pl.program_id(0), pl.program_id(1)
        step = p * ni + i
        x, y, c = _mesh_pos()
        me, sibling = (x, y, c), (x, y, 1 - c)
        chips = _chip_order(x, y, c)

        def slab(dev):
            return slabs_ref.at[4 * dev[0] + 2 * dev[1] + dev[2]]

        def copy(k, block, to, src=None, part=None):
            dst = slab(block) if part is None else slab(block).at[part]
            return pltpu.make_async_remote_copy(
                src_ref=dst if src is None else src, dst_ref=dst, send_sem=send_sems.at[k], recv_sem=recv_sems.at[k],
                device_id=to, device_id_type=MESH)

        nb_1, nb_2, diag = ((*chip, c) for chip in chips)
        cut = (rp // 32) * 16
        half_a, half_b = pl.ds(0, cut), pl.ds(cut, rp - cut)
        mine = pltpu.make_async_copy(own_ref, slab(me), local_sem)
        first = [copy(0, me, sibling, src=own_ref), copy(1, me, nb_1, src=own_ref), copy(2, me, nb_2, src=own_ref)]
        relays = [copy(3, nb_1, nb_2, part=half_a), copy(4, nb_2, nb_1, part=half_b)]
        axis_of = [jnp.where(c == 1, 0, 1), jnp.where(c == 1, 1, 0), 2]
        passed = [copy(5 + axis_of[j], (*chip, c), sibling) for j, chip in enumerate(chips)]
        later = min(n_steps // 5, n_steps - 1)

        @pl.when(step == 0)
        def _():
            mine.start()
            first[0].start()
            first[1].start()

        @pl.when(step == later)
        def _():
            first[2].start()

        _run_comm(comm, cin, cout, csem, step, n_steps, "start")

        def load(src):
            cp = pltpu.make_async_copy(src, slab_vm, load_sem)
            cp.start()
            cp.wait()

        @pl.when(i == 0)
        def _():
            @pl.when(p == 0)
            def _():
                load(own_ref)

            @pl.when(p == 1)
            def _():
                copy(0, sibling, me).wait_recv()
                load(slab(sibling))

            for j, chip in enumerate(chips):
                @pl.when(p == 2 + 2 * j)
                def _():
                    if j < 2:
                        copy(1 + j, (*chip, c), me).wait_recv()
                        relays[j].start()
                    else:
                        copy(3, diag, me, part=half_a).wait_recv()
                        copy(4, diag, me, part=half_b).wait_recv()
                    passed[j].start()
                    load(slab((*chip, c)))

                other = (1, 0, 2)[j]

                @pl.when(p == 3 + 2 * j)
                def _():
                    copy(5 + axis_of[other], (*chips[other], 1 - c), me).wait_recv()
                    load(slab((*chips[other], 1 - c)))

        k = order_ref[p]
        y_k = lax.dot_general(h_ref[...], slab_vm[...], (((1,), (1,)), ((), ())), preferred_element_type=F32)
        o_ref[...] = pltpu.roll(y_k[:, 0:SLAB_WIN], k * sh, axis=1).astype(BF16)

        @pl.when(step == n_steps - 1)
        def _():
            for cp in first + relays + passed:
                cp.wait_send()
            mine.wait()

        _run_comm(comm, cin, cout, csem, step, n_steps, "end")

    outs = _pcall(
        body, name="mm_z", out_shape=tuple([_sds((t, n_pad), BF16), _sds((n_slab, rp, d), BF16)] + c_out_shape),
        grid_spec=pltpu.PrefetchScalarGridSpec(
            num_scalar_prefetch=1, grid=(n_slab, ni),
            in_specs=[pl.BlockSpec((tm, d), lambda p, i, order: (i, 0)), HBM_SPEC] + c_in_specs,
            out_specs=tuple([pl.BlockSpec((tm, SLAB_WIN), lambda p, i, order: (i, order[p])), HBM_SPEC] + [HBM_SPEC] * len(c_ins)),
            scratch_shapes=[pltpu.VMEM((rp, d), BF16), pltpu.SemaphoreType.DMA((8,)), pltpu.SemaphoreType.DMA((8,)),
                            pltpu.SemaphoreType.DMA, pltpu.SemaphoreType.DMA] + c_scratch),
        compiler_params=_cp(dimension_semantics=("arbitrary", "arbitrary")),
    )(order, h1, own_slab, *c_ins)
    return outs[0], outs[1], list(outs[2:])


def _z_fix(h1, slabs, z, *, t, d, rows, tm=4096):
    n_slab, rp, _ = slabs.shape
    tm = min(tm, t)
    sh = rows - SLAB_WIN
    tail_lo = rp - SLAB_TAIL
    assert rows - (n_slab - 1) * sh >= tail_lo and rp % SLAB_TAIL == 0

    def body(h_ref, t_ref, z_ref, o_ref):
        k = pl.program_id(0) + 1
        p = lax.dot_general(h_ref[...], t_ref[0], (((1,), (1,)), ((), ())), preferred_element_type=F32)
        pp = jnp.concatenate([p, jnp.zeros((tm, 128 - SLAB_TAIL), F32)], axis=1)
        pr = pltpu.roll(pp, 128 - (rows - tail_lo) + k * sh, axis=1)
        lane = lax.broadcasted_iota(jnp.int32, (tm, 128), 1)
        o_ref[...] = jnp.where(lane < k * sh, pr.astype(BF16), z_ref[...])

    blk = pl.BlockSpec((tm, 128), lambda b, i: (i, (SLAB_WIN // 128) * (b + 1)))
    return _pcall(
        body, name="mm_z_fix", out_shape=_sds(z.shape, z.dtype), grid=(n_slab - 1, t // tm),
        in_specs=[pl.BlockSpec((tm, d), lambda b, i: (i, 0)),
                  pl.BlockSpec((1, SLAB_TAIL, d), lambda b, i: (b, tail_lo // SLAB_TAIL, 0)), blk],
        out_specs=blk, input_output_aliases={2: 0}, compiler_params=_cp(),
    )(h1, slabs, z)


def _slab_window(a1_ref, a2_ref, k, sh):
    w = jnp.concatenate([a1_ref[...], a2_ref[...]], axis=1)
    width = SLAB_WIN + 128
    return pltpu.roll(w, width - k * sh, axis=1)


def _g_win_slabs(dz, h1, my_c, *, t, d, n_slab, rows, rp, tk=2048):
    sh = rows - SLAB_WIN
    tk = min(tk, t)
    nk = t // tk
    width = SLAB_WIN + 128
    half = n_slab // 2

    def slab_of(p, c):
        return 2 * (p % half) + jnp.where(p < half, 1 - c, c)

    def body(c_ref, a1_ref, a2_ref, h_ref, keep_ref, land_ref, acc, stage, send_sems, recv_sems):
        p, kk = pl.program_id(0), pl.program_id(1)
        x, y, c = _mesh_pos()
        ws = _slab_window(a1_ref, a2_ref, slab_of(p, c), sh)
        prod = lax.dot_general(ws, h_ref[...], (((0,), (0,)), ((), ())), preferred_element_type=F32)

        def push(q):
            return pltpu.make_async_remote_copy(
                src_ref=stage.at[q % 2], dst_ref=land_ref.at[q], send_sem=send_sems.at[q], recv_sem=recv_sems.at[q],
                device_id=(x, y, 1 - c), device_id_type=MESH)

        @pl.when(kk == 0)
        def _():
            acc[...] = prod

        @pl.when(kk > 0)
        def _():
            acc[...] += prod

        @pl.when(kk == nk - 1)
        def _():
            res = acc[0:rp, :].astype(BF16)
            for q in range(half):
                @pl.when(p == q)
                def _():
                    if q >= 2:
                        push(q - 2).wait_send()
                    stage[q % 2] = res
                    push(q).start()

            @pl.when(p >= half)
            def _():
                keep_ref[0] = res

        @pl.when((p == n_slab - 1) & (kk == nk - 1))
        def _():
            for q in range(max(half - 2, 0), half):
                push(q).wait_send()
            for q in range(half):
                push(q).wait_recv()

    outs = _pcall(
        body, name="mm_g_win", out_shape=(_sds((half, rp, d), BF16), _sds((half, rp, d), BF16)),
        grid_spec=pltpu.PrefetchScalarGridSpec(
            num_scalar_prefetch=1, grid=(n_slab, nk),
            in_specs=[pl.BlockSpec((tk, SLAB_WIN), lambda p, kk, c: (kk, slab_of(p, c[0]))),
                      pl.BlockSpec((tk, 128), lambda p, kk, c: (kk, (SLAB_WIN // 128) * (slab_of(p, c[0]) + 1))),
                      pl.BlockSpec((tk, d), lambda p, kk, c: (kk, 0))],
            out_specs=(pl.BlockSpec((1, rp, d), lambda p, kk, c: (jnp.maximum(p - half, 0), 0, 0)), HBM_SPEC),
            scratch_shapes=[pltpu.VMEM((width, d), F32), pltpu.VMEM((2, rp, d), BF16),
                            pltpu.SemaphoreType.DMA((half,)), pltpu.SemaphoreType.DMA((half,))]),
        compiler_params=_cp(dimension_semantics=("arbitrary", "arbitrary")),
    )(my_c.reshape(1).astype(jnp.int32), dz, dz, h1)
    return outs[0], outs[1]


def _dh1_from_slabs(dz, slabs, *, t, d, rows, comm=None, tm=1024):
    n_slab, rp, _ = slabs.shape
    sh = rows - SLAB_WIN
    width = SLAB_WIN + 128
    c_ins, c_in_specs, c_out_shape, c_scratch = _comm_lists(comm)
    grid = (t // tm, n_slab)

    def body(*refs):
        _, ins, cin, outs, cout, scr, csem = _split_refs(refs, 0, 3, len(c_ins), 1, len(c_ins), 1)
        a1_ref, a2_ref, s_ref = ins
        o_ref, acc = outs[0], scr[0]
        k = pl.program_id(1)
        step = pl.program_id(0) * n_slab + k
        _run_comm(comm, cin, cout, csem, step, grid[0] * n_slab, "start")
        ws = _slab_window(a1_ref, a2_ref, k, sh)
        s_end = jnp.concatenate([s_ref[0, SLAB_WIN:rp, :], jnp.zeros((width - rp, d), BF16)], axis=0)
        p = (jnp.dot(ws[:, 0:SLAB_WIN], s_ref[0, 0:SLAB_WIN, :], preferred_element_type=F32)
             + jnp.dot(ws[:, SLAB_WIN:width], s_end, preferred_element_type=F32))

        @pl.when(k == 0)
        def _():
            acc[...] = p

        @pl.when(k > 0)
        def _():
            acc[...] += p

        @pl.when(k == n_slab - 1)
        def _():
            o_ref[...] = acc[...]

        _run_comm(comm, cin, cout, csem, step, grid[0] * n_slab, "end")

    outs = _pcall(
        body, name="mm_dh1", out_shape=tuple([_sds((t, d), F32)] + c_out_shape), grid=grid,
        in_specs=[pl.BlockSpec((tm, SLAB_WIN), lambda i, k: (i, k)),
                  pl.BlockSpec((tm, 128), lambda i, k: (i, (SLAB_WIN // 128) * (k + 1))),
                  pl.BlockSpec((1, rp, d), lambda i, k: (k, 0, 0))] + c_in_specs,
        out_specs=tuple([pl.BlockSpec((tm, d), lambda i, k: (i, 0))] + [HBM_SPEC] * len(c_ins)),
        scratch_shapes=[pltpu.VMEM((tm, d), F32)] + c_scratch,
        compiler_params=_cp(dimension_semantics=("arbitrary", "arbitrary")),
    )(dz, dz, slabs, *c_ins)
    return outs[0], list(outs[1:])


def _rms_fwd(x, g, *, name, tm=1024):
    t, d = x.shape
    tm = min(tm, t)

    def body(x_ref, g_ref, h_ref):
        xv = x_ref[...]
        r = lax.rsqrt(jnp.mean(xv * xv, axis=-1, keepdims=True) + EPS)
        h_ref[...] = (xv * r * g_ref[...]).astype(BF16)

    return _pcall(
        body, name=name, out_shape=_sds((t, d), BF16), grid=(t // tm,),
        in_specs=[pl.BlockSpec((tm, d), lambda i: (i, 0)), pl.BlockSpec((1, d), lambda i: (0, 0))],
        out_specs=pl.BlockSpec((tm, d), lambda i: (i, 0)), compiler_params=_cp(),
    )(x, g)


def _rms_bwd(dh, x, g, res, *, name, tm=1024):
    t, d = x.shape
    tm = min(tm, t)
    has_res = res is not None

    def body(*refs):
        dh_ref, x_ref, g_ref = refs[:3]
        r_ref = refs[3] if has_res else None
        dx_ref, dg_ref = refs[-2], refs[-1]
        xv = x_ref[...]
        r = lax.rsqrt(jnp.mean(xv * xv, axis=-1, keepdims=True) + EPS)
        xh = xv * r
        dhv = dh_ref[...]
        gd = dhv * g_ref[...]
        dx = r * (gd - xh * jnp.mean(gd * xh, axis=-1, keepdims=True))
        if has_res:
            dx = dx + r_ref[...]
        dx_ref[...] = dx

        @pl.when(pl.program_id(0) == 0)
        def _():
            dg_ref[...] = jnp.zeros_like(dg_ref)

        dg_ref[...] += jnp.sum(dhv * xh, axis=0, keepdims=True)

    row = pl.BlockSpec((tm, d), lambda i: (i, 0))
    vec = pl.BlockSpec((1, d), lambda i: (0, 0))
    return _pcall(
        body, name=name, out_shape=(_sds((t, d), F32), _sds((1, d), F32)), grid=(t // tm,),
        in_specs=[row, row, vec] + ([row] if has_res else []), out_specs=(row, vec),
        compiler_params=_cp(dimension_semantics=("arbitrary",)),
    )(*([dh, x, g] + ([res] if has_res else [])))


def _conv_fwd(z, conv_w, conv_b, *, nb, s, d, tc=512):
    nc = d // tc

    def body(cb_ref, cc_ref, cv_ref, w_ref, b_ref, y_ref):
        u = cc_ref[...].astype(F32) * cv_ref[...].astype(F32)
        row = lax.broadcasted_iota(jnp.int32, u.shape, 0)
        u1 = jnp.where(row >= 1, pltpu.roll(u, 1, axis=0), 0.0)
        u2 = jnp.where(row >= 2, pltpu.roll(u, 2, axis=0), 0.0)
        w = w_ref[...]
        y = w[2:3, :] * u + w[1:2, :] * u1 + w[0:1, :] * u2 + b_ref[...]
        y_ref[...] = (cb_ref[...].astype(F32) * y).astype(BF16)

    def part(p):
        return pl.BlockSpec((s, tc), lambda b, j: (b, p * nc + j))

    return _pcall(
        body, name="conv_fwd", out_shape=_sds((nb * s, d), BF16), grid=(nb, nc),
        in_specs=[part(0), part(1), part(2), pl.BlockSpec((3, tc), lambda b, j: (0, j)), pl.BlockSpec((1, tc), lambda b, j: (0, j))],
        out_specs=pl.BlockSpec((s, tc), lambda b, j: (b, j)), compiler_params=_cp(),
    )(z, z, z, conv_w, conv_b)


def _conv_bwd(z, dy, conv_w, conv_b, dz, *, nb, s, d, tc=256, comm=None):
    nc = d // tc
    c_ins, c_in_specs, c_out_shape, c_scratch = _comm_lists(comm)
    n_steps = nc * nb * 3

    def body(*refs):
        _, ins, cin, outs, cout, scr, csem = _split_refs(refs, 0, 7, len(c_ins), 3, len(c_ins), 1)
        cb_ref, cc_ref, cv_ref, dy_ref, w_ref, b_ref, _ = ins
        dz_ref, dw_ref, db_ref = outs
        stash = scr[0]
        b_i, p = pl.program_id(1), pl.program_id(2)
        step = (pl.program_id(0) * nb + b_i) * 3 + p
        _run_comm(comm, cin, cout, csem, step, n_steps, "start")

        @pl.when(p == 0)
        def _():
            cc, cv = cc_ref[...].astype(F32), cv_ref[...].astype(F32)
            u = cc * cv
            n = u.shape[0]
            row = lax.broadcasted_iota(jnp.int32, u.shape, 0)
            u1 = jnp.where(row >= 1, pltpu.roll(u, 1, axis=0), 0.0)
            u2 = jnp.where(row >= 2, pltpu.roll(u, 2, axis=0), 0.0)
            w = w_ref[...]
            y = w[2:3, :] * u + w[1:2, :] * u1 + w[0:1, :] * u2 + b_ref[...]
            dyv = dy_ref[...]
            dconv = dyv * cb_ref[...].astype(F32)
            g1 = jnp.where(row < n - 1, pltpu.roll(dconv, n - 1, axis=0), 0.0)
            g2 = jnp.where(row < n - 2, pltpu.roll(dconv, n - 2, axis=0), 0.0)
            du = w[2:3, :] * dconv + w[1:2, :] * g1 + w[0:1, :] * g2
            stash[0] = (dyv * y).astype(BF16)
            stash[1] = (du * cv).astype(BF16)
            stash[2] = (du * cc).astype(BF16)
            dws = [jnp.sum(dconv * u2, axis=0, keepdims=True), jnp.sum(dconv * u1, axis=0, keepdims=True),
                   jnp.sum(dconv * u, axis=0, keepdims=True)]
            db = jnp.sum(dconv, axis=0, keepdims=True)

            @pl.when(b_i == 0)
            def _():
                for r in range(3):
                    dw_ref[r:r + 1, :] = dws[r]
                db_ref[...] = db

            @pl.when(b_i > 0)
            def _():
                for r in range(3):
                    dw_ref[r:r + 1, :] += dws[r]
                db_ref[...] += db

        dz_ref[...] = stash[p]
        _run_comm(comm, cin, cout, csem, step, n_steps, "end")

    outs = _pcall(
        body, name="conv_bwd",
        out_shape=tuple([_sds(dz.shape, dz.dtype), _sds((3, d), F32), _sds((1, d), F32)] + c_out_shape), grid=(nc, nb, 3),
        in_specs=[
            pl.BlockSpec((s, tc), lambda j, b, p: (b, j)), pl.BlockSpec((s, tc), lambda j, b, p: (b, nc + j)),
            pl.BlockSpec((s, tc), lambda j, b, p: (b, 2 * nc + j)), pl.BlockSpec((s, tc), lambda j, b, p: (b, j)),
            pl.BlockSpec((3, tc), lambda j, b, p: (0, j)), pl.BlockSpec((1, tc), lambda j, b, p: (0, j)),
            HBM_SPEC,
        ] + c_in_specs,
        out_specs=tuple([pl.BlockSpec((s, tc), lambda j, b, p: (b, p * nc + j)),
                         pl.BlockSpec((3, tc), lambda j, b, p: (0, j)), pl.BlockSpec((1, tc), lambda j, b, p: (0, j))]
                        + [HBM_SPEC] * len(c_ins)),
        scratch_shapes=[pltpu.VMEM((3, s, tc), BF16)] + c_scratch,
        input_output_aliases={6: 0},
        compiler_params=_cp(dimension_semantics=("arbitrary", "arbitrary", "arbitrary")),
    )(z, z, z, dy, conv_w, conv_b, dz, *c_ins)
    return outs[0], outs[1], outs[2], list(outs[3:])


def _head_rms(xv, g):
    r = lax.rsqrt(jnp.mean(xv * xv, axis=-1, keepdims=True) + EPS)
    return xv * r * g


def _head_rms_bwd(xv, g, dy):
    r = lax.rsqrt(jnp.mean(xv * xv, axis=-1, keepdims=True) + EPS)
    xh = xv * r
    gd = dy * g
    dx = r * (gd - xh * jnp.mean(gd * xh, axis=-1, keepdims=True))
    return dx, jnp.sum(dy * xh, axis=0, keepdims=True)


def _fox_prep(z, q_g, k_g, *, t, d, tm=512):
    nh = d // HD

    def body(z_ref, qg_ref, kg_ref, q_ref, k_ref, v_ref):
        qg, kg = qg_ref[...], kg_ref[...]
        for h in range(nh):
            c0 = h * HD
            q_ref[:, c0:c0 + HD] = _head_rms(z_ref[:, c0:c0 + HD].astype(F32), qg).astype(BF16)
            k_ref[:, c0:c0 + HD] = _head_rms(z_ref[:, d + c0:d + c0 + HD].astype(F32), kg).astype(BF16)
        v_ref[...] = z_ref[:, 2 * d:3 * d]

    o = pl.BlockSpec((tm, d), lambda i: (i, 0))
    g = pl.BlockSpec((1, HD), lambda i: (0, 0))
    return _pcall(
        body, name="fox_prep", out_shape=(_sds((t, d), BF16),) * 3, grid=(t // tm,),
        in_specs=[pl.BlockSpec((tm, 3 * d), lambda i: (i, 1)), g, g], out_specs=(o, o, o), compiler_params=_cp(),
    )(z, q_g, k_g)


def _lane_cumsum(v, reverse=False):
    n = v.shape[1]
    lane = lax.broadcasted_iota(jnp.int32, v.shape, 1)
    sh = 1
    while sh < n:
        if reverse:
            v = v + jnp.where(lane < n - sh, pltpu.roll(v, n - sh, axis=1), 0.0)
        else:
            v = v + jnp.where(lane >= sh, pltpu.roll(v, sh, axis=1), 0.0)
        sh *= 2
    return v


def _fox_gates(z, f_bias_col, *, nb, s, nh):
    def body(z_ref, b_ref, c_ref):
        ff = z_ref[...].T[0:nh, :] + b_ref[...]
        lf = jnp.minimum(ff, 0.0) - jnp.log(1.0 + jnp.exp(-jnp.abs(ff)))
        c_ref[0] = _lane_cumsum(lf) * SQRT_HD

    return _pcall(
        body, name="fox_gates", out_shape=_sds((nb, nh, s), F32), grid=(nb,),
        in_specs=[pl.BlockSpec((s, HD), lambda b: (b, 0)), pl.BlockSpec((nh, 1), lambda b: (0, 0))],
        out_specs=pl.BlockSpec((1, nh, s), lambda b: (b, 0, 0)), compiler_params=_cp(),
    )(z, f_bias_col)


def _fox_gates_bwd(z, f_bias_col, row_sums, col_sums, dz, *, nb, s, nh, ff_block):
    def body(z_ref, b_ref, rs_ref, cs_ref, dz_in, dz_ref, db_ref, dc_sc):
        b_i, h = pl.program_id(0), pl.program_id(1)
        dc_sc[pl.ds(h, 1), :] = (rs_ref[...] - cs_ref[...]).T[0:1, :]

        @pl.when(h == nh - 1)
        def _():
            ff = z_ref[...].T[0:nh, :] + b_ref[...]
            dlf = _lane_cumsum(dc_sc[...], reverse=True)
            dff = dlf * (1.0 / (1.0 + jnp.exp(ff)))
            pad = jnp.concatenate([dff, jnp.zeros((HD - nh, s), F32)], axis=0)
            dz_ref[...] = pad.T.astype(BF16)
            dbv = jnp.sum(dff, axis=1, keepdims=True)

            @pl.when(b_i == 0)
            def _():
                db_ref[...] = dbv

            @pl.when(b_i > 0)
            def _():
                db_ref[...] += dbv

    hs = pl.BlockSpec((s, HD), lambda b, h: (b, h))
    return _pcall(
        body, name="fox_gates_bwd", out_shape=(_sds(dz.shape, dz.dtype), _sds((nh, 1), F32)), grid=(nb, nh),
        in_specs=[pl.BlockSpec((s, HD), lambda b, h: (b, 0)), pl.BlockSpec((nh, 1), lambda b, h: (0, 0)), hs, hs, HBM_SPEC],
        out_specs=(pl.BlockSpec((s, HD), lambda b, h: (b, ff_block)), pl.BlockSpec((nh, 1), lambda b, h: (0, 0))),
        scratch_shapes=[pltpu.VMEM((nh, s), F32)],
        input_output_aliases={4: 0}, compiler_params=_cp(dimension_semantics=("arbitrary", "arbitrary")),
    )(z, f_bias_col, row_sums, col_sums, dz)


def _fox_pairs(nq, key_major):
    if key_major:
        pairs = [(i, j) for j in range(nq) for i in range(j, nq)]
    else:
        pairs = [(i, j) for i in range(nq) for j in range(i + 1)]
    return jnp.array([p[0] for p in pairs], jnp.int32), jnp.array([p[1] for p in pairs], jnp.int32)


def _with_ones(x):
    return jnp.concatenate([x, jnp.ones_like(x)], axis=1)


def _fox_specs(nb, nh, nq, tq, tk, hp):
    qs = pl.BlockSpec((tq, hp * HD), lambda b, h, p, *tb: (b * nq + tb[0][p], h))
    ks = pl.BlockSpec((tk, hp * HD), lambda b, h, p, *tb: (b * nq + tb[1][p], h))
    cs = pl.BlockSpec((hp, 1, tk), lambda b, h, p, *tb: (b * (nh // hp) + h, 0, tb[1][p]))
    return qs, ks, cs


def _fox_raw(qv, kv, ckv, diag, tq, tk):
    sc = lax.dot_general(qv, kv, (((1,), (1,)), ((), ())), preferred_element_type=F32) - ckv
    if diag:
        rows = lax.broadcasted_iota(jnp.int32, (tq, tk), 0)
        cols = lax.broadcasted_iota(jnp.int32, (tq, tk), 1)
        sc = jnp.where(rows >= cols, sc, NEG)
    return sc


def _fox_probs(qv, kv, ckv, lse_col, diag, tq, tk):
    sc = _fox_raw(qv, kv, ckv, diag, tq, tk)
    return jnp.exp2(sc * EXP2_C - lse_col * LOG2E)


def _fox_call(body, *, name, ins, in_specs, out_shape, out_specs, scratch, grid, tables, comm):
    c_ins, c_in_specs, c_out_shape, c_scratch = _comm_lists(comm)
    n_in, n_out, n_scr, nc = len(ins), len(out_shape), len(scratch), len(c_ins)
    n_steps = grid[0] * grid[1] * grid[2]

    def wrapped(*refs):
        pre, rin, cin, rout, cout, scr, csem = _split_refs(refs, len(tables), n_in, nc, n_out, nc, n_scr)
        step = (pl.program_id(0) * grid[1] + pl.program_id(1)) * grid[2] + pl.program_id(2)
        _run_comm(comm, cin, cout, csem, step, n_steps, "start")
        body(*pre, *rin, *rout, *scr)
        _run_comm(comm, cin, cout, csem, step, n_steps, "end")

    sem = ("arbitrary",) * 3 if comm is not None else ("parallel", "parallel", "arbitrary")
    outs = _pcall(
        wrapped, name=name, out_shape=tuple(list(out_shape) + c_out_shape),
        grid_spec=pltpu.PrefetchScalarGridSpec(
            num_scalar_prefetch=len(tables), grid=grid, in_specs=list(in_specs) + c_in_specs,
            out_specs=tuple(list(out_specs) + [HBM_SPEC] * nc), scratch_shapes=list(scratch) + c_scratch),
        compiler_params=_cp(dimension_semantics=sem),
    )(*tables, *ins, *c_ins)
    return list(outs[:n_out]), list(outs[n_out:])


def _fox_fwd(q, k, v, cks, *, nb, s, nh, tq=512, comm=None):
    tk = tq
    nq = s // tq
    t = nb * s
    hp = nh
    it, jt = _fox_pairs(nq, key_major=False)

    def body(it_ref, jt_ref, q_ref, k_ref, v_ref, c_ref, o_ref, lse_ref, m_sc, acc_sc):
        p_id = pl.program_id(2)
        i, j = it_ref[p_id], jt_ref[p_id]

        @pl.when(j == 0)
        def _():
            m_sc[...] = jnp.full_like(m_sc, NEG)
            acc_sc[...] = jnp.zeros_like(acc_sc)

        def step(diag):
            for hh in range(hp):
                cols = slice(hh * HD, (hh + 1) * HD)
                sc = _fox_raw(q_ref[:, cols], k_ref[:, cols], c_ref[hh], diag, tq, tk)
                m_old = m_sc[hh]
                m_new = jnp.maximum(m_old, jnp.max(sc, axis=-1, keepdims=True))
                alpha = jnp.exp2((m_old - m_new) * EXP2_C)
                p = jnp.exp2((sc - m_new) * EXP2_C).astype(BF16)
                acc = alpha * acc_sc[hh] + jnp.dot(p, _with_ones(v_ref[:, cols]), preferred_element_type=F32)
                if diag:
                    l = acc[:, HD:2 * HD]
                    o_ref[:, cols] = (acc[:, 0:HD] / l).astype(BF16)
                    lse_ref[:, cols] = m_new * (1.0 / SQRT_HD) + jnp.log(l)
                else:
                    acc_sc[hh] = acc
                    m_sc[hh] = m_new

        @pl.when(j < i)
        def _():
            step(False)

        @pl.when(j == i)
        def _():
            step(True)

    qs, ks, cs = _fox_specs(nb, nh, nq, tq, tk, hp)
    (o, lse), c_out = _fox_call(
        body, name="fox_fwd", ins=[q, k, v, cks], in_specs=[qs, ks, ks, cs],
        out_shape=[_sds((t, nh * HD), BF16), _sds((t, nh * HD), F32)], out_specs=[qs, qs],
        scratch=[pltpu.VMEM((hp, tq, 1), F32), pltpu.VMEM((hp, tq, 2 * HD), F32)],
        grid=(nb, nh // hp, int(it.shape[0])), tables=(it, jt), comm=comm)
    return o, lse, c_out


def _fox_bwd(q, k, v, cks, do, o, lse, *, nb, s, nh, tq=512, comm=None):
    tk = tq
    nq = s // tq
    t = nb * s
    hp = FOX_HP
    it, jt = _fox_pairs(nq, key_major=True)
    done = jnp.array([j for j in range(nq) for _ in range(j, nq)], jnp.int32)

    def body(it_ref, jt_ref, done_ref, q_ref, k_ref, v_ref, c_ref, do_ref, o_ref, lse_ref,
             dq_ref, rs_ref, dk_ref, dv_ref, cs_ref, dq_sc, dk_sc, dv_sc):
        p_id = pl.program_id(2)
        i, j = it_ref[p_id], jt_ref[p_id]
        rows_i = pl.ds(pl.multiple_of(i * tq, tq), tq)

        def step(diag):
            for hh in range(hp):
                cols = slice(hh * HD, (hh + 1) * HD)
                qv, kv = q_ref[:, cols], k_ref[:, cols]
                p = _fox_probs(qv, kv, c_ref[hh], lse_ref[:, hh * HD:hh * HD + 1], diag, tq, tk)
                dov = do_ref[:, cols]
                dob = dov.astype(BF16)
                dvp = lax.dot_general(p.astype(BF16), dob, (((0,), (0,)), ((), ())), preferred_element_type=F32)
                dp = lax.dot_general(dob, v_ref[:, cols], (((1,), (1,)), ((), ())), preferred_element_type=F32)
                delta = jnp.sum(dov * o_ref[:, cols].astype(F32), axis=-1, keepdims=True)
                ds = (p * (dp - delta)).astype(BF16)
                dkp = lax.dot_general(ds, _with_ones(qv), (((0,), (0,)), ((), ())), preferred_element_type=F32)
                dqp = jnp.dot(ds, _with_ones(kv), preferred_element_type=F32)
                if diag:
                    dk_sc[hh] = dkp
                    dv_sc[hh] = dvp
                else:
                    dk_sc[hh] += dkp
                    dv_sc[hh] += dvp

                @pl.when(j == 0)
                def _():
                    dq_sc[hh, rows_i, :] = dqp

                @pl.when(j > 0)
                def _():
                    dq_sc[hh, rows_i, :] += dqp

                if diag:
                    acc = dq_sc[hh, rows_i, :]
                    dq_ref[:, cols] = acc[:, 0:HD] * (1.0 / SQRT_HD)
                    rs_ref[:, cols] = acc[:, HD:2 * HD]

        @pl.when(i == j)
        def _():
            step(True)

        @pl.when(i > j)
        def _():
            step(False)

        @pl.when(i == nq - 1)
        def _():
            for hh in range(hp):
                cols = slice(hh * HD, (hh + 1) * HD)
                dk_ref[:, cols] = dk_sc[hh, :, 0:HD] * (1.0 / SQRT_HD)
                cs_ref[:, cols] = dk_sc[hh, :, HD:2 * HD]
                dv_ref[:, cols] = dv_sc[hh].astype(BF16)

    qs, ks, cs = _fox_specs(nb, nh, nq, tq, tk, hp)
    ds_spec = pl.BlockSpec((tq, hp * HD), lambda b, h, p, *tb: (b * nq + tb[2][p], h))
    (dq, rs, dk, dv, csum), c_out = _fox_call(
        body, name="fox_bwd", ins=[q, k, v, cks, do, o, lse], in_specs=[qs, ks, ks, cs, qs, qs, qs],
        out_shape=[_sds((t, nh * HD), F32), _sds((t, nh * HD), F32), _sds((t, nh * HD), F32), _sds((t, nh * HD), BF16),
                   _sds((t, nh * HD), F32)],
        out_specs=[ds_spec, ds_spec, ks, ks, ks],
        scratch=[pltpu.VMEM((hp, s, 2 * HD), F32), pltpu.VMEM((hp, tk, 2 * HD), F32), pltpu.VMEM((hp, tk, HD), F32)],
        grid=(nb, nh // hp, int(it.shape[0])), tables=(it, jt, done), comm=comm)
    return dq, rs, dk, dv, csum, c_out


def _fox_post(z, dqn, dkn, dv, q_g, k_g, dz, *, t, d, tm=512):
    nh = d // HD

    def body(z_ref, dq_ref, dk_ref, dv_ref, qg_ref, kg_ref, dz_in, dz_ref, dqg_ref, dkg_ref):
        qg, kg = qg_ref[...], kg_ref[...]
        aq = jnp.zeros((1, HD), F32)
        ak = jnp.zeros((1, HD), F32)
        for h in range(nh):
            c0 = h * HD
            dx, sg = _head_rms_bwd(z_ref[:, c0:c0 + HD].astype(F32), qg, dq_ref[:, c0:c0 + HD])
            dz_ref[:, c0:c0 + HD] = dx.astype(BF16)
            aq = aq + sg
            dx, sg = _head_rms_bwd(z_ref[:, d + c0:d + c0 + HD].astype(F32), kg, dk_ref[:, c0:c0 + HD])
            dz_ref[:, d + c0:d + c0 + HD] = dx.astype(BF16)
            ak = ak + sg
        dz_ref[:, 2 * d:3 * d] = dv_ref[...]

        @pl.when(pl.program_id(0) == 0)
        def _():
            dqg_ref[...] = aq
            dkg_ref[...] = ak

        @pl.when(pl.program_id(0) > 0)
        def _():
            dqg_ref[...] += aq
            dkg_ref[...] += ak

    o = pl.BlockSpec((tm, d), lambda i: (i, 0))
    g = pl.BlockSpec((1, HD), lambda i: (0, 0))
    z3 = pl.BlockSpec((tm, 3 * d), lambda i: (i, 1))
    return _pcall(
        body, name="fox_post", out_shape=(_sds(dz.shape, dz.dtype), _sds((1, HD), F32), _sds((1, HD), F32)), grid=(t // tm,),
        in_specs=[z3, o, o, o, g, g, HBM_SPEC], out_specs=(z3, g, g),
        input_output_aliases={6: 0}, compiler_params=_cp(dimension_semantics=("arbitrary",)),
    )(z, dqn, dkn, dv, q_g, k_g, dz)


def _xa_prep_kv(kv, k_g, *, rows, d):
    xd = d // XH

    def body(kv_ref, g_ref, k_ref, v_ref):
        g = g_ref[...]
        for h in range(XH):
            c0 = h * xd
            k_ref[:, c0:c0 + xd] = _head_rms(kv_ref[:, c0:c0 + xd], g).astype(BF16)
        v_ref[...] = kv_ref[:, d:2 * d].astype(BF16)

    return _pcall(body, name="xa_prep_kv", out_shape=(_sds((rows, d), BF16),) * 2, compiler_params=_cp())(kv, k_g)


def _xa_fwd(z, kn, vb, q_g, *, nb, s, d, nm, q_block, tq=1024):
    xd = d // XH
    tq = min(tq, s)
    nq = s // tq

    def body(z_ref, k_ref, v_ref, g_ref, y_ref):
        g = g_ref[...]
        for h in range(XH):
            c0 = h * xd
            qn = _head_rms(z_ref[:, c0:c0 + xd].astype(F32), g).astype(BF16)
            sc = lax.dot_general(qn, k_ref[:, c0:c0 + xd], (((1,), (1,)), ((), ())), preferred_element_type=F32)
            sc = sc / math.sqrt(xd)
            p = jnp.exp(sc - jnp.max(sc, axis=-1, keepdims=True))
            p = p / jnp.sum(p, axis=-1, keepdims=True)
            y_ref[:, c0:c0 + xd] = jnp.dot(p.astype(BF16), v_ref[:, c0:c0 + xd], preferred_element_type=F32).astype(BF16)

    kvs = pl.BlockSpec((nm, d), lambda b, i: (b, 0))
    return _pcall(
        body, name="xa_fwd", out_shape=_sds((nb * s, d), BF16), grid=(nb, nq),
        in_specs=[pl.BlockSpec((tq, d), lambda b, i: (b * nq + i, q_block)), kvs, kvs, pl.BlockSpec((1, xd), lambda b, i: (0, 0))],
        out_specs=pl.BlockSpec((tq, d), lambda b, i: (b * nq + i, 0)), compiler_params=_cp(),
    )(z, kn, vb, q_g)


def _xa_bwd(z, kn, vb, q_g, dy, dz, *, nb, s, d, nm, q_block, tq=1024):
    xd = d // XH
    tq = min(tq, s)
    nq = s // tq

    def body(z_ref, k_ref, v_ref, g_ref, dy_ref, dz_in, dz_ref, dk_ref, dv_ref, dg_ref):
        b_i, i = pl.program_id(0), pl.program_id(1)
        g = g_ref[...]

        @pl.when(i == 0)
        def _():
            dk_ref[...] = jnp.zeros_like(dk_ref)
            dv_ref[...] = jnp.zeros_like(dv_ref)

        @pl.when((i == 0) & (b_i == 0))
        def _():
            dg_ref[...] = jnp.zeros_like(dg_ref)

        for h in range(XH):
            c0 = h * xd
            xq = z_ref[:, c0:c0 + xd].astype(F32)
            qn = _head_rms(xq, g).astype(BF16)
            kh, vh = k_ref[:, c0:c0 + xd], v_ref[:, c0:c0 + xd]
            sc = lax.dot_general(qn, kh, (((1,), (1,)), ((), ())), preferred_element_type=F32) / math.sqrt(xd)
            p = jnp.exp(sc - jnp.max(sc, axis=-1, keepdims=True))
            p = p / jnp.sum(p, axis=-1, keepdims=True)
            dob = dy_ref[:, c0:c0 + xd].astype(BF16)
            dv_ref[:, c0:c0 + xd] += lax.dot_general(p.astype(BF16), dob, (((0,), (0,)), ((), ())), preferred_element_type=F32)
            dp = lax.dot_general(dob, vh, (((1,), (1,)), ((), ())), preferred_element_type=F32)
            ds = p * (dp - jnp.sum(p * dp, axis=-1, keepdims=True)) / math.sqrt(xd)
            dsb = ds.astype(BF16)
            dqn = jnp.dot(dsb, kh, preferred_element_type=F32)
            dk_ref[:, c0:c0 + xd] += lax.dot_general(dsb, qn, (((0,), (0,)), ((), ())), preferred_element_type=F32)
            dx, sg = _head_rms_bwd(xq, g, dqn)
            dz_ref[:, c0:c0 + xd] = dx.astype(BF16)
            dg_ref[...] += sg

    kvs = pl.BlockSpec((nm, d), lambda b, i: (b, 0))
    qs = pl.BlockSpec((tq, d), lambda b, i: (b * nq + i, q_block))
    gs = pl.BlockSpec((1, xd), lambda b, i: (0, 0))
    return _pcall(
        body, name="xa_bwd",
        out_shape=(_sds(dz.shape, dz.dtype), _sds((nb * nm, d), F32), _sds((nb * nm, d), F32), _sds((1, xd), F32)), grid=(nb, nq),
        in_specs=[qs, kvs, kvs, gs, pl.BlockSpec((tq, d), lambda b, i: (b * nq + i, 0)), HBM_SPEC],
        out_specs=(qs, kvs, kvs, gs), input_output_aliases={5: 0},
        compiler_params=_cp(dimension_semantics=("arbitrary", "arbitrary")),
    )(z, kn, vb, q_g, dy, dz)


def _xa_post_kv(kv, dkn, dv, k_g, *, rows, d):
    xd = d // XH

    def body(kv_ref, dk_ref, dv_ref, g_ref, dkv_ref, dg_ref):
        g = g_ref[...]
        acc = jnp.zeros((1, xd), F32)
        for h in range(XH):
            c0 = h * xd
            dx, sg = _head_rms_bwd(kv_ref[:, c0:c0 + xd], g, dk_ref[:, c0:c0 + xd])
            dkv_ref[:, c0:c0 + xd] = dx.astype(BF16)
            acc = acc + sg
        dkv_ref[:, d:2 * d] = dv_ref[...].astype(BF16)
        dg_ref[...] = acc

    return _pcall(body, name="xa_post_kv", out_shape=(_sds((rows, 2 * d), BF16), _sds((1, xd), F32)), compiler_params=_cp())(
        kv, dkn, dv, k_g)


def _sigmoid(v):
    return 1.0 / (1.0 + jnp.exp(-v))


def _branches_fwd(z, ys, ws, *, t, d, gate_block, tm=1024, tn=512, comm=None):
    nj = d // tn
    c_ins, c_in_specs, c_out_shape, c_scratch = _comm_lists(comm)
    n_steps = (t // tm) * nj

    def body(*refs):
        _, ins, cin, outs, cout, _, csem = _split_refs(refs, 0, 9, len(c_ins), 2, len(c_ins), 0)
        ya_ref, yb_ref, yc_ref, wa_ref, wb_ref, wc_ref, ga_ref, gb_ref, gc_ref = ins
        m_ref, p_ref = outs
        step = pl.program_id(0) * nj + pl.program_id(1)
        _run_comm(comm, cin, cout, csem, step, n_steps, "start")
        acc = None
        for q, (y_ref, w_ref, g_ref) in enumerate(((ya_ref, wa_ref, ga_ref), (yb_ref, wb_ref, gb_ref), (yc_ref, wc_ref, gc_ref))):
            pr = jnp.dot(y_ref[...], w_ref[...], preferred_element_type=F32)
            p_ref[q] = pr.astype(BF16)
            term = _sigmoid(g_ref[...].astype(F32)) * pr
            acc = term if acc is None else acc + term
        m_ref[...] = acc.astype(BF16)
        _run_comm(comm, cin, cout, csem, step, n_steps, "end")

    y_spec = pl.BlockSpec((tm, d), lambda i, j: (i, 0))
    w_spec = pl.BlockSpec((d, tn), lambda i, j: (0, j))

    def gate(q):
        return pl.BlockSpec((tm, tn), lambda i, j: (i, (gate_block + q) * nj + j))

    outs = _pcall(
        body, name="branches_fwd", out_shape=tuple([_sds((t, d), BF16), _sds((3, t, d), BF16)] + c_out_shape), grid=(t // tm, nj),
        in_specs=[y_spec] * 3 + [w_spec] * 3 + [gate(0), gate(1), gate(2)] + c_in_specs,
        out_specs=tuple([pl.BlockSpec((tm, tn), lambda i, j: (i, j)), pl.BlockSpec((3, tm, tn), lambda i, j: (0, i, j))]
                        + [HBM_SPEC] * len(c_ins)),
        scratch_shapes=c_scratch, compiler_params=_cp(dimension_semantics=("arbitrary", "arbitrary")),
    )(*ys, *ws, z, z, z, *c_ins)
    return outs[0], outs[1], list(outs[2:])


def _gates_bwd(z, dm, proj, dz, *, t, d, gate_block, tm=1024):
    def body(g_ref, dm_ref, p_ref, dz_in, dz_ref, da_ref, db_ref, dc_ref):
        q = pl.program_id(1)
        sg = _sigmoid(g_ref[...].astype(F32))
        dmv = dm_ref[...]
        dz_ref[...] = (dmv * p_ref[0].astype(F32) * sg * (1.0 - sg)).astype(BF16)
        dp = (dmv * sg).astype(BF16)
        @pl.when(q == 0)
        def _():
            da_ref[...] = dp

        @pl.when(q == 1)
        def _():
            db_ref[...] = dp

        @pl.when(q == 2)
        def _():
            dc_ref[...] = dp

    gs = pl.BlockSpec((tm, d), lambda i, q: (i, gate_block + q))
    o = pl.BlockSpec((tm, d), lambda i, q: (i, 0))
    return _pcall(
        body, name="gates_bwd", out_shape=(_sds(dz.shape, dz.dtype),) + (_sds((t, d), BF16),) * 3, grid=(t // tm, 3),
        in_specs=[gs, o, pl.BlockSpec((1, tm, d), lambda i, q: (q, i, 0)), HBM_SPEC], out_specs=(gs, o, o, o),
        input_output_aliases={3: 0}, compiler_params=_cp(dimension_semantics=("arbitrary", "arbitrary")),
    )(z, dm, proj, dz)


def _ffn_in_fwd(h2, wfi_t, *, t, d, dff, tm=512, tn=1408):
    nj = dff // tn

    def body(h_ref, wg_ref, wu_ref, g_ref, u_ref, a_ref):
        hv = h_ref[...]
        dn = (((1,), (1,)), ((), ()))
        gv = lax.dot_general(hv, wg_ref[...], dn, preferred_element_type=F32)
        uv = lax.dot_general(hv, wu_ref[...], dn, preferred_element_type=F32)
        g_ref[...] = gv.astype(BF16)
        u_ref[...] = uv.astype(BF16)
        a_ref[...] = (gv * _sigmoid(gv) * uv).astype(BF16)

    o = pl.BlockSpec((tm, tn), lambda i, j: (i, j))
    return _pcall(
        body, name="ffn_in_fwd", out_shape=(_sds((t, dff), BF16),) * 3, grid=(t // tm, nj),
        in_specs=[pl.BlockSpec((tm, d), lambda i, j: (i, 0)), pl.BlockSpec((tn, d), lambda i, j: (j, 0)),
                  pl.BlockSpec((tn, d), lambda i, j: (nj + j, 0))],
        out_specs=(o, o, o), compiler_params=_cp(),
    )(h2, wfi_t, wfi_t)


def _ffn_out_bwd(dy, wfo, gate, up, *, t, d, dff, tm=512, tn=1408):
    def body(dy_ref, w_ref, g_ref, u_ref, dg_ref, du_ref):
        dav = lax.dot_general(dy_ref[...].astype(BF16), w_ref[...], (((1,), (1,)), ((), ())), preferred_element_type=F32)
        gv = g_ref[...].astype(F32)
        sg = _sigmoid(gv)
        dg_ref[...] = (dav * u_ref[...].astype(F32) * sg * (1.0 + gv * (1.0 - sg))).astype(BF16)
        du_ref[...] = (dav * gv * sg).astype(BF16)

    o = pl.BlockSpec((tm, tn), lambda i, j: (i, j))
    return _pcall(
        body, name="ffn_out_bwd", out_shape=(_sds((t, dff), BF16),) * 2, grid=(t // tm, dff // tn),
        in_specs=[pl.BlockSpec((tm, d), lambda i, j: (i, 0)), pl.BlockSpec((tn, d), lambda i, j: (j, 0)), o, o],
        out_specs=(o, o), compiler_params=_cp(),
    )(dy, wfo, gate, up)


def _ffn_out_loss(act, wfo, x1, target, *, t, d, dff, tm=512):
    def body(a_ref, w_ref, x_ref, t_ref, dy_ref, l_ref):
        yv = x_ref[...] + jnp.dot(a_ref[...], w_ref[...], preferred_element_type=F32)
        e = yv - t_ref[...]
        dy_ref[...] = e * (1.0 / d)

        @pl.when(pl.program_id(0) == 0)
        def _():
            l_ref[...] = jnp.zeros_like(l_ref)

        l_ref[...] += jnp.sum(jnp.sum(e * e, axis=-1, keepdims=True), axis=0, keepdims=True) * (0.5 / d)

    o = pl.BlockSpec((tm, d), lambda i: (i, 0))
    dy, l = _pcall(
        body, name="ffn_out_loss", out_shape=(_sds((t, d), F32), _sds((1, HD), F32)), grid=(t // tm,),
        in_specs=[pl.BlockSpec((tm, dff), lambda i: (i, 0)), pl.BlockSpec((dff, d), lambda i: (0, 0)), o, o],
        out_specs=(o, pl.BlockSpec((1, HD), lambda i: (0, 0))),
        compiler_params=_cp(dimension_semantics=("arbitrary",)),
    )(act, wfo, x1, target)
    return dy, l[0, 0]


def _adamw(w, g, m, v, *, name):
    if w.ndim == 3:
        rows, _, cols = w.shape
        tm = max(tt for tt in range(1, 65) if rows % tt == 0)
        block, imap, grid = (tm, 1, cols), (lambda i, j: (i, 0, 0)), (rows // tm, 1)
    else:
        rows, cols = w.shape
        tm, tn = rows, cols
        if rows > 512 and rows % 8 == 0:
            tm = max(tt for tt in range(8, 513, 8) if rows % tt == 0)
        elif rows > 512 and cols % 128 == 0:
            tn = 128
        block, imap, grid = (tm, tn), (lambda i, j: (i, j)), (rows // tm, cols // tn)
    c1 = 1.0 - ADAM_B1 ** ADAM_STEP
    c2 = 1.0 - ADAM_B2 ** ADAM_STEP

    at = (slice(None), 0, slice(None)) if w.ndim == 3 else Ellipsis

    def body(w_ref, g_ref, m_ref, v_ref, d_ref, nm_ref, nv_ref):
        gv = g_ref[at]
        mn = ADAM_B1 * m_ref[at] + (1.0 - ADAM_B1) * gv
        vn = ADAM_B2 * v_ref[at] + (1.0 - ADAM_B2) * (gv * gv)
        d_ref[at] = -ADAM_LR * ((mn / c1) / (jnp.sqrt(vn / c2) + ADAM_EPS) + ADAM_WD * w_ref[at])
        nm_ref[at] = mn
        nv_ref[at] = vn

    spec = pl.BlockSpec(block, imap)
    return _pcall(
        body, name=name, out_shape=(_sds(w.shape, F32),) * 3, grid=grid,
        in_specs=[spec] * 4, out_specs=(spec,) * 3, compiler_params=_cp(),
    )(w, g, m, v)


def _pad_rows(a, rows):
    return a if a.shape[0] == rows else jnp.pad(a, ((0, rows - a.shape[0]),) + ((0, 0),) * (a.ndim - 1))


class _Pack:
    def __init__(self, row_sizes):
        self.rows = list(row_sizes)
        self.padded = [-(-r // 16) * 16 for r in self.rows]
        self.offs = [sum(self.padded[:i]) for i in range(len(self.rows))]
        self.total = sum(self.padded)

    def stack(self, blocks):
        return jnp.concatenate([_pad_rows(b.astype(BF16), p) for b, p in zip(blocks, self.padded)], axis=0)

    def full(self, gathered, i):
        r = self.rows[i]
        return gathered[:, self.offs[i]:self.offs[i] + r, :].reshape(N_DEV * r, gathered.shape[2])

    def for_core(self, full_grads, core):
        parts = []
        for gfull, r, p in zip(full_grads, self.rows, self.padded):
            pieces = gfull if isinstance(gfull, (tuple, list)) else (gfull,)
            blk = jnp.concatenate(
                [lax.dynamic_index_in_dim(g.reshape(-1, 2, r, g.shape[1]), core, axis=1, keepdims=False) for g in pieces],
                axis=0).astype(BF16)
            if p != r:
                blk = jnp.pad(blk, ((0, 0), (0, p - r), (0, 0)))
            parts.append(blk)
        return jnp.concatenate(parts, axis=1)

    def shard(self, g_pack, i):
        return g_pack[self.offs[i]:self.offs[i] + self.rows[i], :]


def kernel(x, mem, norm1_g, w_in, conv_w, conv_b, fox_f_bias, fox_q_g, fox_k_g, mem_norm_g, w_mem_kv, xa_q_g, xa_k_g, w_br_conv, w_br_fox, w_br_xa, w_o, norm2_g, w_ffn_in, w_ffn_out, loss_target, m_norm1_g, m_w_in, m_conv_w, m_conv_b, m_fox_f_bias, m_fox_q_g, m_fox_k_g, m_mem_norm_g, m_w_mem_kv, m_xa_q_g, m_xa_k_g, m_w_br_conv, m_w_br_fox, m_w_br_xa, m_w_o, m_norm2_g, m_w_ffn_in, m_w_ffn_out, v_norm1_g, v_w_in, v_conv_w, v_conv_b, v_fox_f_bias, v_fox_q_g, v_fox_k_g, v_mem_norm_g, v_w_mem_kv, v_xa_q_g, v_xa_k_g, v_w_br_conv, v_w_br_fox, v_w_br_xa, v_w_o, v_norm2_g, v_w_ffn_in, v_w_ffn_out):
    nb, s, d = x.shape
    t = nb * s
    nm = mem.shape[1]
    nh = d // HD
    dff = w_ffn_out.shape[1] * N_DEV
    n_in = w_in.shape[2] * N_DEV
    n_in_pad = -(-n_in // 1152) * 1152
    ff_block = (10 * d) // HD
    my_c = lax.axis_index("c")
    my_q = 2 * lax.axis_index("x") + lax.axis_index("y")

    pack_a = _Pack([w_in.shape[2]])
    pack_b = _Pack([w_mem_kv.shape[2], w_br_conv.shape[1], w_br_fox.shape[1], w_br_xa.shape[1], w_o.shape[1],
                    w_ffn_in.shape[2], w_ffn_out.shape[1]])
    gather_2 = _Pack([w_mem_kv.shape[2], w_br_conv.shape[1], w_br_fox.shape[1], w_br_xa.shape[1], w_o.shape[1]])
    stack_a = pack_a.stack([w_in[0].T])
    stack_2 = gather_2.stack([w_mem_kv[0].T, w_br_conv[0], w_br_fox[0], w_br_xa[0], w_o[0]])
    stack_fi = w_ffn_in[0].T.astype(BF16)
    stack_fo = w_ffn_out[0].astype(BF16)
    cw_block = _pad_rows(conv_w[0], 8)
    rows_a = w_in.shape[2]
    my_x, my_y = lax.axis_index("x"), lax.axis_index("y")
    arrival = [(my_x, my_y, my_c), (my_x, my_y, 1 - my_c)]
    chip_1, chip_2, chip_d = _chip_order(my_x, my_y, my_c)
    arrival += [(*chip_1, my_c), (*chip_2, 1 - my_c), (*chip_2, my_c), (*chip_1, 1 - my_c), (*chip_d, my_c), (*chip_d, 1 - my_c)]
    order = jnp.stack([4 * ax + 2 * ay + ac for ax, ay, ac in arrival]).astype(jnp.int32)

    x2 = x.reshape(t, d)
    mem2 = mem.reshape(nb * nm, d)
    tgt2 = loss_target.reshape(t, d)
    h1 = _rms_fwd(x2, norm1_g, name="rms1_fwd")
    z, slabs, _ = _z_gathered(h1, stack_a, order, t=t, d=d, n_pad=N_DEV * SLAB_WIN, rows=rows_a)
    z = _z_fix(h1, slabs, z, t=t, d=d, rows=rows_a)
    ff_rows = _pad_rows(slabs[N_DEV - 1, rows_a - nh:rows_a, :], HD)
    zff = _mm(h1, ff_rows, m=t, n=HD, k=d, tb=True, tm=1024, name="mm_z_ff")
    qn, kn, vb = _fox_prep(z, fox_q_g, fox_k_g, t=t, d=d)
    f_bias_col = fox_f_bias.reshape(nh, 1)
    ck_rows = _fox_gates(zff, f_bias_col, nb=nb, s=s, nh=nh).reshape(nb * nh, 1, s)
    y_fox, lse, (gathered_2, cw_g, gathered_fo) = _fox_fwd(qn, kn, vb, ck_rows, nb=nb, s=s, nh=nh,
                                                           comm=_AllGather([stack_2, cw_block, stack_fo], forward_at=0.8))
    wkv_t, wbc, wbf, wbx, wo = (gather_2.full(gathered_2, i) for i in range(5))
    wfo = gathered_fo.reshape(dff, d)
    conv_w_full = jnp.transpose(cw_g[:, 0:3, :], (1, 0, 2)).reshape(3, d)

    y_conv = _conv_fwd(z, conv_w_full, conv_b, nb=nb, s=s, d=d)
    mem_n = _rms_fwd(mem2, mem_norm_g, name="rms_mem_fwd")
    kv = _mm(mem_n, wkv_t, m=nb * nm, n=2 * d, k=d, tb=True, name="mm_kv")
    xkn, xvb = _xa_prep_kv(kv, xa_k_g, rows=nb * nm, d=d)
    y_xa = _xa_fwd(z, xkn, xvb, xa_q_g, nb=nb, s=s, d=d, nm=nm, q_block=6)
    merged, proj, (gathered_fi,) = _branches_fwd(z, (y_conv, y_fox, y_xa), (wbc, wbf, wbx), t=t, d=d, gate_block=7,
                                                 comm=_AllGather([stack_fi]))
    wfi_t = gathered_fi.reshape(2 * dff, d)
    x1 = _mm(merged, wo, m=t, n=d, k=d, res=x2, tm=1024, name="mm_x1")
    h2 = _rms_fwd(x1, norm2_g, name="rms2_fwd")
    ff_gate, ff_up, act = _ffn_in_fwd(h2, wfi_t, t=t, d=d, dff=dff)
    dy, loss_local = _ffn_out_loss(act, wfo, x1, tgt2, t=t, d=d, dff=dff)

    g_wfo = _mm(act, dy, m=dff, n=d, k=t, ta=True, tm=1408, tk=2048, out_dtype=BF16, name="mm_g_wfo")
    dgate, dup = _ffn_out_bwd(dy, wfo, ff_gate, ff_up, t=t, d=d, dff=dff)
    dh2 = _mm(dgate, wfi_t, m=t, n=d, k=dff, tk=dff, name="mm_dh2_g")
    dh2 = _mm(dup, wfi_t, m=t, n=d, k=dff, tk=dff, b_off=(1, 0), res=dh2, name="mm_dh2_u")
    g_wfi_g = _mm(dgate, h2, m=dff, n=d, k=t, ta=True, tm=1408, tk=2048, out_dtype=BF16, name="mm_g_wfi_g")
    g_wfi_u = _mm(dup, h2, m=dff, n=d, k=t, ta=True, tm=1408, tk=2048, out_dtype=BF16, name="mm_g_wfi_u")
    dx1, g_norm2 = _rms_bwd(dh2, x1, norm2_g, dy, name="rms2_bwd")
    dmerged = _mm(dx1, wo, m=t, n=d, k=d, tb=True, tm=1024, name="mm_dmerged")
    g_wo = _mm(merged, dx1, m=d, n=d, k=t, ta=True, tk=2048, out_dtype=BF16, name="mm_g_wo")

    dz = lax.empty((t, n_in_pad), BF16)
    dz, dpa, dpb, dpc = _gates_bwd(z, dmerged, proj, dz, t=t, d=d, gate_block=7)
    dy_conv = _mm(dpa, wbc, m=t, n=d, k=d, tb=True, tm=1024, name="mm_dy_conv")
    g_wbc = _mm(y_conv, dpa, m=d, n=d, k=t, ta=True, tk=4096, out_dtype=BF16, name="mm_g_wbc")
    dy_fox = _mm(dpb, wbf, m=t, n=d, k=d, tb=True, tm=1024, name="mm_dy_fox")
    g_wbf = _mm(y_fox, dpb, m=d, n=d, k=t, ta=True, tk=4096, out_dtype=BF16, name="mm_g_wbf")
    dy_xa = _mm(dpc, wbx, m=t, n=d, k=d, tb=True, tm=1024, name="mm_dy_xa")
    g_wbx = _mm(y_xa, dpc, m=d, n=d, k=t, ta=True, tk=4096, out_dtype=BF16, name="mm_g_wbx")

    dz, dxkn, dxv, g_xa_q = _xa_bwd(z, xkn, xvb, xa_q_g, dy_xa, dz, nb=nb, s=s, d=d, nm=nm, q_block=6)
    dkv, g_xa_k = _xa_post_kv(kv, dxkn, dxv, xa_k_g, rows=nb * nm, d=d)
    g_wkv_t = _mm(dkv, mem_n, m=2 * d, n=d, k=nb * nm, ta=True, out_dtype=BF16, name="mm_g_wkv")
    dmem_n = _mm(dkv, wkv_t, m=nb * nm, n=d, k=2 * d, tk=2 * d, name="mm_dmem")
    _, g_mem_norm = _rms_bwd(dmem_n, mem2, mem_norm_g, None, name="rms_mem_bwd")

    grads_b = [g_wkv_t, g_wbc, g_wbf, g_wbx, g_wo, (g_wfi_g, g_wfi_u), g_wfo]
    keep_b, send_b = pack_b.for_core(grads_b, my_c), pack_b.for_core(grads_b, 1 - my_c)
    dz, g_conv_w, g_conv_b, (from_sibling_b,) = _conv_bwd(z, dy_conv, conv_w_full, conv_b, dz, nb=nb, s=s, d=d,
                                                          comm=_SiblingSwap(send_b))
    part_b = _add2(keep_b, from_sibling_b, name="rs_add_sibling_b")
    dqn, ds_rows, dkn, dvb, ds_cols, (from_chips_b,) = _fox_bwd(qn, kn, vb, ck_rows, dy_fox, y_fox, lse, nb=nb, s=s, nh=nh,
                                                                comm=_ChipExchange(part_b))
    g_pack_b = _add4(lax.dynamic_index_in_dim(part_b, my_q, axis=0, keepdims=False), from_chips_b, name="rs_add_chips_b")
    dz, g_fox_q, g_fox_k = _fox_post(z, dqn, dkn, dvb, fox_q_g, fox_k_g, dz, t=t, d=d)
    dz, g_f_bias = _fox_gates_bwd(zff, f_bias_col, ds_rows, ds_cols, dz, nb=nb, s=s, nh=nh, ff_block=ff_block)

    keep_a, from_sibling_a = _g_win_slabs(dz, h1, my_c, t=t, d=d, n_slab=N_DEV, rows=rows_a, rp=slabs.shape[1])
    part_a = _add2(keep_a, from_sibling_a, name="rs_add_sibling_a")
    dh1, (from_chips_a,) = _dh1_from_slabs(dz, slabs, t=t, d=d, rows=rows_a, comm=_ChipExchange(part_a))
    g_pack_a = _add4(lax.dynamic_index_in_dim(part_a, my_q, axis=0, keepdims=False), from_chips_a, name="rs_add_chips_a",
                     rows_apart=True)
    dx, g_norm1 = _rms_bwd(dh1, x2, norm1_g, dx1, name="rms1_bwd")

    xd = d // XH
    tail = jnp.concatenate(
        [_pad_rows(g_f_bias.reshape(nh), HD).reshape(1, HD), g_fox_q, g_fox_k, g_xa_q, g_xa_k,
         jnp.zeros((1, d - 3 * HD - 2 * xd), F32)], axis=1)
    small = jnp.concatenate([g_norm1, g_norm2, g_conv_w, g_conv_b, g_mem_norm, tail], axis=0)
    small = _all_reduce_small(small)
    gs_conv_w = lax.dynamic_slice_in_dim(small[2:5], (2 * my_q + my_c) * (d // N_DEV), d // N_DEV, axis=1)

    grads = {
        "norm1_g": small[0:1], "w_in": g_pack_a, "conv_w": gs_conv_w, "conv_b": small[5:6],
        "fox_f_bias": small[7:8, 0:nh], "fox_q_g": small[7:8, HD:2 * HD], "fox_k_g": small[7:8, 2 * HD:3 * HD],
        "mem_norm_g": small[6:7], "w_mem_kv": pack_b.shard(g_pack_b, 0), "xa_q_g": small[7:8, 3 * HD:3 * HD + xd],
        "xa_k_g": small[7:8, 3 * HD + xd:3 * HD + 2 * xd], "w_br_conv": pack_b.shard(g_pack_b, 1),
        "w_br_fox": pack_b.shard(g_pack_b, 2), "w_br_xa": pack_b.shard(g_pack_b, 3), "w_o": pack_b.shard(g_pack_b, 4),
        "norm2_g": small[1:2], "w_ffn_in": pack_b.shard(g_pack_b, 5), "w_ffn_out": pack_b.shard(g_pack_b, 6),
    }
    transposed = {"w_in", "w_mem_kv", "w_ffn_in"}
    weights = {
        "norm1_g": (norm1_g, m_norm1_g, v_norm1_g), "w_in": (w_in, m_w_in, v_w_in), "conv_w": (conv_w, m_conv_w, v_conv_w),
        "conv_b": (conv_b, m_conv_b, v_conv_b), "fox_f_bias": (fox_f_bias, m_fox_f_bias, v_fox_f_bias),
        "fox_q_g": (fox_q_g, m_fox_q_g, v_fox_q_g), "fox_k_g": (fox_k_g, m_fox_k_g, v_fox_k_g),
        "mem_norm_g": (mem_norm_g, m_mem_norm_g, v_mem_norm_g), "w_mem_kv": (w_mem_kv, m_w_mem_kv, v_w_mem_kv),
        "xa_q_g": (xa_q_g, m_xa_q_g, v_xa_q_g), "xa_k_g": (xa_k_g, m_xa_k_g, v_xa_k_g),
        "w_br_conv": (w_br_conv, m_w_br_conv, v_w_br_conv), "w_br_fox": (w_br_fox, m_w_br_fox, v_w_br_fox),
        "w_br_xa": (w_br_xa, m_w_br_xa, v_w_br_xa), "w_o": (w_o, m_w_o, v_w_o), "norm2_g": (norm2_g, m_norm2_g, v_norm2_g),
        "w_ffn_in": (w_ffn_in, m_w_ffn_in, v_w_ffn_in), "w_ffn_out": (w_ffn_out, m_w_ffn_out, v_w_ffn_out),
    }
    out_g, out_d, out_m, out_v = [], [], [], []
    for name, (w, m, v) in weights.items():
        shape = w.shape
        if name == "w_in":
            w3, m3, v3 = (a.transpose(2, 0, 1) for a in (w, m, v))
            dl, mn, vn = _adamw(w3, g_pack_a, m3, v3, name="adamw_" + name)
            out_g.append(g_pack_a[:shape[2]].transpose(1, 2, 0))
            out_d.append(dl.transpose(1, 2, 0))
            out_m.append(mn.transpose(1, 2, 0))
            out_v.append(vn.transpose(1, 2, 0))
            continue
        tr = name in transposed

        def as2d(a):
            a = a.reshape(shape[-2], shape[-1])
            return a.T if tr else a

        def back(a):
            return (a.T if tr else a).reshape(shape)

        w2 = as2d(w)
        g2 = grads[name].reshape(w2.shape)
        dl, mn, vn = _adamw(w2, g2, as2d(m), as2d(v), name="adamw_" + name)
        out_g.append(back(g2))
        out_d.append(back(dl))
        out_m.append(back(mn))
        out_v.append(back(vn))

    loss = lax.psum(loss_local, ("x", "y", "c"))
    return (loss, dx.reshape(nb, s, d), *out_g, *out_d, *out_m, *out_v)
```

```python
import math

import jax
import jax.numpy as jnp
from jax import lax
from jax.experimental import pallas as pl
from jax.experimental.pallas import tpu as pltpu

F32, BF16 = jnp.float32, jnp.bfloat16
EPS = 1e-6
HD = 128
XH = 4
N_DEV = 8
MESH = pl.DeviceIdType.MESH
VMEM_LIMIT = 52 * 1024 * 1024
NEG = -1e30
SQRT_HD = math.sqrt(HD)
EXP2_C = math.log2(math.e) / SQRT_HD
LOG2E = math.log2(math.e)
FOX_HP = 4

ADAM_LR, ADAM_B1, ADAM_B2, ADAM_EPS, ADAM_WD, ADAM_STEP = 0.001, 0.9, 0.999, 1e-08, 0.01, 10


def _cp(**kw):
    return pltpu.CompilerParams(vmem_limit_bytes=VMEM_LIMIT, **kw)


def _pcall(body, **kw):
    return pl.pallas_call(body, **kw)


def _sds(shape, dtype):
    return jax.ShapeDtypeStruct(shape, dtype)


HBM_SPEC = pl.BlockSpec(memory_space=pl.ANY)


def _mesh_pos():
    return lax.axis_index("x"), lax.axis_index("y"), lax.axis_index("c")


class _AllGather:
    def __init__(self, blocks, forward_at=None):
        self.blocks = list(blocks)
        self.forward_at = forward_at
        n = len(self.blocks)
        self.out_shape = [_sds((N_DEV,) + b.shape, b.dtype) for b in self.blocks]
        self.scratch = [pltpu.SemaphoreType.DMA((n, 7)), pltpu.SemaphoreType.DMA((n, 7)), pltpu.SemaphoreType.DMA((n,))]

    def bind(self, ins, outs, sems):
        n = len(ins)
        send_sems, recv_sems, local_sems = sems
        x, y, c = _mesh_pos()
        me, sibling = (x, y, c), (x, y, 1 - c)
        chips = [(1 - x, y), (x, 1 - y), (1 - x, 1 - y)]

        def slab(t, dev):
            return outs[t].at[4 * dev[0] + 2 * dev[1] + dev[2]]

        def copy(t, k, block, to, src=None):
            dst = slab(t, block)
            return pltpu.make_async_remote_copy(
                src_ref=dst if src is None else src, dst_ref=dst, send_sem=send_sems.at[t, k], recv_sem=recv_sems.at[t, k],
                device_id=to, device_id_type=MESH)

        mine = [pltpu.make_async_copy(ins[t], slab(t, me), local_sems.at[t]) for t in range(n)]
        first = []
        for t in range(n):
            first.append(copy(t, 0, me, sibling, src=ins[t]))
            first += [copy(t, 1 + j, me, (*chip, c), src=ins[t]) for j, chip in enumerate(chips)]
        passed = [copy(t, 4 + j, (*chip, c), sibling) for j, chip in enumerate(chips) for t in range(n)]

        def start():
            for cp in mine + first:
                cp.start()

        def mid():
            for j, chip in enumerate(chips):
                for t in range(n):
                    copy(t, 1 + j, (*chip, c), me).wait_recv()
                    passed[j * n + t].start()

        def finish():
            for t in range(n):
                copy(t, 0, sibling, me).wait_recv()
                for j, chip in enumerate(chips):
                    copy(t, 4 + j, (*chip, 1 - c), me).wait_recv()
            for cp in first + passed:
                cp.wait_send()
            for cp in mine:
                cp.wait()

        return start, mid, finish


class _SiblingSwap:
    def __init__(self, send):
        self.blocks = [send]
        self.out_shape = [_sds(send.shape, send.dtype)]
        self.scratch = [pltpu.SemaphoreType.DMA, pltpu.SemaphoreType.DMA]

    def bind(self, ins, outs, sems):
        x, y, c = _mesh_pos()
        cp = pltpu.make_async_remote_copy(src_ref=ins[0], dst_ref=outs[0], send_sem=sems[0], recv_sem=sems[1],
                                          device_id=(x, y, 1 - c), device_id_type=MESH)
        return cp.start, (lambda: None), cp.wait


class _ChipExchange:
    def __init__(self, part):
        self.blocks = [part]
        self.out_shape = [_sds((3,) + part.shape[1:], part.dtype)]
        self.scratch = [pltpu.SemaphoreType.DMA((3,)), pltpu.SemaphoreType.DMA((3,))]

    def bind(self, ins, outs, sems):
        x, y, c = _mesh_pos()
        chips = [(1 - x, y), (x, 1 - y), (1 - x, 1 - y)]
        cps = [
            pltpu.make_async_remote_copy(src_ref=ins[0].at[2 * cx + cy], dst_ref=outs[0].at[j], send_sem=sems[0].at[j],
                                         recv_sem=sems[1].at[j], device_id=(cx, cy, c), device_id_type=MESH)
            for j, (cx, cy) in enumerate(chips)
        ]

        def start():
            for cp in cps:
                cp.start()

        def finish():
            for cp in cps:
                cp.wait()

        return start, (lambda: None), finish


def _comm_lists(comm):
    if comm is None:
        return [], [], [], []
    n = len(comm.blocks)
    return list(comm.blocks), [HBM_SPEC] * n, list(comm.out_shape), list(comm.scratch)


def _split_refs(refs, n_pre, n_in, n_cin, n_out, n_cout, n_scr):
    cuts = [n_pre, n_in, n_cin, n_out, n_cout, n_scr]
    out, at = [], 0
    for c in cuts:
        out.append(refs[at:at + c])
        at += c
    out.append(refs[at:])
    return out


def _run_comm(comm, cin, cout, csem, step, n_steps, when="start"):
    if comm is None:
        return
    start, mid, finish = comm.bind(cin, cout, csem)
    if when == "start":
        pl.when(step == 0)(start)
    else:
        frac = getattr(comm, "forward_at", None)
        mid_step = max(n_steps - 2, 0) if frac is None else min(int(frac * n_steps), max(n_steps - 2, 0))
        pl.when(step == mid_step)(mid)
        pl.when(step == n_steps - 1)(finish)


def _all_reduce_small(v):
    rows, cols = v.shape

    def body(v_ref, o_ref, slots, send_sems, recv_sems):
        x, y, c = _mesh_pos()
        me = 4 * x + 2 * y + c
        slots[me] = v_ref[...]
        cps = []
        for k in range(1, N_DEV):
            px, py, pc = x ^ (k >> 2), y ^ ((k >> 1) & 1), c ^ (k & 1)
            cps.append(pltpu.make_async_remote_copy(
                src_ref=v_ref, dst_ref=slots.at[me], send_sem=send_sems.at[k - 1], recv_sem=recv_sems.at[k - 1],
                device_id=(px, py, pc), device_id_type=MESH))
        for cp in cps:
            cp.start()
        for cp in cps:
            cp.wait()
        acc = slots[0]
        for k in range(1, N_DEV):
            acc = acc + slots[k]
        o_ref[...] = acc

    return _pcall(
        body, name="all_reduce_small", out_shape=_sds((rows, cols), F32),
        in_specs=[pl.BlockSpec(memory_space=pltpu.VMEM)], out_specs=pl.BlockSpec(memory_space=pltpu.VMEM),
        scratch_shapes=[pltpu.VMEM((N_DEV, rows, cols), F32), pltpu.SemaphoreType.DMA((7,)), pltpu.SemaphoreType.DMA((7,))],
        compiler_params=_cp(),
    )(v)


def _row_tile(rows, cap=1040):
    return max(tt for tt in range(16, cap + 1, 16) if rows % tt == 0)


def _add2(a, b, *, name):
    nq, rows, cols = a.shape
    tm = _row_tile(rows)

    def body(a_ref, b_ref, o_ref):
        o_ref[...] = (a_ref[...].astype(F32) + b_ref[...].astype(F32)).astype(BF16)

    spec = pl.BlockSpec((1, tm, cols), lambda q, i: (q, i, 0))
    return _pcall(body, name=name, out_shape=_sds(a.shape, BF16), grid=(nq, rows // tm),
                  in_specs=[spec, spec], out_specs=spec, compiler_params=_cp())(a, b)


def _add4(own, recv, *, name, rows_apart=False):
    rows, cols = own.shape
    tm = _row_tile(rows, 256 if rows_apart else 1040)

    def body(o_ref, r_ref, g_ref):
        tot = ((o_ref[...].astype(F32) + r_ref[0].astype(F32)) + r_ref[1].astype(F32)) + r_ref[2].astype(F32)
        if rows_apart:
            g_ref[:, 0, :] = tot
        else:
            g_ref[...] = tot

    if rows_apart:
        out_shape, out_spec = _sds((rows, 1, cols), F32), pl.BlockSpec((tm, 1, cols), lambda i: (i, 0, 0))
    else:
        out_shape, out_spec = _sds((rows, cols), F32), pl.BlockSpec((tm, cols), lambda i: (i, 0))
    return _pcall(
        body, name=name, out_shape=out_shape, grid=(rows // tm,),
        in_specs=[pl.BlockSpec((tm, cols), lambda i: (i, 0)), pl.BlockSpec((3, tm, cols), lambda i: (0, i, 0))],
        out_specs=out_spec, compiler_params=_cp(),
    )(own, recv)


def _mm(a, b, *, m, n, k, name, ta=False, tb=False, b_off=(0, 0), res=None, out_dtype=F32, tm=512, tn=1024, tk=1024,
        comm=None):
    tm, tn, tk = min(tm, m), min(tn, n), min(tk, k)
    assert m % tm == 0 and n % tn == 0 and k % tk == 0, (name, m, n, k, tm, tn, tk)
    nk = k // tk
    grid = (m // tm, n // tn, nk)
    bo0, bo1 = b_off
    if ta:
        a_spec = pl.BlockSpec((tk, tm), lambda i, j, kk: (kk, i))
    else:
        a_spec = pl.BlockSpec((tm, tk), lambda i, j, kk: (i, kk))
    if tb:
        b_spec = pl.BlockSpec((tn, tk), lambda i, j, kk: (j + bo0, kk + bo1))
    else:
        b_spec = pl.BlockSpec((tk, tn), lambda i, j, kk: (kk + bo0, j + bo1))
    o_spec = pl.BlockSpec((tm, tn), lambda i, j, kk: (i, j))
    dn = (((0 if ta else 1,), (1 if tb else 0,)), ((), ()))
    has_res = res is not None
    c_ins, c_in_specs, c_out_shape, c_scratch = _comm_lists(comm)
    n_in = 3 if has_res else 2
    n_scr = 1 if nk > 1 else 0

    def body(*refs):
        _, ins, cin, outs, cout, scr, csem = _split_refs(refs, 0, n_in, len(c_ins), 1, len(c_ins), n_scr)
        a_ref, b_ref = ins[0], ins[1]
        r_ref = ins[2] if has_res else None
        o_ref = outs[0]
        step = (pl.program_id(0) * grid[1] + pl.program_id(1)) * nk + pl.program_id(2)
        _run_comm(comm, cin, cout, csem, step, grid[0] * grid[1] * nk, "start")
        p = lax.dot_general(a_ref[...].astype(BF16), b_ref[...].astype(BF16), dn, preferred_element_type=F32)

        def finish(acc):
            if has_res:
                acc = acc + r_ref[...]
            o_ref[...] = acc.astype(out_dtype)

        if nk == 1:
            finish(p)
        else:
            acc_ref = scr[0]
            kk = pl.program_id(2)

            @pl.when(kk == 0)
            def _():
                acc_ref[...] = p

            @pl.when(kk > 0)
            def _():
                acc_ref[...] += p

            @pl.when(kk == nk - 1)
            def _():
                finish(acc_ref[...])

        _run_comm(comm, cin, cout, csem, step, grid[0] * grid[1] * nk, "end")

    ins = [a, b] + ([res] if has_res else [])
    in_specs = [a_spec, b_spec] + ([o_spec] if has_res else [])
    sem = ("arbitrary",) * 3 if comm is not None else ("parallel", "parallel", "arbitrary")
    outs = _pcall(
        body, name=name, out_shape=tuple([_sds((m, n), out_dtype)] + c_out_shape), grid=grid,
        in_specs=in_specs + c_in_specs, out_specs=tuple([o_spec] + [HBM_SPEC] * len(c_ins)),
        scratch_shapes=([pltpu.VMEM((tm, tn), F32)] if nk > 1 else []) + c_scratch,
        compiler_params=_cp(dimension_semantics=sem),
    )(*(ins + c_ins))
    return outs[0] if comm is None else (outs[0], list(outs[1:]))


SLAB_WIN = 1280
SLAB_TAIL = 48


def _chip_order(x, y, c):
    north = c == 1
    first = (jnp.where(north, 1 - x, x), jnp.where(north, y, 1 - y))
    second = (jnp.where(north, x, 1 - x), jnp.where(north, 1 - y, y))
    return [first, second, (1 - x, 1 - y)]


def _z_gathered(h1, own_slab, order, *, t, d, n_pad, rows, comm=None, tm=1024):
    rp = own_slab.shape[0]
    n_slab = N_DEV
    tm = min(tm, t)
    ni = t // tm
    sh = rows - SLAB_WIN
    assert sh >= 0 and (n_slab - 1) * sh <= 128
    c_ins, c_in_specs, c_out_shape, c_scratch = _comm_lists(comm)
    n_steps = n_slab * ni

    def body(*refs):
        pre, ins, cin, outs, cout, scr, csem = _split_refs(refs, 1, 2, len(c_ins), 2, len(c_ins), 5)
        order_ref = pre[0]
        h_ref, own_ref = ins
        o_ref, slabs_ref = outs
        slab_vm, send_sems, recv_sems, local_sem, load_sem = scr
        p, i = pl.program_id(0), pl.program_id(1)
        step = p * ni + i
        x, y, c = _mesh_pos()
        me, sibling = (x, y, c), (x, y, 1 - c)
        chips = _chip_order(x, y, c)

        def slab(dev):
            return slabs_ref.at[4 * dev[0] + 2 * dev[1] + dev[2]]

        def copy(k, block, to, src=None, part=None):
            dst = slab(block) if part is None else slab(block).at[part]
            return pltpu.make_async_remote_copy(
                src_ref=dst if src is None else src, dst_ref=dst, send_sem=send_sems.at[k], recv_sem=recv_sems.at[k],
                device_id=to, device_id_type=MESH)

        nb_1, nb_2, diag = ((*chip, c) for chip in chips)
        cut = (rp // 32) * 16
        half_a, half_b = pl.ds(0, cut), pl.ds(cut, rp - cut)
        mine = pltpu.make_async_copy(own_ref, slab(me), local_sem)
        first = [copy(0, me, sibling, src=own_ref), copy(1, me, nb_1, src=own_ref), copy(2, me, nb_2, src=own_ref)]
        relays = [copy(3, nb_1, nb_2, part=half_a), copy(4, nb_2, nb_1, part=half_b)]
        axis_of = [jnp.where(c == 1, 0, 1), jnp.where(c == 1, 1, 0), 2]
        passed = [copy(5 + axis_of[j], (*chip, c), sibling) for j, chip in enumerate(chips)]
        later = min(n_steps // 5, n_steps - 1)

        @pl.when(step == 0)
        def _():
            mine.start()
            first[0].start()
            first[1].start()

        @pl.when(step == later)
        def _():
            first[2].start()

        _run_comm(comm, cin, cout, csem, step, n_steps, "start")

        def load(src):
            cp = pltpu.make_async_copy(src, slab_vm, load_sem)
            cp.start()
            cp.wait()

        @pl.when(i == 0)
        def _():
            @pl.when(p == 0)
            def _():
                load(own_ref)

            @pl.when(p == 1)
            def _():
                copy(0, sibling, me).wait_recv()
                load(slab(sibling))

            for j, chip in enumerate(chips):
                @pl.when(p == 2 + 2 * j)
                def _():
                    if j < 2:
                        copy(1 + j, (*chip, c), me).wait_recv()
                        relays[j].start()
                    else:
                        copy(3, diag, me, part=half_a).wait_recv()
                        copy(4, diag, me, part=half_b).wait_recv()
                    passed[j].start()
                    load(slab((*chip, c)))

                other = (1, 0, 2)[j]

                @pl.when(p == 3 + 2 * j)
                def _():
                    copy(5 + axis_of[other], (*chips[other], 1 - c), me).wait_recv()
                    load(slab((*chips[other], 1 - c)))

        k = order_ref[p]
        y_k = lax.dot_general(h_ref[...], slab_vm[...], (((1,), (1,)), ((), ())), preferred_element_type=F32)
        o_ref[...] = pltpu.roll(y_k[:, 0:SLAB_WIN], k * sh, axis=1).astype(BF16)

        @pl.when(step == n_steps - 1)
        def _():
            for cp in first + relays + passed:
                cp.wait_send()
            mine.wait()

        _run_comm(comm, cin, cout, csem, step, n_steps, "end")

    outs = _pcall(
        body, name="mm_z", out_shape=tuple([_sds((t, n_pad), BF16), _sds((n_slab, rp, d), BF16)] + c_out_shape),
        grid_spec=pltpu.PrefetchScalarGridSpec(
            num_scalar_prefetch=1, grid=(n_slab, ni),
            in_specs=[pl.BlockSpec((tm, d), lambda p, i, order: (i, 0)), HBM_SPEC] + c_in_specs,
            out_specs=tuple([pl.BlockSpec((tm, SLAB_WIN), lambda p, i, order: (i, order[p])), HBM_SPEC] + [HBM_SPEC] * len(c_ins)),
            scratch_shapes=[pltpu.VMEM((rp, d), BF16), pltpu.SemaphoreType.DMA((8,)), pltpu.SemaphoreType.DMA((8,)),
                            pltpu.SemaphoreType.DMA, pltpu.SemaphoreType.DMA] + c_scratch),
        compiler_params=_cp(dimension_semantics=("arbitrary", "arbitrary")),
    )(order, h1, own_slab, *c_ins)
    return outs[0], outs[1], list(outs[2:])


def _z_fix(h1, slabs, z, *, t, d, rows, tm=4096):
    n_slab, rp, _ = slabs.shape
    tm = min(tm, t)
    sh = rows - SLAB_WIN
    tail_lo = rp - SLAB_TAIL
    assert rows - (n_slab - 1) * sh >= tail_lo and rp % SLAB_TAIL == 0

    def body(h_ref, t_ref, z_ref, o_ref):
        k = pl.program_id(0) + 1
        p = lax.dot_general(h_ref[...], t_ref[0], (((1,), (1,)), ((), ())), preferred_element_type=F32)
        pp = jnp.concatenate([p, jnp.zeros((tm, 128 - SLAB_TAIL), F32)], axis=1)
        pr = pltpu.roll(pp, 128 - (rows - tail_lo) + k * sh, axis=1)
        lane = lax.broadcasted_iota(jnp.int32, (tm, 128), 1)
        o_ref[...] = jnp.where(lane < k * sh, pr.astype(BF16), z_ref[...])

    blk = pl.BlockSpec((tm, 128), lambda b, i: (i, (SLAB_WIN // 128) * (b + 1)))
    return _pcall(
        body, name="mm_z_fix", out_shape=_sds(z.shape, z.dtype), grid=(n_slab - 1, t // tm),
        in_specs=[pl.BlockSpec((tm, d), lambda b, i: (i, 0)),
                  pl.BlockSpec((1, SLAB_TAIL, d), lambda b, i: (b, tail_lo // SLAB_TAIL, 0)), blk],
        out_specs=blk, input_output_aliases={2: 0}, compiler_params=_cp(),
    )(h1, slabs, z)


def _slab_window(a1_ref, a2_ref, k, sh):
    w = jnp.concatenate([a1_ref[...], a2_ref[...]], axis=1)
    width = SLAB_WIN + 128
    return pltpu.roll(w, width - k * sh, axis=1)


def _g_win_slabs(dz, h1, my_c, *, t, d, n_slab, rows, rp, tk=2048):
    sh = rows - SLAB_WIN
    tk = min(tk, t)
    nk = t // tk
    width = SLAB_WIN + 128
    half = n_slab // 2

    def slab_of(p, c):
        return 2 * (p % half) + jnp.where(p < half, 1 - c, c)

    def body(c_ref, a1_ref, a2_ref, h_ref, keep_ref, land_ref, acc, stage, send_sems, recv_sems):
        p, kk = pl.program_id(0), pl.program_id(1)
        x, y, c = _mesh_pos()
        ws = _slab_window(a1_ref, a2_ref, slab_of(p, c), sh)
        prod = lax.dot_general(ws, h_ref[...], (((0,), (0,)), ((), ())), preferred_element_type=F32)

        def push(q):
            return pltpu.make_async_remote_copy(
                src_ref=stage.at[q % 2], dst_ref=land_ref.at[q], send_sem=send_sems.at[q], recv_sem=recv_sems.at[q],
                device_id=(x, y, 1 - c), device_id_type=MESH)

        @pl.when(kk == 0)
        def _():
            acc[...] = prod

        @pl.when(kk > 0)
        def _():
            acc[...] += prod

        @pl.when(kk == nk - 1)
        def _():
            res = acc[0:rp, :].astype(BF16)
            for q in range(half):
                @pl.when(p == q)
                def _():
                    if q >= 2:
                        push(q - 2).wait_send()
                    stage[q % 2] = res
                    push(q).start()

            @pl.when(p >= half)
            def _():
                keep_ref[0] = res

        @pl.when((p == n_slab - 1) & (kk == nk - 1))
        def _():
            for q in range(max(half - 2, 0), half):
                push(q).wait_send()
            for q in range(half):
                push(q).wait_recv()

    outs = _pcall(
        body, name="mm_g_win", out_shape=(_sds((half, rp, d), BF16), _sds((half, rp, d), BF16)),
        grid_spec=pltpu.PrefetchScalarGridSpec(
            num_scalar_prefetch=1, grid=(n_slab, nk),
            in_specs=[pl.BlockSpec((tk, SLAB_WIN), lambda p, kk, c: (kk, slab_of(p, c[0]))),
                      pl.BlockSpec((tk, 128), lambda p, kk, c: (kk, (SLAB_WIN // 128) * (slab_of(p, c[0]) + 1))),
                      pl.BlockSpec((tk, d), lambda p, kk, c: (kk, 0))],
            out_specs=(pl.BlockSpec((1, rp, d), lambda p, kk, c: (jnp.maximum(p - half, 0), 0, 0)), HBM_SPEC),
            scratch_shapes=[pltpu.VMEM((width, d), F32), pltpu.VMEM((2, rp, d), BF16),
                            pltpu.SemaphoreType.DMA((half,)), pltpu.SemaphoreType.DMA((half,))]),
        compiler_params=_cp(dimension_semantics=("arbitrary", "arbitrary")),
    )(my_c.reshape(1).astype(jnp.int32), dz, dz, h1)
    return outs[0], outs[1]


def _dh1_from_slabs(dz, slabs, *, t, d, rows, comm=None, tm=1024):
    n_slab, rp, _ = slabs.shape
    sh = rows - SLAB_WIN
    width = SLAB_WIN + 128
    c_ins, c_in_specs, c_out_shape, c_scratch = _comm_lists(comm)
    grid = (t // tm, n_slab)

    def body(*refs):
        _, ins, cin, outs, cout, scr, csem = _split_refs(refs, 0, 3, len(c_ins), 1, len(c_ins), 1)
        a1_ref, a2_ref, s_ref = ins
        o_ref, acc = outs[0], scr[0]
        k = pl.program_id(1)
        step = pl.program_id(0) * n_slab + k
        _run_comm(comm, cin, cout, csem, step, grid[0] * n_slab, "start")
        ws = _slab_window(a1_ref, a2_ref, k, sh)
        s_end = jnp.concatenate([s_ref[0, SLAB_WIN:rp, :], jnp.zeros((width - rp, d), BF16)], axis=0)
        p = (jnp.dot(ws[:, 0:SLAB_WIN], s_ref[0, 0:SLAB_WIN, :], preferred_element_type=F32)
             + jnp.dot(ws[:, SLAB_WIN:width], s_end, preferred_element_type=F32))

        @pl.when(k == 0)
        def _():
            acc[...] = p

        @pl.when(k > 0)
        def _():
            acc[...] += p

        @pl.when(k == n_slab - 1)
        def _():
            o_ref[...] = acc[...]

        _run_comm(comm, cin, cout, csem, step, grid[0] * n_slab, "end")

    outs = _pcall(
        body, name="mm_dh1", out_shape=tuple([_sds((t, d), F32)] + c_out_shape), grid=grid,
        in_specs=[pl.BlockSpec((tm, SLAB_WIN), lambda i, k: (i, k)),
                  pl.BlockSpec((tm, 128), lambda i, k: (i, (SLAB_WIN // 128) * (k + 1))),
                  pl.BlockSpec((1, rp, d), lambda i, k: (k, 0, 0))] + c_in_specs,
        out_specs=tuple([pl.BlockSpec((tm, d), lambda i, k: (i, 0))] + [HBM_SPEC] * len(c_ins)),
        scratch_shapes=[pltpu.VMEM((tm, d), F32)] + c_scratch,
        compiler_params=_cp(dimension_semantics=("arbitrary", "arbitrary")),
    )(dz, dz, slabs, *c_ins)
    return outs[0], list(outs[1:])


def _rms_fwd(x, g, *, name, tm=1024):
    t, d = x.shape
    tm = min(tm, t)

    def body(x_ref, g_ref, h_ref):
        xv = x_ref[...]
        r = lax.rsqrt(jnp.mean(xv * xv, axis=-1, keepdims=True) + EPS)
        h_ref[...] = (xv * r * g_ref[...]).astype(BF16)

    return _pcall(
        body, name=name, out_shape=_sds((t, d), BF16), grid=(t // tm,),
        in_specs=[pl.BlockSpec((tm, d), lambda i: (i, 0)), pl.BlockSpec((1, d), lambda i: (0, 0))],
        out_specs=pl.BlockSpec((tm, d), lambda i: (i, 0)), compiler_params=_cp(),
    )(x, g)


def _rms_bwd(dh, x, g, res, *, name, tm=1024, with_bf16=False):
    t, d = x.shape
    tm = min(tm, t)
    has_res = res is not None

    def body(*refs):
        dh_ref, x_ref, g_ref = refs[:3]
        r_ref = refs[3] if has_res else None
        dx_ref, dg_ref = (refs[-3], refs[-1]) if with_bf16 else (refs[-2], refs[-1])
        xv = x_ref[...]
        r = lax.rsqrt(jnp.mean(xv * xv, axis=-1, keepdims=True) + EPS)
        xh = xv * r
        dhv = dh_ref[...]
        gd = dhv * g_ref[...]
        dx = r * (gd - xh * jnp.mean(gd * xh, axis=-1, keepdims=True))
        if has_res:
            dx = dx + r_ref[...]
        dx_ref[...] = dx
        if with_bf16:
            refs[-2][...] = dx.astype(BF16)

        @pl.when(pl.program_id(0) == 0)
        def _():
            dg_ref[...] = jnp.zeros_like(dg_ref)

        dg_ref[...] += jnp.sum(dhv * xh, axis=0, keepdims=True)

    row = pl.BlockSpec((tm, d), lambda i: (i, 0))
    vec = pl.BlockSpec((1, d), lambda i: (0, 0))
    out_shape = [_sds((t, d), F32)] + ([_sds((t, d), BF16)] if with_bf16 else []) + [_sds((1, d), F32)]
    return _pcall(
        body, name=name, out_shape=tuple(out_shape), grid=(t // tm,),
        in_specs=[row, row, vec] + ([row] if has_res else []), out_specs=tuple([row] * (len(out_shape) - 1) + [vec]),
        compiler_params=_cp(dimension_semantics=("arbitrary",)),
    )(*([dh, x, g] + ([res] if has_res else [])))


def _conv_fwd(z, conv_w, conv_b, *, nb, s, d, tc=512):
    nc = d // tc

    def body(cb_ref, cc_ref, cv_ref, w_ref, b_ref, y_ref):
        u = cc_ref[...].astype(F32) * cv_ref[...].astype(F32)
        row = lax.broadcasted_iota(jnp.int32, u.shape, 0)
        u1 = jnp.where(row >= 1, pltpu.roll(u, 1, axis=0), 0.0)
        u2 = jnp.where(row >= 2, pltpu.roll(u, 2, axis=0), 0.0)
        w = w_ref[...]
        y = w[2:3, :] * u + w[1:2, :] * u1 + w[0:1, :] * u2 + b_ref[...]
        y_ref[...] = (cb_ref[...].astype(F32) * y).astype(BF16)

    def part(p):
        return pl.BlockSpec((s, tc), lambda b, j: (b, p * nc + j))

    return _pcall(
        body, name="conv_fwd", out_shape=_sds((nb * s, d), BF16), grid=(nb, nc),
        in_specs=[part(0), part(1), part(2), pl.BlockSpec((3, tc), lambda b, j: (0, j)), pl.BlockSpec((1, tc), lambda b, j: (0, j))],
        out_specs=pl.BlockSpec((s, tc), lambda b, j: (b, j)), compiler_params=_cp(),
    )(z, z, z, conv_w, conv_b)


def _conv_bwd(z, dy, conv_w, conv_b, dz, *, nb, s, d, tc=256, comm=None):
    nc = d // tc
    c_ins, c_in_specs, c_out_shape, c_scratch = _comm_lists(comm)
    n_steps = nc * nb * 3

    def body(*refs):
        _, ins, cin, outs, cout, scr, csem = _split_refs(refs, 0, 7, len(c_ins), 3, len(c_ins), 1)
        cb_ref, cc_ref, cv_ref, dy_ref, w_ref, b_ref, _ = ins
        dz_ref, dw_ref, db_ref = outs
        stash = scr[0]
        b_i, p = pl.program_id(1), pl.program_id(2)
        step = (pl.program_id(0) * nb + b_i) * 3 + p
        _run_comm(comm, cin, cout, csem, step, n_steps, "start")

        @pl.when(p == 0)
        def _():
            cc, cv = cc_ref[...].astype(F32), cv_ref[...].astype(F32)
            u = cc * cv
            n = u.shape[0]
            row = lax.broadcasted_iota(jnp.int32, u.shape, 0)
            u1 = jnp.where(row >= 1, pltpu.roll(u, 1, axis=0), 0.0)
            u2 = jnp.where(row >= 2, pltpu.roll(u, 2, axis=0), 0.0)
            w = w_ref[...]
            y = w[2:3, :] * u + w[1:2, :] * u1 + w[0:1, :] * u2 + b_ref[...]
            dyv = dy_ref[...]
            dconv = dyv * cb_ref[...].astype(F32)
            g1 = jnp.where(row < n - 1, pltpu.roll(dconv, n - 1, axis=0), 0.0)
            g2 = jnp.where(row < n - 2, pltpu.roll(dconv, n - 2, axis=0), 0.0)
            du = w[2:3, :] * dconv + w[1:2, :] * g1 + w[0:1, :] * g2
            stash[0] = (dyv * y).astype(BF16)
            stash[1] = (du * cv).astype(BF16)
            stash[2] = (du * cc).astype(BF16)
            dws = [jnp.sum(dconv * u2, axis=0, keepdims=True), jnp.sum(dconv * u1, axis=0, keepdims=True),
                   jnp.sum(dconv * u, axis=0, keepdims=True)]
            db = jnp.sum(dconv, axis=0, keepdims=True)

            @pl.when(b_i == 0)
            def _():
                for r in range(3):
                    dw_ref[r:r + 1, :] = dws[r]
                db_ref[...] = db

            @pl.when(b_i > 0)
            def _():
                for r in range(3):
                    dw_ref[r:r + 1, :] += dws[r]
                db_ref[...] += db

        dz_ref[...] = stash[p]
        _run_comm(comm, cin, cout, csem, step, n_steps, "end")

    outs = _pcall(
        body, name="conv_bwd",
        out_shape=tuple([_sds(dz.shape, dz.dtype), _sds((3, d), F32), _sds((1, d), F32)] + c_out_shape), grid=(nc, nb, 3),
        in_specs=[
            pl.BlockSpec((s, tc), lambda j, b, p: (b, j)), pl.BlockSpec((s, tc), lambda j, b, p: (b, nc + j)),
            pl.BlockSpec((s, tc), lambda j, b, p: (b, 2 * nc + j)), pl.BlockSpec((s, tc), lambda j, b, p: (b, j)),
            pl.BlockSpec((3, tc), lambda j, b, p: (0, j)), pl.BlockSpec((1, tc), lambda j, b, p: (0, j)),
            HBM_SPEC,
        ] + c_in_specs,
        out_specs=tuple([pl.BlockSpec((s, tc), lambda j, b, p: (b, p * nc + j)),
                         pl.BlockSpec((3, tc), lambda j, b, p: (0, j)), pl.BlockSpec((1, tc), lambda j, b, p: (0, j))]
                        + [HBM_SPEC] * len(c_ins)),
        scratch_shapes=[pltpu.VMEM((3, s, tc), BF16)] + c_scratch,
        input_output_aliases={6: 0},
        compiler_params=_cp(dimension_semantics=("arbitrary", "arbitrary", "arbitrary")),
    )(z, z, z, dy, conv_w, conv_b, dz, *c_ins)
    return outs[0], outs[1], outs[2], list(outs[3:])


def _head_rms(xv, g):
    r = lax.rsqrt(jnp.mean(xv * xv, axis=-1, keepdims=True) + EPS)
    return xv * r * g


def _head_rms_bwd(xv, g, dy):
    r = lax.rsqrt(jnp.mean(xv * xv, axis=-1, keepdims=True) + EPS)
    xh = xv * r
    gd = dy * g
    dx = r * (gd - xh * jnp.mean(gd * xh, axis=-1, keepdims=True))
    return dx, jnp.sum(dy * xh, axis=0, keepdims=True)


def _fox_prep(z, q_g, k_g, *, t, d, tm=512):
    nh = d // HD

    def body(z_ref, qg_ref, kg_ref, q_ref, k_ref, v_ref):
        qg, kg = qg_ref[...], kg_ref[...]
        for h in range(nh):
            c0 = h * HD
            q_ref[:, c0:c0 + HD] = _head_rms(z_ref[:, c0:c0 + HD].astype(F32), qg).astype(BF16)
            k_ref[:, c0:c0 + HD] = _head_rms(z_ref[:, d + c0:d + c0 + HD].astype(F32), kg).astype(BF16)
        v_ref[...] = z_ref[:, 2 * d:3 * d]

    o = pl.BlockSpec((tm, d), lambda i: (i, 0))
    g = pl.BlockSpec((1, HD), lambda i: (0, 0))
    return _pcall(
        body, name="fox_prep", out_shape=(_sds((t, d), BF16),) * 3, grid=(t // tm,),
        in_specs=[pl.BlockSpec((tm, 3 * d), lambda i: (i, 1)), g, g], out_specs=(o, o, o), compiler_params=_cp(),
    )(z, q_g, k_g)


def _lane_cumsum(v, reverse=False):
    n = v.shape[1]
    lane = lax.broadcasted_iota(jnp.int32, v.shape, 1)
    sh = 1
    while sh < n:
        if reverse:
            v = v + jnp.where(lane < n - sh, pltpu.roll(v, n - sh, axis=1), 0.0)
        else:
            v = v + jnp.where(lane >= sh, pltpu.roll(v, sh, axis=1), 0.0)
        sh *= 2
    return v


def _fox_gates(z, f_bias_col, *, nb, s, nh):
    def body(z_ref, b_ref, c_ref):
        ff = z_ref[...].T[0:nh, :] + b_ref[...]
        lf = jnp.minimum(ff, 0.0) - jnp.log(1.0 + jnp.exp(-jnp.abs(ff)))
        c_ref[0] = _lane_cumsum(lf) * SQRT_HD

    return _pcall(
        body, name="fox_gates", out_shape=_sds((nb, nh, s), F32), grid=(nb,),
        in_specs=[pl.BlockSpec((s, HD), lambda b: (b, 0)), pl.BlockSpec((nh, 1), lambda b: (0, 0))],
        out_specs=pl.BlockSpec((1, nh, s), lambda b: (b, 0, 0)), compiler_params=_cp(),
    )(z, f_bias_col)


def _fox_gates_bwd(z, f_bias_col, row_sums, col_sums, dz, *, nb, s, nh, ff_block):
    def body(z_ref, b_ref, rs_ref, cs_ref, dz_in, dz_ref, db_ref, dc_sc):
        b_i, h = pl.program_id(0), pl.program_id(1)
        dc_sc[pl.ds(h, 1), :] = (rs_ref[...] - cs_ref[...]).T[0:1, :]

        @pl.when(h == nh - 1)
        def _():
            ff = z_ref[...].T[0:nh, :] + b_ref[...]
            dlf = _lane_cumsum(dc_sc[...], reverse=True)
            dff = dlf * (1.0 / (1.0 + jnp.exp(ff)))
            pad = jnp.concatenate([dff, jnp.zeros((HD - nh, s), F32)], axis=0)
            dz_ref[...] = pad.T.astype(BF16)
            dbv = jnp.sum(dff, axis=1, keepdims=True)

            @pl.when(b_i == 0)
            def _():
                db_ref[...] = dbv

            @pl.when(b_i > 0)
            def _():
                db_ref[...] += dbv

    hs = pl.BlockSpec((s, HD), lambda b, h: (b, h))
    return _pcall(
        body, name="fox_gates_bwd", out_shape=(_sds(dz.shape, dz.dtype), _sds((nh, 1), F32)), grid=(nb, nh),
        in_specs=[pl.BlockSpec((s, HD), lambda b, h: (b, 0)), pl.BlockSpec((nh, 1), lambda b, h: (0, 0)), hs, hs, HBM_SPEC],
        out_specs=(pl.BlockSpec((s, HD), lambda b, h: (b, ff_block)), pl.BlockSpec((nh, 1), lambda b, h: (0, 0))),
        scratch_shapes=[pltpu.VMEM((nh, s), F32)],
        input_output_aliases={4: 0}, compiler_params=_cp(dimension_semantics=("arbitrary", "arbitrary")),
    )(z, f_bias_col, row_sums, col_sums, dz)


def _fox_pairs(nq, key_major):
    if key_major:
        pairs = [(i, j) for j in range(nq) for i in range(j, nq)]
    else:
        pairs = [(i, j) for i in range(nq) for j in range(i + 1)]
    return jnp.array([p[0] for p in pairs], jnp.int32), jnp.array([p[1] for p in pairs], jnp.int32)


def _with_ones(x):
    return jnp.concatenate([x, jnp.ones_like(x)], axis=1)


def _fox_specs(nb, nh, nq, tq, tk, hp):
    qs = pl.BlockSpec((tq, hp * HD), lambda b, h, p, *tb: (b * nq + tb[0][p], h))
    ks = pl.BlockSpec((tk, hp * HD), lambda b, h, p, *tb: (b * nq + tb[1][p], h))
    cs = pl.BlockSpec((hp, 1, tk), lambda b, h, p, *tb: (b * (nh // hp) + h, 0, tb[1][p]))
    return qs, ks, cs


def _fox_raw(qv, kv, ckv, diag, tq, tk):
    sc = lax.dot_general(qv, kv, (((1,), (1,)), ((), ())), preferred_element_type=F32) - ckv
    if diag:
        rows = lax.broadcasted_iota(jnp.int32, (tq, tk), 0)
        cols = lax.broadcasted_iota(jnp.int32, (tq, tk), 1)
        sc = jnp.where(rows >= cols, sc, NEG)
    return sc


def _fox_probs(qv, kv, ckv, lse_col, diag, tq, tk):
    sc = _fox_raw(qv, kv, ckv, diag, tq, tk)
    return jnp.exp2(sc * EXP2_C - lse_col * LOG2E)


def _fox_call(body, *, name, ins, in_specs, out_shape, out_specs, scratch, grid, tables, comm):
    c_ins, c_in_specs, c_out_shape, c_scratch = _comm_lists(comm)
    n_in, n_out, n_scr, nc = len(ins), len(out_shape), len(scratch), len(c_ins)
    n_steps = grid[0] * grid[1] * grid[2]

    def wrapped(*refs):
        pre, rin, cin, rout, cout, scr, csem = _split_refs(refs, len(tables), n_in, nc, n_out, nc, n_scr)
        step = (pl.program_id(0) * grid[1] + pl.program_id(1)) * grid[2] + pl.program_id(2)
        _run_comm(comm, cin, cout, csem, step, n_steps, "start")
        body(*pre, *rin, *rout, *scr)
        _run_comm(comm, cin, cout, csem, step, n_steps, "end")

    sem = ("arbitrary",) * 3 if comm is not None else ("parallel", "parallel", "arbitrary")
    outs = _pcall(
        wrapped, name=name, out_shape=tuple(list(out_shape) + c_out_shape),
        grid_spec=pltpu.PrefetchScalarGridSpec(
            num_scalar_prefetch=len(tables), grid=grid, in_specs=list(in_specs) + c_in_specs,
            out_specs=tuple(list(out_specs) + [HBM_SPEC] * nc), scratch_shapes=list(scratch) + c_scratch),
        compiler_params=_cp(dimension_semantics=sem),
    )(*tables, *ins, *c_ins)
    return list(outs[:n_out]), list(outs[n_out:])


def _fox_fwd(q, k, v, cks, *, nb, s, nh, tq=512, comm=None):
    tk = tq
    nq = s // tq
    t = nb * s
    hp = FOX_HP
    it, jt = _fox_pairs(nq, key_major=False)

    def body(it_ref, jt_ref, q_ref, k_ref, v_ref, c_ref, o_ref, lse_ref, m_sc, acc_sc):
        p_id = pl.program_id(2)
        i, j = it_ref[p_id], jt_ref[p_id]

        @pl.when(j == 0)
        def _():
            m_sc[...] = jnp.full_like(m_sc, NEG)
            acc_sc[...] = jnp.zeros_like(acc_sc)

        def step(diag):
            for hh in range(hp):
                cols = slice(hh * HD, (hh + 1) * HD)
                sc = _fox_raw(q_ref[:, cols], k_ref[:, cols], c_ref[hh], diag, tq, tk)
                m_old = m_sc[hh]
                m_new = jnp.maximum(m_old, jnp.max(sc, axis=-1, keepdims=True))
                alpha = jnp.exp2((m_old - m_new) * EXP2_C)
                p = jnp.exp2((sc - m_new) * EXP2_C).astype(BF16)
                acc = alpha * acc_sc[hh] + jnp.dot(p, _with_ones(v_ref[:, cols]), preferred_element_type=F32)
                if diag:
                    l = acc[:, HD:2 * HD]
                    o_ref[:, cols] = (acc[:, 0:HD] / l).astype(BF16)
                    lse_ref[:, cols] = m_new * (1.0 / SQRT_HD) + jnp.log(l)
                else:
                    acc_sc[hh] = acc
                    m_sc[hh] = m_new

        @pl.when(j < i)
        def _():
            step(False)

        @pl.when(j == i)
        def _():
            step(True)

    qs, ks, cs = _fox_specs(nb, nh, nq, tq, tk, hp)
    (o, lse), c_out = _fox_call(
        body, name="fox_fwd", ins=[q, k, v, cks], in_specs=[qs, ks, ks, cs],
        out_shape=[_sds((t, nh * HD), BF16), _sds((t, nh * HD), F32)], out_specs=[qs, qs],
        scratch=[pltpu.VMEM((hp, tq, 1), F32), pltpu.VMEM((hp, tq, 2 * HD), F32)],
        grid=(nb, nh // hp, int(it.shape[0])), tables=(it, jt), comm=comm)
    return o, lse, c_out


def _fox_bwd(q, k, v, cks, do, o, lse, *, nb, s, nh, tq=512, comm=None):
    tk = tq
    nq = s // tq
    t = nb * s
    hp = FOX_HP
    it, jt = _fox_pairs(nq, key_major=True)
    done = jnp.array([j for j in range(nq) for _ in range(j, nq)], jnp.int32)

    def body(it_ref, jt_ref, done_ref, q_ref, k_ref, v_ref, c_ref, do_ref, o_ref, lse_ref,
             dq_ref, rs_ref, dk_ref, dv_ref, cs_ref, dq_sc, dk_sc, dv_sc):
        p_id = pl.program_id(2)
        i, j = it_ref[p_id], jt_ref[p_id]
        rows_i = pl.ds(pl.multiple_of(i * tq, tq), tq)

        def step(diag):
            for hh in range(hp):
                cols = slice(hh * HD, (hh + 1) * HD)
                qv, kv = q_ref[:, cols], k_ref[:, cols]
                p = _fox_probs(qv, kv, c_ref[hh], lse_ref[:, hh * HD:hh * HD + 1], diag, tq, tk)
                dov = do_ref[:, cols]
                dob = dov.astype(BF16)
                dvp = lax.dot_general(p.astype(BF16), dob, (((0,), (0,)), ((), ())), preferred_element_type=F32)
                dp = lax.dot_general(dob, v_ref[:, cols], (((1,), (1,)), ((), ())), preferred_element_type=F32)
                delta = jnp.sum(dov * o_ref[:, cols].astype(F32), axis=-1, keepdims=True)
                ds = (p * (dp - delta)).astype(BF16)
                dkp = lax.dot_general(ds, _with_ones(qv), (((0,), (0,)), ((), ())), preferred_element_type=F32)
                dqp = jnp.dot(ds, _with_ones(kv), preferred_element_type=F32)
                if diag:
                    dk_sc[hh] = dkp
                    dv_sc[hh] = dvp
                else:
                    dk_sc[hh] += dkp
                    dv_sc[hh] += dvp

                @pl.when(j == 0)
                def _():
                    dq_sc[hh, rows_i, :] = dqp

                @pl.when(j > 0)
                def _():
                    dq_sc[hh, rows_i, :] += dqp

                if diag:
                    acc = dq_sc[hh, rows_i, :]
                    dq_ref[:, cols] = acc[:, 0:HD] * (1.0 / SQRT_HD)
                    rs_ref[:, cols] = acc[:, HD:2 * HD]

        @pl.when(i == j)
        def _():
            step(True)

        @pl.when(i > j)
        def _():
            step(False)

        @pl.when(i == nq - 1)
        def _():
            for hh in range(hp):
                cols = slice(hh * HD, (hh + 1) * HD)
                dk_ref[:, cols] = dk_sc[hh, :, 0:HD] * (1.0 / SQRT_HD)
                cs_ref[:, cols] = dk_sc[hh, :, HD:2 * HD]
                dv_ref[:, cols] = dv_sc[hh].astype(BF16)

    qs, ks, cs = _fox_specs(nb, nh, nq, tq, tk, hp)
    ds_spec = pl.BlockSpec((tq, hp * HD), lambda b, h, p, *tb: (b * nq + tb[2][p], h))
    (dq, rs, dk, dv, csum), c_out = _fox_call(
        body, name="fox_bwd", ins=[q, k, v, cks, do, o, lse], in_specs=[qs, ks, ks, cs, qs, qs, qs],
        out_shape=[_sds((t, nh * HD), F32), _sds((t, nh * HD), F32), _sds((t, nh * HD), F32), _sds((t, nh * HD), BF16),
                   _sds((t, nh * HD), F32)],
        out_specs=[ds_spec, ds_spec, ks, ks, ks],
        scratch=[pltpu.VMEM((hp, s, 2 * HD), F32), pltpu.VMEM((hp, tk, 2 * HD), F32), pltpu.VMEM((hp, tk, HD), F32)],
        grid=(nb, nh // hp, int(it.shape[0])), tables=(it, jt, done), comm=comm)
    return dq, rs, dk, dv, csum, c_out


def _fox_post(z, dqn, dkn, dv, q_g, k_g, dz, *, t, d, tm=512):
    nh = d // HD

    def body(z_ref, dq_ref, dk_ref, dv_ref, qg_ref, kg_ref, dz_in, dz_ref, dqg_ref, dkg_ref):
        qg, kg = qg_ref[...], kg_ref[...]
        aq = jnp.zeros((1, HD), F32)
        ak = jnp.zeros((1, HD), F32)
        for h in range(nh):
            c0 = h * HD
            dx, sg = _head_rms_bwd(z_ref[:, c0:c0 + HD].astype(F32), qg, dq_ref[:, c0:c0 + HD])
            dz_ref[:, c0:c0 + HD] = dx.astype(BF16)
            aq = aq + sg
            dx, sg = _head_rms_bwd(z_ref[:, d + c0:d + c0 + HD].astype(F32), kg, dk_ref[:, c0:c0 + HD])
            dz_ref[:, d + c0:d + c0 + HD] = dx.astype(BF16)
            ak = ak + sg
        dz_ref[:, 2 * d:3 * d] = dv_ref[...]

        @pl.when(pl.program_id(0) == 0)
        def _():
            dqg_ref[...] = aq
            dkg_ref[...] = ak

        @pl.when(pl.program_id(0) > 0)
        def _():
            dqg_ref[...] += aq
            dkg_ref[...] += ak

    o = pl.BlockSpec((tm, d), lambda i: (i, 0))
    g = pl.BlockSpec((1, HD), lambda i: (0, 0))
    z3 = pl.BlockSpec((tm, 3 * d), lambda i: (i, 1))
    return _pcall(
        body, name="fox_post", out_shape=(_sds(dz.shape, dz.dtype), _sds((1, HD), F32), _sds((1, HD), F32)), grid=(t // tm,),
        in_specs=[z3, o, o, o, g, g, HBM_SPEC], out_specs=(z3, g, g),
        input_output_aliases={6: 0}, compiler_params=_cp(dimension_semantics=("arbitrary",)),
    )(z, dqn, dkn, dv, q_g, k_g, dz)


def _xa_prep_kv(kv, k_g, *, rows, d):
    xd = d // XH

    def body(kv_ref, g_ref, k_ref, v_ref):
        g = g_ref[...]
        for h in range(XH):
            c0 = h * xd
            k_ref[:, c0:c0 + xd] = _head_rms(kv_ref[:, c0:c0 + xd], g).astype(BF16)
        v_ref[...] = kv_ref[:, d:2 * d].astype(BF16)

    return _pcall(body, name="xa_prep_kv", out_shape=(_sds((rows, d), BF16),) * 2, compiler_params=_cp())(kv, k_g)


def _xa_fwd(z, kn, vb, q_g, *, nb, s, d, nm, q_block, tq=1024):
    xd = d // XH
    tq = min(tq, s)
    nq = s // tq

    def body(z_ref, k_ref, v_ref, g_ref, y_ref):
        g = g_ref[...]
        for h in range(XH):
            c0 = h * xd
            qn = _head_rms(z_ref[:, c0:c0 + xd].astype(F32), g).astype(BF16)
            sc = lax.dot_general(qn, k_ref[:, c0:c0 + xd], (((1,), (1,)), ((), ())), preferred_element_type=F32)
            sc = sc / math.sqrt(xd)
            p = jnp.exp(sc - jnp.max(sc, axis=-1, keepdims=True))
            p = p / jnp.sum(p, axis=-1, keepdims=True)
            y_ref[:, c0:c0 + xd] = jnp.dot(p.astype(BF16), v_ref[:, c0:c0 + xd], preferred_element_type=F32).astype(BF16)

    kvs = pl.BlockSpec((nm, d), lambda b, i: (b, 0))
    return _pcall(
        body, name="xa_fwd", out_shape=_sds((nb * s, d), BF16), grid=(nb, nq),
        in_specs=[pl.BlockSpec((tq, d), lambda b, i: (b * nq + i, q_block)), kvs, kvs, pl.BlockSpec((1, xd), lambda b, i: (0, 0))],
        out_specs=pl.BlockSpec((tq, d), lambda b, i: (b * nq + i, 0)), compiler_params=_cp(),
    )(z, kn, vb, q_g)


def _xa_bwd(z, kn, vb, q_g, dy, dz, *, nb, s, d, nm, q_block, tq=1024):
    xd = d // XH
    tq = min(tq, s)
    nq = s // tq

    def body(z_ref, k_ref, v_ref, g_ref, dy_ref, dz_in, dz_ref, dk_ref, dv_ref, dg_ref):
        b_i, i = pl.program_id(0), pl.program_id(1)
        g = g_ref[...]

        @pl.when(i == 0)
        def _():
            dk_ref[...] = jnp.zeros_like(dk_ref)
            dv_ref[...] = jnp.zeros_like(dv_ref)

        @pl.when((i == 0) & (b_i == 0))
        def _():
            dg_ref[...] = jnp.zeros_like(dg_ref)

        for h in range(XH):
            c0 = h * xd
            xq = z_ref[:, c0:c0 + xd].astype(F32)
            qn = _head_rms(xq, g).astype(BF16)
            kh, vh = k_ref[:, c0:c0 + xd], v_ref[:, c0:c0 + xd]
            sc = lax.dot_general(qn, kh, (((1,), (1,)), ((), ())), preferred_element_type=F32) / math.sqrt(xd)
            p = jnp.exp(sc - jnp.max(sc, axis=-1, keepdims=True))
            p = p / jnp.sum(p, axis=-1, keepdims=True)
            dob = dy_ref[:, c0:c0 + xd].astype(BF16)
            dv_ref[:, c0:c0 + xd] += lax.dot_general(p.astype(BF16), dob, (((0,), (0,)), ((), ())), preferred_element_type=F32)
            dp = lax.dot_general(dob, vh, (((1,), (1,)), ((), ())), preferred_element_type=F32)
            ds = p * (dp - jnp.sum(p * dp, axis=-1, keepdims=True)) / math.sqrt(xd)
            dsb = ds.astype(BF16)
            dqn = jnp.dot(dsb, kh, preferred_element_type=F32)
            dk_ref[:, c0:c0 + xd] += lax.dot_general(dsb, qn, (((0,), (0,)), ((), ())), preferred_element_type=F32)
            dx, sg = _head_rms_bwd(xq, g, dqn)
            dz_ref[:, c0:c0 + xd] = dx.astype(BF16)
            dg_ref[...] += sg

    kvs = pl.BlockSpec((nm, d), lambda b, i: (b, 0))
    qs = pl.BlockSpec((tq, d), lambda b, i: (b * nq + i, q_block))
    gs = pl.BlockSpec((1, xd), lambda b, i: (0, 0))
    return _pcall(
        body, name="xa_bwd",
        out_shape=(_sds(dz.shape, dz.dtype), _sds((nb * nm, d), F32), _sds((nb * nm, d), F32), _sds((1, xd), F32)), grid=(nb, nq),
        in_specs=[qs, kvs, kvs, gs, pl.BlockSpec((tq, d), lambda b, i: (b * nq + i, 0)), HBM_SPEC],
        out_specs=(qs, kvs, kvs, gs), input_output_aliases={5: 0},
        compiler_params=_cp(dimension_semantics=("arbitrary", "arbitrary")),
    )(z, kn, vb, q_g, dy, dz)


def _xa_post_kv(kv, dkn, dv, k_g, *, rows, d):
    xd = d // XH

    def body(kv_ref, dk_ref, dv_ref, g_ref, dkv_ref, dg_ref):
        g = g_ref[...]
        acc = jnp.zeros((1, xd), F32)
        for h in range(XH):
            c0 = h * xd
            dx, sg = _head_rms_bwd(kv_ref[:, c0:c0 + xd], g, dk_ref[:, c0:c0 + xd])
            dkv_ref[:, c0:c0 + xd] = dx.astype(BF16)
            acc = acc + sg
        dkv_ref[:, d:2 * d] = dv_ref[...].astype(BF16)
        dg_ref[...] = acc

    return _pcall(body, name="xa_post_kv", out_shape=(_sds((rows, 2 * d), BF16), _sds((1, xd), F32)), compiler_params=_cp())(
        kv, dkn, dv, k_g)


def _sigmoid(v):
    return 1.0 / (1.0 + jnp.exp(-v))


def _branches_fwd(z, ys, ws, *, t, d, gate_block, tm=1024, tn=512, comm=None):
    nj = d // tn
    c_ins, c_in_specs, c_out_shape, c_scratch = _comm_lists(comm)
    n_steps = (t // tm) * nj

    def body(*refs):
        _, ins, cin, outs, cout, _, csem = _split_refs(refs, 0, 9, len(c_ins), 2, len(c_ins), 0)
        ya_ref, yb_ref, yc_ref, wa_ref, wb_ref, wc_ref, ga_ref, gb_ref, gc_ref = ins
        m_ref, p_ref = outs
        step = pl.program_id(0) * nj + pl.program_id(1)
        _run_comm(comm, cin, cout, csem, step, n_steps, "start")
        acc = None
        for q, (y_ref, w_ref, g_ref) in enumerate(((ya_ref, wa_ref, ga_ref), (yb_ref, wb_ref, gb_ref), (yc_ref, wc_ref, gc_ref))):
            pr = jnp.dot(y_ref[...], w_ref[...], preferred_element_type=F32)
            p_ref[q] = pr.astype(BF16)
            term = _sigmoid(g_ref[...].astype(F32)) * pr
            acc = term if acc is None else acc + term
        m_ref[...] = acc.astype(BF16)
        _run_comm(comm, cin, cout, csem, step, n_steps, "end")

    y_spec = pl.BlockSpec((tm, d), lambda i, j: (i, 0))
    w_spec = pl.BlockSpec((d, tn), lambda i, j: (0, j))

    def gate(q):
        return pl.BlockSpec((tm, tn), lambda i, j: (i, (gate_block + q) * nj + j))

    outs = _pcall(
        body, name="branches_fwd", out_shape=tuple([_sds((t, d), BF16), _sds((3, t, d), BF16)] + c_out_shape), grid=(t // tm, nj),
        in_specs=[y_spec] * 3 + [w_spec] * 3 + [gate(0), gate(1), gate(2)] + c_in_specs,
        out_specs=tuple([pl.BlockSpec((tm, tn), lambda i, j: (i, j)), pl.BlockSpec((3, tm, tn), lambda i, j: (0, i, j))]
                        + [HBM_SPEC] * len(c_ins)),
        scratch_shapes=c_scratch, compiler_params=_cp(dimension_semantics=("arbitrary", "arbitrary")),
    )(*ys, *ws, z, z, z, *c_ins)
    return outs[0], outs[1], list(outs[2:])


def _gates_bwd(z, dm, proj, dz, *, t, d, gate_block, tm=1024):
    def body(g_ref, dm_ref, p_ref, dz_in, dz_ref, da_ref, db_ref, dc_ref):
        q = pl.program_id(1)
        sg = _sigmoid(g_ref[...].astype(F32))
        dmv = dm_ref[...]
        dz_ref[...] = (dmv * p_ref[0].astype(F32) * sg * (1.0 - sg)).astype(BF16)
        dp = (dmv * sg).astype(BF16)
        @pl.when(q == 0)
        def _():
            da_ref[...] = dp

        @pl.when(q == 1)
        def _():
            db_ref[...] = dp

        @pl.when(q == 2)
        def _():
            dc_ref[...] = dp

    gs = pl.BlockSpec((tm, d), lambda i, q: (i, gate_block + q))
    o = pl.BlockSpec((tm, d), lambda i, q: (i, 0))
    return _pcall(
        body, name="gates_bwd", out_shape=(_sds(dz.shape, dz.dtype),) + (_sds((t, d), BF16),) * 3, grid=(t // tm, 3),
        in_specs=[gs, o, pl.BlockSpec((1, tm, d), lambda i, q: (q, i, 0)), HBM_SPEC], out_specs=(gs, o, o, o),
        input_output_aliases={3: 0}, compiler_params=_cp(dimension_semantics=("arbitrary", "arbitrary")),
    )(z, dm, proj, dz)


def _ffn_in_fwd(h2, wfi_t, *, t, d, dff, tm=512, tn=1408):
    nj = dff // tn

    def body(h_ref, wg_ref, wu_ref, g_ref, u_ref, a_ref):
        hv = h_ref[...]
        dn = (((1,), (1,)), ((), ()))
        gv = lax.dot_general(hv, wg_ref[...], dn, preferred_element_type=F32)
        uv = lax.dot_general(hv, wu_ref[...], dn, preferred_element_type=F32)
        g_ref[...] = gv.astype(BF16)
        u_ref[...] = uv.astype(BF16)
        a_ref[...] = (gv * _sigmoid(gv) * uv).astype(BF16)

    o = pl.BlockSpec((tm, tn), lambda i, j: (i, j))
    return _pcall(
        body, name="ffn_in_fwd", out_shape=(_sds((t, dff), BF16),) * 3, grid=(t // tm, nj),
        in_specs=[pl.BlockSpec((tm, d), lambda i, j: (i, 0)), pl.BlockSpec((tn, d), lambda i, j: (j, 0)),
                  pl.BlockSpec((tn, d), lambda i, j: (nj + j, 0))],
        out_specs=(o, o, o), compiler_params=_cp(),
    )(h2, wfi_t, wfi_t)


def _ffn_out_bwd(dy, wfo, gate, up, *, t, d, dff, tm=512, tn=1408):
    def body(dy_ref, w_ref, g_ref, u_ref, dg_ref, du_ref):
        dav = lax.dot_general(dy_ref[...].astype(BF16), w_ref[...], (((1,), (1,)), ((), ())), preferred_element_type=F32)
        gv = g_ref[...].astype(F32)
        sg = _sigmoid(gv)
        dg_ref[...] = (dav * u_ref[...].astype(F32) * sg * (1.0 + gv * (1.0 - sg))).astype(BF16)
        du_ref[...] = (dav * gv * sg).astype(BF16)

    o = pl.BlockSpec((tm, tn), lambda i, j: (i, j))
    return _pcall(
        body, name="ffn_out_bwd", out_shape=(_sds((t, dff), BF16),) * 2, grid=(t // tm, dff // tn),
        in_specs=[pl.BlockSpec((tm, d), lambda i, j: (i, 0)), pl.BlockSpec((tn, d), lambda i, j: (j, 0)), o, o],
        out_specs=(o, o), compiler_params=_cp(),
    )(dy, wfo, gate, up)


def _ffn_out_loss(act, wfo, x1, target, *, t, d, dff, tm=512):
    def body(a_ref, w_ref, x_ref, t_ref, dy_ref, dyb_ref, l_ref):
        yv = x_ref[...] + jnp.dot(a_ref[...], w_ref[...], preferred_element_type=F32)
        e = yv - t_ref[...]
        dyv = e * (1.0 / d)
        dy_ref[...] = dyv
        dyb_ref[...] = dyv.astype(BF16)

        @pl.when(pl.program_id(0) == 0)
        def _():
            l_ref[...] = jnp.zeros_like(l_ref)

        l_ref[...] += jnp.sum(jnp.sum(e * e, axis=-1, keepdims=True), axis=0, keepdims=True) * (0.5 / d)

    o = pl.BlockSpec((tm, d), lambda i: (i, 0))
    dy, dyb, l = _pcall(
        body, name="ffn_out_loss", out_shape=(_sds((t, d), F32), _sds((t, d), BF16), _sds((1, HD), F32)), grid=(t // tm,),
        in_specs=[pl.BlockSpec((tm, dff), lambda i: (i, 0)), pl.BlockSpec((dff, d), lambda i: (0, 0)), o, o],
        out_specs=(o, o, pl.BlockSpec((1, HD), lambda i: (0, 0))),
        compiler_params=_cp(dimension_semantics=("arbitrary",)),
    )(act, wfo, x1, target)
    return dy, dyb, l[0, 0]


def _adamw(w, g, m, v, *, name):
    if w.ndim == 3:
        rows, _, cols = w.shape
        tm = max(tt for tt in range(1, 65) if rows % tt == 0)
        block, imap, grid = (tm, 1, cols), (lambda i, j: (i, 0, 0)), (rows // tm, 1)
    else:
        rows, cols = w.shape
        tm, tn = rows, cols
        if rows > 512 and rows % 8 == 0:
            tm = max(tt for tt in range(8, 513, 8) if rows % tt == 0)
        elif rows > 512 and cols % 128 == 0:
            tn = 128
        block, imap, grid = (tm, tn), (lambda i, j: (i, j)), (rows // tm, cols // tn)
    c1 = 1.0 - ADAM_B1 ** ADAM_STEP
    c2 = 1.0 - ADAM_B2 ** ADAM_STEP

    at = (slice(None), 0, slice(None)) if w.ndim == 3 else Ellipsis

    def body(w_ref, g_ref, m_ref, v_ref, d_ref, nm_ref, nv_ref):
        gv = g_ref[at]
        mn = ADAM_B1 * m_ref[at] + (1.0 - ADAM_B1) * gv
        vn = ADAM_B2 * v_ref[at] + (1.0 - ADAM_B2) * (gv * gv)
        d_ref[at] = -ADAM_LR * ((mn / c1) / (jnp.sqrt(vn / c2) + ADAM_EPS) + ADAM_WD * w_ref[at])
        nm_ref[at] = mn
        nv_ref[at] = vn

    spec = pl.BlockSpec(block, imap)
    return _pcall(
        body, name=name, out_shape=(_sds(w.shape, F32),) * 3, grid=grid,
        in_specs=[spec] * 4, out_specs=(spec,) * 3, compiler_params=_cp(),
    )(w, g, m, v)


def _pad_rows(a, rows):
    return a if a.shape[0] == rows else jnp.pad(a, ((0, rows - a.shape[0]),) + ((0, 0),) * (a.ndim - 1))


class _Pack:
    def __init__(self, row_sizes):
        self.rows = list(row_sizes)
        self.padded = [-(-r // 16) * 16 for r in self.rows]
        self.offs = [sum(self.padded[:i]) for i in range(len(self.rows))]
        self.total = sum(self.padded)

    def stack(self, blocks):
        return jnp.concatenate([_pad_rows(b.astype(BF16), p) for b, p in zip(blocks, self.padded)], axis=0)

    def full(self, gathered, i):
        r = self.rows[i]
        return gathered[:, self.offs[i]:self.offs[i] + r, :].reshape(N_DEV * r, gathered.shape[2])

    def for_core(self, full_grads, core):
        parts = []
        for gfull, r, p in zip(full_grads, self.rows, self.padded):
            pieces = gfull if isinstance(gfull, (tuple, list)) else (gfull,)
            blk = jnp.concatenate(
                [lax.dynamic_index_in_dim(g.reshape(-1, 2, r, g.shape[1]), core, axis=1, keepdims=False) for g in pieces],
                axis=0).astype(BF16)
            if p != r:
                blk = jnp.pad(blk, ((0, 0), (0, p - r), (0, 0)))
            parts.append(blk)
        return jnp.concatenate(parts, axis=1)

    def shard(self, g_pack, i):
        return g_pack[self.offs[i]:self.offs[i] + self.rows[i], :]


def kernel(x, mem, norm1_g, w_in, conv_w, conv_b, fox_f_bias, fox_q_g, fox_k_g, mem_norm_g, w_mem_kv, xa_q_g, xa_k_g, w_br_conv, w_br_fox, w_br_xa, w_o, norm2_g, w_ffn_in, w_ffn_out, loss_target, m_norm1_g, m_w_in, m_conv_w, m_conv_b, m_fox_f_bias, m_fox_q_g, m_fox_k_g, m_mem_norm_g, m_w_mem_kv, m_xa_q_g, m_xa_k_g, m_w_br_conv, m_w_br_fox, m_w_br_xa, m_w_o, m_norm2_g, m_w_ffn_in, m_w_ffn_out, v_norm1_g, v_w_in, v_conv_w, v_conv_b, v_fox_f_bias, v_fox_q_g, v_fox_k_g, v_mem_norm_g, v_w_mem_kv, v_xa_q_g, v_xa_k_g, v_w_br_conv, v_w_br_fox, v_w_br_xa, v_w_o, v_norm2_g, v_w_ffn_in, v_w_ffn_out):
    nb, s, d = x.shape
    t = nb * s
    nm = mem.shape[1]
    nh = d // HD
    dff = w_ffn_out.shape[1] * N_DEV
    n_in = w_in.shape[2] * N_DEV
    n_in_pad = -(-n_in // 1152) * 1152
    ff_block = (10 * d) // HD
    my_c = lax.axis_index("c")
    my_q = 2 * lax.axis_index("x") + lax.axis_index("y")

    pack_a = _Pack([w_in.shape[2]])
    pack_b = _Pack([w_mem_kv.shape[2], w_br_conv.shape[1], w_br_fox.shape[1], w_br_xa.shape[1], w_o.shape[1],
                    w_ffn_in.shape[2], w_ffn_out.shape[1]])
    gather_2 = _Pack([w_mem_kv.shape[2], w_br_conv.shape[1], w_br_fox.shape[1], w_br_xa.shape[1], w_o.shape[1]])
    stack_a = pack_a.stack([w_in[0].T])
    stack_2 = gather_2.stack([w_mem_kv[0].T, w_br_conv[0], w_br_fox[0], w_br_xa[0], w_o[0]])
    stack_fi = w_ffn_in[0].T.astype(BF16)
    stack_fo = w_ffn_out[0].astype(BF16)
    cw_block = _pad_rows(conv_w[0], 8)
    rows_a = w_in.shape[2]
    my_x, my_y = lax.axis_index("x"), lax.axis_index("y")
    arrival = [(my_x, my_y, my_c), (my_x, my_y, 1 - my_c)]
    chip_1, chip_2, chip_d = _chip_order(my_x, my_y, my_c)
    arrival += [(*chip_1, my_c), (*chip_2, 1 - my_c), (*chip_2, my_c), (*chip_1, 1 - my_c), (*chip_d, my_c), (*chip_d, 1 - my_c)]
    order = jnp.stack([4 * ax + 2 * ay + ac for ax, ay, ac in arrival]).astype(jnp.int32)

    x2 = x.reshape(t, d)
    mem2 = mem.reshape(nb * nm, d)
    tgt2 = loss_target.reshape(t, d)
    h1 = _rms_fwd(x2, norm1_g, name="rms1_fwd")
    z, slabs, _ = _z_gathered(h1, stack_a, order, t=t, d=d, n_pad=N_DEV * SLAB_WIN, rows=rows_a)
    z = _z_fix(h1, slabs, z, t=t, d=d, rows=rows_a)
    ff_rows = _pad_rows(slabs[N_DEV - 1, rows_a - nh:rows_a, :], HD)
    zff = _mm(h1, ff_rows, m=t, n=HD, k=d, tb=True, tm=1024, name="mm_z_ff")
    qn, kn, vb = _fox_prep(z, fox_q_g, fox_k_g, t=t, d=d)
    f_bias_col = fox_f_bias.reshape(nh, 1)
    ck_rows = _fox_gates(zff, f_bias_col, nb=nb, s=s, nh=nh).reshape(nb * nh, 1, s)
    y_fox, lse, (gathered_2, cw_g, gathered_fo) = _fox_fwd(qn, kn, vb, ck_rows, nb=nb, s=s, nh=nh,
                                                           comm=_AllGather([stack_2, cw_block, stack_fo], forward_at=0.8))
    wkv_t, wbc, wbf, wbx, wo = (gather_2.full(gathered_2, i) for i in range(5))
    wfo = gathered_fo.reshape(dff, d)
    conv_w_full = jnp.transpose(cw_g[:, 0:3, :], (1, 0, 2)).reshape(3, d)

    y_conv = _conv_fwd(z, conv_w_full, conv_b, nb=nb, s=s, d=d)
    mem_n = _rms_fwd(mem2, mem_norm_g, name="rms_mem_fwd")
    kv = _mm(mem_n, wkv_t, m=nb * nm, n=2 * d, k=d, tb=True, name="mm_kv")
    xkn, xvb = _xa_prep_kv(kv, xa_k_g, rows=nb * nm, d=d)
    y_xa = _xa_fwd(z, xkn, xvb, xa_q_g, nb=nb, s=s, d=d, nm=nm, q_block=6)
    merged, proj, (gathered_fi,) = _branches_fwd(z, (y_conv, y_fox, y_xa), (wbc, wbf, wbx), t=t, d=d, gate_block=7,
                                                 comm=_AllGather([stack_fi]))
    wfi_t = gathered_fi.reshape(2 * dff, d)
    x1 = _mm(merged, wo, m=t, n=d, k=d, res=x2, tm=1024, name="mm_x1")
    h2 = _rms_fwd(x1, norm2_g, name="rms2_fwd")
    ff_gate, ff_up, act = _ffn_in_fwd(h2, wfi_t, t=t, d=d, dff=dff)
    dy, dy_b, loss_local = _ffn_out_loss(act, wfo, x1, tgt2, t=t, d=d, dff=dff)

    g_wfo = _mm(act, dy_b, m=dff, n=d, k=t, ta=True, tm=1408, tk=2048, out_dtype=BF16, name="mm_g_wfo")
    dgate, dup = _ffn_out_bwd(dy_b, wfo, ff_gate, ff_up, t=t, d=d, dff=dff)
    dh2 = _mm(dgate, wfi_t, m=t, n=d, k=dff, tk=dff, name="mm_dh2_g")
    dh2 = _mm(dup, wfi_t, m=t, n=d, k=dff, tk=dff, b_off=(1, 0), res=dh2, name="mm_dh2_u")
    g_wfi_g = _mm(dgate, h2, m=dff, n=d, k=t, ta=True, tm=1408, tk=2048, out_dtype=BF16, name="mm_g_wfi_g")
    g_wfi_u = _mm(dup, h2, m=dff, n=d, k=t, ta=True, tm=1408, tk=2048, out_dtype=BF16, name="mm_g_wfi_u")
    dx1, dx1_b, g_norm2 = _rms_bwd(dh2, x1, norm2_g, dy, name="rms2_bwd", with_bf16=True)
    dmerged = _mm(dx1_b, wo, m=t, n=d, k=d, tb=True, tm=1024, name="mm_dmerged")
    g_wo = _mm(merged, dx1_b, m=d, n=d, k=t, ta=True, tk=2048, out_dtype=BF16, name="mm_g_wo")

    dz = lax.empty((t, n_in_pad), BF16)
    dz, dpa, dpb, dpc = _gates_bwd(z, dmerged, proj, dz, t=t, d=d, gate_block=7)
    dy_conv = _mm(dpa, wbc, m=t, n=d, k=d, tb=True, tm=1024, name="mm_dy_conv")
    g_wbc = _mm(y_conv, dpa, m=d, n=d, k=t, ta=True, tk=4096, out_dtype=BF16, name="mm_g_wbc")
    dy_fox = _mm(dpb, wbf, m=t, n=d, k=d, tb=True, tm=1024, name="mm_dy_fox")
    g_wbf = _mm(y_fox, dpb, m=d, n=d, k=t, ta=True, tk=4096, out_dtype=BF16, name="mm_g_wbf")
    dy_xa = _mm(dpc, wbx, m=t, n=d, k=d, tb=True, tm=1024, out_dtype=BF16, name="mm_dy_xa")
    g_wbx = _mm(y_xa, dpc, m=d, n=d, k=t, ta=True, tk=4096, out_dtype=BF16, name="mm_g_wbx")

    dz, dxkn, dxv, g_xa_q = _xa_bwd(z, xkn, xvb, xa_q_g, dy_xa, dz, nb=nb, s=s, d=d, nm=nm, q_block=6)
    dkv, g_xa_k = _xa_post_kv(kv, dxkn, dxv, xa_k_g, rows=nb * nm, d=d)
    g_wkv_t = _mm(dkv, mem_n, m=2 * d, n=d, k=nb * nm, ta=True, out_dtype=BF16, name="mm_g_wkv")
    dmem_n = _mm(dkv, wkv_t, m=nb * nm, n=d, k=2 * d, tk=2 * d, name="mm_dmem")
    _, g_mem_norm = _rms_bwd(dmem_n, mem2, mem_norm_g, None, name="rms_mem_bwd")

    grads_b = [g_wkv_t, g_wbc, g_wbf, g_wbx, g_wo, (g_wfi_g, g_wfi_u), g_wfo]
    keep_b, send_b = pack_b.for_core(grads_b, my_c), pack_b.for_core(grads_b, 1 - my_c)
    dz, g_conv_w, g_conv_b, (from_sibling_b,) = _conv_bwd(z, dy_conv, conv_w_full, conv_b, dz, nb=nb, s=s, d=d,
                                                          comm=_SiblingSwap(send_b))
    part_b = _add2(keep_b, from_sibling_b, name="rs_add_sibling_b")
    dqn, ds_rows, dkn, dvb, ds_cols, (from_chips_b,) = _fox_bwd(qn, kn, vb, ck_rows, dy_fox, y_fox, lse, nb=nb, s=s, nh=nh,
                                                                comm=_ChipExchange(part_b))
    g_pack_b = _add4(lax.dynamic_index_in_dim(part_b, my_q, axis=0, keepdims=False), from_chips_b, name="rs_add_chips_b")
    dz, g_fox_q, g_fox_k = _fox_post(z, dqn, dkn, dvb, fox_q_g, fox_k_g, dz, t=t, d=d)
    dz, g_f_bias = _fox_gates_bwd(zff, f_bias_col, ds_rows, ds_cols, dz, nb=nb, s=s, nh=nh, ff_block=ff_block)

    keep_a, from_sibling_a = _g_win_slabs(dz, h1, my_c, t=t, d=d, n_slab=N_DEV, rows=rows_a, rp=slabs.shape[1])
    part_a = _add2(keep_a, from_sibling_a, name="rs_add_sibling_a")
    dh1, (from_chips_a,) = _dh1_from_slabs(dz, slabs, t=t, d=d, rows=rows_a, comm=_ChipExchange(part_a))
    g_pack_a = _add4(lax.dynamic_index_in_dim(part_a, my_q, axis=0, keepdims=False), from_chips_a, name="rs_add_chips_a",
                     rows_apart=True)
    dx, g_norm1 = _rms_bwd(dh1, x2, norm1_g, dx1, name="rms1_bwd")

    xd = d // XH
    tail = jnp.concatenate(
        [_pad_rows(g_f_bias.reshape(nh), HD).reshape(1, HD), g_fox_q, g_fox_k, g_xa_q, g_xa_k,
         jnp.zeros((1, d - 3 * HD - 2 * xd), F32)], axis=1)
    small = jnp.concatenate([g_norm1, g_norm2, g_conv_w, g_conv_b, g_mem_norm, tail], axis=0)
    small = _all_reduce_small(small)
    gs_conv_w = lax.dynamic_slice_in_dim(small[2:5], (2 * my_q + my_c) * (d // N_DEV), d // N_DEV, axis=1)

    grads = {
        "norm1_g": small[0:1], "w_in": g_pack_a, "conv_w": gs_conv_w, "conv_b": small[5:6],
        "fox_f_bias": small[7:8, 0:nh], "fox_q_g": small[7:8, HD:2 * HD], "fox_k_g": small[7:8, 2 * HD:3 * HD],
        "mem_norm_g": small[6:7], "w_mem_kv": pack_b.shard(g_pack_b, 0), "xa_q_g": small[7:8, 3 * HD:3 * HD + xd],
        "xa_k_g": small[7:8, 3 * HD + xd:3 * HD + 2 * xd], "w_br_conv": pack_b.shard(g_pack_b, 1),
        "w_br_fox": pack_b.shard(g_pack_b, 2), "w_br_xa": pack_b.shard(g_pack_b, 3), "w_o": pack_b.shard(g_pack_b, 4),
        "norm2_g": small[1:2], "w_ffn_in": pack_b.shard(g_pack_b, 5), "w_ffn_out": pack_b.shard(g_pack_b, 6),
    }
    transposed = {"w_in", "w_mem_kv", "w_ffn_in"}
    weights = {
        "norm1_g": (norm1_g, m_norm1_g, v_norm1_g), "w_in": (w_in, m_w_in, v_w_in), "conv_w": (conv_w, m_conv_w, v_conv_w),
        "conv_b": (conv_b, m_conv_b, v_conv_b), "fox_f_bias": (fox_f_bias, m_fox_f_bias, v_fox_f_bias),
        "fox_q_g": (fox_q_g, m_fox_q_g, v_fox_q_g), "fox_k_g": (fox_k_g, m_fox_k_g, v_fox_k_g),
        "mem_norm_g": (mem_norm_g, m_mem_norm_g, v_mem_norm_g), "w_mem_kv": (w_mem_kv, m_w_mem_kv, v_w_mem_kv),
        "xa_q_g": (xa_q_g, m_xa_q_g, v_xa_q_g), "xa_k_g": (xa_k_g, m_xa_k_g, v_xa_k_g),
        "w_br_conv": (w_br_conv, m_w_br_conv, v_w_br_conv), "w_br_fox": (w_br_fox, m_w_br_fox, v_w_br_fox),
        "w_br_xa": (w_br_xa, m_w_br_xa, v_w_br_xa), "w_o": (w_o, m_w_o, v_w_o), "norm2_g": (norm2_g, m_norm2_g, v_norm2_g),
        "w_ffn_in": (w_ffn_in, m_w_ffn_in, v_w_ffn_in), "w_ffn_out": (w_ffn_out, m_w_ffn_out, v_w_ffn_out),
    }
    out_g, out_d, out_m, out_v = [], [], [], []
    for name, (w, m, v) in weights.items():
        shape = w.shape
        if name == "w_in":
            w3, m3, v3 = (a.transpose(2, 0, 1) for a in (w, m, v))
            dl, mn, vn = _adamw(w3, g_pack_a, m3, v3, name="adamw_" + name)
            out_g.append(g_pack_a[:shape[2]].transpose(1, 2, 0))
            out_d.append(dl.transpose(1, 2, 0))
            out_m.append(mn.transpose(1, 2, 0))
            out_v.append(vn.transpose(1, 2, 0))
            continue
        tr = name in transposed

        def as2d(a):
            a = a.reshape(shape[-2], shape[-1])
            return a.T if tr else a

        def back(a):
            return (a.T if tr else a).reshape(shape)

        w2 = as2d(w)
        g2 = grads[name].reshape(w2.shape)
        dl, mn, vn = _adamw(w2, g2, as2d(m), as2d(v), name="adamw_" + name)
        out_g.append(back(g2))
        out_d.append(back(dl))
        out_m.append(back(mn))
        out_v.append(back(vn))

    loss = lax.psum(loss_local, ("x", "y", "c"))
    return (loss, dx.reshape(nb, s, d), *out_g, *out_d, *out_m, *out_v)
```
